```python
import jax, jax.numpy as jnp
from jax import lax
import numpy as np

D_MODEL = 1024
BATCH = 8
SEQ = 2048
DEPTH = 2

GRID_W = 64
CTX_LEN = 256
N_MIXERS = 2
SSD_EXPAND = 2
SSD_D_INNER = SSD_EXPAND * D_MODEL
SSD_HEAD_DIM = 64
SSD_HEADS = SSD_D_INNER // SSD_HEAD_DIM
SSD_GROUPS = 8
SSD_HPG = SSD_HEADS // SSD_GROUPS
SSD_STATE = 128
SSD_CONV = 5
SSD_CHUNK = 128
SSD_CONV_DIM = SSD_D_INNER + 2 * SSD_GROUPS * SSD_STATE
SSD_IN_DIM = SSD_D_INNER + SSD_CONV_DIM + 2 * SSD_HEADS
CONF_KERNEL = 31
FFN_HIDDEN = ((8 * D_MODEL // 3 + 255) // 256) * 256
FFN_CONV = 3
N_SSD_LAYERS = (DEPTH + 1) // 2
N_CONF_LAYERS = DEPTH // 2
EPS = 1e-6

kernel_name = 'hybrid_ssd_conformer_dit_ctx_prefix'


def rmsnorm(h, w):
    hf = h.astype(jnp.float32)
    y = hf * lax.rsqrt(jnp.mean(hf * hf, axis=-1, keepdims=True) + EPS)
    return (y * w.astype(jnp.float32)).astype(h.dtype)


def layernorm(h, w, b):
    hf = h.astype(jnp.float32)
    mu = jnp.mean(hf, axis=-1, keepdims=True)
    d = hf - mu
    y = d * lax.rsqrt(jnp.mean(d * d, axis=-1, keepdims=True) + EPS)
    return (y * w.astype(jnp.float32) + b.astype(jnp.float32)).astype(h.dtype)


def modulate(h, g, shift, scale):
    return rmsnorm(h, g) * (1 + scale) + shift


def ada_params(cond, w, b):
    m = jax.nn.silu(cond) @ w + b
    return jnp.split(m, 6, axis=-1)


def dwconv1d(u, w, b):
    k, ch = w.shape
    pad = k // 2
    y = lax.conv_general_dilated(u, w[:, None, :].astype(u.dtype), window_strides=(1,),
                                 padding=[(pad, pad)], dimension_numbers=('NWC', 'WIO', 'NWC'),
                                 feature_group_count=ch)
    return y + b


def dwconv2d_grid(u, w, b):
    bsz, l, ch = u.shape
    rows = l // GRID_W
    u4 = u.reshape(bsz, rows, GRID_W, ch)
    kh, kw, _ = w.shape
    y = lax.conv_general_dilated(u4, w[:, :, None, :].astype(u.dtype), window_strides=(1, 1),
                                 padding=[(kh // 2, kh // 2), (kw // 2, kw // 2)],
                                 dimension_numbers=('NHWC', 'HWIO', 'NHWC'),
                                 feature_group_count=ch)
    return y.reshape(bsz, l, ch) + b


def ssd_scan(x, dt, A, B, C, s0):
    bsz, l, g, r, p = x.shape
    n = B.shape[-1]
    q = SSD_CHUNK
    nc = l // q
    x = x.astype(jnp.float32).reshape(bsz, nc, q, g, r, p)
    dt = dt.reshape(bsz, nc, q, g, r)
    B = B.astype(jnp.float32).reshape(bsz, nc, q, g, n)
    C = C.astype(jnp.float32).reshape(bsz, nc, q, g, n)
    acum = jnp.cumsum(dt * A, axis=2)
    xdt = x * dt[..., None]
    seg = acum[:, :, :, None] - acum[:, :, None, :]
    mask = jnp.tril(jnp.ones((q, q), dtype=bool))[:, :, None, None]
    decay = jnp.exp(jnp.where(mask, seg, -jnp.inf))
    cb = jnp.einsum('bcign,bcjgn->bcijg', C, B)
    y_diag = jnp.einsum('bcijgr,bcjgrp->bcigrp', cb[..., None] * decay, xdt)
    decay_to_end = jnp.exp(acum[:, :, -1:] - acum)
    chunk_states = jnp.einsum('bcjgn,bcjgrp->bcgrpn', B, xdt * decay_to_end[..., None])
    chunk_decay = jnp.exp(acum[:, :, -1])

    def step(s, inp):
        dec, st = inp
        return dec[..., None, None] * s + st, s

    final, entering = lax.scan(step, s0.astype(jnp.float32),
                               (jnp.moveaxis(chunk_decay, 1, 0), jnp.moveaxis(chunk_states, 1, 0)))
    entering = jnp.moveaxis(entering, 0, 1)
    y_off = jnp.einsum('bcign,bcgrpn->bcigrp', C, entering) * jnp.exp(acum)[..., None]
    y = (y_diag + y_off).reshape(bsz, l, g, r, p)
    return y, final


def ssd_mixer(u, w_in, conv_w, conv_b, dt_bias, a_log, d_skip, norm_w, w_out, s0_fwd, s0_bwd):
    bsz, l, _ = u.shape
    di, gn = SSD_D_INNER, SSD_GROUPS * SSD_STATE
    proj = u @ w_in
    z = proj[..., :di]
    xbc = jax.nn.silu(dwconv1d(proj[..., di:di + SSD_CONV_DIM], conv_w, conv_b))
    dt_raw = proj[..., di + SSD_CONV_DIM:]
    xs = xbc[..., :di].reshape(bsz, l, SSD_GROUPS, SSD_HPG, SSD_HEAD_DIM)
    Bm = xbc[..., di:di + gn].reshape(bsz, l, SSD_GROUPS, SSD_STATE)
    Cm = xbc[..., di + gn:].reshape(bsz, l, SSD_GROUPS, SSD_STATE)
    dt = jax.nn.softplus(dt_raw.astype(jnp.float32).reshape(bsz, l, 2, SSD_GROUPS, SSD_HPG)
                         + dt_bias.astype(jnp.float32).reshape(2, SSD_GROUPS, SSD_HPG))
    A = -jnp.exp(a_log.astype(jnp.float32)).reshape(2, SSD_GROUPS, SSD_HPG)
    y_f, s_f = ssd_scan(xs, dt[:, :, 0], A[0], Bm, Cm, s0_fwd)
    y_b, s_b = ssd_scan(jnp.flip(xs, 1), jnp.flip(dt[:, :, 1], 1), A[1],
                        jnp.flip(Bm, 1), jnp.flip(Cm, 1), s0_bwd)
    y = y_f + jnp.flip(y_b, 1) + d_skip.astype(jnp.float32).reshape(SSD_GROUPS, SSD_HPG)[:, :, None] * xs.astype(jnp.float32)
    y = y.reshape(bsz, l, di) * jax.nn.silu(z.astype(jnp.float32))
    out = rmsnorm(y, norm_w).astype(u.dtype) @ w_out
    return out, (s_f, s_b)


def conformer_conv(u, w1, b1, w_dw, b_dw, ln_w, ln_b, w2, b2):
    h = u @ w1 + b1
    a, g = jnp.split(h, 2, axis=-1)
    h = a * jax.nn.sigmoid(g)
    h = dwconv1d(h, w_dw, b_dw)
    h = jax.nn.silu(layernorm(h, ln_w, ln_b))
    return h @ w2 + b2


def conv_ffn(u, w_up, conv_w, conv_b, w_down, on_grid):
    h = u @ w_up
    val, gate = jnp.split(h, 2, axis=-1)
    if on_grid:
        gate = dwconv2d_grid(gate, conv_w, conv_b)
    else:
        gate = dwconv1d(gate, conv_w[FFN_CONV // 2], conv_b)
    return (jax.nn.silu(gate) * val) @ w_down


def _fwd_setup_inputs(seed: int = 0) -> dict:
    key = jax.random.key(seed)
    ks = jax.random.split(key, 32)
    f32 = jnp.float32

    def nrm(k, shape, scale):
        return jax.random.normal(k, shape, f32) * scale

    D = D_MODEL
    u = jax.random.uniform(ks[10], (N_SSD_LAYERS, 2, SSD_HEADS), f32)
    dt0 = jnp.exp(u * (np.log(0.1) - np.log(0.001)) + np.log(0.001)).astype(f32)
    dt_bias = dt0 + jnp.log(-jnp.expm1(-dt0))
    a_log = jnp.log(jax.random.uniform(ks[11], (N_SSD_LAYERS, 2, SSD_HEADS), f32, 1.0, 16.0))
    return {
        'x': nrm(ks[0], (BATCH, SEQ, D), 1.0),
        'c': nrm(ks[1], (BATCH, D), 1.0),
        'ctx': nrm(ks[2], (BATCH, CTX_LEN, D), 1.0),
        'c_ctx': nrm(ks[3], (D,), 1.0),
        'mod_w': nrm(ks[4], (DEPTH, D, 6 * D), 0.5 * D ** -0.5),
        'mod_b': nrm(ks[5], (DEPTH, 6 * D), 0.02),
        'norm1_w': 1.0 + nrm(ks[6], (DEPTH, D), 0.02),
        'norm2_w': 1.0 + nrm(ks[7], (DEPTH, D), 0.02),
        'ssd_w_in': nrm(ks[8], (N_SSD_LAYERS, D, SSD_IN_DIM), D ** -0.5),
        'ssd_conv_w': nrm(ks[9], (N_SSD_LAYERS, SSD_CONV, SSD_CONV_DIM), SSD_CONV ** -0.5),
        'ssd_conv_b': nrm(ks[12], (N_SSD_LAYERS, SSD_CONV_DIM), 0.02),
        'ssd_dt_bias': dt_bias,
        'ssd_a_log': a_log,
        'ssd_d': 1.0 + nrm(ks[13], (N_SSD_LAYERS, SSD_HEADS), 0.1),
        'ssd_norm_w': 1.0 + nrm(ks[14], (N_SSD_LAYERS, SSD_D_INNER), 0.02),
        'ssd_w_out': nrm(ks[15], (N_SSD_LAYERS, SSD_D_INNER, D), SSD_D_INNER ** -0.5),
        'conf_w_pw1': nrm(ks[16], (N_CONF_LAYERS, D, 2 * D), D ** -0.5),
        'conf_b_pw1': nrm(ks[17], (N_CONF_LAYERS, 2 * D), 0.02),
        'conf_w_dw': nrm(ks[18], (N_CONF_LAYERS, CONF_KERNEL, D), CONF_KERNEL ** -0.5),
        'conf_b_dw': nrm(ks[19], (N_CONF_LAYERS, D), 0.02),
        'conf_ln_w': 1.0 + nrm(ks[20], (N_CONF_LAYERS, D), 0.02),
        'conf_ln_b': nrm(ks[21], (N_CONF_LAYERS, D), 0.02),
        'conf_w_pw2': nrm(ks[22], (N_CONF_LAYERS, D, D), D ** -0.5),
        'conf_b_pw2': nrm(ks[23], (N_CONF_LAYERS, D), 0.02),
        'ffn_w_up': nrm(ks[24], (DEPTH, D, 2 * FFN_HIDDEN), D ** -0.5),
        'ffn_conv_w': nrm(ks[25], (DEPTH, FFN_CONV, FFN_CONV, FFN_HIDDEN), (FFN_CONV * FFN_CONV) ** -0.5),
        'ffn_conv_b': nrm(ks[26], (DEPTH, FFN_HIDDEN), 0.02),
        'ffn_w_down': nrm(ks[27], (DEPTH, FFN_HIDDEN, D), FFN_HIDDEN ** -0.5),
        'final_norm_w': 1.0 + nrm(ks[28], (D,), 0.02),
    }


def _fwd_reference(x, c, ctx, c_ctx, mod_w, mod_b, norm1_w, norm2_w,
              ssd_w_in, ssd_conv_w, ssd_conv_b, ssd_dt_bias, ssd_a_log, ssd_d, ssd_norm_w, ssd_w_out,
              conf_w_pw1, conf_b_pw1, conf_w_dw, conf_b_dw, conf_ln_w, conf_ln_b, conf_w_pw2, conf_b_pw2,
              ffn_w_up, ffn_conv_w, ffn_conv_b, ffn_w_down, final_norm_w):
    h, hc = x, ctx
    bsz = x.shape[0]
    for i in range(DEPTH):
        kind = i % N_MIXERS
        j = i // N_MIXERS
        last = i == DEPTH - 1
        need_ctx = (not last) or kind == 0
        sh1, sc1, g1, sh2, sc2, g2 = ada_params(c[:, None, :], mod_w[i], mod_b[i])
        a = modulate(h, norm1_w[i], sh1, sc1)
        if need_ctx:
            csh1, csc1, cg1, csh2, csc2, cg2 = ada_params(c_ctx, mod_w[i], mod_b[i])
            ac = modulate(hc, norm1_w[i], csh1, csc1)
        if kind == 0:
            ssd_p = (ssd_w_in[j], ssd_conv_w[j], ssd_conv_b[j], ssd_dt_bias[j], ssd_a_log[j],
                     ssd_d[j], ssd_norm_w[j], ssd_w_out[j])
            zeros = jnp.zeros((bsz, SSD_GROUPS, SSD_HPG, SSD_HEAD_DIM, SSD_STATE), jnp.float32)
            yc, (s_f, s_b) = ssd_mixer(ac, *ssd_p, zeros, zeros)
            y, _ = ssd_mixer(a, *ssd_p, s_f, s_b)
        else:
            conf_p = (conf_w_pw1[j], conf_b_pw1[j], conf_w_dw[j], conf_b_dw[j],
                      conf_ln_w[j], conf_ln_b[j], conf_w_pw2[j], conf_b_pw2[j])
            y = conformer_conv(a, *conf_p)
            if not last:
                yc = conformer_conv(ac, *conf_p)
        h = h + g1 * y
        h = h + g2 * conv_ffn(modulate(h, norm2_w[i], sh2, sc2), ffn_w_up[i], ffn_conv_w[i],
                              ffn_conv_b[i], ffn_w_down[i], True)
        if not last:
            hc = hc + cg1 * yc
            hc = hc + cg2 * conv_ffn(modulate(hc, norm2_w[i], csh2, csc2), ffn_w_up[i], ffn_conv_w[i],
                                     ffn_conv_b[i], ffn_w_down[i], False)
    return rmsnorm(h, final_norm_w)


import jax as _jax
import jax.numpy as _jnp

TWIN_FORMAT = 'train_step'
FWD_PARAMS = ['x', 'c', 'ctx', 'c_ctx', 'mod_w', 'mod_b', 'norm1_w', 'norm2_w', 'ssd_w_in', 'ssd_conv_w', 'ssd_conv_b', 'ssd_dt_bias', 'ssd_a_log', 'ssd_d', 'ssd_norm_w', 'ssd_w_out', 'conf_w_pw1', 'conf_b_pw1', 'conf_w_dw', 'conf_b_dw', 'conf_ln_w', 'conf_ln_b', 'conf_w_pw2', 'conf_b_pw2', 'ffn_w_up', 'ffn_conv_w', 'ffn_conv_b', 'ffn_w_down', 'final_norm_w']
TWIN_WEIGHTS = ['c_ctx', 'mod_w', 'mod_b', 'norm1_w', 'norm2_w', 'ssd_w_in', 'ssd_conv_w', 'ssd_conv_b', 'ssd_dt_bias', 'ssd_a_log', 'ssd_d', 'ssd_norm_w', 'ssd_w_out', 'conf_w_pw1', 'conf_b_pw1', 'conf_w_dw', 'conf_b_dw', 'conf_ln_w', 'conf_ln_b', 'conf_w_pw2', 'conf_b_pw2', 'ffn_w_up', 'ffn_conv_w', 'ffn_conv_b', 'ffn_w_down', 'final_norm_w']
TWIN_DIFF_INPUT = 'x'
TWIN_INPUTS = ['x', 'c', 'ctx', 'c_ctx', 'mod_w', 'mod_b', 'norm1_w', 'norm2_w', 'ssd_w_in', 'ssd_conv_w', 'ssd_conv_b', 'ssd_dt_bias', 'ssd_a_log', 'ssd_d', 'ssd_norm_w', 'ssd_w_out', 'conf_w_pw1', 'conf_b_pw1', 'conf_w_dw', 'conf_b_dw', 'conf_ln_w', 'conf_ln_b', 'conf_w_pw2', 'conf_b_pw2', 'ffn_w_up', 'ffn_conv_w', 'ffn_conv_b', 'ffn_w_down', 'final_norm_w', 'loss_target', 'm_c_ctx', 'm_mod_w', 'm_mod_b', 'm_norm1_w', 'm_norm2_w', 'm_ssd_w_in', 'm_ssd_conv_w', 'm_ssd_conv_b', 'm_ssd_dt_bias', 'm_ssd_a_log', 'm_ssd_d', 'm_ssd_norm_w', 'm_ssd_w_out', 'm_conf_w_pw1', 'm_conf_b_pw1', 'm_conf_w_dw', 'm_conf_b_dw', 'm_conf_ln_w', 'm_conf_ln_b', 'm_conf_w_pw2', 'm_conf_b_pw2', 'm_ffn_w_up', 'm_ffn_conv_w', 'm_ffn_conv_b', 'm_ffn_w_down', 'm_final_norm_w', 'v_c_ctx', 'v_mod_w', 'v_mod_b', 'v_norm1_w', 'v_norm2_w', 'v_ssd_w_in', 'v_ssd_conv_w', 'v_ssd_conv_b', 'v_ssd_dt_bias', 'v_ssd_a_log', 'v_ssd_d', 'v_ssd_norm_w', 'v_ssd_w_out', 'v_conf_w_pw1', 'v_conf_b_pw1', 'v_conf_w_dw', 'v_conf_b_dw', 'v_conf_ln_w', 'v_conf_ln_b', 'v_conf_w_pw2', 'v_conf_b_pw2', 'v_ffn_w_up', 'v_ffn_conv_w', 'v_ffn_conv_b', 'v_ffn_w_down', 'v_final_norm_w']
TWIN_OUTPUTS = ['loss', 'grad_x', 'grad_c_ctx', 'grad_mod_w', 'grad_mod_b', 'grad_norm1_w', 'grad_norm2_w', 'grad_ssd_w_in', 'grad_ssd_conv_w', 'grad_ssd_conv_b', 'grad_ssd_dt_bias', 'grad_ssd_a_log', 'grad_ssd_d', 'grad_ssd_norm_w', 'grad_ssd_w_out', 'grad_conf_w_pw1', 'grad_conf_b_pw1', 'grad_conf_w_dw', 'grad_conf_b_dw', 'grad_conf_ln_w', 'grad_conf_ln_b', 'grad_conf_w_pw2', 'grad_conf_b_pw2', 'grad_ffn_w_up', 'grad_ffn_conv_w', 'grad_ffn_conv_b', 'grad_ffn_w_down', 'grad_final_norm_w', 'delta_c_ctx', 'delta_mod_w', 'delta_mod_b', 'delta_norm1_w', 'delta_norm2_w', 'delta_ssd_w_in', 'delta_ssd_conv_w', 'delta_ssd_conv_b', 'delta_ssd_dt_bias', 'delta_ssd_a_log', 'delta_ssd_d', 'delta_ssd_norm_w', 'delta_ssd_w_out', 'delta_conf_w_pw1', 'delta_conf_b_pw1', 'delta_conf_w_dw', 'delta_conf_b_dw', 'delta_conf_ln_w', 'delta_conf_ln_b', 'delta_conf_w_pw2', 'delta_conf_b_pw2', 'delta_ffn_w_up', 'delta_ffn_conv_w', 'delta_ffn_conv_b', 'delta_ffn_w_down', 'delta_final_norm_w', 'new_m_c_ctx', 'new_m_mod_w', 'new_m_mod_b', 'new_m_norm1_w', 'new_m_norm2_w', 'new_m_ssd_w_in', 'new_m_ssd_conv_w', 'new_m_ssd_conv_b', 'new_m_ssd_dt_bias', 'new_m_ssd_a_log', 'new_m_ssd_d', 'new_m_ssd_norm_w', 'new_m_ssd_w_out', 'new_m_conf_w_pw1', 'new_m_conf_b_pw1', 'new_m_conf_w_dw', 'new_m_conf_b_dw', 'new_m_conf_ln_w', 'new_m_conf_ln_b', 'new_m_conf_w_pw2', 'new_m_conf_b_pw2', 'new_m_ffn_w_up', 'new_m_ffn_conv_w', 'new_m_ffn_conv_b', 'new_m_ffn_w_down', 'new_m_final_norm_w', 'new_v_c_ctx', 'new_v_mod_w', 'new_v_mod_b', 'new_v_norm1_w', 'new_v_norm2_w', 'new_v_ssd_w_in', 'new_v_ssd_conv_w', 'new_v_ssd_conv_b', 'new_v_ssd_dt_bias', 'new_v_ssd_a_log', 'new_v_ssd_d', 'new_v_ssd_norm_w', 'new_v_ssd_w_out', 'new_v_conf_w_pw1', 'new_v_conf_b_pw1', 'new_v_conf_w_dw', 'new_v_conf_b_dw', 'new_v_conf_ln_w', 'new_v_conf_ln_b', 'new_v_conf_w_pw2', 'new_v_conf_b_pw2', 'new_v_ffn_w_up', 'new_v_ffn_conv_w', 'new_v_ffn_conv_b', 'new_v_ffn_w_down', 'new_v_final_norm_w']
TWIN_LEAF_KINDS = {'loss': 'loss', 'grad_x': 'grad_x', 'grad_c_ctx': 'grad_w', 'grad_mod_w': 'grad_w', 'grad_mod_b': 'grad_w', 'grad_norm1_w': 'grad_w', 'grad_norm2_w': 'grad_w', 'grad_ssd_w_in': 'grad_w', 'grad_ssd_conv_w': 'grad_w', 'grad_ssd_conv_b': 'grad_w', 'grad_ssd_dt_bias': 'grad_w', 'grad_ssd_a_log': 'grad_w', 'grad_ssd_d': 'grad_w', 'grad_ssd_norm_w': 'grad_w', 'grad_ssd_w_out': 'grad_w', 'grad_conf_w_pw1': 'grad_w', 'grad_conf_b_pw1': 'grad_w', 'grad_conf_w_dw': 'grad_w', 'grad_conf_b_dw': 'grad_w', 'grad_conf_ln_w': 'grad_w', 'grad_conf_ln_b': 'grad_w', 'grad_conf_w_pw2': 'grad_w', 'grad_conf_b_pw2': 'grad_w', 'grad_ffn_w_up': 'grad_w', 'grad_ffn_conv_w': 'grad_w', 'grad_ffn_conv_b': 'grad_w', 'grad_ffn_w_down': 'grad_w', 'grad_final_norm_w': 'grad_w', 'delta_c_ctx': 'delta_w', 'delta_mod_w': 'delta_w', 'delta_mod_b': 'delta_w', 'delta_norm1_w': 'delta_w', 'delta_norm2_w': 'delta_w', 'delta_ssd_w_in': 'delta_w', 'delta_ssd_conv_w': 'delta_w', 'delta_ssd_conv_b': 'delta_w', 'delta_ssd_dt_bias': 'delta_w', 'delta_ssd_a_log': 'delta_w', 'delta_ssd_d': 'delta_w', 'delta_ssd_norm_w': 'delta_w', 'delta_ssd_w_out': 'delta_w', 'delta_conf_w_pw1': 'delta_w', 'delta_conf_b_pw1': 'delta_w', 'delta_conf_w_dw': 'delta_w', 'delta_conf_b_dw': 'delta_w', 'delta_conf_ln_w': 'delta_w', 'delta_conf_ln_b': 'delta_w', 'delta_conf_w_pw2': 'delta_w', 'delta_conf_b_pw2': 'delta_w', 'delta_ffn_w_up': 'delta_w', 'delta_ffn_conv_w': 'delta_w', 'delta_ffn_conv_b': 'delta_w', 'delta_ffn_w_down': 'delta_w', 'delta_final_norm_w': 'delta_w', 'new_m_c_ctx': 'new_m', 'new_m_mod_w': 'new_m', 'new_m_mod_b': 'new_m', 'new_m_norm1_w': 'new_m', 'new_m_norm2_w': 'new_m', 'new_m_ssd_w_in': 'new_m', 'new_m_ssd_conv_w': 'new_m', 'new_m_ssd_conv_b': 'new_m', 'new_m_ssd_dt_bias': 'new_m', 'new_m_ssd_a_log': 'new_m', 'new_m_ssd_d': 'new_m', 'new_m_ssd_norm_w': 'new_m', 'new_m_ssd_w_out': 'new_m', 'new_m_conf_w_pw1': 'new_m', 'new_m_conf_b_pw1': 'new_m', 'new_m_conf_w_dw': 'new_m', 'new_m_conf_b_dw': 'new_m', 'new_m_conf_ln_w': 'new_m', 'new_m_conf_ln_b': 'new_m', 'new_m_conf_w_pw2': 'new_m', 'new_m_conf_b_pw2': 'new_m', 'new_m_ffn_w_up': 'new_m', 'new_m_ffn_conv_w': 'new_m', 'new_m_ffn_conv_b': 'new_m', 'new_m_ffn_w_down': 'new_m', 'new_m_final_norm_w': 'new_m', 'new_v_c_ctx': 'new_v', 'new_v_mod_w': 'new_v', 'new_v_mod_b': 'new_v', 'new_v_norm1_w': 'new_v', 'new_v_norm2_w': 'new_v', 'new_v_ssd_w_in': 'new_v', 'new_v_ssd_conv_w': 'new_v', 'new_v_ssd_conv_b': 'new_v', 'new_v_ssd_dt_bias': 'new_v', 'new_v_ssd_a_log': 'new_v', 'new_v_ssd_d': 'new_v', 'new_v_ssd_norm_w': 'new_v', 'new_v_ssd_w_out': 'new_v', 'new_v_conf_w_pw1': 'new_v', 'new_v_conf_b_pw1': 'new_v', 'new_v_conf_w_dw': 'new_v', 'new_v_conf_b_dw': 'new_v', 'new_v_conf_ln_w': 'new_v', 'new_v_conf_ln_b': 'new_v', 'new_v_conf_w_pw2': 'new_v', 'new_v_conf_b_pw2': 'new_v', 'new_v_ffn_w_up': 'new_v', 'new_v_ffn_conv_w': 'new_v', 'new_v_ffn_conv_b': 'new_v', 'new_v_ffn_w_down': 'new_v', 'new_v_final_norm_w': 'new_v'}


def _forward(args):
    return _fwd_reference(*[args[k] for k in FWD_PARAMS])


def _output_shape():
    out = _jax.eval_shape(lambda: _forward(_fwd_setup_inputs(0)))
    return out.shape, out.dtype

N_MICROBATCH = 1
ADAM_LR = 0.001
ADAM_B1 = 0.9
ADAM_B2 = 0.999
ADAM_EPS = 1e-08
ADAM_WD = 0.01
ADAM_STEP = 10
PER_EXAMPLE_BATCH_AXIS = {'x': 0, 'c': 0, 'ctx': 0, 'loss_target': 0}
SHARED_INPUTS = []
_WEIGHT_DTYPES = {'c_ctx': _jnp.float32, 'mod_w': _jnp.float32, 'mod_b': _jnp.float32, 'norm1_w': _jnp.float32, 'norm2_w': _jnp.float32, 'ssd_w_in': _jnp.float32, 'ssd_conv_w': _jnp.float32, 'ssd_conv_b': _jnp.float32, 'ssd_dt_bias': _jnp.float32, 'ssd_a_log': _jnp.float32, 'ssd_d': _jnp.float32, 'ssd_norm_w': _jnp.float32, 'ssd_w_out': _jnp.float32, 'conf_w_pw1': _jnp.float32, 'conf_b_pw1': _jnp.float32, 'conf_w_dw': _jnp.float32, 'conf_b_dw': _jnp.float32, 'conf_ln_w': _jnp.float32, 'conf_ln_b': _jnp.float32, 'conf_w_pw2': _jnp.float32, 'conf_b_pw2': _jnp.float32, 'ffn_w_up': _jnp.float32, 'ffn_conv_w': _jnp.float32, 'ffn_conv_b': _jnp.float32, 'ffn_w_down': _jnp.float32, 'final_norm_w': _jnp.float32}
MOMENT_SCALE = {'c_ctx': 3.504371e-03, 'mod_w': 4.251505e-02, 'mod_b': 6.966845e-02, 'norm1_w': 4.336284e-02, 'norm2_w': 3.669195e-02, 'ssd_w_in': 2.351113e-02, 'ssd_conv_w': 2.049040e-02, 'ssd_conv_b': 2.759416e-02, 'ssd_dt_bias': 6.292049e-02, 'ssd_a_log': 7.646961e-02, 'ssd_d': 9.224342e-02, 'ssd_norm_w': 2.705220e-02, 'ssd_w_out': 3.856893e-02, 'conf_w_pw1': 1.720752e-02, 'conf_b_pw1': 1.715704e-02, 'conf_w_dw': 2.267638e-02, 'conf_b_dw': 3.882408e-02, 'conf_ln_w': 2.611184e-02, 'conf_ln_b': 2.336109e-02, 'conf_w_pw2': 2.213982e-02, 'conf_b_pw2': 4.127119e-02, 'ffn_w_up': 1.703710e-02, 'ffn_conv_w': 1.729645e-02, 'ffn_conv_b': 1.506495e-02, 'ffn_w_down': 2.773936e-02, 'final_norm_w': 1.603038e+01}


def _to_microbatches(a, axis):
    t = _jnp.moveaxis(a, axis, 0)
    t = t.reshape((N_MICROBATCH, t.shape[0] // N_MICROBATCH) + t.shape[1:])
    return _jnp.moveaxis(t, 1, axis + 1)


def setup_inputs(seed: int = 0) -> dict:
    inp = _fwd_setup_inputs(seed)
    key = _jax.random.fold_in(_jax.random.key(seed), 7919)
    shape, _ = _output_shape()
    out = dict(inp)
    out["loss_target"] = _jax.random.normal(_jax.random.fold_in(key, 0), shape, _jnp.float32)
    for i, name in enumerate(TWIN_WEIGHTS):
        w = inp[name].astype(_jnp.float32)
        if MOMENT_SCALE is None:
            s = _jnp.sqrt(_jnp.mean(_jnp.square(w)) + 1e-30)
        else:
            s = MOMENT_SCALE[name]
        km, kv = _jax.random.split(_jax.random.fold_in(key, i + 1))
        out[name] = w
        out["m_" + name] = s * _jax.random.normal(km, w.shape, _jnp.float32)
        out["v_" + name] = (s * s) * _jax.random.uniform(kv, w.shape, _jnp.float32, 0.5, 1.5)
    if N_MICROBATCH > 1:
        for name, axis in PER_EXAMPLE_BATCH_AXIS.items():
            out[name] = _to_microbatches(out[name], axis)
    return {'x': out['x'], 'c': out['c'], 'ctx': out['ctx'], 'c_ctx': out['c_ctx'], 'mod_w': out['mod_w'], 'mod_b': out['mod_b'], 'norm1_w': out['norm1_w'], 'norm2_w': out['norm2_w'], 'ssd_w_in': out['ssd_w_in'], 'ssd_conv_w': out['ssd_conv_w'], 'ssd_conv_b': out['ssd_conv_b'], 'ssd_dt_bias': out['ssd_dt_bias'], 'ssd_a_log': out['ssd_a_log'], 'ssd_d': out['ssd_d'], 'ssd_norm_w': out['ssd_norm_w'], 'ssd_w_out': out['ssd_w_out'], 'conf_w_pw1': out['conf_w_pw1'], 'conf_b_pw1': out['conf_b_pw1'], 'conf_w_dw': out['conf_w_dw'], 'conf_b_dw': out['conf_b_dw'], 'conf_ln_w': out['conf_ln_w'], 'conf_ln_b': out['conf_ln_b'], 'conf_w_pw2': out['conf_w_pw2'], 'conf_b_pw2': out['conf_b_pw2'], 'ffn_w_up': out['ffn_w_up'], 'ffn_conv_w': out['ffn_conv_w'], 'ffn_conv_b': out['ffn_conv_b'], 'ffn_w_down': out['ffn_w_down'], 'final_norm_w': out['final_norm_w'], 'loss_target': out['loss_target'], 'm_c_ctx': out['m_c_ctx'], 'm_mod_w': out['m_mod_w'], 'm_mod_b': out['m_mod_b'], 'm_norm1_w': out['m_norm1_w'], 'm_norm2_w': out['m_norm2_w'], 'm_ssd_w_in': out['m_ssd_w_in'], 'm_ssd_conv_w': out['m_ssd_conv_w'], 'm_ssd_conv_b': out['m_ssd_conv_b'], 'm_ssd_dt_bias': out['m_ssd_dt_bias'], 'm_ssd_a_log': out['m_ssd_a_log'], 'm_ssd_d': out['m_ssd_d'], 'm_ssd_norm_w': out['m_ssd_norm_w'], 'm_ssd_w_out': out['m_ssd_w_out'], 'm_conf_w_pw1': out['m_conf_w_pw1'], 'm_conf_b_pw1': out['m_conf_b_pw1'], 'm_conf_w_dw': out['m_conf_w_dw'], 'm_conf_b_dw': out['m_conf_b_dw'], 'm_conf_ln_w': out['m_conf_ln_w'], 'm_conf_ln_b': out['m_conf_ln_b'], 'm_conf_w_pw2': out['m_conf_w_pw2'], 'm_conf_b_pw2': out['m_conf_b_pw2'], 'm_ffn_w_up': out['m_ffn_w_up'], 'm_ffn_conv_w': out['m_ffn_conv_w'], 'm_ffn_conv_b': out['m_ffn_conv_b'], 'm_ffn_w_down': out['m_ffn_w_down'], 'm_final_norm_w': out['m_final_norm_w'], 'v_c_ctx': out['v_c_ctx'], 'v_mod_w': out['v_mod_w'], 'v_mod_b': out['v_mod_b'], 'v_norm1_w': out['v_norm1_w'], 'v_norm2_w': out['v_norm2_w'], 'v_ssd_w_in': out['v_ssd_w_in'], 'v_ssd_conv_w': out['v_ssd_conv_w'], 'v_ssd_conv_b': out['v_ssd_conv_b'], 'v_ssd_dt_bias': out['v_ssd_dt_bias'], 'v_ssd_a_log': out['v_ssd_a_log'], 'v_ssd_d': out['v_ssd_d'], 'v_ssd_norm_w': out['v_ssd_norm_w'], 'v_ssd_w_out': out['v_ssd_w_out'], 'v_conf_w_pw1': out['v_conf_w_pw1'], 'v_conf_b_pw1': out['v_conf_b_pw1'], 'v_conf_w_dw': out['v_conf_w_dw'], 'v_conf_b_dw': out['v_conf_b_dw'], 'v_conf_ln_w': out['v_conf_ln_w'], 'v_conf_ln_b': out['v_conf_ln_b'], 'v_conf_w_pw2': out['v_conf_w_pw2'], 'v_conf_b_pw2': out['v_conf_b_pw2'], 'v_ffn_w_up': out['v_ffn_w_up'], 'v_ffn_conv_w': out['v_ffn_conv_w'], 'v_ffn_conv_b': out['v_ffn_conv_b'], 'v_ffn_w_down': out['v_ffn_w_down'], 'v_final_norm_w': out['v_final_norm_w']}


def _loss(weights, diff, rest, loss_target):
    with _jax.named_scope("forward"):
        args = {**rest, TWIN_DIFF_INPUT: diff, **{k: w.astype(_WEIGHT_DTYPES[k]) for k, w in weights.items()}}
        y = _forward(args)
    with _jax.named_scope("loss_head"):
        err = _jnp.square(y.astype(_jnp.float32) - loss_target)
        return 0.5 * _jnp.sum(_jnp.mean(err, axis=-1)) if err.ndim else 0.5 * err


def _adamw(w, g, m, v):
    m = ADAM_B1 * m + (1.0 - ADAM_B1) * g
    v = ADAM_B2 * v + (1.0 - ADAM_B2) * _jnp.square(g)
    m_hat = m / (1.0 - ADAM_B1 ** ADAM_STEP)
    v_hat = v / (1.0 - ADAM_B2 ** ADAM_STEP)
    delta = -ADAM_LR * (m_hat / (_jnp.sqrt(v_hat) + ADAM_EPS) + ADAM_WD * w)
    return delta, m, v


def reference(x, c, ctx, c_ctx, mod_w, mod_b, norm1_w, norm2_w, ssd_w_in, ssd_conv_w, ssd_conv_b, ssd_dt_bias, ssd_a_log, ssd_d, ssd_norm_w, ssd_w_out, conf_w_pw1, conf_b_pw1, conf_w_dw, conf_b_dw, conf_ln_w, conf_ln_b, conf_w_pw2, conf_b_pw2, ffn_w_up, ffn_conv_w, ffn_conv_b, ffn_w_down, final_norm_w, loss_target, m_c_ctx, m_mod_w, m_mod_b, m_norm1_w, m_norm2_w, m_ssd_w_in, m_ssd_conv_w, m_ssd_conv_b, m_ssd_dt_bias, m_ssd_a_log, m_ssd_d, m_ssd_norm_w, m_ssd_w_out, m_conf_w_pw1, m_conf_b_pw1, m_conf_w_dw, m_conf_b_dw, m_conf_ln_w, m_conf_ln_b, m_conf_w_pw2, m_conf_b_pw2, m_ffn_w_up, m_ffn_conv_w, m_ffn_conv_b, m_ffn_w_down, m_final_norm_w, v_c_ctx, v_mod_w, v_mod_b, v_norm1_w, v_norm2_w, v_ssd_w_in, v_ssd_conv_w, v_ssd_conv_b, v_ssd_dt_bias, v_ssd_a_log, v_ssd_d, v_ssd_norm_w, v_ssd_w_out, v_conf_w_pw1, v_conf_b_pw1, v_conf_w_dw, v_conf_b_dw, v_conf_ln_w, v_conf_ln_b, v_conf_w_pw2, v_conf_b_pw2, v_ffn_w_up, v_ffn_conv_w, v_ffn_conv_b, v_ffn_w_down, v_final_norm_w):
    given = dict(x=x, c=c, ctx=ctx, c_ctx=c_ctx, mod_w=mod_w, mod_b=mod_b, norm1_w=norm1_w, norm2_w=norm2_w, ssd_w_in=ssd_w_in, ssd_conv_w=ssd_conv_w, ssd_conv_b=ssd_conv_b, ssd_dt_bias=ssd_dt_bias, ssd_a_log=ssd_a_log, ssd_d=ssd_d, ssd_norm_w=ssd_norm_w, ssd_w_out=ssd_w_out, conf_w_pw1=conf_w_pw1, conf_b_pw1=conf_b_pw1, conf_w_dw=conf_w_dw, conf_b_dw=conf_b_dw, conf_ln_w=conf_ln_w, conf_ln_b=conf_ln_b, conf_w_pw2=conf_w_pw2, conf_b_pw2=conf_b_pw2, ffn_w_up=ffn_w_up, ffn_conv_w=ffn_conv_w, ffn_conv_b=ffn_conv_b, ffn_w_down=ffn_w_down, final_norm_w=final_norm_w, loss_target=loss_target, m_c_ctx=m_c_ctx, m_mod_w=m_mod_w, m_mod_b=m_mod_b, m_norm1_w=m_norm1_w, m_norm2_w=m_norm2_w, m_ssd_w_in=m_ssd_w_in, m_ssd_conv_w=m_ssd_conv_w, m_ssd_conv_b=m_ssd_conv_b, m_ssd_dt_bias=m_ssd_dt_bias, m_ssd_a_log=m_ssd_a_log, m_ssd_d=m_ssd_d, m_ssd_norm_w=m_ssd_norm_w, m_ssd_w_out=m_ssd_w_out, m_conf_w_pw1=m_conf_w_pw1, m_conf_b_pw1=m_conf_b_pw1, m_conf_w_dw=m_conf_w_dw, m_conf_b_dw=m_conf_b_dw, m_conf_ln_w=m_conf_ln_w, m_conf_ln_b=m_conf_ln_b, m_conf_w_pw2=m_conf_w_pw2, m_conf_b_pw2=m_conf_b_pw2, m_ffn_w_up=m_ffn_w_up, m_ffn_conv_w=m_ffn_conv_w, m_ffn_conv_b=m_ffn_conv_b, m_ffn_w_down=m_ffn_w_down, m_final_norm_w=m_final_norm_w, v_c_ctx=v_c_ctx, v_mod_w=v_mod_w, v_mod_b=v_mod_b, v_norm1_w=v_norm1_w, v_norm2_w=v_norm2_w, v_ssd_w_in=v_ssd_w_in, v_ssd_conv_w=v_ssd_conv_w, v_ssd_conv_b=v_ssd_conv_b, v_ssd_dt_bias=v_ssd_dt_bias, v_ssd_a_log=v_ssd_a_log, v_ssd_d=v_ssd_d, v_ssd_norm_w=v_ssd_norm_w, v_ssd_w_out=v_ssd_w_out, v_conf_w_pw1=v_conf_w_pw1, v_conf_b_pw1=v_conf_b_pw1, v_conf_w_dw=v_conf_w_dw, v_conf_b_dw=v_conf_b_dw, v_conf_ln_w=v_conf_ln_w, v_conf_ln_b=v_conf_ln_b, v_conf_w_pw2=v_conf_w_pw2, v_conf_b_pw2=v_conf_b_pw2, v_ffn_w_up=v_ffn_w_up, v_ffn_conv_w=v_ffn_conv_w, v_ffn_conv_b=v_ffn_conv_b, v_ffn_w_down=v_ffn_w_down, v_final_norm_w=v_final_norm_w)
    weights = {n: given[n] for n in TWIN_WEIGHTS}
    shared = {n: given[n] for n in SHARED_INPUTS}
    per_example = {n: given[n] for n in ['x', 'c', 'ctx']}
    grad_fn = _jax.value_and_grad(_loss, argnums=(0, 1))

    def one_microbatch(ex, loss_target):
        ex = dict(ex)
        diff = ex.pop(TWIN_DIFF_INPUT)
        return grad_fn(weights, diff, {**shared, **ex}, loss_target)

    if N_MICROBATCH == 1:
        loss, (grad_w, grad_x) = one_microbatch(per_example, given["loss_target"])
    else:
        def body(carry, xs):
            loss_sum, grad_sum = carry
            l_k, (gw_k, gx_k) = one_microbatch(xs[0], xs[1])
            with _jax.named_scope("update"):
                return (loss_sum + l_k, _jax.tree.map(_jnp.add, grad_sum, gw_k)), gx_k

        init = (_jnp.zeros((), _jnp.float32), _jax.tree.map(_jnp.zeros_like, weights))
        (loss, grad_w), grad_x = _jax.lax.scan(body, init, (per_example, given["loss_target"]))
    with _jax.named_scope("update"):
        delta_w, new_m, new_v = {}, {}, {}
        for n in TWIN_WEIGHTS:
            delta_w[n], new_m[n], new_v[n] = _adamw(weights[n], grad_w[n], given["m_" + n], given["v_" + n])
    return (loss, grad_x, *[grad_w[n] for n in TWIN_WEIGHTS], *[delta_w[n] for n in TWIN_WEIGHTS],
            *[new_m[n] for n in TWIN_WEIGHTS], *[new_v[n] for n in TWIN_WEIGHTS])
```

```python
import functools

import jax
import jax.numpy as jnp
from jax import lax
from jax.experimental import pallas as pl
from jax.experimental.pallas import tpu as pltpu

f32 = jnp.float32
bf16 = jnp.bfloat16
HI = lax.Precision.HIGHEST
S = jax.ShapeDtypeStruct
MESH = pl.DeviceIdType.MESH
ANY = pl.BlockSpec(memory_space=pl.ANY)
VMEM = pl.BlockSpec(memory_space=pltpu.VMEM)

N_DEV = 8
D = 1024
DI = 2048
CONVD = 4096
FH = 2816
GRID_W = 64
Q = 128
HPG = 4
P = 64
N = 128
G = 8
GW = HPG * P
NH_SSD = G * HPG
EPS = 1e-6
ADAM_LR, ADAM_B1, ADAM_B2, ADAM_EPS, ADAM_WD, ADAM_STEP = 0.001, 0.9, 0.999, 1e-08, 0.01, 10
VMEM_LIMIT_BYTES = 56 * 1024 * 1024
PACK_W = 1024
TB = 256


def _cparams(*sem):
    return pltpu.CompilerParams(dimension_semantics=sem, vmem_limit_bytes=VMEM_LIMIT_BYTES)


def _pick(n, prefs):
    for p in prefs:
        if n % p == 0:
            return p
    return n


def _sigmoid(x):
    return 1.0 / (1.0 + jnp.exp(-x))


def _softplus(x):
    return jnp.maximum(x, 0.0) + jnp.log(1.0 + jnp.exp(-jnp.abs(x)))


def matmul(a, b, mode, out_dtype, name):
    if mode == "nn":
        (M, K), (_, Nn) = a.shape, b.shape
    elif mode == "tn":
        (K, M), (_, Nn) = a.shape, b.shape
    else:
        (M, K), (Nn, _) = a.shape, b.shape
    bm = _pick(M, (512, 384, 256, 128))
    bn = _pick(Nn, (512, 384, 256, 128))
    bk = _pick(K, (1024, 768, 512, 256, 128))
    nk = K // bk
    dims = {"nn": (((1,), (0,)), ((), ())), "tn": (((0,), (0,)), ((), ())), "nt": (((1,), (1,)), ((), ()))}[mode]

    def body(a_ref, b_ref, o_ref, acc_ref):
        k = pl.program_id(2)

        @pl.when(k == 0)
        def _():
            acc_ref[...] = jnp.zeros_like(acc_ref)

        acc_ref[...] += lax.dot_general(a_ref[...].astype(bf16), b_ref[...].astype(bf16), dims,
                                        preferred_element_type=f32)

        @pl.when(k == nk - 1)
        def _():
            o_ref[...] = acc_ref[...].astype(out_dtype)

    if mode == "nn":
        a_spec = pl.BlockSpec((bm, bk), lambda i, j, k: (i, k))
        b_spec = pl.BlockSpec((bk, bn), lambda i, j, k: (k, j))
    elif mode == "tn":
        a_spec = pl.BlockSpec((bk, bm), lambda i, j, k: (k, i))
        b_spec = pl.BlockSpec((bk, bn), lambda i, j, k: (k, j))
    else:
        a_spec = pl.BlockSpec((bm, bk), lambda i, j, k: (i, k))
        b_spec = pl.BlockSpec((bn, bk), lambda i, j, k: (j, k))
    return pl.pallas_call(
        body, grid=(M // bm, Nn // bn, nk), in_specs=[a_spec, b_spec],
        out_specs=pl.BlockSpec((bm, bn), lambda i, j, k: (i, j)),
        out_shape=S((M, Nn), out_dtype), scratch_shapes=[pltpu.VMEM((bm, bn), f32)],
        compiler_params=_cparams("parallel", "parallel", "arbitrary"), name=name,
    )(a, b)


def _modnorm_f(h, w, sc, sh):
    y = h * lax.rsqrt(jnp.mean(h * h, axis=-1, keepdims=True) + EPS)
    return (y * w) * (1.0 + sc) + sh


def _kind_specs(nctxb):
    if nctxb > 0:
        return pl.BlockSpec((None, 1, D), lambda i: (jnp.where(i < nctxb, 0, 1), 0, 0))
    return pl.BlockSpec((None, 1, D), lambda i: (0, 0, 0))


def modnorm_fwd(h, w, sc, sh, nctxb, name):
    T = h.shape[0]

    def body(h_ref, w_ref, sc_ref, sh_ref, o_ref):
        o_ref[...] = _modnorm_f(h_ref[...], w_ref[...], sc_ref[...], sh_ref[...]).astype(bf16)

    blk = pl.BlockSpec((TB, D), lambda i: (i, 0))
    row = pl.BlockSpec((1, D), lambda i: (0, 0))
    ks = _kind_specs(nctxb)
    return pl.pallas_call(body, grid=(T // TB,), in_specs=[blk, row, ks, ks], out_specs=blk,
                          out_shape=S((T, D), bf16), compiler_params=_cparams("parallel"), name=name)(h, w, sc, sh)


def modnorm_bwd(h, w, sc, sh, da, dres, nctxb, name):
    T = h.shape[0]
    kinds = sc.shape[0]

    def body(h_ref, w_ref, sc_ref, sh_ref, da_ref, dres_ref, dh_ref, dw_ref, dsc_ref, dsh_ref):
        i = pl.program_id(0)
        _, vjp = jax.vjp(_modnorm_f, h_ref[...], w_ref[...], sc_ref[...], sh_ref[...])
        dh, dw, dsc, dsh = vjp(da_ref[...].astype(f32))
        dh_ref[...] = dres_ref[...] + dh

        @pl.when(i == 0)
        def _():
            dw_ref[...] = jnp.zeros_like(dw_ref)

        @pl.when((i == 0) | (i == nctxb))
        def _():
            dsc_ref[...] = jnp.zeros_like(dsc_ref)
            dsh_ref[...] = jnp.zeros_like(dsh_ref)

        dw_ref[...] += dw
        dsc_ref[...] += dsc
        dsh_ref[...] += dsh

    blk = pl.BlockSpec((TB, D), lambda i: (i, 0))
    row = pl.BlockSpec((1, D), lambda i: (0, 0))
    ks = _kind_specs(nctxb)
    return pl.pallas_call(
        body, grid=(T // TB,), in_specs=[blk, row, ks, ks, blk, blk], out_specs=[blk, row, ks, ks],
        out_shape=[S((T, D), f32), S((1, D), f32), S((kinds, 1, D), f32), S((kinds, 1, D), f32)],
        compiler_params=_cparams("arbitrary"), name=name)(h, w, sc, sh, da, dres)


def resgate_fwd(h, o, g, b, name):
    T = h.shape[0]

    def body(h_ref, o_ref, g_ref, b_ref, out_ref):
        out_ref[...] = h_ref[...] + g_ref[...] * (o_ref[...] + b_ref[...])

    blk = pl.BlockSpec((TB, D), lambda i: (i, 0))
    row = pl.BlockSpec((1, D), lambda i: (0, 0))
    return pl.pallas_call(body, grid=(T // TB,), in_specs=[blk, blk, row, row], out_specs=blk,
                          out_shape=S((T, D), f32), compiler_params=_cparams("parallel"), name=name)(h, o, g, b)


def resgate_bwd(dh, o, g, b, name):
    T = dh.shape[0]

    def body(dh_ref, o_ref, g_ref, b_ref, do_ref, dg_ref, db_ref):
        i = pl.program_id(0)

        @pl.when(i == 0)
        def _():
            dg_ref[...] = jnp.zeros_like(dg_ref)
            db_ref[...] = jnp.zeros_like(db_ref)

        dh = dh_ref[...]
        do = g_ref[...] * dh
        do_ref[...] = do.astype(bf16)
        dg_ref[...] += jnp.sum(dh * (o_ref[...] + b_ref[...]), axis=0, keepdims=True)
        db_ref[...] += jnp.sum(do, axis=0, keepdims=True)

    blk = pl.BlockSpec((TB, D), lambda i: (i, 0))
    row = pl.BlockSpec((1, D), lambda i: (0, 0))
    return pl.pallas_call(body, grid=(T // TB,), in_specs=[blk, blk, row, row], out_specs=[blk, row, row],
                          out_shape=[S((T, D), bf16), S((1, D), f32), S((1, D), f32)],
                          compiler_params=_cparams("arbitrary"), name=name)(dh, o, g, b)


def final_loss(h, w, tgt, name):
    T = h.shape[0]

    def f(hv, wv, tv):
        y = (hv * lax.rsqrt(jnp.mean(hv * hv, axis=-1, keepdims=True) + EPS)) * wv
        e = y - tv
        return 0.5 * jnp.sum(jnp.sum(e * e, axis=-1, keepdims=True), axis=0, keepdims=True) * (1.0 / D)

    def body(h_ref, w_ref, t_ref, loss_ref, dh_ref, dw_ref):
        i = pl.program_id(0)
        tv = t_ref[...]
        val, vjp = jax.vjp(lambda a, b_: f(a, b_, tv), h_ref[...], w_ref[...])
        dh, dw = vjp(jnp.ones((1, 1), f32))
        dh_ref[...] = dh

        @pl.when(i == 0)
        def _():
            loss_ref[...] = jnp.zeros_like(loss_ref)
            dw_ref[...] = jnp.zeros_like(dw_ref)

        loss_ref[...] += jnp.broadcast_to(val, (1, 128))
        dw_ref[...] += dw

    blk = pl.BlockSpec((TB, D), lambda i: (i, 0))
    row = pl.BlockSpec((1, D), lambda i: (0, 0))
    return pl.pallas_call(body, grid=(T // TB,), in_specs=[blk, row, blk],
                          out_specs=[pl.BlockSpec((1, 128), lambda i: (0, 0)), blk, row],
                          out_shape=[S((1, 128), f32), S((T, D), f32), S((1, D), f32)],
                          compiler_params=_cparams("arbitrary"), name=name)(h, w, tgt)


CB = 256


def _fill_padded(pad_ref, vals, padr, ln):
    pad_ref[pl.ds(0, padr), :] = jnp.zeros((padr, CB), f32)
    pad_ref[pl.ds(padr + ln, padr), :] = jnp.zeros((padr, CB), f32)
    pad_ref[pl.ds(padr, ln), :] = vals


def ssd_conv_fwd(proj, w, b, segs, name):
    T = proj.shape[0]
    K, half, padr = 5, 2, 8
    maxlen = max(ln for _, ln in segs)

    def body(u_ref, w_ref, b_ref, o_ref, pad_ref):
        for s0, ln in segs:
            _fill_padded(pad_ref, u_ref[pl.ds(s0, ln), :], padr, ln)
            acc = jnp.broadcast_to(b_ref[...], (ln, CB))
            for k in range(K):
                acc = acc + pad_ref[pl.ds(padr + k - half, ln), :] * w_ref[pl.ds(k, 1), :]
            o_ref[pl.ds(s0, ln), :] = acc * _sigmoid(acc)

    off = DI // CB
    return pl.pallas_call(
        body, grid=(CONVD // CB,),
        in_specs=[pl.BlockSpec((T, CB), lambda j: (0, off + j)), pl.BlockSpec((K, CB), lambda j: (0, j)),
                  pl.BlockSpec((1, CB), lambda j: (0, j))],
        out_specs=pl.BlockSpec((T, CB), lambda j: (0, j)), out_shape=S((T, CONVD), f32),
        scratch_shapes=[pltpu.VMEM((maxlen + 2 * padr, CB), f32)],
        compiler_params=_cparams("parallel"), name=name)(proj, w, b)


def ssd_conv_bwd(proj, w, b, dy, segs, name):
    T = proj.shape[0]
    K, half, padr = 5, 2, 8
    maxlen = max(ln for _, ln in segs)

    def body(u_ref, w_ref, b_ref, dy_ref, du_ref, dw_ref, db_ref, pad_ref, gpad_ref):
        dw_acc = [jnp.zeros((1, CB), f32) for _ in range(K)]
        db_acc = jnp.zeros((1, CB), f32)
        for s0, ln in segs:
            _fill_padded(pad_ref, u_ref[pl.ds(s0, ln), :], padr, ln)
            pre = jnp.broadcast_to(b_ref[...], (ln, CB))
            for k in range(K):
                pre = pre + pad_ref[pl.ds(padr + k - half, ln), :] * w_ref[pl.ds(k, 1), :]
            sg = _sigmoid(pre)
            dpre = dy_ref[pl.ds(s0, ln), :] * (sg * (1.0 + pre * (1.0 - sg)))
            db_acc = db_acc + jnp.sum(dpre, axis=0, keepdims=True)
            for k in range(K):
                dw_acc[k] = dw_acc[k] + jnp.sum(dpre * pad_ref[pl.ds(padr + k - half, ln), :], axis=0, keepdims=True)
            _fill_padded(gpad_ref, dpre, padr, ln)
            du = jnp.zeros((ln, CB), f32)
            for k in range(K):
                du = du + gpad_ref[pl.ds(padr - (k - half), ln), :] * w_ref[pl.ds(k, 1), :]
            du_ref[pl.ds(s0, ln), :] = du.astype(bf16)
        for k in range(K):
            dw_ref[pl.ds(k, 1), :] = dw_acc[k]
        db_ref[...] = db_acc

    off = DI // CB
    cblk = pl.BlockSpec((T, CB), lambda j: (0, j))
    return pl.pallas_call(
        body, grid=(CONVD // CB,),
        in_specs=[pl.BlockSpec((T, CB), lambda j: (0, off + j)), pl.BlockSpec((K, CB), lambda j: (0, j)),
                  pl.BlockSpec((1, CB), lambda j: (0, j)), cblk],
        out_specs=[cblk, pl.BlockSpec((K, CB), lambda j: (0, j)), pl.BlockSpec((1, CB), lambda j: (0, j))],
        out_shape=[S((T, CONVD), bf16), S((K, CONVD), f32), S((1, CONVD), f32)],
        scratch_shapes=[pltpu.VMEM((maxlen + 2 * padr, CB), f32), pltpu.VMEM((maxlen + 2 * padr, CB), f32)],
        compiler_params=_cparams("parallel"), name=name)(proj, w, b, dy)


FFN_PADR = 72


def _grid_taps():
    return [(dr, dc) for dr in (-1, 0, 1) for dc in (-1, 0, 1)]


def ffn_gate_fwd(hh, cw, cb_, name):
    L = hh.shape[0]
    nb = FH // CB

    def body(val_ref, gate_ref, w_ref, b_ref, o_ref, pad_ref):
        _fill_padded(pad_ref, gate_ref[...], FFN_PADR, L)
        col = lax.broadcasted_iota(jnp.int32, (L, CB), 0) & (GRID_W - 1)
        acc = jnp.broadcast_to(b_ref[...], (L, CB))
        for t, (dr, dc) in enumerate(_grid_taps()):
            tap = pad_ref[pl.ds(FFN_PADR + dr * GRID_W + dc, L), :]
            if dc == -1:
                tap = jnp.where(col != 0, tap, 0.0)
            elif dc == 1:
                tap = jnp.where(col != GRID_W - 1, tap, 0.0)
            acc = acc + tap * w_ref[pl.ds(t, 1), :]
        o_ref[...] = (acc * _sigmoid(acc) * val_ref[...]).astype(bf16)

    return pl.pallas_call(
        body, grid=(nb,),
        in_specs=[pl.BlockSpec((L, CB), lambda j: (0, j)), pl.BlockSpec((L, CB), lambda j: (0, nb + j)),
                  pl.BlockSpec((9, CB), lambda j: (0, j)), pl.BlockSpec((1, CB), lambda j: (0, j))],
        out_specs=pl.BlockSpec((L, CB), lambda j: (0, j)), out_shape=S((L, FH), bf16),
        scratch_shapes=[pltpu.VMEM((L + 2 * FFN_PADR, CB), f32)],
        compiler_params=_cparams("parallel"), name=name)(hh, hh, cw, cb_)


def ffn_gate_bwd(hh, cw, cb_, dact, name):
    L = hh.shape[0]
    nb = FH // CB

    def body(val_ref, gate_ref, w_ref, b_ref, da_ref, dval_ref, dgate_ref, dw_ref, db_ref, pad_ref, gpad_ref):
        _fill_padded(pad_ref, gate_ref[...], FFN_PADR, L)
        col = lax.broadcasted_iota(jnp.int32, (L, CB), 0) & (GRID_W - 1)
        not_first = col != 0
        not_last = col != GRID_W - 1
        pre = jnp.broadcast_to(b_ref[...], (L, CB))
        for t, (dr, dc) in enumerate(_grid_taps()):
            tap = pad_ref[pl.ds(FFN_PADR + dr * GRID_W + dc, L), :]
            if dc == -1:
                tap = jnp.where(not_first, tap, 0.0)
            elif dc == 1:
                tap = jnp.where(not_last, tap, 0.0)
            pre = pre + tap * w_ref[pl.ds(t, 1), :]
        sg = _sigmoid(pre)
        da = da_ref[...].astype(f32)
        dval_ref[...] = (da * pre * sg).astype(bf16)
        dpre = da * val_ref[...] * (sg * (1.0 + pre * (1.0 - sg)))
        db_ref[...] = jnp.sum(dpre, axis=0, keepdims=True)
        _fill_padded(gpad_ref, dpre, FFN_PADR, L)
        dg = jnp.zeros((L, CB), f32)
        for t, (dr, dc) in enumerate(_grid_taps()):
            off = dr * GRID_W + dc
            tap = pad_ref[pl.ds(FFN_PADR + off, L), :]
            back = gpad_ref[pl.ds(FFN_PADR - off, L), :]
            if dc == -1:
                tap = jnp.where(not_first, tap, 0.0)
                back = jnp.where(not_last, back, 0.0)
            elif dc == 1:
                tap = jnp.where(not_last, tap, 0.0)
                back = jnp.where(not_first, back, 0.0)
            dw_ref[pl.ds(t, 1), :] = jnp.sum(dpre * tap, axis=0, keepdims=True)
            dg = dg + back * w_ref[pl.ds(t, 1), :]
        dgate_ref[...] = dg.astype(bf16)

    cblk = pl.BlockSpec((L, CB), lambda j: (0, j))
    return pl.pallas_call(
        body, grid=(nb,),
        in_specs=[cblk, pl.BlockSpec((L, CB), lambda j: (0, nb + j)), pl.BlockSpec((9, CB), lambda j: (0, j)),
                  pl.BlockSpec((1, CB), lambda j: (0, j)), cblk],
        out_specs=[cblk, cblk, pl.BlockSpec((9, CB), lambda j: (0, j)), pl.BlockSpec((1, CB), lambda j: (0, j))],
        out_shape=[S((L, FH), bf16), S((L, FH), bf16), S((9, FH), f32), S((1, FH), f32)],
        scratch_shapes=[pltpu.VMEM((L + 2 * FFN_PADR, CB), f32), pltpu.VMEM((L + 2 * FFN_PADR, CB), f32)],
        compiler_params=_cparams("parallel"), name=name)(hh, hh, cw, cb_, dact)


CONF_K = 31
CONF_PADR = 16


def conf_glu_conv_fwd(p, b1, wdw, bdw, name):
    L = p.shape[0]
    nb = D // CB
    half = CONF_K // 2

    def body(pa_ref, pg_ref, ba_ref, bg_ref, w_ref, bdw_ref, o_ref, pad_ref):
        glu = (pa_ref[...] + ba_ref[...]) * _sigmoid(pg_ref[...] + bg_ref[...])
        _fill_padded(pad_ref, glu, CONF_PADR, L)
        acc = jnp.broadcast_to(bdw_ref[...], (L, CB))
        for k in range(CONF_K):
            acc = acc + pad_ref[pl.ds(CONF_PADR + k - half, L), :] * w_ref[pl.ds(k, 1), :]
        o_ref[...] = acc

    cblk = pl.BlockSpec((L, CB), lambda j: (0, j))
    gblk = pl.BlockSpec((L, CB), lambda j: (0, nb + j))
    rblk = pl.BlockSpec((1, CB), lambda j: (0, j))
    rgblk = pl.BlockSpec((1, CB), lambda j: (0, nb + j))
    return pl.pallas_call(
        body, grid=(nb,), in_specs=[cblk, gblk, rblk, rgblk, pl.BlockSpec((CONF_K, CB), lambda j: (0, j)), rblk],
        out_specs=cblk, out_shape=S((L, D), f32), scratch_shapes=[pltpu.VMEM((L + 2 * CONF_PADR, CB), f32)],
        compiler_params=_cparams("parallel"), name=name)(p, p, b1, b1, wdw, bdw)


def conf_glu_conv_bwd(p, b1, wdw, bdw, dy, name):
    L = p.shape[0]
    nb = D // CB
    half = CONF_K // 2

    def body(pa_ref, pg_ref, ba_ref, bg_ref, w_ref, dy_ref, dpa_ref, dpg_ref, dba_ref, dbg_ref, dw_ref, dbdw_ref,
             pad_ref, gpad_ref):
        a = pa_ref[...] + ba_ref[...]
        sg = _sigmoid(pg_ref[...] + bg_ref[...])
        _fill_padded(pad_ref, a * sg, CONF_PADR, L)
        dy = dy_ref[...]
        dbdw_ref[...] = jnp.sum(dy, axis=0, keepdims=True)
        _fill_padded(gpad_ref, dy, CONF_PADR, L)
        dglu = jnp.zeros((L, CB), f32)
        for k in range(CONF_K):
            dw_ref[pl.ds(k, 1), :] = jnp.sum(dy * pad_ref[pl.ds(CONF_PADR + k - half, L), :], axis=0, keepdims=True)
            dglu = dglu + gpad_ref[pl.ds(CONF_PADR - (k - half), L), :] * w_ref[pl.ds(k, 1), :]
        dpa = dglu * sg
        dpg = dglu * a * (sg * (1.0 - sg))
        dpa_ref[...] = dpa.astype(bf16)
        dpg_ref[...] = dpg.astype(bf16)
        dba_ref[...] = jnp.sum(dpa, axis=0, keepdims=True)
        dbg_ref[...] = jnp.sum(dpg, axis=0, keepdims=True)

    cblk = pl.BlockSpec((L, CB), lambda j: (0, j))
    gblk = pl.BlockSpec((L, CB), lambda j: (0, nb + j))
    rblk = pl.BlockSpec((1, CB), lambda j: (0, j))
    rgblk = pl.BlockSpec((1, CB), lambda j: (0, nb + j))
    wblk = pl.BlockSpec((CONF_K, CB), lambda j: (0, j))
    del bdw
    return pl.pallas_call(
        body, grid=(nb,), in_specs=[cblk, gblk, rblk, rgblk, wblk, cblk],
        out_specs=[cblk, cblk, rblk, rblk, wblk, rblk],
        out_shape=[S((L, D), bf16), S((L, D), bf16), S((1, D), f32), S((1, D), f32), S((CONF_K, D), f32), S((1, D), f32)],
        scratch_shapes=[pltpu.VMEM((L + 2 * CONF_PADR, CB), f32), pltpu.VMEM((L + 2 * CONF_PADR, CB), f32)],
        compiler_params=_cparams("parallel"), name=name)(p, p, b1, b1, wdw, dy)


def _ln_silu_f(x, w, b):
    mu = jnp.mean(x, axis=-1, keepdims=True)
    d = x - mu
    y = d * lax.rsqrt(jnp.mean(d * d, axis=-1, keepdims=True) + EPS) * w + b
    return y * _sigmoid(y)


def ln_silu_fwd(x, w, b, name):
    T = x.shape[0]

    def body(x_ref, w_ref, b_ref, o_ref):
        o_ref[...] = _ln_silu_f(x_ref[...], w_ref[...], b_ref[...]).astype(bf16)

    blk = pl.BlockSpec((TB, D), lambda i: (i, 0))
    row = pl.BlockSpec((1, D), lambda i: (0, 0))
    return pl.pallas_call(body, grid=(T // TB,), in_specs=[blk, row, row], out_specs=blk, out_shape=S((T, D), bf16),
                          compiler_params=_cparams("parallel"), name=name)(x, w, b)


def ln_silu_bwd(x, w, b, ds, name):
    T = x.shape[0]

    def body(x_ref, w_ref, b_ref, ds_ref, dx_ref, dw_ref, db_ref):
        i = pl.program_id(0)
        _, vjp = jax.vjp(_ln_silu_f, x_ref[...], w_ref[...], b_ref[...])
        dx, dw, db = vjp(ds_ref[...].astype(f32))
        dx_ref[...] = dx

        @pl.when(i == 0)
        def _():
            dw_ref[...] = jnp.zeros_like(dw_ref)
            db_ref[...] = jnp.zeros_like(db_ref)

        dw_ref[...] += dw
        db_ref[...] += db

    blk = pl.BlockSpec((TB, D), lambda i: (i, 0))
    row = pl.BlockSpec((1, D), lambda i: (0, 0))
    return pl.pallas_call(body, grid=(T // TB,), in_specs=[blk, row, row, blk], out_specs=[blk, row, row],
                          out_shape=[S((T, D), f32), S((1, D), f32), S((1, D), f32)],
                          compiler_params=_cparams("arbitrary"), name=name)(x, w, b, ds)


def _lanes_to_rows(v):
    r = lax.broadcasted_iota(jnp.int32, (GW, GW), 0)
    c = lax.broadcasted_iota(jnp.int32, (GW, GW), 1)
    return jnp.sum(jnp.where(r == c, jnp.broadcast_to(v, (GW, GW)), 0.0), axis=1, keepdims=True)


def _ssd_chunk(x, B, C, dtc, dtr, bc, br, alc, alr, s_in, is_fwd):
    row = lax.broadcasted_iota(jnp.int32, (Q, Q), 0)
    col = lax.broadcasted_iota(jnp.int32, (Q, Q), 1)
    sgn = jnp.where(is_fwd, 1, -1).astype(jnp.int32)
    mask = (row - col) * sgn >= 0
    mf = mask.astype(f32)
    er = lax.broadcasted_iota(jnp.int32, (HPG, GW), 0)
    ec = lax.broadcasted_iota(jnp.int32, (HPG, GW), 1)
    E = (ec // P == er).astype(f32)
    dt_c = _softplus(dtc + bc)
    dt_r = _softplus(dtr + br)
    a_c = dt_c * (-jnp.exp(alc))
    a_r = dt_r * (-jnp.exp(alr))
    acum_c = jnp.dot(mf, a_c, precision=HI, preferred_element_type=f32)
    acum_r = lax.dot_general(a_r, mf, (((1,), (1,)), ((), ())), precision=HI, preferred_element_type=f32)
    tot_c = jnp.sum(a_c, axis=0, keepdims=True)
    dt_e = jnp.dot(dt_c, E, precision=HI, preferred_element_type=f32)
    acum_e = jnp.dot(acum_c, E, precision=HI, preferred_element_type=f32)
    tot_e = jnp.dot(tot_c, E, precision=HI, preferred_element_type=f32)
    xdt = x * dt_e
    cb = lax.dot_general(C, B, (((1,), (1,)), ((), ())), preferred_element_type=f32)
    y = jnp.zeros((Q, GW), f32)
    lane_head = lax.broadcasted_iota(jnp.int32, (1, GW), 1) // P
    for r in range(HPG):
        seg = acum_c[:, r:r + 1] - acum_r[r:r + 1, :]
        dec = jnp.exp(jnp.where(mask, seg, -jnp.inf))
        yr = jnp.dot(cb * dec, xdt, preferred_element_type=f32)
        y = y + jnp.where(lane_head == r, yr, 0.0)
    yoff = lax.dot_general(C, s_in, (((1,), (1,)), ((), ())), preferred_element_type=f32)
    y = y + yoff * jnp.exp(acum_e)
    xe = xdt * jnp.exp(tot_e - acum_e)
    st = lax.dot_general(xe, B, (((0,), (0,)), ((), ())), preferred_element_type=f32)
    s_out = _lanes_to_rows(jnp.exp(tot_e)) * s_in + st
    return y, s_out


def _chunk_index(d, t, nctx, nc):
    bwd = jnp.where(t < nctx, nctx - 1 - t, nc - 1 - (t - nctx))
    return jnp.where(d == 0, t, bwd)


def _ssd_in_specs(ci):
    small_c = pl.BlockSpec((None, None, 1, HPG), lambda d, g, t: (d, g, 0, 0))
    small_r = pl.BlockSpec((None, None, HPG, 1), lambda d, g, t: (d, g, 0, 0))
    return [
        pl.BlockSpec((Q, GW), lambda d, g, t: (ci(d, t), g)),
        pl.BlockSpec((Q, N), lambda d, g, t: (ci(d, t), DI // N + g)),
        pl.BlockSpec((Q, N), lambda d, g, t: (ci(d, t), (DI + G * N) // N + g)),
        pl.BlockSpec((None, None, Q, HPG), lambda d, g, t: (d, g, ci(d, t), 0)),
        pl.BlockSpec((None, None, HPG, Q), lambda d, g, t: (d, g, 0, ci(d, t))),
        small_c, small_r, small_c, small_r,
    ]


def ssd_scan_fwd(xbc, dtc, dtr, bc, br, alc, alr, nctx, name):
    T = xbc.shape[0]
    nc = T // Q

    def body(x_ref, b_ref, c_ref, dtc_ref, dtr_ref, bc_ref, br_ref, alc_ref, alr_ref, y_ref, sin_ref, st_ref):
        d = pl.program_id(0)
        t = pl.program_id(2)

        @pl.when(t == 0)
        def _():
            st_ref[...] = jnp.zeros_like(st_ref)

        s_in = st_ref[...]
        sin_ref[...] = s_in
        y, s_out = _ssd_chunk(x_ref[...], b_ref[...], c_ref[...], dtc_ref[...], dtr_ref[...], bc_ref[...], br_ref[...],
                              alc_ref[...], alr_ref[...], s_in, d == 0)
        y_ref[...] = y
        st_ref[...] = s_out

    ci = lambda d, t: _chunk_index(d, t, nctx, nc)
    out_specs = [
        pl.BlockSpec((None, Q, GW), lambda d, g, t: (d, ci(d, t), g)),
        pl.BlockSpec((None, None, None, GW, N), lambda d, g, t: (d, g, ci(d, t), 0, 0)),
    ]
    return pl.pallas_call(
        body, grid=(2, G, nc), in_specs=_ssd_in_specs(ci), out_specs=out_specs,
        out_shape=[S((2, T, DI), f32), S((2, G, nc, GW, N), f32)],
        scratch_shapes=[pltpu.VMEM((GW, N), f32)],
        compiler_params=_cparams("arbitrary", "arbitrary", "arbitrary"), name=name,
    )(xbc, xbc, xbc, dtc, dtr, bc, br, alc, alr)


def ssd_scan_bwd(xbc, dtc, dtr, bc, br, alc, alr, s_in_all, dy, nctx, name):
    T = xbc.shape[0]
    nc = T // Q

    def body(x_ref, b_ref, c_ref, dtc_ref, dtr_ref, bc_ref, br_ref, alc_ref, alr_ref, sin_ref, dy_ref,
             dx_ref, db_ref, dc_ref, ddtc_ref, ddtr_ref, dbc_ref, dbr_ref, dalc_ref, dalr_ref, ds_ref):
        d = pl.program_id(0)
        t = pl.program_id(2)

        @pl.when(t == 0)
        def _():
            ds_ref[...] = jnp.zeros_like(ds_ref)
            dbc_ref[...] = jnp.zeros_like(dbc_ref)
            dbr_ref[...] = jnp.zeros_like(dbr_ref)
            dalc_ref[...] = jnp.zeros_like(dalc_ref)
            dalr_ref[...] = jnp.zeros_like(dalr_ref)

        f = functools.partial(_ssd_chunk, is_fwd=(d == 0))
        _, vjp = jax.vjp(f, x_ref[...], b_ref[...], c_ref[...], dtc_ref[...], dtr_ref[...], bc_ref[...], br_ref[...],
                         alc_ref[...], alr_ref[...], sin_ref[...])
        dx, dB, dC, ddtc, ddtr, dbc, dbr, dalc, dalr, ds = vjp((dy_ref[...], ds_ref[...]))
        dx_ref[...] = dx
        db_ref[...] = dB
        dc_ref[...] = dC
        ddtc_ref[...] = ddtc
        ddtr_ref[...] = ddtr
        dbc_ref[...] += dbc
        dbr_ref[...] += dbr
        dalc_ref[...] += dalc
        dalr_ref[...] += dalr
        ds_ref[...] = ds

    ci = lambda d, t: _chunk_index(d, nc - 1 - t, nctx, nc)
    in_specs = _ssd_in_specs(ci) + [
        pl.BlockSpec((None, None, None, GW, N), lambda d, g, t: (d, g, ci(d, t), 0, 0)),
        pl.BlockSpec((Q, GW), lambda d, g, t: (ci(d, t), g)),
    ]
    small_c = pl.BlockSpec((None, None, 1, HPG), lambda d, g, t: (d, g, 0, 0))
    small_r = pl.BlockSpec((None, None, HPG, 1), lambda d, g, t: (d, g, 0, 0))
    out_specs = [
        pl.BlockSpec((None, Q, GW), lambda d, g, t: (d, ci(d, t), g)),
        pl.BlockSpec((None, Q, N), lambda d, g, t: (d, ci(d, t), g)),
        pl.BlockSpec((None, Q, N), lambda d, g, t: (d, ci(d, t), g)),
        pl.BlockSpec((None, None, Q, HPG), lambda d, g, t: (d, g, ci(d, t), 0)),
        pl.BlockSpec((None, None, HPG, Q), lambda d, g, t: (d, g, 0, ci(d, t))),
        small_c, small_r, small_c, small_r,
    ]
    out_shape = [S((2, T, DI), f32), S((2, T, G * N), f32), S((2, T, G * N), f32),
                 S((2, G, T, HPG), f32), S((2, G, HPG, T), f32),
                 S((2, G, 1, HPG), f32), S((2, G, HPG, 1), f32), S((2, G, 1, HPG), f32), S((2, G, HPG, 1), f32)]
    return pl.pallas_call(
        body, grid=(2, G, nc), in_specs=in_specs, out_specs=out_specs, out_shape=out_shape,
        scratch_shapes=[pltpu.VMEM((GW, N), f32)],
        compiler_params=_cparams("arbitrary", "arbitrary", "arbitrary"), name=name,
    )(xbc, xbc, xbc, dtc, dtr, bc, br, alc, alr, s_in_all, dy)


GTB = 128


def _gate_norm_f(yf, yb, x, z, dexp, w):
    y = (yf + yb + dexp * x) * (z * _sigmoid(z))
    return y * lax.rsqrt(jnp.mean(y * y, axis=-1, keepdims=True) + EPS) * w


def ssd_gate_fwd(y2, xbc, proj, dexp, w, nctxb, name):
    T = xbc.shape[0]
    L = T - nctxb * GTB

    def body(yf_ref, yb_ref, x_ref, z_ref, d_ref, w_ref, o_ref):
        o_ref[...] = _gate_norm_f(yf_ref[...], yb_ref[...], x_ref[...], z_ref[...], d_ref[...], w_ref[...]).astype(bf16)

    wide = pl.BlockSpec((GTB, DI), lambda i: (i + nctxb, 0))
    row = pl.BlockSpec((1, DI), lambda i: (0, 0))
    return pl.pallas_call(
        body, grid=(L // GTB,),
        in_specs=[pl.BlockSpec((None, GTB, DI), lambda i: (0, i + nctxb, 0)),
                  pl.BlockSpec((None, GTB, DI), lambda i: (1, i + nctxb, 0)), wide, wide, row, row],
        out_specs=pl.BlockSpec((GTB, DI), lambda i: (i, 0)), out_shape=S((L, DI), bf16),
        compiler_params=_cparams("parallel"), name=name)(y2, y2, xbc, proj, dexp, w)


def ssd_gate_bwd(y2, xbc, proj, dexp, w, dyn, nctxb, name):
    T = xbc.shape[0]
    nb = T // GTB

    def body(yf_ref, yb_ref, x_ref, z_ref, d_ref, w_ref, dyn_ref, dy_ref, dx_ref, dz_ref, dd_ref, dw_ref):
        i = pl.program_id(0)

        @pl.when(i == 0)
        def _():
            dd_ref[...] = jnp.zeros_like(dd_ref)
            dw_ref[...] = jnp.zeros_like(dw_ref)

        @pl.when(i < nctxb)
        def _():
            dy_ref[...] = jnp.zeros_like(dy_ref)
            dx_ref[...] = jnp.zeros_like(dx_ref)
            dz_ref[...] = jnp.zeros_like(dz_ref)

        @pl.when(i >= nctxb)
        def _():
            _, vjp = jax.vjp(_gate_norm_f, yf_ref[...], yb_ref[...], x_ref[...], z_ref[...], d_ref[...], w_ref[...])
            dyf, _, dx, dz, dd, dw = vjp(dyn_ref[...].astype(f32))
            dy_ref[...] = dyf
            dx_ref[...] = dx
            dz_ref[...] = dz.astype(bf16)
            fold = (lax.broadcasted_iota(jnp.int32, (DI, 128), 0) // P == lax.broadcasted_iota(jnp.int32, (DI, 128), 1))
            dd_ref[...] += jnp.dot(dd, fold.astype(f32), precision=HI, preferred_element_type=f32)
            dw_ref[...] += dw

    wide = pl.BlockSpec((GTB, DI), lambda i: (i, 0))
    row = pl.BlockSpec((1, DI), lambda i: (0, 0))
    hrow = pl.BlockSpec((1, 128), lambda i: (0, 0))
    return pl.pallas_call(
        body, grid=(nb,),
        in_specs=[pl.BlockSpec((None, GTB, DI), lambda i: (0, i, 0)), pl.BlockSpec((None, GTB, DI), lambda i: (1, i, 0)),
                  wide, wide, row, row, pl.BlockSpec((GTB, DI), lambda i: (jnp.maximum(i - nctxb, 0), 0))],
        out_specs=[wide, wide, wide, hrow, row],
        out_shape=[S((T, DI), f32), S((T, DI), f32), S((T, DI), bf16), S((1, 128), f32), S((1, DI), f32)],
        compiler_params=_cparams("arbitrary"), name=name)(y2, y2, xbc, proj, dexp, w, dyn)


CROWS = 2 * N_DEV


def mod_fwd(c16, modw, name):
    nl, _, cols = modw.shape

    def body(c_ref, w_ref, o_ref):
        cv = c_ref[...]
        s = cv * _sigmoid(cv)
        for l in range(nl):
            o_ref[l] = jnp.dot(s, w_ref[l], precision=HI, preferred_element_type=f32)

    return pl.pallas_call(body, in_specs=[VMEM, VMEM], out_specs=VMEM, out_shape=S((nl, CROWS, cols), f32),
                          compiler_params=pltpu.CompilerParams(vmem_limit_bytes=VMEM_LIMIT_BYTES), name=name)(c16, modw)


def mod_bwd(c16, modw, dm_sh, dm_all, name):
    nl, _, cols = modw.shape

    def body(c_ref, w_ref, dm_ref, dmall_ref, dw_ref, dc_ref, db_ref):
        cv = c_ref[...]
        sg = _sigmoid(cv)
        s = cv * sg
        ds_dc = sg * (1.0 + cv * (1.0 - sg))
        is_ctx = lax.broadcasted_iota(jnp.int32, (CROWS, D), 0) >= N_DEV
        dc = jnp.zeros((1, D), f32)
        for l in range(nl):
            dm = dm_ref[l]
            dw_ref[l] = lax.dot_general(s, dm, (((0,), (0,)), ((), ())), precision=HI, preferred_element_type=f32)
            dsv = lax.dot_general(dm, w_ref[l], (((1,), (1,)), ((), ())), precision=HI, preferred_element_type=f32)
            dc = dc + jnp.sum(jnp.where(is_ctx, dsv * ds_dc, 0.0), axis=0, keepdims=True)
            db_ref[pl.ds(l, 1), :] = jnp.sum(dmall_ref[l], axis=0, keepdims=True)
        dc_ref[...] = dc

    return pl.pallas_call(
        body, in_specs=[VMEM, VMEM, VMEM, VMEM], out_specs=[VMEM, VMEM, VMEM],
        out_shape=[S(modw.shape, f32), S((1, D), f32), S((nl, 6 * D), f32)],
        compiler_params=pltpu.CompilerParams(vmem_limit_bytes=VMEM_LIMIT_BYTES), name=name)(c16, modw, dm_sh, dm_all)


def adamw(w, g, m, v, name):
    R, C = w.shape
    rb = R if R <= 512 else max(r_ for r_ in range(8, 513, 8) if R % r_ == 0)
    bc1 = 1.0 - ADAM_B1 ** ADAM_STEP
    bc2 = 1.0 - ADAM_B2 ** ADAM_STEP

    def body(w_ref, g_ref, m_ref, v_ref, d_ref, nm_ref, nv_ref):
        gv = g_ref[...]
        m_new = ADAM_B1 * m_ref[...] + (1.0 - ADAM_B1) * gv
        v_new = ADAM_B2 * v_ref[...] + (1.0 - ADAM_B2) * (gv * gv)
        m_hat = m_new / bc1
        v_hat = v_new / bc2
        d_ref[...] = -ADAM_LR * (m_hat / (jnp.sqrt(v_hat) + ADAM_EPS) + ADAM_WD * w_ref[...])
        nm_ref[...] = m_new
        nv_ref[...] = v_new

    blk = pl.BlockSpec((rb, C), lambda i: (i, 0))
    return pl.pallas_call(body, grid=(R // rb,), in_specs=[blk] * 4, out_specs=[blk] * 3,
                          out_shape=[S((R, C), f32)] * 3, compiler_params=_cparams("parallel"), name=name)(w, g, m, v)


def _me():
    return lax.axis_index("x"), lax.axis_index("y"), lax.axis_index("c")


def allgather_small(x, name, with_sum=False):
    r, w = x.shape

    def body(x_ref, *refs):
        if with_sum:
            out_ref, sum_ref, send_sems, recv_sems = refs
        else:
            out_ref, send_sems, recv_sems = refs
        mx, my, mc = _me()
        me = 4 * mx + 2 * my + mc
        out_ref[me] = x_ref[...]
        peers = []
        for k in range(1, N_DEV):
            kx, ky, kc = (k >> 2) & 1, (k >> 1) & 1, k & 1
            peers.append((mx + kx - 2 * mx * kx, my + ky - 2 * my * ky, mc + kc - 2 * mc * kc))
        copies = []
        for k, peer in enumerate(peers):
            cp = pltpu.make_async_remote_copy(src_ref=x_ref, dst_ref=out_ref.at[me], send_sem=send_sems.at[k],
                                              recv_sem=recv_sems.at[k], device_id=peer, device_id_type=MESH)
            cp.start()
            copies.append(cp)
        for k, (px, py, pc) in enumerate(peers):
            pltpu.make_async_remote_copy(src_ref=x_ref, dst_ref=out_ref.at[4 * px + 2 * py + pc], send_sem=send_sems.at[k],
                                         recv_sem=recv_sems.at[k], device_id=(px, py, pc), device_id_type=MESH).wait_recv()
        for cp in copies:
            cp.wait_send()
        if with_sum:
            acc = out_ref[0]
            for j in range(1, N_DEV):
                acc = acc + out_ref[j]
            sum_ref[...] = acc

    out_shape = [S((N_DEV, r, w), f32)] + ([S((r, w), f32)] if with_sum else [])
    outs = pl.pallas_call(
        body, in_specs=[VMEM], out_specs=[VMEM] * len(out_shape), out_shape=out_shape,
        scratch_shapes=[pltpu.SemaphoreType.DMA((N_DEV - 1,)), pltpu.SemaphoreType.DMA((N_DEV - 1,))],
        compiler_params=pltpu.CompilerParams(vmem_limit_bytes=VMEM_LIMIT_BYTES), name=name)(x)
    return outs if with_sum else outs[0]


def allgather_big(shard, name):
    R, W = shard.shape

    def body(x_ref, out_ref, send_sems, recv_sems, local_sem):
        x, y, c = _me()
        me, sibling = (x, y, c), (x, y, 1 - c)
        chips = [(1 - x, y), (x, 1 - y), (1 - x, 1 - y)]

        def rows(px, py, pc):
            return out_ref.at[4 * px + 2 * py + pc]

        def copy(k, block, to, src=None):
            return pltpu.make_async_remote_copy(
                src_ref=rows(*block) if src is None else src, dst_ref=rows(*block),
                send_sem=send_sems.at[k], recv_sem=recv_sems.at[k], device_id=to, device_id_type=MESH)

        mine = pltpu.make_async_copy(x_ref, rows(*me), local_sem)
        mine.start()
        first = [copy(0, me, sibling, src=x_ref)]
        first += [copy(1 + j, me, (*chip, c), src=x_ref) for j, chip in enumerate(chips)]
        for cp in first:
            cp.start()
        passed = [copy(4 + j, (*chip, c), sibling) for j, chip in enumerate(chips)]
        for j, chip in enumerate(chips):
            copy(1 + j, (*chip, c), me).wait_recv()
            passed[j].start()
        copy(0, sibling, me).wait_recv()
        for j, chip in enumerate(chips):
            copy(4 + j, (*chip, 1 - c), me).wait_recv()
        for cp in first + passed:
            cp.wait_send()
        mine.wait()

    return pl.pallas_call(
        body, in_specs=[ANY], out_specs=ANY, out_shape=S((N_DEV, R, W), shard.dtype),
        scratch_shapes=[pltpu.SemaphoreType.DMA((7,)), pltpu.SemaphoreType.DMA((7,)), pltpu.SemaphoreType.DMA],
        name=name)(shard)


def exchange_sibling(g, name):
    def body(g_ref, out_ref, send_sem, recv_sem):
        x, y, c = _me()
        cp = pltpu.make_async_remote_copy(src_ref=g_ref, dst_ref=out_ref, send_sem=send_sem, recv_sem=recv_sem,
                                          device_id=(x, y, 1 - c), device_id_type=MESH)
        cp.start()
        cp.wait()

    return pl.pallas_call(body, in_specs=[ANY], out_specs=ANY, out_shape=S(g.shape, g.dtype),
                          scratch_shapes=[pltpu.SemaphoreType.DMA, pltpu.SemaphoreType.DMA], name=name)(g)


def exchange_chips(p, name):
    def body(p_ref, out_ref, send_sems, recv_sems, local_sem):
        x, y, c = _me()
        mine = 2 * x + y
        loc = pltpu.make_async_copy(p_ref.at[mine], out_ref.at[mine], local_sem)
        loc.start()
        chips = [(1 - x, y), (x, 1 - y), (1 - x, 1 - y)]
        copies = []
        for j, (px, py) in enumerate(chips):
            cp = pltpu.make_async_remote_copy(src_ref=p_ref.at[2 * px + py], dst_ref=out_ref.at[mine],
                                              send_sem=send_sems.at[j], recv_sem=recv_sems.at[j],
                                              device_id=(px, py, c), device_id_type=MESH)
            cp.start()
            copies.append(cp)
        for j, (px, py) in enumerate(chips):
            pltpu.make_async_remote_copy(src_ref=p_ref.at[mine], dst_ref=out_ref.at[2 * px + py],
                                         send_sem=send_sems.at[j], recv_sem=recv_sems.at[j],
                                         device_id=(px, py, c), device_id_type=MESH).wait_recv()
        for cp in copies:
            cp.wait_send()
        loc.wait()

    return pl.pallas_call(
        body, in_specs=[ANY], out_specs=ANY, out_shape=S(p.shape, p.dtype),
        scratch_shapes=[pltpu.SemaphoreType.DMA((3,)), pltpu.SemaphoreType.DMA((3,)), pltpu.SemaphoreType.DMA],
        name=name)(p)


def add_bf16(a, b, name):
    K, R, W = a.shape
    rb = _pick(R, (512, 256, 128, 64, 32, 16))

    def body(a_ref, b_ref, o_ref):
        o_ref[...] = (a_ref[...].astype(f32) + b_ref[...].astype(f32)).astype(bf16)

    blk = pl.BlockSpec((None, rb, W), lambda k, i: (k, i, 0))
    return pl.pallas_call(body, grid=(K, R // rb), in_specs=[blk, blk], out_specs=blk, out_shape=S((K, R, W), bf16),
                          compiler_params=_cparams("parallel", "parallel"), name=name)(a, b)


def sum_rows(a, name):
    K, R, W = a.shape
    rb = _pick(R, (512, 256, 128, 64, 32, 16))

    def body(a_ref, o_ref):
        acc = a_ref[0].astype(f32)
        for k in range(1, K):
            acc = acc + a_ref[k].astype(f32)
        o_ref[...] = acc

    return pl.pallas_call(body, grid=(R // rb,), in_specs=[pl.BlockSpec((K, rb, W), lambda i: (0, i, 0))],
                          out_specs=pl.BlockSpec((rb, W), lambda i: (i, 0)), out_shape=S((R, W), f32),
                          compiler_params=_cparams("parallel"), name=name)(a)


PACK_ALIGN = 16 * PACK_W


def _pad_to(v, mult):
    n = v.shape[-1]
    extra = (-n) % mult
    if extra == 0:
        return v
    return jnp.concatenate([v, jnp.zeros(v.shape[:-1] + (extra,), v.dtype)], axis=-1)


def _f32_as_bf16_pairs(v):
    return lax.bitcast_convert_type(v.reshape(-1), bf16).reshape(-1)


def _bf16_pairs_as_f32(v):
    return lax.bitcast_convert_type(v.reshape(v.shape[:-1] + (v.shape[-1] // 2, 2)), f32)


def _col_shards(gw):
    lead = gw.shape[:-1]
    n = gw.shape[-1] // N_DEV
    t = gw.reshape(lead + (N_DEV, n))
    t = jnp.moveaxis(t, -2, 0)
    return t.reshape(N_DEV, -1)


def _row_shards(gw):
    return gw.reshape(N_DEV, -1)


def _row_shards_stacked(gw):
    nl, K, n = gw.shape
    t = gw.reshape(nl, N_DEV, K // N_DEV, n)
    return jnp.moveaxis(t, 1, 0).reshape(N_DEV, -1)


def kernel(x, c, ctx, c_ctx, mod_w, mod_b, norm1_w, norm2_w, ssd_w_in, ssd_conv_w, ssd_conv_b, ssd_dt_bias, ssd_a_log, ssd_d, ssd_norm_w, ssd_w_out, conf_w_pw1, conf_b_pw1, conf_w_dw, conf_b_dw, conf_ln_w, conf_ln_b, conf_w_pw2, conf_b_pw2, ffn_w_up, ffn_conv_w, ffn_conv_b, ffn_w_down, final_norm_w, loss_target, m_c_ctx, m_mod_w, m_mod_b, m_norm1_w, m_norm2_w, m_ssd_w_in, m_ssd_conv_w, m_ssd_conv_b, m_ssd_dt_bias, m_ssd_a_log, m_ssd_d, m_ssd_norm_w, m_ssd_w_out, m_conf_w_pw1, m_conf_b_pw1, m_conf_w_dw, m_conf_b_dw, m_conf_ln_w, m_conf_ln_b, m_conf_w_pw2, m_conf_b_pw2, m_ffn_w_up, m_ffn_conv_w, m_ffn_conv_b, m_ffn_w_down, m_final_norm_w, v_c_ctx, v_mod_w, v_mod_b, v_norm1_w, v_norm2_w, v_ssd_w_in, v_ssd_conv_w, v_ssd_conv_b, v_ssd_dt_bias, v_ssd_a_log, v_ssd_d, v_ssd_norm_w, v_ssd_w_out, v_conf_w_pw1, v_conf_b_pw1, v_conf_w_dw, v_conf_b_dw, v_conf_ln_w, v_conf_ln_b, v_conf_w_pw2, v_conf_b_pw2, v_ffn_w_up, v_ffn_conv_w, v_ffn_conv_b, v_ffn_w_down, v_final_norm_w):
    mx, my, mc = _me()
    me = 4 * mx + 2 * my + mc
    L = x.shape[1]
    LC = ctx.shape[1]
    T = LC + L
    w_in_cols = ssd_w_in.shape[2] * N_DEV
    n_dt = w_in_cols - DI - CONVD

    big = [ssd_w_in[0], ssd_w_out[0], conf_w_pw1[0], conf_w_pw2[0], ffn_w_up, ffn_w_down]
    small = [c[0], ssd_conv_w[0], conf_b_pw1[0], conf_w_dw[0], conf_b_dw[0], conf_ln_w[0], conf_ln_b[0], conf_b_pw2[0],
             ffn_conv_w]
    parts = [t.astype(bf16).reshape(-1) for t in big] + [_f32_as_bf16_pairs(t) for t in small]
    sizes = [p.shape[0] for p in parts]
    flat = _pad_to(jnp.concatenate(parts), PACK_ALIGN)
    gath = allgather_big(flat.reshape(-1, PACK_W), "gather_weights").reshape(N_DEV, -1)
    offs = [0]
    for s_ in sizes:
        offs.append(offs[-1] + s_)
    piece = [gath[:, offs[i]:offs[i + 1]] for i in range(len(sizes))]

    def cols(pc, K):
        return jnp.moveaxis(pc.reshape(N_DEV, K, -1), 0, 1).reshape(K, -1)

    w_in = cols(piece[0], D)
    w_out = piece[1].reshape(DI, D)
    w_pw1 = cols(piece[2], D)
    w_pw2 = piece[3].reshape(D, D)
    up = piece[4].reshape(N_DEV, 2, D, -1)
    w_up = [cols(up[:, i].reshape(N_DEV, -1), D) for i in range(2)]
    dn = piece[5].reshape(N_DEV, 2, FH // N_DEV, D)
    w_down = [dn[:, i].reshape(FH, D) for i in range(2)]
    sm = [_bf16_pairs_as_f32(p) for p in piece[6:]]
    c_all = sm[0]
    conv_w5 = cols(sm[1], 5)
    b_pw1 = sm[2].reshape(1, 2 * D)
    w_dw = cols(sm[3], CONF_K)
    b_dw, ln_w, ln_b, b_pw2 = (sm[i].reshape(1, D) for i in (4, 5, 6, 7))
    fcw = sm[8].reshape(N_DEV, 2, 9, FH // N_DEV)
    ffn_cw = [cols(fcw[:, i].reshape(N_DEV, -1), 9) for i in range(2)]
    w_in_main = w_in[:, :DI + CONVD]
    w_in_dt = _pad_to(w_in[:, DI + CONVD:], 128)

    c16 = jnp.concatenate([c_all, jnp.broadcast_to(c_ctx[None, :], (N_DEV, D))], axis=0)
    m_sh = mod_fwd(c16, mod_w, "mod_fwd")
    mod_cols = mod_w.shape[2]
    m_all = allgather_small(m_sh.reshape(2 * CROWS, mod_cols), "gather_mod")
    m_all = jnp.moveaxis(m_all.reshape(N_DEV, 2, CROWS, mod_cols), 0, 2).reshape(2, CROWS, 6 * D) + mod_b[:, None, :]
    m_lat = lax.dynamic_index_in_dim(m_all, me, axis=1, keepdims=False).reshape(2, 6, 1, D)
    m_ctx = m_all[:, N_DEV].reshape(2, 6, 1, D)
    zero_row = jnp.zeros((1, D), f32)

    def ffn_fwd(h, i, tag):
        a2 = modnorm_fwd(h, norm2_w[i][None], m_lat[i, 4][None], m_lat[i, 3][None], 0, f"ffn{tag}_norm")
        hh = matmul(a2, w_up[i], "nn", f32, f"ffn{tag}_up")
        act = ffn_gate_fwd(hh, ffn_cw[i], ffn_conv_b[i][None], f"ffn{tag}_gate")
        o2 = matmul(act, w_down[i], "nn", f32, f"ffn{tag}_down")
        h_new = resgate_fwd(h, o2, m_lat[i, 5], zero_row, f"ffn{tag}_res")
        return h_new, (a2, hh, act, o2)

    def ffn_bwd(dh, h, i, saved, tag):
        a2, hh, act, o2 = saved
        do2, dg2, _ = resgate_bwd(dh, o2, m_lat[i, 5], zero_row, f"ffn{tag}_res_bwd")
        g_down = matmul(act, do2, "tn", f32, f"ffn{tag}_down_dw")
        dact = matmul(do2, w_down[i], "nt", bf16, f"ffn{tag}_down_dx")
        dval, dgate, dcw, dcb = ffn_gate_bwd(hh, ffn_cw[i], ffn_conv_b[i][None], dact, f"ffn{tag}_gate_bwd")
        dhh = jnp.concatenate([dval, dgate], axis=1)
        g_up = matmul(a2, dhh, "tn", f32, f"ffn{tag}_up_dw")
        da2 = matmul(dhh, w_up[i], "nt", bf16, f"ffn{tag}_up_dx")
        dh_in, dn2, dsc2, dsh2 = modnorm_bwd(h, norm2_w[i][None], m_lat[i, 4][None], m_lat[i, 3][None], da2, dh, 0,
                                             f"ffn{tag}_norm_bwd")
        return dh_in, dict(w_up=g_up, w_down=g_down, conv_w=dcw, conv_b=dcb, norm2=dn2, sh2=dsh2[0], sc2=dsc2[0], g2=dg2)

    nctx = LC // Q
    h0 = jnp.concatenate([ctx[0], x[0]], axis=0)
    sc0 = jnp.stack([m_ctx[0, 1], m_lat[0, 1]])
    sh0 = jnp.stack([m_ctx[0, 0], m_lat[0, 0]])
    a0 = modnorm_fwd(h0, norm1_w[0][None], sc0, sh0, LC // TB, "ssd_norm")
    proj = matmul(a0, w_in_main, "nn", f32, "ssd_in")
    dt_raw = matmul(a0, w_in_dt, "nn", f32, "ssd_in_dt")
    segs = ((0, LC), (LC, L))
    xbc = ssd_conv_fwd(proj, conv_w5, ssd_conv_b, segs, "ssd_conv")
    dt4 = dt_raw[:, :n_dt].reshape(T, 2, G, HPG)
    dtc = jnp.transpose(dt4, (1, 2, 0, 3))
    dtr = jnp.transpose(dt4, (1, 2, 3, 0))
    bias3 = ssd_dt_bias[0].reshape(2, G, HPG)
    alog3 = ssd_a_log[0].reshape(2, G, HPG)
    bc_, br_ = bias3[:, :, None, :], bias3[:, :, :, None]
    alc, alr = alog3[:, :, None, :], alog3[:, :, :, None]
    y2, s_in_all = ssd_scan_fwd(xbc, dtc, dtr, bc_, br_, alc, alr, nctx, "ssd_scan")
    dexp = jnp.repeat(ssd_d[0], P)[None, :]
    yn = ssd_gate_fwd(y2, xbc, proj, dexp, ssd_norm_w, LC // GTB, "ssd_gate")
    o_ssd = matmul(yn, w_out, "nn", f32, "ssd_out")
    hx = x[0]
    h1 = resgate_fwd(hx, o_ssd, m_lat[0, 2], zero_row, "ssd_res")
    h2, ffn0_saved = ffn_fwd(h1, 0, "0")

    a1 = modnorm_fwd(h2, norm1_w[1][None], m_lat[1, 1][None], m_lat[1, 0][None], 0, "conf_norm")
    p1 = matmul(a1, w_pw1, "nn", f32, "conf_pw1")
    dwc = conf_glu_conv_fwd(p1, b_pw1, w_dw, b_dw, "conf_conv")
    s1 = ln_silu_fwd(dwc, ln_w, ln_b, "conf_ln")
    o_conf = matmul(s1, w_pw2, "nn", f32, "conf_pw2")
    h3 = resgate_fwd(h2, o_conf, m_lat[1, 2], b_pw2, "conf_res")
    h4, ffn1_saved = ffn_fwd(h3, 1, "1")

    loss_part, dh4, g_final = final_loss(h4, final_norm_w[None], loss_target[0], "loss_head")
    dh3, gf1 = ffn_bwd(dh4, h3, 1, ffn1_saved, "1")

    do_conf, dg1_1, g_b_pw2 = resgate_bwd(dh3, o_conf, m_lat[1, 2], b_pw2, "conf_res_bwd")
    g_pw2 = matmul(s1, do_conf, "tn", f32, "conf_pw2_dw")
    ds1 = matmul(do_conf, w_pw2, "nt", bf16, "conf_pw2_dx")
    ddwc, g_ln_w, g_ln_b = ln_silu_bwd(dwc, ln_w, ln_b, ds1, "conf_ln_bwd")
    dpa, dpg, dba, dbg, g_w_dw, g_b_dw = conf_glu_conv_bwd(p1, b_pw1, w_dw, b_dw, ddwc, "conf_conv_bwd")
    dp1 = jnp.concatenate([dpa, dpg], axis=1)
    g_b_pw1 = jnp.concatenate([dba, dbg], axis=1)
    g_pw1 = matmul(a1, dp1, "tn", f32, "conf_pw1_dw")
    da1 = matmul(dp1, w_pw1, "nt", bf16, "conf_pw1_dx")
    dh2, g_n1_1, dsc1_1, dsh1_1 = modnorm_bwd(h2, norm1_w[1][None], m_lat[1, 1][None], m_lat[1, 0][None], da1, dh3, 0,
                                              "conf_norm_bwd")
    dh1, gf0 = ffn_bwd(dh2, h1, 0, ffn0_saved, "0")

    do_ssd, dg1_0, _ = resgate_bwd(dh1, o_ssd, m_lat[0, 2], zero_row, "ssd_res_bwd")
    g_w_out = matmul(yn, do_ssd, "tn", f32, "ssd_out_dw")
    dyn = matmul(do_ssd, w_out, "nt", bf16, "ssd_out_dx")
    dy, dx_skip, dz, g_dexp, g_ssd_norm = ssd_gate_bwd(y2, xbc, proj, dexp, ssd_norm_w, dyn, LC // GTB, "ssd_gate_bwd")
    dxs, dBs, dCs, ddtc, ddtr, dbc, dbr, dalc, dalr = ssd_scan_bwd(xbc, dtc, dtr, bc_, br_, alc, alr, s_in_all, dy, nctx,
                                                                   "ssd_scan_bwd")
    dxbc = jnp.concatenate([dxs[0] + dxs[1] + dx_skip, dBs[0] + dBs[1], dCs[0] + dCs[1]], axis=1)
    ddt = (jnp.transpose(ddtc, (2, 0, 1, 3)) + jnp.transpose(ddtr, (3, 0, 1, 2))).reshape(T, n_dt)
    g_dt_bias = (dbc[:, :, 0, :] + dbr[:, :, :, 0]).reshape(2, NH_SSD)
    g_a_log = (dalc[:, :, 0, :] + dalr[:, :, :, 0]).reshape(2, NH_SSD)
    g_ssd_d = g_dexp[0, :NH_SSD]
    du, g_conv_w5, g_conv_b5 = ssd_conv_bwd(proj, conv_w5, ssd_conv_b, dxbc, segs, "ssd_conv_bwd")
    dproj = jnp.concatenate([dz, du], axis=1)
    ddt_p = _pad_to(ddt, 128).astype(bf16)
    g_w_in = jnp.concatenate([matmul(a0, dproj, "tn", f32, "ssd_in_dw"),
                              matmul(a0, ddt_p, "tn", f32, "ssd_in_dt_dw")[:, :n_dt]], axis=1)
    da0 = matmul(dproj, w_in_main, "nt", f32, "ssd_in_dx") + matmul(ddt_p, w_in_dt, "nt", f32, "ssd_in_dt_dx")
    dres0 = jnp.concatenate([jnp.zeros((LC, D), f32), dh1], axis=0)
    dh0, g_n1_0, dsc1_0, dsh1_0 = modnorm_bwd(h0, norm1_w[0][None], sc0, sh0, da0, dres0, LC // TB, "ssd_norm_bwd")
    grad_x = dh0[LC:][None]

    zeros_d = jnp.zeros((1, D), f32)
    dm_lat = jnp.stack([
        jnp.concatenate([dsh1_0[1], dsc1_0[1], dg1_0, gf0["sh2"], gf0["sc2"], gf0["g2"]], axis=1),
        jnp.concatenate([dsh1_1[0], dsc1_1[0], dg1_1, gf1["sh2"], gf1["sc2"], gf1["g2"]], axis=1)])
    dm_ctx = jnp.stack([
        jnp.concatenate([dsh1_0[0], dsc1_0[0]] + [zeros_d] * 4, axis=1), jnp.zeros((1, 6 * D), f32)])
    dm_mine = jnp.concatenate([dm_lat.reshape(2, 6 * D), dm_ctx.reshape(2, 6 * D),
                               jnp.zeros((4, 6 * D), f32)], axis=0)
    dm_g = allgather_small(dm_mine, "gather_dmod")
    dm_all = jnp.concatenate([jnp.moveaxis(dm_g[:, 0:2], 0, 1), jnp.moveaxis(dm_g[:, 2:4], 0, 1)], axis=1)
    dm_sh = lax.dynamic_slice_in_dim(dm_all, me * mod_cols, mod_cols, axis=2)
    g_mod_w, g_cctx_part, g_mod_b = mod_bwd(c16, mod_w, dm_sh, dm_all, "mod_bwd")

    rep = [jnp.stack([g_n1_0[0], g_n1_1[0]]), jnp.stack([gf0["norm2"][0], gf1["norm2"][0]]), g_conv_b5, g_dt_bias, g_a_log,
           g_ssd_d, g_ssd_norm, jnp.stack([gf0["conv_b"][0], gf1["conv_b"][0]]), g_final, g_cctx_part, loss_part[:, :1]]
    rep_sizes = [r_.size for r_ in rep]
    rep_flat = _pad_to(jnp.concatenate([r_.reshape(-1) for r_ in rep]), 8 * PACK_W).reshape(-1, PACK_W)
    _, rep_sum = allgather_small(rep_flat, "reduce_replicated", with_sum=True)
    rep_sum = rep_sum.reshape(-1)
    roffs = [0]
    for s_ in rep_sizes:
        roffs.append(roffs[-1] + s_)
    rp = [rep_sum[roffs[i]:roffs[i + 1]] for i in range(len(rep_sizes))]
    loss = rp[10].reshape(())

    g_ffn_up = jnp.stack([gf0["w_up"], gf1["w_up"]])
    g_ffn_down = jnp.stack([gf0["w_down"], gf1["w_down"]])
    g_ffn_cw = jnp.stack([gf0["conv_w"], gf1["conv_w"]])
    shards = [_col_shards(g_w_in), _row_shards(g_w_out), _col_shards(g_pw1), _row_shards(g_pw2), _col_shards(g_ffn_up),
              _row_shards_stacked(g_ffn_down), _col_shards(g_conv_w5), _col_shards(g_b_pw1), _col_shards(g_w_dw),
              _col_shards(g_b_dw), _col_shards(g_ln_w), _col_shards(g_ln_b), _col_shards(g_b_pw2), _col_shards(g_ffn_cw)]
    gsizes = [s_.shape[1] for s_ in shards]

    def pack(core):
        rows = [lax.dynamic_index_in_dim(s_.reshape(4, 2, -1), core, axis=1, keepdims=False).astype(bf16) for s_ in shards]
        return _pad_to(jnp.concatenate(rows, axis=1), PACK_ALIGN).reshape(4, -1, PACK_W)

    from_sibling = exchange_sibling(pack(1 - mc), "reduce_sibling")
    chip_part = add_bf16(pack(mc), from_sibling, "reduce_add")
    from_chips = exchange_chips(chip_part, "reduce_chips")
    g_flat = sum_rows(from_chips, "reduce_sum").reshape(-1)
    goffs = [0]
    for s_ in gsizes:
        goffs.append(goffs[-1] + s_)
    gs = [g_flat[goffs[i]:goffs[i + 1]] for i in range(len(gsizes))]

    grads = {
        "c_ctx": rp[9], "mod_w": g_mod_w, "mod_b": g_mod_b, "norm1_w": rp[0], "norm2_w": rp[1],
        "ssd_w_in": gs[0], "ssd_conv_w": gs[6], "ssd_conv_b": rp[2], "ssd_dt_bias": rp[3], "ssd_a_log": rp[4], "ssd_d": rp[5],
        "ssd_norm_w": rp[6], "ssd_w_out": gs[1], "conf_w_pw1": gs[2], "conf_b_pw1": gs[7], "conf_w_dw": gs[8],
        "conf_b_dw": gs[9], "conf_ln_w": gs[10], "conf_ln_b": gs[11], "conf_w_pw2": gs[3], "conf_b_pw2": gs[12],
        "ffn_w_up": gs[4], "ffn_conv_w": gs[13], "ffn_conv_b": rp[7], "ffn_w_down": gs[5], "final_norm_w": rp[8],
    }
    weights = dict(c_ctx=c_ctx, mod_w=mod_w, mod_b=mod_b, norm1_w=norm1_w, norm2_w=norm2_w, ssd_w_in=ssd_w_in, ssd_conv_w=ssd_conv_w, ssd_conv_b=ssd_conv_b, ssd_dt_bias=ssd_dt_bias, ssd_a_log=ssd_a_log, ssd_d=ssd_d, ssd_norm_w=ssd_norm_w, ssd_w_out=ssd_w_out, conf_w_pw1=conf_w_pw1, conf_b_pw1=conf_b_pw1, conf_w_dw=conf_w_dw, conf_b_dw=conf_b_dw, conf_ln_w=conf_ln_w, conf_ln_b=conf_ln_b, conf_w_pw2=conf_w_pw2, conf_b_pw2=conf_b_pw2, ffn_w_up=ffn_w_up, ffn_conv_w=ffn_conv_w, ffn_conv_b=ffn_conv_b, ffn_w_down=ffn_w_down, final_norm_w=final_norm_w)
    m_in = dict(c_ctx=m_c_ctx, mod_w=m_mod_w, mod_b=m_mod_b, norm1_w=m_norm1_w, norm2_w=m_norm2_w, ssd_w_in=m_ssd_w_in, ssd_conv_w=m_ssd_conv_w, ssd_conv_b=m_ssd_conv_b, ssd_dt_bias=m_ssd_dt_bias, ssd_a_log=m_ssd_a_log, ssd_d=m_ssd_d, ssd_norm_w=m_ssd_norm_w, ssd_w_out=m_ssd_w_out, conf_w_pw1=m_conf_w_pw1, conf_b_pw1=m_conf_b_pw1, conf_w_dw=m_conf_w_dw, conf_b_dw=m_conf_b_dw, conf_ln_w=m_conf_ln_w, conf_ln_b=m_conf_ln_b, conf_w_pw2=m_conf_w_pw2, conf_b_pw2=m_conf_b_pw2, ffn_w_up=m_ffn_w_up, ffn_conv_w=m_ffn_conv_w, ffn_conv_b=m_ffn_conv_b, ffn_w_down=m_ffn_w_down, final_norm_w=m_final_norm_w)
    v_in = dict(c_ctx=v_c_ctx, mod_w=v_mod_w, mod_b=v_mod_b, norm1_w=v_norm1_w, norm2_w=v_norm2_w, ssd_w_in=v_ssd_w_in, ssd_conv_w=v_ssd_conv_w, ssd_conv_b=v_ssd_conv_b, ssd_dt_bias=v_ssd_dt_bias, ssd_a_log=v_ssd_a_log, ssd_d=v_ssd_d, ssd_norm_w=v_ssd_norm_w, ssd_w_out=v_ssd_w_out, conf_w_pw1=v_conf_w_pw1, conf_b_pw1=v_conf_b_pw1, conf_w_dw=v_conf_w_dw, conf_b_dw=v_conf_b_dw, conf_ln_w=v_conf_ln_w, conf_ln_b=v_conf_ln_b, conf_w_pw2=v_conf_w_pw2, conf_b_pw2=v_conf_b_pw2, ffn_w_up=v_ffn_w_up, ffn_conv_w=v_ffn_conv_w, ffn_conv_b=v_ffn_conv_b, ffn_w_down=v_ffn_w_down, final_norm_w=v_final_norm_w)

    out_g, out_d, out_m, out_v = [], [], [], []
    for name_, w_ in weights.items():
        shape = w_.shape
        cols2 = shape[-1] if len(shape) > 1 else shape[0]
        g2 = grads[name_].reshape(-1, cols2)
        d_, nm_, nv_ = adamw(w_.reshape(-1, cols2), g2, m_in[name_].reshape(-1, cols2), v_in[name_].reshape(-1, cols2),
                             f"adamw_{name_}")
        out_g.append(g2.reshape(shape))
        out_d.append(d_.reshape(shape))
        out_m.append(nm_.reshape(shape))
        out_v.append(nv_.reshape(shape))
    return (loss, grad_x, *out_g, *out_d, *out_m, *out_v)
```

```python
import functools

import jax
import jax.numpy as jnp
from jax import lax
from jax.experimental import pallas as pl
from jax.experimental.pallas import tpu as pltpu

f32 = jnp.float32
bf16 = jnp.bfloat16
HI = lax.Precision.HIGHEST
S = jax.ShapeDtypeStruct
MESH = pl.DeviceIdType.MESH
ANY = pl.BlockSpec(memory_space=pl.ANY)
VMEM = pl.BlockSpec(memory_space=pltpu.VMEM)

N_DEV = 8
D = 1024
DI = 2048
CONVD = 4096
FH = 2816
GRID_W = 64
Q = 128
HPG = 4
P = 64
N = 128
G = 8
GW = HPG * P
NH_SSD = G * HPG
EPS = 1e-6
ADAM_LR, ADAM_B1, ADAM_B2, ADAM_EPS, ADAM_WD, ADAM_STEP = 0.001, 0.9, 0.999, 1e-08, 0.01, 10
VMEM_LIMIT_BYTES = 56 * 1024 * 1024
PACK_W = 1024
TB = 256


def _cparams(*sem):
    return pltpu.CompilerParams(dimension_semantics=sem, vmem_limit_bytes=VMEM_LIMIT_BYTES)


def _pick(n, prefs):
    for p in prefs:
        if n % p == 0:
            return p
    return n


def _sigmoid(x):
    return 1.0 / (1.0 + jnp.exp(-x))


def _softplus(x):
    return jnp.maximum(x, 0.0) + jnp.log(1.0 + jnp.exp(-jnp.abs(x)))


def matmul(a, b, mode, out_dtype, name):
    if mode == "nn":
        (M, K), (_, Nn) = a.shape, b.shape
    elif mode == "tn":
        (K, M), (_, Nn) = a.shape, b.shape
    else:
        (M, K), (Nn, _) = a.shape, b.shape
    bm = _pick(M, (512, 384, 256, 128))
    bn = _pick(Nn, (512, 384, 256, 128))
    bk = _pick(K, (1024, 768, 512, 256, 128))
    nk = K // bk
    dims = {"nn": (((1,), (0,)), ((), ())), "tn": (((0,), (0,)), ((), ())), "nt": (((1,), (1,)), ((), ()))}[mode]

    def body(a_ref, b_ref, o_ref, acc_ref):
        k = pl.program_id(2)

        @pl.when(k == 0)
        def _():
            acc_ref[...] = jnp.zeros_like(acc_ref)

        acc_ref[...] += lax.dot_general(a_ref[...].astype(bf16), b_ref[...].astype(bf16), dims,
                                        preferred_element_type=f32)

        @pl.when(k == nk - 1)
        def _():
            o_ref[...] = acc_ref[...].astype(out_dtype)

    if mode == "nn":
        a_spec = pl.BlockSpec((bm, bk), lambda i, j, k: (i, k))
        b_spec = pl.BlockSpec((bk, bn), lambda i, j, k: (k, j))
    elif mode == "tn":
        a_spec = pl.BlockSpec((bk, bm), lambda i, j, k: (k, i))
        b_spec = pl.BlockSpec((bk, bn), lambda i, j, k: (k, j))
    else:
        a_spec = pl.BlockSpec((bm, bk), lambda i, j, k: (i, k))
        b_spec = pl.BlockSpec((bn, bk), lambda i, j, k: (j, k))
    return pl.pallas_call(
        body, grid=(M // bm, Nn // bn, nk), in_specs=[a_spec, b_spec],
        out_specs=pl.BlockSpec((bm, bn), lambda i, j, k: (i, j)),
        out_shape=S((M, Nn), out_dtype), scratch_shapes=[pltpu.VMEM((bm, bn), f32)],
        compiler_params=_cparams("parallel", "parallel", "arbitrary"), name=name,
    )(a, b)


SMM_BM = 256


def _shard_pieces(seg_widths, n):
    bounds = [0]
    for sw in seg_widths:
        bounds.append(bounds[-1] + sw)
    assert bounds[-1] == N_DEV * n, (seg_widths, n)
    out = []
    for j in range(N_DEV):
        lo, hi = j * n, (j + 1) * n
        pcs = []
        for si in range(len(seg_widths)):
            a, b = max(lo, bounds[si]), min(hi, bounds[si + 1])
            if a < b:
                pcs.append((si, a - bounds[si], a - lo, b - a))
        out.append(pcs)
    return out


def _w_spec(w, layer):
    if layer is None:
        return pl.BlockSpec(w.shape, lambda *idx: (0, 0, 0))
    return pl.BlockSpec((N_DEV, None) + w.shape[2:], lambda *idx: (0, layer, 0, 0))


def smm_fwd(a, w, layer, seg_widths, name):
    M, K = a.shape
    n = w.shape[-1]
    pieces = _shard_pieces(seg_widths, n)
    padded = [sw + (-sw) % 128 for sw in seg_widths]

    def body(a_ref, w_ref, *o_refs):
        av = a_ref[...]
        for si, sw in enumerate(seg_widths):
            if padded[si] != sw:
                o_refs[si][:, pl.ds(padded[si] - 128, 128)] = jnp.zeros((SMM_BM, 128), f32)
        for j in range(N_DEV):
            for si, soff, woff, wd in pieces[j]:
                o_refs[si][:, pl.ds(soff, wd)] = jnp.dot(av, w_ref[j, :, pl.ds(woff, wd)], preferred_element_type=f32)

    return pl.pallas_call(
        body, grid=(M // SMM_BM,), in_specs=[pl.BlockSpec((SMM_BM, K), lambda i: (i, 0)), _w_spec(w, layer)],
        out_specs=[pl.BlockSpec((SMM_BM, pw), lambda i: (i, 0)) for pw in padded],
        out_shape=[S((M, pw), f32) for pw in padded], compiler_params=_cparams("parallel"), name=name)(a, w)


def smm_dx(d_segs, w, layer, seg_widths, out_dtype, name):
    M = d_segs[0].shape[0]
    K, n = w.shape[-2], w.shape[-1]
    pieces = _shard_pieces(seg_widths, n)
    ns = len(d_segs)

    def body(*refs):
        d_refs, w_ref, o_ref = refs[:ns], refs[ns], refs[ns + 1]
        acc = jnp.zeros((SMM_BM, K), f32)
        for j in range(N_DEV):
            for si, soff, woff, wd in pieces[j]:
                acc = acc + lax.dot_general(d_refs[si][:, pl.ds(soff, wd)], w_ref[j, :, pl.ds(woff, wd)],
                                            (((1,), (1,)), ((), ())), preferred_element_type=f32)
        o_ref[...] = acc.astype(out_dtype)

    return pl.pallas_call(
        body, grid=(M // SMM_BM,),
        in_specs=[pl.BlockSpec((SMM_BM, d.shape[1]), lambda i: (i, 0)) for d in d_segs] + [_w_spec(w, layer)],
        out_specs=pl.BlockSpec((SMM_BM, K), lambda i: (i, 0)), out_shape=S((M, K), out_dtype),
        compiler_params=_cparams("parallel"), name=name)(*d_segs, w)


def smm_dw(a, d_segs, n, seg_widths, ngrp, name):
    M, K = a.shape
    pieces = _shard_pieces(seg_widths, n)
    per = N_DEV // ngrp
    nI = M // SMM_BM
    ns = len(d_segs)

    def body(*refs):
        a_ref, d_refs, o_ref, acc_ref = refs[0], refs[1:1 + ns], refs[1 + ns], refs[2 + ns]
        grp = pl.program_id(0)
        i = pl.program_id(1)

        @pl.when(i == 0)
        def _():
            acc_ref[...] = jnp.zeros_like(acc_ref)

        av = a_ref[...]
        for gs in range(ngrp):
            def one_group(gs=gs):
                for jj in range(per):
                    for si, soff, woff, wd in pieces[gs * per + jj]:
                        acc_ref[jj, :, pl.ds(woff, wd)] += lax.dot_general(
                            av, d_refs[si][:, pl.ds(soff, wd)], (((0,), (0,)), ((), ())), preferred_element_type=f32)
            pl.when(grp == gs)(one_group)

        @pl.when(i == nI - 1)
        def _():
            o_ref[...] = acc_ref[...].astype(bf16)

    return pl.pallas_call(
        body, grid=(ngrp, nI),
        in_specs=[pl.BlockSpec((SMM_BM, K), lambda g, i: (i, 0))]
        + [pl.BlockSpec((SMM_BM, d.shape[1]), lambda g, i: (i, 0)) for d in d_segs],
        out_specs=pl.BlockSpec((per, K, n), lambda g, i: (g, 0, 0)), out_shape=S((N_DEV, K, n), bf16),
        scratch_shapes=[pltpu.VMEM((per, K, n), f32)],
        compiler_params=_cparams("arbitrary", "arbitrary"), name=name)(a, *d_segs)


def _modnorm_f(h, w, sc, sh):
    y = h * lax.rsqrt(jnp.mean(h * h, axis=-1, keepdims=True) + EPS)
    return (y * w) * (1.0 + sc) + sh


def _kind_specs(nctxb):
    if nctxb > 0:
        return pl.BlockSpec((None, 1, D), lambda i: (jnp.where(i < nctxb, 0, 1), 0, 0))
    return pl.BlockSpec((None, 1, D), lambda i: (0, 0, 0))


def modnorm_fwd(h, w, sc, sh, nctxb, name):
    T = h.shape[0]

    def body(h_ref, w_ref, sc_ref, sh_ref, o_ref):
        o_ref[...] = _modnorm_f(h_ref[...], w_ref[...], sc_ref[...], sh_ref[...]).astype(bf16)

    blk = pl.BlockSpec((TB, D), lambda i: (i, 0))
    row = pl.BlockSpec((1, D), lambda i: (0, 0))
    ks = _kind_specs(nctxb)
    return pl.pallas_call(body, grid=(T // TB,), in_specs=[blk, row, ks, ks], out_specs=blk,
                          out_shape=S((T, D), bf16), compiler_params=_cparams("parallel"), name=name)(h, w, sc, sh)


def modnorm_bwd(h, w, sc, sh, da, dres, nctxb, name):
    T = h.shape[0]
    kinds = sc.shape[0]

    def body(h_ref, w_ref, sc_ref, sh_ref, da_ref, dres_ref, dh_ref, dw_ref, dsc_ref, dsh_ref):
        i = pl.program_id(0)
        _, vjp = jax.vjp(_modnorm_f, h_ref[...], w_ref[...], sc_ref[...], sh_ref[...])
        dh, dw, dsc, dsh = vjp(da_ref[...].astype(f32))
        dh_ref[...] = dres_ref[...] + dh

        @pl.when(i == 0)
        def _():
            dw_ref[...] = jnp.zeros_like(dw_ref)

        @pl.when((i == 0) | (i == nctxb))
        def _():
            dsc_ref[...] = jnp.zeros_like(dsc_ref)
            dsh_ref[...] = jnp.zeros_like(dsh_ref)

        dw_ref[...] += dw
        dsc_ref[...] += dsc
        dsh_ref[...] += dsh

    blk = pl.BlockSpec((TB, D), lambda i: (i, 0))
    row = pl.BlockSpec((1, D), lambda i: (0, 0))
    ks = _kind_specs(nctxb)
    return pl.pallas_call(
        body, grid=(T // TB,), in_specs=[blk, row, ks, ks, blk, blk], out_specs=[blk, row, ks, ks],
        out_shape=[S((T, D), f32), S((1, D), f32), S((kinds, 1, D), f32), S((kinds, 1, D), f32)],
        compiler_params=_cparams("arbitrary"), name=name)(h, w, sc, sh, da, dres)


def resgate_fwd(h, o, g, b, name):
    T = h.shape[0]

    def body(h_ref, o_ref, g_ref, b_ref, out_ref):
        out_ref[...] = h_ref[...] + g_ref[...] * (o_ref[...] + b_ref[...])

    blk = pl.BlockSpec((TB, D), lambda i: (i, 0))
    row = pl.BlockSpec((1, D), lambda i: (0, 0))
    return pl.pallas_call(body, grid=(T // TB,), in_specs=[blk, blk, row, row], out_specs=blk,
                          out_shape=S((T, D), f32), compiler_params=_cparams("parallel"), name=name)(h, o, g, b)


def resgate_bwd(dh, o, g, b, name):
    T = dh.shape[0]

    def body(dh_ref, o_ref, g_ref, b_ref, do_ref, dg_ref, db_ref):
        i = pl.program_id(0)

        @pl.when(i == 0)
        def _():
            dg_ref[...] = jnp.zeros_like(dg_ref)
            db_ref[...] = jnp.zeros_like(db_ref)

        dh = dh_ref[...]
        do = g_ref[...] * dh
        do_ref[...] = do.astype(bf16)
        dg_ref[...] += jnp.sum(dh * (o_ref[...] + b_ref[...]), axis=0, keepdims=True)
        db_ref[...] += jnp.sum(do, axis=0, keepdims=True)

    blk = pl.BlockSpec((TB, D), lambda i: (i, 0))
    row = pl.BlockSpec((1, D), lambda i: (0, 0))
    return pl.pallas_call(body, grid=(T // TB,), in_specs=[blk, blk, row, row], out_specs=[blk, row, row],
                          out_shape=[S((T, D), bf16), S((1, D), f32), S((1, D), f32)],
                          compiler_params=_cparams("arbitrary"), name=name)(dh, o, g, b)


def final_loss(h, w, tgt, name):
    T = h.shape[0]

    def f(hv, wv, tv):
        y = (hv * lax.rsqrt(jnp.mean(hv * hv, axis=-1, keepdims=True) + EPS)) * wv
        e = y - tv
        return 0.5 * jnp.sum(jnp.sum(e * e, axis=-1, keepdims=True), axis=0, keepdims=True) * (1.0 / D)

    def body(h_ref, w_ref, t_ref, loss_ref, dh_ref, dw_ref):
        i = pl.program_id(0)
        tv = t_ref[...]
        val, vjp = jax.vjp(lambda a, b_: f(a, b_, tv), h_ref[...], w_ref[...])
        dh, dw = vjp(jnp.ones((1, 1), f32))
        dh_ref[...] = dh

        @pl.when(i == 0)
        def _():
            loss_ref[...] = jnp.zeros_like(loss_ref)
            dw_ref[...] = jnp.zeros_like(dw_ref)

        loss_ref[...] += jnp.broadcast_to(val, (1, 128))
        dw_ref[...] += dw

    blk = pl.BlockSpec((TB, D), lambda i: (i, 0))
    row = pl.BlockSpec((1, D), lambda i: (0, 0))
    return pl.pallas_call(body, grid=(T // TB,), in_specs=[blk, row, blk],
                          out_specs=[pl.BlockSpec((1, 128), lambda i: (0, 0)), blk, row],
                          out_shape=[S((1, 128), f32), S((T, D), f32), S((1, D), f32)],
                          compiler_params=_cparams("arbitrary"), name=name)(h, w, tgt)


CB = 256


def _fill_padded(pad_ref, vals, padr, ln):
    pad_ref[pl.ds(0, padr), :] = jnp.zeros((padr, CB), f32)
    pad_ref[pl.ds(padr + ln, padr), :] = jnp.zeros((padr, CB), f32)
    pad_ref[pl.ds(padr, ln), :] = vals


def ssd_conv_fwd(proj, w, b, segs, name):
    T = proj.shape[0]
    K, half, padr = 5, 2, 8
    maxlen = max(ln for _, ln in segs)

    def body(u_ref, w_ref, b_ref, o_ref, pad_ref):
        for s0, ln in segs:
            _fill_padded(pad_ref, u_ref[pl.ds(s0, ln), :], padr, ln)
            acc = jnp.broadcast_to(b_ref[...], (ln, CB))
            for k in range(K):
                acc = acc + pad_ref[pl.ds(padr + k - half, ln), :] * w_ref[pl.ds(k, 1), :]
            o_ref[pl.ds(s0, ln), :] = acc * _sigmoid(acc)

    return pl.pallas_call(
        body, grid=(CONVD // CB,),
        in_specs=[pl.BlockSpec((T, CB), lambda j: (0, j)), pl.BlockSpec((K, CB), lambda j: (0, j)),
                  pl.BlockSpec((1, CB), lambda j: (0, j))],
        out_specs=pl.BlockSpec((T, CB), lambda j: (0, j)), out_shape=S((T, CONVD), f32),
        scratch_shapes=[pltpu.VMEM((maxlen + 2 * padr, CB), f32)],
        compiler_params=_cparams("parallel"), name=name)(proj, w, b)


def ssd_conv_bwd(proj, w, b, dy2, dskip, segs, name):
    T = proj.shape[0]
    K, half, padr = 5, 2, 8
    maxlen = max(ln for _, ln in segs)
    nskip = DI // CB

    def body(u_ref, w_ref, b_ref, dya_ref, dyb_ref, dsk_ref, du_ref, dw_ref, db_ref, pad_ref, gpad_ref):
        dw_acc = [jnp.zeros((1, CB), f32) for _ in range(K)]
        db_acc = jnp.zeros((1, CB), f32)
        has_skip = (pl.program_id(0) < nskip).astype(f32)
        for s0, ln in segs:
            _fill_padded(pad_ref, u_ref[pl.ds(s0, ln), :], padr, ln)
            pre = jnp.broadcast_to(b_ref[...], (ln, CB))
            for k in range(K):
                pre = pre + pad_ref[pl.ds(padr + k - half, ln), :] * w_ref[pl.ds(k, 1), :]
            sg = _sigmoid(pre)
            dy = dya_ref[pl.ds(s0, ln), :] + dyb_ref[pl.ds(s0, ln), :] + has_skip * dsk_ref[pl.ds(s0, ln), :]
            dpre = dy * (sg * (1.0 + pre * (1.0 - sg)))
            db_acc = db_acc + jnp.sum(dpre, axis=0, keepdims=True)
            for k in range(K):
                dw_acc[k] = dw_acc[k] + jnp.sum(dpre * pad_ref[pl.ds(padr + k - half, ln), :], axis=0, keepdims=True)
            _fill_padded(gpad_ref, dpre, padr, ln)
            du = jnp.zeros((ln, CB), f32)
            for k in range(K):
                du = du + gpad_ref[pl.ds(padr - (k - half), ln), :] * w_ref[pl.ds(k, 1), :]
            du_ref[pl.ds(s0, ln), :] = du.astype(bf16)
        for k in range(K):
            dw_ref[pl.ds(k, 1), :] = dw_acc[k]
        db_ref[...] = db_acc

    cblk = pl.BlockSpec((T, CB), lambda j: (0, j))
    return pl.pallas_call(
        body, grid=(CONVD // CB,),
        in_specs=[cblk, pl.BlockSpec((K, CB), lambda j: (0, j)), pl.BlockSpec((1, CB), lambda j: (0, j)),
                  pl.BlockSpec((None, T, CB), lambda j: (0, 0, j)), pl.BlockSpec((None, T, CB), lambda j: (1, 0, j)),
                  pl.BlockSpec((T, CB), lambda j: (0, jnp.minimum(j, nskip - 1)))],
        out_specs=[cblk, pl.BlockSpec((K, CB), lambda j: (0, j)), pl.BlockSpec((1, CB), lambda j: (0, j))],
        out_shape=[S((T, CONVD), bf16), S((K, CONVD), f32), S((1, CONVD), f32)],
        scratch_shapes=[pltpu.VMEM((maxlen + 2 * padr, CB), f32), pltpu.VMEM((maxlen + 2 * padr, CB), f32)],
        compiler_params=_cparams("parallel"), name=name)(proj, w, b, dy2, dy2, dskip)


FFN_PADR = 72


def _grid_taps():
    return [(dr, dc) for dr in (-1, 0, 1) for dc in (-1, 0, 1)]


def ffn_gate_fwd(val, gate, cw, cb_, name):
    L = val.shape[0]
    nb = FH // CB

    def body(val_ref, gate_ref, w_ref, b_ref, o_ref, pad_ref):
        _fill_padded(pad_ref, gate_ref[...], FFN_PADR, L)
        col = lax.broadcasted_iota(jnp.int32, (L, CB), 0) & (GRID_W - 1)
        acc = jnp.broadcast_to(b_ref[...], (L, CB))
        for t, (dr, dc) in enumerate(_grid_taps()):
            tap = pad_ref[pl.ds(FFN_PADR + dr * GRID_W + dc, L), :]
            if dc == -1:
                tap = jnp.where(col != 0, tap, 0.0)
            elif dc == 1:
                tap = jnp.where(col != GRID_W - 1, tap, 0.0)
            acc = acc + tap * w_ref[pl.ds(t, 1), :]
        o_ref[...] = (acc * _sigmoid(acc) * val_ref[...]).astype(bf16)

    return pl.pallas_call(
        body, grid=(nb,),
        in_specs=[pl.BlockSpec((L, CB), lambda j: (0, j)), pl.BlockSpec((L, CB), lambda j: (0, j)),
                  pl.BlockSpec((9, CB), lambda j: (0, j)), pl.BlockSpec((1, CB), lambda j: (0, j))],
        out_specs=pl.BlockSpec((L, CB), lambda j: (0, j)), out_shape=S((L, FH), bf16),
        scratch_shapes=[pltpu.VMEM((L + 2 * FFN_PADR, CB), f32)],
        compiler_params=_cparams("parallel"), name=name)(val, gate, cw, cb_)


def ffn_gate_bwd(val, gate, cw, cb_, dact, name):
    L = val.shape[0]
    nb = FH // CB

    def body(val_ref, gate_ref, w_ref, b_ref, da_ref, dval_ref, dgate_ref, dw_ref, db_ref, pad_ref, gpad_ref):
        _fill_padded(pad_ref, gate_ref[...], FFN_PADR, L)
        col = lax.broadcasted_iota(jnp.int32, (L, CB), 0) & (GRID_W - 1)
        not_first = col != 0
        not_last = col != GRID_W - 1
        pre = jnp.broadcast_to(b_ref[...], (L, CB))
        for t, (dr, dc) in enumerate(_grid_taps()):
            tap = pad_ref[pl.ds(FFN_PADR + dr * GRID_W + dc, L), :]
            if dc == -1:
                tap = jnp.where(not_first, tap, 0.0)
            elif dc == 1:
                tap = jnp.where(not_last, tap, 0.0)
            pre = pre + tap * w_ref[pl.ds(t, 1), :]
        sg = _sigmoid(pre)
        da = da_ref[...].astype(f32)
        dval_ref[...] = (da * pre * sg).astype(bf16)
        dpre = da * val_ref[...] * (sg * (1.0 + pre * (1.0 - sg)))
        db_ref[...] = jnp.sum(dpre, axis=0, keepdims=True)
        _fill_padded(gpad_ref, dpre, FFN_PADR, L)
        dg = jnp.zeros((L, CB), f32)
        for t, (dr, dc) in enumerate(_grid_taps()):
            off = dr * GRID_W + dc
            tap = pad_ref[pl.ds(FFN_PADR + off, L), :]
            back = gpad_ref[pl.ds(FFN_PADR - off, L), :]
            if dc == -1:
                tap = jnp.where(not_first, tap, 0.0)
                back = jnp.where(not_last, back, 0.0)
            elif dc == 1:
                tap = jnp.where(not_last, tap, 0.0)
                back = jnp.where(not_first, back, 0.0)
            dw_ref[pl.ds(t, 1), :] = jnp.sum(dpre * tap, axis=0, keepdims=True)
            dg = dg + back * w_ref[pl.ds(t, 1), :]
        dgate_ref[...] = dg.astype(bf16)

    cblk = pl.BlockSpec((L, CB), lambda j: (0, j))
    return pl.pallas_call(
        body, grid=(nb,),
        in_specs=[cblk, cblk, pl.BlockSpec((9, CB), lambda j: (0, j)), pl.BlockSpec((1, CB), lambda j: (0, j)), cblk],
        out_specs=[cblk, cblk, pl.BlockSpec((9, CB), lambda j: (0, j)), pl.BlockSpec((1, CB), lambda j: (0, j))],
        out_shape=[S((L, FH), bf16), S((L, FH), bf16), S((9, FH), f32), S((1, FH), f32)],
        scratch_shapes=[pltpu.VMEM((L + 2 * FFN_PADR, CB), f32), pltpu.VMEM((L + 2 * FFN_PADR, CB), f32)],
        compiler_params=_cparams("parallel"), name=name)(val, gate, cw, cb_, dact)


CONF_K = 31
CONF_PADR = 16


def conf_glu_conv_fwd(pa, pg, b1, wdw, bdw, name):
    L = pa.shape[0]
    nb = D // CB
    half = CONF_K // 2

    def body(pa_ref, pg_ref, ba_ref, bg_ref, w_ref, bdw_ref, o_ref, pad_ref):
        glu = (pa_ref[...] + ba_ref[...]) * _sigmoid(pg_ref[...] + bg_ref[...])
        _fill_padded(pad_ref, glu, CONF_PADR, L)
        acc = jnp.broadcast_to(bdw_ref[...], (L, CB))
        for k in range(CONF_K):
            acc = acc + pad_ref[pl.ds(CONF_PADR + k - half, L), :] * w_ref[pl.ds(k, 1), :]
        o_ref[...] = acc

    cblk = pl.BlockSpec((L, CB), lambda j: (0, j))
    rblk = pl.BlockSpec((1, CB), lambda j: (0, j))
    rgblk = pl.BlockSpec((1, CB), lambda j: (0, nb + j))
    return pl.pallas_call(
        body, grid=(nb,), in_specs=[cblk, cblk, rblk, rgblk, pl.BlockSpec((CONF_K, CB), lambda j: (0, j)), rblk],
        out_specs=cblk, out_shape=S((L, D), f32), scratch_shapes=[pltpu.VMEM((L + 2 * CONF_PADR, CB), f32)],
        compiler_params=_cparams("parallel"), name=name)(pa, pg, b1, b1, wdw, bdw)


def conf_glu_conv_bwd(pa, pg, b1, wdw, dy, name):
    L = pa.shape[0]
    nb = D // CB
    half = CONF_K // 2

    def body(pa_ref, pg_ref, ba_ref, bg_ref, w_ref, dy_ref, dpa_ref, dpg_ref, dba_ref, dbg_ref, dw_ref, dbdw_ref,
             pad_ref, gpad_ref):
        a = pa_ref[...] + ba_ref[...]
        sg = _sigmoid(pg_ref[...] + bg_ref[...])
        _fill_padded(pad_ref, a * sg, CONF_PADR, L)
        dy = dy_ref[...]
        dbdw_ref[...] = jnp.sum(dy, axis=0, keepdims=True)
        _fill_padded(gpad_ref, dy, CONF_PADR, L)
        dglu = jnp.zeros((L, CB), f32)
        for k in range(CONF_K):
            dw_ref[pl.ds(k, 1), :] = jnp.sum(dy * pad_ref[pl.ds(CONF_PADR + k - half, L), :], axis=0, keepdims=True)
            dglu = dglu + gpad_ref[pl.ds(CONF_PADR - (k - half), L), :] * w_ref[pl.ds(k, 1), :]
        dpa = dglu * sg
        dpg = dglu * a * (sg * (1.0 - sg))
        dpa_ref[...] = dpa.astype(bf16)
        dpg_ref[...] = dpg.astype(bf16)
        dba_ref[...] = jnp.sum(dpa, axis=0, keepdims=True)
        dbg_ref[...] = jnp.sum(dpg, axis=0, keepdims=True)

    cblk = pl.BlockSpec((L, CB), lambda j: (0, j))
    rblk = pl.BlockSpec((1, CB), lambda j: (0, j))
    rgblk = pl.BlockSpec((1, CB), lambda j: (0, nb + j))
    wblk = pl.BlockSpec((CONF_K, CB), lambda j: (0, j))
    return pl.pallas_call(
        body, grid=(nb,), in_specs=[cblk, cblk, rblk, rgblk, wblk, cblk],
        out_specs=[cblk, cblk, rblk, rblk, wblk, rblk],
        out_shape=[S((L, D), bf16), S((L, D), bf16), S((1, D), f32), S((1, D), f32), S((CONF_K, D), f32), S((1, D), f32)],
        scratch_shapes=[pltpu.VMEM((L + 2 * CONF_PADR, CB), f32), pltpu.VMEM((L + 2 * CONF_PADR, CB), f32)],
        compiler_params=_cparams("parallel"), name=name)(pa, pg, b1, b1, wdw, dy)


def _ln_silu_f(x, w, b):
    mu = jnp.mean(x, axis=-1, keepdims=True)
    d = x - mu
    y = d * lax.rsqrt(jnp.mean(d * d, axis=-1, keepdims=True) + EPS) * w + b
    return y * _sigmoid(y)


def ln_silu_fwd(x, w, b, name):
    T = x.shape[0]

    def body(x_ref, w_ref, b_ref, o_ref):
        o_ref[...] = _ln_silu_f(x_ref[...], w_ref[...], b_ref[...]).astype(bf16)

    blk = pl.BlockSpec((TB, D), lambda i: (i, 0))
    row = pl.BlockSpec((1, D), lambda i: (0, 0))
    return pl.pallas_call(body, grid=(T // TB,), in_specs=[blk, row, row], out_specs=blk, out_shape=S((T, D), bf16),
                          compiler_params=_cparams("parallel"), name=name)(x, w, b)


def ln_silu_bwd(x, w, b, ds, name):
    T = x.shape[0]

    def body(x_ref, w_ref, b_ref, ds_ref, dx_ref, dw_ref, db_ref):
        i = pl.program_id(0)
        _, vjp = jax.vjp(_ln_silu_f, x_ref[...], w_ref[...], b_ref[...])
        dx, dw, db = vjp(ds_ref[...].astype(f32))
        dx_ref[...] = dx

        @pl.when(i == 0)
        def _():
            dw_ref[...] = jnp.zeros_like(dw_ref)
            db_ref[...] = jnp.zeros_like(db_ref)

        dw_ref[...] += dw
        db_ref[...] += db

    blk = pl.BlockSpec((TB, D), lambda i: (i, 0))
    row = pl.BlockSpec((1, D), lambda i: (0, 0))
    return pl.pallas_call(body, grid=(T // TB,), in_specs=[blk, row, row, blk], out_specs=[blk, row, row],
                          out_shape=[S((T, D), f32), S((1, D), f32), S((1, D), f32)],
                          compiler_params=_cparams("arbitrary"), name=name)(x, w, b, ds)


def _lanes_to_rows(v):
    r = lax.broadcasted_iota(jnp.int32, (GW, GW), 0)
    c = lax.broadcasted_iota(jnp.int32, (GW, GW), 1)
    return jnp.sum(jnp.where(r == c, jnp.broadcast_to(v, (GW, GW)), 0.0), axis=1, keepdims=True)


def _ssd_chunk(x, B, C, dtc, dtr, bc, br, alc, alr, s_in, is_fwd):
    row = lax.broadcasted_iota(jnp.int32, (Q, Q), 0)
    col = lax.broadcasted_iota(jnp.int32, (Q, Q), 1)
    sgn = jnp.where(is_fwd, 1, -1).astype(jnp.int32)
    mask = (row - col) * sgn >= 0
    mf = mask.astype(f32)
    er = lax.broadcasted_iota(jnp.int32, (HPG, GW), 0)
    ec = lax.broadcasted_iota(jnp.int32, (HPG, GW), 1)
    E = (ec // P == er).astype(f32)
    dt_c = _softplus(dtc + bc)
    dt_r = _softplus(dtr + br)
    a_c = dt_c * (-jnp.exp(alc))
    a_r = dt_r * (-jnp.exp(alr))
    acum_c = jnp.dot(mf, a_c, precision=HI, preferred_element_type=f32)
    acum_r = lax.dot_general(a_r, mf, (((1,), (1,)), ((), ())), precision=HI, preferred_element_type=f32)
    tot_c = jnp.sum(a_c, axis=0, keepdims=True)
    dt_e = jnp.dot(dt_c, E, precision=HI, preferred_element_type=f32)
    acum_e = jnp.dot(acum_c, E, precision=HI, preferred_element_type=f32)
    tot_e = jnp.dot(tot_c, E, precision=HI, preferred_element_type=f32)
    xdt = x * dt_e
    cb = lax.dot_general(C, B, (((1,), (1,)), ((), ())), preferred_element_type=f32)
    y = jnp.zeros((Q, GW), f32)
    lane_head = lax.broadcasted_iota(jnp.int32, (1, GW), 1) // P
    for r in range(HPG):
        seg = acum_c[:, r:r + 1] - acum_r[r:r + 1, :]
        dec = jnp.exp(jnp.where(mask, seg, -jnp.inf))
        yr = jnp.dot(cb * dec, xdt, preferred_element_type=f32)
        y = y + jnp.where(lane_head == r, yr, 0.0)
    yoff = lax.dot_general(C, s_in, (((1,), (1,)), ((), ())), preferred_element_type=f32)
    y = y + yoff * jnp.exp(acum_e)
    xe = xdt * jnp.exp(tot_e - acum_e)
    st = lax.dot_general(xe, B, (((0,), (0,)), ((), ())), preferred_element_type=f32)
    s_out = _lanes_to_rows(jnp.exp(tot_e)) * s_in + st
    return y, s_out


def _chunk_index(d, t, nctx, nc):
    bwd = jnp.where(t < nctx, nctx - 1 - t, nc - 1 - (t - nctx))
    return jnp.where(d == 0, t, bwd)


def _ssd_in_specs(ci):
    small_c = pl.BlockSpec((None, G, 1, HPG), lambda d, t: (d, 0, 0, 0))
    small_r = pl.BlockSpec((None, G, HPG, 1), lambda d, t: (d, 0, 0, 0))
    return [
        pl.BlockSpec((Q, CONVD), lambda d, t: (ci(d, t), 0)),
        pl.BlockSpec((None, G, Q, HPG), lambda d, t: (d, 0, ci(d, t), 0)),
        pl.BlockSpec((None, G, HPG, Q), lambda d, t: (d, 0, 0, ci(d, t))),
        small_c, small_r, small_c, small_r,
    ]


def _group_cols(g):
    return pl.ds(g * GW, GW), pl.ds(DI + g * N, N), pl.ds(DI + G * N + g * N, N)


def ssd_scan_fwd(xbc, dtc, dtr, bc, br, alc, alr, nctx, name):
    T = xbc.shape[0]
    nc = T // Q

    def body(xbc_ref, dtc_ref, dtr_ref, bc_ref, br_ref, alc_ref, alr_ref, y_ref, sin_ref, st_ref):
        d = pl.program_id(0)
        t = pl.program_id(1)

        @pl.when(t == 0)
        def _():
            st_ref[...] = jnp.zeros_like(st_ref)

        for g in range(G):
            xs, bs, cs = _group_cols(g)
            s_in = st_ref[g]
            sin_ref[g] = s_in
            y, s_out = _ssd_chunk(xbc_ref[:, xs], xbc_ref[:, bs], xbc_ref[:, cs], dtc_ref[g], dtr_ref[g], bc_ref[g], br_ref[g],
                                  alc_ref[g], alr_ref[g], s_in, d == 0)
            y_ref[:, xs] = y
            st_ref[g] = s_out

    ci = lambda d, t: _chunk_index(d, t, nctx, nc)
    out_specs = [
        pl.BlockSpec((None, Q, DI), lambda d, t: (d, ci(d, t), 0)),
        pl.BlockSpec((None, None, G, GW, N), lambda d, t: (d, ci(d, t), 0, 0, 0)),
    ]
    return pl.pallas_call(
        body, grid=(2, nc), in_specs=_ssd_in_specs(ci), out_specs=out_specs,
        out_shape=[S((2, T, DI), f32), S((2, nc, G, GW, N), f32)],
        scratch_shapes=[pltpu.VMEM((G, GW, N), f32)],
        compiler_params=_cparams("arbitrary", "arbitrary"), name=name,
    )(xbc, dtc, dtr, bc, br, alc, alr)


def ssd_scan_bwd(xbc, dtc, dtr, bc, br, alc, alr, s_in_all, dy, nctx, name):
    T = xbc.shape[0]
    nc = T // Q

    def body(xbc_ref, dtc_ref, dtr_ref, bc_ref, br_ref, alc_ref, alr_ref, sin_ref, dy_ref,
             dxbc_ref, ddtc_ref, ddtr_ref, dbc_ref, dbr_ref, dalc_ref, dalr_ref, ds_ref):
        d = pl.program_id(0)
        t = pl.program_id(1)

        @pl.when(t == 0)
        def _():
            ds_ref[...] = jnp.zeros_like(ds_ref)
            dbc_ref[...] = jnp.zeros_like(dbc_ref)
            dbr_ref[...] = jnp.zeros_like(dbr_ref)
            dalc_ref[...] = jnp.zeros_like(dalc_ref)
            dalr_ref[...] = jnp.zeros_like(dalr_ref)

        f = functools.partial(_ssd_chunk, is_fwd=(d == 0))
        for g in range(G):
            xs, bs, cs = _group_cols(g)
            _, vjp = jax.vjp(f, xbc_ref[:, xs], xbc_ref[:, bs], xbc_ref[:, cs], dtc_ref[g], dtr_ref[g], bc_ref[g], br_ref[g],
                             alc_ref[g], alr_ref[g], sin_ref[g])
            dx, dB, dC, ddtc, ddtr, dbc, dbr, dalc, dalr, ds = vjp((dy_ref[:, xs], ds_ref[g]))
            dxbc_ref[:, xs] = dx
            dxbc_ref[:, bs] = dB
            dxbc_ref[:, cs] = dC
            ddtc_ref[g] = ddtc
            ddtr_ref[g] = ddtr
            dbc_ref[g] += dbc
            dbr_ref[g] += dbr
            dalc_ref[g] += dalc
            dalr_ref[g] += dalr
            ds_ref[g] = ds

    ci = lambda d, t: _chunk_index(d, nc - 1 - t, nctx, nc)
    in_specs = _ssd_in_specs(ci) + [
        pl.BlockSpec((None, None, G, GW, N), lambda d, t: (d, ci(d, t), 0, 0, 0)),
        pl.BlockSpec((Q, DI), lambda d, t: (ci(d, t), 0)),
    ]
    small_c = pl.BlockSpec((None, G, 1, HPG), lambda d, t: (d, 0, 0, 0))
    small_r = pl.BlockSpec((None, G, HPG, 1), lambda d, t: (d, 0, 0, 0))
    out_specs = [
        pl.BlockSpec((None, Q, CONVD), lambda d, t: (d, ci(d, t), 0)),
        pl.BlockSpec((None, G, Q, HPG), lambda d, t: (d, 0, ci(d, t), 0)),
        pl.BlockSpec((None, G, HPG, Q), lambda d, t: (d, 0, 0, ci(d, t))),
        small_c, small_r, small_c, small_r,
    ]
    out_shape = [S((2, T, CONVD), f32), S((2, G, T, HPG), f32), S((2, G, HPG, T), f32),
                 S((2, G, 1, HPG), f32), S((2, G, HPG, 1), f32), S((2, G, 1, HPG), f32), S((2, G, HPG, 1), f32)]
    return pl.pallas_call(
        body, grid=(2, nc), in_specs=in_specs, out_specs=out_specs, out_shape=out_shape,
        scratch_shapes=[pltpu.VMEM((G, GW, N), f32)],
        compiler_params=_cparams("arbitrary", "arbitrary"), name=name,
    )(xbc, dtc, dtr, bc, br, alc, alr, s_in_all, dy)


GTB = 128


def _gate_norm_f(yf, yb, x, z, dexp, w):
    y = (yf + yb + dexp * x) * (z * _sigmoid(z))
    return y * lax.rsqrt(jnp.mean(y * y, axis=-1, keepdims=True) + EPS) * w


def ssd_gate_fwd(y2, xbc, proj, dexp, w, nctxb, name):
    T = xbc.shape[0]
    L = T - nctxb * GTB

    def body(yf_ref, yb_ref, x_ref, z_ref, d_ref, w_ref, o_ref):
        o_ref[...] = _gate_norm_f(yf_ref[...], yb_ref[...], x_ref[...], z_ref[...], d_ref[...], w_ref[...]).astype(bf16)

    wide = pl.BlockSpec((GTB, DI), lambda i: (i + nctxb, 0))
    row = pl.BlockSpec((1, DI), lambda i: (0, 0))
    return pl.pallas_call(
        body, grid=(L // GTB,),
        in_specs=[pl.BlockSpec((None, GTB, DI), lambda i: (0, i + nctxb, 0)),
                  pl.BlockSpec((None, GTB, DI), lambda i: (1, i + nctxb, 0)), wide, wide, row, row],
        out_specs=pl.BlockSpec((GTB, DI), lambda i: (i, 0)), out_shape=S((L, DI), bf16),
        compiler_params=_cparams("parallel"), name=name)(y2, y2, xbc, proj, dexp, w)


def ssd_gate_bwd(y2, xbc, proj, dexp, w, dyn, nctxb, name):
    T = xbc.shape[0]
    nb = T // GTB

    def body(yf_ref, yb_ref, x_ref, z_ref, d_ref, w_ref, dyn_ref, dy_ref, dx_ref, dz_ref, dd_ref, dw_ref):
        i = pl.program_id(0)

        @pl.when(i == 0)
        def _():
            dd_ref[...] = jnp.zeros_like(dd_ref)
            dw_ref[...] = jnp.zeros_like(dw_ref)

        @pl.when(i < nctxb)
        def _():
            dy_ref[...] = jnp.zeros_like(dy_ref)
            dx_ref[...] = jnp.zeros_like(dx_ref)
            dz_ref[...] = jnp.zeros_like(dz_ref)

        @pl.when(i >= nctxb)
        def _():
            _, vjp = jax.vjp(_gate_norm_f, yf_ref[...], yb_ref[...], x_ref[...], z_ref[...], d_ref[...], w_ref[...])
            dyf, _, dx, dz, dd, dw = vjp(dyn_ref[...].astype(f32))
            dy_ref[...] = dyf
            dx_ref[...] = dx
            dz_ref[...] = dz.astype(bf16)
            fold = (lax.broadcasted_iota(jnp.int32, (DI, 128), 0) // P == lax.broadcasted_iota(jnp.int32, (DI, 128), 1))
            dd_ref[...] += jnp.dot(dd, fold.astype(f32), precision=HI, preferred_element_type=f32)
            dw_ref[...] += dw

    wide = pl.BlockSpec((GTB, DI), lambda i: (i, 0))
    row = pl.BlockSpec((1, DI), lambda i: (0, 0))
    hrow = pl.BlockSpec((1, 128), lambda i: (0, 0))
    return pl.pallas_call(
        body, grid=(nb,),
        in_specs=[pl.BlockSpec((None, GTB, DI), lambda i: (0, i, 0)), pl.BlockSpec((None, GTB, DI), lambda i: (1, i, 0)),
                  wide, wide, row, row, pl.BlockSpec((GTB, DI), lambda i: (jnp.maximum(i - nctxb, 0), 0))],
        out_specs=[wide, wide, wide, hrow, row],
        out_shape=[S((T, DI), f32), S((T, DI), f32), S((T, DI), bf16), S((1, 128), f32), S((1, DI), f32)],
        compiler_params=_cparams("arbitrary"), name=name)(y2, y2, xbc, proj, dexp, w, dyn)


CROWS = 2 * N_DEV


def mod_fwd(c16, modw, name):
    nl, _, cols = modw.shape

    def body(c_ref, w_ref, o_ref):
        cv = c_ref[...]
        s = cv * _sigmoid(cv)
        for l in range(nl):
            o_ref[l] = jnp.dot(s, w_ref[l], precision=HI, preferred_element_type=f32)

    return pl.pallas_call(body, in_specs=[VMEM, VMEM], out_specs=VMEM, out_shape=S((nl, CROWS, cols), f32),
                          compiler_params=pltpu.CompilerParams(vmem_limit_bytes=VMEM_LIMIT_BYTES), name=name)(c16, modw)


def mod_bwd(c16, modw, dm_sh, dm_all, name):
    nl, _, cols = modw.shape

    def body(c_ref, w_ref, dm_ref, dmall_ref, dw_ref, dc_ref, db_ref):
        cv = c_ref[...]
        sg = _sigmoid(cv)
        s = cv * sg
        ds_dc = sg * (1.0 + cv * (1.0 - sg))
        is_ctx = lax.broadcasted_iota(jnp.int32, (CROWS, D), 0) >= N_DEV
        dc = jnp.zeros((1, D), f32)
        for l in range(nl):
            dm = dm_ref[l]
            dw_ref[l] = lax.dot_general(s, dm, (((0,), (0,)), ((), ())), precision=HI, preferred_element_type=f32)
            dsv = lax.dot_general(dm, w_ref[l], (((1,), (1,)), ((), ())), precision=HI, preferred_element_type=f32)
            dc = dc + jnp.sum(jnp.where(is_ctx, dsv * ds_dc, 0.0), axis=0, keepdims=True)
            db_ref[pl.ds(l, 1), :] = jnp.sum(dmall_ref[l], axis=0, keepdims=True)
        dc_ref[...] = dc

    return pl.pallas_call(
        body, in_specs=[VMEM, VMEM, VMEM, VMEM], out_specs=[VMEM, VMEM, VMEM],
        out_shape=[S(modw.shape, f32), S((1, D), f32), S((nl, 6 * D), f32)],
        compiler_params=pltpu.CompilerParams(vmem_limit_bytes=VMEM_LIMIT_BYTES), name=name)(c16, modw, dm_sh, dm_all)


def adamw(w, g, m, v, name):
    R, C = w.shape
    rb = R if R <= 512 else max(r_ for r_ in range(8, 513, 8) if R % r_ == 0)
    bc1 = 1.0 - ADAM_B1 ** ADAM_STEP
    bc2 = 1.0 - ADAM_B2 ** ADAM_STEP

    def body(w_ref, g_ref, m_ref, v_ref, d_ref, nm_ref, nv_ref):
        gv = g_ref[...]
        m_new = ADAM_B1 * m_ref[...] + (1.0 - ADAM_B1) * gv
        v_new = ADAM_B2 * v_ref[...] + (1.0 - ADAM_B2) * (gv * gv)
        m_hat = m_new / bc1
        v_hat = v_new / bc2
        d_ref[...] = -ADAM_LR * (m_hat / (jnp.sqrt(v_hat) + ADAM_EPS) + ADAM_WD * w_ref[...])
        nm_ref[...] = m_new
        nv_ref[...] = v_new

    blk = pl.BlockSpec((rb, C), lambda i: (i, 0))
    return pl.pallas_call(body, grid=(R // rb,), in_specs=[blk] * 4, out_specs=[blk] * 3,
                          out_shape=[S((R, C), f32)] * 3, compiler_params=_cparams("parallel"), name=name)(w, g, m, v)


def _me():
    return lax.axis_index("x"), lax.axis_index("y"), lax.axis_index("c")


def allgather_small(x, name, with_sum=False):
    r, w = x.shape

    def body(x_ref, *refs):
        if with_sum:
            out_ref, sum_ref, send_sems, recv_sems = refs
        else:
            out_ref, send_sems, recv_sems = refs
        mx, my, mc = _me()
        me = 4 * mx + 2 * my + mc
        out_ref[me] = x_ref[...]
        peers = []
        for k in range(1, N_DEV):
            kx, ky, kc = (k >> 2) & 1, (k >> 1) & 1, k & 1
            peers.append((mx + kx - 2 * mx * kx, my + ky - 2 * my * ky, mc + kc - 2 * mc * kc))
        copies = []
        for k, peer in enumerate(peers):
            cp = pltpu.make_async_remote_copy(src_ref=x_ref, dst_ref=out_ref.at[me], send_sem=send_sems.at[k],
                                              recv_sem=recv_sems.at[k], device_id=peer, device_id_type=MESH)
            cp.start()
            copies.append(cp)
        for k, (px, py, pc) in enumerate(peers):
            pltpu.make_async_remote_copy(src_ref=x_ref, dst_ref=out_ref.at[4 * px + 2 * py + pc], send_sem=send_sems.at[k],
                                         recv_sem=recv_sems.at[k], device_id=(px, py, pc), device_id_type=MESH).wait_recv()
        for cp in copies:
            cp.wait_send()
        if with_sum:
            acc = out_ref[0]
            for j in range(1, N_DEV):
                acc = acc + out_ref[j]
            sum_ref[...] = acc

    out_shape = [S((N_DEV, r, w), f32)] + ([S((r, w), f32)] if with_sum else [])
    outs = pl.pallas_call(
        body, in_specs=[VMEM], out_specs=[VMEM] * len(out_shape), out_shape=out_shape,
        scratch_shapes=[pltpu.SemaphoreType.DMA((N_DEV - 1,)), pltpu.SemaphoreType.DMA((N_DEV - 1,))],
        compiler_params=pltpu.CompilerParams(vmem_limit_bytes=VMEM_LIMIT_BYTES), name=name)(x)
    return outs if with_sum else outs[0]


def allgather_big(shards, name):
    na = len(shards)

    def body(*refs):
        x_refs, out_refs = refs[:na], refs[na:2 * na]
        send_sems, recv_sems, local_sems = refs[2 * na:]
        x, y, c = _me()
        me, sibling = (x, y, c), (x, y, 1 - c)
        chips = [(1 - x, y), (x, 1 - y), (1 - x, 1 - y)]

        def rows(a, px, py, pc):
            return out_refs[a].at[4 * px + 2 * py + pc]

        def copy(a, k, block, to, src=None):
            return pltpu.make_async_remote_copy(
                src_ref=rows(a, *block) if src is None else src, dst_ref=rows(a, *block),
                send_sem=send_sems.at[7 * a + k], recv_sem=recv_sems.at[7 * a + k], device_id=to, device_id_type=MESH)

        mine, first, passed = [], [], []
        for a in range(na):
            mine.append(pltpu.make_async_copy(x_refs[a], rows(a, *me), local_sems.at[a]))
            mine[a].start()
            first.append([copy(a, 0, me, sibling, src=x_refs[a])]
                         + [copy(a, 1 + j, me, (*chip, c), src=x_refs[a]) for j, chip in enumerate(chips)])
            for cp in first[a]:
                cp.start()
            passed.append([copy(a, 4 + j, (*chip, c), sibling) for j, chip in enumerate(chips)])
        for a in range(na):
            for j, chip in enumerate(chips):
                copy(a, 1 + j, (*chip, c), me).wait_recv()
                passed[a][j].start()
        for a in range(na):
            copy(a, 0, sibling, me).wait_recv()
            for j, chip in enumerate(chips):
                copy(a, 4 + j, (*chip, 1 - c), me).wait_recv()
            for cp in first[a] + passed[a]:
                cp.wait_send()
            mine[a].wait()

    return pl.pallas_call(
        body, in_specs=[ANY] * na, out_specs=[ANY] * na,
        out_shape=[S((N_DEV,) + s_.shape, s_.dtype) for s_ in shards],
        scratch_shapes=[pltpu.SemaphoreType.DMA((7 * na,)), pltpu.SemaphoreType.DMA((7 * na,)),
                        pltpu.SemaphoreType.DMA((na,))],
        name=name)(*shards)


def exchange_sibling(gs, name):
    na = len(gs)

    def body(*refs):
        g_refs, out_refs = refs[:na], refs[na:2 * na]
        send_sems, recv_sems = refs[2 * na:]
        x, y, c = _me()
        copies = []
        for a in range(na):
            cp = pltpu.make_async_remote_copy(src_ref=g_refs[a].at[:, 1 - c], dst_ref=out_refs[a],
                                              send_sem=send_sems.at[a], recv_sem=recv_sems.at[a],
                                              device_id=(x, y, 1 - c), device_id_type=MESH)
            cp.start()
            copies.append(cp)
        for cp in copies:
            cp.wait()

    return pl.pallas_call(
        body, in_specs=[ANY] * na, out_specs=[ANY] * na,
        out_shape=[S((4,) + g.shape[2:], g.dtype) for g in gs],
        scratch_shapes=[pltpu.SemaphoreType.DMA((na,)), pltpu.SemaphoreType.DMA((na,))], name=name)(*gs)


def exchange_chips(ps, name):
    na = len(ps)

    def body(*refs):
        p_refs, out_refs = refs[:na], refs[na:2 * na]
        send_sems, recv_sems, local_sems = refs[2 * na:]
        x, y, c = _me()
        mine = 2 * x + y
        chips = [(1 - x, y), (x, 1 - y), (1 - x, 1 - y)]
        copies, locs = [], []
        for a in range(na):
            loc = pltpu.make_async_copy(p_refs[a].at[mine], out_refs[a].at[mine], local_sems.at[a])
            loc.start()
            locs.append(loc)
            for j, (px, py) in enumerate(chips):
                cp = pltpu.make_async_remote_copy(src_ref=p_refs[a].at[2 * px + py], dst_ref=out_refs[a].at[mine],
                                                  send_sem=send_sems.at[3 * a + j], recv_sem=recv_sems.at[3 * a + j],
                                                  device_id=(px, py, c), device_id_type=MESH)
                cp.start()
                copies.append(cp)
        for a in range(na):
            for j, (px, py) in enumerate(chips):
                pltpu.make_async_remote_copy(src_ref=p_refs[a].at[mine], dst_ref=out_refs[a].at[2 * px + py],
                                             send_sem=send_sems.at[3 * a + j], recv_sem=recv_sems.at[3 * a + j],
                                             device_id=(px, py, c), device_id_type=MESH).wait_recv()
        for cp in copies:
            cp.wait_send()
        for loc in locs:
            loc.wait()

    return pl.pallas_call(
        body, in_specs=[ANY] * na, out_specs=[ANY] * na, out_shape=[S(p.shape, p.dtype) for p in ps],
        scratch_shapes=[pltpu.SemaphoreType.DMA((3 * na,)), pltpu.SemaphoreType.DMA((3 * na,)),
                        pltpu.SemaphoreType.DMA((na,))],
        name=name)(*ps)


def add_own(g, r, core, name):
    _, _, R, W = g.shape
    rb = R if R <= 512 else max(r_ for r_ in range(16, 513, 16) if R % r_ == 0)

    def body(core_ref, a_ref, b_ref, o_ref):
        o_ref[...] = (a_ref[...].astype(f32) + b_ref[...].astype(f32)).astype(bf16)

    blk = pl.BlockSpec((None, rb, W), lambda k, i, core_ref: (k, i, 0))
    gs = pltpu.PrefetchScalarGridSpec(
        num_scalar_prefetch=1, grid=(4, R // rb),
        in_specs=[pl.BlockSpec((None, None, rb, W), lambda k, i, core_ref: (k, core_ref[0], i, 0)), blk], out_specs=blk)
    return pl.pallas_call(body, grid_spec=gs, out_shape=S((4, R, W), bf16),
                          compiler_params=_cparams("parallel", "parallel"), name=name)(core, g, r)


def sum_adamw(recv, w, m, v, layer, name):
    _, R, W = recv.shape
    rb = R if R <= 256 else max(r_ for r_ in range(16, 257, 16) if R % r_ == 0)
    bc1 = 1.0 - ADAM_B1 ** ADAM_STEP
    bc2 = 1.0 - ADAM_B2 ** ADAM_STEP

    def body(r_ref, w_ref, m_ref, v_ref, g_ref, d_ref, nm_ref, nv_ref):
        gv = r_ref[0].astype(f32)
        for k in range(1, 4):
            gv = gv + r_ref[k].astype(f32)
        m_new = ADAM_B1 * m_ref[...] + (1.0 - ADAM_B1) * gv
        v_new = ADAM_B2 * v_ref[...] + (1.0 - ADAM_B2) * (gv * gv)
        g_ref[...] = gv
        d_ref[...] = -ADAM_LR * ((m_new / bc1) / (jnp.sqrt(v_new / bc2) + ADAM_EPS) + ADAM_WD * w_ref[...])
        nm_ref[...] = m_new
        nv_ref[...] = v_new

    blk = pl.BlockSpec((rb, W), lambda i: (i, 0))
    wblk = blk if layer is None else pl.BlockSpec((None, rb, W), lambda i: (layer, i, 0))
    return pl.pallas_call(body, grid=(R // rb,), in_specs=[pl.BlockSpec((4, rb, W), lambda i: (0, i, 0)), wblk, wblk, wblk],
                          out_specs=[blk] * 4, out_shape=[S((R, W), f32)] * 4,
                          compiler_params=_cparams("parallel"), name=name)(recv, w, m, v)


def sum_rows(a, name):
    K, R, W = a.shape
    rb = _pick(R, (512, 256, 128, 64, 32, 16))

    def body(a_ref, o_ref):
        acc = a_ref[0].astype(f32)
        for k in range(1, K):
            acc = acc + a_ref[k].astype(f32)
        o_ref[...] = acc

    return pl.pallas_call(body, grid=(R // rb,), in_specs=[pl.BlockSpec((K, rb, W), lambda i: (0, i, 0))],
                          out_specs=pl.BlockSpec((rb, W), lambda i: (i, 0)), out_shape=S((R, W), f32),
                          compiler_params=_cparams("parallel"), name=name)(a)


PACK_ALIGN = 16 * PACK_W


def _pad_to(v, mult):
    n = v.shape[-1]
    extra = (-n) % mult
    if extra == 0:
        return v
    return jnp.concatenate([v, jnp.zeros(v.shape[:-1] + (extra,), v.dtype)], axis=-1)


def _f32_as_bf16_pairs(v):
    return lax.bitcast_convert_type(v.reshape(-1), bf16).reshape(-1)


def _bf16_pairs_as_f32(v):
    return lax.bitcast_convert_type(v.reshape(v.shape[:-1] + (v.shape[-1] // 2, 2)), f32)


def _col_shards(gw):
    lead = gw.shape[:-1]
    n = gw.shape[-1] // N_DEV
    t = gw.reshape(lead + (N_DEV, n))
    t = jnp.moveaxis(t, -2, 0)
    return t.reshape(N_DEV, -1)


def kernel(x, c, ctx, c_ctx, mod_w, mod_b, norm1_w, norm2_w, ssd_w_in, ssd_conv_w, ssd_conv_b, ssd_dt_bias, ssd_a_log, ssd_d, ssd_norm_w, ssd_w_out, conf_w_pw1, conf_b_pw1, conf_w_dw, conf_b_dw, conf_ln_w, conf_ln_b, conf_w_pw2, conf_b_pw2, ffn_w_up, ffn_conv_w, ffn_conv_b, ffn_w_down, final_norm_w, loss_target, m_c_ctx, m_mod_w, m_mod_b, m_norm1_w, m_norm2_w, m_ssd_w_in, m_ssd_conv_w, m_ssd_conv_b, m_ssd_dt_bias, m_ssd_a_log, m_ssd_d, m_ssd_norm_w, m_ssd_w_out, m_conf_w_pw1, m_conf_b_pw1, m_conf_w_dw, m_conf_b_dw, m_conf_ln_w, m_conf_ln_b, m_conf_w_pw2, m_conf_b_pw2, m_ffn_w_up, m_ffn_conv_w, m_ffn_conv_b, m_ffn_w_down, m_final_norm_w, v_c_ctx, v_mod_w, v_mod_b, v_norm1_w, v_norm2_w, v_ssd_w_in, v_ssd_conv_w, v_ssd_conv_b, v_ssd_dt_bias, v_ssd_a_log, v_ssd_d, v_ssd_norm_w, v_ssd_w_out, v_conf_w_pw1, v_conf_b_pw1, v_conf_w_dw, v_conf_b_dw, v_conf_ln_w, v_conf_ln_b, v_conf_w_pw2, v_conf_b_pw2, v_ffn_w_up, v_ffn_conv_w, v_ffn_conv_b, v_ffn_w_down, v_final_norm_w):
    mx, my, mc = _me()
    me = 4 * mx + 2 * my + mc
    L = x.shape[1]
    LC = ctx.shape[1]
    T = LC + L
    w_in_cols = ssd_w_in.shape[2] * N_DEV
    n_dt = w_in_cols - DI - CONVD

    small = [c[0], ssd_conv_w[0], conf_b_pw1[0], conf_w_dw[0], conf_b_dw[0], conf_ln_w[0], conf_ln_b[0], conf_b_pw2[0],
             ffn_conv_w]
    parts = [_f32_as_bf16_pairs(t) for t in small]
    sizes = [p.shape[0] for p in parts]
    small_flat = _pad_to(jnp.concatenate(parts), PACK_ALIGN).reshape(-1, PACK_W)
    w_in, w_up, w_pw1, w_out_g, w_down_g, w_pw2_g, small_g = allgather_big(
        [ssd_w_in[0].astype(bf16), ffn_w_up.astype(bf16), conf_w_pw1[0].astype(bf16), ssd_w_out[0].astype(bf16),
         ffn_w_down.astype(bf16), conf_w_pw2[0].astype(bf16), small_flat], "gather_weights")
    w_out = w_out_g.reshape(DI, D)
    w_pw2 = w_pw2_g.reshape(D, D)
    w_down = [w_down_g[:, i].reshape(FH, D) for i in range(2)]
    small_g = small_g.reshape(N_DEV, -1)
    offs = [0]
    for s_ in sizes:
        offs.append(offs[-1] + s_)
    sm = [_bf16_pairs_as_f32(small_g[:, offs[i]:offs[i + 1]]) for i in range(len(sizes))]

    def cols(pc, K):
        return jnp.moveaxis(pc.reshape(N_DEV, K, -1), 0, 1).reshape(K, -1)

    c_all = sm[0]
    conv_w5 = cols(sm[1], 5)
    b_pw1 = sm[2].reshape(1, 2 * D)
    w_dw = cols(sm[3], CONF_K)
    b_dw, ln_w, ln_b, b_pw2 = (sm[i].reshape(1, D) for i in (4, 5, 6, 7))
    fcw = sm[8].reshape(N_DEV, 2, 9, FH // N_DEV)
    ffn_cw = [cols(fcw[:, i].reshape(N_DEV, -1), 9) for i in range(2)]
    in_segs = (DI, CONVD, n_dt)
    up_segs = (FH, FH)
    pw1_segs = (D, D)

    c16 = jnp.concatenate([c_all, jnp.broadcast_to(c_ctx[None, :], (N_DEV, D))], axis=0)
    m_sh = mod_fwd(c16, mod_w, "mod_fwd")
    mod_cols = mod_w.shape[2]
    m_all = allgather_small(m_sh.reshape(2 * CROWS, mod_cols), "gather_mod")
    m_all = jnp.moveaxis(m_all.reshape(N_DEV, 2, CROWS, mod_cols), 0, 2).reshape(2, CROWS, 6 * D) + mod_b[:, None, :]
    m_lat = lax.dynamic_index_in_dim(m_all, me, axis=1, keepdims=False).reshape(2, 6, 1, D)
    m_ctx = m_all[:, N_DEV].reshape(2, 6, 1, D)
    zero_row = jnp.zeros((1, D), f32)

    def ffn_fwd(h, i, tag):
        a2 = modnorm_fwd(h, norm2_w[i][None], m_lat[i, 4][None], m_lat[i, 3][None], 0, f"ffn{tag}_norm")
        val, gate = smm_fwd(a2, w_up, i, up_segs, f"ffn{tag}_up")
        act = ffn_gate_fwd(val, gate, ffn_cw[i], ffn_conv_b[i][None], f"ffn{tag}_gate")
        o2 = matmul(act, w_down[i], "nn", f32, f"ffn{tag}_down")
        h_new = resgate_fwd(h, o2, m_lat[i, 5], zero_row, f"ffn{tag}_res")
        return h_new, (a2, val, gate, act, o2)

    def ffn_bwd(dh, h, i, saved, tag):
        a2, val, gate, act, o2 = saved
        do2, dg2, _ = resgate_bwd(dh, o2, m_lat[i, 5], zero_row, f"ffn{tag}_res_bwd")
        g_down = matmul(act, do2, "tn", bf16, f"ffn{tag}_down_dw")
        dact = matmul(do2, w_down[i], "nt", bf16, f"ffn{tag}_down_dx")
        dval, dgate, dcw, dcb = ffn_gate_bwd(val, gate, ffn_cw[i], ffn_conv_b[i][None], dact, f"ffn{tag}_gate_bwd")
        g_up = smm_dw(a2, [dval, dgate], FH // 4, up_segs, 2, f"ffn{tag}_up_dw")
        da2 = smm_dx([dval, dgate], w_up, i, up_segs, bf16, f"ffn{tag}_up_dx")
        dh_in, dn2, dsc2, dsh2 = modnorm_bwd(h, norm2_w[i][None], m_lat[i, 4][None], m_lat[i, 3][None], da2, dh, 0,
                                             f"ffn{tag}_norm_bwd")
        return dh_in, dict(w_up=g_up, w_down=g_down, conv_w=dcw, conv_b=dcb, norm2=dn2, sh2=dsh2[0], sc2=dsc2[0], g2=dg2)

    nctx = LC // Q
    h0 = jnp.concatenate([ctx[0], x[0]], axis=0)
    sc0 = jnp.stack([m_ctx[0, 1], m_lat[0, 1]])
    sh0 = jnp.stack([m_ctx[0, 0], m_lat[0, 0]])
    a0 = modnorm_fwd(h0, norm1_w[0][None], sc0, sh0, LC // TB, "ssd_norm")
    z, xbc_pre, dt_raw = smm_fwd(a0, w_in, None, in_segs, "ssd_in")
    segs = ((0, LC), (LC, L))
    xbc = ssd_conv_fwd(xbc_pre, conv_w5, ssd_conv_b, segs, "ssd_conv")
    dt4 = dt_raw[:, :n_dt].reshape(T, 2, G, HPG)
    dtc = jnp.transpose(dt4, (1, 2, 0, 3))
    dtr = jnp.transpose(dt4, (1, 2, 3, 0))
    bias3 = ssd_dt_bias[0].reshape(2, G, HPG)
    alog3 = ssd_a_log[0].reshape(2, G, HPG)
    bc_, br_ = bias3[:, :, None, :], bias3[:, :, :, None]
    alc, alr = alog3[:, :, None, :], alog3[:, :, :, None]
    y2, s_in_all = ssd_scan_fwd(xbc, dtc, dtr, bc_, br_, alc, alr, nctx, "ssd_scan")
    dexp = jnp.repeat(ssd_d[0], P)[None, :]
    yn = ssd_gate_fwd(y2, xbc, z, dexp, ssd_norm_w, LC // GTB, "ssd_gate")
    o_ssd = matmul(yn, w_out, "nn", f32, "ssd_out")
    hx = x[0]
    h1 = resgate_fwd(hx, o_ssd, m_lat[0, 2], zero_row, "ssd_res")
    h2, ffn0_saved = ffn_fwd(h1, 0, "0")

    a1 = modnorm_fwd(h2, norm1_w[1][None], m_lat[1, 1][None], m_lat[1, 0][None], 0, "conf_norm")
    pa, pg = smm_fwd(a1, w_pw1, None, pw1_segs, "conf_pw1")
    dwc = conf_glu_conv_fwd(pa, pg, b_pw1, w_dw, b_dw, "conf_conv")
    s1 = ln_silu_fwd(dwc, ln_w, ln_b, "conf_ln")
    o_conf = matmul(s1, w_pw2, "nn", f32, "conf_pw2")
    h3 = resgate_fwd(h2, o_conf, m_lat[1, 2], b_pw2, "conf_res")
    h4, ffn1_saved = ffn_fwd(h3, 1, "1")

    loss_part, dh4, g_final = final_loss(h4, final_norm_w[None], loss_target[0], "loss_head")
    dh3, gf1 = ffn_bwd(dh4, h3, 1, ffn1_saved, "1")

    do_conf, dg1_1, g_b_pw2 = resgate_bwd(dh3, o_conf, m_lat[1, 2], b_pw2, "conf_res_bwd")
    g_pw2 = matmul(s1, do_conf, "tn", bf16, "conf_pw2_dw")
    ds1 = matmul(do_conf, w_pw2, "nt", bf16, "conf_pw2_dx")
    ddwc, g_ln_w, g_ln_b = ln_silu_bwd(dwc, ln_w, ln_b, ds1, "conf_ln_bwd")
    dpa, dpg, dba, dbg, g_w_dw, g_b_dw = conf_glu_conv_bwd(pa, pg, b_pw1, w_dw, ddwc, "conf_conv_bwd")
    g_b_pw1 = jnp.concatenate([dba, dbg], axis=1)
    g_pw1 = smm_dw(a1, [dpa, dpg], 2 * D // N_DEV, pw1_segs, 1, "conf_pw1_dw")
    da1 = smm_dx([dpa, dpg], w_pw1, None, pw1_segs, bf16, "conf_pw1_dx")
    dh2, g_n1_1, dsc1_1, dsh1_1 = modnorm_bwd(h2, norm1_w[1][None], m_lat[1, 1][None], m_lat[1, 0][None], da1, dh3, 0,
                                              "conf_norm_bwd")
    dh1, gf0 = ffn_bwd(dh2, h1, 0, ffn0_saved, "0")

    do_ssd, dg1_0, _ = resgate_bwd(dh1, o_ssd, m_lat[0, 2], zero_row, "ssd_res_bwd")
    g_w_out = matmul(yn, do_ssd, "tn", bf16, "ssd_out_dw")
    dyn = matmul(do_ssd, w_out, "nt", bf16, "ssd_out_dx")
    dy, dx_skip, dz, g_dexp, g_ssd_norm = ssd_gate_bwd(y2, xbc, z, dexp, ssd_norm_w, dyn, LC // GTB, "ssd_gate_bwd")
    dxbc2, ddtc, ddtr, dbc, dbr, dalc, dalr = ssd_scan_bwd(xbc, dtc, dtr, bc_, br_, alc, alr, s_in_all, dy, nctx,
                                                           "ssd_scan_bwd")
    ddt = (jnp.transpose(ddtc, (2, 0, 1, 3)) + jnp.transpose(ddtr, (3, 0, 1, 2))).reshape(T, n_dt)
    g_dt_bias = (dbc[:, :, 0, :] + dbr[:, :, :, 0]).reshape(2, NH_SSD)
    g_a_log = (dalc[:, :, 0, :] + dalr[:, :, :, 0]).reshape(2, NH_SSD)
    g_ssd_d = g_dexp[0, :NH_SSD]
    du, g_conv_w5, g_conv_b5 = ssd_conv_bwd(xbc_pre, conv_w5, ssd_conv_b, dxbc2, dx_skip, segs, "ssd_conv_bwd")
    ddt_p = _pad_to(ddt, 128).astype(bf16)
    g_w_in = smm_dw(a0, [dz, du, ddt_p], w_in.shape[-1], in_segs, 2, "ssd_in_dw")
    da0 = smm_dx([dz, du, ddt_p], w_in, None, in_segs, f32, "ssd_in_dx")
    dres0 = jnp.concatenate([jnp.zeros((LC, D), f32), dh1], axis=0)
    dh0, g_n1_0, dsc1_0, dsh1_0 = modnorm_bwd(h0, norm1_w[0][None], sc0, sh0, da0, dres0, LC // TB, "ssd_norm_bwd")
    grad_x = dh0[LC:][None]

    zeros_d = jnp.zeros((1, D), f32)
    dm_lat = jnp.stack([
        jnp.concatenate([dsh1_0[1], dsc1_0[1], dg1_0, gf0["sh2"], gf0["sc2"], gf0["g2"]], axis=1),
        jnp.concatenate([dsh1_1[0], dsc1_1[0], dg1_1, gf1["sh2"], gf1["sc2"], gf1["g2"]], axis=1)])
    dm_ctx = jnp.stack([
        jnp.concatenate([dsh1_0[0], dsc1_0[0]] + [zeros_d] * 4, axis=1), jnp.zeros((1, 6 * D), f32)])
    dm_mine = jnp.concatenate([dm_lat.reshape(2, 6 * D), dm_ctx.reshape(2, 6 * D),
                               jnp.zeros((4, 6 * D), f32)], axis=0)
    dm_g = allgather_small(dm_mine, "gather_dmod")
    dm_all = jnp.concatenate([jnp.moveaxis(dm_g[:, 0:2], 0, 1), jnp.moveaxis(dm_g[:, 2:4], 0, 1)], axis=1)
    dm_sh = lax.dynamic_slice_in_dim(dm_all, me * mod_cols, mod_cols, axis=2)
    g_mod_w, g_cctx_part, g_mod_b = mod_bwd(c16, mod_w, dm_sh, dm_all, "mod_bwd")

    rep = [jnp.stack([g_n1_0[0], g_n1_1[0]]), jnp.stack([gf0["norm2"][0], gf1["norm2"][0]]), g_conv_b5, g_dt_bias, g_a_log,
           g_ssd_d, g_ssd_norm, jnp.stack([gf0["conv_b"][0], gf1["conv_b"][0]]), g_final, g_cctx_part, loss_part[:, :1]]
    rep_sizes = [r_.size for r_ in rep]
    rep_flat = _pad_to(jnp.concatenate([r_.reshape(-1) for r_ in rep]), 8 * PACK_W).reshape(-1, PACK_W)
    _, rep_sum = allgather_small(rep_flat, "reduce_replicated", with_sum=True)
    rep_sum = rep_sum.reshape(-1)
    roffs = [0]
    for s_ in rep_sizes:
        roffs.append(roffs[-1] + s_)
    rp = [rep_sum[roffs[i]:roffs[i + 1]] for i in range(len(rep_sizes))]
    loss = rp[10].reshape(())

    g_ffn_cw = jnp.stack([gf0["conv_w"], gf1["conv_w"]])
    small_shards = [_col_shards(t) for t in (g_conv_w5, g_b_pw1, g_w_dw, g_b_dw, g_ln_w, g_ln_b, g_b_pw2, g_ffn_cw)]
    gsizes = [s_.shape[1] for s_ in small_shards]
    g_small = _pad_to(jnp.concatenate(small_shards, axis=1), PACK_ALIGN).astype(bf16)
    to_reduce = [g_w_in, gf0["w_up"], gf1["w_up"], g_pw1, g_w_out, gf0["w_down"], gf1["w_down"], g_pw2,
                 g_small.reshape(N_DEV, -1, PACK_W)]
    to_reduce = [t.reshape((4, 2, -1, t.shape[-1])) for t in to_reduce]
    core = mc.reshape(1).astype(jnp.int32)
    from_sibling = exchange_sibling(to_reduce, "reduce_sibling")
    chip_part = [add_own(t, r_, core, f"reduce_add{i}") for i, (t, r_) in enumerate(zip(to_reduce, from_sibling))]
    from_chips = exchange_chips(chip_part, "reduce_chips")
    g_flat = sum_rows(from_chips[8], "reduce_sum_small").reshape(-1)
    goffs = [0]
    for s_ in gsizes:
        goffs.append(goffs[-1] + s_)
    gs = [g_flat[goffs[i]:goffs[i + 1]] for i in range(len(gsizes))]

    big = {}
    big["ssd_w_in"] = sum_adamw(from_chips[0], ssd_w_in[0], m_ssd_w_in[0], v_ssd_w_in[0], None, "adamw_ssd_w_in")
    up = [sum_adamw(from_chips[1 + i], ffn_w_up, m_ffn_w_up, v_ffn_w_up, i, f"adamw_ffn_w_up{i}") for i in range(2)]
    big["ffn_w_up"] = tuple(jnp.stack([up[0][k], up[1][k]]) for k in range(4))
    big["conf_w_pw1"] = sum_adamw(from_chips[3], conf_w_pw1[0], m_conf_w_pw1[0], v_conf_w_pw1[0], None, "adamw_conf_w_pw1")
    big["ssd_w_out"] = sum_adamw(from_chips[4], ssd_w_out[0], m_ssd_w_out[0], v_ssd_w_out[0], None, "adamw_ssd_w_out")
    dn = [sum_adamw(from_chips[5 + i], ffn_w_down, m_ffn_w_down, v_ffn_w_down, i, f"adamw_ffn_w_down{i}") for i in range(2)]
    big["ffn_w_down"] = tuple(jnp.stack([dn[0][k], dn[1][k]]) for k in range(4))
    big["conf_w_pw2"] = sum_adamw(from_chips[7], conf_w_pw2[0], m_conf_w_pw2[0], v_conf_w_pw2[0], None, "adamw_conf_w_pw2")
    grads = {
        "c_ctx": rp[9], "mod_w": g_mod_w, "mod_b": g_mod_b, "norm1_w": rp[0], "norm2_w": rp[1],
        "ssd_conv_w": gs[0], "ssd_conv_b": rp[2], "ssd_dt_bias": rp[3], "ssd_a_log": rp[4], "ssd_d": rp[5],
        "ssd_norm_w": rp[6], "conf_b_pw1": gs[1], "conf_w_dw": gs[2],
        "conf_b_dw": gs[3], "conf_ln_w": gs[4], "conf_ln_b": gs[5], "conf_b_pw2": gs[6],
        "ffn_conv_w": gs[7], "ffn_conv_b": rp[7], "final_norm_w": rp[8],
    }
    weights = dict(c_ctx=c_ctx, mod_w=mod_w, mod_b=mod_b, norm1_w=norm1_w, norm2_w=norm2_w, ssd_w_in=ssd_w_in, ssd_conv_w=ssd_conv_w, ssd_conv_b=ssd_conv_b, ssd_dt_bias=ssd_dt_bias, ssd_a_log=ssd_a_log, ssd_d=ssd_d, ssd_norm_w=ssd_norm_w, ssd_w_out=ssd_w_out, conf_w_pw1=conf_w_pw1, conf_b_pw1=conf_b_pw1, conf_w_dw=conf_w_dw, conf_b_dw=conf_b_dw, conf_ln_w=conf_ln_w, conf_ln_b=conf_ln_b, conf_w_pw2=conf_w_pw2, conf_b_pw2=conf_b_pw2, ffn_w_up=ffn_w_up, ffn_conv_w=ffn_conv_w, ffn_conv_b=ffn_conv_b, ffn_w_down=ffn_w_down, final_norm_w=final_norm_w)
    m_in = dict(c_ctx=m_c_ctx, mod_w=m_mod_w, mod_b=m_mod_b, norm1_w=m_norm1_w, norm2_w=m_norm2_w, ssd_w_in=m_ssd_w_in, ssd_conv_w=m_ssd_conv_w, ssd_conv_b=m_ssd_conv_b, ssd_dt_bias=m_ssd_dt_bias, ssd_a_log=m_ssd_a_log, ssd_d=m_ssd_d, ssd_norm_w=m_ssd_norm_w, ssd_w_out=m_ssd_w_out, conf_w_pw1=m_conf_w_pw1, conf_b_pw1=m_conf_b_pw1, conf_w_dw=m_conf_w_dw, conf_b_dw=m_conf_b_dw, conf_ln_w=m_conf_ln_w, conf_ln_b=m_conf_ln_b, conf_w_pw2=m_conf_w_pw2, conf_b_pw2=m_conf_b_pw2, ffn_w_up=m_ffn_w_up, ffn_conv_w=m_ffn_conv_w, ffn_conv_b=m_ffn_conv_b, ffn_w_down=m_ffn_w_down, final_norm_w=m_final_norm_w)
    v_in = dict(c_ctx=v_c_ctx, mod_w=v_mod_w, mod_b=v_mod_b, norm1_w=v_norm1_w, norm2_w=v_norm2_w, ssd_w_in=v_ssd_w_in, ssd_conv_w=v_ssd_conv_w, ssd_conv_b=v_ssd_conv_b, ssd_dt_bias=v_ssd_dt_bias, ssd_a_log=v_ssd_a_log, ssd_d=v_ssd_d, ssd_norm_w=v_ssd_norm_w, ssd_w_out=v_ssd_w_out, conf_w_pw1=v_conf_w_pw1, conf_b_pw1=v_conf_b_pw1, conf_w_dw=v_conf_w_dw, conf_b_dw=v_conf_b_dw, conf_ln_w=v_conf_ln_w, conf_ln_b=v_conf_ln_b, conf_w_pw2=v_conf_w_pw2, conf_b_pw2=v_conf_b_pw2, ffn_w_up=v_ffn_w_up, ffn_conv_w=v_ffn_conv_w, ffn_conv_b=v_ffn_conv_b, ffn_w_down=v_ffn_w_down, final_norm_w=v_final_norm_w)

    out_g, out_d, out_m, out_v = [], [], [], []
    for name_, w_ in weights.items():
        shape = w_.shape
        if name_ in big:
            for lst, t in zip((out_g, out_d, out_m, out_v), big[name_]):
                lst.append(t.reshape(shape))
            continue
        cols2 = shape[-1] if len(shape) > 1 else shape[0]
        g2 = grads[name_].reshape(-1, cols2)
        d_, nm_, nv_ = adamw(w_.reshape(-1, cols2), g2, m_in[name_].reshape(-1, cols2), v_in[name_].reshape(-1, cols2),
                             f"adamw_{name_}")
        out_g.append(g2.reshape(shape))
        out_d.append(d_.reshape(shape))
        out_m.append(nm_.reshape(shape))
        out_v.append(nv_.reshape(shape))
    return (loss, grad_x, *out_g, *out_d, *out_m, *out_v)
```

```python
import functools

import jax
import jax.numpy as jnp
from jax import lax
from jax.experimental import pallas as pl
from jax.experimental.pallas import tpu as pltpu

f32 = jnp.float32
bf16 = jnp.bfloat16
HI = lax.Precision.HIGHEST
S = jax.ShapeDtypeStruct
MESH = pl.DeviceIdType.MESH
ANY = pl.BlockSpec(memory_space=pl.ANY)
VMEM = pl.BlockSpec(memory_space=pltpu.VMEM)

N_DEV = 8
D = 1024
DI = 2048
CONVD = 4096
FH = 2816
GRID_W = 64
Q = 128
HPG = 4
P = 64
N = 128
G = 8
GW = HPG * P
NH_SSD = G * HPG
EPS = 1e-6
ADAM_LR, ADAM_B1, ADAM_B2, ADAM_EPS, ADAM_WD, ADAM_STEP = 0.001, 0.9, 0.999, 1e-08, 0.01, 10
VMEM_LIMIT_BYTES = 56 * 1024 * 1024
PACK_W = 1024
TB = 256


def _cparams(*sem):
    return pltpu.CompilerParams(dimension_semantics=sem, vmem_limit_bytes=VMEM_LIMIT_BYTES)


def _pick(n, prefs):
    for p in prefs:
        if n % p == 0:
            return p
    return n


def _sigmoid(x):
    return 1.0 / (1.0 + jnp.exp(-x))


def _softplus(x):
    return jnp.maximum(x, 0.0) + jnp.log(1.0 + jnp.exp(-jnp.abs(x)))


def matmul(a, b, mode, out_dtype, name):
    if mode == "nn":
        (M, K), (_, Nn) = a.shape, b.shape
        bm, bn, bk = _pick(M, (512, 384, 256, 128)), Nn, K
    elif mode == "tn":
        (K, M), (_, Nn) = a.shape, b.shape
        bm, bn, bk = M, Nn, _pick(K, (256, 128))
    else:
        (M, K), (Nn, _) = a.shape, b.shape
        bm, bn, bk = _pick(M, (512, 384, 256, 128)), Nn, K
    nk = K // bk
    dims = {"nn": (((1,), (0,)), ((), ())), "tn": (((0,), (0,)), ((), ())), "nt": (((1,), (1,)), ((), ()))}[mode]

    def body(a_ref, b_ref, o_ref, acc_ref):
        k = pl.program_id(2)

        @pl.when(k == 0)
        def _():
            acc_ref[...] = jnp.zeros_like(acc_ref)

        acc_ref[...] += lax.dot_general(a_ref[...].astype(bf16), b_ref[...].astype(bf16), dims,
                                        preferred_element_type=f32)

        @pl.when(k == nk - 1)
        def _():
            o_ref[...] = acc_ref[...].astype(out_dtype)

    if mode == "nn":
        a_spec = pl.BlockSpec((bm, bk), lambda i, j, k: (i, k))
        b_spec = pl.BlockSpec((bk, bn), lambda i, j, k: (k, j))
    elif mode == "tn":
        a_spec = pl.BlockSpec((bk, bm), lambda i, j, k: (k, i))
        b_spec = pl.BlockSpec((bk, bn), lambda i, j, k: (k, j))
    else:
        a_spec = pl.BlockSpec((bm, bk), lambda i, j, k: (i, k))
        b_spec = pl.BlockSpec((bn, bk), lambda i, j, k: (j, k))
    return pl.pallas_call(
        body, grid=(M // bm, Nn // bn, nk), in_specs=[a_spec, b_spec],
        out_specs=pl.BlockSpec((bm, bn), lambda i, j, k: (i, j)),
        out_shape=S((M, Nn), out_dtype), scratch_shapes=[pltpu.VMEM((bm, bn), f32)],
        compiler_params=_cparams("parallel", "parallel", "arbitrary"), name=name,
    )(a, b)


SMM_BM = 256


def _shard_pieces(seg_widths, n):
    bounds = [0]
    for sw in seg_widths:
        bounds.append(bounds[-1] + sw)
    assert bounds[-1] == N_DEV * n, (seg_widths, n)
    out = []
    for j in range(N_DEV):
        lo, hi = j * n, (j + 1) * n
        pcs = []
        for si in range(len(seg_widths)):
            a, b = max(lo, bounds[si]), min(hi, bounds[si + 1])
            if a < b:
                pcs.append((si, a - bounds[si], a - lo, b - a))
        out.append(pcs)
    return out


def _w_spec(w, layer):
    if layer is None:
        return pl.BlockSpec(w.shape, lambda *idx: (0, 0, 0))
    return pl.BlockSpec((N_DEV, None) + w.shape[2:], lambda *idx: (0, layer, 0, 0))


def smm_fwd(a, w, layer, seg_widths, name):
    M, K = a.shape
    n = w.shape[-1]
    pieces = _shard_pieces(seg_widths, n)
    padded = [sw + (-sw) % 128 for sw in seg_widths]

    def body(a_ref, w_ref, *o_refs):
        av = a_ref[...]
        for si, sw in enumerate(seg_widths):
            if padded[si] != sw:
                o_refs[si][:, pl.ds(padded[si] - 128, 128)] = jnp.zeros((SMM_BM, 128), f32)
        for j in range(N_DEV):
            for si, soff, woff, wd in pieces[j]:
                o_refs[si][:, pl.ds(soff, wd)] = jnp.dot(av, w_ref[j, :, pl.ds(woff, wd)], preferred_element_type=f32)

    return pl.pallas_call(
        body, grid=(M // SMM_BM,), in_specs=[pl.BlockSpec((SMM_BM, K), lambda i: (i, 0)), _w_spec(w, layer)],
        out_specs=[pl.BlockSpec((SMM_BM, pw), lambda i: (i, 0)) for pw in padded],
        out_shape=[S((M, pw), f32) for pw in padded], compiler_params=_cparams("parallel"), name=name)(a, w)


def smm_dx(d_segs, w, layer, seg_widths, out_dtype, name):
    M = d_segs[0].shape[0]
    K, n = w.shape[-2], w.shape[-1]
    pieces = _shard_pieces(seg_widths, n)
    ns = len(d_segs)

    def body(*refs):
        d_refs, w_ref, o_ref = refs[:ns], refs[ns], refs[ns + 1]
        acc = jnp.zeros((SMM_BM, K), f32)
        for j in range(N_DEV):
            for si, soff, woff, wd in pieces[j]:
                acc = acc + lax.dot_general(d_refs[si][:, pl.ds(soff, wd)], w_ref[j, :, pl.ds(woff, wd)],
                                            (((1,), (1,)), ((), ())), preferred_element_type=f32)
        o_ref[...] = acc.astype(out_dtype)

    return pl.pallas_call(
        body, grid=(M // SMM_BM,),
        in_specs=[pl.BlockSpec((SMM_BM, d.shape[1]), lambda i: (i, 0)) for d in d_segs] + [_w_spec(w, layer)],
        out_specs=pl.BlockSpec((SMM_BM, K), lambda i: (i, 0)), out_shape=S((M, K), out_dtype),
        compiler_params=_cparams("parallel"), name=name)(*d_segs, w)


def smm_dw(a, d_segs, n, seg_widths, ngrp, name):
    M, K = a.shape
    pieces = _shard_pieces(seg_widths, n)
    per = N_DEV // ngrp
    nI = M // SMM_BM
    ns = len(d_segs)

    def body(*refs):
        a_ref, d_refs, o_ref, acc_ref = refs[0], refs[1:1 + ns], refs[1 + ns], refs[2 + ns]
        grp = pl.program_id(0)
        i = pl.program_id(1)

        @pl.when(i == 0)
        def _():
            acc_ref[...] = jnp.zeros_like(acc_ref)

        av = a_ref[...]
        for gs in range(ngrp):
            def one_group(gs=gs):
                for jj in range(per):
                    for si, soff, woff, wd in pieces[gs * per + jj]:
                        acc_ref[jj, :, pl.ds(woff, wd)] += lax.dot_general(
                            av, d_refs[si][:, pl.ds(soff, wd)], (((0,), (0,)), ((), ())), preferred_element_type=f32)
            pl.when(grp == gs)(one_group)

        @pl.when(i == nI - 1)
        def _():
            o_ref[...] = acc_ref[...].astype(bf16)

    return pl.pallas_call(
        body, grid=(ngrp, nI),
        in_specs=[pl.BlockSpec((SMM_BM, K), lambda g, i: (i, 0))]
        + [pl.BlockSpec((SMM_BM, d.shape[1]), lambda g, i: (i, 0)) for d in d_segs],
        out_specs=pl.BlockSpec((per, K, n), lambda g, i: (g, 0, 0)), out_shape=S((N_DEV, K, n), bf16),
        scratch_shapes=[pltpu.VMEM((per, K, n), f32)],
        compiler_params=_cparams("arbitrary", "arbitrary"), name=name)(a, *d_segs)


def _modnorm_f(h, w, sc, sh):
    y = h * lax.rsqrt(jnp.mean(h * h, axis=-1, keepdims=True) + EPS)
    return (y * w) * (1.0 + sc) + sh


def _kind_specs(nctxb):
    if nctxb > 0:
        return pl.BlockSpec((None, 1, D), lambda i: (jnp.where(i < nctxb, 0, 1), 0, 0))
    return pl.BlockSpec((None, 1, D), lambda i: (0, 0, 0))


def modnorm_fwd(h, w, sc, sh, nctxb, name):
    T = h.shape[0]

    def body(h_ref, w_ref, sc_ref, sh_ref, o_ref):
        o_ref[...] = _modnorm_f(h_ref[...], w_ref[...], sc_ref[...], sh_ref[...]).astype(bf16)

    blk = pl.BlockSpec((TB, D), lambda i: (i, 0))
    row = pl.BlockSpec((1, D), lambda i: (0, 0))
    ks = _kind_specs(nctxb)
    return pl.pallas_call(body, grid=(T // TB,), in_specs=[blk, row, ks, ks], out_specs=blk,
                          out_shape=S((T, D), bf16), compiler_params=_cparams("parallel"), name=name)(h, w, sc, sh)


def modnorm_bwd(h, w, sc, sh, da, dres, nctxb, name):
    T = h.shape[0]
    kinds = sc.shape[0]

    def body(h_ref, w_ref, sc_ref, sh_ref, da_ref, dres_ref, dh_ref, dw_ref, dsc_ref, dsh_ref):
        i = pl.program_id(0)
        _, vjp = jax.vjp(_modnorm_f, h_ref[...], w_ref[...], sc_ref[...], sh_ref[...])
        dh, dw, dsc, dsh = vjp(da_ref[...].astype(f32))
        dh_ref[...] = dres_ref[...] + dh

        @pl.when(i == 0)
        def _():
            dw_ref[...] = jnp.zeros_like(dw_ref)

        @pl.when((i == 0) | (i == nctxb))
        def _():
            dsc_ref[...] = jnp.zeros_like(dsc_ref)
            dsh_ref[...] = jnp.zeros_like(dsh_ref)

        dw_ref[...] += dw
        dsc_ref[...] += dsc
        dsh_ref[...] += dsh

    blk = pl.BlockSpec((TB, D), lambda i: (i, 0))
    row = pl.BlockSpec((1, D), lambda i: (0, 0))
    ks = _kind_specs(nctxb)
    return pl.pallas_call(
        body, grid=(T // TB,), in_specs=[blk, row, ks, ks, blk, blk], out_specs=[blk, row, ks, ks],
        out_shape=[S((T, D), f32), S((1, D), f32), S((kinds, 1, D), f32), S((kinds, 1, D), f32)],
        compiler_params=_cparams("arbitrary"), name=name)(h, w, sc, sh, da, dres)


def resgate_fwd(h, o, g, b, name):
    T = h.shape[0]

    def body(h_ref, o_ref, g_ref, b_ref, out_ref):
        out_ref[...] = h_ref[...] + g_ref[...] * (o_ref[...] + b_ref[...])

    blk = pl.BlockSpec((TB, D), lambda i: (i, 0))
    row = pl.BlockSpec((1, D), lambda i: (0, 0))
    return pl.pallas_call(body, grid=(T // TB,), in_specs=[blk, blk, row, row], out_specs=blk,
                          out_shape=S((T, D), f32), compiler_params=_cparams("parallel"), name=name)(h, o, g, b)


def resgate_bwd(dh, o, g, b, name):
    T = dh.shape[0]

    def body(dh_ref, o_ref, g_ref, b_ref, do_ref, dg_ref, db_ref):
        i = pl.program_id(0)

        @pl.when(i == 0)
        def _():
            dg_ref[...] = jnp.zeros_like(dg_ref)
            db_ref[...] = jnp.zeros_like(db_ref)

        dh = dh_ref[...]
        do = g_ref[...] * dh
        do_ref[...] = do.astype(bf16)
        dg_ref[...] += jnp.sum(dh * (o_ref[...] + b_ref[...]), axis=0, keepdims=True)
        db_ref[...] += jnp.sum(do, axis=0, keepdims=True)

    blk = pl.BlockSpec((TB, D), lambda i: (i, 0))
    row = pl.BlockSpec((1, D), lambda i: (0, 0))
    return pl.pallas_call(body, grid=(T // TB,), in_specs=[blk, blk, row, row], out_specs=[blk, row, row],
                          out_shape=[S((T, D), bf16), S((1, D), f32), S((1, D), f32)],
                          compiler_params=_cparams("arbitrary"), name=name)(dh, o, g, b)


def final_loss(h, w, tgt, name):
    T = h.shape[0]

    def f(hv, wv, tv):
        y = (hv * lax.rsqrt(jnp.mean(hv * hv, axis=-1, keepdims=True) + EPS)) * wv
        e = y - tv
        return 0.5 * jnp.sum(jnp.sum(e * e, axis=-1, keepdims=True), axis=0, keepdims=True) * (1.0 / D)

    def body(h_ref, w_ref, t_ref, loss_ref, dh_ref, dw_ref):
        i = pl.program_id(0)
        tv = t_ref[...]
        val, vjp = jax.vjp(lambda a, b_: f(a, b_, tv), h_ref[...], w_ref[...])
        dh, dw = vjp(jnp.ones((1, 1), f32))
        dh_ref[...] = dh

        @pl.when(i == 0)
        def _():
            loss_ref[...] = jnp.zeros_like(loss_ref)
            dw_ref[...] = jnp.zeros_like(dw_ref)

        loss_ref[...] += jnp.broadcast_to(val, (1, 128))
        dw_ref[...] += dw

    blk = pl.BlockSpec((TB, D), lambda i: (i, 0))
    row = pl.BlockSpec((1, D), lambda i: (0, 0))
    return pl.pallas_call(body, grid=(T // TB,), in_specs=[blk, row, blk],
                          out_specs=[pl.BlockSpec((1, 128), lambda i: (0, 0)), blk, row],
                          out_shape=[S((1, 128), f32), S((T, D), f32), S((1, D), f32)],
                          compiler_params=_cparams("arbitrary"), name=name)(h, w, tgt)


CB = 256


def _fill_padded(pad_ref, vals, padr, ln):
    pad_ref[pl.ds(0, padr), :] = jnp.zeros((padr, CB), f32)
    pad_ref[pl.ds(padr + ln, padr), :] = jnp.zeros((padr, CB), f32)
    pad_ref[pl.ds(padr, ln), :] = vals


def ssd_conv_fwd(proj, w, b, segs, name):
    T = proj.shape[0]
    K, half, padr = 5, 2, 8
    maxlen = max(ln for _, ln in segs)

    def body(u_ref, w_ref, b_ref, o_ref, pad_ref):
        for s0, ln in segs:
            _fill_padded(pad_ref, u_ref[pl.ds(s0, ln), :], padr, ln)
            acc = jnp.broadcast_to(b_ref[...], (ln, CB))
            for k in range(K):
                acc = acc + pad_ref[pl.ds(padr + k - half, ln), :] * w_ref[pl.ds(k, 1), :]
            o_ref[pl.ds(s0, ln), :] = acc * _sigmoid(acc)

    return pl.pallas_call(
        body, grid=(CONVD // CB,),
        in_specs=[pl.BlockSpec((T, CB), lambda j: (0, j)), pl.BlockSpec((K, CB), lambda j: (0, j)),
                  pl.BlockSpec((1, CB), lambda j: (0, j))],
        out_specs=pl.BlockSpec((T, CB), lambda j: (0, j)), out_shape=S((T, CONVD), f32),
        scratch_shapes=[pltpu.VMEM((maxlen + 2 * padr, CB), f32)],
        compiler_params=_cparams("parallel"), name=name)(proj, w, b)


def ssd_conv_bwd(proj, w, b, dy2, dskip, segs, name):
    T = proj.shape[0]
    K, half, padr = 5, 2, 8
    maxlen = max(ln for _, ln in segs)
    nskip = DI // CB

    def body(u_ref, w_ref, b_ref, dya_ref, dyb_ref, dsk_ref, du_ref, dw_ref, db_ref, pad_ref, gpad_ref):
        dw_acc = [jnp.zeros((1, CB), f32) for _ in range(K)]
        db_acc = jnp.zeros((1, CB), f32)
        has_skip = (pl.program_id(0) < nskip).astype(f32)
        for s0, ln in segs:
            _fill_padded(pad_ref, u_ref[pl.ds(s0, ln), :], padr, ln)
            pre = jnp.broadcast_to(b_ref[...], (ln, CB))
            for k in range(K):
                pre = pre + pad_ref[pl.ds(padr + k - half, ln), :] * w_ref[pl.ds(k, 1), :]
            sg = _sigmoid(pre)
            dy = dya_ref[pl.ds(s0, ln), :] + dyb_ref[pl.ds(s0, ln), :] + has_skip * dsk_ref[pl.ds(s0, ln), :]
            dpre = dy * (sg * (1.0 + pre * (1.0 - sg)))
            db_acc = db_acc + jnp.sum(dpre, axis=0, keepdims=True)
            for k in range(K):
                dw_acc[k] = dw_acc[k] + jnp.sum(dpre * pad_ref[pl.ds(padr + k - half, ln), :], axis=0, keepdims=True)
            _fill_padded(gpad_ref, dpre, padr, ln)
            du = jnp.zeros((ln, CB), f32)
            for k in range(K):
                du = du + gpad_ref[pl.ds(padr - (k - half), ln), :] * w_ref[pl.ds(k, 1), :]
            du_ref[pl.ds(s0, ln), :] = du.astype(bf16)
        for k in range(K):
            dw_ref[pl.ds(k, 1), :] = dw_acc[k]
        db_ref[...] = db_acc

    cblk = pl.BlockSpec((T, CB), lambda j: (0, j))
    return pl.pallas_call(
        body, grid=(CONVD // CB,),
        in_specs=[cblk, pl.BlockSpec((K, CB), lambda j: (0, j)), pl.BlockSpec((1, CB), lambda j: (0, j)),
                  pl.BlockSpec((None, T, CB), lambda j: (0, 0, j)), pl.BlockSpec((None, T, CB), lambda j: (1, 0, j)),
                  pl.BlockSpec((T, CB), lambda j: (0, jnp.minimum(j, nskip - 1)))],
        out_specs=[cblk, pl.BlockSpec((K, CB), lambda j: (0, j)), pl.BlockSpec((1, CB), lambda j: (0, j))],
        out_shape=[S((T, CONVD), bf16), S((K, CONVD), f32), S((1, CONVD), f32)],
        scratch_shapes=[pltpu.VMEM((maxlen + 2 * padr, CB), f32), pltpu.VMEM((maxlen + 2 * padr, CB), f32)],
        compiler_params=_cparams("parallel"), name=name)(proj, w, b, dy2, dy2, dskip)


FFN_PADR = 72


def _grid_taps():
    return [(dr, dc) for dr in (-1, 0, 1) for dc in (-1, 0, 1)]


def ffn_gate_fwd(val, gate, cw, cb_, name):
    L = val.shape[0]
    nb = FH // CB

    def body(val_ref, gate_ref, w_ref, b_ref, o_ref, pad_ref):
        _fill_padded(pad_ref, gate_ref[...], FFN_PADR, L)
        col = lax.broadcasted_iota(jnp.int32, (L, CB), 0) & (GRID_W - 1)
        acc = jnp.broadcast_to(b_ref[...], (L, CB))
        for t, (dr, dc) in enumerate(_grid_taps()):
            tap = pad_ref[pl.ds(FFN_PADR + dr * GRID_W + dc, L), :]
            if dc == -1:
                tap = jnp.where(col != 0, tap, 0.0)
            elif dc == 1:
                tap = jnp.where(col != GRID_W - 1, tap, 0.0)
            acc = acc + tap * w_ref[pl.ds(t, 1), :]
        o_ref[...] = (acc * _sigmoid(acc) * val_ref[...]).astype(bf16)

    return pl.pallas_call(
        body, grid=(nb,),
        in_specs=[pl.BlockSpec((L, CB), lambda j: (0, j)), pl.BlockSpec((L, CB), lambda j: (0, j)),
                  pl.BlockSpec((9, CB), lambda j: (0, j)), pl.BlockSpec((1, CB), lambda j: (0, j))],
        out_specs=pl.BlockSpec((L, CB), lambda j: (0, j)), out_shape=S((L, FH), bf16),
        scratch_shapes=[pltpu.VMEM((L + 2 * FFN_PADR, CB), f32)],
        compiler_params=_cparams("parallel"), name=name)(val, gate, cw, cb_)


def ffn_gate_bwd(val, gate, cw, cb_, dact, name):
    L = val.shape[0]
    nb = FH // CB

    def body(val_ref, gate_ref, w_ref, b_ref, da_ref, dval_ref, dgate_ref, dw_ref, db_ref, pad_ref, gpad_ref):
        _fill_padded(pad_ref, gate_ref[...], FFN_PADR, L)
        col = lax.broadcasted_iota(jnp.int32, (L, CB), 0) & (GRID_W - 1)
        not_first = col != 0
        not_last = col != GRID_W - 1
        pre = jnp.broadcast_to(b_ref[...], (L, CB))
        for t, (dr, dc) in enumerate(_grid_taps()):
            tap = pad_ref[pl.ds(FFN_PADR + dr * GRID_W + dc, L), :]
            if dc == -1:
                tap = jnp.where(not_first, tap, 0.0)
            elif dc == 1:
                tap = jnp.where(not_last, tap, 0.0)
            pre = pre + tap * w_ref[pl.ds(t, 1), :]
        sg = _sigmoid(pre)
        da = da_ref[...].astype(f32)
        dval_ref[...] = (da * pre * sg).astype(bf16)
        dpre = da * val_ref[...] * (sg * (1.0 + pre * (1.0 - sg)))
        db_ref[...] = jnp.sum(dpre, axis=0, keepdims=True)
        _fill_padded(gpad_ref, dpre, FFN_PADR, L)
        dg = jnp.zeros((L, CB), f32)
        for t, (dr, dc) in enumerate(_grid_taps()):
            off = dr * GRID_W + dc
            tap = pad_ref[pl.ds(FFN_PADR + off, L), :]
            back = gpad_ref[pl.ds(FFN_PADR - off, L), :]
            if dc == -1:
                tap = jnp.where(not_first, tap, 0.0)
                back = jnp.where(not_last, back, 0.0)
            elif dc == 1:
                tap = jnp.where(not_last, tap, 0.0)
                back = jnp.where(not_first, back, 0.0)
            dw_ref[pl.ds(t, 1), :] = jnp.sum(dpre * tap, axis=0, keepdims=True)
            dg = dg + back * w_ref[pl.ds(t, 1), :]
        dgate_ref[...] = dg.astype(bf16)

    cblk = pl.BlockSpec((L, CB), lambda j: (0, j))
    return pl.pallas_call(
        body, grid=(nb,),
        in_specs=[cblk, cblk, pl.BlockSpec((9, CB), lambda j: (0, j)), pl.BlockSpec((1, CB), lambda j: (0, j)), cblk],
        out_specs=[cblk, cblk, pl.BlockSpec((9, CB), lambda j: (0, j)), pl.BlockSpec((1, CB), lambda j: (0, j))],
        out_shape=[S((L, FH), bf16), S((L, FH), bf16), S((9, FH), f32), S((1, FH), f32)],
        scratch_shapes=[pltpu.VMEM((L + 2 * FFN_PADR, CB), f32), pltpu.VMEM((L + 2 * FFN_PADR, CB), f32)],
        compiler_params=_cparams("parallel"), name=name)(val, gate, cw, cb_, dact)


CONF_K = 31
CONF_PADR = 16


def conf_glu_conv_fwd(pa, pg, b1, wdw, bdw, name):
    L = pa.shape[0]
    nb = D // CB
    half = CONF_K // 2

    def body(pa_ref, pg_ref, ba_ref, bg_ref, w_ref, bdw_ref, o_ref, pad_ref):
        glu = (pa_ref[...] + ba_ref[...]) * _sigmoid(pg_ref[...] + bg_ref[...])
        _fill_padded(pad_ref, glu, CONF_PADR, L)
        acc = jnp.broadcast_to(bdw_ref[...], (L, CB))
        for k in range(CONF_K):
            acc = acc + pad_ref[pl.ds(CONF_PADR + k - half, L), :] * w_ref[pl.ds(k, 1), :]
        o_ref[...] = acc

    cblk = pl.BlockSpec((L, CB), lambda j: (0, j))
    rblk = pl.BlockSpec((1, CB), lambda j: (0, j))
    rgblk = pl.BlockSpec((1, CB), lambda j: (0, nb + j))
    return pl.pallas_call(
        body, grid=(nb,), in_specs=[cblk, cblk, rblk, rgblk, pl.BlockSpec((CONF_K, CB), lambda j: (0, j)), rblk],
        out_specs=cblk, out_shape=S((L, D), f32), scratch_shapes=[pltpu.VMEM((L + 2 * CONF_PADR, CB), f32)],
        compiler_params=_cparams("parallel"), name=name)(pa, pg, b1, b1, wdw, bdw)


def conf_glu_conv_bwd(pa, pg, b1, wdw, dy, name):
    L = pa.shape[0]
    nb = D // CB
    half = CONF_K // 2

    def body(pa_ref, pg_ref, ba_ref, bg_ref, w_ref, dy_ref, dpa_ref, dpg_ref, dba_ref, dbg_ref, dw_ref, dbdw_ref,
             pad_ref, gpad_ref):
        a = pa_ref[...] + ba_ref[...]
        sg = _sigmoid(pg_ref[...] + bg_ref[...])
        _fill_padded(pad_ref, a * sg, CONF_PADR, L)
        dy = dy_ref[...]
        dbdw_ref[...] = jnp.sum(dy, axis=0, keepdims=True)
        _fill_padded(gpad_ref, dy, CONF_PADR, L)
        dglu = jnp.zeros((L, CB), f32)
        for k in range(CONF_K):
            dw_ref[pl.ds(k, 1), :] = jnp.sum(dy * pad_ref[pl.ds(CONF_PADR + k - half, L), :], axis=0, keepdims=True)
            dglu = dglu + gpad_ref[pl.ds(CONF_PADR - (k - half), L), :] * w_ref[pl.ds(k, 1), :]
        dpa = dglu * sg
        dpg = dglu * a * (sg * (1.0 - sg))
        dpa_ref[...] = dpa.astype(bf16)
        dpg_ref[...] = dpg.astype(bf16)
        dba_ref[...] = jnp.sum(dpa, axis=0, keepdims=True)
        dbg_ref[...] = jnp.sum(dpg, axis=0, keepdims=True)

    cblk = pl.BlockSpec((L, CB), lambda j: (0, j))
    rblk = pl.BlockSpec((1, CB), lambda j: (0, j))
    rgblk = pl.BlockSpec((1, CB), lambda j: (0, nb + j))
    wblk = pl.BlockSpec((CONF_K, CB), lambda j: (0, j))
    return pl.pallas_call(
        body, grid=(nb,), in_specs=[cblk, cblk, rblk, rgblk, wblk, cblk],
        out_specs=[cblk, cblk, rblk, rblk, wblk, rblk],
        out_shape=[S((L, D), bf16), S((L, D), bf16), S((1, D), f32), S((1, D), f32), S((CONF_K, D), f32), S((1, D), f32)],
        scratch_shapes=[pltpu.VMEM((L + 2 * CONF_PADR, CB), f32), pltpu.VMEM((L + 2 * CONF_PADR, CB), f32)],
        compiler_params=_cparams("parallel"), name=name)(pa, pg, b1, b1, wdw, dy)


def _ln_silu_f(x, w, b):
    mu = jnp.mean(x, axis=-1, keepdims=True)
    d = x - mu
    y = d * lax.rsqrt(jnp.mean(d * d, axis=-1, keepdims=True) + EPS) * w + b
    return y * _sigmoid(y)


def ln_silu_fwd(x, w, b, name):
    T = x.shape[0]

    def body(x_ref, w_ref, b_ref, o_ref):
        o_ref[...] = _ln_silu_f(x_ref[...], w_ref[...], b_ref[...]).astype(bf16)

    blk = pl.BlockSpec((TB, D), lambda i: (i, 0))
    row = pl.BlockSpec((1, D), lambda i: (0, 0))
    return pl.pallas_call(body, grid=(T // TB,), in_specs=[blk, row, row], out_specs=blk, out_shape=S((T, D), bf16),
                          compiler_params=_cparams("parallel"), name=name)(x, w, b)


def ln_silu_bwd(x, w, b, ds, name):
    T = x.shape[0]

    def body(x_ref, w_ref, b_ref, ds_ref, dx_ref, dw_ref, db_ref):
        i = pl.program_id(0)
        _, vjp = jax.vjp(_ln_silu_f, x_ref[...], w_ref[...], b_ref[...])
        dx, dw, db = vjp(ds_ref[...].astype(f32))
        dx_ref[...] = dx

        @pl.when(i == 0)
        def _():
            dw_ref[...] = jnp.zeros_like(dw_ref)
            db_ref[...] = jnp.zeros_like(db_ref)

        dw_ref[...] += dw
        db_ref[...] += db

    blk = pl.BlockSpec((TB, D), lambda i: (i, 0))
    row = pl.BlockSpec((1, D), lambda i: (0, 0))
    return pl.pallas_call(body, grid=(T // TB,), in_specs=[blk, row, row, blk], out_specs=[blk, row, row],
                          out_shape=[S((T, D), f32), S((1, D), f32), S((1, D), f32)],
                          compiler_params=_cparams("arbitrary"), name=name)(x, w, b, ds)


def _mxu(a, b, dims):
    return lax.dot_general(a.astype(bf16), b.astype(bf16), (dims, ((), ())), preferred_element_type=f32)


def _nn(a, b):
    return _mxu(a, b, ((1,), (0,)))


def _nt(a, b):
    return _mxu(a, b, ((1,), (1,)))


def _tn(a, b):
    return _mxu(a, b, ((0,), (0,)))


@jax.custom_vjp
def _dot_nn(a, b):
    return _nn(a, b)


@jax.custom_vjp
def _dot_nt(a, b):
    return _nt(a, b)


@jax.custom_vjp
def _dot_tn(a, b):
    return _tn(a, b)


_dot_nn.defvjp(lambda a, b: (_nn(a, b), (a, b)), lambda res, g: (_nt(g, res[1]), _tn(res[0], g)))
_dot_nt.defvjp(lambda a, b: (_nt(a, b), (a, b)), lambda res, g: (_nn(g, res[1]), _tn(g, res[0])))
_dot_tn.defvjp(lambda a, b: (_tn(a, b), (a, b)), lambda res, g: (_nt(res[1], g), _nn(res[0], g)))


def _lanes_to_rows(v):
    r = lax.broadcasted_iota(jnp.int32, (GW, GW), 0)
    c = lax.broadcasted_iota(jnp.int32, (GW, GW), 1)
    return jnp.sum(jnp.where(r == c, jnp.broadcast_to(v, (GW, GW)), 0.0), axis=1, keepdims=True)


def _ssd_chunk(x, B, C, dtc, dtr, bc, br, alc, alr, s_in, is_fwd):
    row = lax.broadcasted_iota(jnp.int32, (Q, Q), 0)
    col = lax.broadcasted_iota(jnp.int32, (Q, Q), 1)
    sgn = jnp.where(is_fwd, 1, -1).astype(jnp.int32)
    mask = (row - col) * sgn >= 0
    mf = mask.astype(f32)
    lane_head = lax.broadcasted_iota(jnp.int32, (1, GW), 1) // P

    def spread(v):
        out = jnp.zeros((v.shape[0], GW), f32)
        for r in range(HPG):
            out = jnp.where(lane_head == r, v[:, r:r + 1], out)
        return out

    dt_c = _softplus(dtc + bc)
    dt_r = _softplus(dtr + br)
    a_c = dt_c * (-jnp.exp(alc))
    a_r = dt_r * (-jnp.exp(alr))
    acum_c = jnp.dot(mf, a_c, precision=HI, preferred_element_type=f32)
    acum_r = lax.dot_general(a_r, mf, (((1,), (1,)), ((), ())), precision=HI, preferred_element_type=f32)
    tot_c = jnp.sum(a_c, axis=0, keepdims=True)
    dt_e = spread(dt_c)
    acum_e = spread(acum_c)
    tot_e = spread(tot_c)
    xdt = x * dt_e
    cb = _dot_nt(C, B)
    y = jnp.zeros((Q, GW), f32)
    for r in range(HPG):
        seg = acum_c[:, r:r + 1] - acum_r[r:r + 1, :]
        dec = jnp.exp(jnp.where(mask, seg, -jnp.inf))
        y = y + jnp.where(lane_head == r, _dot_nn(cb * dec, xdt), 0.0)
    y = y + _dot_nt(C, s_in) * jnp.exp(acum_e)
    xe = xdt * jnp.exp(tot_e - acum_e)
    s_out = _lanes_to_rows(jnp.exp(tot_e)) * s_in + _dot_tn(xe, B)
    return y, s_out


def _chunk_index(d, t, nctx, nc):
    bwd = jnp.where(t < nctx, nctx - 1 - t, nc - 1 - (t - nctx))
    return jnp.where(d == 0, t, bwd)


def _ssd_in_specs(ci):
    small_c = pl.BlockSpec((None, G, 1, HPG), lambda d, t: (d, 0, 0, 0))
    small_r = pl.BlockSpec((None, G, HPG, 1), lambda d, t: (d, 0, 0, 0))
    return [
        pl.BlockSpec((Q, CONVD), lambda d, t: (ci(d, t), 0)),
        pl.BlockSpec((None, G, Q, HPG), lambda d, t: (d, 0, ci(d, t), 0)),
        pl.BlockSpec((None, G, HPG, Q), lambda d, t: (d, 0, 0, ci(d, t))),
        small_c, small_r, small_c, small_r,
    ]


def _group_cols(g):
    return pl.ds(g * GW, GW), pl.ds(DI + g * N, N), pl.ds(DI + G * N + g * N, N)


def ssd_scan_fwd(xbc, dtc, dtr, bc, br, alc, alr, nctx, name):
    T = xbc.shape[0]
    nc = T // Q

    def body(xbc_ref, dtc_ref, dtr_ref, bc_ref, br_ref, alc_ref, alr_ref, y_ref, sin_ref, st_ref):
        d = pl.program_id(0)
        t = pl.program_id(1)

        @pl.when(t == 0)
        def _():
            st_ref[...] = jnp.zeros_like(st_ref)

        for g in range(G):
            xs, bs, cs = _group_cols(g)
            s_in = st_ref[g]
            sin_ref[g] = s_in
            y, s_out = _ssd_chunk(xbc_ref[:, xs], xbc_ref[:, bs], xbc_ref[:, cs], dtc_ref[g], dtr_ref[g], bc_ref[g], br_ref[g],
                                  alc_ref[g], alr_ref[g], s_in, d == 0)
            y_ref[:, xs] = y
            st_ref[g] = s_out

    ci = lambda d, t: _chunk_index(d, t, nctx, nc)
    out_specs = [
        pl.BlockSpec((None, Q, DI), lambda d, t: (d, ci(d, t), 0)),
        pl.BlockSpec((None, None, G, GW, N), lambda d, t: (d, ci(d, t), 0, 0, 0)),
    ]
    return pl.pallas_call(
        body, grid=(2, nc), in_specs=_ssd_in_specs(ci), out_specs=out_specs,
        out_shape=[S((2, T, DI), f32), S((2, nc, G, GW, N), f32)],
        scratch_shapes=[pltpu.VMEM((G, GW, N), f32)],
        compiler_params=_cparams("arbitrary", "arbitrary"), name=name,
    )(xbc, dtc, dtr, bc, br, alc, alr)


def ssd_scan_bwd(xbc, dtc, dtr, bc, br, alc, alr, s_in_all, dy, nctx, name):
    T = xbc.shape[0]
    nc = T // Q

    def body(xbc_ref, dtc_ref, dtr_ref, bc_ref, br_ref, alc_ref, alr_ref, sin_ref, dy_ref,
             dxbc_ref, ddtc_ref, ddtr_ref, dbc_ref, dbr_ref, dalc_ref, dalr_ref, ds_ref):
        d = pl.program_id(0)
        t = pl.program_id(1)

        @pl.when(t == 0)
        def _():
            ds_ref[...] = jnp.zeros_like(ds_ref)
            dbc_ref[...] = jnp.zeros_like(dbc_ref)
            dbr_ref[...] = jnp.zeros_like(dbr_ref)
            dalc_ref[...] = jnp.zeros_like(dalc_ref)
            dalr_ref[...] = jnp.zeros_like(dalr_ref)

        f = functools.partial(_ssd_chunk, is_fwd=(d == 0))
        for g in range(G):
            xs, bs, cs = _group_cols(g)
            _, vjp = jax.vjp(f, xbc_ref[:, xs], xbc_ref[:, bs], xbc_ref[:, cs], dtc_ref[g], dtr_ref[g], bc_ref[g], br_ref[g],
                             alc_ref[g], alr_ref[g], sin_ref[g])
            dx, dB, dC, ddtc, ddtr, dbc, dbr, dalc, dalr, ds = vjp((dy_ref[:, xs], ds_ref[g]))
            dxbc_ref[:, xs] = dx
            dxbc_ref[:, bs] = dB
            dxbc_ref[:, cs] = dC
            ddtc_ref[g] = ddtc
            ddtr_ref[g] = ddtr
            dbc_ref[g] += dbc
            dbr_ref[g] += dbr
            dalc_ref[g] += dalc
            dalr_ref[g] += dalr
            ds_ref[g] = ds

    ci = lambda d, t: _chunk_index(d, nc - 1 - t, nctx, nc)
    in_specs = _ssd_in_specs(ci) + [
        pl.BlockSpec((None, None, G, GW, N), lambda d, t: (d, ci(d, t), 0, 0, 0)),
        pl.BlockSpec((Q, DI), lambda d, t: (ci(d, t), 0)),
    ]
    small_c = pl.BlockSpec((None, G, 1, HPG), lambda d, t: (d, 0, 0, 0))
    small_r = pl.BlockSpec((None, G, HPG, 1), lambda d, t: (d, 0, 0, 0))
    out_specs = [
        pl.BlockSpec((None, Q, CONVD), lambda d, t: (d, ci(d, t), 0)),
        pl.BlockSpec((None, G, Q, HPG), lambda d, t: (d, 0, ci(d, t), 0)),
        pl.BlockSpec((None, G, HPG, Q), lambda d, t: (d, 0, 0, ci(d, t))),
        small_c, small_r, small_c, small_r,
    ]
    out_shape = [S((2, T, CONVD), f32), S((2, G, T, HPG), f32), S((2, G, HPG, T), f32),
                 S((2, G, 1, HPG), f32), S((2, G, HPG, 1), f32), S((2, G, 1, HPG), f32), S((2, G, HPG, 1), f32)]
    return pl.pallas_call(
        body, grid=(2, nc), in_specs=in_specs, out_specs=out_specs, out_shape=out_shape,
        scratch_shapes=[pltpu.VMEM((G, GW, N), f32)],
        compiler_params=_cparams("arbitrary", "arbitrary"), name=name,
    )(xbc, dtc, dtr, bc, br, alc, alr, s_in_all, dy)


GTB = 128


def _gate_norm_f(yf, yb, x, z, dexp, w):
    y = (yf + yb + dexp * x) * (z * _sigmoid(z))
    return y * lax.rsqrt(jnp.mean(y * y, axis=-1, keepdims=True) + EPS) * w


def ssd_gate_fwd(y2, xbc, proj, dexp, w, nctxb, name):
    T = xbc.shape[0]
    L = T - nctxb * GTB

    def body(yf_ref, yb_ref, x_ref, z_ref, d_ref, w_ref, o_ref):
        o_ref[...] = _gate_norm_f(yf_ref[...], yb_ref[...], x_ref[...], z_ref[...], d_ref[...], w_ref[...]).astype(bf16)

    wide = pl.BlockSpec((GTB, DI), lambda i: (i + nctxb, 0))
    row = pl.BlockSpec((1, DI), lambda i: (0, 0))
    return pl.pallas_call(
        body, grid=(L // GTB,),
        in_specs=[pl.BlockSpec((None, GTB, DI), lambda i: (0, i + nctxb, 0)),
                  pl.BlockSpec((None, GTB, DI), lambda i: (1, i + nctxb, 0)), wide, wide, row, row],
        out_specs=pl.BlockSpec((GTB, DI), lambda i: (i, 0)), out_shape=S((L, DI), bf16),
        compiler_params=_cparams("parallel"), name=name)(y2, y2, xbc, proj, dexp, w)


def ssd_gate_bwd(y2, xbc, proj, dexp, w, dyn, nctxb, name):
    T = xbc.shape[0]
    nb = T // GTB

    def body(yf_ref, yb_ref, x_ref, z_ref, d_ref, w_ref, dyn_ref, dy_ref, dx_ref, dz_ref, dd_ref, dw_ref):
        i = pl.program_id(0)

        @pl.when(i == 0)
        def _():
            dd_ref[...] = jnp.zeros_like(dd_ref)
            dw_ref[...] = jnp.zeros_like(dw_ref)

        @pl.when(i < nctxb)
        def _():
            dy_ref[...] = jnp.zeros_like(dy_ref)
            dx_ref[...] = jnp.zeros_like(dx_ref)
            dz_ref[...] = jnp.zeros_like(dz_ref)

        @pl.when(i >= nctxb)
        def _():
            _, vjp = jax.vjp(_gate_norm_f, yf_ref[...], yb_ref[...], x_ref[...], z_ref[...], d_ref[...], w_ref[...])
            dyf, _, dx, dz, dd, dw = vjp(dyn_ref[...].astype(f32))
            dy_ref[...] = dyf
            dx_ref[...] = dx
            dz_ref[...] = dz.astype(bf16)
            fold = (lax.broadcasted_iota(jnp.int32, (DI, 128), 0) // P == lax.broadcasted_iota(jnp.int32, (DI, 128), 1))
            dd_ref[...] += jnp.dot(dd, fold.astype(f32), precision=HI, preferred_element_type=f32)
            dw_ref[...] += dw

    wide = pl.BlockSpec((GTB, DI), lambda i: (i, 0))
    row = pl.BlockSpec((1, DI), lambda i: (0, 0))
    hrow = pl.BlockSpec((1, 128), lambda i: (0, 0))
    return pl.pallas_call(
        body, grid=(nb,),
        in_specs=[pl.BlockSpec((None, GTB, DI), lambda i: (0, i, 0)), pl.BlockSpec((None, GTB, DI), lambda i: (1, i, 0)),
                  wide, wide, row, row, pl.BlockSpec((GTB, DI), lambda i: (jnp.maximum(i - nctxb, 0), 0))],
        out_specs=[wide, wide, wide, hrow, row],
        out_shape=[S((T, DI), f32), S((T, DI), f32), S((T, DI), bf16), S((1, 128), f32), S((1, DI), f32)],
        compiler_params=_cparams("arbitrary"), name=name)(y2, y2, xbc, proj, dexp, w, dyn)


CROWS = 2 * N_DEV


def mod_fwd(c16, modw, name):
    nl, _, cols = modw.shape

    def body(c_ref, w_ref, o_ref):
        cv = c_ref[...]
        s = cv * _sigmoid(cv)
        for l in range(nl):
            o_ref[l] = jnp.dot(s, w_ref[l], precision=HI, preferred_element_type=f32)

    return pl.pallas_call(body, in_specs=[VMEM, VMEM], out_specs=VMEM, out_shape=S((nl, CROWS, cols), f32),
                          compiler_params=pltpu.CompilerParams(vmem_limit_bytes=VMEM_LIMIT_BYTES), name=name)(c16, modw)


def mod_bwd(c16, modw, dm_sh, dm_all, name):
    nl, _, cols = modw.shape

    def body(c_ref, w_ref, dm_ref, dmall_ref, dw_ref, dc_ref, db_ref):
        cv = c_ref[...]
        sg = _sigmoid(cv)
        s = cv * sg
        ds_dc = sg * (1.0 + cv * (1.0 - sg))
        is_ctx = lax.broadcasted_iota(jnp.int32, (CROWS, D), 0) >= N_DEV
        dc = jnp.zeros((1, D), f32)
        for l in range(nl):
            dm = dm_ref[l]
            dw_ref[l] = lax.dot_general(s, dm, (((0,), (0,)), ((), ())), precision=HI, preferred_element_type=f32)
            dsv = lax.dot_general(dm, w_ref[l], (((1,), (1,)), ((), ())), precision=HI, preferred_element_type=f32)
            dc = dc + jnp.sum(jnp.where(is_ctx, dsv * ds_dc, 0.0), axis=0, keepdims=True)
            db_ref[pl.ds(l, 1), :] = jnp.sum(dmall_ref[l], axis=0, keepdims=True)
        dc_ref[...] = dc

    return pl.pallas_call(
        body, in_specs=[VMEM, VMEM, VMEM, VMEM], out_specs=[VMEM, VMEM, VMEM],
        out_shape=[S(modw.shape, f32), S((1, D), f32), S((nl, 6 * D), f32)],
        compiler_params=pltpu.CompilerParams(vmem_limit_bytes=VMEM_LIMIT_BYTES), name=name)(c16, modw, dm_sh, dm_all)


def adamw(w, g, m, v, name):
    R, C = w.shape
    rb = R if R <= 512 else max(r_ for r_ in range(8, 513, 8) if R % r_ == 0)
    bc1 = 1.0 - ADAM_B1 ** ADAM_STEP
    bc2 = 1.0 - ADAM_B2 ** ADAM_STEP

    def body(w_ref, g_ref, m_ref, v_ref, d_ref, nm_ref, nv_ref):
        gv = g_ref[...]
        m_new = ADAM_B1 * m_ref[...] + (1.0 - ADAM_B1) * gv
        v_new = ADAM_B2 * v_ref[...] + (1.0 - ADAM_B2) * (gv * gv)
        m_hat = m_new / bc1
        v_hat = v_new / bc2
        d_ref[...] = -ADAM_LR * (m_hat / (jnp.sqrt(v_hat) + ADAM_EPS) + ADAM_WD * w_ref[...])
        nm_ref[...] = m_new
        nv_ref[...] = v_new

    blk = pl.BlockSpec((rb, C), lambda i: (i, 0))
    return pl.pallas_call(body, grid=(R // rb,), in_specs=[blk] * 4, out_specs=[blk] * 3,
                          out_shape=[S((R, C), f32)] * 3, compiler_params=_cparams("parallel"), name=name)(w, g, m, v)


def _me():
    return lax.axis_index("x"), lax.axis_index("y"), lax.axis_index("c")


def allgather_small(x, name, with_sum=False):
    r, w = x.shape

    def body(x_ref, *refs):
        if with_sum:
            out_ref, sum_ref, send_sems, recv_sems = refs
        else:
            out_ref, send_sems, recv_sems = refs
        mx, my, mc = _me()
        me = 4 * mx + 2 * my + mc
        out_ref[me] = x_ref[...]
        peers = []
        for k in range(1, N_DEV):
            kx, ky, kc = (k >> 2) & 1, (k >> 1) & 1, k & 1
            peers.append((mx + kx - 2 * mx * kx, my + ky - 2 * my * ky, mc + kc - 2 * mc * kc))
        copies = []
        for k, peer in enumerate(peers):
            cp = pltpu.make_async_remote_copy(src_ref=x_ref, dst_ref=out_ref.at[me], send_sem=send_sems.at[k],
                                              recv_sem=recv_sems.at[k], device_id=peer, device_id_type=MESH)
            cp.start()
            copies.append(cp)
        for k, (px, py, pc) in enumerate(peers):
            pltpu.make_async_remote_copy(src_ref=x_ref, dst_ref=out_ref.at[4 * px + 2 * py + pc], send_sem=send_sems.at[k],
                                         recv_sem=recv_sems.at[k], device_id=(px, py, pc), device_id_type=MESH).wait_recv()
        for cp in copies:
            cp.wait_send()
        if with_sum:
            acc = out_ref[0]
            for j in range(1, N_DEV):
                acc = acc + out_ref[j]
            sum_ref[...] = acc

    out_shape = [S((N_DEV, r, w), f32)] + ([S((r, w), f32)] if with_sum else [])
    outs = pl.pallas_call(
        body, in_specs=[VMEM], out_specs=[VMEM] * len(out_shape), out_shape=out_shape,
        scratch_shapes=[pltpu.SemaphoreType.DMA((N_DEV - 1,)), pltpu.SemaphoreType.DMA((N_DEV - 1,))],
        compiler_params=pltpu.CompilerParams(vmem_limit_bytes=VMEM_LIMIT_BYTES), name=name)(x)
    return outs if with_sum else outs[0]


def allgather_big(shards, name):
    na = len(shards)

    def body(*refs):
        x_refs, out_refs = refs[:na], refs[na:2 * na]
        send_sems, recv_sems, local_sems = refs[2 * na:]
        x, y, c = _me()
        me, sibling = (x, y, c), (x, y, 1 - c)
        chips = [(1 - x, y), (x, 1 - y), (1 - x, 1 - y)]

        def rows(a, px, py, pc):
            return out_refs[a].at[4 * px + 2 * py + pc]

        def copy(a, k, block, to, src=None):
            return pltpu.make_async_remote_copy(
                src_ref=rows(a, *block) if src is None else src, dst_ref=rows(a, *block),
                send_sem=send_sems.at[7 * a + k], recv_sem=recv_sems.at[7 * a + k], device_id=to, device_id_type=MESH)

        mine, first, passed = [], [], []
        for a in range(na):
            mine.append(pltpu.make_async_copy(x_refs[a], rows(a, *me), local_sems.at[a]))
            mine[a].start()
            first.append([copy(a, 0, me, sibling, src=x_refs[a])]
                         + [copy(a, 1 + j, me, (*chip, c), src=x_refs[a]) for j, chip in enumerate(chips)])
            for cp in first[a]:
                cp.start()
            passed.append([copy(a, 4 + j, (*chip, c), sibling) for j, chip in enumerate(chips)])
        for a in range(na):
            for j, chip in enumerate(chips):
                copy(a, 1 + j, (*chip, c), me).wait_recv()
                passed[a][j].start()
        for a in range(na):
            copy(a, 0, sibling, me).wait_recv()
            for j, chip in enumerate(chips):
                copy(a, 4 + j, (*chip, 1 - c), me).wait_recv()
            for cp in first[a] + passed[a]:
                cp.wait_send()
            mine[a].wait()

    return pl.pallas_call(
        body, in_specs=[ANY] * na, out_specs=[ANY] * na,
        out_shape=[S((N_DEV,) + s_.shape, s_.dtype) for s_ in shards],
        scratch_shapes=[pltpu.SemaphoreType.DMA((7 * na,)), pltpu.SemaphoreType.DMA((7 * na,)),
                        pltpu.SemaphoreType.DMA((na,))],
        name=name)(*shards)


def exchange_sibling(gs, name):
    na = len(gs)

    def body(*refs):
        g_refs, out_refs = refs[:na], refs[na:2 * na]
        send_sems, recv_sems = refs[2 * na:]
        x, y, c = _me()
        copies = []
        for a in range(na):
            cp = pltpu.make_async_remote_copy(src_ref=g_refs[a].at[:, 1 - c], dst_ref=out_refs[a],
                                              send_sem=send_sems.at[a], recv_sem=recv_sems.at[a],
                                              device_id=(x, y, 1 - c), device_id_type=MESH)
            cp.start()
            copies.append(cp)
        for cp in copies:
            cp.wait()

    return pl.pallas_call(
        body, in_specs=[ANY] * na, out_specs=[ANY] * na,
        out_shape=[S((4,) + g.shape[2:], g.dtype) for g in gs],
        scratch_shapes=[pltpu.SemaphoreType.DMA((na,)), pltpu.SemaphoreType.DMA((na,))], name=name)(*gs)


def exchange_chips(ps, name):
    na = len(ps)

    def body(*refs):
        p_refs, out_refs = refs[:na], refs[na:2 * na]
        send_sems, recv_sems, local_sems = refs[2 * na:]
        x, y, c = _me()
        mine = 2 * x + y
        chips = [(1 - x, y), (x, 1 - y), (1 - x, 1 - y)]
        copies, locs = [], []
        for a in range(na):
            loc = pltpu.make_async_copy(p_refs[a].at[mine], out_refs[a].at[mine], local_sems.at[a])
            loc.start()
            locs.append(loc)
            for j, (px, py) in enumerate(chips):
                cp = pltpu.make_async_remote_copy(src_ref=p_refs[a].at[2 * px + py], dst_ref=out_refs[a].at[mine],
                                                  send_sem=send_sems.at[3 * a + j], recv_sem=recv_sems.at[3 * a + j],
                                                  device_id=(px, py, c), device_id_type=MESH)
                cp.start()
                copies.append(cp)
        for a in range(na):
            for j, (px, py) in enumerate(chips):
                pltpu.make_async_remote_copy(src_ref=p_refs[a].at[mine], dst_ref=out_refs[a].at[2 * px + py],
                                             send_sem=send_sems.at[3 * a + j], recv_sem=recv_sems.at[3 * a + j],
                                             device_id=(px, py, c), device_id_type=MESH).wait_recv()
        for cp in copies:
            cp.wait_send()
        for loc in locs:
            loc.wait()

    return pl.pallas_call(
        body, in_specs=[ANY] * na, out_specs=[ANY] * na, out_shape=[S(p.shape, p.dtype) for p in ps],
        scratch_shapes=[pltpu.SemaphoreType.DMA((3 * na,)), pltpu.SemaphoreType.DMA((3 * na,)),
                        pltpu.SemaphoreType.DMA((na,))],
        name=name)(*ps)


def add_own(g, r, core, name):
    _, _, R, W = g.shape
    rb = R if R <= 512 else max(r_ for r_ in range(16, 513, 16) if R % r_ == 0)

    def body(core_ref, a_ref, b_ref, o_ref):
        o_ref[...] = (a_ref[...].astype(f32) + b_ref[...].astype(f32)).astype(bf16)

    blk = pl.BlockSpec((None, rb, W), lambda k, i, core_ref: (k, i, 0))
    gs = pltpu.PrefetchScalarGridSpec(
        num_scalar_prefetch=1, grid=(4, R // rb),
        in_specs=[pl.BlockSpec((None, None, rb, W), lambda k, i, core_ref: (k, core_ref[0], i, 0)), blk], out_specs=blk)
    return pl.pallas_call(body, grid_spec=gs, out_shape=S((4, R, W), bf16),
                          compiler_params=_cparams("parallel", "parallel"), name=name)(core, g, r)


def sum_adamw(recv, w, m, v, layer, name):
    _, R, W = recv.shape
    rb = R if R <= 256 else max(r_ for r_ in range(16, 257, 16) if R % r_ == 0)
    bc1 = 1.0 - ADAM_B1 ** ADAM_STEP
    bc2 = 1.0 - ADAM_B2 ** ADAM_STEP

    def body(r_ref, w_ref, m_ref, v_ref, g_ref, d_ref, nm_ref, nv_ref):
        gv = r_ref[0].astype(f32)
        for k in range(1, 4):
            gv = gv + r_ref[k].astype(f32)
        m_new = ADAM_B1 * m_ref[...] + (1.0 - ADAM_B1) * gv
        v_new = ADAM_B2 * v_ref[...] + (1.0 - ADAM_B2) * (gv * gv)
        g_ref[...] = gv
        d_ref[...] = -ADAM_LR * ((m_new / bc1) / (jnp.sqrt(v_new / bc2) + ADAM_EPS) + ADAM_WD * w_ref[...])
        nm_ref[...] = m_new
        nv_ref[...] = v_new

    blk = pl.BlockSpec((rb, W), lambda i: (i, 0))
    wblk = blk if layer is None else pl.BlockSpec((None, rb, W), lambda i: (layer, i, 0))
    return pl.pallas_call(body, grid=(R // rb,), in_specs=[pl.BlockSpec((4, rb, W), lambda i: (0, i, 0)), wblk, wblk, wblk],
                          out_specs=[blk] * 4, out_shape=[S((R, W), f32)] * 4,
                          compiler_params=_cparams("parallel"), name=name)(recv, w, m, v)


def sum_rows(a, name):
    K, R, W = a.shape
    rb = _pick(R, (512, 256, 128, 64, 32, 16))

    def body(a_ref, o_ref):
        acc = a_ref[0].astype(f32)
        for k in range(1, K):
            acc = acc + a_ref[k].astype(f32)
        o_ref[...] = acc

    return pl.pallas_call(body, grid=(R // rb,), in_specs=[pl.BlockSpec((K, rb, W), lambda i: (0, i, 0))],
                          out_specs=pl.BlockSpec((rb, W), lambda i: (i, 0)), out_shape=S((R, W), f32),
                          compiler_params=_cparams("parallel"), name=name)(a)


PACK_ALIGN = 16 * PACK_W


def _pad_to(v, mult):
    n = v.shape[-1]
    extra = (-n) % mult
    if extra == 0:
        return v
    return jnp.concatenate([v, jnp.zeros(v.shape[:-1] + (extra,), v.dtype)], axis=-1)


def _f32_as_bf16_pairs(v):
    return lax.bitcast_convert_type(v.reshape(-1), bf16).reshape(-1)


def _bf16_pairs_as_f32(v):
    return lax.bitcast_convert_type(v.reshape(v.shape[:-1] + (v.shape[-1] // 2, 2)), f32)


def _col_shards(gw):
    lead = gw.shape[:-1]
    n = gw.shape[-1] // N_DEV
    t = gw.reshape(lead + (N_DEV, n))
    t = jnp.moveaxis(t, -2, 0)
    return t.reshape(N_DEV, -1)


def kernel(x, c, ctx, c_ctx, mod_w, mod_b, norm1_w, norm2_w, ssd_w_in, ssd_conv_w, ssd_conv_b, ssd_dt_bias, ssd_a_log, ssd_d, ssd_norm_w, ssd_w_out, conf_w_pw1, conf_b_pw1, conf_w_dw, conf_b_dw, conf_ln_w, conf_ln_b, conf_w_pw2, conf_b_pw2, ffn_w_up, ffn_conv_w, ffn_conv_b, ffn_w_down, final_norm_w, loss_target, m_c_ctx, m_mod_w, m_mod_b, m_norm1_w, m_norm2_w, m_ssd_w_in, m_ssd_conv_w, m_ssd_conv_b, m_ssd_dt_bias, m_ssd_a_log, m_ssd_d, m_ssd_norm_w, m_ssd_w_out, m_conf_w_pw1, m_conf_b_pw1, m_conf_w_dw, m_conf_b_dw, m_conf_ln_w, m_conf_ln_b, m_conf_w_pw2, m_conf_b_pw2, m_ffn_w_up, m_ffn_conv_w, m_ffn_conv_b, m_ffn_w_down, m_final_norm_w, v_c_ctx, v_mod_w, v_mod_b, v_norm1_w, v_norm2_w, v_ssd_w_in, v_ssd_conv_w, v_ssd_conv_b, v_ssd_dt_bias, v_ssd_a_log, v_ssd_d, v_ssd_norm_w, v_ssd_w_out, v_conf_w_pw1, v_conf_b_pw1, v_conf_w_dw, v_conf_b_dw, v_conf_ln_w, v_conf_ln_b, v_conf_w_pw2, v_conf_b_pw2, v_ffn_w_up, v_ffn_conv_w, v_ffn_conv_b, v_ffn_w_down, v_final_norm_w):
    mx, my, mc = _me()
    me = 4 * mx + 2 * my + mc
    L = x.shape[1]
    LC = ctx.shape[1]
    T = LC + L
    w_in_cols = ssd_w_in.shape[2] * N_DEV
    n_dt = w_in_cols - DI - CONVD

    small = [c[0], ssd_conv_w[0], conf_b_pw1[0], conf_w_dw[0], conf_b_dw[0], conf_ln_w[0], conf_ln_b[0], conf_b_pw2[0],
             ffn_conv_w]
    parts = [_f32_as_bf16_pairs(t) for t in small]
    sizes = [p.shape[0] for p in parts]
    small_flat = _pad_to(jnp.concatenate(parts), PACK_ALIGN).reshape(-1, PACK_W)
    w_in, w_up, w_pw1, w_out_g, w_down_g, w_pw2_g, small_g = allgather_big(
        [ssd_w_in[0].astype(bf16), ffn_w_up.astype(bf16), conf_w_pw1[0].astype(bf16), ssd_w_out[0].astype(bf16),
         ffn_w_down.astype(bf16), conf_w_pw2[0].astype(bf16), small_flat], "gather_weights")
    w_out = w_out_g.reshape(DI, D)
    w_pw2 = w_pw2_g.reshape(D, D)
    w_down = [w_down_g[:, i].reshape(FH, D) for i in range(2)]
    small_g = small_g.reshape(N_DEV, -1)
    offs = [0]
    for s_ in sizes:
        offs.append(offs[-1] + s_)
    sm = [_bf16_pairs_as_f32(small_g[:, offs[i]:offs[i + 1]]) for i in range(len(sizes))]

    def cols(pc, K):
        return jnp.moveaxis(pc.reshape(N_DEV, K, -1), 0, 1).reshape(K, -1)

    c_all = sm[0]
    conv_w5 = cols(sm[1], 5)
    b_pw1 = sm[2].reshape(1, 2 * D)
    w_dw = cols(sm[3], CONF_K)
    b_dw, ln_w, ln_b, b_pw2 = (sm[i].reshape(1, D) for i in (4, 5, 6, 7))
    fcw = sm[8].reshape(N_DEV, 2, 9, FH // N_DEV)
    ffn_cw = [cols(fcw[:, i].reshape(N_DEV, -1), 9) for i in range(2)]
    in_segs = (DI, CONVD, n_dt)
    up_segs = (FH, FH)
    pw1_segs = (D, D)

    c16 = jnp.concatenate([c_all, jnp.broadcast_to(c_ctx[None, :], (N_DEV, D))], axis=0)
    m_sh = mod_fwd(c16, mod_w, "mod_fwd")
    mod_cols = mod_w.shape[2]
    m_all = allgather_small(m_sh.reshape(2 * CROWS, mod_cols), "gather_mod")
    m_all = jnp.moveaxis(m_all.reshape(N_DEV, 2, CROWS, mod_cols), 0, 2).reshape(2, CROWS, 6 * D) + mod_b[:, None, :]
    m_lat = lax.dynamic_index_in_dim(m_all, me, axis=1, keepdims=False).reshape(2, 6, 1, D)
    m_ctx = m_all[:, N_DEV].reshape(2, 6, 1, D)
    zero_row = jnp.zeros((1, D), f32)

    def ffn_fwd(h, i, tag):
        a2 = modnorm_fwd(h, norm2_w[i][None], m_lat[i, 4][None], m_lat[i, 3][None], 0, f"ffn{tag}_norm")
        val, gate = smm_fwd(a2, w_up, i, up_segs, f"ffn{tag}_up")
        act = ffn_gate_fwd(val, gate, ffn_cw[i], ffn_conv_b[i][None], f"ffn{tag}_gate")
        o2 = matmul(act, w_down[i], "nn", f32, f"ffn{tag}_down")
        h_new = resgate_fwd(h, o2, m_lat[i, 5], zero_row, f"ffn{tag}_res")
        return h_new, (a2, val, gate, act, o2)

    def ffn_bwd(dh, h, i, saved, tag):
        a2, val, gate, act, o2 = saved
        do2, dg2, _ = resgate_bwd(dh, o2, m_lat[i, 5], zero_row, f"ffn{tag}_res_bwd")
        g_down = matmul(act, do2, "tn", bf16, f"ffn{tag}_down_dw")
        dact = matmul(do2, w_down[i], "nt", bf16, f"ffn{tag}_down_dx")
        dval, dgate, dcw, dcb = ffn_gate_bwd(val, gate, ffn_cw[i], ffn_conv_b[i][None], dact, f"ffn{tag}_gate_bwd")
        g_up = smm_dw(a2, [dval, dgate], FH // 4, up_segs, 2, f"ffn{tag}_up_dw")
        da2 = smm_dx([dval, dgate], w_up, i, up_segs, bf16, f"ffn{tag}_up_dx")
        dh_in, dn2, dsc2, dsh2 = modnorm_bwd(h, norm2_w[i][None], m_lat[i, 4][None], m_lat[i, 3][None], da2, dh, 0,
                                             f"ffn{tag}_norm_bwd")
        return dh_in, dict(w_up=g_up, w_down=g_down, conv_w=dcw, conv_b=dcb, norm2=dn2, sh2=dsh2[0], sc2=dsc2[0], g2=dg2)

    nctx = LC // Q
    h0 = jnp.concatenate([ctx[0], x[0]], axis=0)
    sc0 = jnp.stack([m_ctx[0, 1], m_lat[0, 1]])
    sh0 = jnp.stack([m_ctx[0, 0], m_lat[0, 0]])
    a0 = modnorm_fwd(h0, norm1_w[0][None], sc0, sh0, LC // TB, "ssd_norm")
    z, xbc_pre, dt_raw = smm_fwd(a0, w_in, None, in_segs, "ssd_in")
    segs = ((0, LC), (LC, L))
    xbc = ssd_conv_fwd(xbc_pre, conv_w5, ssd_conv_b, segs, "ssd_conv")
    dt4 = dt_raw[:, :n_dt].reshape(T, 2, G, HPG)
    dtc = jnp.transpose(dt4, (1, 2, 0, 3))
    dtr = jnp.transpose(dt4, (1, 2, 3, 0))
    bias3 = ssd_dt_bias[0].reshape(2, G, HPG)
    alog3 = ssd_a_log[0].reshape(2, G, HPG)
    bc_, br_ = bias3[:, :, None, :], bias3[:, :, :, None]
    alc, alr = alog3[:, :, None, :], alog3[:, :, :, None]
    y2, s_in_all = ssd_scan_fwd(xbc, dtc, dtr, bc_, br_, alc, alr, nctx, "ssd_scan")
    dexp = jnp.repeat(ssd_d[0], P)[None, :]
    yn = ssd_gate_fwd(y2, xbc, z, dexp, ssd_norm_w, LC // GTB, "ssd_gate")
    o_ssd = matmul(yn, w_out, "nn", f32, "ssd_out")
    hx = x[0]
    h1 = resgate_fwd(hx, o_ssd, m_lat[0, 2], zero_row, "ssd_res")
    h2, ffn0_saved = ffn_fwd(h1, 0, "0")

    a1 = modnorm_fwd(h2, norm1_w[1][None], m_lat[1, 1][None], m_lat[1, 0][None], 0, "conf_norm")
    pa, pg = smm_fwd(a1, w_pw1, None, pw1_segs, "conf_pw1")
    dwc = conf_glu_conv_fwd(pa, pg, b_pw1, w_dw, b_dw, "conf_conv")
    s1 = ln_silu_fwd(dwc, ln_w, ln_b, "conf_ln")
    o_conf = matmul(s1, w_pw2, "nn", f32, "conf_pw2")
    h3 = resgate_fwd(h2, o_conf, m_lat[1, 2], b_pw2, "conf_res")
    h4, ffn1_saved = ffn_fwd(h3, 1, "1")

    loss_part, dh4, g_final = final_loss(h4, final_norm_w[None], loss_target[0], "loss_head")
    dh3, gf1 = ffn_bwd(dh4, h3, 1, ffn1_saved, "1")

    do_conf, dg1_1, g_b_pw2 = resgate_bwd(dh3, o_conf, m_lat[1, 2], b_pw2, "conf_res_bwd")
    g_pw2 = matmul(s1, do_conf, "tn", bf16, "conf_pw2_dw")
    ds1 = matmul(do_conf, w_pw2, "nt", bf16, "conf_pw2_dx")
    ddwc, g_ln_w, g_ln_b = ln_silu_bwd(dwc, ln_w, ln_b, ds1, "conf_ln_bwd")
    dpa, dpg, dba, dbg, g_w_dw, g_b_dw = conf_glu_conv_bwd(pa, pg, b_pw1, w_dw, ddwc, "conf_conv_bwd")
    g_b_pw1 = jnp.concatenate([dba, dbg], axis=1)
    g_pw1 = smm_dw(a1, [dpa, dpg], 2 * D // N_DEV, pw1_segs, 1, "conf_pw1_dw")
    da1 = smm_dx([dpa, dpg], w_pw1, None, pw1_segs, bf16, "conf_pw1_dx")
    dh2, g_n1_1, dsc1_1, dsh1_1 = modnorm_bwd(h2, norm1_w[1][None], m_lat[1, 1][None], m_lat[1, 0][None], da1, dh3, 0,
                                              "conf_norm_bwd")
    dh1, gf0 = ffn_bwd(dh2, h1, 0, ffn0_saved, "0")

    do_ssd, dg1_0, _ = resgate_bwd(dh1, o_ssd, m_lat[0, 2], zero_row, "ssd_res_bwd")
    g_w_out = matmul(yn, do_ssd, "tn", bf16, "ssd_out_dw")
    dyn = matmul(do_ssd, w_out, "nt", bf16, "ssd_out_dx")
    dy, dx_skip, dz, g_dexp, g_ssd_norm = ssd_gate_bwd(y2, xbc, z, dexp, ssd_norm_w, dyn, LC // GTB, "ssd_gate_bwd")
    dxbc2, ddtc, ddtr, dbc, dbr, dalc, dalr = ssd_scan_bwd(xbc, dtc, dtr, bc_, br_, alc, alr, s_in_all, dy, nctx,
                                                           "ssd_scan_bwd")
    ddt = (jnp.transpose(ddtc, (2, 0, 1, 3)) + jnp.transpose(ddtr, (3, 0, 1, 2))).reshape(T, n_dt)
    g_dt_bias = (dbc[:, :, 0, :] + dbr[:, :, :, 0]).reshape(2, NH_SSD)
    g_a_log = (dalc[:, :, 0, :] + dalr[:, :, :, 0]).reshape(2, NH_SSD)
    g_ssd_d = g_dexp[0, :NH_SSD]
    du, g_conv_w5, g_conv_b5 = ssd_conv_bwd(xbc_pre, conv_w5, ssd_conv_b, dxbc2, dx_skip, segs, "ssd_conv_bwd")
    ddt_p = _pad_to(ddt, 128).astype(bf16)
    g_w_in = smm_dw(a0, [dz, du, ddt_p], w_in.shape[-1], in_segs, 2, "ssd_in_dw")
    da0 = smm_dx([dz, du, ddt_p], w_in, None, in_segs, f32, "ssd_in_dx")
    dres0 = jnp.concatenate([jnp.zeros((LC, D), f32), dh1], axis=0)
    dh0, g_n1_0, dsc1_0, dsh1_0 = modnorm_bwd(h0, norm1_w[0][None], sc0, sh0, da0, dres0, LC // TB, "ssd_norm_bwd")
    grad_x = dh0[LC:][None]

    zeros_d = jnp.zeros((1, D), f32)
    dm_lat = jnp.stack([
        jnp.concatenate([dsh1_0[1], dsc1_0[1], dg1_0, gf0["sh2"], gf0["sc2"], gf0["g2"]], axis=1),
        jnp.concatenate([dsh1_1[0], dsc1_1[0], dg1_1, gf1["sh2"], gf1["sc2"], gf1["g2"]], axis=1)])
    dm_ctx = jnp.stack([
        jnp.concatenate([dsh1_0[0], dsc1_0[0]] + [zeros_d] * 4, axis=1), jnp.zeros((1, 6 * D), f32)])
    dm_mine = jnp.concatenate([dm_lat.reshape(2, 6 * D), dm_ctx.reshape(2, 6 * D),
                               jnp.zeros((4, 6 * D), f32)], axis=0)
    dm_g = allgather_small(dm_mine, "gather_dmod")
    dm_all = jnp.concatenate([jnp.moveaxis(dm_g[:, 0:2], 0, 1), jnp.moveaxis(dm_g[:, 2:4], 0, 1)], axis=1)
    dm_sh = lax.dynamic_slice_in_dim(dm_all, me * mod_cols, mod_cols, axis=2)
    g_mod_w, g_cctx_part, g_mod_b = mod_bwd(c16, mod_w, dm_sh, dm_all, "mod_bwd")

    rep = [jnp.stack([g_n1_0[0], g_n1_1[0]]), jnp.stack([gf0["norm2"][0], gf1["norm2"][0]]), g_conv_b5, g_dt_bias, g_a_log,
           g_ssd_d, g_ssd_norm, jnp.stack([gf0["conv_b"][0], gf1["conv_b"][0]]), g_final, g_cctx_part, loss_part[:, :1]]
    rep_sizes = [r_.size for r_ in rep]
    rep_flat = _pad_to(jnp.concatenate([r_.reshape(-1) for r_ in rep]), 8 * PACK_W).reshape(-1, PACK_W)
    _, rep_sum = allgather_small(rep_flat, "reduce_replicated", with_sum=True)
    rep_sum = rep_sum.reshape(-1)
    roffs = [0]
    for s_ in rep_sizes:
        roffs.append(roffs[-1] + s_)
    rp = [rep_sum[roffs[i]:roffs[i + 1]] for i in range(len(rep_sizes))]
    loss = rp[10].reshape(())

    g_ffn_cw = jnp.stack([gf0["conv_w"], gf1["conv_w"]])
    small_shards = [_col_shards(t) for t in (g_conv_w5, g_b_pw1, g_w_dw, g_b_dw, g_ln_w, g_ln_b, g_b_pw2, g_ffn_cw)]
    gsizes = [s_.shape[1] for s_ in small_shards]
    g_small = _pad_to(jnp.concatenate(small_shards, axis=1), PACK_ALIGN).astype(bf16)
    to_reduce = [g_w_in, gf0["w_up"], gf1["w_up"], g_pw1, g_w_out, gf0["w_down"], gf1["w_down"], g_pw2,
                 g_small.reshape(N_DEV, -1, PACK_W)]
    to_reduce = [t.reshape((4, 2, -1, t.shape[-1])) for t in to_reduce]
    core = mc.reshape(1).astype(jnp.int32)
    from_sibling = exchange_sibling(to_reduce, "reduce_sibling")
    chip_part = [add_own(t, r_, core, f"reduce_add{i}") for i, (t, r_) in enumerate(zip(to_reduce, from_sibling))]
    from_chips = exchange_chips(chip_part, "reduce_chips")
    g_flat = sum_rows(from_chips[8], "reduce_sum_small").reshape(-1)
    goffs = [0]
    for s_ in gsizes:
        goffs.append(goffs[-1] + s_)
    gs = [g_flat[goffs[i]:goffs[i + 1]] for i in range(len(gsizes))]

    big = {}
    big["ssd_w_in"] = sum_adamw(from_chips[0], ssd_w_in[0], m_ssd_w_in[0], v_ssd_w_in[0], None, "adamw_ssd_w_in")
    up = [sum_adamw(from_chips[1 + i], ffn_w_up, m_ffn_w_up, v_ffn_w_up, i, f"adamw_ffn_w_up{i}") for i in range(2)]
    big["ffn_w_up"] = tuple(jnp.stack([up[0][k], up[1][k]]) for k in range(4))
    big["conf_w_pw1"] = sum_adamw(from_chips[3], conf_w_pw1[0], m_conf_w_pw1[0], v_conf_w_pw1[0], None, "adamw_conf_w_pw1")
    big["ssd_w_out"] = sum_adamw(from_chips[4], ssd_w_out[0], m_ssd_w_out[0], v_ssd_w_out[0], None, "adamw_ssd_w_out")
    dn = [sum_adamw(from_chips[5 + i], ffn_w_down, m_ffn_w_down, v_ffn_w_down, i, f"adamw_ffn_w_down{i}") for i in range(2)]
    big["ffn_w_down"] = tuple(jnp.stack([dn[0][k], dn[1][k]]) for k in range(4))
    big["conf_w_pw2"] = sum_adamw(from_chips[7], conf_w_pw2[0], m_conf_w_pw2[0], v_conf_w_pw2[0], None, "adamw_conf_w_pw2")
    grads = {
        "c_ctx": rp[9], "mod_w": g_mod_w, "mod_b": g_mod_b, "norm1_w": rp[0], "norm2_w": rp[1],
        "ssd_conv_w": gs[0], "ssd_conv_b": rp[2], "ssd_dt_bias": rp[3], "ssd_a_log": rp[4], "ssd_d": rp[5],
        "ssd_norm_w": rp[6], "conf_b_pw1": gs[1], "conf_w_dw": gs[2],
        "conf_b_dw": gs[3], "conf_ln_w": gs[4], "conf_ln_b": gs[5], "conf_b_pw2": gs[6],
        "ffn_conv_w": gs[7], "ffn_conv_b": rp[7], "final_norm_w": rp[8],
    }
    weights = dict(c_ctx=c_ctx, mod_w=mod_w, mod_b=mod_b, norm1_w=norm1_w, norm2_w=norm2_w, ssd_w_in=ssd_w_in, ssd_conv_w=ssd_conv_w, ssd_conv_b=ssd_conv_b, ssd_dt_bias=ssd_dt_bias, ssd_a_log=ssd_a_log, ssd_d=ssd_d, ssd_norm_w=ssd_norm_w, ssd_w_out=ssd_w_out, conf_w_pw1=conf_w_pw1, conf_b_pw1=conf_b_pw1, conf_w_dw=conf_w_dw, conf_b_dw=conf_b_dw, conf_ln_w=conf_ln_w, conf_ln_b=conf_ln_b, conf_w_pw2=conf_w_pw2, conf_b_pw2=conf_b_pw2, ffn_w_up=ffn_w_up, ffn_conv_w=ffn_conv_w, ffn_conv_b=ffn_conv_b, ffn_w_down=ffn_w_down, final_norm_w=final_norm_w)
    m_in = dict(c_ctx=m_c_ctx, mod_w=m_mod_w, mod_b=m_mod_b, norm1_w=m_norm1_w, norm2_w=m_norm2_w, ssd_w_in=m_ssd_w_in, ssd_conv_w=m_ssd_conv_w, ssd_conv_b=m_ssd_conv_b, ssd_dt_bias=m_ssd_dt_bias, ssd_a_log=m_ssd_a_log, ssd_d=m_ssd_d, ssd_norm_w=m_ssd_norm_w, ssd_w_out=m_ssd_w_out, conf_w_pw1=m_conf_w_pw1, conf_b_pw1=m_conf_b_pw1, conf_w_dw=m_conf_w_dw, conf_b_dw=m_conf_b_dw, conf_ln_w=m_conf_ln_w, conf_ln_b=m_conf_ln_b, conf_w_pw2=m_conf_w_pw2, conf_b_pw2=m_conf_b_pw2, ffn_w_up=m_ffn_w_up, ffn_conv_w=m_ffn_conv_w, ffn_conv_b=m_ffn_conv_b, ffn_w_down=m_ffn_w_down, final_norm_w=m_final_norm_w)
    v_in = dict(c_ctx=v_c_ctx, mod_w=v_mod_w, mod_b=v_mod_b, norm1_w=v_norm1_w, norm2_w=v_norm2_w, ssd_w_in=v_ssd_w_in, ssd_conv_w=v_ssd_conv_w, ssd_conv_b=v_ssd_conv_b, ssd_dt_bias=v_ssd_dt_bias, ssd_a_log=v_ssd_a_log, ssd_d=v_ssd_d, ssd_norm_w=v_ssd_norm_w, ssd_w_out=v_ssd_w_out, conf_w_pw1=v_conf_w_pw1, conf_b_pw1=v_conf_b_pw1, conf_w_dw=v_conf_w_dw, conf_b_dw=v_conf_b_dw, conf_ln_w=v_conf_ln_w, conf_ln_b=v_conf_ln_b, conf_w_pw2=v_conf_w_pw2, conf_b_pw2=v_conf_b_pw2, ffn_w_up=v_ffn_w_up, ffn_conv_w=v_ffn_conv_w, ffn_conv_b=v_ffn_conv_b, ffn_w_down=v_ffn_w_down, final_norm_w=v_final_norm_w)

    out_g, out_d, out_m, out_v = [], [], [], []
    for name_, w_ in weights.items():
        shape = w_.shape
        if name_ in big:
            for lst, t in zip((out_g, out_d, out_m, out_v), big[name_]):
                lst.append(t.reshape(shape))
            continue
        cols2 = shape[-1] if len(shape) > 1 else shape[0]
        g2 = grads[name_].reshape(-1, cols2)
        d_, nm_, nv_ = adamw(w_.reshape(-1, cols2), g2, m_in[name_].reshape(-1, cols2), v_in[name_].reshape(-1, cols2),
                             f"adamw_{name_}")
        out_g.append(g2.reshape(shape))
        out_d.append(d_.reshape(shape))
        out_m.append(nm_.reshape(shape))
        out_v.append(nv_.reshape(shape))
    return (loss, grad_x, *out_g, *out_d, *out_m, *out_v)
```

```python
import functools

import jax
import jax.numpy as jnp
from jax import lax
from jax.experimental import pallas as pl
from jax.experimental.pallas import tpu as pltpu

f32 = jnp.float32
bf16 = jnp.bfloat16
HI = lax.Precision.HIGHEST
S = jax.ShapeDtypeStruct
MESH = pl.DeviceIdType.MESH
ANY = pl.BlockSpec(memory_space=pl.ANY)
VMEM = pl.BlockSpec(memory_space=pltpu.VMEM)

N_DEV = 8
D = 1024
DI = 2048
CONVD = 4096
FH = 2816
GRID_W = 64
Q = 128
HPG = 4
P = 64
N = 128
G = 8
GW = HPG * P
NH_SSD = G * HPG
EPS = 1e-6
ADAM_LR, ADAM_B1, ADAM_B2, ADAM_EPS, ADAM_WD, ADAM_STEP = 0.001, 0.9, 0.999, 1e-08, 0.01, 10
VMEM_LIMIT_BYTES = 56 * 1024 * 1024
PACK_W = 1024
TB = 256


def _cparams(*sem):
    return pltpu.CompilerParams(dimension_semantics=sem, vmem_limit_bytes=VMEM_LIMIT_BYTES)


def _pick(n, prefs):
    for p in prefs:
        if n % p == 0:
            return p
    return n


def _sigmoid(x):
    return 1.0 / (1.0 + jnp.exp(-x))


def _softplus(x):
    return jnp.maximum(x, 0.0) + jnp.log(1.0 + jnp.exp(-jnp.abs(x)))


def matmul(a, b, mode, out_dtype, name):
    if mode == "nn":
        (M, K), (_, Nn) = a.shape, b.shape
        bm, bn, bk = _pick(M, (512, 384, 256, 128)), Nn, K
    elif mode == "tn":
        (K, M), (_, Nn) = a.shape, b.shape
        bm, bn, bk = M, Nn, _pick(K, (256, 128))
    else:
        (M, K), (Nn, _) = a.shape, b.shape
        bm, bn, bk = _pick(M, (512, 384, 256, 128)), Nn, K
    nk = K // bk
    dims = {"nn": (((1,), (0,)), ((), ())), "tn": (((0,), (0,)), ((), ())), "nt": (((1,), (1,)), ((), ()))}[mode]

    def body(a_ref, b_ref, o_ref, acc_ref):
        k = pl.program_id(2)

        @pl.when(k == 0)
        def _():
            acc_ref[...] = jnp.zeros_like(acc_ref)

        acc_ref[...] += lax.dot_general(a_ref[...].astype(bf16), b_ref[...].astype(bf16), dims,
                                        preferred_element_type=f32)

        @pl.when(k == nk - 1)
        def _():
            o_ref[...] = acc_ref[...].astype(out_dtype)

    if mode == "nn":
        a_spec = pl.BlockSpec((bm, bk), lambda i, j, k: (i, k))
        b_spec = pl.BlockSpec((bk, bn), lambda i, j, k: (k, j))
    elif mode == "tn":
        a_spec = pl.BlockSpec((bk, bm), lambda i, j, k: (k, i))
        b_spec = pl.BlockSpec((bk, bn), lambda i, j, k: (k, j))
    else:
        a_spec = pl.BlockSpec((bm, bk), lambda i, j, k: (i, k))
        b_spec = pl.BlockSpec((bn, bk), lambda i, j, k: (j, k))
    return pl.pallas_call(
        body, grid=(M // bm, Nn // bn, nk), in_specs=[a_spec, b_spec],
        out_specs=pl.BlockSpec((bm, bn), lambda i, j, k: (i, j)),
        out_shape=S((M, Nn), out_dtype), scratch_shapes=[pltpu.VMEM((bm, bn), f32)],
        compiler_params=_cparams("parallel", "parallel", "arbitrary"), name=name,
    )(a, b)


SMM_BM = 256


def _shard_pieces(seg_widths, n):
    bounds = [0]
    for sw in seg_widths:
        bounds.append(bounds[-1] + sw)
    assert bounds[-1] == N_DEV * n, (seg_widths, n)
    out = []
    for j in range(N_DEV):
        lo, hi = j * n, (j + 1) * n
        pcs = []
        for si in range(len(seg_widths)):
            a, b = max(lo, bounds[si]), min(hi, bounds[si + 1])
            if a < b:
                pcs.append((si, a - bounds[si], a - lo, b - a))
        out.append(pcs)
    return out


def _w_spec(w, layer):
    if layer is None:
        return pl.BlockSpec(w.shape, lambda *idx: (0, 0, 0))
    return pl.BlockSpec((N_DEV, None) + w.shape[2:], lambda *idx: (0, layer, 0, 0))


def smm_fwd(a, w, layer, seg_widths, name):
    M, K = a.shape
    n = w.shape[-1]
    pieces = _shard_pieces(seg_widths, n)
    padded = [sw + (-sw) % 128 for sw in seg_widths]

    def body(a_ref, w_ref, *o_refs):
        av = a_ref[...]
        for si, sw in enumerate(seg_widths):
            if padded[si] != sw:
                o_refs[si][:, pl.ds(padded[si] - 128, 128)] = jnp.zeros((SMM_BM, 128), f32)
        for j in range(N_DEV):
            for si, soff, woff, wd in pieces[j]:
                o_refs[si][:, pl.ds(soff, wd)] = jnp.dot(av, w_ref[j, :, pl.ds(woff, wd)], preferred_element_type=f32)

    return pl.pallas_call(
        body, grid=(M // SMM_BM,), in_specs=[pl.BlockSpec((SMM_BM, K), lambda i: (i, 0)), _w_spec(w, layer)],
        out_specs=[pl.BlockSpec((SMM_BM, pw), lambda i: (i, 0)) for pw in padded],
        out_shape=[S((M, pw), f32) for pw in padded], compiler_params=_cparams("parallel"), name=name)(a, w)


def smm_dx(d_segs, w, layer, seg_widths, out_dtype, name):
    M = d_segs[0].shape[0]
    K, n = w.shape[-2], w.shape[-1]
    pieces = _shard_pieces(seg_widths, n)
    ns = len(d_segs)

    def body(*refs):
        d_refs, w_ref, o_ref = refs[:ns], refs[ns], refs[ns + 1]
        acc = jnp.zeros((SMM_BM, K), f32)
        for j in range(N_DEV):
            for si, soff, woff, wd in pieces[j]:
                acc = acc + lax.dot_general(d_refs[si][:, pl.ds(soff, wd)], w_ref[j, :, pl.ds(woff, wd)],
                                            (((1,), (1,)), ((), ())), preferred_element_type=f32)
        o_ref[...] = acc.astype(out_dtype)

    return pl.pallas_call(
        body, grid=(M // SMM_BM,),
        in_specs=[pl.BlockSpec((SMM_BM, d.shape[1]), lambda i: (i, 0)) for d in d_segs] + [_w_spec(w, layer)],
        out_specs=pl.BlockSpec((SMM_BM, K), lambda i: (i, 0)), out_shape=S((M, K), out_dtype),
        compiler_params=_cparams("parallel"), name=name)(*d_segs, w)


def smm_dw(a, d_segs, n, seg_widths, ngrp, name):
    M, K = a.shape
    pieces = _shard_pieces(seg_widths, n)
    per = N_DEV // ngrp
    nI = M // SMM_BM
    ns = len(d_segs)

    def body(*refs):
        a_ref, d_refs, o_ref, acc_ref = refs[0], refs[1:1 + ns], refs[1 + ns], refs[2 + ns]
        grp = pl.program_id(0)
        i = pl.program_id(1)

        @pl.when(i == 0)
        def _():
            acc_ref[...] = jnp.zeros_like(acc_ref)

        av = a_ref[...]
        for gs in range(ngrp):
            def one_group(gs=gs):
                for jj in range(per):
                    for si, soff, woff, wd in pieces[gs * per + jj]:
                        acc_ref[jj, :, pl.ds(woff, wd)] += lax.dot_general(
                            av, d_refs[si][:, pl.ds(soff, wd)], (((0,), (0,)), ((), ())), preferred_element_type=f32)
            pl.when(grp == gs)(one_group)

        @pl.when(i == nI - 1)
        def _():
            o_ref[...] = acc_ref[...].astype(bf16)

    return pl.pallas_call(
        body, grid=(ngrp, nI),
        in_specs=[pl.BlockSpec((SMM_BM, K), lambda g, i: (i, 0))]
        + [pl.BlockSpec((SMM_BM, d.shape[1]), lambda g, i: (i, 0)) for d in d_segs],
        out_specs=pl.BlockSpec((per, K, n), lambda g, i: (g, 0, 0)), out_shape=S((N_DEV, K, n), bf16),
        scratch_shapes=[pltpu.VMEM((per, K, n), f32)],
        compiler_params=_cparams("arbitrary", "arbitrary"), name=name)(a, *d_segs)


def _modnorm_f(h, w, sc, sh):
    y = h * lax.rsqrt(jnp.mean(h * h, axis=-1, keepdims=True) + EPS)
    return (y * w) * (1.0 + sc) + sh


def _kind_specs(nctxb):
    if nctxb > 0:
        return pl.BlockSpec((None, 1, D), lambda i: (jnp.where(i < nctxb, 0, 1), 0, 0))
    return pl.BlockSpec((None, 1, D), lambda i: (0, 0, 0))


def modnorm_fwd(h, w, sc, sh, nctxb, name):
    T = h.shape[0]

    def body(h_ref, w_ref, sc_ref, sh_ref, o_ref):
        o_ref[...] = _modnorm_f(h_ref[...], w_ref[...], sc_ref[...], sh_ref[...]).astype(bf16)

    blk = pl.BlockSpec((TB, D), lambda i: (i, 0))
    row = pl.BlockSpec((1, D), lambda i: (0, 0))
    ks = _kind_specs(nctxb)
    return pl.pallas_call(body, grid=(T // TB,), in_specs=[blk, row, ks, ks], out_specs=blk,
                          out_shape=S((T, D), bf16), compiler_params=_cparams("parallel"), name=name)(h, w, sc, sh)


def modnorm_bwd(h, w, sc, sh, da, dres, nctxb, name):
    T = h.shape[0]
    kinds = sc.shape[0]

    def body(h_ref, w_ref, sc_ref, sh_ref, da_ref, dres_ref, dh_ref, dw_ref, dsc_ref, dsh_ref):
        i = pl.program_id(0)
        _, vjp = jax.vjp(_modnorm_f, h_ref[...], w_ref[...], sc_ref[...], sh_ref[...])
        dh, dw, dsc, dsh = vjp(da_ref[...].astype(f32))
        dh_ref[...] = dres_ref[...] + dh

        @pl.when(i == 0)
        def _():
            dw_ref[...] = jnp.zeros_like(dw_ref)

        @pl.when((i == 0) | (i == nctxb))
        def _():
            dsc_ref[...] = jnp.zeros_like(dsc_ref)
            dsh_ref[...] = jnp.zeros_like(dsh_ref)

        dw_ref[...] += dw
        dsc_ref[...] += dsc
        dsh_ref[...] += dsh

    blk = pl.BlockSpec((TB, D), lambda i: (i, 0))
    row = pl.BlockSpec((1, D), lambda i: (0, 0))
    ks = _kind_specs(nctxb)
    return pl.pallas_call(
        body, grid=(T // TB,), in_specs=[blk, row, ks, ks, blk, blk], out_specs=[blk, row, ks, ks],
        out_shape=[S((T, D), f32), S((1, D), f32), S((kinds, 1, D), f32), S((kinds, 1, D), f32)],
        compiler_params=_cparams("arbitrary"), name=name)(h, w, sc, sh, da, dres)


def resgate_fwd(h, o, g, b, name):
    T = h.shape[0]

    def body(h_ref, o_ref, g_ref, b_ref, out_ref):
        out_ref[...] = h_ref[...] + g_ref[...] * (o_ref[...] + b_ref[...])

    blk = pl.BlockSpec((TB, D), lambda i: (i, 0))
    row = pl.BlockSpec((1, D), lambda i: (0, 0))
    return pl.pallas_call(body, grid=(T // TB,), in_specs=[blk, blk, row, row], out_specs=blk,
                          out_shape=S((T, D), f32), compiler_params=_cparams("parallel"), name=name)(h, o, g, b)


def resgate_bwd(dh, o, g, b, name):
    T = dh.shape[0]

    def body(dh_ref, o_ref, g_ref, b_ref, do_ref, dg_ref, db_ref):
        i = pl.program_id(0)

        @pl.when(i == 0)
        def _():
            dg_ref[...] = jnp.zeros_like(dg_ref)
            db_ref[...] = jnp.zeros_like(db_ref)

        dh = dh_ref[...]
        do = g_ref[...] * dh
        do_ref[...] = do.astype(bf16)
        dg_ref[...] += jnp.sum(dh * (o_ref[...] + b_ref[...]), axis=0, keepdims=True)
        db_ref[...] += jnp.sum(do, axis=0, keepdims=True)

    blk = pl.BlockSpec((TB, D), lambda i: (i, 0))
    row = pl.BlockSpec((1, D), lambda i: (0, 0))
    return pl.pallas_call(body, grid=(T // TB,), in_specs=[blk, blk, row, row], out_specs=[blk, row, row],
                          out_shape=[S((T, D), bf16), S((1, D), f32), S((1, D), f32)],
                          compiler_params=_cparams("arbitrary"), name=name)(dh, o, g, b)


def final_loss(h, w, tgt, name):
    T = h.shape[0]

    def f(hv, wv, tv):
        y = (hv * lax.rsqrt(jnp.mean(hv * hv, axis=-1, keepdims=True) + EPS)) * wv
        e = y - tv
        return 0.5 * jnp.sum(jnp.sum(e * e, axis=-1, keepdims=True), axis=0, keepdims=True) * (1.0 / D)

    def body(h_ref, w_ref, t_ref, loss_ref, dh_ref, dw_ref):
        i = pl.program_id(0)
        tv = t_ref[...]
        val, vjp = jax.vjp(lambda a, b_: f(a, b_, tv), h_ref[...], w_ref[...])
        dh, dw = vjp(jnp.ones((1, 1), f32))
        dh_ref[...] = dh

        @pl.when(i == 0)
        def _():
            loss_ref[...] = jnp.zeros_like(loss_ref)
            dw_ref[...] = jnp.zeros_like(dw_ref)

        loss_ref[...] += jnp.broadcast_to(val, (1, 128))
        dw_ref[...] += dw

    blk = pl.BlockSpec((TB, D), lambda i: (i, 0))
    row = pl.BlockSpec((1, D), lambda i: (0, 0))
    return pl.pallas_call(body, grid=(T // TB,), in_specs=[blk, row, blk],
                          out_specs=[pl.BlockSpec((1, 128), lambda i: (0, 0)), blk, row],
                          out_shape=[S((1, 128), f32), S((T, D), f32), S((1, D), f32)],
                          compiler_params=_cparams("arbitrary"), name=name)(h, w, tgt)


CB = 256
RT = 32
RTB = 16


def _fold8(t):
    acc = t[0:8]
    for k in range(1, t.shape[0] // 8):
        acc = acc + t[8 * k:8 * (k + 1)]
    return acc


def _rows(start, off=0, rt=RT):
    return pl.ds(pl.multiple_of(start + off, 8), rt)


def _rowsb(start, off=0):
    return _rows(start, off, RTB)


def _zero_rows(ref, start, n):
    ref[pl.ds(start, n), :] = jnp.zeros((n, ref.shape[1]), f32)


K5, HALF5, PAD5 = 5, 2, 8


def _shift_copies5(base_ref, s_ref, ln, sign):
    for k in range(K5):
        s_ref[k, pl.ds(0, ln), :] = base_ref[pl.ds(PAD5 + sign * (k - HALF5), ln), :]


def ssd_conv_fwd(u, w, b, segs, name):
    T = u.shape[0]
    maxlen = max(ln for _, ln in segs)

    def body(u_ref, w_ref, b_ref, o_ref, base_ref, s_ref):
        wv = [w_ref[pl.ds(k, 1), :] for k in range(K5)]
        bv = b_ref[...]
        for s0, ln in segs:
            _zero_rows(base_ref, 0, PAD5)
            _zero_rows(base_ref, PAD5 + ln, PAD5)
            base_ref[pl.ds(PAD5, ln), :] = u_ref[pl.ds(s0, ln), :]
            _shift_copies5(base_ref, s_ref, ln, 1)

            def tile(i, carry):
                r = i * RT
                acc = jnp.broadcast_to(bv, (RT, CB))
                for k in range(K5):
                    acc = acc + s_ref[k, _rows(r), :] * wv[k]
                o_ref[_rows(r, s0), :] = acc * _sigmoid(acc)
                return carry

            lax.fori_loop(0, ln // RT, tile, 0)

    return pl.pallas_call(
        body, grid=(CONVD // CB,),
        in_specs=[pl.BlockSpec((T, CB), lambda j: (0, j)), pl.BlockSpec((K5, CB), lambda j: (0, j)),
                  pl.BlockSpec((1, CB), lambda j: (0, j))],
        out_specs=pl.BlockSpec((T, CB), lambda j: (0, j)), out_shape=S((T, CONVD), f32),
        scratch_shapes=[pltpu.VMEM((maxlen + 2 * PAD5, CB), f32), pltpu.VMEM((K5, maxlen, CB), f32)],
        compiler_params=_cparams("parallel"), name=name)(u, w, b)


def ssd_conv_bwd(proj, w, b, dy2, dskip, segs, name):
    T = proj.shape[0]
    maxlen = max(ln for _, ln in segs)
    nskip = DI // CB

    def body(u_ref, w_ref, b_ref, dya_ref, dyb_ref, dsk_ref, du_ref, dw_ref, db_ref, base_ref, s_ref):
        wv = [w_ref[pl.ds(k, 1), :] for k in range(K5)]
        bv = b_ref[...]
        has_skip = (pl.program_id(0) < nskip).astype(f32)
        acc8 = tuple(jnp.zeros((8, CB), f32) for _ in range(K5 + 1))
        for s0, ln in segs:
            _zero_rows(base_ref, 0, PAD5)
            _zero_rows(base_ref, PAD5 + ln, PAD5)
            base_ref[pl.ds(PAD5, ln), :] = u_ref[pl.ds(s0, ln), :]
            _shift_copies5(base_ref, s_ref, ln, 1)

            def tile1(i, carry):
                r = i * RTB
                taps = [s_ref[k, _rowsb(r), :] for k in range(K5)]
                pre = jnp.broadcast_to(bv, (RTB, CB))
                for k in range(K5):
                    pre = pre + taps[k] * wv[k]
                sg = _sigmoid(pre)
                dy = dya_ref[_rowsb(r, s0), :] + dyb_ref[_rowsb(r, s0), :] + has_skip * dsk_ref[_rowsb(r, s0), :]
                dpre = dy * (sg * (1.0 + pre * (1.0 - sg)))
                base_ref[_rowsb(r, PAD5), :] = dpre
                new = [carry[k] + _fold8(dpre * taps[k]) for k in range(K5)]
                new.append(carry[K5] + _fold8(dpre))
                return tuple(new)

            acc8 = lax.fori_loop(0, ln // RTB, tile1, acc8)
            _shift_copies5(base_ref, s_ref, ln, -1)

            def tile2(i, carry):
                r = i * RTB
                du = jnp.zeros((RTB, CB), f32)
                for k in range(K5):
                    du = du + s_ref[k, _rowsb(r), :] * wv[k]
                du_ref[_rowsb(r, s0), :] = du.astype(bf16)
                return carry

            lax.fori_loop(0, ln // RTB, tile2, 0)
        for k in range(K5):
            dw_ref[pl.ds(k, 1), :] = jnp.sum(acc8[k], axis=0, keepdims=True)
        db_ref[...] = jnp.sum(acc8[K5], axis=0, keepdims=True)

    cblk = pl.BlockSpec((T, CB), lambda j: (0, j))
    return pl.pallas_call(
        body, grid=(CONVD // CB,),
        in_specs=[cblk, pl.BlockSpec((K5, CB), lambda j: (0, j)), pl.BlockSpec((1, CB), lambda j: (0, j)),
                  pl.BlockSpec((None, T, CB), lambda j: (0, 0, j)), pl.BlockSpec((None, T, CB), lambda j: (1, 0, j)),
                  pl.BlockSpec((T, CB), lambda j: (0, jnp.minimum(j, nskip - 1)))],
        out_specs=[cblk, pl.BlockSpec((K5, CB), lambda j: (0, j)), pl.BlockSpec((1, CB), lambda j: (0, j))],
        out_shape=[S((T, CONVD), bf16), S((K5, CONVD), f32), S((1, CONVD), f32)],
        scratch_shapes=[pltpu.VMEM((maxlen + 2 * PAD5, CB), f32), pltpu.VMEM((K5, maxlen, CB), f32)],
        compiler_params=_cparams("parallel"), name=name)(proj, w, b, dy2, dy2, dskip)


GPAD = GRID_W


def _grid_copies(g_ref, src, L):
    col = lax.broadcasted_iota(jnp.int32, (L, CB), 0) & (GRID_W - 1)
    for d in range(3):
        _zero_rows(g_ref.at[d], 0, GPAD)
        _zero_rows(g_ref.at[d], GPAD + L, GPAD)
    g_ref[1, pl.ds(GPAD, L), :] = src
    g_ref[0, pl.ds(GPAD, L), :] = jnp.where(col != 0, g_ref[1, pl.ds(GPAD - 1, L), :], 0.0)
    g_ref[2, pl.ds(GPAD, L), :] = jnp.where(col != GRID_W - 1, g_ref[1, pl.ds(GPAD + 1, L), :], 0.0)


def ffn_gate_fwd(val, gate, cw, cb_, name):
    L = val.shape[0]
    nb = FH // CB

    def body(val_ref, gate_ref, w_ref, b_ref, o_ref, g_ref):
        wv = [w_ref[pl.ds(t, 1), :] for t in range(9)]
        bv = b_ref[...]
        _grid_copies(g_ref, gate_ref[...], L)

        def tile(i, carry):
            r = i * RT
            acc = jnp.broadcast_to(bv, (RT, CB))
            for dr in range(3):
                for dc in range(3):
                    acc = acc + g_ref[dc, _rows(r, GPAD + (dr - 1) * GRID_W), :] * wv[3 * dr + dc]
            o_ref[_rows(r), :] = (acc * _sigmoid(acc) * val_ref[_rows(r), :]).astype(bf16)
            return carry

        lax.fori_loop(0, L // RT, tile, 0)

    cblk = pl.BlockSpec((L, CB), lambda j: (0, j))
    return pl.pallas_call(
        body, grid=(nb,),
        in_specs=[cblk, cblk, pl.BlockSpec((9, CB), lambda j: (0, j)), pl.BlockSpec((1, CB), lambda j: (0, j))],
        out_specs=cblk, out_shape=S((L, FH), bf16),
        scratch_shapes=[pltpu.VMEM((3, L + 2 * GPAD, CB), f32)],
        compiler_params=_cparams("parallel"), name=name)(val, gate, cw, cb_)


def ffn_gate_bwd(val, gate, cw, cb_, dact, name):
    L = val.shape[0]
    nb = FH // CB

    def body(val_ref, gate_ref, w_ref, b_ref, da_ref, dval_ref, dgate_ref, dw_ref, db_ref, g_ref, d_ref):
        wv = [w_ref[pl.ds(t, 1), :] for t in range(9)]
        bv = b_ref[...]
        _grid_copies(g_ref, gate_ref[...], L)

        def tile1(i, carry):
            r = i * RTB

            def tap(t):
                return g_ref[t % 3, _rowsb(r, GPAD + (t // 3 - 1) * GRID_W), :]

            pre = jnp.broadcast_to(bv, (RTB, CB))
            for t in range(9):
                pre = pre + tap(t) * wv[t]
            sg = _sigmoid(pre)
            da = da_ref[_rowsb(r), :].astype(f32)
            dval_ref[_rowsb(r), :] = (da * pre * sg).astype(bf16)
            dpre = da * val_ref[_rowsb(r), :] * (sg * (1.0 + pre * (1.0 - sg)))
            d_ref[_rowsb(r), :] = dpre
            new = [carry[t] + _fold8(dpre * tap(t)) for t in range(9)]
            new.append(carry[9] + _fold8(dpre))
            return tuple(new)

        acc8 = lax.fori_loop(0, L // RTB, tile1, tuple(jnp.zeros((8, CB), f32) for _ in range(10)))
        for t in range(9):
            dw_ref[pl.ds(t, 1), :] = jnp.sum(acc8[t], axis=0, keepdims=True)
        db_ref[...] = jnp.sum(acc8[9], axis=0, keepdims=True)
        _grid_copies(g_ref, d_ref[...], L)

        def tile2(i, carry):
            r = i * RTB
            dg = jnp.zeros((RTB, CB), f32)
            for dr in range(3):
                for dc in range(3):
                    dg = dg + g_ref[2 - dc, _rowsb(r, GPAD - (dr - 1) * GRID_W), :] * wv[3 * dr + dc]
            dgate_ref[_rowsb(r), :] = dg.astype(bf16)
            return carry

        lax.fori_loop(0, L // RTB, tile2, 0)

    cblk = pl.BlockSpec((L, CB), lambda j: (0, j))
    return pl.pallas_call(
        body, grid=(nb,),
        in_specs=[cblk, cblk, pl.BlockSpec((9, CB), lambda j: (0, j)), pl.BlockSpec((1, CB), lambda j: (0, j)), cblk],
        out_specs=[cblk, cblk, pl.BlockSpec((9, CB), lambda j: (0, j)), pl.BlockSpec((1, CB), lambda j: (0, j))],
        out_shape=[S((L, FH), bf16), S((L, FH), bf16), S((9, FH), f32), S((1, FH), f32)],
        scratch_shapes=[pltpu.VMEM((3, L + 2 * GPAD, CB), f32), pltpu.VMEM((L, CB), f32)],
        compiler_params=_cparams("parallel"), name=name)(val, gate, cw, cb_, dact)


CONF_K = 31
CHALF = CONF_K // 2
CPAD = 16


def _shift_copies8(c_ref, base_ref, L):
    n = L + 2 * CPAD - 8
    for b_ in range(8):
        c_ref[b_, pl.ds(0, n), :] = base_ref[pl.ds(b_, n), :]


def _tap_ab(o):
    return o % 8, o - o % 8


def conf_glu_conv_fwd(pa, pg, b1, wdw, bdw, name):
    L = pa.shape[0]
    nb = D // CB

    def body(pa_ref, pg_ref, ba_ref, bg_ref, w_ref, bdw_ref, o_ref, base_ref, c_ref):
        _zero_rows(base_ref, 0, CPAD)
        _zero_rows(base_ref, CPAD + L, CPAD)
        base_ref[pl.ds(CPAD, L), :] = (pa_ref[...] + ba_ref[...]) * _sigmoid(pg_ref[...] + bg_ref[...])
        _shift_copies8(c_ref, base_ref, L)
        bv = bdw_ref[...]

        def tile(i, carry):
            r = i * RT
            acc = jnp.broadcast_to(bv, (RT, CB))
            for k in range(CONF_K):
                b_, a8 = _tap_ab(k - CHALF)
                acc = acc + c_ref[b_, _rows(r, CPAD + a8), :] * w_ref[pl.ds(k, 1), :]
            o_ref[_rows(r), :] = acc
            return carry

        lax.fori_loop(0, L // RT, tile, 0)

    cblk = pl.BlockSpec((L, CB), lambda j: (0, j))
    rblk = pl.BlockSpec((1, CB), lambda j: (0, j))
    rgblk = pl.BlockSpec((1, CB), lambda j: (0, nb + j))
    return pl.pallas_call(
        body, grid=(nb,), in_specs=[cblk, cblk, rblk, rgblk, pl.BlockSpec((CONF_K, CB), lambda j: (0, j)), rblk],
        out_specs=cblk, out_shape=S((L, D), f32),
        scratch_shapes=[pltpu.VMEM((L + 2 * CPAD, CB), f32), pltpu.VMEM((8, L + 2 * CPAD, CB), f32)],
        compiler_params=_cparams("parallel"), name=name)(pa, pg, b1, b1, wdw, bdw)


def conf_glu_conv_bwd(pa, pg, b1, wdw, dy, name):
    L = pa.shape[0]
    nb = D // CB

    def body(pa_ref, pg_ref, ba_ref, bg_ref, w_ref, dy_ref, dpa_ref, dpg_ref, dba_ref, dbg_ref, dw_ref, dbdw_ref,
             base_ref, c_ref, acc_ref):
        _zero_rows(base_ref, 0, CPAD)
        _zero_rows(base_ref, CPAD + L, CPAD)
        base_ref[pl.ds(CPAD, L), :] = (pa_ref[...] + ba_ref[...]) * _sigmoid(pg_ref[...] + bg_ref[...])
        _shift_copies8(c_ref, base_ref, L)
        acc_ref[...] = jnp.zeros_like(acc_ref)

        def tile1(i, carry):
            r = i * RTB
            dyt = dy_ref[_rowsb(r), :]
            for k in range(CONF_K):
                b_, a8 = _tap_ab(k - CHALF)
                acc_ref[k] += _fold8(dyt * c_ref[b_, _rowsb(r, CPAD + a8), :])
            return carry + _fold8(dyt)

        db8 = lax.fori_loop(0, L // RTB, tile1, jnp.zeros((8, CB), f32))
        dbdw_ref[...] = jnp.sum(db8, axis=0, keepdims=True)
        for k in range(CONF_K):
            dw_ref[pl.ds(k, 1), :] = jnp.sum(acc_ref[k], axis=0, keepdims=True)
        base_ref[pl.ds(CPAD, L), :] = dy_ref[...]
        _shift_copies8(c_ref, base_ref, L)
        ba = ba_ref[...]
        bg = bg_ref[...]

        def tile2(i, carry):
            r = i * RTB
            dglu = jnp.zeros((RTB, CB), f32)
            for k in range(CONF_K):
                b_, a8 = _tap_ab(CHALF - k)
                dglu = dglu + c_ref[b_, _rowsb(r, CPAD + a8), :] * w_ref[pl.ds(k, 1), :]
            a = pa_ref[_rowsb(r), :] + ba
            sg = _sigmoid(pg_ref[_rowsb(r), :] + bg)
            dpa = dglu * sg
            dpg = dglu * a * (sg * (1.0 - sg))
            dpa_ref[_rowsb(r), :] = dpa.astype(bf16)
            dpg_ref[_rowsb(r), :] = dpg.astype(bf16)
            return carry[0] + _fold8(dpa), carry[1] + _fold8(dpg)

        s8 = lax.fori_loop(0, L // RTB, tile2, (jnp.zeros((8, CB), f32), jnp.zeros((8, CB), f32)))
        dba_ref[...] = jnp.sum(s8[0], axis=0, keepdims=True)
        dbg_ref[...] = jnp.sum(s8[1], axis=0, keepdims=True)

    cblk = pl.BlockSpec((L, CB), lambda j: (0, j))
    rblk = pl.BlockSpec((1, CB), lambda j: (0, j))
    rgblk = pl.BlockSpec((1, CB), lambda j: (0, nb + j))
    wblk = pl.BlockSpec((CONF_K, CB), lambda j: (0, j))
    return pl.pallas_call(
        body, grid=(nb,), in_specs=[cblk, cblk, rblk, rgblk, wblk, cblk],
        out_specs=[cblk, cblk, rblk, rblk, wblk, rblk],
        out_shape=[S((L, D), bf16), S((L, D), bf16), S((1, D), f32), S((1, D), f32), S((CONF_K, D), f32), S((1, D), f32)],
        scratch_shapes=[pltpu.VMEM((L + 2 * CPAD, CB), f32), pltpu.VMEM((8, L + 2 * CPAD, CB), f32),
                        pltpu.VMEM((CONF_K, 8, CB), f32)],
        compiler_params=_cparams("parallel"), name=name)(pa, pg, b1, b1, wdw, dy)


def _ln_silu_f(x, w, b):
    mu = jnp.mean(x, axis=-1, keepdims=True)
    d = x - mu
    y = d * lax.rsqrt(jnp.mean(d * d, axis=-1, keepdims=True) + EPS) * w + b
    return y * _sigmoid(y)


def ln_silu_fwd(x, w, b, name):
    T = x.shape[0]

    def body(x_ref, w_ref, b_ref, o_ref):
        o_ref[...] = _ln_silu_f(x_ref[...], w_ref[...], b_ref[...]).astype(bf16)

    blk = pl.BlockSpec((TB, D), lambda i: (i, 0))
    row = pl.BlockSpec((1, D), lambda i: (0, 0))
    return pl.pallas_call(body, grid=(T // TB,), in_specs=[blk, row, row], out_specs=blk, out_shape=S((T, D), bf16),
                          compiler_params=_cparams("parallel"), name=name)(x, w, b)


def ln_silu_bwd(x, w, b, ds, name):
    T = x.shape[0]

    def body(x_ref, w_ref, b_ref, ds_ref, dx_ref, dw_ref, db_ref):
        i = pl.program_id(0)
        _, vjp = jax.vjp(_ln_silu_f, x_ref[...], w_ref[...], b_ref[...])
        dx, dw, db = vjp(ds_ref[...].astype(f32))
        dx_ref[...] = dx

        @pl.when(i == 0)
        def _():
            dw_ref[...] = jnp.zeros_like(dw_ref)
            db_ref[...] = jnp.zeros_like(db_ref)

        dw_ref[...] += dw
        db_ref[...] += db

    blk = pl.BlockSpec((TB, D), lambda i: (i, 0))
    row = pl.BlockSpec((1, D), lambda i: (0, 0))
    return pl.pallas_call(body, grid=(T // TB,), in_specs=[blk, row, row, blk], out_specs=[blk, row, row],
                          out_shape=[S((T, D), f32), S((1, D), f32), S((1, D), f32)],
                          compiler_params=_cparams("arbitrary"), name=name)(x, w, b, ds)


def _mxu(a, b, dims):
    return lax.dot_general(a.astype(bf16), b.astype(bf16), (dims, ((), ())), preferred_element_type=f32)


def _nn(a, b):
    return _mxu(a, b, ((1,), (0,)))


def _nt(a, b):
    return _mxu(a, b, ((1,), (1,)))


def _tn(a, b):
    return _mxu(a, b, ((0,), (0,)))


@jax.custom_vjp
def _dot_nn(a, b):
    return _nn(a, b)


@jax.custom_vjp
def _dot_nt(a, b):
    return _nt(a, b)


@jax.custom_vjp
def _dot_tn(a, b):
    return _tn(a, b)


_dot_nn.defvjp(lambda a, b: (_nn(a, b), (a, b)), lambda res, g: (_nt(g, res[1]), _tn(res[0], g)))
_dot_nt.defvjp(lambda a, b: (_nt(a, b), (a, b)), lambda res, g: (_nn(g, res[1]), _tn(g, res[0])))
_dot_tn.defvjp(lambda a, b: (_tn(a, b), (a, b)), lambda res, g: (_nt(res[1], g), _nn(res[0], g)))


def _lanes_to_rows(v):
    r = lax.broadcasted_iota(jnp.int32, (GW, GW), 0)
    c = lax.broadcasted_iota(jnp.int32, (GW, GW), 1)
    return jnp.sum(jnp.where(r == c, jnp.broadcast_to(v, (GW, GW)), 0.0), axis=1, keepdims=True)


def _ssd_chunk(x, B, C, dtc, dtr, bc, br, alc, alr, s_in, is_fwd):
    row = lax.broadcasted_iota(jnp.int32, (Q, Q), 0)
    col = lax.broadcasted_iota(jnp.int32, (Q, Q), 1)
    sgn = jnp.where(is_fwd, 1, -1).astype(jnp.int32)
    mask = (row - col) * sgn >= 0
    mf = mask.astype(f32)
    lane_head = lax.broadcasted_iota(jnp.int32, (1, GW), 1) // P

    def spread(v):
        out = jnp.zeros((v.shape[0], GW), f32)
        for r in range(HPG):
            out = jnp.where(lane_head == r, v[:, r:r + 1], out)
        return out

    dt_c = _softplus(dtc + bc)
    dt_r = _softplus(dtr + br)
    a_c = dt_c * (-jnp.exp(alc))
    a_r = dt_r * (-jnp.exp(alr))
    acum_c = jnp.dot(mf, a_c, precision=HI, preferred_element_type=f32)
    acum_r = lax.dot_general(a_r, mf, (((1,), (1,)), ((), ())), precision=HI, preferred_element_type=f32)
    tot_c = jnp.sum(a_c, axis=0, keepdims=True)
    dt_e = spread(dt_c)
    acum_e = spread(acum_c)
    tot_e = spread(tot_c)
    xdt = x * dt_e
    cb = _dot_nt(C, B)
    y = jnp.zeros((Q, GW), f32)
    for r in range(HPG):
        seg = acum_c[:, r:r + 1] - acum_r[r:r + 1, :]
        dec = jnp.exp(jnp.where(mask, seg, -jnp.inf))
        y = y + jnp.where(lane_head == r, _dot_nn(cb * dec, xdt), 0.0)
    y = y + _dot_nt(C, s_in) * jnp.exp(acum_e)
    xe = xdt * jnp.exp(tot_e - acum_e)
    s_out = _lanes_to_rows(jnp.exp(tot_e)) * s_in + _dot_tn(xe, B)
    return y, s_out


def _chunk_index(d, t, nctx, nc):
    bwd = jnp.where(t < nctx, nctx - 1 - t, nc - 1 - (t - nctx))
    return jnp.where(d == 0, t, bwd)


def _ssd_in_specs(ci):
    small_c = pl.BlockSpec((None, G, 1, HPG), lambda d, t: (d, 0, 0, 0))
    small_r = pl.BlockSpec((None, G, HPG, 1), lambda d, t: (d, 0, 0, 0))
    return [
        pl.BlockSpec((Q, CONVD), lambda d, t: (ci(d, t), 0)),
        pl.BlockSpec((None, G, Q, HPG), lambda d, t: (d, 0, ci(d, t), 0)),
        pl.BlockSpec((None, G, HPG, Q), lambda d, t: (d, 0, 0, ci(d, t))),
        small_c, small_r, small_c, small_r,
    ]


def _group_cols(g):
    return pl.ds(g * GW, GW), pl.ds(DI + g * N, N), pl.ds(DI + G * N + g * N, N)


def ssd_scan_fwd(xbc, dtc, dtr, bc, br, alc, alr, nctx, name):
    T = xbc.shape[0]
    nc = T // Q

    def body(xbc_ref, dtc_ref, dtr_ref, bc_ref, br_ref, alc_ref, alr_ref, y_ref, sin_ref, st_ref):
        d = pl.program_id(0)
        t = pl.program_id(1)

        @pl.when(t == 0)
        def _():
            st_ref[...] = jnp.zeros_like(st_ref)

        for g in range(G):
            xs, bs, cs = _group_cols(g)
            s_in = st_ref[g]
            sin_ref[g] = s_in
            y, s_out = _ssd_chunk(xbc_ref[:, xs], xbc_ref[:, bs], xbc_ref[:, cs], dtc_ref[g], dtr_ref[g], bc_ref[g], br_ref[g],
                                  alc_ref[g], alr_ref[g], s_in, d == 0)
            y_ref[:, xs] = y
            st_ref[g] = s_out

    ci = lambda d, t: _chunk_index(d, t, nctx, nc)
    out_specs = [
        pl.BlockSpec((None, Q, DI), lambda d, t: (d, ci(d, t), 0)),
        pl.BlockSpec((None, None, G, GW, N), lambda d, t: (d, ci(d, t), 0, 0, 0)),
    ]
    return pl.pallas_call(
        body, grid=(2, nc), in_specs=_ssd_in_specs(ci), out_specs=out_specs,
        out_shape=[S((2, T, DI), f32), S((2, nc, G, GW, N), f32)],
        scratch_shapes=[pltpu.VMEM((G, GW, N), f32)],
        compiler_params=_cparams("arbitrary", "arbitrary"), name=name,
    )(xbc, dtc, dtr, bc, br, alc, alr)


def ssd_scan_bwd(xbc, dtc, dtr, bc, br, alc, alr, s_in_all, dy, nctx, name):
    T = xbc.shape[0]
    nc = T // Q

    def body(xbc_ref, dtc_ref, dtr_ref, bc_ref, br_ref, alc_ref, alr_ref, sin_ref, dy_ref,
             dxbc_ref, ddtc_ref, ddtr_ref, dbc_ref, dbr_ref, dalc_ref, dalr_ref, ds_ref):
        d = pl.program_id(0)
        t = pl.program_id(1)

        @pl.when(t == 0)
        def _():
            ds_ref[...] = jnp.zeros_like(ds_ref)
            dbc_ref[...] = jnp.zeros_like(dbc_ref)
            dbr_ref[...] = jnp.zeros_like(dbr_ref)
            dalc_ref[...] = jnp.zeros_like(dalc_ref)
            dalr_ref[...] = jnp.zeros_like(dalr_ref)

        f = functools.partial(_ssd_chunk, is_fwd=(d == 0))
        for g in range(G):
            xs, bs, cs = _group_cols(g)
            _, vjp = jax.vjp(f, xbc_ref[:, xs], xbc_ref[:, bs], xbc_ref[:, cs], dtc_ref[g], dtr_ref[g], bc_ref[g], br_ref[g],
                             alc_ref[g], alr_ref[g], sin_ref[g])
            dx, dB, dC, ddtc, ddtr, dbc, dbr, dalc, dalr, ds = vjp((dy_ref[:, xs], ds_ref[g]))
            dxbc_ref[:, xs] = dx
            dxbc_ref[:, bs] = dB
            dxbc_ref[:, cs] = dC
            ddtc_ref[g] = ddtc
            ddtr_ref[g] = ddtr
            dbc_ref[g] += dbc
            dbr_ref[g] += dbr
            dalc_ref[g] += dalc
            dalr_ref[g] += dalr
            ds_ref[g] = ds

    ci = lambda d, t: _chunk_index(d, nc - 1 - t, nctx, nc)
    in_specs = _ssd_in_specs(ci) + [
        pl.BlockSpec((None, None, G, GW, N), lambda d, t: (d, ci(d, t), 0, 0, 0)),
        pl.BlockSpec((Q, DI), lambda d, t: (ci(d, t), 0)),
    ]
    small_c = pl.BlockSpec((None, G, 1, HPG), lambda d, t: (d, 0, 0, 0))
    small_r = pl.BlockSpec((None, G, HPG, 1), lambda d, t: (d, 0, 0, 0))
    out_specs = [
        pl.BlockSpec((None, Q, CONVD), lambda d, t: (d, ci(d, t), 0)),
        pl.BlockSpec((None, G, Q, HPG), lambda d, t: (d, 0, ci(d, t), 0)),
        pl.BlockSpec((None, G, HPG, Q), lambda d, t: (d, 0, 0, ci(d, t))),
        small_c, small_r, small_c, small_r,
    ]
    out_shape = [S((2, T, CONVD), f32), S((2, G, T, HPG), f32), S((2, G, HPG, T), f32),
                 S((2, G, 1, HPG), f32), S((2, G, HPG, 1), f32), S((2, G, 1, HPG), f32), S((2, G, HPG, 1), f32)]
    return pl.pallas_call(
        body, grid=(2, nc), in_specs=in_specs, out_specs=out_specs, out_shape=out_shape,
        scratch_shapes=[pltpu.VMEM((G, GW, N), f32)],
        compiler_params=_cparams("arbitrary", "arbitrary"), name=name,
    )(xbc, dtc, dtr, bc, br, alc, alr, s_in_all, dy)


GTB = 128


def _gate_norm_f(yf, yb, x, z, dexp, w):
    y = (yf + yb + dexp * x) * (z * _sigmoid(z))
    return y * lax.rsqrt(jnp.mean(y * y, axis=-1, keepdims=True) + EPS) * w


def ssd_gate_fwd(y2, xbc, proj, dexp, w, nctxb, name):
    T = xbc.shape[0]
    L = T - nctxb * GTB

    def body(yf_ref, yb_ref, x_ref, z_ref, d_ref, w_ref, o_ref):
        o_ref[...] = _gate_norm_f(yf_ref[...], yb_ref[...], x_ref[...], z_ref[...], d_ref[...], w_ref[...]).astype(bf16)

    wide = pl.BlockSpec((GTB, DI), lambda i: (i + nctxb, 0))
    row = pl.BlockSpec((1, DI), lambda i: (0, 0))
    return pl.pallas_call(
        body, grid=(L // GTB,),
        in_specs=[pl.BlockSpec((None, GTB, DI), lambda i: (0, i + nctxb, 0)),
                  pl.BlockSpec((None, GTB, DI), lambda i: (1, i + nctxb, 0)), wide, wide, row, row],
        out_specs=pl.BlockSpec((GTB, DI), lambda i: (i, 0)), out_shape=S((L, DI), bf16),
        compiler_params=_cparams("parallel"), name=name)(y2, y2, xbc, proj, dexp, w)


def ssd_gate_bwd(y2, xbc, proj, dexp, w, dyn, nctxb, name):
    T = xbc.shape[0]
    nb = T // GTB

    def body(yf_ref, yb_ref, x_ref, z_ref, d_ref, w_ref, dyn_ref, dy_ref, dx_ref, dz_ref, dd_ref, dw_ref):
        i = pl.program_id(0)

        @pl.when(i == 0)
        def _():
            dd_ref[...] = jnp.zeros_like(dd_ref)
            dw_ref[...] = jnp.zeros_like(dw_ref)

        @pl.when(i < nctxb)
        def _():
            dy_ref[...] = jnp.zeros_like(dy_ref)
            dx_ref[...] = jnp.zeros_like(dx_ref)
            dz_ref[...] = jnp.zeros_like(dz_ref)

        @pl.when(i >= nctxb)
        def _():
            _, vjp = jax.vjp(_gate_norm_f, yf_ref[...], yb_ref[...], x_ref[...], z_ref[...], d_ref[...], w_ref[...])
            dyf, _, dx, dz, dd, dw = vjp(dyn_ref[...].astype(f32))
            dy_ref[...] = dyf
            dx_ref[...] = dx
            dz_ref[...] = dz.astype(bf16)
            fold = (lax.broadcasted_iota(jnp.int32, (DI, 128), 0) // P == lax.broadcasted_iota(jnp.int32, (DI, 128), 1))
            dd_ref[...] += jnp.dot(dd, fold.astype(f32), precision=HI, preferred_element_type=f32)
            dw_ref[...] += dw

    wide = pl.BlockSpec((GTB, DI), lambda i: (i, 0))
    row = pl.BlockSpec((1, DI), lambda i: (0, 0))
    hrow = pl.BlockSpec((1, 128), lambda i: (0, 0))
    return pl.pallas_call(
        body, grid=(nb,),
        in_specs=[pl.BlockSpec((None, GTB, DI), lambda i: (0, i, 0)), pl.BlockSpec((None, GTB, DI), lambda i: (1, i, 0)),
                  wide, wide, row, row, pl.BlockSpec((GTB, DI), lambda i: (jnp.maximum(i - nctxb, 0), 0))],
        out_specs=[wide, wide, wide, hrow, row],
        out_shape=[S((T, DI), f32), S((T, DI), f32), S((T, DI), bf16), S((1, 128), f32), S((1, DI), f32)],
        compiler_params=_cparams("arbitrary"), name=name)(y2, y2, xbc, proj, dexp, w, dyn)


CROWS = 2 * N_DEV


def mod_fwd(c16, modw, name):
    nl, _, cols = modw.shape

    def body(c_ref, w_ref, o_ref):
        cv = c_ref[...]
        s = cv * _sigmoid(cv)
        for l in range(nl):
            o_ref[l] = jnp.dot(s, w_ref[l], precision=HI, preferred_element_type=f32)

    return pl.pallas_call(body, in_specs=[VMEM, VMEM], out_specs=VMEM, out_shape=S((nl, CROWS, cols), f32),
                          compiler_params=pltpu.CompilerParams(vmem_limit_bytes=VMEM_LIMIT_BYTES), name=name)(c16, modw)


def mod_bwd(c16, modw, dm_sh, dm_all, name):
    nl, _, cols = modw.shape

    def body(c_ref, w_ref, dm_ref, dmall_ref, dw_ref, dc_ref, db_ref):
        cv = c_ref[...]
        sg = _sigmoid(cv)
        s = cv * sg
        ds_dc = sg * (1.0 + cv * (1.0 - sg))
        is_ctx = lax.broadcasted_iota(jnp.int32, (CROWS, D), 0) >= N_DEV
        dc = jnp.zeros((1, D), f32)
        for l in range(nl):
            dm = dm_ref[l]
            dw_ref[l] = lax.dot_general(s, dm, (((0,), (0,)), ((), ())), precision=HI, preferred_element_type=f32)
            dsv = lax.dot_general(dm, w_ref[l], (((1,), (1,)), ((), ())), precision=HI, preferred_element_type=f32)
            dc = dc + jnp.sum(jnp.where(is_ctx, dsv * ds_dc, 0.0), axis=0, keepdims=True)
            db_ref[pl.ds(l, 1), :] = jnp.sum(dmall_ref[l], axis=0, keepdims=True)
        dc_ref[...] = dc

    return pl.pallas_call(
        body, in_specs=[VMEM, VMEM, VMEM, VMEM], out_specs=[VMEM, VMEM, VMEM],
        out_shape=[S(modw.shape, f32), S((1, D), f32), S((nl, 6 * D), f32)],
        compiler_params=pltpu.CompilerParams(vmem_limit_bytes=VMEM_LIMIT_BYTES), name=name)(c16, modw, dm_sh, dm_all)


def adamw(w, g, m, v, name):
    R, C = w.shape
    rb = R if R <= 512 else max(r_ for r_ in range(8, 513, 8) if R % r_ == 0)
    bc1 = 1.0 - ADAM_B1 ** ADAM_STEP
    bc2 = 1.0 - ADAM_B2 ** ADAM_STEP

    def body(w_ref, g_ref, m_ref, v_ref, d_ref, nm_ref, nv_ref):
        gv = g_ref[...]
        m_new = ADAM_B1 * m_ref[...] + (1.0 - ADAM_B1) * gv
        v_new = ADAM_B2 * v_ref[...] + (1.0 - ADAM_B2) * (gv * gv)
        m_hat = m_new / bc1
        v_hat = v_new / bc2
        d_ref[...] = -ADAM_LR * (m_hat / (jnp.sqrt(v_hat) + ADAM_EPS) + ADAM_WD * w_ref[...])
        nm_ref[...] = m_new
        nv_ref[...] = v_new

    blk = pl.BlockSpec((rb, C), lambda i: (i, 0))
    return pl.pallas_call(body, grid=(R // rb,), in_specs=[blk] * 4, out_specs=[blk] * 3,
                          out_shape=[S((R, C), f32)] * 3, compiler_params=_cparams("parallel"), name=name)(w, g, m, v)


def _me():
    return lax.axis_index("x"), lax.axis_index("y"), lax.axis_index("c")


def allgather_small(x, name, with_sum=False):
    r, w = x.shape

    def body(x_ref, *refs):
        if with_sum:
            out_ref, sum_ref, send_sems, recv_sems = refs
        else:
            out_ref, send_sems, recv_sems = refs
        mx, my, mc = _me()
        me = 4 * mx + 2 * my + mc
        out_ref[me] = x_ref[...]
        peers = []
        for k in range(1, N_DEV):
            kx, ky, kc = (k >> 2) & 1, (k >> 1) & 1, k & 1
            peers.append((mx + kx - 2 * mx * kx, my + ky - 2 * my * ky, mc + kc - 2 * mc * kc))
        copies = []
        for k, peer in enumerate(peers):
            cp = pltpu.make_async_remote_copy(src_ref=x_ref, dst_ref=out_ref.at[me], send_sem=send_sems.at[k],
                                              recv_sem=recv_sems.at[k], device_id=peer, device_id_type=MESH)
            cp.start()
            copies.append(cp)
        for k, (px, py, pc) in enumerate(peers):
            pltpu.make_async_remote_copy(src_ref=x_ref, dst_ref=out_ref.at[4 * px + 2 * py + pc], send_sem=send_sems.at[k],
                                         recv_sem=recv_sems.at[k], device_id=(px, py, pc), device_id_type=MESH).wait_recv()
        for cp in copies:
            cp.wait_send()
        if with_sum:
            acc = out_ref[0]
            for j in range(1, N_DEV):
                acc = acc + out_ref[j]
            sum_ref[...] = acc

    out_shape = [S((N_DEV, r, w), f32)] + ([S((r, w), f32)] if with_sum else [])
    outs = pl.pallas_call(
        body, in_specs=[VMEM], out_specs=[VMEM] * len(out_shape), out_shape=out_shape,
        scratch_shapes=[pltpu.SemaphoreType.DMA((N_DEV - 1,)), pltpu.SemaphoreType.DMA((N_DEV - 1,))],
        compiler_params=pltpu.CompilerParams(vmem_limit_bytes=VMEM_LIMIT_BYTES), name=name)(x)
    return outs if with_sum else outs[0]


def allgather_big(shards, name):
    na = len(shards)

    def body(*refs):
        x_refs, out_refs = refs[:na], refs[na:2 * na]
        send_sems, recv_sems, local_sems = refs[2 * na:]
        x, y, c = _me()
        me, sibling = (x, y, c), (x, y, 1 - c)
        chips = [(1 - x, y), (x, 1 - y), (1 - x, 1 - y)]

        def rows(a, px, py, pc):
            return out_refs[a].at[4 * px + 2 * py + pc]

        def copy(a, k, block, to, src=None):
            return pltpu.make_async_remote_copy(
                src_ref=rows(a, *block) if src is None else src, dst_ref=rows(a, *block),
                send_sem=send_sems.at[7 * a + k], recv_sem=recv_sems.at[7 * a + k], device_id=to, device_id_type=MESH)

        mine, first, passed = [], [], []
        for a in range(na):
            mine.append(pltpu.make_async_copy(x_refs[a], rows(a, *me), local_sems.at[a]))
            mine[a].start()
            first.append([copy(a, 0, me, sibling, src=x_refs[a])]
                         + [copy(a, 1 + j, me, (*chip, c), src=x_refs[a]) for j, chip in enumerate(chips)])
            for cp in first[a]:
                cp.start()
            passed.append([copy(a, 4 + j, (*chip, c), sibling) for j, chip in enumerate(chips)])
        for a in range(na):
            for j, chip in enumerate(chips):
                copy(a, 1 + j, (*chip, c), me).wait_recv()
                passed[a][j].start()
        for a in range(na):
            copy(a, 0, sibling, me).wait_recv()
            for j, chip in enumerate(chips):
                copy(a, 4 + j, (*chip, 1 - c), me).wait_recv()
            for cp in first[a] + passed[a]:
                cp.wait_send()
            mine[a].wait()

    return pl.pallas_call(
        body, in_specs=[ANY] * na, out_specs=[ANY] * na,
        out_shape=[S((N_DEV,) + s_.shape, s_.dtype) for s_ in shards],
        scratch_shapes=[pltpu.SemaphoreType.DMA((7 * na,)), pltpu.SemaphoreType.DMA((7 * na,)),
                        pltpu.SemaphoreType.DMA((na,))],
        name=name)(*shards)


def exchange_sibling(gs, name):
    na = len(gs)

    def body(*refs):
        g_refs, out_refs = refs[:na], refs[na:2 * na]
        send_sems, recv_sems = refs[2 * na:]
        x, y, c = _me()
        copies = []
        for a in range(na):
            cp = pltpu.make_async_remote_copy(src_ref=g_refs[a].at[:, 1 - c], dst_ref=out_refs[a],
                                              send_sem=send_sems.at[a], recv_sem=recv_sems.at[a],
                                              device_id=(x, y, 1 - c), device_id_type=MESH)
            cp.start()
            copies.append(cp)
        for cp in copies:
            cp.wait()

    return pl.pallas_call(
        body, in_specs=[ANY] * na, out_specs=[ANY] * na,
        out_shape=[S((4,) + g.shape[2:], g.dtype) for g in gs],
        scratch_shapes=[pltpu.SemaphoreType.DMA((na,)), pltpu.SemaphoreType.DMA((na,))], name=name)(*gs)


def exchange_chips(ps, name):
    na = len(ps)

    def body(*refs):
        p_refs, out_refs = refs[:na], refs[na:2 * na]
        send_sems, recv_sems, local_sems = refs[2 * na:]
        x, y, c = _me()
        mine = 2 * x + y
        chips = [(1 - x, y), (x, 1 - y), (1 - x, 1 - y)]
        copies, locs = [], []
        for a in range(na):
            loc = pltpu.make_async_copy(p_refs[a].at[mine], out_refs[a].at[mine], local_sems.at[a])
            loc.start()
            locs.append(loc)
            for j, (px, py) in enumerate(chips):
                cp = pltpu.make_async_remote_copy(src_ref=p_refs[a].at[2 * px + py], dst_ref=out_refs[a].at[mine],
                                                  send_sem=send_sems.at[3 * a + j], recv_sem=recv_sems.at[3 * a + j],
                                                  device_id=(px, py, c), device_id_type=MESH)
                cp.start()
                copies.append(cp)
        for a in range(na):
            for j, (px, py) in enumerate(chips):
                pltpu.make_async_remote_copy(src_ref=p_refs[a].at[mine], dst_ref=out_refs[a].at[2 * px + py],
                                             send_sem=send_sems.at[3 * a + j], recv_sem=recv_sems.at[3 * a + j],
                                             device_id=(px, py, c), device_id_type=MESH).wait_recv()
        for cp in copies:
            cp.wait_send()
        for loc in locs:
            loc.wait()

    return pl.pallas_call(
        body, in_specs=[ANY] * na, out_specs=[ANY] * na, out_shape=[S(p.shape, p.dtype) for p in ps],
        scratch_shapes=[pltpu.SemaphoreType.DMA((3 * na,)), pltpu.SemaphoreType.DMA((3 * na,)),
                        pltpu.SemaphoreType.DMA((na,))],
        name=name)(*ps)


def add_own(g, r, core, name):
    _, _, R, W = g.shape
    rb = R if R <= 512 else max(r_ for r_ in range(16, 513, 16) if R % r_ == 0)

    def body(core_ref, a_ref, b_ref, o_ref):
        o_ref[...] = (a_ref[...].astype(f32) + b_ref[...].astype(f32)).astype(bf16)

    blk = pl.BlockSpec((None, rb, W), lambda k, i, core_ref: (k, i, 0))
    gs = pltpu.PrefetchScalarGridSpec(
        num_scalar_prefetch=1, grid=(4, R // rb),
        in_specs=[pl.BlockSpec((None, None, rb, W), lambda k, i, core_ref: (k, core_ref[0], i, 0)), blk], out_specs=blk)
    return pl.pallas_call(body, grid_spec=gs, out_shape=S((4, R, W), bf16),
                          compiler_params=_cparams("parallel", "parallel"), name=name)(core, g, r)


def sum_adamw(recv, w, m, v, layer, name):
    _, R, W = recv.shape
    rb = R if R <= 256 else max(r_ for r_ in range(16, 257, 16) if R % r_ == 0)
    bc1 = 1.0 - ADAM_B1 ** ADAM_STEP
    bc2 = 1.0 - ADAM_B2 ** ADAM_STEP

    def body(r_ref, w_ref, m_ref, v_ref, g_ref, d_ref, nm_ref, nv_ref):
        gv = r_ref[0].astype(f32)
        for k in range(1, 4):
            gv = gv + r_ref[k].astype(f32)
        m_new = ADAM_B1 * m_ref[...] + (1.0 - ADAM_B1) * gv
        v_new = ADAM_B2 * v_ref[...] + (1.0 - ADAM_B2) * (gv * gv)
        g_ref[...] = gv
        d_ref[...] = -ADAM_LR * ((m_new / bc1) / (jnp.sqrt(v_new / bc2) + ADAM_EPS) + ADAM_WD * w_ref[...])
        nm_ref[...] = m_new
        nv_ref[...] = v_new

    blk = pl.BlockSpec((rb, W), lambda i: (i, 0))
    wblk = blk if layer is None else pl.BlockSpec((None, rb, W), lambda i: (layer, i, 0))
    return pl.pallas_call(body, grid=(R // rb,), in_specs=[pl.BlockSpec((4, rb, W), lambda i: (0, i, 0)), wblk, wblk, wblk],
                          out_specs=[blk] * 4, out_shape=[S((R, W), f32)] * 4,
                          compiler_params=_cparams("parallel"), name=name)(recv, w, m, v)


def sum_rows(a, name):
    K, R, W = a.shape
    rb = _pick(R, (512, 256, 128, 64, 32, 16))

    def body(a_ref, o_ref):
        acc = a_ref[0].astype(f32)
        for k in range(1, K):
            acc = acc + a_ref[k].astype(f32)
        o_ref[...] = acc

    return pl.pallas_call(body, grid=(R // rb,), in_specs=[pl.BlockSpec((K, rb, W), lambda i: (0, i, 0))],
                          out_specs=pl.BlockSpec((rb, W), lambda i: (i, 0)), out_shape=S((R, W), f32),
                          compiler_params=_cparams("parallel"), name=name)(a)


PACK_ALIGN = 16 * PACK_W


def _pad_to(v, mult):
    n = v.shape[-1]
    extra = (-n) % mult
    if extra == 0:
        return v
    return jnp.concatenate([v, jnp.zeros(v.shape[:-1] + (extra,), v.dtype)], axis=-1)


def _f32_as_bf16_pairs(v):
    return lax.bitcast_convert_type(v.reshape(-1), bf16).reshape(-1)


def _bf16_pairs_as_f32(v):
    return lax.bitcast_convert_type(v.reshape(v.shape[:-1] + (v.shape[-1] // 2, 2)), f32)


def _col_shards(gw):
    lead = gw.shape[:-1]
    n = gw.shape[-1] // N_DEV
    t = gw.reshape(lead + (N_DEV, n))
    t = jnp.moveaxis(t, -2, 0)
    return t.reshape(N_DEV, -1)


def kernel(x, c, ctx, c_ctx, mod_w, mod_b, norm1_w, norm2_w, ssd_w_in, ssd_conv_w, ssd_conv_b, ssd_dt_bias, ssd_a_log, ssd_d, ssd_norm_w, ssd_w_out, conf_w_pw1, conf_b_pw1, conf_w_dw, conf_b_dw, conf_ln_w, conf_ln_b, conf_w_pw2, conf_b_pw2, ffn_w_up, ffn_conv_w, ffn_conv_b, ffn_w_down, final_norm_w, loss_target, m_c_ctx, m_mod_w, m_mod_b, m_norm1_w, m_norm2_w, m_ssd_w_in, m_ssd_conv_w, m_ssd_conv_b, m_ssd_dt_bias, m_ssd_a_log, m_ssd_d, m_ssd_norm_w, m_ssd_w_out, m_conf_w_pw1, m_conf_b_pw1, m_conf_w_dw, m_conf_b_dw, m_conf_ln_w, m_conf_ln_b, m_conf_w_pw2, m_conf_b_pw2, m_ffn_w_up, m_ffn_conv_w, m_ffn_conv_b, m_ffn_w_down, m_final_norm_w, v_c_ctx, v_mod_w, v_mod_b, v_norm1_w, v_norm2_w, v_ssd_w_in, v_ssd_conv_w, v_ssd_conv_b, v_ssd_dt_bias, v_ssd_a_log, v_ssd_d, v_ssd_norm_w, v_ssd_w_out, v_conf_w_pw1, v_conf_b_pw1, v_conf_w_dw, v_conf_b_dw, v_conf_ln_w, v_conf_ln_b, v_conf_w_pw2, v_conf_b_pw2, v_ffn_w_up, v_ffn_conv_w, v_ffn_conv_b, v_ffn_w_down, v_final_norm_w):
    mx, my, mc = _me()
    me = 4 * mx + 2 * my + mc
    L = x.shape[1]
    LC = ctx.shape[1]
    T = LC + L
    w_in_cols = ssd_w_in.shape[2] * N_DEV
    n_dt = w_in_cols - DI - CONVD

    small = [c[0], ssd_conv_w[0], conf_b_pw1[0], conf_w_dw[0], conf_b_dw[0], conf_ln_w[0], conf_ln_b[0], conf_b_pw2[0],
             ffn_conv_w]
    parts = [_f32_as_bf16_pairs(t) for t in small]
    sizes = [p.shape[0] for p in parts]
    small_flat = _pad_to(jnp.concatenate(parts), PACK_ALIGN).reshape(-1, PACK_W)
    w_in, w_up, w_pw1, w_out_g, w_down_g, w_pw2_g, small_g = allgather_big(
        [ssd_w_in[0].astype(bf16), ffn_w_up.astype(bf16), conf_w_pw1[0].astype(bf16), ssd_w_out[0].astype(bf16),
         ffn_w_down.astype(bf16), conf_w_pw2[0].astype(bf16), small_flat], "gather_weights")
    w_out = w_out_g.reshape(DI, D)
    w_pw2 = w_pw2_g.reshape(D, D)
    w_down = [w_down_g[:, i].reshape(FH, D) for i in range(2)]
    small_g = small_g.reshape(N_DEV, -1)
    offs = [0]
    for s_ in sizes:
        offs.append(offs[-1] + s_)
    sm = [_bf16_pairs_as_f32(small_g[:, offs[i]:offs[i + 1]]) for i in range(len(sizes))]

    def cols(pc, K):
        return jnp.moveaxis(pc.reshape(N_DEV, K, -1), 0, 1).reshape(K, -1)

    c_all = sm[0]
    conv_w5 = cols(sm[1], 5)
    b_pw1 = sm[2].reshape(1, 2 * D)
    w_dw = cols(sm[3], CONF_K)
    b_dw, ln_w, ln_b, b_pw2 = (sm[i].reshape(1, D) for i in (4, 5, 6, 7))
    fcw = sm[8].reshape(N_DEV, 2, 9, FH // N_DEV)
    ffn_cw = [cols(fcw[:, i].reshape(N_DEV, -1), 9) for i in range(2)]
    in_segs = (DI, CONVD, n_dt)
    up_segs = (FH, FH)
    pw1_segs = (D, D)

    c16 = jnp.concatenate([c_all, jnp.broadcast_to(c_ctx[None, :], (N_DEV, D))], axis=0)
    m_sh = mod_fwd(c16, mod_w, "mod_fwd")
    mod_cols = mod_w.shape[2]
    m_all = allgather_small(m_sh.reshape(2 * CROWS, mod_cols), "gather_mod")
    m_all = jnp.moveaxis(m_all.reshape(N_DEV, 2, CROWS, mod_cols), 0, 2).reshape(2, CROWS, 6 * D) + mod_b[:, None, :]
    m_lat = lax.dynamic_index_in_dim(m_all, me, axis=1, keepdims=False).reshape(2, 6, 1, D)
    m_ctx = m_all[:, N_DEV].reshape(2, 6, 1, D)
    zero_row = jnp.zeros((1, D), f32)

    def ffn_fwd(h, i, tag):
        a2 = modnorm_fwd(h, norm2_w[i][None], m_lat[i, 4][None], m_lat[i, 3][None], 0, f"ffn{tag}_norm")
        val, gate = smm_fwd(a2, w_up, i, up_segs, f"ffn{tag}_up")
        act = ffn_gate_fwd(val, gate, ffn_cw[i], ffn_conv_b[i][None], f"ffn{tag}_gate")
        o2 = matmul(act, w_down[i], "nn", f32, f"ffn{tag}_down")
        h_new = resgate_fwd(h, o2, m_lat[i, 5], zero_row, f"ffn{tag}_res")
        return h_new, (a2, val, gate, act, o2)

    def ffn_bwd(dh, h, i, saved, tag):
        a2, val, gate, act, o2 = saved
        do2, dg2, _ = resgate_bwd(dh, o2, m_lat[i, 5], zero_row, f"ffn{tag}_res_bwd")
        g_down = matmul(act, do2, "tn", bf16, f"ffn{tag}_down_dw")
        dact = matmul(do2, w_down[i], "nt", bf16, f"ffn{tag}_down_dx")
        dval, dgate, dcw, dcb = ffn_gate_bwd(val, gate, ffn_cw[i], ffn_conv_b[i][None], dact, f"ffn{tag}_gate_bwd")
        g_up = smm_dw(a2, [dval, dgate], FH // 4, up_segs, 2, f"ffn{tag}_up_dw")
        da2 = smm_dx([dval, dgate], w_up, i, up_segs, bf16, f"ffn{tag}_up_dx")
        dh_in, dn2, dsc2, dsh2 = modnorm_bwd(h, norm2_w[i][None], m_lat[i, 4][None], m_lat[i, 3][None], da2, dh, 0,
                                             f"ffn{tag}_norm_bwd")
        return dh_in, dict(w_up=g_up, w_down=g_down, conv_w=dcw, conv_b=dcb, norm2=dn2, sh2=dsh2[0], sc2=dsc2[0], g2=dg2)

    nctx = LC // Q
    h0 = jnp.concatenate([ctx[0], x[0]], axis=0)
    sc0 = jnp.stack([m_ctx[0, 1], m_lat[0, 1]])
    sh0 = jnp.stack([m_ctx[0, 0], m_lat[0, 0]])
    a0 = modnorm_fwd(h0, norm1_w[0][None], sc0, sh0, LC // TB, "ssd_norm")
    z, xbc_pre, dt_raw = smm_fwd(a0, w_in, None, in_segs, "ssd_in")
    segs = ((0, LC), (LC, L))
    xbc = ssd_conv_fwd(xbc_pre, conv_w5, ssd_conv_b, segs, "ssd_conv")
    dt4 = dt_raw[:, :n_dt].reshape(T, 2, G, HPG)
    dtc = jnp.transpose(dt4, (1, 2, 0, 3))
    dtr = jnp.transpose(dt4, (1, 2, 3, 0))
    bias3 = ssd_dt_bias[0].reshape(2, G, HPG)
    alog3 = ssd_a_log[0].reshape(2, G, HPG)
    bc_, br_ = bias3[:, :, None, :], bias3[:, :, :, None]
    alc, alr = alog3[:, :, None, :], alog3[:, :, :, None]
    y2, s_in_all = ssd_scan_fwd(xbc, dtc, dtr, bc_, br_, alc, alr, nctx, "ssd_scan")
    dexp = jnp.repeat(ssd_d[0], P)[None, :]
    yn = ssd_gate_fwd(y2, xbc, z, dexp, ssd_norm_w, LC // GTB, "ssd_gate")
    o_ssd = matmul(yn, w_out, "nn", f32, "ssd_out")
    hx = x[0]
    h1 = resgate_fwd(hx, o_ssd, m_lat[0, 2], zero_row, "ssd_res")
    h2, ffn0_saved = ffn_fwd(h1, 0, "0")

    a1 = modnorm_fwd(h2, norm1_w[1][None], m_lat[1, 1][None], m_lat[1, 0][None], 0, "conf_norm")
    pa, pg = smm_fwd(a1, w_pw1, None, pw1_segs, "conf_pw1")
    dwc = conf_glu_conv_fwd(pa, pg, b_pw1, w_dw, b_dw, "conf_conv")
    s1 = ln_silu_fwd(dwc, ln_w, ln_b, "conf_ln")
    o_conf = matmul(s1, w_pw2, "nn", f32, "conf_pw2")
    h3 = resgate_fwd(h2, o_conf, m_lat[1, 2], b_pw2, "conf_res")
    h4, ffn1_saved = ffn_fwd(h3, 1, "1")

    loss_part, dh4, g_final = final_loss(h4, final_norm_w[None], loss_target[0], "loss_head")
    dh3, gf1 = ffn_bwd(dh4, h3, 1, ffn1_saved, "1")

    do_conf, dg1_1, g_b_pw2 = resgate_bwd(dh3, o_conf, m_lat[1, 2], b_pw2, "conf_res_bwd")
    g_pw2 = matmul(s1, do_conf, "tn", bf16, "conf_pw2_dw")
    ds1 = matmul(do_conf, w_pw2, "nt", bf16, "conf_pw2_dx")
    ddwc, g_ln_w, g_ln_b = ln_silu_bwd(dwc, ln_w, ln_b, ds1, "conf_ln_bwd")
    dpa, dpg, dba, dbg, g_w_dw, g_b_dw = conf_glu_conv_bwd(pa, pg, b_pw1, w_dw, ddwc, "conf_conv_bwd")
    g_b_pw1 = jnp.concatenate([dba, dbg], axis=1)
    g_pw1 = smm_dw(a1, [dpa, dpg], 2 * D // N_DEV, pw1_segs, 1, "conf_pw1_dw")
    da1 = smm_dx([dpa, dpg], w_pw1, None, pw1_segs, bf16, "conf_pw1_dx")
    dh2, g_n1_1, dsc1_1, dsh1_1 = modnorm_bwd(h2, norm1_w[1][None], m_lat[1, 1][None], m_lat[1, 0][None], da1, dh3, 0,
                                              "conf_norm_bwd")
    dh1, gf0 = ffn_bwd(dh2, h1, 0, ffn0_saved, "0")

    do_ssd, dg1_0, _ = resgate_bwd(dh1, o_ssd, m_lat[0, 2], zero_row, "ssd_res_bwd")
    g_w_out = matmul(yn, do_ssd, "tn", bf16, "ssd_out_dw")
    dyn = matmul(do_ssd, w_out, "nt", bf16, "ssd_out_dx")
    dy, dx_skip, dz, g_dexp, g_ssd_norm = ssd_gate_bwd(y2, xbc, z, dexp, ssd_norm_w, dyn, LC // GTB, "ssd_gate_bwd")
    dxbc2, ddtc, ddtr, dbc, dbr, dalc, dalr = ssd_scan_bwd(xbc, dtc, dtr, bc_, br_, alc, alr, s_in_all, dy, nctx,
                                                           "ssd_scan_bwd")
    ddt = (jnp.transpose(ddtc, (2, 0, 1, 3)) + jnp.transpose(ddtr, (3, 0, 1, 2))).reshape(T, n_dt)
    g_dt_bias = (dbc[:, :, 0, :] + dbr[:, :, :, 0]).reshape(2, NH_SSD)
    g_a_log = (dalc[:, :, 0, :] + dalr[:, :, :, 0]).reshape(2, NH_SSD)
    g_ssd_d = g_dexp[0, :NH_SSD]
    du, g_conv_w5, g_conv_b5 = ssd_conv_bwd(xbc_pre, conv_w5, ssd_conv_b, dxbc2, dx_skip, segs, "ssd_conv_bwd")
    ddt_p = _pad_to(ddt, 128).astype(bf16)
    g_w_in = smm_dw(a0, [dz, du, ddt_p], w_in.shape[-1], in_segs, 2, "ssd_in_dw")
    da0 = smm_dx([dz, du, ddt_p], w_in, None, in_segs, f32, "ssd_in_dx")
    dres0 = jnp.concatenate([jnp.zeros((LC, D), f32), dh1], axis=0)
    dh0, g_n1_0, dsc1_0, dsh1_0 = modnorm_bwd(h0, norm1_w[0][None], sc0, sh0, da0, dres0, LC // TB, "ssd_norm_bwd")
    grad_x = dh0[LC:][None]

    zeros_d = jnp.zeros((1, D), f32)
    dm_lat = jnp.stack([
        jnp.concatenate([dsh1_0[1], dsc1_0[1], dg1_0, gf0["sh2"], gf0["sc2"], gf0["g2"]], axis=1),
        jnp.concatenate([dsh1_1[0], dsc1_1[0], dg1_1, gf1["sh2"], gf1["sc2"], gf1["g2"]], axis=1)])
    dm_ctx = jnp.stack([
        jnp.concatenate([dsh1_0[0], dsc1_0[0]] + [zeros_d] * 4, axis=1), jnp.zeros((1, 6 * D), f32)])
    dm_mine = jnp.concatenate([dm_lat.reshape(2, 6 * D), dm_ctx.reshape(2, 6 * D),
                               jnp.zeros((4, 6 * D), f32)], axis=0)
    dm_g = allgather_small(dm_mine, "gather_dmod")
    dm_all = jnp.concatenate([jnp.moveaxis(dm_g[:, 0:2], 0, 1), jnp.moveaxis(dm_g[:, 2:4], 0, 1)], axis=1)
    dm_sh = lax.dynamic_slice_in_dim(dm_all, me * mod_cols, mod_cols, axis=2)
    g_mod_w, g_cctx_part, g_mod_b = mod_bwd(c16, mod_w, dm_sh, dm_all, "mod_bwd")

    rep = [jnp.stack([g_n1_0[0], g_n1_1[0]]), jnp.stack([gf0["norm2"][0], gf1["norm2"][0]]), g_conv_b5, g_dt_bias, g_a_log,
           g_ssd_d, g_ssd_norm, jnp.stack([gf0["conv_b"][0], gf1["conv_b"][0]]), g_final, g_cctx_part, loss_part[:, :1]]
    rep_sizes = [r_.size for r_ in rep]
    rep_flat = _pad_to(jnp.concatenate([r_.reshape(-1) for r_ in rep]), 8 * PACK_W).reshape(-1, PACK_W)
    _, rep_sum = allgather_small(rep_flat, "reduce_replicated", with_sum=True)
    rep_sum = rep_sum.reshape(-1)
    roffs = [0]
    for s_ in rep_sizes:
        roffs.append(roffs[-1] + s_)
    rp = [rep_sum[roffs[i]:roffs[i + 1]] for i in range(len(rep_sizes))]
    loss = rp[10].reshape(())

    g_ffn_cw = jnp.stack([gf0["conv_w"], gf1["conv_w"]])
    small_shards = [_col_shards(t) for t in (g_conv_w5, g_b_pw1, g_w_dw, g_b_dw, g_ln_w, g_ln_b, g_b_pw2, g_ffn_cw)]
    gsizes = [s_.shape[1] for s_ in small_shards]
    g_small = _pad_to(jnp.concatenate(small_shards, axis=1), PACK_ALIGN).astype(bf16)
    to_reduce = [g_w_in, gf0["w_up"], gf1["w_up"], g_pw1, g_w_out, gf0["w_down"], gf1["w_down"], g_pw2,
                 g_small.reshape(N_DEV, -1, PACK_W)]
    to_reduce = [t.reshape((4, 2, -1, t.shape[-1])) for t in to_reduce]
    core = mc.reshape(1).astype(jnp.int32)
    from_sibling = exchange_sibling(to_reduce, "reduce_sibling")
    chip_part = [add_own(t, r_, core, f"reduce_add{i}") for i, (t, r_) in enumerate(zip(to_reduce, from_sibling))]
    from_chips = exchange_chips(chip_part, "reduce_chips")
    g_flat = sum_rows(from_chips[8], "reduce_sum_small").reshape(-1)
    goffs = [0]
    for s_ in gsizes:
        goffs.append(goffs[-1] + s_)
    gs = [g_flat[goffs[i]:goffs[i + 1]] for i in range(len(gsizes))]

    big = {}
    big["ssd_w_in"] = sum_adamw(from_chips[0], ssd_w_in[0], m_ssd_w_in[0], v_ssd_w_in[0], None, "adamw_ssd_w_in")
    up = [sum_adamw(from_chips[1 + i], ffn_w_up, m_ffn_w_up, v_ffn_w_up, i, f"adamw_ffn_w_up{i}") for i in range(2)]
    big["ffn_w_up"] = tuple(jnp.stack([up[0][k], up[1][k]]) for k in range(4))
    big["conf_w_pw1"] = sum_adamw(from_chips[3], conf_w_pw1[0], m_conf_w_pw1[0], v_conf_w_pw1[0], None, "adamw_conf_w_pw1")
    big["ssd_w_out"] = sum_adamw(from_chips[4], ssd_w_out[0], m_ssd_w_out[0], v_ssd_w_out[0], None, "adamw_ssd_w_out")
    dn = [sum_adamw(from_chips[5 + i], ffn_w_down, m_ffn_w_down, v_ffn_w_down, i, f"adamw_ffn_w_down{i}") for i in range(2)]
    big["ffn_w_down"] = tuple(jnp.stack([dn[0][k], dn[1][k]]) for k in range(4))
    big["conf_w_pw2"] = sum_adamw(from_chips[7], conf_w_pw2[0], m_conf_w_pw2[0], v_conf_w_pw2[0], None, "adamw_conf_w_pw2")
    grads = {
        "c_ctx": rp[9], "mod_w": g_mod_w, "mod_b": g_mod_b, "norm1_w": rp[0], "norm2_w": rp[1],
        "ssd_conv_w": gs[0], "ssd_conv_b": rp[2], "ssd_dt_bias": rp[3], "ssd_a_log": rp[4], "ssd_d": rp[5],
        "ssd_norm_w": rp[6], "conf_b_pw1": gs[1], "conf_w_dw": gs[2],
        "conf_b_dw": gs[3], "conf_ln_w": gs[4], "conf_ln_b": gs[5], "conf_b_pw2": gs[6],
        "ffn_conv_w": gs[7], "ffn_conv_b": rp[7], "final_norm_w": rp[8],
    }
    weights = dict(c_ctx=c_ctx, mod_w=mod_w, mod_b=mod_b, norm1_w=norm1_w, norm2_w=norm2_w, ssd_w_in=ssd_w_in, ssd_conv_w=ssd_conv_w, ssd_conv_b=ssd_conv_b, ssd_dt_bias=ssd_dt_bias, ssd_a_log=ssd_a_log, ssd_d=ssd_d, ssd_norm_w=ssd_norm_w, ssd_w_out=ssd_w_out, conf_w_pw1=conf_w_pw1, conf_b_pw1=conf_b_pw1, conf_w_dw=conf_w_dw, conf_b_dw=conf_b_dw, conf_ln_w=conf_ln_w, conf_ln_b=conf_ln_b, conf_w_pw2=conf_w_pw2, conf_b_pw2=conf_b_pw2, ffn_w_up=ffn_w_up, ffn_conv_w=ffn_conv_w, ffn_conv_b=ffn_conv_b, ffn_w_down=ffn_w_down, final_norm_w=final_norm_w)
    m_in = dict(c_ctx=m_c_ctx, mod_w=m_mod_w, mod_b=m_mod_b, norm1_w=m_norm1_w, norm2_w=m_norm2_w, ssd_w_in=m_ssd_w_in, ssd_conv_w=m_ssd_conv_w, ssd_conv_b=m_ssd_conv_b, ssd_dt_bias=m_ssd_dt_bias, ssd_a_log=m_ssd_a_log, ssd_d=m_ssd_d, ssd_norm_w=m_ssd_norm_w, ssd_w_out=m_ssd_w_out, conf_w_pw1=m_conf_w_pw1, conf_b_pw1=m_conf_b_pw1, conf_w_dw=m_conf_w_dw, conf_b_dw=m_conf_b_dw, conf_ln_w=m_conf_ln_w, conf_ln_b=m_conf_ln_b, conf_w_pw2=m_conf_w_pw2, conf_b_pw2=m_conf_b_pw2, ffn_w_up=m_ffn_w_up, ffn_conv_w=m_ffn_conv_w, ffn_conv_b=m_ffn_conv_b, ffn_w_down=m_ffn_w_down, final_norm_w=m_final_norm_w)
    v_in = dict(c_ctx=v_c_ctx, mod_w=v_mod_w, mod_b=v_mod_b, norm1_w=v_norm1_w, norm2_w=v_norm2_w, ssd_w_in=v_ssd_w_in, ssd_conv_w=v_ssd_conv_w, ssd_conv_b=v_ssd_conv_b, ssd_dt_bias=v_ssd_dt_bias, ssd_a_log=v_ssd_a_log, ssd_d=v_ssd_d, ssd_norm_w=v_ssd_norm_w, ssd_w_out=v_ssd_w_out, conf_w_pw1=v_conf_w_pw1, conf_b_pw1=v_conf_b_pw1, conf_w_dw=v_conf_w_dw, conf_b_dw=v_conf_b_dw, conf_ln_w=v_conf_ln_w, conf_ln_b=v_conf_ln_b, conf_w_pw2=v_conf_w_pw2, conf_b_pw2=v_conf_b_pw2, ffn_w_up=v_ffn_w_up, ffn_conv_w=v_ffn_conv_w, ffn_conv_b=v_ffn_conv_b, ffn_w_down=v_ffn_w_down, final_norm_w=v_final_norm_w)

    out_g, out_d, out_m, out_v = [], [], [], []
    for name_, w_ in weights.items():
        shape = w_.shape
        if name_ in big:
            for lst, t in zip((out_g, out_d, out_m, out_v), big[name_]):
                lst.append(t.reshape(shape))
            continue
        cols2 = shape[-1] if len(shape) > 1 else shape[0]
        g2 = grads[name_].reshape(-1, cols2)
        d_, nm_, nv_ = adamw(w_.reshape(-1, cols2), g2, m_in[name_].reshape(-1, cols2), v_in[name_].reshape(-1, cols2),
                             f"adamw_{name_}")
        out_g.append(g2.reshape(shape))
        out_d.append(d_.reshape(shape))
        out_m.append(nm_.reshape(shape))
        out_v.append(nv_.reshape(shape))
    return (loss, grad_x, *out_g, *out_d, *out_m, *out_v)
```

```python
import functools

import jax
import jax.numpy as jnp
from jax import lax
from jax.experimental import pallas as pl
from jax.experimental.pallas import tpu as pltpu

f32 = jnp.float32
bf16 = jnp.bfloat16
HI = lax.Precision.HIGHEST
S = jax.ShapeDtypeStruct
MESH = pl.DeviceIdType.MESH
ANY = pl.BlockSpec(memory_space=pl.ANY)
VMEM = pl.BlockSpec(memory_space=pltpu.VMEM)

N_DEV = 8
D = 1024
DI = 2048
CONVD = 4096
FH = 2816
GRID_W = 64
Q = 128
HPG = 4
P = 64
N = 128
G = 8
GW = HPG * P
NH_SSD = G * HPG
EPS = 1e-6
ADAM_LR, ADAM_B1, ADAM_B2, ADAM_EPS, ADAM_WD, ADAM_STEP = 0.001, 0.9, 0.999, 1e-08, 0.01, 10
VMEM_LIMIT_BYTES = 56 * 1024 * 1024
PACK_W = 1024
TB = 256


def _cparams(*sem):
    return pltpu.CompilerParams(dimension_semantics=sem, vmem_limit_bytes=VMEM_LIMIT_BYTES)


def _pick(n, prefs):
    for p in prefs:
        if n % p == 0:
            return p
    return n


def _sigmoid(x):
    return 1.0 / (1.0 + jnp.exp(-x))


def _softplus(x):
    return jnp.maximum(x, 0.0) + jnp.log(1.0 + jnp.exp(-jnp.abs(x)))


def matmul(a, b, mode, out_dtype, name):
    if mode == "nn":
        (M, K), (_, Nn) = a.shape, b.shape
        bm, bn, bk = _pick(M, (512, 384, 256, 128)), Nn, K
    elif mode == "tn":
        (K, M), (_, Nn) = a.shape, b.shape
        bm, bn, bk = M, Nn, _pick(K, (256, 128))
    else:
        (M, K), (Nn, _) = a.shape, b.shape
        bm, bn, bk = _pick(M, (512, 384, 256, 128)), Nn, K
    nk = K // bk
    dims = {"nn": (((1,), (0,)), ((), ())), "tn": (((0,), (0,)), ((), ())), "nt": (((1,), (1,)), ((), ()))}[mode]

    def body(a_ref, b_ref, o_ref, acc_ref):
        k = pl.program_id(2)

        @pl.when(k == 0)
        def _():
            acc_ref[...] = jnp.zeros_like(acc_ref)

        acc_ref[...] += lax.dot_general(a_ref[...].astype(bf16), b_ref[...].astype(bf16), dims,
                                        preferred_element_type=f32)

        @pl.when(k == nk - 1)
        def _():
            o_ref[...] = acc_ref[...].astype(out_dtype)

    if mode == "nn":
        a_spec = pl.BlockSpec((bm, bk), lambda i, j, k: (i, k))
        b_spec = pl.BlockSpec((bk, bn), lambda i, j, k: (k, j))
    elif mode == "tn":
        a_spec = pl.BlockSpec((bk, bm), lambda i, j, k: (k, i))
        b_spec = pl.BlockSpec((bk, bn), lambda i, j, k: (k, j))
    else:
        a_spec = pl.BlockSpec((bm, bk), lambda i, j, k: (i, k))
        b_spec = pl.BlockSpec((bn, bk), lambda i, j, k: (j, k))
    return pl.pallas_call(
        body, grid=(M // bm, Nn // bn, nk), in_specs=[a_spec, b_spec],
        out_specs=pl.BlockSpec((bm, bn), lambda i, j, k: (i, j)),
        out_shape=S((M, Nn), out_dtype), scratch_shapes=[pltpu.VMEM((bm, bn), f32)],
        compiler_params=_cparams("parallel", "parallel", "arbitrary"), name=name,
    )(a, b)


SMM_BM = 256


def _shard_pieces(seg_widths, n):
    bounds = [0]
    for sw in seg_widths:
        bounds.append(bounds[-1] + sw)
    assert bounds[-1] == N_DEV * n, (seg_widths, n)
    out = []
    for j in range(N_DEV):
        lo, hi = j * n, (j + 1) * n
        pcs = []
        for si in range(len(seg_widths)):
            a, b = max(lo, bounds[si]), min(hi, bounds[si + 1])
            if a < b:
                pcs.append((si, a - bounds[si], a - lo, b - a))
        out.append(pcs)
    return out


def _w_spec(w, layer):
    if layer is None:
        return pl.BlockSpec(w.shape, lambda *idx: (0, 0, 0))
    return pl.BlockSpec((N_DEV, None) + w.shape[2:], lambda *idx: (0, layer, 0, 0))


def smm_fwd(a, w, layer, seg_widths, name):
    M, K = a.shape
    n = w.shape[-1]
    pieces = _shard_pieces(seg_widths, n)
    padded = [sw + (-sw) % 128 for sw in seg_widths]

    def body(a_ref, w_ref, *o_refs):
        av = a_ref[...]
        for si, sw in enumerate(seg_widths):
            if padded[si] != sw:
                o_refs[si][:, pl.ds(padded[si] - 128, 128)] = jnp.zeros((SMM_BM, 128), f32)
        for j in range(N_DEV):
            for si, soff, woff, wd in pieces[j]:
                o_refs[si][:, pl.ds(soff, wd)] = jnp.dot(av, w_ref[j, :, pl.ds(woff, wd)], preferred_element_type=f32)

    return pl.pallas_call(
        body, grid=(M // SMM_BM,), in_specs=[pl.BlockSpec((SMM_BM, K), lambda i: (i, 0)), _w_spec(w, layer)],
        out_specs=[pl.BlockSpec((SMM_BM, pw), lambda i: (i, 0)) for pw in padded],
        out_shape=[S((M, pw), f32) for pw in padded], compiler_params=_cparams("parallel"), name=name)(a, w)


def smm_dx(d_segs, w, layer, seg_widths, out_dtype, name, hosted=None):
    M = d_segs[0].shape[0]
    K, n = w.shape[-2], w.shape[-1]
    pieces = _shard_pieces(seg_widths, n)
    ns = len(d_segs)

    def body(*refs):
        d_refs, w_ref, o_ref = refs[:ns], refs[ns], refs[ns + 1]
        acc = jnp.zeros((SMM_BM, K), f32)
        for j in range(N_DEV):
            for si, soff, woff, wd in pieces[j]:
                acc = acc + lax.dot_general(d_refs[si][:, pl.ds(soff, wd)], w_ref[j, :, pl.ds(woff, wd)],
                                            (((1,), (1,)), ((), ())), preferred_element_type=f32)
        o_ref[...] = acc.astype(out_dtype)

    (out,), extra = _host_call(
        body, (M // SMM_BM,),
        [pl.BlockSpec((SMM_BM, d.shape[1]), lambda i: (i, 0)) for d in d_segs] + [_w_spec(w, layer)],
        [pl.BlockSpec((SMM_BM, K), lambda i: (i, 0))], [S((M, K), out_dtype)], [], ("parallel",), name,
        (*d_segs, w), hosted)
    return out, extra


def smm_dw(a, d_segs, n, seg_widths, ngrp, name):
    M, K = a.shape
    pieces = _shard_pieces(seg_widths, n)
    per = N_DEV // ngrp
    nI = M // SMM_BM
    ns = len(d_segs)

    def body(*refs):
        a_ref, d_refs, o_ref, acc_ref = refs[0], refs[1:1 + ns], refs[1 + ns], refs[2 + ns]
        grp = pl.program_id(0)
        i = pl.program_id(1)

        @pl.when(i == 0)
        def _():
            acc_ref[...] = jnp.zeros_like(acc_ref)

        av = a_ref[...]
        for gs in range(ngrp):
            def one_group(gs=gs):
                for jj in range(per):
                    for si, soff, woff, wd in pieces[gs * per + jj]:
                        acc_ref[jj, :, pl.ds(woff, wd)] += lax.dot_general(
                            av, d_refs[si][:, pl.ds(soff, wd)], (((0,), (0,)), ((), ())), preferred_element_type=f32)
            pl.when(grp == gs)(one_group)

        @pl.when(i == nI - 1)
        def _():
            o_ref[...] = acc_ref[...].astype(bf16)

    return pl.pallas_call(
        body, grid=(ngrp, nI),
        in_specs=[pl.BlockSpec((SMM_BM, K), lambda g, i: (i, 0))]
        + [pl.BlockSpec((SMM_BM, d.shape[1]), lambda g, i: (i, 0)) for d in d_segs],
        out_specs=pl.BlockSpec((per, K, n), lambda g, i: (g, 0, 0)), out_shape=S((N_DEV, K, n), bf16),
        scratch_shapes=[pltpu.VMEM((per, K, n), f32)],
        compiler_params=_cparams("arbitrary", "arbitrary"), name=name)(a, *d_segs)


def _modnorm_f(h, w, sc, sh):
    y = h * lax.rsqrt(jnp.mean(h * h, axis=-1, keepdims=True) + EPS)
    return (y * w) * (1.0 + sc) + sh


def _kind_specs(nctxb):
    if nctxb > 0:
        return pl.BlockSpec((None, 1, D), lambda i: (jnp.where(i < nctxb, 0, 1), 0, 0))
    return pl.BlockSpec((None, 1, D), lambda i: (0, 0, 0))


def modnorm_fwd(h, w, sc, sh, nctxb, name):
    T = h.shape[0]

    def body(h_ref, w_ref, sc_ref, sh_ref, o_ref):
        o_ref[...] = _modnorm_f(h_ref[...], w_ref[...], sc_ref[...], sh_ref[...]).astype(bf16)

    blk = pl.BlockSpec((TB, D), lambda i: (i, 0))
    row = pl.BlockSpec((1, D), lambda i: (0, 0))
    ks = _kind_specs(nctxb)
    return pl.pallas_call(body, grid=(T // TB,), in_specs=[blk, row, ks, ks], out_specs=blk,
                          out_shape=S((T, D), bf16), compiler_params=_cparams("parallel"), name=name)(h, w, sc, sh)


def modnorm_bwd(h, w, sc, sh, da, dres, nctxb, name):
    T = h.shape[0]
    kinds = sc.shape[0]

    def body(h_ref, w_ref, sc_ref, sh_ref, da_ref, dres_ref, dh_ref, dw_ref, dsc_ref, dsh_ref):
        i = pl.program_id(0)
        _, vjp = jax.vjp(_modnorm_f, h_ref[...], w_ref[...], sc_ref[...], sh_ref[...])
        dh, dw, dsc, dsh = vjp(da_ref[...].astype(f32))
        dh_ref[...] = dres_ref[...] + dh

        @pl.when(i == 0)
        def _():
            dw_ref[...] = jnp.zeros_like(dw_ref)

        @pl.when((i == 0) | (i == nctxb))
        def _():
            dsc_ref[...] = jnp.zeros_like(dsc_ref)
            dsh_ref[...] = jnp.zeros_like(dsh_ref)

        dw_ref[...] += dw
        dsc_ref[...] += dsc
        dsh_ref[...] += dsh

    blk = pl.BlockSpec((TB, D), lambda i: (i, 0))
    row = pl.BlockSpec((1, D), lambda i: (0, 0))
    ks = _kind_specs(nctxb)
    return pl.pallas_call(
        body, grid=(T // TB,), in_specs=[blk, row, ks, ks, blk, blk], out_specs=[blk, row, ks, ks],
        out_shape=[S((T, D), f32), S((1, D), f32), S((kinds, 1, D), f32), S((kinds, 1, D), f32)],
        compiler_params=_cparams("arbitrary"), name=name)(h, w, sc, sh, da, dres)


def resgate_fwd(h, o, g, b, name):
    T = h.shape[0]

    def body(h_ref, o_ref, g_ref, b_ref, out_ref):
        out_ref[...] = h_ref[...] + g_ref[...] * (o_ref[...] + b_ref[...])

    blk = pl.BlockSpec((TB, D), lambda i: (i, 0))
    row = pl.BlockSpec((1, D), lambda i: (0, 0))
    return pl.pallas_call(body, grid=(T // TB,), in_specs=[blk, blk, row, row], out_specs=blk,
                          out_shape=S((T, D), f32), compiler_params=_cparams("parallel"), name=name)(h, o, g, b)


def resgate_bwd(dh, o, g, b, name):
    T = dh.shape[0]

    def body(dh_ref, o_ref, g_ref, b_ref, do_ref, dg_ref, db_ref):
        i = pl.program_id(0)

        @pl.when(i == 0)
        def _():
            dg_ref[...] = jnp.zeros_like(dg_ref)
            db_ref[...] = jnp.zeros_like(db_ref)

        dh = dh_ref[...]
        do = g_ref[...] * dh
        do_ref[...] = do.astype(bf16)
        dg_ref[...] += jnp.sum(dh * (o_ref[...] + b_ref[...]), axis=0, keepdims=True)
        db_ref[...] += jnp.sum(do, axis=0, keepdims=True)

    blk = pl.BlockSpec((TB, D), lambda i: (i, 0))
    row = pl.BlockSpec((1, D), lambda i: (0, 0))
    return pl.pallas_call(body, grid=(T // TB,), in_specs=[blk, blk, row, row], out_specs=[blk, row, row],
                          out_shape=[S((T, D), bf16), S((1, D), f32), S((1, D), f32)],
                          compiler_params=_cparams("arbitrary"), name=name)(dh, o, g, b)


def final_loss(h, w, tgt, name):
    T = h.shape[0]

    def f(hv, wv, tv):
        y = (hv * lax.rsqrt(jnp.mean(hv * hv, axis=-1, keepdims=True) + EPS)) * wv
        e = y - tv
        return 0.5 * jnp.sum(jnp.sum(e * e, axis=-1, keepdims=True), axis=0, keepdims=True) * (1.0 / D)

    def body(h_ref, w_ref, t_ref, loss_ref, dh_ref, dw_ref):
        i = pl.program_id(0)
        tv = t_ref[...]
        val, vjp = jax.vjp(lambda a, b_: f(a, b_, tv), h_ref[...], w_ref[...])
        dh, dw = vjp(jnp.ones((1, 1), f32))
        dh_ref[...] = dh

        @pl.when(i == 0)
        def _():
            loss_ref[...] = jnp.zeros_like(loss_ref)
            dw_ref[...] = jnp.zeros_like(dw_ref)

        loss_ref[...] += jnp.broadcast_to(val, (1, 128))
        dw_ref[...] += dw

    blk = pl.BlockSpec((TB, D), lambda i: (i, 0))
    row = pl.BlockSpec((1, D), lambda i: (0, 0))
    return pl.pallas_call(body, grid=(T // TB,), in_specs=[blk, row, blk],
                          out_specs=[pl.BlockSpec((1, 128), lambda i: (0, 0)), blk, row],
                          out_shape=[S((1, 128), f32), S((T, D), f32), S((1, D), f32)],
                          compiler_params=_cparams("arbitrary"), name=name)(h, w, tgt)


CB = 256
RT = 32
RTB = 16


def _fold8(t):
    acc = t[0:8]
    for k in range(1, t.shape[0] // 8):
        acc = acc + t[8 * k:8 * (k + 1)]
    return acc


def _rows(start, off=0, rt=RT):
    return pl.ds(pl.multiple_of(start + off, 8), rt)


def _rowsb(start, off=0):
    return _rows(start, off, RTB)


def _zero_rows(ref, start, n):
    ref[pl.ds(start, n), :] = jnp.zeros((n, ref.shape[1]), f32)


K5, HALF5, PAD5 = 5, 2, 8


def _shift_copies5(base_ref, s_ref, ln, sign):
    for k in range(K5):
        s_ref[k, pl.ds(0, ln), :] = base_ref[pl.ds(PAD5 + sign * (k - HALF5), ln), :]


def ssd_conv_fwd(u, w, b, segs, name, hosted=None):
    T = u.shape[0]
    maxlen = max(ln for _, ln in segs)

    def body(u_ref, w_ref, b_ref, o_ref, base_ref, s_ref):
        wv = [w_ref[pl.ds(k, 1), :] for k in range(K5)]
        bv = b_ref[...]
        for s0, ln in segs:
            _zero_rows(base_ref, 0, PAD5)
            _zero_rows(base_ref, PAD5 + ln, PAD5)
            base_ref[pl.ds(PAD5, ln), :] = u_ref[pl.ds(s0, ln), :]
            _shift_copies5(base_ref, s_ref, ln, 1)

            def tile(i, carry):
                r = i * RT
                acc = jnp.broadcast_to(bv, (RT, CB))
                for k in range(K5):
                    acc = acc + s_ref[k, _rows(r), :] * wv[k]
                o_ref[_rows(r, s0), :] = acc * _sigmoid(acc)
                return carry

            lax.fori_loop(0, ln // RT, tile, 0)

    (out,), extra = _host_call(
        body, (CONVD // CB,),
        [pl.BlockSpec((T, CB), lambda j: (0, j)), pl.BlockSpec((K5, CB), lambda j: (0, j)),
         pl.BlockSpec((1, CB), lambda j: (0, j))],
        [pl.BlockSpec((T, CB), lambda j: (0, j))], [S((T, CONVD), f32)],
        [pltpu.VMEM((maxlen + 2 * PAD5, CB), f32), pltpu.VMEM((K5, maxlen, CB), f32)],
        ("parallel",), name, (u, w, b), hosted)
    return out, extra


def ssd_conv_bwd(proj, w, b, dy2, dskip, segs, name):
    T = proj.shape[0]
    maxlen = max(ln for _, ln in segs)
    nskip = DI // CB

    def body(u_ref, w_ref, b_ref, dya_ref, dyb_ref, dsk_ref, du_ref, dw_ref, db_ref, base_ref, s_ref):
        wv = [w_ref[pl.ds(k, 1), :] for k in range(K5)]
        bv = b_ref[...]
        has_skip = (pl.program_id(0) < nskip).astype(f32)
        acc8 = tuple(jnp.zeros((8, CB), f32) for _ in range(K5 + 1))
        for s0, ln in segs:
            _zero_rows(base_ref, 0, PAD5)
            _zero_rows(base_ref, PAD5 + ln, PAD5)
            base_ref[pl.ds(PAD5, ln), :] = u_ref[pl.ds(s0, ln), :]
            _shift_copies5(base_ref, s_ref, ln, 1)

            def tile1(i, carry):
                r = i * RTB
                taps = [s_ref[k, _rowsb(r), :] for k in range(K5)]
                pre = jnp.broadcast_to(bv, (RTB, CB))
                for k in range(K5):
                    pre = pre + taps[k] * wv[k]
                sg = _sigmoid(pre)
                dy = dya_ref[_rowsb(r, s0), :] + dyb_ref[_rowsb(r, s0), :] + has_skip * dsk_ref[_rowsb(r, s0), :]
                dpre = dy * (sg * (1.0 + pre * (1.0 - sg)))
                base_ref[_rowsb(r, PAD5), :] = dpre
                new = [carry[k] + _fold8(dpre * taps[k]) for k in range(K5)]
                new.append(carry[K5] + _fold8(dpre))
                return tuple(new)

            acc8 = lax.fori_loop(0, ln // RTB, tile1, acc8)
            _shift_copies5(base_ref, s_ref, ln, -1)

            def tile2(i, carry):
                r = i * RTB
                du = jnp.zeros((RTB, CB), f32)
                for k in range(K5):
                    du = du + s_ref[k, _rowsb(r), :] * wv[k]
                du_ref[_rowsb(r, s0), :] = du.astype(bf16)
                return carry

            lax.fori_loop(0, ln // RTB, tile2, 0)
        for k in range(K5):
            dw_ref[pl.ds(k, 1), :] = jnp.sum(acc8[k], axis=0, keepdims=True)
        db_ref[...] = jnp.sum(acc8[K5], axis=0, keepdims=True)

    cblk = pl.BlockSpec((T, CB), lambda j: (0, j))
    return pl.pallas_call(
        body, grid=(CONVD // CB,),
        in_specs=[cblk, pl.BlockSpec((K5, CB), lambda j: (0, j)), pl.BlockSpec((1, CB), lambda j: (0, j)),
                  pl.BlockSpec((None, T, CB), lambda j: (0, 0, j)), pl.BlockSpec((None, T, CB), lambda j: (1, 0, j)),
                  pl.BlockSpec((T, CB), lambda j: (0, jnp.minimum(j, nskip - 1)))],
        out_specs=[cblk, pl.BlockSpec((K5, CB), lambda j: (0, j)), pl.BlockSpec((1, CB), lambda j: (0, j))],
        out_shape=[S((T, CONVD), bf16), S((K5, CONVD), f32), S((1, CONVD), f32)],
        scratch_shapes=[pltpu.VMEM((maxlen + 2 * PAD5, CB), f32), pltpu.VMEM((K5, maxlen, CB), f32)],
        compiler_params=_cparams("parallel"), name=name)(proj, w, b, dy2, dy2, dskip)


GPAD = GRID_W


def _grid_copies(g_ref, src, L):
    col = lax.broadcasted_iota(jnp.int32, (L, CB), 0) & (GRID_W - 1)
    for d in range(3):
        _zero_rows(g_ref.at[d], 0, GPAD)
        _zero_rows(g_ref.at[d], GPAD + L, GPAD)
    g_ref[1, pl.ds(GPAD, L), :] = src
    g_ref[0, pl.ds(GPAD, L), :] = jnp.where(col != 0, g_ref[1, pl.ds(GPAD - 1, L), :], 0.0)
    g_ref[2, pl.ds(GPAD, L), :] = jnp.where(col != GRID_W - 1, g_ref[1, pl.ds(GPAD + 1, L), :], 0.0)


def ffn_gate_fwd(val, gate, cw, cb_, name):
    L = val.shape[0]
    nb = FH // CB

    def body(val_ref, gate_ref, w_ref, b_ref, o_ref, g_ref):
        wv = [w_ref[pl.ds(t, 1), :] for t in range(9)]
        bv = b_ref[...]
        _grid_copies(g_ref, gate_ref[...], L)

        def tile(i, carry):
            r = i * RT
            acc = jnp.broadcast_to(bv, (RT, CB))
            for dr in range(3):
                for dc in range(3):
                    acc = acc + g_ref[dc, _rows(r, GPAD + (dr - 1) * GRID_W), :] * wv[3 * dr + dc]
            o_ref[_rows(r), :] = (acc * _sigmoid(acc) * val_ref[_rows(r), :]).astype(bf16)
            return carry

        lax.fori_loop(0, L // RT, tile, 0)

    cblk = pl.BlockSpec((L, CB), lambda j: (0, j))
    return pl.pallas_call(
        body, grid=(nb,),
        in_specs=[cblk, cblk, pl.BlockSpec((9, CB), lambda j: (0, j)), pl.BlockSpec((1, CB), lambda j: (0, j))],
        out_specs=cblk, out_shape=S((L, FH), bf16),
        scratch_shapes=[pltpu.VMEM((3, L + 2 * GPAD, CB), f32)],
        compiler_params=_cparams("parallel"), name=name)(val, gate, cw, cb_)


def ffn_gate_bwd(val, gate, cw, cb_, dact, name):
    L = val.shape[0]
    nb = FH // CB

    def body(val_ref, gate_ref, w_ref, b_ref, da_ref, dval_ref, dgate_ref, dw_ref, db_ref, g_ref, d_ref):
        wv = [w_ref[pl.ds(t, 1), :] for t in range(9)]
        bv = b_ref[...]
        _grid_copies(g_ref, gate_ref[...], L)

        def tile1(i, carry):
            r = i * RTB

            def tap(t):
                return g_ref[t % 3, _rowsb(r, GPAD + (t // 3 - 1) * GRID_W), :]

            pre = jnp.broadcast_to(bv, (RTB, CB))
            for t in range(9):
                pre = pre + tap(t) * wv[t]
            sg = _sigmoid(pre)
            da = da_ref[_rowsb(r), :].astype(f32)
            dval_ref[_rowsb(r), :] = (da * pre * sg).astype(bf16)
            dpre = da * val_ref[_rowsb(r), :] * (sg * (1.0 + pre * (1.0 - sg)))
            d_ref[_rowsb(r), :] = dpre
            new = [carry[t] + _fold8(dpre * tap(t)) for t in range(9)]
            new.append(carry[9] + _fold8(dpre))
            return tuple(new)

        acc8 = lax.fori_loop(0, L // RTB, tile1, tuple(jnp.zeros((8, CB), f32) for _ in range(10)))
        for t in range(9):
            dw_ref[pl.ds(t, 1), :] = jnp.sum(acc8[t], axis=0, keepdims=True)
        db_ref[...] = jnp.sum(acc8[9], axis=0, keepdims=True)
        _grid_copies(g_ref, d_ref[...], L)

        def tile2(i, carry):
            r = i * RTB
            dg = jnp.zeros((RTB, CB), f32)
            for dr in range(3):
                for dc in range(3):
                    dg = dg + g_ref[2 - dc, _rowsb(r, GPAD - (dr - 1) * GRID_W), :] * wv[3 * dr + dc]
            dgate_ref[_rowsb(r), :] = dg.astype(bf16)
            return carry

        lax.fori_loop(0, L // RTB, tile2, 0)

    cblk = pl.BlockSpec((L, CB), lambda j: (0, j))
    return pl.pallas_call(
        body, grid=(nb,),
        in_specs=[cblk, cblk, pl.BlockSpec((9, CB), lambda j: (0, j)), pl.BlockSpec((1, CB), lambda j: (0, j)), cblk],
        out_specs=[cblk, cblk, pl.BlockSpec((9, CB), lambda j: (0, j)), pl.BlockSpec((1, CB), lambda j: (0, j))],
        out_shape=[S((L, FH), bf16), S((L, FH), bf16), S((9, FH), f32), S((1, FH), f32)],
        scratch_shapes=[pltpu.VMEM((3, L + 2 * GPAD, CB), f32), pltpu.VMEM((L, CB), f32)],
        compiler_params=_cparams("parallel"), name=name)(val, gate, cw, cb_, dact)


CONF_K = 31
CHALF = CONF_K // 2
CPAD = 16


def _shift_copies8(c_ref, base_ref, L):
    n = L + 2 * CPAD - 8
    for b_ in range(8):
        c_ref[b_, pl.ds(0, n), :] = base_ref[pl.ds(b_, n), :]


def _tap_ab(o):
    return o % 8, o - o % 8


def conf_glu_conv_fwd(pa, pg, b1, wdw, bdw, name):
    L = pa.shape[0]
    nb = D // CB

    def body(pa_ref, pg_ref, ba_ref, bg_ref, w_ref, bdw_ref, o_ref, base_ref, c_ref):
        _zero_rows(base_ref, 0, CPAD)
        _zero_rows(base_ref, CPAD + L, CPAD)
        base_ref[pl.ds(CPAD, L), :] = (pa_ref[...] + ba_ref[...]) * _sigmoid(pg_ref[...] + bg_ref[...])
        _shift_copies8(c_ref, base_ref, L)
        bv = bdw_ref[...]

        def tile(i, carry):
            r = i * RT
            acc = jnp.broadcast_to(bv, (RT, CB))
            for k in range(CONF_K):
                b_, a8 = _tap_ab(k - CHALF)
                acc = acc + c_ref[b_, _rows(r, CPAD + a8), :] * w_ref[pl.ds(k, 1), :]
            o_ref[_rows(r), :] = acc
            return carry

        lax.fori_loop(0, L // RT, tile, 0)

    cblk = pl.BlockSpec((L, CB), lambda j: (0, j))
    rblk = pl.BlockSpec((1, CB), lambda j: (0, j))
    rgblk = pl.BlockSpec((1, CB), lambda j: (0, nb + j))
    return pl.pallas_call(
        body, grid=(nb,), in_specs=[cblk, cblk, rblk, rgblk, pl.BlockSpec((CONF_K, CB), lambda j: (0, j)), rblk],
        out_specs=cblk, out_shape=S((L, D), f32),
        scratch_shapes=[pltpu.VMEM((L + 2 * CPAD, CB), f32), pltpu.VMEM((8, L + 2 * CPAD, CB), f32)],
        compiler_params=_cparams("parallel"), name=name)(pa, pg, b1, b1, wdw, bdw)


def conf_glu_conv_bwd(pa, pg, b1, wdw, dy, name):
    L = pa.shape[0]
    nb = D // CB

    def body(pa_ref, pg_ref, ba_ref, bg_ref, w_ref, dy_ref, dpa_ref, dpg_ref, dba_ref, dbg_ref, dw_ref, dbdw_ref,
             base_ref, c_ref, acc_ref):
        _zero_rows(base_ref, 0, CPAD)
        _zero_rows(base_ref, CPAD + L, CPAD)
        base_ref[pl.ds(CPAD, L), :] = (pa_ref[...] + ba_ref[...]) * _sigmoid(pg_ref[...] + bg_ref[...])
        _shift_copies8(c_ref, base_ref, L)
        acc_ref[...] = jnp.zeros_like(acc_ref)

        def tile1(i, carry):
            r = i * RTB
            dyt = dy_ref[_rowsb(r), :]
            for k in range(CONF_K):
                b_, a8 = _tap_ab(k - CHALF)
                acc_ref[k] += _fold8(dyt * c_ref[b_, _rowsb(r, CPAD + a8), :])
            return carry + _fold8(dyt)

        db8 = lax.fori_loop(0, L // RTB, tile1, jnp.zeros((8, CB), f32))
        dbdw_ref[...] = jnp.sum(db8, axis=0, keepdims=True)
        for k in range(CONF_K):
            dw_ref[pl.ds(k, 1), :] = jnp.sum(acc_ref[k], axis=0, keepdims=True)
        base_ref[pl.ds(CPAD, L), :] = dy_ref[...]
        _shift_copies8(c_ref, base_ref, L)
        ba = ba_ref[...]
        bg = bg_ref[...]

        def tile2(i, carry):
            r = i * RTB
            dglu = jnp.zeros((RTB, CB), f32)
            for k in range(CONF_K):
                b_, a8 = _tap_ab(CHALF - k)
                dglu = dglu + c_ref[b_, _rowsb(r, CPAD + a8), :] * w_ref[pl.ds(k, 1), :]
            a = pa_ref[_rowsb(r), :] + ba
            sg = _sigmoid(pg_ref[_rowsb(r), :] + bg)
            dpa = dglu * sg
            dpg = dglu * a * (sg * (1.0 - sg))
            dpa_ref[_rowsb(r), :] = dpa.astype(bf16)
            dpg_ref[_rowsb(r), :] = dpg.astype(bf16)
            return carry[0] + _fold8(dpa), carry[1] + _fold8(dpg)

        s8 = lax.fori_loop(0, L // RTB, tile2, (jnp.zeros((8, CB), f32), jnp.zeros((8, CB), f32)))
        dba_ref[...] = jnp.sum(s8[0], axis=0, keepdims=True)
        dbg_ref[...] = jnp.sum(s8[1], axis=0, keepdims=True)

    cblk = pl.BlockSpec((L, CB), lambda j: (0, j))
    rblk = pl.BlockSpec((1, CB), lambda j: (0, j))
    rgblk = pl.BlockSpec((1, CB), lambda j: (0, nb + j))
    wblk = pl.BlockSpec((CONF_K, CB), lambda j: (0, j))
    return pl.pallas_call(
        body, grid=(nb,), in_specs=[cblk, cblk, rblk, rgblk, wblk, cblk],
        out_specs=[cblk, cblk, rblk, rblk, wblk, rblk],
        out_shape=[S((L, D), bf16), S((L, D), bf16), S((1, D), f32), S((1, D), f32), S((CONF_K, D), f32), S((1, D), f32)],
        scratch_shapes=[pltpu.VMEM((L + 2 * CPAD, CB), f32), pltpu.VMEM((8, L + 2 * CPAD, CB), f32),
                        pltpu.VMEM((CONF_K, 8, CB), f32)],
        compiler_params=_cparams("parallel"), name=name)(pa, pg, b1, b1, wdw, dy)


def _ln_silu_f(x, w, b):
    mu = jnp.mean(x, axis=-1, keepdims=True)
    d = x - mu
    y = d * lax.rsqrt(jnp.mean(d * d, axis=-1, keepdims=True) + EPS) * w + b
    return y * _sigmoid(y)


def ln_silu_fwd(x, w, b, name):
    T = x.shape[0]

    def body(x_ref, w_ref, b_ref, o_ref):
        o_ref[...] = _ln_silu_f(x_ref[...], w_ref[...], b_ref[...]).astype(bf16)

    blk = pl.BlockSpec((TB, D), lambda i: (i, 0))
    row = pl.BlockSpec((1, D), lambda i: (0, 0))
    return pl.pallas_call(body, grid=(T // TB,), in_specs=[blk, row, row], out_specs=blk, out_shape=S((T, D), bf16),
                          compiler_params=_cparams("parallel"), name=name)(x, w, b)


def ln_silu_bwd(x, w, b, ds, name):
    T = x.shape[0]

    def body(x_ref, w_ref, b_ref, ds_ref, dx_ref, dw_ref, db_ref):
        i = pl.program_id(0)
        _, vjp = jax.vjp(_ln_silu_f, x_ref[...], w_ref[...], b_ref[...])
        dx, dw, db = vjp(ds_ref[...].astype(f32))
        dx_ref[...] = dx

        @pl.when(i == 0)
        def _():
            dw_ref[...] = jnp.zeros_like(dw_ref)
            db_ref[...] = jnp.zeros_like(db_ref)

        dw_ref[...] += dw
        db_ref[...] += db

    blk = pl.BlockSpec((TB, D), lambda i: (i, 0))
    row = pl.BlockSpec((1, D), lambda i: (0, 0))
    return pl.pallas_call(body, grid=(T // TB,), in_specs=[blk, row, row, blk], out_specs=[blk, row, row],
                          out_shape=[S((T, D), f32), S((1, D), f32), S((1, D), f32)],
                          compiler_params=_cparams("arbitrary"), name=name)(x, w, b, ds)


def _mxu(a, b, dims):
    return lax.dot_general(a.astype(bf16), b.astype(bf16), (dims, ((), ())), preferred_element_type=f32)


def _nn(a, b):
    return _mxu(a, b, ((1,), (0,)))


def _nt(a, b):
    return _mxu(a, b, ((1,), (1,)))


def _tn(a, b):
    return _mxu(a, b, ((0,), (0,)))


@jax.custom_vjp
def _dot_nn(a, b):
    return _nn(a, b)


@jax.custom_vjp
def _dot_nt(a, b):
    return _nt(a, b)


@jax.custom_vjp
def _dot_tn(a, b):
    return _tn(a, b)


_dot_nn.defvjp(lambda a, b: (_nn(a, b), (a, b)), lambda res, g: (_nt(g, res[1]), _tn(res[0], g)))
_dot_nt.defvjp(lambda a, b: (_nt(a, b), (a, b)), lambda res, g: (_nn(g, res[1]), _tn(g, res[0])))
_dot_tn.defvjp(lambda a, b: (_tn(a, b), (a, b)), lambda res, g: (_nt(res[1], g), _nn(res[0], g)))


def _lanes_to_rows(v):
    r = lax.broadcasted_iota(jnp.int32, (GW, GW), 0)
    c = lax.broadcasted_iota(jnp.int32, (GW, GW), 1)
    return jnp.sum(jnp.where(r == c, jnp.broadcast_to(v, (GW, GW)), 0.0), axis=1, keepdims=True)


def _ssd_chunk(x, B, C, dtc, dtr, bc, br, alc, alr, s_in, is_fwd):
    row = lax.broadcasted_iota(jnp.int32, (Q, Q), 0)
    col = lax.broadcasted_iota(jnp.int32, (Q, Q), 1)
    sgn = jnp.where(is_fwd, 1, -1).astype(jnp.int32)
    mask = (row - col) * sgn >= 0
    mf = mask.astype(f32)
    lane_head = lax.broadcasted_iota(jnp.int32, (1, GW), 1) // P

    def spread(v):
        out = jnp.zeros((v.shape[0], GW), f32)
        for r in range(HPG):
            out = jnp.where(lane_head == r, v[:, r:r + 1], out)
        return out

    dt_c = _softplus(dtc + bc)
    dt_r = _softplus(dtr + br)
    a_c = dt_c * (-jnp.exp(alc))
    a_r = dt_r * (-jnp.exp(alr))
    acum_c = jnp.dot(mf, a_c, precision=HI, preferred_element_type=f32)
    acum_r = lax.dot_general(a_r, mf, (((1,), (1,)), ((), ())), precision=HI, preferred_element_type=f32)
    tot_c = jnp.sum(a_c, axis=0, keepdims=True)
    dt_e = spread(dt_c)
    acum_e = spread(acum_c)
    tot_e = spread(tot_c)
    xdt = x * dt_e
    cb = _dot_nt(C, B)
    y = jnp.zeros((Q, GW), f32)
    for r in range(HPG):
        seg = acum_c[:, r:r + 1] - acum_r[r:r + 1, :]
        dec = jnp.exp(jnp.where(mask, seg, -jnp.inf))
        y = y + jnp.where(lane_head == r, _dot_nn(cb * dec, xdt), 0.0)
    y = y + _dot_nt(C, s_in) * jnp.exp(acum_e)
    xe = xdt * jnp.exp(tot_e - acum_e)
    s_out = _lanes_to_rows(jnp.exp(tot_e)) * s_in + _dot_tn(xe, B)
    return y, s_out


def _chunk_index(d, t, nctx, nc):
    bwd = jnp.where(t < nctx, nctx - 1 - t, nc - 1 - (t - nctx))
    return jnp.where(d == 0, t, bwd)


def _ssd_in_specs(ci):
    small_c = pl.BlockSpec((None, G, 1, HPG), lambda d, t: (d, 0, 0, 0))
    small_r = pl.BlockSpec((None, G, HPG, 1), lambda d, t: (d, 0, 0, 0))
    return [
        pl.BlockSpec((Q, CONVD), lambda d, t: (ci(d, t), 0)),
        pl.BlockSpec((None, G, Q, HPG), lambda d, t: (d, 0, ci(d, t), 0)),
        pl.BlockSpec((None, G, HPG, Q), lambda d, t: (d, 0, 0, ci(d, t))),
        small_c, small_r, small_c, small_r,
    ]


def _group_cols(g):
    return pl.ds(g * GW, GW), pl.ds(DI + g * N, N), pl.ds(DI + G * N + g * N, N)


def ssd_scan_fwd(xbc, dtc, dtr, bc, br, alc, alr, nctx, name, hosted=None):
    T = xbc.shape[0]
    nc = T // Q

    def body(xbc_ref, dtc_ref, dtr_ref, bc_ref, br_ref, alc_ref, alr_ref, y_ref, sin_ref, st_ref):
        d = pl.program_id(0)
        t = pl.program_id(1)

        @pl.when(t == 0)
        def _():
            st_ref[...] = jnp.zeros_like(st_ref)

        for g in range(G):
            xs, bs, cs = _group_cols(g)
            s_in = st_ref[g]
            sin_ref[g] = s_in
            y, s_out = _ssd_chunk(xbc_ref[:, xs], xbc_ref[:, bs], xbc_ref[:, cs], dtc_ref[g], dtr_ref[g], bc_ref[g], br_ref[g],
                                  alc_ref[g], alr_ref[g], s_in, d == 0)
            y_ref[:, xs] = y
            st_ref[g] = s_out

    ci = lambda d, t: _chunk_index(d, t, nctx, nc)
    out_specs = [
        pl.BlockSpec((None, Q, DI), lambda d, t: (d, ci(d, t), 0)),
        pl.BlockSpec((None, None, G, GW, N), lambda d, t: (d, ci(d, t), 0, 0, 0)),
    ]
    return _host_call(
        body, (2, nc), _ssd_in_specs(ci), out_specs, [S((2, T, DI), f32), S((2, nc, G, GW, N), f32)],
        [pltpu.VMEM((G, GW, N), f32)], ("arbitrary", "arbitrary"), name, (xbc, dtc, dtr, bc, br, alc, alr), hosted)


def ssd_scan_bwd(xbc, dtc, dtr, bc, br, alc, alr, s_in_all, dy, nctx, name, hosted=None):
    T = xbc.shape[0]
    nc = T // Q

    def body(xbc_ref, dtc_ref, dtr_ref, bc_ref, br_ref, alc_ref, alr_ref, sin_ref, dy_ref,
             dxbc_ref, ddtc_ref, ddtr_ref, dbc_ref, dbr_ref, dalc_ref, dalr_ref, ds_ref):
        d = pl.program_id(0)
        t = pl.program_id(1)

        @pl.when(t == 0)
        def _():
            ds_ref[...] = jnp.zeros_like(ds_ref)
            dbc_ref[...] = jnp.zeros_like(dbc_ref)
            dbr_ref[...] = jnp.zeros_like(dbr_ref)
            dalc_ref[...] = jnp.zeros_like(dalc_ref)
            dalr_ref[...] = jnp.zeros_like(dalr_ref)

        f = functools.partial(_ssd_chunk, is_fwd=(d == 0))
        for g in range(G):
            xs, bs, cs = _group_cols(g)
            _, vjp = jax.vjp(f, xbc_ref[:, xs], xbc_ref[:, bs], xbc_ref[:, cs], dtc_ref[g], dtr_ref[g], bc_ref[g], br_ref[g],
                             alc_ref[g], alr_ref[g], sin_ref[g])
            dx, dB, dC, ddtc, ddtr, dbc, dbr, dalc, dalr, ds = vjp((dy_ref[:, xs], ds_ref[g]))
            dxbc_ref[:, xs] = dx
            dxbc_ref[:, bs] = dB
            dxbc_ref[:, cs] = dC
            ddtc_ref[g] = ddtc
            ddtr_ref[g] = ddtr
            dbc_ref[g] += dbc
            dbr_ref[g] += dbr
            dalc_ref[g] += dalc
            dalr_ref[g] += dalr
            ds_ref[g] = ds

    ci = lambda d, t: _chunk_index(d, nc - 1 - t, nctx, nc)
    in_specs = _ssd_in_specs(ci) + [
        pl.BlockSpec((None, None, G, GW, N), lambda d, t: (d, ci(d, t), 0, 0, 0)),
        pl.BlockSpec((Q, DI), lambda d, t: (ci(d, t), 0)),
    ]
    small_c = pl.BlockSpec((None, G, 1, HPG), lambda d, t: (d, 0, 0, 0))
    small_r = pl.BlockSpec((None, G, HPG, 1), lambda d, t: (d, 0, 0, 0))
    out_specs = [
        pl.BlockSpec((None, Q, CONVD), lambda d, t: (d, ci(d, t), 0)),
        pl.BlockSpec((None, G, Q, HPG), lambda d, t: (d, 0, ci(d, t), 0)),
        pl.BlockSpec((None, G, HPG, Q), lambda d, t: (d, 0, 0, ci(d, t))),
        small_c, small_r, small_c, small_r,
    ]
    out_shape = [S((2, T, CONVD), f32), S((2, G, T, HPG), f32), S((2, G, HPG, T), f32),
                 S((2, G, 1, HPG), f32), S((2, G, HPG, 1), f32), S((2, G, 1, HPG), f32), S((2, G, HPG, 1), f32)]
    return _host_call(body, (2, nc), in_specs, out_specs, out_shape, [pltpu.VMEM((G, GW, N), f32)],
                      ("arbitrary", "arbitrary"), name, (xbc, dtc, dtr, bc, br, alc, alr, s_in_all, dy), hosted)


GTB = 128


def _gate_norm_f(yf, yb, x, z, dexp, w):
    y = (yf + yb + dexp * x) * (z * _sigmoid(z))
    return y * lax.rsqrt(jnp.mean(y * y, axis=-1, keepdims=True) + EPS) * w


def ssd_gate_fwd(y2, xbc, proj, dexp, w, nctxb, name):
    T = xbc.shape[0]
    L = T - nctxb * GTB

    def body(yf_ref, yb_ref, x_ref, z_ref, d_ref, w_ref, o_ref):
        o_ref[...] = _gate_norm_f(yf_ref[...], yb_ref[...], x_ref[...], z_ref[...], d_ref[...], w_ref[...]).astype(bf16)

    wide = pl.BlockSpec((GTB, DI), lambda i: (i + nctxb, 0))
    row = pl.BlockSpec((1, DI), lambda i: (0, 0))
    return pl.pallas_call(
        body, grid=(L // GTB,),
        in_specs=[pl.BlockSpec((None, GTB, DI), lambda i: (0, i + nctxb, 0)),
                  pl.BlockSpec((None, GTB, DI), lambda i: (1, i + nctxb, 0)), wide, wide, row, row],
        out_specs=pl.BlockSpec((GTB, DI), lambda i: (i, 0)), out_shape=S((L, DI), bf16),
        compiler_params=_cparams("parallel"), name=name)(y2, y2, xbc, proj, dexp, w)


def ssd_gate_bwd(y2, xbc, proj, dexp, w, dyn, nctxb, name, hosted=None):
    T = xbc.shape[0]
    nb = T // GTB

    def body(yf_ref, yb_ref, x_ref, z_ref, d_ref, w_ref, dyn_ref, dy_ref, dx_ref, dz_ref, dd_ref, dw_ref):
        i = pl.program_id(0)

        @pl.when(i == 0)
        def _():
            dd_ref[...] = jnp.zeros_like(dd_ref)
            dw_ref[...] = jnp.zeros_like(dw_ref)

        @pl.when(i < nctxb)
        def _():
            dy_ref[...] = jnp.zeros_like(dy_ref)
            dx_ref[...] = jnp.zeros_like(dx_ref)
            dz_ref[...] = jnp.zeros_like(dz_ref)

        @pl.when(i >= nctxb)
        def _():
            _, vjp = jax.vjp(_gate_norm_f, yf_ref[...], yb_ref[...], x_ref[...], z_ref[...], d_ref[...], w_ref[...])
            dyf, _, dx, dz, dd, dw = vjp(dyn_ref[...].astype(f32))
            dy_ref[...] = dyf
            dx_ref[...] = dx
            dz_ref[...] = dz.astype(bf16)
            fold = (lax.broadcasted_iota(jnp.int32, (DI, 128), 0) // P == lax.broadcasted_iota(jnp.int32, (DI, 128), 1))
            dd_ref[...] += jnp.dot(dd, fold.astype(f32), precision=HI, preferred_element_type=f32)
            dw_ref[...] += dw

    wide = pl.BlockSpec((GTB, DI), lambda i: (i, 0))
    row = pl.BlockSpec((1, DI), lambda i: (0, 0))
    hrow = pl.BlockSpec((1, 128), lambda i: (0, 0))
    return _host_call(
        body, (nb,),
        [pl.BlockSpec((None, GTB, DI), lambda i: (0, i, 0)), pl.BlockSpec((None, GTB, DI), lambda i: (1, i, 0)),
         wide, wide, row, row, pl.BlockSpec((GTB, DI), lambda i: (jnp.maximum(i - nctxb, 0), 0))],
        [wide, wide, wide, hrow, row],
        [S((T, DI), f32), S((T, DI), f32), S((T, DI), bf16), S((1, 128), f32), S((1, DI), f32)],
        [], ("arbitrary",), name, (y2, y2, xbc, proj, dexp, w, dyn), hosted)


CROWS = 2 * N_DEV


def mod_fwd(c16, modw, name):
    nl, _, cols = modw.shape

    def body(c_ref, w_ref, o_ref):
        cv = c_ref[...]
        s = cv * _sigmoid(cv)
        for l in range(nl):
            o_ref[l] = jnp.dot(s, w_ref[l], precision=HI, preferred_element_type=f32)

    return pl.pallas_call(body, in_specs=[VMEM, VMEM], out_specs=VMEM, out_shape=S((nl, CROWS, cols), f32),
                          compiler_params=pltpu.CompilerParams(vmem_limit_bytes=VMEM_LIMIT_BYTES), name=name)(c16, modw)


def mod_bwd(c16, modw, dm_sh, dm_all, name):
    nl, _, cols = modw.shape

    def body(c_ref, w_ref, dm_ref, dmall_ref, dw_ref, dc_ref, db_ref):
        cv = c_ref[...]
        sg = _sigmoid(cv)
        s = cv * sg
        ds_dc = sg * (1.0 + cv * (1.0 - sg))
        is_ctx = lax.broadcasted_iota(jnp.int32, (CROWS, D), 0) >= N_DEV
        dc = jnp.zeros((1, D), f32)
        for l in range(nl):
            dm = dm_ref[l]
            dw_ref[l] = lax.dot_general(s, dm, (((0,), (0,)), ((), ())), precision=HI, preferred_element_type=f32)
            dsv = lax.dot_general(dm, w_ref[l], (((1,), (1,)), ((), ())), precision=HI, preferred_element_type=f32)
            dc = dc + jnp.sum(jnp.where(is_ctx, dsv * ds_dc, 0.0), axis=0, keepdims=True)
            db_ref[pl.ds(l, 1), :] = jnp.sum(dmall_ref[l], axis=0, keepdims=True)
        dc_ref[...] = dc

    return pl.pallas_call(
        body, in_specs=[VMEM, VMEM, VMEM, VMEM], out_specs=[VMEM, VMEM, VMEM],
        out_shape=[S(modw.shape, f32), S((1, D), f32), S((nl, 6 * D), f32)],
        compiler_params=pltpu.CompilerParams(vmem_limit_bytes=VMEM_LIMIT_BYTES), name=name)(c16, modw, dm_sh, dm_all)


def adamw(w, g, m, v, name):
    R, C = w.shape
    rb = R if R <= 512 else max(r_ for r_ in range(8, 513, 8) if R % r_ == 0)
    bc1 = 1.0 - ADAM_B1 ** ADAM_STEP
    bc2 = 1.0 - ADAM_B2 ** ADAM_STEP

    def body(w_ref, g_ref, m_ref, v_ref, d_ref, nm_ref, nv_ref):
        gv = g_ref[...]
        m_new = ADAM_B1 * m_ref[...] + (1.0 - ADAM_B1) * gv
        v_new = ADAM_B2 * v_ref[...] + (1.0 - ADAM_B2) * (gv * gv)
        m_hat = m_new / bc1
        v_hat = v_new / bc2
        d_ref[...] = -ADAM_LR * (m_hat / (jnp.sqrt(v_hat) + ADAM_EPS) + ADAM_WD * w_ref[...])
        nm_ref[...] = m_new
        nv_ref[...] = v_new

    blk = pl.BlockSpec((rb, C), lambda i: (i, 0))
    return pl.pallas_call(body, grid=(R // rb,), in_specs=[blk] * 4, out_specs=[blk] * 3,
                          out_shape=[S((R, C), f32)] * 3, compiler_params=_cparams("parallel"), name=name)(w, g, m, v)


def _me():
    return lax.axis_index("x"), lax.axis_index("y"), lax.axis_index("c")


def allgather_small(x, name, with_sum=False):
    r, w = x.shape

    def body(x_ref, *refs):
        if with_sum:
            out_ref, sum_ref, send_sems, recv_sems = refs
        else:
            out_ref, send_sems, recv_sems = refs
        mx, my, mc = _me()
        me = 4 * mx + 2 * my + mc
        out_ref[me] = x_ref[...]
        peers = []
        for k in range(1, N_DEV):
            kx, ky, kc = (k >> 2) & 1, (k >> 1) & 1, k & 1
            peers.append((mx + kx - 2 * mx * kx, my + ky - 2 * my * ky, mc + kc - 2 * mc * kc))
        copies = []
        for k, peer in enumerate(peers):
            cp = pltpu.make_async_remote_copy(src_ref=x_ref, dst_ref=out_ref.at[me], send_sem=send_sems.at[k],
                                              recv_sem=recv_sems.at[k], device_id=peer, device_id_type=MESH)
            cp.start()
            copies.append(cp)
        for k, (px, py, pc) in enumerate(peers):
            pltpu.make_async_remote_copy(src_ref=x_ref, dst_ref=out_ref.at[4 * px + 2 * py + pc], send_sem=send_sems.at[k],
                                         recv_sem=recv_sems.at[k], device_id=(px, py, pc), device_id_type=MESH).wait_recv()
        for cp in copies:
            cp.wait_send()
        if with_sum:
            acc = out_ref[0]
            for j in range(1, N_DEV):
                acc = acc + out_ref[j]
            sum_ref[...] = acc

    out_shape = [S((N_DEV, r, w), f32)] + ([S((r, w), f32)] if with_sum else [])
    outs = pl.pallas_call(
        body, in_specs=[VMEM], out_specs=[VMEM] * len(out_shape), out_shape=out_shape,
        scratch_shapes=[pltpu.SemaphoreType.DMA((N_DEV - 1,)), pltpu.SemaphoreType.DMA((N_DEV - 1,))],
        compiler_params=pltpu.CompilerParams(vmem_limit_bytes=VMEM_LIMIT_BYTES), name=name)(x)
    return outs if with_sum else outs[0]


def add_own(g, r, core, name):
    _, _, R, W = g.shape
    rb = R if R <= 512 else max(r_ for r_ in range(16, 513, 16) if R % r_ == 0)

    def body(core_ref, a_ref, b_ref, o_ref):
        o_ref[...] = (a_ref[...].astype(f32) + b_ref[...].astype(f32)).astype(bf16)

    blk = pl.BlockSpec((None, rb, W), lambda k, i, core_ref: (k, i, 0))
    gs = pltpu.PrefetchScalarGridSpec(
        num_scalar_prefetch=1, grid=(4, R // rb),
        in_specs=[pl.BlockSpec((None, None, rb, W), lambda k, i, core_ref: (k, core_ref[0], i, 0)), blk], out_specs=blk)
    return pl.pallas_call(body, grid_spec=gs, out_shape=S((4, R, W), bf16),
                          compiler_params=_cparams("parallel", "parallel"), name=name)(core, g, r)


def sum_adamw(recv, w, m, v, layer, name):
    _, R, W = recv.shape
    rb = R if R <= 256 else max(r_ for r_ in range(16, 257, 16) if R % r_ == 0)
    bc1 = 1.0 - ADAM_B1 ** ADAM_STEP
    bc2 = 1.0 - ADAM_B2 ** ADAM_STEP

    def body(r_ref, w_ref, m_ref, v_ref, g_ref, d_ref, nm_ref, nv_ref):
        gv = r_ref[0].astype(f32)
        for k in range(1, 4):
            gv = gv + r_ref[k].astype(f32)
        m_new = ADAM_B1 * m_ref[...] + (1.0 - ADAM_B1) * gv
        v_new = ADAM_B2 * v_ref[...] + (1.0 - ADAM_B2) * (gv * gv)
        g_ref[...] = gv
        d_ref[...] = -ADAM_LR * ((m_new / bc1) / (jnp.sqrt(v_new / bc2) + ADAM_EPS) + ADAM_WD * w_ref[...])
        nm_ref[...] = m_new
        nv_ref[...] = v_new

    blk = pl.BlockSpec((rb, W), lambda i: (i, 0))
    wblk = blk if layer is None else pl.BlockSpec((None, rb, W), lambda i: (layer, i, 0))
    return pl.pallas_call(body, grid=(R // rb,), in_specs=[pl.BlockSpec((4, rb, W), lambda i: (0, i, 0)), wblk, wblk, wblk],
                          out_specs=[blk] * 4, out_shape=[S((R, W), f32)] * 4,
                          compiler_params=_cparams("parallel"), name=name)(recv, w, m, v)


def sum_rows(a, name):
    K, R, W = a.shape
    rb = _pick(R, (512, 256, 128, 64, 32, 16))

    def body(a_ref, o_ref):
        acc = a_ref[0].astype(f32)
        for k in range(1, K):
            acc = acc + a_ref[k].astype(f32)
        o_ref[...] = acc

    return pl.pallas_call(body, grid=(R // rb,), in_specs=[pl.BlockSpec((K, rb, W), lambda i: (0, i, 0))],
                          out_specs=pl.BlockSpec((rb, W), lambda i: (i, 0)), out_shape=S((R, W), f32),
                          compiler_params=_cparams("parallel"), name=name)(a)


DMA = pltpu.SemaphoreType.DMA


class GatherExchange:
    def __init__(self, arrays):
        self.arrays = list(arrays)
        self.na = len(self.arrays)
        self.out_shape = [S((N_DEV,) + a.shape, a.dtype) for a in self.arrays]
        self.scratch = [DMA((7 * self.na,)), DMA((7 * self.na,)), DMA((self.na,))]

    def ops(self, x_refs, out_refs, sems):
        send_sems, recv_sems, local_sems = sems
        na = self.na
        x, y, c = _me()
        me, sibling = (x, y, c), (x, y, 1 - c)
        chips = [(1 - x, y), (x, 1 - y), (1 - x, 1 - y)]

        def rows(a, px, py, pc):
            return out_refs[a].at[4 * px + 2 * py + pc]

        def copy(a, k, block, to, src=None):
            return pltpu.make_async_remote_copy(
                src_ref=rows(a, *block) if src is None else src, dst_ref=rows(a, *block),
                send_sem=send_sems.at[7 * a + k], recv_sem=recv_sems.at[7 * a + k], device_id=to, device_id_type=MESH)

        def local(a):
            return pltpu.make_async_copy(x_refs[a], rows(a, *me), local_sems.at[a])

        def first(a):
            return [copy(a, 0, me, sibling, src=x_refs[a])] + [copy(a, 1 + j, me, (*chip, c), src=x_refs[a])
                                                                for j, chip in enumerate(chips)]

        def start():
            for a in range(na):
                local(a).start()
                for cp in first(a):
                    cp.start()

        def mid():
            for a in range(na):
                for j, chip in enumerate(chips):
                    copy(a, 1 + j, (*chip, c), me).wait_recv()
                    copy(a, 4 + j, (*chip, c), sibling).start()

        def finish():
            for a in range(na):
                copy(a, 0, sibling, me).wait_recv()
                for j, chip in enumerate(chips):
                    copy(a, 4 + j, (*chip, 1 - c), me).wait_recv()
                for cp in first(a) + [copy(a, 4 + j, (*chip, c), sibling) for j, chip in enumerate(chips)]:
                    cp.wait_send()
                local(a).wait()

        return start, mid, finish


class SiblingExchange:
    def __init__(self, arrays):
        self.arrays = list(arrays)
        self.na = len(self.arrays)
        self.out_shape = [S((4,) + g.shape[2:], g.dtype) for g in self.arrays]
        self.scratch = [DMA((self.na,)), DMA((self.na,))]

    def ops(self, g_refs, out_refs, sems):
        send_sems, recv_sems = sems
        x, y, c = _me()

        def copy(a):
            return pltpu.make_async_remote_copy(src_ref=g_refs[a].at[:, 1 - c], dst_ref=out_refs[a],
                                                send_sem=send_sems.at[a], recv_sem=recv_sems.at[a],
                                                device_id=(x, y, 1 - c), device_id_type=MESH)

        def start():
            for a in range(self.na):
                copy(a).start()

        def finish():
            for a in range(self.na):
                copy(a).wait()

        return start, None, finish


class ChipsExchange:
    def __init__(self, arrays):
        self.arrays = list(arrays)
        self.na = len(self.arrays)
        self.out_shape = [S(p.shape, p.dtype) for p in self.arrays]
        self.scratch = [DMA((3 * self.na,)), DMA((3 * self.na,)), DMA((self.na,))]

    def ops(self, p_refs, out_refs, sems):
        send_sems, recv_sems, local_sems = sems
        x, y, c = _me()
        mine = 2 * x + y
        chips = [(1 - x, y), (x, 1 - y), (1 - x, 1 - y)]

        def local(a):
            return pltpu.make_async_copy(p_refs[a].at[mine], out_refs[a].at[mine], local_sems.at[a])

        def send(a, j):
            px, py = chips[j]
            return pltpu.make_async_remote_copy(src_ref=p_refs[a].at[2 * px + py], dst_ref=out_refs[a].at[mine],
                                                send_sem=send_sems.at[3 * a + j], recv_sem=recv_sems.at[3 * a + j],
                                                device_id=(px, py, c), device_id_type=MESH)

        def recv(a, j):
            px, py = chips[j]
            return pltpu.make_async_remote_copy(src_ref=p_refs[a].at[mine], dst_ref=out_refs[a].at[2 * px + py],
                                                send_sem=send_sems.at[3 * a + j], recv_sem=recv_sems.at[3 * a + j],
                                                device_id=(px, py, c), device_id_type=MESH)

        def start():
            for a in range(self.na):
                local(a).start()
                for j in range(3):
                    send(a, j).start()

        def finish():
            for a in range(self.na):
                for j in range(3):
                    recv(a, j).wait_recv()
                for j in range(3):
                    send(a, j).wait_send()
                local(a).wait()

        return start, None, finish


def exchange(ex, name):
    na = ex.na

    def body(*refs):
        start, mid, finish = ex.ops(refs[:na], refs[na:2 * na], refs[2 * na:])
        start()
        if mid is not None:
            mid()
        finish()

    return pl.pallas_call(body, in_specs=[ANY] * na, out_specs=[ANY] * na, out_shape=ex.out_shape,
                          scratch_shapes=ex.scratch, name=name)(*ex.arrays)


def _host_call(body, grid, in_specs, out_specs, out_shape, scratch_shapes, sem, name, args, hosted):
    if hosted is None:
        res = pl.pallas_call(body, grid=grid, in_specs=in_specs, out_specs=out_specs, out_shape=out_shape,
                             scratch_shapes=scratch_shapes, compiler_params=_cparams(*sem), name=name)(*args)
        return res, None
    n_in, n_out, n_sc, na = len(in_specs), len(out_shape), len(scratch_shapes), hosted.na
    nsteps = 1
    for g_ in grid:
        nsteps *= g_
    mid_step = (3 * nsteps) // 4
    i1 = n_in + na
    i2 = i1 + n_out
    i3 = i2 + na
    i4 = i3 + n_sc

    def wrapped(*refs):
        step = pl.program_id(0)
        for ax in range(1, len(grid)):
            step = step * grid[ax] + pl.program_id(ax)
        start, mid, finish = hosted.ops(refs[n_in:i1], refs[i2:i3], refs[i4:])
        pl.when(step == 0)(start)
        if mid is not None:
            pl.when(step == mid_step)(mid)
        body(*refs[:n_in], *refs[i1:i2], *refs[i3:i4])
        pl.when(step == nsteps - 1)(finish)

    res = pl.pallas_call(
        wrapped, grid=grid, in_specs=list(in_specs) + [ANY] * na, out_specs=list(out_specs) + [ANY] * na,
        out_shape=list(out_shape) + hosted.out_shape, scratch_shapes=list(scratch_shapes) + hosted.scratch,
        compiler_params=_cparams(*(("arbitrary",) * len(grid))), name=name)(*args, *hosted.arrays)
    return res[:n_out], res[n_out:]


PACK_ALIGN = 16 * PACK_W


def _pad_to(v, mult):
    n = v.shape[-1]
    extra = (-n) % mult
    if extra == 0:
        return v
    return jnp.concatenate([v, jnp.zeros(v.shape[:-1] + (extra,), v.dtype)], axis=-1)


def _f32_as_bf16_pairs(v):
    return lax.bitcast_convert_type(v.reshape(-1), bf16).reshape(-1)


def _bf16_pairs_as_f32(v):
    return lax.bitcast_convert_type(v.reshape(v.shape[:-1] + (v.shape[-1] // 2, 2)), f32)


def _col_shards(gw):
    lead = gw.shape[:-1]
    n = gw.shape[-1] // N_DEV
    t = gw.reshape(lead + (N_DEV, n))
    t = jnp.moveaxis(t, -2, 0)
    return t.reshape(N_DEV, -1)


def kernel(x, c, ctx, c_ctx, mod_w, mod_b, norm1_w, norm2_w, ssd_w_in, ssd_conv_w, ssd_conv_b, ssd_dt_bias, ssd_a_log, ssd_d, ssd_norm_w, ssd_w_out, conf_w_pw1, conf_b_pw1, conf_w_dw, conf_b_dw, conf_ln_w, conf_ln_b, conf_w_pw2, conf_b_pw2, ffn_w_up, ffn_conv_w, ffn_conv_b, ffn_w_down, final_norm_w, loss_target, m_c_ctx, m_mod_w, m_mod_b, m_norm1_w, m_norm2_w, m_ssd_w_in, m_ssd_conv_w, m_ssd_conv_b, m_ssd_dt_bias, m_ssd_a_log, m_ssd_d, m_ssd_norm_w, m_ssd_w_out, m_conf_w_pw1, m_conf_b_pw1, m_conf_w_dw, m_conf_b_dw, m_conf_ln_w, m_conf_ln_b, m_conf_w_pw2, m_conf_b_pw2, m_ffn_w_up, m_ffn_conv_w, m_ffn_conv_b, m_ffn_w_down, m_final_norm_w, v_c_ctx, v_mod_w, v_mod_b, v_norm1_w, v_norm2_w, v_ssd_w_in, v_ssd_conv_w, v_ssd_conv_b, v_ssd_dt_bias, v_ssd_a_log, v_ssd_d, v_ssd_norm_w, v_ssd_w_out, v_conf_w_pw1, v_conf_b_pw1, v_conf_w_dw, v_conf_b_dw, v_conf_ln_w, v_conf_ln_b, v_conf_w_pw2, v_conf_b_pw2, v_ffn_w_up, v_ffn_conv_w, v_ffn_conv_b, v_ffn_w_down, v_final_norm_w):
    mx, my, mc = _me()
    me = 4 * mx + 2 * my + mc
    L = x.shape[1]
    LC = ctx.shape[1]
    T = LC + L
    w_in_cols = ssd_w_in.shape[2] * N_DEV
    n_dt = w_in_cols - DI - CONVD

    small = [c[0], ssd_conv_w[0], conf_b_pw1[0], conf_w_dw[0], conf_b_dw[0], conf_ln_w[0], conf_ln_b[0], conf_b_pw2[0],
             ffn_conv_w]
    parts = [_f32_as_bf16_pairs(t) for t in small]
    sizes = [p.shape[0] for p in parts]
    small_flat = _pad_to(jnp.concatenate(parts), PACK_ALIGN).reshape(-1, PACK_W)
    w_in, small_g = exchange(GatherExchange([ssd_w_in[0].astype(bf16), small_flat]), "gather_first")
    gather_in_conv = GatherExchange([ssd_w_out[0].astype(bf16), conf_w_pw2[0].astype(bf16), conf_w_pw1[0].astype(bf16)])
    gather_in_scan = GatherExchange([ffn_w_up.astype(bf16), ffn_w_down.astype(bf16)])
    small_g = small_g.reshape(N_DEV, -1)
    offs = [0]
    for s_ in sizes:
        offs.append(offs[-1] + s_)
    sm = [_bf16_pairs_as_f32(small_g[:, offs[i]:offs[i + 1]]) for i in range(len(sizes))]

    def cols(pc, K):
        return jnp.moveaxis(pc.reshape(N_DEV, K, -1), 0, 1).reshape(K, -1)

    c_all = sm[0]
    conv_w5 = cols(sm[1], 5)
    b_pw1 = sm[2].reshape(1, 2 * D)
    w_dw = cols(sm[3], CONF_K)
    b_dw, ln_w, ln_b, b_pw2 = (sm[i].reshape(1, D) for i in (4, 5, 6, 7))
    fcw = sm[8].reshape(N_DEV, 2, 9, FH // N_DEV)
    ffn_cw = [cols(fcw[:, i].reshape(N_DEV, -1), 9) for i in range(2)]
    in_segs = (DI, CONVD, n_dt)
    up_segs = (FH, FH)
    pw1_segs = (D, D)

    c16 = jnp.concatenate([c_all, jnp.broadcast_to(c_ctx[None, :], (N_DEV, D))], axis=0)
    m_sh = mod_fwd(c16, mod_w, "mod_fwd")
    mod_cols = mod_w.shape[2]
    m_all = allgather_small(m_sh.reshape(2 * CROWS, mod_cols), "gather_mod")
    m_all = jnp.moveaxis(m_all.reshape(N_DEV, 2, CROWS, mod_cols), 0, 2).reshape(2, CROWS, 6 * D) + mod_b[:, None, :]
    m_lat = lax.dynamic_index_in_dim(m_all, me, axis=1, keepdims=False).reshape(2, 6, 1, D)
    m_ctx = m_all[:, N_DEV].reshape(2, 6, 1, D)
    zero_row = jnp.zeros((1, D), f32)

    def ffn_fwd(h, i, tag):
        a2 = modnorm_fwd(h, norm2_w[i][None], m_lat[i, 4][None], m_lat[i, 3][None], 0, f"ffn{tag}_norm")
        val, gate = smm_fwd(a2, w_up, i, up_segs, f"ffn{tag}_up")
        act = ffn_gate_fwd(val, gate, ffn_cw[i], ffn_conv_b[i][None], f"ffn{tag}_gate")
        o2 = matmul(act, w_down[i], "nn", f32, f"ffn{tag}_down")
        h_new = resgate_fwd(h, o2, m_lat[i, 5], zero_row, f"ffn{tag}_res")
        return h_new, (a2, val, gate, act, o2)

    def ffn_bwd(dh, h, i, saved, tag):
        a2, val, gate, act, o2 = saved
        do2, dg2, _ = resgate_bwd(dh, o2, m_lat[i, 5], zero_row, f"ffn{tag}_res_bwd")
        g_down = matmul(act, do2, "tn", bf16, f"ffn{tag}_down_dw")
        dact = matmul(do2, w_down[i], "nt", bf16, f"ffn{tag}_down_dx")
        dval, dgate, dcw, dcb = ffn_gate_bwd(val, gate, ffn_cw[i], ffn_conv_b[i][None], dact, f"ffn{tag}_gate_bwd")
        g_up = smm_dw(a2, [dval, dgate], FH // 4, up_segs, 2, f"ffn{tag}_up_dw")
        da2, _ = smm_dx([dval, dgate], w_up, i, up_segs, bf16, f"ffn{tag}_up_dx")
        dh_in, dn2, dsc2, dsh2 = modnorm_bwd(h, norm2_w[i][None], m_lat[i, 4][None], m_lat[i, 3][None], da2, dh, 0,
                                             f"ffn{tag}_norm_bwd")
        return dh_in, dict(w_up=g_up, w_down=g_down, conv_w=dcw, conv_b=dcb, norm2=dn2, sh2=dsh2[0], sc2=dsc2[0], g2=dg2)

    nctx = LC // Q
    h0 = jnp.concatenate([ctx[0], x[0]], axis=0)
    sc0 = jnp.stack([m_ctx[0, 1], m_lat[0, 1]])
    sh0 = jnp.stack([m_ctx[0, 0], m_lat[0, 0]])
    a0 = modnorm_fwd(h0, norm1_w[0][None], sc0, sh0, LC // TB, "ssd_norm")
    z, xbc_pre, dt_raw = smm_fwd(a0, w_in, None, in_segs, "ssd_in")
    segs = ((0, LC), (LC, L))
    xbc, (w_out_g, w_pw2_g, w_pw1) = ssd_conv_fwd(xbc_pre, conv_w5, ssd_conv_b, segs, "ssd_conv", gather_in_conv)
    w_out = w_out_g.reshape(DI, D)
    w_pw2 = w_pw2_g.reshape(D, D)
    dt4 = dt_raw[:, :n_dt].reshape(T, 2, G, HPG)
    dtc = jnp.transpose(dt4, (1, 2, 0, 3))
    dtr = jnp.transpose(dt4, (1, 2, 3, 0))
    bias3 = ssd_dt_bias[0].reshape(2, G, HPG)
    alog3 = ssd_a_log[0].reshape(2, G, HPG)
    bc_, br_ = bias3[:, :, None, :], bias3[:, :, :, None]
    alc, alr = alog3[:, :, None, :], alog3[:, :, :, None]
    (y2, s_in_all), (w_up, w_down_g) = ssd_scan_fwd(xbc, dtc, dtr, bc_, br_, alc, alr, nctx, "ssd_scan", gather_in_scan)
    w_down = [w_down_g[:, i].reshape(FH, D) for i in range(2)]
    dexp = jnp.repeat(ssd_d[0], P)[None, :]
    yn = ssd_gate_fwd(y2, xbc, z, dexp, ssd_norm_w, LC // GTB, "ssd_gate")
    o_ssd = matmul(yn, w_out, "nn", f32, "ssd_out")
    hx = x[0]
    h1 = resgate_fwd(hx, o_ssd, m_lat[0, 2], zero_row, "ssd_res")
    h2, ffn0_saved = ffn_fwd(h1, 0, "0")

    a1 = modnorm_fwd(h2, norm1_w[1][None], m_lat[1, 1][None], m_lat[1, 0][None], 0, "conf_norm")
    pa, pg = smm_fwd(a1, w_pw1, None, pw1_segs, "conf_pw1")
    dwc = conf_glu_conv_fwd(pa, pg, b_pw1, w_dw, b_dw, "conf_conv")
    s1 = ln_silu_fwd(dwc, ln_w, ln_b, "conf_ln")
    o_conf = matmul(s1, w_pw2, "nn", f32, "conf_pw2")
    h3 = resgate_fwd(h2, o_conf, m_lat[1, 2], b_pw2, "conf_res")
    h4, ffn1_saved = ffn_fwd(h3, 1, "1")

    loss_part, dh4, g_final = final_loss(h4, final_norm_w[None], loss_target[0], "loss_head")
    dh3, gf1 = ffn_bwd(dh4, h3, 1, ffn1_saved, "1")

    do_conf, dg1_1, g_b_pw2 = resgate_bwd(dh3, o_conf, m_lat[1, 2], b_pw2, "conf_res_bwd")
    g_pw2 = matmul(s1, do_conf, "tn", bf16, "conf_pw2_dw")
    ds1 = matmul(do_conf, w_pw2, "nt", bf16, "conf_pw2_dx")
    ddwc, g_ln_w, g_ln_b = ln_silu_bwd(dwc, ln_w, ln_b, ds1, "conf_ln_bwd")
    dpa, dpg, dba, dbg, g_w_dw, g_b_dw = conf_glu_conv_bwd(pa, pg, b_pw1, w_dw, ddwc, "conf_conv_bwd")
    g_b_pw1 = jnp.concatenate([dba, dbg], axis=1)
    g_pw1 = smm_dw(a1, [dpa, dpg], 2 * D // N_DEV, pw1_segs, 1, "conf_pw1_dw")
    da1, _ = smm_dx([dpa, dpg], w_pw1, None, pw1_segs, bf16, "conf_pw1_dx")
    dh2, g_n1_1, dsc1_1, dsh1_1 = modnorm_bwd(h2, norm1_w[1][None], m_lat[1, 1][None], m_lat[1, 0][None], da1, dh3, 0,
                                              "conf_norm_bwd")
    dh1, gf0 = ffn_bwd(dh2, h1, 0, ffn0_saved, "0")

    do_ssd, dg1_0, _ = resgate_bwd(dh1, o_ssd, m_lat[0, 2], zero_row, "ssd_res_bwd")
    g_w_out = matmul(yn, do_ssd, "tn", bf16, "ssd_out_dw")
    dyn = matmul(do_ssd, w_out, "nt", bf16, "ssd_out_dx")
    core = mc.reshape(1).astype(jnp.int32)

    def by_device(t):
        return t.reshape((4, 2, -1, t.shape[-1]))

    early = [by_device(t) for t in (gf1["w_up"], gf1["w_down"], g_pw2, g_pw1, gf0["w_up"], gf0["w_down"], g_w_out)]
    (dy, dx_skip, dz, g_dexp, g_ssd_norm), early_sib = ssd_gate_bwd(
        y2, xbc, z, dexp, ssd_norm_w, dyn, LC // GTB, "ssd_gate_bwd", SiblingExchange(early))
    early_part = [add_own(t, r_, core, f"reduce_add{i}") for i, (t, r_) in enumerate(zip(early, early_sib))]
    (dxbc2, ddtc, ddtr, dbc, dbr, dalc, dalr), early_red = ssd_scan_bwd(
        xbc, dtc, dtr, bc_, br_, alc, alr, s_in_all, dy, nctx, "ssd_scan_bwd", ChipsExchange(early_part))
    ddt = (jnp.transpose(ddtc, (2, 0, 1, 3)) + jnp.transpose(ddtr, (3, 0, 1, 2))).reshape(T, n_dt)
    g_dt_bias = (dbc[:, :, 0, :] + dbr[:, :, :, 0]).reshape(2, NH_SSD)
    g_a_log = (dalc[:, :, 0, :] + dalr[:, :, :, 0]).reshape(2, NH_SSD)
    g_ssd_d = g_dexp[0, :NH_SSD]
    du, g_conv_w5, g_conv_b5 = ssd_conv_bwd(xbc_pre, conv_w5, ssd_conv_b, dxbc2, dx_skip, segs, "ssd_conv_bwd")
    ddt_p = _pad_to(ddt, 128).astype(bf16)
    g_w_in = smm_dw(a0, [dz, du, ddt_p], w_in.shape[-1], in_segs, 2, "ssd_in_dw")
    g_ffn_cw = jnp.stack([gf0["conv_w"], gf1["conv_w"]])
    small_shards = [_col_shards(t) for t in (g_conv_w5, g_b_pw1, g_w_dw, g_b_dw, g_ln_w, g_ln_b, g_b_pw2, g_ffn_cw)]
    gsizes = [s_.shape[1] for s_ in small_shards]
    g_small = _pad_to(jnp.concatenate(small_shards, axis=1), PACK_ALIGN).astype(bf16)
    late = [by_device(g_w_in), by_device(g_small.reshape(N_DEV, -1, PACK_W))]
    da0, late_sib = smm_dx([dz, du, ddt_p], w_in, None, in_segs, f32, "ssd_in_dx", SiblingExchange(late))
    late_part = [add_own(t, r_, core, f"reduce_add_late{i}") for i, (t, r_) in enumerate(zip(late, late_sib))]
    late_red = exchange(ChipsExchange(late_part), "reduce_chips_late")
    dres0 = jnp.concatenate([jnp.zeros((LC, D), f32), dh1], axis=0)
    dh0, g_n1_0, dsc1_0, dsh1_0 = modnorm_bwd(h0, norm1_w[0][None], sc0, sh0, da0, dres0, LC // TB, "ssd_norm_bwd")
    grad_x = dh0[LC:][None]

    zeros_d = jnp.zeros((1, D), f32)
    dm_lat = jnp.stack([
        jnp.concatenate([dsh1_0[1], dsc1_0[1], dg1_0, gf0["sh2"], gf0["sc2"], gf0["g2"]], axis=1),
        jnp.concatenate([dsh1_1[0], dsc1_1[0], dg1_1, gf1["sh2"], gf1["sc2"], gf1["g2"]], axis=1)])
    dm_ctx = jnp.stack([
        jnp.concatenate([dsh1_0[0], dsc1_0[0]] + [zeros_d] * 4, axis=1), jnp.zeros((1, 6 * D), f32)])
    dm_mine = jnp.concatenate([dm_lat.reshape(2, 6 * D), dm_ctx.reshape(2, 6 * D),
                               jnp.zeros((4, 6 * D), f32)], axis=0)
    dm_g = allgather_small(dm_mine, "gather_dmod")
    dm_all = jnp.concatenate([jnp.moveaxis(dm_g[:, 0:2], 0, 1), jnp.moveaxis(dm_g[:, 2:4], 0, 1)], axis=1)
    dm_sh = lax.dynamic_slice_in_dim(dm_all, me * mod_cols, mod_cols, axis=2)
    g_mod_w, g_cctx_part, g_mod_b = mod_bwd(c16, mod_w, dm_sh, dm_all, "mod_bwd")

    rep = [jnp.stack([g_n1_0[0], g_n1_1[0]]), jnp.stack([gf0["norm2"][0], gf1["norm2"][0]]), g_conv_b5, g_dt_bias, g_a_log,
           g_ssd_d, g_ssd_norm, jnp.stack([gf0["conv_b"][0], gf1["conv_b"][0]]), g_final, g_cctx_part, loss_part[:, :1]]
    rep_sizes = [r_.size for r_ in rep]
    rep_flat = _pad_to(jnp.concatenate([r_.reshape(-1) for r_ in rep]), 8 * PACK_W).reshape(-1, PACK_W)
    _, rep_sum = allgather_small(rep_flat, "reduce_replicated", with_sum=True)
    rep_sum = rep_sum.reshape(-1)
    roffs = [0]
    for s_ in rep_sizes:
        roffs.append(roffs[-1] + s_)
    rp = [rep_sum[roffs[i]:roffs[i + 1]] for i in range(len(rep_sizes))]
    loss = rp[10].reshape(())

    r_up1, r_down1, r_pw2, r_pw1, r_up0, r_down0, r_out = early_red
    r_in, r_small = late_red
    from_chips = [r_in, r_up0, r_up1, r_pw1, r_out, r_down0, r_down1, r_pw2]
    g_flat = sum_rows(r_small, "reduce_sum_small").reshape(-1)
    goffs = [0]
    for s_ in gsizes:
        goffs.append(goffs[-1] + s_)
    gs = [g_flat[goffs[i]:goffs[i + 1]] for i in range(len(gsizes))]

    big = {}
    big["ssd_w_in"] = sum_adamw(from_chips[0], ssd_w_in[0], m_ssd_w_in[0], v_ssd_w_in[0], None, "adamw_ssd_w_in")
    up = [sum_adamw(from_chips[1 + i], ffn_w_up, m_ffn_w_up, v_ffn_w_up, i, f"adamw_ffn_w_up{i}") for i in range(2)]
    big["ffn_w_up"] = tuple(jnp.stack([up[0][k], up[1][k]]) for k in range(4))
    big["conf_w_pw1"] = sum_adamw(from_chips[3], conf_w_pw1[0], m_conf_w_pw1[0], v_conf_w_pw1[0], None, "adamw_conf_w_pw1")
    big["ssd_w_out"] = sum_adamw(from_chips[4], ssd_w_out[0], m_ssd_w_out[0], v_ssd_w_out[0], None, "adamw_ssd_w_out")
    dn = [sum_adamw(from_chips[5 + i], ffn_w_down, m_ffn_w_down, v_ffn_w_down, i, f"adamw_ffn_w_down{i}") for i in range(2)]
    big["ffn_w_down"] = tuple(jnp.stack([dn[0][k], dn[1][k]]) for k in range(4))
    big["conf_w_pw2"] = sum_adamw(from_chips[7], conf_w_pw2[0], m_conf_w_pw2[0], v_conf_w_pw2[0], None, "adamw_conf_w_pw2")
    grads = {
        "c_ctx": rp[9], "mod_w": g_mod_w, "mod_b": g_mod_b, "norm1_w": rp[0], "norm2_w": rp[1],
        "ssd_conv_w": gs[0], "ssd_conv_b": rp[2], "ssd_dt_bias": rp[3], "ssd_a_log": rp[4], "ssd_d": rp[5],
        "ssd_norm_w": rp[6], "conf_b_pw1": gs[1], "conf_w_dw": gs[2],
        "conf_b_dw": gs[3], "conf_ln_w": gs[4], "conf_ln_b": gs[5], "conf_b_pw2": gs[6],
        "ffn_conv_w": gs[7], "ffn_conv_b": rp[7], "final_norm_w": rp[8],
    }
    weights = dict(c_ctx=c_ctx, mod_w=mod_w, mod_b=mod_b, norm1_w=norm1_w, norm2_w=norm2_w, ssd_w_in=ssd_w_in, ssd_conv_w=ssd_conv_w, ssd_conv_b=ssd_conv_b, ssd_dt_bias=ssd_dt_bias, ssd_a_log=ssd_a_log, ssd_d=ssd_d, ssd_norm_w=ssd_norm_w, ssd_w_out=ssd_w_out, conf_w_pw1=conf_w_pw1, conf_b_pw1=conf_b_pw1, conf_w_dw=conf_w_dw, conf_b_dw=conf_b_dw, conf_ln_w=conf_ln_w, conf_ln_b=conf_ln_b, conf_w_pw2=conf_w_pw2, conf_b_pw2=conf_b_pw2, ffn_w_up=ffn_w_up, ffn_conv_w=ffn_conv_w, ffn_conv_b=ffn_conv_b, ffn_w_down=ffn_w_down, final_norm_w=final_norm_w)
    m_in = dict(c_ctx=m_c_ctx, mod_w=m_mod_w, mod_b=m_mod_b, norm1_w=m_norm1_w, norm2_w=m_norm2_w, ssd_w_in=m_ssd_w_in, ssd_conv_w=m_ssd_conv_w, ssd_conv_b=m_ssd_conv_b, ssd_dt_bias=m_ssd_dt_bias, ssd_a_log=m_ssd_a_log, ssd_d=m_ssd_d, ssd_norm_w=m_ssd_norm_w, ssd_w_out=m_ssd_w_out, conf_w_pw1=m_conf_w_pw1, conf_b_pw1=m_conf_b_pw1, conf_w_dw=m_conf_w_dw, conf_b_dw=m_conf_b_dw, conf_ln_w=m_conf_ln_w, conf_ln_b=m_conf_ln_b, conf_w_pw2=m_conf_w_pw2, conf_b_pw2=m_conf_b_pw2, ffn_w_up=m_ffn_w_up, ffn_conv_w=m_ffn_conv_w, ffn_conv_b=m_ffn_conv_b, ffn_w_down=m_ffn_w_down, final_norm_w=m_final_norm_w)
    v_in = dict(c_ctx=v_c_ctx, mod_w=v_mod_w, mod_b=v_mod_b, norm1_w=v_norm1_w, norm2_w=v_norm2_w, ssd_w_in=v_ssd_w_in, ssd_conv_w=v_ssd_conv_w, ssd_conv_b=v_ssd_conv_b, ssd_dt_bias=v_ssd_dt_bias, ssd_a_log=v_ssd_a_log, ssd_d=v_ssd_d, ssd_norm_w=v_ssd_norm_w, ssd_w_out=v_ssd_w_out, conf_w_pw1=v_conf_w_pw1, conf_b_pw1=v_conf_b_pw1, conf_w_dw=v_conf_w_dw, conf_b_dw=v_conf_b_dw, conf_ln_w=v_conf_ln_w, conf_ln_b=v_conf_ln_b, conf_w_pw2=v_conf_w_pw2, conf_b_pw2=v_conf_b_pw2, ffn_w_up=v_ffn_w_up, ffn_conv_w=v_ffn_conv_w, ffn_conv_b=v_ffn_conv_b, ffn_w_down=v_ffn_w_down, final_norm_w=v_final_norm_w)

    out_g, out_d, out_m, out_v = [], [], [], []
    for name_, w_ in weights.items():
        shape = w_.shape
        if name_ in big:
            for lst, t in zip((out_g, out_d, out_m, out_v), big[name_]):
                lst.append(t.reshape(shape))
            continue
        cols2 = shape[-1] if len(shape) > 1 else shape[0]
        g2 = grads[name_].reshape(-1, cols2)
        d_, nm_, nv_ = adamw(w_.reshape(-1, cols2), g2, m_in[name_].reshape(-1, cols2), v_in[name_].reshape(-1, cols2),
                             f"adamw_{name_}")
        out_g.append(g2.reshape(shape))
        out_d.append(d_.reshape(shape))
        out_m.append(nm_.reshape(shape))
        out_v.append(nv_.reshape(shape))
    return (loss, grad_x, *out_g, *out_d, *out_m, *out_v)
```

```python
import functools

import jax
import jax.numpy as jnp
from jax import lax
from jax.experimental import pallas as pl
from jax.experimental.pallas import tpu as pltpu

f32 = jnp.float32
bf16 = jnp.bfloat16
HI = lax.Precision.HIGHEST
S = jax.ShapeDtypeStruct
MESH = pl.DeviceIdType.MESH
ANY = pl.BlockSpec(memory_space=pl.ANY)
VMEM = pl.BlockSpec(memory_space=pltpu.VMEM)

N_DEV = 8
D = 1024
DI = 2048
CONVD = 4096
FH = 2816
GRID_W = 64
Q = 128
HPG = 4
P = 64
N = 128
G = 8
GW = HPG * P
NH_SSD = G * HPG
EPS = 1e-6
ADAM_LR, ADAM_B1, ADAM_B2, ADAM_EPS, ADAM_WD, ADAM_STEP = 0.001, 0.9, 0.999, 1e-08, 0.01, 10
VMEM_LIMIT_BYTES = 56 * 1024 * 1024
PACK_W = 1024
TB = 256


def _cparams(*sem):
    return pltpu.CompilerParams(dimension_semantics=sem, vmem_limit_bytes=VMEM_LIMIT_BYTES)


def _pick(n, prefs):
    for p in prefs:
        if n % p == 0:
            return p
    return n


def _sigmoid(x):
    return 1.0 / (1.0 + jnp.exp(-x))


def _softplus(x):
    return jnp.maximum(x, 0.0) + jnp.log(1.0 + jnp.exp(-jnp.abs(x)))


def matmul(a, b, mode, out_dtype, name):
    if mode == "nn":
        (M, K), (_, Nn) = a.shape, b.shape
        bm, bn, bk = _pick(M, (512, 384, 256, 128)), Nn, K
    elif mode == "tn":
        (K, M), (_, Nn) = a.shape, b.shape
        bm, bn, bk = M, Nn, _pick(K, (256, 128))
    else:
        (M, K), (Nn, _) = a.shape, b.shape
        bm, bn, bk = _pick(M, (512, 384, 256, 128)), Nn, K
    nk = K // bk
    dims = {"nn": (((1,), (0,)), ((), ())), "tn": (((0,), (0,)), ((), ())), "nt": (((1,), (1,)), ((), ()))}[mode]

    def body(a_ref, b_ref, o_ref, acc_ref):
        k = pl.program_id(2)

        @pl.when(k == 0)
        def _():
            acc_ref[...] = jnp.zeros_like(acc_ref)

        acc_ref[...] += lax.dot_general(a_ref[...].astype(bf16), b_ref[...].astype(bf16), dims,
                                        preferred_element_type=f32)

        @pl.when(k == nk - 1)
        def _():
            o_ref[...] = acc_ref[...].astype(out_dtype)

    if mode == "nn":
        a_spec = pl.BlockSpec((bm, bk), lambda i, j, k: (i, k))
        b_spec = pl.BlockSpec((bk, bn), lambda i, j, k: (k, j))
    elif mode == "tn":
        a_spec = pl.BlockSpec((bk, bm), lambda i, j, k: (k, i))
        b_spec = pl.BlockSpec((bk, bn), lambda i, j, k: (k, j))
    else:
        a_spec = pl.BlockSpec((bm, bk), lambda i, j, k: (i, k))
        b_spec = pl.BlockSpec((bn, bk), lambda i, j, k: (j, k))
    return pl.pallas_call(
        body, grid=(M // bm, Nn // bn, nk), in_specs=[a_spec, b_spec],
        out_specs=pl.BlockSpec((bm, bn), lambda i, j, k: (i, j)),
        out_shape=S((M, Nn), out_dtype), scratch_shapes=[pltpu.VMEM((bm, bn), f32)],
        compiler_params=_cparams("parallel", "parallel", "arbitrary"), name=name,
    )(a, b)


SMM_BM = 256


def _shard_pieces(seg_widths, n):
    bounds = [0]
    for sw in seg_widths:
        bounds.append(bounds[-1] + sw)
    assert bounds[-1] == N_DEV * n, (seg_widths, n)
    out = []
    for j in range(N_DEV):
        lo, hi = j * n, (j + 1) * n
        pcs = []
        for si in range(len(seg_widths)):
            a, b = max(lo, bounds[si]), min(hi, bounds[si + 1])
            if a < b:
                pcs.append((si, a - bounds[si], a - lo, b - a))
        out.append(pcs)
    return out


def _w_spec(w, layer):
    if layer is None:
        return pl.BlockSpec(w.shape, lambda *idx: (0, 0, 0))
    return pl.BlockSpec((N_DEV, None) + w.shape[2:], lambda *idx: (0, layer, 0, 0))


def smm_fwd(a, w, layer, seg_widths, name):
    M, K = a.shape
    n = w.shape[-1]
    pieces = _shard_pieces(seg_widths, n)
    padded = [sw + (-sw) % 128 for sw in seg_widths]

    def body(a_ref, w_ref, *o_refs):
        av = a_ref[...]
        for si, sw in enumerate(seg_widths):
            if padded[si] != sw:
                o_refs[si][:, pl.ds(padded[si] - 128, 128)] = jnp.zeros((SMM_BM, 128), f32)
        for j in range(N_DEV):
            for si, soff, woff, wd in pieces[j]:
                o_refs[si][:, pl.ds(soff, wd)] = jnp.dot(av, w_ref[j, :, pl.ds(woff, wd)], preferred_element_type=f32)

    return pl.pallas_call(
        body, grid=(M // SMM_BM,), in_specs=[pl.BlockSpec((SMM_BM, K), lambda i: (i, 0)), _w_spec(w, layer)],
        out_specs=[pl.BlockSpec((SMM_BM, pw), lambda i: (i, 0)) for pw in padded],
        out_shape=[S((M, pw), f32) for pw in padded], compiler_params=_cparams("parallel"), name=name)(a, w)


def smm_dx(d_segs, w, layer, seg_widths, out_dtype, name, hosted=None):
    M = d_segs[0].shape[0]
    K, n = w.shape[-2], w.shape[-1]
    pieces = _shard_pieces(seg_widths, n)
    ns = len(d_segs)

    def body(*refs):
        d_refs, w_ref, o_ref = refs[:ns], refs[ns], refs[ns + 1]
        acc = jnp.zeros((SMM_BM, K), f32)
        for j in range(N_DEV):
            for si, soff, woff, wd in pieces[j]:
                acc = acc + lax.dot_general(d_refs[si][:, pl.ds(soff, wd)], w_ref[j, :, pl.ds(woff, wd)],
                                            (((1,), (1,)), ((), ())), preferred_element_type=f32)
        o_ref[...] = acc.astype(out_dtype)

    (out,), extra = _host_call(
        body, (M // SMM_BM,),
        [pl.BlockSpec((SMM_BM, d.shape[1]), lambda i: (i, 0)) for d in d_segs] + [_w_spec(w, layer)],
        [pl.BlockSpec((SMM_BM, K), lambda i: (i, 0))], [S((M, K), out_dtype)], [], ("parallel",), name,
        (*d_segs, w), hosted)
    return out, extra


def smm_dw(a, d_segs, n, seg_widths, ngrp, name):
    M, K = a.shape
    pieces = _shard_pieces(seg_widths, n)
    per = N_DEV // ngrp
    nI = M // SMM_BM
    ns = len(d_segs)

    def body(*refs):
        a_ref, d_refs, o_ref, acc_ref = refs[0], refs[1:1 + ns], refs[1 + ns], refs[2 + ns]
        grp = pl.program_id(0)
        i = pl.program_id(1)

        @pl.when(i == 0)
        def _():
            acc_ref[...] = jnp.zeros_like(acc_ref)

        av = a_ref[...]
        for gs in range(ngrp):
            def one_group(gs=gs):
                for jj in range(per):
                    for si, soff, woff, wd in pieces[gs * per + jj]:
                        acc_ref[jj, :, pl.ds(woff, wd)] += lax.dot_general(
                            av, d_refs[si][:, pl.ds(soff, wd)], (((0,), (0,)), ((), ())), preferred_element_type=f32)
            pl.when(grp == gs)(one_group)

        @pl.when(i == nI - 1)
        def _():
            o_ref[...] = acc_ref[...].astype(bf16)

    return pl.pallas_call(
        body, grid=(ngrp, nI),
        in_specs=[pl.BlockSpec((SMM_BM, K), lambda g, i: (i, 0))]
        + [pl.BlockSpec((SMM_BM, d.shape[1]), lambda g, i: (i, 0)) for d in d_segs],
        out_specs=pl.BlockSpec((per, K, n), lambda g, i: (g, 0, 0)), out_shape=S((N_DEV, K, n), bf16),
        scratch_shapes=[pltpu.VMEM((per, K, n), f32)],
        compiler_params=_cparams("arbitrary", "arbitrary"), name=name)(a, *d_segs)


def _modnorm_f(h, w, sc, sh):
    y = h * lax.rsqrt(jnp.mean(h * h, axis=-1, keepdims=True) + EPS)
    return (y * w) * (1.0 + sc) + sh


def _kind_specs(nctxb):
    if nctxb > 0:
        return pl.BlockSpec((None, 1, D), lambda i: (jnp.where(i < nctxb, 0, 1), 0, 0))
    return pl.BlockSpec((None, 1, D), lambda i: (0, 0, 0))


def modnorm_fwd(h, w, sc, sh, nctxb, name):
    T = h.shape[0]

    def body(h_ref, w_ref, sc_ref, sh_ref, o_ref):
        o_ref[...] = _modnorm_f(h_ref[...], w_ref[...], sc_ref[...], sh_ref[...]).astype(bf16)

    blk = pl.BlockSpec((TB, D), lambda i: (i, 0))
    row = pl.BlockSpec((1, D), lambda i: (0, 0))
    ks = _kind_specs(nctxb)
    return pl.pallas_call(body, grid=(T // TB,), in_specs=[blk, row, ks, ks], out_specs=blk,
                          out_shape=S((T, D), bf16), compiler_params=_cparams("parallel"), name=name)(h, w, sc, sh)


def modnorm_bwd(h, w, sc, sh, da, dres, nctxb, name):
    T = h.shape[0]
    kinds = sc.shape[0]

    def body(h_ref, w_ref, sc_ref, sh_ref, da_ref, dres_ref, dh_ref, dw_ref, dsc_ref, dsh_ref):
        i = pl.program_id(0)
        _, vjp = jax.vjp(_modnorm_f, h_ref[...], w_ref[...], sc_ref[...], sh_ref[...])
        dh, dw, dsc, dsh = vjp(da_ref[...].astype(f32))
        dh_ref[...] = dres_ref[...] + dh

        @pl.when(i == 0)
        def _():
            dw_ref[...] = jnp.zeros_like(dw_ref)

        @pl.when((i == 0) | (i == nctxb))
        def _():
            dsc_ref[...] = jnp.zeros_like(dsc_ref)
            dsh_ref[...] = jnp.zeros_like(dsh_ref)

        dw_ref[...] += dw
        dsc_ref[...] += dsc
        dsh_ref[...] += dsh

    blk = pl.BlockSpec((TB, D), lambda i: (i, 0))
    row = pl.BlockSpec((1, D), lambda i: (0, 0))
    ks = _kind_specs(nctxb)
    return pl.pallas_call(
        body, grid=(T // TB,), in_specs=[blk, row, ks, ks, blk, blk], out_specs=[blk, row, ks, ks],
        out_shape=[S((T, D), f32), S((1, D), f32), S((kinds, 1, D), f32), S((kinds, 1, D), f32)],
        compiler_params=_cparams("arbitrary"), name=name)(h, w, sc, sh, da, dres)


def resgate_fwd(h, o, g, b, name):
    T = h.shape[0]

    def body(h_ref, o_ref, g_ref, b_ref, out_ref):
        out_ref[...] = h_ref[...] + g_ref[...] * (o_ref[...] + b_ref[...])

    blk = pl.BlockSpec((TB, D), lambda i: (i, 0))
    row = pl.BlockSpec((1, D), lambda i: (0, 0))
    return pl.pallas_call(body, grid=(T // TB,), in_specs=[blk, blk, row, row], out_specs=blk,
                          out_shape=S((T, D), f32), compiler_params=_cparams("parallel"), name=name)(h, o, g, b)


def resgate_bwd(dh, o, g, b, name):
    T = dh.shape[0]

    def body(dh_ref, o_ref, g_ref, b_ref, do_ref, dg_ref, db_ref):
        i = pl.program_id(0)

        @pl.when(i == 0)
        def _():
            dg_ref[...] = jnp.zeros_like(dg_ref)
            db_ref[...] = jnp.zeros_like(db_ref)

        dh = dh_ref[...]
        do = g_ref[...] * dh
        do_ref[...] = do.astype(bf16)
        dg_ref[...] += jnp.sum(dh * (o_ref[...] + b_ref[...]), axis=0, keepdims=True)
        db_ref[...] += jnp.sum(do, axis=0, keepdims=True)

    blk = pl.BlockSpec((TB, D), lambda i: (i, 0))
    row = pl.BlockSpec((1, D), lambda i: (0, 0))
    return pl.pallas_call(body, grid=(T // TB,), in_specs=[blk, blk, row, row], out_specs=[blk, row, row],
                          out_shape=[S((T, D), bf16), S((1, D), f32), S((1, D), f32)],
                          compiler_params=_cparams("arbitrary"), name=name)(dh, o, g, b)


def final_loss(h, w, tgt, name):
    T = h.shape[0]

    def f(hv, wv, tv):
        y = (hv * lax.rsqrt(jnp.mean(hv * hv, axis=-1, keepdims=True) + EPS)) * wv
        e = y - tv
        return 0.5 * jnp.sum(jnp.sum(e * e, axis=-1, keepdims=True), axis=0, keepdims=True) * (1.0 / D)

    def body(h_ref, w_ref, t_ref, loss_ref, dh_ref, dw_ref):
        i = pl.program_id(0)
        tv = t_ref[...]
        val, vjp = jax.vjp(lambda a, b_: f(a, b_, tv), h_ref[...], w_ref[...])
        dh, dw = vjp(jnp.ones((1, 1), f32))
        dh_ref[...] = dh

        @pl.when(i == 0)
        def _():
            loss_ref[...] = jnp.zeros_like(loss_ref)
            dw_ref[...] = jnp.zeros_like(dw_ref)

        loss_ref[...] += jnp.broadcast_to(val, (1, 128))
        dw_ref[...] += dw

    blk = pl.BlockSpec((TB, D), lambda i: (i, 0))
    row = pl.BlockSpec((1, D), lambda i: (0, 0))
    return pl.pallas_call(body, grid=(T // TB,), in_specs=[blk, row, blk],
                          out_specs=[pl.BlockSpec((1, 128), lambda i: (0, 0)), blk, row],
                          out_shape=[S((1, 128), f32), S((T, D), f32), S((1, D), f32)],
                          compiler_params=_cparams("arbitrary"), name=name)(h, w, tgt)


CB = 256
RT = 32
RTB = 16


def _fold8(t):
    acc = t[0:8]
    for k in range(1, t.shape[0] // 8):
        acc = acc + t[8 * k:8 * (k + 1)]
    return acc


def _rows(start, off=0, rt=RT):
    return pl.ds(pl.multiple_of(start + off, 8), rt)


def _rowsb(start, off=0):
    return _rows(start, off, RTB)


def _zero_rows(ref, start, n):
    ref[pl.ds(start, n), :] = jnp.zeros((n, ref.shape[1]), f32)


K5, HALF5, PAD5 = 5, 2, 8


def _shift_copies5(base_ref, s_ref, ln, sign):
    for k in range(K5):
        s_ref[k, pl.ds(0, ln), :] = base_ref[pl.ds(PAD5 + sign * (k - HALF5), ln), :]


def ssd_conv_fwd(u, w, b, segs, name, hosted=None):
    T = u.shape[0]
    maxlen = max(ln for _, ln in segs)

    def body(u_ref, w_ref, b_ref, o_ref, base_ref, s_ref):
        wv = [w_ref[pl.ds(k, 1), :] for k in range(K5)]
        bv = b_ref[...]
        for s0, ln in segs:
            _zero_rows(base_ref, 0, PAD5)
            _zero_rows(base_ref, PAD5 + ln, PAD5)
            base_ref[pl.ds(PAD5, ln), :] = u_ref[pl.ds(s0, ln), :]
            _shift_copies5(base_ref, s_ref, ln, 1)

            def tile(i, carry):
                r = i * RT
                acc = jnp.broadcast_to(bv, (RT, CB))
                for k in range(K5):
                    acc = acc + s_ref[k, _rows(r), :] * wv[k]
                o_ref[_rows(r, s0), :] = acc * _sigmoid(acc)
                return carry

            lax.fori_loop(0, ln // RT, tile, 0)

    (out,), extra = _host_call(
        body, (CONVD // CB,),
        [pl.BlockSpec((T, CB), lambda j: (0, j)), pl.BlockSpec((K5, CB), lambda j: (0, j)),
         pl.BlockSpec((1, CB), lambda j: (0, j))],
        [pl.BlockSpec((T, CB), lambda j: (0, j))], [S((T, CONVD), f32)],
        [pltpu.VMEM((maxlen + 2 * PAD5, CB), f32), pltpu.VMEM((K5, maxlen, CB), f32)],
        ("parallel",), name, (u, w, b), hosted)
    return out, extra


def ssd_conv_bwd(proj, w, b, dy2, dskip, segs, name):
    T = proj.shape[0]
    maxlen = max(ln for _, ln in segs)
    nskip = DI // CB

    def body(u_ref, w_ref, b_ref, dya_ref, dyb_ref, dsk_ref, du_ref, dw_ref, db_ref, base_ref, s_ref):
        wv = [w_ref[pl.ds(k, 1), :] for k in range(K5)]
        bv = b_ref[...]
        has_skip = (pl.program_id(0) < nskip).astype(f32)
        acc8 = tuple(jnp.zeros((8, CB), f32) for _ in range(K5 + 1))
        for s0, ln in segs:
            _zero_rows(base_ref, 0, PAD5)
            _zero_rows(base_ref, PAD5 + ln, PAD5)
            base_ref[pl.ds(PAD5, ln), :] = u_ref[pl.ds(s0, ln), :]
            _shift_copies5(base_ref, s_ref, ln, 1)

            def tile1(i, carry):
                r = i * RTB
                taps = [s_ref[k, _rowsb(r), :] for k in range(K5)]
                pre = jnp.broadcast_to(bv, (RTB, CB))
                for k in range(K5):
                    pre = pre + taps[k] * wv[k]
                sg = _sigmoid(pre)
                dy = dya_ref[_rowsb(r, s0), :] + dyb_ref[_rowsb(r, s0), :] + has_skip * dsk_ref[_rowsb(r, s0), :]
                dpre = dy * (sg * (1.0 + pre * (1.0 - sg)))
                base_ref[_rowsb(r, PAD5), :] = dpre
                new = [carry[k] + _fold8(dpre * taps[k]) for k in range(K5)]
                new.append(carry[K5] + _fold8(dpre))
                return tuple(new)

            acc8 = lax.fori_loop(0, ln // RTB, tile1, acc8)
            _shift_copies5(base_ref, s_ref, ln, -1)

            def tile2(i, carry):
                r = i * RTB
                du = jnp.zeros((RTB, CB), f32)
                for k in range(K5):
                    du = du + s_ref[k, _rowsb(r), :] * wv[k]
                du_ref[_rowsb(r, s0), :] = du.astype(bf16)
                return carry

            lax.fori_loop(0, ln // RTB, tile2, 0)
        for k in range(K5):
            dw_ref[pl.ds(k, 1), :] = jnp.sum(acc8[k], axis=0, keepdims=True)
        db_ref[...] = jnp.sum(acc8[K5], axis=0, keepdims=True)

    cblk = pl.BlockSpec((T, CB), lambda j: (0, j))
    return pl.pallas_call(
        body, grid=(CONVD // CB,),
        in_specs=[cblk, pl.BlockSpec((K5, CB), lambda j: (0, j)), pl.BlockSpec((1, CB), lambda j: (0, j)),
                  pl.BlockSpec((None, T, CB), lambda j: (0, 0, j)), pl.BlockSpec((None, T, CB), lambda j: (1, 0, j)),
                  pl.BlockSpec((T, CB), lambda j: (0, jnp.minimum(j, nskip - 1)))],
        out_specs=[cblk, pl.BlockSpec((K5, CB), lambda j: (0, j)), pl.BlockSpec((1, CB), lambda j: (0, j))],
        out_shape=[S((T, CONVD), bf16), S((K5, CONVD), f32), S((1, CONVD), f32)],
        scratch_shapes=[pltpu.VMEM((maxlen + 2 * PAD5, CB), f32), pltpu.VMEM((K5, maxlen, CB), f32)],
        compiler_params=_cparams("parallel"), name=name)(proj, w, b, dy2, dy2, dskip)


GPAD = GRID_W


def _grid_copies(g_ref, src, L):
    col = lax.broadcasted_iota(jnp.int32, (L, CB), 0) & (GRID_W - 1)
    for d in range(3):
        _zero_rows(g_ref.at[d], 0, GPAD)
        _zero_rows(g_ref.at[d], GPAD + L, GPAD)
    g_ref[1, pl.ds(GPAD, L), :] = src
    g_ref[0, pl.ds(GPAD, L), :] = jnp.where(col != 0, g_ref[1, pl.ds(GPAD - 1, L), :], 0.0)
    g_ref[2, pl.ds(GPAD, L), :] = jnp.where(col != GRID_W - 1, g_ref[1, pl.ds(GPAD + 1, L), :], 0.0)


def ffn_gate_fwd(val, gate, cw, cb_, name, hosted=None):
    L = val.shape[0]
    nb = FH // CB

    def body(val_ref, gate_ref, w_ref, b_ref, o_ref, g_ref):
        wv = [w_ref[pl.ds(t, 1), :] for t in range(9)]
        bv = b_ref[...]
        _grid_copies(g_ref, gate_ref[...], L)

        def tile(i, carry):
            r = i * RT
            acc = jnp.broadcast_to(bv, (RT, CB))
            for dr in range(3):
                for dc in range(3):
                    acc = acc + g_ref[dc, _rows(r, GPAD + (dr - 1) * GRID_W), :] * wv[3 * dr + dc]
            o_ref[_rows(r), :] = (acc * _sigmoid(acc) * val_ref[_rows(r), :]).astype(bf16)
            return carry

        lax.fori_loop(0, L // RT, tile, 0)

    cblk = pl.BlockSpec((L, CB), lambda j: (0, j))
    (out,), extra = _host_call(
        body, (nb,), [cblk, cblk, pl.BlockSpec((9, CB), lambda j: (0, j)), pl.BlockSpec((1, CB), lambda j: (0, j))],
        [cblk], [S((L, FH), bf16)], [pltpu.VMEM((3, L + 2 * GPAD, CB), f32)], ("parallel",), name,
        (val, gate, cw, cb_), hosted)
    return out, extra


def ffn_gate_bwd(val, gate, cw, cb_, dact, name):
    L = val.shape[0]
    nb = FH // CB

    def body(val_ref, gate_ref, w_ref, b_ref, da_ref, dval_ref, dgate_ref, dw_ref, db_ref, g_ref, d_ref):
        wv = [w_ref[pl.ds(t, 1), :] for t in range(9)]
        bv = b_ref[...]
        _grid_copies(g_ref, gate_ref[...], L)

        def tile1(i, carry):
            r = i * RTB

            def tap(t):
                return g_ref[t % 3, _rowsb(r, GPAD + (t // 3 - 1) * GRID_W), :]

            pre = jnp.broadcast_to(bv, (RTB, CB))
            for t in range(9):
                pre = pre + tap(t) * wv[t]
            sg = _sigmoid(pre)
            da = da_ref[_rowsb(r), :].astype(f32)
            dval_ref[_rowsb(r), :] = (da * pre * sg).astype(bf16)
            dpre = da * val_ref[_rowsb(r), :] * (sg * (1.0 + pre * (1.0 - sg)))
            d_ref[_rowsb(r), :] = dpre
            new = [carry[t] + _fold8(dpre * tap(t)) for t in range(9)]
            new.append(carry[9] + _fold8(dpre))
            return tuple(new)

        acc8 = lax.fori_loop(0, L // RTB, tile1, tuple(jnp.zeros((8, CB), f32) for _ in range(10)))
        for t in range(9):
            dw_ref[pl.ds(t, 1), :] = jnp.sum(acc8[t], axis=0, keepdims=True)
        db_ref[...] = jnp.sum(acc8[9], axis=0, keepdims=True)
        _grid_copies(g_ref, d_ref[...], L)

        def tile2(i, carry):
            r = i * RTB
            dg = jnp.zeros((RTB, CB), f32)
            for dr in range(3):
                for dc in range(3):
                    dg = dg + g_ref[2 - dc, _rowsb(r, GPAD - (dr - 1) * GRID_W), :] * wv[3 * dr + dc]
            dgate_ref[_rowsb(r), :] = dg.astype(bf16)
            return carry

        lax.fori_loop(0, L // RTB, tile2, 0)

    cblk = pl.BlockSpec((L, CB), lambda j: (0, j))
    return pl.pallas_call(
        body, grid=(nb,),
        in_specs=[cblk, cblk, pl.BlockSpec((9, CB), lambda j: (0, j)), pl.BlockSpec((1, CB), lambda j: (0, j)), cblk],
        out_specs=[cblk, cblk, pl.BlockSpec((9, CB), lambda j: (0, j)), pl.BlockSpec((1, CB), lambda j: (0, j))],
        out_shape=[S((L, FH), bf16), S((L, FH), bf16), S((9, FH), f32), S((1, FH), f32)],
        scratch_shapes=[pltpu.VMEM((3, L + 2 * GPAD, CB), f32), pltpu.VMEM((L, CB), f32)],
        compiler_params=_cparams("parallel"), name=name)(val, gate, cw, cb_, dact)


CONF_K = 31
CHALF = CONF_K // 2
CPAD = 16


def _shift_copies8(c_ref, base_ref, L):
    n = L + 2 * CPAD - 8
    for b_ in range(8):
        c_ref[b_, pl.ds(0, n), :] = base_ref[pl.ds(b_, n), :]


def _tap_ab(o):
    return o % 8, o - o % 8


def conf_glu_conv_fwd(pa, pg, b1, wdw, bdw, name, hosted=None):
    L = pa.shape[0]
    nb = D // CB

    def body(pa_ref, pg_ref, ba_ref, bg_ref, w_ref, bdw_ref, o_ref, base_ref, c_ref):
        _zero_rows(base_ref, 0, CPAD)
        _zero_rows(base_ref, CPAD + L, CPAD)
        base_ref[pl.ds(CPAD, L), :] = (pa_ref[...] + ba_ref[...]) * _sigmoid(pg_ref[...] + bg_ref[...])
        _shift_copies8(c_ref, base_ref, L)
        bv = bdw_ref[...]

        def tile(i, carry):
            r = i * RT
            acc = jnp.broadcast_to(bv, (RT, CB))
            for k in range(CONF_K):
                b_, a8 = _tap_ab(k - CHALF)
                acc = acc + c_ref[b_, _rows(r, CPAD + a8), :] * w_ref[pl.ds(k, 1), :]
            o_ref[_rows(r), :] = acc
            return carry

        lax.fori_loop(0, L // RT, tile, 0)

    cblk = pl.BlockSpec((L, CB), lambda j: (0, j))
    rblk = pl.BlockSpec((1, CB), lambda j: (0, j))
    rgblk = pl.BlockSpec((1, CB), lambda j: (0, nb + j))
    (out,), extra = _host_call(
        body, (nb,), [cblk, cblk, rblk, rgblk, pl.BlockSpec((CONF_K, CB), lambda j: (0, j)), rblk],
        [cblk], [S((L, D), f32)], [pltpu.VMEM((L + 2 * CPAD, CB), f32), pltpu.VMEM((8, L + 2 * CPAD, CB), f32)],
        ("parallel",), name, (pa, pg, b1, b1, wdw, bdw), hosted)
    return out, extra


def conf_glu_conv_bwd(pa, pg, b1, wdw, dy, name):
    L = pa.shape[0]
    nb = D // CB

    def body(pa_ref, pg_ref, ba_ref, bg_ref, w_ref, dy_ref, dpa_ref, dpg_ref, dba_ref, dbg_ref, dw_ref, dbdw_ref,
             base_ref, c_ref, acc_ref):
        _zero_rows(base_ref, 0, CPAD)
        _zero_rows(base_ref, CPAD + L, CPAD)
        base_ref[pl.ds(CPAD, L), :] = (pa_ref[...] + ba_ref[...]) * _sigmoid(pg_ref[...] + bg_ref[...])
        _shift_copies8(c_ref, base_ref, L)
        acc_ref[...] = jnp.zeros_like(acc_ref)

        def tile1(i, carry):
            r = i * RTB
            dyt = dy_ref[_rowsb(r), :]
            for k in range(CONF_K):
                b_, a8 = _tap_ab(k - CHALF)
                acc_ref[k] += _fold8(dyt * c_ref[b_, _rowsb(r, CPAD + a8), :])
            return carry + _fold8(dyt)

        db8 = lax.fori_loop(0, L // RTB, tile1, jnp.zeros((8, CB), f32))
        dbdw_ref[...] = jnp.sum(db8, axis=0, keepdims=True)
        for k in range(CONF_K):
            dw_ref[pl.ds(k, 1), :] = jnp.sum(acc_ref[k], axis=0, keepdims=True)
        base_ref[pl.ds(CPAD, L), :] = dy_ref[...]
        _shift_copies8(c_ref, base_ref, L)
        ba = ba_ref[...]
        bg = bg_ref[...]

        def tile2(i, carry):
            r = i * RTB
            dglu = jnp.zeros((RTB, CB), f32)
            for k in range(CONF_K):
                b_, a8 = _tap_ab(CHALF - k)
                dglu = dglu + c_ref[b_, _rowsb(r, CPAD + a8), :] * w_ref[pl.ds(k, 1), :]
            a = pa_ref[_rowsb(r), :] + ba
            sg = _sigmoid(pg_ref[_rowsb(r), :] + bg)
            dpa = dglu * sg
            dpg = dglu * a * (sg * (1.0 - sg))
            dpa_ref[_rowsb(r), :] = dpa.astype(bf16)
            dpg_ref[_rowsb(r), :] = dpg.astype(bf16)
            return carry[0] + _fold8(dpa), carry[1] + _fold8(dpg)

        s8 = lax.fori_loop(0, L // RTB, tile2, (jnp.zeros((8, CB), f32), jnp.zeros((8, CB), f32)))
        dba_ref[...] = jnp.sum(s8[0], axis=0, keepdims=True)
        dbg_ref[...] = jnp.sum(s8[1], axis=0, keepdims=True)

    cblk = pl.BlockSpec((L, CB), lambda j: (0, j))
    rblk = pl.BlockSpec((1, CB), lambda j: (0, j))
    rgblk = pl.BlockSpec((1, CB), lambda j: (0, nb + j))
    wblk = pl.BlockSpec((CONF_K, CB), lambda j: (0, j))
    return pl.pallas_call(
        body, grid=(nb,), in_specs=[cblk, cblk, rblk, rgblk, wblk, cblk],
        out_specs=[cblk, cblk, rblk, rblk, wblk, rblk],
        out_shape=[S((L, D), bf16), S((L, D), bf16), S((1, D), f32), S((1, D), f32), S((CONF_K, D), f32), S((1, D), f32)],
        scratch_shapes=[pltpu.VMEM((L + 2 * CPAD, CB), f32), pltpu.VMEM((8, L + 2 * CPAD, CB), f32),
                        pltpu.VMEM((CONF_K, 8, CB), f32)],
        compiler_params=_cparams("parallel"), name=name)(pa, pg, b1, b1, wdw, dy)


def _ln_silu_f(x, w, b):
    mu = jnp.mean(x, axis=-1, keepdims=True)
    d = x - mu
    y = d * lax.rsqrt(jnp.mean(d * d, axis=-1, keepdims=True) + EPS) * w + b
    return y * _sigmoid(y)


def ln_silu_fwd(x, w, b, name):
    T = x.shape[0]

    def body(x_ref, w_ref, b_ref, o_ref):
        o_ref[...] = _ln_silu_f(x_ref[...], w_ref[...], b_ref[...]).astype(bf16)

    blk = pl.BlockSpec((TB, D), lambda i: (i, 0))
    row = pl.BlockSpec((1, D), lambda i: (0, 0))
    return pl.pallas_call(body, grid=(T // TB,), in_specs=[blk, row, row], out_specs=blk, out_shape=S((T, D), bf16),
                          compiler_params=_cparams("parallel"), name=name)(x, w, b)


def ln_silu_bwd(x, w, b, ds, name):
    T = x.shape[0]

    def body(x_ref, w_ref, b_ref, ds_ref, dx_ref, dw_ref, db_ref):
        i = pl.program_id(0)
        _, vjp = jax.vjp(_ln_silu_f, x_ref[...], w_ref[...], b_ref[...])
        dx, dw, db = vjp(ds_ref[...].astype(f32))
        dx_ref[...] = dx

        @pl.when(i == 0)
        def _():
            dw_ref[...] = jnp.zeros_like(dw_ref)
            db_ref[...] = jnp.zeros_like(db_ref)

        dw_ref[...] += dw
        db_ref[...] += db

    blk = pl.BlockSpec((TB, D), lambda i: (i, 0))
    row = pl.BlockSpec((1, D), lambda i: (0, 0))
    return pl.pallas_call(body, grid=(T // TB,), in_specs=[blk, row, row, blk], out_specs=[blk, row, row],
                          out_shape=[S((T, D), f32), S((1, D), f32), S((1, D), f32)],
                          compiler_params=_cparams("arbitrary"), name=name)(x, w, b, ds)


def _mxu(a, b, dims):
    return lax.dot_general(a.astype(bf16), b.astype(bf16), (dims, ((), ())), preferred_element_type=f32)


def _nn(a, b):
    return _mxu(a, b, ((1,), (0,)))


def _nt(a, b):
    return _mxu(a, b, ((1,), (1,)))


def _tn(a, b):
    return _mxu(a, b, ((0,), (0,)))


@jax.custom_vjp
def _dot_nn(a, b):
    return _nn(a, b)


@jax.custom_vjp
def _dot_nt(a, b):
    return _nt(a, b)


@jax.custom_vjp
def _dot_tn(a, b):
    return _tn(a, b)


_dot_nn.defvjp(lambda a, b: (_nn(a, b), (a, b)), lambda res, g: (_nt(g, res[1]), _tn(res[0], g)))
_dot_nt.defvjp(lambda a, b: (_nt(a, b), (a, b)), lambda res, g: (_nn(g, res[1]), _tn(g, res[0])))
_dot_tn.defvjp(lambda a, b: (_tn(a, b), (a, b)), lambda res, g: (_nt(res[1], g), _nn(res[0], g)))


def _exact_dot(a, b, dims, split_first):
    v = a if split_first else b
    p1 = v.astype(bf16)
    r1 = v - p1.astype(f32)
    p2 = r1.astype(bf16)
    p3 = (r1 - p2.astype(f32)).astype(bf16)
    out = None
    for p in (p1, p2, p3):
        lhs, rhs = (p, b.astype(bf16)) if split_first else (a.astype(bf16), p)
        t = lax.dot_general(lhs, rhs, (dims, ((), ())), preferred_element_type=f32)
        out = t if out is None else out + t
    return out


@jax.custom_vjp
def _masked_sum_cols(mf, a):
    return _exact_dot(mf, a, ((1,), (0,)), False)


@jax.custom_vjp
def _masked_sum_rows(mf, a):
    return _exact_dot(a, mf, ((1,), (1,)), True)


_masked_sum_cols.defvjp(lambda mf, a: (_exact_dot(mf, a, ((1,), (0,)), False), mf),
                        lambda mf, g: (jnp.zeros_like(mf), _exact_dot(mf, g, ((0,), (0,)), False)))
_masked_sum_rows.defvjp(lambda mf, a: (_exact_dot(a, mf, ((1,), (1,)), True), mf),
                        lambda mf, g: (jnp.zeros_like(mf), _exact_dot(g, mf, ((1,), (0,)), True)))


def _masked_sum(mf, a, rows):
    return _masked_sum_rows(mf, a) if rows else _masked_sum_cols(mf, a)


def _lanes_to_rows(v):
    r = lax.broadcasted_iota(jnp.int32, (GW, GW), 0)
    c = lax.broadcasted_iota(jnp.int32, (GW, GW), 1)
    return jnp.sum(jnp.where(r == c, jnp.broadcast_to(v, (GW, GW)), 0.0), axis=1, keepdims=True)


def _ssd_chunk(x, B, C, dtc, dtr, bc, br, alc, alr, s_in, is_fwd):
    row = lax.broadcasted_iota(jnp.int32, (Q, Q), 0)
    col = lax.broadcasted_iota(jnp.int32, (Q, Q), 1)
    sgn = jnp.where(is_fwd, 1, -1).astype(jnp.int32)
    mask = (row - col) * sgn >= 0
    mf = mask.astype(f32)
    lane_head = lax.broadcasted_iota(jnp.int32, (1, GW), 1) // P

    def spread(v):
        out = jnp.zeros((v.shape[0], GW), f32)
        for r in range(HPG):
            out = jnp.where(lane_head == r, v[:, r:r + 1], out)
        return out

    dt_c = _softplus(dtc + bc)
    dt_r = _softplus(dtr + br)
    a_c = dt_c * (-jnp.exp(alc))
    a_r = dt_r * (-jnp.exp(alr))
    acum_c = _masked_sum(mf, a_c, False)
    acum_r = _masked_sum(mf, a_r, True)
    tot_c = jnp.sum(a_c, axis=0, keepdims=True)
    dt_e = spread(dt_c)
    acum_e = spread(acum_c)
    tot_e = spread(tot_c)
    xdt = x * dt_e
    cb = _dot_nt(C, B)
    scores, xs = [], []
    for r in range(HPG):
        seg = acum_c[:, r:r + 1] - acum_r[r:r + 1, :]
        scores.append(cb * jnp.exp(jnp.where(mask, seg, -jnp.inf)))
        xs.append(jnp.where(lane_head == r, xdt, 0.0))
    y = _dot_nn(jnp.concatenate(scores, axis=1), jnp.concatenate(xs, axis=0))
    y = y + _dot_nt(C, s_in) * jnp.exp(acum_e)
    xe = xdt * jnp.exp(tot_e - acum_e)
    s_out = _lanes_to_rows(jnp.exp(tot_e)) * s_in + _dot_tn(xe, B)
    return y, s_out


def _chunk_index(d, t, nctx, nc):
    bwd = jnp.where(t < nctx, nctx - 1 - t, nc - 1 - (t - nctx))
    return jnp.where(d == 0, t, bwd)


def _ssd_in_specs(ci):
    small_c = pl.BlockSpec((None, G, 1, HPG), lambda d, t: (d, 0, 0, 0))
    small_r = pl.BlockSpec((None, G, HPG, 1), lambda d, t: (d, 0, 0, 0))
    return [
        pl.BlockSpec((Q, CONVD), lambda d, t: (ci(d, t), 0)),
        pl.BlockSpec((None, G, Q, HPG), lambda d, t: (d, 0, ci(d, t), 0)),
        pl.BlockSpec((None, G, HPG, Q), lambda d, t: (d, 0, 0, ci(d, t))),
        small_c, small_r, small_c, small_r,
    ]


def _group_cols(g):
    return pl.ds(g * GW, GW), pl.ds(DI + g * N, N), pl.ds(DI + G * N + g * N, N)


def ssd_scan_fwd(xbc, dtc, dtr, bc, br, alc, alr, nctx, name, hosted=None):
    T = xbc.shape[0]
    nc = T // Q

    def body(xbc_ref, dtc_ref, dtr_ref, bc_ref, br_ref, alc_ref, alr_ref, y_ref, sin_ref, st_ref):
        d = pl.program_id(0)
        t = pl.program_id(1)

        @pl.when(t == 0)
        def _():
            st_ref[...] = jnp.zeros_like(st_ref)

        for g in range(G):
            xs, bs, cs = _group_cols(g)
            s_in = st_ref[g]
            sin_ref[g] = s_in
            y, s_out = _ssd_chunk(xbc_ref[:, xs], xbc_ref[:, bs], xbc_ref[:, cs], dtc_ref[g], dtr_ref[g], bc_ref[g], br_ref[g],
                                  alc_ref[g], alr_ref[g], s_in, d == 0)
            y_ref[:, xs] = y
            st_ref[g] = s_out

    ci = lambda d, t: _chunk_index(d, t, nctx, nc)
    out_specs = [
        pl.BlockSpec((None, Q, DI), lambda d, t: (d, ci(d, t), 0)),
        pl.BlockSpec((None, None, G, GW, N), lambda d, t: (d, ci(d, t), 0, 0, 0)),
    ]
    return _host_call(
        body, (2, nc), _ssd_in_specs(ci), out_specs, [S((2, T, DI), f32), S((2, nc, G, GW, N), f32)],
        [pltpu.VMEM((G, GW, N), f32)], ("arbitrary", "arbitrary"), name, (xbc, dtc, dtr, bc, br, alc, alr), hosted)


def ssd_scan_bwd(xbc, dtc, dtr, bc, br, alc, alr, s_in_all, dy, nctx, name, hosted=None):
    T = xbc.shape[0]
    nc = T // Q

    def body(xbc_ref, dtc_ref, dtr_ref, bc_ref, br_ref, alc_ref, alr_ref, sin_ref, dy_ref,
             dxbc_ref, ddtc_ref, ddtr_ref, dbc_ref, dbr_ref, dalc_ref, dalr_ref, ds_ref):
        d = pl.program_id(0)
        t = pl.program_id(1)

        @pl.when(t == 0)
        def _():
            ds_ref[...] = jnp.zeros_like(ds_ref)
            dbc_ref[...] = jnp.zeros_like(dbc_ref)
            dbr_ref[...] = jnp.zeros_like(dbr_ref)
            dalc_ref[...] = jnp.zeros_like(dalc_ref)
            dalr_ref[...] = jnp.zeros_like(dalr_ref)

        f = functools.partial(_ssd_chunk, is_fwd=(d == 0))
        for g in range(G):
            xs, bs, cs = _group_cols(g)
            _, vjp = jax.vjp(f, xbc_ref[:, xs], xbc_ref[:, bs], xbc_ref[:, cs], dtc_ref[g], dtr_ref[g], bc_ref[g], br_ref[g],
                             alc_ref[g], alr_ref[g], sin_ref[g])
            dx, dB, dC, ddtc, ddtr, dbc, dbr, dalc, dalr, ds = vjp((dy_ref[:, xs], ds_ref[g]))
            dxbc_ref[:, xs] = dx
            dxbc_ref[:, bs] = dB
            dxbc_ref[:, cs] = dC
            ddtc_ref[g] = ddtc
            ddtr_ref[g] = ddtr
            dbc_ref[g] += dbc
            dbr_ref[g] += dbr
            dalc_ref[g] += dalc
            dalr_ref[g] += dalr
            ds_ref[g] = ds

    ci = lambda d, t: _chunk_index(d, nc - 1 - t, nctx, nc)
    in_specs = _ssd_in_specs(ci) + [
        pl.BlockSpec((None, None, G, GW, N), lambda d, t: (d, ci(d, t), 0, 0, 0)),
        pl.BlockSpec((Q, DI), lambda d, t: (ci(d, t), 0)),
    ]
    small_c = pl.BlockSpec((None, G, 1, HPG), lambda d, t: (d, 0, 0, 0))
    small_r = pl.BlockSpec((None, G, HPG, 1), lambda d, t: (d, 0, 0, 0))
    out_specs = [
        pl.BlockSpec((None, Q, CONVD), lambda d, t: (d, ci(d, t), 0)),
        pl.BlockSpec((None, G, Q, HPG), lambda d, t: (d, 0, ci(d, t), 0)),
        pl.BlockSpec((None, G, HPG, Q), lambda d, t: (d, 0, 0, ci(d, t))),
        small_c, small_r, small_c, small_r,
    ]
    out_shape = [S((2, T, CONVD), f32), S((2, G, T, HPG), f32), S((2, G, HPG, T), f32),
                 S((2, G, 1, HPG), f32), S((2, G, HPG, 1), f32), S((2, G, 1, HPG), f32), S((2, G, HPG, 1), f32)]
    return _host_call(body, (2, nc), in_specs, out_specs, out_shape, [pltpu.VMEM((G, GW, N), f32)],
                      ("arbitrary", "arbitrary"), name, (xbc, dtc, dtr, bc, br, alc, alr, s_in_all, dy), hosted)


GTB = 128


def _gate_norm_f(yf, yb, x, z, dexp, w):
    y = (yf + yb + dexp * x) * (z * _sigmoid(z))
    return y * lax.rsqrt(jnp.mean(y * y, axis=-1, keepdims=True) + EPS) * w


def ssd_gate_fwd(y2, xbc, proj, dexp, w, nctxb, name):
    T = xbc.shape[0]
    L = T - nctxb * GTB

    def body(yf_ref, yb_ref, x_ref, z_ref, d_ref, w_ref, o_ref):
        o_ref[...] = _gate_norm_f(yf_ref[...], yb_ref[...], x_ref[...], z_ref[...], d_ref[...], w_ref[...]).astype(bf16)

    wide = pl.BlockSpec((GTB, DI), lambda i: (i + nctxb, 0))
    row = pl.BlockSpec((1, DI), lambda i: (0, 0))
    return pl.pallas_call(
        body, grid=(L // GTB,),
        in_specs=[pl.BlockSpec((None, GTB, DI), lambda i: (0, i + nctxb, 0)),
                  pl.BlockSpec((None, GTB, DI), lambda i: (1, i + nctxb, 0)), wide, wide, row, row],
        out_specs=pl.BlockSpec((GTB, DI), lambda i: (i, 0)), out_shape=S((L, DI), bf16),
        compiler_params=_cparams("parallel"), name=name)(y2, y2, xbc, proj, dexp, w)


def ssd_gate_bwd(y2, xbc, proj, dexp, w, dyn, nctxb, name, hosted=None):
    T = xbc.shape[0]
    nb = T // GTB

    def body(yf_ref, yb_ref, x_ref, z_ref, d_ref, w_ref, dyn_ref, dy_ref, dx_ref, dz_ref, dd_ref, dw_ref):
        i = pl.program_id(0)

        @pl.when(i == 0)
        def _():
            dd_ref[...] = jnp.zeros_like(dd_ref)
            dw_ref[...] = jnp.zeros_like(dw_ref)

        @pl.when(i < nctxb)
        def _():
            dy_ref[...] = jnp.zeros_like(dy_ref)
            dx_ref[...] = jnp.zeros_like(dx_ref)
            dz_ref[...] = jnp.zeros_like(dz_ref)

        @pl.when(i >= nctxb)
        def _():
            _, vjp = jax.vjp(_gate_norm_f, yf_ref[...], yb_ref[...], x_ref[...], z_ref[...], d_ref[...], w_ref[...])
            dyf, _, dx, dz, dd, dw = vjp(dyn_ref[...].astype(f32))
            dy_ref[...] = dyf
            dx_ref[...] = dx
            dz_ref[...] = dz.astype(bf16)
            fold = (lax.broadcasted_iota(jnp.int32, (DI, 128), 0) // P == lax.broadcasted_iota(jnp.int32, (DI, 128), 1))
            dd_ref[...] += jnp.dot(dd, fold.astype(f32), precision=HI, preferred_element_type=f32)
            dw_ref[...] += dw

    wide = pl.BlockSpec((GTB, DI), lambda i: (i, 0))
    row = pl.BlockSpec((1, DI), lambda i: (0, 0))
    hrow = pl.BlockSpec((1, 128), lambda i: (0, 0))
    return _host_call(
        body, (nb,),
        [pl.BlockSpec((None, GTB, DI), lambda i: (0, i, 0)), pl.BlockSpec((None, GTB, DI), lambda i: (1, i, 0)),
         wide, wide, row, row, pl.BlockSpec((GTB, DI), lambda i: (jnp.maximum(i - nctxb, 0), 0))],
        [wide, wide, wide, hrow, row],
        [S((T, DI), f32), S((T, DI), f32), S((T, DI), bf16), S((1, 128), f32), S((1, DI), f32)],
        [], ("arbitrary",), name, (y2, y2, xbc, proj, dexp, w, dyn), hosted)


CROWS = 2 * N_DEV


def mod_fwd(c16, modw, name):
    nl, _, cols = modw.shape

    def body(c_ref, w_ref, o_ref):
        cv = c_ref[...]
        s = cv * _sigmoid(cv)
        for l in range(nl):
            o_ref[l] = jnp.dot(s, w_ref[l], precision=HI, preferred_element_type=f32)

    return pl.pallas_call(body, in_specs=[VMEM, VMEM], out_specs=VMEM, out_shape=S((nl, CROWS, cols), f32),
                          compiler_params=pltpu.CompilerParams(vmem_limit_bytes=VMEM_LIMIT_BYTES), name=name)(c16, modw)


def mod_bwd(c16, modw, dm_sh, dm_all, name):
    nl, _, cols = modw.shape

    def body(c_ref, w_ref, dm_ref, dmall_ref, dw_ref, dc_ref, db_ref):
        cv = c_ref[...]
        sg = _sigmoid(cv)
        s = cv * sg
        ds_dc = sg * (1.0 + cv * (1.0 - sg))
        is_ctx = lax.broadcasted_iota(jnp.int32, (CROWS, D), 0) >= N_DEV
        dc = jnp.zeros((1, D), f32)
        for l in range(nl):
            dm = dm_ref[l]
            dw_ref[l] = lax.dot_general(s, dm, (((0,), (0,)), ((), ())), precision=HI, preferred_element_type=f32)
            dsv = lax.dot_general(dm, w_ref[l], (((1,), (1,)), ((), ())), precision=HI, preferred_element_type=f32)
            dc = dc + jnp.sum(jnp.where(is_ctx, dsv * ds_dc, 0.0), axis=0, keepdims=True)
            db_ref[pl.ds(l, 1), :] = jnp.sum(dmall_ref[l], axis=0, keepdims=True)
        dc_ref[...] = dc

    return pl.pallas_call(
        body, in_specs=[VMEM, VMEM, VMEM, VMEM], out_specs=[VMEM, VMEM, VMEM],
        out_shape=[S(modw.shape, f32), S((1, D), f32), S((nl, 6 * D), f32)],
        compiler_params=pltpu.CompilerParams(vmem_limit_bytes=VMEM_LIMIT_BYTES), name=name)(c16, modw, dm_sh, dm_all)


def adamw(w, g, m, v, name):
    R, C = w.shape
    rb = R if R <= 512 else max(r_ for r_ in range(8, 513, 8) if R % r_ == 0)
    bc1 = 1.0 - ADAM_B1 ** ADAM_STEP
    bc2 = 1.0 - ADAM_B2 ** ADAM_STEP

    def body(w_ref, g_ref, m_ref, v_ref, d_ref, nm_ref, nv_ref):
        gv = g_ref[...]
        m_new = ADAM_B1 * m_ref[...] + (1.0 - ADAM_B1) * gv
        v_new = ADAM_B2 * v_ref[...] + (1.0 - ADAM_B2) * (gv * gv)
        m_hat = m_new / bc1
        v_hat = v_new / bc2
        d_ref[...] = -ADAM_LR * (m_hat / (jnp.sqrt(v_hat) + ADAM_EPS) + ADAM_WD * w_ref[...])
        nm_ref[...] = m_new
        nv_ref[...] = v_new

    blk = pl.BlockSpec((rb, C), lambda i: (i, 0))
    return pl.pallas_call(body, grid=(R // rb,), in_specs=[blk] * 4, out_specs=[blk] * 3,
                          out_shape=[S((R, C), f32)] * 3, compiler_params=_cparams("parallel"), name=name)(w, g, m, v)


def _me():
    return lax.axis_index("x"), lax.axis_index("y"), lax.axis_index("c")


def allgather_small(x, name, with_sum=False):
    r, w = x.shape

    def body(x_ref, *refs):
        if with_sum:
            out_ref, sum_ref, send_sems, recv_sems = refs
        else:
            out_ref, send_sems, recv_sems = refs
        mx, my, mc = _me()
        me = 4 * mx + 2 * my + mc
        out_ref[me] = x_ref[...]
        peers = []
        for k in range(1, N_DEV):
            kx, ky, kc = (k >> 2) & 1, (k >> 1) & 1, k & 1
            peers.append((mx + kx - 2 * mx * kx, my + ky - 2 * my * ky, mc + kc - 2 * mc * kc))
        copies = []
        for k, peer in enumerate(peers):
            cp = pltpu.make_async_remote_copy(src_ref=x_ref, dst_ref=out_ref.at[me], send_sem=send_sems.at[k],
                                              recv_sem=recv_sems.at[k], device_id=peer, device_id_type=MESH)
            cp.start()
            copies.append(cp)
        for k, (px, py, pc) in enumerate(peers):
            pltpu.make_async_remote_copy(src_ref=x_ref, dst_ref=out_ref.at[4 * px + 2 * py + pc], send_sem=send_sems.at[k],
                                         recv_sem=recv_sems.at[k], device_id=(px, py, pc), device_id_type=MESH).wait_recv()
        for cp in copies:
            cp.wait_send()
        if with_sum:
            acc = out_ref[0]
            for j in range(1, N_DEV):
                acc = acc + out_ref[j]
            sum_ref[...] = acc

    out_shape = [S((N_DEV, r, w), f32)] + ([S((r, w), f32)] if with_sum else [])
    outs = pl.pallas_call(
        body, in_specs=[VMEM], out_specs=[VMEM] * len(out_shape), out_shape=out_shape,
        scratch_shapes=[pltpu.SemaphoreType.DMA((N_DEV - 1,)), pltpu.SemaphoreType.DMA((N_DEV - 1,))],
        compiler_params=pltpu.CompilerParams(vmem_limit_bytes=VMEM_LIMIT_BYTES), name=name)(x)
    return outs if with_sum else outs[0]


def add_own(g, r, core, name):
    _, _, R, W = g.shape
    rb = R if R <= 512 else max(r_ for r_ in range(16, 513, 16) if R % r_ == 0)

    def body(core_ref, a_ref, b_ref, o_ref):
        o_ref[...] = (a_ref[...].astype(f32) + b_ref[...].astype(f32)).astype(bf16)

    blk = pl.BlockSpec((None, rb, W), lambda k, i, core_ref: (k, i, 0))
    gs = pltpu.PrefetchScalarGridSpec(
        num_scalar_prefetch=1, grid=(4, R // rb),
        in_specs=[pl.BlockSpec((None, None, rb, W), lambda k, i, core_ref: (k, core_ref[0], i, 0)), blk], out_specs=blk)
    return pl.pallas_call(body, grid_spec=gs, out_shape=S((4, R, W), bf16),
                          compiler_params=_cparams("parallel", "parallel"), name=name)(core, g, r)


def sum_adamw(recv, w, m, v, layer, name):
    _, R, W = recv.shape
    rb = R if R <= 256 else max(r_ for r_ in range(16, 257, 16) if R % r_ == 0)
    bc1 = 1.0 - ADAM_B1 ** ADAM_STEP
    bc2 = 1.0 - ADAM_B2 ** ADAM_STEP

    def body(r_ref, w_ref, m_ref, v_ref, g_ref, d_ref, nm_ref, nv_ref):
        gv = r_ref[0].astype(f32)
        for k in range(1, 4):
            gv = gv + r_ref[k].astype(f32)
        m_new = ADAM_B1 * m_ref[...] + (1.0 - ADAM_B1) * gv
        v_new = ADAM_B2 * v_ref[...] + (1.0 - ADAM_B2) * (gv * gv)
        g_ref[...] = gv
        d_ref[...] = -ADAM_LR * ((m_new / bc1) / (jnp.sqrt(v_new / bc2) + ADAM_EPS) + ADAM_WD * w_ref[...])
        nm_ref[...] = m_new
        nv_ref[...] = v_new

    blk = pl.BlockSpec((rb, W), lambda i: (i, 0))
    wblk = blk if layer is None else pl.BlockSpec((None, rb, W), lambda i: (layer, i, 0))
    return pl.pallas_call(body, grid=(R // rb,), in_specs=[pl.BlockSpec((4, rb, W), lambda i: (0, i, 0)), wblk, wblk, wblk],
                          out_specs=[blk] * 4, out_shape=[S((R, W), f32)] * 4,
                          compiler_params=_cparams("parallel"), name=name)(recv, w, m, v)


def sum_rows(a, name):
    K, R, W = a.shape
    rb = _pick(R, (512, 256, 128, 64, 32, 16))

    def body(a_ref, o_ref):
        acc = a_ref[0].astype(f32)
        for k in range(1, K):
            acc = acc + a_ref[k].astype(f32)
        o_ref[...] = acc

    return pl.pallas_call(body, grid=(R // rb,), in_specs=[pl.BlockSpec((K, rb, W), lambda i: (0, i, 0))],
                          out_specs=pl.BlockSpec((rb, W), lambda i: (i, 0)), out_shape=S((R, W), f32),
                          compiler_params=_cparams("parallel"), name=name)(a)


DMA = pltpu.SemaphoreType.DMA


class GatherExchange:
    def __init__(self, arrays):
        self.arrays = list(arrays)
        self.na = len(self.arrays)
        self.out_shape = [S((N_DEV,) + a.shape, a.dtype) for a in self.arrays]
        self.scratch = [DMA((7 * self.na,)), DMA((7 * self.na,)), DMA((self.na,))]

    def ops(self, x_refs, out_refs, sems):
        send_sems, recv_sems, local_sems = sems
        na = self.na
        x, y, c = _me()
        me, sibling = (x, y, c), (x, y, 1 - c)
        chips = [(1 - x, y), (x, 1 - y), (1 - x, 1 - y)]

        def rows(a, px, py, pc):
            return out_refs[a].at[4 * px + 2 * py + pc]

        def copy(a, k, block, to, src=None):
            return pltpu.make_async_remote_copy(
                src_ref=rows(a, *block) if src is None else src, dst_ref=rows(a, *block),
                send_sem=send_sems.at[7 * a + k], recv_sem=recv_sems.at[7 * a + k], device_id=to, device_id_type=MESH)

        def local(a):
            return pltpu.make_async_copy(x_refs[a], rows(a, *me), local_sems.at[a])

        def first(a):
            return [copy(a, 0, me, sibling, src=x_refs[a])] + [copy(a, 1 + j, me, (*chip, c), src=x_refs[a])
                                                                for j, chip in enumerate(chips)]

        def start():
            for a in range(na):
                local(a).start()
                for cp in first(a):
                    cp.start()

        def mid():
            for a in range(na):
                for j, chip in enumerate(chips):
                    copy(a, 1 + j, (*chip, c), me).wait_recv()
                    copy(a, 4 + j, (*chip, c), sibling).start()

        def finish():
            for a in range(na):
                copy(a, 0, sibling, me).wait_recv()
                for j, chip in enumerate(chips):
                    copy(a, 4 + j, (*chip, 1 - c), me).wait_recv()
                for cp in first(a) + [copy(a, 4 + j, (*chip, c), sibling) for j, chip in enumerate(chips)]:
                    cp.wait_send()
                local(a).wait()

        return start, mid, finish


class SiblingExchange:
    def __init__(self, arrays):
        self.arrays = list(arrays)
        self.na = len(self.arrays)
        self.out_shape = [S((4,) + g.shape[2:], g.dtype) for g in self.arrays]
        self.scratch = [DMA((self.na,)), DMA((self.na,))]

    def ops(self, g_refs, out_refs, sems):
        send_sems, recv_sems = sems
        x, y, c = _me()

        def copy(a):
            return pltpu.make_async_remote_copy(src_ref=g_refs[a].at[:, 1 - c], dst_ref=out_refs[a],
                                                send_sem=send_sems.at[a], recv_sem=recv_sems.at[a],
                                                device_id=(x, y, 1 - c), device_id_type=MESH)

        def start():
            for a in range(self.na):
                copy(a).start()

        def finish():
            for a in range(self.na):
                copy(a).wait()

        return start, None, finish


class ChipsExchange:
    def __init__(self, arrays):
        self.arrays = list(arrays)
        self.na = len(self.arrays)
        self.out_shape = [S(p.shape, p.dtype) for p in self.arrays]
        self.scratch = [DMA((3 * self.na,)), DMA((3 * self.na,)), DMA((self.na,))]

    def ops(self, p_refs, out_refs, sems):
        send_sems, recv_sems, local_sems = sems
        x, y, c = _me()
        mine = 2 * x + y
        chips = [(1 - x, y), (x, 1 - y), (1 - x, 1 - y)]

        def local(a):
            return pltpu.make_async_copy(p_refs[a].at[mine], out_refs[a].at[mine], local_sems.at[a])

        def send(a, j):
            px, py = chips[j]
            return pltpu.make_async_remote_copy(src_ref=p_refs[a].at[2 * px + py], dst_ref=out_refs[a].at[mine],
                                                send_sem=send_sems.at[3 * a + j], recv_sem=recv_sems.at[3 * a + j],
                                                device_id=(px, py, c), device_id_type=MESH)

        def recv(a, j):
            px, py = chips[j]
            return pltpu.make_async_remote_copy(src_ref=p_refs[a].at[mine], dst_ref=out_refs[a].at[2 * px + py],
                                                send_sem=send_sems.at[3 * a + j], recv_sem=recv_sems.at[3 * a + j],
                                                device_id=(px, py, c), device_id_type=MESH)

        def start():
            for a in range(self.na):
                local(a).start()
                for j in range(3):
                    send(a, j).start()

        def finish():
            for a in range(self.na):
                for j in range(3):
                    recv(a, j).wait_recv()
                for j in range(3):
                    send(a, j).wait_send()
                local(a).wait()

        return start, None, finish


def exchange(ex, name):
    na = ex.na

    def body(*refs):
        start, mid, finish = ex.ops(refs[:na], refs[na:2 * na], refs[2 * na:])
        start()
        if mid is not None:
            mid()
        finish()

    return pl.pallas_call(body, in_specs=[ANY] * na, out_specs=[ANY] * na, out_shape=ex.out_shape,
                          scratch_shapes=ex.scratch, name=name)(*ex.arrays)


def _host_call(body, grid, in_specs, out_specs, out_shape, scratch_shapes, sem, name, args, hosted):
    if hosted is None:
        res = pl.pallas_call(body, grid=grid, in_specs=in_specs, out_specs=out_specs, out_shape=out_shape,
                             scratch_shapes=scratch_shapes, compiler_params=_cparams(*sem), name=name)(*args)
        return res, None
    n_in, n_out, n_sc, na = len(in_specs), len(out_shape), len(scratch_shapes), hosted.na
    nsteps = 1
    for g_ in grid:
        nsteps *= g_
    mid_step = (3 * nsteps) // 4
    i1 = n_in + na
    i2 = i1 + n_out
    i3 = i2 + na
    i4 = i3 + n_sc

    def wrapped(*refs):
        step = pl.program_id(0)
        for ax in range(1, len(grid)):
            step = step * grid[ax] + pl.program_id(ax)
        start, mid, finish = hosted.ops(refs[n_in:i1], refs[i2:i3], refs[i4:])
        pl.when(step == 0)(start)
        if mid is not None:
            pl.when(step == mid_step)(mid)
        body(*refs[:n_in], *refs[i1:i2], *refs[i3:i4])
        pl.when(step == nsteps - 1)(finish)

    res = pl.pallas_call(
        wrapped, grid=grid, in_specs=list(in_specs) + [ANY] * na, out_specs=list(out_specs) + [ANY] * na,
        out_shape=list(out_shape) + hosted.out_shape, scratch_shapes=list(scratch_shapes) + hosted.scratch,
        compiler_params=_cparams(*(("arbitrary",) * len(grid))), name=name)(*args, *hosted.arrays)
    return res[:n_out], res[n_out:]


PACK_ALIGN = 16 * PACK_W


def _pad_to(v, mult):
    n = v.shape[-1]
    extra = (-n) % mult
    if extra == 0:
        return v
    return jnp.concatenate([v, jnp.zeros(v.shape[:-1] + (extra,), v.dtype)], axis=-1)


def _f32_as_bf16_pairs(v):
    return lax.bitcast_convert_type(v.reshape(-1), bf16).reshape(-1)


def _bf16_pairs_as_f32(v):
    return lax.bitcast_convert_type(v.reshape(v.shape[:-1] + (v.shape[-1] // 2, 2)), f32)


def _col_shards(gw):
    lead = gw.shape[:-1]
    n = gw.shape[-1] // N_DEV
    t = gw.reshape(lead + (N_DEV, n))
    t = jnp.moveaxis(t, -2, 0)
    return t.reshape(N_DEV, -1)


def kernel(x, c, ctx, c_ctx, mod_w, mod_b, norm1_w, norm2_w, ssd_w_in, ssd_conv_w, ssd_conv_b, ssd_dt_bias, ssd_a_log, ssd_d, ssd_norm_w, ssd_w_out, conf_w_pw1, conf_b_pw1, conf_w_dw, conf_b_dw, conf_ln_w, conf_ln_b, conf_w_pw2, conf_b_pw2, ffn_w_up, ffn_conv_w, ffn_conv_b, ffn_w_down, final_norm_w, loss_target, m_c_ctx, m_mod_w, m_mod_b, m_norm1_w, m_norm2_w, m_ssd_w_in, m_ssd_conv_w, m_ssd_conv_b, m_ssd_dt_bias, m_ssd_a_log, m_ssd_d, m_ssd_norm_w, m_ssd_w_out, m_conf_w_pw1, m_conf_b_pw1, m_conf_w_dw, m_conf_b_dw, m_conf_ln_w, m_conf_ln_b, m_conf_w_pw2, m_conf_b_pw2, m_ffn_w_up, m_ffn_conv_w, m_ffn_conv_b, m_ffn_w_down, m_final_norm_w, v_c_ctx, v_mod_w, v_mod_b, v_norm1_w, v_norm2_w, v_ssd_w_in, v_ssd_conv_w, v_ssd_conv_b, v_ssd_dt_bias, v_ssd_a_log, v_ssd_d, v_ssd_norm_w, v_ssd_w_out, v_conf_w_pw1, v_conf_b_pw1, v_conf_w_dw, v_conf_b_dw, v_conf_ln_w, v_conf_ln_b, v_conf_w_pw2, v_conf_b_pw2, v_ffn_w_up, v_ffn_conv_w, v_ffn_conv_b, v_ffn_w_down, v_final_norm_w):
    mx, my, mc = _me()
    me = 4 * mx + 2 * my + mc
    L = x.shape[1]
    LC = ctx.shape[1]
    T = LC + L
    w_in_cols = ssd_w_in.shape[2] * N_DEV
    n_dt = w_in_cols - DI - CONVD

    small = [c[0], ssd_conv_w[0], conf_b_pw1[0], conf_w_dw[0], conf_b_dw[0], conf_ln_w[0], conf_ln_b[0], conf_b_pw2[0],
             ffn_conv_w]
    parts = [_f32_as_bf16_pairs(t) for t in small]
    sizes = [p.shape[0] for p in parts]
    small_flat = _pad_to(jnp.concatenate(parts), PACK_ALIGN).reshape(-1, PACK_W)
    w_in, small_g = exchange(GatherExchange([ssd_w_in[0].astype(bf16), small_flat]), "gather_first")
    gather_in_conv = GatherExchange([ssd_w_out[0].astype(bf16), conf_w_pw2[0].astype(bf16), conf_w_pw1[0].astype(bf16)])
    gather_in_scan = GatherExchange([ffn_w_up[0].astype(bf16), ffn_w_down[0].astype(bf16)])
    gather_in_gate = GatherExchange([ffn_w_up[1].astype(bf16)])
    gather_in_conf = GatherExchange([ffn_w_down[1].astype(bf16)])
    w_up, w_down = [None, None], [None, None]
    small_g = small_g.reshape(N_DEV, -1)
    offs = [0]
    for s_ in sizes:
        offs.append(offs[-1] + s_)
    sm = [_bf16_pairs_as_f32(small_g[:, offs[i]:offs[i + 1]]) for i in range(len(sizes))]

    def cols(pc, K):
        return jnp.moveaxis(pc.reshape(N_DEV, K, -1), 0, 1).reshape(K, -1)

    c_all = sm[0]
    conv_w5 = cols(sm[1], 5)
    b_pw1 = sm[2].reshape(1, 2 * D)
    w_dw = cols(sm[3], CONF_K)
    b_dw, ln_w, ln_b, b_pw2 = (sm[i].reshape(1, D) for i in (4, 5, 6, 7))
    fcw = sm[8].reshape(N_DEV, 2, 9, FH // N_DEV)
    ffn_cw = [cols(fcw[:, i].reshape(N_DEV, -1), 9) for i in range(2)]
    in_segs = (DI, CONVD, n_dt)
    up_segs = (FH, FH)
    pw1_segs = (D, D)

    c16 = jnp.concatenate([c_all, jnp.broadcast_to(c_ctx[None, :], (N_DEV, D))], axis=0)
    m_sh = mod_fwd(c16, mod_w, "mod_fwd")
    mod_cols = mod_w.shape[2]
    m_all = allgather_small(m_sh.reshape(2 * CROWS, mod_cols), "gather_mod")
    m_all = jnp.moveaxis(m_all.reshape(N_DEV, 2, CROWS, mod_cols), 0, 2).reshape(2, CROWS, 6 * D) + mod_b[:, None, :]
    m_lat = lax.dynamic_index_in_dim(m_all, me, axis=1, keepdims=False).reshape(2, 6, 1, D)
    m_ctx = m_all[:, N_DEV].reshape(2, 6, 1, D)
    zero_row = jnp.zeros((1, D), f32)

    def ffn_fwd(h, i, tag, hosted=None):
        a2 = modnorm_fwd(h, norm2_w[i][None], m_lat[i, 4][None], m_lat[i, 3][None], 0, f"ffn{tag}_norm")
        val, gate = smm_fwd(a2, w_up[i], None, up_segs, f"ffn{tag}_up")
        act, extra = ffn_gate_fwd(val, gate, ffn_cw[i], ffn_conv_b[i][None], f"ffn{tag}_gate", hosted)
        o2 = matmul(act, w_down[i], "nn", f32, f"ffn{tag}_down")
        h_new = resgate_fwd(h, o2, m_lat[i, 5], zero_row, f"ffn{tag}_res")
        return h_new, (a2, val, gate, act, o2), extra

    def ffn_bwd(dh, h, i, saved, tag):
        a2, val, gate, act, o2 = saved
        do2, dg2, _ = resgate_bwd(dh, o2, m_lat[i, 5], zero_row, f"ffn{tag}_res_bwd")
        g_down = matmul(act, do2, "tn", bf16, f"ffn{tag}_down_dw")
        dact = matmul(do2, w_down[i], "nt", bf16, f"ffn{tag}_down_dx")
        dval, dgate, dcw, dcb = ffn_gate_bwd(val, gate, ffn_cw[i], ffn_conv_b[i][None], dact, f"ffn{tag}_gate_bwd")
        g_up = smm_dw(a2, [dval, dgate], FH // 4, up_segs, 2, f"ffn{tag}_up_dw")
        da2, _ = smm_dx([dval, dgate], w_up[i], None, up_segs, bf16, f"ffn{tag}_up_dx")
        dh_in, dn2, dsc2, dsh2 = modnorm_bwd(h, norm2_w[i][None], m_lat[i, 4][None], m_lat[i, 3][None], da2, dh, 0,
                                             f"ffn{tag}_norm_bwd")
        return dh_in, dict(w_up=g_up, w_down=g_down, conv_w=dcw, conv_b=dcb, norm2=dn2, sh2=dsh2[0], sc2=dsc2[0], g2=dg2)

    nctx = LC // Q
    h0 = jnp.concatenate([ctx[0], x[0]], axis=0)
    sc0 = jnp.stack([m_ctx[0, 1], m_lat[0, 1]])
    sh0 = jnp.stack([m_ctx[0, 0], m_lat[0, 0]])
    a0 = modnorm_fwd(h0, norm1_w[0][None], sc0, sh0, LC // TB, "ssd_norm")
    z, xbc_pre, dt_raw = smm_fwd(a0, w_in, None, in_segs, "ssd_in")
    segs = ((0, LC), (LC, L))
    xbc, (w_out_g, w_pw2_g, w_pw1) = ssd_conv_fwd(xbc_pre, conv_w5, ssd_conv_b, segs, "ssd_conv", gather_in_conv)
    w_out = w_out_g.reshape(DI, D)
    w_pw2 = w_pw2_g.reshape(D, D)
    dt4 = dt_raw[:, :n_dt].reshape(T, 2, G, HPG)
    dtc = jnp.transpose(dt4, (1, 2, 0, 3))
    dtr = jnp.transpose(dt4, (1, 2, 3, 0))
    bias3 = ssd_dt_bias[0].reshape(2, G, HPG)
    alog3 = ssd_a_log[0].reshape(2, G, HPG)
    bc_, br_ = bias3[:, :, None, :], bias3[:, :, :, None]
    alc, alr = alog3[:, :, None, :], alog3[:, :, :, None]
    (y2, s_in_all), (w_up[0], w_down0_g) = ssd_scan_fwd(xbc, dtc, dtr, bc_, br_, alc, alr, nctx, "ssd_scan",
                                                        gather_in_scan)
    w_down[0] = w_down0_g.reshape(FH, D)
    dexp = jnp.repeat(ssd_d[0], P)[None, :]
    yn = ssd_gate_fwd(y2, xbc, z, dexp, ssd_norm_w, LC // GTB, "ssd_gate")
    o_ssd = matmul(yn, w_out, "nn", f32, "ssd_out")
    hx = x[0]
    h1 = resgate_fwd(hx, o_ssd, m_lat[0, 2], zero_row, "ssd_res")
    h2, ffn0_saved, (w_up[1],) = ffn_fwd(h1, 0, "0", gather_in_gate)

    a1 = modnorm_fwd(h2, norm1_w[1][None], m_lat[1, 1][None], m_lat[1, 0][None], 0, "conf_norm")
    pa, pg = smm_fwd(a1, w_pw1, None, pw1_segs, "conf_pw1")
    dwc, (w_down1_g,) = conf_glu_conv_fwd(pa, pg, b_pw1, w_dw, b_dw, "conf_conv", gather_in_conf)
    w_down[1] = w_down1_g.reshape(FH, D)
    s1 = ln_silu_fwd(dwc, ln_w, ln_b, "conf_ln")
    o_conf = matmul(s1, w_pw2, "nn", f32, "conf_pw2")
    h3 = resgate_fwd(h2, o_conf, m_lat[1, 2], b_pw2, "conf_res")
    h4, ffn1_saved, _ = ffn_fwd(h3, 1, "1")

    loss_part, dh4, g_final = final_loss(h4, final_norm_w[None], loss_target[0], "loss_head")
    dh3, gf1 = ffn_bwd(dh4, h3, 1, ffn1_saved, "1")

    do_conf, dg1_1, g_b_pw2 = resgate_bwd(dh3, o_conf, m_lat[1, 2], b_pw2, "conf_res_bwd")
    g_pw2 = matmul(s1, do_conf, "tn", bf16, "conf_pw2_dw")
    ds1 = matmul(do_conf, w_pw2, "nt", bf16, "conf_pw2_dx")
    ddwc, g_ln_w, g_ln_b = ln_silu_bwd(dwc, ln_w, ln_b, ds1, "conf_ln_bwd")
    dpa, dpg, dba, dbg, g_w_dw, g_b_dw = conf_glu_conv_bwd(pa, pg, b_pw1, w_dw, ddwc, "conf_conv_bwd")
    g_b_pw1 = jnp.concatenate([dba, dbg], axis=1)
    g_pw1 = smm_dw(a1, [dpa, dpg], 2 * D // N_DEV, pw1_segs, 1, "conf_pw1_dw")
    da1, _ = smm_dx([dpa, dpg], w_pw1, None, pw1_segs, bf16, "conf_pw1_dx")
    dh2, g_n1_1, dsc1_1, dsh1_1 = modnorm_bwd(h2, norm1_w[1][None], m_lat[1, 1][None], m_lat[1, 0][None], da1, dh3, 0,
                                              "conf_norm_bwd")
    dh1, gf0 = ffn_bwd(dh2, h1, 0, ffn0_saved, "0")

    do_ssd, dg1_0, _ = resgate_bwd(dh1, o_ssd, m_lat[0, 2], zero_row, "ssd_res_bwd")
    g_w_out = matmul(yn, do_ssd, "tn", bf16, "ssd_out_dw")
    dyn = matmul(do_ssd, w_out, "nt", bf16, "ssd_out_dx")
    core = mc.reshape(1).astype(jnp.int32)

    def by_device(t):
        return t.reshape((4, 2, -1, t.shape[-1]))

    early = [by_device(t) for t in (gf1["w_up"], gf1["w_down"], g_pw2, g_pw1, gf0["w_up"], gf0["w_down"], g_w_out)]
    (dy, dx_skip, dz, g_dexp, g_ssd_norm), early_sib = ssd_gate_bwd(
        y2, xbc, z, dexp, ssd_norm_w, dyn, LC // GTB, "ssd_gate_bwd", SiblingExchange(early))
    early_part = [add_own(t, r_, core, f"reduce_add{i}") for i, (t, r_) in enumerate(zip(early, early_sib))]
    (dxbc2, ddtc, ddtr, dbc, dbr, dalc, dalr), early_red = ssd_scan_bwd(
        xbc, dtc, dtr, bc_, br_, alc, alr, s_in_all, dy, nctx, "ssd_scan_bwd", ChipsExchange(early_part))
    ddt = (jnp.transpose(ddtc, (2, 0, 1, 3)) + jnp.transpose(ddtr, (3, 0, 1, 2))).reshape(T, n_dt)
    g_dt_bias = (dbc[:, :, 0, :] + dbr[:, :, :, 0]).reshape(2, NH_SSD)
    g_a_log = (dalc[:, :, 0, :] + dalr[:, :, :, 0]).reshape(2, NH_SSD)
    g_ssd_d = g_dexp[0, :NH_SSD]
    du, g_conv_w5, g_conv_b5 = ssd_conv_bwd(xbc_pre, conv_w5, ssd_conv_b, dxbc2, dx_skip, segs, "ssd_conv_bwd")
    ddt_p = _pad_to(ddt, 128).astype(bf16)
    g_w_in = smm_dw(a0, [dz, du, ddt_p], w_in.shape[-1], in_segs, 2, "ssd_in_dw")
    g_ffn_cw = jnp.stack([gf0["conv_w"], gf1["conv_w"]])
    small_shards = [_col_shards(t) for t in (g_conv_w5, g_b_pw1, g_w_dw, g_b_dw, g_ln_w, g_ln_b, g_b_pw2, g_ffn_cw)]
    gsizes = [s_.shape[1] for s_ in small_shards]
    g_small = _pad_to(jnp.concatenate(small_shards, axis=1), PACK_ALIGN).astype(bf16)
    late = [by_device(g_w_in), by_device(g_small.reshape(N_DEV, -1, PACK_W))]
    da0, late_sib = smm_dx([dz, du, ddt_p], w_in, None, in_segs, f32, "ssd_in_dx", SiblingExchange(late))
    late_part = [add_own(t, r_, core, f"reduce_add_late{i}") for i, (t, r_) in enumerate(zip(late, late_sib))]
    late_red = exchange(ChipsExchange(late_part), "reduce_chips_late")
    dres0 = jnp.concatenate([jnp.zeros((LC, D), f32), dh1], axis=0)
    dh0, g_n1_0, dsc1_0, dsh1_0 = modnorm_bwd(h0, norm1_w[0][None], sc0, sh0, da0, dres0, LC // TB, "ssd_norm_bwd")
    grad_x = dh0[LC:][None]

    zeros_d = jnp.zeros((1, D), f32)
    dm_lat = jnp.stack([
        jnp.concatenate([dsh1_0[1], dsc1_0[1], dg1_0, gf0["sh2"], gf0["sc2"], gf0["g2"]], axis=1),
        jnp.concatenate([dsh1_1[0], dsc1_1[0], dg1_1, gf1["sh2"], gf1["sc2"], gf1["g2"]], axis=1)])
    dm_ctx = jnp.stack([
        jnp.concatenate([dsh1_0[0], dsc1_0[0]] + [zeros_d] * 4, axis=1), jnp.zeros((1, 6 * D), f32)])
    dm_mine = jnp.concatenate([dm_lat.reshape(2, 6 * D), dm_ctx.reshape(2, 6 * D),
                               jnp.zeros((4, 6 * D), f32)], axis=0)
    dm_g = allgather_small(dm_mine, "gather_dmod")
    dm_all = jnp.concatenate([jnp.moveaxis(dm_g[:, 0:2], 0, 1), jnp.moveaxis(dm_g[:, 2:4], 0, 1)], axis=1)
    dm_sh = lax.dynamic_slice_in_dim(dm_all, me * mod_cols, mod_cols, axis=2)
    g_mod_w, g_cctx_part, g_mod_b = mod_bwd(c16, mod_w, dm_sh, dm_all, "mod_bwd")

    rep = [jnp.stack([g_n1_0[0], g_n1_1[0]]), jnp.stack([gf0["norm2"][0], gf1["norm2"][0]]), g_conv_b5, g_dt_bias, g_a_log,
           g_ssd_d, g_ssd_norm, jnp.stack([gf0["conv_b"][0], gf1["conv_b"][0]]), g_final, g_cctx_part, loss_part[:, :1]]
    rep_sizes = [r_.size for r_ in rep]
    rep_flat = _pad_to(jnp.concatenate([r_.reshape(-1) for r_ in rep]), 8 * PACK_W).reshape(-1, PACK_W)
    _, rep_sum = allgather_small(rep_flat, "reduce_replicated", with_sum=True)
    rep_sum = rep_sum.reshape(-1)
    roffs = [0]
    for s_ in rep_sizes:
        roffs.append(roffs[-1] + s_)
    rp = [rep_sum[roffs[i]:roffs[i + 1]] for i in range(len(rep_sizes))]
    loss = rp[10].reshape(())

    r_up1, r_down1, r_pw2, r_pw1, r_up0, r_down0, r_out = early_red
    r_in, r_small = late_red
    from_chips = [r_in, r_up0, r_up1, r_pw1, r_out, r_down0, r_down1, r_pw2]
    g_flat = sum_rows(r_small, "reduce_sum_small").reshape(-1)
    goffs = [0]
    for s_ in gsizes:
        goffs.append(goffs[-1] + s_)
    gs = [g_flat[goffs[i]:goffs[i + 1]] for i in range(len(gsizes))]

    big = {}
    big["ssd_w_in"] = sum_adamw(from_chips[0], ssd_w_in[0], m_ssd_w_in[0], v_ssd_w_in[0], None, "adamw_ssd_w_in")
    up = [sum_adamw(from_chips[1 + i], ffn_w_up, m_ffn_w_up, v_ffn_w_up, i, f"adamw_ffn_w_up{i}") for i in range(2)]
    big["ffn_w_up"] = tuple(jnp.stack([up[0][k], up[1][k]]) for k in range(4))
    big["conf_w_pw1"] = sum_adamw(from_chips[3], conf_w_pw1[0], m_conf_w_pw1[0], v_conf_w_pw1[0], None, "adamw_conf_w_pw1")
    big["ssd_w_out"] = sum_adamw(from_chips[4], ssd_w_out[0], m_ssd_w_out[0], v_ssd_w_out[0], None, "adamw_ssd_w_out")
    dn = [sum_adamw(from_chips[5 + i], ffn_w_down, m_ffn_w_down, v_ffn_w_down, i, f"adamw_ffn_w_down{i}") for i in range(2)]
    big["ffn_w_down"] = tuple(jnp.stack([dn[0][k], dn[1][k]]) for k in range(4))
    big["conf_w_pw2"] = sum_adamw(from_chips[7], conf_w_pw2[0], m_conf_w_pw2[0], v_conf_w_pw2[0], None, "adamw_conf_w_pw2")
    grads = {
        "c_ctx": rp[9], "mod_w": g_mod_w, "mod_b": g_mod_b, "norm1_w": rp[0], "norm2_w": rp[1],
        "ssd_conv_w": gs[0], "ssd_conv_b": rp[2], "ssd_dt_bias": rp[3], "ssd_a_log": rp[4], "ssd_d": rp[5],
        "ssd_norm_w": rp[6], "conf_b_pw1": gs[1], "conf_w_dw": gs[2],
        "conf_b_dw": gs[3], "conf_ln_w": gs[4], "conf_ln_b": gs[5], "conf_b_pw2": gs[6],
        "ffn_conv_w": gs[7], "ffn_conv_b": rp[7], "final_norm_w": rp[8],
    }
    weights = dict(c_ctx=c_ctx, mod_w=mod_w, mod_b=mod_b, norm1_w=norm1_w, norm2_w=norm2_w, ssd_w_in=ssd_w_in, ssd_conv_w=ssd_conv_w, ssd_conv_b=ssd_conv_b, ssd_dt_bias=ssd_dt_bias, ssd_a_log=ssd_a_log, ssd_d=ssd_d, ssd_norm_w=ssd_norm_w, ssd_w_out=ssd_w_out, conf_w_pw1=conf_w_pw1, conf_b_pw1=conf_b_pw1, conf_w_dw=conf_w_dw, conf_b_dw=conf_b_dw, conf_ln_w=conf_ln_w, conf_ln_b=conf_ln_b, conf_w_pw2=conf_w_pw2, conf_b_pw2=conf_b_pw2, ffn_w_up=ffn_w_up, ffn_conv_w=ffn_conv_w, ffn_conv_b=ffn_conv_b, ffn_w_down=ffn_w_down, final_norm_w=final_norm_w)
    m_in = dict(c_ctx=m_c_ctx, mod_w=m_mod_w, mod_b=m_mod_b, norm1_w=m_norm1_w, norm2_w=m_norm2_w, ssd_w_in=m_ssd_w_in, ssd_conv_w=m_ssd_conv_w, ssd_conv_b=m_ssd_conv_b, ssd_dt_bias=m_ssd_dt_bias, ssd_a_log=m_ssd_a_log, ssd_d=m_ssd_d, ssd_norm_w=m_ssd_norm_w, ssd_w_out=m_ssd_w_out, conf_w_pw1=m_conf_w_pw1, conf_b_pw1=m_conf_b_pw1, conf_w_dw=m_conf_w_dw, conf_b_dw=m_conf_b_dw, conf_ln_w=m_conf_ln_w, conf_ln_b=m_conf_ln_b, conf_w_pw2=m_conf_w_pw2, conf_b_pw2=m_conf_b_pw2, ffn_w_up=m_ffn_w_up, ffn_conv_w=m_ffn_conv_w, ffn_conv_b=m_ffn_conv_b, ffn_w_down=m_ffn_w_down, final_norm_w=m_final_norm_w)
    v_in = dict(c_ctx=v_c_ctx, mod_w=v_mod_w, mod_b=v_mod_b, norm1_w=v_norm1_w, norm2_w=v_norm2_w, ssd_w_in=v_ssd_w_in, ssd_conv_w=v_ssd_conv_w, ssd_conv_b=v_ssd_conv_b, ssd_dt_bias=v_ssd_dt_bias, ssd_a_log=v_ssd_a_log, ssd_d=v_ssd_d, ssd_norm_w=v_ssd_norm_w, ssd_w_out=v_ssd_w_out, conf_w_pw1=v_conf_w_pw1, conf_b_pw1=v_conf_b_pw1, conf_w_dw=v_conf_w_dw, conf_b_dw=v_conf_b_dw, conf_ln_w=v_conf_ln_w, conf_ln_b=v_conf_ln_b, conf_w_pw2=v_conf_w_pw2, conf_b_pw2=v_conf_b_pw2, ffn_w_up=v_ffn_w_up, ffn_conv_w=v_ffn_conv_w, ffn_conv_b=v_ffn_conv_b, ffn_w_down=v_ffn_w_down, final_norm_w=v_final_norm_w)

    out_g, out_d, out_m, out_v = [], [], [], []
    for name_, w_ in weights.items():
        shape = w_.shape
        if name_ in big:
            for lst, t in zip((out_g, out_d, out_m, out_v), big[name_]):
                lst.append(t.reshape(shape))
            continue
        cols2 = shape[-1] if len(shape) > 1 else shape[0]
        g2 = grads[name_].reshape(-1, cols2)
        d_, nm_, nv_ = adamw(w_.reshape(-1, cols2), g2, m_in[name_].reshape(-1, cols2), v_in[name_].reshape(-1, cols2),
                             f"adamw_{name_}")
        out_g.append(g2.reshape(shape))
        out_d.append(d_.reshape(shape))
        out_m.append(nm_.reshape(shape))
        out_v.append(nv_.reshape(shape))
    return (loss, grad_x, *out_g, *out_d, *out_m, *out_v)
```

```python
import functools

import jax
import jax.numpy as jnp
from jax import lax
from jax.experimental import pallas as pl
from jax.experimental.pallas import tpu as pltpu

f32 = jnp.float32
bf16 = jnp.bfloat16
HI = lax.Precision.HIGHEST
S = jax.ShapeDtypeStruct
MESH = pl.DeviceIdType.MESH
ANY = pl.BlockSpec(memory_space=pl.ANY)
VMEM = pl.BlockSpec(memory_space=pltpu.VMEM)

N_DEV = 8
D = 1024
DI = 2048
CONVD = 4096
FH = 2816
GRID_W = 64
Q = 128
HPG = 4
P = 64
N = 128
G = 8
GW = HPG * P
NH_SSD = G * HPG
EPS = 1e-6
ADAM_LR, ADAM_B1, ADAM_B2, ADAM_EPS, ADAM_WD, ADAM_STEP = 0.001, 0.9, 0.999, 1e-08, 0.01, 10
VMEM_LIMIT_BYTES = 56 * 1024 * 1024
PACK_W = 1024
TB = 256


def _cparams(*sem):
    return pltpu.CompilerParams(dimension_semantics=sem, vmem_limit_bytes=VMEM_LIMIT_BYTES)


def _pick(n, prefs):
    for p in prefs:
        if n % p == 0:
            return p
    return n


def _sigmoid(x):
    return 1.0 / (1.0 + jnp.exp(-x))


def _softplus(x):
    return jnp.maximum(x, 0.0) + jnp.log(1.0 + jnp.exp(-jnp.abs(x)))


def matmul(a, b, mode, out_dtype, name):
    if mode == "nn":
        (M, K), (_, Nn) = a.shape, b.shape
        bm, bn, bk = _pick(M, (512, 384, 256, 128)), Nn, K
    elif mode == "tn":
        (K, M), (_, Nn) = a.shape, b.shape
        bm, bn, bk = M, Nn, _pick(K, (256, 128))
    else:
        (M, K), (Nn, _) = a.shape, b.shape
        bm, bn, bk = _pick(M, (512, 384, 256, 128)), Nn, K
    nk = K // bk
    dims = {"nn": (((1,), (0,)), ((), ())), "tn": (((0,), (0,)), ((), ())), "nt": (((1,), (1,)), ((), ()))}[mode]

    def body(a_ref, b_ref, o_ref, acc_ref):
        k = pl.program_id(2)

        @pl.when(k == 0)
        def _():
            acc_ref[...] = jnp.zeros_like(acc_ref)

        acc_ref[...] += lax.dot_general(a_ref[...].astype(bf16), b_ref[...].astype(bf16), dims,
                                        preferred_element_type=f32)

        @pl.when(k == nk - 1)
        def _():
            o_ref[...] = acc_ref[...].astype(out_dtype)

    if mode == "nn":
        a_spec = pl.BlockSpec((bm, bk), lambda i, j, k: (i, k))
        b_spec = pl.BlockSpec((bk, bn), lambda i, j, k: (k, j))
    elif mode == "tn":
        a_spec = pl.BlockSpec((bk, bm), lambda i, j, k: (k, i))
        b_spec = pl.BlockSpec((bk, bn), lambda i, j, k: (k, j))
    else:
        a_spec = pl.BlockSpec((bm, bk), lambda i, j, k: (i, k))
        b_spec = pl.BlockSpec((bn, bk), lambda i, j, k: (j, k))
    return pl.pallas_call(
        body, grid=(M // bm, Nn // bn, nk), in_specs=[a_spec, b_spec],
        out_specs=pl.BlockSpec((bm, bn), lambda i, j, k: (i, j)),
        out_shape=S((M, Nn), out_dtype), scratch_shapes=[pltpu.VMEM((bm, bn), f32)],
        compiler_params=_cparams("parallel", "parallel", "arbitrary"), name=name,
    )(a, b)


SMM_BM = 256


def _shard_pieces(seg_widths, n):
    bounds = [0]
    for sw in seg_widths:
        bounds.append(bounds[-1] + sw)
    assert bounds[-1] == N_DEV * n, (seg_widths, n)
    out = []
    for j in range(N_DEV):
        lo, hi = j * n, (j + 1) * n
        pcs = []
        for si in range(len(seg_widths)):
            a, b = max(lo, bounds[si]), min(hi, bounds[si + 1])
            if a < b:
                pcs.append((si, a - bounds[si], a - lo, b - a))
        out.append(pcs)
    return out


def _w_spec(w, layer):
    if layer is None:
        return pl.BlockSpec(w.shape, lambda *idx: (0, 0, 0))
    return pl.BlockSpec((N_DEV, None) + w.shape[2:], lambda *idx: (0, layer, 0, 0))


def smm_fwd(a, w, layer, seg_widths, name, hosted=None):
    M, K = a.shape
    n = w.shape[-1]
    pieces = _shard_pieces(seg_widths, n)
    padded = [sw + (-sw) % 128 for sw in seg_widths]

    def body(a_ref, w_ref, *o_refs):
        av = a_ref[...]
        for si, sw in enumerate(seg_widths):
            if padded[si] != sw:
                o_refs[si][:, pl.ds(padded[si] - 128, 128)] = jnp.zeros((SMM_BM, 128), f32)
        for j in range(N_DEV):
            for si, soff, woff, wd in pieces[j]:
                o_refs[si][:, pl.ds(soff, wd)] = jnp.dot(av, w_ref[j, :, pl.ds(woff, wd)], preferred_element_type=f32)

    outs, extra = _host_call(
        body, (M // SMM_BM,), [pl.BlockSpec((SMM_BM, K), lambda i: (i, 0)), _w_spec(w, layer)],
        [pl.BlockSpec((SMM_BM, pw), lambda i: (i, 0)) for pw in padded], [S((M, pw), f32) for pw in padded], [],
        ("parallel",), name, (a, w), hosted)
    return outs if hosted is None else (outs, extra)


def smm_dx(d_segs, w, layer, seg_widths, out_dtype, name, hosted=None):
    M = d_segs[0].shape[0]
    K, n = w.shape[-2], w.shape[-1]
    pieces = _shard_pieces(seg_widths, n)
    ns = len(d_segs)

    def body(*refs):
        d_refs, w_ref, o_ref = refs[:ns], refs[ns], refs[ns + 1]
        acc = jnp.zeros((SMM_BM, K), f32)
        for j in range(N_DEV):
            for si, soff, woff, wd in pieces[j]:
                acc = acc + lax.dot_general(d_refs[si][:, pl.ds(soff, wd)], w_ref[j, :, pl.ds(woff, wd)],
                                            (((1,), (1,)), ((), ())), preferred_element_type=f32)
        o_ref[...] = acc.astype(out_dtype)

    (out,), extra = _host_call(
        body, (M // SMM_BM,),
        [pl.BlockSpec((SMM_BM, d.shape[1]), lambda i: (i, 0)) for d in d_segs] + [_w_spec(w, layer)],
        [pl.BlockSpec((SMM_BM, K), lambda i: (i, 0))], [S((M, K), out_dtype)], [], ("parallel",), name,
        (*d_segs, w), hosted)
    return out, extra


def smm_dw(a, d_segs, n, seg_widths, ngrp, name):
    M, K = a.shape
    pieces = _shard_pieces(seg_widths, n)
    per = N_DEV // ngrp
    nI = M // SMM_BM
    ns = len(d_segs)

    def body(*refs):
        a_ref, d_refs, o_ref, acc_ref = refs[0], refs[1:1 + ns], refs[1 + ns], refs[2 + ns]
        grp = pl.program_id(0)
        i = pl.program_id(1)

        @pl.when(i == 0)
        def _():
            acc_ref[...] = jnp.zeros_like(acc_ref)

        av = a_ref[...]
        for gs in range(ngrp):
            def one_group(gs=gs):
                for jj in range(per):
                    for si, soff, woff, wd in pieces[gs * per + jj]:
                        acc_ref[jj, :, pl.ds(woff, wd)] += lax.dot_general(
                            av, d_refs[si][:, pl.ds(soff, wd)], (((0,), (0,)), ((), ())), preferred_element_type=f32)
            pl.when(grp == gs)(one_group)

        @pl.when(i == nI - 1)
        def _():
            o_ref[...] = acc_ref[...].astype(bf16)

    return pl.pallas_call(
        body, grid=(ngrp, nI),
        in_specs=[pl.BlockSpec((SMM_BM, K), lambda g, i: (i, 0))]
        + [pl.BlockSpec((SMM_BM, d.shape[1]), lambda g, i: (i, 0)) for d in d_segs],
        out_specs=pl.BlockSpec((per, K, n), lambda g, i: (g, 0, 0)), out_shape=S((N_DEV, K, n), bf16),
        scratch_shapes=[pltpu.VMEM((per, K, n), f32)],
        compiler_params=_cparams("arbitrary", "arbitrary"), name=name)(a, *d_segs)


def _modnorm_f(h, w, sc, sh):
    y = h * lax.rsqrt(jnp.mean(h * h, axis=-1, keepdims=True) + EPS)
    return (y * w) * (1.0 + sc) + sh


def _kind_specs(nctxb):
    if nctxb > 0:
        return pl.BlockSpec((None, 1, D), lambda i: (jnp.where(i < nctxb, 0, 1), 0, 0))
    return pl.BlockSpec((None, 1, D), lambda i: (0, 0, 0))


def modnorm_fwd(h, w, sc, sh, nctxb, name):
    T = h.shape[0]

    def body(h_ref, w_ref, sc_ref, sh_ref, o_ref):
        o_ref[...] = _modnorm_f(h_ref[...], w_ref[...], sc_ref[...], sh_ref[...]).astype(bf16)

    blk = pl.BlockSpec((TB, D), lambda i: (i, 0))
    row = pl.BlockSpec((1, D), lambda i: (0, 0))
    ks = _kind_specs(nctxb)
    return pl.pallas_call(body, grid=(T // TB,), in_specs=[blk, row, ks, ks], out_specs=blk,
                          out_shape=S((T, D), bf16), compiler_params=_cparams("parallel"), name=name)(h, w, sc, sh)


def modnorm_bwd(h, w, sc, sh, da, dres, nctxb, name):
    T = h.shape[0]
    kinds = sc.shape[0]

    def body(h_ref, w_ref, sc_ref, sh_ref, da_ref, dres_ref, dh_ref, dw_ref, dsc_ref, dsh_ref):
        i = pl.program_id(0)
        _, vjp = jax.vjp(_modnorm_f, h_ref[...], w_ref[...], sc_ref[...], sh_ref[...])
        dh, dw, dsc, dsh = vjp(da_ref[...].astype(f32))
        dh_ref[...] = dres_ref[...] + dh

        @pl.when(i == 0)
        def _():
            dw_ref[...] = jnp.zeros_like(dw_ref)

        @pl.when((i == 0) | (i == nctxb))
        def _():
            dsc_ref[...] = jnp.zeros_like(dsc_ref)
            dsh_ref[...] = jnp.zeros_like(dsh_ref)

        dw_ref[...] += dw
        dsc_ref[...] += dsc
        dsh_ref[...] += dsh

    blk = pl.BlockSpec((TB, D), lambda i: (i, 0))
    row = pl.BlockSpec((1, D), lambda i: (0, 0))
    ks = _kind_specs(nctxb)
    return pl.pallas_call(
        body, grid=(T // TB,), in_specs=[blk, row, ks, ks, blk, blk], out_specs=[blk, row, ks, ks],
        out_shape=[S((T, D), f32), S((1, D), f32), S((kinds, 1, D), f32), S((kinds, 1, D), f32)],
        compiler_params=_cparams("arbitrary"), name=name)(h, w, sc, sh, da, dres)


def resgate_fwd(h, o, g, b, name):
    T = h.shape[0]

    def body(h_ref, o_ref, g_ref, b_ref, out_ref):
        out_ref[...] = h_ref[...] + g_ref[...] * (o_ref[...] + b_ref[...])

    blk = pl.BlockSpec((TB, D), lambda i: (i, 0))
    row = pl.BlockSpec((1, D), lambda i: (0, 0))
    return pl.pallas_call(body, grid=(T // TB,), in_specs=[blk, blk, row, row], out_specs=blk,
                          out_shape=S((T, D), f32), compiler_params=_cparams("parallel"), name=name)(h, o, g, b)


def resgate_bwd(dh, o, g, b, name):
    T = dh.shape[0]

    def body(dh_ref, o_ref, g_ref, b_ref, do_ref, dg_ref, db_ref):
        i = pl.program_id(0)

        @pl.when(i == 0)
        def _():
            dg_ref[...] = jnp.zeros_like(dg_ref)
            db_ref[...] = jnp.zeros_like(db_ref)

        dh = dh_ref[...]
        do = g_ref[...] * dh
        do_ref[...] = do.astype(bf16)
        dg_ref[...] += jnp.sum(dh * (o_ref[...] + b_ref[...]), axis=0, keepdims=True)
        db_ref[...] += jnp.sum(do, axis=0, keepdims=True)

    blk = pl.BlockSpec((TB, D), lambda i: (i, 0))
    row = pl.BlockSpec((1, D), lambda i: (0, 0))
    return pl.pallas_call(body, grid=(T // TB,), in_specs=[blk, blk, row, row], out_specs=[blk, row, row],
                          out_shape=[S((T, D), bf16), S((1, D), f32), S((1, D), f32)],
                          compiler_params=_cparams("arbitrary"), name=name)(dh, o, g, b)


def final_loss(h, w, tgt, name):
    T = h.shape[0]

    def f(hv, wv, tv):
        y = (hv * lax.rsqrt(jnp.mean(hv * hv, axis=-1, keepdims=True) + EPS)) * wv
        e = y - tv
        return 0.5 * jnp.sum(jnp.sum(e * e, axis=-1, keepdims=True), axis=0, keepdims=True) * (1.0 / D)

    def body(h_ref, w_ref, t_ref, loss_ref, dh_ref, dw_ref):
        i = pl.program_id(0)
        tv = t_ref[...]
        val, vjp = jax.vjp(lambda a, b_: f(a, b_, tv), h_ref[...], w_ref[...])
        dh, dw = vjp(jnp.ones((1, 1), f32))
        dh_ref[...] = dh

        @pl.when(i == 0)
        def _():
            loss_ref[...] = jnp.zeros_like(loss_ref)
            dw_ref[...] = jnp.zeros_like(dw_ref)

        loss_ref[...] += jnp.broadcast_to(val, (1, 128))
        dw_ref[...] += dw

    blk = pl.BlockSpec((TB, D), lambda i: (i, 0))
    row = pl.BlockSpec((1, D), lambda i: (0, 0))
    return pl.pallas_call(body, grid=(T // TB,), in_specs=[blk, row, blk],
                          out_specs=[pl.BlockSpec((1, 128), lambda i: (0, 0)), blk, row],
                          out_shape=[S((1, 128), f32), S((T, D), f32), S((1, D), f32)],
                          compiler_params=_cparams("arbitrary"), name=name)(h, w, tgt)


CB = 256
RT = 32
RTB = 16


def _fold8(t):
    acc = t[0:8]
    for k in range(1, t.shape[0] // 8):
        acc = acc + t[8 * k:8 * (k + 1)]
    return acc


def _rows(start, off=0, rt=RT):
    return pl.ds(pl.multiple_of(start + off, 8), rt)


def _rowsb(start, off=0):
    return _rows(start, off, RTB)


def _zero_rows(ref, start, n):
    ref[pl.ds(start, n), :] = jnp.zeros((n, ref.shape[1]), f32)


K5, HALF5, PAD5 = 5, 2, 8


def _shift_copies5(base_ref, s_ref, ln, sign):
    for k in range(K5):
        s_ref[k, pl.ds(0, ln), :] = base_ref[pl.ds(PAD5 + sign * (k - HALF5), ln), :]


def ssd_conv_fwd(u, w, b, segs, name, hosted=None):
    T = u.shape[0]
    maxlen = max(ln for _, ln in segs)

    def body(u_ref, w_ref, b_ref, o_ref, base_ref, s_ref):
        wv = [w_ref[pl.ds(k, 1), :] for k in range(K5)]
        bv = b_ref[...]
        for s0, ln in segs:
            _zero_rows(base_ref, 0, PAD5)
            _zero_rows(base_ref, PAD5 + ln, PAD5)
            base_ref[pl.ds(PAD5, ln), :] = u_ref[pl.ds(s0, ln), :]
            _shift_copies5(base_ref, s_ref, ln, 1)

            def tile(i, carry):
                r = i * RT
                acc = jnp.broadcast_to(bv, (RT, CB))
                for k in range(K5):
                    acc = acc + s_ref[k, _rows(r), :] * wv[k]
                o_ref[_rows(r, s0), :] = acc * _sigmoid(acc)
                return carry

            lax.fori_loop(0, ln // RT, tile, 0, unroll=2)

    (out,), extra = _host_call(
        body, (CONVD // CB,),
        [pl.BlockSpec((T, CB), lambda j: (0, j)), pl.BlockSpec((K5, CB), lambda j: (0, j)),
         pl.BlockSpec((1, CB), lambda j: (0, j))],
        [pl.BlockSpec((T, CB), lambda j: (0, j))], [S((T, CONVD), f32)],
        [pltpu.VMEM((maxlen + 2 * PAD5, CB), f32), pltpu.VMEM((K5, maxlen, CB), f32)],
        ("parallel",), name, (u, w, b), hosted)
    return out, extra


def ssd_conv_bwd(proj, w, b, dy2, dskip, segs, name):
    T = proj.shape[0]
    maxlen = max(ln for _, ln in segs)
    nskip = DI // CB

    def body(u_ref, w_ref, b_ref, dya_ref, dyb_ref, dsk_ref, du_ref, dw_ref, db_ref, base_ref, s_ref):
        wv = [w_ref[pl.ds(k, 1), :] for k in range(K5)]
        bv = b_ref[...]
        has_skip = (pl.program_id(0) < nskip).astype(f32)
        acc8 = tuple(jnp.zeros((8, CB), f32) for _ in range(K5 + 1))
        for s0, ln in segs:
            _zero_rows(base_ref, 0, PAD5)
            _zero_rows(base_ref, PAD5 + ln, PAD5)
            base_ref[pl.ds(PAD5, ln), :] = u_ref[pl.ds(s0, ln), :]
            _shift_copies5(base_ref, s_ref, ln, 1)

            def tile1(i, carry):
                r = i * RTB
                taps = [s_ref[k, _rowsb(r), :] for k in range(K5)]
                pre = jnp.broadcast_to(bv, (RTB, CB))
                for k in range(K5):
                    pre = pre + taps[k] * wv[k]
                sg = _sigmoid(pre)
                dy = dya_ref[_rowsb(r, s0), :] + dyb_ref[_rowsb(r, s0), :] + has_skip * dsk_ref[_rowsb(r, s0), :]
                dpre = dy * (sg * (1.0 + pre * (1.0 - sg)))
                base_ref[_rowsb(r, PAD5), :] = dpre
                new = [carry[k] + _fold8(dpre * taps[k]) for k in range(K5)]
                new.append(carry[K5] + _fold8(dpre))
                return tuple(new)

            acc8 = lax.fori_loop(0, ln // RTB, tile1, acc8, unroll=2)
            _shift_copies5(base_ref, s_ref, ln, -1)

            def tile2(i, carry):
                r = i * RTB
                du = jnp.zeros((RTB, CB), f32)
                for k in range(K5):
                    du = du + s_ref[k, _rowsb(r), :] * wv[k]
                du_ref[_rowsb(r, s0), :] = du.astype(bf16)
                return carry

            lax.fori_loop(0, ln // RTB, tile2, 0, unroll=4)
        for k in range(K5):
            dw_ref[pl.ds(k, 1), :] = jnp.sum(acc8[k], axis=0, keepdims=True)
        db_ref[...] = jnp.sum(acc8[K5], axis=0, keepdims=True)

    cblk = pl.BlockSpec((T, CB), lambda j: (0, j))
    return pl.pallas_call(
        body, grid=(CONVD // CB,),
        in_specs=[cblk, pl.BlockSpec((K5, CB), lambda j: (0, j)), pl.BlockSpec((1, CB), lambda j: (0, j)),
                  pl.BlockSpec((None, T, CB), lambda j: (0, 0, j)), pl.BlockSpec((None, T, CB), lambda j: (1, 0, j)),
                  pl.BlockSpec((T, CB), lambda j: (0, jnp.minimum(j, nskip - 1)))],
        out_specs=[cblk, pl.BlockSpec((K5, CB), lambda j: (0, j)), pl.BlockSpec((1, CB), lambda j: (0, j))],
        out_shape=[S((T, CONVD), bf16), S((K5, CONVD), f32), S((1, CONVD), f32)],
        scratch_shapes=[pltpu.VMEM((maxlen + 2 * PAD5, CB), f32), pltpu.VMEM((K5, maxlen, CB), f32)],
        compiler_params=_cparams("parallel"), name=name)(proj, w, b, dy2, dy2, dskip)


GPAD = GRID_W


def _grid_copies(g_ref, src, L):
    col = lax.broadcasted_iota(jnp.int32, (L, CB), 0) & (GRID_W - 1)
    for d in range(3):
        _zero_rows(g_ref.at[d], 0, GPAD)
        _zero_rows(g_ref.at[d], GPAD + L, GPAD)
    g_ref[1, pl.ds(GPAD, L), :] = src
    g_ref[0, pl.ds(GPAD, L), :] = jnp.where(col != 0, g_ref[1, pl.ds(GPAD - 1, L), :], 0.0)
    g_ref[2, pl.ds(GPAD, L), :] = jnp.where(col != GRID_W - 1, g_ref[1, pl.ds(GPAD + 1, L), :], 0.0)


def ffn_gate_fwd(val, gate, cw, cb_, name, hosted=None):
    L = val.shape[0]
    nb = FH // CB

    def body(val_ref, gate_ref, w_ref, b_ref, o_ref, g_ref):
        wv = [w_ref[pl.ds(t, 1), :] for t in range(9)]
        bv = b_ref[...]
        _grid_copies(g_ref, gate_ref[...], L)

        def tile(i, carry):
            r = i * RT
            acc = jnp.broadcast_to(bv, (RT, CB))
            for dr in range(3):
                for dc in range(3):
                    acc = acc + g_ref[dc, _rows(r, GPAD + (dr - 1) * GRID_W), :] * wv[3 * dr + dc]
            o_ref[_rows(r), :] = (acc * _sigmoid(acc) * val_ref[_rows(r), :]).astype(bf16)
            return carry

        lax.fori_loop(0, L // RT, tile, 0, unroll=2)

    cblk = pl.BlockSpec((L, CB), lambda j: (0, j))
    (out,), extra = _host_call(
        body, (nb,), [cblk, cblk, pl.BlockSpec((9, CB), lambda j: (0, j)), pl.BlockSpec((1, CB), lambda j: (0, j))],
        [cblk], [S((L, FH), bf16)], [pltpu.VMEM((3, L + 2 * GPAD, CB), f32)], ("parallel",), name,
        (val, gate, cw, cb_), hosted)
    return out, extra


def ffn_gate_bwd(val, gate, cw, cb_, dact, name):
    L = val.shape[0]
    nb = FH // CB

    def body(val_ref, gate_ref, w_ref, b_ref, da_ref, dval_ref, dgate_ref, dw_ref, db_ref, g_ref, d_ref):
        wv = [w_ref[pl.ds(t, 1), :] for t in range(9)]
        bv = b_ref[...]
        _grid_copies(g_ref, gate_ref[...], L)

        def tile1(i, carry):
            r = i * RTB

            def tap(t):
                return g_ref[t % 3, _rowsb(r, GPAD + (t // 3 - 1) * GRID_W), :]

            pre = jnp.broadcast_to(bv, (RTB, CB))
            for t in range(9):
                pre = pre + tap(t) * wv[t]
            sg = _sigmoid(pre)
            da = da_ref[_rowsb(r), :].astype(f32)
            dval_ref[_rowsb(r), :] = (da * pre * sg).astype(bf16)
            dpre = da * val_ref[_rowsb(r), :] * (sg * (1.0 + pre * (1.0 - sg)))
            d_ref[_rowsb(r), :] = dpre
            new = [carry[t] + _fold8(dpre * tap(t)) for t in range(9)]
            new.append(carry[9] + _fold8(dpre))
            return tuple(new)

        acc8 = lax.fori_loop(0, L // RTB, tile1, tuple(jnp.zeros((8, CB), f32) for _ in range(10)), unroll=2)
        for t in range(9):
            dw_ref[pl.ds(t, 1), :] = jnp.sum(acc8[t], axis=0, keepdims=True)
        db_ref[...] = jnp.sum(acc8[9], axis=0, keepdims=True)
        _grid_copies(g_ref, d_ref[...], L)

        def tile2(i, carry):
            r = i * RTB
            dg = jnp.zeros((RTB, CB), f32)
            for dr in range(3):
                for dc in range(3):
                    dg = dg + g_ref[2 - dc, _rowsb(r, GPAD - (dr - 1) * GRID_W), :] * wv[3 * dr + dc]
            dgate_ref[_rowsb(r), :] = dg.astype(bf16)
            return carry

        lax.fori_loop(0, L // RTB, tile2, 0, unroll=4)

    cblk = pl.BlockSpec((L, CB), lambda j: (0, j))
    return pl.pallas_call(
        body, grid=(nb,),
        in_specs=[cblk, cblk, pl.BlockSpec((9, CB), lambda j: (0, j)), pl.BlockSpec((1, CB), lambda j: (0, j)), cblk],
        out_specs=[cblk, cblk, pl.BlockSpec((9, CB), lambda j: (0, j)), pl.BlockSpec((1, CB), lambda j: (0, j))],
        out_shape=[S((L, FH), bf16), S((L, FH), bf16), S((9, FH), f32), S((1, FH), f32)],
        scratch_shapes=[pltpu.VMEM((3, L + 2 * GPAD, CB), f32), pltpu.VMEM((L, CB), f32)],
        compiler_params=_cparams("parallel"), name=name)(val, gate, cw, cb_, dact)


CONF_K = 31
CHALF = CONF_K // 2
CPAD = 16


def _shift_copies8(c_ref, base_ref, L):
    n = L + 2 * CPAD - 8
    for b_ in range(8):
        c_ref[b_, pl.ds(0, n), :] = base_ref[pl.ds(b_, n), :]


def _tap_ab(o):
    return o % 8, o - o % 8


def conf_glu_conv_fwd(pa, pg, b1, wdw, bdw, name, hosted=None):
    L = pa.shape[0]
    nb = D // CB

    def body(pa_ref, pg_ref, ba_ref, bg_ref, w_ref, bdw_ref, o_ref, base_ref, c_ref):
        _zero_rows(base_ref, 0, CPAD)
        _zero_rows(base_ref, CPAD + L, CPAD)
        base_ref[pl.ds(CPAD, L), :] = (pa_ref[...] + ba_ref[...]) * _sigmoid(pg_ref[...] + bg_ref[...])
        _shift_copies8(c_ref, base_ref, L)
        bv = bdw_ref[...]

        def tile(i, carry):
            r = i * RT
            acc = jnp.broadcast_to(bv, (RT, CB))
            for k in range(CONF_K):
                b_, a8 = _tap_ab(k - CHALF)
                acc = acc + c_ref[b_, _rows(r, CPAD + a8), :] * w_ref[pl.ds(k, 1), :]
            o_ref[_rows(r), :] = acc
            return carry

        lax.fori_loop(0, L // RT, tile, 0, unroll=2)

    cblk = pl.BlockSpec((L, CB), lambda j: (0, j))
    rblk = pl.BlockSpec((1, CB), lambda j: (0, j))
    rgblk = pl.BlockSpec((1, CB), lambda j: (0, nb + j))
    (out,), extra = _host_call(
        body, (nb,), [cblk, cblk, rblk, rgblk, pl.BlockSpec((CONF_K, CB), lambda j: (0, j)), rblk],
        [cblk], [S((L, D), f32)], [pltpu.VMEM((L + 2 * CPAD, CB), f32), pltpu.VMEM((8, L + 2 * CPAD, CB), f32)],
        ("parallel",), name, (pa, pg, b1, b1, wdw, bdw), hosted)
    return out, extra


def conf_glu_conv_bwd(pa, pg, b1, wdw, dy, name):
    L = pa.shape[0]
    nb = D // CB

    def body(pa_ref, pg_ref, ba_ref, bg_ref, w_ref, dy_ref, dpa_ref, dpg_ref, dba_ref, dbg_ref, dw_ref, dbdw_ref,
             base_ref, c_ref, acc_ref):
        _zero_rows(base_ref, 0, CPAD)
        _zero_rows(base_ref, CPAD + L, CPAD)
        base_ref[pl.ds(CPAD, L), :] = (pa_ref[...] + ba_ref[...]) * _sigmoid(pg_ref[...] + bg_ref[...])
        _shift_copies8(c_ref, base_ref, L)
        acc_ref[...] = jnp.zeros_like(acc_ref)

        def tile1(i, carry):
            r = i * RTB
            dyt = dy_ref[_rowsb(r), :]
            for k in range(CONF_K):
                b_, a8 = _tap_ab(k - CHALF)
                acc_ref[k] += _fold8(dyt * c_ref[b_, _rowsb(r, CPAD + a8), :])
            return carry + _fold8(dyt)

        db8 = lax.fori_loop(0, L // RTB, tile1, jnp.zeros((8, CB), f32), unroll=2)
        dbdw_ref[...] = jnp.sum(db8, axis=0, keepdims=True)
        for k in range(CONF_K):
            dw_ref[pl.ds(k, 1), :] = jnp.sum(acc_ref[k], axis=0, keepdims=True)
        base_ref[pl.ds(CPAD, L), :] = dy_ref[...]
        _shift_copies8(c_ref, base_ref, L)
        ba = ba_ref[...]
        bg = bg_ref[...]

        def tile2(i, carry):
            r = i * RTB
            dglu = jnp.zeros((RTB, CB), f32)
            for k in range(CONF_K):
                b_, a8 = _tap_ab(CHALF - k)
                dglu = dglu + c_ref[b_, _rowsb(r, CPAD + a8), :] * w_ref[pl.ds(k, 1), :]
            a = pa_ref[_rowsb(r), :] + ba
            sg = _sigmoid(pg_ref[_rowsb(r), :] + bg)
            dpa = dglu * sg
            dpg = dglu * a * (sg * (1.0 - sg))
            dpa_ref[_rowsb(r), :] = dpa.astype(bf16)
            dpg_ref[_rowsb(r), :] = dpg.astype(bf16)
            return carry[0] + _fold8(dpa), carry[1] + _fold8(dpg)

        s8 = lax.fori_loop(0, L // RTB, tile2, (jnp.zeros((8, CB), f32), jnp.zeros((8, CB), f32)), unroll=2)
        dba_ref[...] = jnp.sum(s8[0], axis=0, keepdims=True)
        dbg_ref[...] = jnp.sum(s8[1], axis=0, keepdims=True)

    cblk = pl.BlockSpec((L, CB), lambda j: (0, j))
    rblk = pl.BlockSpec((1, CB), lambda j: (0, j))
    rgblk = pl.BlockSpec((1, CB), lambda j: (0, nb + j))
    wblk = pl.BlockSpec((CONF_K, CB), lambda j: (0, j))
    return pl.pallas_call(
        body, grid=(nb,), in_specs=[cblk, cblk, rblk, rgblk, wblk, cblk],
        out_specs=[cblk, cblk, rblk, rblk, wblk, rblk],
        out_shape=[S((L, D), bf16), S((L, D), bf16), S((1, D), f32), S((1, D), f32), S((CONF_K, D), f32), S((1, D), f32)],
        scratch_shapes=[pltpu.VMEM((L + 2 * CPAD, CB), f32), pltpu.VMEM((8, L + 2 * CPAD, CB), f32),
                        pltpu.VMEM((CONF_K, 8, CB), f32)],
        compiler_params=_cparams("parallel"), name=name)(pa, pg, b1, b1, wdw, dy)


def _ln_silu_f(x, w, b):
    mu = jnp.mean(x, axis=-1, keepdims=True)
    d = x - mu
    y = d * lax.rsqrt(jnp.mean(d * d, axis=-1, keepdims=True) + EPS) * w + b
    return y * _sigmoid(y)


def ln_silu_fwd(x, w, b, name):
    T = x.shape[0]

    def body(x_ref, w_ref, b_ref, o_ref):
        o_ref[...] = _ln_silu_f(x_ref[...], w_ref[...], b_ref[...]).astype(bf16)

    blk = pl.BlockSpec((TB, D), lambda i: (i, 0))
    row = pl.BlockSpec((1, D), lambda i: (0, 0))
    return pl.pallas_call(body, grid=(T // TB,), in_specs=[blk, row, row], out_specs=blk, out_shape=S((T, D), bf16),
                          compiler_params=_cparams("parallel"), name=name)(x, w, b)


def ln_silu_bwd(x, w, b, ds, name):
    T = x.shape[0]

    def body(x_ref, w_ref, b_ref, ds_ref, dx_ref, dw_ref, db_ref):
        i = pl.program_id(0)
        _, vjp = jax.vjp(_ln_silu_f, x_ref[...], w_ref[...], b_ref[...])
        dx, dw, db = vjp(ds_ref[...].astype(f32))
        dx_ref[...] = dx

        @pl.when(i == 0)
        def _():
            dw_ref[...] = jnp.zeros_like(dw_ref)
            db_ref[...] = jnp.zeros_like(db_ref)

        dw_ref[...] += dw
        db_ref[...] += db

    blk = pl.BlockSpec((TB, D), lambda i: (i, 0))
    row = pl.BlockSpec((1, D), lambda i: (0, 0))
    return pl.pallas_call(body, grid=(T // TB,), in_specs=[blk, row, row, blk], out_specs=[blk, row, row],
                          out_shape=[S((T, D), f32), S((1, D), f32), S((1, D), f32)],
                          compiler_params=_cparams("arbitrary"), name=name)(x, w, b, ds)


def _mxu(a, b, dims):
    return lax.dot_general(a.astype(bf16), b.astype(bf16), (dims, ((), ())), preferred_element_type=f32)


def _nn(a, b):
    return _mxu(a, b, ((1,), (0,)))


def _nt(a, b):
    return _mxu(a, b, ((1,), (1,)))


def _tn(a, b):
    return _mxu(a, b, ((0,), (0,)))


@jax.custom_vjp
def _dot_nn(a, b):
    return _nn(a, b)


@jax.custom_vjp
def _dot_nt(a, b):
    return _nt(a, b)


@jax.custom_vjp
def _dot_tn(a, b):
    return _tn(a, b)


_dot_nn.defvjp(lambda a, b: (_nn(a, b), (a, b)), lambda res, g: (_nt(g, res[1]), _tn(res[0], g)))
_dot_nt.defvjp(lambda a, b: (_nt(a, b), (a, b)), lambda res, g: (_nn(g, res[1]), _tn(g, res[0])))
_dot_tn.defvjp(lambda a, b: (_tn(a, b), (a, b)), lambda res, g: (_nt(res[1], g), _nn(res[0], g)))


def _exact_dot(a, b, dims, split_first):
    v = a if split_first else b
    p1 = v.astype(bf16)
    r1 = v - p1.astype(f32)
    p2 = r1.astype(bf16)
    p3 = (r1 - p2.astype(f32)).astype(bf16)
    out = None
    for p in (p1, p2, p3):
        lhs, rhs = (p, b.astype(bf16)) if split_first else (a.astype(bf16), p)
        t = lax.dot_general(lhs, rhs, (dims, ((), ())), preferred_element_type=f32)
        out = t if out is None else out + t
    return out


@jax.custom_vjp
def _masked_sum_cols(mf, a):
    return _exact_dot(mf, a, ((1,), (0,)), False)


@jax.custom_vjp
def _masked_sum_rows(mf, a):
    return _exact_dot(a, mf, ((1,), (1,)), True)


_masked_sum_cols.defvjp(lambda mf, a: (_exact_dot(mf, a, ((1,), (0,)), False), mf),
                        lambda mf, g: (jnp.zeros_like(mf), _exact_dot(mf, g, ((0,), (0,)), False)))
_masked_sum_rows.defvjp(lambda mf, a: (_exact_dot(a, mf, ((1,), (1,)), True), mf),
                        lambda mf, g: (jnp.zeros_like(mf), _exact_dot(g, mf, ((1,), (0,)), True)))


def _masked_sum(mf, a, rows):
    return _masked_sum_rows(mf, a) if rows else _masked_sum_cols(mf, a)


def _lanes_to_rows(v):
    r = lax.broadcasted_iota(jnp.int32, (GW, GW), 0)
    c = lax.broadcasted_iota(jnp.int32, (GW, GW), 1)
    return jnp.sum(jnp.where(r == c, jnp.broadcast_to(v, (GW, GW)), 0.0), axis=1, keepdims=True)


def _ssd_chunk(x, B, C, dtc, dtr, bc, br, alc, alr, s_in, is_fwd):
    row = lax.broadcasted_iota(jnp.int32, (Q, Q), 0)
    col = lax.broadcasted_iota(jnp.int32, (Q, Q), 1)
    sgn = jnp.where(is_fwd, 1, -1).astype(jnp.int32)
    mask = (row - col) * sgn >= 0
    mf = mask.astype(f32)
    lane_head = lax.broadcasted_iota(jnp.int32, (1, GW), 1) // P

    def spread(v):
        out = jnp.zeros((v.shape[0], GW), f32)
        for r in range(HPG):
            out = jnp.where(lane_head == r, v[:, r:r + 1], out)
        return out

    dt_c = _softplus(dtc + bc)
    dt_r = _softplus(dtr + br)
    a_c = dt_c * (-jnp.exp(alc))
    a_r = dt_r * (-jnp.exp(alr))
    acum_c = _masked_sum(mf, a_c, False)
    acum_r = _masked_sum(mf, a_r, True)
    tot_c = jnp.sum(a_c, axis=0, keepdims=True)
    dt_e = spread(dt_c)
    acum_e = spread(acum_c)
    tot_e = spread(tot_c)
    xdt = x * dt_e
    cb = _dot_nt(C, B)
    scores, xs = [], []
    for r in range(HPG):
        seg = acum_c[:, r:r + 1] - acum_r[r:r + 1, :]
        scores.append(cb * jnp.exp(jnp.where(mask, seg, -jnp.inf)))
        xs.append(jnp.where(lane_head == r, xdt, 0.0))
    y = _dot_nn(jnp.concatenate(scores, axis=1), jnp.concatenate(xs, axis=0))
    y = y + _dot_nt(C, s_in) * jnp.exp(acum_e)
    xe = xdt * jnp.exp(tot_e - acum_e)
    s_out = _lanes_to_rows(jnp.exp(tot_e)) * s_in + _dot_tn(xe, B)
    return y, s_out


def _chunk_index(d, t, nctx, nc):
    bwd = jnp.where(t < nctx, nctx - 1 - t, nc - 1 - (t - nctx))
    return jnp.where(d == 0, t, bwd)


def _ssd_in_specs(ci):
    small_c = pl.BlockSpec((None, G, 1, HPG), lambda d, t: (d, 0, 0, 0))
    small_r = pl.BlockSpec((None, G, HPG, 1), lambda d, t: (d, 0, 0, 0))
    return [
        pl.BlockSpec((Q, CONVD), lambda d, t: (ci(d, t), 0)),
        pl.BlockSpec((None, G, Q, HPG), lambda d, t: (d, 0, ci(d, t), 0)),
        pl.BlockSpec((None, G, HPG, Q), lambda d, t: (d, 0, 0, ci(d, t))),
        small_c, small_r, small_c, small_r,
    ]


def _group_cols(g):
    return pl.ds(g * GW, GW), pl.ds(DI + g * N, N), pl.ds(DI + G * N + g * N, N)


def ssd_scan_fwd(xbc, dtc, dtr, bc, br, alc, alr, nctx, name, hosted=None):
    T = xbc.shape[0]
    nc = T // Q

    def body(xbc_ref, dtc_ref, dtr_ref, bc_ref, br_ref, alc_ref, alr_ref, y_ref, sin_ref, st_ref):
        d = pl.program_id(0)
        t = pl.program_id(1)

        @pl.when(t == 0)
        def _():
            st_ref[...] = jnp.zeros_like(st_ref)

        for g in range(G):
            xs, bs, cs = _group_cols(g)
            s_in = st_ref[g]
            sin_ref[g] = s_in
            y, s_out = _ssd_chunk(xbc_ref[:, xs], xbc_ref[:, bs], xbc_ref[:, cs], dtc_ref[g], dtr_ref[g], bc_ref[g], br_ref[g],
                                  alc_ref[g], alr_ref[g], s_in, d == 0)
            y_ref[:, xs] = y
            st_ref[g] = s_out

    ci = lambda d, t: _chunk_index(d, t, nctx, nc)
    out_specs = [
        pl.BlockSpec((None, Q, DI), lambda d, t: (d, ci(d, t), 0)),
        pl.BlockSpec((None, None, G, GW, N), lambda d, t: (d, ci(d, t), 0, 0, 0)),
    ]
    return _host_call(
        body, (2, nc), _ssd_in_specs(ci), out_specs, [S((2, T, DI), f32), S((2, nc, G, GW, N), f32)],
        [pltpu.VMEM((G, GW, N), f32)], ("arbitrary", "arbitrary"), name, (xbc, dtc, dtr, bc, br, alc, alr), hosted)


def ssd_scan_bwd(xbc, dtc, dtr, bc, br, alc, alr, s_in_all, dy, nctx, name, hosted=None):
    T = xbc.shape[0]
    nc = T // Q

    def body(xbc_ref, dtc_ref, dtr_ref, bc_ref, br_ref, alc_ref, alr_ref, sin_ref, dy_ref,
             dxbc_ref, ddtc_ref, ddtr_ref, dbc_ref, dbr_ref, dalc_ref, dalr_ref, ds_ref):
        d = pl.program_id(0)
        t = pl.program_id(1)

        @pl.when(t == 0)
        def _():
            ds_ref[...] = jnp.zeros_like(ds_ref)
            dbc_ref[...] = jnp.zeros_like(dbc_ref)
            dbr_ref[...] = jnp.zeros_like(dbr_ref)
            dalc_ref[...] = jnp.zeros_like(dalc_ref)
            dalr_ref[...] = jnp.zeros_like(dalr_ref)

        f = functools.partial(_ssd_chunk, is_fwd=(d == 0))
        for g in range(G):
            xs, bs, cs = _group_cols(g)
            _, vjp = jax.vjp(f, xbc_ref[:, xs], xbc_ref[:, bs], xbc_ref[:, cs], dtc_ref[g], dtr_ref[g], bc_ref[g], br_ref[g],
                             alc_ref[g], alr_ref[g], sin_ref[g])
            dx, dB, dC, ddtc, ddtr, dbc, dbr, dalc, dalr, ds = vjp((dy_ref[:, xs], ds_ref[g]))
            dxbc_ref[:, xs] = dx
            dxbc_ref[:, bs] = dB
            dxbc_ref[:, cs] = dC
            ddtc_ref[g] = ddtc
            ddtr_ref[g] = ddtr
            dbc_ref[g] += dbc
            dbr_ref[g] += dbr
            dalc_ref[g] += dalc
            dalr_ref[g] += dalr
            ds_ref[g] = ds

    ci = lambda d, t: _chunk_index(d, nc - 1 - t, nctx, nc)
    in_specs = _ssd_in_specs(ci) + [
        pl.BlockSpec((None, None, G, GW, N), lambda d, t: (d, ci(d, t), 0, 0, 0)),
        pl.BlockSpec((Q, DI), lambda d, t: (ci(d, t), 0)),
    ]
    small_c = pl.BlockSpec((None, G, 1, HPG), lambda d, t: (d, 0, 0, 0))
    small_r = pl.BlockSpec((None, G, HPG, 1), lambda d, t: (d, 0, 0, 0))
    out_specs = [
        pl.BlockSpec((None, Q, CONVD), lambda d, t: (d, ci(d, t), 0)),
        pl.BlockSpec((None, G, Q, HPG), lambda d, t: (d, 0, ci(d, t), 0)),
        pl.BlockSpec((None, G, HPG, Q), lambda d, t: (d, 0, 0, ci(d, t))),
        small_c, small_r, small_c, small_r,
    ]
    out_shape = [S((2, T, CONVD), f32), S((2, G, T, HPG), f32), S((2, G, HPG, T), f32),
                 S((2, G, 1, HPG), f32), S((2, G, HPG, 1), f32), S((2, G, 1, HPG), f32), S((2, G, HPG, 1), f32)]
    return _host_call(body, (2, nc), in_specs, out_specs, out_shape, [pltpu.VMEM((G, GW, N), f32)],
                      ("arbitrary", "arbitrary"), name, (xbc, dtc, dtr, bc, br, alc, alr, s_in_all, dy), hosted)


GTB = 128


def _gate_norm_f(yf, yb, x, z, dexp, w):
    y = (yf + yb + dexp * x) * (z * _sigmoid(z))
    return y * lax.rsqrt(jnp.mean(y * y, axis=-1, keepdims=True) + EPS) * w


def ssd_gate_fwd(y2, xbc, proj, dexp, w, nctxb, name):
    T = xbc.shape[0]
    L = T - nctxb * GTB

    def body(yf_ref, yb_ref, x_ref, z_ref, d_ref, w_ref, o_ref):
        o_ref[...] = _gate_norm_f(yf_ref[...], yb_ref[...], x_ref[...], z_ref[...], d_ref[...], w_ref[...]).astype(bf16)

    wide = pl.BlockSpec((GTB, DI), lambda i: (i + nctxb, 0))
    row = pl.BlockSpec((1, DI), lambda i: (0, 0))
    return pl.pallas_call(
        body, grid=(L // GTB,),
        in_specs=[pl.BlockSpec((None, GTB, DI), lambda i: (0, i + nctxb, 0)),
                  pl.BlockSpec((None, GTB, DI), lambda i: (1, i + nctxb, 0)), wide, wide, row, row],
        out_specs=pl.BlockSpec((GTB, DI), lambda i: (i, 0)), out_shape=S((L, DI), bf16),
        compiler_params=_cparams("parallel"), name=name)(y2, y2, xbc, proj, dexp, w)


def ssd_gate_bwd(y2, xbc, proj, dexp, w, dyn, nctxb, name, hosted=None):
    T = xbc.shape[0]
    nb = T // GTB

    def body(yf_ref, yb_ref, x_ref, z_ref, d_ref, w_ref, dyn_ref, dy_ref, dx_ref, dz_ref, dd_ref, dw_ref):
        i = pl.program_id(0)

        @pl.when(i == 0)
        def _():
            dd_ref[...] = jnp.zeros_like(dd_ref)
            dw_ref[...] = jnp.zeros_like(dw_ref)

        @pl.when(i < nctxb)
        def _():
            dy_ref[...] = jnp.zeros_like(dy_ref)
            dx_ref[...] = jnp.zeros_like(dx_ref)
            dz_ref[...] = jnp.zeros_like(dz_ref)

        @pl.when(i >= nctxb)
        def _():
            _, vjp = jax.vjp(_gate_norm_f, yf_ref[...], yb_ref[...], x_ref[...], z_ref[...], d_ref[...], w_ref[...])
            dyf, _, dx, dz, dd, dw = vjp(dyn_ref[...].astype(f32))
            dy_ref[...] = dyf
            dx_ref[...] = dx
            dz_ref[...] = dz.astype(bf16)
            fold = (lax.broadcasted_iota(jnp.int32, (DI, 128), 0) // P == lax.broadcasted_iota(jnp.int32, (DI, 128), 1))
            dd_ref[...] += jnp.dot(dd, fold.astype(f32), precision=HI, preferred_element_type=f32)
            dw_ref[...] += dw

    wide = pl.BlockSpec((GTB, DI), lambda i: (i, 0))
    row = pl.BlockSpec((1, DI), lambda i: (0, 0))
    hrow = pl.BlockSpec((1, 128), lambda i: (0, 0))
    return _host_call(
        body, (nb,),
        [pl.BlockSpec((None, GTB, DI), lambda i: (0, i, 0)), pl.BlockSpec((None, GTB, DI), lambda i: (1, i, 0)),
         wide, wide, row, row, pl.BlockSpec((GTB, DI), lambda i: (jnp.maximum(i - nctxb, 0), 0))],
        [wide, wide, wide, hrow, row],
        [S((T, DI), f32), S((T, DI), f32), S((T, DI), bf16), S((1, 128), f32), S((1, DI), f32)],
        [], ("arbitrary",), name, (y2, y2, xbc, proj, dexp, w, dyn), hosted)


CROWS = 2 * N_DEV


def mod_fwd(c16, modw, name):
    nl, _, cols = modw.shape

    def body(c_ref, w_ref, o_ref):
        cv = c_ref[...]
        s = cv * _sigmoid(cv)
        for l in range(nl):
            o_ref[l] = jnp.dot(s, w_ref[l], precision=HI, preferred_element_type=f32)

    return pl.pallas_call(body, in_specs=[VMEM, VMEM], out_specs=VMEM, out_shape=S((nl, CROWS, cols), f32),
                          compiler_params=pltpu.CompilerParams(vmem_limit_bytes=VMEM_LIMIT_BYTES), name=name)(c16, modw)


def mod_bwd(c16, modw, dm_sh, dm_all, name):
    nl, _, cols = modw.shape

    def body(c_ref, w_ref, dm_ref, dmall_ref, dw_ref, dc_ref, db_ref):
        cv = c_ref[...]
        sg = _sigmoid(cv)
        s = cv * sg
        ds_dc = sg * (1.0 + cv * (1.0 - sg))
        is_ctx = lax.broadcasted_iota(jnp.int32, (CROWS, D), 0) >= N_DEV
        dc = jnp.zeros((1, D), f32)
        for l in range(nl):
            dm = dm_ref[l]
            dw_ref[l] = lax.dot_general(s, dm, (((0,), (0,)), ((), ())), precision=HI, preferred_element_type=f32)
            dsv = lax.dot_general(dm, w_ref[l], (((1,), (1,)), ((), ())), precision=HI, preferred_element_type=f32)
            dc = dc + jnp.sum(jnp.where(is_ctx, dsv * ds_dc, 0.0), axis=0, keepdims=True)
            db_ref[pl.ds(l, 1), :] = jnp.sum(dmall_ref[l], axis=0, keepdims=True)
        dc_ref[...] = dc

    return pl.pallas_call(
        body, in_specs=[VMEM, VMEM, VMEM, VMEM], out_specs=[VMEM, VMEM, VMEM],
        out_shape=[S(modw.shape, f32), S((1, D), f32), S((nl, 6 * D), f32)],
        compiler_params=pltpu.CompilerParams(vmem_limit_bytes=VMEM_LIMIT_BYTES), name=name)(c16, modw, dm_sh, dm_all)


def adamw(w, g, m, v, name):
    R, C = w.shape
    rb = R if R <= 512 else max(r_ for r_ in range(8, 513, 8) if R % r_ == 0)
    bc1 = 1.0 - ADAM_B1 ** ADAM_STEP
    bc2 = 1.0 - ADAM_B2 ** ADAM_STEP

    def body(w_ref, g_ref, m_ref, v_ref, d_ref, nm_ref, nv_ref):
        gv = g_ref[...]
        m_new = ADAM_B1 * m_ref[...] + (1.0 - ADAM_B1) * gv
        v_new = ADAM_B2 * v_ref[...] + (1.0 - ADAM_B2) * (gv * gv)
        m_hat = m_new / bc1
        v_hat = v_new / bc2
        d_ref[...] = -ADAM_LR * (m_hat / (jnp.sqrt(v_hat) + ADAM_EPS) + ADAM_WD * w_ref[...])
        nm_ref[...] = m_new
        nv_ref[...] = v_new

    blk = pl.BlockSpec((rb, C), lambda i: (i, 0))
    return pl.pallas_call(body, grid=(R // rb,), in_specs=[blk] * 4, out_specs=[blk] * 3,
                          out_shape=[S((R, C), f32)] * 3, compiler_params=_cparams("parallel"), name=name)(w, g, m, v)


def _me():
    return lax.axis_index("x"), lax.axis_index("y"), lax.axis_index("c")


def allgather_small(x, name, with_sum=False):
    r, w = x.shape

    def body(x_ref, *refs):
        if with_sum:
            out_ref, sum_ref, send_sems, recv_sems = refs
        else:
            out_ref, send_sems, recv_sems = refs
        mx, my, mc = _me()
        me = 4 * mx + 2 * my + mc
        out_ref[me] = x_ref[...]
        peers = []
        for k in range(1, N_DEV):
            kx, ky, kc = (k >> 2) & 1, (k >> 1) & 1, k & 1
            peers.append((mx + kx - 2 * mx * kx, my + ky - 2 * my * ky, mc + kc - 2 * mc * kc))
        copies = []
        for k, peer in enumerate(peers):
            cp = pltpu.make_async_remote_copy(src_ref=x_ref, dst_ref=out_ref.at[me], send_sem=send_sems.at[k],
                                              recv_sem=recv_sems.at[k], device_id=peer, device_id_type=MESH)
            cp.start()
            copies.append(cp)
        for k, (px, py, pc) in enumerate(peers):
            pltpu.make_async_remote_copy(src_ref=x_ref, dst_ref=out_ref.at[4 * px + 2 * py + pc], send_sem=send_sems.at[k],
                                         recv_sem=recv_sems.at[k], device_id=(px, py, pc), device_id_type=MESH).wait_recv()
        for cp in copies:
            cp.wait_send()
        if with_sum:
            acc = out_ref[0]
            for j in range(1, N_DEV):
                acc = acc + out_ref[j]
            sum_ref[...] = acc

    out_shape = [S((N_DEV, r, w), f32)] + ([S((r, w), f32)] if with_sum else [])
    outs = pl.pallas_call(
        body, in_specs=[VMEM], out_specs=[VMEM] * len(out_shape), out_shape=out_shape,
        scratch_shapes=[pltpu.SemaphoreType.DMA((N_DEV - 1,)), pltpu.SemaphoreType.DMA((N_DEV - 1,))],
        compiler_params=pltpu.CompilerParams(vmem_limit_bytes=VMEM_LIMIT_BYTES), name=name)(x)
    return outs if with_sum else outs[0]


def add_own(g, r, core, name):
    _, _, R, W = g.shape
    rb = R if R <= 512 else max(r_ for r_ in range(16, 513, 16) if R % r_ == 0)

    def body(core_ref, a_ref, b_ref, o_ref):
        o_ref[...] = (a_ref[...].astype(f32) + b_ref[...].astype(f32)).astype(bf16)

    blk = pl.BlockSpec((None, rb, W), lambda k, i, core_ref: (k, i, 0))
    gs = pltpu.PrefetchScalarGridSpec(
        num_scalar_prefetch=1, grid=(4, R // rb),
        in_specs=[pl.BlockSpec((None, None, rb, W), lambda k, i, core_ref: (k, core_ref[0], i, 0)), blk], out_specs=blk)
    return pl.pallas_call(body, grid_spec=gs, out_shape=S((4, R, W), bf16),
                          compiler_params=_cparams("parallel", "parallel"), name=name)(core, g, r)


def sum_adamw(recv, w, m, v, layer, name):
    _, R, W = recv.shape
    rb = R if R <= 256 else max(r_ for r_ in range(16, 257, 16) if R % r_ == 0)
    bc1 = 1.0 - ADAM_B1 ** ADAM_STEP
    bc2 = 1.0 - ADAM_B2 ** ADAM_STEP

    def body(r_ref, w_ref, m_ref, v_ref, g_ref, d_ref, nm_ref, nv_ref):
        gv = r_ref[0].astype(f32)
        for k in range(1, 4):
            gv = gv + r_ref[k].astype(f32)
        m_new = ADAM_B1 * m_ref[...] + (1.0 - ADAM_B1) * gv
        v_new = ADAM_B2 * v_ref[...] + (1.0 - ADAM_B2) * (gv * gv)
        g_ref[...] = gv
        d_ref[...] = -ADAM_LR * ((m_new / bc1) / (jnp.sqrt(v_new / bc2) + ADAM_EPS) + ADAM_WD * w_ref[...])
        nm_ref[...] = m_new
        nv_ref[...] = v_new

    blk = pl.BlockSpec((rb, W), lambda i: (i, 0))
    wblk = blk if layer is None else pl.BlockSpec((None, rb, W), lambda i: (layer, i, 0))
    return pl.pallas_call(body, grid=(R // rb,), in_specs=[pl.BlockSpec((4, rb, W), lambda i: (0, i, 0)), wblk, wblk, wblk],
                          out_specs=[blk] * 4, out_shape=[S((R, W), f32)] * 4,
                          compiler_params=_cparams("parallel"), name=name)(recv, w, m, v)


def sum_rows(a, name):
    K, R, W = a.shape
    rb = _pick(R, (512, 256, 128, 64, 32, 16))

    def body(a_ref, o_ref):
        acc = a_ref[0].astype(f32)
        for k in range(1, K):
            acc = acc + a_ref[k].astype(f32)
        o_ref[...] = acc

    return pl.pallas_call(body, grid=(R // rb,), in_specs=[pl.BlockSpec((K, rb, W), lambda i: (0, i, 0))],
                          out_specs=pl.BlockSpec((rb, W), lambda i: (i, 0)), out_shape=S((R, W), f32),
                          compiler_params=_cparams("parallel"), name=name)(a)


DMA = pltpu.SemaphoreType.DMA


class GatherExchange:
    def __init__(self, arrays):
        self.arrays = list(arrays)
        self.na = len(self.arrays)
        self.out_shape = [S((N_DEV,) + a.shape, a.dtype) for a in self.arrays]
        self.scratch = [DMA((7 * self.na,)), DMA((7 * self.na,)), DMA((self.na,))]

    def ops(self, x_refs, out_refs, sems):
        send_sems, recv_sems, local_sems = sems
        na = self.na
        x, y, c = _me()
        me, sibling = (x, y, c), (x, y, 1 - c)
        chips = [(1 - x, y), (x, 1 - y), (1 - x, 1 - y)]

        def rows(a, px, py, pc):
            return out_refs[a].at[4 * px + 2 * py + pc]

        def copy(a, k, block, to, src=None):
            return pltpu.make_async_remote_copy(
                src_ref=rows(a, *block) if src is None else src, dst_ref=rows(a, *block),
                send_sem=send_sems.at[7 * a + k], recv_sem=recv_sems.at[7 * a + k], device_id=to, device_id_type=MESH)

        def local(a):
            return pltpu.make_async_copy(x_refs[a], rows(a, *me), local_sems.at[a])

        def first(a):
            return [copy(a, 0, me, sibling, src=x_refs[a])] + [copy(a, 1 + j, me, (*chip, c), src=x_refs[a])
                                                                for j, chip in enumerate(chips)]

        def start():
            for a in range(na):
                local(a).start()
                for cp in first(a):
                    cp.start()

        def mid():
            for a in range(na):
                for j, chip in enumerate(chips):
                    copy(a, 1 + j, (*chip, c), me).wait_recv()
                    copy(a, 4 + j, (*chip, c), sibling).start()

        def finish():
            for a in range(na):
                copy(a, 0, sibling, me).wait_recv()
                for j, chip in enumerate(chips):
                    copy(a, 4 + j, (*chip, 1 - c), me).wait_recv()
                for cp in first(a) + [copy(a, 4 + j, (*chip, c), sibling) for j, chip in enumerate(chips)]:
                    cp.wait_send()
                local(a).wait()

        return start, mid, finish


class SiblingExchange:
    def __init__(self, arrays):
        self.arrays = list(arrays)
        self.na = len(self.arrays)
        self.out_shape = [S((4,) + g.shape[2:], g.dtype) for g in self.arrays]
        self.scratch = [DMA((self.na,)), DMA((self.na,))]

    def ops(self, g_refs, out_refs, sems):
        send_sems, recv_sems = sems
        x, y, c = _me()

        def copy(a):
            return pltpu.make_async_remote_copy(src_ref=g_refs[a].at[:, 1 - c], dst_ref=out_refs[a],
                                                send_sem=send_sems.at[a], recv_sem=recv_sems.at[a],
                                                device_id=(x, y, 1 - c), device_id_type=MESH)

        def start():
            for a in range(self.na):
                copy(a).start()

        def finish():
            for a in range(self.na):
                copy(a).wait()

        return start, None, finish


class ChipsExchange:
    def __init__(self, arrays):
        self.arrays = list(arrays)
        self.na = len(self.arrays)
        self.out_shape = [S(p.shape, p.dtype) for p in self.arrays]
        self.scratch = [DMA((3 * self.na,)), DMA((3 * self.na,)), DMA((self.na,))]

    def ops(self, p_refs, out_refs, sems):
        send_sems, recv_sems, local_sems = sems
        x, y, c = _me()
        mine = 2 * x + y
        chips = [(1 - x, y), (x, 1 - y), (1 - x, 1 - y)]

        def local(a):
            return pltpu.make_async_copy(p_refs[a].at[mine], out_refs[a].at[mine], local_sems.at[a])

        def send(a, j):
            px, py = chips[j]
            return pltpu.make_async_remote_copy(src_ref=p_refs[a].at[2 * px + py], dst_ref=out_refs[a].at[mine],
                                                send_sem=send_sems.at[3 * a + j], recv_sem=recv_sems.at[3 * a + j],
                                                device_id=(px, py, c), device_id_type=MESH)

        def recv(a, j):
            px, py = chips[j]
            return pltpu.make_async_remote_copy(src_ref=p_refs[a].at[mine], dst_ref=out_refs[a].at[2 * px + py],
                                                send_sem=send_sems.at[3 * a + j], recv_sem=recv_sems.at[3 * a + j],
                                                device_id=(px, py, c), device_id_type=MESH)

        def start():
            for a in range(self.na):
                local(a).start()
                for j in range(3):
                    send(a, j).start()

        def finish():
            for a in range(self.na):
                for j in range(3):
                    recv(a, j).wait_recv()
                for j in range(3):
                    send(a, j).wait_send()
                local(a).wait()

        return start, None, finish


def exchange(ex, name):
    na = ex.na

    def body(*refs):
        start, mid, finish = ex.ops(refs[:na], refs[na:2 * na], refs[2 * na:])
        start()
        if mid is not None:
            mid()
        finish()

    return pl.pallas_call(body, in_specs=[ANY] * na, out_specs=[ANY] * na, out_shape=ex.out_shape,
                          scratch_shapes=ex.scratch, name=name)(*ex.arrays)


def _host_call(body, grid, in_specs, out_specs, out_shape, scratch_shapes, sem, name, args, hosted):
    if hosted is None:
        res = pl.pallas_call(body, grid=grid, in_specs=in_specs, out_specs=out_specs, out_shape=out_shape,
                             scratch_shapes=scratch_shapes, compiler_params=_cparams(*sem), name=name)(*args)
        return res, None
    n_in, n_out, n_sc, na = len(in_specs), len(out_shape), len(scratch_shapes), hosted.na
    nsteps = 1
    for g_ in grid:
        nsteps *= g_
    mid_step = (3 * nsteps) // 4
    i1 = n_in + na
    i2 = i1 + n_out
    i3 = i2 + na
    i4 = i3 + n_sc

    def wrapped(*refs):
        step = pl.program_id(0)
        for ax in range(1, len(grid)):
            step = step * grid[ax] + pl.program_id(ax)
        start, mid, finish = hosted.ops(refs[n_in:i1], refs[i2:i3], refs[i4:])
        pl.when(step == 0)(start)
        if mid is not None:
            pl.when(step == mid_step)(mid)
        body(*refs[:n_in], *refs[i1:i2], *refs[i3:i4])
        pl.when(step == nsteps - 1)(finish)

    res = pl.pallas_call(
        wrapped, grid=grid, in_specs=list(in_specs) + [ANY] * na, out_specs=list(out_specs) + [ANY] * na,
        out_shape=list(out_shape) + hosted.out_shape, scratch_shapes=list(scratch_shapes) + hosted.scratch,
        compiler_params=_cparams(*(("arbitrary",) * len(grid))), name=name)(*args, *hosted.arrays)
    return res[:n_out], res[n_out:]


PACK_ALIGN = 16 * PACK_W


def _pad_to(v, mult):
    n = v.shape[-1]
    extra = (-n) % mult
    if extra == 0:
        return v
    return jnp.concatenate([v, jnp.zeros(v.shape[:-1] + (extra,), v.dtype)], axis=-1)


def _f32_as_bf16_pairs(v):
    return lax.bitcast_convert_type(v.reshape(-1), bf16).reshape(-1)


def _bf16_pairs_as_f32(v):
    return lax.bitcast_convert_type(v.reshape(v.shape[:-1] + (v.shape[-1] // 2, 2)), f32)


def _col_shards(gw):
    lead = gw.shape[:-1]
    n = gw.shape[-1] // N_DEV
    t = gw.reshape(lead + (N_DEV, n))
    t = jnp.moveaxis(t, -2, 0)
    return t.reshape(N_DEV, -1)


def kernel(x, c, ctx, c_ctx, mod_w, mod_b, norm1_w, norm2_w, ssd_w_in, ssd_conv_w, ssd_conv_b, ssd_dt_bias, ssd_a_log, ssd_d, ssd_norm_w, ssd_w_out, conf_w_pw1, conf_b_pw1, conf_w_dw, conf_b_dw, conf_ln_w, conf_ln_b, conf_w_pw2, conf_b_pw2, ffn_w_up, ffn_conv_w, ffn_conv_b, ffn_w_down, final_norm_w, loss_target, m_c_ctx, m_mod_w, m_mod_b, m_norm1_w, m_norm2_w, m_ssd_w_in, m_ssd_conv_w, m_ssd_conv_b, m_ssd_dt_bias, m_ssd_a_log, m_ssd_d, m_ssd_norm_w, m_ssd_w_out, m_conf_w_pw1, m_conf_b_pw1, m_conf_w_dw, m_conf_b_dw, m_conf_ln_w, m_conf_ln_b, m_conf_w_pw2, m_conf_b_pw2, m_ffn_w_up, m_ffn_conv_w, m_ffn_conv_b, m_ffn_w_down, m_final_norm_w, v_c_ctx, v_mod_w, v_mod_b, v_norm1_w, v_norm2_w, v_ssd_w_in, v_ssd_conv_w, v_ssd_conv_b, v_ssd_dt_bias, v_ssd_a_log, v_ssd_d, v_ssd_norm_w, v_ssd_w_out, v_conf_w_pw1, v_conf_b_pw1, v_conf_w_dw, v_conf_b_dw, v_conf_ln_w, v_conf_ln_b, v_conf_w_pw2, v_conf_b_pw2, v_ffn_w_up, v_ffn_conv_w, v_ffn_conv_b, v_ffn_w_down, v_final_norm_w):
    mx, my, mc = _me()
    me = 4 * mx + 2 * my + mc
    L = x.shape[1]
    LC = ctx.shape[1]
    T = LC + L
    w_in_cols = ssd_w_in.shape[2] * N_DEV
    n_dt = w_in_cols - DI - CONVD

    small = [c[0], ssd_conv_w[0], conf_b_pw1[0], conf_w_dw[0], conf_b_dw[0], conf_ln_w[0], conf_ln_b[0], conf_b_pw2[0],
             ffn_conv_w]
    parts = [_f32_as_bf16_pairs(t) for t in small]
    sizes = [p.shape[0] for p in parts]
    small_flat = _pad_to(jnp.concatenate(parts), PACK_ALIGN).reshape(-1, PACK_W)
    w_in, small_g = exchange(GatherExchange([ssd_w_in[0].astype(bf16), small_flat]), "gather_first")
    gather_in_proj = GatherExchange([ssd_w_out[0].astype(bf16), conf_w_pw2[0].astype(bf16)])
    gather_in_conv = GatherExchange([ffn_w_down[0].astype(bf16), conf_w_pw1[0].astype(bf16)])
    gather_in_scan = GatherExchange([ffn_w_up[0].astype(bf16), ffn_w_up[1].astype(bf16)])
    gather_in_gate = GatherExchange([ffn_w_down[1].astype(bf16)])
    w_up, w_down = [None, None], [None, None]
    small_g = small_g.reshape(N_DEV, -1)
    offs = [0]
    for s_ in sizes:
        offs.append(offs[-1] + s_)
    sm = [_bf16_pairs_as_f32(small_g[:, offs[i]:offs[i + 1]]) for i in range(len(sizes))]

    def cols(pc, K):
        return jnp.moveaxis(pc.reshape(N_DEV, K, -1), 0, 1).reshape(K, -1)

    c_all = sm[0]
    conv_w5 = cols(sm[1], 5)
    b_pw1 = sm[2].reshape(1, 2 * D)
    w_dw = cols(sm[3], CONF_K)
    b_dw, ln_w, ln_b, b_pw2 = (sm[i].reshape(1, D) for i in (4, 5, 6, 7))
    fcw = sm[8].reshape(N_DEV, 2, 9, FH // N_DEV)
    ffn_cw = [cols(fcw[:, i].reshape(N_DEV, -1), 9) for i in range(2)]
    in_segs = (DI, CONVD, n_dt)
    up_segs = (FH, FH)
    pw1_segs = (D, D)

    c16 = jnp.concatenate([c_all, jnp.broadcast_to(c_ctx[None, :], (N_DEV, D))], axis=0)
    m_sh = mod_fwd(c16, mod_w, "mod_fwd")
    mod_cols = mod_w.shape[2]
    m_all = allgather_small(m_sh.reshape(2 * CROWS, mod_cols), "gather_mod")
    m_all = jnp.moveaxis(m_all.reshape(N_DEV, 2, CROWS, mod_cols), 0, 2).reshape(2, CROWS, 6 * D) + mod_b[:, None, :]
    m_lat = lax.dynamic_index_in_dim(m_all, me, axis=1, keepdims=False).reshape(2, 6, 1, D)
    m_ctx = m_all[:, N_DEV].reshape(2, 6, 1, D)
    zero_row = jnp.zeros((1, D), f32)

    def ffn_fwd(h, i, tag, hosted=None):
        a2 = modnorm_fwd(h, norm2_w[i][None], m_lat[i, 4][None], m_lat[i, 3][None], 0, f"ffn{tag}_norm")
        val, gate = smm_fwd(a2, w_up[i], None, up_segs, f"ffn{tag}_up")
        act, extra = ffn_gate_fwd(val, gate, ffn_cw[i], ffn_conv_b[i][None], f"ffn{tag}_gate", hosted)
        o2 = matmul(act, w_down[i], "nn", f32, f"ffn{tag}_down")
        h_new = resgate_fwd(h, o2, m_lat[i, 5], zero_row, f"ffn{tag}_res")
        return h_new, (a2, val, gate, act, o2), extra

    def ffn_bwd(dh, h, i, saved, tag):
        a2, val, gate, act, o2 = saved
        do2, dg2, _ = resgate_bwd(dh, o2, m_lat[i, 5], zero_row, f"ffn{tag}_res_bwd")
        g_down = matmul(act, do2, "tn", bf16, f"ffn{tag}_down_dw")
        dact = matmul(do2, w_down[i], "nt", bf16, f"ffn{tag}_down_dx")
        dval, dgate, dcw, dcb = ffn_gate_bwd(val, gate, ffn_cw[i], ffn_conv_b[i][None], dact, f"ffn{tag}_gate_bwd")
        g_up = smm_dw(a2, [dval, dgate], FH // 4, up_segs, 2, f"ffn{tag}_up_dw")
        da2, _ = smm_dx([dval, dgate], w_up[i], None, up_segs, bf16, f"ffn{tag}_up_dx")
        dh_in, dn2, dsc2, dsh2 = modnorm_bwd(h, norm2_w[i][None], m_lat[i, 4][None], m_lat[i, 3][None], da2, dh, 0,
                                             f"ffn{tag}_norm_bwd")
        return dh_in, dict(w_up=g_up, w_down=g_down, conv_w=dcw, conv_b=dcb, norm2=dn2, sh2=dsh2[0], sc2=dsc2[0], g2=dg2)

    nctx = LC // Q
    h0 = jnp.concatenate([ctx[0], x[0]], axis=0)
    sc0 = jnp.stack([m_ctx[0, 1], m_lat[0, 1]])
    sh0 = jnp.stack([m_ctx[0, 0], m_lat[0, 0]])
    a0 = modnorm_fwd(h0, norm1_w[0][None], sc0, sh0, LC // TB, "ssd_norm")
    (z, xbc_pre, dt_raw), (w_out_g, w_pw2_g) = smm_fwd(a0, w_in, None, in_segs, "ssd_in", gather_in_proj)
    w_out = w_out_g.reshape(DI, D)
    w_pw2 = w_pw2_g.reshape(D, D)
    segs = ((0, LC), (LC, L))
    xbc, (w_down0_g, w_pw1) = ssd_conv_fwd(xbc_pre, conv_w5, ssd_conv_b, segs, "ssd_conv", gather_in_conv)
    w_down[0] = w_down0_g.reshape(FH, D)
    dt4 = dt_raw[:, :n_dt].reshape(T, 2, G, HPG)
    dtc = jnp.transpose(dt4, (1, 2, 0, 3))
    dtr = jnp.transpose(dt4, (1, 2, 3, 0))
    bias3 = ssd_dt_bias[0].reshape(2, G, HPG)
    alog3 = ssd_a_log[0].reshape(2, G, HPG)
    bc_, br_ = bias3[:, :, None, :], bias3[:, :, :, None]
    alc, alr = alog3[:, :, None, :], alog3[:, :, :, None]
    (y2, s_in_all), (w_up[0], w_up[1]) = ssd_scan_fwd(xbc, dtc, dtr, bc_, br_, alc, alr, nctx, "ssd_scan", gather_in_scan)
    dexp = jnp.repeat(ssd_d[0], P)[None, :]
    yn = ssd_gate_fwd(y2, xbc, z, dexp, ssd_norm_w, LC // GTB, "ssd_gate")
    o_ssd = matmul(yn, w_out, "nn", f32, "ssd_out")
    hx = x[0]
    h1 = resgate_fwd(hx, o_ssd, m_lat[0, 2], zero_row, "ssd_res")
    h2, ffn0_saved, (w_down1_g,) = ffn_fwd(h1, 0, "0", gather_in_gate)
    w_down[1] = w_down1_g.reshape(FH, D)

    a1 = modnorm_fwd(h2, norm1_w[1][None], m_lat[1, 1][None], m_lat[1, 0][None], 0, "conf_norm")
    pa, pg = smm_fwd(a1, w_pw1, None, pw1_segs, "conf_pw1")
    dwc, _ = conf_glu_conv_fwd(pa, pg, b_pw1, w_dw, b_dw, "conf_conv")
    s1 = ln_silu_fwd(dwc, ln_w, ln_b, "conf_ln")
    o_conf = matmul(s1, w_pw2, "nn", f32, "conf_pw2")
    h3 = resgate_fwd(h2, o_conf, m_lat[1, 2], b_pw2, "conf_res")
    h4, ffn1_saved, _ = ffn_fwd(h3, 1, "1")

    loss_part, dh4, g_final = final_loss(h4, final_norm_w[None], loss_target[0], "loss_head")
    dh3, gf1 = ffn_bwd(dh4, h3, 1, ffn1_saved, "1")

    do_conf, dg1_1, g_b_pw2 = resgate_bwd(dh3, o_conf, m_lat[1, 2], b_pw2, "conf_res_bwd")
    g_pw2 = matmul(s1, do_conf, "tn", bf16, "conf_pw2_dw")
    ds1 = matmul(do_conf, w_pw2, "nt", bf16, "conf_pw2_dx")
    ddwc, g_ln_w, g_ln_b = ln_silu_bwd(dwc, ln_w, ln_b, ds1, "conf_ln_bwd")
    dpa, dpg, dba, dbg, g_w_dw, g_b_dw = conf_glu_conv_bwd(pa, pg, b_pw1, w_dw, ddwc, "conf_conv_bwd")
    g_b_pw1 = jnp.concatenate([dba, dbg], axis=1)
    g_pw1 = smm_dw(a1, [dpa, dpg], 2 * D // N_DEV, pw1_segs, 1, "conf_pw1_dw")
    da1, _ = smm_dx([dpa, dpg], w_pw1, None, pw1_segs, bf16, "conf_pw1_dx")
    dh2, g_n1_1, dsc1_1, dsh1_1 = modnorm_bwd(h2, norm1_w[1][None], m_lat[1, 1][None], m_lat[1, 0][None], da1, dh3, 0,
                                              "conf_norm_bwd")
    dh1, gf0 = ffn_bwd(dh2, h1, 0, ffn0_saved, "0")

    do_ssd, dg1_0, _ = resgate_bwd(dh1, o_ssd, m_lat[0, 2], zero_row, "ssd_res_bwd")
    g_w_out = matmul(yn, do_ssd, "tn", bf16, "ssd_out_dw")
    dyn = matmul(do_ssd, w_out, "nt", bf16, "ssd_out_dx")
    core = mc.reshape(1).astype(jnp.int32)

    def by_device(t):
        return t.reshape((4, 2, -1, t.shape[-1]))

    early = [by_device(t) for t in (gf1["w_up"], gf1["w_down"], g_pw2, g_pw1, gf0["w_up"], gf0["w_down"], g_w_out)]
    (dy, dx_skip, dz, g_dexp, g_ssd_norm), early_sib = ssd_gate_bwd(
        y2, xbc, z, dexp, ssd_norm_w, dyn, LC // GTB, "ssd_gate_bwd", SiblingExchange(early))
    early_part = [add_own(t, r_, core, f"reduce_add{i}") for i, (t, r_) in enumerate(zip(early, early_sib))]
    (dxbc2, ddtc, ddtr, dbc, dbr, dalc, dalr), early_red = ssd_scan_bwd(
        xbc, dtc, dtr, bc_, br_, alc, alr, s_in_all, dy, nctx, "ssd_scan_bwd", ChipsExchange(early_part))
    ddt = (jnp.transpose(ddtc, (2, 0, 1, 3)) + jnp.transpose(ddtr, (3, 0, 1, 2))).reshape(T, n_dt)
    g_dt_bias = (dbc[:, :, 0, :] + dbr[:, :, :, 0]).reshape(2, NH_SSD)
    g_a_log = (dalc[:, :, 0, :] + dalr[:, :, :, 0]).reshape(2, NH_SSD)
    g_ssd_d = g_dexp[0, :NH_SSD]
    du, g_conv_w5, g_conv_b5 = ssd_conv_bwd(xbc_pre, conv_w5, ssd_conv_b, dxbc2, dx_skip, segs, "ssd_conv_bwd")
    ddt_p = _pad_to(ddt, 128).astype(bf16)
    g_w_in = smm_dw(a0, [dz, du, ddt_p], w_in.shape[-1], in_segs, 2, "ssd_in_dw")
    g_ffn_cw = jnp.stack([gf0["conv_w"], gf1["conv_w"]])
    small_shards = [_col_shards(t) for t in (g_conv_w5, g_b_pw1, g_w_dw, g_b_dw, g_ln_w, g_ln_b, g_b_pw2, g_ffn_cw)]
    gsizes = [s_.shape[1] for s_ in small_shards]
    g_small = _pad_to(jnp.concatenate(small_shards, axis=1), PACK_ALIGN).astype(bf16)
    late = [by_device(g_w_in), by_device(g_small.reshape(N_DEV, -1, PACK_W))]
    da0, late_sib = smm_dx([dz, du, ddt_p], w_in, None, in_segs, f32, "ssd_in_dx", SiblingExchange(late))
    late_part = [add_own(t, r_, core, f"reduce_add_late{i}") for i, (t, r_) in enumerate(zip(late, late_sib))]
    late_red = exchange(ChipsExchange(late_part), "reduce_chips_late")
    dres0 = jnp.concatenate([jnp.zeros((LC, D), f32), dh1], axis=0)
    dh0, g_n1_0, dsc1_0, dsh1_0 = modnorm_bwd(h0, norm1_w[0][None], sc0, sh0, da0, dres0, LC // TB, "ssd_norm_bwd")
    grad_x = dh0[LC:][None]

    zeros_d = jnp.zeros((1, D), f32)
    dm_lat = jnp.stack([
        jnp.concatenate([dsh1_0[1], dsc1_0[1], dg1_0, gf0["sh2"], gf0["sc2"], gf0["g2"]], axis=1),
        jnp.concatenate([dsh1_1[0], dsc1_1[0], dg1_1, gf1["sh2"], gf1["sc2"], gf1["g2"]], axis=1)])
    dm_ctx = jnp.stack([
        jnp.concatenate([dsh1_0[0], dsc1_0[0]] + [zeros_d] * 4, axis=1), jnp.zeros((1, 6 * D), f32)])
    dm_mine = jnp.concatenate([dm_lat.reshape(2, 6 * D), dm_ctx.reshape(2, 6 * D),
                               jnp.zeros((4, 6 * D), f32)], axis=0)
    dm_g = allgather_small(dm_mine, "gather_dmod")
    dm_all = jnp.concatenate([jnp.moveaxis(dm_g[:, 0:2], 0, 1), jnp.moveaxis(dm_g[:, 2:4], 0, 1)], axis=1)
    dm_sh = lax.dynamic_slice_in_dim(dm_all, me * mod_cols, mod_cols, axis=2)
    g_mod_w, g_cctx_part, g_mod_b = mod_bwd(c16, mod_w, dm_sh, dm_all, "mod_bwd")

    rep = [jnp.stack([g_n1_0[0], g_n1_1[0]]), jnp.stack([gf0["norm2"][0], gf1["norm2"][0]]), g_conv_b5, g_dt_bias, g_a_log,
           g_ssd_d, g_ssd_norm, jnp.stack([gf0["conv_b"][0], gf1["conv_b"][0]]), g_final, g_cctx_part, loss_part[:, :1]]
    rep_sizes = [r_.size for r_ in rep]
    rep_flat = _pad_to(jnp.concatenate([r_.reshape(-1) for r_ in rep]), 8 * PACK_W).reshape(-1, PACK_W)
    _, rep_sum = allgather_small(rep_flat, "reduce_replicated", with_sum=True)
    rep_sum = rep_sum.reshape(-1)
    roffs = [0]
    for s_ in rep_sizes:
        roffs.append(roffs[-1] + s_)
    rp = [rep_sum[roffs[i]:roffs[i + 1]] for i in range(len(rep_sizes))]
    loss = rp[10].reshape(())

    r_up1, r_down1, r_pw2, r_pw1, r_up0, r_down0, r_out = early_red
    r_in, r_small = late_red
    from_chips = [r_in, r_up0, r_up1, r_pw1, r_out, r_down0, r_down1, r_pw2]
    g_flat = sum_rows(r_small, "reduce_sum_small").reshape(-1)
    goffs = [0]
    for s_ in gsizes:
        goffs.append(goffs[-1] + s_)
    gs = [g_flat[goffs[i]:goffs[i + 1]] for i in range(len(gsizes))]

    big = {}
    big["ssd_w_in"] = sum_adamw(from_chips[0], ssd_w_in[0], m_ssd_w_in[0], v_ssd_w_in[0], None, "adamw_ssd_w_in")
    up = [sum_adamw(from_chips[1 + i], ffn_w_up, m_ffn_w_up, v_ffn_w_up, i, f"adamw_ffn_w_up{i}") for i in range(2)]
    big["ffn_w_up"] = tuple(jnp.stack([up[0][k], up[1][k]]) for k in range(4))
    big["conf_w_pw1"] = sum_adamw(from_chips[3], conf_w_pw1[0], m_conf_w_pw1[0], v_conf_w_pw1[0], None, "adamw_conf_w_pw1")
    big["ssd_w_out"] = sum_adamw(from_chips[4], ssd_w_out[0], m_ssd_w_out[0], v_ssd_w_out[0], None, "adamw_ssd_w_out")
    dn = [sum_adamw(from_chips[5 + i], ffn_w_down, m_ffn_w_down, v_ffn_w_down, i, f"adamw_ffn_w_down{i}") for i in range(2)]
    big["ffn_w_down"] = tuple(jnp.stack([dn[0][k], dn[1][k]]) for k in range(4))
    big["conf_w_pw2"] = sum_adamw(from_chips[7], conf_w_pw2[0], m_conf_w_pw2[0], v_conf_w_pw2[0], None, "adamw_conf_w_pw2")
    grads = {
        "c_ctx": rp[9], "mod_w": g_mod_w, "mod_b": g_mod_b, "norm1_w": rp[0], "norm2_w": rp[1],
        "ssd_conv_w": gs[0], "ssd_conv_b": rp[2], "ssd_dt_bias": rp[3], "ssd_a_log": rp[4], "ssd_d": rp[5],
        "ssd_norm_w": rp[6], "conf_b_pw1": gs[1], "conf_w_dw": gs[2],
        "conf_b_dw": gs[3], "conf_ln_w": gs[4], "conf_ln_b": gs[5], "conf_b_pw2": gs[6],
        "ffn_conv_w": gs[7], "ffn_conv_b": rp[7], "final_norm_w": rp[8],
    }
    weights = dict(c_ctx=c_ctx, mod_w=mod_w, mod_b=mod_b, norm1_w=norm1_w, norm2_w=norm2_w, ssd_w_in=ssd_w_in, ssd_conv_w=ssd_conv_w, ssd_conv_b=ssd_conv_b, ssd_dt_bias=ssd_dt_bias, ssd_a_log=ssd_a_log, ssd_d=ssd_d, ssd_norm_w=ssd_norm_w, ssd_w_out=ssd_w_out, conf_w_pw1=conf_w_pw1, conf_b_pw1=conf_b_pw1, conf_w_dw=conf_w_dw, conf_b_dw=conf_b_dw, conf_ln_w=conf_ln_w, conf_ln_b=conf_ln_b, conf_w_pw2=conf_w_pw2, conf_b_pw2=conf_b_pw2, ffn_w_up=ffn_w_up, ffn_conv_w=ffn_conv_w, ffn_conv_b=ffn_conv_b, ffn_w_down=ffn_w_down, final_norm_w=final_norm_w)
    m_in = dict(c_ctx=m_c_ctx, mod_w=m_mod_w, mod_b=m_mod_b, norm1_w=m_norm1_w, norm2_w=m_norm2_w, ssd_w_in=m_ssd_w_in, ssd_conv_w=m_ssd_conv_w, ssd_conv_b=m_ssd_conv_b, ssd_dt_bias=m_ssd_dt_bias, ssd_a_log=m_ssd_a_log, ssd_d=m_ssd_d, ssd_norm_w=m_ssd_norm_w, ssd_w_out=m_ssd_w_out, conf_w_pw1=m_conf_w_pw1, conf_b_pw1=m_conf_b_pw1, conf_w_dw=m_conf_w_dw, conf_b_dw=m_conf_b_dw, conf_ln_w=m_conf_ln_w, conf_ln_b=m_conf_ln_b, conf_w_pw2=m_conf_w_pw2, conf_b_pw2=m_conf_b_pw2, ffn_w_up=m_ffn_w_up, ffn_conv_w=m_ffn_conv_w, ffn_conv_b=m_ffn_conv_b, ffn_w_down=m_ffn_w_down, final_norm_w=m_final_norm_w)
    v_in = dict(c_ctx=v_c_ctx, mod_w=v_mod_w, mod_b=v_mod_b, norm1_w=v_norm1_w, norm2_w=v_norm2_w, ssd_w_in=v_ssd_w_in, ssd_conv_w=v_ssd_conv_w, ssd_conv_b=v_ssd_conv_b, ssd_dt_bias=v_ssd_dt_bias, ssd_a_log=v_ssd_a_log, ssd_d=v_ssd_d, ssd_norm_w=v_ssd_norm_w, ssd_w_out=v_ssd_w_out, conf_w_pw1=v_conf_w_pw1, conf_b_pw1=v_conf_b_pw1, conf_w_dw=v_conf_w_dw, conf_b_dw=v_conf_b_dw, conf_ln_w=v_conf_ln_w, conf_ln_b=v_conf_ln_b, conf_w_pw2=v_conf_w_pw2, conf_b_pw2=v_conf_b_pw2, ffn_w_up=v_ffn_w_up, ffn_conv_w=v_ffn_conv_w, ffn_conv_b=v_ffn_conv_b, ffn_w_down=v_ffn_w_down, final_norm_w=v_final_norm_w)

    out_g, out_d, out_m, out_v = [], [], [], []
    for name_, w_ in weights.items():
        shape = w_.shape
        if name_ in big:
            for lst, t in zip((out_g, out_d, out_m, out_v), big[name_]):
                lst.append(t.reshape(shape))
            continue
        cols2 = shape[-1] if len(shape) > 1 else shape[0]
        g2 = grads[name_].reshape(-1, cols2)
        d_, nm_, nv_ = adamw(w_.reshape(-1, cols2), g2, m_in[name_].reshape(-1, cols2), v_in[name_].reshape(-1, cols2),
                             f"adamw_{name_}")
        out_g.append(g2.reshape(shape))
        out_d.append(d_.reshape(shape))
        out_m.append(nm_.reshape(shape))
        out_v.append(nv_.reshape(shape))
    return (loss, grad_x, *out_g, *out_d, *out_m, *out_v)
```

```python
import functools

import jax
import jax.numpy as jnp
from jax import lax
from jax.experimental import pallas as pl
from jax.experimental.pallas import tpu as pltpu

f32 = jnp.float32
bf16 = jnp.bfloat16
HI = lax.Precision.HIGHEST
S = jax.ShapeDtypeStruct
MESH = pl.DeviceIdType.MESH
ANY = pl.BlockSpec(memory_space=pl.ANY)
VMEM = pl.BlockSpec(memory_space=pltpu.VMEM)

N_DEV = 8
D = 1024
DI = 2048
CONVD = 4096
FH = 2816
GRID_W = 64
Q = 128
HPG = 4
P = 64
N = 128
G = 8
GW = HPG * P
NH_SSD = G * HPG
EPS = 1e-6
ADAM_LR, ADAM_B1, ADAM_B2, ADAM_EPS, ADAM_WD, ADAM_STEP = 0.001, 0.9, 0.999, 1e-08, 0.01, 10
VMEM_LIMIT_BYTES = 56 * 1024 * 1024
PACK_W = 1024
TB = 256


def _cparams(*sem):
    return pltpu.CompilerParams(dimension_semantics=sem, vmem_limit_bytes=VMEM_LIMIT_BYTES)


def _pick(n, prefs):
    for p in prefs:
        if n % p == 0:
            return p
    return n


def _sigmoid(x):
    return 1.0 / (1.0 + jnp.exp(-x))


def _softplus(x):
    return jnp.maximum(x, 0.0) + jnp.log(1.0 + jnp.exp(-jnp.abs(x)))


def matmul(a, b, mode, out_dtype, name):
    if mode == "nn":
        (M, K), (_, Nn) = a.shape, b.shape
        bm, bn, bk = _pick(M, (512, 384, 256, 128)), Nn, K
    elif mode == "tn":
        (K, M), (_, Nn) = a.shape, b.shape
        bm, bn, bk = M, Nn, _pick(K, (256, 128))
    else:
        (M, K), (Nn, _) = a.shape, b.shape
        bm, bn, bk = _pick(M, (512, 384, 256, 128)), Nn, K
    nk = K // bk
    dims = {"nn": (((1,), (0,)), ((), ())), "tn": (((0,), (0,)), ((), ())), "nt": (((1,), (1,)), ((), ()))}[mode]

    def body(a_ref, b_ref, o_ref, acc_ref):
        k = pl.program_id(2)

        @pl.when(k == 0)
        def _():
            acc_ref[...] = jnp.zeros_like(acc_ref)

        acc_ref[...] += lax.dot_general(a_ref[...].astype(bf16), b_ref[...].astype(bf16), dims,
                                        preferred_element_type=f32)

        @pl.when(k == nk - 1)
        def _():
            o_ref[...] = acc_ref[...].astype(out_dtype)

    if mode == "nn":
        a_spec = pl.BlockSpec((bm, bk), lambda i, j, k: (i, k))
        b_spec = pl.BlockSpec((bk, bn), lambda i, j, k: (k, j))
    elif mode == "tn":
        a_spec = pl.BlockSpec((bk, bm), lambda i, j, k: (k, i))
        b_spec = pl.BlockSpec((bk, bn), lambda i, j, k: (k, j))
    else:
        a_spec = pl.BlockSpec((bm, bk), lambda i, j, k: (i, k))
        b_spec = pl.BlockSpec((bn, bk), lambda i, j, k: (j, k))
    return pl.pallas_call(
        body, grid=(M // bm, Nn // bn, nk), in_specs=[a_spec, b_spec],
        out_specs=pl.BlockSpec((bm, bn), lambda i, j, k: (i, j)),
        out_shape=S((M, Nn), out_dtype), scratch_shapes=[pltpu.VMEM((bm, bn), f32)],
        compiler_params=_cparams("parallel", "parallel", "arbitrary"), name=name,
    )(a, b)


SMM_BM = 256
SMM_ROWS = (512, 384, 256)


def _shard_pieces(seg_widths, n):
    bounds = [0]
    for sw in seg_widths:
        bounds.append(bounds[-1] + sw)
    assert bounds[-1] == N_DEV * n, (seg_widths, n)
    out = []
    for j in range(N_DEV):
        lo, hi = j * n, (j + 1) * n
        pcs = []
        for si in range(len(seg_widths)):
            a, b = max(lo, bounds[si]), min(hi, bounds[si + 1])
            if a < b:
                pcs.append((si, a - bounds[si], a - lo, b - a))
        out.append(pcs)
    return out


def _w_spec(w, layer):
    if layer is None:
        return pl.BlockSpec(w.shape, lambda *idx: (0, 0, 0))
    return pl.BlockSpec((N_DEV, None) + w.shape[2:], lambda *idx: (0, layer, 0, 0))


def smm_fwd(a, w, layer, seg_widths, name, hosted=None):
    M, K = a.shape
    n = w.shape[-1]
    pieces = _shard_pieces(seg_widths, n)
    padded = [sw + (-sw) % 128 for sw in seg_widths]
    bm = _pick(M, SMM_ROWS)

    def body(a_ref, w_ref, *o_refs):
        av = a_ref[...]
        for si, sw in enumerate(seg_widths):
            if padded[si] != sw:
                o_refs[si][:, pl.ds(padded[si] - 128, 128)] = jnp.zeros((bm, 128), f32)
        for j in range(N_DEV):
            for si, soff, woff, wd in pieces[j]:
                o_refs[si][:, pl.ds(soff, wd)] = jnp.dot(av, w_ref[j, :, pl.ds(woff, wd)], preferred_element_type=f32)

    outs, extra = _host_call(
        body, (M // bm,), [pl.BlockSpec((bm, K), lambda i: (i, 0)), _w_spec(w, layer)],
        [pl.BlockSpec((bm, pw), lambda i: (i, 0)) for pw in padded], [S((M, pw), f32) for pw in padded], [],
        ("parallel",), name, (a, w), hosted)
    return outs if hosted is None else (outs, extra)


def smm_dx(d_segs, w, layer, seg_widths, out_dtype, name, hosted=None):
    M = d_segs[0].shape[0]
    K, n = w.shape[-2], w.shape[-1]
    pieces = _shard_pieces(seg_widths, n)
    ns = len(d_segs)
    bm = _pick(M, SMM_ROWS)

    def body(*refs):
        d_refs, w_ref, o_ref = refs[:ns], refs[ns], refs[ns + 1]
        acc = jnp.zeros((bm, K), f32)
        for j in range(N_DEV):
            for si, soff, woff, wd in pieces[j]:
                acc = acc + lax.dot_general(d_refs[si][:, pl.ds(soff, wd)], w_ref[j, :, pl.ds(woff, wd)],
                                            (((1,), (1,)), ((), ())), preferred_element_type=f32)
        o_ref[...] = acc.astype(out_dtype)

    (out,), extra = _host_call(
        body, (M // bm,),
        [pl.BlockSpec((bm, d.shape[1]), lambda i: (i, 0)) for d in d_segs] + [_w_spec(w, layer)],
        [pl.BlockSpec((bm, K), lambda i: (i, 0))], [S((M, K), out_dtype)], [], ("parallel",), name,
        (*d_segs, w), hosted)
    return out, extra


def smm_dw(a, d_segs, n, seg_widths, ngrp, name):
    M, K = a.shape
    pieces = _shard_pieces(seg_widths, n)
    per = N_DEV // ngrp
    nI = M // SMM_BM
    ns = len(d_segs)

    def body(*refs):
        a_ref, d_refs, o_ref, acc_ref = refs[0], refs[1:1 + ns], refs[1 + ns], refs[2 + ns]
        grp = pl.program_id(0)
        i = pl.program_id(1)

        @pl.when(i == 0)
        def _():
            acc_ref[...] = jnp.zeros_like(acc_ref)

        av = a_ref[...]
        for gs in range(ngrp):
            def one_group(gs=gs):
                for jj in range(per):
                    for si, soff, woff, wd in pieces[gs * per + jj]:
                        acc_ref[jj, :, pl.ds(woff, wd)] += lax.dot_general(
                            av, d_refs[si][:, pl.ds(soff, wd)], (((0,), (0,)), ((), ())), preferred_element_type=f32)
            pl.when(grp == gs)(one_group)

        @pl.when(i == nI - 1)
        def _():
            o_ref[...] = acc_ref[...].astype(bf16)

    return pl.pallas_call(
        body, grid=(ngrp, nI),
        in_specs=[pl.BlockSpec((SMM_BM, K), lambda g, i: (i, 0))]
        + [pl.BlockSpec((SMM_BM, d.shape[1]), lambda g, i: (i, 0)) for d in d_segs],
        out_specs=pl.BlockSpec((per, K, n), lambda g, i: (g, 0, 0)), out_shape=S((N_DEV, K, n), bf16),
        scratch_shapes=[pltpu.VMEM((per, K, n), f32)],
        compiler_params=_cparams("arbitrary", "arbitrary"), name=name)(a, *d_segs)


def _modnorm_f(h, w, sc, sh):
    y = h * lax.rsqrt(jnp.mean(h * h, axis=-1, keepdims=True) + EPS)
    return (y * w) * (1.0 + sc) + sh


def _kind_specs(nctxb):
    if nctxb > 0:
        return pl.BlockSpec((None, 1, D), lambda i: (jnp.where(i < nctxb, 0, 1), 0, 0))
    return pl.BlockSpec((None, 1, D), lambda i: (0, 0, 0))


def modnorm_fwd(h, w, sc, sh, nctxb, name):
    T = h.shape[0]

    def body(h_ref, w_ref, sc_ref, sh_ref, o_ref):
        o_ref[...] = _modnorm_f(h_ref[...], w_ref[...], sc_ref[...], sh_ref[...]).astype(bf16)

    blk = pl.BlockSpec((TB, D), lambda i: (i, 0))
    row = pl.BlockSpec((1, D), lambda i: (0, 0))
    ks = _kind_specs(nctxb)
    return pl.pallas_call(body, grid=(T // TB,), in_specs=[blk, row, ks, ks], out_specs=blk,
                          out_shape=S((T, D), bf16), compiler_params=_cparams("parallel"), name=name)(h, w, sc, sh)


def modnorm_bwd(h, w, sc, sh, da, dres, nctxb, name):
    T = h.shape[0]
    kinds = sc.shape[0]

    def body(h_ref, w_ref, sc_ref, sh_ref, da_ref, dres_ref, dh_ref, dw_ref, dsc_ref, dsh_ref):
        i = pl.program_id(0)
        _, vjp = jax.vjp(_modnorm_f, h_ref[...], w_ref[...], sc_ref[...], sh_ref[...])
        dh, dw, dsc, dsh = vjp(da_ref[...].astype(f32))
        dh_ref[...] = dres_ref[...] + dh

        @pl.when(i == 0)
        def _():
            dw_ref[...] = jnp.zeros_like(dw_ref)

        @pl.when((i == 0) | (i == nctxb))
        def _():
            dsc_ref[...] = jnp.zeros_like(dsc_ref)
            dsh_ref[...] = jnp.zeros_like(dsh_ref)

        dw_ref[...] += dw
        dsc_ref[...] += dsc
        dsh_ref[...] += dsh

    blk = pl.BlockSpec((TB, D), lambda i: (i, 0))
    row = pl.BlockSpec((1, D), lambda i: (0, 0))
    ks = _kind_specs(nctxb)
    return pl.pallas_call(
        body, grid=(T // TB,), in_specs=[blk, row, ks, ks, blk, blk], out_specs=[blk, row, ks, ks],
        out_shape=[S((T, D), f32), S((1, D), f32), S((kinds, 1, D), f32), S((kinds, 1, D), f32)],
        compiler_params=_cparams("arbitrary"), name=name)(h, w, sc, sh, da, dres)


def resgate_fwd(h, o, g, b, name):
    T = h.shape[0]

    def body(h_ref, o_ref, g_ref, b_ref, out_ref):
        out_ref[...] = h_ref[...] + g_ref[...] * (o_ref[...] + b_ref[...])

    blk = pl.BlockSpec((TB, D), lambda i: (i, 0))
    row = pl.BlockSpec((1, D), lambda i: (0, 0))
    return pl.pallas_call(body, grid=(T // TB,), in_specs=[blk, blk, row, row], out_specs=blk,
                          out_shape=S((T, D), f32), compiler_params=_cparams("parallel"), name=name)(h, o, g, b)


def resgate_bwd(dh, o, g, b, name):
    T = dh.shape[0]

    def body(dh_ref, o_ref, g_ref, b_ref, do_ref, dg_ref, db_ref):
        i = pl.program_id(0)

        @pl.when(i == 0)
        def _():
            dg_ref[...] = jnp.zeros_like(dg_ref)
            db_ref[...] = jnp.zeros_like(db_ref)

        dh = dh_ref[...]
        do = g_ref[...] * dh
        do_ref[...] = do.astype(bf16)
        dg_ref[...] += jnp.sum(dh * (o_ref[...] + b_ref[...]), axis=0, keepdims=True)
        db_ref[...] += jnp.sum(do, axis=0, keepdims=True)

    blk = pl.BlockSpec((TB, D), lambda i: (i, 0))
    row = pl.BlockSpec((1, D), lambda i: (0, 0))
    return pl.pallas_call(body, grid=(T // TB,), in_specs=[blk, blk, row, row], out_specs=[blk, row, row],
                          out_shape=[S((T, D), bf16), S((1, D), f32), S((1, D), f32)],
                          compiler_params=_cparams("arbitrary"), name=name)(dh, o, g, b)


def final_loss(h, w, tgt, name):
    T = h.shape[0]

    def f(hv, wv, tv):
        y = (hv * lax.rsqrt(jnp.mean(hv * hv, axis=-1, keepdims=True) + EPS)) * wv
        e = y - tv
        return 0.5 * jnp.sum(jnp.sum(e * e, axis=-1, keepdims=True), axis=0, keepdims=True) * (1.0 / D)

    def body(h_ref, w_ref, t_ref, loss_ref, dh_ref, dw_ref):
        i = pl.program_id(0)
        tv = t_ref[...]
        val, vjp = jax.vjp(lambda a, b_: f(a, b_, tv), h_ref[...], w_ref[...])
        dh, dw = vjp(jnp.ones((1, 1), f32))
        dh_ref[...] = dh

        @pl.when(i == 0)
        def _():
            loss_ref[...] = jnp.zeros_like(loss_ref)
            dw_ref[...] = jnp.zeros_like(dw_ref)

        loss_ref[...] += jnp.broadcast_to(val, (1, 128))
        dw_ref[...] += dw

    blk = pl.BlockSpec((TB, D), lambda i: (i, 0))
    row = pl.BlockSpec((1, D), lambda i: (0, 0))
    return pl.pallas_call(body, grid=(T // TB,), in_specs=[blk, row, blk],
                          out_specs=[pl.BlockSpec((1, 128), lambda i: (0, 0)), blk, row],
                          out_shape=[S((1, 128), f32), S((T, D), f32), S((1, D), f32)],
                          compiler_params=_cparams("arbitrary"), name=name)(h, w, tgt)


CB = 256
RT = 32
RTB = 16


def _fold8(t):
    acc = t[0:8]
    for k in range(1, t.shape[0] // 8):
        acc = acc + t[8 * k:8 * (k + 1)]
    return acc


def _rows(start, off=0, rt=RT):
    return pl.ds(pl.multiple_of(start + off, 8), rt)


def _rowsb(start, off=0):
    return _rows(start, off, RTB)


def _zero_rows(ref, start, n):
    ref[pl.ds(start, n), :] = jnp.zeros((n, ref.shape[1]), f32)


K5, HALF5, PAD5 = 5, 2, 8


def _shift_copies5(base_ref, s_ref, ln, sign):
    for k in range(K5):
        s_ref[k, pl.ds(0, ln), :] = base_ref[pl.ds(PAD5 + sign * (k - HALF5), ln), :]


def ssd_conv_fwd(u, w, b, segs, name, hosted=None):
    T = u.shape[0]
    maxlen = max(ln for _, ln in segs)

    def body(u_ref, w_ref, b_ref, o_ref, base_ref, s_ref):
        wv = [w_ref[pl.ds(k, 1), :] for k in range(K5)]
        bv = b_ref[...]
        for s0, ln in segs:
            _zero_rows(base_ref, 0, PAD5)
            _zero_rows(base_ref, PAD5 + ln, PAD5)
            base_ref[pl.ds(PAD5, ln), :] = u_ref[pl.ds(s0, ln), :]
            _shift_copies5(base_ref, s_ref, ln, 1)

            def tile(i, carry):
                r = i * RT
                acc = jnp.broadcast_to(bv, (RT, CB))
                for k in range(K5):
                    acc = acc + s_ref[k, _rows(r), :] * wv[k]
                o_ref[_rows(r, s0), :] = acc * _sigmoid(acc)
                return carry

            lax.fori_loop(0, ln // RT, tile, 0, unroll=2)

    (out,), extra = _host_call(
        body, (CONVD // CB,),
        [pl.BlockSpec((T, CB), lambda j: (0, j)), pl.BlockSpec((K5, CB), lambda j: (0, j)),
         pl.BlockSpec((1, CB), lambda j: (0, j))],
        [pl.BlockSpec((T, CB), lambda j: (0, j))], [S((T, CONVD), f32)],
        [pltpu.VMEM((maxlen + 2 * PAD5, CB), f32), pltpu.VMEM((K5, maxlen, CB), f32)],
        ("parallel",), name, (u, w, b), hosted)
    return out, extra


def ssd_conv_bwd(proj, w, b, dy2, dskip, segs, name):
    T = proj.shape[0]
    maxlen = max(ln for _, ln in segs)
    nskip = DI // CB

    def body(u_ref, w_ref, b_ref, dya_ref, dyb_ref, dsk_ref, du_ref, dw_ref, db_ref, base_ref, s_ref):
        wv = [w_ref[pl.ds(k, 1), :] for k in range(K5)]
        bv = b_ref[...]
        has_skip = (pl.program_id(0) < nskip).astype(f32)
        acc8 = tuple(jnp.zeros((8, CB), f32) for _ in range(K5 + 1))
        for s0, ln in segs:
            _zero_rows(base_ref, 0, PAD5)
            _zero_rows(base_ref, PAD5 + ln, PAD5)
            base_ref[pl.ds(PAD5, ln), :] = u_ref[pl.ds(s0, ln), :]
            _shift_copies5(base_ref, s_ref, ln, 1)

            def tile1(i, carry):
                r = i * RTB
                taps = [s_ref[k, _rowsb(r), :] for k in range(K5)]
                pre = jnp.broadcast_to(bv, (RTB, CB))
                for k in range(K5):
                    pre = pre + taps[k] * wv[k]
                sg = _sigmoid(pre)
                dy = dya_ref[_rowsb(r, s0), :] + dyb_ref[_rowsb(r, s0), :] + has_skip * dsk_ref[_rowsb(r, s0), :]
                dpre = dy * (sg * (1.0 + pre * (1.0 - sg)))
                base_ref[_rowsb(r, PAD5), :] = dpre
                new = [carry[k] + _fold8(dpre * taps[k]) for k in range(K5)]
                new.append(carry[K5] + _fold8(dpre))
                return tuple(new)

            acc8 = lax.fori_loop(0, ln // RTB, tile1, acc8, unroll=2)
            _shift_copies5(base_ref, s_ref, ln, -1)

            def tile2(i, carry):
                r = i * RTB
                du = jnp.zeros((RTB, CB), f32)
                for k in range(K5):
                    du = du + s_ref[k, _rowsb(r), :] * wv[k]
                du_ref[_rowsb(r, s0), :] = du.astype(bf16)
                return carry

            lax.fori_loop(0, ln // RTB, tile2, 0, unroll=4)
        for k in range(K5):
            dw_ref[pl.ds(k, 1), :] = jnp.sum(acc8[k], axis=0, keepdims=True)
        db_ref[...] = jnp.sum(acc8[K5], axis=0, keepdims=True)

    cblk = pl.BlockSpec((T, CB), lambda j: (0, j))
    return pl.pallas_call(
        body, grid=(CONVD // CB,),
        in_specs=[cblk, pl.BlockSpec((K5, CB), lambda j: (0, j)), pl.BlockSpec((1, CB), lambda j: (0, j)),
                  pl.BlockSpec((None, T, CB), lambda j: (0, 0, j)), pl.BlockSpec((None, T, CB), lambda j: (1, 0, j)),
                  pl.BlockSpec((T, CB), lambda j: (0, jnp.minimum(j, nskip - 1)))],
        out_specs=[cblk, pl.BlockSpec((K5, CB), lambda j: (0, j)), pl.BlockSpec((1, CB), lambda j: (0, j))],
        out_shape=[S((T, CONVD), bf16), S((K5, CONVD), f32), S((1, CONVD), f32)],
        scratch_shapes=[pltpu.VMEM((maxlen + 2 * PAD5, CB), f32), pltpu.VMEM((K5, maxlen, CB), f32)],
        compiler_params=_cparams("parallel"), name=name)(proj, w, b, dy2, dy2, dskip)


GPAD = GRID_W


def _grid_copies(g_ref, src, L):
    col = lax.broadcasted_iota(jnp.int32, (L, CB), 0) & (GRID_W - 1)
    for d in range(3):
        _zero_rows(g_ref.at[d], 0, GPAD)
        _zero_rows(g_ref.at[d], GPAD + L, GPAD)
    g_ref[1, pl.ds(GPAD, L), :] = src
    g_ref[0, pl.ds(GPAD, L), :] = jnp.where(col != 0, g_ref[1, pl.ds(GPAD - 1, L), :], 0.0)
    g_ref[2, pl.ds(GPAD, L), :] = jnp.where(col != GRID_W - 1, g_ref[1, pl.ds(GPAD + 1, L), :], 0.0)


def ffn_gate_fwd(val, gate, cw, cb_, name, hosted=None):
    L = val.shape[0]
    nb = FH // CB

    def body(val_ref, gate_ref, w_ref, b_ref, o_ref, g_ref):
        wv = [w_ref[pl.ds(t, 1), :] for t in range(9)]
        bv = b_ref[...]
        _grid_copies(g_ref, gate_ref[...], L)

        def tile(i, carry):
            r = i * RT
            acc = jnp.broadcast_to(bv, (RT, CB))
            for dr in range(3):
                for dc in range(3):
                    acc = acc + g_ref[dc, _rows(r, GPAD + (dr - 1) * GRID_W), :] * wv[3 * dr + dc]
            o_ref[_rows(r), :] = (acc * _sigmoid(acc) * val_ref[_rows(r), :]).astype(bf16)
            return carry

        lax.fori_loop(0, L // RT, tile, 0, unroll=2)

    cblk = pl.BlockSpec((L, CB), lambda j: (0, j))
    (out,), extra = _host_call(
        body, (nb,), [cblk, cblk, pl.BlockSpec((9, CB), lambda j: (0, j)), pl.BlockSpec((1, CB), lambda j: (0, j))],
        [cblk], [S((L, FH), bf16)], [pltpu.VMEM((3, L + 2 * GPAD, CB), f32)], ("parallel",), name,
        (val, gate, cw, cb_), hosted)
    return out, extra


def ffn_gate_bwd(val, gate, cw, cb_, dact, name):
    L = val.shape[0]
    nb = FH // CB

    def body(val_ref, gate_ref, w_ref, b_ref, da_ref, dval_ref, dgate_ref, dw_ref, db_ref, g_ref, d_ref):
        wv = [w_ref[pl.ds(t, 1), :] for t in range(9)]
        bv = b_ref[...]
        _grid_copies(g_ref, gate_ref[...], L)

        def tile1(i, carry):
            r = i * RTB

            def tap(t):
                return g_ref[t % 3, _rowsb(r, GPAD + (t // 3 - 1) * GRID_W), :]

            pre = jnp.broadcast_to(bv, (RTB, CB))
            for t in range(9):
                pre = pre + tap(t) * wv[t]
            sg = _sigmoid(pre)
            da = da_ref[_rowsb(r), :].astype(f32)
            dval_ref[_rowsb(r), :] = (da * pre * sg).astype(bf16)
            dpre = da * val_ref[_rowsb(r), :] * (sg * (1.0 + pre * (1.0 - sg)))
            d_ref[_rowsb(r), :] = dpre
            new = [carry[t] + _fold8(dpre * tap(t)) for t in range(9)]
            new.append(carry[9] + _fold8(dpre))
            return tuple(new)

        acc8 = lax.fori_loop(0, L // RTB, tile1, tuple(jnp.zeros((8, CB), f32) for _ in range(10)), unroll=2)
        for t in range(9):
            dw_ref[pl.ds(t, 1), :] = jnp.sum(acc8[t], axis=0, keepdims=True)
        db_ref[...] = jnp.sum(acc8[9], axis=0, keepdims=True)
        _grid_copies(g_ref, d_ref[...], L)

        def tile2(i, carry):
            r = i * RTB
            dg = jnp.zeros((RTB, CB), f32)
            for dr in range(3):
                for dc in range(3):
                    dg = dg + g_ref[2 - dc, _rowsb(r, GPAD - (dr - 1) * GRID_W), :] * wv[3 * dr + dc]
            dgate_ref[_rowsb(r), :] = dg.astype(bf16)
            return carry

        lax.fori_loop(0, L // RTB, tile2, 0, unroll=4)

    cblk = pl.BlockSpec((L, CB), lambda j: (0, j))
    return pl.pallas_call(
        body, grid=(nb,),
        in_specs=[cblk, cblk, pl.BlockSpec((9, CB), lambda j: (0, j)), pl.BlockSpec((1, CB), lambda j: (0, j)), cblk],
        out_specs=[cblk, cblk, pl.BlockSpec((9, CB), lambda j: (0, j)), pl.BlockSpec((1, CB), lambda j: (0, j))],
        out_shape=[S((L, FH), bf16), S((L, FH), bf16), S((9, FH), f32), S((1, FH), f32)],
        scratch_shapes=[pltpu.VMEM((3, L + 2 * GPAD, CB), f32), pltpu.VMEM((L, CB), f32)],
        compiler_params=_cparams("parallel"), name=name)(val, gate, cw, cb_, dact)


CONF_K = 31
CHALF = CONF_K // 2
CPAD = 16


def _shift_copies8(c_ref, base_ref, L):
    n = L + 2 * CPAD - 8
    for b_ in range(8):
        c_ref[b_, pl.ds(0, n), :] = base_ref[pl.ds(b_, n), :]


def _tap_ab(o):
    return o % 8, o - o % 8


def conf_glu_conv_fwd(pa, pg, b1, wdw, bdw, name, hosted=None):
    L = pa.shape[0]
    nb = D // CB

    def body(pa_ref, pg_ref, ba_ref, bg_ref, w_ref, bdw_ref, o_ref, base_ref, c_ref):
        _zero_rows(base_ref, 0, CPAD)
        _zero_rows(base_ref, CPAD + L, CPAD)
        base_ref[pl.ds(CPAD, L), :] = (pa_ref[...] + ba_ref[...]) * _sigmoid(pg_ref[...] + bg_ref[...])
        _shift_copies8(c_ref, base_ref, L)
        bv = bdw_ref[...]

        def tile(i, carry):
            r = i * RT
            acc = jnp.broadcast_to(bv, (RT, CB))
            for k in range(CONF_K):
                b_, a8 = _tap_ab(k - CHALF)
                acc = acc + c_ref[b_, _rows(r, CPAD + a8), :] * w_ref[pl.ds(k, 1), :]
            o_ref[_rows(r), :] = acc
            return carry

        lax.fori_loop(0, L // RT, tile, 0, unroll=2)

    cblk = pl.BlockSpec((L, CB), lambda j: (0, j))
    rblk = pl.BlockSpec((1, CB), lambda j: (0, j))
    rgblk = pl.BlockSpec((1, CB), lambda j: (0, nb + j))
    (out,), extra = _host_call(
        body, (nb,), [cblk, cblk, rblk, rgblk, pl.BlockSpec((CONF_K, CB), lambda j: (0, j)), rblk],
        [cblk], [S((L, D), f32)], [pltpu.VMEM((L + 2 * CPAD, CB), f32), pltpu.VMEM((8, L + 2 * CPAD, CB), f32)],
        ("parallel",), name, (pa, pg, b1, b1, wdw, bdw), hosted)
    return out, extra


def conf_glu_conv_bwd(pa, pg, b1, wdw, dy, name):
    L = pa.shape[0]
    nb = D // CB

    def body(pa_ref, pg_ref, ba_ref, bg_ref, w_ref, dy_ref, dpa_ref, dpg_ref, dba_ref, dbg_ref, dw_ref, dbdw_ref,
             base_ref, c_ref, acc_ref):
        _zero_rows(base_ref, 0, CPAD)
        _zero_rows(base_ref, CPAD + L, CPAD)
        base_ref[pl.ds(CPAD, L), :] = (pa_ref[...] + ba_ref[...]) * _sigmoid(pg_ref[...] + bg_ref[...])
        _shift_copies8(c_ref, base_ref, L)
        acc_ref[...] = jnp.zeros_like(acc_ref)

        def tile1(i, carry):
            r = i * RTB
            dyt = dy_ref[_rowsb(r), :]
            for k in range(CONF_K):
                b_, a8 = _tap_ab(k - CHALF)
                acc_ref[k] += _fold8(dyt * c_ref[b_, _rowsb(r, CPAD + a8), :])
            return carry + _fold8(dyt)

        db8 = lax.fori_loop(0, L // RTB, tile1, jnp.zeros((8, CB), f32), unroll=2)
        dbdw_ref[...] = jnp.sum(db8, axis=0, keepdims=True)
        for k in range(CONF_K):
            dw_ref[pl.ds(k, 1), :] = jnp.sum(acc_ref[k], axis=0, keepdims=True)
        base_ref[pl.ds(CPAD, L), :] = dy_ref[...]
        _shift_copies8(c_ref, base_ref, L)
        ba = ba_ref[...]
        bg = bg_ref[...]

        def tile2(i, carry):
            r = i * RTB
            dglu = jnp.zeros((RTB, CB), f32)
            for k in range(CONF_K):
                b_, a8 = _tap_ab(CHALF - k)
                dglu = dglu + c_ref[b_, _rowsb(r, CPAD + a8), :] * w_ref[pl.ds(k, 1), :]
            a = pa_ref[_rowsb(r), :] + ba
            sg = _sigmoid(pg_ref[_rowsb(r), :] + bg)
            dpa = dglu * sg
            dpg = dglu * a * (sg * (1.0 - sg))
            dpa_ref[_rowsb(r), :] = dpa.astype(bf16)
            dpg_ref[_rowsb(r), :] = dpg.astype(bf16)
            return carry[0] + _fold8(dpa), carry[1] + _fold8(dpg)

        s8 = lax.fori_loop(0, L // RTB, tile2, (jnp.zeros((8, CB), f32), jnp.zeros((8, CB), f32)), unroll=2)
        dba_ref[...] = jnp.sum(s8[0], axis=0, keepdims=True)
        dbg_ref[...] = jnp.sum(s8[1], axis=0, keepdims=True)

    cblk = pl.BlockSpec((L, CB), lambda j: (0, j))
    rblk = pl.BlockSpec((1, CB), lambda j: (0, j))
    rgblk = pl.BlockSpec((1, CB), lambda j: (0, nb + j))
    wblk = pl.BlockSpec((CONF_K, CB), lambda j: (0, j))
    return pl.pallas_call(
        body, grid=(nb,), in_specs=[cblk, cblk, rblk, rgblk, wblk, cblk],
        out_specs=[cblk, cblk, rblk, rblk, wblk, rblk],
        out_shape=[S((L, D), bf16), S((L, D), bf16), S((1, D), f32), S((1, D), f32), S((CONF_K, D), f32), S((1, D), f32)],
        scratch_shapes=[pltpu.VMEM((L + 2 * CPAD, CB), f32), pltpu.VMEM((8, L + 2 * CPAD, CB), f32),
                        pltpu.VMEM((CONF_K, 8, CB), f32)],
        compiler_params=_cparams("parallel"), name=name)(pa, pg, b1, b1, wdw, dy)


def _ln_silu_f(x, w, b):
    mu = jnp.mean(x, axis=-1, keepdims=True)
    d = x - mu
    y = d * lax.rsqrt(jnp.mean(d * d, axis=-1, keepdims=True) + EPS) * w + b
    return y * _sigmoid(y)


def ln_silu_fwd(x, w, b, name):
    T = x.shape[0]

    def body(x_ref, w_ref, b_ref, o_ref):
        o_ref[...] = _ln_silu_f(x_ref[...], w_ref[...], b_ref[...]).astype(bf16)

    blk = pl.BlockSpec((TB, D), lambda i: (i, 0))
    row = pl.BlockSpec((1, D), lambda i: (0, 0))
    return pl.pallas_call(body, grid=(T // TB,), in_specs=[blk, row, row], out_specs=blk, out_shape=S((T, D), bf16),
                          compiler_params=_cparams("parallel"), name=name)(x, w, b)


def ln_silu_bwd(x, w, b, ds, name):
    T = x.shape[0]

    def body(x_ref, w_ref, b_ref, ds_ref, dx_ref, dw_ref, db_ref):
        i = pl.program_id(0)
        _, vjp = jax.vjp(_ln_silu_f, x_ref[...], w_ref[...], b_ref[...])
        dx, dw, db = vjp(ds_ref[...].astype(f32))
        dx_ref[...] = dx

        @pl.when(i == 0)
        def _():
            dw_ref[...] = jnp.zeros_like(dw_ref)
            db_ref[...] = jnp.zeros_like(db_ref)

        dw_ref[...] += dw
        db_ref[...] += db

    blk = pl.BlockSpec((TB, D), lambda i: (i, 0))
    row = pl.BlockSpec((1, D), lambda i: (0, 0))
    return pl.pallas_call(body, grid=(T // TB,), in_specs=[blk, row, row, blk], out_specs=[blk, row, row],
                          out_shape=[S((T, D), f32), S((1, D), f32), S((1, D), f32)],
                          compiler_params=_cparams("arbitrary"), name=name)(x, w, b, ds)


def _mxu(a, b, dims):
    return lax.dot_general(a.astype(bf16), b.astype(bf16), (dims, ((), ())), preferred_element_type=f32)


def _nn(a, b):
    return _mxu(a, b, ((1,), (0,)))


def _nt(a, b):
    return _mxu(a, b, ((1,), (1,)))


def _tn(a, b):
    return _mxu(a, b, ((0,), (0,)))


@jax.custom_vjp
def _dot_nn(a, b):
    return _nn(a, b)


@jax.custom_vjp
def _dot_nt(a, b):
    return _nt(a, b)


@jax.custom_vjp
def _dot_tn(a, b):
    return _tn(a, b)


_dot_nn.defvjp(lambda a, b: (_nn(a, b), (a, b)), lambda res, g: (_nt(g, res[1]), _tn(res[0], g)))
_dot_nt.defvjp(lambda a, b: (_nt(a, b), (a, b)), lambda res, g: (_nn(g, res[1]), _tn(g, res[0])))
_dot_tn.defvjp(lambda a, b: (_tn(a, b), (a, b)), lambda res, g: (_nt(res[1], g), _nn(res[0], g)))


def _exact_dot(a, b, dims, split_first):
    v = a if split_first else b
    p1 = v.astype(bf16)
    r1 = v - p1.astype(f32)
    p2 = r1.astype(bf16)
    p3 = (r1 - p2.astype(f32)).astype(bf16)
    out = None
    for p in (p1, p2, p3):
        lhs, rhs = (p, b.astype(bf16)) if split_first else (a.astype(bf16), p)
        t = lax.dot_general(lhs, rhs, (dims, ((), ())), preferred_element_type=f32)
        out = t if out is None else out + t
    return out


@jax.custom_vjp
def _masked_sum_cols(mf, a):
    return _exact_dot(mf, a, ((1,), (0,)), False)


@jax.custom_vjp
def _masked_sum_rows(mf, a):
    return _exact_dot(a, mf, ((1,), (1,)), True)


_masked_sum_cols.defvjp(lambda mf, a: (_exact_dot(mf, a, ((1,), (0,)), False), mf),
                        lambda mf, g: (jnp.zeros_like(mf), _exact_dot(mf, g, ((0,), (0,)), False)))
_masked_sum_rows.defvjp(lambda mf, a: (_exact_dot(a, mf, ((1,), (1,)), True), mf),
                        lambda mf, g: (jnp.zeros_like(mf), _exact_dot(g, mf, ((1,), (0,)), True)))


def _masked_sum(mf, a, rows):
    return _masked_sum_rows(mf, a) if rows else _masked_sum_cols(mf, a)


def _lanes_to_rows(v):
    r = lax.broadcasted_iota(jnp.int32, (GW, GW), 0)
    c = lax.broadcasted_iota(jnp.int32, (GW, GW), 1)
    return jnp.sum(jnp.where(r == c, jnp.broadcast_to(v, (GW, GW)), 0.0), axis=1, keepdims=True)


def _ssd_chunk(x, B, C, dtc, dtr, bc, br, alc, alr, s_in, is_fwd):
    row = lax.broadcasted_iota(jnp.int32, (Q, Q), 0)
    col = lax.broadcasted_iota(jnp.int32, (Q, Q), 1)
    sgn = jnp.where(is_fwd, 1, -1).astype(jnp.int32)
    mask = (row - col) * sgn >= 0
    mf = mask.astype(f32)
    lane_head = lax.broadcasted_iota(jnp.int32, (1, GW), 1) // P

    def spread(v):
        out = jnp.zeros((v.shape[0], GW), f32)
        for r in range(HPG):
            out = jnp.where(lane_head == r, v[:, r:r + 1], out)
        return out

    dt_c = _softplus(dtc + bc)
    dt_r = _softplus(dtr + br)
    a_c = dt_c * (-jnp.exp(alc))
    a_r = dt_r * (-jnp.exp(alr))
    acum_c = _masked_sum(mf, a_c, False)
    acum_r = _masked_sum(mf, a_r, True)
    tot_c = jnp.sum(a_c, axis=0, keepdims=True)
    dt_e = spread(dt_c)
    acum_e = spread(acum_c)
    tot_e = spread(tot_c)
    xdt = x * dt_e
    cb = _dot_nt(C, B)
    scores, xs = [], []
    for r in range(HPG):
        seg = acum_c[:, r:r + 1] - acum_r[r:r + 1, :]
        scores.append(cb * jnp.exp(jnp.where(mask, seg, -jnp.inf)))
        xs.append(jnp.where(lane_head == r, xdt, 0.0))
    y = _dot_nn(jnp.concatenate(scores, axis=1), jnp.concatenate(xs, axis=0))
    y = y + _dot_nt(C, s_in) * jnp.exp(acum_e)
    xe = xdt * jnp.exp(tot_e - acum_e)
    s_out = _lanes_to_rows(jnp.exp(tot_e)) * s_in + _dot_tn(xe, B)
    return y, s_out


def _chunk_index(d, t, nctx, nc):
    bwd = jnp.where(t < nctx, nctx - 1 - t, nc - 1 - (t - nctx))
    return jnp.where(d == 0, t, bwd)


def _ssd_in_specs(ci):
    small_c = pl.BlockSpec((None, G, 1, HPG), lambda d, t: (d, 0, 0, 0))
    small_r = pl.BlockSpec((None, G, HPG, 1), lambda d, t: (d, 0, 0, 0))
    return [
        pl.BlockSpec((Q, CONVD), lambda d, t: (ci(d, t), 0)),
        pl.BlockSpec((None, G, Q, HPG), lambda d, t: (d, 0, ci(d, t), 0)),
        pl.BlockSpec((None, G, HPG, Q), lambda d, t: (d, 0, 0, ci(d, t))),
        small_c, small_r, small_c, small_r,
    ]


def _group_cols(g):
    return pl.ds(g * GW, GW), pl.ds(DI + g * N, N), pl.ds(DI + G * N + g * N, N)


def ssd_scan_fwd(xbc, dtc, dtr, bc, br, alc, alr, nctx, name, hosted=None):
    T = xbc.shape[0]
    nc = T // Q

    def body(xbc_ref, dtc_ref, dtr_ref, bc_ref, br_ref, alc_ref, alr_ref, y_ref, sin_ref, st_ref):
        d = pl.program_id(0)
        t = pl.program_id(1)

        @pl.when(t == 0)
        def _():
            st_ref[...] = jnp.zeros_like(st_ref)

        for g in range(G):
            xs, bs, cs = _group_cols(g)
            s_in = st_ref[g]
            sin_ref[g] = s_in
            y, s_out = _ssd_chunk(xbc_ref[:, xs], xbc_ref[:, bs], xbc_ref[:, cs], dtc_ref[g], dtr_ref[g], bc_ref[g], br_ref[g],
                                  alc_ref[g], alr_ref[g], s_in, d == 0)
            y_ref[:, xs] = y
            st_ref[g] = s_out

    ci = lambda d, t: _chunk_index(d, t, nctx, nc)
    out_specs = [
        pl.BlockSpec((None, Q, DI), lambda d, t: (d, ci(d, t), 0)),
        pl.BlockSpec((None, None, G, GW, N), lambda d, t: (d, ci(d, t), 0, 0, 0)),
    ]
    return _host_call(
        body, (2, nc), _ssd_in_specs(ci), out_specs, [S((2, T, DI), f32), S((2, nc, G, GW, N), f32)],
        [pltpu.VMEM((G, GW, N), f32)], ("arbitrary", "arbitrary"), name, (xbc, dtc, dtr, bc, br, alc, alr), hosted)


def ssd_scan_bwd(xbc, dtc, dtr, bc, br, alc, alr, s_in_all, dy, nctx, name, hosted=None):
    T = xbc.shape[0]
    nc = T // Q

    def body(xbc_ref, dtc_ref, dtr_ref, bc_ref, br_ref, alc_ref, alr_ref, sin_ref, dy_ref,
             dxbc_ref, ddtc_ref, ddtr_ref, dbc_ref, dbr_ref, dalc_ref, dalr_ref, ds_ref):
        d = pl.program_id(0)
        t = pl.program_id(1)

        @pl.when(t == 0)
        def _():
            ds_ref[...] = jnp.zeros_like(ds_ref)
            dbc_ref[...] = jnp.zeros_like(dbc_ref)
            dbr_ref[...] = jnp.zeros_like(dbr_ref)
            dalc_ref[...] = jnp.zeros_like(dalc_ref)
            dalr_ref[...] = jnp.zeros_like(dalr_ref)

        f = functools.partial(_ssd_chunk, is_fwd=(d == 0))
        for g in range(G):
            xs, bs, cs = _group_cols(g)
            _, vjp = jax.vjp(f, xbc_ref[:, xs], xbc_ref[:, bs], xbc_ref[:, cs], dtc_ref[g], dtr_ref[g], bc_ref[g], br_ref[g],
                             alc_ref[g], alr_ref[g], sin_ref[g])
            dx, dB, dC, ddtc, ddtr, dbc, dbr, dalc, dalr, ds = vjp((dy_ref[:, xs], ds_ref[g]))
            dxbc_ref[:, xs] = dx
            dxbc_ref[:, bs] = dB
            dxbc_ref[:, cs] = dC
            ddtc_ref[g] = ddtc
            ddtr_ref[g] = ddtr
            dbc_ref[g] += dbc
            dbr_ref[g] += dbr
            dalc_ref[g] += dalc
            dalr_ref[g] += dalr
            ds_ref[g] = ds

    ci = lambda d, t: _chunk_index(d, nc - 1 - t, nctx, nc)
    in_specs = _ssd_in_specs(ci) + [
        pl.BlockSpec((None, None, G, GW, N), lambda d, t: (d, ci(d, t), 0, 0, 0)),
        pl.BlockSpec((Q, DI), lambda d, t: (ci(d, t), 0)),
    ]
    small_c = pl.BlockSpec((None, G, 1, HPG), lambda d, t: (d, 0, 0, 0))
    small_r = pl.BlockSpec((None, G, HPG, 1), lambda d, t: (d, 0, 0, 0))
    out_specs = [
        pl.BlockSpec((None, Q, CONVD), lambda d, t: (d, ci(d, t), 0)),
        pl.BlockSpec((None, G, Q, HPG), lambda d, t: (d, 0, ci(d, t), 0)),
        pl.BlockSpec((None, G, HPG, Q), lambda d, t: (d, 0, 0, ci(d, t))),
        small_c, small_r, small_c, small_r,
    ]
    out_shape = [S((2, T, CONVD), f32), S((2, G, T, HPG), f32), S((2, G, HPG, T), f32),
                 S((2, G, 1, HPG), f32), S((2, G, HPG, 1), f32), S((2, G, 1, HPG), f32), S((2, G, HPG, 1), f32)]
    return _host_call(body, (2, nc), in_specs, out_specs, out_shape, [pltpu.VMEM((G, GW, N), f32)],
                      ("arbitrary", "arbitrary"), name, (xbc, dtc, dtr, bc, br, alc, alr, s_in_all, dy), hosted)


GTB = 128


def _gate_norm_f(yf, yb, x, z, dexp, w):
    y = (yf + yb + dexp * x) * (z * _sigmoid(z))
    return y * lax.rsqrt(jnp.mean(y * y, axis=-1, keepdims=True) + EPS) * w


def ssd_gate_fwd(y2, xbc, proj, dexp, w, nctxb, name):
    T = xbc.shape[0]
    L = T - nctxb * GTB

    def body(yf_ref, yb_ref, x_ref, z_ref, d_ref, w_ref, o_ref):
        o_ref[...] = _gate_norm_f(yf_ref[...], yb_ref[...], x_ref[...], z_ref[...], d_ref[...], w_ref[...]).astype(bf16)

    wide = pl.BlockSpec((GTB, DI), lambda i: (i + nctxb, 0))
    row = pl.BlockSpec((1, DI), lambda i: (0, 0))
    return pl.pallas_call(
        body, grid=(L // GTB,),
        in_specs=[pl.BlockSpec((None, GTB, DI), lambda i: (0, i + nctxb, 0)),
                  pl.BlockSpec((None, GTB, DI), lambda i: (1, i + nctxb, 0)), wide, wide, row, row],
        out_specs=pl.BlockSpec((GTB, DI), lambda i: (i, 0)), out_shape=S((L, DI), bf16),
        compiler_params=_cparams("parallel"), name=name)(y2, y2, xbc, proj, dexp, w)


def ssd_gate_bwd(y2, xbc, proj, dexp, w, dyn, nctxb, name, hosted=None):
    T = xbc.shape[0]
    nb = T // GTB

    def body(yf_ref, yb_ref, x_ref, z_ref, d_ref, w_ref, dyn_ref, dy_ref, dx_ref, dz_ref, dd_ref, dw_ref):
        i = pl.program_id(0)

        @pl.when(i == 0)
        def _():
            dd_ref[...] = jnp.zeros_like(dd_ref)
            dw_ref[...] = jnp.zeros_like(dw_ref)

        @pl.when(i < nctxb)
        def _():
            dy_ref[...] = jnp.zeros_like(dy_ref)
            dx_ref[...] = jnp.zeros_like(dx_ref)
            dz_ref[...] = jnp.zeros_like(dz_ref)

        @pl.when(i >= nctxb)
        def _():
            _, vjp = jax.vjp(_gate_norm_f, yf_ref[...], yb_ref[...], x_ref[...], z_ref[...], d_ref[...], w_ref[...])
            dyf, _, dx, dz, dd, dw = vjp(dyn_ref[...].astype(f32))
            dy_ref[...] = dyf
            dx_ref[...] = dx
            dz_ref[...] = dz.astype(bf16)
            fold = (lax.broadcasted_iota(jnp.int32, (DI, 128), 0) // P == lax.broadcasted_iota(jnp.int32, (DI, 128), 1))
            dd_ref[...] += jnp.dot(dd, fold.astype(f32), precision=HI, preferred_element_type=f32)
            dw_ref[...] += dw

    wide = pl.BlockSpec((GTB, DI), lambda i: (i, 0))
    row = pl.BlockSpec((1, DI), lambda i: (0, 0))
    hrow = pl.BlockSpec((1, 128), lambda i: (0, 0))
    return _host_call(
        body, (nb,),
        [pl.BlockSpec((None, GTB, DI), lambda i: (0, i, 0)), pl.BlockSpec((None, GTB, DI), lambda i: (1, i, 0)),
         wide, wide, row, row, pl.BlockSpec((GTB, DI), lambda i: (jnp.maximum(i - nctxb, 0), 0))],
        [wide, wide, wide, hrow, row],
        [S((T, DI), f32), S((T, DI), f32), S((T, DI), bf16), S((1, 128), f32), S((1, DI), f32)],
        [], ("arbitrary",), name, (y2, y2, xbc, proj, dexp, w, dyn), hosted)


CROWS = 2 * N_DEV


def mod_fwd(c16, modw, name):
    nl, _, cols = modw.shape

    def body(c_ref, w_ref, o_ref):
        cv = c_ref[...]
        s = cv * _sigmoid(cv)
        for l in range(nl):
            o_ref[l] = jnp.dot(s, w_ref[l], precision=HI, preferred_element_type=f32)

    return pl.pallas_call(body, in_specs=[VMEM, VMEM], out_specs=VMEM, out_shape=S((nl, CROWS, cols), f32),
                          compiler_params=pltpu.CompilerParams(vmem_limit_bytes=VMEM_LIMIT_BYTES), name=name)(c16, modw)


def mod_bwd(c16, modw, dm_sh, dm_all, name):
    nl, _, cols = modw.shape

    def body(c_ref, w_ref, dm_ref, dmall_ref, dw_ref, dc_ref, db_ref):
        cv = c_ref[...]
        sg = _sigmoid(cv)
        s = cv * sg
        ds_dc = sg * (1.0 + cv * (1.0 - sg))
        is_ctx = lax.broadcasted_iota(jnp.int32, (CROWS, D), 0) >= N_DEV
        dc = jnp.zeros((1, D), f32)
        for l in range(nl):
            dm = dm_ref[l]
            dw_ref[l] = lax.dot_general(s, dm, (((0,), (0,)), ((), ())), precision=HI, preferred_element_type=f32)
            dsv = lax.dot_general(dm, w_ref[l], (((1,), (1,)), ((), ())), precision=HI, preferred_element_type=f32)
            dc = dc + jnp.sum(jnp.where(is_ctx, dsv * ds_dc, 0.0), axis=0, keepdims=True)
            db_ref[pl.ds(l, 1), :] = jnp.sum(dmall_ref[l], axis=0, keepdims=True)
        dc_ref[...] = dc

    return pl.pallas_call(
        body, in_specs=[VMEM, VMEM, VMEM, VMEM], out_specs=[VMEM, VMEM, VMEM],
        out_shape=[S(modw.shape, f32), S((1, D), f32), S((nl, 6 * D), f32)],
        compiler_params=pltpu.CompilerParams(vmem_limit_bytes=VMEM_LIMIT_BYTES), name=name)(c16, modw, dm_sh, dm_all)


def adamw(w, g, m, v, name):
    R, C = w.shape
    rb = R if R <= 512 else max(r_ for r_ in range(8, 513, 8) if R % r_ == 0)
    bc1 = 1.0 - ADAM_B1 ** ADAM_STEP
    bc2 = 1.0 - ADAM_B2 ** ADAM_STEP

    def body(w_ref, g_ref, m_ref, v_ref, d_ref, nm_ref, nv_ref):
        gv = g_ref[...]
        m_new = ADAM_B1 * m_ref[...] + (1.0 - ADAM_B1) * gv
        v_new = ADAM_B2 * v_ref[...] + (1.0 - ADAM_B2) * (gv * gv)
        m_hat = m_new / bc1
        v_hat = v_new / bc2
        d_ref[...] = -ADAM_LR * (m_hat / (jnp.sqrt(v_hat) + ADAM_EPS) + ADAM_WD * w_ref[...])
        nm_ref[...] = m_new
        nv_ref[...] = v_new

    blk = pl.BlockSpec((rb, C), lambda i: (i, 0))
    return pl.pallas_call(body, grid=(R // rb,), in_specs=[blk] * 4, out_specs=[blk] * 3,
                          out_shape=[S((R, C), f32)] * 3, compiler_params=_cparams("parallel"), name=name)(w, g, m, v)


def _me():
    return lax.axis_index("x"), lax.axis_index("y"), lax.axis_index("c")


def allgather_small(x, name, with_sum=False):
    r, w = x.shape

    def body(x_ref, *refs):
        if with_sum:
            out_ref, sum_ref, send_sems, recv_sems = refs
        else:
            out_ref, send_sems, recv_sems = refs
        mx, my, mc = _me()
        me = 4 * mx + 2 * my + mc
        out_ref[me] = x_ref[...]
        peers = []
        for k in range(1, N_DEV):
            kx, ky, kc = (k >> 2) & 1, (k >> 1) & 1, k & 1
            peers.append((mx + kx - 2 * mx * kx, my + ky - 2 * my * ky, mc + kc - 2 * mc * kc))
        copies = []
        for k, peer in enumerate(peers):
            cp = pltpu.make_async_remote_copy(src_ref=x_ref, dst_ref=out_ref.at[me], send_sem=send_sems.at[k],
                                              recv_sem=recv_sems.at[k], device_id=peer, device_id_type=MESH)
            cp.start()
            copies.append(cp)
        for k, (px, py, pc) in enumerate(peers):
            pltpu.make_async_remote_copy(src_ref=x_ref, dst_ref=out_ref.at[4 * px + 2 * py + pc], send_sem=send_sems.at[k],
                                         recv_sem=recv_sems.at[k], device_id=(px, py, pc), device_id_type=MESH).wait_recv()
        for cp in copies:
            cp.wait_send()
        if with_sum:
            acc = out_ref[0]
            for j in range(1, N_DEV):
                acc = acc + out_ref[j]
            sum_ref[...] = acc

    out_shape = [S((N_DEV, r, w), f32)] + ([S((r, w), f32)] if with_sum else [])
    outs = pl.pallas_call(
        body, in_specs=[VMEM], out_specs=[VMEM] * len(out_shape), out_shape=out_shape,
        scratch_shapes=[pltpu.SemaphoreType.DMA((N_DEV - 1,)), pltpu.SemaphoreType.DMA((N_DEV - 1,))],
        compiler_params=pltpu.CompilerParams(vmem_limit_bytes=VMEM_LIMIT_BYTES), name=name)(x)
    return outs if with_sum else outs[0]


def add_own(g, r, core, name):
    _, _, R, W = g.shape
    rb = R if R <= 512 else max(r_ for r_ in range(16, 513, 16) if R % r_ == 0)

    def body(core_ref, a_ref, b_ref, o_ref):
        o_ref[...] = (a_ref[...].astype(f32) + b_ref[...].astype(f32)).astype(bf16)

    blk = pl.BlockSpec((None, rb, W), lambda k, i, core_ref: (k, i, 0))
    gs = pltpu.PrefetchScalarGridSpec(
        num_scalar_prefetch=1, grid=(4, R // rb),
        in_specs=[pl.BlockSpec((None, None, rb, W), lambda k, i, core_ref: (k, core_ref[0], i, 0)), blk], out_specs=blk)
    return pl.pallas_call(body, grid_spec=gs, out_shape=S((4, R, W), bf16),
                          compiler_params=_cparams("parallel", "parallel"), name=name)(core, g, r)


def sum_adamw(recv, w, m, v, layer, name):
    _, R, W = recv.shape
    rb = R if R <= 256 else max(r_ for r_ in range(16, 257, 16) if R % r_ == 0)
    bc1 = 1.0 - ADAM_B1 ** ADAM_STEP
    bc2 = 1.0 - ADAM_B2 ** ADAM_STEP

    def body(r_ref, w_ref, m_ref, v_ref, g_ref, d_ref, nm_ref, nv_ref):
        gv = r_ref[0].astype(f32)
        for k in range(1, 4):
            gv = gv + r_ref[k].astype(f32)
        m_new = ADAM_B1 * m_ref[...] + (1.0 - ADAM_B1) * gv
        v_new = ADAM_B2 * v_ref[...] + (1.0 - ADAM_B2) * (gv * gv)
        g_ref[...] = gv
        d_ref[...] = -ADAM_LR * ((m_new / bc1) / (jnp.sqrt(v_new / bc2) + ADAM_EPS) + ADAM_WD * w_ref[...])
        nm_ref[...] = m_new
        nv_ref[...] = v_new

    blk = pl.BlockSpec((rb, W), lambda i: (i, 0))
    wblk = blk if layer is None else pl.BlockSpec((None, rb, W), lambda i: (layer, i, 0))
    return pl.pallas_call(body, grid=(R // rb,), in_specs=[pl.BlockSpec((4, rb, W), lambda i: (0, i, 0)), wblk, wblk, wblk],
                          out_specs=[blk] * 4, out_shape=[S((R, W), f32)] * 4,
                          compiler_params=_cparams("parallel"), name=name)(recv, w, m, v)


def sum_rows(a, name):
    K, R, W = a.shape
    rb = _pick(R, (512, 256, 128, 64, 32, 16))

    def body(a_ref, o_ref):
        acc = a_ref[0].astype(f32)
        for k in range(1, K):
            acc = acc + a_ref[k].astype(f32)
        o_ref[...] = acc

    return pl.pallas_call(body, grid=(R // rb,), in_specs=[pl.BlockSpec((K, rb, W), lambda i: (0, i, 0))],
                          out_specs=pl.BlockSpec((rb, W), lambda i: (i, 0)), out_shape=S((R, W), f32),
                          compiler_params=_cparams("parallel"), name=name)(a)


DMA = pltpu.SemaphoreType.DMA


class GatherExchange:
    def __init__(self, arrays):
        self.arrays = list(arrays)
        self.na = len(self.arrays)
        self.out_shape = [S((N_DEV,) + a.shape, a.dtype) for a in self.arrays]
        self.scratch = [DMA((7 * self.na,)), DMA((7 * self.na,)), DMA((self.na,))]

    def ops(self, x_refs, out_refs, sems):
        send_sems, recv_sems, local_sems = sems
        na = self.na
        x, y, c = _me()
        me, sibling = (x, y, c), (x, y, 1 - c)
        chips = [(1 - x, y), (x, 1 - y), (1 - x, 1 - y)]

        def rows(a, px, py, pc):
            return out_refs[a].at[4 * px + 2 * py + pc]

        def copy(a, k, block, to, src=None):
            return pltpu.make_async_remote_copy(
                src_ref=rows(a, *block) if src is None else src, dst_ref=rows(a, *block),
                send_sem=send_sems.at[7 * a + k], recv_sem=recv_sems.at[7 * a + k], device_id=to, device_id_type=MESH)

        def local(a):
            return pltpu.make_async_copy(x_refs[a], rows(a, *me), local_sems.at[a])

        def first(a):
            return [copy(a, 0, me, sibling, src=x_refs[a])] + [copy(a, 1 + j, me, (*chip, c), src=x_refs[a])
                                                                for j, chip in enumerate(chips)]

        def start():
            for a in range(na):
                local(a).start()
                for cp in first(a):
                    cp.start()

        def mid():
            for a in range(na):
                for j, chip in enumerate(chips):
                    copy(a, 1 + j, (*chip, c), me).wait_recv()
                    copy(a, 4 + j, (*chip, c), sibling).start()

        def finish():
            for a in range(na):
                copy(a, 0, sibling, me).wait_recv()
                for j, chip in enumerate(chips):
                    copy(a, 4 + j, (*chip, 1 - c), me).wait_recv()
                for cp in first(a) + [copy(a, 4 + j, (*chip, c), sibling) for j, chip in enumerate(chips)]:
                    cp.wait_send()
                local(a).wait()

        return start, mid, finish


class SiblingExchange:
    def __init__(self, arrays):
        self.arrays = list(arrays)
        self.na = len(self.arrays)
        self.out_shape = [S((4,) + g.shape[2:], g.dtype) for g in self.arrays]
        self.scratch = [DMA((self.na,)), DMA((self.na,))]

    def ops(self, g_refs, out_refs, sems):
        send_sems, recv_sems = sems
        x, y, c = _me()

        def copy(a):
            return pltpu.make_async_remote_copy(src_ref=g_refs[a].at[:, 1 - c], dst_ref=out_refs[a],
                                                send_sem=send_sems.at[a], recv_sem=recv_sems.at[a],
                                                device_id=(x, y, 1 - c), device_id_type=MESH)

        def start():
            for a in range(self.na):
                copy(a).start()

        def finish():
            for a in range(self.na):
                copy(a).wait()

        return start, None, finish


class ChipsExchange:
    def __init__(self, arrays):
        self.arrays = list(arrays)
        self.na = len(self.arrays)
        self.out_shape = [S(p.shape, p.dtype) for p in self.arrays]
        self.scratch = [DMA((3 * self.na,)), DMA((3 * self.na,)), DMA((self.na,))]

    def ops(self, p_refs, out_refs, sems):
        send_sems, recv_sems, local_sems = sems
        x, y, c = _me()
        mine = 2 * x + y
        chips = [(1 - x, y), (x, 1 - y), (1 - x, 1 - y)]

        def local(a):
            return pltpu.make_async_copy(p_refs[a].at[mine], out_refs[a].at[mine], local_sems.at[a])

        def send(a, j):
            px, py = chips[j]
            return pltpu.make_async_remote_copy(src_ref=p_refs[a].at[2 * px + py], dst_ref=out_refs[a].at[mine],
                                                send_sem=send_sems.at[3 * a + j], recv_sem=recv_sems.at[3 * a + j],
                                                device_id=(px, py, c), device_id_type=MESH)

        def recv(a, j):
            px, py = chips[j]
            return pltpu.make_async_remote_copy(src_ref=p_refs[a].at[mine], dst_ref=out_refs[a].at[2 * px + py],
                                                send_sem=send_sems.at[3 * a + j], recv_sem=recv_sems.at[3 * a + j],
                                                device_id=(px, py, c), device_id_type=MESH)

        def start():
            for a in range(self.na):
                local(a).start()
                for j in range(3):
                    send(a, j).start()

        def finish():
            for a in range(self.na):
                for j in range(3):
                    recv(a, j).wait_recv()
                for j in range(3):
                    send(a, j).wait_send()
                local(a).wait()

        return start, None, finish


def exchange(ex, name):
    na = ex.na

    def body(*refs):
        start, mid, finish = ex.ops(refs[:na], refs[na:2 * na], refs[2 * na:])
        start()
        if mid is not None:
            mid()
        finish()

    return pl.pallas_call(body, in_specs=[ANY] * na, out_specs=[ANY] * na, out_shape=ex.out_shape,
                          scratch_shapes=ex.scratch, name=name)(*ex.arrays)


def _host_call(body, grid, in_specs, out_specs, out_shape, scratch_shapes, sem, name, args, hosted):
    if hosted is None:
        res = pl.pallas_call(body, grid=grid, in_specs=in_specs, out_specs=out_specs, out_shape=out_shape,
                             scratch_shapes=scratch_shapes, compiler_params=_cparams(*sem), name=name)(*args)
        return res, None
    n_in, n_out, n_sc, na = len(in_specs), len(out_shape), len(scratch_shapes), hosted.na
    nsteps = 1
    for g_ in grid:
        nsteps *= g_
    mid_step = (3 * nsteps) // 4
    i1 = n_in + na
    i2 = i1 + n_out
    i3 = i2 + na
    i4 = i3 + n_sc

    def wrapped(*refs):
        step = pl.program_id(0)
        for ax in range(1, len(grid)):
            step = step * grid[ax] + pl.program_id(ax)
        start, mid, finish = hosted.ops(refs[n_in:i1], refs[i2:i3], refs[i4:])
        pl.when(step == 0)(start)
        if mid is not None:
            pl.when(step == mid_step)(mid)
        body(*refs[:n_in], *refs[i1:i2], *refs[i3:i4])
        pl.when(step == nsteps - 1)(finish)

    res = pl.pallas_call(
        wrapped, grid=grid, in_specs=list(in_specs) + [ANY] * na, out_specs=list(out_specs) + [ANY] * na,
        out_shape=list(out_shape) + hosted.out_shape, scratch_shapes=list(scratch_shapes) + hosted.scratch,
        compiler_params=_cparams(*(("arbitrary",) * len(grid))), name=name)(*args, *hosted.arrays)
    return res[:n_out], res[n_out:]


PACK_ALIGN = 16 * PACK_W


def _pad_to(v, mult):
    n = v.shape[-1]
    extra = (-n) % mult
    if extra == 0:
        return v
    return jnp.concatenate([v, jnp.zeros(v.shape[:-1] + (extra,), v.dtype)], axis=-1)


def _f32_as_bf16_pairs(v):
    return lax.bitcast_convert_type(v.reshape(-1), bf16).reshape(-1)


def _bf16_pairs_as_f32(v):
    return lax.bitcast_convert_type(v.reshape(v.shape[:-1] + (v.shape[-1] // 2, 2)), f32)


def _col_shards(gw):
    lead = gw.shape[:-1]
    n = gw.shape[-1] // N_DEV
    t = gw.reshape(lead + (N_DEV, n))
    t = jnp.moveaxis(t, -2, 0)
    return t.reshape(N_DEV, -1)


def kernel(x, c, ctx, c_ctx, mod_w, mod_b, norm1_w, norm2_w, ssd_w_in, ssd_conv_w, ssd_conv_b, ssd_dt_bias, ssd_a_log, ssd_d, ssd_norm_w, ssd_w_out, conf_w_pw1, conf_b_pw1, conf_w_dw, conf_b_dw, conf_ln_w, conf_ln_b, conf_w_pw2, conf_b_pw2, ffn_w_up, ffn_conv_w, ffn_conv_b, ffn_w_down, final_norm_w, loss_target, m_c_ctx, m_mod_w, m_mod_b, m_norm1_w, m_norm2_w, m_ssd_w_in, m_ssd_conv_w, m_ssd_conv_b, m_ssd_dt_bias, m_ssd_a_log, m_ssd_d, m_ssd_norm_w, m_ssd_w_out, m_conf_w_pw1, m_conf_b_pw1, m_conf_w_dw, m_conf_b_dw, m_conf_ln_w, m_conf_ln_b, m_conf_w_pw2, m_conf_b_pw2, m_ffn_w_up, m_ffn_conv_w, m_ffn_conv_b, m_ffn_w_down, m_final_norm_w, v_c_ctx, v_mod_w, v_mod_b, v_norm1_w, v_norm2_w, v_ssd_w_in, v_ssd_conv_w, v_ssd_conv_b, v_ssd_dt_bias, v_ssd_a_log, v_ssd_d, v_ssd_norm_w, v_ssd_w_out, v_conf_w_pw1, v_conf_b_pw1, v_conf_w_dw, v_conf_b_dw, v_conf_ln_w, v_conf_ln_b, v_conf_w_pw2, v_conf_b_pw2, v_ffn_w_up, v_ffn_conv_w, v_ffn_conv_b, v_ffn_w_down, v_final_norm_w):
    mx, my, mc = _me()
    me = 4 * mx + 2 * my + mc
    L = x.shape[1]
    LC = ctx.shape[1]
    T = LC + L
    w_in_cols = ssd_w_in.shape[2] * N_DEV
    n_dt = w_in_cols - DI - CONVD

    small = [c[0], ssd_conv_w[0], conf_b_pw1[0], conf_w_dw[0], conf_b_dw[0], conf_ln_w[0], conf_ln_b[0], conf_b_pw2[0],
             ffn_conv_w]
    parts = [_f32_as_bf16_pairs(t) for t in small]
    sizes = [p.shape[0] for p in parts]
    small_flat = _pad_to(jnp.concatenate(parts), PACK_ALIGN).reshape(-1, PACK_W)
    w_in, small_g = exchange(GatherExchange([ssd_w_in[0].astype(bf16), small_flat]), "gather_first")
    gather_in_proj = GatherExchange([ssd_w_out[0].astype(bf16), conf_w_pw2[0].astype(bf16)])
    gather_in_conv = GatherExchange([ffn_w_down[0].astype(bf16), conf_w_pw1[0].astype(bf16)])
    gather_in_scan = GatherExchange([ffn_w_up[0].astype(bf16), ffn_w_down[1].astype(bf16)])
    gather_in_gate = GatherExchange([ffn_w_up[1].astype(bf16)])
    w_up, w_down = [None, None], [None, None]
    small_g = small_g.reshape(N_DEV, -1)
    offs = [0]
    for s_ in sizes:
        offs.append(offs[-1] + s_)
    sm = [_bf16_pairs_as_f32(small_g[:, offs[i]:offs[i + 1]]) for i in range(len(sizes))]

    def cols(pc, K):
        return jnp.moveaxis(pc.reshape(N_DEV, K, -1), 0, 1).reshape(K, -1)

    c_all = sm[0]
    conv_w5 = cols(sm[1], 5)
    b_pw1 = sm[2].reshape(1, 2 * D)
    w_dw = cols(sm[3], CONF_K)
    b_dw, ln_w, ln_b, b_pw2 = (sm[i].reshape(1, D) for i in (4, 5, 6, 7))
    fcw = sm[8].reshape(N_DEV, 2, 9, FH // N_DEV)
    ffn_cw = [cols(fcw[:, i].reshape(N_DEV, -1), 9) for i in range(2)]
    in_segs = (DI, CONVD, n_dt)
    up_segs = (FH, FH)
    pw1_segs = (D, D)

    c16 = jnp.concatenate([c_all, jnp.broadcast_to(c_ctx[None, :], (N_DEV, D))], axis=0)
    m_sh = mod_fwd(c16, mod_w, "mod_fwd")
    mod_cols = mod_w.shape[2]
    m_all = allgather_small(m_sh.reshape(2 * CROWS, mod_cols), "gather_mod")
    m_all = jnp.moveaxis(m_all.reshape(N_DEV, 2, CROWS, mod_cols), 0, 2).reshape(2, CROWS, 6 * D) + mod_b[:, None, :]
    m_lat = lax.dynamic_index_in_dim(m_all, me, axis=1, keepdims=False).reshape(2, 6, 1, D)
    m_ctx = m_all[:, N_DEV].reshape(2, 6, 1, D)
    zero_row = jnp.zeros((1, D), f32)

    def ffn_fwd(h, i, tag, hosted=None):
        a2 = modnorm_fwd(h, norm2_w[i][None], m_lat[i, 4][None], m_lat[i, 3][None], 0, f"ffn{tag}_norm")
        val, gate = smm_fwd(a2, w_up[i], None, up_segs, f"ffn{tag}_up")
        act, extra = ffn_gate_fwd(val, gate, ffn_cw[i], ffn_conv_b[i][None], f"ffn{tag}_gate", hosted)
        o2 = matmul(act, w_down[i], "nn", f32, f"ffn{tag}_down")
        h_new = resgate_fwd(h, o2, m_lat[i, 5], zero_row, f"ffn{tag}_res")
        return h_new, (a2, val, gate, act, o2), extra

    def ffn_bwd(dh, h, i, saved, tag):
        a2, val, gate, act, o2 = saved
        do2, dg2, _ = resgate_bwd(dh, o2, m_lat[i, 5], zero_row, f"ffn{tag}_res_bwd")
        g_down = matmul(act, do2, "tn", bf16, f"ffn{tag}_down_dw")
        dact = matmul(do2, w_down[i], "nt", bf16, f"ffn{tag}_down_dx")
        dval, dgate, dcw, dcb = ffn_gate_bwd(val, gate, ffn_cw[i], ffn_conv_b[i][None], dact, f"ffn{tag}_gate_bwd")
        g_up = smm_dw(a2, [dval, dgate], FH // 4, up_segs, 2, f"ffn{tag}_up_dw")
        da2, _ = smm_dx([dval, dgate], w_up[i], None, up_segs, bf16, f"ffn{tag}_up_dx")
        dh_in, dn2, dsc2, dsh2 = modnorm_bwd(h, norm2_w[i][None], m_lat[i, 4][None], m_lat[i, 3][None], da2, dh, 0,
                                             f"ffn{tag}_norm_bwd")
        return dh_in, dict(w_up=g_up, w_down=g_down, conv_w=dcw, conv_b=dcb, norm2=dn2, sh2=dsh2[0], sc2=dsc2[0], g2=dg2)

    nctx = LC // Q
    h0 = jnp.concatenate([ctx[0], x[0]], axis=0)
    sc0 = jnp.stack([m_ctx[0, 1], m_lat[0, 1]])
    sh0 = jnp.stack([m_ctx[0, 0], m_lat[0, 0]])
    a0 = modnorm_fwd(h0, norm1_w[0][None], sc0, sh0, LC // TB, "ssd_norm")
    (z, xbc_pre, dt_raw), (w_out_g, w_pw2_g) = smm_fwd(a0, w_in, None, in_segs, "ssd_in", gather_in_proj)
    w_out = w_out_g.reshape(DI, D)
    w_pw2 = w_pw2_g.reshape(D, D)
    segs = ((0, LC), (LC, L))
    xbc, (w_down0_g, w_pw1) = ssd_conv_fwd(xbc_pre, conv_w5, ssd_conv_b, segs, "ssd_conv", gather_in_conv)
    w_down[0] = w_down0_g.reshape(FH, D)
    dt4 = dt_raw[:, :n_dt].reshape(T, 2, G, HPG)
    dtc = jnp.transpose(dt4, (1, 2, 0, 3))
    dtr = jnp.transpose(dt4, (1, 2, 3, 0))
    bias3 = ssd_dt_bias[0].reshape(2, G, HPG)
    alog3 = ssd_a_log[0].reshape(2, G, HPG)
    bc_, br_ = bias3[:, :, None, :], bias3[:, :, :, None]
    alc, alr = alog3[:, :, None, :], alog3[:, :, :, None]
    (y2, s_in_all), (w_up[0], w_down1_g) = ssd_scan_fwd(xbc, dtc, dtr, bc_, br_, alc, alr, nctx, "ssd_scan", gather_in_scan)
    w_down[1] = w_down1_g.reshape(FH, D)
    dexp = jnp.repeat(ssd_d[0], P)[None, :]
    yn = ssd_gate_fwd(y2, xbc, z, dexp, ssd_norm_w, LC // GTB, "ssd_gate")
    o_ssd = matmul(yn, w_out, "nn", f32, "ssd_out")
    hx = x[0]
    h1 = resgate_fwd(hx, o_ssd, m_lat[0, 2], zero_row, "ssd_res")
    h2, ffn0_saved, (w_up[1],) = ffn_fwd(h1, 0, "0", gather_in_gate)

    a1 = modnorm_fwd(h2, norm1_w[1][None], m_lat[1, 1][None], m_lat[1, 0][None], 0, "conf_norm")
    pa, pg = smm_fwd(a1, w_pw1, None, pw1_segs, "conf_pw1")
    dwc, _ = conf_glu_conv_fwd(pa, pg, b_pw1, w_dw, b_dw, "conf_conv")
    s1 = ln_silu_fwd(dwc, ln_w, ln_b, "conf_ln")
    o_conf = matmul(s1, w_pw2, "nn", f32, "conf_pw2")
    h3 = resgate_fwd(h2, o_conf, m_lat[1, 2], b_pw2, "conf_res")
    h4, ffn1_saved, _ = ffn_fwd(h3, 1, "1")

    loss_part, dh4, g_final = final_loss(h4, final_norm_w[None], loss_target[0], "loss_head")
    dh3, gf1 = ffn_bwd(dh4, h3, 1, ffn1_saved, "1")

    do_conf, dg1_1, g_b_pw2 = resgate_bwd(dh3, o_conf, m_lat[1, 2], b_pw2, "conf_res_bwd")
    g_pw2 = matmul(s1, do_conf, "tn", bf16, "conf_pw2_dw")
    ds1 = matmul(do_conf, w_pw2, "nt", bf16, "conf_pw2_dx")
    ddwc, g_ln_w, g_ln_b = ln_silu_bwd(dwc, ln_w, ln_b, ds1, "conf_ln_bwd")
    dpa, dpg, dba, dbg, g_w_dw, g_b_dw = conf_glu_conv_bwd(pa, pg, b_pw1, w_dw, ddwc, "conf_conv_bwd")
    g_b_pw1 = jnp.concatenate([dba, dbg], axis=1)
    g_pw1 = smm_dw(a1, [dpa, dpg], 2 * D // N_DEV, pw1_segs, 1, "conf_pw1_dw")
    da1, _ = smm_dx([dpa, dpg], w_pw1, None, pw1_segs, bf16, "conf_pw1_dx")
    dh2, g_n1_1, dsc1_1, dsh1_1 = modnorm_bwd(h2, norm1_w[1][None], m_lat[1, 1][None], m_lat[1, 0][None], da1, dh3, 0,
                                              "conf_norm_bwd")
    dh1, gf0 = ffn_bwd(dh2, h1, 0, ffn0_saved, "0")

    do_ssd, dg1_0, _ = resgate_bwd(dh1, o_ssd, m_lat[0, 2], zero_row, "ssd_res_bwd")
    g_w_out = matmul(yn, do_ssd, "tn", bf16, "ssd_out_dw")
    dyn = matmul(do_ssd, w_out, "nt", bf16, "ssd_out_dx")
    core = mc.reshape(1).astype(jnp.int32)

    def by_device(t):
        return t.reshape((4, 2, -1, t.shape[-1]))

    early = [by_device(t) for t in (gf1["w_up"], gf1["w_down"], g_pw2, g_pw1, gf0["w_up"], gf0["w_down"], g_w_out)]
    (dy, dx_skip, dz, g_dexp, g_ssd_norm), early_sib = ssd_gate_bwd(
        y2, xbc, z, dexp, ssd_norm_w, dyn, LC // GTB, "ssd_gate_bwd", SiblingExchange(early))
    early_part = [add_own(t, r_, core, f"reduce_add{i}") for i, (t, r_) in enumerate(zip(early, early_sib))]
    (dxbc2, ddtc, ddtr, dbc, dbr, dalc, dalr), early_red = ssd_scan_bwd(
        xbc, dtc, dtr, bc_, br_, alc, alr, s_in_all, dy, nctx, "ssd_scan_bwd", ChipsExchange(early_part))
    ddt = (jnp.transpose(ddtc, (2, 0, 1, 3)) + jnp.transpose(ddtr, (3, 0, 1, 2))).reshape(T, n_dt)
    g_dt_bias = (dbc[:, :, 0, :] + dbr[:, :, :, 0]).reshape(2, NH_SSD)
    g_a_log = (dalc[:, :, 0, :] + dalr[:, :, :, 0]).reshape(2, NH_SSD)
    g_ssd_d = g_dexp[0, :NH_SSD]
    du, g_conv_w5, g_conv_b5 = ssd_conv_bwd(xbc_pre, conv_w5, ssd_conv_b, dxbc2, dx_skip, segs, "ssd_conv_bwd")
    ddt_p = _pad_to(ddt, 128).astype(bf16)
    g_w_in = smm_dw(a0, [dz, du, ddt_p], w_in.shape[-1], in_segs, 2, "ssd_in_dw")
    g_ffn_cw = jnp.stack([gf0["conv_w"], gf1["conv_w"]])
    small_shards = [_col_shards(t) for t in (g_conv_w5, g_b_pw1, g_w_dw, g_b_dw, g_ln_w, g_ln_b, g_b_pw2, g_ffn_cw)]
    gsizes = [s_.shape[1] for s_ in small_shards]
    g_small = _pad_to(jnp.concatenate(small_shards, axis=1), PACK_ALIGN).astype(bf16)
    late = [by_device(g_w_in), by_device(g_small.reshape(N_DEV, -1, PACK_W))]
    da0, late_sib = smm_dx([dz, du, ddt_p], w_in, None, in_segs, f32, "ssd_in_dx", SiblingExchange(late))
    late_part = [add_own(t, r_, core, f"reduce_add_late{i}") for i, (t, r_) in enumerate(zip(late, late_sib))]
    late_red = exchange(ChipsExchange(late_part), "reduce_chips_late")
    dres0 = jnp.concatenate([jnp.zeros((LC, D), f32), dh1], axis=0)
    dh0, g_n1_0, dsc1_0, dsh1_0 = modnorm_bwd(h0, norm1_w[0][None], sc0, sh0, da0, dres0, LC // TB, "ssd_norm_bwd")
    grad_x = dh0[LC:][None]

    zeros_d = jnp.zeros((1, D), f32)
    dm_lat = jnp.stack([
        jnp.concatenate([dsh1_0[1], dsc1_0[1], dg1_0, gf0["sh2"], gf0["sc2"], gf0["g2"]], axis=1),
        jnp.concatenate([dsh1_1[0], dsc1_1[0], dg1_1, gf1["sh2"], gf1["sc2"], gf1["g2"]], axis=1)])
    dm_ctx = jnp.stack([
        jnp.concatenate([dsh1_0[0], dsc1_0[0]] + [zeros_d] * 4, axis=1), jnp.zeros((1, 6 * D), f32)])
    dm_mine = jnp.concatenate([dm_lat.reshape(2, 6 * D), dm_ctx.reshape(2, 6 * D),
                               jnp.zeros((4, 6 * D), f32)], axis=0)
    dm_g = allgather_small(dm_mine, "gather_dmod")
    dm_all = jnp.concatenate([jnp.moveaxis(dm_g[:, 0:2], 0, 1), jnp.moveaxis(dm_g[:, 2:4], 0, 1)], axis=1)
    dm_sh = lax.dynamic_slice_in_dim(dm_all, me * mod_cols, mod_cols, axis=2)
    g_mod_w, g_cctx_part, g_mod_b = mod_bwd(c16, mod_w, dm_sh, dm_all, "mod_bwd")

    rep = [jnp.stack([g_n1_0[0], g_n1_1[0]]), jnp.stack([gf0["norm2"][0], gf1["norm2"][0]]), g_conv_b5, g_dt_bias, g_a_log,
           g_ssd_d, g_ssd_norm, jnp.stack([gf0["conv_b"][0], gf1["conv_b"][0]]), g_final, g_cctx_part, loss_part[:, :1]]
    rep_sizes = [r_.size for r_ in rep]
    rep_flat = _pad_to(jnp.concatenate([r_.reshape(-1) for r_ in rep]), 8 * PACK_W).reshape(-1, PACK_W)
    _, rep_sum = allgather_small(rep_flat, "reduce_replicated", with_sum=True)
    rep_sum = rep_sum.reshape(-1)
    roffs = [0]
    for s_ in rep_sizes:
        roffs.append(roffs[-1] + s_)
    rp = [rep_sum[roffs[i]:roffs[i + 1]] for i in range(len(rep_sizes))]
    loss = rp[10].reshape(())

    r_up1, r_down1, r_pw2, r_pw1, r_up0, r_down0, r_out = early_red
    r_in, r_small = late_red
    from_chips = [r_in, r_up0, r_up1, r_pw1, r_out, r_down0, r_down1, r_pw2]
    g_flat = sum_rows(r_small, "reduce_sum_small").reshape(-1)
    goffs = [0]
    for s_ in gsizes:
        goffs.append(goffs[-1] + s_)
    gs = [g_flat[goffs[i]:goffs[i + 1]] for i in range(len(gsizes))]

    big = {}
    big["ssd_w_in"] = sum_adamw(from_chips[0], ssd_w_in[0], m_ssd_w_in[0], v_ssd_w_in[0], None, "adamw_ssd_w_in")
    up = [sum_adamw(from_chips[1 + i], ffn_w_up, m_ffn_w_up, v_ffn_w_up, i, f"adamw_ffn_w_up{i}") for i in range(2)]
    big["ffn_w_up"] = tuple(jnp.stack([up[0][k], up[1][k]]) for k in range(4))
    big["conf_w_pw1"] = sum_adamw(from_chips[3], conf_w_pw1[0], m_conf_w_pw1[0], v_conf_w_pw1[0], None, "adamw_conf_w_pw1")
    big["ssd_w_out"] = sum_adamw(from_chips[4], ssd_w_out[0], m_ssd_w_out[0], v_ssd_w_out[0], None, "adamw_ssd_w_out")
    dn = [sum_adamw(from_chips[5 + i], ffn_w_down, m_ffn_w_down, v_ffn_w_down, i, f"adamw_ffn_w_down{i}") for i in range(2)]
    big["ffn_w_down"] = tuple(jnp.stack([dn[0][k], dn[1][k]]) for k in range(4))
    big["conf_w_pw2"] = sum_adamw(from_chips[7], conf_w_pw2[0], m_conf_w_pw2[0], v_conf_w_pw2[0], None, "adamw_conf_w_pw2")
    grads = {
        "c_ctx": rp[9], "mod_w": g_mod_w, "mod_b": g_mod_b, "norm1_w": rp[0], "norm2_w": rp[1],
        "ssd_conv_w": gs[0], "ssd_conv_b": rp[2], "ssd_dt_bias": rp[3], "ssd_a_log": rp[4], "ssd_d": rp[5],
        "ssd_norm_w": rp[6], "conf_b_pw1": gs[1], "conf_w_dw": gs[2],
        "conf_b_dw": gs[3], "conf_ln_w": gs[4], "conf_ln_b": gs[5], "conf_b_pw2": gs[6],
        "ffn_conv_w": gs[7], "ffn_conv_b": rp[7], "final_norm_w": rp[8],
    }
    weights = dict(c_ctx=c_ctx, mod_w=mod_w, mod_b=mod_b, norm1_w=norm1_w, norm2_w=norm2_w, ssd_w_in=ssd_w_in, ssd_conv_w=ssd_conv_w, ssd_conv_b=ssd_conv_b, ssd_dt_bias=ssd_dt_bias, ssd_a_log=ssd_a_log, ssd_d=ssd_d, ssd_norm_w=ssd_norm_w, ssd_w_out=ssd_w_out, conf_w_pw1=conf_w_pw1, conf_b_pw1=conf_b_pw1, conf_w_dw=conf_w_dw, conf_b_dw=conf_b_dw, conf_ln_w=conf_ln_w, conf_ln_b=conf_ln_b, conf_w_pw2=conf_w_pw2, conf_b_pw2=conf_b_pw2, ffn_w_up=ffn_w_up, ffn_conv_w=ffn_conv_w, ffn_conv_b=ffn_conv_b, ffn_w_down=ffn_w_down, final_norm_w=final_norm_w)
    m_in = dict(c_ctx=m_c_ctx, mod_w=m_mod_w, mod_b=m_mod_b, norm1_w=m_norm1_w, norm2_w=m_norm2_w, ssd_w_in=m_ssd_w_in, ssd_conv_w=m_ssd_conv_w, ssd_conv_b=m_ssd_conv_b, ssd_dt_bias=m_ssd_dt_bias, ssd_a_log=m_ssd_a_log, ssd_d=m_ssd_d, ssd_norm_w=m_ssd_norm_w, ssd_w_out=m_ssd_w_out, conf_w_pw1=m_conf_w_pw1, conf_b_pw1=m_conf_b_pw1, conf_w_dw=m_conf_w_dw, conf_b_dw=m_conf_b_dw, conf_ln_w=m_conf_ln_w, conf_ln_b=m_conf_ln_b, conf_w_pw2=m_conf_w_pw2, conf_b_pw2=m_conf_b_pw2, ffn_w_up=m_ffn_w_up, ffn_conv_w=m_ffn_conv_w, ffn_conv_b=m_ffn_conv_b, ffn_w_down=m_ffn_w_down, final_norm_w=m_final_norm_w)
    v_in = dict(c_ctx=v_c_ctx, mod_w=v_mod_w, mod_b=v_mod_b, norm1_w=v_norm1_w, norm2_w=v_norm2_w, ssd_w_in=v_ssd_w_in, ssd_conv_w=v_ssd_conv_w, ssd_conv_b=v_ssd_conv_b, ssd_dt_bias=v_ssd_dt_bias, ssd_a_log=v_ssd_a_log, ssd_d=v_ssd_d, ssd_norm_w=v_ssd_norm_w, ssd_w_out=v_ssd_w_out, conf_w_pw1=v_conf_w_pw1, conf_b_pw1=v_conf_b_pw1, conf_w_dw=v_conf_w_dw, conf_b_dw=v_conf_b_dw, conf_ln_w=v_conf_ln_w, conf_ln_b=v_conf_ln_b, conf_w_pw2=v_conf_w_pw2, conf_b_pw2=v_conf_b_pw2, ffn_w_up=v_ffn_w_up, ffn_conv_w=v_ffn_conv_w, ffn_conv_b=v_ffn_conv_b, ffn_w_down=v_ffn_w_down, final_norm_w=v_final_norm_w)

    out_g, out_d, out_m, out_v = [], [], [], []
    for name_, w_ in weights.items():
        shape = w_.shape
        if name_ in big:
            for lst, t in zip((out_g, out_d, out_m, out_v), big[name_]):
                lst.append(t.reshape(shape))
            continue
        cols2 = shape[-1] if len(shape) > 1 else shape[0]
        g2 = grads[name_].reshape(-1, cols2)
        d_, nm_, nv_ = adamw(w_.reshape(-1, cols2), g2, m_in[name_].reshape(-1, cols2), v_in[name_].reshape(-1, cols2),
                             f"adamw_{name_}")
        out_g.append(g2.reshape(shape))
        out_d.append(d_.reshape(shape))
        out_m.append(nm_.reshape(shape))
        out_v.append(nv_.reshape(shape))
    return (loss, grad_x, *out_g, *out_d, *out_m, *out_v)
```

```python
import functools

import jax
import jax.numpy as jnp
from jax import lax
from jax.experimental import pallas as pl
from jax.experimental.pallas import tpu as pltpu

f32 = jnp.float32
bf16 = jnp.bfloat16
HI = lax.Precision.HIGHEST
S = jax.ShapeDtypeStruct
MESH = pl.DeviceIdType.MESH
ANY = pl.BlockSpec(memory_space=pl.ANY)
VMEM = pl.BlockSpec(memory_space=pltpu.VMEM)

N_DEV = 8
D = 1024
DI = 2048
CONVD = 4096
FH = 2816
GRID_W = 64
Q = 128
HPG = 4
P = 64
N = 128
G = 8
GW = HPG * P
NH_SSD = G * HPG
EPS = 1e-6
ADAM_LR, ADAM_B1, ADAM_B2, ADAM_EPS, ADAM_WD, ADAM_STEP = 0.001, 0.9, 0.999, 1e-08, 0.01, 10
VMEM_LIMIT_BYTES = 56 * 1024 * 1024
PACK_W = 1024
TB = 256


def _cparams(*sem):
    return pltpu.CompilerParams(dimension_semantics=sem, vmem_limit_bytes=VMEM_LIMIT_BYTES)


def _pick(n, prefs):
    for p in prefs:
        if n % p == 0:
            return p
    return n


def _sigmoid(x):
    return 1.0 / (1.0 + jnp.exp(-x))


def _softplus(x):
    return jnp.maximum(x, 0.0) + jnp.log(1.0 + jnp.exp(-jnp.abs(x)))


def matmul(a, b, mode, out_dtype, name):
    if mode == "nn":
        (M, K), (_, Nn) = a.shape, b.shape
        bm, bn, bk = _pick(M, (512, 384, 256, 128)), Nn, K
    elif mode == "tn":
        (K, M), (_, Nn) = a.shape, b.shape
        bm, bn, bk = M, Nn, _pick(K, (256, 128))
    else:
        (M, K), (Nn, _) = a.shape, b.shape
        bm, bn, bk = _pick(M, (512, 384, 256, 128)), Nn, K
    nk = K // bk
    dims = {"nn": (((1,), (0,)), ((), ())), "tn": (((0,), (0,)), ((), ())), "nt": (((1,), (1,)), ((), ()))}[mode]

    def body(a_ref, b_ref, o_ref, acc_ref):
        k = pl.program_id(2)

        @pl.when(k == 0)
        def _():
            acc_ref[...] = jnp.zeros_like(acc_ref)

        acc_ref[...] += lax.dot_general(a_ref[...].astype(bf16), b_ref[...].astype(bf16), dims,
                                        preferred_element_type=f32)

        @pl.when(k == nk - 1)
        def _():
            o_ref[...] = acc_ref[...].astype(out_dtype)

    if mode == "nn":
        a_spec = pl.BlockSpec((bm, bk), lambda i, j, k: (i, k))
        b_spec = pl.BlockSpec((bk, bn), lambda i, j, k: (k, j))
    elif mode == "tn":
        a_spec = pl.BlockSpec((bk, bm), lambda i, j, k: (k, i))
        b_spec = pl.BlockSpec((bk, bn), lambda i, j, k: (k, j))
    else:
        a_spec = pl.BlockSpec((bm, bk), lambda i, j, k: (i, k))
        b_spec = pl.BlockSpec((bn, bk), lambda i, j, k: (j, k))
    return pl.pallas_call(
        body, grid=(M // bm, Nn // bn, nk), in_specs=[a_spec, b_spec],
        out_specs=pl.BlockSpec((bm, bn), lambda i, j, k: (i, j)),
        out_shape=S((M, Nn), out_dtype), scratch_shapes=[pltpu.VMEM((bm, bn), f32)],
        compiler_params=_cparams("parallel", "parallel", "arbitrary"), name=name,
    )(a, b)


SMM_BM = 256
SMM_ROWS = (256,)


def _shard_pieces(seg_widths, n):
    bounds = [0]
    for sw in seg_widths:
        bounds.append(bounds[-1] + sw)
    assert bounds[-1] == N_DEV * n, (seg_widths, n)
    out = []
    for j in range(N_DEV):
        lo, hi = j * n, (j + 1) * n
        pcs = []
        for si in range(len(seg_widths)):
            a, b = max(lo, bounds[si]), min(hi, bounds[si + 1])
            if a < b:
                pcs.append((si, a - bounds[si], a - lo, b - a))
        out.append(pcs)
    return out


def _w_spec(w, layer):
    if layer is None:
        return pl.BlockSpec(w.shape, lambda *idx: (0, 0, 0))
    return pl.BlockSpec((N_DEV, None) + w.shape[2:], lambda *idx: (0, layer, 0, 0))


def smm_fwd(a, w, layer, seg_widths, name, hosted=None):
    M, K = a.shape
    n = w.shape[-1]
    pieces = _shard_pieces(seg_widths, n)
    padded = [sw + (-sw) % 128 for sw in seg_widths]
    bm = _pick(M, SMM_ROWS)

    def body(a_ref, w_ref, *o_refs):
        av = a_ref[...]
        for si, sw in enumerate(seg_widths):
            if padded[si] != sw:
                o_refs[si][:, pl.ds(padded[si] - 128, 128)] = jnp.zeros((bm, 128), f32)
        for j in range(N_DEV):
            for si, soff, woff, wd in pieces[j]:
                o_refs[si][:, pl.ds(soff, wd)] = jnp.dot(av, w_ref[j, :, pl.ds(woff, wd)], preferred_element_type=f32)

    outs, extra = _host_call(
        body, (M // bm,), [pl.BlockSpec((bm, K), lambda i: (i, 0)), _w_spec(w, layer)],
        [pl.BlockSpec((bm, pw), lambda i: (i, 0)) for pw in padded], [S((M, pw), f32) for pw in padded], [],
        ("parallel",), name, (a, w), hosted)
    return outs if hosted is None else (outs, extra)


def smm_dx(d_segs, w, layer, seg_widths, out_dtype, name, hosted=None):
    M = d_segs[0].shape[0]
    K, n = w.shape[-2], w.shape[-1]
    pieces = _shard_pieces(seg_widths, n)
    ns = len(d_segs)
    bm = _pick(M, SMM_ROWS)

    def body(*refs):
        d_refs, w_ref, o_ref = refs[:ns], refs[ns], refs[ns + 1]
        acc = jnp.zeros((bm, K), f32)
        for j in range(N_DEV):
            for si, soff, woff, wd in pieces[j]:
                acc = acc + lax.dot_general(d_refs[si][:, pl.ds(soff, wd)], w_ref[j, :, pl.ds(woff, wd)],
                                            (((1,), (1,)), ((), ())), preferred_element_type=f32)
        o_ref[...] = acc.astype(out_dtype)

    (out,), extra = _host_call(
        body, (M // bm,),
        [pl.BlockSpec((bm, d.shape[1]), lambda i: (i, 0)) for d in d_segs] + [_w_spec(w, layer)],
        [pl.BlockSpec((bm, K), lambda i: (i, 0))], [S((M, K), out_dtype)], [], ("parallel",), name,
        (*d_segs, w), hosted)
    return out, extra


def smm_dw(a, d_segs, n, seg_widths, ngrp, transposed, name):
    M, K = a.shape
    pieces = _shard_pieces(seg_widths, n)
    per = N_DEV // ngrp
    nI = M // SMM_BM
    ns = len(d_segs)
    shard = (n, K) if transposed else (K, n)

    def body(*refs):
        a_ref, d_refs, o_ref, acc_ref = refs[0], refs[1:1 + ns], refs[1 + ns], refs[2 + ns]
        grp = pl.program_id(0)
        i = pl.program_id(1)

        @pl.when(i == 0)
        def _():
            acc_ref[...] = jnp.zeros_like(acc_ref)

        av = a_ref[...]
        for gs in range(ngrp):
            def one_group(gs=gs):
                for jj in range(per):
                    for si, soff, woff, wd in pieces[gs * per + jj]:
                        dv = d_refs[si][:, pl.ds(soff, wd)]
                        if transposed:
                            acc_ref[jj, pl.ds(woff, wd), :] += lax.dot_general(
                                dv, av, (((0,), (0,)), ((), ())), preferred_element_type=f32)
                        else:
                            acc_ref[jj, :, pl.ds(woff, wd)] += lax.dot_general(
                                av, dv, (((0,), (0,)), ((), ())), preferred_element_type=f32)
            pl.when(grp == gs)(one_group)

        @pl.when(i == nI - 1)
        def _():
            o_ref[...] = acc_ref[...].astype(bf16)

    return pl.pallas_call(
        body, grid=(ngrp, nI),
        in_specs=[pl.BlockSpec((SMM_BM, K), lambda g, i: (i, 0))]
        + [pl.BlockSpec((SMM_BM, d.shape[1]), lambda g, i: (i, 0)) for d in d_segs],
        out_specs=pl.BlockSpec((per,) + shard, lambda g, i: (g, 0, 0)), out_shape=S((N_DEV,) + shard, bf16),
        scratch_shapes=[pltpu.VMEM((per,) + shard, f32)],
        compiler_params=_cparams("arbitrary", "arbitrary"), name=name)(a, *d_segs)


def _modnorm_f(h, w, sc, sh):
    y = h * lax.rsqrt(jnp.mean(h * h, axis=-1, keepdims=True) + EPS)
    return (y * w) * (1.0 + sc) + sh


def _kind_specs(nctxb):
    if nctxb > 0:
        return pl.BlockSpec((None, 1, D), lambda i: (jnp.where(i < nctxb, 0, 1), 0, 0))
    return pl.BlockSpec((None, 1, D), lambda i: (0, 0, 0))


def modnorm_fwd(h, w, sc, sh, nctxb, name):
    T = h.shape[0]

    def body(h_ref, w_ref, sc_ref, sh_ref, o_ref):
        o_ref[...] = _modnorm_f(h_ref[...], w_ref[...], sc_ref[...], sh_ref[...]).astype(bf16)

    blk = pl.BlockSpec((TB, D), lambda i: (i, 0))
    row = pl.BlockSpec((1, D), lambda i: (0, 0))
    ks = _kind_specs(nctxb)
    return pl.pallas_call(body, grid=(T // TB,), in_specs=[blk, row, ks, ks], out_specs=blk,
                          out_shape=S((T, D), bf16), compiler_params=_cparams("parallel"), name=name)(h, w, sc, sh)


def modnorm_bwd(h, w, sc, sh, da, dres, nctxb, name):
    T = h.shape[0]
    kinds = sc.shape[0]

    def body(h_ref, w_ref, sc_ref, sh_ref, da_ref, dres_ref, dh_ref, dw_ref, dsc_ref, dsh_ref):
        i = pl.program_id(0)
        _, vjp = jax.vjp(_modnorm_f, h_ref[...], w_ref[...], sc_ref[...], sh_ref[...])
        dh, dw, dsc, dsh = vjp(da_ref[...].astype(f32))
        dh_ref[...] = dres_ref[...] + dh

        @pl.when(i == 0)
        def _():
            dw_ref[...] = jnp.zeros_like(dw_ref)

        @pl.when((i == 0) | (i == nctxb))
        def _():
            dsc_ref[...] = jnp.zeros_like(dsc_ref)
            dsh_ref[...] = jnp.zeros_like(dsh_ref)

        dw_ref[...] += dw
        dsc_ref[...] += dsc
        dsh_ref[...] += dsh

    blk = pl.BlockSpec((TB, D), lambda i: (i, 0))
    lat = pl.BlockSpec((TB, D), lambda i: (jnp.maximum(i - nctxb, 0), 0))
    row = pl.BlockSpec((1, D), lambda i: (0, 0))
    ks = _kind_specs(nctxb)
    return pl.pallas_call(
        body, grid=(T // TB,), in_specs=[blk, row, ks, ks, blk, lat], out_specs=[lat, row, ks, ks],
        out_shape=[S((T - nctxb * TB, D), f32), S((1, D), f32), S((kinds, 1, D), f32), S((kinds, 1, D), f32)],
        compiler_params=_cparams("arbitrary"), name=name)(h, w, sc, sh, da, dres)


def resgate_fwd(h, o, g, b, name):
    T = h.shape[0]

    def body(h_ref, o_ref, g_ref, b_ref, out_ref):
        out_ref[...] = h_ref[...] + g_ref[...] * (o_ref[...] + b_ref[...])

    blk = pl.BlockSpec((TB, D), lambda i: (i, 0))
    row = pl.BlockSpec((1, D), lambda i: (0, 0))
    return pl.pallas_call(body, grid=(T // TB,), in_specs=[blk, blk, row, row], out_specs=blk,
                          out_shape=S((T, D), f32), compiler_params=_cparams("parallel"), name=name)(h, o, g, b)


def resgate_bwd(dh, o, g, b, name):
    T = dh.shape[0]

    def body(dh_ref, o_ref, g_ref, b_ref, do_ref, dg_ref, db_ref):
        i = pl.program_id(0)

        @pl.when(i == 0)
        def _():
            dg_ref[...] = jnp.zeros_like(dg_ref)
            db_ref[...] = jnp.zeros_like(db_ref)

        dh = dh_ref[...]
        do = g_ref[...] * dh
        do_ref[...] = do.astype(bf16)
        dg_ref[...] += jnp.sum(dh * (o_ref[...] + b_ref[...]), axis=0, keepdims=True)
        db_ref[...] += jnp.sum(do, axis=0, keepdims=True)

    blk = pl.BlockSpec((TB, D), lambda i: (i, 0))
    row = pl.BlockSpec((1, D), lambda i: (0, 0))
    return pl.pallas_call(body, grid=(T // TB,), in_specs=[blk, blk, row, row], out_specs=[blk, row, row],
                          out_shape=[S((T, D), bf16), S((1, D), f32), S((1, D), f32)],
                          compiler_params=_cparams("arbitrary"), name=name)(dh, o, g, b)


def final_loss(h, w, tgt, name):
    T = h.shape[0]

    def f(hv, wv, tv):
        y = (hv * lax.rsqrt(jnp.mean(hv * hv, axis=-1, keepdims=True) + EPS)) * wv
        e = y - tv
        return 0.5 * jnp.sum(jnp.sum(e * e, axis=-1, keepdims=True), axis=0, keepdims=True) * (1.0 / D)

    def body(h_ref, w_ref, t_ref, loss_ref, dh_ref, dw_ref):
        i = pl.program_id(0)
        tv = t_ref[...]
        val, vjp = jax.vjp(lambda a, b_: f(a, b_, tv), h_ref[...], w_ref[...])
        dh, dw = vjp(jnp.ones((1, 1), f32))
        dh_ref[...] = dh

        @pl.when(i == 0)
        def _():
            loss_ref[...] = jnp.zeros_like(loss_ref)
            dw_ref[...] = jnp.zeros_like(dw_ref)

        loss_ref[...] += jnp.broadcast_to(val, (1, 128))
        dw_ref[...] += dw

    blk = pl.BlockSpec((TB, D), lambda i: (i, 0))
    row = pl.BlockSpec((1, D), lambda i: (0, 0))
    return pl.pallas_call(body, grid=(T // TB,), in_specs=[blk, row, blk],
                          out_specs=[pl.BlockSpec((1, 128), lambda i: (0, 0)), blk, row],
                          out_shape=[S((1, 128), f32), S((T, D), f32), S((1, D), f32)],
                          compiler_params=_cparams("arbitrary"), name=name)(h, w, tgt)


CB = 256
RT = 32
RTB = 16


def _fold8(t):
    acc = t[0:8]
    for k in range(1, t.shape[0] // 8):
        acc = acc + t[8 * k:8 * (k + 1)]
    return acc


def _rows(start, off=0, rt=RT):
    return pl.ds(pl.multiple_of(start + off, 8), rt)


def _rowsb(start, off=0):
    return _rows(start, off, RTB)


def _zero_rows(ref, start, n):
    ref[pl.ds(start, n), :] = jnp.zeros((n, ref.shape[1]), f32)


K5, HALF5, PAD5 = 5, 2, 8


def _shift_copies5(base_ref, s_ref, ln, sign):
    for k in range(K5):
        s_ref[k, pl.ds(0, ln), :] = base_ref[pl.ds(PAD5 + sign * (k - HALF5), ln), :]


def ssd_conv_fwd(u, w, b, segs, name, hosted=None):
    T = u.shape[0]
    maxlen = max(ln for _, ln in segs)

    def body(u_ref, w_ref, b_ref, o_ref, base_ref, s_ref):
        wv = [w_ref[pl.ds(k, 1), :] for k in range(K5)]
        bv = b_ref[...]
        for s0, ln in segs:
            _zero_rows(base_ref, 0, PAD5)
            _zero_rows(base_ref, PAD5 + ln, PAD5)
            base_ref[pl.ds(PAD5, ln), :] = u_ref[pl.ds(s0, ln), :]
            _shift_copies5(base_ref, s_ref, ln, 1)

            def tile(i, carry):
                r = i * RT
                acc = jnp.broadcast_to(bv, (RT, CB))
                for k in range(K5):
                    acc = acc + s_ref[k, _rows(r), :] * wv[k]
                o_ref[_rows(r, s0), :] = acc * _sigmoid(acc)
                return carry

            lax.fori_loop(0, ln // RT, tile, 0, unroll=2)

    (out,), extra = _host_call(
        body, (CONVD // CB,),
        [pl.BlockSpec((T, CB), lambda j: (0, j)), pl.BlockSpec((K5, CB), lambda j: (0, j)),
         pl.BlockSpec((1, CB), lambda j: (0, j))],
        [pl.BlockSpec((T, CB), lambda j: (0, j))], [S((T, CONVD), f32)],
        [pltpu.VMEM((maxlen + 2 * PAD5, CB), f32), pltpu.VMEM((K5, maxlen, CB), f32)],
        ("parallel",), name, (u, w, b), hosted)
    return out, extra


def ssd_conv_bwd(proj, w, b, dy2, dskip, segs, name):
    T = proj.shape[0]
    maxlen = max(ln for _, ln in segs)
    nskip = DI // CB

    def body(u_ref, w_ref, b_ref, dya_ref, dyb_ref, dsk_ref, du_ref, dw_ref, db_ref, base_ref, s_ref):
        wv = [w_ref[pl.ds(k, 1), :] for k in range(K5)]
        bv = b_ref[...]
        has_skip = (pl.program_id(0) < nskip).astype(f32)
        acc8 = tuple(jnp.zeros((8, CB), f32) for _ in range(K5 + 1))
        for s0, ln in segs:
            _zero_rows(base_ref, 0, PAD5)
            _zero_rows(base_ref, PAD5 + ln, PAD5)
            base_ref[pl.ds(PAD5, ln), :] = u_ref[pl.ds(s0, ln), :]
            _shift_copies5(base_ref, s_ref, ln, 1)

            def tile1(i, carry):
                r = i * RTB
                taps = [s_ref[k, _rowsb(r), :] for k in range(K5)]
                pre = jnp.broadcast_to(bv, (RTB, CB))
                for k in range(K5):
                    pre = pre + taps[k] * wv[k]
                sg = _sigmoid(pre)
                dy = dya_ref[_rowsb(r, s0), :] + dyb_ref[_rowsb(r, s0), :] + has_skip * dsk_ref[_rowsb(r, s0), :]
                dpre = dy * (sg * (1.0 + pre * (1.0 - sg)))
                base_ref[_rowsb(r, PAD5), :] = dpre
                new = [carry[k] + _fold8(dpre * taps[k]) for k in range(K5)]
                new.append(carry[K5] + _fold8(dpre))
                return tuple(new)

            acc8 = lax.fori_loop(0, ln // RTB, tile1, acc8, unroll=2)
            _shift_copies5(base_ref, s_ref, ln, -1)

            def tile2(i, carry):
                r = i * RTB
                du = jnp.zeros((RTB, CB), f32)
                for k in range(K5):
                    du = du + s_ref[k, _rowsb(r), :] * wv[k]
                du_ref[_rowsb(r, s0), :] = du.astype(bf16)
                return carry

            lax.fori_loop(0, ln // RTB, tile2, 0, unroll=4)
        for k in range(K5):
            dw_ref[pl.ds(k, 1), :] = jnp.sum(acc8[k], axis=0, keepdims=True)
        db_ref[...] = jnp.sum(acc8[K5], axis=0, keepdims=True)

    cblk = pl.BlockSpec((T, CB), lambda j: (0, j))
    return pl.pallas_call(
        body, grid=(CONVD // CB,),
        in_specs=[cblk, pl.BlockSpec((K5, CB), lambda j: (0, j)), pl.BlockSpec((1, CB), lambda j: (0, j)),
                  pl.BlockSpec((None, T, CB), lambda j: (0, 0, j)), pl.BlockSpec((None, T, CB), lambda j: (1, 0, j)),
                  pl.BlockSpec((T, CB), lambda j: (0, jnp.minimum(j, nskip - 1)))],
        out_specs=[cblk, pl.BlockSpec((K5, CB), lambda j: (0, j)), pl.BlockSpec((1, CB), lambda j: (0, j))],
        out_shape=[S((T, CONVD), bf16), S((K5, CONVD), f32), S((1, CONVD), f32)],
        scratch_shapes=[pltpu.VMEM((maxlen + 2 * PAD5, CB), f32), pltpu.VMEM((K5, maxlen, CB), f32)],
        compiler_params=_cparams("parallel"), name=name)(proj, w, b, dy2, dy2, dskip)


GPAD = GRID_W


def _grid_copies(g_ref, src, L):
    col = lax.broadcasted_iota(jnp.int32, (L, CB), 0) & (GRID_W - 1)
    for d in range(3):
        _zero_rows(g_ref.at[d], 0, GPAD)
        _zero_rows(g_ref.at[d], GPAD + L, GPAD)
    g_ref[1, pl.ds(GPAD, L), :] = src
    g_ref[0, pl.ds(GPAD, L), :] = jnp.where(col != 0, g_ref[1, pl.ds(GPAD - 1, L), :], 0.0)
    g_ref[2, pl.ds(GPAD, L), :] = jnp.where(col != GRID_W - 1, g_ref[1, pl.ds(GPAD + 1, L), :], 0.0)


def ffn_gate_fwd(val, gate, cw, cb_, name, hosted=None):
    L = val.shape[0]
    nb = FH // CB

    def body(val_ref, gate_ref, w_ref, b_ref, o_ref, g_ref):
        wv = [w_ref[pl.ds(t, 1), :] for t in range(9)]
        bv = b_ref[...]
        _grid_copies(g_ref, gate_ref[...], L)

        def tile(i, carry):
            r = i * RT
            acc = jnp.broadcast_to(bv, (RT, CB))
            for dr in range(3):
                for dc in range(3):
                    acc = acc + g_ref[dc, _rows(r, GPAD + (dr - 1) * GRID_W), :] * wv[3 * dr + dc]
            o_ref[_rows(r), :] = (acc * _sigmoid(acc) * val_ref[_rows(r), :]).astype(bf16)
            return carry

        lax.fori_loop(0, L // RT, tile, 0, unroll=2)

    cblk = pl.BlockSpec((L, CB), lambda j: (0, j))
    (out,), extra = _host_call(
        body, (nb,), [cblk, cblk, pl.BlockSpec((9, CB), lambda j: (0, j)), pl.BlockSpec((1, CB), lambda j: (0, j))],
        [cblk], [S((L, FH), bf16)], [pltpu.VMEM((3, L + 2 * GPAD, CB), f32)], ("parallel",), name,
        (val, gate, cw, cb_), hosted)
    return out, extra


def ffn_gate_bwd(val, gate, cw, cb_, dact, name):
    L = val.shape[0]
    nb = FH // CB

    def body(val_ref, gate_ref, w_ref, b_ref, da_ref, dval_ref, dgate_ref, dw_ref, db_ref, g_ref, d_ref):
        wv = [w_ref[pl.ds(t, 1), :] for t in range(9)]
        bv = b_ref[...]
        _grid_copies(g_ref, gate_ref[...], L)

        def tile1(i, carry):
            r = i * RTB

            def tap(t):
                return g_ref[t % 3, _rowsb(r, GPAD + (t // 3 - 1) * GRID_W), :]

            pre = jnp.broadcast_to(bv, (RTB, CB))
            for t in range(9):
                pre = pre + tap(t) * wv[t]
            sg = _sigmoid(pre)
            da = da_ref[_rowsb(r), :].astype(f32)
            dval_ref[_rowsb(r), :] = (da * pre * sg).astype(bf16)
            dpre = da * val_ref[_rowsb(r), :] * (sg * (1.0 + pre * (1.0 - sg)))
            d_ref[_rowsb(r), :] = dpre
            new = [carry[t] + _fold8(dpre * tap(t)) for t in range(9)]
            new.append(carry[9] + _fold8(dpre))
            return tuple(new)

        acc8 = lax.fori_loop(0, L // RTB, tile1, tuple(jnp.zeros((8, CB), f32) for _ in range(10)), unroll=2)
        for t in range(9):
            dw_ref[pl.ds(t, 1), :] = jnp.sum(acc8[t], axis=0, keepdims=True)
        db_ref[...] = jnp.sum(acc8[9], axis=0, keepdims=True)
        _grid_copies(g_ref, d_ref[...], L)

        def tile2(i, carry):
            r = i * RTB
            dg = jnp.zeros((RTB, CB), f32)
            for dr in range(3):
                for dc in range(3):
                    dg = dg + g_ref[2 - dc, _rowsb(r, GPAD - (dr - 1) * GRID_W), :] * wv[3 * dr + dc]
            dgate_ref[_rowsb(r), :] = dg.astype(bf16)
            return carry

        lax.fori_loop(0, L // RTB, tile2, 0, unroll=4)

    cblk = pl.BlockSpec((L, CB), lambda j: (0, j))
    return pl.pallas_call(
        body, grid=(nb,),
        in_specs=[cblk, cblk, pl.BlockSpec((9, CB), lambda j: (0, j)), pl.BlockSpec((1, CB), lambda j: (0, j)), cblk],
        out_specs=[cblk, cblk, pl.BlockSpec((9, CB), lambda j: (0, j)), pl.BlockSpec((1, CB), lambda j: (0, j))],
        out_shape=[S((L, FH), bf16), S((L, FH), bf16), S((9, FH), f32), S((1, FH), f32)],
        scratch_shapes=[pltpu.VMEM((3, L + 2 * GPAD, CB), f32), pltpu.VMEM((L, CB), f32)],
        compiler_params=_cparams("parallel"), name=name)(val, gate, cw, cb_, dact)


CONF_K = 31
CHALF = CONF_K // 2
CPAD = 16


def _shift_copies8(c_ref, base_ref, L):
    n = L + 2 * CPAD - 8
    for b_ in range(8):
        c_ref[b_, pl.ds(0, n), :] = base_ref[pl.ds(b_, n), :]


def _tap_ab(o):
    return o % 8, o - o % 8


def conf_glu_conv_fwd(pa, pg, b1, wdw, bdw, name, hosted=None):
    L = pa.shape[0]
    nb = D // CB

    def body(pa_ref, pg_ref, ba_ref, bg_ref, w_ref, bdw_ref, o_ref, base_ref, c_ref):
        _zero_rows(base_ref, 0, CPAD)
        _zero_rows(base_ref, CPAD + L, CPAD)
        base_ref[pl.ds(CPAD, L), :] = (pa_ref[...] + ba_ref[...]) * _sigmoid(pg_ref[...] + bg_ref[...])
        _shift_copies8(c_ref, base_ref, L)
        bv = bdw_ref[...]

        def tile(i, carry):
            r = i * RT
            acc = jnp.broadcast_to(bv, (RT, CB))
            for k in range(CONF_K):
                b_, a8 = _tap_ab(k - CHALF)
                acc = acc + c_ref[b_, _rows(r, CPAD + a8), :] * w_ref[pl.ds(k, 1), :]
            o_ref[_rows(r), :] = acc
            return carry

        lax.fori_loop(0, L // RT, tile, 0, unroll=2)

    cblk = pl.BlockSpec((L, CB), lambda j: (0, j))
    rblk = pl.BlockSpec((1, CB), lambda j: (0, j))
    rgblk = pl.BlockSpec((1, CB), lambda j: (0, nb + j))
    (out,), extra = _host_call(
        body, (nb,), [cblk, cblk, rblk, rgblk, pl.BlockSpec((CONF_K, CB), lambda j: (0, j)), rblk],
        [cblk], [S((L, D), f32)], [pltpu.VMEM((L + 2 * CPAD, CB), f32), pltpu.VMEM((8, L + 2 * CPAD, CB), f32)],
        ("parallel",), name, (pa, pg, b1, b1, wdw, bdw), hosted)
    return out, extra


def conf_glu_conv_bwd(pa, pg, b1, wdw, dy, name):
    L = pa.shape[0]
    nb = D // CB

    def body(pa_ref, pg_ref, ba_ref, bg_ref, w_ref, dy_ref, dpa_ref, dpg_ref, dba_ref, dbg_ref, dw_ref, dbdw_ref,
             base_ref, c_ref, acc_ref):
        _zero_rows(base_ref, 0, CPAD)
        _zero_rows(base_ref, CPAD + L, CPAD)
        base_ref[pl.ds(CPAD, L), :] = (pa_ref[...] + ba_ref[...]) * _sigmoid(pg_ref[...] + bg_ref[...])
        _shift_copies8(c_ref, base_ref, L)
        acc_ref[...] = jnp.zeros_like(acc_ref)

        def tile1(i, carry):
            r = i * RTB
            dyt = dy_ref[_rowsb(r), :]
            for k in range(CONF_K):
                b_, a8 = _tap_ab(k - CHALF)
                acc_ref[k] += _fold8(dyt * c_ref[b_, _rowsb(r, CPAD + a8), :])
            return carry + _fold8(dyt)

        db8 = lax.fori_loop(0, L // RTB, tile1, jnp.zeros((8, CB), f32), unroll=2)
        dbdw_ref[...] = jnp.sum(db8, axis=0, keepdims=True)
        for k in range(CONF_K):
            dw_ref[pl.ds(k, 1), :] = jnp.sum(acc_ref[k], axis=0, keepdims=True)
        base_ref[pl.ds(CPAD, L), :] = dy_ref[...]
        _shift_copies8(c_ref, base_ref, L)
        ba = ba_ref[...]
        bg = bg_ref[...]

        def tile2(i, carry):
            r = i * RTB
            dglu = jnp.zeros((RTB, CB), f32)
            for k in range(CONF_K):
                b_, a8 = _tap_ab(CHALF - k)
                dglu = dglu + c_ref[b_, _rowsb(r, CPAD + a8), :] * w_ref[pl.ds(k, 1), :]
            a = pa_ref[_rowsb(r), :] + ba
            sg = _sigmoid(pg_ref[_rowsb(r), :] + bg)
            dpa = dglu * sg
            dpg = dglu * a * (sg * (1.0 - sg))
            dpa_ref[_rowsb(r), :] = dpa.astype(bf16)
            dpg_ref[_rowsb(r), :] = dpg.astype(bf16)
            return carry[0] + _fold8(dpa), carry[1] + _fold8(dpg)

        s8 = lax.fori_loop(0, L // RTB, tile2, (jnp.zeros((8, CB), f32), jnp.zeros((8, CB), f32)), unroll=2)
        dba_ref[...] = jnp.sum(s8[0], axis=0, keepdims=True)
        dbg_ref[...] = jnp.sum(s8[1], axis=0, keepdims=True)

    cblk = pl.BlockSpec((L, CB), lambda j: (0, j))
    rblk = pl.BlockSpec((1, CB), lambda j: (0, j))
    rgblk = pl.BlockSpec((1, CB), lambda j: (0, nb + j))
    wblk = pl.BlockSpec((CONF_K, CB), lambda j: (0, j))
    return pl.pallas_call(
        body, grid=(nb,), in_specs=[cblk, cblk, rblk, rgblk, wblk, cblk],
        out_specs=[cblk, cblk, rblk, rblk, wblk, rblk],
        out_shape=[S((L, D), bf16), S((L, D), bf16), S((1, D), f32), S((1, D), f32), S((CONF_K, D), f32), S((1, D), f32)],
        scratch_shapes=[pltpu.VMEM((L + 2 * CPAD, CB), f32), pltpu.VMEM((8, L + 2 * CPAD, CB), f32),
                        pltpu.VMEM((CONF_K, 8, CB), f32)],
        compiler_params=_cparams("parallel"), name=name)(pa, pg, b1, b1, wdw, dy)


def _ln_silu_f(x, w, b):
    mu = jnp.mean(x, axis=-1, keepdims=True)
    d = x - mu
    y = d * lax.rsqrt(jnp.mean(d * d, axis=-1, keepdims=True) + EPS) * w + b
    return y * _sigmoid(y)


def ln_silu_fwd(x, w, b, name):
    T = x.shape[0]

    def body(x_ref, w_ref, b_ref, o_ref):
        o_ref[...] = _ln_silu_f(x_ref[...], w_ref[...], b_ref[...]).astype(bf16)

    blk = pl.BlockSpec((TB, D), lambda i: (i, 0))
    row = pl.BlockSpec((1, D), lambda i: (0, 0))
    return pl.pallas_call(body, grid=(T // TB,), in_specs=[blk, row, row], out_specs=blk, out_shape=S((T, D), bf16),
                          compiler_params=_cparams("parallel"), name=name)(x, w, b)


def ln_silu_bwd(x, w, b, ds, name):
    T = x.shape[0]

    def body(x_ref, w_ref, b_ref, ds_ref, dx_ref, dw_ref, db_ref):
        i = pl.program_id(0)
        _, vjp = jax.vjp(_ln_silu_f, x_ref[...], w_ref[...], b_ref[...])
        dx, dw, db = vjp(ds_ref[...].astype(f32))
        dx_ref[...] = dx

        @pl.when(i == 0)
        def _():
            dw_ref[...] = jnp.zeros_like(dw_ref)
            db_ref[...] = jnp.zeros_like(db_ref)

        dw_ref[...] += dw
        db_ref[...] += db

    blk = pl.BlockSpec((TB, D), lambda i: (i, 0))
    row = pl.BlockSpec((1, D), lambda i: (0, 0))
    return pl.pallas_call(body, grid=(T // TB,), in_specs=[blk, row, row, blk], out_specs=[blk, row, row],
                          out_shape=[S((T, D), f32), S((1, D), f32), S((1, D), f32)],
                          compiler_params=_cparams("arbitrary"), name=name)(x, w, b, ds)


def _mxu(a, b, dims):
    return lax.dot_general(a.astype(bf16), b.astype(bf16), (dims, ((), ())), preferred_element_type=f32)


def _nn(a, b):
    return _mxu(a, b, ((1,), (0,)))


def _nt(a, b):
    return _mxu(a, b, ((1,), (1,)))


def _tn(a, b):
    return _mxu(a, b, ((0,), (0,)))


@jax.custom_vjp
def _dot_nn(a, b):
    return _nn(a, b)


@jax.custom_vjp
def _dot_nt(a, b):
    return _nt(a, b)


@jax.custom_vjp
def _dot_tn(a, b):
    return _tn(a, b)


_dot_nn.defvjp(lambda a, b: (_nn(a, b), (a, b)), lambda res, g: (_nt(g, res[1]), _tn(res[0], g)))
_dot_nt.defvjp(lambda a, b: (_nt(a, b), (a, b)), lambda res, g: (_nn(g, res[1]), _tn(g, res[0])))
_dot_tn.defvjp(lambda a, b: (_tn(a, b), (a, b)), lambda res, g: (_nt(res[1], g), _nn(res[0], g)))


def _exact_dot(a, b, dims, split_first):
    v = a if split_first else b
    p1 = v.astype(bf16)
    r1 = v - p1.astype(f32)
    p2 = r1.astype(bf16)
    p3 = (r1 - p2.astype(f32)).astype(bf16)
    out = None
    for p in (p1, p2, p3):
        lhs, rhs = (p, b.astype(bf16)) if split_first else (a.astype(bf16), p)
        t = lax.dot_general(lhs, rhs, (dims, ((), ())), preferred_element_type=f32)
        out = t if out is None else out + t
    return out


@jax.custom_vjp
def _masked_sum_cols(mf, a):
    return _exact_dot(mf, a, ((1,), (0,)), False)


@jax.custom_vjp
def _masked_sum_rows(mf, a):
    return _exact_dot(a, mf, ((1,), (1,)), True)


_masked_sum_cols.defvjp(lambda mf, a: (_exact_dot(mf, a, ((1,), (0,)), False), mf),
                        lambda mf, g: (jnp.zeros_like(mf), _exact_dot(mf, g, ((0,), (0,)), False)))
_masked_sum_rows.defvjp(lambda mf, a: (_exact_dot(a, mf, ((1,), (1,)), True), mf),
                        lambda mf, g: (jnp.zeros_like(mf), _exact_dot(g, mf, ((1,), (0,)), True)))


def _masked_sum(mf, a, rows):
    return _masked_sum_rows(mf, a) if rows else _masked_sum_cols(mf, a)


def _lanes_to_rows(v):
    r = lax.broadcasted_iota(jnp.int32, (GW, GW), 0)
    c = lax.broadcasted_iota(jnp.int32, (GW, GW), 1)
    return jnp.sum(jnp.where(r == c, jnp.broadcast_to(v, (GW, GW)), 0.0), axis=1, keepdims=True)


def _ssd_chunk(x, B, C, dtc, dtr, bc, br, alc, alr, s_in, is_fwd):
    row = lax.broadcasted_iota(jnp.int32, (Q, Q), 0)
    col = lax.broadcasted_iota(jnp.int32, (Q, Q), 1)
    sgn = jnp.where(is_fwd, 1, -1).astype(jnp.int32)
    mask = (row - col) * sgn >= 0
    mf = mask.astype(f32)
    lane_head = lax.broadcasted_iota(jnp.int32, (1, GW), 1) // P

    def spread(v):
        out = jnp.zeros((v.shape[0], GW), f32)
        for r in range(HPG):
            out = jnp.where(lane_head == r, v[:, r:r + 1], out)
        return out

    dt_c = _softplus(dtc + bc)
    dt_r = _softplus(dtr + br)
    a_c = dt_c * (-jnp.exp(alc))
    a_r = dt_r * (-jnp.exp(alr))
    acum_c = _masked_sum(mf, a_c, False)
    acum_r = _masked_sum(mf, a_r, True)
    tot_c = jnp.sum(a_c, axis=0, keepdims=True)
    dt_e = spread(dt_c)
    acum_e = spread(acum_c)
    tot_e = spread(tot_c)
    xdt = x * dt_e
    cb = _dot_nt(C, B)
    scores, xs = [], []
    for r in range(HPG):
        seg = acum_c[:, r:r + 1] - acum_r[r:r + 1, :]
        scores.append(cb * jnp.exp(jnp.where(mask, seg, -jnp.inf)))
        xs.append(jnp.where(lane_head == r, xdt, 0.0))
    y = _dot_nn(jnp.concatenate(scores, axis=1), jnp.concatenate(xs, axis=0))
    y = y + _dot_nt(C, s_in) * jnp.exp(acum_e)
    xe = xdt * jnp.exp(tot_e - acum_e)
    s_out = _lanes_to_rows(jnp.exp(tot_e)) * s_in + _dot_tn(xe, B)
    return y, s_out


def _chunk_index(d, t, nctx, nc):
    bwd = jnp.where(t < nctx, nctx - 1 - t, nc - 1 - (t - nctx))
    return jnp.where(d == 0, t, bwd)


def _ssd_in_specs(ci):
    small_c = pl.BlockSpec((None, G, 1, HPG), lambda d, t: (d, 0, 0, 0))
    small_r = pl.BlockSpec((None, G, HPG, 1), lambda d, t: (d, 0, 0, 0))
    return [
        pl.BlockSpec((Q, CONVD), lambda d, t: (ci(d, t), 0)),
        pl.BlockSpec((None, G, Q, HPG), lambda d, t: (d, 0, ci(d, t), 0)),
        pl.BlockSpec((None, G, HPG, Q), lambda d, t: (d, 0, 0, ci(d, t))),
        small_c, small_r, small_c, small_r,
    ]


def _group_cols(g):
    return pl.ds(g * GW, GW), pl.ds(DI + g * N, N), pl.ds(DI + G * N + g * N, N)


def ssd_scan_fwd(xbc, dtc, dtr, bc, br, alc, alr, nctx, name, hosted=None):
    T = xbc.shape[0]
    nc = T // Q

    def body(xbc_ref, dtc_ref, dtr_ref, bc_ref, br_ref, alc_ref, alr_ref, y_ref, sin_ref, st_ref):
        d = pl.program_id(0)
        t = pl.program_id(1)

        @pl.when(t == 0)
        def _():
            st_ref[...] = jnp.zeros_like(st_ref)

        for g in range(G):
            xs, bs, cs = _group_cols(g)
            s_in = st_ref[g]
            sin_ref[g] = s_in
            y, s_out = _ssd_chunk(xbc_ref[:, xs], xbc_ref[:, bs], xbc_ref[:, cs], dtc_ref[g], dtr_ref[g], bc_ref[g], br_ref[g],
                                  alc_ref[g], alr_ref[g], s_in, d == 0)
            y_ref[:, xs] = y
            st_ref[g] = s_out

    ci = lambda d, t: _chunk_index(d, t, nctx, nc)
    out_specs = [
        pl.BlockSpec((None, Q, DI), lambda d, t: (d, ci(d, t), 0)),
        pl.BlockSpec((None, None, G, GW, N), lambda d, t: (d, ci(d, t), 0, 0, 0)),
    ]
    return _host_call(
        body, (2, nc), _ssd_in_specs(ci), out_specs, [S((2, T, DI), f32), S((2, nc, G, GW, N), f32)],
        [pltpu.VMEM((G, GW, N), f32)], ("arbitrary", "arbitrary"), name, (xbc, dtc, dtr, bc, br, alc, alr), hosted)


def ssd_scan_bwd(xbc, dtc, dtr, bc, br, alc, alr, s_in_all, dy, nctx, name, hosted=None):
    T = xbc.shape[0]
    nc = T // Q

    def body(xbc_ref, dtc_ref, dtr_ref, bc_ref, br_ref, alc_ref, alr_ref, sin_ref, dy_ref,
             dxbc_ref, ddtc_ref, ddtr_ref, dbc_ref, dbr_ref, dalc_ref, dalr_ref, ds_ref):
        d = pl.program_id(0)
        t = pl.program_id(1)

        @pl.when(t == 0)
        def _():
            ds_ref[...] = jnp.zeros_like(ds_ref)
            dbc_ref[...] = jnp.zeros_like(dbc_ref)
            dbr_ref[...] = jnp.zeros_like(dbr_ref)
            dalc_ref[...] = jnp.zeros_like(dalc_ref)
            dalr_ref[...] = jnp.zeros_like(dalr_ref)

        f = functools.partial(_ssd_chunk, is_fwd=(d == 0))
        for g in range(G):
            xs, bs, cs = _group_cols(g)
            _, vjp = jax.vjp(f, xbc_ref[:, xs], xbc_ref[:, bs], xbc_ref[:, cs], dtc_ref[g], dtr_ref[g], bc_ref[g], br_ref[g],
                             alc_ref[g], alr_ref[g], sin_ref[g])
            dx, dB, dC, ddtc, ddtr, dbc, dbr, dalc, dalr, ds = vjp((dy_ref[:, xs], ds_ref[g]))
            dxbc_ref[:, xs] = dx
            dxbc_ref[:, bs] = dB
            dxbc_ref[:, cs] = dC
            ddtc_ref[g] = ddtc
            ddtr_ref[g] = ddtr
            dbc_ref[g] += dbc
            dbr_ref[g] += dbr
            dalc_ref[g] += dalc
            dalr_ref[g] += dalr
            ds_ref[g] = ds

    ci = lambda d, t: _chunk_index(d, nc - 1 - t, nctx, nc)
    in_specs = _ssd_in_specs(ci) + [
        pl.BlockSpec((None, None, G, GW, N), lambda d, t: (d, ci(d, t), 0, 0, 0)),
        pl.BlockSpec((Q, DI), lambda d, t: (ci(d, t), 0)),
    ]
    small_c = pl.BlockSpec((None, G, 1, HPG), lambda d, t: (d, 0, 0, 0))
    small_r = pl.BlockSpec((None, G, HPG, 1), lambda d, t: (d, 0, 0, 0))
    out_specs = [
        pl.BlockSpec((None, Q, CONVD), lambda d, t: (d, ci(d, t), 0)),
        pl.BlockSpec((None, G, Q, HPG), lambda d, t: (d, 0, ci(d, t), 0)),
        pl.BlockSpec((None, G, HPG, Q), lambda d, t: (d, 0, 0, ci(d, t))),
        small_c, small_r, small_c, small_r,
    ]
    out_shape = [S((2, T, CONVD), f32), S((2, G, T, HPG), f32), S((2, G, HPG, T), f32),
                 S((2, G, 1, HPG), f32), S((2, G, HPG, 1), f32), S((2, G, 1, HPG), f32), S((2, G, HPG, 1), f32)]
    return _host_call(body, (2, nc), in_specs, out_specs, out_shape, [pltpu.VMEM((G, GW, N), f32)],
                      ("arbitrary", "arbitrary"), name, (xbc, dtc, dtr, bc, br, alc, alr, s_in_all, dy), hosted)


GTB = 128


def _gate_norm_f(yf, yb, x, z, dexp, w):
    y = (yf + yb + dexp * x) * (z * _sigmoid(z))
    return y * lax.rsqrt(jnp.mean(y * y, axis=-1, keepdims=True) + EPS) * w


def ssd_gate_fwd(y2, xbc, proj, dexp, w, nctxb, name):
    T = xbc.shape[0]
    L = T - nctxb * GTB

    def body(yf_ref, yb_ref, x_ref, z_ref, d_ref, w_ref, o_ref):
        o_ref[...] = _gate_norm_f(yf_ref[...], yb_ref[...], x_ref[...], z_ref[...], d_ref[...], w_ref[...]).astype(bf16)

    wide = pl.BlockSpec((GTB, DI), lambda i: (i + nctxb, 0))
    row = pl.BlockSpec((1, DI), lambda i: (0, 0))
    return pl.pallas_call(
        body, grid=(L // GTB,),
        in_specs=[pl.BlockSpec((None, GTB, DI), lambda i: (0, i + nctxb, 0)),
                  pl.BlockSpec((None, GTB, DI), lambda i: (1, i + nctxb, 0)), wide, wide, row, row],
        out_specs=pl.BlockSpec((GTB, DI), lambda i: (i, 0)), out_shape=S((L, DI), bf16),
        compiler_params=_cparams("parallel"), name=name)(y2, y2, xbc, proj, dexp, w)


def ssd_gate_bwd(y2, xbc, proj, dexp, w, dyn, nctxb, name, hosted=None):
    T = xbc.shape[0]
    nb = T // GTB

    def body(yf_ref, yb_ref, x_ref, z_ref, d_ref, w_ref, dyn_ref, dy_ref, dx_ref, dz_ref, dd_ref, dw_ref):
        i = pl.program_id(0)

        @pl.when(i == 0)
        def _():
            dd_ref[...] = jnp.zeros_like(dd_ref)
            dw_ref[...] = jnp.zeros_like(dw_ref)

        @pl.when(i < nctxb)
        def _():
            dy_ref[...] = jnp.zeros_like(dy_ref)
            dx_ref[...] = jnp.zeros_like(dx_ref)
            dz_ref[...] = jnp.zeros_like(dz_ref)

        @pl.when(i >= nctxb)
        def _():
            _, vjp = jax.vjp(_gate_norm_f, yf_ref[...], yb_ref[...], x_ref[...], z_ref[...], d_ref[...], w_ref[...])
            dyf, _, dx, dz, dd, dw = vjp(dyn_ref[...].astype(f32))
            dy_ref[...] = dyf
            dx_ref[...] = dx
            dz_ref[...] = dz.astype(bf16)
            fold = (lax.broadcasted_iota(jnp.int32, (DI, 128), 0) // P == lax.broadcasted_iota(jnp.int32, (DI, 128), 1))
            dd_ref[...] += jnp.dot(dd, fold.astype(f32), precision=HI, preferred_element_type=f32)
            dw_ref[...] += dw

    wide = pl.BlockSpec((GTB, DI), lambda i: (i, 0))
    row = pl.BlockSpec((1, DI), lambda i: (0, 0))
    hrow = pl.BlockSpec((1, 128), lambda i: (0, 0))
    return _host_call(
        body, (nb,),
        [pl.BlockSpec((None, GTB, DI), lambda i: (0, i, 0)), pl.BlockSpec((None, GTB, DI), lambda i: (1, i, 0)),
         wide, wide, row, row, pl.BlockSpec((GTB, DI), lambda i: (jnp.maximum(i - nctxb, 0), 0))],
        [wide, wide, wide, hrow, row],
        [S((T, DI), f32), S((T, DI), f32), S((T, DI), bf16), S((1, 128), f32), S((1, DI), f32)],
        [], ("arbitrary",), name, (y2, y2, xbc, proj, dexp, w, dyn), hosted)


CROWS = 2 * N_DEV


def mod_fwd(c16, modw, name):
    nl, _, cols = modw.shape

    def body(c_ref, w_ref, o_ref):
        cv = c_ref[...]
        s = cv * _sigmoid(cv)
        for l in range(nl):
            o_ref[l] = jnp.dot(s, w_ref[l], precision=HI, preferred_element_type=f32)

    return pl.pallas_call(body, in_specs=[VMEM, VMEM], out_specs=VMEM, out_shape=S((nl, CROWS, cols), f32),
                          compiler_params=pltpu.CompilerParams(vmem_limit_bytes=VMEM_LIMIT_BYTES), name=name)(c16, modw)


def mod_bwd(c16, modw, dm_sh, dm_all, name):
    nl, _, cols = modw.shape

    def body(c_ref, w_ref, dm_ref, dmall_ref, dw_ref, dc_ref, db_ref):
        cv = c_ref[...]
        sg = _sigmoid(cv)
        s = cv * sg
        ds_dc = sg * (1.0 + cv * (1.0 - sg))
        is_ctx = lax.broadcasted_iota(jnp.int32, (CROWS, D), 0) >= N_DEV
        dc = jnp.zeros((1, D), f32)
        for l in range(nl):
            dm = dm_ref[l]
            dw_ref[l] = lax.dot_general(s, dm, (((0,), (0,)), ((), ())), precision=HI, preferred_element_type=f32)
            dsv = lax.dot_general(dm, w_ref[l], (((1,), (1,)), ((), ())), precision=HI, preferred_element_type=f32)
            dc = dc + jnp.sum(jnp.where(is_ctx, dsv * ds_dc, 0.0), axis=0, keepdims=True)
            db_ref[pl.ds(l, 1), :] = jnp.sum(dmall_ref[l], axis=0, keepdims=True)
        dc_ref[...] = dc

    return pl.pallas_call(
        body, in_specs=[VMEM, VMEM, VMEM, VMEM], out_specs=[VMEM, VMEM, VMEM],
        out_shape=[S(modw.shape, f32), S((1, D), f32), S((nl, 6 * D), f32)],
        compiler_params=pltpu.CompilerParams(vmem_limit_bytes=VMEM_LIMIT_BYTES), name=name)(c16, modw, dm_sh, dm_all)


def adamw(w, g, m, v, name):
    R, C = w.shape
    rb = R if R <= 512 else max(r_ for r_ in range(8, 513, 8) if R % r_ == 0)
    bc1 = 1.0 - ADAM_B1 ** ADAM_STEP
    bc2 = 1.0 - ADAM_B2 ** ADAM_STEP

    def body(w_ref, g_ref, m_ref, v_ref, d_ref, nm_ref, nv_ref):
        gv = g_ref[...]
        m_new = ADAM_B1 * m_ref[...] + (1.0 - ADAM_B1) * gv
        v_new = ADAM_B2 * v_ref[...] + (1.0 - ADAM_B2) * (gv * gv)
        m_hat = m_new / bc1
        v_hat = v_new / bc2
        d_ref[...] = -ADAM_LR * (m_hat / (jnp.sqrt(v_hat) + ADAM_EPS) + ADAM_WD * w_ref[...])
        nm_ref[...] = m_new
        nv_ref[...] = v_new

    blk = pl.BlockSpec((rb, C), lambda i: (i, 0))
    return pl.pallas_call(body, grid=(R // rb,), in_specs=[blk] * 4, out_specs=[blk] * 3,
                          out_shape=[S((R, C), f32)] * 3, compiler_params=_cparams("parallel"), name=name)(w, g, m, v)


def _me():
    return lax.axis_index("x"), lax.axis_index("y"), lax.axis_index("c")


def allgather_small(x, name, with_sum=False):
    r, w = x.shape

    def body(x_ref, *refs):
        if with_sum:
            out_ref, sum_ref, send_sems, recv_sems = refs
        else:
            out_ref, send_sems, recv_sems = refs
        mx, my, mc = _me()
        me = 4 * mx + 2 * my + mc
        out_ref[me] = x_ref[...]
        peers = []
        for k in range(1, N_DEV):
            kx, ky, kc = (k >> 2) & 1, (k >> 1) & 1, k & 1
            peers.append((mx + kx - 2 * mx * kx, my + ky - 2 * my * ky, mc + kc - 2 * mc * kc))
        copies = []
        for k, peer in enumerate(peers):
            cp = pltpu.make_async_remote_copy(src_ref=x_ref, dst_ref=out_ref.at[me], send_sem=send_sems.at[k],
                                              recv_sem=recv_sems.at[k], device_id=peer, device_id_type=MESH)
            cp.start()
            copies.append(cp)
        for k, (px, py, pc) in enumerate(peers):
            pltpu.make_async_remote_copy(src_ref=x_ref, dst_ref=out_ref.at[4 * px + 2 * py + pc], send_sem=send_sems.at[k],
                                         recv_sem=recv_sems.at[k], device_id=(px, py, pc), device_id_type=MESH).wait_recv()
        for cp in copies:
            cp.wait_send()
        if with_sum:
            acc = out_ref[0]
            for j in range(1, N_DEV):
                acc = acc + out_ref[j]
            sum_ref[...] = acc

    out_shape = [S((N_DEV, r, w), f32)] + ([S((r, w), f32)] if with_sum else [])
    outs = pl.pallas_call(
        body, in_specs=[VMEM], out_specs=[VMEM] * len(out_shape), out_shape=out_shape,
        scratch_shapes=[pltpu.SemaphoreType.DMA((N_DEV - 1,)), pltpu.SemaphoreType.DMA((N_DEV - 1,))],
        compiler_params=pltpu.CompilerParams(vmem_limit_bytes=VMEM_LIMIT_BYTES), name=name)(x)
    return outs if with_sum else outs[0]


def _tile2d(R, W, max_rows):
    if R <= max_rows:
        return R, W
    fits = [r_ for r_ in range(16, max_rows + 1, 16) if R % r_ == 0]
    return (max(fits), W) if fits else (R, 256)


def add_own(g, r, core, name):
    _, _, R, W = g.shape
    rb, wb = _tile2d(R, W, 512)

    def body(core_ref, a_ref, b_ref, o_ref):
        o_ref[...] = (a_ref[...].astype(f32) + b_ref[...].astype(f32)).astype(bf16)

    blk = pl.BlockSpec((None, rb, wb), lambda k, i, j, core_ref: (k, i, j))
    gs = pltpu.PrefetchScalarGridSpec(
        num_scalar_prefetch=1, grid=(4, R // rb, W // wb),
        in_specs=[pl.BlockSpec((None, None, rb, wb), lambda k, i, j, core_ref: (k, core_ref[0], i, j)), blk], out_specs=blk)
    return pl.pallas_call(body, grid_spec=gs, out_shape=S((4, R, W), bf16),
                          compiler_params=_cparams("parallel", "parallel", "parallel"), name=name)(core, g, r)


def sum_adamw(recv, w, m, v, layer, name, into=None):
    _, R, W = recv.shape
    rb, wb = _tile2d(R, W, 256)
    bc1 = 1.0 - ADAM_B1 ** ADAM_STEP
    bc2 = 1.0 - ADAM_B2 ** ADAM_STEP
    n_into = 0 if into is None else 4

    def body(r_ref, w_ref, m_ref, v_ref, *refs):
        g_ref, d_ref, nm_ref, nv_ref = refs[n_into:]
        gv = r_ref[0].astype(f32)
        for k in range(1, 4):
            gv = gv + r_ref[k].astype(f32)
        m_new = ADAM_B1 * m_ref[...] + (1.0 - ADAM_B1) * gv
        v_new = ADAM_B2 * v_ref[...] + (1.0 - ADAM_B2) * (gv * gv)
        g_ref[...] = gv
        d_ref[...] = -ADAM_LR * ((m_new / bc1) / (jnp.sqrt(v_new / bc2) + ADAM_EPS) + ADAM_WD * w_ref[...])
        nm_ref[...] = m_new
        nv_ref[...] = v_new

    if layer is None:
        wblk = pl.BlockSpec((rb, wb), lambda i, j: (i, j))
        oshape = S((R, W), f32)
    else:
        wblk = pl.BlockSpec((None, rb, wb), lambda i, j: (layer, i, j))
        oshape = S(w.shape, f32)
    return pl.pallas_call(
        body, grid=(R // rb, W // wb),
        in_specs=[pl.BlockSpec((4, rb, wb), lambda i, j: (0, i, j)), wblk, wblk, wblk] + [ANY] * n_into,
        out_specs=[wblk] * 4, out_shape=[oshape] * 4, input_output_aliases={4 + k: k for k in range(n_into)},
        compiler_params=_cparams("parallel", "parallel"), name=name)(recv, w, m, v, *(into or ()))


def sum_rows(a, name):
    K, R, W = a.shape
    rb = _pick(R, (512, 256, 128, 64, 32, 16))

    def body(a_ref, o_ref):
        acc = a_ref[0].astype(f32)
        for k in range(1, K):
            acc = acc + a_ref[k].astype(f32)
        o_ref[...] = acc

    return pl.pallas_call(body, grid=(R // rb,), in_specs=[pl.BlockSpec((K, rb, W), lambda i: (0, i, 0))],
                          out_specs=pl.BlockSpec((rb, W), lambda i: (i, 0)), out_shape=S((R, W), f32),
                          compiler_params=_cparams("parallel"), name=name)(a)


DMA = pltpu.SemaphoreType.DMA


class GatherExchange:
    def __init__(self, arrays):
        self.arrays = list(arrays)
        self.na = len(self.arrays)
        self.out_shape = [S((N_DEV,) + a.shape, a.dtype) for a in self.arrays]
        self.scratch = [DMA((7 * self.na,)), DMA((7 * self.na,)), DMA((self.na,))]

    def ops(self, x_refs, out_refs, sems):
        send_sems, recv_sems, local_sems = sems
        na = self.na
        x, y, c = _me()
        me, sibling = (x, y, c), (x, y, 1 - c)
        chips = [(1 - x, y), (x, 1 - y), (1 - x, 1 - y)]

        def rows(a, px, py, pc):
            return out_refs[a].at[4 * px + 2 * py + pc]

        def copy(a, k, block, to, src=None):
            return pltpu.make_async_remote_copy(
                src_ref=rows(a, *block) if src is None else src, dst_ref=rows(a, *block),
                send_sem=send_sems.at[7 * a + k], recv_sem=recv_sems.at[7 * a + k], device_id=to, device_id_type=MESH)

        def local(a):
            return pltpu.make_async_copy(x_refs[a], rows(a, *me), local_sems.at[a])

        def first(a):
            return [copy(a, 0, me, sibling, src=x_refs[a])] + [copy(a, 1 + j, me, (*chip, c), src=x_refs[a])
                                                                for j, chip in enumerate(chips)]

        def start():
            for a in range(na):
                local(a).start()
                for cp in first(a):
                    cp.start()

        def mid():
            for a in range(na):
                for j, chip in enumerate(chips):
                    copy(a, 1 + j, (*chip, c), me).wait_recv()
                    copy(a, 4 + j, (*chip, c), sibling).start()

        def finish():
            for a in range(na):
                copy(a, 0, sibling, me).wait_recv()
                for j, chip in enumerate(chips):
                    copy(a, 4 + j, (*chip, 1 - c), me).wait_recv()
                for cp in first(a) + [copy(a, 4 + j, (*chip, c), sibling) for j, chip in enumerate(chips)]:
                    cp.wait_send()
                local(a).wait()

        return start, mid, finish


class SiblingExchange:
    def __init__(self, arrays):
        self.arrays = list(arrays)
        self.na = len(self.arrays)
        self.out_shape = [S((4,) + g.shape[2:], g.dtype) for g in self.arrays]
        self.scratch = [DMA((self.na,)), DMA((self.na,))]

    def ops(self, g_refs, out_refs, sems):
        send_sems, recv_sems = sems
        x, y, c = _me()

        def copy(a):
            return pltpu.make_async_remote_copy(src_ref=g_refs[a].at[:, 1 - c], dst_ref=out_refs[a],
                                                send_sem=send_sems.at[a], recv_sem=recv_sems.at[a],
                                                device_id=(x, y, 1 - c), device_id_type=MESH)

        def start():
            for a in range(self.na):
                copy(a).start()

        def finish():
            for a in range(self.na):
                copy(a).wait()

        return start, None, finish


class ChipsExchange:
    def __init__(self, arrays):
        self.arrays = list(arrays)
        self.na = len(self.arrays)
        self.out_shape = [S(p.shape, p.dtype) for p in self.arrays]
        self.scratch = [DMA((3 * self.na,)), DMA((3 * self.na,)), DMA((self.na,))]

    def ops(self, p_refs, out_refs, sems):
        send_sems, recv_sems, local_sems = sems
        x, y, c = _me()
        mine = 2 * x + y
        chips = [(1 - x, y), (x, 1 - y), (1 - x, 1 - y)]

        def local(a):
            return pltpu.make_async_copy(p_refs[a].at[mine], out_refs[a].at[mine], local_sems.at[a])

        def send(a, j):
            px, py = chips[j]
            return pltpu.make_async_remote_copy(src_ref=p_refs[a].at[2 * px + py], dst_ref=out_refs[a].at[mine],
                                                send_sem=send_sems.at[3 * a + j], recv_sem=recv_sems.at[3 * a + j],
                                                device_id=(px, py, c), device_id_type=MESH)

        def recv(a, j):
            px, py = chips[j]
            return pltpu.make_async_remote_copy(src_ref=p_refs[a].at[mine], dst_ref=out_refs[a].at[2 * px + py],
                                                send_sem=send_sems.at[3 * a + j], recv_sem=recv_sems.at[3 * a + j],
                                                device_id=(px, py, c), device_id_type=MESH)

        def start():
            for a in range(self.na):
                local(a).start()
                for j in range(3):
                    send(a, j).start()

        def finish():
            for a in range(self.na):
                for j in range(3):
                    recv(a, j).wait_recv()
                for j in range(3):
                    send(a, j).wait_send()
                local(a).wait()

        return start, None, finish


def exchange(ex, name):
    na = ex.na

    def body(*refs):
        start, mid, finish = ex.ops(refs[:na], refs[na:2 * na], refs[2 * na:])
        start()
        if mid is not None:
            mid()
        finish()

    return pl.pallas_call(body, in_specs=[ANY] * na, out_specs=[ANY] * na, out_shape=ex.out_shape,
                          scratch_shapes=ex.scratch, name=name)(*ex.arrays)


def _host_call(body, grid, in_specs, out_specs, out_shape, scratch_shapes, sem, name, args, hosted):
    if hosted is None:
        res = pl.pallas_call(body, grid=grid, in_specs=in_specs, out_specs=out_specs, out_shape=out_shape,
                             scratch_shapes=scratch_shapes, compiler_params=_cparams(*sem), name=name)(*args)
        return res, None
    n_in, n_out, n_sc, na = len(in_specs), len(out_shape), len(scratch_shapes), hosted.na
    nsteps = 1
    for g_ in grid:
        nsteps *= g_
    mid_step = (3 * nsteps) // 4
    i1 = n_in + na
    i2 = i1 + n_out
    i3 = i2 + na
    i4 = i3 + n_sc

    def wrapped(*refs):
        step = pl.program_id(0)
        for ax in range(1, len(grid)):
            step = step * grid[ax] + pl.program_id(ax)
        start, mid, finish = hosted.ops(refs[n_in:i1], refs[i2:i3], refs[i4:])
        pl.when(step == 0)(start)
        if mid is not None:
            pl.when(step == mid_step)(mid)
        body(*refs[:n_in], *refs[i1:i2], *refs[i3:i4])
        pl.when(step == nsteps - 1)(finish)

    res = pl.pallas_call(
        wrapped, grid=grid, in_specs=list(in_specs) + [ANY] * na, out_specs=list(out_specs) + [ANY] * na,
        out_shape=list(out_shape) + hosted.out_shape, scratch_shapes=list(scratch_shapes) + hosted.scratch,
        compiler_params=_cparams(*(("arbitrary",) * len(grid))), name=name)(*args, *hosted.arrays)
    return res[:n_out], res[n_out:]


PACK_ALIGN = 16 * PACK_W


def _pad_to(v, mult):
    n = v.shape[-1]
    extra = (-n) % mult
    if extra == 0:
        return v
    return jnp.concatenate([v, jnp.zeros(v.shape[:-1] + (extra,), v.dtype)], axis=-1)


def _f32_as_bf16_pairs(v):
    return lax.bitcast_convert_type(v.reshape(-1), bf16).reshape(-1)


def _bf16_pairs_as_f32(v):
    return lax.bitcast_convert_type(v.reshape(v.shape[:-1] + (v.shape[-1] // 2, 2)), f32)


def _col_shards(gw):
    lead = gw.shape[:-1]
    n = gw.shape[-1] // N_DEV
    t = gw.reshape(lead + (N_DEV, n))
    t = jnp.moveaxis(t, -2, 0)
    return t.reshape(N_DEV, -1)


def kernel(x, c, ctx, c_ctx, mod_w, mod_b, norm1_w, norm2_w, ssd_w_in, ssd_conv_w, ssd_conv_b, ssd_dt_bias, ssd_a_log, ssd_d, ssd_norm_w, ssd_w_out, conf_w_pw1, conf_b_pw1, conf_w_dw, conf_b_dw, conf_ln_w, conf_ln_b, conf_w_pw2, conf_b_pw2, ffn_w_up, ffn_conv_w, ffn_conv_b, ffn_w_down, final_norm_w, loss_target, m_c_ctx, m_mod_w, m_mod_b, m_norm1_w, m_norm2_w, m_ssd_w_in, m_ssd_conv_w, m_ssd_conv_b, m_ssd_dt_bias, m_ssd_a_log, m_ssd_d, m_ssd_norm_w, m_ssd_w_out, m_conf_w_pw1, m_conf_b_pw1, m_conf_w_dw, m_conf_b_dw, m_conf_ln_w, m_conf_ln_b, m_conf_w_pw2, m_conf_b_pw2, m_ffn_w_up, m_ffn_conv_w, m_ffn_conv_b, m_ffn_w_down, m_final_norm_w, v_c_ctx, v_mod_w, v_mod_b, v_norm1_w, v_norm2_w, v_ssd_w_in, v_ssd_conv_w, v_ssd_conv_b, v_ssd_dt_bias, v_ssd_a_log, v_ssd_d, v_ssd_norm_w, v_ssd_w_out, v_conf_w_pw1, v_conf_b_pw1, v_conf_w_dw, v_conf_b_dw, v_conf_ln_w, v_conf_ln_b, v_conf_w_pw2, v_conf_b_pw2, v_ffn_w_up, v_ffn_conv_w, v_ffn_conv_b, v_ffn_w_down, v_final_norm_w):
    mx, my, mc = _me()
    me = 4 * mx + 2 * my + mc
    L = x.shape[1]
    LC = ctx.shape[1]
    T = LC + L
    w_in_cols = ssd_w_in.shape[2] * N_DEV
    n_dt = w_in_cols - DI - CONVD

    small = [c[0], ssd_conv_w[0], conf_b_pw1[0], conf_w_dw[0], conf_b_dw[0], conf_ln_w[0], conf_ln_b[0], conf_b_pw2[0],
             ffn_conv_w]
    parts = [_f32_as_bf16_pairs(t) for t in small]
    sizes = [p.shape[0] for p in parts]
    small_flat = _pad_to(jnp.concatenate(parts), PACK_ALIGN).reshape(-1, PACK_W)
    w_in, small_g = exchange(GatherExchange([ssd_w_in[0].astype(bf16), small_flat]), "gather_first")
    gather_in_proj = GatherExchange([ssd_w_out[0].astype(bf16), conf_w_pw2[0].astype(bf16)])
    gather_in_conv = GatherExchange([ffn_w_down[0].astype(bf16), conf_w_pw1[0].astype(bf16)])
    gather_in_scan = GatherExchange([ffn_w_up[0].astype(bf16), ffn_w_up[1].astype(bf16)])
    gather_in_gate = GatherExchange([ffn_w_down[1].astype(bf16)])
    w_up, w_down = [None, None], [None, None]
    small_g = small_g.reshape(N_DEV, -1)
    offs = [0]
    for s_ in sizes:
        offs.append(offs[-1] + s_)
    sm = [_bf16_pairs_as_f32(small_g[:, offs[i]:offs[i + 1]]) for i in range(len(sizes))]

    def cols(pc, K):
        return jnp.moveaxis(pc.reshape(N_DEV, K, -1), 0, 1).reshape(K, -1)

    c_all = sm[0]
    conv_w5 = cols(sm[1], 5)
    b_pw1 = sm[2].reshape(1, 2 * D)
    w_dw = cols(sm[3], CONF_K)
    b_dw, ln_w, ln_b, b_pw2 = (sm[i].reshape(1, D) for i in (4, 5, 6, 7))
    fcw = sm[8].reshape(N_DEV, 2, 9, FH // N_DEV)
    ffn_cw = [cols(fcw[:, i].reshape(N_DEV, -1), 9) for i in range(2)]
    in_segs = (DI, CONVD, n_dt)
    up_segs = (FH, FH)
    pw1_segs = (D, D)

    c16 = jnp.concatenate([c_all, jnp.broadcast_to(c_ctx[None, :], (N_DEV, D))], axis=0)
    m_sh = mod_fwd(c16, mod_w, "mod_fwd")
    mod_cols = mod_w.shape[2]
    m_all = allgather_small(m_sh.reshape(2 * CROWS, mod_cols), "gather_mod")
    m_all = jnp.moveaxis(m_all.reshape(N_DEV, 2, CROWS, mod_cols), 0, 2).reshape(2, CROWS, 6 * D) + mod_b[:, None, :]
    m_lat = lax.dynamic_index_in_dim(m_all, me, axis=1, keepdims=False).reshape(2, 6, 1, D)
    m_ctx = m_all[:, N_DEV].reshape(2, 6, 1, D)
    zero_row = jnp.zeros((1, D), f32)

    def ffn_fwd(h, i, tag, hosted=None):
        a2 = modnorm_fwd(h, norm2_w[i][None], m_lat[i, 4][None], m_lat[i, 3][None], 0, f"ffn{tag}_norm")
        val, gate = smm_fwd(a2, w_up[i], None, up_segs, f"ffn{tag}_up")
        act, extra = ffn_gate_fwd(val, gate, ffn_cw[i], ffn_conv_b[i][None], f"ffn{tag}_gate", hosted)
        o2 = matmul(act, w_down[i], "nn", f32, f"ffn{tag}_down")
        h_new = resgate_fwd(h, o2, m_lat[i, 5], zero_row, f"ffn{tag}_res")
        return h_new, (a2, val, gate, act, o2), extra

    def ffn_bwd(dh, h, i, saved, tag):
        a2, val, gate, act, o2 = saved
        do2, dg2, _ = resgate_bwd(dh, o2, m_lat[i, 5], zero_row, f"ffn{tag}_res_bwd")
        g_down = matmul(act, do2, "tn", bf16, f"ffn{tag}_down_dw")
        dact = matmul(do2, w_down[i], "nt", bf16, f"ffn{tag}_down_dx")
        dval, dgate, dcw, dcb = ffn_gate_bwd(val, gate, ffn_cw[i], ffn_conv_b[i][None], dact, f"ffn{tag}_gate_bwd")
        g_up = smm_dw(a2, [dval, dgate], FH // 4, up_segs, 2, True, f"ffn{tag}_up_dw")
        da2, _ = smm_dx([dval, dgate], w_up[i], None, up_segs, bf16, f"ffn{tag}_up_dx")
        dh_in, dn2, dsc2, dsh2 = modnorm_bwd(h, norm2_w[i][None], m_lat[i, 4][None], m_lat[i, 3][None], da2, dh, 0,
                                             f"ffn{tag}_norm_bwd")
        return dh_in, dict(w_up=g_up, w_down=g_down, conv_w=dcw, conv_b=dcb, norm2=dn2, sh2=dsh2[0], sc2=dsc2[0], g2=dg2)

    nctx = LC // Q
    h0 = jnp.concatenate([ctx[0], x[0]], axis=0)
    sc0 = jnp.stack([m_ctx[0, 1], m_lat[0, 1]])
    sh0 = jnp.stack([m_ctx[0, 0], m_lat[0, 0]])
    a0 = modnorm_fwd(h0, norm1_w[0][None], sc0, sh0, LC // TB, "ssd_norm")
    (z, xbc_pre, dt_raw), (w_out_g, w_pw2_g) = smm_fwd(a0, w_in, None, in_segs, "ssd_in", gather_in_proj)
    w_out = w_out_g.reshape(DI, D)
    w_pw2 = w_pw2_g.reshape(D, D)
    segs = ((0, LC), (LC, L))
    xbc, (w_down0_g, w_pw1) = ssd_conv_fwd(xbc_pre, conv_w5, ssd_conv_b, segs, "ssd_conv", gather_in_conv)
    w_down[0] = w_down0_g.reshape(FH, D)
    dt4 = dt_raw[:, :n_dt].reshape(T, 2, G, HPG)
    dtc = jnp.transpose(dt4, (1, 2, 0, 3))
    dtr = jnp.transpose(dt4, (1, 2, 3, 0))
    bias3 = ssd_dt_bias[0].reshape(2, G, HPG)
    alog3 = ssd_a_log[0].reshape(2, G, HPG)
    bc_, br_ = bias3[:, :, None, :], bias3[:, :, :, None]
    alc, alr = alog3[:, :, None, :], alog3[:, :, :, None]
    (y2, s_in_all), (w_up[0], w_up[1]) = ssd_scan_fwd(xbc, dtc, dtr, bc_, br_, alc, alr, nctx, "ssd_scan", gather_in_scan)
    dexp = jnp.repeat(ssd_d[0], P)[None, :]
    yn = ssd_gate_fwd(y2, xbc, z, dexp, ssd_norm_w, LC // GTB, "ssd_gate")
    o_ssd = matmul(yn, w_out, "nn", f32, "ssd_out")
    hx = x[0]
    h1 = resgate_fwd(hx, o_ssd, m_lat[0, 2], zero_row, "ssd_res")
    h2, ffn0_saved, (w_down1_g,) = ffn_fwd(h1, 0, "0", gather_in_gate)
    w_down[1] = w_down1_g.reshape(FH, D)

    a1 = modnorm_fwd(h2, norm1_w[1][None], m_lat[1, 1][None], m_lat[1, 0][None], 0, "conf_norm")
    pa, pg = smm_fwd(a1, w_pw1, None, pw1_segs, "conf_pw1")
    dwc, _ = conf_glu_conv_fwd(pa, pg, b_pw1, w_dw, b_dw, "conf_conv")
    s1 = ln_silu_fwd(dwc, ln_w, ln_b, "conf_ln")
    o_conf = matmul(s1, w_pw2, "nn", f32, "conf_pw2")
    h3 = resgate_fwd(h2, o_conf, m_lat[1, 2], b_pw2, "conf_res")
    h4, ffn1_saved, _ = ffn_fwd(h3, 1, "1")

    loss_part, dh4, g_final = final_loss(h4, final_norm_w[None], loss_target[0], "loss_head")
    dh3, gf1 = ffn_bwd(dh4, h3, 1, ffn1_saved, "1")

    do_conf, dg1_1, g_b_pw2 = resgate_bwd(dh3, o_conf, m_lat[1, 2], b_pw2, "conf_res_bwd")
    g_pw2 = matmul(s1, do_conf, "tn", bf16, "conf_pw2_dw")
    ds1 = matmul(do_conf, w_pw2, "nt", bf16, "conf_pw2_dx")
    ddwc, g_ln_w, g_ln_b = ln_silu_bwd(dwc, ln_w, ln_b, ds1, "conf_ln_bwd")
    dpa, dpg, dba, dbg, g_w_dw, g_b_dw = conf_glu_conv_bwd(pa, pg, b_pw1, w_dw, ddwc, "conf_conv_bwd")
    g_b_pw1 = jnp.concatenate([dba, dbg], axis=1)
    g_pw1 = smm_dw(a1, [dpa, dpg], 2 * D // N_DEV, pw1_segs, 1, False, "conf_pw1_dw")
    da1, _ = smm_dx([dpa, dpg], w_pw1, None, pw1_segs, bf16, "conf_pw1_dx")
    dh2, g_n1_1, dsc1_1, dsh1_1 = modnorm_bwd(h2, norm1_w[1][None], m_lat[1, 1][None], m_lat[1, 0][None], da1, dh3, 0,
                                              "conf_norm_bwd")
    dh1, gf0 = ffn_bwd(dh2, h1, 0, ffn0_saved, "0")

    do_ssd, dg1_0, _ = resgate_bwd(dh1, o_ssd, m_lat[0, 2], zero_row, "ssd_res_bwd")
    g_w_out = matmul(yn, do_ssd, "tn", bf16, "ssd_out_dw")
    dyn = matmul(do_ssd, w_out, "nt", bf16, "ssd_out_dx")
    core = mc.reshape(1).astype(jnp.int32)

    def by_device(t):
        return t.reshape((4, 2, -1, t.shape[-1]))

    early = [by_device(t) for t in (gf1["w_up"], gf1["w_down"], g_pw2, g_pw1, gf0["w_up"], gf0["w_down"], g_w_out)]
    (dy, dx_skip, dz, g_dexp, g_ssd_norm), early_sib = ssd_gate_bwd(
        y2, xbc, z, dexp, ssd_norm_w, dyn, LC // GTB, "ssd_gate_bwd", SiblingExchange(early))
    early_part = [add_own(t, r_, core, f"reduce_add{i}") for i, (t, r_) in enumerate(zip(early, early_sib))]
    (dxbc2, ddtc, ddtr, dbc, dbr, dalc, dalr), early_red = ssd_scan_bwd(
        xbc, dtc, dtr, bc_, br_, alc, alr, s_in_all, dy, nctx, "ssd_scan_bwd", ChipsExchange(early_part))
    ddt = (jnp.transpose(ddtc, (2, 0, 1, 3)) + jnp.transpose(ddtr, (3, 0, 1, 2))).reshape(T, n_dt)
    g_dt_bias = (dbc[:, :, 0, :] + dbr[:, :, :, 0]).reshape(2, NH_SSD)
    g_a_log = (dalc[:, :, 0, :] + dalr[:, :, :, 0]).reshape(2, NH_SSD)
    g_ssd_d = g_dexp[0, :NH_SSD]
    du, g_conv_w5, g_conv_b5 = ssd_conv_bwd(xbc_pre, conv_w5, ssd_conv_b, dxbc2, dx_skip, segs, "ssd_conv_bwd")
    ddt_p = _pad_to(ddt, 128).astype(bf16)
    g_w_in = smm_dw(a0, [dz, du, ddt_p], w_in.shape[-1], in_segs, 2, True, "ssd_in_dw")
    g_ffn_cw = jnp.stack([gf0["conv_w"], gf1["conv_w"]])
    small_shards = [_col_shards(t) for t in (g_conv_w5, g_b_pw1, g_w_dw, g_b_dw, g_ln_w, g_ln_b, g_b_pw2, g_ffn_cw)]
    gsizes = [s_.shape[1] for s_ in small_shards]
    g_small = _pad_to(jnp.concatenate(small_shards, axis=1), PACK_ALIGN).astype(bf16)
    late = [by_device(g_w_in), by_device(g_small.reshape(N_DEV, -1, PACK_W))]
    da0, late_sib = smm_dx([dz, du, ddt_p], w_in, None, in_segs, f32, "ssd_in_dx", SiblingExchange(late))
    late_part = [add_own(t, r_, core, f"reduce_add_late{i}") for i, (t, r_) in enumerate(zip(late, late_sib))]
    late_red = exchange(ChipsExchange(late_part), "reduce_chips_late")
    dh0, g_n1_0, dsc1_0, dsh1_0 = modnorm_bwd(h0, norm1_w[0][None], sc0, sh0, da0, dh1, LC // TB, "ssd_norm_bwd")
    grad_x = dh0[None]

    zeros_d = jnp.zeros((1, D), f32)
    dm_lat = jnp.stack([
        jnp.concatenate([dsh1_0[1], dsc1_0[1], dg1_0, gf0["sh2"], gf0["sc2"], gf0["g2"]], axis=1),
        jnp.concatenate([dsh1_1[0], dsc1_1[0], dg1_1, gf1["sh2"], gf1["sc2"], gf1["g2"]], axis=1)])
    dm_ctx = jnp.stack([
        jnp.concatenate([dsh1_0[0], dsc1_0[0]] + [zeros_d] * 4, axis=1), jnp.zeros((1, 6 * D), f32)])
    dm_mine = jnp.concatenate([dm_lat.reshape(2, 6 * D), dm_ctx.reshape(2, 6 * D),
                               jnp.zeros((4, 6 * D), f32)], axis=0)
    dm_g = allgather_small(dm_mine, "gather_dmod")
    dm_all = jnp.concatenate([jnp.moveaxis(dm_g[:, 0:2], 0, 1), jnp.moveaxis(dm_g[:, 2:4], 0, 1)], axis=1)
    dm_sh = lax.dynamic_slice_in_dim(dm_all, me * mod_cols, mod_cols, axis=2)
    g_mod_w, g_cctx_part, g_mod_b = mod_bwd(c16, mod_w, dm_sh, dm_all, "mod_bwd")

    rep = [jnp.stack([g_n1_0[0], g_n1_1[0]]), jnp.stack([gf0["norm2"][0], gf1["norm2"][0]]), g_conv_b5, g_dt_bias, g_a_log,
           g_ssd_d, g_ssd_norm, jnp.stack([gf0["conv_b"][0], gf1["conv_b"][0]]), g_final, g_cctx_part, loss_part[:, :1]]
    rep_sizes = [r_.size for r_ in rep]
    rep_flat = _pad_to(jnp.concatenate([r_.reshape(-1) for r_ in rep]), 8 * PACK_W).reshape(-1, PACK_W)
    _, rep_sum = allgather_small(rep_flat, "reduce_replicated", with_sum=True)
    rep_sum = rep_sum.reshape(-1)
    roffs = [0]
    for s_ in rep_sizes:
        roffs.append(roffs[-1] + s_)
    rp = [rep_sum[roffs[i]:roffs[i + 1]] for i in range(len(rep_sizes))]
    loss = rp[10].reshape(())

    r_up1, r_down1, r_pw2, r_pw1, r_up0, r_down0, r_out = early_red
    r_in, r_small = late_red
    from_chips = [r_in, r_up0, r_up1, r_pw1, r_out, r_down0, r_down1, r_pw2]
    g_flat = sum_rows(r_small, "reduce_sum_small").reshape(-1)
    goffs = [0]
    for s_ in gsizes:
        goffs.append(goffs[-1] + s_)
    gs = [g_flat[goffs[i]:goffs[i + 1]] for i in range(len(gsizes))]

    big = {}
    def tr(t):
        return jnp.swapaxes(t, -1, -2)

    w_in_res = sum_adamw(from_chips[0], tr(ssd_w_in[0]), tr(m_ssd_w_in[0]), tr(v_ssd_w_in[0]), None, "adamw_ssd_w_in")
    big["ssd_w_in"] = tuple(tr(t) for t in w_in_res)
    up_t, m_up_t, v_up_t = tr(ffn_w_up), tr(m_ffn_w_up), tr(v_ffn_w_up)
    up0 = sum_adamw(from_chips[1], up_t, m_up_t, v_up_t, 0, "adamw_ffn_w_up0")
    big["ffn_w_up"] = tuple(tr(t) for t in sum_adamw(from_chips[2], up_t, m_up_t, v_up_t, 1, "adamw_ffn_w_up1", into=up0))
    big["conf_w_pw1"] = sum_adamw(from_chips[3], conf_w_pw1[0], m_conf_w_pw1[0], v_conf_w_pw1[0], None, "adamw_conf_w_pw1")
    big["ssd_w_out"] = sum_adamw(from_chips[4], ssd_w_out[0], m_ssd_w_out[0], v_ssd_w_out[0], None, "adamw_ssd_w_out")
    dn0 = sum_adamw(from_chips[5], ffn_w_down, m_ffn_w_down, v_ffn_w_down, 0, "adamw_ffn_w_down0")
    big["ffn_w_down"] = sum_adamw(from_chips[6], ffn_w_down, m_ffn_w_down, v_ffn_w_down, 1, "adamw_ffn_w_down1", into=dn0)
    big["conf_w_pw2"] = sum_adamw(from_chips[7], conf_w_pw2[0], m_conf_w_pw2[0], v_conf_w_pw2[0], None, "adamw_conf_w_pw2")
    grads = {
        "c_ctx": rp[9], "mod_w": g_mod_w, "mod_b": g_mod_b, "norm1_w": rp[0], "norm2_w": rp[1],
        "ssd_conv_w": gs[0], "ssd_conv_b": rp[2], "ssd_dt_bias": rp[3], "ssd_a_log": rp[4], "ssd_d": rp[5],
        "ssd_norm_w": rp[6], "conf_b_pw1": gs[1], "conf_w_dw": gs[2],
        "conf_b_dw": gs[3], "conf_ln_w": gs[4], "conf_ln_b": gs[5], "conf_b_pw2": gs[6],
        "ffn_conv_w": gs[7], "ffn_conv_b": rp[7], "final_norm_w": rp[8],
    }
    weights = dict(c_ctx=c_ctx, mod_w=mod_w, mod_b=mod_b, norm1_w=norm1_w, norm2_w=norm2_w, ssd_w_in=ssd_w_in, ssd_conv_w=ssd_conv_w, ssd_conv_b=ssd_conv_b, ssd_dt_bias=ssd_dt_bias, ssd_a_log=ssd_a_log, ssd_d=ssd_d, ssd_norm_w=ssd_norm_w, ssd_w_out=ssd_w_out, conf_w_pw1=conf_w_pw1, conf_b_pw1=conf_b_pw1, conf_w_dw=conf_w_dw, conf_b_dw=conf_b_dw, conf_ln_w=conf_ln_w, conf_ln_b=conf_ln_b, conf_w_pw2=conf_w_pw2, conf_b_pw2=conf_b_pw2, ffn_w_up=ffn_w_up, ffn_conv_w=ffn_conv_w, ffn_conv_b=ffn_conv_b, ffn_w_down=ffn_w_down, final_norm_w=final_norm_w)
    m_in = dict(c_ctx=m_c_ctx, mod_w=m_mod_w, mod_b=m_mod_b, norm1_w=m_norm1_w, norm2_w=m_norm2_w, ssd_w_in=m_ssd_w_in, ssd_conv_w=m_ssd_conv_w, ssd_conv_b=m_ssd_conv_b, ssd_dt_bias=m_ssd_dt_bias, ssd_a_log=m_ssd_a_log, ssd_d=m_ssd_d, ssd_norm_w=m_ssd_norm_w, ssd_w_out=m_ssd_w_out, conf_w_pw1=m_conf_w_pw1, conf_b_pw1=m_conf_b_pw1, conf_w_dw=m_conf_w_dw, conf_b_dw=m_conf_b_dw, conf_ln_w=m_conf_ln_w, conf_ln_b=m_conf_ln_b, conf_w_pw2=m_conf_w_pw2, conf_b_pw2=m_conf_b_pw2, ffn_w_up=m_ffn_w_up, ffn_conv_w=m_ffn_conv_w, ffn_conv_b=m_ffn_conv_b, ffn_w_down=m_ffn_w_down, final_norm_w=m_final_norm_w)
    v_in = dict(c_ctx=v_c_ctx, mod_w=v_mod_w, mod_b=v_mod_b, norm1_w=v_norm1_w, norm2_w=v_norm2_w, ssd_w_in=v_ssd_w_in, ssd_conv_w=v_ssd_conv_w, ssd_conv_b=v_ssd_conv_b, ssd_dt_bias=v_ssd_dt_bias, ssd_a_log=v_ssd_a_log, ssd_d=v_ssd_d, ssd_norm_w=v_ssd_norm_w, ssd_w_out=v_ssd_w_out, conf_w_pw1=v_conf_w_pw1, conf_b_pw1=v_conf_b_pw1, conf_w_dw=v_conf_w_dw, conf_b_dw=v_conf_b_dw, conf_ln_w=v_conf_ln_w, conf_ln_b=v_conf_ln_b, conf_w_pw2=v_conf_w_pw2, conf_b_pw2=v_conf_b_pw2, ffn_w_up=v_ffn_w_up, ffn_conv_w=v_ffn_conv_w, ffn_conv_b=v_ffn_conv_b, ffn_w_down=v_ffn_w_down, final_norm_w=v_final_norm_w)

    out_g, out_d, out_m, out_v = [], [], [], []
    for name_, w_ in weights.items():
        shape = w_.shape
        if name_ in big:
            for lst, t in zip((out_g, out_d, out_m, out_v), big[name_]):
                lst.append(t.reshape(shape))
            continue
        cols2 = shape[-1] if len(shape) > 1 else shape[0]
        g2 = grads[name_].reshape(-1, cols2)
        d_, nm_, nv_ = adamw(w_.reshape(-1, cols2), g2, m_in[name_].reshape(-1, cols2), v_in[name_].reshape(-1, cols2),
                             f"adamw_{name_}")
        out_g.append(g2.reshape(shape))
        out_d.append(d_.reshape(shape))
        out_m.append(nm_.reshape(shape))
        out_v.append(nv_.reshape(shape))
    return (loss, grad_x, *out_g, *out_d, *out_m, *out_v)
```

```python
import functools

import jax
import jax.numpy as jnp
from jax import lax
from jax.experimental import pallas as pl
from jax.experimental.pallas import tpu as pltpu

f32 = jnp.float32
bf16 = jnp.bfloat16
HI = lax.Precision.HIGHEST
S = jax.ShapeDtypeStruct
MESH = pl.DeviceIdType.MESH
ANY = pl.BlockSpec(memory_space=pl.ANY)
VMEM = pl.BlockSpec(memory_space=pltpu.VMEM)

N_DEV = 8
D = 1024
DI = 2048
CONVD = 4096
FH = 2816
GRID_W = 64
Q = 128
HPG = 4
P = 64
N = 128
G = 8
GW = HPG * P
NH_SSD = G * HPG
EPS = 1e-6
ADAM_LR, ADAM_B1, ADAM_B2, ADAM_EPS, ADAM_WD, ADAM_STEP = 0.001, 0.9, 0.999, 1e-08, 0.01, 10
VMEM_LIMIT_BYTES = 56 * 1024 * 1024
PACK_W = 1024
TB = 256


def _cparams(*sem):
    return pltpu.CompilerParams(dimension_semantics=sem, vmem_limit_bytes=VMEM_LIMIT_BYTES)


def _pick(n, prefs):
    for p in prefs:
        if n % p == 0:
            return p
    return n


def _sigmoid(x):
    return 1.0 / (1.0 + jnp.exp(-x))


def _softplus(x):
    return jnp.maximum(x, 0.0) + jnp.log(1.0 + jnp.exp(-jnp.abs(x)))


def matmul(a, b, mode, out_dtype, name):
    if mode == "nn":
        (M, K), (_, Nn) = a.shape, b.shape
        bm, bn, bk = _pick(M, (512, 384, 256, 128)), Nn, K
    elif mode == "tn":
        (K, M), (_, Nn) = a.shape, b.shape
        bm, bn, bk = M, Nn, _pick(K, (256, 128))
    else:
        (M, K), (Nn, _) = a.shape, b.shape
        bm, bn, bk = _pick(M, (512, 384, 256, 128)), Nn, K
    nk = K // bk
    dims = {"nn": (((1,), (0,)), ((), ())), "tn": (((0,), (0,)), ((), ())), "nt": (((1,), (1,)), ((), ()))}[mode]

    def body(a_ref, b_ref, o_ref, acc_ref):
        k = pl.program_id(2)

        @pl.when(k == 0)
        def _():
            acc_ref[...] = jnp.zeros_like(acc_ref)

        acc_ref[...] += lax.dot_general(a_ref[...].astype(bf16), b_ref[...].astype(bf16), dims,
                                        preferred_element_type=f32)

        @pl.when(k == nk - 1)
        def _():
            o_ref[...] = acc_ref[...].astype(out_dtype)

    if mode == "nn":
        a_spec = pl.BlockSpec((bm, bk), lambda i, j, k: (i, k))
        b_spec = pl.BlockSpec((bk, bn), lambda i, j, k: (k, j))
    elif mode == "tn":
        a_spec = pl.BlockSpec((bk, bm), lambda i, j, k: (k, i))
        b_spec = pl.BlockSpec((bk, bn), lambda i, j, k: (k, j))
    else:
        a_spec = pl.BlockSpec((bm, bk), lambda i, j, k: (i, k))
        b_spec = pl.BlockSpec((bn, bk), lambda i, j, k: (j, k))
    return pl.pallas_call(
        body, grid=(M // bm, Nn // bn, nk), in_specs=[a_spec, b_spec],
        out_specs=pl.BlockSpec((bm, bn), lambda i, j, k: (i, j)),
        out_shape=S((M, Nn), out_dtype), scratch_shapes=[pltpu.VMEM((bm, bn), f32)],
        compiler_params=_cparams("parallel", "parallel", "arbitrary"), name=name,
    )(a, b)


SMM_BM = 256
SMM_ROWS = (256,)


def _shard_pieces(seg_widths, n):
    bounds = [0]
    for sw in seg_widths:
        bounds.append(bounds[-1] + sw)
    assert bounds[-1] == N_DEV * n, (seg_widths, n)
    out = []
    for j in range(N_DEV):
        lo, hi = j * n, (j + 1) * n
        pcs = []
        for si in range(len(seg_widths)):
            a, b = max(lo, bounds[si]), min(hi, bounds[si + 1])
            if a < b:
                pcs.append((si, a - bounds[si], a - lo, b - a))
        out.append(pcs)
    return out


def _w_spec(w, layer):
    if layer is None:
        return pl.BlockSpec(w.shape, lambda *idx: (0, 0, 0))
    return pl.BlockSpec((N_DEV, None) + w.shape[2:], lambda *idx: (0, layer, 0, 0))


def smm_fwd(a, w, layer, seg_widths, name, hosted=None):
    M, K = a.shape
    n = w.shape[-1]
    pieces = _shard_pieces(seg_widths, n)
    padded = [sw + (-sw) % 128 for sw in seg_widths]
    bm = _pick(M, SMM_ROWS)

    def body(a_ref, w_ref, *o_refs):
        av = a_ref[...]
        for si, sw in enumerate(seg_widths):
            if padded[si] != sw:
                o_refs[si][:, pl.ds(padded[si] - 128, 128)] = jnp.zeros((bm, 128), f32)
        for j in range(N_DEV):
            for si, soff, woff, wd in pieces[j]:
                o_refs[si][:, pl.ds(soff, wd)] = jnp.dot(av, w_ref[j, :, pl.ds(woff, wd)], preferred_element_type=f32)

    outs, extra = _host_call(
        body, (M // bm,), [pl.BlockSpec((bm, K), lambda i: (i, 0)), _w_spec(w, layer)],
        [pl.BlockSpec((bm, pw), lambda i: (i, 0)) for pw in padded], [S((M, pw), f32) for pw in padded], [],
        ("parallel",), name, (a, w), hosted)
    return outs if hosted is None else (outs, extra)


def smm_dx(d_segs, w, layer, seg_widths, out_dtype, name, hosted=None):
    M = d_segs[0].shape[0]
    K, n = w.shape[-2], w.shape[-1]
    pieces = _shard_pieces(seg_widths, n)
    ns = len(d_segs)
    bm = _pick(M, SMM_ROWS)

    def body(*refs):
        d_refs, w_ref, o_ref = refs[:ns], refs[ns], refs[ns + 1]
        acc = jnp.zeros((bm, K), f32)
        for j in range(N_DEV):
            for si, soff, woff, wd in pieces[j]:
                acc = acc + lax.dot_general(d_refs[si][:, pl.ds(soff, wd)], w_ref[j, :, pl.ds(woff, wd)],
                                            (((1,), (1,)), ((), ())), preferred_element_type=f32)
        o_ref[...] = acc.astype(out_dtype)

    (out,), extra = _host_call(
        body, (M // bm,),
        [pl.BlockSpec((bm, d.shape[1]), lambda i: (i, 0)) for d in d_segs] + [_w_spec(w, layer)],
        [pl.BlockSpec((bm, K), lambda i: (i, 0))], [S((M, K), out_dtype)], [], ("parallel",), name,
        (*d_segs, w), hosted)
    return out, extra


def smm_dw(a, d_segs, n, seg_widths, ngrp, transposed, name):
    M, K = a.shape
    pieces = _shard_pieces(seg_widths, n)
    per = N_DEV // ngrp
    nI = M // SMM_BM
    ns = len(d_segs)
    shard = (n, K) if transposed else (K, n)

    def body(*refs):
        a_ref, d_refs, o_ref, acc_ref = refs[0], refs[1:1 + ns], refs[1 + ns], refs[2 + ns]
        grp = pl.program_id(0)
        i = pl.program_id(1)

        @pl.when(i == 0)
        def _():
            acc_ref[...] = jnp.zeros_like(acc_ref)

        av = a_ref[...]
        for gs in range(ngrp):
            def one_group(gs=gs):
                for jj in range(per):
                    for si, soff, woff, wd in pieces[gs * per + jj]:
                        dv = d_refs[si][:, pl.ds(soff, wd)]
                        if transposed:
                            acc_ref[jj, pl.ds(woff, wd), :] += lax.dot_general(
                                dv, av, (((0,), (0,)), ((), ())), preferred_element_type=f32)
                        else:
                            acc_ref[jj, :, pl.ds(woff, wd)] += lax.dot_general(
                                av, dv, (((0,), (0,)), ((), ())), preferred_element_type=f32)
            pl.when(grp == gs)(one_group)

        @pl.when(i == nI - 1)
        def _():
            o_ref[...] = acc_ref[...].astype(bf16)

    return pl.pallas_call(
        body, grid=(ngrp, nI),
        in_specs=[pl.BlockSpec((SMM_BM, K), lambda g, i: (i, 0))]
        + [pl.BlockSpec((SMM_BM, d.shape[1]), lambda g, i: (i, 0)) for d in d_segs],
        out_specs=pl.BlockSpec((per,) + shard, lambda g, i: (g, 0, 0)), out_shape=S((N_DEV,) + shard, bf16),
        scratch_shapes=[pltpu.VMEM((per,) + shard, f32)],
        compiler_params=_cparams("arbitrary", "arbitrary"), name=name)(a, *d_segs)


def _modnorm_f(h, w, sc, sh):
    y = h * lax.rsqrt(jnp.mean(h * h, axis=-1, keepdims=True) + EPS)
    return (y * w) * (1.0 + sc) + sh


def _kind_specs(nctxb):
    if nctxb > 0:
        return pl.BlockSpec((None, 1, D), lambda i: (jnp.where(i < nctxb, 0, 1), 0, 0))
    return pl.BlockSpec((None, 1, D), lambda i: (0, 0, 0))


def modnorm_fwd(h, w, sc, sh, nctxb, name):
    T = h.shape[0]

    def body(h_ref, w_ref, sc_ref, sh_ref, o_ref):
        o_ref[...] = _modnorm_f(h_ref[...], w_ref[...], sc_ref[...], sh_ref[...]).astype(bf16)

    blk = pl.BlockSpec((TB, D), lambda i: (i, 0))
    row = pl.BlockSpec((1, D), lambda i: (0, 0))
    ks = _kind_specs(nctxb)
    return pl.pallas_call(body, grid=(T // TB,), in_specs=[blk, row, ks, ks], out_specs=blk,
                          out_shape=S((T, D), bf16), compiler_params=_cparams("parallel"), name=name)(h, w, sc, sh)


def modnorm_bwd(h, w, sc, sh, da, dres, nctxb, name):
    T = h.shape[0]
    kinds = sc.shape[0]

    def body(h_ref, w_ref, sc_ref, sh_ref, da_ref, dres_ref, dh_ref, dw_ref, dsc_ref, dsh_ref):
        i = pl.program_id(0)
        _, vjp = jax.vjp(_modnorm_f, h_ref[...], w_ref[...], sc_ref[...], sh_ref[...])
        dh, dw, dsc, dsh = vjp(da_ref[...].astype(f32))
        dh_ref[...] = dres_ref[...] + dh

        @pl.when(i == 0)
        def _():
            dw_ref[...] = jnp.zeros_like(dw_ref)

        @pl.when((i == 0) | (i == nctxb))
        def _():
            dsc_ref[...] = jnp.zeros_like(dsc_ref)
            dsh_ref[...] = jnp.zeros_like(dsh_ref)

        dw_ref[...] += dw
        dsc_ref[...] += dsc
        dsh_ref[...] += dsh

    blk = pl.BlockSpec((TB, D), lambda i: (i, 0))
    lat = pl.BlockSpec((TB, D), lambda i: (jnp.maximum(i - nctxb, 0), 0))
    row = pl.BlockSpec((1, D), lambda i: (0, 0))
    ks = _kind_specs(nctxb)
    return pl.pallas_call(
        body, grid=(T // TB,), in_specs=[blk, row, ks, ks, blk, lat], out_specs=[lat, row, ks, ks],
        out_shape=[S((T - nctxb * TB, D), f32), S((1, D), f32), S((kinds, 1, D), f32), S((kinds, 1, D), f32)],
        compiler_params=_cparams("arbitrary"), name=name)(h, w, sc, sh, da, dres)


def resgate_fwd(h, o, g, b, name):
    T = h.shape[0]

    def body(h_ref, o_ref, g_ref, b_ref, out_ref):
        out_ref[...] = h_ref[...] + g_ref[...] * (o_ref[...] + b_ref[...])

    blk = pl.BlockSpec((TB, D), lambda i: (i, 0))
    row = pl.BlockSpec((1, D), lambda i: (0, 0))
    return pl.pallas_call(body, grid=(T // TB,), in_specs=[blk, blk, row, row], out_specs=blk,
                          out_shape=S((T, D), f32), compiler_params=_cparams("parallel"), name=name)(h, o, g, b)


def resgate_bwd(dh, o, g, b, name):
    T = dh.shape[0]

    def body(dh_ref, o_ref, g_ref, b_ref, do_ref, dg_ref, db_ref):
        i = pl.program_id(0)

        @pl.when(i == 0)
        def _():
            dg_ref[...] = jnp.zeros_like(dg_ref)
            db_ref[...] = jnp.zeros_like(db_ref)

        dh = dh_ref[...]
        do = g_ref[...] * dh
        do_ref[...] = do.astype(bf16)
        dg_ref[...] += jnp.sum(dh * (o_ref[...] + b_ref[...]), axis=0, keepdims=True)
        db_ref[...] += jnp.sum(do, axis=0, keepdims=True)

    blk = pl.BlockSpec((TB, D), lambda i: (i, 0))
    row = pl.BlockSpec((1, D), lambda i: (0, 0))
    return pl.pallas_call(body, grid=(T // TB,), in_specs=[blk, blk, row, row], out_specs=[blk, row, row],
                          out_shape=[S((T, D), bf16), S((1, D), f32), S((1, D), f32)],
                          compiler_params=_cparams("arbitrary"), name=name)(dh, o, g, b)


def final_loss(h, w, tgt, name):
    T = h.shape[0]

    def f(hv, wv, tv):
        y = (hv * lax.rsqrt(jnp.mean(hv * hv, axis=-1, keepdims=True) + EPS)) * wv
        e = y - tv
        return 0.5 * jnp.sum(jnp.sum(e * e, axis=-1, keepdims=True), axis=0, keepdims=True) * (1.0 / D)

    def body(h_ref, w_ref, t_ref, loss_ref, dh_ref, dw_ref):
        i = pl.program_id(0)
        tv = t_ref[...]
        val, vjp = jax.vjp(lambda a, b_: f(a, b_, tv), h_ref[...], w_ref[...])
        dh, dw = vjp(jnp.ones((1, 1), f32))
        dh_ref[...] = dh

        @pl.when(i == 0)
        def _():
            loss_ref[...] = jnp.zeros_like(loss_ref)
            dw_ref[...] = jnp.zeros_like(dw_ref)

        loss_ref[...] += jnp.broadcast_to(val, (1, 128))
        dw_ref[...] += dw

    blk = pl.BlockSpec((TB, D), lambda i: (i, 0))
    row = pl.BlockSpec((1, D), lambda i: (0, 0))
    return pl.pallas_call(body, grid=(T // TB,), in_specs=[blk, row, blk],
                          out_specs=[pl.BlockSpec((1, 128), lambda i: (0, 0)), blk, row],
                          out_shape=[S((1, 128), f32), S((T, D), f32), S((1, D), f32)],
                          compiler_params=_cparams("arbitrary"), name=name)(h, w, tgt)


CB = 256
RT = 32
RTB = 16


def _fold8(t):
    acc = t[0:8]
    for k in range(1, t.shape[0] // 8):
        acc = acc + t[8 * k:8 * (k + 1)]
    return acc


def _rows(start, off=0, rt=RT):
    return pl.ds(pl.multiple_of(start + off, 8), rt)


def _rowsb(start, off=0):
    return _rows(start, off, RTB)


def _zero_rows(ref, start, n):
    ref[pl.ds(start, n), :] = jnp.zeros((n, ref.shape[1]), f32)


K5, HALF5, PAD5 = 5, 2, 8


def _shift_copies5(base_ref, s_ref, ln, sign):
    for k in range(K5):
        s_ref[k, pl.ds(0, ln), :] = base_ref[pl.ds(PAD5 + sign * (k - HALF5), ln), :]


def ssd_conv_fwd(u, w, b, segs, name, hosted=None):
    T = u.shape[0]
    maxlen = max(ln for _, ln in segs)

    def body(u_ref, w_ref, b_ref, o_ref, ds_ref, base_ref, s_ref):
        wv = [w_ref[pl.ds(k, 1), :] for k in range(K5)]
        bv = b_ref[...]
        for s0, ln in segs:
            _zero_rows(base_ref, 0, PAD5)
            _zero_rows(base_ref, PAD5 + ln, PAD5)
            base_ref[pl.ds(PAD5, ln), :] = u_ref[pl.ds(s0, ln), :]
            _shift_copies5(base_ref, s_ref, ln, 1)

            def tile(i, carry):
                r = i * RT
                acc = jnp.broadcast_to(bv, (RT, CB))
                for k in range(K5):
                    acc = acc + s_ref[k, _rows(r), :] * wv[k]
                sg = _sigmoid(acc)
                o_ref[_rows(r, s0), :] = acc * sg
                ds_ref[_rows(r, s0), :] = sg * (1.0 + acc * (1.0 - sg))
                return carry

            lax.fori_loop(0, ln // RT, tile, 0, unroll=2)

    cblk = pl.BlockSpec((T, CB), lambda j: (0, j))
    (out, dsilu), extra = _host_call(
        body, (CONVD // CB,),
        [cblk, pl.BlockSpec((K5, CB), lambda j: (0, j)), pl.BlockSpec((1, CB), lambda j: (0, j))],
        [cblk, cblk], [S((T, CONVD), f32), S((T, CONVD), f32)],
        [pltpu.VMEM((maxlen + 2 * PAD5, CB), f32), pltpu.VMEM((K5, maxlen, CB), f32)],
        ("parallel",), name, (u, w, b), hosted)
    return out, dsilu, extra


def ssd_conv_bwd(proj, w, dsilu, dy2, dskip, segs, name):
    T = proj.shape[0]
    maxlen = max(ln for _, ln in segs)
    nskip = DI // CB

    def body(u_ref, w_ref, ds_ref, dya_ref, dyb_ref, dsk_ref, du_ref, dw_ref, db_ref, base_ref, s_ref):
        wv = [w_ref[pl.ds(k, 1), :] for k in range(K5)]
        has_skip = (pl.program_id(0) < nskip).astype(f32)
        acc8 = tuple(jnp.zeros((8, CB), f32) for _ in range(K5 + 1))
        for s0, ln in segs:
            _zero_rows(base_ref, 0, PAD5)
            _zero_rows(base_ref, PAD5 + ln, PAD5)
            base_ref[pl.ds(PAD5, ln), :] = u_ref[pl.ds(s0, ln), :]
            _shift_copies5(base_ref, s_ref, ln, 1)

            def tile1(i, carry):
                r = i * RTB
                dy = dya_ref[_rowsb(r, s0), :] + dyb_ref[_rowsb(r, s0), :] + has_skip * dsk_ref[_rowsb(r, s0), :]
                dpre = dy * ds_ref[_rowsb(r, s0), :]
                base_ref[_rowsb(r, PAD5), :] = dpre
                new = [carry[k] + _fold8(dpre * s_ref[k, _rowsb(r), :]) for k in range(K5)]
                new.append(carry[K5] + _fold8(dpre))
                return tuple(new)

            acc8 = lax.fori_loop(0, ln // RTB, tile1, acc8, unroll=2)
            _shift_copies5(base_ref, s_ref, ln, -1)

            def tile2(i, carry):
                r = i * RTB
                du = jnp.zeros((RTB, CB), f32)
                for k in range(K5):
                    du = du + s_ref[k, _rowsb(r), :] * wv[k]
                du_ref[_rowsb(r, s0), :] = du.astype(bf16)
                return carry

            lax.fori_loop(0, ln // RTB, tile2, 0, unroll=4)
        for k in range(K5):
            dw_ref[pl.ds(k, 1), :] = jnp.sum(acc8[k], axis=0, keepdims=True)
        db_ref[...] = jnp.sum(acc8[K5], axis=0, keepdims=True)

    cblk = pl.BlockSpec((T, CB), lambda j: (0, j))
    return pl.pallas_call(
        body, grid=(CONVD // CB,),
        in_specs=[cblk, pl.BlockSpec((K5, CB), lambda j: (0, j)), cblk,
                  pl.BlockSpec((None, T, CB), lambda j: (0, 0, j)), pl.BlockSpec((None, T, CB), lambda j: (1, 0, j)),
                  pl.BlockSpec((T, CB), lambda j: (0, jnp.minimum(j, nskip - 1)))],
        out_specs=[cblk, pl.BlockSpec((K5, CB), lambda j: (0, j)), pl.BlockSpec((1, CB), lambda j: (0, j))],
        out_shape=[S((T, CONVD), bf16), S((K5, CONVD), f32), S((1, CONVD), f32)],
        scratch_shapes=[pltpu.VMEM((maxlen + 2 * PAD5, CB), f32), pltpu.VMEM((K5, maxlen, CB), f32)],
        compiler_params=_cparams("parallel"), name=name)(proj, w, dsilu, dy2, dy2, dskip)


GPAD = GRID_W


def _grid_copies(g_ref, src, L):
    col = lax.broadcasted_iota(jnp.int32, (L, CB), 0) & (GRID_W - 1)
    for d in range(3):
        _zero_rows(g_ref.at[d], 0, GPAD)
        _zero_rows(g_ref.at[d], GPAD + L, GPAD)
    g_ref[1, pl.ds(GPAD, L), :] = src
    g_ref[0, pl.ds(GPAD, L), :] = jnp.where(col != 0, g_ref[1, pl.ds(GPAD - 1, L), :], 0.0)
    g_ref[2, pl.ds(GPAD, L), :] = jnp.where(col != GRID_W - 1, g_ref[1, pl.ds(GPAD + 1, L), :], 0.0)


def ffn_gate_fwd(val, gate, cw, cb_, name, hosted=None):
    L = val.shape[0]
    nb = FH // CB

    def body(val_ref, gate_ref, w_ref, b_ref, o_ref, s_ref, vds_ref, g_ref):
        wv = [w_ref[pl.ds(t, 1), :] for t in range(9)]
        bv = b_ref[...]
        _grid_copies(g_ref, gate_ref[...], L)

        def tile(i, carry):
            r = i * RT
            acc = jnp.broadcast_to(bv, (RT, CB))
            for dr in range(3):
                for dc in range(3):
                    acc = acc + g_ref[dc, _rows(r, GPAD + (dr - 1) * GRID_W), :] * wv[3 * dr + dc]
            sg = _sigmoid(acc)
            s = acc * sg
            v = val_ref[_rows(r), :]
            o_ref[_rows(r), :] = (s * v).astype(bf16)
            s_ref[_rows(r), :] = s
            vds_ref[_rows(r), :] = v * (sg * (1.0 + acc * (1.0 - sg)))
            return carry

        lax.fori_loop(0, L // RT, tile, 0, unroll=2)

    cblk = pl.BlockSpec((L, CB), lambda j: (0, j))
    (out, s_, vds), extra = _host_call(
        body, (nb,), [cblk, cblk, pl.BlockSpec((9, CB), lambda j: (0, j)), pl.BlockSpec((1, CB), lambda j: (0, j))],
        [cblk, cblk, cblk], [S((L, FH), bf16), S((L, FH), f32), S((L, FH), f32)],
        [pltpu.VMEM((3, L + 2 * GPAD, CB), f32)], ("parallel",), name, (val, gate, cw, cb_), hosted)
    return out, s_, vds, extra


def ffn_gate_bwd(gate, s_, vds, cw, dact, name):
    L = gate.shape[0]
    nb = FH // CB

    def body(gate_ref, s_ref, vds_ref, w_ref, da_ref, dval_ref, dgate_ref, dw_ref, db_ref, g_ref, d_ref):
        wv = [w_ref[pl.ds(t, 1), :] for t in range(9)]
        _grid_copies(g_ref, gate_ref[...], L)

        def tile1(i, carry):
            r = i * RTB
            da = da_ref[_rowsb(r), :].astype(f32)
            dval_ref[_rowsb(r), :] = (da * s_ref[_rowsb(r), :]).astype(bf16)
            dpre = da * vds_ref[_rowsb(r), :]
            d_ref[_rowsb(r), :] = dpre
            new = [carry[t] + _fold8(dpre * g_ref[t % 3, _rowsb(r, GPAD + (t // 3 - 1) * GRID_W), :]) for t in range(9)]
            new.append(carry[9] + _fold8(dpre))
            return tuple(new)

        acc8 = lax.fori_loop(0, L // RTB, tile1, tuple(jnp.zeros((8, CB), f32) for _ in range(10)), unroll=2)
        for t in range(9):
            dw_ref[pl.ds(t, 1), :] = jnp.sum(acc8[t], axis=0, keepdims=True)
        db_ref[...] = jnp.sum(acc8[9], axis=0, keepdims=True)
        _grid_copies(g_ref, d_ref[...], L)

        def tile2(i, carry):
            r = i * RTB
            dg = jnp.zeros((RTB, CB), f32)
            for dr in range(3):
                for dc in range(3):
                    dg = dg + g_ref[2 - dc, _rowsb(r, GPAD - (dr - 1) * GRID_W), :] * wv[3 * dr + dc]
            dgate_ref[_rowsb(r), :] = dg.astype(bf16)
            return carry

        lax.fori_loop(0, L // RTB, tile2, 0, unroll=4)

    cblk = pl.BlockSpec((L, CB), lambda j: (0, j))
    return pl.pallas_call(
        body, grid=(nb,),
        in_specs=[cblk, cblk, cblk, pl.BlockSpec((9, CB), lambda j: (0, j)), cblk],
        out_specs=[cblk, cblk, pl.BlockSpec((9, CB), lambda j: (0, j)), pl.BlockSpec((1, CB), lambda j: (0, j))],
        out_shape=[S((L, FH), bf16), S((L, FH), bf16), S((9, FH), f32), S((1, FH), f32)],
        scratch_shapes=[pltpu.VMEM((3, L + 2 * GPAD, CB), f32), pltpu.VMEM((L, CB), f32)],
        compiler_params=_cparams("parallel"), name=name)(gate, s_, vds, cw, dact)


CONF_K = 31
CHALF = CONF_K // 2
CPAD = 16


def _shift_copies8(c_ref, base_ref, L):
    n = L + 2 * CPAD - 8
    for b_ in range(8):
        c_ref[b_, pl.ds(0, n), :] = base_ref[pl.ds(b_, n), :]


def _tap_ab(o):
    return o % 8, o - o % 8


def conf_glu_conv_fwd(pa, pg, b1, wdw, bdw, name, hosted=None):
    L = pa.shape[0]
    nb = D // CB

    def body(pa_ref, pg_ref, ba_ref, bg_ref, w_ref, bdw_ref, o_ref, base_ref, c_ref):
        _zero_rows(base_ref, 0, CPAD)
        _zero_rows(base_ref, CPAD + L, CPAD)
        base_ref[pl.ds(CPAD, L), :] = (pa_ref[...] + ba_ref[...]) * _sigmoid(pg_ref[...] + bg_ref[...])
        _shift_copies8(c_ref, base_ref, L)
        bv = bdw_ref[...]

        def tile(i, carry):
            r = i * RT
            acc = jnp.broadcast_to(bv, (RT, CB))
            for k in range(CONF_K):
                b_, a8 = _tap_ab(k - CHALF)
                acc = acc + c_ref[b_, _rows(r, CPAD + a8), :] * w_ref[pl.ds(k, 1), :]
            o_ref[_rows(r), :] = acc
            return carry

        lax.fori_loop(0, L // RT, tile, 0, unroll=2)

    cblk = pl.BlockSpec((L, CB), lambda j: (0, j))
    rblk = pl.BlockSpec((1, CB), lambda j: (0, j))
    rgblk = pl.BlockSpec((1, CB), lambda j: (0, nb + j))
    (out,), extra = _host_call(
        body, (nb,), [cblk, cblk, rblk, rgblk, pl.BlockSpec((CONF_K, CB), lambda j: (0, j)), rblk],
        [cblk], [S((L, D), f32)], [pltpu.VMEM((L + 2 * CPAD, CB), f32), pltpu.VMEM((8, L + 2 * CPAD, CB), f32)],
        ("parallel",), name, (pa, pg, b1, b1, wdw, bdw), hosted)
    return out, extra


def conf_glu_conv_bwd(pa, pg, b1, wdw, dy, name):
    L = pa.shape[0]
    nb = D // CB

    def body(pa_ref, pg_ref, ba_ref, bg_ref, w_ref, dy_ref, dpa_ref, dpg_ref, dba_ref, dbg_ref, dw_ref, dbdw_ref,
             base_ref, c_ref, acc_ref):
        _zero_rows(base_ref, 0, CPAD)
        _zero_rows(base_ref, CPAD + L, CPAD)
        base_ref[pl.ds(CPAD, L), :] = (pa_ref[...] + ba_ref[...]) * _sigmoid(pg_ref[...] + bg_ref[...])
        _shift_copies8(c_ref, base_ref, L)
        acc_ref[...] = jnp.zeros_like(acc_ref)

        def tile1(i, carry):
            r = i * RTB
            dyt = dy_ref[_rowsb(r), :]
            for k in range(CONF_K):
                b_, a8 = _tap_ab(k - CHALF)
                acc_ref[k] += _fold8(dyt * c_ref[b_, _rowsb(r, CPAD + a8), :])
            return carry + _fold8(dyt)

        db8 = lax.fori_loop(0, L // RTB, tile1, jnp.zeros((8, CB), f32), unroll=2)
        dbdw_ref[...] = jnp.sum(db8, axis=0, keepdims=True)
        for k in range(CONF_K):
            dw_ref[pl.ds(k, 1), :] = jnp.sum(acc_ref[k], axis=0, keepdims=True)
        base_ref[pl.ds(CPAD, L), :] = dy_ref[...]
        _shift_copies8(c_ref, base_ref, L)
        ba = ba_ref[...]
        bg = bg_ref[...]

        def tile2(i, carry):
            r = i * RTB
            dglu = jnp.zeros((RTB, CB), f32)
            for k in range(CONF_K):
                b_, a8 = _tap_ab(CHALF - k)
                dglu = dglu + c_ref[b_, _rowsb(r, CPAD + a8), :] * w_ref[pl.ds(k, 1), :]
            a = pa_ref[_rowsb(r), :] + ba
            sg = _sigmoid(pg_ref[_rowsb(r), :] + bg)
            dpa = dglu * sg
            dpg = dglu * a * (sg * (1.0 - sg))
            dpa_ref[_rowsb(r), :] = dpa.astype(bf16)
            dpg_ref[_rowsb(r), :] = dpg.astype(bf16)
            return carry[0] + _fold8(dpa), carry[1] + _fold8(dpg)

        s8 = lax.fori_loop(0, L // RTB, tile2, (jnp.zeros((8, CB), f32), jnp.zeros((8, CB), f32)), unroll=2)
        dba_ref[...] = jnp.sum(s8[0], axis=0, keepdims=True)
        dbg_ref[...] = jnp.sum(s8[1], axis=0, keepdims=True)

    cblk = pl.BlockSpec((L, CB), lambda j: (0, j))
    rblk = pl.BlockSpec((1, CB), lambda j: (0, j))
    rgblk = pl.BlockSpec((1, CB), lambda j: (0, nb + j))
    wblk = pl.BlockSpec((CONF_K, CB), lambda j: (0, j))
    return pl.pallas_call(
        body, grid=(nb,), in_specs=[cblk, cblk, rblk, rgblk, wblk, cblk],
        out_specs=[cblk, cblk, rblk, rblk, wblk, rblk],
        out_shape=[S((L, D), bf16), S((L, D), bf16), S((1, D), f32), S((1, D), f32), S((CONF_K, D), f32), S((1, D), f32)],
        scratch_shapes=[pltpu.VMEM((L + 2 * CPAD, CB), f32), pltpu.VMEM((8, L + 2 * CPAD, CB), f32),
                        pltpu.VMEM((CONF_K, 8, CB), f32)],
        compiler_params=_cparams("parallel"), name=name)(pa, pg, b1, b1, wdw, dy)


def _ln_silu_f(x, w, b):
    mu = jnp.mean(x, axis=-1, keepdims=True)
    d = x - mu
    y = d * lax.rsqrt(jnp.mean(d * d, axis=-1, keepdims=True) + EPS) * w + b
    return y * _sigmoid(y)


def ln_silu_fwd(x, w, b, name):
    T = x.shape[0]

    def body(x_ref, w_ref, b_ref, o_ref):
        o_ref[...] = _ln_silu_f(x_ref[...], w_ref[...], b_ref[...]).astype(bf16)

    blk = pl.BlockSpec((TB, D), lambda i: (i, 0))
    row = pl.BlockSpec((1, D), lambda i: (0, 0))
    return pl.pallas_call(body, grid=(T // TB,), in_specs=[blk, row, row], out_specs=blk, out_shape=S((T, D), bf16),
                          compiler_params=_cparams("parallel"), name=name)(x, w, b)


def ln_silu_bwd(x, w, b, ds, name):
    T = x.shape[0]

    def body(x_ref, w_ref, b_ref, ds_ref, dx_ref, dw_ref, db_ref):
        i = pl.program_id(0)
        _, vjp = jax.vjp(_ln_silu_f, x_ref[...], w_ref[...], b_ref[...])
        dx, dw, db = vjp(ds_ref[...].astype(f32))
        dx_ref[...] = dx

        @pl.when(i == 0)
        def _():
            dw_ref[...] = jnp.zeros_like(dw_ref)
            db_ref[...] = jnp.zeros_like(db_ref)

        dw_ref[...] += dw
        db_ref[...] += db

    blk = pl.BlockSpec((TB, D), lambda i: (i, 0))
    row = pl.BlockSpec((1, D), lambda i: (0, 0))
    return pl.pallas_call(body, grid=(T // TB,), in_specs=[blk, row, row, blk], out_specs=[blk, row, row],
                          out_shape=[S((T, D), f32), S((1, D), f32), S((1, D), f32)],
                          compiler_params=_cparams("arbitrary"), name=name)(x, w, b, ds)


def _mxu(a, b, dims):
    return lax.dot_general(a.astype(bf16), b.astype(bf16), (dims, ((), ())), preferred_element_type=f32)


def _nn(a, b):
    return _mxu(a, b, ((1,), (0,)))


def _nt(a, b):
    return _mxu(a, b, ((1,), (1,)))


def _tn(a, b):
    return _mxu(a, b, ((0,), (0,)))


@jax.custom_vjp
def _dot_nn(a, b):
    return _nn(a, b)


@jax.custom_vjp
def _dot_nt(a, b):
    return _nt(a, b)


@jax.custom_vjp
def _dot_tn(a, b):
    return _tn(a, b)


_dot_nn.defvjp(lambda a, b: (_nn(a, b), (a, b)), lambda res, g: (_nt(g, res[1]), _tn(res[0], g)))
_dot_nt.defvjp(lambda a, b: (_nt(a, b), (a, b)), lambda res, g: (_nn(g, res[1]), _tn(g, res[0])))
_dot_tn.defvjp(lambda a, b: (_tn(a, b), (a, b)), lambda res, g: (_nt(res[1], g), _nn(res[0], g)))


def _exact_dot(a, b, dims, split_first):
    v = a if split_first else b
    p1 = v.astype(bf16)
    r1 = v - p1.astype(f32)
    p2 = r1.astype(bf16)
    p3 = (r1 - p2.astype(f32)).astype(bf16)
    out = None
    for p in (p1, p2, p3):
        lhs, rhs = (p, b.astype(bf16)) if split_first else (a.astype(bf16), p)
        t = lax.dot_general(lhs, rhs, (dims, ((), ())), preferred_element_type=f32)
        out = t if out is None else out + t
    return out


@jax.custom_vjp
def _masked_sum_cols(mf, a):
    return _exact_dot(mf, a, ((1,), (0,)), False)


@jax.custom_vjp
def _masked_sum_rows(mf, a):
    return _exact_dot(a, mf, ((1,), (1,)), True)


_masked_sum_cols.defvjp(lambda mf, a: (_exact_dot(mf, a, ((1,), (0,)), False), mf),
                        lambda mf, g: (jnp.zeros_like(mf), _exact_dot(mf, g, ((0,), (0,)), False)))
_masked_sum_rows.defvjp(lambda mf, a: (_exact_dot(a, mf, ((1,), (1,)), True), mf),
                        lambda mf, g: (jnp.zeros_like(mf), _exact_dot(g, mf, ((1,), (0,)), True)))


def _masked_sum(mf, a, rows):
    return _masked_sum_rows(mf, a) if rows else _masked_sum_cols(mf, a)


def _lanes_to_rows(v):
    r = lax.broadcasted_iota(jnp.int32, (GW, GW), 0)
    c = lax.broadcasted_iota(jnp.int32, (GW, GW), 1)
    return jnp.sum(jnp.where(r == c, jnp.broadcast_to(v, (GW, GW)), 0.0), axis=1, keepdims=True)


def _ssd_chunk(x, B, C, dtc, dtr, bc, br, alc, alr, s_in, is_fwd):
    row = lax.broadcasted_iota(jnp.int32, (Q, Q), 0)
    col = lax.broadcasted_iota(jnp.int32, (Q, Q), 1)
    sgn = jnp.where(is_fwd, 1, -1).astype(jnp.int32)
    mask = (row - col) * sgn >= 0
    mf = mask.astype(f32)
    lane_head = lax.broadcasted_iota(jnp.int32, (1, GW), 1) // P

    def spread(v):
        out = jnp.zeros((v.shape[0], GW), f32)
        for r in range(HPG):
            out = jnp.where(lane_head == r, v[:, r:r + 1], out)
        return out

    dt_c = _softplus(dtc + bc)
    dt_r = _softplus(dtr + br)
    a_c = dt_c * (-jnp.exp(alc))
    a_r = dt_r * (-jnp.exp(alr))
    acum_c = _masked_sum(mf, a_c, False)
    acum_r = _masked_sum(mf, a_r, True)
    tot_c = jnp.sum(a_c, axis=0, keepdims=True)
    dt_e = spread(dt_c)
    acum_e = spread(acum_c)
    tot_e = spread(tot_c)
    xdt = x * dt_e
    cb = _dot_nt(C, B)
    scores, xs = [], []
    for r in range(HPG):
        seg = acum_c[:, r:r + 1] - acum_r[r:r + 1, :]
        scores.append(cb * jnp.exp(jnp.where(mask, seg, -jnp.inf)))
        xs.append(jnp.where(lane_head == r, xdt, 0.0))
    y = _dot_nn(jnp.concatenate(scores, axis=1), jnp.concatenate(xs, axis=0))
    y = y + _dot_nt(C, s_in) * jnp.exp(acum_e)
    xe = xdt * jnp.exp(tot_e - acum_e)
    s_out = _lanes_to_rows(jnp.exp(tot_e)) * s_in + _dot_tn(xe, B)
    return y, s_out


def _chunk_index(d, t, nctx, nc):
    bwd = jnp.where(t < nctx, nctx - 1 - t, nc - 1 - (t - nctx))
    return jnp.where(d == 0, t, bwd)


def _ssd_in_specs(ci):
    small_c = pl.BlockSpec((None, G, 1, HPG), lambda d, t: (d, 0, 0, 0))
    small_r = pl.BlockSpec((None, G, HPG, 1), lambda d, t: (d, 0, 0, 0))
    return [
        pl.BlockSpec((Q, CONVD), lambda d, t: (ci(d, t), 0)),
        pl.BlockSpec((None, G, Q, HPG), lambda d, t: (d, 0, ci(d, t), 0)),
        pl.BlockSpec((None, G, HPG, Q), lambda d, t: (d, 0, 0, ci(d, t))),
        small_c, small_r, small_c, small_r,
    ]


def _group_cols(g):
    return pl.ds(g * GW, GW), pl.ds(DI + g * N, N), pl.ds(DI + G * N + g * N, N)


def ssd_scan_fwd(xbc, dtc, dtr, bc, br, alc, alr, nctx, name, hosted=None):
    T = xbc.shape[0]
    nc = T // Q

    def body(xbc_ref, dtc_ref, dtr_ref, bc_ref, br_ref, alc_ref, alr_ref, y_ref, sin_ref, st_ref):
        d = pl.program_id(0)
        t = pl.program_id(1)

        @pl.when(t == 0)
        def _():
            st_ref[...] = jnp.zeros_like(st_ref)

        for g in range(G):
            xs, bs, cs = _group_cols(g)
            s_in = st_ref[g]
            sin_ref[g] = s_in
            y, s_out = _ssd_chunk(xbc_ref[:, xs], xbc_ref[:, bs], xbc_ref[:, cs], dtc_ref[g], dtr_ref[g], bc_ref[g], br_ref[g],
                                  alc_ref[g], alr_ref[g], s_in, d == 0)
            y_ref[:, xs] = y
            st_ref[g] = s_out

    ci = lambda d, t: _chunk_index(d, t, nctx, nc)
    out_specs = [
        pl.BlockSpec((None, Q, DI), lambda d, t: (d, ci(d, t), 0)),
        pl.BlockSpec((None, None, G, GW, N), lambda d, t: (d, ci(d, t), 0, 0, 0)),
    ]
    return _host_call(
        body, (2, nc), _ssd_in_specs(ci), out_specs, [S((2, T, DI), f32), S((2, nc, G, GW, N), f32)],
        [pltpu.VMEM((G, GW, N), f32)], ("arbitrary", "arbitrary"), name, (xbc, dtc, dtr, bc, br, alc, alr), hosted)


def ssd_scan_bwd(xbc, dtc, dtr, bc, br, alc, alr, s_in_all, dy, nctx, name, hosted=None):
    T = xbc.shape[0]
    nc = T // Q

    def body(xbc_ref, dtc_ref, dtr_ref, bc_ref, br_ref, alc_ref, alr_ref, sin_ref, dy_ref,
             dxbc_ref, ddtc_ref, ddtr_ref, dbc_ref, dbr_ref, dalc_ref, dalr_ref, ds_ref):
        d = pl.program_id(0)
        t = pl.program_id(1)

        @pl.when(t == 0)
        def _():
            ds_ref[...] = jnp.zeros_like(ds_ref)
            dbc_ref[...] = jnp.zeros_like(dbc_ref)
            dbr_ref[...] = jnp.zeros_like(dbr_ref)
            dalc_ref[...] = jnp.zeros_like(dalc_ref)
            dalr_ref[...] = jnp.zeros_like(dalr_ref)

        f = functools.partial(_ssd_chunk, is_fwd=(d == 0))
        for g in range(G):
            xs, bs, cs = _group_cols(g)
            _, vjp = jax.vjp(f, xbc_ref[:, xs], xbc_ref[:, bs], xbc_ref[:, cs], dtc_ref[g], dtr_ref[g], bc_ref[g], br_ref[g],
                             alc_ref[g], alr_ref[g], sin_ref[g])
            dx, dB, dC, ddtc, ddtr, dbc, dbr, dalc, dalr, ds = vjp((dy_ref[:, xs], ds_ref[g]))
            dxbc_ref[:, xs] = dx
            dxbc_ref[:, bs] = dB
            dxbc_ref[:, cs] = dC
            ddtc_ref[g] = ddtc
            ddtr_ref[g] = ddtr
            dbc_ref[g] += dbc
            dbr_ref[g] += dbr
            dalc_ref[g] += dalc
            dalr_ref[g] += dalr
            ds_ref[g] = ds

    ci = lambda d, t: _chunk_index(d, nc - 1 - t, nctx, nc)
    in_specs = _ssd_in_specs(ci) + [
        pl.BlockSpec((None, None, G, GW, N), lambda d, t: (d, ci(d, t), 0, 0, 0)),
        pl.BlockSpec((Q, DI), lambda d, t: (ci(d, t), 0)),
    ]
    small_c = pl.BlockSpec((None, G, 1, HPG), lambda d, t: (d, 0, 0, 0))
    small_r = pl.BlockSpec((None, G, HPG, 1), lambda d, t: (d, 0, 0, 0))
    out_specs = [
        pl.BlockSpec((None, Q, CONVD), lambda d, t: (d, ci(d, t), 0)),
        pl.BlockSpec((None, G, Q, HPG), lambda d, t: (d, 0, ci(d, t), 0)),
        pl.BlockSpec((None, G, HPG, Q), lambda d, t: (d, 0, 0, ci(d, t))),
        small_c, small_r, small_c, small_r,
    ]
    out_shape = [S((2, T, CONVD), f32), S((2, G, T, HPG), f32), S((2, G, HPG, T), f32),
                 S((2, G, 1, HPG), f32), S((2, G, HPG, 1), f32), S((2, G, 1, HPG), f32), S((2, G, HPG, 1), f32)]
    return _host_call(body, (2, nc), in_specs, out_specs, out_shape, [pltpu.VMEM((G, GW, N), f32)],
                      ("arbitrary", "arbitrary"), name, (xbc, dtc, dtr, bc, br, alc, alr, s_in_all, dy), hosted)


GTB = 128


def _gate_norm_f(yf, yb, x, z, dexp, w):
    y = (yf + yb + dexp * x) * (z * _sigmoid(z))
    return y * lax.rsqrt(jnp.mean(y * y, axis=-1, keepdims=True) + EPS) * w


def ssd_gate_fwd(y2, xbc, proj, dexp, w, nctxb, name):
    T = xbc.shape[0]
    L = T - nctxb * GTB

    def body(yf_ref, yb_ref, x_ref, z_ref, d_ref, w_ref, o_ref):
        o_ref[...] = _gate_norm_f(yf_ref[...], yb_ref[...], x_ref[...], z_ref[...], d_ref[...], w_ref[...]).astype(bf16)

    wide = pl.BlockSpec((GTB, DI), lambda i: (i + nctxb, 0))
    row = pl.BlockSpec((1, DI), lambda i: (0, 0))
    return pl.pallas_call(
        body, grid=(L // GTB,),
        in_specs=[pl.BlockSpec((None, GTB, DI), lambda i: (0, i + nctxb, 0)),
                  pl.BlockSpec((None, GTB, DI), lambda i: (1, i + nctxb, 0)), wide, wide, row, row],
        out_specs=pl.BlockSpec((GTB, DI), lambda i: (i, 0)), out_shape=S((L, DI), bf16),
        compiler_params=_cparams("parallel"), name=name)(y2, y2, xbc, proj, dexp, w)


def ssd_gate_bwd(y2, xbc, proj, dexp, w, dyn, nctxb, name, hosted=None):
    T = xbc.shape[0]
    nb = T // GTB

    def body(yf_ref, yb_ref, x_ref, z_ref, d_ref, w_ref, dyn_ref, dy_ref, dx_ref, dz_ref, dd_ref, dw_ref):
        i = pl.program_id(0)

        @pl.when(i == 0)
        def _():
            dd_ref[...] = jnp.zeros_like(dd_ref)
            dw_ref[...] = jnp.zeros_like(dw_ref)

        @pl.when(i < nctxb)
        def _():
            dy_ref[...] = jnp.zeros_like(dy_ref)
            dx_ref[...] = jnp.zeros_like(dx_ref)
            dz_ref[...] = jnp.zeros_like(dz_ref)

        @pl.when(i >= nctxb)
        def _():
            _, vjp = jax.vjp(_gate_norm_f, yf_ref[...], yb_ref[...], x_ref[...], z_ref[...], d_ref[...], w_ref[...])
            dyf, _, dx, dz, dd, dw = vjp(dyn_ref[...].astype(f32))
            dy_ref[...] = dyf
            dx_ref[...] = dx
            dz_ref[...] = dz.astype(bf16)
            fold = (lax.broadcasted_iota(jnp.int32, (DI, 128), 0) // P == lax.broadcasted_iota(jnp.int32, (DI, 128), 1))
            dd_ref[...] += jnp.dot(dd, fold.astype(f32), precision=HI, preferred_element_type=f32)
            dw_ref[...] += dw

    wide = pl.BlockSpec((GTB, DI), lambda i: (i, 0))
    row = pl.BlockSpec((1, DI), lambda i: (0, 0))
    hrow = pl.BlockSpec((1, 128), lambda i: (0, 0))
    return _host_call(
        body, (nb,),
        [pl.BlockSpec((None, GTB, DI), lambda i: (0, i, 0)), pl.BlockSpec((None, GTB, DI), lambda i: (1, i, 0)),
         wide, wide, row, row, pl.BlockSpec((GTB, DI), lambda i: (jnp.maximum(i - nctxb, 0), 0))],
        [wide, wide, wide, hrow, row],
        [S((T, DI), f32), S((T, DI), f32), S((T, DI), bf16), S((1, 128), f32), S((1, DI), f32)],
        [], ("arbitrary",), name, (y2, y2, xbc, proj, dexp, w, dyn), hosted)


CROWS = 2 * N_DEV


def mod_fwd(c16, modw, name):
    nl, _, cols = modw.shape

    def body(c_ref, w_ref, o_ref):
        cv = c_ref[...]
        s = cv * _sigmoid(cv)
        for l in range(nl):
            o_ref[l] = jnp.dot(s, w_ref[l], precision=HI, preferred_element_type=f32)

    return pl.pallas_call(body, in_specs=[VMEM, VMEM], out_specs=VMEM, out_shape=S((nl, CROWS, cols), f32),
                          compiler_params=pltpu.CompilerParams(vmem_limit_bytes=VMEM_LIMIT_BYTES), name=name)(c16, modw)


def mod_bwd(c16, modw, dm_sh, dm_all, name):
    nl, _, cols = modw.shape

    def body(c_ref, w_ref, dm_ref, dmall_ref, dw_ref, dc_ref, db_ref):
        cv = c_ref[...]
        sg = _sigmoid(cv)
        s = cv * sg
        ds_dc = sg * (1.0 + cv * (1.0 - sg))
        is_ctx = lax.broadcasted_iota(jnp.int32, (CROWS, D), 0) >= N_DEV
        dc = jnp.zeros((1, D), f32)
        for l in range(nl):
            dm = dm_ref[l]
            dw_ref[l] = lax.dot_general(s, dm, (((0,), (0,)), ((), ())), precision=HI, preferred_element_type=f32)
            dsv = lax.dot_general(dm, w_ref[l], (((1,), (1,)), ((), ())), precision=HI, preferred_element_type=f32)
            dc = dc + jnp.sum(jnp.where(is_ctx, dsv * ds_dc, 0.0), axis=0, keepdims=True)
            db_ref[pl.ds(l, 1), :] = jnp.sum(dmall_ref[l], axis=0, keepdims=True)
        dc_ref[...] = dc

    return pl.pallas_call(
        body, in_specs=[VMEM, VMEM, VMEM, VMEM], out_specs=[VMEM, VMEM, VMEM],
        out_shape=[S(modw.shape, f32), S((1, D), f32), S((nl, 6 * D), f32)],
        compiler_params=pltpu.CompilerParams(vmem_limit_bytes=VMEM_LIMIT_BYTES), name=name)(c16, modw, dm_sh, dm_all)


def adamw(w, g, m, v, name):
    R, C = w.shape
    rb = R if R <= 512 else max(r_ for r_ in range(8, 513, 8) if R % r_ == 0)
    bc1 = 1.0 - ADAM_B1 ** ADAM_STEP
    bc2 = 1.0 - ADAM_B2 ** ADAM_STEP

    def body(w_ref, g_ref, m_ref, v_ref, d_ref, nm_ref, nv_ref):
        gv = g_ref[...]
        m_new = ADAM_B1 * m_ref[...] + (1.0 - ADAM_B1) * gv
        v_new = ADAM_B2 * v_ref[...] + (1.0 - ADAM_B2) * (gv * gv)
        m_hat = m_new / bc1
        v_hat = v_new / bc2
        d_ref[...] = -ADAM_LR * (m_hat / (jnp.sqrt(v_hat) + ADAM_EPS) + ADAM_WD * w_ref[...])
        nm_ref[...] = m_new
        nv_ref[...] = v_new

    blk = pl.BlockSpec((rb, C), lambda i: (i, 0))
    return pl.pallas_call(body, grid=(R // rb,), in_specs=[blk] * 4, out_specs=[blk] * 3,
                          out_shape=[S((R, C), f32)] * 3, compiler_params=_cparams("parallel"), name=name)(w, g, m, v)


def _me():
    return lax.axis_index("x"), lax.axis_index("y"), lax.axis_index("c")


def allgather_small(x, name, with_sum=False):
    r, w = x.shape

    def body(x_ref, *refs):
        if with_sum:
            out_ref, sum_ref, send_sems, recv_sems = refs
        else:
            out_ref, send_sems, recv_sems = refs
        mx, my, mc = _me()
        me = 4 * mx + 2 * my + mc
        out_ref[me] = x_ref[...]
        peers = []
        for k in range(1, N_DEV):
            kx, ky, kc = (k >> 2) & 1, (k >> 1) & 1, k & 1
            peers.append((mx + kx - 2 * mx * kx, my + ky - 2 * my * ky, mc + kc - 2 * mc * kc))
        copies = []
        for k, peer in enumerate(peers):
            cp = pltpu.make_async_remote_copy(src_ref=x_ref, dst_ref=out_ref.at[me], send_sem=send_sems.at[k],
                                              recv_sem=recv_sems.at[k], device_id=peer, device_id_type=MESH)
            cp.start()
            copies.append(cp)
        for k, (px, py, pc) in enumerate(peers):
            pltpu.make_async_remote_copy(src_ref=x_ref, dst_ref=out_ref.at[4 * px + 2 * py + pc], send_sem=send_sems.at[k],
                                         recv_sem=recv_sems.at[k], device_id=(px, py, pc), device_id_type=MESH).wait_recv()
        for cp in copies:
            cp.wait_send()
        if with_sum:
            acc = out_ref[0]
            for j in range(1, N_DEV):
                acc = acc + out_ref[j]
            sum_ref[...] = acc

    out_shape = [S((N_DEV, r, w), f32)] + ([S((r, w), f32)] if with_sum else [])
    outs = pl.pallas_call(
        body, in_specs=[VMEM], out_specs=[VMEM] * len(out_shape), out_shape=out_shape,
        scratch_shapes=[pltpu.SemaphoreType.DMA((N_DEV - 1,)), pltpu.SemaphoreType.DMA((N_DEV - 1,))],
        compiler_params=pltpu.CompilerParams(vmem_limit_bytes=VMEM_LIMIT_BYTES), name=name)(x)
    return outs if with_sum else outs[0]


def _tile2d(R, W, max_rows):
    if R <= max_rows:
        return R, W
    fits = [r_ for r_ in range(16, max_rows + 1, 16) if R % r_ == 0]
    return (max(fits), W) if fits else (R, 256)


def add_own(g, r, core, name):
    _, _, R, W = g.shape
    rb, wb = _tile2d(R, W, 512)

    def body(core_ref, a_ref, b_ref, o_ref):
        o_ref[...] = (a_ref[...].astype(f32) + b_ref[...].astype(f32)).astype(bf16)

    blk = pl.BlockSpec((None, rb, wb), lambda k, i, j, core_ref: (k, i, j))
    gs = pltpu.PrefetchScalarGridSpec(
        num_scalar_prefetch=1, grid=(4, R // rb, W // wb),
        in_specs=[pl.BlockSpec((None, None, rb, wb), lambda k, i, j, core_ref: (k, core_ref[0], i, j)), blk], out_specs=blk)
    return pl.pallas_call(body, grid_spec=gs, out_shape=S((4, R, W), bf16),
                          compiler_params=_cparams("parallel", "parallel", "parallel"), name=name)(core, g, r)


def sum_adamw(recv, w, m, v, layer, name, into=None):
    _, R, W = recv.shape
    rb, wb = _tile2d(R, W, 256)
    bc1 = 1.0 - ADAM_B1 ** ADAM_STEP
    bc2 = 1.0 - ADAM_B2 ** ADAM_STEP
    n_into = 0 if into is None else 4

    def body(r_ref, w_ref, m_ref, v_ref, *refs):
        g_ref, d_ref, nm_ref, nv_ref = refs[n_into:]
        gv = r_ref[0].astype(f32)
        for k in range(1, 4):
            gv = gv + r_ref[k].astype(f32)
        m_new = ADAM_B1 * m_ref[...] + (1.0 - ADAM_B1) * gv
        v_new = ADAM_B2 * v_ref[...] + (1.0 - ADAM_B2) * (gv * gv)
        g_ref[...] = gv
        d_ref[...] = -ADAM_LR * ((m_new / bc1) / (jnp.sqrt(v_new / bc2) + ADAM_EPS) + ADAM_WD * w_ref[...])
        nm_ref[...] = m_new
        nv_ref[...] = v_new

    if layer is None:
        wblk = pl.BlockSpec((rb, wb), lambda i, j: (i, j))
        oshape = S((R, W), f32)
    else:
        wblk = pl.BlockSpec((None, rb, wb), lambda i, j: (layer, i, j))
        oshape = S(w.shape, f32)
    return pl.pallas_call(
        body, grid=(R // rb, W // wb),
        in_specs=[pl.BlockSpec((4, rb, wb), lambda i, j: (0, i, j)), wblk, wblk, wblk] + [ANY] * n_into,
        out_specs=[wblk] * 4, out_shape=[oshape] * 4, input_output_aliases={4 + k: k for k in range(n_into)},
        compiler_params=_cparams("parallel", "parallel"), name=name)(recv, w, m, v, *(into or ()))


def sum_rows(a, name):
    K, R, W = a.shape
    rb = _pick(R, (512, 256, 128, 64, 32, 16))

    def body(a_ref, o_ref):
        acc = a_ref[0].astype(f32)
        for k in range(1, K):
            acc = acc + a_ref[k].astype(f32)
        o_ref[...] = acc

    return pl.pallas_call(body, grid=(R // rb,), in_specs=[pl.BlockSpec((K, rb, W), lambda i: (0, i, 0))],
                          out_specs=pl.BlockSpec((rb, W), lambda i: (i, 0)), out_shape=S((R, W), f32),
                          compiler_params=_cparams("parallel"), name=name)(a)


DMA = pltpu.SemaphoreType.DMA


class GatherExchange:
    def __init__(self, arrays):
        self.arrays = list(arrays)
        self.na = len(self.arrays)
        self.out_shape = [S((N_DEV,) + a.shape, a.dtype) for a in self.arrays]
        self.scratch = [DMA((7 * self.na,)), DMA((7 * self.na,)), DMA((self.na,))]

    def ops(self, x_refs, out_refs, sems):
        send_sems, recv_sems, local_sems = sems
        na = self.na
        x, y, c = _me()
        me, sibling = (x, y, c), (x, y, 1 - c)
        chips = [(1 - x, y), (x, 1 - y), (1 - x, 1 - y)]

        def rows(a, px, py, pc):
            return out_refs[a].at[4 * px + 2 * py + pc]

        def copy(a, k, block, to, src=None):
            return pltpu.make_async_remote_copy(
                src_ref=rows(a, *block) if src is None else src, dst_ref=rows(a, *block),
                send_sem=send_sems.at[7 * a + k], recv_sem=recv_sems.at[7 * a + k], device_id=to, device_id_type=MESH)

        def local(a):
            return pltpu.make_async_copy(x_refs[a], rows(a, *me), local_sems.at[a])

        def first(a):
            return [copy(a, 0, me, sibling, src=x_refs[a])] + [copy(a, 1 + j, me, (*chip, c), src=x_refs[a])
                                                                for j, chip in enumerate(chips)]

        def start():
            for a in range(na):
                local(a).start()
                for cp in first(a):
                    cp.start()

        def mid():
            for a in range(na):
                for j, chip in enumerate(chips):
                    copy(a, 1 + j, (*chip, c), me).wait_recv()
                    copy(a, 4 + j, (*chip, c), sibling).start()

        def finish():
            for a in range(na):
                copy(a, 0, sibling, me).wait_recv()
                for j, chip in enumerate(chips):
                    copy(a, 4 + j, (*chip, 1 - c), me).wait_recv()
                for cp in first(a) + [copy(a, 4 + j, (*chip, c), sibling) for j, chip in enumerate(chips)]:
                    cp.wait_send()
                local(a).wait()

        return start, mid, finish


class SiblingExchange:
    def __init__(self, arrays):
        self.arrays = list(arrays)
        self.na = len(self.arrays)
        self.out_shape = [S((4,) + g.shape[2:], g.dtype) for g in self.arrays]
        self.scratch = [DMA((self.na,)), DMA((self.na,))]

    def ops(self, g_refs, out_refs, sems):
        send_sems, recv_sems = sems
        x, y, c = _me()

        def copy(a):
            return pltpu.make_async_remote_copy(src_ref=g_refs[a].at[:, 1 - c], dst_ref=out_refs[a],
                                                send_sem=send_sems.at[a], recv_sem=recv_sems.at[a],
                                                device_id=(x, y, 1 - c), device_id_type=MESH)

        def start():
            for a in range(self.na):
                copy(a).start()

        def finish():
            for a in range(self.na):
                copy(a).wait()

        return start, None, finish


class ChipsExchange:
    def __init__(self, arrays):
        self.arrays = list(arrays)
        self.na = len(self.arrays)
        self.out_shape = [S(p.shape, p.dtype) for p in self.arrays]
        self.scratch = [DMA((3 * self.na,)), DMA((3 * self.na,)), DMA((self.na,))]

    def ops(self, p_refs, out_refs, sems):
        send_sems, recv_sems, local_sems = sems
        x, y, c = _me()
        mine = 2 * x + y
        chips = [(1 - x, y), (x, 1 - y), (1 - x, 1 - y)]

        def local(a):
            return pltpu.make_async_copy(p_refs[a].at[mine], out_refs[a].at[mine], local_sems.at[a])

        def send(a, j):
            px, py = chips[j]
            return pltpu.make_async_remote_copy(src_ref=p_refs[a].at[2 * px + py], dst_ref=out_refs[a].at[mine],
                                                send_sem=send_sems.at[3 * a + j], recv_sem=recv_sems.at[3 * a + j],
                                                device_id=(px, py, c), device_id_type=MESH)

        def recv(a, j):
            px, py = chips[j]
            return pltpu.make_async_remote_copy(src_ref=p_refs[a].at[mine], dst_ref=out_refs[a].at[2 * px + py],
                                                send_sem=send_sems.at[3 * a + j], recv_sem=recv_sems.at[3 * a + j],
                                                device_id=(px, py, c), device_id_type=MESH)

        def start():
            for a in range(self.na):
                local(a).start()
                for j in range(3):
                    send(a, j).start()

        def finish():
            for a in range(self.na):
                for j in range(3):
                    recv(a, j).wait_recv()
                for j in range(3):
                    send(a, j).wait_send()
                local(a).wait()

        return start, None, finish


def exchange(ex, name):
    na = ex.na

    def body(*refs):
        start, mid, finish = ex.ops(refs[:na], refs[na:2 * na], refs[2 * na:])
        start()
        if mid is not None:
            mid()
        finish()

    return pl.pallas_call(body, in_specs=[ANY] * na, out_specs=[ANY] * na, out_shape=ex.out_shape,
                          scratch_shapes=ex.scratch, name=name)(*ex.arrays)


def _host_call(body, grid, in_specs, out_specs, out_shape, scratch_shapes, sem, name, args, hosted):
    if hosted is None:
        res = pl.pallas_call(body, grid=grid, in_specs=in_specs, out_specs=out_specs, out_shape=out_shape,
                             scratch_shapes=scratch_shapes, compiler_params=_cparams(*sem), name=name)(*args)
        return res, None
    n_in, n_out, n_sc, na = len(in_specs), len(out_shape), len(scratch_shapes), hosted.na
    nsteps = 1
    for g_ in grid:
        nsteps *= g_
    mid_step = (3 * nsteps) // 4
    i1 = n_in + na
    i2 = i1 + n_out
    i3 = i2 + na
    i4 = i3 + n_sc

    def wrapped(*refs):
        step = pl.program_id(0)
        for ax in range(1, len(grid)):
            step = step * grid[ax] + pl.program_id(ax)
        start, mid, finish = hosted.ops(refs[n_in:i1], refs[i2:i3], refs[i4:])
        pl.when(step == 0)(start)
        if mid is not None:
            pl.when(step == mid_step)(mid)
        body(*refs[:n_in], *refs[i1:i2], *refs[i3:i4])
        pl.when(step == nsteps - 1)(finish)

    res = pl.pallas_call(
        wrapped, grid=grid, in_specs=list(in_specs) + [ANY] * na, out_specs=list(out_specs) + [ANY] * na,
        out_shape=list(out_shape) + hosted.out_shape, scratch_shapes=list(scratch_shapes) + hosted.scratch,
        compiler_params=_cparams(*(("arbitrary",) * len(grid))), name=name)(*args, *hosted.arrays)
    return res[:n_out], res[n_out:]


PACK_ALIGN = 16 * PACK_W


def _pad_to(v, mult):
    n = v.shape[-1]
    extra = (-n) % mult
    if extra == 0:
        return v
    return jnp.concatenate([v, jnp.zeros(v.shape[:-1] + (extra,), v.dtype)], axis=-1)


def _f32_as_bf16_pairs(v):
    return lax.bitcast_convert_type(v.reshape(-1), bf16).reshape(-1)


def _bf16_pairs_as_f32(v):
    return lax.bitcast_convert_type(v.reshape(v.shape[:-1] + (v.shape[-1] // 2, 2)), f32)


def _col_shards(gw):
    lead = gw.shape[:-1]
    n = gw.shape[-1] // N_DEV
    t = gw.reshape(lead + (N_DEV, n))
    t = jnp.moveaxis(t, -2, 0)
    return t.reshape(N_DEV, -1)


def kernel(x, c, ctx, c_ctx, mod_w, mod_b, norm1_w, norm2_w, ssd_w_in, ssd_conv_w, ssd_conv_b, ssd_dt_bias, ssd_a_log, ssd_d, ssd_norm_w, ssd_w_out, conf_w_pw1, conf_b_pw1, conf_w_dw, conf_b_dw, conf_ln_w, conf_ln_b, conf_w_pw2, conf_b_pw2, ffn_w_up, ffn_conv_w, ffn_conv_b, ffn_w_down, final_norm_w, loss_target, m_c_ctx, m_mod_w, m_mod_b, m_norm1_w, m_norm2_w, m_ssd_w_in, m_ssd_conv_w, m_ssd_conv_b, m_ssd_dt_bias, m_ssd_a_log, m_ssd_d, m_ssd_norm_w, m_ssd_w_out, m_conf_w_pw1, m_conf_b_pw1, m_conf_w_dw, m_conf_b_dw, m_conf_ln_w, m_conf_ln_b, m_conf_w_pw2, m_conf_b_pw2, m_ffn_w_up, m_ffn_conv_w, m_ffn_conv_b, m_ffn_w_down, m_final_norm_w, v_c_ctx, v_mod_w, v_mod_b, v_norm1_w, v_norm2_w, v_ssd_w_in, v_ssd_conv_w, v_ssd_conv_b, v_ssd_dt_bias, v_ssd_a_log, v_ssd_d, v_ssd_norm_w, v_ssd_w_out, v_conf_w_pw1, v_conf_b_pw1, v_conf_w_dw, v_conf_b_dw, v_conf_ln_w, v_conf_ln_b, v_conf_w_pw2, v_conf_b_pw2, v_ffn_w_up, v_ffn_conv_w, v_ffn_conv_b, v_ffn_w_down, v_final_norm_w):
    mx, my, mc = _me()
    me = 4 * mx + 2 * my + mc
    L = x.shape[1]
    LC = ctx.shape[1]
    T = LC + L
    w_in_cols = ssd_w_in.shape[2] * N_DEV
    n_dt = w_in_cols - DI - CONVD

    small = [c[0], ssd_conv_w[0], conf_b_pw1[0], conf_w_dw[0], conf_b_dw[0], conf_ln_w[0], conf_ln_b[0], conf_b_pw2[0],
             ffn_conv_w]
    parts = [_f32_as_bf16_pairs(t) for t in small]
    sizes = [p.shape[0] for p in parts]
    small_flat = _pad_to(jnp.concatenate(parts), PACK_ALIGN).reshape(-1, PACK_W)
    w_in, small_g = exchange(GatherExchange([ssd_w_in[0].astype(bf16), small_flat]), "gather_first")
    gather_in_proj = GatherExchange([ssd_w_out[0].astype(bf16), conf_w_pw2[0].astype(bf16)])
    gather_in_conv = GatherExchange([ffn_w_down[0].astype(bf16), conf_w_pw1[0].astype(bf16)])
    gather_in_scan = GatherExchange([ffn_w_up[0].astype(bf16), ffn_w_up[1].astype(bf16)])
    gather_in_gate = GatherExchange([ffn_w_down[1].astype(bf16)])
    w_up, w_down = [None, None], [None, None]
    small_g = small_g.reshape(N_DEV, -1)
    offs = [0]
    for s_ in sizes:
        offs.append(offs[-1] + s_)
    sm = [_bf16_pairs_as_f32(small_g[:, offs[i]:offs[i + 1]]) for i in range(len(sizes))]

    def cols(pc, K):
        return jnp.moveaxis(pc.reshape(N_DEV, K, -1), 0, 1).reshape(K, -1)

    c_all = sm[0]
    conv_w5 = cols(sm[1], 5)
    b_pw1 = sm[2].reshape(1, 2 * D)
    w_dw = cols(sm[3], CONF_K)
    b_dw, ln_w, ln_b, b_pw2 = (sm[i].reshape(1, D) for i in (4, 5, 6, 7))
    fcw = sm[8].reshape(N_DEV, 2, 9, FH // N_DEV)
    ffn_cw = [cols(fcw[:, i].reshape(N_DEV, -1), 9) for i in range(2)]
    in_segs = (DI, CONVD, n_dt)
    up_segs = (FH, FH)
    pw1_segs = (D, D)

    c16 = jnp.concatenate([c_all, jnp.broadcast_to(c_ctx[None, :], (N_DEV, D))], axis=0)
    m_sh = mod_fwd(c16, mod_w, "mod_fwd")
    mod_cols = mod_w.shape[2]
    m_all = allgather_small(m_sh.reshape(2 * CROWS, mod_cols), "gather_mod")
    m_all = jnp.moveaxis(m_all.reshape(N_DEV, 2, CROWS, mod_cols), 0, 2).reshape(2, CROWS, 6 * D) + mod_b[:, None, :]
    m_lat = lax.dynamic_index_in_dim(m_all, me, axis=1, keepdims=False).reshape(2, 6, 1, D)
    m_ctx = m_all[:, N_DEV].reshape(2, 6, 1, D)
    zero_row = jnp.zeros((1, D), f32)

    def ffn_fwd(h, i, tag, hosted=None):
        a2 = modnorm_fwd(h, norm2_w[i][None], m_lat[i, 4][None], m_lat[i, 3][None], 0, f"ffn{tag}_norm")
        val, gate = smm_fwd(a2, w_up[i], None, up_segs, f"ffn{tag}_up")
        act, gs_, gvds, extra = ffn_gate_fwd(val, gate, ffn_cw[i], ffn_conv_b[i][None], f"ffn{tag}_gate", hosted)
        o2 = matmul(act, w_down[i], "nn", f32, f"ffn{tag}_down")
        h_new = resgate_fwd(h, o2, m_lat[i, 5], zero_row, f"ffn{tag}_res")
        return h_new, (a2, gate, gs_, gvds, act, o2), extra

    def ffn_bwd(dh, h, i, saved, tag):
        a2, gate, gs_, gvds, act, o2 = saved
        do2, dg2, _ = resgate_bwd(dh, o2, m_lat[i, 5], zero_row, f"ffn{tag}_res_bwd")
        g_down = matmul(act, do2, "tn", bf16, f"ffn{tag}_down_dw")
        dact = matmul(do2, w_down[i], "nt", bf16, f"ffn{tag}_down_dx")
        dval, dgate, dcw, dcb = ffn_gate_bwd(gate, gs_, gvds, ffn_cw[i], dact, f"ffn{tag}_gate_bwd")
        g_up = smm_dw(a2, [dval, dgate], FH // 4, up_segs, 2, True, f"ffn{tag}_up_dw")
        da2, _ = smm_dx([dval, dgate], w_up[i], None, up_segs, bf16, f"ffn{tag}_up_dx")
        dh_in, dn2, dsc2, dsh2 = modnorm_bwd(h, norm2_w[i][None], m_lat[i, 4][None], m_lat[i, 3][None], da2, dh, 0,
                                             f"ffn{tag}_norm_bwd")
        return dh_in, dict(w_up=g_up, w_down=g_down, conv_w=dcw, conv_b=dcb, norm2=dn2, sh2=dsh2[0], sc2=dsc2[0], g2=dg2)

    nctx = LC // Q
    h0 = jnp.concatenate([ctx[0], x[0]], axis=0)
    sc0 = jnp.stack([m_ctx[0, 1], m_lat[0, 1]])
    sh0 = jnp.stack([m_ctx[0, 0], m_lat[0, 0]])
    a0 = modnorm_fwd(h0, norm1_w[0][None], sc0, sh0, LC // TB, "ssd_norm")
    (z, xbc_pre, dt_raw), (w_out_g, w_pw2_g) = smm_fwd(a0, w_in, None, in_segs, "ssd_in", gather_in_proj)
    w_out = w_out_g.reshape(DI, D)
    w_pw2 = w_pw2_g.reshape(D, D)
    segs = ((0, LC), (LC, L))
    xbc, xbc_dsilu, (w_down0_g, w_pw1) = ssd_conv_fwd(xbc_pre, conv_w5, ssd_conv_b, segs, "ssd_conv", gather_in_conv)
    w_down[0] = w_down0_g.reshape(FH, D)
    dt4 = dt_raw[:, :n_dt].reshape(T, 2, G, HPG)
    dtc = jnp.transpose(dt4, (1, 2, 0, 3))
    dtr = jnp.transpose(dt4, (1, 2, 3, 0))
    bias3 = ssd_dt_bias[0].reshape(2, G, HPG)
    alog3 = ssd_a_log[0].reshape(2, G, HPG)
    bc_, br_ = bias3[:, :, None, :], bias3[:, :, :, None]
    alc, alr = alog3[:, :, None, :], alog3[:, :, :, None]
    (y2, s_in_all), (w_up[0], w_up[1]) = ssd_scan_fwd(xbc, dtc, dtr, bc_, br_, alc, alr, nctx, "ssd_scan", gather_in_scan)
    dexp = jnp.repeat(ssd_d[0], P)[None, :]
    yn = ssd_gate_fwd(y2, xbc, z, dexp, ssd_norm_w, LC // GTB, "ssd_gate")
    o_ssd = matmul(yn, w_out, "nn", f32, "ssd_out")
    hx = x[0]
    h1 = resgate_fwd(hx, o_ssd, m_lat[0, 2], zero_row, "ssd_res")
    h2, ffn0_saved, (w_down1_g,) = ffn_fwd(h1, 0, "0", gather_in_gate)
    w_down[1] = w_down1_g.reshape(FH, D)

    a1 = modnorm_fwd(h2, norm1_w[1][None], m_lat[1, 1][None], m_lat[1, 0][None], 0, "conf_norm")
    pa, pg = smm_fwd(a1, w_pw1, None, pw1_segs, "conf_pw1")
    dwc, _ = conf_glu_conv_fwd(pa, pg, b_pw1, w_dw, b_dw, "conf_conv")
    s1 = ln_silu_fwd(dwc, ln_w, ln_b, "conf_ln")
    o_conf = matmul(s1, w_pw2, "nn", f32, "conf_pw2")
    h3 = resgate_fwd(h2, o_conf, m_lat[1, 2], b_pw2, "conf_res")
    h4, ffn1_saved, _ = ffn_fwd(h3, 1, "1")

    loss_part, dh4, g_final = final_loss(h4, final_norm_w[None], loss_target[0], "loss_head")
    dh3, gf1 = ffn_bwd(dh4, h3, 1, ffn1_saved, "1")

    do_conf, dg1_1, g_b_pw2 = resgate_bwd(dh3, o_conf, m_lat[1, 2], b_pw2, "conf_res_bwd")
    g_pw2 = matmul(s1, do_conf, "tn", bf16, "conf_pw2_dw")
    ds1 = matmul(do_conf, w_pw2, "nt", bf16, "conf_pw2_dx")
    ddwc, g_ln_w, g_ln_b = ln_silu_bwd(dwc, ln_w, ln_b, ds1, "conf_ln_bwd")
    dpa, dpg, dba, dbg, g_w_dw, g_b_dw = conf_glu_conv_bwd(pa, pg, b_pw1, w_dw, ddwc, "conf_conv_bwd")
    g_b_pw1 = jnp.concatenate([dba, dbg], axis=1)
    g_pw1 = smm_dw(a1, [dpa, dpg], 2 * D // N_DEV, pw1_segs, 1, False, "conf_pw1_dw")
    da1, _ = smm_dx([dpa, dpg], w_pw1, None, pw1_segs, bf16, "conf_pw1_dx")
    dh2, g_n1_1, dsc1_1, dsh1_1 = modnorm_bwd(h2, norm1_w[1][None], m_lat[1, 1][None], m_lat[1, 0][None], da1, dh3, 0,
                                              "conf_norm_bwd")
    dh1, gf0 = ffn_bwd(dh2, h1, 0, ffn0_saved, "0")

    do_ssd, dg1_0, _ = resgate_bwd(dh1, o_ssd, m_lat[0, 2], zero_row, "ssd_res_bwd")
    g_w_out = matmul(yn, do_ssd, "tn", bf16, "ssd_out_dw")
    dyn = matmul(do_ssd, w_out, "nt", bf16, "ssd_out_dx")
    core = mc.reshape(1).astype(jnp.int32)

    def by_device(t):
        return t.reshape((4, 2, -1, t.shape[-1]))

    early = [by_device(t) for t in (gf1["w_up"], gf1["w_down"], g_pw2, g_pw1, gf0["w_up"], gf0["w_down"], g_w_out)]
    (dy, dx_skip, dz, g_dexp, g_ssd_norm), early_sib = ssd_gate_bwd(
        y2, xbc, z, dexp, ssd_norm_w, dyn, LC // GTB, "ssd_gate_bwd", SiblingExchange(early))
    early_part = [add_own(t, r_, core, f"reduce_add{i}") for i, (t, r_) in enumerate(zip(early, early_sib))]
    (dxbc2, ddtc, ddtr, dbc, dbr, dalc, dalr), early_red = ssd_scan_bwd(
        xbc, dtc, dtr, bc_, br_, alc, alr, s_in_all, dy, nctx, "ssd_scan_bwd", ChipsExchange(early_part))
    ddt = (jnp.transpose(ddtc, (2, 0, 1, 3)) + jnp.transpose(ddtr, (3, 0, 1, 2))).reshape(T, n_dt)
    g_dt_bias = (dbc[:, :, 0, :] + dbr[:, :, :, 0]).reshape(2, NH_SSD)
    g_a_log = (dalc[:, :, 0, :] + dalr[:, :, :, 0]).reshape(2, NH_SSD)
    g_ssd_d = g_dexp[0, :NH_SSD]
    du, g_conv_w5, g_conv_b5 = ssd_conv_bwd(xbc_pre, conv_w5, xbc_dsilu, dxbc2, dx_skip, segs, "ssd_conv_bwd")
    ddt_p = _pad_to(ddt, 128).astype(bf16)
    g_w_in = smm_dw(a0, [dz, du, ddt_p], w_in.shape[-1], in_segs, 2, True, "ssd_in_dw")
    g_ffn_cw = jnp.stack([gf0["conv_w"], gf1["conv_w"]])
    small_shards = [_col_shards(t) for t in (g_conv_w5, g_b_pw1, g_w_dw, g_b_dw, g_ln_w, g_ln_b, g_b_pw2, g_ffn_cw)]
    gsizes = [s_.shape[1] for s_ in small_shards]
    g_small = _pad_to(jnp.concatenate(small_shards, axis=1), PACK_ALIGN).astype(bf16)
    late = [by_device(g_w_in), by_device(g_small.reshape(N_DEV, -1, PACK_W))]
    da0, late_sib = smm_dx([dz, du, ddt_p], w_in, None, in_segs, f32, "ssd_in_dx", SiblingExchange(late))
    late_part = [add_own(t, r_, core, f"reduce_add_late{i}") for i, (t, r_) in enumerate(zip(late, late_sib))]
    late_red = exchange(ChipsExchange(late_part), "reduce_chips_late")
    dh0, g_n1_0, dsc1_0, dsh1_0 = modnorm_bwd(h0, norm1_w[0][None], sc0, sh0, da0, dh1, LC // TB, "ssd_norm_bwd")
    grad_x = dh0[None]

    zeros_d = jnp.zeros((1, D), f32)
    dm_lat = jnp.stack([
        jnp.concatenate([dsh1_0[1], dsc1_0[1], dg1_0, gf0["sh2"], gf0["sc2"], gf0["g2"]], axis=1),
        jnp.concatenate([dsh1_1[0], dsc1_1[0], dg1_1, gf1["sh2"], gf1["sc2"], gf1["g2"]], axis=1)])
    dm_ctx = jnp.stack([
        jnp.concatenate([dsh1_0[0], dsc1_0[0]] + [zeros_d] * 4, axis=1), jnp.zeros((1, 6 * D), f32)])
    dm_mine = jnp.concatenate([dm_lat.reshape(2, 6 * D), dm_ctx.reshape(2, 6 * D),
                               jnp.zeros((4, 6 * D), f32)], axis=0)
    dm_g = allgather_small(dm_mine, "gather_dmod")
    dm_all = jnp.concatenate([jnp.moveaxis(dm_g[:, 0:2], 0, 1), jnp.moveaxis(dm_g[:, 2:4], 0, 1)], axis=1)
    dm_sh = lax.dynamic_slice_in_dim(dm_all, me * mod_cols, mod_cols, axis=2)
    g_mod_w, g_cctx_part, g_mod_b = mod_bwd(c16, mod_w, dm_sh, dm_all, "mod_bwd")

    rep = [jnp.stack([g_n1_0[0], g_n1_1[0]]), jnp.stack([gf0["norm2"][0], gf1["norm2"][0]]), g_conv_b5, g_dt_bias, g_a_log,
           g_ssd_d, g_ssd_norm, jnp.stack([gf0["conv_b"][0], gf1["conv_b"][0]]), g_final, g_cctx_part, loss_part[:, :1]]
    rep_sizes = [r_.size for r_ in rep]
    rep_flat = _pad_to(jnp.concatenate([r_.reshape(-1) for r_ in rep]), 8 * PACK_W).reshape(-1, PACK_W)
    _, rep_sum = allgather_small(rep_flat, "reduce_replicated", with_sum=True)
    rep_sum = rep_sum.reshape(-1)
    roffs = [0]
    for s_ in rep_sizes:
        roffs.append(roffs[-1] + s_)
    rp = [rep_sum[roffs[i]:roffs[i + 1]] for i in range(len(rep_sizes))]
    loss = rp[10].reshape(())

    r_up1, r_down1, r_pw2, r_pw1, r_up0, r_down0, r_out = early_red
    r_in, r_small = late_red
    from_chips = [r_in, r_up0, r_up1, r_pw1, r_out, r_down0, r_down1, r_pw2]
    g_flat = sum_rows(r_small, "reduce_sum_small").reshape(-1)
    goffs = [0]
    for s_ in gsizes:
        goffs.append(goffs[-1] + s_)
    gs = [g_flat[goffs[i]:goffs[i + 1]] for i in range(len(gsizes))]

    big = {}
    def tr(t):
        return jnp.swapaxes(t, -1, -2)

    w_in_res = sum_adamw(from_chips[0], tr(ssd_w_in[0]), tr(m_ssd_w_in[0]), tr(v_ssd_w_in[0]), None, "adamw_ssd_w_in")
    big["ssd_w_in"] = tuple(tr(t) for t in w_in_res)
    up_t, m_up_t, v_up_t = tr(ffn_w_up), tr(m_ffn_w_up), tr(v_ffn_w_up)
    up0 = sum_adamw(from_chips[1], up_t, m_up_t, v_up_t, 0, "adamw_ffn_w_up0")
    big["ffn_w_up"] = tuple(tr(t) for t in sum_adamw(from_chips[2], up_t, m_up_t, v_up_t, 1, "adamw_ffn_w_up1", into=up0))
    big["conf_w_pw1"] = sum_adamw(from_chips[3], conf_w_pw1[0], m_conf_w_pw1[0], v_conf_w_pw1[0], None, "adamw_conf_w_pw1")
    big["ssd_w_out"] = sum_adamw(from_chips[4], ssd_w_out[0], m_ssd_w_out[0], v_ssd_w_out[0], None, "adamw_ssd_w_out")
    dn0 = sum_adamw(from_chips[5], ffn_w_down, m_ffn_w_down, v_ffn_w_down, 0, "adamw_ffn_w_down0")
    big["ffn_w_down"] = sum_adamw(from_chips[6], ffn_w_down, m_ffn_w_down, v_ffn_w_down, 1, "adamw_ffn_w_down1", into=dn0)
    big["conf_w_pw2"] = sum_adamw(from_chips[7], conf_w_pw2[0], m_conf_w_pw2[0], v_conf_w_pw2[0], None, "adamw_conf_w_pw2")
    grads = {
        "c_ctx": rp[9], "mod_w": g_mod_w, "mod_b": g_mod_b, "norm1_w": rp[0], "norm2_w": rp[1],
        "ssd_conv_w": gs[0], "ssd_conv_b": rp[2], "ssd_dt_bias": rp[3], "ssd_a_log": rp[4], "ssd_d": rp[5],
        "ssd_norm_w": rp[6], "conf_b_pw1": gs[1], "conf_w_dw": gs[2],
        "conf_b_dw": gs[3], "conf_ln_w": gs[4], "conf_ln_b": gs[5], "conf_b_pw2": gs[6],
        "ffn_conv_w": gs[7], "ffn_conv_b": rp[7], "final_norm_w": rp[8],
    }
    weights = dict(c_ctx=c_ctx, mod_w=mod_w, mod_b=mod_b, norm1_w=norm1_w, norm2_w=norm2_w, ssd_w_in=ssd_w_in, ssd_conv_w=ssd_conv_w, ssd_conv_b=ssd_conv_b, ssd_dt_bias=ssd_dt_bias, ssd_a_log=ssd_a_log, ssd_d=ssd_d, ssd_norm_w=ssd_norm_w, ssd_w_out=ssd_w_out, conf_w_pw1=conf_w_pw1, conf_b_pw1=conf_b_pw1, conf_w_dw=conf_w_dw, conf_b_dw=conf_b_dw, conf_ln_w=conf_ln_w, conf_ln_b=conf_ln_b, conf_w_pw2=conf_w_pw2, conf_b_pw2=conf_b_pw2, ffn_w_up=ffn_w_up, ffn_conv_w=ffn_conv_w, ffn_conv_b=ffn_conv_b, ffn_w_down=ffn_w_down, final_norm_w=final_norm_w)
    m_in = dict(c_ctx=m_c_ctx, mod_w=m_mod_w, mod_b=m_mod_b, norm1_w=m_norm1_w, norm2_w=m_norm2_w, ssd_w_in=m_ssd_w_in, ssd_conv_w=m_ssd_conv_w, ssd_conv_b=m_ssd_conv_b, ssd_dt_bias=m_ssd_dt_bias, ssd_a_log=m_ssd_a_log, ssd_d=m_ssd_d, ssd_norm_w=m_ssd_norm_w, ssd_w_out=m_ssd_w_out, conf_w_pw1=m_conf_w_pw1, conf_b_pw1=m_conf_b_pw1, conf_w_dw=m_conf_w_dw, conf_b_dw=m_conf_b_dw, conf_ln_w=m_conf_ln_w, conf_ln_b=m_conf_ln_b, conf_w_pw2=m_conf_w_pw2, conf_b_pw2=m_conf_b_pw2, ffn_w_up=m_ffn_w_up, ffn_conv_w=m_ffn_conv_w, ffn_conv_b=m_ffn_conv_b, ffn_w_down=m_ffn_w_down, final_norm_w=m_final_norm_w)
    v_in = dict(c_ctx=v_c_ctx, mod_w=v_mod_w, mod_b=v_mod_b, norm1_w=v_norm1_w, norm2_w=v_norm2_w, ssd_w_in=v_ssd_w_in, ssd_conv_w=v_ssd_conv_w, ssd_conv_b=v_ssd_conv_b, ssd_dt_bias=v_ssd_dt_bias, ssd_a_log=v_ssd_a_log, ssd_d=v_ssd_d, ssd_norm_w=v_ssd_norm_w, ssd_w_out=v_ssd_w_out, conf_w_pw1=v_conf_w_pw1, conf_b_pw1=v_conf_b_pw1, conf_w_dw=v_conf_w_dw, conf_b_dw=v_conf_b_dw, conf_ln_w=v_conf_ln_w, conf_ln_b=v_conf_ln_b, conf_w_pw2=v_conf_w_pw2, conf_b_pw2=v_conf_b_pw2, ffn_w_up=v_ffn_w_up, ffn_conv_w=v_ffn_conv_w, ffn_conv_b=v_ffn_conv_b, ffn_w_down=v_ffn_w_down, final_norm_w=v_final_norm_w)

    out_g, out_d, out_m, out_v = [], [], [], []
    for name_, w_ in weights.items():
        shape = w_.shape
        if name_ in big:
            for lst, t in zip((out_g, out_d, out_m, out_v), big[name_]):
                lst.append(t.reshape(shape))
            continue
        cols2 = shape[-1] if len(shape) > 1 else shape[0]
        g2 = grads[name_].reshape(-1, cols2)
        d_, nm_, nv_ = adamw(w_.reshape(-1, cols2), g2, m_in[name_].reshape(-1, cols2), v_in[name_].reshape(-1, cols2),
                             f"adamw_{name_}")
        out_g.append(g2.reshape(shape))
        out_d.append(d_.reshape(shape))
        out_m.append(nm_.reshape(shape))
        out_v.append(nv_.reshape(shape))
    return (loss, grad_x, *out_g, *out_d, *out_m, *out_v)
```

```python
import functools

import jax
import jax.numpy as jnp
from jax import lax
from jax.experimental import pallas as pl
from jax.experimental.pallas import tpu as pltpu

f32 = jnp.float32
bf16 = jnp.bfloat16
HI = lax.Precision.HIGHEST
S = jax.ShapeDtypeStruct
MESH = pl.DeviceIdType.MESH
ANY = pl.BlockSpec(memory_space=pl.ANY)
VMEM = pl.BlockSpec(memory_space=pltpu.VMEM)

N_DEV = 8
D = 1024
DI = 2048
CONVD = 4096
FH = 2816
GRID_W = 64
Q = 128
HPG = 4
P = 64
N = 128
G = 8
GW = HPG * P
NH_SSD = G * HPG
EPS = 1e-6
ADAM_LR, ADAM_B1, ADAM_B2, ADAM_EPS, ADAM_WD, ADAM_STEP = 0.001, 0.9, 0.999, 1e-08, 0.01, 10
VMEM_LIMIT_BYTES = 56 * 1024 * 1024
PACK_W = 1024
TB = 256


def _cparams(*sem):
    return pltpu.CompilerParams(dimension_semantics=sem, vmem_limit_bytes=VMEM_LIMIT_BYTES)


def _pick(n, prefs):
    for p in prefs:
        if n % p == 0:
            return p
    return n


def _sigmoid(x):
    return 1.0 / (1.0 + jnp.exp(-x))


def _softplus(x):
    return jnp.maximum(x, 0.0) + jnp.log(1.0 + jnp.exp(-jnp.abs(x)))


def matmul(a, b, mode, out_dtype, name):
    if mode == "nn":
        (M, K), (_, Nn) = a.shape, b.shape
        bm, bn, bk = _pick(M, (512, 384, 256, 128)), Nn, K
    elif mode == "tn":
        (K, M), (_, Nn) = a.shape, b.shape
        bm, bn, bk = M, Nn, _pick(K, (256, 128))
    else:
        (M, K), (Nn, _) = a.shape, b.shape
        bm, bn, bk = _pick(M, (512, 384, 256, 128)), Nn, K
    nk = K // bk
    dims = {"nn": (((1,), (0,)), ((), ())), "tn": (((0,), (0,)), ((), ())), "nt": (((1,), (1,)), ((), ()))}[mode]

    def body(a_ref, b_ref, o_ref, acc_ref):
        k = pl.program_id(2)

        @pl.when(k == 0)
        def _():
            acc_ref[...] = jnp.zeros_like(acc_ref)

        acc_ref[...] += lax.dot_general(a_ref[...].astype(bf16), b_ref[...].astype(bf16), dims,
                                        preferred_element_type=f32)

        @pl.when(k == nk - 1)
        def _():
            o_ref[...] = acc_ref[...].astype(out_dtype)

    if mode == "nn":
        a_spec = pl.BlockSpec((bm, bk), lambda i, j, k: (i, k))
        b_spec = pl.BlockSpec((bk, bn), lambda i, j, k: (k, j))
    elif mode == "tn":
        a_spec = pl.BlockSpec((bk, bm), lambda i, j, k: (k, i))
        b_spec = pl.BlockSpec((bk, bn), lambda i, j, k: (k, j))
    else:
        a_spec = pl.BlockSpec((bm, bk), lambda i, j, k: (i, k))
        b_spec = pl.BlockSpec((bn, bk), lambda i, j, k: (j, k))
    return pl.pallas_call(
        body, grid=(M // bm, Nn // bn, nk), in_specs=[a_spec, b_spec],
        out_specs=pl.BlockSpec((bm, bn), lambda i, j, k: (i, j)),
        out_shape=S((M, Nn), out_dtype), scratch_shapes=[pltpu.VMEM((bm, bn), f32)],
        compiler_params=_cparams("parallel", "parallel", "arbitrary"), name=name,
    )(a, b)


SMM_BM = 256
SMM_ROWS = (256,)


def _shard_pieces(seg_widths, n):
    bounds = [0]
    for sw in seg_widths:
        bounds.append(bounds[-1] + sw)
    assert bounds[-1] == N_DEV * n, (seg_widths, n)
    out = []
    for j in range(N_DEV):
        lo, hi = j * n, (j + 1) * n
        pcs = []
        for si in range(len(seg_widths)):
            a, b = max(lo, bounds[si]), min(hi, bounds[si + 1])
            if a < b:
                pcs.append((si, a - bounds[si], a - lo, b - a))
        out.append(pcs)
    return out


def _w_spec(w, layer):
    if layer is None:
        return pl.BlockSpec(w.shape, lambda *idx: (0, 0, 0))
    return pl.BlockSpec((N_DEV, None) + w.shape[2:], lambda *idx: (0, layer, 0, 0))


def smm_fwd(a, w, layer, seg_widths, name, hosted=None):
    M, K = a.shape
    n = w.shape[-1]
    pieces = _shard_pieces(seg_widths, n)
    padded = [sw + (-sw) % 128 for sw in seg_widths]
    bm = _pick(M, SMM_ROWS)

    def body(a_ref, w_ref, *o_refs):
        av = a_ref[...]
        for si, sw in enumerate(seg_widths):
            if padded[si] != sw:
                o_refs[si][:, pl.ds(padded[si] - 128, 128)] = jnp.zeros((bm, 128), f32)
        for j in range(N_DEV):
            for si, soff, woff, wd in pieces[j]:
                o_refs[si][:, pl.ds(soff, wd)] = jnp.dot(av, w_ref[j, :, pl.ds(woff, wd)], preferred_element_type=f32)

    outs, extra = _host_call(
        body, (M // bm,), [pl.BlockSpec((bm, K), lambda i: (i, 0)), _w_spec(w, layer)],
        [pl.BlockSpec((bm, pw), lambda i: (i, 0)) for pw in padded], [S((M, pw), f32) for pw in padded], [],
        ("parallel",), name, (a, w), hosted)
    return outs if hosted is None else (outs, extra)


def smm_dx(d_segs, w, layer, seg_widths, out_dtype, name, hosted=None):
    M = d_segs[0].shape[0]
    K, n = w.shape[-2], w.shape[-1]
    pieces = _shard_pieces(seg_widths, n)
    ns = len(d_segs)
    bm = _pick(M, SMM_ROWS)

    def body(*refs):
        d_refs, w_ref, o_ref = refs[:ns], refs[ns], refs[ns + 1]
        acc = jnp.zeros((bm, K), f32)
        for j in range(N_DEV):
            for si, soff, woff, wd in pieces[j]:
                acc = acc + lax.dot_general(d_refs[si][:, pl.ds(soff, wd)], w_ref[j, :, pl.ds(woff, wd)],
                                            (((1,), (1,)), ((), ())), preferred_element_type=f32)
        o_ref[...] = acc.astype(out_dtype)

    (out,), extra = _host_call(
        body, (M // bm,),
        [pl.BlockSpec((bm, d.shape[1]), lambda i: (i, 0)) for d in d_segs] + [_w_spec(w, layer)],
        [pl.BlockSpec((bm, K), lambda i: (i, 0))], [S((M, K), out_dtype)], [], ("parallel",), name,
        (*d_segs, w), hosted)
    return out, extra


def smm_dw(a, d_segs, n, seg_widths, ngrp, transposed, name):
    M, K = a.shape
    pieces = _shard_pieces(seg_widths, n)
    per = N_DEV // ngrp
    nI = M // SMM_BM
    ns = len(d_segs)
    shard = (n, K) if transposed else (K, n)

    def body(*refs):
        a_ref, d_refs, o_ref, acc_ref = refs[0], refs[1:1 + ns], refs[1 + ns], refs[2 + ns]
        grp = pl.program_id(0)
        i = pl.program_id(1)

        @pl.when(i == 0)
        def _():
            acc_ref[...] = jnp.zeros_like(acc_ref)

        av = a_ref[...]
        for gs in range(ngrp):
            def one_group(gs=gs):
                for jj in range(per):
                    for si, soff, woff, wd in pieces[gs * per + jj]:
                        dv = d_refs[si][:, pl.ds(soff, wd)]
                        if transposed:
                            acc_ref[jj, pl.ds(woff, wd), :] += lax.dot_general(
                                dv, av, (((0,), (0,)), ((), ())), preferred_element_type=f32)
                        else:
                            acc_ref[jj, :, pl.ds(woff, wd)] += lax.dot_general(
                                av, dv, (((0,), (0,)), ((), ())), preferred_element_type=f32)
            pl.when(grp == gs)(one_group)

        @pl.when(i == nI - 1)
        def _():
            o_ref[...] = acc_ref[...].astype(bf16)

    return pl.pallas_call(
        body, grid=(ngrp, nI),
        in_specs=[pl.BlockSpec((SMM_BM, K), lambda g, i: (i, 0))]
        + [pl.BlockSpec((SMM_BM, d.shape[1]), lambda g, i: (i, 0)) for d in d_segs],
        out_specs=pl.BlockSpec((per,) + shard, lambda g, i: (g, 0, 0)), out_shape=S((N_DEV,) + shard, bf16),
        scratch_shapes=[pltpu.VMEM((per,) + shard, f32)],
        compiler_params=_cparams("arbitrary", "arbitrary"), name=name)(a, *d_segs)


def _modnorm_f(h, w, sc, sh):
    y = h * lax.rsqrt(jnp.mean(h * h, axis=-1, keepdims=True) + EPS)
    return (y * w) * (1.0 + sc) + sh


def _kind_specs(nctxb):
    if nctxb > 0:
        return pl.BlockSpec((None, 1, D), lambda i: (jnp.where(i < nctxb, 0, 1), 0, 0))
    return pl.BlockSpec((None, 1, D), lambda i: (0, 0, 0))


def _two_part_specs(nctxb):
    return (pl.BlockSpec((TB, D), lambda i: (jnp.minimum(i, nctxb - 1), 0)),
            pl.BlockSpec((TB, D), lambda i: (jnp.maximum(i - nctxb, 0), 0)))


def modnorm_fwd(h, w, sc, sh, nctxb, name, ctx=None):
    if ctx is None:
        T = h.shape[0]

        def body(h_ref, w_ref, sc_ref, sh_ref, o_ref):
            o_ref[...] = _modnorm_f(h_ref[...], w_ref[...], sc_ref[...], sh_ref[...]).astype(bf16)

        hspecs, hargs = [pl.BlockSpec((TB, D), lambda i: (i, 0))], (h,)
    else:
        T = h.shape[0] + ctx.shape[0]

        def body(c_ref, h_ref, w_ref, sc_ref, sh_ref, o_ref):
            hv = jnp.where(pl.program_id(0) < nctxb, c_ref[...], h_ref[...])
            o_ref[...] = _modnorm_f(hv, w_ref[...], sc_ref[...], sh_ref[...]).astype(bf16)

        hspecs, hargs = list(_two_part_specs(nctxb)), (ctx, h)
    row = pl.BlockSpec((1, D), lambda i: (0, 0))
    ks = _kind_specs(nctxb)
    return pl.pallas_call(body, grid=(T // TB,), in_specs=hspecs + [row, ks, ks],
                          out_specs=pl.BlockSpec((TB, D), lambda i: (i, 0)), out_shape=S((T, D), bf16),
                          compiler_params=_cparams("parallel"), name=name)(*hargs, w, sc, sh)


def modnorm_bwd(h, w, sc, sh, da, dres, nctxb, name, ctx=None):
    T = h.shape[0] + (0 if ctx is None else ctx.shape[0])
    kinds = sc.shape[0]
    nh = 1 if ctx is None else 2

    def body(*refs):
        w_ref, sc_ref, sh_ref, da_ref, dres_ref, dh_ref, dw_ref, dsc_ref, dsh_ref = refs[nh:]
        i = pl.program_id(0)
        hv = refs[0][...] if ctx is None else jnp.where(i < nctxb, refs[0][...], refs[1][...])
        _, vjp = jax.vjp(_modnorm_f, hv, w_ref[...], sc_ref[...], sh_ref[...])
        dh, dw, dsc, dsh = vjp(da_ref[...].astype(f32))
        dh_ref[...] = dres_ref[...] + dh

        @pl.when(i == 0)
        def _():
            dw_ref[...] = jnp.zeros_like(dw_ref)

        @pl.when((i == 0) | (i == nctxb))
        def _():
            dsc_ref[...] = jnp.zeros_like(dsc_ref)
            dsh_ref[...] = jnp.zeros_like(dsh_ref)

        dw_ref[...] += dw
        dsc_ref[...] += dsc
        dsh_ref[...] += dsh

    blk = pl.BlockSpec((TB, D), lambda i: (i, 0))
    lat = pl.BlockSpec((TB, D), lambda i: (jnp.maximum(i - nctxb, 0), 0))
    row = pl.BlockSpec((1, D), lambda i: (0, 0))
    ks = _kind_specs(nctxb)
    hspecs, hargs = ([blk], (h,)) if ctx is None else (list(_two_part_specs(nctxb)), (ctx, h))
    return pl.pallas_call(
        body, grid=(T // TB,), in_specs=hspecs + [row, ks, ks, blk, lat], out_specs=[lat, row, ks, ks],
        out_shape=[S((T - nctxb * TB, D), f32), S((1, D), f32), S((kinds, 1, D), f32), S((kinds, 1, D), f32)],
        compiler_params=_cparams("arbitrary"), name=name)(*hargs, w, sc, sh, da, dres)


def resgate_fwd(h, o, g, b, name):
    T = h.shape[0]

    def body(h_ref, o_ref, g_ref, b_ref, out_ref):
        out_ref[...] = h_ref[...] + g_ref[...] * (o_ref[...] + b_ref[...])

    blk = pl.BlockSpec((TB, D), lambda i: (i, 0))
    row = pl.BlockSpec((1, D), lambda i: (0, 0))
    return pl.pallas_call(body, grid=(T // TB,), in_specs=[blk, blk, row, row], out_specs=blk,
                          out_shape=S((T, D), f32), compiler_params=_cparams("parallel"), name=name)(h, o, g, b)


def resgate_bwd(dh, o, g, b, name):
    T = dh.shape[0]

    def body(dh_ref, o_ref, g_ref, b_ref, do_ref, dg_ref, db_ref):
        i = pl.program_id(0)

        @pl.when(i == 0)
        def _():
            dg_ref[...] = jnp.zeros_like(dg_ref)
            db_ref[...] = jnp.zeros_like(db_ref)

        dh = dh_ref[...]
        do = g_ref[...] * dh
        do_ref[...] = do.astype(bf16)
        dg_ref[...] += jnp.sum(dh * (o_ref[...] + b_ref[...]), axis=0, keepdims=True)
        db_ref[...] += jnp.sum(do, axis=0, keepdims=True)

    blk = pl.BlockSpec((TB, D), lambda i: (i, 0))
    row = pl.BlockSpec((1, D), lambda i: (0, 0))
    return pl.pallas_call(body, grid=(T // TB,), in_specs=[blk, blk, row, row], out_specs=[blk, row, row],
                          out_shape=[S((T, D), bf16), S((1, D), f32), S((1, D), f32)],
                          compiler_params=_cparams("arbitrary"), name=name)(dh, o, g, b)


def final_loss(h, w, tgt, name):
    T = h.shape[0]

    def f(hv, wv, tv):
        y = (hv * lax.rsqrt(jnp.mean(hv * hv, axis=-1, keepdims=True) + EPS)) * wv
        e = y - tv
        return 0.5 * jnp.sum(jnp.sum(e * e, axis=-1, keepdims=True), axis=0, keepdims=True) * (1.0 / D)

    def body(h_ref, w_ref, t_ref, loss_ref, dh_ref, dw_ref):
        i = pl.program_id(0)
        tv = t_ref[...]
        val, vjp = jax.vjp(lambda a, b_: f(a, b_, tv), h_ref[...], w_ref[...])
        dh, dw = vjp(jnp.ones((1, 1), f32))
        dh_ref[...] = dh

        @pl.when(i == 0)
        def _():
            loss_ref[...] = jnp.zeros_like(loss_ref)
            dw_ref[...] = jnp.zeros_like(dw_ref)

        loss_ref[...] += jnp.broadcast_to(val, (1, 128))
        dw_ref[...] += dw

    blk = pl.BlockSpec((TB, D), lambda i: (i, 0))
    row = pl.BlockSpec((1, D), lambda i: (0, 0))
    return pl.pallas_call(body, grid=(T // TB,), in_specs=[blk, row, blk],
                          out_specs=[pl.BlockSpec((1, 128), lambda i: (0, 0)), blk, row],
                          out_shape=[S((1, 128), f32), S((T, D), f32), S((1, D), f32)],
                          compiler_params=_cparams("arbitrary"), name=name)(h, w, tgt)


CB = 256
RT = 32
RTB = 16


def _fold8(t):
    acc = t[0:8]
    for k in range(1, t.shape[0] // 8):
        acc = acc + t[8 * k:8 * (k + 1)]
    return acc


def _rows(start, off=0, rt=RT):
    return pl.ds(pl.multiple_of(start + off, 8), rt)


def _rowsb(start, off=0):
    return _rows(start, off, RTB)


def _zero_rows(ref, start, n):
    ref[pl.ds(start, n), :] = jnp.zeros((n, ref.shape[1]), f32)


K5, HALF5, PAD5 = 5, 2, 8


def _shift_copies5(base_ref, s_ref, ln, sign):
    for k in range(K5):
        s_ref[k, pl.ds(0, ln), :] = base_ref[pl.ds(PAD5 + sign * (k - HALF5), ln), :]


def ssd_conv_fwd(u, w, b, segs, name, hosted=None):
    T = u.shape[0]
    maxlen = max(ln for _, ln in segs)

    def body(u_ref, w_ref, b_ref, o_ref, ds_ref, base_ref, s_ref):
        wv = [w_ref[pl.ds(k, 1), :] for k in range(K5)]
        bv = b_ref[...]
        for s0, ln in segs:
            _zero_rows(base_ref, 0, PAD5)
            _zero_rows(base_ref, PAD5 + ln, PAD5)
            base_ref[pl.ds(PAD5, ln), :] = u_ref[pl.ds(s0, ln), :]
            _shift_copies5(base_ref, s_ref, ln, 1)

            def tile(i, carry):
                r = i * RT
                acc = jnp.broadcast_to(bv, (RT, CB))
                for k in range(K5):
                    acc = acc + s_ref[k, _rows(r), :] * wv[k]
                sg = _sigmoid(acc)
                o_ref[_rows(r, s0), :] = acc * sg
                ds_ref[_rows(r, s0), :] = sg * (1.0 + acc * (1.0 - sg))
                return carry

            lax.fori_loop(0, ln // RT, tile, 0, unroll=2)

    cblk = pl.BlockSpec((T, CB), lambda j: (0, j))
    (out, dsilu), extra = _host_call(
        body, (CONVD // CB,),
        [cblk, pl.BlockSpec((K5, CB), lambda j: (0, j)), pl.BlockSpec((1, CB), lambda j: (0, j))],
        [cblk, cblk], [S((T, CONVD), f32), S((T, CONVD), f32)],
        [pltpu.VMEM((maxlen + 2 * PAD5, CB), f32), pltpu.VMEM((K5, maxlen, CB), f32)],
        ("parallel",), name, (u, w, b), hosted)
    return out, dsilu, extra


def ssd_conv_bwd(proj, w, dsilu, dy2, dyskip, dexp, segs, name):
    T = proj.shape[0]
    maxlen = max(ln for _, ln in segs)
    nskip = DI // CB

    def body(u_ref, w_ref, ds_ref, dya_ref, dyb_ref, dsk_ref, dexp_ref, du_ref, dw_ref, db_ref, base_ref, s_ref):
        wv = [w_ref[pl.ds(k, 1), :] for k in range(K5)]
        has_skip = (pl.program_id(0) < nskip).astype(f32) * dexp_ref[...]
        acc8 = tuple(jnp.zeros((8, CB), f32) for _ in range(K5 + 1))
        for s0, ln in segs:
            _zero_rows(base_ref, 0, PAD5)
            _zero_rows(base_ref, PAD5 + ln, PAD5)
            base_ref[pl.ds(PAD5, ln), :] = u_ref[pl.ds(s0, ln), :]
            _shift_copies5(base_ref, s_ref, ln, 1)

            def tile1(i, carry):
                r = i * RTB
                dy = dya_ref[_rowsb(r, s0), :] + dyb_ref[_rowsb(r, s0), :] + has_skip * dsk_ref[_rowsb(r, s0), :]
                dpre = dy * ds_ref[_rowsb(r, s0), :]
                base_ref[_rowsb(r, PAD5), :] = dpre
                new = [carry[k] + _fold8(dpre * s_ref[k, _rowsb(r), :]) for k in range(K5)]
                new.append(carry[K5] + _fold8(dpre))
                return tuple(new)

            acc8 = lax.fori_loop(0, ln // RTB, tile1, acc8, unroll=2)
            _shift_copies5(base_ref, s_ref, ln, -1)

            def tile2(i, carry):
                r = i * RTB
                du = jnp.zeros((RTB, CB), f32)
                for k in range(K5):
                    du = du + s_ref[k, _rowsb(r), :] * wv[k]
                du_ref[_rowsb(r, s0), :] = du.astype(bf16)
                return carry

            lax.fori_loop(0, ln // RTB, tile2, 0, unroll=4)
        for k in range(K5):
            dw_ref[pl.ds(k, 1), :] = jnp.sum(acc8[k], axis=0, keepdims=True)
        db_ref[...] = jnp.sum(acc8[K5], axis=0, keepdims=True)

    cblk = pl.BlockSpec((T, CB), lambda j: (0, j))
    return pl.pallas_call(
        body, grid=(CONVD // CB,),
        in_specs=[cblk, pl.BlockSpec((K5, CB), lambda j: (0, j)), cblk,
                  pl.BlockSpec((None, T, CB), lambda j: (0, 0, j)), pl.BlockSpec((None, T, CB), lambda j: (1, 0, j)),
                  pl.BlockSpec((T, CB), lambda j: (0, jnp.minimum(j, nskip - 1))),
                  pl.BlockSpec((1, CB), lambda j: (0, jnp.minimum(j, nskip - 1)))],
        out_specs=[cblk, pl.BlockSpec((K5, CB), lambda j: (0, j)), pl.BlockSpec((1, CB), lambda j: (0, j))],
        out_shape=[S((T, CONVD), bf16), S((K5, CONVD), f32), S((1, CONVD), f32)],
        scratch_shapes=[pltpu.VMEM((maxlen + 2 * PAD5, CB), f32), pltpu.VMEM((K5, maxlen, CB), f32)],
        compiler_params=_cparams("parallel"), name=name)(proj, w, dsilu, dy2, dy2, dyskip, dexp)


GPAD = GRID_W


def _grid_copies(g_ref, src, L):
    col = lax.broadcasted_iota(jnp.int32, (L, CB), 0) & (GRID_W - 1)
    for d in range(3):
        _zero_rows(g_ref.at[d], 0, GPAD)
        _zero_rows(g_ref.at[d], GPAD + L, GPAD)
    g_ref[1, pl.ds(GPAD, L), :] = src
    g_ref[0, pl.ds(GPAD, L), :] = jnp.where(col != 0, g_ref[1, pl.ds(GPAD - 1, L), :], 0.0)
    g_ref[2, pl.ds(GPAD, L), :] = jnp.where(col != GRID_W - 1, g_ref[1, pl.ds(GPAD + 1, L), :], 0.0)


def ffn_gate_fwd(val, gate, cw, cb_, name, hosted=None):
    L = val.shape[0]
    nb = FH // CB

    def body(val_ref, gate_ref, w_ref, b_ref, o_ref, s_ref, vds_ref, g_ref):
        wv = [w_ref[pl.ds(t, 1), :] for t in range(9)]
        bv = b_ref[...]
        _grid_copies(g_ref, gate_ref[...], L)

        def tile(i, carry):
            r = i * RT
            acc = jnp.broadcast_to(bv, (RT, CB))
            for dr in range(3):
                for dc in range(3):
                    acc = acc + g_ref[dc, _rows(r, GPAD + (dr - 1) * GRID_W), :] * wv[3 * dr + dc]
            sg = _sigmoid(acc)
            s = acc * sg
            v = val_ref[_rows(r), :]
            o_ref[_rows(r), :] = (s * v).astype(bf16)
            s_ref[_rows(r), :] = s
            vds_ref[_rows(r), :] = v * (sg * (1.0 + acc * (1.0 - sg)))
            return carry

        lax.fori_loop(0, L // RT, tile, 0, unroll=2)

    cblk = pl.BlockSpec((L, CB), lambda j: (0, j))
    (out, s_, vds), extra = _host_call(
        body, (nb,), [cblk, cblk, pl.BlockSpec((9, CB), lambda j: (0, j)), pl.BlockSpec((1, CB), lambda j: (0, j))],
        [cblk, cblk, cblk], [S((L, FH), bf16), S((L, FH), f32), S((L, FH), f32)],
        [pltpu.VMEM((3, L + 2 * GPAD, CB), f32)], ("parallel",), name, (val, gate, cw, cb_), hosted)
    return out, s_, vds, extra


def ffn_gate_bwd(gate, s_, vds, cw, dact, name):
    L = gate.shape[0]
    nb = FH // CB

    def body(gate_ref, s_ref, vds_ref, w_ref, da_ref, dval_ref, dgate_ref, dw_ref, db_ref, g_ref, d_ref):
        wv = [w_ref[pl.ds(t, 1), :] for t in range(9)]
        _grid_copies(g_ref, gate_ref[...], L)

        def tile1(i, carry):
            r = i * RTB
            da = da_ref[_rowsb(r), :].astype(f32)
            dval_ref[_rowsb(r), :] = (da * s_ref[_rowsb(r), :]).astype(bf16)
            dpre = da * vds_ref[_rowsb(r), :]
            d_ref[_rowsb(r), :] = dpre
            new = [carry[t] + _fold8(dpre * g_ref[t % 3, _rowsb(r, GPAD + (t // 3 - 1) * GRID_W), :]) for t in range(9)]
            new.append(carry[9] + _fold8(dpre))
            return tuple(new)

        acc8 = lax.fori_loop(0, L // RTB, tile1, tuple(jnp.zeros((8, CB), f32) for _ in range(10)), unroll=2)
        for t in range(9):
            dw_ref[pl.ds(t, 1), :] = jnp.sum(acc8[t], axis=0, keepdims=True)
        db_ref[...] = jnp.sum(acc8[9], axis=0, keepdims=True)
        _grid_copies(g_ref, d_ref[...], L)

        def tile2(i, carry):
            r = i * RTB
            dg = jnp.zeros((RTB, CB), f32)
            for dr in range(3):
                for dc in range(3):
                    dg = dg + g_ref[2 - dc, _rowsb(r, GPAD - (dr - 1) * GRID_W), :] * wv[3 * dr + dc]
            dgate_ref[_rowsb(r), :] = dg.astype(bf16)
            return carry

        lax.fori_loop(0, L // RTB, tile2, 0, unroll=4)

    cblk = pl.BlockSpec((L, CB), lambda j: (0, j))
    return pl.pallas_call(
        body, grid=(nb,),
        in_specs=[cblk, cblk, cblk, pl.BlockSpec((9, CB), lambda j: (0, j)), cblk],
        out_specs=[cblk, cblk, pl.BlockSpec((9, CB), lambda j: (0, j)), pl.BlockSpec((1, CB), lambda j: (0, j))],
        out_shape=[S((L, FH), bf16), S((L, FH), bf16), S((9, FH), f32), S((1, FH), f32)],
        scratch_shapes=[pltpu.VMEM((3, L + 2 * GPAD, CB), f32), pltpu.VMEM((L, CB), f32)],
        compiler_params=_cparams("parallel"), name=name)(gate, s_, vds, cw, dact)


CONF_K = 31
CHALF = CONF_K // 2
CPAD = 16


def _shift_copies8(c_ref, base_ref, L):
    n = L + 2 * CPAD - 8
    for b_ in range(8):
        c_ref[b_, pl.ds(0, n), :] = base_ref[pl.ds(b_, n), :]


def _tap_ab(o):
    return o % 8, o - o % 8


def conf_glu_conv_fwd(pa, pg, b1, wdw, bdw, name, hosted=None):
    L = pa.shape[0]
    nb = D // CB

    def body(pa_ref, pg_ref, ba_ref, bg_ref, w_ref, bdw_ref, o_ref, base_ref, c_ref):
        _zero_rows(base_ref, 0, CPAD)
        _zero_rows(base_ref, CPAD + L, CPAD)
        base_ref[pl.ds(CPAD, L), :] = (pa_ref[...] + ba_ref[...]) * _sigmoid(pg_ref[...] + bg_ref[...])
        _shift_copies8(c_ref, base_ref, L)
        bv = bdw_ref[...]

        def tile(i, carry):
            r = i * RT
            acc = jnp.broadcast_to(bv, (RT, CB))
            for k in range(CONF_K):
                b_, a8 = _tap_ab(k - CHALF)
                acc = acc + c_ref[b_, _rows(r, CPAD + a8), :] * w_ref[pl.ds(k, 1), :]
            o_ref[_rows(r), :] = acc
            return carry

        lax.fori_loop(0, L // RT, tile, 0, unroll=2)

    cblk = pl.BlockSpec((L, CB), lambda j: (0, j))
    rblk = pl.BlockSpec((1, CB), lambda j: (0, j))
    rgblk = pl.BlockSpec((1, CB), lambda j: (0, nb + j))
    (out,), extra = _host_call(
        body, (nb,), [cblk, cblk, rblk, rgblk, pl.BlockSpec((CONF_K, CB), lambda j: (0, j)), rblk],
        [cblk], [S((L, D), f32)], [pltpu.VMEM((L + 2 * CPAD, CB), f32), pltpu.VMEM((8, L + 2 * CPAD, CB), f32)],
        ("parallel",), name, (pa, pg, b1, b1, wdw, bdw), hosted)
    return out, extra


def conf_glu_conv_bwd(pa, pg, b1, wdw, dy, name):
    L = pa.shape[0]
    nb = D // CB

    def body(pa_ref, pg_ref, ba_ref, bg_ref, w_ref, dy_ref, dpa_ref, dpg_ref, dba_ref, dbg_ref, dw_ref, dbdw_ref,
             base_ref, c_ref, acc_ref):
        _zero_rows(base_ref, 0, CPAD)
        _zero_rows(base_ref, CPAD + L, CPAD)
        base_ref[pl.ds(CPAD, L), :] = (pa_ref[...] + ba_ref[...]) * _sigmoid(pg_ref[...] + bg_ref[...])
        _shift_copies8(c_ref, base_ref, L)
        acc_ref[...] = jnp.zeros_like(acc_ref)

        def tile1(i, carry):
            r = i * RTB
            dyt = dy_ref[_rowsb(r), :]
            for k in range(CONF_K):
                b_, a8 = _tap_ab(k - CHALF)
                acc_ref[k] += _fold8(dyt * c_ref[b_, _rowsb(r, CPAD + a8), :])
            return carry + _fold8(dyt)

        db8 = lax.fori_loop(0, L // RTB, tile1, jnp.zeros((8, CB), f32), unroll=2)
        dbdw_ref[...] = jnp.sum(db8, axis=0, keepdims=True)
        for k in range(CONF_K):
            dw_ref[pl.ds(k, 1), :] = jnp.sum(acc_ref[k], axis=0, keepdims=True)
        base_ref[pl.ds(CPAD, L), :] = dy_ref[...]
        _shift_copies8(c_ref, base_ref, L)
        ba = ba_ref[...]
        bg = bg_ref[...]

        def tile2(i, carry):
            r = i * RTB
            dglu = jnp.zeros((RTB, CB), f32)
            for k in range(CONF_K):
                b_, a8 = _tap_ab(CHALF - k)
                dglu = dglu + c_ref[b_, _rowsb(r, CPAD + a8), :] * w_ref[pl.ds(k, 1), :]
            a = pa_ref[_rowsb(r), :] + ba
            sg = _sigmoid(pg_ref[_rowsb(r), :] + bg)
            dpa = dglu * sg
            dpg = dglu * a * (sg * (1.0 - sg))
            dpa_ref[_rowsb(r), :] = dpa.astype(bf16)
            dpg_ref[_rowsb(r), :] = dpg.astype(bf16)
            return carry[0] + _fold8(dpa), carry[1] + _fold8(dpg)

        s8 = lax.fori_loop(0, L // RTB, tile2, (jnp.zeros((8, CB), f32), jnp.zeros((8, CB), f32)), unroll=2)
        dba_ref[...] = jnp.sum(s8[0], axis=0, keepdims=True)
        dbg_ref[...] = jnp.sum(s8[1], axis=0, keepdims=True)

    cblk = pl.BlockSpec((L, CB), lambda j: (0, j))
    rblk = pl.BlockSpec((1, CB), lambda j: (0, j))
    rgblk = pl.BlockSpec((1, CB), lambda j: (0, nb + j))
    wblk = pl.BlockSpec((CONF_K, CB), lambda j: (0, j))
    return pl.pallas_call(
        body, grid=(nb,), in_specs=[cblk, cblk, rblk, rgblk, wblk, cblk],
        out_specs=[cblk, cblk, rblk, rblk, wblk, rblk],
        out_shape=[S((L, D), bf16), S((L, D), bf16), S((1, D), f32), S((1, D), f32), S((CONF_K, D), f32), S((1, D), f32)],
        scratch_shapes=[pltpu.VMEM((L + 2 * CPAD, CB), f32), pltpu.VMEM((8, L + 2 * CPAD, CB), f32),
                        pltpu.VMEM((CONF_K, 8, CB), f32)],
        compiler_params=_cparams("parallel"), name=name)(pa, pg, b1, b1, wdw, dy)


def _ln_silu_f(x, w, b):
    mu = jnp.mean(x, axis=-1, keepdims=True)
    d = x - mu
    y = d * lax.rsqrt(jnp.mean(d * d, axis=-1, keepdims=True) + EPS) * w + b
    return y * _sigmoid(y)


def ln_silu_fwd(x, w, b, name):
    T = x.shape[0]

    def body(x_ref, w_ref, b_ref, o_ref):
        o_ref[...] = _ln_silu_f(x_ref[...], w_ref[...], b_ref[...]).astype(bf16)

    blk = pl.BlockSpec((TB, D), lambda i: (i, 0))
    row = pl.BlockSpec((1, D), lambda i: (0, 0))
    return pl.pallas_call(body, grid=(T // TB,), in_specs=[blk, row, row], out_specs=blk, out_shape=S((T, D), bf16),
                          compiler_params=_cparams("parallel"), name=name)(x, w, b)


def ln_silu_bwd(x, w, b, ds, name):
    T = x.shape[0]

    def body(x_ref, w_ref, b_ref, ds_ref, dx_ref, dw_ref, db_ref):
        i = pl.program_id(0)
        _, vjp = jax.vjp(_ln_silu_f, x_ref[...], w_ref[...], b_ref[...])
        dx, dw, db = vjp(ds_ref[...].astype(f32))
        dx_ref[...] = dx

        @pl.when(i == 0)
        def _():
            dw_ref[...] = jnp.zeros_like(dw_ref)
            db_ref[...] = jnp.zeros_like(db_ref)

        dw_ref[...] += dw
        db_ref[...] += db

    blk = pl.BlockSpec((TB, D), lambda i: (i, 0))
    row = pl.BlockSpec((1, D), lambda i: (0, 0))
    return pl.pallas_call(body, grid=(T // TB,), in_specs=[blk, row, row, blk], out_specs=[blk, row, row],
                          out_shape=[S((T, D), f32), S((1, D), f32), S((1, D), f32)],
                          compiler_params=_cparams("arbitrary"), name=name)(x, w, b, ds)


def _mxu(a, b, dims):
    return lax.dot_general(a.astype(bf16), b.astype(bf16), (dims, ((), ())), preferred_element_type=f32)


def _nn(a, b):
    return _mxu(a, b, ((1,), (0,)))


def _nt(a, b):
    return _mxu(a, b, ((1,), (1,)))


def _tn(a, b):
    return _mxu(a, b, ((0,), (0,)))


@jax.custom_vjp
def _dot_nn(a, b):
    return _nn(a, b)


@jax.custom_vjp
def _dot_nt(a, b):
    return _nt(a, b)


@jax.custom_vjp
def _dot_tn(a, b):
    return _tn(a, b)


_dot_nn.defvjp(lambda a, b: (_nn(a, b), (a, b)), lambda res, g: (_nt(g, res[1]), _tn(res[0], g)))
_dot_nt.defvjp(lambda a, b: (_nt(a, b), (a, b)), lambda res, g: (_nn(g, res[1]), _tn(g, res[0])))
_dot_tn.defvjp(lambda a, b: (_tn(a, b), (a, b)), lambda res, g: (_nt(res[1], g), _nn(res[0], g)))


def _exact_dot(a, b, dims, split_first):
    v = a if split_first else b
    p1 = v.astype(bf16)
    r1 = v - p1.astype(f32)
    p2 = r1.astype(bf16)
    p3 = (r1 - p2.astype(f32)).astype(bf16)
    out = None
    for p in (p1, p2, p3):
        lhs, rhs = (p, b.astype(bf16)) if split_first else (a.astype(bf16), p)
        t = lax.dot_general(lhs, rhs, (dims, ((), ())), preferred_element_type=f32)
        out = t if out is None else out + t
    return out


@jax.custom_vjp
def _masked_sum_cols(mf, a):
    return _exact_dot(mf, a, ((1,), (0,)), False)


@jax.custom_vjp
def _masked_sum_rows(mf, a):
    return _exact_dot(a, mf, ((1,), (1,)), True)


_masked_sum_cols.defvjp(lambda mf, a: (_exact_dot(mf, a, ((1,), (0,)), False), mf),
                        lambda mf, g: (jnp.zeros_like(mf), _exact_dot(mf, g, ((0,), (0,)), False)))
_masked_sum_rows.defvjp(lambda mf, a: (_exact_dot(a, mf, ((1,), (1,)), True), mf),
                        lambda mf, g: (jnp.zeros_like(mf), _exact_dot(g, mf, ((1,), (0,)), True)))


def _masked_sum(mf, a, rows):
    return _masked_sum_rows(mf, a) if rows else _masked_sum_cols(mf, a)


def _lanes_to_rows(v):
    r = lax.broadcasted_iota(jnp.int32, (GW, GW), 0)
    c = lax.broadcasted_iota(jnp.int32, (GW, GW), 1)
    return jnp.sum(jnp.where(r == c, jnp.broadcast_to(v, (GW, GW)), 0.0), axis=1, keepdims=True)


def _ssd_chunk(x, B, C, dtc, dtr, bc, br, alc, alr, s_in, is_fwd):
    row = lax.broadcasted_iota(jnp.int32, (Q, Q), 0)
    col = lax.broadcasted_iota(jnp.int32, (Q, Q), 1)
    sgn = jnp.where(is_fwd, 1, -1).astype(jnp.int32)
    mask = (row - col) * sgn >= 0
    mf = mask.astype(f32)
    lane_head = lax.broadcasted_iota(jnp.int32, (1, GW), 1) // P

    def spread(v):
        out = jnp.zeros((v.shape[0], GW), f32)
        for r in range(HPG):
            out = jnp.where(lane_head == r, v[:, r:r + 1], out)
        return out

    dt_c = _softplus(dtc + bc)
    dt_r = _softplus(dtr + br)
    a_c = dt_c * (-jnp.exp(alc))
    a_r = dt_r * (-jnp.exp(alr))
    acum_c = _masked_sum(mf, a_c, False)
    acum_r = _masked_sum(mf, a_r, True)
    tot_c = jnp.sum(a_c, axis=0, keepdims=True)
    dt_e = spread(dt_c)
    acum_e = spread(acum_c)
    tot_e = spread(tot_c)
    xdt = x * dt_e
    cb = _dot_nt(C, B)
    scores, xs = [], []
    for r in range(HPG):
        seg = acum_c[:, r:r + 1] - acum_r[r:r + 1, :]
        scores.append(cb * jnp.exp(jnp.where(mask, seg, -jnp.inf)))
        xs.append(jnp.where(lane_head == r, xdt, 0.0))
    y = _dot_nn(jnp.concatenate(scores, axis=1), jnp.concatenate(xs, axis=0))
    y = y + _dot_nt(C, s_in) * jnp.exp(acum_e)
    xe = xdt * jnp.exp(tot_e - acum_e)
    s_out = _lanes_to_rows(jnp.exp(tot_e)) * s_in + _dot_tn(xe, B)
    return y, s_out


def _chunk_index(d, t, nctx, nc):
    bwd = jnp.where(t < nctx, nctx - 1 - t, nc - 1 - (t - nctx))
    return jnp.where(d == 0, t, bwd)


def _ssd_in_specs(ci):
    small_c = pl.BlockSpec((None, G, 1, HPG), lambda d, t: (d, 0, 0, 0))
    small_r = pl.BlockSpec((None, G, HPG, 1), lambda d, t: (d, 0, 0, 0))
    return [
        pl.BlockSpec((Q, CONVD), lambda d, t: (ci(d, t), 0)),
        pl.BlockSpec((None, G, Q, HPG), lambda d, t: (d, 0, ci(d, t), 0)),
        pl.BlockSpec((None, G, HPG, Q), lambda d, t: (d, 0, 0, ci(d, t))),
        small_c, small_r, small_c, small_r,
    ]


def _group_cols(g):
    return pl.ds(g * GW, GW), pl.ds(DI + g * N, N), pl.ds(DI + G * N + g * N, N)


def ssd_scan_fwd(xbc, dtc, dtr, bc, br, alc, alr, nctx, name, hosted=None):
    T = xbc.shape[0]
    nc = T // Q

    def body(xbc_ref, dtc_ref, dtr_ref, bc_ref, br_ref, alc_ref, alr_ref, y_ref, sin_ref, st_ref):
        d = pl.program_id(0)
        t = pl.program_id(1)

        @pl.when(t == 0)
        def _():
            st_ref[...] = jnp.zeros_like(st_ref)

        for g in range(G):
            xs, bs, cs = _group_cols(g)
            s_in = st_ref[g]
            sin_ref[g] = s_in
            y, s_out = _ssd_chunk(xbc_ref[:, xs], xbc_ref[:, bs], xbc_ref[:, cs], dtc_ref[g], dtr_ref[g], bc_ref[g], br_ref[g],
                                  alc_ref[g], alr_ref[g], s_in, d == 0)
            y_ref[:, xs] = y
            st_ref[g] = s_out

    ci = lambda d, t: _chunk_index(d, t, nctx, nc)
    out_specs = [
        pl.BlockSpec((None, Q, DI), lambda d, t: (d, ci(d, t), 0)),
        pl.BlockSpec((None, None, G, GW, N), lambda d, t: (d, ci(d, t), 0, 0, 0)),
    ]
    return _host_call(
        body, (2, nc), _ssd_in_specs(ci), out_specs, [S((2, T, DI), f32), S((2, nc, G, GW, N), f32)],
        [pltpu.VMEM((G, GW, N), f32)], ("arbitrary", "arbitrary"), name, (xbc, dtc, dtr, bc, br, alc, alr), hosted)


def ssd_scan_bwd(xbc, dtc, dtr, bc, br, alc, alr, s_in_all, dy, nctx, name, hosted=None):
    T = xbc.shape[0]
    nc = T // Q

    def body(xbc_ref, dtc_ref, dtr_ref, bc_ref, br_ref, alc_ref, alr_ref, sin_ref, dy_ref,
             dxbc_ref, ddtc_ref, ddtr_ref, dbc_ref, dbr_ref, dalc_ref, dalr_ref, ds_ref):
        d = pl.program_id(0)
        t = pl.program_id(1)

        @pl.when(t == 0)
        def _():
            ds_ref[...] = jnp.zeros_like(ds_ref)
            dbc_ref[...] = jnp.zeros_like(dbc_ref)
            dbr_ref[...] = jnp.zeros_like(dbr_ref)
            dalc_ref[...] = jnp.zeros_like(dalc_ref)
            dalr_ref[...] = jnp.zeros_like(dalr_ref)

        f = functools.partial(_ssd_chunk, is_fwd=(d == 0))
        for g in range(G):
            xs, bs, cs = _group_cols(g)
            _, vjp = jax.vjp(f, xbc_ref[:, xs], xbc_ref[:, bs], xbc_ref[:, cs], dtc_ref[g], dtr_ref[g], bc_ref[g], br_ref[g],
                             alc_ref[g], alr_ref[g], sin_ref[g])
            dx, dB, dC, ddtc, ddtr, dbc, dbr, dalc, dalr, ds = vjp((dy_ref[:, xs], ds_ref[g]))
            dxbc_ref[:, xs] = dx
            dxbc_ref[:, bs] = dB
            dxbc_ref[:, cs] = dC
            ddtc_ref[g] = ddtc
            ddtr_ref[g] = ddtr
            dbc_ref[g] += dbc
            dbr_ref[g] += dbr
            dalc_ref[g] += dalc
            dalr_ref[g] += dalr
            ds_ref[g] = ds

    ci = lambda d, t: _chunk_index(d, nc - 1 - t, nctx, nc)
    in_specs = _ssd_in_specs(ci) + [
        pl.BlockSpec((None, None, G, GW, N), lambda d, t: (d, ci(d, t), 0, 0, 0)),
        pl.BlockSpec((Q, DI), lambda d, t: (ci(d, t), 0)),
    ]
    small_c = pl.BlockSpec((None, G, 1, HPG), lambda d, t: (d, 0, 0, 0))
    small_r = pl.BlockSpec((None, G, HPG, 1), lambda d, t: (d, 0, 0, 0))
    out_specs = [
        pl.BlockSpec((None, Q, CONVD), lambda d, t: (d, ci(d, t), 0)),
        pl.BlockSpec((None, G, Q, HPG), lambda d, t: (d, 0, ci(d, t), 0)),
        pl.BlockSpec((None, G, HPG, Q), lambda d, t: (d, 0, 0, ci(d, t))),
        small_c, small_r, small_c, small_r,
    ]
    out_shape = [S((2, T, CONVD), f32), S((2, G, T, HPG), f32), S((2, G, HPG, T), f32),
                 S((2, G, 1, HPG), f32), S((2, G, HPG, 1), f32), S((2, G, 1, HPG), f32), S((2, G, HPG, 1), f32)]
    return _host_call(body, (2, nc), in_specs, out_specs, out_shape, [pltpu.VMEM((G, GW, N), f32)],
                      ("arbitrary", "arbitrary"), name, (xbc, dtc, dtr, bc, br, alc, alr, s_in_all, dy), hosted)


GTB = 128


def _gate_norm_f(yf, yb, x, z, dexp, w):
    y = (yf + yb + dexp * x) * (z * _sigmoid(z))
    return y * lax.rsqrt(jnp.mean(y * y, axis=-1, keepdims=True) + EPS) * w


def ssd_gate_fwd(y2, xbc, proj, dexp, w, nctxb, name):
    T = xbc.shape[0]
    L = T - nctxb * GTB

    def body(yf_ref, yb_ref, x_ref, z_ref, d_ref, w_ref, o_ref):
        o_ref[...] = _gate_norm_f(yf_ref[...], yb_ref[...], x_ref[...], z_ref[...], d_ref[...], w_ref[...]).astype(bf16)

    wide = pl.BlockSpec((GTB, DI), lambda i: (i + nctxb, 0))
    row = pl.BlockSpec((1, DI), lambda i: (0, 0))
    return pl.pallas_call(
        body, grid=(L // GTB,),
        in_specs=[pl.BlockSpec((None, GTB, DI), lambda i: (0, i + nctxb, 0)),
                  pl.BlockSpec((None, GTB, DI), lambda i: (1, i + nctxb, 0)), wide, wide, row, row],
        out_specs=pl.BlockSpec((GTB, DI), lambda i: (i, 0)), out_shape=S((L, DI), bf16),
        compiler_params=_cparams("parallel"), name=name)(y2, y2, xbc, proj, dexp, w)


def ssd_gate_bwd(y2, xbc, proj, dexp, w, dyn, nctxb, name, hosted=None):
    T = xbc.shape[0]
    nb = T // GTB

    def body(yf_ref, yb_ref, x_ref, z_ref, d_ref, w_ref, dyn_ref, dy_ref, dz_ref, dd_ref, dw_ref):
        i = pl.program_id(0)

        @pl.when(i == 0)
        def _():
            dd_ref[...] = jnp.zeros_like(dd_ref)
            dw_ref[...] = jnp.zeros_like(dw_ref)

        @pl.when(i < nctxb)
        def _():
            dy_ref[...] = jnp.zeros_like(dy_ref)
            dz_ref[...] = jnp.zeros_like(dz_ref)

        @pl.when(i >= nctxb)
        def _():
            _, vjp = jax.vjp(_gate_norm_f, yf_ref[...], yb_ref[...], x_ref[...], z_ref[...], d_ref[...], w_ref[...])
            dyf, _, _, dz, dd, dw = vjp(dyn_ref[...].astype(f32))
            dy_ref[...] = dyf
            dz_ref[...] = dz.astype(bf16)
            fold = (lax.broadcasted_iota(jnp.int32, (DI, 128), 0) // P == lax.broadcasted_iota(jnp.int32, (DI, 128), 1))
            dd_ref[...] += jnp.dot(dd, fold.astype(f32), precision=HI, preferred_element_type=f32)
            dw_ref[...] += dw

    wide = pl.BlockSpec((GTB, DI), lambda i: (i, 0))
    row = pl.BlockSpec((1, DI), lambda i: (0, 0))
    hrow = pl.BlockSpec((1, 128), lambda i: (0, 0))
    return _host_call(
        body, (nb,),
        [pl.BlockSpec((None, GTB, DI), lambda i: (0, i, 0)), pl.BlockSpec((None, GTB, DI), lambda i: (1, i, 0)),
         wide, wide, row, row, pl.BlockSpec((GTB, DI), lambda i: (jnp.maximum(i - nctxb, 0), 0))],
        [wide, wide, hrow, row],
        [S((T, DI), f32), S((T, DI), bf16), S((1, 128), f32), S((1, DI), f32)],
        [], ("arbitrary",), name, (y2, y2, xbc, proj, dexp, w, dyn), hosted)


CROWS = 2 * N_DEV


def mod_fwd(c16, modw, name):
    nl, _, cols = modw.shape

    def body(c_ref, w_ref, o_ref):
        cv = c_ref[...]
        s = cv * _sigmoid(cv)
        for l in range(nl):
            o_ref[l] = jnp.dot(s, w_ref[l], precision=HI, preferred_element_type=f32)

    return pl.pallas_call(body, in_specs=[VMEM, VMEM], out_specs=VMEM, out_shape=S((nl, CROWS, cols), f32),
                          compiler_params=pltpu.CompilerParams(vmem_limit_bytes=VMEM_LIMIT_BYTES), name=name)(c16, modw)


def mod_bwd(c16, modw, dm_sh, dm_all, name):
    nl, _, cols = modw.shape

    def body(c_ref, w_ref, dm_ref, dmall_ref, dw_ref, dc_ref, db_ref):
        cv = c_ref[...]
        sg = _sigmoid(cv)
        s = cv * sg
        ds_dc = sg * (1.0 + cv * (1.0 - sg))
        is_ctx = lax.broadcasted_iota(jnp.int32, (CROWS, D), 0) >= N_DEV
        dc = jnp.zeros((1, D), f32)
        for l in range(nl):
            dm = dm_ref[l]
            dw_ref[l] = lax.dot_general(s, dm, (((0,), (0,)), ((), ())), precision=HI, preferred_element_type=f32)
            dsv = lax.dot_general(dm, w_ref[l], (((1,), (1,)), ((), ())), precision=HI, preferred_element_type=f32)
            dc = dc + jnp.sum(jnp.where(is_ctx, dsv * ds_dc, 0.0), axis=0, keepdims=True)
            db_ref[pl.ds(l, 1), :] = jnp.sum(dmall_ref[l], axis=0, keepdims=True)
        dc_ref[...] = dc

    return pl.pallas_call(
        body, in_specs=[VMEM, VMEM, VMEM, VMEM], out_specs=[VMEM, VMEM, VMEM],
        out_shape=[S(modw.shape, f32), S((1, D), f32), S((nl, 6 * D), f32)],
        compiler_params=pltpu.CompilerParams(vmem_limit_bytes=VMEM_LIMIT_BYTES), name=name)(c16, modw, dm_sh, dm_all)


def adamw(w, g, m, v, name):
    R, C = w.shape
    rb = R if R <= 512 else max(r_ for r_ in range(8, 513, 8) if R % r_ == 0)
    bc1 = 1.0 - ADAM_B1 ** ADAM_STEP
    bc2 = 1.0 - ADAM_B2 ** ADAM_STEP

    def body(w_ref, g_ref, m_ref, v_ref, d_ref, nm_ref, nv_ref):
        gv = g_ref[...]
        m_new = ADAM_B1 * m_ref[...] + (1.0 - ADAM_B1) * gv
        v_new = ADAM_B2 * v_ref[...] + (1.0 - ADAM_B2) * (gv * gv)
        m_hat = m_new / bc1
        v_hat = v_new / bc2
        d_ref[...] = -ADAM_LR * (m_hat / (jnp.sqrt(v_hat) + ADAM_EPS) + ADAM_WD * w_ref[...])
        nm_ref[...] = m_new
        nv_ref[...] = v_new

    blk = pl.BlockSpec((rb, C), lambda i: (i, 0))
    return pl.pallas_call(body, grid=(R // rb,), in_specs=[blk] * 4, out_specs=[blk] * 3,
                          out_shape=[S((R, C), f32)] * 3, compiler_params=_cparams("parallel"), name=name)(w, g, m, v)


def _me():
    return lax.axis_index("x"), lax.axis_index("y"), lax.axis_index("c")


def allgather_small(x, name, with_sum=False):
    r, w = x.shape

    def body(x_ref, *refs):
        if with_sum:
            out_ref, sum_ref, send_sems, recv_sems = refs
        else:
            out_ref, send_sems, recv_sems = refs
        mx, my, mc = _me()
        me = 4 * mx + 2 * my + mc
        out_ref[me] = x_ref[...]
        peers = []
        for k in range(1, N_DEV):
            kx, ky, kc = (k >> 2) & 1, (k >> 1) & 1, k & 1
            peers.append((mx + kx - 2 * mx * kx, my + ky - 2 * my * ky, mc + kc - 2 * mc * kc))
        copies = []
        for k, peer in enumerate(peers):
            cp = pltpu.make_async_remote_copy(src_ref=x_ref, dst_ref=out_ref.at[me], send_sem=send_sems.at[k],
                                              recv_sem=recv_sems.at[k], device_id=peer, device_id_type=MESH)
            cp.start()
            copies.append(cp)
        for k, (px, py, pc) in enumerate(peers):
            pltpu.make_async_remote_copy(src_ref=x_ref, dst_ref=out_ref.at[4 * px + 2 * py + pc], send_sem=send_sems.at[k],
                                         recv_sem=recv_sems.at[k], device_id=(px, py, pc), device_id_type=MESH).wait_recv()
        for cp in copies:
            cp.wait_send()
        if with_sum:
            acc = out_ref[0]
            for j in range(1, N_DEV):
                acc = acc + out_ref[j]
            sum_ref[...] = acc

    out_shape = [S((N_DEV, r, w), f32)] + ([S((r, w), f32)] if with_sum else [])
    outs = pl.pallas_call(
        body, in_specs=[VMEM], out_specs=[VMEM] * len(out_shape), out_shape=out_shape,
        scratch_shapes=[pltpu.SemaphoreType.DMA((N_DEV - 1,)), pltpu.SemaphoreType.DMA((N_DEV - 1,))],
        compiler_params=pltpu.CompilerParams(vmem_limit_bytes=VMEM_LIMIT_BYTES), name=name)(x)
    return outs if with_sum else outs[0]


def _tile2d(R, W, max_rows):
    if R <= max_rows:
        return R, W
    fits = [r_ for r_ in range(16, max_rows + 1, 16) if R % r_ == 0]
    return (max(fits), W) if fits else (R, 256)


def add_own(g, r, core, name):
    _, _, R, W = g.shape
    rb, wb = _tile2d(R, W, 512)

    def body(core_ref, a_ref, b_ref, o_ref):
        o_ref[...] = (a_ref[...].astype(f32) + b_ref[...].astype(f32)).astype(bf16)

    blk = pl.BlockSpec((None, rb, wb), lambda k, i, j, core_ref: (k, i, j))
    gs = pltpu.PrefetchScalarGridSpec(
        num_scalar_prefetch=1, grid=(4, R // rb, W // wb),
        in_specs=[pl.BlockSpec((None, None, rb, wb), lambda k, i, j, core_ref: (k, core_ref[0], i, j)), blk], out_specs=blk)
    return pl.pallas_call(body, grid_spec=gs, out_shape=S((4, R, W), bf16),
                          compiler_params=_cparams("parallel", "parallel", "parallel"), name=name)(core, g, r)


def sum_adamw(recv, w, m, v, layer, name, into=None):
    _, R, W = recv.shape
    rb, wb = _tile2d(R, W, 256)
    bc1 = 1.0 - ADAM_B1 ** ADAM_STEP
    bc2 = 1.0 - ADAM_B2 ** ADAM_STEP
    n_into = 0 if into is None else 4

    def body(r_ref, w_ref, m_ref, v_ref, *refs):
        g_ref, d_ref, nm_ref, nv_ref = refs[n_into:]
        gv = r_ref[0].astype(f32)
        for k in range(1, 4):
            gv = gv + r_ref[k].astype(f32)
        m_new = ADAM_B1 * m_ref[...] + (1.0 - ADAM_B1) * gv
        v_new = ADAM_B2 * v_ref[...] + (1.0 - ADAM_B2) * (gv * gv)
        g_ref[...] = gv
        d_ref[...] = -ADAM_LR * ((m_new / bc1) / (jnp.sqrt(v_new / bc2) + ADAM_EPS) + ADAM_WD * w_ref[...])
        nm_ref[...] = m_new
        nv_ref[...] = v_new

    if layer is None:
        wblk = pl.BlockSpec((rb, wb), lambda i, j: (i, j))
        oshape = S((R, W), f32)
    else:
        wblk = pl.BlockSpec((None, rb, wb), lambda i, j: (layer, i, j))
        oshape = S(w.shape, f32)
    return pl.pallas_call(
        body, grid=(R // rb, W // wb),
        in_specs=[pl.BlockSpec((4, rb, wb), lambda i, j: (0, i, j)), wblk, wblk, wblk] + [ANY] * n_into,
        out_specs=[wblk] * 4, out_shape=[oshape] * 4, input_output_aliases={4 + k: k for k in range(n_into)},
        compiler_params=_cparams("parallel", "parallel"), name=name)(recv, w, m, v, *(into or ()))


def sum_rows(a, name):
    K, R, W = a.shape
    rb = _pick(R, (512, 256, 128, 64, 32, 16))

    def body(a_ref, o_ref):
        acc = a_ref[0].astype(f32)
        for k in range(1, K):
            acc = acc + a_ref[k].astype(f32)
        o_ref[...] = acc

    return pl.pallas_call(body, grid=(R // rb,), in_specs=[pl.BlockSpec((K, rb, W), lambda i: (0, i, 0))],
                          out_specs=pl.BlockSpec((rb, W), lambda i: (i, 0)), out_shape=S((R, W), f32),
                          compiler_params=_cparams("parallel"), name=name)(a)


DMA = pltpu.SemaphoreType.DMA


class GatherExchange:
    def __init__(self, arrays):
        self.arrays = list(arrays)
        self.na = len(self.arrays)
        self.out_shape = [S((N_DEV,) + a.shape, a.dtype) for a in self.arrays]
        self.scratch = [DMA((7 * self.na,)), DMA((7 * self.na,)), DMA((self.na,))]

    def ops(self, x_refs, out_refs, sems):
        send_sems, recv_sems, local_sems = sems
        na = self.na
        x, y, c = _me()
        me, sibling = (x, y, c), (x, y, 1 - c)
        chips = [(1 - x, y), (x, 1 - y), (1 - x, 1 - y)]

        def rows(a, px, py, pc):
            return out_refs[a].at[4 * px + 2 * py + pc]

        def copy(a, k, block, to, src=None):
            return pltpu.make_async_remote_copy(
                src_ref=rows(a, *block) if src is None else src, dst_ref=rows(a, *block),
                send_sem=send_sems.at[7 * a + k], recv_sem=recv_sems.at[7 * a + k], device_id=to, device_id_type=MESH)

        def local(a):
            return pltpu.make_async_copy(x_refs[a], rows(a, *me), local_sems.at[a])

        def first(a):
            return [copy(a, 0, me, sibling, src=x_refs[a])] + [copy(a, 1 + j, me, (*chip, c), src=x_refs[a])
                                                                for j, chip in enumerate(chips)]

        def start():
            for a in range(na):
                local(a).start()
                for cp in first(a):
                    cp.start()

        def mid():
            for a in range(na):
                for j, chip in enumerate(chips):
                    copy(a, 1 + j, (*chip, c), me).wait_recv()
                    copy(a, 4 + j, (*chip, c), sibling).start()

        def finish():
            for a in range(na):
                copy(a, 0, sibling, me).wait_recv()
                for j, chip in enumerate(chips):
                    copy(a, 4 + j, (*chip, 1 - c), me).wait_recv()
                for cp in first(a) + [copy(a, 4 + j, (*chip, c), sibling) for j, chip in enumerate(chips)]:
                    cp.wait_send()
                local(a).wait()

        return start, mid, finish


class SiblingExchange:
    def __init__(self, arrays):
        self.arrays = list(arrays)
        self.na = len(self.arrays)
        self.out_shape = [S((4,) + g.shape[2:], g.dtype) for g in self.arrays]
        self.scratch = [DMA((self.na,)), DMA((self.na,))]

    def ops(self, g_refs, out_refs, sems):
        send_sems, recv_sems = sems
        x, y, c = _me()

        def copy(a):
            return pltpu.make_async_remote_copy(src_ref=g_refs[a].at[:, 1 - c], dst_ref=out_refs[a],
                                                send_sem=send_sems.at[a], recv_sem=recv_sems.at[a],
                                                device_id=(x, y, 1 - c), device_id_type=MESH)

        def start():
            for a in range(self.na):
                copy(a).start()

        def finish():
            for a in range(self.na):
                copy(a).wait()

        return start, None, finish


class ChipsExchange:
    def __init__(self, arrays):
        self.arrays = list(arrays)
        self.na = len(self.arrays)
        self.out_shape = [S(p.shape, p.dtype) for p in self.arrays]
        self.scratch = [DMA((3 * self.na,)), DMA((3 * self.na,)), DMA((self.na,))]

    def ops(self, p_refs, out_refs, sems):
        send_sems, recv_sems, local_sems = sems
        x, y, c = _me()
        mine = 2 * x + y
        chips = [(1 - x, y), (x, 1 - y), (1 - x, 1 - y)]

        def local(a):
            return pltpu.make_async_copy(p_refs[a].at[mine], out_refs[a].at[mine], local_sems.at[a])

        def send(a, j):
            px, py = chips[j]
            return pltpu.make_async_remote_copy(src_ref=p_refs[a].at[2 * px + py], dst_ref=out_refs[a].at[mine],
                                                send_sem=send_sems.at[3 * a + j], recv_sem=recv_sems.at[3 * a + j],
                                                device_id=(px, py, c), device_id_type=MESH)

        def recv(a, j):
            px, py = chips[j]
            return pltpu.make_async_remote_copy(src_ref=p_refs[a].at[mine], dst_ref=out_refs[a].at[2 * px + py],
                                                send_sem=send_sems.at[3 * a + j], recv_sem=recv_sems.at[3 * a + j],
                                                device_id=(px, py, c), device_id_type=MESH)

        def start():
            for a in range(self.na):
                local(a).start()
                for j in range(3):
                    send(a, j).start()

        def finish():
            for a in range(self.na):
                for j in range(3):
                    recv(a, j).wait_recv()
                for j in range(3):
                    send(a, j).wait_send()
                local(a).wait()

        return start, None, finish


def exchange(ex, name):
    na = ex.na

    def body(*refs):
        start, mid, finish = ex.ops(refs[:na], refs[na:2 * na], refs[2 * na:])
        start()
        if mid is not None:
            mid()
        finish()

    return pl.pallas_call(body, in_specs=[ANY] * na, out_specs=[ANY] * na, out_shape=ex.out_shape,
                          scratch_shapes=ex.scratch, name=name)(*ex.arrays)


def _host_call(body, grid, in_specs, out_specs, out_shape, scratch_shapes, sem, name, args, hosted):
    if hosted is None:
        res = pl.pallas_call(body, grid=grid, in_specs=in_specs, out_specs=out_specs, out_shape=out_shape,
                             scratch_shapes=scratch_shapes, compiler_params=_cparams(*sem), name=name)(*args)
        return res, None
    n_in, n_out, n_sc, na = len(in_specs), len(out_shape), len(scratch_shapes), hosted.na
    nsteps = 1
    for g_ in grid:
        nsteps *= g_
    mid_step = (3 * nsteps) // 4
    i1 = n_in + na
    i2 = i1 + n_out
    i3 = i2 + na
    i4 = i3 + n_sc

    def wrapped(*refs):
        step = pl.program_id(0)
        for ax in range(1, len(grid)):
            step = step * grid[ax] + pl.program_id(ax)
        start, mid, finish = hosted.ops(refs[n_in:i1], refs[i2:i3], refs[i4:])
        pl.when(step == 0)(start)
        if mid is not None:
            pl.when(step == mid_step)(mid)
        body(*refs[:n_in], *refs[i1:i2], *refs[i3:i4])
        pl.when(step == nsteps - 1)(finish)

    res = pl.pallas_call(
        wrapped, grid=grid, in_specs=list(in_specs) + [ANY] * na, out_specs=list(out_specs) + [ANY] * na,
        out_shape=list(out_shape) + hosted.out_shape, scratch_shapes=list(scratch_shapes) + hosted.scratch,
        compiler_params=_cparams(*(("arbitrary",) * len(grid))), name=name)(*args, *hosted.arrays)
    return res[:n_out], res[n_out:]


PACK_ALIGN = 16 * PACK_W


def _pad_to(v, mult):
    n = v.shape[-1]
    extra = (-n) % mult
    if extra == 0:
        return v
    return jnp.concatenate([v, jnp.zeros(v.shape[:-1] + (extra,), v.dtype)], axis=-1)


def _f32_as_bf16_pairs(v):
    return lax.bitcast_convert_type(v.reshape(-1), bf16).reshape(-1)


def _bf16_pairs_as_f32(v):
    return lax.bitcast_convert_type(v.reshape(v.shape[:-1] + (v.shape[-1] // 2, 2)), f32)


def _col_shards(gw):
    lead = gw.shape[:-1]
    n = gw.shape[-1] // N_DEV
    t = gw.reshape(lead + (N_DEV, n))
    t = jnp.moveaxis(t, -2, 0)
    return t.reshape(N_DEV, -1)


def kernel(x, c, ctx, c_ctx, mod_w, mod_b, norm1_w, norm2_w, ssd_w_in, ssd_conv_w, ssd_conv_b, ssd_dt_bias, ssd_a_log, ssd_d, ssd_norm_w, ssd_w_out, conf_w_pw1, conf_b_pw1, conf_w_dw, conf_b_dw, conf_ln_w, conf_ln_b, conf_w_pw2, conf_b_pw2, ffn_w_up, ffn_conv_w, ffn_conv_b, ffn_w_down, final_norm_w, loss_target, m_c_ctx, m_mod_w, m_mod_b, m_norm1_w, m_norm2_w, m_ssd_w_in, m_ssd_conv_w, m_ssd_conv_b, m_ssd_dt_bias, m_ssd_a_log, m_ssd_d, m_ssd_norm_w, m_ssd_w_out, m_conf_w_pw1, m_conf_b_pw1, m_conf_w_dw, m_conf_b_dw, m_conf_ln_w, m_conf_ln_b, m_conf_w_pw2, m_conf_b_pw2, m_ffn_w_up, m_ffn_conv_w, m_ffn_conv_b, m_ffn_w_down, m_final_norm_w, v_c_ctx, v_mod_w, v_mod_b, v_norm1_w, v_norm2_w, v_ssd_w_in, v_ssd_conv_w, v_ssd_conv_b, v_ssd_dt_bias, v_ssd_a_log, v_ssd_d, v_ssd_norm_w, v_ssd_w_out, v_conf_w_pw1, v_conf_b_pw1, v_conf_w_dw, v_conf_b_dw, v_conf_ln_w, v_conf_ln_b, v_conf_w_pw2, v_conf_b_pw2, v_ffn_w_up, v_ffn_conv_w, v_ffn_conv_b, v_ffn_w_down, v_final_norm_w):
    mx, my, mc = _me()
    me = 4 * mx + 2 * my + mc
    L = x.shape[1]
    LC = ctx.shape[1]
    T = LC + L
    w_in_cols = ssd_w_in.shape[2] * N_DEV
    n_dt = w_in_cols - DI - CONVD

    small = [c[0], ssd_conv_w[0], conf_b_pw1[0], conf_w_dw[0], conf_b_dw[0], conf_ln_w[0], conf_ln_b[0], conf_b_pw2[0],
             ffn_conv_w]
    parts = [_f32_as_bf16_pairs(t) for t in small]
    sizes = [p.shape[0] for p in parts]
    small_flat = _pad_to(jnp.concatenate(parts), PACK_ALIGN).reshape(-1, PACK_W)
    w_in, small_g = exchange(GatherExchange([ssd_w_in[0].astype(bf16), small_flat]), "gather_first")
    gather_in_proj = GatherExchange([ssd_w_out[0].astype(bf16), conf_w_pw2[0].astype(bf16)])
    gather_in_conv = GatherExchange([ffn_w_down[0].astype(bf16), conf_w_pw1[0].astype(bf16)])
    gather_in_scan = GatherExchange([ffn_w_up[0].astype(bf16), ffn_w_up[1].astype(bf16)])
    gather_in_gate = GatherExchange([ffn_w_down[1].astype(bf16)])
    w_up, w_down = [None, None], [None, None]
    small_g = small_g.reshape(N_DEV, -1)
    offs = [0]
    for s_ in sizes:
        offs.append(offs[-1] + s_)
    sm = [_bf16_pairs_as_f32(small_g[:, offs[i]:offs[i + 1]]) for i in range(len(sizes))]

    def cols(pc, K):
        return jnp.moveaxis(pc.reshape(N_DEV, K, -1), 0, 1).reshape(K, -1)

    c_all = sm[0]
    conv_w5 = cols(sm[1], 5)
    b_pw1 = sm[2].reshape(1, 2 * D)
    w_dw = cols(sm[3], CONF_K)
    b_dw, ln_w, ln_b, b_pw2 = (sm[i].reshape(1, D) for i in (4, 5, 6, 7))
    fcw = sm[8].reshape(N_DEV, 2, 9, FH // N_DEV)
    ffn_cw = [cols(fcw[:, i].reshape(N_DEV, -1), 9) for i in range(2)]
    in_segs = (DI, CONVD, n_dt)
    up_segs = (FH, FH)
    pw1_segs = (D, D)

    c16 = jnp.concatenate([c_all, jnp.broadcast_to(c_ctx[None, :], (N_DEV, D))], axis=0)
    m_sh = mod_fwd(c16, mod_w, "mod_fwd")
    mod_cols = mod_w.shape[2]
    m_all = allgather_small(m_sh.reshape(2 * CROWS, mod_cols), "gather_mod")
    m_all = jnp.moveaxis(m_all.reshape(N_DEV, 2, CROWS, mod_cols), 0, 2).reshape(2, CROWS, 6 * D) + mod_b[:, None, :]
    m_lat = lax.dynamic_index_in_dim(m_all, me, axis=1, keepdims=False).reshape(2, 6, 1, D)
    m_ctx = m_all[:, N_DEV].reshape(2, 6, 1, D)
    zero_row = jnp.zeros((1, D), f32)

    def ffn_fwd(h, i, tag, hosted=None):
        a2 = modnorm_fwd(h, norm2_w[i][None], m_lat[i, 4][None], m_lat[i, 3][None], 0, f"ffn{tag}_norm")
        val, gate = smm_fwd(a2, w_up[i], None, up_segs, f"ffn{tag}_up")
        act, gs_, gvds, extra = ffn_gate_fwd(val, gate, ffn_cw[i], ffn_conv_b[i][None], f"ffn{tag}_gate", hosted)
        o2 = matmul(act, w_down[i], "nn", f32, f"ffn{tag}_down")
        h_new = resgate_fwd(h, o2, m_lat[i, 5], zero_row, f"ffn{tag}_res")
        return h_new, (a2, gate, gs_, gvds, act, o2), extra

    def ffn_bwd(dh, h, i, saved, tag):
        a2, gate, gs_, gvds, act, o2 = saved
        do2, dg2, _ = resgate_bwd(dh, o2, m_lat[i, 5], zero_row, f"ffn{tag}_res_bwd")
        g_down = matmul(act, do2, "tn", bf16, f"ffn{tag}_down_dw")
        dact = matmul(do2, w_down[i], "nt", bf16, f"ffn{tag}_down_dx")
        dval, dgate, dcw, dcb = ffn_gate_bwd(gate, gs_, gvds, ffn_cw[i], dact, f"ffn{tag}_gate_bwd")
        g_up = smm_dw(a2, [dval, dgate], FH // 4, up_segs, 2, True, f"ffn{tag}_up_dw")
        da2, _ = smm_dx([dval, dgate], w_up[i], None, up_segs, bf16, f"ffn{tag}_up_dx")
        dh_in, dn2, dsc2, dsh2 = modnorm_bwd(h, norm2_w[i][None], m_lat[i, 4][None], m_lat[i, 3][None], da2, dh, 0,
                                             f"ffn{tag}_norm_bwd")
        return dh_in, dict(w_up=g_up, w_down=g_down, conv_w=dcw, conv_b=dcb, norm2=dn2, sh2=dsh2[0], sc2=dsc2[0], g2=dg2)

    nctx = LC // Q
    hx = x[0]
    sc0 = jnp.stack([m_ctx[0, 1], m_lat[0, 1]])
    sh0 = jnp.stack([m_ctx[0, 0], m_lat[0, 0]])
    a0 = modnorm_fwd(hx, norm1_w[0][None], sc0, sh0, LC // TB, "ssd_norm", ctx=ctx[0])
    (z, xbc_pre, dt_raw), (w_out_g, w_pw2_g) = smm_fwd(a0, w_in, None, in_segs, "ssd_in", gather_in_proj)
    w_out = w_out_g.reshape(DI, D)
    w_pw2 = w_pw2_g.reshape(D, D)
    segs = ((0, LC), (LC, L))
    xbc, xbc_dsilu, (w_down0_g, w_pw1) = ssd_conv_fwd(xbc_pre, conv_w5, ssd_conv_b, segs, "ssd_conv", gather_in_conv)
    w_down[0] = w_down0_g.reshape(FH, D)
    dt4 = dt_raw[:, :n_dt].reshape(T, 2, G, HPG)
    dtc = jnp.transpose(dt4, (1, 2, 0, 3))
    dtr = jnp.transpose(dt4, (1, 2, 3, 0))
    bias3 = ssd_dt_bias[0].reshape(2, G, HPG)
    alog3 = ssd_a_log[0].reshape(2, G, HPG)
    bc_, br_ = bias3[:, :, None, :], bias3[:, :, :, None]
    alc, alr = alog3[:, :, None, :], alog3[:, :, :, None]
    (y2, s_in_all), (w_up[0], w_up[1]) = ssd_scan_fwd(xbc, dtc, dtr, bc_, br_, alc, alr, nctx, "ssd_scan", gather_in_scan)
    dexp = jnp.repeat(ssd_d[0], P)[None, :]
    yn = ssd_gate_fwd(y2, xbc, z, dexp, ssd_norm_w, LC // GTB, "ssd_gate")
    o_ssd = matmul(yn, w_out, "nn", f32, "ssd_out")
    h1 = resgate_fwd(hx, o_ssd, m_lat[0, 2], zero_row, "ssd_res")
    h2, ffn0_saved, (w_down1_g,) = ffn_fwd(h1, 0, "0", gather_in_gate)
    w_down[1] = w_down1_g.reshape(FH, D)

    a1 = modnorm_fwd(h2, norm1_w[1][None], m_lat[1, 1][None], m_lat[1, 0][None], 0, "conf_norm")
    pa, pg = smm_fwd(a1, w_pw1, None, pw1_segs, "conf_pw1")
    dwc, _ = conf_glu_conv_fwd(pa, pg, b_pw1, w_dw, b_dw, "conf_conv")
    s1 = ln_silu_fwd(dwc, ln_w, ln_b, "conf_ln")
    o_conf = matmul(s1, w_pw2, "nn", f32, "conf_pw2")
    h3 = resgate_fwd(h2, o_conf, m_lat[1, 2], b_pw2, "conf_res")
    h4, ffn1_saved, _ = ffn_fwd(h3, 1, "1")

    loss_part, dh4, g_final = final_loss(h4, final_norm_w[None], loss_target[0], "loss_head")
    dh3, gf1 = ffn_bwd(dh4, h3, 1, ffn1_saved, "1")

    do_conf, dg1_1, g_b_pw2 = resgate_bwd(dh3, o_conf, m_lat[1, 2], b_pw2, "conf_res_bwd")
    g_pw2 = matmul(s1, do_conf, "tn", bf16, "conf_pw2_dw")
    ds1 = matmul(do_conf, w_pw2, "nt", bf16, "conf_pw2_dx")
    ddwc, g_ln_w, g_ln_b = ln_silu_bwd(dwc, ln_w, ln_b, ds1, "conf_ln_bwd")
    dpa, dpg, dba, dbg, g_w_dw, g_b_dw = conf_glu_conv_bwd(pa, pg, b_pw1, w_dw, ddwc, "conf_conv_bwd")
    g_b_pw1 = jnp.concatenate([dba, dbg], axis=1)
    g_pw1 = smm_dw(a1, [dpa, dpg], 2 * D // N_DEV, pw1_segs, 1, False, "conf_pw1_dw")
    da1, _ = smm_dx([dpa, dpg], w_pw1, None, pw1_segs, bf16, "conf_pw1_dx")
    dh2, g_n1_1, dsc1_1, dsh1_1 = modnorm_bwd(h2, norm1_w[1][None], m_lat[1, 1][None], m_lat[1, 0][None], da1, dh3, 0,
                                              "conf_norm_bwd")
    dh1, gf0 = ffn_bwd(dh2, h1, 0, ffn0_saved, "0")

    do_ssd, dg1_0, _ = resgate_bwd(dh1, o_ssd, m_lat[0, 2], zero_row, "ssd_res_bwd")
    g_w_out = matmul(yn, do_ssd, "tn", bf16, "ssd_out_dw")
    dyn = matmul(do_ssd, w_out, "nt", bf16, "ssd_out_dx")
    core = mc.reshape(1).astype(jnp.int32)

    def by_device(t):
        return t.reshape((4, 2, -1, t.shape[-1]))

    early = [by_device(t) for t in (gf1["w_up"], gf1["w_down"], g_pw2, g_pw1, gf0["w_up"], gf0["w_down"], g_w_out)]
    (dy, dz, g_dexp, g_ssd_norm), early_sib = ssd_gate_bwd(
        y2, xbc, z, dexp, ssd_norm_w, dyn, LC // GTB, "ssd_gate_bwd", SiblingExchange(early))
    early_part = [add_own(t, r_, core, f"reduce_add{i}") for i, (t, r_) in enumerate(zip(early, early_sib))]
    (dxbc2, ddtc, ddtr, dbc, dbr, dalc, dalr), early_red = ssd_scan_bwd(
        xbc, dtc, dtr, bc_, br_, alc, alr, s_in_all, dy, nctx, "ssd_scan_bwd", ChipsExchange(early_part))
    ddt = (jnp.transpose(ddtc, (2, 0, 1, 3)) + jnp.transpose(ddtr, (3, 0, 1, 2))).reshape(T, n_dt)
    g_dt_bias = (dbc[:, :, 0, :] + dbr[:, :, :, 0]).reshape(2, NH_SSD)
    g_a_log = (dalc[:, :, 0, :] + dalr[:, :, :, 0]).reshape(2, NH_SSD)
    g_ssd_d = g_dexp[0, :NH_SSD]
    du, g_conv_w5, g_conv_b5 = ssd_conv_bwd(xbc_pre, conv_w5, xbc_dsilu, dxbc2, dy, dexp, segs, "ssd_conv_bwd")
    ddt_p = _pad_to(ddt, 128).astype(bf16)
    g_w_in = smm_dw(a0, [dz, du, ddt_p], w_in.shape[-1], in_segs, 2, True, "ssd_in_dw")
    g_ffn_cw = jnp.stack([gf0["conv_w"], gf1["conv_w"]])
    small_shards = [_col_shards(t) for t in (g_conv_w5, g_b_pw1, g_w_dw, g_b_dw, g_ln_w, g_ln_b, g_b_pw2, g_ffn_cw)]
    gsizes = [s_.shape[1] for s_ in small_shards]
    g_small = _pad_to(jnp.concatenate(small_shards, axis=1), PACK_ALIGN).astype(bf16)
    late = [by_device(g_w_in), by_device(g_small.reshape(N_DEV, -1, PACK_W))]
    da0, late_sib = smm_dx([dz, du, ddt_p], w_in, None, in_segs, f32, "ssd_in_dx", SiblingExchange(late))
    late_part = [add_own(t, r_, core, f"reduce_add_late{i}") for i, (t, r_) in enumerate(zip(late, late_sib))]
    late_red = exchange(ChipsExchange(late_part), "reduce_chips_late")
    dh0, g_n1_0, dsc1_0, dsh1_0 = modnorm_bwd(hx, norm1_w[0][None], sc0, sh0, da0, dh1, LC // TB, "ssd_norm_bwd",
                                              ctx=ctx[0])
    grad_x = dh0[None]

    zeros_d = jnp.zeros((1, D), f32)
    dm_lat = jnp.stack([
        jnp.concatenate([dsh1_0[1], dsc1_0[1], dg1_0, gf0["sh2"], gf0["sc2"], gf0["g2"]], axis=1),
        jnp.concatenate([dsh1_1[0], dsc1_1[0], dg1_1, gf1["sh2"], gf1["sc2"], gf1["g2"]], axis=1)])
    dm_ctx = jnp.stack([
        jnp.concatenate([dsh1_0[0], dsc1_0[0]] + [zeros_d] * 4, axis=1), jnp.zeros((1, 6 * D), f32)])
    dm_mine = jnp.concatenate([dm_lat.reshape(2, 6 * D), dm_ctx.reshape(2, 6 * D),
                               jnp.zeros((4, 6 * D), f32)], axis=0)
    dm_g = allgather_small(dm_mine, "gather_dmod")
    dm_all = jnp.concatenate([jnp.moveaxis(dm_g[:, 0:2], 0, 1), jnp.moveaxis(dm_g[:, 2:4], 0, 1)], axis=1)
    dm_sh = lax.dynamic_slice_in_dim(dm_all, me * mod_cols, mod_cols, axis=2)
    g_mod_w, g_cctx_part, g_mod_b = mod_bwd(c16, mod_w, dm_sh, dm_all, "mod_bwd")

    rep = [jnp.stack([g_n1_0[0], g_n1_1[0]]), jnp.stack([gf0["norm2"][0], gf1["norm2"][0]]), g_conv_b5, g_dt_bias, g_a_log,
           g_ssd_d, g_ssd_norm, jnp.stack([gf0["conv_b"][0], gf1["conv_b"][0]]), g_final, g_cctx_part, loss_part[:, :1]]
    rep_sizes = [r_.size for r_ in rep]
    rep_flat = _pad_to(jnp.concatenate([r_.reshape(-1) for r_ in rep]), 8 * PACK_W).reshape(-1, PACK_W)
    _, rep_sum = allgather_small(rep_flat, "reduce_replicated", with_sum=True)
    rep_sum = rep_sum.reshape(-1)
    roffs = [0]
    for s_ in rep_sizes:
        roffs.append(roffs[-1] + s_)
    rp = [rep_sum[roffs[i]:roffs[i + 1]] for i in range(len(rep_sizes))]
    loss = rp[10].reshape(())

    r_up1, r_down1, r_pw2, r_pw1, r_up0, r_down0, r_out = early_red
    r_in, r_small = late_red
    from_chips = [r_in, r_up0, r_up1, r_pw1, r_out, r_down0, r_down1, r_pw2]
    g_flat = sum_rows(r_small, "reduce_sum_small").reshape(-1)
    goffs = [0]
    for s_ in gsizes:
        goffs.append(goffs[-1] + s_)
    gs = [g_flat[goffs[i]:goffs[i + 1]] for i in range(len(gsizes))]

    big = {}
    def tr(t):
        return jnp.swapaxes(t, -1, -2)

    w_in_res = sum_adamw(from_chips[0], tr(ssd_w_in[0]), tr(m_ssd_w_in[0]), tr(v_ssd_w_in[0]), None, "adamw_ssd_w_in")
    big["ssd_w_in"] = tuple(tr(t) for t in w_in_res)
    up_t, m_up_t, v_up_t = tr(ffn_w_up), tr(m_ffn_w_up), tr(v_ffn_w_up)
    up0 = sum_adamw(from_chips[1], up_t, m_up_t, v_up_t, 0, "adamw_ffn_w_up0")
    big["ffn_w_up"] = tuple(tr(t) for t in sum_adamw(from_chips[2], up_t, m_up_t, v_up_t, 1, "adamw_ffn_w_up1", into=up0))
    big["conf_w_pw1"] = sum_adamw(from_chips[3], conf_w_pw1[0], m_conf_w_pw1[0], v_conf_w_pw1[0], None, "adamw_conf_w_pw1")
    big["ssd_w_out"] = sum_adamw(from_chips[4], ssd_w_out[0], m_ssd_w_out[0], v_ssd_w_out[0], None, "adamw_ssd_w_out")
    dn0 = sum_adamw(from_chips[5], ffn_w_down, m_ffn_w_down, v_ffn_w_down, 0, "adamw_ffn_w_down0")
    big["ffn_w_down"] = sum_adamw(from_chips[6], ffn_w_down, m_ffn_w_down, v_ffn_w_down, 1, "adamw_ffn_w_down1", into=dn0)
    big["conf_w_pw2"] = sum_adamw(from_chips[7], conf_w_pw2[0], m_conf_w_pw2[0], v_conf_w_pw2[0], None, "adamw_conf_w_pw2")
    grads = {
        "c_ctx": rp[9], "mod_w": g_mod_w, "mod_b": g_mod_b, "norm1_w": rp[0], "norm2_w": rp[1],
        "ssd_conv_w": gs[0], "ssd_conv_b": rp[2], "ssd_dt_bias": rp[3], "ssd_a_log": rp[4], "ssd_d": rp[5],
        "ssd_norm_w": rp[6], "conf_b_pw1": gs[1], "conf_w_dw": gs[2],
        "conf_b_dw": gs[3], "conf_ln_w": gs[4], "conf_ln_b": gs[5], "conf_b_pw2": gs[6],
        "ffn_conv_w": gs[7], "ffn_conv_b": rp[7], "final_norm_w": rp[8],
    }
    weights = dict(c_ctx=c_ctx, mod_w=mod_w, mod_b=mod_b, norm1_w=norm1_w, norm2_w=norm2_w, ssd_w_in=ssd_w_in, ssd_conv_w=ssd_conv_w, ssd_conv_b=ssd_conv_b, ssd_dt_bias=ssd_dt_bias, ssd_a_log=ssd_a_log, ssd_d=ssd_d, ssd_norm_w=ssd_norm_w, ssd_w_out=ssd_w_out, conf_w_pw1=conf_w_pw1, conf_b_pw1=conf_b_pw1, conf_w_dw=conf_w_dw, conf_b_dw=conf_b_dw, conf_ln_w=conf_ln_w, conf_ln_b=conf_ln_b, conf_w_pw2=conf_w_pw2, conf_b_pw2=conf_b_pw2, ffn_w_up=ffn_w_up, ffn_conv_w=ffn_conv_w, ffn_conv_b=ffn_conv_b, ffn_w_down=ffn_w_down, final_norm_w=final_norm_w)
    m_in = dict(c_ctx=m_c_ctx, mod_w=m_mod_w, mod_b=m_mod_b, norm1_w=m_norm1_w, norm2_w=m_norm2_w, ssd_w_in=m_ssd_w_in, ssd_conv_w=m_ssd_conv_w, ssd_conv_b=m_ssd_conv_b, ssd_dt_bias=m_ssd_dt_bias, ssd_a_log=m_ssd_a_log, ssd_d=m_ssd_d, ssd_norm_w=m_ssd_norm_w, ssd_w_out=m_ssd_w_out, conf_w_pw1=m_conf_w_pw1, conf_b_pw1=m_conf_b_pw1, conf_w_dw=m_conf_w_dw, conf_b_dw=m_conf_b_dw, conf_ln_w=m_conf_ln_w, conf_ln_b=m_conf_ln_b, conf_w_pw2=m_conf_w_pw2, conf_b_pw2=m_conf_b_pw2, ffn_w_up=m_ffn_w_up, ffn_conv_w=m_ffn_conv_w, ffn_conv_b=m_ffn_conv_b, ffn_w_down=m_ffn_w_down, final_norm_w=m_final_norm_w)
    v_in = dict(c_ctx=v_c_ctx, mod_w=v_mod_w, mod_b=v_mod_b, norm1_w=v_norm1_w, norm2_w=v_norm2_w, ssd_w_in=v_ssd_w_in, ssd_conv_w=v_ssd_conv_w, ssd_conv_b=v_ssd_conv_b, ssd_dt_bias=v_ssd_dt_bias, ssd_a_log=v_ssd_a_log, ssd_d=v_ssd_d, ssd_norm_w=v_ssd_norm_w, ssd_w_out=v_ssd_w_out, conf_w_pw1=v_conf_w_pw1, conf_b_pw1=v_conf_b_pw1, conf_w_dw=v_conf_w_dw, conf_b_dw=v_conf_b_dw, conf_ln_w=v_conf_ln_w, conf_ln_b=v_conf_ln_b, conf_w_pw2=v_conf_w_pw2, conf_b_pw2=v_conf_b_pw2, ffn_w_up=v_ffn_w_up, ffn_conv_w=v_ffn_conv_w, ffn_conv_b=v_ffn_conv_b, ffn_w_down=v_ffn_w_down, final_norm_w=v_final_norm_w)

    out_g, out_d, out_m, out_v = [], [], [], []
    for name_, w_ in weights.items():
        shape = w_.shape
        if name_ in big:
            for lst, t in zip((out_g, out_d, out_m, out_v), big[name_]):
                lst.append(t.reshape(shape))
            continue
        cols2 = shape[-1] if len(shape) > 1 else shape[0]
        g2 = grads[name_].reshape(-1, cols2)
        d_, nm_, nv_ = adamw(w_.reshape(-1, cols2), g2, m_in[name_].reshape(-1, cols2), v_in[name_].reshape(-1, cols2),
                             f"adamw_{name_}")
        out_g.append(g2.reshape(shape))
        out_d.append(d_.reshape(shape))
        out_m.append(nm_.reshape(shape))
        out_v.append(nv_.reshape(shape))
    return (loss, grad_x, *out_g, *out_d, *out_m, *out_v)
```

```python
import functools

import jax
import jax.numpy as jnp
from jax import lax
from jax.experimental import pallas as pl
from jax.experimental.pallas import tpu as pltpu

f32 = jnp.float32
bf16 = jnp.bfloat16
HI = lax.Precision.HIGHEST
S = jax.ShapeDtypeStruct
MESH = pl.DeviceIdType.MESH
ANY = pl.BlockSpec(memory_space=pl.ANY)
VMEM = pl.BlockSpec(memory_space=pltpu.VMEM)

N_DEV = 8
D = 1024
DI = 2048
CONVD = 4096
FH = 2816
GRID_W = 64
Q = 128
HPG = 4
P = 64
N = 128
G = 8
GW = HPG * P
NH_SSD = G * HPG
EPS = 1e-6
ADAM_LR, ADAM_B1, ADAM_B2, ADAM_EPS, ADAM_WD, ADAM_STEP = 0.001, 0.9, 0.999, 1e-08, 0.01, 10
VMEM_LIMIT_BYTES = 56 * 1024 * 1024
PACK_W = 1024
TB = 256


def _cparams(*sem):
    return pltpu.CompilerParams(dimension_semantics=sem, vmem_limit_bytes=VMEM_LIMIT_BYTES)


def _pick(n, prefs):
    for p in prefs:
        if n % p == 0:
            return p
    return n


def _sigmoid(x):
    return 1.0 / (1.0 + jnp.exp(-x))


def _softplus(x):
    return jnp.maximum(x, 0.0) + jnp.log(1.0 + jnp.exp(-jnp.abs(x)))


def matmul(a, b, mode, out_dtype, name):
    if mode == "nn":
        (M, K), (_, Nn) = a.shape, b.shape
        bm, bn, bk = _pick(M, (512, 384, 256, 128)), Nn, K
    elif mode == "tn":
        (K, M), (_, Nn) = a.shape, b.shape
        bm, bn, bk = M, Nn, _pick(K, (256, 128))
    else:
        (M, K), (Nn, _) = a.shape, b.shape
        bm, bn, bk = _pick(M, (512, 384, 256, 128)), Nn, K
    nk = K // bk
    dims = {"nn": (((1,), (0,)), ((), ())), "tn": (((0,), (0,)), ((), ())), "nt": (((1,), (1,)), ((), ()))}[mode]

    def body(a_ref, b_ref, o_ref, acc_ref):
        k = pl.program_id(2)

        @pl.when(k == 0)
        def _():
            acc_ref[...] = jnp.zeros_like(acc_ref)

        acc_ref[...] += lax.dot_general(a_ref[...].astype(bf16), b_ref[...].astype(bf16), dims,
                                        preferred_element_type=f32)

        @pl.when(k == nk - 1)
        def _():
            o_ref[...] = acc_ref[...].astype(out_dtype)

    if mode == "nn":
        a_spec = pl.BlockSpec((bm, bk), lambda i, j, k: (i, k))
        b_spec = pl.BlockSpec((bk, bn), lambda i, j, k: (k, j))
    elif mode == "tn":
        a_spec = pl.BlockSpec((bk, bm), lambda i, j, k: (k, i))
        b_spec = pl.BlockSpec((bk, bn), lambda i, j, k: (k, j))
    else:
        a_spec = pl.BlockSpec((bm, bk), lambda i, j, k: (i, k))
        b_spec = pl.BlockSpec((bn, bk), lambda i, j, k: (j, k))
    return pl.pallas_call(
        body, grid=(M // bm, Nn // bn, nk), in_specs=[a_spec, b_spec],
        out_specs=pl.BlockSpec((bm, bn), lambda i, j, k: (i, j)),
        out_shape=S((M, Nn), out_dtype), scratch_shapes=[pltpu.VMEM((bm, bn), f32)],
        compiler_params=_cparams("parallel", "parallel", "arbitrary"), name=name,
    )(a, b)


SMM_BM = 256
SMM_ROWS = (256,)


def _shard_pieces(seg_widths, n):
    bounds = [0]
    for sw in seg_widths:
        bounds.append(bounds[-1] + sw)
    assert bounds[-1] == N_DEV * n, (seg_widths, n)
    out = []
    for j in range(N_DEV):
        lo, hi = j * n, (j + 1) * n
        pcs = []
        for si in range(len(seg_widths)):
            a, b = max(lo, bounds[si]), min(hi, bounds[si + 1])
            if a < b:
                pcs.append((si, a - bounds[si], a - lo, b - a))
        out.append(pcs)
    return out


def _w_spec(w, layer):
    if layer is None:
        return pl.BlockSpec(w.shape, lambda *idx: (0, 0, 0))
    return pl.BlockSpec((N_DEV, None) + w.shape[2:], lambda *idx: (0, layer, 0, 0))


def smm_fwd(a, w, layer, seg_widths, name, hosted=None):
    M, K = a.shape
    n = w.shape[-1]
    pieces = _shard_pieces(seg_widths, n)
    padded = [sw + (-sw) % 128 for sw in seg_widths]
    bm = _pick(M, SMM_ROWS)

    def body(a_ref, w_ref, *o_refs):
        av = a_ref[...]
        for si, sw in enumerate(seg_widths):
            if padded[si] != sw:
                o_refs[si][:, pl.ds(padded[si] - 128, 128)] = jnp.zeros((bm, 128), f32)
        for j in range(N_DEV):
            for si, soff, woff, wd in pieces[j]:
                o_refs[si][:, pl.ds(soff, wd)] = jnp.dot(av, w_ref[j, :, pl.ds(woff, wd)], preferred_element_type=f32)

    outs, extra = _host_call(
        body, (M // bm,), [pl.BlockSpec((bm, K), lambda i: (i, 0)), _w_spec(w, layer)],
        [pl.BlockSpec((bm, pw), lambda i: (i, 0)) for pw in padded], [S((M, pw), f32) for pw in padded], [],
        ("parallel",), name, (a, w), hosted)
    return outs if hosted is None else (outs, extra)


def smm_dx(d_segs, w, layer, seg_widths, out_dtype, name, hosted=None):
    M = d_segs[0].shape[0]
    K, n = w.shape[-2], w.shape[-1]
    pieces = _shard_pieces(seg_widths, n)
    ns = len(d_segs)
    bm = _pick(M, SMM_ROWS)

    def body(*refs):
        d_refs, w_ref, o_ref = refs[:ns], refs[ns], refs[ns + 1]
        acc = jnp.zeros((bm, K), f32)
        for j in range(N_DEV):
            for si, soff, woff, wd in pieces[j]:
                acc = acc + lax.dot_general(d_refs[si][:, pl.ds(soff, wd)], w_ref[j, :, pl.ds(woff, wd)],
                                            (((1,), (1,)), ((), ())), preferred_element_type=f32)
        o_ref[...] = acc.astype(out_dtype)

    (out,), extra = _host_call(
        body, (M // bm,),
        [pl.BlockSpec((bm, d.shape[1]), lambda i: (i, 0)) for d in d_segs] + [_w_spec(w, layer)],
        [pl.BlockSpec((bm, K), lambda i: (i, 0))], [S((M, K), out_dtype)], [], ("parallel",), name,
        (*d_segs, w), hosted)
    return out, extra


def smm_dw(a, d_segs, n, seg_widths, ngrp, transposed, name):
    M, K = a.shape
    pieces = _shard_pieces(seg_widths, n)
    per = N_DEV // ngrp
    nI = M // SMM_BM
    ns = len(d_segs)
    shard = (n, K) if transposed else (K, n)

    def body(*refs):
        a_ref, d_refs, o_ref, acc_ref = refs[0], refs[1:1 + ns], refs[1 + ns], refs[2 + ns]
        grp = pl.program_id(0)
        i = pl.program_id(1)

        @pl.when(i == 0)
        def _():
            acc_ref[...] = jnp.zeros_like(acc_ref)

        av = a_ref[...]
        for gs in range(ngrp):
            def one_group(gs=gs):
                for jj in range(per):
                    for si, soff, woff, wd in pieces[gs * per + jj]:
                        dv = d_refs[si][:, pl.ds(soff, wd)]
                        if transposed:
                            acc_ref[jj, pl.ds(woff, wd), :] += lax.dot_general(
                                dv, av, (((0,), (0,)), ((), ())), preferred_element_type=f32)
                        else:
                            acc_ref[jj, :, pl.ds(woff, wd)] += lax.dot_general(
                                av, dv, (((0,), (0,)), ((), ())), preferred_element_type=f32)
            pl.when(grp == gs)(one_group)

        @pl.when(i == nI - 1)
        def _():
            o_ref[...] = acc_ref[...].astype(bf16)

    return pl.pallas_call(
        body, grid=(ngrp, nI),
        in_specs=[pl.BlockSpec((SMM_BM, K), lambda g, i: (i, 0))]
        + [pl.BlockSpec((SMM_BM, d.shape[1]), lambda g, i: (i, 0)) for d in d_segs],
        out_specs=pl.BlockSpec((per,) + shard, lambda g, i: (g, 0, 0)), out_shape=S((N_DEV,) + shard, bf16),
        scratch_shapes=[pltpu.VMEM((per,) + shard, f32)],
        compiler_params=_cparams("arbitrary", "arbitrary"), name=name)(a, *d_segs)


def _modnorm_f(h, w, sc, sh):
    y = h * lax.rsqrt(jnp.mean(h * h, axis=-1, keepdims=True) + EPS)
    return (y * w) * (1.0 + sc) + sh


def _kind_specs(nctxb):
    if nctxb > 0:
        return pl.BlockSpec((None, 1, D), lambda i: (jnp.where(i < nctxb, 0, 1), 0, 0))
    return pl.BlockSpec((None, 1, D), lambda i: (0, 0, 0))


def _two_part_specs(nctxb):
    return (pl.BlockSpec((TB, D), lambda i: (jnp.minimum(i, nctxb - 1), 0)),
            pl.BlockSpec((TB, D), lambda i: (jnp.maximum(i - nctxb, 0), 0)))


def modnorm_fwd(h, w, sc, sh, nctxb, name, ctx=None):
    if ctx is None:
        T = h.shape[0]

        def body(h_ref, w_ref, sc_ref, sh_ref, o_ref):
            o_ref[...] = _modnorm_f(h_ref[...], w_ref[...], sc_ref[...], sh_ref[...]).astype(bf16)

        hspecs, hargs = [pl.BlockSpec((TB, D), lambda i: (i, 0))], (h,)
    else:
        T = h.shape[0] + ctx.shape[0]

        def body(c_ref, h_ref, w_ref, sc_ref, sh_ref, o_ref):
            hv = jnp.where(pl.program_id(0) < nctxb, c_ref[...], h_ref[...])
            o_ref[...] = _modnorm_f(hv, w_ref[...], sc_ref[...], sh_ref[...]).astype(bf16)

        hspecs, hargs = list(_two_part_specs(nctxb)), (ctx, h)
    row = pl.BlockSpec((1, D), lambda i: (0, 0))
    ks = _kind_specs(nctxb)
    return pl.pallas_call(body, grid=(T // TB,), in_specs=hspecs + [row, ks, ks],
                          out_specs=pl.BlockSpec((TB, D), lambda i: (i, 0)), out_shape=S((T, D), bf16),
                          compiler_params=_cparams("parallel"), name=name)(*hargs, w, sc, sh)


def modnorm_bwd(h, w, sc, sh, da, dres, nctxb, name, ctx=None):
    T = h.shape[0] + (0 if ctx is None else ctx.shape[0])
    kinds = sc.shape[0]
    nh = 1 if ctx is None else 2

    def body(*refs):
        w_ref, sc_ref, sh_ref, da_ref, dres_ref, dh_ref, dw_ref, dsc_ref, dsh_ref = refs[nh:]
        i = pl.program_id(0)
        hv = refs[0][...] if ctx is None else jnp.where(i < nctxb, refs[0][...], refs[1][...])
        _, vjp = jax.vjp(_modnorm_f, hv, w_ref[...], sc_ref[...], sh_ref[...])
        dh, dw, dsc, dsh = vjp(da_ref[...].astype(f32))
        dh_ref[...] = dres_ref[...] + dh

        @pl.when(i == 0)
        def _():
            dw_ref[...] = jnp.zeros_like(dw_ref)

        @pl.when((i == 0) | (i == nctxb))
        def _():
            dsc_ref[...] = jnp.zeros_like(dsc_ref)
            dsh_ref[...] = jnp.zeros_like(dsh_ref)

        dw_ref[...] += dw
        dsc_ref[...] += dsc
        dsh_ref[...] += dsh

    blk = pl.BlockSpec((TB, D), lambda i: (i, 0))
    lat = pl.BlockSpec((TB, D), lambda i: (jnp.maximum(i - nctxb, 0), 0))
    row = pl.BlockSpec((1, D), lambda i: (0, 0))
    ks = _kind_specs(nctxb)
    hspecs, hargs = ([blk], (h,)) if ctx is None else (list(_two_part_specs(nctxb)), (ctx, h))
    return pl.pallas_call(
        body, grid=(T // TB,), in_specs=hspecs + [row, ks, ks, blk, lat], out_specs=[lat, row, ks, ks],
        out_shape=[S((T - nctxb * TB, D), f32), S((1, D), f32), S((kinds, 1, D), f32), S((kinds, 1, D), f32)],
        compiler_params=_cparams("arbitrary"), name=name)(*hargs, w, sc, sh, da, dres)


def resgate_fwd(h, o, g, b, name):
    T = h.shape[0]

    def body(h_ref, o_ref, g_ref, b_ref, out_ref):
        out_ref[...] = h_ref[...] + g_ref[...] * (o_ref[...] + b_ref[...])

    blk = pl.BlockSpec((TB, D), lambda i: (i, 0))
    row = pl.BlockSpec((1, D), lambda i: (0, 0))
    return pl.pallas_call(body, grid=(T // TB,), in_specs=[blk, blk, row, row], out_specs=blk,
                          out_shape=S((T, D), f32), compiler_params=_cparams("parallel"), name=name)(h, o, g, b)


def resgate_bwd(dh, o, g, b, name):
    T = dh.shape[0]

    def body(dh_ref, o_ref, g_ref, b_ref, do_ref, dg_ref, db_ref):
        i = pl.program_id(0)

        @pl.when(i == 0)
        def _():
            dg_ref[...] = jnp.zeros_like(dg_ref)
            db_ref[...] = jnp.zeros_like(db_ref)

        dh = dh_ref[...]
        do = g_ref[...] * dh
        do_ref[...] = do.astype(bf16)
        dg_ref[...] += jnp.sum(dh * (o_ref[...] + b_ref[...]), axis=0, keepdims=True)
        db_ref[...] += jnp.sum(do, axis=0, keepdims=True)

    blk = pl.BlockSpec((TB, D), lambda i: (i, 0))
    row = pl.BlockSpec((1, D), lambda i: (0, 0))
    return pl.pallas_call(body, grid=(T // TB,), in_specs=[blk, blk, row, row], out_specs=[blk, row, row],
                          out_shape=[S((T, D), bf16), S((1, D), f32), S((1, D), f32)],
                          compiler_params=_cparams("arbitrary"), name=name)(dh, o, g, b)


def final_loss(h, w, tgt, name):
    T = h.shape[0]

    def f(hv, wv, tv):
        y = (hv * lax.rsqrt(jnp.mean(hv * hv, axis=-1, keepdims=True) + EPS)) * wv
        e = y - tv
        return 0.5 * jnp.sum(jnp.sum(e * e, axis=-1, keepdims=True), axis=0, keepdims=True) * (1.0 / D)

    def body(h_ref, w_ref, t_ref, loss_ref, dh_ref, dw_ref):
        i = pl.program_id(0)
        tv = t_ref[...]
        val, vjp = jax.vjp(lambda a, b_: f(a, b_, tv), h_ref[...], w_ref[...])
        dh, dw = vjp(jnp.ones((1, 1), f32))
        dh_ref[...] = dh

        @pl.when(i == 0)
        def _():
            loss_ref[...] = jnp.zeros_like(loss_ref)
            dw_ref[...] = jnp.zeros_like(dw_ref)

        loss_ref[...] += jnp.broadcast_to(val, (1, 128))
        dw_ref[...] += dw

    blk = pl.BlockSpec((TB, D), lambda i: (i, 0))
    row = pl.BlockSpec((1, D), lambda i: (0, 0))
    return pl.pallas_call(body, grid=(T // TB,), in_specs=[blk, row, blk],
                          out_specs=[pl.BlockSpec((1, 128), lambda i: (0, 0)), blk, row],
                          out_shape=[S((1, 128), f32), S((T, D), f32), S((1, D), f32)],
                          compiler_params=_cparams("arbitrary"), name=name)(h, w, tgt)


CB = 256
RT = 32
RTB = 16


def _fold8(t):
    acc = t[0:8]
    for k in range(1, t.shape[0] // 8):
        acc = acc + t[8 * k:8 * (k + 1)]
    return acc


def _rows(start, off=0, rt=RT):
    return pl.ds(pl.multiple_of(start + off, 8), rt)


def _rowsb(start, off=0):
    return _rows(start, off, RTB)


def _zero_rows(ref, start, n):
    ref[pl.ds(start, n), :] = jnp.zeros((n, ref.shape[1]), f32)


K5, HALF5, PAD5 = 5, 2, 8


def _shift_copies5(base_ref, s_ref, ln, sign):
    for k in range(K5):
        s_ref[k, pl.ds(0, ln), :] = base_ref[pl.ds(PAD5 + sign * (k - HALF5), ln), :]


def ssd_conv_fwd(u, w, b, segs, name, hosted=None):
    T = u.shape[0]
    maxlen = max(ln for _, ln in segs)

    def body(u_ref, w_ref, b_ref, o_ref, ds_ref, base_ref, s_ref):
        wv = [w_ref[pl.ds(k, 1), :] for k in range(K5)]
        bv = b_ref[...]
        for s0, ln in segs:
            _zero_rows(base_ref, 0, PAD5)
            _zero_rows(base_ref, PAD5 + ln, PAD5)
            base_ref[pl.ds(PAD5, ln), :] = u_ref[pl.ds(s0, ln), :]
            _shift_copies5(base_ref, s_ref, ln, 1)

            def tile(i, carry):
                r = i * RT
                acc = jnp.broadcast_to(bv, (RT, CB))
                for k in range(K5):
                    acc = acc + s_ref[k, _rows(r), :] * wv[k]
                sg = _sigmoid(acc)
                o_ref[_rows(r, s0), :] = acc * sg
                ds_ref[_rows(r, s0), :] = sg * (1.0 + acc * (1.0 - sg))
                return carry

            lax.fori_loop(0, ln // RT, tile, 0, unroll=2)

    cblk = pl.BlockSpec((T, CB), lambda j: (0, j))
    (out, dsilu), extra = _host_call(
        body, (CONVD // CB,),
        [cblk, pl.BlockSpec((K5, CB), lambda j: (0, j)), pl.BlockSpec((1, CB), lambda j: (0, j))],
        [cblk, cblk], [S((T, CONVD), f32), S((T, CONVD), f32)],
        [pltpu.VMEM((maxlen + 2 * PAD5, CB), f32), pltpu.VMEM((K5, maxlen, CB), f32)],
        ("parallel",), name, (u, w, b), hosted)
    return out, dsilu, extra


def ssd_conv_bwd(proj, w, dsilu, dy2, dyskip, dexp, segs, name):
    T = proj.shape[0]
    maxlen = max(ln for _, ln in segs)
    nskip = DI // CB

    def body(u_ref, w_ref, ds_ref, dya_ref, dyb_ref, dsk_ref, dexp_ref, du_ref, dw_ref, db_ref, base_ref, s_ref):
        wv = [w_ref[pl.ds(k, 1), :] for k in range(K5)]
        has_skip = (pl.program_id(0) < nskip).astype(f32) * dexp_ref[...]
        acc8 = tuple(jnp.zeros((8, CB), f32) for _ in range(K5 + 1))
        for s0, ln in segs:
            _zero_rows(base_ref, 0, PAD5)
            _zero_rows(base_ref, PAD5 + ln, PAD5)
            base_ref[pl.ds(PAD5, ln), :] = u_ref[pl.ds(s0, ln), :]
            _shift_copies5(base_ref, s_ref, ln, 1)

            def tile1(i, carry):
                r = i * RTB
                dy = dya_ref[_rowsb(r, s0), :] + dyb_ref[_rowsb(r, s0), :] + has_skip * dsk_ref[_rowsb(r, s0), :]
                dpre = dy * ds_ref[_rowsb(r, s0), :]
                base_ref[_rowsb(r, PAD5), :] = dpre
                new = [carry[k] + _fold8(dpre * s_ref[k, _rowsb(r), :]) for k in range(K5)]
                new.append(carry[K5] + _fold8(dpre))
                return tuple(new)

            acc8 = lax.fori_loop(0, ln // RTB, tile1, acc8, unroll=2)
            _shift_copies5(base_ref, s_ref, ln, -1)

            def tile2(i, carry):
                r = i * RTB
                du = jnp.zeros((RTB, CB), f32)
                for k in range(K5):
                    du = du + s_ref[k, _rowsb(r), :] * wv[k]
                du_ref[_rowsb(r, s0), :] = du.astype(bf16)
                return carry

            lax.fori_loop(0, ln // RTB, tile2, 0, unroll=4)
        for k in range(K5):
            dw_ref[pl.ds(k, 1), :] = jnp.sum(acc8[k], axis=0, keepdims=True)
        db_ref[...] = jnp.sum(acc8[K5], axis=0, keepdims=True)

    cblk = pl.BlockSpec((T, CB), lambda j: (0, j))
    return pl.pallas_call(
        body, grid=(CONVD // CB,),
        in_specs=[cblk, pl.BlockSpec((K5, CB), lambda j: (0, j)), cblk,
                  pl.BlockSpec((None, T, CB), lambda j: (0, 0, j)), pl.BlockSpec((None, T, CB), lambda j: (1, 0, j)),
                  pl.BlockSpec((T, CB), lambda j: (0, jnp.minimum(j, nskip - 1))),
                  pl.BlockSpec((1, CB), lambda j: (0, jnp.minimum(j, nskip - 1)))],
        out_specs=[cblk, pl.BlockSpec((K5, CB), lambda j: (0, j)), pl.BlockSpec((1, CB), lambda j: (0, j))],
        out_shape=[S((T, CONVD), bf16), S((K5, CONVD), f32), S((1, CONVD), f32)],
        scratch_shapes=[pltpu.VMEM((maxlen + 2 * PAD5, CB), f32), pltpu.VMEM((K5, maxlen, CB), f32)],
        compiler_params=_cparams("parallel"), name=name)(proj, w, dsilu, dy2, dy2, dyskip, dexp)


GPAD = GRID_W


def _grid_copies(g_ref, src, L):
    col = lax.broadcasted_iota(jnp.int32, (L, CB), 0) & (GRID_W - 1)
    for d in range(3):
        _zero_rows(g_ref.at[d], 0, GPAD)
        _zero_rows(g_ref.at[d], GPAD + L, GPAD)
    g_ref[1, pl.ds(GPAD, L), :] = src
    g_ref[0, pl.ds(GPAD, L), :] = jnp.where(col != 0, g_ref[1, pl.ds(GPAD - 1, L), :], 0.0)
    g_ref[2, pl.ds(GPAD, L), :] = jnp.where(col != GRID_W - 1, g_ref[1, pl.ds(GPAD + 1, L), :], 0.0)


def ffn_gate_fwd(val, gate, cw, cb_, name, hosted=None):
    L = val.shape[0]
    nb = FH // CB

    def body(val_ref, gate_ref, w_ref, b_ref, o_ref, s_ref, vds_ref, g_ref):
        wv = [w_ref[pl.ds(t, 1), :] for t in range(9)]
        bv = b_ref[...]
        _grid_copies(g_ref, gate_ref[...], L)

        def tile(i, carry):
            r = i * RT
            acc = jnp.broadcast_to(bv, (RT, CB))
            for dr in range(3):
                for dc in range(3):
                    acc = acc + g_ref[dc, _rows(r, GPAD + (dr - 1) * GRID_W), :] * wv[3 * dr + dc]
            sg = _sigmoid(acc)
            s = acc * sg
            v = val_ref[_rows(r), :]
            o_ref[_rows(r), :] = (s * v).astype(bf16)
            s_ref[_rows(r), :] = s
            vds_ref[_rows(r), :] = v * (sg * (1.0 + acc * (1.0 - sg)))
            return carry

        lax.fori_loop(0, L // RT, tile, 0, unroll=2)

    cblk = pl.BlockSpec((L, CB), lambda j: (0, j))
    (out, s_, vds), extra = _host_call(
        body, (nb,), [cblk, cblk, pl.BlockSpec((9, CB), lambda j: (0, j)), pl.BlockSpec((1, CB), lambda j: (0, j))],
        [cblk, cblk, cblk], [S((L, FH), bf16), S((L, FH), f32), S((L, FH), f32)],
        [pltpu.VMEM((3, L + 2 * GPAD, CB), f32)], ("parallel",), name, (val, gate, cw, cb_), hosted)
    return out, s_, vds, extra


def ffn_gate_bwd(gate, s_, vds, cw, dact, name):
    L = gate.shape[0]
    nb = FH // CB

    def body(gate_ref, s_ref, vds_ref, w_ref, da_ref, dval_ref, dgate_ref, dw_ref, db_ref, g_ref, d_ref):
        wv = [w_ref[pl.ds(t, 1), :] for t in range(9)]
        _grid_copies(g_ref, gate_ref[...], L)

        def tile1(i, carry):
            r = i * RTB
            da = da_ref[_rowsb(r), :].astype(f32)
            dval_ref[_rowsb(r), :] = (da * s_ref[_rowsb(r), :]).astype(bf16)
            dpre = da * vds_ref[_rowsb(r), :]
            d_ref[_rowsb(r), :] = dpre
            new = [carry[t] + _fold8(dpre * g_ref[t % 3, _rowsb(r, GPAD + (t // 3 - 1) * GRID_W), :]) for t in range(9)]
            new.append(carry[9] + _fold8(dpre))
            return tuple(new)

        acc8 = lax.fori_loop(0, L // RTB, tile1, tuple(jnp.zeros((8, CB), f32) for _ in range(10)), unroll=2)
        for t in range(9):
            dw_ref[pl.ds(t, 1), :] = jnp.sum(acc8[t], axis=0, keepdims=True)
        db_ref[...] = jnp.sum(acc8[9], axis=0, keepdims=True)
        _grid_copies(g_ref, d_ref[...], L)

        def tile2(i, carry):
            r = i * RTB
            dg = jnp.zeros((RTB, CB), f32)
            for dr in range(3):
                for dc in range(3):
                    dg = dg + g_ref[2 - dc, _rowsb(r, GPAD - (dr - 1) * GRID_W), :] * wv[3 * dr + dc]
            dgate_ref[_rowsb(r), :] = dg.astype(bf16)
            return carry

        lax.fori_loop(0, L // RTB, tile2, 0, unroll=4)

    cblk = pl.BlockSpec((L, CB), lambda j: (0, j))
    return pl.pallas_call(
        body, grid=(nb,),
        in_specs=[cblk, cblk, cblk, pl.BlockSpec((9, CB), lambda j: (0, j)), cblk],
        out_specs=[cblk, cblk, pl.BlockSpec((9, CB), lambda j: (0, j)), pl.BlockSpec((1, CB), lambda j: (0, j))],
        out_shape=[S((L, FH), bf16), S((L, FH), bf16), S((9, FH), f32), S((1, FH), f32)],
        scratch_shapes=[pltpu.VMEM((3, L + 2 * GPAD, CB), f32), pltpu.VMEM((L, CB), f32)],
        compiler_params=_cparams("parallel"), name=name)(gate, s_, vds, cw, dact)


CONF_K = 31
CHALF = CONF_K // 2
CPAD = 16


def _shift_copies8(c_ref, base_ref, L):
    n = L + 2 * CPAD - 8
    for b_ in range(8):
        c_ref[b_, pl.ds(0, n), :] = base_ref[pl.ds(b_, n), :]


def _tap_ab(o):
    return o % 8, o - o % 8


def conf_glu_conv_fwd(pa, pg, b1, wdw, bdw, name, hosted=None):
    L = pa.shape[0]
    nb = D // CB

    def body(pa_ref, pg_ref, ba_ref, bg_ref, w_ref, bdw_ref, o_ref, base_ref, c_ref):
        _zero_rows(base_ref, 0, CPAD)
        _zero_rows(base_ref, CPAD + L, CPAD)
        base_ref[pl.ds(CPAD, L), :] = (pa_ref[...] + ba_ref[...]) * _sigmoid(pg_ref[...] + bg_ref[...])
        _shift_copies8(c_ref, base_ref, L)
        bv = bdw_ref[...]

        def tile(i, carry):
            r = i * RT
            acc = jnp.broadcast_to(bv, (RT, CB))
            for k in range(CONF_K):
                b_, a8 = _tap_ab(k - CHALF)
                acc = acc + c_ref[b_, _rows(r, CPAD + a8), :] * w_ref[pl.ds(k, 1), :]
            o_ref[_rows(r), :] = acc
            return carry

        lax.fori_loop(0, L // RT, tile, 0, unroll=2)

    cblk = pl.BlockSpec((L, CB), lambda j: (0, j))
    rblk = pl.BlockSpec((1, CB), lambda j: (0, j))
    rgblk = pl.BlockSpec((1, CB), lambda j: (0, nb + j))
    (out,), extra = _host_call(
        body, (nb,), [cblk, cblk, rblk, rgblk, pl.BlockSpec((CONF_K, CB), lambda j: (0, j)), rblk],
        [cblk], [S((L, D), f32)], [pltpu.VMEM((L + 2 * CPAD, CB), f32), pltpu.VMEM((8, L + 2 * CPAD, CB), f32)],
        ("parallel",), name, (pa, pg, b1, b1, wdw, bdw), hosted)
    return out, extra


def conf_glu_conv_bwd(pa, pg, b1, wdw, dy, name):
    L = pa.shape[0]
    nb = D // CB

    def body(pa_ref, pg_ref, ba_ref, bg_ref, w_ref, dy_ref, dpa_ref, dpg_ref, dba_ref, dbg_ref, dw_ref, dbdw_ref,
             base_ref, c_ref, acc_ref):
        _zero_rows(base_ref, 0, CPAD)
        _zero_rows(base_ref, CPAD + L, CPAD)
        base_ref[pl.ds(CPAD, L), :] = (pa_ref[...] + ba_ref[...]) * _sigmoid(pg_ref[...] + bg_ref[...])
        _shift_copies8(c_ref, base_ref, L)
        acc_ref[...] = jnp.zeros_like(acc_ref)

        def tile1(i, carry):
            r = i * RTB
            dyt = dy_ref[_rowsb(r), :]
            for k in range(CONF_K):
                b_, a8 = _tap_ab(k - CHALF)
                acc_ref[k] += _fold8(dyt * c_ref[b_, _rowsb(r, CPAD + a8), :])
            return carry + _fold8(dyt)

        db8 = lax.fori_loop(0, L // RTB, tile1, jnp.zeros((8, CB), f32), unroll=2)
        dbdw_ref[...] = jnp.sum(db8, axis=0, keepdims=True)
        for k in range(CONF_K):
            dw_ref[pl.ds(k, 1), :] = jnp.sum(acc_ref[k], axis=0, keepdims=True)
        base_ref[pl.ds(CPAD, L), :] = dy_ref[...]
        _shift_copies8(c_ref, base_ref, L)
        ba = ba_ref[...]
        bg = bg_ref[...]

        def tile2(i, carry):
            r = i * RTB
            dglu = jnp.zeros((RTB, CB), f32)
            for k in range(CONF_K):
                b_, a8 = _tap_ab(CHALF - k)
                dglu = dglu + c_ref[b_, _rowsb(r, CPAD + a8), :] * w_ref[pl.ds(k, 1), :]
            a = pa_ref[_rowsb(r), :] + ba
            sg = _sigmoid(pg_ref[_rowsb(r), :] + bg)
            dpa = dglu * sg
            dpg = dglu * a * (sg * (1.0 - sg))
            dpa_ref[_rowsb(r), :] = dpa.astype(bf16)
            dpg_ref[_rowsb(r), :] = dpg.astype(bf16)
            return carry[0] + _fold8(dpa), carry[1] + _fold8(dpg)

        s8 = lax.fori_loop(0, L // RTB, tile2, (jnp.zeros((8, CB), f32), jnp.zeros((8, CB), f32)), unroll=2)
        dba_ref[...] = jnp.sum(s8[0], axis=0, keepdims=True)
        dbg_ref[...] = jnp.sum(s8[1], axis=0, keepdims=True)

    cblk = pl.BlockSpec((L, CB), lambda j: (0, j))
    rblk = pl.BlockSpec((1, CB), lambda j: (0, j))
    rgblk = pl.BlockSpec((1, CB), lambda j: (0, nb + j))
    wblk = pl.BlockSpec((CONF_K, CB), lambda j: (0, j))
    return pl.pallas_call(
        body, grid=(nb,), in_specs=[cblk, cblk, rblk, rgblk, wblk, cblk],
        out_specs=[cblk, cblk, rblk, rblk, wblk, rblk],
        out_shape=[S((L, D), bf16), S((L, D), bf16), S((1, D), f32), S((1, D), f32), S((CONF_K, D), f32), S((1, D), f32)],
        scratch_shapes=[pltpu.VMEM((L + 2 * CPAD, CB), f32), pltpu.VMEM((8, L + 2 * CPAD, CB), f32),
                        pltpu.VMEM((CONF_K, 8, CB), f32)],
        compiler_params=_cparams("parallel"), name=name)(pa, pg, b1, b1, wdw, dy)


def _ln_silu_f(x, w, b):
    mu = jnp.mean(x, axis=-1, keepdims=True)
    d = x - mu
    y = d * lax.rsqrt(jnp.mean(d * d, axis=-1, keepdims=True) + EPS) * w + b
    return y * _sigmoid(y)


def ln_silu_fwd(x, w, b, name):
    T = x.shape[0]

    def body(x_ref, w_ref, b_ref, o_ref):
        o_ref[...] = _ln_silu_f(x_ref[...], w_ref[...], b_ref[...]).astype(bf16)

    blk = pl.BlockSpec((TB, D), lambda i: (i, 0))
    row = pl.BlockSpec((1, D), lambda i: (0, 0))
    return pl.pallas_call(body, grid=(T // TB,), in_specs=[blk, row, row], out_specs=blk, out_shape=S((T, D), bf16),
                          compiler_params=_cparams("parallel"), name=name)(x, w, b)


def ln_silu_bwd(x, w, b, ds, name):
    T = x.shape[0]

    def body(x_ref, w_ref, b_ref, ds_ref, dx_ref, dw_ref, db_ref):
        i = pl.program_id(0)
        _, vjp = jax.vjp(_ln_silu_f, x_ref[...], w_ref[...], b_ref[...])
        dx, dw, db = vjp(ds_ref[...].astype(f32))
        dx_ref[...] = dx

        @pl.when(i == 0)
        def _():
            dw_ref[...] = jnp.zeros_like(dw_ref)
            db_ref[...] = jnp.zeros_like(db_ref)

        dw_ref[...] += dw
        db_ref[...] += db

    blk = pl.BlockSpec((TB, D), lambda i: (i, 0))
    row = pl.BlockSpec((1, D), lambda i: (0, 0))
    return pl.pallas_call(body, grid=(T // TB,), in_specs=[blk, row, row, blk], out_specs=[blk, row, row],
                          out_shape=[S((T, D), f32), S((1, D), f32), S((1, D), f32)],
                          compiler_params=_cparams("arbitrary"), name=name)(x, w, b, ds)


def _mxu(a, b, dims):
    return lax.dot_general(a.astype(bf16), b.astype(bf16), (dims, ((), ())), preferred_element_type=f32)


def _nn(a, b):
    return _mxu(a, b, ((1,), (0,)))


def _nt(a, b):
    return _mxu(a, b, ((1,), (1,)))


def _tn(a, b):
    return _mxu(a, b, ((0,), (0,)))


@jax.custom_vjp
def _dot_nn(a, b):
    return _nn(a, b)


@jax.custom_vjp
def _dot_nt(a, b):
    return _nt(a, b)


@jax.custom_vjp
def _dot_tn(a, b):
    return _tn(a, b)


_dot_nn.defvjp(lambda a, b: (_nn(a, b), (a, b)), lambda res, g: (_nt(g, res[1]), _tn(res[0], g)))
_dot_nt.defvjp(lambda a, b: (_nt(a, b), (a, b)), lambda res, g: (_nn(g, res[1]), _tn(g, res[0])))
_dot_tn.defvjp(lambda a, b: (_tn(a, b), (a, b)), lambda res, g: (_nt(res[1], g), _nn(res[0], g)))


def _exact_dot(a, b, dims, split_first):
    v = a if split_first else b
    p1 = v.astype(bf16)
    r1 = v - p1.astype(f32)
    p2 = r1.astype(bf16)
    p3 = (r1 - p2.astype(f32)).astype(bf16)
    out = None
    for p in (p1, p2, p3):
        lhs, rhs = (p, b.astype(bf16)) if split_first else (a.astype(bf16), p)
        t = lax.dot_general(lhs, rhs, (dims, ((), ())), preferred_element_type=f32)
        out = t if out is None else out + t
    return out


@jax.custom_vjp
def _masked_sum_cols(mf, a):
    return _exact_dot(mf, a, ((1,), (0,)), False)


@jax.custom_vjp
def _masked_sum_rows(mf, a):
    return _exact_dot(a, mf, ((1,), (1,)), True)


_masked_sum_cols.defvjp(lambda mf, a: (_exact_dot(mf, a, ((1,), (0,)), False), mf),
                        lambda mf, g: (jnp.zeros_like(mf), _exact_dot(mf, g, ((0,), (0,)), False)))
_masked_sum_rows.defvjp(lambda mf, a: (_exact_dot(a, mf, ((1,), (1,)), True), mf),
                        lambda mf, g: (jnp.zeros_like(mf), _exact_dot(g, mf, ((1,), (0,)), True)))


def _masked_sum(mf, a, rows):
    return _masked_sum_rows(mf, a) if rows else _masked_sum_cols(mf, a)


def _lanes_to_rows(v):
    r = lax.broadcasted_iota(jnp.int32, (GW, GW), 0)
    c = lax.broadcasted_iota(jnp.int32, (GW, GW), 1)
    return jnp.sum(jnp.where(r == c, jnp.broadcast_to(v, (GW, GW)), 0.0), axis=1, keepdims=True)


def _ssd_chunk(x, B, C, dtc, dtr, bc, br, alc, alr, s_in, is_fwd):
    row = lax.broadcasted_iota(jnp.int32, (Q, Q), 0)
    col = lax.broadcasted_iota(jnp.int32, (Q, Q), 1)
    sgn = jnp.where(is_fwd, 1, -1).astype(jnp.int32)
    mask = (row - col) * sgn >= 0
    mf = mask.astype(f32)
    lane_head = lax.broadcasted_iota(jnp.int32, (1, GW), 1) // P

    def spread(v):
        out = jnp.zeros((v.shape[0], GW), f32)
        for r in range(HPG):
            out = jnp.where(lane_head == r, v[:, r:r + 1], out)
        return out

    dt_c = _softplus(dtc + bc)
    dt_r = _softplus(dtr + br)
    a_c = dt_c * (-jnp.exp(alc))
    a_r = dt_r * (-jnp.exp(alr))
    acum_c = _masked_sum(mf, a_c, False)
    acum_r = _masked_sum(mf, a_r, True)
    tot_c = jnp.sum(a_c, axis=0, keepdims=True)
    dt_e = spread(dt_c)
    acum_e = spread(acum_c)
    tot_e = spread(tot_c)
    xdt = x * dt_e
    cb = _dot_nt(C, B)
    scores, xs = [], []
    for r in range(HPG):
        seg = acum_c[:, r:r + 1] - acum_r[r:r + 1, :]
        scores.append(cb * jnp.exp(jnp.where(mask, seg, -jnp.inf)))
        xs.append(jnp.where(lane_head == r, xdt, 0.0))
    y = _dot_nn(jnp.concatenate(scores, axis=1), jnp.concatenate(xs, axis=0))
    y = y + _dot_nt(C, s_in) * jnp.exp(acum_e)
    xe = xdt * jnp.exp(tot_e - acum_e)
    s_out = _lanes_to_rows(jnp.exp(tot_e)) * s_in + _dot_tn(xe, B)
    return y, s_out


def _chunk_index(d, t, nctx, nc):
    bwd = jnp.where(t < nctx, nctx - 1 - t, nc - 1 - (t - nctx))
    return jnp.where(d == 0, t, bwd)


def _ssd_in_specs(ci):
    small_c = pl.BlockSpec((None, G, 1, HPG), lambda d, t: (d, 0, 0, 0))
    small_r = pl.BlockSpec((None, G, HPG, 1), lambda d, t: (d, 0, 0, 0))
    return [
        pl.BlockSpec((Q, CONVD), lambda d, t: (ci(d, t), 0)),
        pl.BlockSpec((None, G, Q, HPG), lambda d, t: (d, 0, ci(d, t), 0)),
        pl.BlockSpec((None, G, HPG, Q), lambda d, t: (d, 0, 0, ci(d, t))),
        small_c, small_r, small_c, small_r,
    ]


def _group_cols(g):
    return pl.ds(g * GW, GW), pl.ds(DI + g * N, N), pl.ds(DI + G * N + g * N, N)


def ssd_scan_fwd(xbc, dtc, dtr, bc, br, alc, alr, nctx, name, hosted=None):
    T = xbc.shape[0]
    nc = T // Q

    def body(xbc_ref, dtc_ref, dtr_ref, bc_ref, br_ref, alc_ref, alr_ref, y_ref, sin_ref, st_ref):
        d = pl.program_id(0)
        t = pl.program_id(1)

        @pl.when(t == 0)
        def _():
            st_ref[...] = jnp.zeros_like(st_ref)

        for g in range(G):
            xs, bs, cs = _group_cols(g)
            s_in = st_ref[g]
            sin_ref[g] = s_in
            y, s_out = _ssd_chunk(xbc_ref[:, xs], xbc_ref[:, bs], xbc_ref[:, cs], dtc_ref[g], dtr_ref[g], bc_ref[g], br_ref[g],
                                  alc_ref[g], alr_ref[g], s_in, d == 0)
            y_ref[:, xs] = y
            st_ref[g] = s_out

    ci = lambda d, t: _chunk_index(d, t, nctx, nc)
    out_specs = [
        pl.BlockSpec((None, Q, DI), lambda d, t: (d, ci(d, t), 0)),
        pl.BlockSpec((None, None, G, GW, N), lambda d, t: (d, ci(d, t), 0, 0, 0)),
    ]
    return _host_call(
        body, (2, nc), _ssd_in_specs(ci), out_specs, [S((2, T, DI), f32), S((2, nc, G, GW, N), f32)],
        [pltpu.VMEM((G, GW, N), f32)], ("arbitrary", "arbitrary"), name, (xbc, dtc, dtr, bc, br, alc, alr), hosted)


def ssd_scan_bwd(xbc, dtc, dtr, bc, br, alc, alr, s_in_all, dy, nctx, name, hosted=None):
    T = xbc.shape[0]
    nc = T // Q

    def body(xbc_ref, dtc_ref, dtr_ref, bc_ref, br_ref, alc_ref, alr_ref, sin_ref, dy_ref,
             dxbc_ref, ddtc_ref, ddtr_ref, dbc_ref, dbr_ref, dalc_ref, dalr_ref, ds_ref):
        d = pl.program_id(0)
        t = pl.program_id(1)

        @pl.when(t == 0)
        def _():
            ds_ref[...] = jnp.zeros_like(ds_ref)
            dbc_ref[...] = jnp.zeros_like(dbc_ref)
            dbr_ref[...] = jnp.zeros_like(dbr_ref)
            dalc_ref[...] = jnp.zeros_like(dalc_ref)
            dalr_ref[...] = jnp.zeros_like(dalr_ref)

        f = functools.partial(_ssd_chunk, is_fwd=(d == 0))
        for g in range(G):
            xs, bs, cs = _group_cols(g)
            _, vjp = jax.vjp(f, xbc_ref[:, xs], xbc_ref[:, bs], xbc_ref[:, cs], dtc_ref[g], dtr_ref[g], bc_ref[g], br_ref[g],
                             alc_ref[g], alr_ref[g], sin_ref[g])
            dx, dB, dC, ddtc, ddtr, dbc, dbr, dalc, dalr, ds = vjp((dy_ref[:, xs], ds_ref[g]))
            dxbc_ref[:, xs] = dx
            dxbc_ref[:, bs] = dB
            dxbc_ref[:, cs] = dC
            ddtc_ref[g] = ddtc
            ddtr_ref[g] = ddtr
            dbc_ref[g] += dbc
            dbr_ref[g] += dbr
            dalc_ref[g] += dalc
            dalr_ref[g] += dalr
            ds_ref[g] = ds

    ci = lambda d, t: _chunk_index(d, nc - 1 - t, nctx, nc)
    in_specs = _ssd_in_specs(ci) + [
        pl.BlockSpec((None, None, G, GW, N), lambda d, t: (d, ci(d, t), 0, 0, 0)),
        pl.BlockSpec((Q, DI), lambda d, t: (ci(d, t), 0)),
    ]
    small_c = pl.BlockSpec((None, G, 1, HPG), lambda d, t: (d, 0, 0, 0))
    small_r = pl.BlockSpec((None, G, HPG, 1), lambda d, t: (d, 0, 0, 0))
    out_specs = [
        pl.BlockSpec((None, Q, CONVD), lambda d, t: (d, ci(d, t), 0)),
        pl.BlockSpec((None, G, Q, HPG), lambda d, t: (d, 0, ci(d, t), 0)),
        pl.BlockSpec((None, G, HPG, Q), lambda d, t: (d, 0, 0, ci(d, t))),
        small_c, small_r, small_c, small_r,
    ]
    out_shape = [S((2, T, CONVD), f32), S((2, G, T, HPG), f32), S((2, G, HPG, T), f32),
                 S((2, G, 1, HPG), f32), S((2, G, HPG, 1), f32), S((2, G, 1, HPG), f32), S((2, G, HPG, 1), f32)]
    return _host_call(body, (2, nc), in_specs, out_specs, out_shape, [pltpu.VMEM((G, GW, N), f32)],
                      ("arbitrary", "arbitrary"), name, (xbc, dtc, dtr, bc, br, alc, alr, s_in_all, dy), hosted)


GTB = 128


def _gate_norm_f(yf, yb, x, z, dexp, w):
    y = (yf + yb + dexp * x) * (z * _sigmoid(z))
    return y * lax.rsqrt(jnp.mean(y * y, axis=-1, keepdims=True) + EPS) * w


def ssd_gate_fwd(y2, xbc, proj, dexp, w, nctxb, name):
    T = xbc.shape[0]
    L = T - nctxb * GTB

    def body(yf_ref, yb_ref, x_ref, z_ref, d_ref, w_ref, o_ref):
        o_ref[...] = _gate_norm_f(yf_ref[...], yb_ref[...], x_ref[...], z_ref[...], d_ref[...], w_ref[...]).astype(bf16)

    wide = pl.BlockSpec((GTB, DI), lambda i: (i + nctxb, 0))
    row = pl.BlockSpec((1, DI), lambda i: (0, 0))
    return pl.pallas_call(
        body, grid=(L // GTB,),
        in_specs=[pl.BlockSpec((None, GTB, DI), lambda i: (0, i + nctxb, 0)),
                  pl.BlockSpec((None, GTB, DI), lambda i: (1, i + nctxb, 0)), wide, wide, row, row],
        out_specs=pl.BlockSpec((GTB, DI), lambda i: (i, 0)), out_shape=S((L, DI), bf16),
        compiler_params=_cparams("parallel"), name=name)(y2, y2, xbc, proj, dexp, w)


def ssd_gate_bwd(y2, xbc, proj, dexp, w, dyn, nctxb, name, hosted=None):
    T = xbc.shape[0]
    nb = T // GTB

    def body(yf_ref, yb_ref, x_ref, z_ref, d_ref, w_ref, dyn_ref, dy_ref, dz_ref, dd_ref, dw_ref):
        i = pl.program_id(0)

        @pl.when(i == 0)
        def _():
            dd_ref[...] = jnp.zeros_like(dd_ref)
            dw_ref[...] = jnp.zeros_like(dw_ref)

        @pl.when(i < nctxb)
        def _():
            dy_ref[...] = jnp.zeros_like(dy_ref)
            dz_ref[...] = jnp.zeros_like(dz_ref)

        @pl.when(i >= nctxb)
        def _():
            _, vjp = jax.vjp(_gate_norm_f, yf_ref[...], yb_ref[...], x_ref[...], z_ref[...], d_ref[...], w_ref[...])
            dyf, _, _, dz, dd, dw = vjp(dyn_ref[...].astype(f32))
            dy_ref[...] = dyf
            dz_ref[...] = dz.astype(bf16)
            fold = (lax.broadcasted_iota(jnp.int32, (DI, 128), 0) // P == lax.broadcasted_iota(jnp.int32, (DI, 128), 1))
            dd_ref[...] += jnp.dot(dd, fold.astype(f32), precision=HI, preferred_element_type=f32)
            dw_ref[...] += dw

    wide = pl.BlockSpec((GTB, DI), lambda i: (i, 0))
    row = pl.BlockSpec((1, DI), lambda i: (0, 0))
    hrow = pl.BlockSpec((1, 128), lambda i: (0, 0))
    return _host_call(
        body, (nb,),
        [pl.BlockSpec((None, GTB, DI), lambda i: (0, i, 0)), pl.BlockSpec((None, GTB, DI), lambda i: (1, i, 0)),
         wide, wide, row, row, pl.BlockSpec((GTB, DI), lambda i: (jnp.maximum(i - nctxb, 0), 0))],
        [wide, wide, hrow, row],
        [S((T, DI), f32), S((T, DI), bf16), S((1, 128), f32), S((1, DI), f32)],
        [], ("arbitrary",), name, (y2, y2, xbc, proj, dexp, w, dyn), hosted)


CROWS = 2 * N_DEV


def mod_fwd(c16, modw, name):
    nl, _, cols = modw.shape

    def body(c_ref, w_ref, o_ref):
        cv = c_ref[...]
        s = cv * _sigmoid(cv)
        for l in range(nl):
            o_ref[l] = jnp.dot(s, w_ref[l], precision=HI, preferred_element_type=f32)

    return pl.pallas_call(body, in_specs=[VMEM, VMEM], out_specs=VMEM, out_shape=S((nl, CROWS, cols), f32),
                          compiler_params=pltpu.CompilerParams(vmem_limit_bytes=VMEM_LIMIT_BYTES), name=name)(c16, modw)


def mod_bwd(c16, modw, dm_sh, dm_all, name):
    nl, _, cols = modw.shape

    def body(c_ref, w_ref, dm_ref, dmall_ref, dw_ref, dc_ref, db_ref):
        cv = c_ref[...]
        sg = _sigmoid(cv)
        s = cv * sg
        ds_dc = sg * (1.0 + cv * (1.0 - sg))
        is_ctx = lax.broadcasted_iota(jnp.int32, (CROWS, D), 0) >= N_DEV
        dc = jnp.zeros((1, D), f32)
        for l in range(nl):
            dm = dm_ref[l]
            dw_ref[l] = lax.dot_general(s, dm, (((0,), (0,)), ((), ())), precision=HI, preferred_element_type=f32)
            dsv = lax.dot_general(dm, w_ref[l], (((1,), (1,)), ((), ())), precision=HI, preferred_element_type=f32)
            dc = dc + jnp.sum(jnp.where(is_ctx, dsv * ds_dc, 0.0), axis=0, keepdims=True)
            db_ref[pl.ds(l, 1), :] = jnp.sum(dmall_ref[l], axis=0, keepdims=True)
        dc_ref[...] = dc

    return pl.pallas_call(
        body, in_specs=[VMEM, VMEM, VMEM, VMEM], out_specs=[VMEM, VMEM, VMEM],
        out_shape=[S(modw.shape, f32), S((1, D), f32), S((nl, 6 * D), f32)],
        compiler_params=pltpu.CompilerParams(vmem_limit_bytes=VMEM_LIMIT_BYTES), name=name)(c16, modw, dm_sh, dm_all)


def adamw(w, g, m, v, name):
    R, C = w.shape
    rb = R if R <= 512 else max(r_ for r_ in range(8, 513, 8) if R % r_ == 0)
    bc1 = 1.0 - ADAM_B1 ** ADAM_STEP
    bc2 = 1.0 - ADAM_B2 ** ADAM_STEP

    def body(w_ref, g_ref, m_ref, v_ref, d_ref, nm_ref, nv_ref):
        gv = g_ref[...]
        m_new = ADAM_B1 * m_ref[...] + (1.0 - ADAM_B1) * gv
        v_new = ADAM_B2 * v_ref[...] + (1.0 - ADAM_B2) * (gv * gv)
        m_hat = m_new / bc1
        v_hat = v_new / bc2
        d_ref[...] = -ADAM_LR * (m_hat / (jnp.sqrt(v_hat) + ADAM_EPS) + ADAM_WD * w_ref[...])
        nm_ref[...] = m_new
        nv_ref[...] = v_new

    blk = pl.BlockSpec((rb, C), lambda i: (i, 0))
    return pl.pallas_call(body, grid=(R // rb,), in_specs=[blk] * 4, out_specs=[blk] * 3,
                          out_shape=[S((R, C), f32)] * 3, compiler_params=_cparams("parallel"), name=name)(w, g, m, v)


def _me():
    return lax.axis_index("x"), lax.axis_index("y"), lax.axis_index("c")


def allgather_small(x, name, with_sum=False):
    r, w = x.shape

    def body(x_ref, *refs):
        if with_sum:
            out_ref, sum_ref, send_sems, recv_sems = refs
        else:
            out_ref, send_sems, recv_sems = refs
        mx, my, mc = _me()
        me = 4 * mx + 2 * my + mc
        out_ref[me] = x_ref[...]
        peers = []
        for k in range(1, N_DEV):
            kx, ky, kc = (k >> 2) & 1, (k >> 1) & 1, k & 1
            peers.append((mx + kx - 2 * mx * kx, my + ky - 2 * my * ky, mc + kc - 2 * mc * kc))
        copies = []
        for k, peer in enumerate(peers):
            cp = pltpu.make_async_remote_copy(src_ref=x_ref, dst_ref=out_ref.at[me], send_sem=send_sems.at[k],
                                              recv_sem=recv_sems.at[k], device_id=peer, device_id_type=MESH)
            cp.start()
            copies.append(cp)
        for k, (px, py, pc) in enumerate(peers):
            pltpu.make_async_remote_copy(src_ref=x_ref, dst_ref=out_ref.at[4 * px + 2 * py + pc], send_sem=send_sems.at[k],
                                         recv_sem=recv_sems.at[k], device_id=(px, py, pc), device_id_type=MESH).wait_recv()
        for cp in copies:
            cp.wait_send()
        if with_sum:
            acc = out_ref[0]
            for j in range(1, N_DEV):
                acc = acc + out_ref[j]
            sum_ref[...] = acc

    out_shape = [S((N_DEV, r, w), f32)] + ([S((r, w), f32)] if with_sum else [])
    outs = pl.pallas_call(
        body, in_specs=[VMEM], out_specs=[VMEM] * len(out_shape), out_shape=out_shape,
        scratch_shapes=[pltpu.SemaphoreType.DMA((N_DEV - 1,)), pltpu.SemaphoreType.DMA((N_DEV - 1,))],
        compiler_params=pltpu.CompilerParams(vmem_limit_bytes=VMEM_LIMIT_BYTES), name=name)(x)
    return outs if with_sum else outs[0]


def _tile2d(R, W, max_rows):
    if R <= max_rows:
        return R, W
    fits = [r_ for r_ in range(16, max_rows + 1, 16) if R % r_ == 0]
    return (max(fits), W) if fits else (R, 256)


def add_own(g, r, core, name, twice=False):
    _, _, R, W = g.shape
    rb, wb = _tile2d(R, W, 512)
    nout = 2 if twice else 1

    def body(core_ref, a_ref, b_ref, *o_refs):
        s = (a_ref[...].astype(f32) + b_ref[...].astype(f32)).astype(bf16)
        for o_ref in o_refs:
            o_ref[...] = s

    blk = pl.BlockSpec((None, rb, wb), lambda k, i, j, core_ref: (k, i, j))
    gs = pltpu.PrefetchScalarGridSpec(
        num_scalar_prefetch=1, grid=(4, R // rb, W // wb),
        in_specs=[pl.BlockSpec((None, None, rb, wb), lambda k, i, j, core_ref: (k, core_ref[0], i, j)), blk],
        out_specs=[blk] * nout)
    outs = pl.pallas_call(body, grid_spec=gs, out_shape=[S((4, R, W), bf16)] * nout,
                          compiler_params=_cparams("parallel", "parallel", "parallel"), name=name)(core, g, r)
    return tuple(outs) if twice else outs[0]


HBM_SPEC = pl.BlockSpec(memory_space=pltpu.HBM)
SEM_SPEC = pl.BlockSpec(memory_space=pltpu.SEMAPHORE)


def _chips_copy(p_ref, land_ref, send_sems, recv_sems, a, j):
    x, y, c = _me()
    px, py = [(1 - x, y), (x, 1 - y), (1 - x, 1 - y)][j]
    return pltpu.make_async_remote_copy(src_ref=p_ref.at[2 * px + py], dst_ref=land_ref.at[2 * x + y],
                                        send_sem=send_sems.at[3 * a + j], recv_sem=recv_sems.at[3 * a + j],
                                        device_id=(px, py, c), device_id_type=MESH)


def _chips_wait_copy(p_ref, land_ref, send_sems, recv_sems, a, j):
    x, y, c = _me()
    px, py = [(1 - x, y), (x, 1 - y), (1 - x, 1 - y)][j]
    return pltpu.make_async_remote_copy(src_ref=p_ref.at[2 * px + py], dst_ref=land_ref.at[2 * px + py],
                                        send_sem=send_sems.at[3 * a + j], recv_sem=recv_sems.at[3 * a + j],
                                        device_id=(px, py, c), device_id_type=MESH)


def chips_start(parts, lands, name):
    na = len(parts)

    def body(*refs):
        p_refs, land_refs = refs[:na], refs[na:2 * na]
        send_sems, recv_sems = refs[2 * na], refs[2 * na + 1]
        token = refs[-1]
        for a in range(na):
            for j in range(3):
                _chips_copy(p_refs[a], land_refs[a], send_sems, recv_sems, a, j).start()
        token[...] = jnp.zeros_like(token)

    arrs = list(parts) + list(lands)
    outs = pl.pallas_call(
        body, name=name, in_specs=[HBM_SPEC] * (2 * na),
        out_shape=[DMA((3 * na,)), DMA((3 * na,))] + [pltpu.HBM(t.shape, t.dtype) for t in arrs] + [S((8, 128), f32)],
        out_specs=[SEM_SPEC, SEM_SPEC] + [HBM_SPEC] * (2 * na) + [VMEM],
        input_output_aliases={k: 2 + k for k in range(2 * na)},
        compiler_params=pltpu.CompilerParams(has_side_effects=pltpu.SideEffectType.DATAFLOW_SIDE_EFFECTING),
    )(*[pltpu.with_memory_space_constraint(t, pltpu.HBM) for t in arrs])
    return outs[0], outs[1], list(outs[2:2 + na]), list(outs[2 + na:2 + 2 * na]), outs[-1]


def chips_wait(send_sems, recv_sems, parts, lands, after, name):
    na = len(parts)

    def body(*refs):
        p_refs, land_refs = refs[:na], refs[na:2 * na]
        ssem, rsem = refs[2 * na], refs[2 * na + 1]
        for a in range(na):
            for j in range(3):
                cp = _chips_wait_copy(p_refs[a], land_refs[a], ssem, rsem, a, j)
                cp.wait_send()
                cp.wait_recv()

    arrs = list(parts) + list(lands)
    outs = pl.pallas_call(
        body, name=name, in_specs=[HBM_SPEC] * (2 * na) + [SEM_SPEC, SEM_SPEC, ANY],
        out_shape=[pltpu.HBM(t.shape, t.dtype) for t in arrs], out_specs=[HBM_SPEC] * (2 * na),
        input_output_aliases={k: k for k in range(2 * na)},
        compiler_params=pltpu.CompilerParams(has_side_effects=pltpu.SideEffectType.DATAFLOW_SIDE_EFFECTING),
    )(*arrs, send_sems, recv_sems, after)
    return list(outs[na:])


def sum_adamw(recv, w, m, v, layer, name, into=None, after=None):
    _, R, W = recv.shape
    rb, wb = _tile2d(R, W, 256)
    bc1 = 1.0 - ADAM_B1 ** ADAM_STEP
    bc2 = 1.0 - ADAM_B2 ** ADAM_STEP
    n_into = 0 if into is None else 4
    extra = () if after is None else (after,)

    def body(r_ref, w_ref, m_ref, v_ref, *refs):
        g_ref, d_ref, nm_ref, nv_ref = refs[n_into + len(extra):]
        gv = r_ref[0].astype(f32)
        for k in range(1, 4):
            gv = gv + r_ref[k].astype(f32)
        m_new = ADAM_B1 * m_ref[...] + (1.0 - ADAM_B1) * gv
        v_new = ADAM_B2 * v_ref[...] + (1.0 - ADAM_B2) * (gv * gv)
        g_ref[...] = gv
        d_ref[...] = -ADAM_LR * ((m_new / bc1) / (jnp.sqrt(v_new / bc2) + ADAM_EPS) + ADAM_WD * w_ref[...])
        nm_ref[...] = m_new
        nv_ref[...] = v_new

    if layer is None:
        wblk = pl.BlockSpec((rb, wb), lambda i, j: (i, j))
        oshape = S((R, W), f32)
    else:
        wblk = pl.BlockSpec((None, rb, wb), lambda i, j: (layer, i, j))
        oshape = S(w.shape, f32)
    return pl.pallas_call(
        body, grid=(R // rb, W // wb),
        in_specs=[pl.BlockSpec((4, rb, wb), lambda i, j: (0, i, j)), wblk, wblk, wblk] + [ANY] * (n_into + len(extra)),
        out_specs=[wblk] * 4, out_shape=[oshape] * 4, input_output_aliases={4 + k: k for k in range(n_into)},
        compiler_params=_cparams("parallel", "parallel"), name=name)(recv, w, m, v, *(into or ()), *extra)


def sum_rows(a, name):
    K, R, W = a.shape
    rb = _pick(R, (512, 256, 128, 64, 32, 16))

    def body(a_ref, o_ref):
        acc = a_ref[0].astype(f32)
        for k in range(1, K):
            acc = acc + a_ref[k].astype(f32)
        o_ref[...] = acc

    return pl.pallas_call(body, grid=(R // rb,), in_specs=[pl.BlockSpec((K, rb, W), lambda i: (0, i, 0))],
                          out_specs=pl.BlockSpec((rb, W), lambda i: (i, 0)), out_shape=S((R, W), f32),
                          compiler_params=_cparams("parallel"), name=name)(a)


DMA = pltpu.SemaphoreType.DMA


class GatherExchange:
    def __init__(self, arrays):
        self.arrays = list(arrays)
        self.na = len(self.arrays)
        self.out_shape = [S((N_DEV,) + a.shape, a.dtype) for a in self.arrays]
        self.scratch = [DMA((7 * self.na,)), DMA((7 * self.na,)), DMA((self.na,))]

    def ops(self, x_refs, out_refs, sems):
        send_sems, recv_sems, local_sems = sems
        na = self.na
        x, y, c = _me()
        me, sibling = (x, y, c), (x, y, 1 - c)
        chips = [(1 - x, y), (x, 1 - y), (1 - x, 1 - y)]

        def rows(a, px, py, pc):
            return out_refs[a].at[4 * px + 2 * py + pc]

        def copy(a, k, block, to, src=None):
            return pltpu.make_async_remote_copy(
                src_ref=rows(a, *block) if src is None else src, dst_ref=rows(a, *block),
                send_sem=send_sems.at[7 * a + k], recv_sem=recv_sems.at[7 * a + k], device_id=to, device_id_type=MESH)

        def local(a):
            return pltpu.make_async_copy(x_refs[a], rows(a, *me), local_sems.at[a])

        def first(a):
            return [copy(a, 0, me, sibling, src=x_refs[a])] + [copy(a, 1 + j, me, (*chip, c), src=x_refs[a])
                                                                for j, chip in enumerate(chips)]

        def start():
            for a in range(na):
                local(a).start()
                for cp in first(a):
                    cp.start()

        def mid():
            for a in range(na):
                for j, chip in enumerate(chips):
                    copy(a, 1 + j, (*chip, c), me).wait_recv()
                    copy(a, 4 + j, (*chip, c), sibling).start()

        def finish():
            for a in range(na):
                copy(a, 0, sibling, me).wait_recv()
                for j, chip in enumerate(chips):
                    copy(a, 4 + j, (*chip, 1 - c), me).wait_recv()
                for cp in first(a) + [copy(a, 4 + j, (*chip, c), sibling) for j, chip in enumerate(chips)]:
                    cp.wait_send()
                local(a).wait()

        return start, mid, finish


class SiblingExchange:
    def __init__(self, arrays):
        self.arrays = list(arrays)
        self.na = len(self.arrays)
        self.out_shape = [S((4,) + g.shape[2:], g.dtype) for g in self.arrays]
        self.scratch = [DMA((self.na,)), DMA((self.na,))]

    def ops(self, g_refs, out_refs, sems):
        send_sems, recv_sems = sems
        x, y, c = _me()

        def copy(a):
            return pltpu.make_async_remote_copy(src_ref=g_refs[a].at[:, 1 - c], dst_ref=out_refs[a],
                                                send_sem=send_sems.at[a], recv_sem=recv_sems.at[a],
                                                device_id=(x, y, 1 - c), device_id_type=MESH)

        def start():
            for a in range(self.na):
                copy(a).start()

        def finish():
            for a in range(self.na):
                copy(a).wait()

        return start, None, finish


class ChipsExchange:
    def __init__(self, arrays):
        self.arrays = list(arrays)
        self.na = len(self.arrays)
        self.out_shape = [S(p.shape, p.dtype) for p in self.arrays]
        self.scratch = [DMA((3 * self.na,)), DMA((3 * self.na,)), DMA((self.na,))]

    def ops(self, p_refs, out_refs, sems):
        send_sems, recv_sems, local_sems = sems
        x, y, c = _me()
        mine = 2 * x + y
        chips = [(1 - x, y), (x, 1 - y), (1 - x, 1 - y)]

        def local(a):
            return pltpu.make_async_copy(p_refs[a].at[mine], out_refs[a].at[mine], local_sems.at[a])

        def send(a, j):
            px, py = chips[j]
            return pltpu.make_async_remote_copy(src_ref=p_refs[a].at[2 * px + py], dst_ref=out_refs[a].at[mine],
                                                send_sem=send_sems.at[3 * a + j], recv_sem=recv_sems.at[3 * a + j],
                                                device_id=(px, py, c), device_id_type=MESH)

        def recv(a, j):
            px, py = chips[j]
            return pltpu.make_async_remote_copy(src_ref=p_refs[a].at[mine], dst_ref=out_refs[a].at[2 * px + py],
                                                send_sem=send_sems.at[3 * a + j], recv_sem=recv_sems.at[3 * a + j],
                                                device_id=(px, py, c), device_id_type=MESH)

        def start():
            for a in range(self.na):
                local(a).start()
                for j in range(3):
                    send(a, j).start()

        def finish():
            for a in range(self.na):
                for j in range(3):
                    recv(a, j).wait_recv()
                for j in range(3):
                    send(a, j).wait_send()
                local(a).wait()

        return start, None, finish


def exchange(ex, name):
    na = ex.na

    def body(*refs):
        start, mid, finish = ex.ops(refs[:na], refs[na:2 * na], refs[2 * na:])
        start()
        if mid is not None:
            mid()
        finish()

    return pl.pallas_call(body, in_specs=[ANY] * na, out_specs=[ANY] * na, out_shape=ex.out_shape,
                          scratch_shapes=ex.scratch, name=name)(*ex.arrays)


def _host_call(body, grid, in_specs, out_specs, out_shape, scratch_shapes, sem, name, args, hosted):
    if hosted is None:
        res = pl.pallas_call(body, grid=grid, in_specs=in_specs, out_specs=out_specs, out_shape=out_shape,
                             scratch_shapes=scratch_shapes, compiler_params=_cparams(*sem), name=name)(*args)
        return res, None
    n_in, n_out, n_sc, na = len(in_specs), len(out_shape), len(scratch_shapes), hosted.na
    nsteps = 1
    for g_ in grid:
        nsteps *= g_
    mid_step = (3 * nsteps) // 4
    i1 = n_in + na
    i2 = i1 + n_out
    i3 = i2 + na
    i4 = i3 + n_sc

    def wrapped(*refs):
        step = pl.program_id(0)
        for ax in range(1, len(grid)):
            step = step * grid[ax] + pl.program_id(ax)
        start, mid, finish = hosted.ops(refs[n_in:i1], refs[i2:i3], refs[i4:])
        pl.when(step == 0)(start)
        if mid is not None:
            pl.when(step == mid_step)(mid)
        body(*refs[:n_in], *refs[i1:i2], *refs[i3:i4])
        pl.when(step == nsteps - 1)(finish)

    res = pl.pallas_call(
        wrapped, grid=grid, in_specs=list(in_specs) + [ANY] * na, out_specs=list(out_specs) + [ANY] * na,
        out_shape=list(out_shape) + hosted.out_shape, scratch_shapes=list(scratch_shapes) + hosted.scratch,
        compiler_params=_cparams(*(("arbitrary",) * len(grid))), name=name)(*args, *hosted.arrays)
    return res[:n_out], res[n_out:]


PACK_ALIGN = 16 * PACK_W


def _pad_to(v, mult):
    n = v.shape[-1]
    extra = (-n) % mult
    if extra == 0:
        return v
    return jnp.concatenate([v, jnp.zeros(v.shape[:-1] + (extra,), v.dtype)], axis=-1)


def _f32_as_bf16_pairs(v):
    return lax.bitcast_convert_type(v.reshape(-1), bf16).reshape(-1)


def _bf16_pairs_as_f32(v):
    return lax.bitcast_convert_type(v.reshape(v.shape[:-1] + (v.shape[-1] // 2, 2)), f32)


def _col_shards(gw):
    lead = gw.shape[:-1]
    n = gw.shape[-1] // N_DEV
    t = gw.reshape(lead + (N_DEV, n))
    t = jnp.moveaxis(t, -2, 0)
    return t.reshape(N_DEV, -1)


def kernel(x, c, ctx, c_ctx, mod_w, mod_b, norm1_w, norm2_w, ssd_w_in, ssd_conv_w, ssd_conv_b, ssd_dt_bias, ssd_a_log, ssd_d, ssd_norm_w, ssd_w_out, conf_w_pw1, conf_b_pw1, conf_w_dw, conf_b_dw, conf_ln_w, conf_ln_b, conf_w_pw2, conf_b_pw2, ffn_w_up, ffn_conv_w, ffn_conv_b, ffn_w_down, final_norm_w, loss_target, m_c_ctx, m_mod_w, m_mod_b, m_norm1_w, m_norm2_w, m_ssd_w_in, m_ssd_conv_w, m_ssd_conv_b, m_ssd_dt_bias, m_ssd_a_log, m_ssd_d, m_ssd_norm_w, m_ssd_w_out, m_conf_w_pw1, m_conf_b_pw1, m_conf_w_dw, m_conf_b_dw, m_conf_ln_w, m_conf_ln_b, m_conf_w_pw2, m_conf_b_pw2, m_ffn_w_up, m_ffn_conv_w, m_ffn_conv_b, m_ffn_w_down, m_final_norm_w, v_c_ctx, v_mod_w, v_mod_b, v_norm1_w, v_norm2_w, v_ssd_w_in, v_ssd_conv_w, v_ssd_conv_b, v_ssd_dt_bias, v_ssd_a_log, v_ssd_d, v_ssd_norm_w, v_ssd_w_out, v_conf_w_pw1, v_conf_b_pw1, v_conf_w_dw, v_conf_b_dw, v_conf_ln_w, v_conf_ln_b, v_conf_w_pw2, v_conf_b_pw2, v_ffn_w_up, v_ffn_conv_w, v_ffn_conv_b, v_ffn_w_down, v_final_norm_w):
    mx, my, mc = _me()
    me = 4 * mx + 2 * my + mc
    L = x.shape[1]
    LC = ctx.shape[1]
    T = LC + L
    w_in_cols = ssd_w_in.shape[2] * N_DEV
    n_dt = w_in_cols - DI - CONVD

    small = [c[0], ssd_conv_w[0], conf_b_pw1[0], conf_w_dw[0], conf_b_dw[0], conf_ln_w[0], conf_ln_b[0], conf_b_pw2[0],
             ffn_conv_w]
    parts = [_f32_as_bf16_pairs(t) for t in small]
    sizes = [p.shape[0] for p in parts]
    small_flat = _pad_to(jnp.concatenate(parts), PACK_ALIGN).reshape(-1, PACK_W)
    w_in, small_g = exchange(GatherExchange([ssd_w_in[0].astype(bf16), small_flat]), "gather_first")
    gather_in_proj = GatherExchange([ssd_w_out[0].astype(bf16), conf_w_pw2[0].astype(bf16)])
    gather_in_conv = GatherExchange([ffn_w_down[0].astype(bf16), conf_w_pw1[0].astype(bf16)])
    gather_in_scan = GatherExchange([ffn_w_up[0].astype(bf16), ffn_w_up[1].astype(bf16)])
    gather_in_gate = GatherExchange([ffn_w_down[1].astype(bf16)])
    w_up, w_down = [None, None], [None, None]
    small_g = small_g.reshape(N_DEV, -1)
    offs = [0]
    for s_ in sizes:
        offs.append(offs[-1] + s_)
    sm = [_bf16_pairs_as_f32(small_g[:, offs[i]:offs[i + 1]]) for i in range(len(sizes))]

    def cols(pc, K):
        return jnp.moveaxis(pc.reshape(N_DEV, K, -1), 0, 1).reshape(K, -1)

    c_all = sm[0]
    conv_w5 = cols(sm[1], 5)
    b_pw1 = sm[2].reshape(1, 2 * D)
    w_dw = cols(sm[3], CONF_K)
    b_dw, ln_w, ln_b, b_pw2 = (sm[i].reshape(1, D) for i in (4, 5, 6, 7))
    fcw = sm[8].reshape(N_DEV, 2, 9, FH // N_DEV)
    ffn_cw = [cols(fcw[:, i].reshape(N_DEV, -1), 9) for i in range(2)]
    in_segs = (DI, CONVD, n_dt)
    up_segs = (FH, FH)
    pw1_segs = (D, D)

    c16 = jnp.concatenate([c_all, jnp.broadcast_to(c_ctx[None, :], (N_DEV, D))], axis=0)
    m_sh = mod_fwd(c16, mod_w, "mod_fwd")
    mod_cols = mod_w.shape[2]
    m_all = allgather_small(m_sh.reshape(2 * CROWS, mod_cols), "gather_mod")
    m_all = jnp.moveaxis(m_all.reshape(N_DEV, 2, CROWS, mod_cols), 0, 2).reshape(2, CROWS, 6 * D) + mod_b[:, None, :]
    m_lat = lax.dynamic_index_in_dim(m_all, me, axis=1, keepdims=False).reshape(2, 6, 1, D)
    m_ctx = m_all[:, N_DEV].reshape(2, 6, 1, D)
    zero_row = jnp.zeros((1, D), f32)

    def ffn_fwd(h, i, tag, hosted=None):
        a2 = modnorm_fwd(h, norm2_w[i][None], m_lat[i, 4][None], m_lat[i, 3][None], 0, f"ffn{tag}_norm")
        val, gate = smm_fwd(a2, w_up[i], None, up_segs, f"ffn{tag}_up")
        act, gs_, gvds, extra = ffn_gate_fwd(val, gate, ffn_cw[i], ffn_conv_b[i][None], f"ffn{tag}_gate", hosted)
        o2 = matmul(act, w_down[i], "nn", f32, f"ffn{tag}_down")
        h_new = resgate_fwd(h, o2, m_lat[i, 5], zero_row, f"ffn{tag}_res")
        return h_new, (a2, gate, gs_, gvds, act, o2), extra

    def ffn_bwd(dh, h, i, saved, tag):
        a2, gate, gs_, gvds, act, o2 = saved
        do2, dg2, _ = resgate_bwd(dh, o2, m_lat[i, 5], zero_row, f"ffn{tag}_res_bwd")
        g_down = matmul(act, do2, "tn", bf16, f"ffn{tag}_down_dw")
        dact = matmul(do2, w_down[i], "nt", bf16, f"ffn{tag}_down_dx")
        dval, dgate, dcw, dcb = ffn_gate_bwd(gate, gs_, gvds, ffn_cw[i], dact, f"ffn{tag}_gate_bwd")
        g_up = smm_dw(a2, [dval, dgate], FH // 4, up_segs, 2, True, f"ffn{tag}_up_dw")
        da2, _ = smm_dx([dval, dgate], w_up[i], None, up_segs, bf16, f"ffn{tag}_up_dx")
        dh_in, dn2, dsc2, dsh2 = modnorm_bwd(h, norm2_w[i][None], m_lat[i, 4][None], m_lat[i, 3][None], da2, dh, 0,
                                             f"ffn{tag}_norm_bwd")
        return dh_in, dict(w_up=g_up, w_down=g_down, conv_w=dcw, conv_b=dcb, norm2=dn2, sh2=dsh2[0], sc2=dsc2[0], g2=dg2)

    nctx = LC // Q
    hx = x[0]
    sc0 = jnp.stack([m_ctx[0, 1], m_lat[0, 1]])
    sh0 = jnp.stack([m_ctx[0, 0], m_lat[0, 0]])
    a0 = modnorm_fwd(hx, norm1_w[0][None], sc0, sh0, LC // TB, "ssd_norm", ctx=ctx[0])
    (z, xbc_pre, dt_raw), (w_out_g, w_pw2_g) = smm_fwd(a0, w_in, None, in_segs, "ssd_in", gather_in_proj)
    w_out = w_out_g.reshape(DI, D)
    w_pw2 = w_pw2_g.reshape(D, D)
    segs = ((0, LC), (LC, L))
    xbc, xbc_dsilu, (w_down0_g, w_pw1) = ssd_conv_fwd(xbc_pre, conv_w5, ssd_conv_b, segs, "ssd_conv", gather_in_conv)
    w_down[0] = w_down0_g.reshape(FH, D)
    dt4 = dt_raw[:, :n_dt].reshape(T, 2, G, HPG)
    dtc = jnp.transpose(dt4, (1, 2, 0, 3))
    dtr = jnp.transpose(dt4, (1, 2, 3, 0))
    bias3 = ssd_dt_bias[0].reshape(2, G, HPG)
    alog3 = ssd_a_log[0].reshape(2, G, HPG)
    bc_, br_ = bias3[:, :, None, :], bias3[:, :, :, None]
    alc, alr = alog3[:, :, None, :], alog3[:, :, :, None]
    (y2, s_in_all), (w_up[0], w_up[1]) = ssd_scan_fwd(xbc, dtc, dtr, bc_, br_, alc, alr, nctx, "ssd_scan", gather_in_scan)
    dexp = jnp.repeat(ssd_d[0], P)[None, :]
    yn = ssd_gate_fwd(y2, xbc, z, dexp, ssd_norm_w, LC // GTB, "ssd_gate")
    o_ssd = matmul(yn, w_out, "nn", f32, "ssd_out")
    h1 = resgate_fwd(hx, o_ssd, m_lat[0, 2], zero_row, "ssd_res")
    h2, ffn0_saved, (w_down1_g,) = ffn_fwd(h1, 0, "0", gather_in_gate)
    w_down[1] = w_down1_g.reshape(FH, D)

    a1 = modnorm_fwd(h2, norm1_w[1][None], m_lat[1, 1][None], m_lat[1, 0][None], 0, "conf_norm")
    pa, pg = smm_fwd(a1, w_pw1, None, pw1_segs, "conf_pw1")
    dwc, _ = conf_glu_conv_fwd(pa, pg, b_pw1, w_dw, b_dw, "conf_conv")
    s1 = ln_silu_fwd(dwc, ln_w, ln_b, "conf_ln")
    o_conf = matmul(s1, w_pw2, "nn", f32, "conf_pw2")
    h3 = resgate_fwd(h2, o_conf, m_lat[1, 2], b_pw2, "conf_res")
    h4, ffn1_saved, _ = ffn_fwd(h3, 1, "1")

    loss_part, dh4, g_final = final_loss(h4, final_norm_w[None], loss_target[0], "loss_head")
    dh3, gf1 = ffn_bwd(dh4, h3, 1, ffn1_saved, "1")

    do_conf, dg1_1, g_b_pw2 = resgate_bwd(dh3, o_conf, m_lat[1, 2], b_pw2, "conf_res_bwd")
    g_pw2 = matmul(s1, do_conf, "tn", bf16, "conf_pw2_dw")
    ds1 = matmul(do_conf, w_pw2, "nt", bf16, "conf_pw2_dx")
    ddwc, g_ln_w, g_ln_b = ln_silu_bwd(dwc, ln_w, ln_b, ds1, "conf_ln_bwd")
    dpa, dpg, dba, dbg, g_w_dw, g_b_dw = conf_glu_conv_bwd(pa, pg, b_pw1, w_dw, ddwc, "conf_conv_bwd")
    g_b_pw1 = jnp.concatenate([dba, dbg], axis=1)
    g_pw1 = smm_dw(a1, [dpa, dpg], 2 * D // N_DEV, pw1_segs, 1, False, "conf_pw1_dw")
    da1, _ = smm_dx([dpa, dpg], w_pw1, None, pw1_segs, bf16, "conf_pw1_dx")
    dh2, g_n1_1, dsc1_1, dsh1_1 = modnorm_bwd(h2, norm1_w[1][None], m_lat[1, 1][None], m_lat[1, 0][None], da1, dh3, 0,
                                              "conf_norm_bwd")
    dh1, gf0 = ffn_bwd(dh2, h1, 0, ffn0_saved, "0")

    do_ssd, dg1_0, _ = resgate_bwd(dh1, o_ssd, m_lat[0, 2], zero_row, "ssd_res_bwd")
    g_w_out = matmul(yn, do_ssd, "tn", bf16, "ssd_out_dw")
    dyn = matmul(do_ssd, w_out, "nt", bf16, "ssd_out_dx")
    core = mc.reshape(1).astype(jnp.int32)

    def by_device(t):
        return t.reshape((4, 2, -1, t.shape[-1]))

    early = [by_device(t) for t in (gf1["w_up"], gf1["w_down"], g_pw2, g_pw1, gf0["w_up"], gf0["w_down"], g_w_out)]
    (dy, dz, g_dexp, g_ssd_norm), early_sib = ssd_gate_bwd(
        y2, xbc, z, dexp, ssd_norm_w, dyn, LC // GTB, "ssd_gate_bwd", SiblingExchange(early))
    early_part = [add_own(t, r_, core, f"reduce_add{i}") for i, (t, r_) in enumerate(zip(early, early_sib))]
    (dxbc2, ddtc, ddtr, dbc, dbr, dalc, dalr), early_red = ssd_scan_bwd(
        xbc, dtc, dtr, bc_, br_, alc, alr, s_in_all, dy, nctx, "ssd_scan_bwd", ChipsExchange(early_part))
    ddt = (jnp.transpose(ddtc, (2, 0, 1, 3)) + jnp.transpose(ddtr, (3, 0, 1, 2))).reshape(T, n_dt)
    g_dt_bias = (dbc[:, :, 0, :] + dbr[:, :, :, 0]).reshape(2, NH_SSD)
    g_a_log = (dalc[:, :, 0, :] + dalr[:, :, :, 0]).reshape(2, NH_SSD)
    g_ssd_d = g_dexp[0, :NH_SSD]
    du, g_conv_w5, g_conv_b5 = ssd_conv_bwd(xbc_pre, conv_w5, xbc_dsilu, dxbc2, dy, dexp, segs, "ssd_conv_bwd")
    ddt_p = _pad_to(ddt, 128).astype(bf16)
    g_w_in = smm_dw(a0, [dz, du, ddt_p], w_in.shape[-1], in_segs, 2, True, "ssd_in_dw")
    g_ffn_cw = jnp.stack([gf0["conv_w"], gf1["conv_w"]])
    small_shards = [_col_shards(t) for t in (g_conv_w5, g_b_pw1, g_w_dw, g_b_dw, g_ln_w, g_ln_b, g_b_pw2, g_ffn_cw)]
    gsizes = [s_.shape[1] for s_ in small_shards]
    g_small = _pad_to(jnp.concatenate(small_shards, axis=1), PACK_ALIGN).astype(bf16)
    late = [by_device(g_w_in), by_device(g_small.reshape(N_DEV, -1, PACK_W))]
    da0, late_sib = smm_dx([dz, du, ddt_p], w_in, None, in_segs, f32, "ssd_in_dx", SiblingExchange(late))
    late_part = [add_own(t, r_, core, f"reduce_add_late{i}", twice=True) for i, (t, r_) in enumerate(zip(late, late_sib))]
    late_flying = chips_start([p_[0] for p_ in late_part], [p_[1] for p_ in late_part], "reduce_chips_late_start")
    dh0, g_n1_0, dsc1_0, dsh1_0 = modnorm_bwd(hx, norm1_w[0][None], sc0, sh0, da0, dh1, LC // TB, "ssd_norm_bwd",
                                              ctx=ctx[0])
    grad_x = dh0[None]

    zeros_d = jnp.zeros((1, D), f32)
    dm_lat = jnp.stack([
        jnp.concatenate([dsh1_0[1], dsc1_0[1], dg1_0, gf0["sh2"], gf0["sc2"], gf0["g2"]], axis=1),
        jnp.concatenate([dsh1_1[0], dsc1_1[0], dg1_1, gf1["sh2"], gf1["sc2"], gf1["g2"]], axis=1)])
    dm_ctx = jnp.stack([
        jnp.concatenate([dsh1_0[0], dsc1_0[0]] + [zeros_d] * 4, axis=1), jnp.zeros((1, 6 * D), f32)])
    dm_mine = jnp.concatenate([dm_lat.reshape(2, 6 * D), dm_ctx.reshape(2, 6 * D),
                               jnp.zeros((4, 6 * D), f32)], axis=0)
    dm_g = allgather_small(dm_mine, "gather_dmod")
    dm_all = jnp.concatenate([jnp.moveaxis(dm_g[:, 0:2], 0, 1), jnp.moveaxis(dm_g[:, 2:4], 0, 1)], axis=1)
    dm_sh = lax.dynamic_slice_in_dim(dm_all, me * mod_cols, mod_cols, axis=2)
    g_mod_w, g_cctx_part, g_mod_b = mod_bwd(c16, mod_w, dm_sh, dm_all, "mod_bwd")

    rep = [jnp.stack([g_n1_0[0], g_n1_1[0]]), jnp.stack([gf0["norm2"][0], gf1["norm2"][0]]), g_conv_b5, g_dt_bias, g_a_log,
           g_ssd_d, g_ssd_norm, jnp.stack([gf0["conv_b"][0], gf1["conv_b"][0]]), g_final, g_cctx_part, loss_part[:, :1]]
    rep_sizes = [r_.size for r_ in rep]
    rep_flat = _pad_to(jnp.concatenate([r_.reshape(-1) for r_ in rep]), 8 * PACK_W).reshape(-1, PACK_W)
    _, rep_sum = allgather_small(rep_flat, "reduce_replicated", with_sum=True)
    rep_sum = rep_sum.reshape(-1)
    roffs = [0]
    for s_ in rep_sizes:
        roffs.append(roffs[-1] + s_)
    rp = [rep_sum[roffs[i]:roffs[i + 1]] for i in range(len(rep_sizes))]
    loss = rp[10].reshape(())

    r_up1, r_down1, r_pw2, r_pw1, r_up0, r_down0, r_out = early_red
    big = {}
    def tr(t):
        return jnp.swapaxes(t, -1, -2)

    up_t, m_up_t, v_up_t = tr(ffn_w_up), tr(m_ffn_w_up), tr(v_ffn_w_up)
    send_sems, recv_sems, late_p, late_land, token = late_flying
    up0 = sum_adamw(r_up0, up_t, m_up_t, v_up_t, 0, "adamw_ffn_w_up0", after=token)
    big["ffn_w_up"] = tuple(tr(t) for t in sum_adamw(r_up1, up_t, m_up_t, v_up_t, 1, "adamw_ffn_w_up1", into=up0))
    big["conf_w_pw1"] = sum_adamw(r_pw1, conf_w_pw1[0], m_conf_w_pw1[0], v_conf_w_pw1[0], None, "adamw_conf_w_pw1")
    big["ssd_w_out"] = sum_adamw(r_out, ssd_w_out[0], m_ssd_w_out[0], v_ssd_w_out[0], None, "adamw_ssd_w_out")
    dn0 = sum_adamw(r_down0, ffn_w_down, m_ffn_w_down, v_ffn_w_down, 0, "adamw_ffn_w_down0")
    big["ffn_w_down"] = sum_adamw(r_down1, ffn_w_down, m_ffn_w_down, v_ffn_w_down, 1, "adamw_ffn_w_down1", into=dn0)
    big["conf_w_pw2"] = sum_adamw(r_pw2, conf_w_pw2[0], m_conf_w_pw2[0], v_conf_w_pw2[0], None, "adamw_conf_w_pw2")
    r_in, r_small = chips_wait(send_sems, recv_sems, late_p, late_land, big["conf_w_pw2"][0], "reduce_chips_late_wait")
    w_in_res = sum_adamw(r_in, tr(ssd_w_in[0]), tr(m_ssd_w_in[0]), tr(v_ssd_w_in[0]), None, "adamw_ssd_w_in")
    big["ssd_w_in"] = tuple(tr(t) for t in w_in_res)
    g_flat = sum_rows(r_small, "reduce_sum_small").reshape(-1)
    goffs = [0]
    for s_ in gsizes:
        goffs.append(goffs[-1] + s_)
    gs = [g_flat[goffs[i]:goffs[i + 1]] for i in range(len(gsizes))]
    grads = {
        "c_ctx": rp[9], "mod_w": g_mod_w, "mod_b": g_mod_b, "norm1_w": rp[0], "norm2_w": rp[1],
        "ssd_conv_w": gs[0], "ssd_conv_b": rp[2], "ssd_dt_bias": rp[3], "ssd_a_log": rp[4], "ssd_d": rp[5],
        "ssd_norm_w": rp[6], "conf_b_pw1": gs[1], "conf_w_dw": gs[2],
        "conf_b_dw": gs[3], "conf_ln_w": gs[4], "conf_ln_b": gs[5], "conf_b_pw2": gs[6],
        "ffn_conv_w": gs[7], "ffn_conv_b": rp[7], "final_norm_w": rp[8],
    }
    weights = dict(c_ctx=c_ctx, mod_w=mod_w, mod_b=mod_b, norm1_w=norm1_w, norm2_w=norm2_w, ssd_w_in=ssd_w_in, ssd_conv_w=ssd_conv_w, ssd_conv_b=ssd_conv_b, ssd_dt_bias=ssd_dt_bias, ssd_a_log=ssd_a_log, ssd_d=ssd_d, ssd_norm_w=ssd_norm_w, ssd_w_out=ssd_w_out, conf_w_pw1=conf_w_pw1, conf_b_pw1=conf_b_pw1, conf_w_dw=conf_w_dw, conf_b_dw=conf_b_dw, conf_ln_w=conf_ln_w, conf_ln_b=conf_ln_b, conf_w_pw2=conf_w_pw2, conf_b_pw2=conf_b_pw2, ffn_w_up=ffn_w_up, ffn_conv_w=ffn_conv_w, ffn_conv_b=ffn_conv_b, ffn_w_down=ffn_w_down, final_norm_w=final_norm_w)
    m_in = dict(c_ctx=m_c_ctx, mod_w=m_mod_w, mod_b=m_mod_b, norm1_w=m_norm1_w, norm2_w=m_norm2_w, ssd_w_in=m_ssd_w_in, ssd_conv_w=m_ssd_conv_w, ssd_conv_b=m_ssd_conv_b, ssd_dt_bias=m_ssd_dt_bias, ssd_a_log=m_ssd_a_log, ssd_d=m_ssd_d, ssd_norm_w=m_ssd_norm_w, ssd_w_out=m_ssd_w_out, conf_w_pw1=m_conf_w_pw1, conf_b_pw1=m_conf_b_pw1, conf_w_dw=m_conf_w_dw, conf_b_dw=m_conf_b_dw, conf_ln_w=m_conf_ln_w, conf_ln_b=m_conf_ln_b, conf_w_pw2=m_conf_w_pw2, conf_b_pw2=m_conf_b_pw2, ffn_w_up=m_ffn_w_up, ffn_conv_w=m_ffn_conv_w, ffn_conv_b=m_ffn_conv_b, ffn_w_down=m_ffn_w_down, final_norm_w=m_final_norm_w)
    v_in = dict(c_ctx=v_c_ctx, mod_w=v_mod_w, mod_b=v_mod_b, norm1_w=v_norm1_w, norm2_w=v_norm2_w, ssd_w_in=v_ssd_w_in, ssd_conv_w=v_ssd_conv_w, ssd_conv_b=v_ssd_conv_b, ssd_dt_bias=v_ssd_dt_bias, ssd_a_log=v_ssd_a_log, ssd_d=v_ssd_d, ssd_norm_w=v_ssd_norm_w, ssd_w_out=v_ssd_w_out, conf_w_pw1=v_conf_w_pw1, conf_b_pw1=v_conf_b_pw1, conf_w_dw=v_conf_w_dw, conf_b_dw=v_conf_b_dw, conf_ln_w=v_conf_ln_w, conf_ln_b=v_conf_ln_b, conf_w_pw2=v_conf_w_pw2, conf_b_pw2=v_conf_b_pw2, ffn_w_up=v_ffn_w_up, ffn_conv_w=v_ffn_conv_w, ffn_conv_b=v_ffn_conv_b, ffn_w_down=v_ffn_w_down, final_norm_w=v_final_norm_w)

    out_g, out_d, out_m, out_v = [], [], [], []
    for name_, w_ in weights.items():
        shape = w_.shape
        if name_ in big:
            for lst, t in zip((out_g, out_d, out_m, out_v), big[name_]):
                lst.append(t.reshape(shape))
            continue
        cols2 = shape[-1] if len(shape) > 1 else shape[0]
        g2 = grads[name_].reshape(-1, cols2)
        d_, nm_, nv_ = adamw(w_.reshape(-1, cols2), g2, m_in[name_].reshape(-1, cols2), v_in[name_].reshape(-1, cols2),
                             f"adamw_{name_}")
        out_g.append(g2.reshape(shape))
        out_d.append(d_.reshape(shape))
        out_m.append(nm_.reshape(shape))
        out_v.append(nv_.reshape(shape))
    return (loss, grad_x, *out_g, *out_d, *out_m, *out_v)
```

```python
import functools

import jax
import jax.numpy as jnp
from jax import lax
from jax.experimental import pallas as pl
from jax.experimental.pallas import tpu as pltpu

f32 = jnp.float32
bf16 = jnp.bfloat16
HI = lax.Precision.HIGHEST
S = jax.ShapeDtypeStruct
MESH = pl.DeviceIdType.MESH
ANY = pl.BlockSpec(memory_space=pl.ANY)
VMEM = pl.BlockSpec(memory_space=pltpu.VMEM)

N_DEV = 8
D = 1024
DI = 2048
CONVD = 4096
FH = 2816
GRID_W = 64
Q = 128
HPG = 4
P = 64
N = 128
G = 8
GW = HPG * P
NH_SSD = G * HPG
EPS = 1e-6
ADAM_LR, ADAM_B1, ADAM_B2, ADAM_EPS, ADAM_WD, ADAM_STEP = 0.001, 0.9, 0.999, 1e-08, 0.01, 10
VMEM_LIMIT_BYTES = 56 * 1024 * 1024
PACK_W = 1024
TB = 256


def _cparams(*sem):
    return pltpu.CompilerParams(dimension_semantics=sem, vmem_limit_bytes=VMEM_LIMIT_BYTES)


def _pick(n, prefs):
    for p in prefs:
        if n % p == 0:
            return p
    return n


def _sigmoid(x):
    return 1.0 / (1.0 + jnp.exp(-x))


def _softplus(x):
    return jnp.maximum(x, 0.0) + jnp.log(1.0 + jnp.exp(-jnp.abs(x)))


def matmul(a, b, mode, out_dtype, name):
    if mode == "nn":
        (M, K), (_, Nn) = a.shape, b.shape
        bm, bn, bk = _pick(M, (512, 384, 256, 128)), Nn, K
    elif mode == "tn":
        (K, M), (_, Nn) = a.shape, b.shape
        bm, bn, bk = M, Nn, _pick(K, (256, 128))
    else:
        (M, K), (Nn, _) = a.shape, b.shape
        bm, bn, bk = _pick(M, (512, 384, 256, 128)), Nn, K
    nk = K // bk
    dims = {"nn": (((1,), (0,)), ((), ())), "tn": (((0,), (0,)), ((), ())), "nt": (((1,), (1,)), ((), ()))}[mode]

    def body(a_ref, b_ref, o_ref, acc_ref):
        k = pl.program_id(2)

        @pl.when(k == 0)
        def _():
            acc_ref[...] = jnp.zeros_like(acc_ref)

        acc_ref[...] += lax.dot_general(a_ref[...].astype(bf16), b_ref[...].astype(bf16), dims,
                                        preferred_element_type=f32)

        @pl.when(k == nk - 1)
        def _():
            o_ref[...] = acc_ref[...].astype(out_dtype)

    if mode == "nn":
        a_spec = pl.BlockSpec((bm, bk), lambda i, j, k: (i, k))
        b_spec = pl.BlockSpec((bk, bn), lambda i, j, k: (k, j))
    elif mode == "tn":
        a_spec = pl.BlockSpec((bk, bm), lambda i, j, k: (k, i))
        b_spec = pl.BlockSpec((bk, bn), lambda i, j, k: (k, j))
    else:
        a_spec = pl.BlockSpec((bm, bk), lambda i, j, k: (i, k))
        b_spec = pl.BlockSpec((bn, bk), lambda i, j, k: (j, k))
    return pl.pallas_call(
        body, grid=(M // bm, Nn // bn, nk), in_specs=[a_spec, b_spec],
        out_specs=pl.BlockSpec((bm, bn), lambda i, j, k: (i, j)),
        out_shape=S((M, Nn), out_dtype), scratch_shapes=[pltpu.VMEM((bm, bn), f32)],
        compiler_params=_cparams("parallel", "parallel", "arbitrary"), name=name,
    )(a, b)


SMM_BM = 256
SMM_ROWS = (256,)


def _shard_pieces(seg_widths, n):
    bounds = [0]
    for sw in seg_widths:
        bounds.append(bounds[-1] + sw)
    assert bounds[-1] == N_DEV * n, (seg_widths, n)
    out = []
    for j in range(N_DEV):
        lo, hi = j * n, (j + 1) * n
        pcs = []
        for si in range(len(seg_widths)):
            a, b = max(lo, bounds[si]), min(hi, bounds[si + 1])
            if a < b:
                pcs.append((si, a - bounds[si], a - lo, b - a))
        out.append(pcs)
    return out


def _w_spec(w, layer):
    if layer is None:
        return pl.BlockSpec(w.shape, lambda *idx: (0, 0, 0))
    return pl.BlockSpec((N_DEV, None) + w.shape[2:], lambda *idx: (0, layer, 0, 0))


def smm_fwd(a, w, layer, seg_widths, name, hosted=None):
    M, K = a.shape
    n = w.shape[-1]
    pieces = _shard_pieces(seg_widths, n)
    padded = [sw + (-sw) % 128 for sw in seg_widths]
    bm = _pick(M, SMM_ROWS)

    def body(a_ref, w_ref, *o_refs):
        av = a_ref[...]
        for si, sw in enumerate(seg_widths):
            if padded[si] != sw:
                o_refs[si][:, pl.ds(padded[si] - 128, 128)] = jnp.zeros((bm, 128), f32)
        for j in range(N_DEV):
            for si, soff, woff, wd in pieces[j]:
                o_refs[si][:, pl.ds(soff, wd)] = jnp.dot(av, w_ref[j, :, pl.ds(woff, wd)], preferred_element_type=f32)

    outs, extra = _host_call(
        body, (M // bm,), [pl.BlockSpec((bm, K), lambda i: (i, 0)), _w_spec(w, layer)],
        [pl.BlockSpec((bm, pw), lambda i: (i, 0)) for pw in padded], [S((M, pw), f32) for pw in padded], [],
        ("parallel",), name, (a, w), hosted)
    return outs if hosted is None else (outs, extra)


def smm_dx(d_segs, w, layer, seg_widths, out_dtype, name, hosted=None):
    M = d_segs[0].shape[0]
    K, n = w.shape[-2], w.shape[-1]
    pieces = _shard_pieces(seg_widths, n)
    ns = len(d_segs)
    bm = _pick(M, SMM_ROWS)

    def body(*refs):
        d_refs, w_ref, o_ref = refs[:ns], refs[ns], refs[ns + 1]
        acc = jnp.zeros((bm, K), f32)
        for j in range(N_DEV):
            for si, soff, woff, wd in pieces[j]:
                acc = acc + lax.dot_general(d_refs[si][:, pl.ds(soff, wd)], w_ref[j, :, pl.ds(woff, wd)],
                                            (((1,), (1,)), ((), ())), preferred_element_type=f32)
        o_ref[...] = acc.astype(out_dtype)

    (out,), extra = _host_call(
        body, (M // bm,),
        [pl.BlockSpec((bm, d.shape[1]), lambda i: (i, 0)) for d in d_segs] + [_w_spec(w, layer)],
        [pl.BlockSpec((bm, K), lambda i: (i, 0))], [S((M, K), out_dtype)], [], ("parallel",), name,
        (*d_segs, w), hosted)
    return out, extra


def smm_dw(a, d_segs, n, seg_widths, ngrp, transposed, name):
    M, K = a.shape
    pieces = _shard_pieces(seg_widths, n)
    per = N_DEV // ngrp
    nI = M // SMM_BM
    ns = len(d_segs)
    shard = (n, K) if transposed else (K, n)

    def body(*refs):
        a_ref, d_refs, o_ref, acc_ref = refs[0], refs[1:1 + ns], refs[1 + ns], refs[2 + ns]
        grp = pl.program_id(0)
        i = pl.program_id(1)

        @pl.when(i == 0)
        def _():
            acc_ref[...] = jnp.zeros_like(acc_ref)

        av = a_ref[...]
        for gs in range(ngrp):
            def one_group(gs=gs):
                for jj in range(per):
                    for si, soff, woff, wd in pieces[gs * per + jj]:
                        dv = d_refs[si][:, pl.ds(soff, wd)]
                        if transposed:
                            acc_ref[jj, pl.ds(woff, wd), :] += lax.dot_general(
                                dv, av, (((0,), (0,)), ((), ())), preferred_element_type=f32)
                        else:
                            acc_ref[jj, :, pl.ds(woff, wd)] += lax.dot_general(
                                av, dv, (((0,), (0,)), ((), ())), preferred_element_type=f32)
            pl.when(grp == gs)(one_group)

        @pl.when(i == nI - 1)
        def _():
            o_ref[...] = acc_ref[...].astype(bf16)

    return pl.pallas_call(
        body, grid=(ngrp, nI),
        in_specs=[pl.BlockSpec((SMM_BM, K), lambda g, i: (i, 0))]
        + [pl.BlockSpec((SMM_BM, d.shape[1]), lambda g, i: (i, 0)) for d in d_segs],
        out_specs=pl.BlockSpec((per,) + shard, lambda g, i: (g, 0, 0)), out_shape=S((N_DEV,) + shard, bf16),
        scratch_shapes=[pltpu.VMEM((per,) + shard, f32)],
        compiler_params=_cparams("arbitrary", "arbitrary"), name=name)(a, *d_segs)


def _modnorm_f(h, w, sc, sh):
    y = h * lax.rsqrt(jnp.mean(h * h, axis=-1, keepdims=True) + EPS)
    return (y * w) * (1.0 + sc) + sh


def _kind_specs(nctxb):
    if nctxb > 0:
        return pl.BlockSpec((None, 1, D), lambda i: (jnp.where(i < nctxb, 0, 1), 0, 0))
    return pl.BlockSpec((None, 1, D), lambda i: (0, 0, 0))


def _two_part_specs(nctxb):
    return (pl.BlockSpec((TB, D), lambda i: (jnp.minimum(i, nctxb - 1), 0)),
            pl.BlockSpec((TB, D), lambda i: (jnp.maximum(i - nctxb, 0), 0)))


def modnorm_fwd(h, w, sc, sh, nctxb, name, ctx=None):
    if ctx is None:
        T = h.shape[0]

        def body(h_ref, w_ref, sc_ref, sh_ref, o_ref):
            o_ref[...] = _modnorm_f(h_ref[...], w_ref[...], sc_ref[...], sh_ref[...]).astype(bf16)

        hspecs, hargs = [pl.BlockSpec((TB, D), lambda i: (i, 0))], (h,)
    else:
        T = h.shape[0] + ctx.shape[0]

        def body(c_ref, h_ref, w_ref, sc_ref, sh_ref, o_ref):
            hv = jnp.where(pl.program_id(0) < nctxb, c_ref[...], h_ref[...])
            o_ref[...] = _modnorm_f(hv, w_ref[...], sc_ref[...], sh_ref[...]).astype(bf16)

        hspecs, hargs = list(_two_part_specs(nctxb)), (ctx, h)
    row = pl.BlockSpec((1, D), lambda i: (0, 0))
    ks = _kind_specs(nctxb)
    return pl.pallas_call(body, grid=(T // TB,), in_specs=hspecs + [row, ks, ks],
                          out_specs=pl.BlockSpec((TB, D), lambda i: (i, 0)), out_shape=S((T, D), bf16),
                          compiler_params=_cparams("parallel"), name=name)(*hargs, w, sc, sh)


def modnorm_bwd(h, w, sc, sh, da, dres, nctxb, name, ctx=None):
    T = h.shape[0] + (0 if ctx is None else ctx.shape[0])
    kinds = sc.shape[0]
    nh = 1 if ctx is None else 2

    def body(*refs):
        w_ref, sc_ref, sh_ref, da_ref, dres_ref, dh_ref, dw_ref, dsc_ref, dsh_ref = refs[nh:]
        i = pl.program_id(0)
        hv = refs[0][...] if ctx is None else jnp.where(i < nctxb, refs[0][...], refs[1][...])
        _, vjp = jax.vjp(_modnorm_f, hv, w_ref[...], sc_ref[...], sh_ref[...])
        dh, dw, dsc, dsh = vjp(da_ref[...].astype(f32))
        dh_ref[...] = dres_ref[...] + dh

        @pl.when(i == 0)
        def _():
            dw_ref[...] = jnp.zeros_like(dw_ref)

        @pl.when((i == 0) | (i == nctxb))
        def _():
            dsc_ref[...] = jnp.zeros_like(dsc_ref)
            dsh_ref[...] = jnp.zeros_like(dsh_ref)

        dw_ref[...] += dw
        dsc_ref[...] += dsc
        dsh_ref[...] += dsh

    blk = pl.BlockSpec((TB, D), lambda i: (i, 0))
    lat = pl.BlockSpec((TB, D), lambda i: (jnp.maximum(i - nctxb, 0), 0))
    row = pl.BlockSpec((1, D), lambda i: (0, 0))
    ks = _kind_specs(nctxb)
    hspecs, hargs = ([blk], (h,)) if ctx is None else (list(_two_part_specs(nctxb)), (ctx, h))
    return pl.pallas_call(
        body, grid=(T // TB,), in_specs=hspecs + [row, ks, ks, blk, lat], out_specs=[lat, row, ks, ks],
        out_shape=[S((T - nctxb * TB, D), f32), S((1, D), f32), S((kinds, 1, D), f32), S((kinds, 1, D), f32)],
        compiler_params=_cparams("arbitrary"), name=name)(*hargs, w, sc, sh, da, dres)


def resgate_fwd(h, o, g, b, name):
    T = h.shape[0]

    def body(h_ref, o_ref, g_ref, b_ref, out_ref):
        out_ref[...] = h_ref[...] + g_ref[...] * (o_ref[...] + b_ref[...])

    blk = pl.BlockSpec((TB, D), lambda i: (i, 0))
    row = pl.BlockSpec((1, D), lambda i: (0, 0))
    return pl.pallas_call(body, grid=(T // TB,), in_specs=[blk, blk, row, row], out_specs=blk,
                          out_shape=S((T, D), f32), compiler_params=_cparams("parallel"), name=name)(h, o, g, b)


def resgate_bwd(dh, o, g, b, name):
    T = dh.shape[0]

    def body(dh_ref, o_ref, g_ref, b_ref, do_ref, dg_ref, db_ref):
        i = pl.program_id(0)

        @pl.when(i == 0)
        def _():
            dg_ref[...] = jnp.zeros_like(dg_ref)
            db_ref[...] = jnp.zeros_like(db_ref)

        dh = dh_ref[...]
        do = g_ref[...] * dh
        do_ref[...] = do.astype(bf16)
        dg_ref[...] += jnp.sum(dh * (o_ref[...] + b_ref[...]), axis=0, keepdims=True)
        db_ref[...] += jnp.sum(do, axis=0, keepdims=True)

    blk = pl.BlockSpec((TB, D), lambda i: (i, 0))
    row = pl.BlockSpec((1, D), lambda i: (0, 0))
    return pl.pallas_call(body, grid=(T // TB,), in_specs=[blk, blk, row, row], out_specs=[blk, row, row],
                          out_shape=[S((T, D), bf16), S((1, D), f32), S((1, D), f32)],
                          compiler_params=_cparams("arbitrary"), name=name)(dh, o, g, b)


def final_loss(h, w, tgt, name):
    T = h.shape[0]

    def f(hv, wv, tv):
        y = (hv * lax.rsqrt(jnp.mean(hv * hv, axis=-1, keepdims=True) + EPS)) * wv
        e = y - tv
        return 0.5 * jnp.sum(jnp.sum(e * e, axis=-1, keepdims=True), axis=0, keepdims=True) * (1.0 / D)

    def body(h_ref, w_ref, t_ref, loss_ref, dh_ref, dw_ref):
        i = pl.program_id(0)
        tv = t_ref[...]
        val, vjp = jax.vjp(lambda a, b_: f(a, b_, tv), h_ref[...], w_ref[...])
        dh, dw = vjp(jnp.ones((1, 1), f32))
        dh_ref[...] = dh

        @pl.when(i == 0)
        def _():
            loss_ref[...] = jnp.zeros_like(loss_ref)
            dw_ref[...] = jnp.zeros_like(dw_ref)

        loss_ref[...] += jnp.broadcast_to(val, (1, 128))
        dw_ref[...] += dw

    blk = pl.BlockSpec((TB, D), lambda i: (i, 0))
    row = pl.BlockSpec((1, D), lambda i: (0, 0))
    return pl.pallas_call(body, grid=(T // TB,), in_specs=[blk, row, blk],
                          out_specs=[pl.BlockSpec((1, 128), lambda i: (0, 0)), blk, row],
                          out_shape=[S((1, 128), f32), S((T, D), f32), S((1, D), f32)],
                          compiler_params=_cparams("arbitrary"), name=name)(h, w, tgt)


CB = 256
RT = 32
RTB = 16


def _fold8(t):
    acc = t[0:8]
    for k in range(1, t.shape[0] // 8):
        acc = acc + t[8 * k:8 * (k + 1)]
    return acc


def _rows(start, off=0, rt=RT):
    return pl.ds(pl.multiple_of(start + off, 8), rt)


def _rowsb(start, off=0):
    return _rows(start, off, RTB)


def _zero_rows(ref, start, n):
    ref[pl.ds(start, n), :] = jnp.zeros((n, ref.shape[1]), f32)


K5, HALF5, PAD5 = 5, 2, 8


def _shift_copies5(base_ref, s_ref, ln, sign):
    for k in range(K5):
        s_ref[k, pl.ds(0, ln), :] = base_ref[pl.ds(PAD5 + sign * (k - HALF5), ln), :]


def ssd_conv_fwd(u, w, b, segs, name, hosted=None):
    T = u.shape[0]
    maxlen = max(ln for _, ln in segs)

    def body(u_ref, w_ref, b_ref, o_ref, ds_ref, base_ref, s_ref):
        wv = [w_ref[pl.ds(k, 1), :] for k in range(K5)]
        bv = b_ref[...]
        for s0, ln in segs:
            _zero_rows(base_ref, 0, PAD5)
            _zero_rows(base_ref, PAD5 + ln, PAD5)
            base_ref[pl.ds(PAD5, ln), :] = u_ref[pl.ds(s0, ln), :]
            _shift_copies5(base_ref, s_ref, ln, 1)

            def tile(i, carry):
                r = i * RT
                acc = jnp.broadcast_to(bv, (RT, CB))
                for k in range(K5):
                    acc = acc + s_ref[k, _rows(r), :] * wv[k]
                sg = _sigmoid(acc)
                o_ref[_rows(r, s0), :] = acc * sg
                ds_ref[_rows(r, s0), :] = sg * (1.0 + acc * (1.0 - sg))
                return carry

            lax.fori_loop(0, ln // RT, tile, 0, unroll=2)

    cblk = pl.BlockSpec((T, CB), lambda j: (0, j))
    (out, dsilu), extra = _host_call(
        body, (CONVD // CB,),
        [cblk, pl.BlockSpec((K5, CB), lambda j: (0, j)), pl.BlockSpec((1, CB), lambda j: (0, j))],
        [cblk, cblk], [S((T, CONVD), f32), S((T, CONVD), f32)],
        [pltpu.VMEM((maxlen + 2 * PAD5, CB), f32), pltpu.VMEM((K5, maxlen, CB), f32)],
        ("parallel",), name, (u, w, b), hosted)
    return out, dsilu, extra


def ssd_conv_bwd(proj, w, dsilu, dy2, dyskip, dexp, segs, name):
    T = proj.shape[0]
    maxlen = max(ln for _, ln in segs)
    nskip = DI // CB

    def body(u_ref, w_ref, ds_ref, dya_ref, dyb_ref, dsk_ref, dexp_ref, du_ref, dw_ref, db_ref, base_ref, s_ref):
        wv = [w_ref[pl.ds(k, 1), :] for k in range(K5)]
        has_skip = (pl.program_id(0) < nskip).astype(f32) * dexp_ref[...]
        acc8 = tuple(jnp.zeros((8, CB), f32) for _ in range(K5 + 1))
        for s0, ln in segs:
            _zero_rows(base_ref, 0, PAD5)
            _zero_rows(base_ref, PAD5 + ln, PAD5)
            base_ref[pl.ds(PAD5, ln), :] = u_ref[pl.ds(s0, ln), :]
            _shift_copies5(base_ref, s_ref, ln, 1)

            def tile1(i, carry):
                r = i * RTB
                dy = dya_ref[_rowsb(r, s0), :] + dyb_ref[_rowsb(r, s0), :] + has_skip * dsk_ref[_rowsb(r, s0), :]
                dpre = dy * ds_ref[_rowsb(r, s0), :]
                base_ref[_rowsb(r, PAD5), :] = dpre
                new = [carry[k] + _fold8(dpre * s_ref[k, _rowsb(r), :]) for k in range(K5)]
                new.append(carry[K5] + _fold8(dpre))
                return tuple(new)

            acc8 = lax.fori_loop(0, ln // RTB, tile1, acc8, unroll=2)
            _shift_copies5(base_ref, s_ref, ln, -1)

            def tile2(i, carry):
                r = i * RTB
                du = jnp.zeros((RTB, CB), f32)
                for k in range(K5):
                    du = du + s_ref[k, _rowsb(r), :] * wv[k]
                du_ref[_rowsb(r, s0), :] = du.astype(bf16)
                return carry

            lax.fori_loop(0, ln // RTB, tile2, 0, unroll=4)
        for k in range(K5):
            dw_ref[pl.ds(k, 1), :] = jnp.sum(acc8[k], axis=0, keepdims=True)
        db_ref[...] = jnp.sum(acc8[K5], axis=0, keepdims=True)

    cblk = pl.BlockSpec((T, CB), lambda j: (0, j))
    return pl.pallas_call(
        body, grid=(CONVD // CB,),
        in_specs=[cblk, pl.BlockSpec((K5, CB), lambda j: (0, j)), cblk,
                  pl.BlockSpec((None, T, CB), lambda j: (0, 0, j)), pl.BlockSpec((None, T, CB), lambda j: (1, 0, j)),
                  pl.BlockSpec((T, CB), lambda j: (0, jnp.minimum(j, nskip - 1))),
                  pl.BlockSpec((1, CB), lambda j: (0, jnp.minimum(j, nskip - 1)))],
        out_specs=[cblk, pl.BlockSpec((K5, CB), lambda j: (0, j)), pl.BlockSpec((1, CB), lambda j: (0, j))],
        out_shape=[S((T, CONVD), bf16), S((K5, CONVD), f32), S((1, CONVD), f32)],
        scratch_shapes=[pltpu.VMEM((maxlen + 2 * PAD5, CB), f32), pltpu.VMEM((K5, maxlen, CB), f32)],
        compiler_params=_cparams("parallel"), name=name)(proj, w, dsilu, dy2, dy2, dyskip, dexp)


GPAD = GRID_W


def _grid_copies(g_ref, src, L):
    col = lax.broadcasted_iota(jnp.int32, (L, CB), 0) & (GRID_W - 1)
    for d in range(3):
        _zero_rows(g_ref.at[d], 0, GPAD)
        _zero_rows(g_ref.at[d], GPAD + L, GPAD)
    g_ref[1, pl.ds(GPAD, L), :] = src
    g_ref[0, pl.ds(GPAD, L), :] = jnp.where(col != 0, g_ref[1, pl.ds(GPAD - 1, L), :], 0.0)
    g_ref[2, pl.ds(GPAD, L), :] = jnp.where(col != GRID_W - 1, g_ref[1, pl.ds(GPAD + 1, L), :], 0.0)


def ffn_gate_fwd(val, gate, cw, cb_, name, hosted=None):
    L = val.shape[0]
    nb = FH // CB

    def body(val_ref, gate_ref, w_ref, b_ref, o_ref, s_ref, vds_ref, g_ref):
        wv = [w_ref[pl.ds(t, 1), :] for t in range(9)]
        bv = b_ref[...]
        _grid_copies(g_ref, gate_ref[...], L)

        def tile(i, carry):
            r = i * RT
            acc = jnp.broadcast_to(bv, (RT, CB))
            for dr in range(3):
                for dc in range(3):
                    acc = acc + g_ref[dc, _rows(r, GPAD + (dr - 1) * GRID_W), :] * wv[3 * dr + dc]
            sg = _sigmoid(acc)
            s = acc * sg
            v = val_ref[_rows(r), :]
            o_ref[_rows(r), :] = (s * v).astype(bf16)
            s_ref[_rows(r), :] = s
            vds_ref[_rows(r), :] = v * (sg * (1.0 + acc * (1.0 - sg)))
            return carry

        lax.fori_loop(0, L // RT, tile, 0, unroll=2)

    cblk = pl.BlockSpec((L, CB), lambda j: (0, j))
    (out, s_, vds), extra = _host_call(
        body, (nb,), [cblk, cblk, pl.BlockSpec((9, CB), lambda j: (0, j)), pl.BlockSpec((1, CB), lambda j: (0, j))],
        [cblk, cblk, cblk], [S((L, FH), bf16), S((L, FH), f32), S((L, FH), f32)],
        [pltpu.VMEM((3, L + 2 * GPAD, CB), f32)], ("parallel",), name, (val, gate, cw, cb_), hosted)
    return out, s_, vds, extra


def ffn_gate_bwd(gate, s_, vds, cw, dact, name):
    L = gate.shape[0]
    nb = FH // CB

    def body(gate_ref, s_ref, vds_ref, w_ref, da_ref, dval_ref, dgate_ref, dw_ref, db_ref, g_ref, d_ref):
        wv = [w_ref[pl.ds(t, 1), :] for t in range(9)]
        _grid_copies(g_ref, gate_ref[...], L)

        def tile1(i, carry):
            r = i * RTB
            da = da_ref[_rowsb(r), :].astype(f32)
            dval_ref[_rowsb(r), :] = (da * s_ref[_rowsb(r), :]).astype(bf16)
            dpre = da * vds_ref[_rowsb(r), :]
            d_ref[_rowsb(r), :] = dpre
            new = [carry[t] + _fold8(dpre * g_ref[t % 3, _rowsb(r, GPAD + (t // 3 - 1) * GRID_W), :]) for t in range(9)]
            new.append(carry[9] + _fold8(dpre))
            return tuple(new)

        acc8 = lax.fori_loop(0, L // RTB, tile1, tuple(jnp.zeros((8, CB), f32) for _ in range(10)), unroll=2)
        for t in range(9):
            dw_ref[pl.ds(t, 1), :] = jnp.sum(acc8[t], axis=0, keepdims=True)
        db_ref[...] = jnp.sum(acc8[9], axis=0, keepdims=True)
        _grid_copies(g_ref, d_ref[...], L)

        def tile2(i, carry):
            r = i * RTB
            dg = jnp.zeros((RTB, CB), f32)
            for dr in range(3):
                for dc in range(3):
                    dg = dg + g_ref[2 - dc, _rowsb(r, GPAD - (dr - 1) * GRID_W), :] * wv[3 * dr + dc]
            dgate_ref[_rowsb(r), :] = dg.astype(bf16)
            return carry

        lax.fori_loop(0, L // RTB, tile2, 0, unroll=4)

    cblk = pl.BlockSpec((L, CB), lambda j: (0, j))
    return pl.pallas_call(
        body, grid=(nb,),
        in_specs=[cblk, cblk, cblk, pl.BlockSpec((9, CB), lambda j: (0, j)), cblk],
        out_specs=[cblk, cblk, pl.BlockSpec((9, CB), lambda j: (0, j)), pl.BlockSpec((1, CB), lambda j: (0, j))],
        out_shape=[S((L, FH), bf16), S((L, FH), bf16), S((9, FH), f32), S((1, FH), f32)],
        scratch_shapes=[pltpu.VMEM((3, L + 2 * GPAD, CB), f32), pltpu.VMEM((L, CB), f32)],
        compiler_params=_cparams("parallel"), name=name)(gate, s_, vds, cw, dact)


CONF_K = 31
CHALF = CONF_K // 2
CPAD = 16


def _shift_copies8(c_ref, base_ref, L):
    n = L + 2 * CPAD - 8
    for b_ in range(8):
        c_ref[b_, pl.ds(0, n), :] = base_ref[pl.ds(b_, n), :]


def _tap_ab(o):
    return o % 8, o - o % 8


def conf_glu_conv_fwd(pa, pg, b1, wdw, bdw, name, hosted=None):
    L = pa.shape[0]
    nb = D // CB

    def body(pa_ref, pg_ref, ba_ref, bg_ref, w_ref, bdw_ref, o_ref, base_ref, c_ref):
        _zero_rows(base_ref, 0, CPAD)
        _zero_rows(base_ref, CPAD + L, CPAD)
        base_ref[pl.ds(CPAD, L), :] = (pa_ref[...] + ba_ref[...]) * _sigmoid(pg_ref[...] + bg_ref[...])
        _shift_copies8(c_ref, base_ref, L)
        bv = bdw_ref[...]

        def tile(i, carry):
            r = i * RT
            acc = jnp.broadcast_to(bv, (RT, CB))
            for k in range(CONF_K):
                b_, a8 = _tap_ab(k - CHALF)
                acc = acc + c_ref[b_, _rows(r, CPAD + a8), :] * w_ref[pl.ds(k, 1), :]
            o_ref[_rows(r), :] = acc
            return carry

        lax.fori_loop(0, L // RT, tile, 0, unroll=2)

    cblk = pl.BlockSpec((L, CB), lambda j: (0, j))
    rblk = pl.BlockSpec((1, CB), lambda j: (0, j))
    rgblk = pl.BlockSpec((1, CB), lambda j: (0, nb + j))
    (out,), extra = _host_call(
        body, (nb,), [cblk, cblk, rblk, rgblk, pl.BlockSpec((CONF_K, CB), lambda j: (0, j)), rblk],
        [cblk], [S((L, D), f32)], [pltpu.VMEM((L + 2 * CPAD, CB), f32), pltpu.VMEM((8, L + 2 * CPAD, CB), f32)],
        ("parallel",), name, (pa, pg, b1, b1, wdw, bdw), hosted)
    return out, extra


def conf_glu_conv_bwd(pa, pg, b1, wdw, dy, name):
    L = pa.shape[0]
    nb = D // CB

    def body(pa_ref, pg_ref, ba_ref, bg_ref, w_ref, dy_ref, dpa_ref, dpg_ref, dba_ref, dbg_ref, dw_ref, dbdw_ref,
             base_ref, c_ref, acc_ref):
        _zero_rows(base_ref, 0, CPAD)
        _zero_rows(base_ref, CPAD + L, CPAD)
        base_ref[pl.ds(CPAD, L), :] = (pa_ref[...] + ba_ref[...]) * _sigmoid(pg_ref[...] + bg_ref[...])
        _shift_copies8(c_ref, base_ref, L)
        acc_ref[...] = jnp.zeros_like(acc_ref)

        def tile1(i, carry):
            r = i * RTB
            dyt = dy_ref[_rowsb(r), :]
            for k in range(CONF_K):
                b_, a8 = _tap_ab(k - CHALF)
                acc_ref[k] += _fold8(dyt * c_ref[b_, _rowsb(r, CPAD + a8), :])
            return carry + _fold8(dyt)

        db8 = lax.fori_loop(0, L // RTB, tile1, jnp.zeros((8, CB), f32), unroll=2)
        dbdw_ref[...] = jnp.sum(db8, axis=0, keepdims=True)
        for k in range(CONF_K):
            dw_ref[pl.ds(k, 1), :] = jnp.sum(acc_ref[k], axis=0, keepdims=True)
        base_ref[pl.ds(CPAD, L), :] = dy_ref[...]
        _shift_copies8(c_ref, base_ref, L)
        ba = ba_ref[...]
        bg = bg_ref[...]

        def tile2(i, carry):
            r = i * RTB
            dglu = jnp.zeros((RTB, CB), f32)
            for k in range(CONF_K):
                b_, a8 = _tap_ab(CHALF - k)
                dglu = dglu + c_ref[b_, _rowsb(r, CPAD + a8), :] * w_ref[pl.ds(k, 1), :]
            a = pa_ref[_rowsb(r), :] + ba
            sg = _sigmoid(pg_ref[_rowsb(r), :] + bg)
            dpa = dglu * sg
            dpg = dglu * a * (sg * (1.0 - sg))
            dpa_ref[_rowsb(r), :] = dpa.astype(bf16)
            dpg_ref[_rowsb(r), :] = dpg.astype(bf16)
            return carry[0] + _fold8(dpa), carry[1] + _fold8(dpg)

        s8 = lax.fori_loop(0, L // RTB, tile2, (jnp.zeros((8, CB), f32), jnp.zeros((8, CB), f32)), unroll=2)
        dba_ref[...] = jnp.sum(s8[0], axis=0, keepdims=True)
        dbg_ref[...] = jnp.sum(s8[1], axis=0, keepdims=True)

    cblk = pl.BlockSpec((L, CB), lambda j: (0, j))
    rblk = pl.BlockSpec((1, CB), lambda j: (0, j))
    rgblk = pl.BlockSpec((1, CB), lambda j: (0, nb + j))
    wblk = pl.BlockSpec((CONF_K, CB), lambda j: (0, j))
    return pl.pallas_call(
        body, grid=(nb,), in_specs=[cblk, cblk, rblk, rgblk, wblk, cblk],
        out_specs=[cblk, cblk, rblk, rblk, wblk, rblk],
        out_shape=[S((L, D), bf16), S((L, D), bf16), S((1, D), f32), S((1, D), f32), S((CONF_K, D), f32), S((1, D), f32)],
        scratch_shapes=[pltpu.VMEM((L + 2 * CPAD, CB), f32), pltpu.VMEM((8, L + 2 * CPAD, CB), f32),
                        pltpu.VMEM((CONF_K, 8, CB), f32)],
        compiler_params=_cparams("parallel"), name=name)(pa, pg, b1, b1, wdw, dy)


def _ln_silu_f(x, w, b):
    mu = jnp.mean(x, axis=-1, keepdims=True)
    d = x - mu
    y = d * lax.rsqrt(jnp.mean(d * d, axis=-1, keepdims=True) + EPS) * w + b
    return y * _sigmoid(y)


def ln_silu_fwd(x, w, b, name):
    T = x.shape[0]

    def body(x_ref, w_ref, b_ref, o_ref):
        o_ref[...] = _ln_silu_f(x_ref[...], w_ref[...], b_ref[...]).astype(bf16)

    blk = pl.BlockSpec((TB, D), lambda i: (i, 0))
    row = pl.BlockSpec((1, D), lambda i: (0, 0))
    return pl.pallas_call(body, grid=(T // TB,), in_specs=[blk, row, row], out_specs=blk, out_shape=S((T, D), bf16),
                          compiler_params=_cparams("parallel"), name=name)(x, w, b)


def ln_silu_bwd(x, w, b, ds, name):
    T = x.shape[0]

    def body(x_ref, w_ref, b_ref, ds_ref, dx_ref, dw_ref, db_ref):
        i = pl.program_id(0)
        _, vjp = jax.vjp(_ln_silu_f, x_ref[...], w_ref[...], b_ref[...])
        dx, dw, db = vjp(ds_ref[...].astype(f32))
        dx_ref[...] = dx

        @pl.when(i == 0)
        def _():
            dw_ref[...] = jnp.zeros_like(dw_ref)
            db_ref[...] = jnp.zeros_like(db_ref)

        dw_ref[...] += dw
        db_ref[...] += db

    blk = pl.BlockSpec((TB, D), lambda i: (i, 0))
    row = pl.BlockSpec((1, D), lambda i: (0, 0))
    return pl.pallas_call(body, grid=(T // TB,), in_specs=[blk, row, row, blk], out_specs=[blk, row, row],
                          out_shape=[S((T, D), f32), S((1, D), f32), S((1, D), f32)],
                          compiler_params=_cparams("arbitrary"), name=name)(x, w, b, ds)


def _mxu(a, b, dims):
    return lax.dot_general(a.astype(bf16), b.astype(bf16), (dims, ((), ())), preferred_element_type=f32)


def _nn(a, b):
    return _mxu(a, b, ((1,), (0,)))


def _nt(a, b):
    return _mxu(a, b, ((1,), (1,)))


def _tn(a, b):
    return _mxu(a, b, ((0,), (0,)))


@jax.custom_vjp
def _dot_nn(a, b):
    return _nn(a, b)


@jax.custom_vjp
def _dot_nt(a, b):
    return _nt(a, b)


@jax.custom_vjp
def _dot_tn(a, b):
    return _tn(a, b)


_dot_nn.defvjp(lambda a, b: (_nn(a, b), (a, b)), lambda res, g: (_nt(g, res[1]), _tn(res[0], g)))
_dot_nt.defvjp(lambda a, b: (_nt(a, b), (a, b)), lambda res, g: (_nn(g, res[1]), _tn(g, res[0])))
_dot_tn.defvjp(lambda a, b: (_tn(a, b), (a, b)), lambda res, g: (_nt(res[1], g), _nn(res[0], g)))


def _exact_dot(a, b, dims, split_first):
    v = a if split_first else b
    p1 = v.astype(bf16)
    r1 = v - p1.astype(f32)
    p2 = r1.astype(bf16)
    p3 = (r1 - p2.astype(f32)).astype(bf16)
    out = None
    for p in (p1, p2, p3):
        lhs, rhs = (p, b.astype(bf16)) if split_first else (a.astype(bf16), p)
        t = lax.dot_general(lhs, rhs, (dims, ((), ())), preferred_element_type=f32)
        out = t if out is None else out + t
    return out


@jax.custom_vjp
def _masked_sum_cols(mf, a):
    return _exact_dot(mf, a, ((1,), (0,)), False)


@jax.custom_vjp
def _masked_sum_rows(mf, a):
    return _exact_dot(a, mf, ((1,), (1,)), True)


_masked_sum_cols.defvjp(lambda mf, a: (_exact_dot(mf, a, ((1,), (0,)), False), mf),
                        lambda mf, g: (jnp.zeros_like(mf), _exact_dot(mf, g, ((0,), (0,)), False)))
_masked_sum_rows.defvjp(lambda mf, a: (_exact_dot(a, mf, ((1,), (1,)), True), mf),
                        lambda mf, g: (jnp.zeros_like(mf), _exact_dot(g, mf, ((1,), (0,)), True)))


def _masked_sum(mf, a, rows):
    return _masked_sum_rows(mf, a) if rows else _masked_sum_cols(mf, a)


def _lanes_to_rows(v):
    r = lax.broadcasted_iota(jnp.int32, (GW, GW), 0)
    c = lax.broadcasted_iota(jnp.int32, (GW, GW), 1)
    return jnp.sum(jnp.where(r == c, jnp.broadcast_to(v, (GW, GW)), 0.0), axis=1, keepdims=True)


def _ssd_chunk(x, B, C, dtc, dtr, bc, br, alc, alr, s_in, is_fwd):
    row = lax.broadcasted_iota(jnp.int32, (Q, Q), 0)
    col = lax.broadcasted_iota(jnp.int32, (Q, Q), 1)
    sgn = jnp.where(is_fwd, 1, -1).astype(jnp.int32)
    mask = (row - col) * sgn >= 0
    mf = mask.astype(f32)
    lane_head = lax.broadcasted_iota(jnp.int32, (1, GW), 1) // P

    def spread(v):
        out = jnp.zeros((v.shape[0], GW), f32)
        for r in range(HPG):
            out = jnp.where(lane_head == r, v[:, r:r + 1], out)
        return out

    dt_c = _softplus(dtc + bc)
    dt_r = _softplus(dtr + br)
    a_c = dt_c * (-jnp.exp(alc))
    a_r = dt_r * (-jnp.exp(alr))
    acum_c = _masked_sum(mf, a_c, False)
    acum_r = _masked_sum(mf, a_r, True)
    tot_c = jnp.sum(a_c, axis=0, keepdims=True)
    dt_e = spread(dt_c)
    acum_e = spread(acum_c)
    tot_e = spread(tot_c)
    xdt = x * dt_e
    cb = _dot_nt(C, B)
    scores, xs = [], []
    for r in range(HPG):
        seg = acum_c[:, r:r + 1] - acum_r[r:r + 1, :]
        scores.append(cb * jnp.exp(jnp.where(mask, seg, -jnp.inf)))
        xs.append(jnp.where(lane_head == r, xdt, 0.0))
    y = _dot_nn(jnp.concatenate(scores, axis=1), jnp.concatenate(xs, axis=0))
    y = y + _dot_nt(C, s_in) * jnp.exp(acum_e)
    xe = xdt * jnp.exp(tot_e - acum_e)
    s_out = _lanes_to_rows(jnp.exp(tot_e)) * s_in + _dot_tn(xe, B)
    return y, s_out


def _chunk_index(d, t, nctx, nc):
    bwd = jnp.where(t < nctx, nctx - 1 - t, nc - 1 - (t - nctx))
    return jnp.where(d == 0, t, bwd)


def _ssd_in_specs(ci):
    small_c = pl.BlockSpec((None, G, 1, HPG), lambda d, t: (d, 0, 0, 0))
    small_r = pl.BlockSpec((None, G, HPG, 1), lambda d, t: (d, 0, 0, 0))
    return [
        pl.BlockSpec((Q, CONVD), lambda d, t: (ci(d, t), 0)),
        pl.BlockSpec((None, G, Q, HPG), lambda d, t: (d, 0, ci(d, t), 0)),
        pl.BlockSpec((None, G, HPG, Q), lambda d, t: (d, 0, 0, ci(d, t))),
        small_c, small_r, small_c, small_r,
    ]


def _group_cols(g):
    return pl.ds(g * GW, GW), pl.ds(DI + g * N, N), pl.ds(DI + G * N + g * N, N)


def ssd_scan_fwd(xbc, dtc, dtr, bc, br, alc, alr, nctx, name, hosted=None):
    T = xbc.shape[0]
    nc = T // Q

    def body(xbc_ref, dtc_ref, dtr_ref, bc_ref, br_ref, alc_ref, alr_ref, y_ref, sin_ref, st_ref):
        d = pl.program_id(0)
        t = pl.program_id(1)

        @pl.when(t == 0)
        def _():
            st_ref[...] = jnp.zeros_like(st_ref)

        for g in range(G):
            xs, bs, cs = _group_cols(g)
            s_in = st_ref[g]
            sin_ref[g] = s_in
            y, s_out = _ssd_chunk(xbc_ref[:, xs], xbc_ref[:, bs], xbc_ref[:, cs], dtc_ref[g], dtr_ref[g], bc_ref[g], br_ref[g],
                                  alc_ref[g], alr_ref[g], s_in, d == 0)
            y_ref[:, xs] = y
            st_ref[g] = s_out

    ci = lambda d, t: _chunk_index(d, t, nctx, nc)
    out_specs = [
        pl.BlockSpec((None, Q, DI), lambda d, t: (d, ci(d, t), 0)),
        pl.BlockSpec((None, None, G, GW, N), lambda d, t: (d, ci(d, t), 0, 0, 0)),
    ]
    return _host_call(
        body, (2, nc), _ssd_in_specs(ci), out_specs, [S((2, T, DI), f32), S((2, nc, G, GW, N), f32)],
        [pltpu.VMEM((G, GW, N), f32)], ("arbitrary", "arbitrary"), name, (xbc, dtc, dtr, bc, br, alc, alr), hosted)


def ssd_scan_bwd(xbc, dtc, dtr, bc, br, alc, alr, s_in_all, dy, nctx, name, hosted=None):
    T = xbc.shape[0]
    nc = T // Q

    def body(xbc_ref, dtc_ref, dtr_ref, bc_ref, br_ref, alc_ref, alr_ref, sin_ref, dy_ref,
             dxbc_ref, ddtc_ref, ddtr_ref, dbc_ref, dbr_ref, dalc_ref, dalr_ref, ds_ref):
        d = pl.program_id(0)
        t = pl.program_id(1)

        @pl.when(t == 0)
        def _():
            ds_ref[...] = jnp.zeros_like(ds_ref)
            dbc_ref[...] = jnp.zeros_like(dbc_ref)
            dbr_ref[...] = jnp.zeros_like(dbr_ref)
            dalc_ref[...] = jnp.zeros_like(dalc_ref)
            dalr_ref[...] = jnp.zeros_like(dalr_ref)

        f = functools.partial(_ssd_chunk, is_fwd=(d == 0))
        for g in range(G):
            xs, bs, cs = _group_cols(g)
            _, vjp = jax.vjp(f, xbc_ref[:, xs], xbc_ref[:, bs], xbc_ref[:, cs], dtc_ref[g], dtr_ref[g], bc_ref[g], br_ref[g],
                             alc_ref[g], alr_ref[g], sin_ref[g])
            dx, dB, dC, ddtc, ddtr, dbc, dbr, dalc, dalr, ds = vjp((dy_ref[:, xs], ds_ref[g]))
            dxbc_ref[:, xs] = dx
            dxbc_ref[:, bs] = dB
            dxbc_ref[:, cs] = dC
            ddtc_ref[g] = ddtc
            ddtr_ref[g] = ddtr
            dbc_ref[g] += dbc
            dbr_ref[g] += dbr
            dalc_ref[g] += dalc
            dalr_ref[g] += dalr
            ds_ref[g] = ds

    ci = lambda d, t: _chunk_index(d, nc - 1 - t, nctx, nc)
    in_specs = _ssd_in_specs(ci) + [
        pl.BlockSpec((None, None, G, GW, N), lambda d, t: (d, ci(d, t), 0, 0, 0)),
        pl.BlockSpec((Q, DI), lambda d, t: (ci(d, t), 0)),
    ]
    small_c = pl.BlockSpec((None, G, 1, HPG), lambda d, t: (d, 0, 0, 0))
    small_r = pl.BlockSpec((None, G, HPG, 1), lambda d, t: (d, 0, 0, 0))
    out_specs = [
        pl.BlockSpec((None, Q, CONVD), lambda d, t: (d, ci(d, t), 0)),
        pl.BlockSpec((None, G, Q, HPG), lambda d, t: (d, 0, ci(d, t), 0)),
        pl.BlockSpec((None, G, HPG, Q), lambda d, t: (d, 0, 0, ci(d, t))),
        small_c, small_r, small_c, small_r,
    ]
    out_shape = [S((2, T, CONVD), f32), S((2, G, T, HPG), f32), S((2, G, HPG, T), f32),
                 S((2, G, 1, HPG), f32), S((2, G, HPG, 1), f32), S((2, G, 1, HPG), f32), S((2, G, HPG, 1), f32)]
    return _host_call(body, (2, nc), in_specs, out_specs, out_shape, [pltpu.VMEM((G, GW, N), f32)],
                      ("arbitrary", "arbitrary"), name, (xbc, dtc, dtr, bc, br, alc, alr, s_in_all, dy), hosted)


GTB = 128


def _gate_norm_f(yf, yb, x, z, dexp, w):
    y = (yf + yb + dexp * x) * (z * _sigmoid(z))
    return y * lax.rsqrt(jnp.mean(y * y, axis=-1, keepdims=True) + EPS) * w


def ssd_gate_fwd(y2, xbc, proj, dexp, w, nctxb, name):
    T = xbc.shape[0]
    L = T - nctxb * GTB

    def body(yf_ref, yb_ref, x_ref, z_ref, d_ref, w_ref, o_ref):
        o_ref[...] = _gate_norm_f(yf_ref[...], yb_ref[...], x_ref[...], z_ref[...], d_ref[...], w_ref[...]).astype(bf16)

    wide = pl.BlockSpec((GTB, DI), lambda i: (i + nctxb, 0))
    row = pl.BlockSpec((1, DI), lambda i: (0, 0))
    return pl.pallas_call(
        body, grid=(L // GTB,),
        in_specs=[pl.BlockSpec((None, GTB, DI), lambda i: (0, i + nctxb, 0)),
                  pl.BlockSpec((None, GTB, DI), lambda i: (1, i + nctxb, 0)), wide, wide, row, row],
        out_specs=pl.BlockSpec((GTB, DI), lambda i: (i, 0)), out_shape=S((L, DI), bf16),
        compiler_params=_cparams("parallel"), name=name)(y2, y2, xbc, proj, dexp, w)


def ssd_gate_bwd(y2, xbc, proj, dexp, w, dyn, nctxb, name, hosted=None):
    T = xbc.shape[0]
    nb = T // GTB

    def body(yf_ref, yb_ref, x_ref, z_ref, d_ref, w_ref, dyn_ref, dy_ref, dz_ref, dd_ref, dw_ref):
        i = pl.program_id(0)

        @pl.when(i == 0)
        def _():
            dd_ref[...] = jnp.zeros_like(dd_ref)
            dw_ref[...] = jnp.zeros_like(dw_ref)

        @pl.when(i < nctxb)
        def _():
            dy_ref[...] = jnp.zeros_like(dy_ref)
            dz_ref[...] = jnp.zeros_like(dz_ref)

        @pl.when(i >= nctxb)
        def _():
            _, vjp = jax.vjp(_gate_norm_f, yf_ref[...], yb_ref[...], x_ref[...], z_ref[...], d_ref[...], w_ref[...])
            dyf, _, _, dz, dd, dw = vjp(dyn_ref[...].astype(f32))
            dy_ref[...] = dyf
            dz_ref[...] = dz.astype(bf16)
            fold = (lax.broadcasted_iota(jnp.int32, (DI, 128), 0) // P == lax.broadcasted_iota(jnp.int32, (DI, 128), 1))
            dd_ref[...] += jnp.dot(dd, fold.astype(f32), precision=HI, preferred_element_type=f32)
            dw_ref[...] += dw

    wide = pl.BlockSpec((GTB, DI), lambda i: (i, 0))
    row = pl.BlockSpec((1, DI), lambda i: (0, 0))
    hrow = pl.BlockSpec((1, 128), lambda i: (0, 0))
    return _host_call(
        body, (nb,),
        [pl.BlockSpec((None, GTB, DI), lambda i: (0, i, 0)), pl.BlockSpec((None, GTB, DI), lambda i: (1, i, 0)),
         wide, wide, row, row, pl.BlockSpec((GTB, DI), lambda i: (jnp.maximum(i - nctxb, 0), 0))],
        [wide, wide, hrow, row],
        [S((T, DI), f32), S((T, DI), bf16), S((1, 128), f32), S((1, DI), f32)],
        [], ("arbitrary",), name, (y2, y2, xbc, proj, dexp, w, dyn), hosted)


CROWS = 2 * N_DEV


def mod_fwd(c16, modw, name):
    nl, _, cols = modw.shape

    def body(c_ref, w_ref, o_ref):
        cv = c_ref[...]
        s = cv * _sigmoid(cv)
        for l in range(nl):
            o_ref[l] = jnp.dot(s, w_ref[l], precision=HI, preferred_element_type=f32)

    return pl.pallas_call(body, in_specs=[VMEM, VMEM], out_specs=VMEM, out_shape=S((nl, CROWS, cols), f32),
                          compiler_params=pltpu.CompilerParams(vmem_limit_bytes=VMEM_LIMIT_BYTES), name=name)(c16, modw)


def mod_bwd(c16, modw, dm_sh, dm_all, name):
    nl, _, cols = modw.shape

    def body(c_ref, w_ref, dm_ref, dmall_ref, dw_ref, dc_ref, db_ref):
        cv = c_ref[...]
        sg = _sigmoid(cv)
        s = cv * sg
        ds_dc = sg * (1.0 + cv * (1.0 - sg))
        is_ctx = lax.broadcasted_iota(jnp.int32, (CROWS, D), 0) >= N_DEV
        dc = jnp.zeros((1, D), f32)
        for l in range(nl):
            dm = dm_ref[l]
            dw_ref[l] = lax.dot_general(s, dm, (((0,), (0,)), ((), ())), precision=HI, preferred_element_type=f32)
            dsv = lax.dot_general(dm, w_ref[l], (((1,), (1,)), ((), ())), precision=HI, preferred_element_type=f32)
            dc = dc + jnp.sum(jnp.where(is_ctx, dsv * ds_dc, 0.0), axis=0, keepdims=True)
            db_ref[pl.ds(l, 1), :] = jnp.sum(dmall_ref[l], axis=0, keepdims=True)
        dc_ref[...] = dc

    return pl.pallas_call(
        body, in_specs=[VMEM, VMEM, VMEM, VMEM], out_specs=[VMEM, VMEM, VMEM],
        out_shape=[S(modw.shape, f32), S((1, D), f32), S((nl, 6 * D), f32)],
        compiler_params=pltpu.CompilerParams(vmem_limit_bytes=VMEM_LIMIT_BYTES), name=name)(c16, modw, dm_sh, dm_all)


def adamw(w, g, m, v, name):
    R, C = w.shape
    rb = R if R <= 512 else max(r_ for r_ in range(8, 513, 8) if R % r_ == 0)
    bc1 = 1.0 - ADAM_B1 ** ADAM_STEP
    bc2 = 1.0 - ADAM_B2 ** ADAM_STEP

    def body(w_ref, g_ref, m_ref, v_ref, d_ref, nm_ref, nv_ref):
        gv = g_ref[...]
        m_new = ADAM_B1 * m_ref[...] + (1.0 - ADAM_B1) * gv
        v_new = ADAM_B2 * v_ref[...] + (1.0 - ADAM_B2) * (gv * gv)
        m_hat = m_new / bc1
        v_hat = v_new / bc2
        d_ref[...] = -ADAM_LR * (m_hat / (jnp.sqrt(v_hat) + ADAM_EPS) + ADAM_WD * w_ref[...])
        nm_ref[...] = m_new
        nv_ref[...] = v_new

    blk = pl.BlockSpec((rb, C), lambda i: (i, 0))
    return pl.pallas_call(body, grid=(R // rb,), in_specs=[blk] * 4, out_specs=[blk] * 3,
                          out_shape=[S((R, C), f32)] * 3, compiler_params=_cparams("parallel"), name=name)(w, g, m, v)


def _me():
    return lax.axis_index("x"), lax.axis_index("y"), lax.axis_index("c")


def allgather_small(x, name, with_sum=False):
    r, w = x.shape

    def body(x_ref, *refs):
        if with_sum:
            out_ref, sum_ref, send_sems, recv_sems = refs
        else:
            out_ref, send_sems, recv_sems = refs
        mx, my, mc = _me()
        me = 4 * mx + 2 * my + mc
        out_ref[me] = x_ref[...]
        peers = []
        for k in range(1, N_DEV):
            kx, ky, kc = (k >> 2) & 1, (k >> 1) & 1, k & 1
            peers.append((mx + kx - 2 * mx * kx, my + ky - 2 * my * ky, mc + kc - 2 * mc * kc))
        copies = []
        for k, peer in enumerate(peers):
            cp = pltpu.make_async_remote_copy(src_ref=x_ref, dst_ref=out_ref.at[me], send_sem=send_sems.at[k],
                                              recv_sem=recv_sems.at[k], device_id=peer, device_id_type=MESH)
            cp.start()
            copies.append(cp)
        for k, (px, py, pc) in enumerate(peers):
            pltpu.make_async_remote_copy(src_ref=x_ref, dst_ref=out_ref.at[4 * px + 2 * py + pc], send_sem=send_sems.at[k],
                                         recv_sem=recv_sems.at[k], device_id=(px, py, pc), device_id_type=MESH).wait_recv()
        for cp in copies:
            cp.wait_send()
        if with_sum:
            acc = out_ref[0]
            for j in range(1, N_DEV):
                acc = acc + out_ref[j]
            sum_ref[...] = acc

    out_shape = [S((N_DEV, r, w), f32)] + ([S((r, w), f32)] if with_sum else [])
    outs = pl.pallas_call(
        body, in_specs=[VMEM], out_specs=[VMEM] * len(out_shape), out_shape=out_shape,
        scratch_shapes=[pltpu.SemaphoreType.DMA((N_DEV - 1,)), pltpu.SemaphoreType.DMA((N_DEV - 1,))],
        compiler_params=pltpu.CompilerParams(vmem_limit_bytes=VMEM_LIMIT_BYTES), name=name)(x)
    return outs if with_sum else outs[0]


def _tile2d(R, W, max_rows):
    if R <= max_rows:
        return R, W
    fits = [r_ for r_ in range(16, max_rows + 1, 16) if R % r_ == 0]
    return (max(fits), W) if fits else (R, 256)


def add_own(g, r, core, name, twice=False):
    _, _, R, W = g.shape
    rb, wb = _tile2d(R, W, 512)
    nout = 2 if twice else 1

    def body(core_ref, a_ref, b_ref, *o_refs):
        s = (a_ref[...].astype(f32) + b_ref[...].astype(f32)).astype(bf16)
        for o_ref in o_refs:
            o_ref[...] = s

    blk = pl.BlockSpec((None, rb, wb), lambda k, i, j, core_ref: (k, i, j))
    gs = pltpu.PrefetchScalarGridSpec(
        num_scalar_prefetch=1, grid=(4, R // rb, W // wb),
        in_specs=[pl.BlockSpec((None, None, rb, wb), lambda k, i, j, core_ref: (k, core_ref[0], i, j)), blk],
        out_specs=[blk] * nout)
    outs = pl.pallas_call(body, grid_spec=gs, out_shape=[S((4, R, W), bf16)] * nout,
                          compiler_params=_cparams("parallel", "parallel", "parallel"), name=name)(core, g, r)
    return tuple(outs) if twice else outs[0]


HBM_SPEC = pl.BlockSpec(memory_space=pltpu.HBM)
SEM_SPEC = pl.BlockSpec(memory_space=pltpu.SEMAPHORE)


def _chips_copy(p_ref, land_ref, send_sems, recv_sems, a, j):
    x, y, c = _me()
    px, py = [(1 - x, y), (x, 1 - y), (1 - x, 1 - y)][j]
    return pltpu.make_async_remote_copy(src_ref=p_ref.at[2 * px + py], dst_ref=land_ref.at[2 * x + y],
                                        send_sem=send_sems.at[3 * a + j], recv_sem=recv_sems.at[3 * a + j],
                                        device_id=(px, py, c), device_id_type=MESH)


def _chips_wait_copy(p_ref, land_ref, send_sems, recv_sems, a, j):
    x, y, c = _me()
    px, py = [(1 - x, y), (x, 1 - y), (1 - x, 1 - y)][j]
    return pltpu.make_async_remote_copy(src_ref=p_ref.at[2 * px + py], dst_ref=land_ref.at[2 * px + py],
                                        send_sem=send_sems.at[3 * a + j], recv_sem=recv_sems.at[3 * a + j],
                                        device_id=(px, py, c), device_id_type=MESH)


def chips_start(parts, lands, name):
    na = len(parts)

    def body(*refs):
        p_refs, land_refs = refs[:na], refs[na:2 * na]
        send_sems, recv_sems = refs[2 * na], refs[2 * na + 1]
        token = refs[-1]
        for a in range(na):
            for j in range(3):
                _chips_copy(p_refs[a], land_refs[a], send_sems, recv_sems, a, j).start()
        token[...] = jnp.zeros_like(token)

    arrs = list(parts) + list(lands)
    outs = pl.pallas_call(
        body, name=name, in_specs=[HBM_SPEC] * (2 * na),
        out_shape=[DMA((3 * na,)), DMA((3 * na,))] + [pltpu.HBM(t.shape, t.dtype) for t in arrs] + [S((8, 128), f32)],
        out_specs=[SEM_SPEC, SEM_SPEC] + [HBM_SPEC] * (2 * na) + [VMEM],
        input_output_aliases={k: 2 + k for k in range(2 * na)},
        compiler_params=pltpu.CompilerParams(has_side_effects=pltpu.SideEffectType.DATAFLOW_SIDE_EFFECTING),
    )(*[pltpu.with_memory_space_constraint(t, pltpu.HBM) for t in arrs])
    return outs[0], outs[1], list(outs[2:2 + na]), list(outs[2 + na:2 + 2 * na]), outs[-1]


def chips_wait(send_sems, recv_sems, parts, lands, after, name):
    na = len(parts)

    def body(*refs):
        p_refs, land_refs = refs[:na], refs[na:2 * na]
        ssem, rsem = refs[2 * na], refs[2 * na + 1]
        for a in range(na):
            for j in range(3):
                cp = _chips_wait_copy(p_refs[a], land_refs[a], ssem, rsem, a, j)
                cp.wait_send()
                cp.wait_recv()

    arrs = list(parts) + list(lands)
    outs = pl.pallas_call(
        body, name=name, in_specs=[HBM_SPEC] * (2 * na) + [SEM_SPEC, SEM_SPEC, ANY],
        out_shape=[pltpu.HBM(t.shape, t.dtype) for t in arrs], out_specs=[HBM_SPEC] * (2 * na),
        input_output_aliases={k: k for k in range(2 * na)},
        compiler_params=pltpu.CompilerParams(has_side_effects=pltpu.SideEffectType.DATAFLOW_SIDE_EFFECTING),
    )(*arrs, send_sems, recv_sems, after)
    return list(outs[na:])


def sum_adamw(recv, w, m, v, layer, name, into=None, after=None):
    _, R, W = recv.shape
    rb, wb = _tile2d(R, W, 256)
    bc1 = 1.0 - ADAM_B1 ** ADAM_STEP
    bc2 = 1.0 - ADAM_B2 ** ADAM_STEP
    n_into = 0 if into is None else 4
    extra = () if after is None else (after,)

    def body(r_ref, w_ref, m_ref, v_ref, *refs):
        g_ref, d_ref, nm_ref, nv_ref = refs[n_into + len(extra):]
        gv = r_ref[0].astype(f32)
        for k in range(1, 4):
            gv = gv + r_ref[k].astype(f32)
        m_new = ADAM_B1 * m_ref[...] + (1.0 - ADAM_B1) * gv
        v_new = ADAM_B2 * v_ref[...] + (1.0 - ADAM_B2) * (gv * gv)
        g_ref[...] = gv
        d_ref[...] = -ADAM_LR * ((m_new / bc1) / (jnp.sqrt(v_new / bc2) + ADAM_EPS) + ADAM_WD * w_ref[...])
        nm_ref[...] = m_new
        nv_ref[...] = v_new

    if layer is None:
        wblk = pl.BlockSpec((rb, wb), lambda i, j: (i, j))
        oshape = S((R, W), f32)
    else:
        wblk = pl.BlockSpec((None, rb, wb), lambda i, j: (layer, i, j))
        oshape = S(w.shape, f32)
    return pl.pallas_call(
        body, grid=(R // rb, W // wb),
        in_specs=[pl.BlockSpec((4, rb, wb), lambda i, j: (0, i, j)), wblk, wblk, wblk] + [ANY] * (n_into + len(extra)),
        out_specs=[wblk] * 4, out_shape=[oshape] * 4, input_output_aliases={4 + k: k for k in range(n_into)},
        compiler_params=_cparams("parallel", "parallel"), name=name)(recv, w, m, v, *(into or ()), *extra)


def sum_rows(a, name):
    K, R, W = a.shape
    rb = _pick(R, (512, 256, 128, 64, 32, 16))

    def body(a_ref, o_ref):
        acc = a_ref[0].astype(f32)
        for k in range(1, K):
            acc = acc + a_ref[k].astype(f32)
        o_ref[...] = acc

    return pl.pallas_call(body, grid=(R // rb,), in_specs=[pl.BlockSpec((K, rb, W), lambda i: (0, i, 0))],
                          out_specs=pl.BlockSpec((rb, W), lambda i: (i, 0)), out_shape=S((R, W), f32),
                          compiler_params=_cparams("parallel"), name=name)(a)


DMA = pltpu.SemaphoreType.DMA


class GatherExchange:
    def __init__(self, arrays):
        self.arrays = list(arrays)
        self.na = len(self.arrays)
        self.out_shape = [S((N_DEV,) + a.shape, a.dtype) for a in self.arrays]
        self.scratch = [DMA((7 * self.na,)), DMA((7 * self.na,)), DMA((self.na,))]

    def ops(self, x_refs, out_refs, sems):
        send_sems, recv_sems, local_sems = sems
        na = self.na
        x, y, c = _me()
        me, sibling = (x, y, c), (x, y, 1 - c)
        chips = [(1 - x, y), (x, 1 - y), (1 - x, 1 - y)]

        def rows(a, px, py, pc):
            return out_refs[a].at[4 * px + 2 * py + pc]

        def copy(a, k, block, to, src=None):
            return pltpu.make_async_remote_copy(
                src_ref=rows(a, *block) if src is None else src, dst_ref=rows(a, *block),
                send_sem=send_sems.at[7 * a + k], recv_sem=recv_sems.at[7 * a + k], device_id=to, device_id_type=MESH)

        def local(a):
            return pltpu.make_async_copy(x_refs[a], rows(a, *me), local_sems.at[a])

        def first(a):
            return [copy(a, 0, me, sibling, src=x_refs[a])] + [copy(a, 1 + j, me, (*chip, c), src=x_refs[a])
                                                                for j, chip in enumerate(chips)]

        def start():
            for a in range(na):
                local(a).start()
                for cp in first(a):
                    cp.start()

        def mid():
            for a in range(na):
                for j, chip in enumerate(chips):
                    copy(a, 1 + j, (*chip, c), me).wait_recv()
                    copy(a, 4 + j, (*chip, c), sibling).start()

        def finish():
            for a in range(na):
                copy(a, 0, sibling, me).wait_recv()
                for j, chip in enumerate(chips):
                    copy(a, 4 + j, (*chip, 1 - c), me).wait_recv()
                for cp in first(a) + [copy(a, 4 + j, (*chip, c), sibling) for j, chip in enumerate(chips)]:
                    cp.wait_send()
                local(a).wait()

        return start, mid, finish


class SiblingExchange:
    def __init__(self, arrays):
        self.arrays = list(arrays)
        self.na = len(self.arrays)
        self.out_shape = [S((4,) + g.shape[2:], g.dtype) for g in self.arrays]
        self.scratch = [DMA((self.na,)), DMA((self.na,))]

    def ops(self, g_refs, out_refs, sems):
        send_sems, recv_sems = sems
        x, y, c = _me()

        def copy(a):
            return pltpu.make_async_remote_copy(src_ref=g_refs[a].at[:, 1 - c], dst_ref=out_refs[a],
                                                send_sem=send_sems.at[a], recv_sem=recv_sems.at[a],
                                                device_id=(x, y, 1 - c), device_id_type=MESH)

        def start():
            for a in range(self.na):
                copy(a).start()

        def finish():
            for a in range(self.na):
                copy(a).wait()

        return start, None, finish


class ChipsExchange:
    def __init__(self, arrays):
        self.arrays = list(arrays)
        self.na = len(self.arrays)
        self.out_shape = [S(p.shape, p.dtype) for p in self.arrays]
        self.scratch = [DMA((3 * self.na,)), DMA((3 * self.na,)), DMA((self.na,))]

    def ops(self, p_refs, out_refs, sems):
        send_sems, recv_sems, local_sems = sems
        x, y, c = _me()
        mine = 2 * x + y
        chips = [(1 - x, y), (x, 1 - y), (1 - x, 1 - y)]

        def local(a):
            return pltpu.make_async_copy(p_refs[a].at[mine], out_refs[a].at[mine], local_sems.at[a])

        def send(a, j):
            px, py = chips[j]
            return pltpu.make_async_remote_copy(src_ref=p_refs[a].at[2 * px + py], dst_ref=out_refs[a].at[mine],
                                                send_sem=send_sems.at[3 * a + j], recv_sem=recv_sems.at[3 * a + j],
                                                device_id=(px, py, c), device_id_type=MESH)

        def recv(a, j):
            px, py = chips[j]
            return pltpu.make_async_remote_copy(src_ref=p_refs[a].at[mine], dst_ref=out_refs[a].at[2 * px + py],
                                                send_sem=send_sems.at[3 * a + j], recv_sem=recv_sems.at[3 * a + j],
                                                device_id=(px, py, c), device_id_type=MESH)

        def start():
            for a in range(self.na):
                local(a).start()
                for j in range(3):
                    send(a, j).start()

        def finish():
            for a in range(self.na):
                for j in range(3):
                    recv(a, j).wait_recv()
                for j in range(3):
                    send(a, j).wait_send()
                local(a).wait()

        return start, None, finish


def exchange(ex, name):
    na = ex.na

    def body(*refs):
        start, mid, finish = ex.ops(refs[:na], refs[na:2 * na], refs[2 * na:])
        start()
        if mid is not None:
            mid()
        finish()

    return pl.pallas_call(body, in_specs=[ANY] * na, out_specs=[ANY] * na, out_shape=ex.out_shape,
                          scratch_shapes=ex.scratch, name=name)(*ex.arrays)


def _host_call(body, grid, in_specs, out_specs, out_shape, scratch_shapes, sem, name, args, hosted):
    if hosted is None:
        res = pl.pallas_call(body, grid=grid, in_specs=in_specs, out_specs=out_specs, out_shape=out_shape,
                             scratch_shapes=scratch_shapes, compiler_params=_cparams(*sem), name=name)(*args)
        return res, None
    n_in, n_out, n_sc, na = len(in_specs), len(out_shape), len(scratch_shapes), hosted.na
    nsteps = 1
    for g_ in grid:
        nsteps *= g_
    mid_step = (3 * nsteps) // 4
    i1 = n_in + na
    i2 = i1 + n_out
    i3 = i2 + na
    i4 = i3 + n_sc

    def wrapped(*refs):
        step = pl.program_id(0)
        for ax in range(1, len(grid)):
            step = step * grid[ax] + pl.program_id(ax)
        start, mid, finish = hosted.ops(refs[n_in:i1], refs[i2:i3], refs[i4:])
        pl.when(step == 0)(start)
        if mid is not None:
            pl.when(step == mid_step)(mid)
        body(*refs[:n_in], *refs[i1:i2], *refs[i3:i4])
        pl.when(step == nsteps - 1)(finish)

    res = pl.pallas_call(
        wrapped, grid=grid, in_specs=list(in_specs) + [ANY] * na, out_specs=list(out_specs) + [ANY] * na,
        out_shape=list(out_shape) + hosted.out_shape, scratch_shapes=list(scratch_shapes) + hosted.scratch,
        compiler_params=_cparams(*(("arbitrary",) * len(grid))), name=name)(*args, *hosted.arrays)
    return res[:n_out], res[n_out:]


PACK_ALIGN = 16 * PACK_W


def _pad_to(v, mult):
    n = v.shape[-1]
    extra = (-n) % mult
    if extra == 0:
        return v
    return jnp.concatenate([v, jnp.zeros(v.shape[:-1] + (extra,), v.dtype)], axis=-1)


def _f32_as_bf16_pairs(v):
    return lax.bitcast_convert_type(v.reshape(-1), bf16).reshape(-1)


def _bf16_pairs_as_f32(v):
    return lax.bitcast_convert_type(v.reshape(v.shape[:-1] + (v.shape[-1] // 2, 2)), f32)


def _col_shards(gw):
    lead = gw.shape[:-1]
    n = gw.shape[-1] // N_DEV
    t = gw.reshape(lead + (N_DEV, n))
    t = jnp.moveaxis(t, -2, 0)
    return t.reshape(N_DEV, -1)


def kernel(x, c, ctx, c_ctx, mod_w, mod_b, norm1_w, norm2_w, ssd_w_in, ssd_conv_w, ssd_conv_b, ssd_dt_bias, ssd_a_log, ssd_d, ssd_norm_w, ssd_w_out, conf_w_pw1, conf_b_pw1, conf_w_dw, conf_b_dw, conf_ln_w, conf_ln_b, conf_w_pw2, conf_b_pw2, ffn_w_up, ffn_conv_w, ffn_conv_b, ffn_w_down, final_norm_w, loss_target, m_c_ctx, m_mod_w, m_mod_b, m_norm1_w, m_norm2_w, m_ssd_w_in, m_ssd_conv_w, m_ssd_conv_b, m_ssd_dt_bias, m_ssd_a_log, m_ssd_d, m_ssd_norm_w, m_ssd_w_out, m_conf_w_pw1, m_conf_b_pw1, m_conf_w_dw, m_conf_b_dw, m_conf_ln_w, m_conf_ln_b, m_conf_w_pw2, m_conf_b_pw2, m_ffn_w_up, m_ffn_conv_w, m_ffn_conv_b, m_ffn_w_down, m_final_norm_w, v_c_ctx, v_mod_w, v_mod_b, v_norm1_w, v_norm2_w, v_ssd_w_in, v_ssd_conv_w, v_ssd_conv_b, v_ssd_dt_bias, v_ssd_a_log, v_ssd_d, v_ssd_norm_w, v_ssd_w_out, v_conf_w_pw1, v_conf_b_pw1, v_conf_w_dw, v_conf_b_dw, v_conf_ln_w, v_conf_ln_b, v_conf_w_pw2, v_conf_b_pw2, v_ffn_w_up, v_ffn_conv_w, v_ffn_conv_b, v_ffn_w_down, v_final_norm_w):
    mx, my, mc = _me()
    me = 4 * mx + 2 * my + mc
    L = x.shape[1]
    LC = ctx.shape[1]
    T = LC + L
    w_in_cols = ssd_w_in.shape[2] * N_DEV
    n_dt = w_in_cols - DI - CONVD

    small = [c[0], ssd_conv_w[0], conf_b_pw1[0], conf_w_dw[0], conf_b_dw[0], conf_ln_w[0], conf_ln_b[0], conf_b_pw2[0],
             ffn_conv_w]
    parts = [_f32_as_bf16_pairs(t) for t in small]
    sizes = [p.shape[0] for p in parts]
    small_flat = _pad_to(jnp.concatenate(parts), PACK_ALIGN).reshape(-1, PACK_W)
    w_in, small_g = exchange(GatherExchange([ssd_w_in[0].astype(bf16), small_flat]), "gather_first")
    gather_in_proj = GatherExchange([ssd_w_out[0].astype(bf16), conf_w_pw2[0].astype(bf16)])
    gather_in_conv = GatherExchange([ffn_w_down[0].astype(bf16), conf_w_pw1[0].astype(bf16)])
    gather_in_scan = GatherExchange([ffn_w_up[0].astype(bf16), ffn_w_up[1].astype(bf16)])
    gather_in_gate = GatherExchange([ffn_w_down[1].astype(bf16)])
    w_up, w_down = [None, None], [None, None]
    small_g = small_g.reshape(N_DEV, -1)
    offs = [0]
    for s_ in sizes:
        offs.append(offs[-1] + s_)
    sm = [_bf16_pairs_as_f32(small_g[:, offs[i]:offs[i + 1]]) for i in range(len(sizes))]

    def cols(pc, K):
        return jnp.moveaxis(pc.reshape(N_DEV, K, -1), 0, 1).reshape(K, -1)

    c_all = sm[0]
    conv_w5 = cols(sm[1], 5)
    b_pw1 = sm[2].reshape(1, 2 * D)
    w_dw = cols(sm[3], CONF_K)
    b_dw, ln_w, ln_b, b_pw2 = (sm[i].reshape(1, D) for i in (4, 5, 6, 7))
    fcw = sm[8].reshape(N_DEV, 2, 9, FH // N_DEV)
    ffn_cw = [cols(fcw[:, i].reshape(N_DEV, -1), 9) for i in range(2)]
    in_segs = (DI, CONVD, n_dt)
    up_segs = (FH, FH)
    pw1_segs = (D, D)

    c16 = jnp.concatenate([c_all, jnp.broadcast_to(c_ctx[None, :], (N_DEV, D))], axis=0)
    m_sh = mod_fwd(c16, mod_w, "mod_fwd")
    mod_cols = mod_w.shape[2]
    m_all = allgather_small(m_sh.reshape(2 * CROWS, mod_cols), "gather_mod")
    m_all = jnp.moveaxis(m_all.reshape(N_DEV, 2, CROWS, mod_cols), 0, 2).reshape(2, CROWS, 6 * D) + mod_b[:, None, :]
    m_lat = lax.dynamic_index_in_dim(m_all, me, axis=1, keepdims=False).reshape(2, 6, 1, D)
    m_ctx = m_all[:, N_DEV].reshape(2, 6, 1, D)
    zero_row = jnp.zeros((1, D), f32)

    def ffn_fwd(h, i, tag, hosted=None):
        a2 = modnorm_fwd(h, norm2_w[i][None], m_lat[i, 4][None], m_lat[i, 3][None], 0, f"ffn{tag}_norm")
        val, gate = smm_fwd(a2, w_up[i], None, up_segs, f"ffn{tag}_up")
        act, gs_, gvds, extra = ffn_gate_fwd(val, gate, ffn_cw[i], ffn_conv_b[i][None], f"ffn{tag}_gate", hosted)
        o2 = matmul(act, w_down[i], "nn", f32, f"ffn{tag}_down")
        h_new = resgate_fwd(h, o2, m_lat[i, 5], zero_row, f"ffn{tag}_res")
        return h_new, (a2, gate, gs_, gvds, act, o2), extra

    def ffn_bwd(dh, h, i, saved, tag):
        a2, gate, gs_, gvds, act, o2 = saved
        do2, dg2, _ = resgate_bwd(dh, o2, m_lat[i, 5], zero_row, f"ffn{tag}_res_bwd")
        g_down = matmul(act, do2, "tn", bf16, f"ffn{tag}_down_dw")
        dact = matmul(do2, w_down[i], "nt", bf16, f"ffn{tag}_down_dx")
        dval, dgate, dcw, dcb = ffn_gate_bwd(gate, gs_, gvds, ffn_cw[i], dact, f"ffn{tag}_gate_bwd")
        g_up = smm_dw(a2, [dval, dgate], FH // 4, up_segs, 2, True, f"ffn{tag}_up_dw")
        da2, _ = smm_dx([dval, dgate], w_up[i], None, up_segs, bf16, f"ffn{tag}_up_dx")
        dh_in, dn2, dsc2, dsh2 = modnorm_bwd(h, norm2_w[i][None], m_lat[i, 4][None], m_lat[i, 3][None], da2, dh, 0,
                                             f"ffn{tag}_norm_bwd")
        return dh_in, dict(w_up=g_up, w_down=g_down, conv_w=dcw, conv_b=dcb, norm2=dn2, sh2=dsh2[0], sc2=dsc2[0], g2=dg2)

    nctx = LC // Q
    hx = x[0]
    sc0 = jnp.stack([m_ctx[0, 1], m_lat[0, 1]])
    sh0 = jnp.stack([m_ctx[0, 0], m_lat[0, 0]])
    a0 = modnorm_fwd(hx, norm1_w[0][None], sc0, sh0, LC // TB, "ssd_norm", ctx=ctx[0])
    (z, xbc_pre, dt_raw), (w_out_g, w_pw2_g) = smm_fwd(a0, w_in, None, in_segs, "ssd_in", gather_in_proj)
    w_out = w_out_g.reshape(DI, D)
    w_pw2 = w_pw2_g.reshape(D, D)
    segs = ((0, LC), (LC, L))
    xbc, xbc_dsilu, (w_down0_g, w_pw1) = ssd_conv_fwd(xbc_pre, conv_w5, ssd_conv_b, segs, "ssd_conv", gather_in_conv)
    w_down[0] = w_down0_g.reshape(FH, D)
    dt4 = dt_raw[:, :n_dt].reshape(T, 2, G, HPG)
    dtc = jnp.transpose(dt4, (1, 2, 0, 3))
    dtr = jnp.transpose(dt4, (1, 2, 3, 0))
    bias3 = ssd_dt_bias[0].reshape(2, G, HPG)
    alog3 = ssd_a_log[0].reshape(2, G, HPG)
    bc_, br_ = bias3[:, :, None, :], bias3[:, :, :, None]
    alc, alr = alog3[:, :, None, :], alog3[:, :, :, None]
    (y2, s_in_all), (w_up[0], w_up[1]) = ssd_scan_fwd(xbc, dtc, dtr, bc_, br_, alc, alr, nctx, "ssd_scan", gather_in_scan)
    dexp = jnp.repeat(ssd_d[0], P)[None, :]
    yn = ssd_gate_fwd(y2, xbc, z, dexp, ssd_norm_w, LC // GTB, "ssd_gate")
    o_ssd = matmul(yn, w_out, "nn", f32, "ssd_out")
    h1 = resgate_fwd(hx, o_ssd, m_lat[0, 2], zero_row, "ssd_res")
    h2, ffn0_saved, (w_down1_g,) = ffn_fwd(h1, 0, "0", gather_in_gate)
    w_down[1] = w_down1_g.reshape(FH, D)

    a1 = modnorm_fwd(h2, norm1_w[1][None], m_lat[1, 1][None], m_lat[1, 0][None], 0, "conf_norm")
    pa, pg = smm_fwd(a1, w_pw1, None, pw1_segs, "conf_pw1")
    dwc, _ = conf_glu_conv_fwd(pa, pg, b_pw1, w_dw, b_dw, "conf_conv")
    s1 = ln_silu_fwd(dwc, ln_w, ln_b, "conf_ln")
    o_conf = matmul(s1, w_pw2, "nn", f32, "conf_pw2")
    h3 = resgate_fwd(h2, o_conf, m_lat[1, 2], b_pw2, "conf_res")
    h4, ffn1_saved, _ = ffn_fwd(h3, 1, "1")

    loss_part, dh4, g_final = final_loss(h4, final_norm_w[None], loss_target[0], "loss_head")
    dh3, gf1 = ffn_bwd(dh4, h3, 1, ffn1_saved, "1")

    do_conf, dg1_1, g_b_pw2 = resgate_bwd(dh3, o_conf, m_lat[1, 2], b_pw2, "conf_res_bwd")
    g_pw2 = matmul(s1, do_conf, "tn", bf16, "conf_pw2_dw")
    ds1 = matmul(do_conf, w_pw2, "nt", bf16, "conf_pw2_dx")
    ddwc, g_ln_w, g_ln_b = ln_silu_bwd(dwc, ln_w, ln_b, ds1, "conf_ln_bwd")
    dpa, dpg, dba, dbg, g_w_dw, g_b_dw = conf_glu_conv_bwd(pa, pg, b_pw1, w_dw, ddwc, "conf_conv_bwd")
    g_b_pw1 = jnp.concatenate([dba, dbg], axis=1)
    g_pw1 = smm_dw(a1, [dpa, dpg], 2 * D // N_DEV, pw1_segs, 1, False, "conf_pw1_dw")
    da1, _ = smm_dx([dpa, dpg], w_pw1, None, pw1_segs, bf16, "conf_pw1_dx")
    dh2, g_n1_1, dsc1_1, dsh1_1 = modnorm_bwd(h2, norm1_w[1][None], m_lat[1, 1][None], m_lat[1, 0][None], da1, dh3, 0,
                                              "conf_norm_bwd")
    dh1, gf0 = ffn_bwd(dh2, h1, 0, ffn0_saved, "0")

    do_ssd, dg1_0, _ = resgate_bwd(dh1, o_ssd, m_lat[0, 2], zero_row, "ssd_res_bwd")
    g_w_out = matmul(yn, do_ssd, "tn", bf16, "ssd_out_dw")
    dyn = matmul(do_ssd, w_out, "nt", bf16, "ssd_out_dx")
    core = mc.reshape(1).astype(jnp.int32)

    def by_device(t):
        return t.reshape((4, 2, -1, t.shape[-1]))

    early = [by_device(t) for t in (gf1["w_up"], gf1["w_down"], g_pw2, g_pw1, gf0["w_up"], gf0["w_down"], g_w_out)]
    (dy, dz, g_dexp, g_ssd_norm), early_sib = ssd_gate_bwd(
        y2, xbc, z, dexp, ssd_norm_w, dyn, LC // GTB, "ssd_gate_bwd", SiblingExchange(early))
    early_part = [add_own(t, r_, core, f"reduce_add{i}") for i, (t, r_) in enumerate(zip(early, early_sib))]
    (dxbc2, ddtc, ddtr, dbc, dbr, dalc, dalr), early_red = ssd_scan_bwd(
        xbc, dtc, dtr, bc_, br_, alc, alr, s_in_all, dy, nctx, "ssd_scan_bwd", ChipsExchange(early_part))
    ddt = (jnp.transpose(ddtc, (2, 0, 1, 3)) + jnp.transpose(ddtr, (3, 0, 1, 2))).reshape(T, n_dt)
    g_dt_bias = (dbc[:, :, 0, :] + dbr[:, :, :, 0]).reshape(2, NH_SSD)
    g_a_log = (dalc[:, :, 0, :] + dalr[:, :, :, 0]).reshape(2, NH_SSD)
    g_ssd_d = g_dexp[0, :NH_SSD]
    du, g_conv_w5, g_conv_b5 = ssd_conv_bwd(xbc_pre, conv_w5, xbc_dsilu, dxbc2, dy, dexp, segs, "ssd_conv_bwd")
    ddt_p = _pad_to(ddt, 128).astype(bf16)
    g_w_in = smm_dw(a0, [dz, du, ddt_p], w_in.shape[-1], in_segs, 2, True, "ssd_in_dw")
    g_ffn_cw = jnp.stack([gf0["conv_w"], gf1["conv_w"]])
    small_shards = [_col_shards(t) for t in (g_conv_w5, g_b_pw1, g_w_dw, g_b_dw, g_ln_w, g_ln_b, g_b_pw2, g_ffn_cw)]
    gsizes = [s_.shape[1] for s_ in small_shards]
    g_small = _pad_to(jnp.concatenate(small_shards, axis=1), PACK_ALIGN).astype(bf16)
    late = [by_device(g_w_in), by_device(g_small.reshape(N_DEV, -1, PACK_W))]
    da0, late_sib = smm_dx([dz, du, ddt_p], w_in, None, in_segs, f32, "ssd_in_dx", SiblingExchange(late))
    late_part = [add_own(t, r_, core, f"reduce_add_late{i}", twice=True) for i, (t, r_) in enumerate(zip(late, late_sib))]
    late_flying = chips_start([p_[0] for p_ in late_part], [p_[1] for p_ in late_part], "reduce_chips_late_start")
    dh0, g_n1_0, dsc1_0, dsh1_0 = modnorm_bwd(hx, norm1_w[0][None], sc0, sh0, da0, dh1, LC // TB, "ssd_norm_bwd",
                                              ctx=ctx[0])
    grad_x = dh0[None]

    zeros_d = jnp.zeros((1, D), f32)
    dm_lat = jnp.stack([
        jnp.concatenate([dsh1_0[1], dsc1_0[1], dg1_0, gf0["sh2"], gf0["sc2"], gf0["g2"]], axis=1),
        jnp.concatenate([dsh1_1[0], dsc1_1[0], dg1_1, gf1["sh2"], gf1["sc2"], gf1["g2"]], axis=1)])
    dm_ctx = jnp.stack([
        jnp.concatenate([dsh1_0[0], dsc1_0[0]] + [zeros_d] * 4, axis=1), jnp.zeros((1, 6 * D), f32)])
    dm_mine = jnp.concatenate([dm_lat.reshape(2, 6 * D), dm_ctx.reshape(2, 6 * D),
                               jnp.zeros((4, 6 * D), f32)], axis=0)
    dm_g = allgather_small(dm_mine, "gather_dmod")
    dm_all = jnp.concatenate([jnp.moveaxis(dm_g[:, 0:2], 0, 1), jnp.moveaxis(dm_g[:, 2:4], 0, 1)], axis=1)
    dm_sh = lax.dynamic_slice_in_dim(dm_all, me * mod_cols, mod_cols, axis=2)
    g_mod_w, g_cctx_part, g_mod_b = mod_bwd(c16, mod_w, dm_sh, dm_all, "mod_bwd")

    rep = [jnp.stack([g_n1_0[0], g_n1_1[0]]), jnp.stack([gf0["norm2"][0], gf1["norm2"][0]]), g_conv_b5, g_dt_bias, g_a_log,
           g_ssd_d, g_ssd_norm, jnp.stack([gf0["conv_b"][0], gf1["conv_b"][0]]), g_final, g_cctx_part, loss_part[:, :1]]
    rep_sizes = [r_.size for r_ in rep]
    rep_flat = _pad_to(jnp.concatenate([r_.reshape(-1) for r_ in rep]), 8 * PACK_W).reshape(-1, PACK_W)
    _, rep_sum = allgather_small(rep_flat, "reduce_replicated", with_sum=True)
    rep_sum = rep_sum.reshape(-1)
    roffs = [0]
    for s_ in rep_sizes:
        roffs.append(roffs[-1] + s_)
    rp = [rep_sum[roffs[i]:roffs[i + 1]] for i in range(len(rep_sizes))]
    loss = rp[10].reshape(())

    r_up1, r_down1, r_pw2, r_pw1, r_up0, r_down0, r_out = early_red
    big = {}
    def tr(t):
        return jnp.swapaxes(t, -1, -2)

    up_t, m_up_t, v_up_t = tr(ffn_w_up), tr(m_ffn_w_up), tr(v_ffn_w_up)
    send_sems, recv_sems, late_p, late_land, token = late_flying
    up0 = sum_adamw(r_up0, up_t, m_up_t, v_up_t, 0, "adamw_ffn_w_up0", after=token)
    up1 = sum_adamw(r_up1, up_t, m_up_t, v_up_t, 1, "adamw_ffn_w_up1", into=up0)
    big["ffn_w_up"] = tuple(tr(t) for t in up1)
    big["conf_w_pw1"] = sum_adamw(r_pw1, conf_w_pw1[0], m_conf_w_pw1[0], v_conf_w_pw1[0], None, "adamw_conf_w_pw1",
                                  after=up1[0])
    big["ssd_w_out"] = sum_adamw(r_out, ssd_w_out[0], m_ssd_w_out[0], v_ssd_w_out[0], None, "adamw_ssd_w_out",
                                 after=big["conf_w_pw1"][0])
    dn0 = sum_adamw(r_down0, ffn_w_down, m_ffn_w_down, v_ffn_w_down, 0, "adamw_ffn_w_down0", after=big["ssd_w_out"][0])
    big["ffn_w_down"] = sum_adamw(r_down1, ffn_w_down, m_ffn_w_down, v_ffn_w_down, 1, "adamw_ffn_w_down1", into=dn0)
    big["conf_w_pw2"] = sum_adamw(r_pw2, conf_w_pw2[0], m_conf_w_pw2[0], v_conf_w_pw2[0], None, "adamw_conf_w_pw2",
                                  after=big["ffn_w_down"][0])
    r_in, r_small = chips_wait(send_sems, recv_sems, late_p, late_land, big["conf_w_pw2"][0], "reduce_chips_late_wait")
    w_in_res = sum_adamw(r_in, tr(ssd_w_in[0]), tr(m_ssd_w_in[0]), tr(v_ssd_w_in[0]), None, "adamw_ssd_w_in")
    big["ssd_w_in"] = tuple(tr(t) for t in w_in_res)
    g_flat = sum_rows(r_small, "reduce_sum_small").reshape(-1)
    goffs = [0]
    for s_ in gsizes:
        goffs.append(goffs[-1] + s_)
    gs = [g_flat[goffs[i]:goffs[i + 1]] for i in range(len(gsizes))]
    grads = {
        "c_ctx": rp[9], "mod_w": g_mod_w, "mod_b": g_mod_b, "norm1_w": rp[0], "norm2_w": rp[1],
        "ssd_conv_w": gs[0], "ssd_conv_b": rp[2], "ssd_dt_bias": rp[3], "ssd_a_log": rp[4], "ssd_d": rp[5],
        "ssd_norm_w": rp[6], "conf_b_pw1": gs[1], "conf_w_dw": gs[2],
        "conf_b_dw": gs[3], "conf_ln_w": gs[4], "conf_ln_b": gs[5], "conf_b_pw2": gs[6],
        "ffn_conv_w": gs[7], "ffn_conv_b": rp[7], "final_norm_w": rp[8],
    }
    weights = dict(c_ctx=c_ctx, mod_w=mod_w, mod_b=mod_b, norm1_w=norm1_w, norm2_w=norm2_w, ssd_w_in=ssd_w_in, ssd_conv_w=ssd_conv_w, ssd_conv_b=ssd_conv_b, ssd_dt_bias=ssd_dt_bias, ssd_a_log=ssd_a_log, ssd_d=ssd_d, ssd_norm_w=ssd_norm_w, ssd_w_out=ssd_w_out, conf_w_pw1=conf_w_pw1, conf_b_pw1=conf_b_pw1, conf_w_dw=conf_w_dw, conf_b_dw=conf_b_dw, conf_ln_w=conf_ln_w, conf_ln_b=conf_ln_b, conf_w_pw2=conf_w_pw2, conf_b_pw2=conf_b_pw2, ffn_w_up=ffn_w_up, ffn_conv_w=ffn_conv_w, ffn_conv_b=ffn_conv_b, ffn_w_down=ffn_w_down, final_norm_w=final_norm_w)
    m_in = dict(c_ctx=m_c_ctx, mod_w=m_mod_w, mod_b=m_mod_b, norm1_w=m_norm1_w, norm2_w=m_norm2_w, ssd_w_in=m_ssd_w_in, ssd_conv_w=m_ssd_conv_w, ssd_conv_b=m_ssd_conv_b, ssd_dt_bias=m_ssd_dt_bias, ssd_a_log=m_ssd_a_log, ssd_d=m_ssd_d, ssd_norm_w=m_ssd_norm_w, ssd_w_out=m_ssd_w_out, conf_w_pw1=m_conf_w_pw1, conf_b_pw1=m_conf_b_pw1, conf_w_dw=m_conf_w_dw, conf_b_dw=m_conf_b_dw, conf_ln_w=m_conf_ln_w, conf_ln_b=m_conf_ln_b, conf_w_pw2=m_conf_w_pw2, conf_b_pw2=m_conf_b_pw2, ffn_w_up=m_ffn_w_up, ffn_conv_w=m_ffn_conv_w, ffn_conv_b=m_ffn_conv_b, ffn_w_down=m_ffn_w_down, final_norm_w=m_final_norm_w)
    v_in = dict(c_ctx=v_c_ctx, mod_w=v_mod_w, mod_b=v_mod_b, norm1_w=v_norm1_w, norm2_w=v_norm2_w, ssd_w_in=v_ssd_w_in, ssd_conv_w=v_ssd_conv_w, ssd_conv_b=v_ssd_conv_b, ssd_dt_bias=v_ssd_dt_bias, ssd_a_log=v_ssd_a_log, ssd_d=v_ssd_d, ssd_norm_w=v_ssd_norm_w, ssd_w_out=v_ssd_w_out, conf_w_pw1=v_conf_w_pw1, conf_b_pw1=v_conf_b_pw1, conf_w_dw=v_conf_w_dw, conf_b_dw=v_conf_b_dw, conf_ln_w=v_conf_ln_w, conf_ln_b=v_conf_ln_b, conf_w_pw2=v_conf_w_pw2, conf_b_pw2=v_conf_b_pw2, ffn_w_up=v_ffn_w_up, ffn_conv_w=v_ffn_conv_w, ffn_conv_b=v_ffn_conv_b, ffn_w_down=v_ffn_w_down, final_norm_w=v_final_norm_w)

    out_g, out_d, out_m, out_v = [], [], [], []
    for name_, w_ in weights.items():
        shape = w_.shape
        if name_ in big:
            for lst, t in zip((out_g, out_d, out_m, out_v), big[name_]):
                lst.append(t.reshape(shape))
            continue
        cols2 = shape[-1] if len(shape) > 1 else shape[0]
        g2 = grads[name_].reshape(-1, cols2)
        d_, nm_, nv_ = adamw(w_.reshape(-1, cols2), g2, m_in[name_].reshape(-1, cols2), v_in[name_].reshape(-1, cols2),
                             f"adamw_{name_}")
        out_g.append(g2.reshape(shape))
        out_d.append(d_.reshape(shape))
        out_m.append(nm_.reshape(shape))
        out_v.append(nv_.reshape(shape))
    return (loss, grad_x, *out_g, *out_d, *out_m, *out_v)
```

```python
import functools

import jax
import jax.numpy as jnp
from jax import lax
from jax.experimental import pallas as pl
from jax.experimental.pallas import tpu as pltpu

f32 = jnp.float32
bf16 = jnp.bfloat16
HI = lax.Precision.HIGHEST
S = jax.ShapeDtypeStruct
MESH = pl.DeviceIdType.MESH
ANY = pl.BlockSpec(memory_space=pl.ANY)
VMEM = pl.BlockSpec(memory_space=pltpu.VMEM)

N_DEV = 8
D = 1024
DI = 2048
CONVD = 4096
FH = 2816
GRID_W = 64
Q = 128
HPG = 4
P = 64
N = 128
G = 8
GW = HPG * P
NH_SSD = G * HPG
EPS = 1e-6
ADAM_LR, ADAM_B1, ADAM_B2, ADAM_EPS, ADAM_WD, ADAM_STEP = 0.001, 0.9, 0.999, 1e-08, 0.01, 10
VMEM_LIMIT_BYTES = 56 * 1024 * 1024
PACK_W = 1024
TB = 256


def _cparams(*sem):
    return pltpu.CompilerParams(dimension_semantics=sem, vmem_limit_bytes=VMEM_LIMIT_BYTES)


def _pick(n, prefs):
    for p in prefs:
        if n % p == 0:
            return p
    return n


def _sigmoid(x):
    return 1.0 / (1.0 + jnp.exp(-x))


def _softplus(x):
    return jnp.maximum(x, 0.0) + jnp.log(1.0 + jnp.exp(-jnp.abs(x)))


def matmul(a, b, mode, out_dtype, name):
    if mode == "nn":
        (M, K), (_, Nn) = a.shape, b.shape
        bm, bn, bk = _pick(M, (512, 384, 256, 128)), Nn, K
    elif mode == "tn":
        (K, M), (_, Nn) = a.shape, b.shape
        bm, bn, bk = M, Nn, _pick(K, (256, 128))
    else:
        (M, K), (Nn, _) = a.shape, b.shape
        bm, bn, bk = _pick(M, (512, 384, 256, 128)), Nn, K
    nk = K // bk
    dims = {"nn": (((1,), (0,)), ((), ())), "tn": (((0,), (0,)), ((), ())), "nt": (((1,), (1,)), ((), ()))}[mode]

    def body(a_ref, b_ref, o_ref, acc_ref):
        k = pl.program_id(2)

        @pl.when(k == 0)
        def _():
            acc_ref[...] = jnp.zeros_like(acc_ref)

        acc_ref[...] += lax.dot_general(a_ref[...].astype(bf16), b_ref[...].astype(bf16), dims,
                                        preferred_element_type=f32)

        @pl.when(k == nk - 1)
        def _():
            o_ref[...] = acc_ref[...].astype(out_dtype)

    if mode == "nn":
        a_spec = pl.BlockSpec((bm, bk), lambda i, j, k: (i, k))
        b_spec = pl.BlockSpec((bk, bn), lambda i, j, k: (k, j))
    elif mode == "tn":
        a_spec = pl.BlockSpec((bk, bm), lambda i, j, k: (k, i))
        b_spec = pl.BlockSpec((bk, bn), lambda i, j, k: (k, j))
    else:
        a_spec = pl.BlockSpec((bm, bk), lambda i, j, k: (i, k))
        b_spec = pl.BlockSpec((bn, bk), lambda i, j, k: (j, k))
    return pl.pallas_call(
        body, grid=(M // bm, Nn // bn, nk), in_specs=[a_spec, b_spec],
        out_specs=pl.BlockSpec((bm, bn), lambda i, j, k: (i, j)),
        out_shape=S((M, Nn), out_dtype), scratch_shapes=[pltpu.VMEM((bm, bn), f32)],
        compiler_params=_cparams("parallel", "parallel", "arbitrary"), name=name,
    )(a, b)


SMM_BM = 256
SMM_ROWS = (256,)


def _shard_pieces(seg_widths, n):
    bounds = [0]
    for sw in seg_widths:
        bounds.append(bounds[-1] + sw)
    assert bounds[-1] == N_DEV * n, (seg_widths, n)
    out = []
    for j in range(N_DEV):
        lo, hi = j * n, (j + 1) * n
        pcs = []
        for si in range(len(seg_widths)):
            a, b = max(lo, bounds[si]), min(hi, bounds[si + 1])
            if a < b:
                pcs.append((si, a - bounds[si], a - lo, b - a))
        out.append(pcs)
    return out


def _w_spec(w, layer):
    if layer is None:
        return pl.BlockSpec(w.shape, lambda *idx: (0, 0, 0))
    return pl.BlockSpec((N_DEV, None) + w.shape[2:], lambda *idx: (0, layer, 0, 0))


def smm_fwd(a, w, layer, seg_widths, name, hosted=None):
    M, K = a.shape
    n = w.shape[-1]
    pieces = _shard_pieces(seg_widths, n)
    padded = [sw + (-sw) % 128 for sw in seg_widths]
    bm = _pick(M, SMM_ROWS)

    def body(a_ref, w_ref, *o_refs):
        av = a_ref[...]
        for si, sw in enumerate(seg_widths):
            if padded[si] != sw:
                o_refs[si][:, pl.ds(padded[si] - 128, 128)] = jnp.zeros((bm, 128), f32)
        for j in range(N_DEV):
            for si, soff, woff, wd in pieces[j]:
                o_refs[si][:, pl.ds(soff, wd)] = jnp.dot(av, w_ref[j, :, pl.ds(woff, wd)], preferred_element_type=f32)

    outs, extra = _host_call(
        body, (M // bm,), [pl.BlockSpec((bm, K), lambda i: (i, 0)), _w_spec(w, layer)],
        [pl.BlockSpec((bm, pw), lambda i: (i, 0)) for pw in padded], [S((M, pw), f32) for pw in padded], [],
        ("parallel",), name, (a, w), hosted)
    return outs if hosted is None else (outs, extra)


def smm_dx(d_segs, w, layer, seg_widths, out_dtype, name, hosted=None):
    M = d_segs[0].shape[0]
    K, n = w.shape[-2], w.shape[-1]
    pieces = _shard_pieces(seg_widths, n)
    ns = len(d_segs)
    bm = _pick(M, SMM_ROWS)

    def body(*refs):
        d_refs, w_ref, o_ref = refs[:ns], refs[ns], refs[ns + 1]
        acc = jnp.zeros((bm, K), f32)
        for j in range(N_DEV):
            for si, soff, woff, wd in pieces[j]:
                acc = acc + lax.dot_general(d_refs[si][:, pl.ds(soff, wd)], w_ref[j, :, pl.ds(woff, wd)],
                                            (((1,), (1,)), ((), ())), preferred_element_type=f32)
        o_ref[...] = acc.astype(out_dtype)

    (out,), extra = _host_call(
        body, (M // bm,),
        [pl.BlockSpec((bm, d.shape[1]), lambda i: (i, 0)) for d in d_segs] + [_w_spec(w, layer)],
        [pl.BlockSpec((bm, K), lambda i: (i, 0))], [S((M, K), out_dtype)], [], ("parallel",), name,
        (*d_segs, w), hosted)
    return out, extra


def smm_dw(a, d_segs, n, seg_widths, ngrp, transposed, name):
    M, K = a.shape
    pieces = _shard_pieces(seg_widths, n)
    per = N_DEV // ngrp
    nI = M // SMM_BM
    ns = len(d_segs)
    shard = (n, K) if transposed else (K, n)

    def body(*refs):
        a_ref, d_refs, o_ref, acc_ref = refs[0], refs[1:1 + ns], refs[1 + ns], refs[2 + ns]
        grp = pl.program_id(0)
        i = pl.program_id(1)

        @pl.when(i == 0)
        def _():
            acc_ref[...] = jnp.zeros_like(acc_ref)

        av = a_ref[...]
        for gs in range(ngrp):
            def one_group(gs=gs):
                for jj in range(per):
                    for si, soff, woff, wd in pieces[gs * per + jj]:
                        dv = d_refs[si][:, pl.ds(soff, wd)]
                        if transposed:
                            acc_ref[jj, pl.ds(woff, wd), :] += lax.dot_general(
                                dv, av, (((0,), (0,)), ((), ())), preferred_element_type=f32)
                        else:
                            acc_ref[jj, :, pl.ds(woff, wd)] += lax.dot_general(
                                av, dv, (((0,), (0,)), ((), ())), preferred_element_type=f32)
            pl.when(grp == gs)(one_group)

        @pl.when(i == nI - 1)
        def _():
            o_ref[...] = acc_ref[...].astype(bf16)

    return pl.pallas_call(
        body, grid=(ngrp, nI),
        in_specs=[pl.BlockSpec((SMM_BM, K), lambda g, i: (i, 0))]
        + [pl.BlockSpec((SMM_BM, d.shape[1]), lambda g, i: (i, 0)) for d in d_segs],
        out_specs=pl.BlockSpec((per,) + shard, lambda g, i: (g, 0, 0)), out_shape=S((N_DEV,) + shard, bf16),
        scratch_shapes=[pltpu.VMEM((per,) + shard, f32)],
        compiler_params=_cparams("arbitrary", "arbitrary"), name=name)(a, *d_segs)


def _modnorm_f(h, w, sc, sh):
    y = h * lax.rsqrt(jnp.mean(h * h, axis=-1, keepdims=True) + EPS)
    return (y * w) * (1.0 + sc) + sh


def _kind_specs(nctxb):
    if nctxb > 0:
        return pl.BlockSpec((None, 1, D), lambda i: (jnp.where(i < nctxb, 0, 1), 0, 0))
    return pl.BlockSpec((None, 1, D), lambda i: (0, 0, 0))


def _two_part_specs(nctxb):
    return (pl.BlockSpec((TB, D), lambda i: (jnp.minimum(i, nctxb - 1), 0)),
            pl.BlockSpec((TB, D), lambda i: (jnp.maximum(i - nctxb, 0), 0)))


def modnorm_fwd(h, w, sc, sh, nctxb, name, ctx=None):
    if ctx is None:
        T = h.shape[0]

        def body(h_ref, w_ref, sc_ref, sh_ref, o_ref):
            o_ref[...] = _modnorm_f(h_ref[...], w_ref[...], sc_ref[...], sh_ref[...]).astype(bf16)

        hspecs, hargs = [pl.BlockSpec((TB, D), lambda i: (i, 0))], (h,)
    else:
        T = h.shape[0] + ctx.shape[0]

        def body(c_ref, h_ref, w_ref, sc_ref, sh_ref, o_ref):
            hv = jnp.where(pl.program_id(0) < nctxb, c_ref[...], h_ref[...])
            o_ref[...] = _modnorm_f(hv, w_ref[...], sc_ref[...], sh_ref[...]).astype(bf16)

        hspecs, hargs = list(_two_part_specs(nctxb)), (ctx, h)
    row = pl.BlockSpec((1, D), lambda i: (0, 0))
    ks = _kind_specs(nctxb)
    return pl.pallas_call(body, grid=(T // TB,), in_specs=hspecs + [row, ks, ks],
                          out_specs=pl.BlockSpec((TB, D), lambda i: (i, 0)), out_shape=S((T, D), bf16),
                          compiler_params=_cparams("parallel"), name=name)(*hargs, w, sc, sh)


def modnorm_bwd(h, w, sc, sh, da, dres, nctxb, name, ctx=None):
    T = h.shape[0] + (0 if ctx is None else ctx.shape[0])
    kinds = sc.shape[0]
    nh = 1 if ctx is None else 2

    def body(*refs):
        w_ref, sc_ref, sh_ref, da_ref, dres_ref, dh_ref, dw_ref, dsc_ref, dsh_ref = refs[nh:]
        i = pl.program_id(0)
        hv = refs[0][...] if ctx is None else jnp.where(i < nctxb, refs[0][...], refs[1][...])
        _, vjp = jax.vjp(_modnorm_f, hv, w_ref[...], sc_ref[...], sh_ref[...])
        dh, dw, dsc, dsh = vjp(da_ref[...].astype(f32))
        dh_ref[...] = dres_ref[...] + dh

        @pl.when(i == 0)
        def _():
            dw_ref[...] = jnp.zeros_like(dw_ref)

        @pl.when((i == 0) | (i == nctxb))
        def _():
            dsc_ref[...] = jnp.zeros_like(dsc_ref)
            dsh_ref[...] = jnp.zeros_like(dsh_ref)

        dw_ref[...] += dw
        dsc_ref[...] += dsc
        dsh_ref[...] += dsh

    blk = pl.BlockSpec((TB, D), lambda i: (i, 0))
    lat = pl.BlockSpec((TB, D), lambda i: (jnp.maximum(i - nctxb, 0), 0))
    row = pl.BlockSpec((1, D), lambda i: (0, 0))
    ks = _kind_specs(nctxb)
    hspecs, hargs = ([blk], (h,)) if ctx is None else (list(_two_part_specs(nctxb)), (ctx, h))
    return pl.pallas_call(
        body, grid=(T // TB,), in_specs=hspecs + [row, ks, ks, blk, lat], out_specs=[lat, row, ks, ks],
        out_shape=[S((T - nctxb * TB, D), f32), S((1, D), f32), S((kinds, 1, D), f32), S((kinds, 1, D), f32)],
        compiler_params=_cparams("arbitrary"), name=name)(*hargs, w, sc, sh, da, dres)


def resgate_fwd(h, o, g, b, name):
    T = h.shape[0]

    def body(h_ref, o_ref, g_ref, b_ref, out_ref):
        out_ref[...] = h_ref[...] + g_ref[...] * (o_ref[...] + b_ref[...])

    blk = pl.BlockSpec((TB, D), lambda i: (i, 0))
    row = pl.BlockSpec((1, D), lambda i: (0, 0))
    return pl.pallas_call(body, grid=(T // TB,), in_specs=[blk, blk, row, row], out_specs=blk,
                          out_shape=S((T, D), f32), compiler_params=_cparams("parallel"), name=name)(h, o, g, b)


def resgate_bwd(dh, o, g, b, name):
    T = dh.shape[0]

    def body(dh_ref, o_ref, g_ref, b_ref, do_ref, dg_ref, db_ref):
        i = pl.program_id(0)

        @pl.when(i == 0)
        def _():
            dg_ref[...] = jnp.zeros_like(dg_ref)
            db_ref[...] = jnp.zeros_like(db_ref)

        dh = dh_ref[...]
        do = g_ref[...] * dh
        do_ref[...] = do.astype(bf16)
        dg_ref[...] += jnp.sum(dh * (o_ref[...] + b_ref[...]), axis=0, keepdims=True)
        db_ref[...] += jnp.sum(do, axis=0, keepdims=True)

    blk = pl.BlockSpec((TB, D), lambda i: (i, 0))
    row = pl.BlockSpec((1, D), lambda i: (0, 0))
    return pl.pallas_call(body, grid=(T // TB,), in_specs=[blk, blk, row, row], out_specs=[blk, row, row],
                          out_shape=[S((T, D), bf16), S((1, D), f32), S((1, D), f32)],
                          compiler_params=_cparams("arbitrary"), name=name)(dh, o, g, b)


def final_loss(h, w, tgt, name):
    T = h.shape[0]

    def f(hv, wv, tv):
        y = (hv * lax.rsqrt(jnp.mean(hv * hv, axis=-1, keepdims=True) + EPS)) * wv
        e = y - tv
        return 0.5 * jnp.sum(jnp.sum(e * e, axis=-1, keepdims=True), axis=0, keepdims=True) * (1.0 / D)

    def body(h_ref, w_ref, t_ref, loss_ref, dh_ref, dw_ref):
        i = pl.program_id(0)
        tv = t_ref[...]
        val, vjp = jax.vjp(lambda a, b_: f(a, b_, tv), h_ref[...], w_ref[...])
        dh, dw = vjp(jnp.ones((1, 1), f32))
        dh_ref[...] = dh

        @pl.when(i == 0)
        def _():
            loss_ref[...] = jnp.zeros_like(loss_ref)
            dw_ref[...] = jnp.zeros_like(dw_ref)

        loss_ref[...] += jnp.broadcast_to(val, (1, 128))
        dw_ref[...] += dw

    blk = pl.BlockSpec((TB, D), lambda i: (i, 0))
    row = pl.BlockSpec((1, D), lambda i: (0, 0))
    return pl.pallas_call(body, grid=(T // TB,), in_specs=[blk, row, blk],
                          out_specs=[pl.BlockSpec((1, 128), lambda i: (0, 0)), blk, row],
                          out_shape=[S((1, 128), f32), S((T, D), f32), S((1, D), f32)],
                          compiler_params=_cparams("arbitrary"), name=name)(h, w, tgt)


CB = 256
RT = 32
RTB = 16


def _fold8(t):
    acc = t[0:8]
    for k in range(1, t.shape[0] // 8):
        acc = acc + t[8 * k:8 * (k + 1)]
    return acc


def _rows(start, off=0, rt=RT):
    return pl.ds(pl.multiple_of(start + off, 8), rt)


def _rowsb(start, off=0):
    return _rows(start, off, RTB)


def _zero_rows(ref, start, n):
    ref[pl.ds(start, n), :] = jnp.zeros((n, ref.shape[1]), f32)


K5, HALF5, PAD5 = 5, 2, 8


def _shift_copies5(base_ref, s_ref, ln, sign):
    for k in range(K5):
        s_ref[k, pl.ds(0, ln), :] = base_ref[pl.ds(PAD5 + sign * (k - HALF5), ln), :]


def ssd_conv_fwd(u, w, b, segs, name, hosted=None):
    T = u.shape[0]
    maxlen = max(ln for _, ln in segs)

    def body(u_ref, w_ref, b_ref, o_ref, ds_ref, base_ref, s_ref):
        wv = [w_ref[pl.ds(k, 1), :] for k in range(K5)]
        bv = b_ref[...]
        for s0, ln in segs:
            _zero_rows(base_ref, 0, PAD5)
            _zero_rows(base_ref, PAD5 + ln, PAD5)
            base_ref[pl.ds(PAD5, ln), :] = u_ref[pl.ds(s0, ln), :]
            _shift_copies5(base_ref, s_ref, ln, 1)

            def tile(i, carry):
                r = i * RT
                acc = jnp.broadcast_to(bv, (RT, CB))
                for k in range(K5):
                    acc = acc + s_ref[k, _rows(r), :] * wv[k]
                sg = _sigmoid(acc)
                o_ref[_rows(r, s0), :] = acc * sg
                ds_ref[_rows(r, s0), :] = sg * (1.0 + acc * (1.0 - sg))
                return carry

            lax.fori_loop(0, ln // RT, tile, 0, unroll=2)

    cblk = pl.BlockSpec((T, CB), lambda j: (0, j))
    (out, dsilu), extra = _host_call(
        body, (CONVD // CB,),
        [cblk, pl.BlockSpec((K5, CB), lambda j: (0, j)), pl.BlockSpec((1, CB), lambda j: (0, j))],
        [cblk, cblk], [S((T, CONVD), f32), S((T, CONVD), f32)],
        [pltpu.VMEM((maxlen + 2 * PAD5, CB), f32), pltpu.VMEM((K5, maxlen, CB), f32)],
        ("parallel",), name, (u, w, b), hosted)
    return out, dsilu, extra


def ssd_conv_bwd(proj, w, dsilu, dy2, dyskip, dexp, segs, name):
    T = proj.shape[0]
    maxlen = max(ln for _, ln in segs)
    nskip = DI // CB

    def body(u_ref, w_ref, ds_ref, dya_ref, dyb_ref, dsk_ref, dexp_ref, du_ref, dw_ref, db_ref, base_ref, s_ref):
        wv = [w_ref[pl.ds(k, 1), :] for k in range(K5)]
        has_skip = (pl.program_id(0) < nskip).astype(f32) * dexp_ref[...]
        acc8 = tuple(jnp.zeros((8, CB), f32) for _ in range(K5 + 1))
        for s0, ln in segs:
            _zero_rows(base_ref, 0, PAD5)
            _zero_rows(base_ref, PAD5 + ln, PAD5)
            base_ref[pl.ds(PAD5, ln), :] = u_ref[pl.ds(s0, ln), :]
            _shift_copies5(base_ref, s_ref, ln, 1)

            def tile1(i, carry):
                r = i * RTB
                dy = dya_ref[_rowsb(r, s0), :] + dyb_ref[_rowsb(r, s0), :] + has_skip * dsk_ref[_rowsb(r, s0), :]
                dpre = dy * ds_ref[_rowsb(r, s0), :]
                base_ref[_rowsb(r, PAD5), :] = dpre
                new = [carry[k] + _fold8(dpre * s_ref[k, _rowsb(r), :]) for k in range(K5)]
                new.append(carry[K5] + _fold8(dpre))
                return tuple(new)

            acc8 = lax.fori_loop(0, ln // RTB, tile1, acc8, unroll=2)
            _shift_copies5(base_ref, s_ref, ln, -1)

            def tile2(i, carry):
                r = i * RTB
                du = jnp.zeros((RTB, CB), f32)
                for k in range(K5):
                    du = du + s_ref[k, _rowsb(r), :] * wv[k]
                du_ref[_rowsb(r, s0), :] = du.astype(bf16)
                return carry

            lax.fori_loop(0, ln // RTB, tile2, 0, unroll=4)
        for k in range(K5):
            dw_ref[pl.ds(k, 1), :] = jnp.sum(acc8[k], axis=0, keepdims=True)
        db_ref[...] = jnp.sum(acc8[K5], axis=0, keepdims=True)

    cblk = pl.BlockSpec((T, CB), lambda j: (0, j))
    return pl.pallas_call(
        body, grid=(CONVD // CB,),
        in_specs=[cblk, pl.BlockSpec((K5, CB), lambda j: (0, j)), cblk,
                  pl.BlockSpec((None, T, CB), lambda j: (0, 0, j)), pl.BlockSpec((None, T, CB), lambda j: (1, 0, j)),
                  pl.BlockSpec((T, CB), lambda j: (0, jnp.minimum(j, nskip - 1))),
                  pl.BlockSpec((1, CB), lambda j: (0, jnp.minimum(j, nskip - 1)))],
        out_specs=[cblk, pl.BlockSpec((K5, CB), lambda j: (0, j)), pl.BlockSpec((1, CB), lambda j: (0, j))],
        out_shape=[S((T, CONVD), bf16), S((K5, CONVD), f32), S((1, CONVD), f32)],
        scratch_shapes=[pltpu.VMEM((maxlen + 2 * PAD5, CB), f32), pltpu.VMEM((K5, maxlen, CB), f32)],
        compiler_params=_cparams("parallel"), name=name)(proj, w, dsilu, dy2, dy2, dyskip, dexp)


GPAD = GRID_W


def _grid_copies(g_ref, src, L):
    col = lax.broadcasted_iota(jnp.int32, (L, CB), 0) & (GRID_W - 1)
    for d in range(3):
        _zero_rows(g_ref.at[d], 0, GPAD)
        _zero_rows(g_ref.at[d], GPAD + L, GPAD)
    g_ref[1, pl.ds(GPAD, L), :] = src
    g_ref[0, pl.ds(GPAD, L), :] = jnp.where(col != 0, g_ref[1, pl.ds(GPAD - 1, L), :], 0.0)
    g_ref[2, pl.ds(GPAD, L), :] = jnp.where(col != GRID_W - 1, g_ref[1, pl.ds(GPAD + 1, L), :], 0.0)


def ffn_gate_fwd(val, gate, cw, cb_, name, hosted=None):
    L = val.shape[0]
    nb = FH // CB

    def body(val_ref, gate_ref, w_ref, b_ref, o_ref, s_ref, vds_ref, g_ref):
        wv = [w_ref[pl.ds(t, 1), :] for t in range(9)]
        bv = b_ref[...]
        _grid_copies(g_ref, gate_ref[...], L)

        def tile(i, carry):
            r = i * RT
            acc = jnp.broadcast_to(bv, (RT, CB))
            for dr in range(3):
                for dc in range(3):
                    acc = acc + g_ref[dc, _rows(r, GPAD + (dr - 1) * GRID_W), :] * wv[3 * dr + dc]
            sg = _sigmoid(acc)
            s = acc * sg
            v = val_ref[_rows(r), :]
            o_ref[_rows(r), :] = (s * v).astype(bf16)
            s_ref[_rows(r), :] = s
            vds_ref[_rows(r), :] = v * (sg * (1.0 + acc * (1.0 - sg)))
            return carry

        lax.fori_loop(0, L // RT, tile, 0, unroll=2)

    cblk = pl.BlockSpec((L, CB), lambda j: (0, j))
    (out, s_, vds), extra = _host_call(
        body, (nb,), [cblk, cblk, pl.BlockSpec((9, CB), lambda j: (0, j)), pl.BlockSpec((1, CB), lambda j: (0, j))],
        [cblk, cblk, cblk], [S((L, FH), bf16), S((L, FH), f32), S((L, FH), f32)],
        [pltpu.VMEM((3, L + 2 * GPAD, CB), f32)], ("parallel",), name, (val, gate, cw, cb_), hosted)
    return out, s_, vds, extra


def ffn_gate_bwd(gate, s_, vds, cw, dact, name):
    L = gate.shape[0]
    nb = FH // CB

    def body(gate_ref, s_ref, vds_ref, w_ref, da_ref, dval_ref, dgate_ref, dw_ref, db_ref, g_ref, d_ref):
        wv = [w_ref[pl.ds(t, 1), :] for t in range(9)]
        _grid_copies(g_ref, gate_ref[...], L)

        def tile1(i, carry):
            r = i * RTB
            da = da_ref[_rowsb(r), :].astype(f32)
            dval_ref[_rowsb(r), :] = (da * s_ref[_rowsb(r), :]).astype(bf16)
            dpre = da * vds_ref[_rowsb(r), :]
            d_ref[_rowsb(r), :] = dpre
            new = [carry[t] + _fold8(dpre * g_ref[t % 3, _rowsb(r, GPAD + (t // 3 - 1) * GRID_W), :]) for t in range(9)]
            new.append(carry[9] + _fold8(dpre))
            return tuple(new)

        acc8 = lax.fori_loop(0, L // RTB, tile1, tuple(jnp.zeros((8, CB), f32) for _ in range(10)), unroll=2)
        for t in range(9):
            dw_ref[pl.ds(t, 1), :] = jnp.sum(acc8[t], axis=0, keepdims=True)
        db_ref[...] = jnp.sum(acc8[9], axis=0, keepdims=True)
        _grid_copies(g_ref, d_ref[...], L)

        def tile2(i, carry):
            r = i * RTB
            dg = jnp.zeros((RTB, CB), f32)
            for dr in range(3):
                for dc in range(3):
                    dg = dg + g_ref[2 - dc, _rowsb(r, GPAD - (dr - 1) * GRID_W), :] * wv[3 * dr + dc]
            dgate_ref[_rowsb(r), :] = dg.astype(bf16)
            return carry

        lax.fori_loop(0, L // RTB, tile2, 0, unroll=4)

    cblk = pl.BlockSpec((L, CB), lambda j: (0, j))
    return pl.pallas_call(
        body, grid=(nb,),
        in_specs=[cblk, cblk, cblk, pl.BlockSpec((9, CB), lambda j: (0, j)), cblk],
        out_specs=[cblk, cblk, pl.BlockSpec((9, CB), lambda j: (0, j)), pl.BlockSpec((1, CB), lambda j: (0, j))],
        out_shape=[S((L, FH), bf16), S((L, FH), bf16), S((9, FH), f32), S((1, FH), f32)],
        scratch_shapes=[pltpu.VMEM((3, L + 2 * GPAD, CB), f32), pltpu.VMEM((L, CB), f32)],
        compiler_params=_cparams("parallel"), name=name)(gate, s_, vds, cw, dact)


CONF_K = 31
CHALF = CONF_K // 2
CPAD = 16


def _shift_copies8(c_ref, base_ref, L):
    n = L + 2 * CPAD - 8
    for b_ in range(8):
        c_ref[b_, pl.ds(0, n), :] = base_ref[pl.ds(b_, n), :]


def _tap_ab(o):
    return o % 8, o - o % 8


def conf_glu_conv_fwd(pa, pg, b1, wdw, bdw, name, hosted=None):
    L = pa.shape[0]
    nb = D // CB

    def body(pa_ref, pg_ref, ba_ref, bg_ref, w_ref, bdw_ref, o_ref, base_ref, c_ref):
        _zero_rows(base_ref, 0, CPAD)
        _zero_rows(base_ref, CPAD + L, CPAD)
        base_ref[pl.ds(CPAD, L), :] = (pa_ref[...] + ba_ref[...]) * _sigmoid(pg_ref[...] + bg_ref[...])
        _shift_copies8(c_ref, base_ref, L)
        bv = bdw_ref[...]

        def tile(i, carry):
            r = i * RT
            acc = jnp.broadcast_to(bv, (RT, CB))
            for k in range(CONF_K):
                b_, a8 = _tap_ab(k - CHALF)
                acc = acc + c_ref[b_, _rows(r, CPAD + a8), :] * w_ref[pl.ds(k, 1), :]
            o_ref[_rows(r), :] = acc
            return carry

        lax.fori_loop(0, L // RT, tile, 0, unroll=2)

    cblk = pl.BlockSpec((L, CB), lambda j: (0, j))
    rblk = pl.BlockSpec((1, CB), lambda j: (0, j))
    rgblk = pl.BlockSpec((1, CB), lambda j: (0, nb + j))
    (out,), extra = _host_call(
        body, (nb,), [cblk, cblk, rblk, rgblk, pl.BlockSpec((CONF_K, CB), lambda j: (0, j)), rblk],
        [cblk], [S((L, D), f32)], [pltpu.VMEM((L + 2 * CPAD, CB), f32), pltpu.VMEM((8, L + 2 * CPAD, CB), f32)],
        ("parallel",), name, (pa, pg, b1, b1, wdw, bdw), hosted)
    return out, extra


def conf_glu_conv_bwd(pa, pg, b1, wdw, dy, name):
    L = pa.shape[0]
    nb = D // CB

    def body(pa_ref, pg_ref, ba_ref, bg_ref, w_ref, dy_ref, dpa_ref, dpg_ref, dba_ref, dbg_ref, dw_ref, dbdw_ref,
             base_ref, c_ref, acc_ref):
        _zero_rows(base_ref, 0, CPAD)
        _zero_rows(base_ref, CPAD + L, CPAD)
        base_ref[pl.ds(CPAD, L), :] = (pa_ref[...] + ba_ref[...]) * _sigmoid(pg_ref[...] + bg_ref[...])
        _shift_copies8(c_ref, base_ref, L)
        acc_ref[...] = jnp.zeros_like(acc_ref)

        def tile1(i, carry):
            r = i * RTB
            dyt = dy_ref[_rowsb(r), :]
            for k in range(CONF_K):
                b_, a8 = _tap_ab(k - CHALF)
                acc_ref[k] += _fold8(dyt * c_ref[b_, _rowsb(r, CPAD + a8), :])
            return carry + _fold8(dyt)

        db8 = lax.fori_loop(0, L // RTB, tile1, jnp.zeros((8, CB), f32), unroll=2)
        dbdw_ref[...] = jnp.sum(db8, axis=0, keepdims=True)
        for k in range(CONF_K):
            dw_ref[pl.ds(k, 1), :] = jnp.sum(acc_ref[k], axis=0, keepdims=True)
        base_ref[pl.ds(CPAD, L), :] = dy_ref[...]
        _shift_copies8(c_ref, base_ref, L)
        ba = ba_ref[...]
        bg = bg_ref[...]

        def tile2(i, carry):
            r = i * RTB
            dglu = jnp.zeros((RTB, CB), f32)
            for k in range(CONF_K):
                b_, a8 = _tap_ab(CHALF - k)
                dglu = dglu + c_ref[b_, _rowsb(r, CPAD + a8), :] * w_ref[pl.ds(k, 1), :]
            a = pa_ref[_rowsb(r), :] + ba
            sg = _sigmoid(pg_ref[_rowsb(r), :] + bg)
            dpa = dglu * sg
            dpg = dglu * a * (sg * (1.0 - sg))
            dpa_ref[_rowsb(r), :] = dpa.astype(bf16)
            dpg_ref[_rowsb(r), :] = dpg.astype(bf16)
            return carry[0] + _fold8(dpa), carry[1] + _fold8(dpg)

        s8 = lax.fori_loop(0, L // RTB, tile2, (jnp.zeros((8, CB), f32), jnp.zeros((8, CB), f32)), unroll=2)
        dba_ref[...] = jnp.sum(s8[0], axis=0, keepdims=True)
        dbg_ref[...] = jnp.sum(s8[1], axis=0, keepdims=True)

    cblk = pl.BlockSpec((L, CB), lambda j: (0, j))
    rblk = pl.BlockSpec((1, CB), lambda j: (0, j))
    rgblk = pl.BlockSpec((1, CB), lambda j: (0, nb + j))
    wblk = pl.BlockSpec((CONF_K, CB), lambda j: (0, j))
    return pl.pallas_call(
        body, grid=(nb,), in_specs=[cblk, cblk, rblk, rgblk, wblk, cblk],
        out_specs=[cblk, cblk, rblk, rblk, wblk, rblk],
        out_shape=[S((L, D), bf16), S((L, D), bf16), S((1, D), f32), S((1, D), f32), S((CONF_K, D), f32), S((1, D), f32)],
        scratch_shapes=[pltpu.VMEM((L + 2 * CPAD, CB), f32), pltpu.VMEM((8, L + 2 * CPAD, CB), f32),
                        pltpu.VMEM((CONF_K, 8, CB), f32)],
        compiler_params=_cparams("parallel"), name=name)(pa, pg, b1, b1, wdw, dy)


def _ln_silu_f(x, w, b):
    mu = jnp.mean(x, axis=-1, keepdims=True)
    d = x - mu
    y = d * lax.rsqrt(jnp.mean(d * d, axis=-1, keepdims=True) + EPS) * w + b
    return y * _sigmoid(y)


def ln_silu_fwd(x, w, b, name):
    T = x.shape[0]

    def body(x_ref, w_ref, b_ref, o_ref):
        o_ref[...] = _ln_silu_f(x_ref[...], w_ref[...], b_ref[...]).astype(bf16)

    blk = pl.BlockSpec((TB, D), lambda i: (i, 0))
    row = pl.BlockSpec((1, D), lambda i: (0, 0))
    return pl.pallas_call(body, grid=(T // TB,), in_specs=[blk, row, row], out_specs=blk, out_shape=S((T, D), bf16),
                          compiler_params=_cparams("parallel"), name=name)(x, w, b)


def ln_silu_bwd(x, w, b, ds, name):
    T = x.shape[0]

    def body(x_ref, w_ref, b_ref, ds_ref, dx_ref, dw_ref, db_ref):
        i = pl.program_id(0)
        _, vjp = jax.vjp(_ln_silu_f, x_ref[...], w_ref[...], b_ref[...])
        dx, dw, db = vjp(ds_ref[...].astype(f32))
        dx_ref[...] = dx

        @pl.when(i == 0)
        def _():
            dw_ref[...] = jnp.zeros_like(dw_ref)
            db_ref[...] = jnp.zeros_like(db_ref)

        dw_ref[...] += dw
        db_ref[...] += db

    blk = pl.BlockSpec((TB, D), lambda i: (i, 0))
    row = pl.BlockSpec((1, D), lambda i: (0, 0))
    return pl.pallas_call(body, grid=(T // TB,), in_specs=[blk, row, row, blk], out_specs=[blk, row, row],
                          out_shape=[S((T, D), f32), S((1, D), f32), S((1, D), f32)],
                          compiler_params=_cparams("arbitrary"), name=name)(x, w, b, ds)


def _mxu(a, b, dims):
    return lax.dot_general(a.astype(bf16), b.astype(bf16), (dims, ((), ())), preferred_element_type=f32)


def _nn(a, b):
    return _mxu(a, b, ((1,), (0,)))


def _nt(a, b):
    return _mxu(a, b, ((1,), (1,)))


def _tn(a, b):
    return _mxu(a, b, ((0,), (0,)))


@jax.custom_vjp
def _dot_nn(a, b):
    return _nn(a, b)


@jax.custom_vjp
def _dot_nt(a, b):
    return _nt(a, b)


@jax.custom_vjp
def _dot_tn(a, b):
    return _tn(a, b)


_dot_nn.defvjp(lambda a, b: (_nn(a, b), (a, b)), lambda res, g: (_nt(g, res[1]), _tn(res[0], g)))
_dot_nt.defvjp(lambda a, b: (_nt(a, b), (a, b)), lambda res, g: (_nn(g, res[1]), _tn(g, res[0])))
_dot_tn.defvjp(lambda a, b: (_tn(a, b), (a, b)), lambda res, g: (_nt(res[1], g), _nn(res[0], g)))


def _exact_dot(a, b, dims, split_first):
    v = a if split_first else b
    p1 = v.astype(bf16)
    r1 = v - p1.astype(f32)
    p2 = r1.astype(bf16)
    p3 = (r1 - p2.astype(f32)).astype(bf16)
    out = None
    for p in (p1, p2, p3):
        lhs, rhs = (p, b.astype(bf16)) if split_first else (a.astype(bf16), p)
        t = lax.dot_general(lhs, rhs, (dims, ((), ())), preferred_element_type=f32)
        out = t if out is None else out + t
    return out


@jax.custom_vjp
def _masked_sum_cols(mf, a):
    return _exact_dot(mf, a, ((1,), (0,)), False)


@jax.custom_vjp
def _masked_sum_rows(mf, a):
    return _exact_dot(a, mf, ((1,), (1,)), True)


_masked_sum_cols.defvjp(lambda mf, a: (_exact_dot(mf, a, ((1,), (0,)), False), mf),
                        lambda mf, g: (jnp.zeros_like(mf), _exact_dot(mf, g, ((0,), (0,)), False)))
_masked_sum_rows.defvjp(lambda mf, a: (_exact_dot(a, mf, ((1,), (1,)), True), mf),
                        lambda mf, g: (jnp.zeros_like(mf), _exact_dot(g, mf, ((1,), (0,)), True)))


def _masked_sum(mf, a, rows):
    return _masked_sum_rows(mf, a) if rows else _masked_sum_cols(mf, a)


def _lanes_to_rows(v):
    r = lax.broadcasted_iota(jnp.int32, (GW, GW), 0)
    c = lax.broadcasted_iota(jnp.int32, (GW, GW), 1)
    return jnp.sum(jnp.where(r == c, jnp.broadcast_to(v, (GW, GW)), 0.0), axis=1, keepdims=True)


def _ssd_chunk(x, B, C, dtc, dtr, bc, br, alc, alr, s_in, is_fwd):
    row = lax.broadcasted_iota(jnp.int32, (Q, Q), 0)
    col = lax.broadcasted_iota(jnp.int32, (Q, Q), 1)
    sgn = jnp.where(is_fwd, 1, -1).astype(jnp.int32)
    mask = (row - col) * sgn >= 0
    mf = mask.astype(f32)
    lane_head = lax.broadcasted_iota(jnp.int32, (1, GW), 1) // P

    def spread(v):
        out = jnp.zeros((v.shape[0], GW), f32)
        for r in range(HPG):
            out = jnp.where(lane_head == r, v[:, r:r + 1], out)
        return out

    dt_c = _softplus(dtc + bc)
    dt_r = _softplus(dtr + br)
    a_c = dt_c * (-jnp.exp(alc))
    a_r = dt_r * (-jnp.exp(alr))
    acum_c = _masked_sum(mf, a_c, False)
    acum_r = _masked_sum(mf, a_r, True)
    tot_c = jnp.sum(a_c, axis=0, keepdims=True)
    dt_e = spread(dt_c)
    acum_e = spread(acum_c)
    tot_e = spread(tot_c)
    xdt = x * dt_e
    cb = _dot_nt(C, B)
    scores, xs = [], []
    for r in range(HPG):
        seg = acum_c[:, r:r + 1] - acum_r[r:r + 1, :]
        scores.append(cb * jnp.exp(jnp.where(mask, seg, -jnp.inf)))
        xs.append(jnp.where(lane_head == r, xdt, 0.0))
    y = _dot_nn(jnp.concatenate(scores, axis=1), jnp.concatenate(xs, axis=0))
    y = y + _dot_nt(C, s_in) * jnp.exp(acum_e)
    xe = xdt * jnp.exp(tot_e - acum_e)
    s_out = _lanes_to_rows(jnp.exp(tot_e)) * s_in + _dot_tn(xe, B)
    return y, s_out


def _chunk_index(d, t, nctx, nc):
    bwd = jnp.where(t < nctx, nctx - 1 - t, nc - 1 - (t - nctx))
    return jnp.where(d == 0, t, bwd)


def _ssd_in_specs(ci):
    small_c = pl.BlockSpec((None, G, 1, HPG), lambda d, t: (d, 0, 0, 0))
    small_r = pl.BlockSpec((None, G, HPG, 1), lambda d, t: (d, 0, 0, 0))
    return [
        pl.BlockSpec((Q, CONVD), lambda d, t: (ci(d, t), 0)),
        pl.BlockSpec((None, G, Q, HPG), lambda d, t: (d, 0, ci(d, t), 0)),
        pl.BlockSpec((None, G, HPG, Q), lambda d, t: (d, 0, 0, ci(d, t))),
        small_c, small_r, small_c, small_r,
    ]


def _group_cols(g):
    return pl.ds(g * GW, GW), pl.ds(DI + g * N, N), pl.ds(DI + G * N + g * N, N)


def ssd_scan_fwd(xbc, dtc, dtr, bc, br, alc, alr, nctx, name, hosted=None):
    T = xbc.shape[0]
    nc = T // Q

    def body(xbc_ref, dtc_ref, dtr_ref, bc_ref, br_ref, alc_ref, alr_ref, y_ref, sin_ref, st_ref):
        d = pl.program_id(0)
        t = pl.program_id(1)

        @pl.when(t == 0)
        def _():
            st_ref[...] = jnp.zeros_like(st_ref)

        for g in range(G):
            xs, bs, cs = _group_cols(g)
            s_in = st_ref[g]
            sin_ref[g] = s_in
            y, s_out = _ssd_chunk(xbc_ref[:, xs], xbc_ref[:, bs], xbc_ref[:, cs], dtc_ref[g], dtr_ref[g], bc_ref[g], br_ref[g],
                                  alc_ref[g], alr_ref[g], s_in, d == 0)
            y_ref[:, xs] = y
            st_ref[g] = s_out

    ci = lambda d, t: _chunk_index(d, t, nctx, nc)
    out_specs = [
        pl.BlockSpec((None, Q, DI), lambda d, t: (d, ci(d, t), 0)),
        pl.BlockSpec((None, None, G, GW, N), lambda d, t: (d, ci(d, t), 0, 0, 0)),
    ]
    return _host_call(
        body, (2, nc), _ssd_in_specs(ci), out_specs, [S((2, T, DI), f32), S((2, nc, G, GW, N), f32)],
        [pltpu.VMEM((G, GW, N), f32)], ("arbitrary", "arbitrary"), name, (xbc, dtc, dtr, bc, br, alc, alr), hosted)


def ssd_scan_bwd(xbc, dtc, dtr, bc, br, alc, alr, s_in_all, dy, nctx, name, hosted=None):
    T = xbc.shape[0]
    nc = T // Q

    def body(xbc_ref, dtc_ref, dtr_ref, bc_ref, br_ref, alc_ref, alr_ref, sin_ref, dy_ref,
             dxbc_ref, ddtc_ref, ddtr_ref, dbc_ref, dbr_ref, dalc_ref, dalr_ref, ds_ref):
        d = pl.program_id(0)
        t = pl.program_id(1)

        @pl.when(t == 0)
        def _():
            ds_ref[...] = jnp.zeros_like(ds_ref)
            dbc_ref[...] = jnp.zeros_like(dbc_ref)
            dbr_ref[...] = jnp.zeros_like(dbr_ref)
            dalc_ref[...] = jnp.zeros_like(dalc_ref)
            dalr_ref[...] = jnp.zeros_like(dalr_ref)

        f = functools.partial(_ssd_chunk, is_fwd=(d == 0))
        for g in range(G):
            xs, bs, cs = _group_cols(g)
            _, vjp = jax.vjp(f, xbc_ref[:, xs], xbc_ref[:, bs], xbc_ref[:, cs], dtc_ref[g], dtr_ref[g], bc_ref[g], br_ref[g],
                             alc_ref[g], alr_ref[g], sin_ref[g])
            dx, dB, dC, ddtc, ddtr, dbc, dbr, dalc, dalr, ds = vjp((dy_ref[:, xs], ds_ref[g]))
            dxbc_ref[:, xs] = dx
            dxbc_ref[:, bs] = dB
            dxbc_ref[:, cs] = dC
            ddtc_ref[g] = ddtc
            ddtr_ref[g] = ddtr
            dbc_ref[g] += dbc
            dbr_ref[g] += dbr
            dalc_ref[g] += dalc
            dalr_ref[g] += dalr
            ds_ref[g] = ds

    ci = lambda d, t: _chunk_index(d, nc - 1 - t, nctx, nc)
    in_specs = _ssd_in_specs(ci) + [
        pl.BlockSpec((None, None, G, GW, N), lambda d, t: (d, ci(d, t), 0, 0, 0)),
        pl.BlockSpec((Q, DI), lambda d, t: (ci(d, t), 0)),
    ]
    small_c = pl.BlockSpec((None, G, 1, HPG), lambda d, t: (d, 0, 0, 0))
    small_r = pl.BlockSpec((None, G, HPG, 1), lambda d, t: (d, 0, 0, 0))
    out_specs = [
        pl.BlockSpec((None, Q, CONVD), lambda d, t: (d, ci(d, t), 0)),
        pl.BlockSpec((None, G, Q, HPG), lambda d, t: (d, 0, ci(d, t), 0)),
        pl.BlockSpec((None, G, HPG, Q), lambda d, t: (d, 0, 0, ci(d, t))),
        small_c, small_r, small_c, small_r,
    ]
    out_shape = [S((2, T, CONVD), f32), S((2, G, T, HPG), f32), S((2, G, HPG, T), f32),
                 S((2, G, 1, HPG), f32), S((2, G, HPG, 1), f32), S((2, G, 1, HPG), f32), S((2, G, HPG, 1), f32)]
    return _host_call(body, (2, nc), in_specs, out_specs, out_shape, [pltpu.VMEM((G, GW, N), f32)],
                      ("arbitrary", "arbitrary"), name, (xbc, dtc, dtr, bc, br, alc, alr, s_in_all, dy), hosted)


GTB = 128


def _gate_norm_f(yf, yb, x, z, dexp, w):
    y = (yf + yb + dexp * x) * (z * _sigmoid(z))
    return y * lax.rsqrt(jnp.mean(y * y, axis=-1, keepdims=True) + EPS) * w


def ssd_gate_fwd(y2, xbc, proj, dexp, w, nctxb, name):
    T = xbc.shape[0]
    L = T - nctxb * GTB

    def body(yf_ref, yb_ref, x_ref, z_ref, d_ref, w_ref, o_ref):
        o_ref[...] = _gate_norm_f(yf_ref[...], yb_ref[...], x_ref[...], z_ref[...], d_ref[...], w_ref[...]).astype(bf16)

    wide = pl.BlockSpec((GTB, DI), lambda i: (i + nctxb, 0))
    row = pl.BlockSpec((1, DI), lambda i: (0, 0))
    return pl.pallas_call(
        body, grid=(L // GTB,),
        in_specs=[pl.BlockSpec((None, GTB, DI), lambda i: (0, i + nctxb, 0)),
                  pl.BlockSpec((None, GTB, DI), lambda i: (1, i + nctxb, 0)), wide, wide, row, row],
        out_specs=pl.BlockSpec((GTB, DI), lambda i: (i, 0)), out_shape=S((L, DI), bf16),
        compiler_params=_cparams("parallel"), name=name)(y2, y2, xbc, proj, dexp, w)


def ssd_gate_bwd(y2, xbc, proj, dexp, w, dyn, nctxb, name, hosted=None):
    T = xbc.shape[0]
    nb = T // GTB

    def body(yf_ref, yb_ref, x_ref, z_ref, d_ref, w_ref, dyn_ref, dy_ref, dz_ref, dd_ref, dw_ref):
        i = pl.program_id(0)

        @pl.when(i == 0)
        def _():
            dd_ref[...] = jnp.zeros_like(dd_ref)
            dw_ref[...] = jnp.zeros_like(dw_ref)

        @pl.when(i < nctxb)
        def _():
            dy_ref[...] = jnp.zeros_like(dy_ref)
            dz_ref[...] = jnp.zeros_like(dz_ref)

        @pl.when(i >= nctxb)
        def _():
            _, vjp = jax.vjp(_gate_norm_f, yf_ref[...], yb_ref[...], x_ref[...], z_ref[...], d_ref[...], w_ref[...])
            dyf, _, _, dz, dd, dw = vjp(dyn_ref[...].astype(f32))
            dy_ref[...] = dyf
            dz_ref[...] = dz.astype(bf16)
            fold = (lax.broadcasted_iota(jnp.int32, (DI, 128), 0) // P == lax.broadcasted_iota(jnp.int32, (DI, 128), 1))
            dd_ref[...] += jnp.dot(dd, fold.astype(f32), precision=HI, preferred_element_type=f32)
            dw_ref[...] += dw

    wide = pl.BlockSpec((GTB, DI), lambda i: (i, 0))
    row = pl.BlockSpec((1, DI), lambda i: (0, 0))
    hrow = pl.BlockSpec((1, 128), lambda i: (0, 0))
    return _host_call(
        body, (nb,),
        [pl.BlockSpec((None, GTB, DI), lambda i: (0, i, 0)), pl.BlockSpec((None, GTB, DI), lambda i: (1, i, 0)),
         wide, wide, row, row, pl.BlockSpec((GTB, DI), lambda i: (jnp.maximum(i - nctxb, 0), 0))],
        [wide, wide, hrow, row],
        [S((T, DI), f32), S((T, DI), bf16), S((1, 128), f32), S((1, DI), f32)],
        [], ("arbitrary",), name, (y2, y2, xbc, proj, dexp, w, dyn), hosted)


CROWS = 2 * N_DEV


def mod_fwd(c16, modw, name):
    nl, _, cols = modw.shape

    def body(c_ref, w_ref, o_ref):
        cv = c_ref[...]
        s = cv * _sigmoid(cv)
        for l in range(nl):
            o_ref[l] = jnp.dot(s, w_ref[l], precision=HI, preferred_element_type=f32)

    return pl.pallas_call(body, in_specs=[VMEM, VMEM], out_specs=VMEM, out_shape=S((nl, CROWS, cols), f32),
                          compiler_params=pltpu.CompilerParams(vmem_limit_bytes=VMEM_LIMIT_BYTES), name=name)(c16, modw)


def mod_bwd(c16, modw, dm_sh, dm_all, name):
    nl, _, cols = modw.shape

    def body(c_ref, w_ref, dm_ref, dmall_ref, dw_ref, dc_ref, db_ref):
        cv = c_ref[...]
        sg = _sigmoid(cv)
        s = cv * sg
        ds_dc = sg * (1.0 + cv * (1.0 - sg))
        is_ctx = lax.broadcasted_iota(jnp.int32, (CROWS, D), 0) >= N_DEV
        dc = jnp.zeros((1, D), f32)
        for l in range(nl):
            dm = dm_ref[l]
            dw_ref[l] = lax.dot_general(s, dm, (((0,), (0,)), ((), ())), precision=HI, preferred_element_type=f32)
            dsv = lax.dot_general(dm, w_ref[l], (((1,), (1,)), ((), ())), precision=HI, preferred_element_type=f32)
            dc = dc + jnp.sum(jnp.where(is_ctx, dsv * ds_dc, 0.0), axis=0, keepdims=True)
            db_ref[pl.ds(l, 1), :] = jnp.sum(dmall_ref[l], axis=0, keepdims=True)
        dc_ref[...] = dc

    return pl.pallas_call(
        body, in_specs=[VMEM, VMEM, VMEM, VMEM], out_specs=[VMEM, VMEM, VMEM],
        out_shape=[S(modw.shape, f32), S((1, D), f32), S((nl, 6 * D), f32)],
        compiler_params=pltpu.CompilerParams(vmem_limit_bytes=VMEM_LIMIT_BYTES), name=name)(c16, modw, dm_sh, dm_all)


def adamw(w, g, m, v, name):
    R, C = w.shape
    rb = R if R <= 512 else max(r_ for r_ in range(8, 513, 8) if R % r_ == 0)
    bc1 = 1.0 - ADAM_B1 ** ADAM_STEP
    bc2 = 1.0 - ADAM_B2 ** ADAM_STEP

    def body(w_ref, g_ref, m_ref, v_ref, d_ref, nm_ref, nv_ref):
        gv = g_ref[...]
        m_new = ADAM_B1 * m_ref[...] + (1.0 - ADAM_B1) * gv
        v_new = ADAM_B2 * v_ref[...] + (1.0 - ADAM_B2) * (gv * gv)
        m_hat = m_new / bc1
        v_hat = v_new / bc2
        d_ref[...] = -ADAM_LR * (m_hat / (jnp.sqrt(v_hat) + ADAM_EPS) + ADAM_WD * w_ref[...])
        nm_ref[...] = m_new
        nv_ref[...] = v_new

    blk = pl.BlockSpec((rb, C), lambda i: (i, 0))
    return pl.pallas_call(body, grid=(R // rb,), in_specs=[blk] * 4, out_specs=[blk] * 3,
                          out_shape=[S((R, C), f32)] * 3, compiler_params=_cparams("parallel"), name=name)(w, g, m, v)


def _me():
    return lax.axis_index("x"), lax.axis_index("y"), lax.axis_index("c")


def allgather_small(x, name, with_sum=False, after=None):
    r, w = x.shape
    extra = () if after is None else (after,)

    def body(x_ref, *refs):
        refs = refs[len(extra):]
        if with_sum:
            out_ref, sum_ref, send_sems, recv_sems = refs
        else:
            out_ref, send_sems, recv_sems = refs
        mx, my, mc = _me()
        me = 4 * mx + 2 * my + mc
        out_ref[me] = x_ref[...]
        peers = []
        for k in range(1, N_DEV):
            kx, ky, kc = (k >> 2) & 1, (k >> 1) & 1, k & 1
            peers.append((mx + kx - 2 * mx * kx, my + ky - 2 * my * ky, mc + kc - 2 * mc * kc))
        copies = []
        for k, peer in enumerate(peers):
            cp = pltpu.make_async_remote_copy(src_ref=x_ref, dst_ref=out_ref.at[me], send_sem=send_sems.at[k],
                                              recv_sem=recv_sems.at[k], device_id=peer, device_id_type=MESH)
            cp.start()
            copies.append(cp)
        for k, (px, py, pc) in enumerate(peers):
            pltpu.make_async_remote_copy(src_ref=x_ref, dst_ref=out_ref.at[4 * px + 2 * py + pc], send_sem=send_sems.at[k],
                                         recv_sem=recv_sems.at[k], device_id=(px, py, pc), device_id_type=MESH).wait_recv()
        for cp in copies:
            cp.wait_send()
        if with_sum:
            acc = out_ref[0]
            for j in range(1, N_DEV):
                acc = acc + out_ref[j]
            sum_ref[...] = acc

    out_shape = [S((N_DEV, r, w), f32)] + ([S((r, w), f32)] if with_sum else [])
    outs = pl.pallas_call(
        body, in_specs=[VMEM] + [ANY] * len(extra), out_specs=[VMEM] * len(out_shape), out_shape=out_shape,
        scratch_shapes=[pltpu.SemaphoreType.DMA((N_DEV - 1,)), pltpu.SemaphoreType.DMA((N_DEV - 1,))],
        compiler_params=pltpu.CompilerParams(vmem_limit_bytes=VMEM_LIMIT_BYTES), name=name)(x, *extra)
    return outs if with_sum else outs[0]


def _tile2d(R, W, max_rows):
    if R <= max_rows:
        return R, W
    fits = [r_ for r_ in range(16, max_rows + 1, 16) if R % r_ == 0]
    return (max(fits), W) if fits else (R, 256)


def add_own(g, r, core, name, twice=False):
    _, _, R, W = g.shape
    rb, wb = _tile2d(R, W, 512)
    nout = 2 if twice else 1

    def body(core_ref, a_ref, b_ref, *o_refs):
        s = (a_ref[...].astype(f32) + b_ref[...].astype(f32)).astype(bf16)
        for o_ref in o_refs:
            o_ref[...] = s

    blk = pl.BlockSpec((None, rb, wb), lambda k, i, j, core_ref: (k, i, j))
    gs = pltpu.PrefetchScalarGridSpec(
        num_scalar_prefetch=1, grid=(4, R // rb, W // wb),
        in_specs=[pl.BlockSpec((None, None, rb, wb), lambda k, i, j, core_ref: (k, core_ref[0], i, j)), blk],
        out_specs=[blk] * nout)
    outs = pl.pallas_call(body, grid_spec=gs, out_shape=[S((4, R, W), bf16)] * nout,
                          compiler_params=_cparams("parallel", "parallel", "parallel"), name=name)(core, g, r)
    return tuple(outs) if twice else outs[0]


HBM_SPEC = pl.BlockSpec(memory_space=pltpu.HBM)
SEM_SPEC = pl.BlockSpec(memory_space=pltpu.SEMAPHORE)


def _chips_copy(p_ref, land_ref, send_sems, recv_sems, a, j):
    x, y, c = _me()
    px, py = [(1 - x, y), (x, 1 - y), (1 - x, 1 - y)][j]
    return pltpu.make_async_remote_copy(src_ref=p_ref.at[2 * px + py], dst_ref=land_ref.at[2 * x + y],
                                        send_sem=send_sems.at[3 * a + j], recv_sem=recv_sems.at[3 * a + j],
                                        device_id=(px, py, c), device_id_type=MESH)


def _chips_wait_copy(p_ref, land_ref, send_sems, recv_sems, a, j):
    x, y, c = _me()
    px, py = [(1 - x, y), (x, 1 - y), (1 - x, 1 - y)][j]
    return pltpu.make_async_remote_copy(src_ref=p_ref.at[2 * px + py], dst_ref=land_ref.at[2 * px + py],
                                        send_sem=send_sems.at[3 * a + j], recv_sem=recv_sems.at[3 * a + j],
                                        device_id=(px, py, c), device_id_type=MESH)


def chips_start(parts, lands, name):
    na = len(parts)

    def body(*refs):
        p_refs, land_refs = refs[:na], refs[na:2 * na]
        send_sems, recv_sems = refs[2 * na], refs[2 * na + 1]
        token = refs[-1]
        for a in range(na):
            for j in range(3):
                _chips_copy(p_refs[a], land_refs[a], send_sems, recv_sems, a, j).start()
        token[...] = jnp.zeros_like(token)

    arrs = list(parts) + list(lands)
    outs = pl.pallas_call(
        body, name=name, in_specs=[HBM_SPEC] * (2 * na),
        out_shape=[DMA((3 * na,)), DMA((3 * na,))] + [pltpu.HBM(t.shape, t.dtype) for t in arrs] + [S((8, 128), f32)],
        out_specs=[SEM_SPEC, SEM_SPEC] + [HBM_SPEC] * (2 * na) + [VMEM],
        input_output_aliases={k: 2 + k for k in range(2 * na)},
        compiler_params=pltpu.CompilerParams(has_side_effects=pltpu.SideEffectType.DATAFLOW_SIDE_EFFECTING),
    )(*[pltpu.with_memory_space_constraint(t, pltpu.HBM) for t in arrs])
    return outs[0], outs[1], list(outs[2:2 + na]), list(outs[2 + na:2 + 2 * na]), outs[-1]


def chips_wait(send_sems, recv_sems, parts, lands, after, name):
    na = len(parts)

    def body(*refs):
        p_refs, land_refs = refs[:na], refs[na:2 * na]
        ssem, rsem = refs[2 * na], refs[2 * na + 1]
        for a in range(na):
            for j in range(3):
                cp = _chips_wait_copy(p_refs[a], land_refs[a], ssem, rsem, a, j)
                cp.wait_send()
                cp.wait_recv()

    arrs = list(parts) + list(lands)
    outs = pl.pallas_call(
        body, name=name, in_specs=[HBM_SPEC] * (2 * na) + [SEM_SPEC, SEM_SPEC, ANY],
        out_shape=[pltpu.HBM(t.shape, t.dtype) for t in arrs], out_specs=[HBM_SPEC] * (2 * na),
        input_output_aliases={k: k for k in range(2 * na)},
        compiler_params=pltpu.CompilerParams(has_side_effects=pltpu.SideEffectType.DATAFLOW_SIDE_EFFECTING),
    )(*arrs, send_sems, recv_sems, after)
    return list(outs[na:])


def sum_adamw(recv, w, m, v, layer, name, into=None, after=None):
    _, R, W = recv.shape
    rb, wb = _tile2d(R, W, 256)
    bc1 = 1.0 - ADAM_B1 ** ADAM_STEP
    bc2 = 1.0 - ADAM_B2 ** ADAM_STEP
    n_into = 0 if into is None else 4
    extra = () if after is None else (after,)

    def body(r_ref, w_ref, m_ref, v_ref, *refs):
        g_ref, d_ref, nm_ref, nv_ref = refs[n_into + len(extra):]
        gv = r_ref[0].astype(f32)
        for k in range(1, 4):
            gv = gv + r_ref[k].astype(f32)
        m_new = ADAM_B1 * m_ref[...] + (1.0 - ADAM_B1) * gv
        v_new = ADAM_B2 * v_ref[...] + (1.0 - ADAM_B2) * (gv * gv)
        g_ref[...] = gv
        d_ref[...] = -ADAM_LR * ((m_new / bc1) / (jnp.sqrt(v_new / bc2) + ADAM_EPS) + ADAM_WD * w_ref[...])
        nm_ref[...] = m_new
        nv_ref[...] = v_new

    if layer is None:
        wblk = pl.BlockSpec((rb, wb), lambda i, j: (i, j))
        oshape = S((R, W), f32)
    else:
        wblk = pl.BlockSpec((None, rb, wb), lambda i, j: (layer, i, j))
        oshape = S(w.shape, f32)
    return pl.pallas_call(
        body, grid=(R // rb, W // wb),
        in_specs=[pl.BlockSpec((4, rb, wb), lambda i, j: (0, i, j)), wblk, wblk, wblk] + [ANY] * (n_into + len(extra)),
        out_specs=[wblk] * 4, out_shape=[oshape] * 4, input_output_aliases={4 + k: k for k in range(n_into)},
        compiler_params=_cparams("parallel", "parallel"), name=name)(recv, w, m, v, *(into or ()), *extra)


def sum_rows(a, name):
    K, R, W = a.shape
    rb = _pick(R, (512, 256, 128, 64, 32, 16))

    def body(a_ref, o_ref):
        acc = a_ref[0].astype(f32)
        for k in range(1, K):
            acc = acc + a_ref[k].astype(f32)
        o_ref[...] = acc

    return pl.pallas_call(body, grid=(R // rb,), in_specs=[pl.BlockSpec((K, rb, W), lambda i: (0, i, 0))],
                          out_specs=pl.BlockSpec((rb, W), lambda i: (i, 0)), out_shape=S((R, W), f32),
                          compiler_params=_cparams("parallel"), name=name)(a)


DMA = pltpu.SemaphoreType.DMA


class GatherExchange:
    def __init__(self, arrays):
        self.arrays = list(arrays)
        self.na = len(self.arrays)
        self.out_shape = [S((N_DEV,) + a.shape, a.dtype) for a in self.arrays]
        self.scratch = [DMA((7 * self.na,)), DMA((7 * self.na,)), DMA((self.na,))]

    def ops(self, x_refs, out_refs, sems):
        send_sems, recv_sems, local_sems = sems
        na = self.na
        x, y, c = _me()
        me, sibling = (x, y, c), (x, y, 1 - c)
        chips = [(1 - x, y), (x, 1 - y), (1 - x, 1 - y)]

        def rows(a, px, py, pc):
            return out_refs[a].at[4 * px + 2 * py + pc]

        def copy(a, k, block, to, src=None):
            return pltpu.make_async_remote_copy(
                src_ref=rows(a, *block) if src is None else src, dst_ref=rows(a, *block),
                send_sem=send_sems.at[7 * a + k], recv_sem=recv_sems.at[7 * a + k], device_id=to, device_id_type=MESH)

        def local(a):
            return pltpu.make_async_copy(x_refs[a], rows(a, *me), local_sems.at[a])

        def first(a):
            return [copy(a, 0, me, sibling, src=x_refs[a])] + [copy(a, 1 + j, me, (*chip, c), src=x_refs[a])
                                                                for j, chip in enumerate(chips)]

        def start():
            for a in range(na):
                local(a).start()
                for cp in first(a):
                    cp.start()

        def mid():
            for a in range(na):
                for j, chip in enumerate(chips):
                    copy(a, 1 + j, (*chip, c), me).wait_recv()
                    copy(a, 4 + j, (*chip, c), sibling).start()

        def finish():
            for a in range(na):
                copy(a, 0, sibling, me).wait_recv()
                for j, chip in enumerate(chips):
                    copy(a, 4 + j, (*chip, 1 - c), me).wait_recv()
                for cp in first(a) + [copy(a, 4 + j, (*chip, c), sibling) for j, chip in enumerate(chips)]:
                    cp.wait_send()
                local(a).wait()

        return start, mid, finish


class SiblingExchange:
    def __init__(self, arrays):
        self.arrays = list(arrays)
        self.na = len(self.arrays)
        self.out_shape = [S((4,) + g.shape[2:], g.dtype) for g in self.arrays]
        self.scratch = [DMA((self.na,)), DMA((self.na,))]

    def ops(self, g_refs, out_refs, sems):
        send_sems, recv_sems = sems
        x, y, c = _me()

        def copy(a):
            return pltpu.make_async_remote_copy(src_ref=g_refs[a].at[:, 1 - c], dst_ref=out_refs[a],
                                                send_sem=send_sems.at[a], recv_sem=recv_sems.at[a],
                                                device_id=(x, y, 1 - c), device_id_type=MESH)

        def start():
            for a in range(self.na):
                copy(a).start()

        def finish():
            for a in range(self.na):
                copy(a).wait()

        return start, None, finish


class ChipsExchange:
    def __init__(self, arrays):
        self.arrays = list(arrays)
        self.na = len(self.arrays)
        self.out_shape = [S(p.shape, p.dtype) for p in self.arrays]
        self.scratch = [DMA((3 * self.na,)), DMA((3 * self.na,)), DMA((self.na,))]

    def ops(self, p_refs, out_refs, sems):
        send_sems, recv_sems, local_sems = sems
        x, y, c = _me()
        mine = 2 * x + y
        chips = [(1 - x, y), (x, 1 - y), (1 - x, 1 - y)]

        def local(a):
            return pltpu.make_async_copy(p_refs[a].at[mine], out_refs[a].at[mine], local_sems.at[a])

        def send(a, j):
            px, py = chips[j]
            return pltpu.make_async_remote_copy(src_ref=p_refs[a].at[2 * px + py], dst_ref=out_refs[a].at[mine],
                                                send_sem=send_sems.at[3 * a + j], recv_sem=recv_sems.at[3 * a + j],
                                                device_id=(px, py, c), device_id_type=MESH)

        def recv(a, j):
            px, py = chips[j]
            return pltpu.make_async_remote_copy(src_ref=p_refs[a].at[mine], dst_ref=out_refs[a].at[2 * px + py],
                                                send_sem=send_sems.at[3 * a + j], recv_sem=recv_sems.at[3 * a + j],
                                                device_id=(px, py, c), device_id_type=MESH)

        def start():
            for a in range(self.na):
                local(a).start()
                for j in range(3):
                    send(a, j).start()

        def finish():
            for a in range(self.na):
                for j in range(3):
                    recv(a, j).wait_recv()
                for j in range(3):
                    send(a, j).wait_send()
                local(a).wait()

        return start, None, finish


def exchange(ex, name):
    na = ex.na

    def body(*refs):
        start, mid, finish = ex.ops(refs[:na], refs[na:2 * na], refs[2 * na:])
        start()
        if mid is not None:
            mid()
        finish()

    return pl.pallas_call(body, in_specs=[ANY] * na, out_specs=[ANY] * na, out_shape=ex.out_shape,
                          scratch_shapes=ex.scratch, name=name)(*ex.arrays)


def _host_call(body, grid, in_specs, out_specs, out_shape, scratch_shapes, sem, name, args, hosted):
    if hosted is None:
        res = pl.pallas_call(body, grid=grid, in_specs=in_specs, out_specs=out_specs, out_shape=out_shape,
                             scratch_shapes=scratch_shapes, compiler_params=_cparams(*sem), name=name)(*args)
        return res, None
    n_in, n_out, n_sc, na = len(in_specs), len(out_shape), len(scratch_shapes), hosted.na
    nsteps = 1
    for g_ in grid:
        nsteps *= g_
    mid_step = (3 * nsteps) // 4
    i1 = n_in + na
    i2 = i1 + n_out
    i3 = i2 + na
    i4 = i3 + n_sc

    def wrapped(*refs):
        step = pl.program_id(0)
        for ax in range(1, len(grid)):
            step = step * grid[ax] + pl.program_id(ax)
        start, mid, finish = hosted.ops(refs[n_in:i1], refs[i2:i3], refs[i4:])
        pl.when(step == 0)(start)
        if mid is not None:
            pl.when(step == mid_step)(mid)
        body(*refs[:n_in], *refs[i1:i2], *refs[i3:i4])
        pl.when(step == nsteps - 1)(finish)

    res = pl.pallas_call(
        wrapped, grid=grid, in_specs=list(in_specs) + [ANY] * na, out_specs=list(out_specs) + [ANY] * na,
        out_shape=list(out_shape) + hosted.out_shape, scratch_shapes=list(scratch_shapes) + hosted.scratch,
        compiler_params=_cparams(*(("arbitrary",) * len(grid))), name=name)(*args, *hosted.arrays)
    return res[:n_out], res[n_out:]


PACK_ALIGN = 16 * PACK_W


def _pad_to(v, mult):
    n = v.shape[-1]
    extra = (-n) % mult
    if extra == 0:
        return v
    return jnp.concatenate([v, jnp.zeros(v.shape[:-1] + (extra,), v.dtype)], axis=-1)


def _f32_as_bf16_pairs(v):
    return lax.bitcast_convert_type(v.reshape(-1), bf16).reshape(-1)


def _bf16_pairs_as_f32(v):
    return lax.bitcast_convert_type(v.reshape(v.shape[:-1] + (v.shape[-1] // 2, 2)), f32)


def _col_shards(gw):
    lead = gw.shape[:-1]
    n = gw.shape[-1] // N_DEV
    t = gw.reshape(lead + (N_DEV, n))
    t = jnp.moveaxis(t, -2, 0)
    return t.reshape(N_DEV, -1)


def kernel(x, c, ctx, c_ctx, mod_w, mod_b, norm1_w, norm2_w, ssd_w_in, ssd_conv_w, ssd_conv_b, ssd_dt_bias, ssd_a_log, ssd_d, ssd_norm_w, ssd_w_out, conf_w_pw1, conf_b_pw1, conf_w_dw, conf_b_dw, conf_ln_w, conf_ln_b, conf_w_pw2, conf_b_pw2, ffn_w_up, ffn_conv_w, ffn_conv_b, ffn_w_down, final_norm_w, loss_target, m_c_ctx, m_mod_w, m_mod_b, m_norm1_w, m_norm2_w, m_ssd_w_in, m_ssd_conv_w, m_ssd_conv_b, m_ssd_dt_bias, m_ssd_a_log, m_ssd_d, m_ssd_norm_w, m_ssd_w_out, m_conf_w_pw1, m_conf_b_pw1, m_conf_w_dw, m_conf_b_dw, m_conf_ln_w, m_conf_ln_b, m_conf_w_pw2, m_conf_b_pw2, m_ffn_w_up, m_ffn_conv_w, m_ffn_conv_b, m_ffn_w_down, m_final_norm_w, v_c_ctx, v_mod_w, v_mod_b, v_norm1_w, v_norm2_w, v_ssd_w_in, v_ssd_conv_w, v_ssd_conv_b, v_ssd_dt_bias, v_ssd_a_log, v_ssd_d, v_ssd_norm_w, v_ssd_w_out, v_conf_w_pw1, v_conf_b_pw1, v_conf_w_dw, v_conf_b_dw, v_conf_ln_w, v_conf_ln_b, v_conf_w_pw2, v_conf_b_pw2, v_ffn_w_up, v_ffn_conv_w, v_ffn_conv_b, v_ffn_w_down, v_final_norm_w):
    mx, my, mc = _me()
    me = 4 * mx + 2 * my + mc
    L = x.shape[1]
    LC = ctx.shape[1]
    T = LC + L
    w_in_cols = ssd_w_in.shape[2] * N_DEV
    n_dt = w_in_cols - DI - CONVD

    small = [c[0], ssd_conv_w[0], conf_b_pw1[0], conf_w_dw[0], conf_b_dw[0], conf_ln_w[0], conf_ln_b[0], conf_b_pw2[0],
             ffn_conv_w]
    parts = [_f32_as_bf16_pairs(t) for t in small]
    sizes = [p.shape[0] for p in parts]
    small_flat = _pad_to(jnp.concatenate(parts), PACK_ALIGN).reshape(-1, PACK_W)
    w_in, small_g = exchange(GatherExchange([ssd_w_in[0].astype(bf16), small_flat]), "gather_first")
    gather_in_proj = GatherExchange([ssd_w_out[0].astype(bf16), conf_w_pw2[0].astype(bf16)])
    gather_in_conv = GatherExchange([ffn_w_down[0].astype(bf16), conf_w_pw1[0].astype(bf16)])
    gather_in_scan = GatherExchange([ffn_w_up[0].astype(bf16), ffn_w_up[1].astype(bf16)])
    gather_in_gate = GatherExchange([ffn_w_down[1].astype(bf16)])
    w_up, w_down = [None, None], [None, None]
    small_g = small_g.reshape(N_DEV, -1)
    offs = [0]
    for s_ in sizes:
        offs.append(offs[-1] + s_)
    sm = [_bf16_pairs_as_f32(small_g[:, offs[i]:offs[i + 1]]) for i in range(len(sizes))]

    def cols(pc, K):
        return jnp.moveaxis(pc.reshape(N_DEV, K, -1), 0, 1).reshape(K, -1)

    c_all = sm[0]
    conv_w5 = cols(sm[1], 5)
    b_pw1 = sm[2].reshape(1, 2 * D)
    w_dw = cols(sm[3], CONF_K)
    b_dw, ln_w, ln_b, b_pw2 = (sm[i].reshape(1, D) for i in (4, 5, 6, 7))
    fcw = sm[8].reshape(N_DEV, 2, 9, FH // N_DEV)
    ffn_cw = [cols(fcw[:, i].reshape(N_DEV, -1), 9) for i in range(2)]
    in_segs = (DI, CONVD, n_dt)
    up_segs = (FH, FH)
    pw1_segs = (D, D)

    c16 = jnp.concatenate([c_all, jnp.broadcast_to(c_ctx[None, :], (N_DEV, D))], axis=0)
    m_sh = mod_fwd(c16, mod_w, "mod_fwd")
    mod_cols = mod_w.shape[2]
    m_all = allgather_small(m_sh.reshape(2 * CROWS, mod_cols), "gather_mod")
    m_all = jnp.moveaxis(m_all.reshape(N_DEV, 2, CROWS, mod_cols), 0, 2).reshape(2, CROWS, 6 * D) + mod_b[:, None, :]
    m_lat = lax.dynamic_index_in_dim(m_all, me, axis=1, keepdims=False).reshape(2, 6, 1, D)
    m_ctx = m_all[:, N_DEV].reshape(2, 6, 1, D)
    zero_row = jnp.zeros((1, D), f32)

    def ffn_fwd(h, i, tag, hosted=None):
        a2 = modnorm_fwd(h, norm2_w[i][None], m_lat[i, 4][None], m_lat[i, 3][None], 0, f"ffn{tag}_norm")
        val, gate = smm_fwd(a2, w_up[i], None, up_segs, f"ffn{tag}_up")
        act, gs_, gvds, extra = ffn_gate_fwd(val, gate, ffn_cw[i], ffn_conv_b[i][None], f"ffn{tag}_gate", hosted)
        o2 = matmul(act, w_down[i], "nn", f32, f"ffn{tag}_down")
        h_new = resgate_fwd(h, o2, m_lat[i, 5], zero_row, f"ffn{tag}_res")
        return h_new, (a2, gate, gs_, gvds, act, o2), extra

    def ffn_bwd(dh, h, i, saved, tag):
        a2, gate, gs_, gvds, act, o2 = saved
        do2, dg2, _ = resgate_bwd(dh, o2, m_lat[i, 5], zero_row, f"ffn{tag}_res_bwd")
        g_down = matmul(act, do2, "tn", bf16, f"ffn{tag}_down_dw")
        dact = matmul(do2, w_down[i], "nt", bf16, f"ffn{tag}_down_dx")
        dval, dgate, dcw, dcb = ffn_gate_bwd(gate, gs_, gvds, ffn_cw[i], dact, f"ffn{tag}_gate_bwd")
        g_up = smm_dw(a2, [dval, dgate], FH // 4, up_segs, 2, True, f"ffn{tag}_up_dw")
        da2, _ = smm_dx([dval, dgate], w_up[i], None, up_segs, bf16, f"ffn{tag}_up_dx")
        dh_in, dn2, dsc2, dsh2 = modnorm_bwd(h, norm2_w[i][None], m_lat[i, 4][None], m_lat[i, 3][None], da2, dh, 0,
                                             f"ffn{tag}_norm_bwd")
        return dh_in, dict(w_up=g_up, w_down=g_down, conv_w=dcw, conv_b=dcb, norm2=dn2, sh2=dsh2[0], sc2=dsc2[0], g2=dg2)

    nctx = LC // Q
    hx = x[0]
    sc0 = jnp.stack([m_ctx[0, 1], m_lat[0, 1]])
    sh0 = jnp.stack([m_ctx[0, 0], m_lat[0, 0]])
    a0 = modnorm_fwd(hx, norm1_w[0][None], sc0, sh0, LC // TB, "ssd_norm", ctx=ctx[0])
    (z, xbc_pre, dt_raw), (w_out_g, w_pw2_g) = smm_fwd(a0, w_in, None, in_segs, "ssd_in", gather_in_proj)
    w_out = w_out_g.reshape(DI, D)
    w_pw2 = w_pw2_g.reshape(D, D)
    segs = ((0, LC), (LC, L))
    xbc, xbc_dsilu, (w_down0_g, w_pw1) = ssd_conv_fwd(xbc_pre, conv_w5, ssd_conv_b, segs, "ssd_conv", gather_in_conv)
    w_down[0] = w_down0_g.reshape(FH, D)
    dt4 = dt_raw[:, :n_dt].reshape(T, 2, G, HPG)
    dtc = jnp.transpose(dt4, (1, 2, 0, 3))
    dtr = jnp.transpose(dt4, (1, 2, 3, 0))
    bias3 = ssd_dt_bias[0].reshape(2, G, HPG)
    alog3 = ssd_a_log[0].reshape(2, G, HPG)
    bc_, br_ = bias3[:, :, None, :], bias3[:, :, :, None]
    alc, alr = alog3[:, :, None, :], alog3[:, :, :, None]
    (y2, s_in_all), (w_up[0], w_up[1]) = ssd_scan_fwd(xbc, dtc, dtr, bc_, br_, alc, alr, nctx, "ssd_scan", gather_in_scan)
    dexp = jnp.repeat(ssd_d[0], P)[None, :]
    yn = ssd_gate_fwd(y2, xbc, z, dexp, ssd_norm_w, LC // GTB, "ssd_gate")
    o_ssd = matmul(yn, w_out, "nn", f32, "ssd_out")
    h1 = resgate_fwd(hx, o_ssd, m_lat[0, 2], zero_row, "ssd_res")
    h2, ffn0_saved, (w_down1_g,) = ffn_fwd(h1, 0, "0", gather_in_gate)
    w_down[1] = w_down1_g.reshape(FH, D)

    a1 = modnorm_fwd(h2, norm1_w[1][None], m_lat[1, 1][None], m_lat[1, 0][None], 0, "conf_norm")
    pa, pg = smm_fwd(a1, w_pw1, None, pw1_segs, "conf_pw1")
    dwc, _ = conf_glu_conv_fwd(pa, pg, b_pw1, w_dw, b_dw, "conf_conv")
    s1 = ln_silu_fwd(dwc, ln_w, ln_b, "conf_ln")
    o_conf = matmul(s1, w_pw2, "nn", f32, "conf_pw2")
    h3 = resgate_fwd(h2, o_conf, m_lat[1, 2], b_pw2, "conf_res")
    h4, ffn1_saved, _ = ffn_fwd(h3, 1, "1")

    loss_part, dh4, g_final = final_loss(h4, final_norm_w[None], loss_target[0], "loss_head")
    dh3, gf1 = ffn_bwd(dh4, h3, 1, ffn1_saved, "1")

    do_conf, dg1_1, g_b_pw2 = resgate_bwd(dh3, o_conf, m_lat[1, 2], b_pw2, "conf_res_bwd")
    g_pw2 = matmul(s1, do_conf, "tn", bf16, "conf_pw2_dw")
    ds1 = matmul(do_conf, w_pw2, "nt", bf16, "conf_pw2_dx")
    ddwc, g_ln_w, g_ln_b = ln_silu_bwd(dwc, ln_w, ln_b, ds1, "conf_ln_bwd")
    dpa, dpg, dba, dbg, g_w_dw, g_b_dw = conf_glu_conv_bwd(pa, pg, b_pw1, w_dw, ddwc, "conf_conv_bwd")
    g_b_pw1 = jnp.concatenate([dba, dbg], axis=1)
    g_pw1 = smm_dw(a1, [dpa, dpg], 2 * D // N_DEV, pw1_segs, 1, False, "conf_pw1_dw")
    da1, _ = smm_dx([dpa, dpg], w_pw1, None, pw1_segs, bf16, "conf_pw1_dx")
    dh2, g_n1_1, dsc1_1, dsh1_1 = modnorm_bwd(h2, norm1_w[1][None], m_lat[1, 1][None], m_lat[1, 0][None], da1, dh3, 0,
                                              "conf_norm_bwd")
    dh1, gf0 = ffn_bwd(dh2, h1, 0, ffn0_saved, "0")

    do_ssd, dg1_0, _ = resgate_bwd(dh1, o_ssd, m_lat[0, 2], zero_row, "ssd_res_bwd")
    g_w_out = matmul(yn, do_ssd, "tn", bf16, "ssd_out_dw")
    dyn = matmul(do_ssd, w_out, "nt", bf16, "ssd_out_dx")
    core = mc.reshape(1).astype(jnp.int32)

    def by_device(t):
        return t.reshape((4, 2, -1, t.shape[-1]))

    early = [by_device(t) for t in (gf1["w_up"], gf1["w_down"], g_pw2, g_pw1, gf0["w_up"], gf0["w_down"], g_w_out)]
    (dy, dz, g_dexp, g_ssd_norm), early_sib = ssd_gate_bwd(
        y2, xbc, z, dexp, ssd_norm_w, dyn, LC // GTB, "ssd_gate_bwd", SiblingExchange(early))
    early_part = [add_own(t, r_, core, f"reduce_add{i}") for i, (t, r_) in enumerate(zip(early, early_sib))]
    (dxbc2, ddtc, ddtr, dbc, dbr, dalc, dalr), early_red = ssd_scan_bwd(
        xbc, dtc, dtr, bc_, br_, alc, alr, s_in_all, dy, nctx, "ssd_scan_bwd", ChipsExchange(early_part))
    ddt = (jnp.transpose(ddtc, (2, 0, 1, 3)) + jnp.transpose(ddtr, (3, 0, 1, 2))).reshape(T, n_dt)
    g_dt_bias = (dbc[:, :, 0, :] + dbr[:, :, :, 0]).reshape(2, NH_SSD)
    g_a_log = (dalc[:, :, 0, :] + dalr[:, :, :, 0]).reshape(2, NH_SSD)
    g_ssd_d = g_dexp[0, :NH_SSD]
    du, g_conv_w5, g_conv_b5 = ssd_conv_bwd(xbc_pre, conv_w5, xbc_dsilu, dxbc2, dy, dexp, segs, "ssd_conv_bwd")
    ddt_p = _pad_to(ddt, 128).astype(bf16)
    g_w_in = smm_dw(a0, [dz, du, ddt_p], w_in.shape[-1], in_segs, 2, True, "ssd_in_dw")
    g_ffn_cw = jnp.stack([gf0["conv_w"], gf1["conv_w"]])
    small_shards = [_col_shards(t) for t in (g_conv_w5, g_b_pw1, g_w_dw, g_b_dw, g_ln_w, g_ln_b, g_b_pw2, g_ffn_cw)]
    gsizes = [s_.shape[1] for s_ in small_shards]
    g_small = _pad_to(jnp.concatenate(small_shards, axis=1), PACK_ALIGN).astype(bf16)
    late = [by_device(g_w_in), by_device(g_small.reshape(N_DEV, -1, PACK_W))]
    da0, late_sib = smm_dx([dz, du, ddt_p], w_in, None, in_segs, f32, "ssd_in_dx", SiblingExchange(late))
    late_part = [add_own(t, r_, core, f"reduce_add_late{i}", twice=True) for i, (t, r_) in enumerate(zip(late, late_sib))]
    late_flying = chips_start([p_[0] for p_ in late_part], [p_[1] for p_ in late_part], "reduce_chips_late_start")
    dh0, g_n1_0, dsc1_0, dsh1_0 = modnorm_bwd(hx, norm1_w[0][None], sc0, sh0, da0, dh1, LC // TB, "ssd_norm_bwd",
                                              ctx=ctx[0])
    grad_x = dh0[None]

    r_up1, r_down1, r_pw2, r_pw1, r_up0, r_down0, r_out = early_red
    big = {}
    def tr(t):
        return jnp.swapaxes(t, -1, -2)

    up_t, m_up_t, v_up_t = tr(ffn_w_up), tr(m_ffn_w_up), tr(v_ffn_w_up)
    send_sems, recv_sems, late_p, late_land, token = late_flying
    up0 = sum_adamw(r_up0, up_t, m_up_t, v_up_t, 0, "adamw_ffn_w_up0", after=token)
    up1 = sum_adamw(r_up1, up_t, m_up_t, v_up_t, 1, "adamw_ffn_w_up1", into=up0)
    big["ffn_w_up"] = tuple(tr(t) for t in up1)
    big["conf_w_pw1"] = sum_adamw(r_pw1, conf_w_pw1[0], m_conf_w_pw1[0], v_conf_w_pw1[0], None, "adamw_conf_w_pw1",
                                  after=up1[0])
    big["ssd_w_out"] = sum_adamw(r_out, ssd_w_out[0], m_ssd_w_out[0], v_ssd_w_out[0], None, "adamw_ssd_w_out",
                                 after=big["conf_w_pw1"][0])
    dn0 = sum_adamw(r_down0, ffn_w_down, m_ffn_w_down, v_ffn_w_down, 0, "adamw_ffn_w_down0", after=big["ssd_w_out"][0])
    big["ffn_w_down"] = sum_adamw(r_down1, ffn_w_down, m_ffn_w_down, v_ffn_w_down, 1, "adamw_ffn_w_down1", into=dn0)
    big["conf_w_pw2"] = sum_adamw(r_pw2, conf_w_pw2[0], m_conf_w_pw2[0], v_conf_w_pw2[0], None, "adamw_conf_w_pw2",
                                  after=big["ffn_w_down"][0])

    zeros_d = jnp.zeros((1, D), f32)
    dm_lat = jnp.stack([
        jnp.concatenate([dsh1_0[1], dsc1_0[1], dg1_0, gf0["sh2"], gf0["sc2"], gf0["g2"]], axis=1),
        jnp.concatenate([dsh1_1[0], dsc1_1[0], dg1_1, gf1["sh2"], gf1["sc2"], gf1["g2"]], axis=1)])
    dm_ctx = jnp.stack([
        jnp.concatenate([dsh1_0[0], dsc1_0[0]] + [zeros_d] * 4, axis=1), jnp.zeros((1, 6 * D), f32)])
    dm_mine = jnp.concatenate([dm_lat.reshape(2, 6 * D), dm_ctx.reshape(2, 6 * D),
                               jnp.zeros((4, 6 * D), f32)], axis=0)
    dm_g = allgather_small(dm_mine, "gather_dmod", after=big["conf_w_pw2"][0])
    dm_all = jnp.concatenate([jnp.moveaxis(dm_g[:, 0:2], 0, 1), jnp.moveaxis(dm_g[:, 2:4], 0, 1)], axis=1)
    dm_sh = lax.dynamic_slice_in_dim(dm_all, me * mod_cols, mod_cols, axis=2)
    g_mod_w, g_cctx_part, g_mod_b = mod_bwd(c16, mod_w, dm_sh, dm_all, "mod_bwd")

    rep = [jnp.stack([g_n1_0[0], g_n1_1[0]]), jnp.stack([gf0["norm2"][0], gf1["norm2"][0]]), g_conv_b5, g_dt_bias, g_a_log,
           g_ssd_d, g_ssd_norm, jnp.stack([gf0["conv_b"][0], gf1["conv_b"][0]]), g_final, g_cctx_part, loss_part[:, :1]]
    rep_sizes = [r_.size for r_ in rep]
    rep_flat = _pad_to(jnp.concatenate([r_.reshape(-1) for r_ in rep]), 8 * PACK_W).reshape(-1, PACK_W)
    _, rep_sum = allgather_small(rep_flat, "reduce_replicated", with_sum=True)
    rep_sum = rep_sum.reshape(-1)
    roffs = [0]
    for s_ in rep_sizes:
        roffs.append(roffs[-1] + s_)
    rp = [rep_sum[roffs[i]:roffs[i + 1]] for i in range(len(rep_sizes))]
    loss = rp[10].reshape(())

    r_in, r_small = chips_wait(send_sems, recv_sems, late_p, late_land, rep_sum, "reduce_chips_late_wait")
    w_in_res = sum_adamw(r_in, tr(ssd_w_in[0]), tr(m_ssd_w_in[0]), tr(v_ssd_w_in[0]), None, "adamw_ssd_w_in")
    big["ssd_w_in"] = tuple(tr(t) for t in w_in_res)
    g_flat = sum_rows(r_small, "reduce_sum_small").reshape(-1)
    goffs = [0]
    for s_ in gsizes:
        goffs.append(goffs[-1] + s_)
    gs = [g_flat[goffs[i]:goffs[i + 1]] for i in range(len(gsizes))]
    grads = {
        "c_ctx": rp[9], "mod_w": g_mod_w, "mod_b": g_mod_b, "norm1_w": rp[0], "norm2_w": rp[1],
        "ssd_conv_w": gs[0], "ssd_conv_b": rp[2], "ssd_dt_bias": rp[3], "ssd_a_log": rp[4], "ssd_d": rp[5],
        "ssd_norm_w": rp[6], "conf_b_pw1": gs[1], "conf_w_dw": gs[2],
        "conf_b_dw": gs[3], "conf_ln_w": gs[4], "conf_ln_b": gs[5], "conf_b_pw2": gs[6],
        "ffn_conv_w": gs[7], "ffn_conv_b": rp[7], "final_norm_w": rp[8],
    }
    weights = dict(c_ctx=c_ctx, mod_w=mod_w, mod_b=mod_b, norm1_w=norm1_w, norm2_w=norm2_w, ssd_w_in=ssd_w_in, ssd_conv_w=ssd_conv_w, ssd_conv_b=ssd_conv_b, ssd_dt_bias=ssd_dt_bias, ssd_a_log=ssd_a_log, ssd_d=ssd_d, ssd_norm_w=ssd_norm_w, ssd_w_out=ssd_w_out, conf_w_pw1=conf_w_pw1, conf_b_pw1=conf_b_pw1, conf_w_dw=conf_w_dw, conf_b_dw=conf_b_dw, conf_ln_w=conf_ln_w, conf_ln_b=conf_ln_b, conf_w_pw2=conf_w_pw2, conf_b_pw2=conf_b_pw2, ffn_w_up=ffn_w_up, ffn_conv_w=ffn_conv_w, ffn_conv_b=ffn_conv_b, ffn_w_down=ffn_w_down, final_norm_w=final_norm_w)
    m_in = dict(c_ctx=m_c_ctx, mod_w=m_mod_w, mod_b=m_mod_b, norm1_w=m_norm1_w, norm2_w=m_norm2_w, ssd_w_in=m_ssd_w_in, ssd_conv_w=m_ssd_conv_w, ssd_conv_b=m_ssd_conv_b, ssd_dt_bias=m_ssd_dt_bias, ssd_a_log=m_ssd_a_log, ssd_d=m_ssd_d, ssd_norm_w=m_ssd_norm_w, ssd_w_out=m_ssd_w_out, conf_w_pw1=m_conf_w_pw1, conf_b_pw1=m_conf_b_pw1, conf_w_dw=m_conf_w_dw, conf_b_dw=m_conf_b_dw, conf_ln_w=m_conf_ln_w, conf_ln_b=m_conf_ln_b, conf_w_pw2=m_conf_w_pw2, conf_b_pw2=m_conf_b_pw2, ffn_w_up=m_ffn_w_up, ffn_conv_w=m_ffn_conv_w, ffn_conv_b=m_ffn_conv_b, ffn_w_down=m_ffn_w_down, final_norm_w=m_final_norm_w)
    v_in = dict(c_ctx=v_c_ctx, mod_w=v_mod_w, mod_b=v_mod_b, norm1_w=v_norm1_w, norm2_w=v_norm2_w, ssd_w_in=v_ssd_w_in, ssd_conv_w=v_ssd_conv_w, ssd_conv_b=v_ssd_conv_b, ssd_dt_bias=v_ssd_dt_bias, ssd_a_log=v_ssd_a_log, ssd_d=v_ssd_d, ssd_norm_w=v_ssd_norm_w, ssd_w_out=v_ssd_w_out, conf_w_pw1=v_conf_w_pw1, conf_b_pw1=v_conf_b_pw1, conf_w_dw=v_conf_w_dw, conf_b_dw=v_conf_b_dw, conf_ln_w=v_conf_ln_w, conf_ln_b=v_conf_ln_b, conf_w_pw2=v_conf_w_pw2, conf_b_pw2=v_conf_b_pw2, ffn_w_up=v_ffn_w_up, ffn_conv_w=v_ffn_conv_w, ffn_conv_b=v_ffn_conv_b, ffn_w_down=v_ffn_w_down, final_norm_w=v_final_norm_w)

    out_g, out_d, out_m, out_v = [], [], [], []
    for name_, w_ in weights.items():
        shape = w_.shape
        if name_ in big:
            for lst, t in zip((out_g, out_d, out_m, out_v), big[name_]):
                lst.append(t.reshape(shape))
            continue
        cols2 = shape[-1] if len(shape) > 1 else shape[0]
        g2 = grads[name_].reshape(-1, cols2)
        d_, nm_, nv_ = adamw(w_.reshape(-1, cols2), g2, m_in[name_].reshape(-1, cols2), v_in[name_].reshape(-1, cols2),
                             f"adamw_{name_}")
        out_g.append(g2.reshape(shape))
        out_d.append(d_.reshape(shape))
        out_m.append(nm_.reshape(shape))
        out_v.append(nv_.reshape(shape))
    return (loss, grad_x, *out_g, *out_d, *out_m, *out_v)
```

```python
import functools

import jax
import jax.numpy as jnp
from jax import lax
from jax.experimental import pallas as pl
from jax.experimental.pallas import tpu as pltpu

f32 = jnp.float32
bf16 = jnp.bfloat16
HI = lax.Precision.HIGHEST
S = jax.ShapeDtypeStruct
MESH = pl.DeviceIdType.MESH
ANY = pl.BlockSpec(memory_space=pl.ANY)
VMEM = pl.BlockSpec(memory_space=pltpu.VMEM)

N_DEV = 8
D = 1024
DI = 2048
CONVD = 4096
FH = 2816
GRID_W = 64
Q = 128
HPG = 4
P = 64
N = 128
G = 8
GW = HPG * P
NH_SSD = G * HPG
EPS = 1e-6
ADAM_LR, ADAM_B1, ADAM_B2, ADAM_EPS, ADAM_WD, ADAM_STEP = 0.001, 0.9, 0.999, 1e-08, 0.01, 10
VMEM_LIMIT_BYTES = 56 * 1024 * 1024
PACK_W = 1024
TB = 256


def _cparams(*sem):
    return pltpu.CompilerParams(dimension_semantics=sem, vmem_limit_bytes=VMEM_LIMIT_BYTES)


def _pick(n, prefs):
    for p in prefs:
        if n % p == 0:
            return p
    return n


def _sigmoid(x):
    return 1.0 / (1.0 + jnp.exp(-x))


def _softplus(x):
    return jnp.maximum(x, 0.0) + jnp.log(1.0 + jnp.exp(-jnp.abs(x)))


def matmul(a, b, mode, out_dtype, name):
    if mode == "nn":
        (M, K), (_, Nn) = a.shape, b.shape
        bm, bn, bk = _pick(M, (512, 384, 256, 128)), Nn, K
    elif mode == "tn":
        (K, M), (_, Nn) = a.shape, b.shape
        bm, bn, bk = M, Nn, _pick(K, (256, 128))
    else:
        (M, K), (Nn, _) = a.shape, b.shape
        bm, bn, bk = _pick(M, (512, 384, 256, 128)), Nn, K
    nk = K // bk
    dims = {"nn": (((1,), (0,)), ((), ())), "tn": (((0,), (0,)), ((), ())), "nt": (((1,), (1,)), ((), ()))}[mode]

    def body(a_ref, b_ref, o_ref, acc_ref):
        k = pl.program_id(2)

        @pl.when(k == 0)
        def _():
            acc_ref[...] = jnp.zeros_like(acc_ref)

        acc_ref[...] += lax.dot_general(a_ref[...].astype(bf16), b_ref[...].astype(bf16), dims,
                                        preferred_element_type=f32)

        @pl.when(k == nk - 1)
        def _():
            o_ref[...] = acc_ref[...].astype(out_dtype)

    if mode == "nn":
        a_spec = pl.BlockSpec((bm, bk), lambda i, j, k: (i, k))
        b_spec = pl.BlockSpec((bk, bn), lambda i, j, k: (k, j))
    elif mode == "tn":
        a_spec = pl.BlockSpec((bk, bm), lambda i, j, k: (k, i))
        b_spec = pl.BlockSpec((bk, bn), lambda i, j, k: (k, j))
    else:
        a_spec = pl.BlockSpec((bm, bk), lambda i, j, k: (i, k))
        b_spec = pl.BlockSpec((bn, bk), lambda i, j, k: (j, k))
    return pl.pallas_call(
        body, grid=(M // bm, Nn // bn, nk), in_specs=[a_spec, b_spec],
        out_specs=pl.BlockSpec((bm, bn), lambda i, j, k: (i, j)),
        out_shape=S((M, Nn), out_dtype), scratch_shapes=[pltpu.VMEM((bm, bn), f32)],
        compiler_params=_cparams("parallel", "parallel", "arbitrary"), name=name,
    )(a, b)


SMM_BM = 256
SMM_ROWS = (256,)


def _shard_pieces(seg_widths, n):
    bounds = [0]
    for sw in seg_widths:
        bounds.append(bounds[-1] + sw)
    assert bounds[-1] == N_DEV * n, (seg_widths, n)
    out = []
    for j in range(N_DEV):
        lo, hi = j * n, (j + 1) * n
        pcs = []
        for si in range(len(seg_widths)):
            a, b = max(lo, bounds[si]), min(hi, bounds[si + 1])
            if a < b:
                pcs.append((si, a - bounds[si], a - lo, b - a))
        out.append(pcs)
    return out


def _w_spec(w, layer):
    if layer is None:
        return pl.BlockSpec(w.shape, lambda *idx: (0, 0, 0))
    return pl.BlockSpec((N_DEV, None) + w.shape[2:], lambda *idx: (0, layer, 0, 0))


def smm_fwd(a, w, layer, seg_widths, name, hosted=None):
    M, K = a.shape
    n = w.shape[-1]
    pieces = _shard_pieces(seg_widths, n)
    padded = [sw + (-sw) % 128 for sw in seg_widths]
    bm = _pick(M, SMM_ROWS)

    def body(a_ref, w_ref, *o_refs):
        av = a_ref[...]
        for si, sw in enumerate(seg_widths):
            if padded[si] != sw:
                o_refs[si][:, pl.ds(padded[si] - 128, 128)] = jnp.zeros((bm, 128), f32)
        for j in range(N_DEV):
            for si, soff, woff, wd in pieces[j]:
                o_refs[si][:, pl.ds(soff, wd)] = jnp.dot(av, w_ref[j, :, pl.ds(woff, wd)], preferred_element_type=f32)

    outs, extra = _host_call(
        body, (M // bm,), [pl.BlockSpec((bm, K), lambda i: (i, 0)), _w_spec(w, layer)],
        [pl.BlockSpec((bm, pw), lambda i: (i, 0)) for pw in padded], [S((M, pw), f32) for pw in padded], [],
        ("parallel",), name, (a, w), hosted)
    return outs if hosted is None else (outs, extra)


def smm_dx(d_segs, w, layer, seg_widths, out_dtype, name, hosted=None):
    M = d_segs[0].shape[0]
    K, n = w.shape[-2], w.shape[-1]
    pieces = _shard_pieces(seg_widths, n)
    ns = len(d_segs)
    bm = _pick(M, SMM_ROWS)

    def body(*refs):
        d_refs, w_ref, o_ref = refs[:ns], refs[ns], refs[ns + 1]
        acc = jnp.zeros((bm, K), f32)
        for j in range(N_DEV):
            for si, soff, woff, wd in pieces[j]:
                acc = acc + lax.dot_general(d_refs[si][:, pl.ds(soff, wd)], w_ref[j, :, pl.ds(woff, wd)],
                                            (((1,), (1,)), ((), ())), preferred_element_type=f32)
        o_ref[...] = acc.astype(out_dtype)

    (out,), extra = _host_call(
        body, (M // bm,),
        [pl.BlockSpec((bm, d.shape[1]), lambda i: (i, 0)) for d in d_segs] + [_w_spec(w, layer)],
        [pl.BlockSpec((bm, K), lambda i: (i, 0))], [S((M, K), out_dtype)], [], ("parallel",), name,
        (*d_segs, w), hosted)
    return out, extra


def smm_dw(a, d_segs, n, seg_widths, ngrp, transposed, name):
    M, K = a.shape
    pieces = _shard_pieces(seg_widths, n)
    per = N_DEV // ngrp
    nI = M // SMM_BM
    ns = len(d_segs)
    shard = (n, K) if transposed else (K, n)

    def body(*refs):
        a_ref, d_refs, o_ref, acc_ref = refs[0], refs[1:1 + ns], refs[1 + ns], refs[2 + ns]
        grp = pl.program_id(0)
        i = pl.program_id(1)

        @pl.when(i == 0)
        def _():
            acc_ref[...] = jnp.zeros_like(acc_ref)

        av = a_ref[...]
        for gs in range(ngrp):
            def one_group(gs=gs):
                for jj in range(per):
                    for si, soff, woff, wd in pieces[gs * per + jj]:
                        dv = d_refs[si][:, pl.ds(soff, wd)]
                        if transposed:
                            acc_ref[jj, pl.ds(woff, wd), :] += lax.dot_general(
                                dv, av, (((0,), (0,)), ((), ())), preferred_element_type=f32)
                        else:
                            acc_ref[jj, :, pl.ds(woff, wd)] += lax.dot_general(
                                av, dv, (((0,), (0,)), ((), ())), preferred_element_type=f32)
            pl.when(grp == gs)(one_group)

        @pl.when(i == nI - 1)
        def _():
            o_ref[...] = acc_ref[...].astype(bf16)

    return pl.pallas_call(
        body, grid=(ngrp, nI),
        in_specs=[pl.BlockSpec((SMM_BM, K), lambda g, i: (i, 0))]
        + [pl.BlockSpec((SMM_BM, d.shape[1]), lambda g, i: (i, 0)) for d in d_segs],
        out_specs=pl.BlockSpec((per,) + shard, lambda g, i: (g, 0, 0)), out_shape=S((N_DEV,) + shard, bf16),
        scratch_shapes=[pltpu.VMEM((per,) + shard, f32)],
        compiler_params=_cparams("arbitrary", "arbitrary"), name=name)(a, *d_segs)


def _modnorm_f(h, w, sc, sh):
    y = h * lax.rsqrt(jnp.mean(h * h, axis=-1, keepdims=True) + EPS)
    return (y * w) * (1.0 + sc) + sh


def _kind_specs(nctxb):
    if nctxb > 0:
        return pl.BlockSpec((None, 1, D), lambda i: (jnp.where(i < nctxb, 0, 1), 0, 0))
    return pl.BlockSpec((None, 1, D), lambda i: (0, 0, 0))


def _two_part_specs(nctxb):
    return (pl.BlockSpec((TB, D), lambda i: (jnp.minimum(i, nctxb - 1), 0)),
            pl.BlockSpec((TB, D), lambda i: (jnp.maximum(i - nctxb, 0), 0)))


def modnorm_fwd(h, w, sc, sh, nctxb, name, ctx=None):
    if ctx is None:
        T = h.shape[0]

        def body(h_ref, w_ref, sc_ref, sh_ref, o_ref):
            o_ref[...] = _modnorm_f(h_ref[...], w_ref[...], sc_ref[...], sh_ref[...]).astype(bf16)

        hspecs, hargs = [pl.BlockSpec((TB, D), lambda i: (i, 0))], (h,)
    else:
        T = h.shape[0] + ctx.shape[0]

        def body(c_ref, h_ref, w_ref, sc_ref, sh_ref, o_ref):
            hv = jnp.where(pl.program_id(0) < nctxb, c_ref[...], h_ref[...])
            o_ref[...] = _modnorm_f(hv, w_ref[...], sc_ref[...], sh_ref[...]).astype(bf16)

        hspecs, hargs = list(_two_part_specs(nctxb)), (ctx, h)
    row = pl.BlockSpec((1, D), lambda i: (0, 0))
    ks = _kind_specs(nctxb)
    return pl.pallas_call(body, grid=(T // TB,), in_specs=hspecs + [row, ks, ks],
                          out_specs=pl.BlockSpec((TB, D), lambda i: (i, 0)), out_shape=S((T, D), bf16),
                          compiler_params=_cparams("parallel"), name=name)(*hargs, w, sc, sh)


def modnorm_bwd(h, w, sc, sh, da, dres, nctxb, name, ctx=None):
    T = h.shape[0] + (0 if ctx is None else ctx.shape[0])
    kinds = sc.shape[0]
    nh = 1 if ctx is None else 2

    def body(*refs):
        w_ref, sc_ref, sh_ref, da_ref, dres_ref, dh_ref, dw_ref, dsc_ref, dsh_ref = refs[nh:]
        i = pl.program_id(0)
        hv = refs[0][...] if ctx is None else jnp.where(i < nctxb, refs[0][...], refs[1][...])
        _, vjp = jax.vjp(_modnorm_f, hv, w_ref[...], sc_ref[...], sh_ref[...])
        dh, dw, dsc, dsh = vjp(da_ref[...].astype(f32))
        dh_ref[...] = dres_ref[...] + dh

        @pl.when(i == 0)
        def _():
            dw_ref[...] = jnp.zeros_like(dw_ref)

        @pl.when((i == 0) | (i == nctxb))
        def _():
            dsc_ref[...] = jnp.zeros_like(dsc_ref)
            dsh_ref[...] = jnp.zeros_like(dsh_ref)

        dw_ref[...] += dw
        dsc_ref[...] += dsc
        dsh_ref[...] += dsh

    blk = pl.BlockSpec((TB, D), lambda i: (i, 0))
    lat = pl.BlockSpec((TB, D), lambda i: (jnp.maximum(i - nctxb, 0), 0))
    row = pl.BlockSpec((1, D), lambda i: (0, 0))
    ks = _kind_specs(nctxb)
    hspecs, hargs = ([blk], (h,)) if ctx is None else (list(_two_part_specs(nctxb)), (ctx, h))
    return pl.pallas_call(
        body, grid=(T // TB,), in_specs=hspecs + [row, ks, ks, blk, lat], out_specs=[lat, row, ks, ks],
        out_shape=[S((T - nctxb * TB, D), f32), S((1, D), f32), S((kinds, 1, D), f32), S((kinds, 1, D), f32)],
        compiler_params=_cparams("arbitrary"), name=name)(*hargs, w, sc, sh, da, dres)


def resgate_fwd(h, o, g, b, name):
    T = h.shape[0]

    def body(h_ref, o_ref, g_ref, b_ref, out_ref):
        out_ref[...] = h_ref[...] + g_ref[...] * (o_ref[...] + b_ref[...])

    blk = pl.BlockSpec((TB, D), lambda i: (i, 0))
    row = pl.BlockSpec((1, D), lambda i: (0, 0))
    return pl.pallas_call(body, grid=(T // TB,), in_specs=[blk, blk, row, row], out_specs=blk,
                          out_shape=S((T, D), f32), compiler_params=_cparams("parallel"), name=name)(h, o, g, b)


def resgate_bwd(dh, o, g, b, name):
    T = dh.shape[0]

    def body(dh_ref, o_ref, g_ref, b_ref, do_ref, dg_ref, db_ref):
        i = pl.program_id(0)

        @pl.when(i == 0)
        def _():
            dg_ref[...] = jnp.zeros_like(dg_ref)
            db_ref[...] = jnp.zeros_like(db_ref)

        dh = dh_ref[...]
        do = g_ref[...] * dh
        do_ref[...] = do.astype(bf16)
        dg_ref[...] += jnp.sum(dh * (o_ref[...] + b_ref[...]), axis=0, keepdims=True)
        db_ref[...] += jnp.sum(do, axis=0, keepdims=True)

    blk = pl.BlockSpec((TB, D), lambda i: (i, 0))
    row = pl.BlockSpec((1, D), lambda i: (0, 0))
    return pl.pallas_call(body, grid=(T // TB,), in_specs=[blk, blk, row, row], out_specs=[blk, row, row],
                          out_shape=[S((T, D), bf16), S((1, D), f32), S((1, D), f32)],
                          compiler_params=_cparams("arbitrary"), name=name)(dh, o, g, b)


def final_loss(h, w, tgt, name):
    T = h.shape[0]

    def f(hv, wv, tv):
        y = (hv * lax.rsqrt(jnp.mean(hv * hv, axis=-1, keepdims=True) + EPS)) * wv
        e = y - tv
        return 0.5 * jnp.sum(jnp.sum(e * e, axis=-1, keepdims=True), axis=0, keepdims=True) * (1.0 / D)

    def body(h_ref, w_ref, t_ref, loss_ref, dh_ref, dw_ref):
        i = pl.program_id(0)
        tv = t_ref[...]
        val, vjp = jax.vjp(lambda a, b_: f(a, b_, tv), h_ref[...], w_ref[...])
        dh, dw = vjp(jnp.ones((1, 1), f32))
        dh_ref[...] = dh

        @pl.when(i == 0)
        def _():
            loss_ref[...] = jnp.zeros_like(loss_ref)
            dw_ref[...] = jnp.zeros_like(dw_ref)

        loss_ref[...] += jnp.broadcast_to(val, (1, 128))
        dw_ref[...] += dw

    blk = pl.BlockSpec((TB, D), lambda i: (i, 0))
    row = pl.BlockSpec((1, D), lambda i: (0, 0))
    return pl.pallas_call(body, grid=(T // TB,), in_specs=[blk, row, blk],
                          out_specs=[pl.BlockSpec((1, 128), lambda i: (0, 0)), blk, row],
                          out_shape=[S((1, 128), f32), S((T, D), f32), S((1, D), f32)],
                          compiler_params=_cparams("arbitrary"), name=name)(h, w, tgt)


CB = 256
RT = 32
RTB = 16


def _fold8(t):
    acc = t[0:8]
    for k in range(1, t.shape[0] // 8):
        acc = acc + t[8 * k:8 * (k + 1)]
    return acc


def _rows(start, off=0, rt=RT):
    return pl.ds(pl.multiple_of(start + off, 8), rt)


def _rowsb(start, off=0):
    return _rows(start, off, RTB)


def _zero_rows(ref, start, n):
    ref[pl.ds(start, n), :] = jnp.zeros((n, ref.shape[1]), f32)


K5, HALF5, PAD5 = 5, 2, 8


def _shift_copies5(base_ref, s_ref, ln, sign):
    for k in range(K5):
        s_ref[k, pl.ds(0, ln), :] = base_ref[pl.ds(PAD5 + sign * (k - HALF5), ln), :]


def ssd_conv_fwd(u, w, b, segs, name, hosted=None):
    T = u.shape[0]
    maxlen = max(ln for _, ln in segs)

    def body(u_ref, w_ref, b_ref, o_ref, ds_ref, base_ref, s_ref):
        wv = [w_ref[pl.ds(k, 1), :] for k in range(K5)]
        bv = b_ref[...]
        for s0, ln in segs:
            _zero_rows(base_ref, 0, PAD5)
            _zero_rows(base_ref, PAD5 + ln, PAD5)
            base_ref[pl.ds(PAD5, ln), :] = u_ref[pl.ds(s0, ln), :]
            _shift_copies5(base_ref, s_ref, ln, 1)

            def tile(i, carry):
                r = i * RT
                acc = jnp.broadcast_to(bv, (RT, CB))
                for k in range(K5):
                    acc = acc + s_ref[k, _rows(r), :] * wv[k]
                sg = _sigmoid(acc)
                o_ref[_rows(r, s0), :] = acc * sg
                ds_ref[_rows(r, s0), :] = sg * (1.0 + acc * (1.0 - sg))
                return carry

            lax.fori_loop(0, ln // RT, tile, 0, unroll=2)

    cblk = pl.BlockSpec((T, CB), lambda j: (0, j))
    (out, dsilu), extra = _host_call(
        body, (CONVD // CB,),
        [cblk, pl.BlockSpec((K5, CB), lambda j: (0, j)), pl.BlockSpec((1, CB), lambda j: (0, j))],
        [cblk, cblk], [S((T, CONVD), f32), S((T, CONVD), f32)],
        [pltpu.VMEM((maxlen + 2 * PAD5, CB), f32), pltpu.VMEM((K5, maxlen, CB), f32)],
        ("parallel",), name, (u, w, b), hosted)
    return out, dsilu, extra


def ssd_conv_bwd(proj, w, dsilu, dy2, dyskip, dexp, segs, name):
    T = proj.shape[0]
    maxlen = max(ln for _, ln in segs)
    nskip = DI // CB

    def body(u_ref, w_ref, ds_ref, dya_ref, dyb_ref, dsk_ref, dexp_ref, du_ref, dw_ref, db_ref, base_ref, s_ref):
        wv = [w_ref[pl.ds(k, 1), :] for k in range(K5)]
        has_skip = (pl.program_id(0) < nskip).astype(f32) * dexp_ref[...]
        acc8 = tuple(jnp.zeros((8, CB), f32) for _ in range(K5 + 1))
        for s0, ln in segs:
            _zero_rows(base_ref, 0, PAD5)
            _zero_rows(base_ref, PAD5 + ln, PAD5)
            base_ref[pl.ds(PAD5, ln), :] = u_ref[pl.ds(s0, ln), :]
            _shift_copies5(base_ref, s_ref, ln, 1)

            def tile1(i, carry):
                r = i * RTB
                dy = dya_ref[_rowsb(r, s0), :] + dyb_ref[_rowsb(r, s0), :] + has_skip * dsk_ref[_rowsb(r, s0), :]
                dpre = dy * ds_ref[_rowsb(r, s0), :]
                base_ref[_rowsb(r, PAD5), :] = dpre
                new = [carry[k] + _fold8(dpre * s_ref[k, _rowsb(r), :]) for k in range(K5)]
                new.append(carry[K5] + _fold8(dpre))
                return tuple(new)

            acc8 = lax.fori_loop(0, ln // RTB, tile1, acc8, unroll=2)
            _shift_copies5(base_ref, s_ref, ln, -1)

            def tile2(i, carry):
                r = i * RTB
                du = jnp.zeros((RTB, CB), f32)
                for k in range(K5):
                    du = du + s_ref[k, _rowsb(r), :] * wv[k]
                du_ref[_rowsb(r, s0), :] = du.astype(bf16)
                return carry

            lax.fori_loop(0, ln // RTB, tile2, 0, unroll=4)
        for k in range(K5):
            dw_ref[pl.ds(k, 1), :] = jnp.sum(acc8[k], axis=0, keepdims=True)
        db_ref[...] = jnp.sum(acc8[K5], axis=0, keepdims=True)

    cblk = pl.BlockSpec((T, CB), lambda j: (0, j))
    return pl.pallas_call(
        body, grid=(CONVD // CB,),
        in_specs=[cblk, pl.BlockSpec((K5, CB), lambda j: (0, j)), cblk,
                  pl.BlockSpec((None, T, CB), lambda j: (0, 0, j)), pl.BlockSpec((None, T, CB), lambda j: (1, 0, j)),
                  pl.BlockSpec((T, CB), lambda j: (0, jnp.minimum(j, nskip - 1))),
                  pl.BlockSpec((1, CB), lambda j: (0, jnp.minimum(j, nskip - 1)))],
        out_specs=[cblk, pl.BlockSpec((K5, CB), lambda j: (0, j)), pl.BlockSpec((1, CB), lambda j: (0, j))],
        out_shape=[S((T, CONVD), bf16), S((K5, CONVD), f32), S((1, CONVD), f32)],
        scratch_shapes=[pltpu.VMEM((maxlen + 2 * PAD5, CB), f32), pltpu.VMEM((K5, maxlen, CB), f32)],
        compiler_params=_cparams("parallel"), name=name)(proj, w, dsilu, dy2, dy2, dyskip, dexp)


GPAD = GRID_W


def _grid_copies(g_ref, src, L):
    col = lax.broadcasted_iota(jnp.int32, (L, CB), 0) & (GRID_W - 1)
    for d in range(3):
        _zero_rows(g_ref.at[d], 0, GPAD)
        _zero_rows(g_ref.at[d], GPAD + L, GPAD)
    g_ref[1, pl.ds(GPAD, L), :] = src
    g_ref[0, pl.ds(GPAD, L), :] = jnp.where(col != 0, g_ref[1, pl.ds(GPAD - 1, L), :], 0.0)
    g_ref[2, pl.ds(GPAD, L), :] = jnp.where(col != GRID_W - 1, g_ref[1, pl.ds(GPAD + 1, L), :], 0.0)


def ffn_gate_fwd(val, gate, cw, cb_, name, hosted=None):
    L = val.shape[0]
    nb = FH // CB

    def body(val_ref, gate_ref, w_ref, b_ref, o_ref, s_ref, vds_ref, g_ref):
        wv = [w_ref[pl.ds(t, 1), :] for t in range(9)]
        bv = b_ref[...]
        _grid_copies(g_ref, gate_ref[...], L)

        def tile(i, carry):
            r = i * RT
            acc = jnp.broadcast_to(bv, (RT, CB))
            for dr in range(3):
                for dc in range(3):
                    acc = acc + g_ref[dc, _rows(r, GPAD + (dr - 1) * GRID_W), :] * wv[3 * dr + dc]
            sg = _sigmoid(acc)
            s = acc * sg
            v = val_ref[_rows(r), :]
            o_ref[_rows(r), :] = (s * v).astype(bf16)
            s_ref[_rows(r), :] = s
            vds_ref[_rows(r), :] = v * (sg * (1.0 + acc * (1.0 - sg)))
            return carry

        lax.fori_loop(0, L // RT, tile, 0, unroll=2)

    cblk = pl.BlockSpec((L, CB), lambda j: (0, j))
    (out, s_, vds), extra = _host_call(
        body, (nb,), [cblk, cblk, pl.BlockSpec((9, CB), lambda j: (0, j)), pl.BlockSpec((1, CB), lambda j: (0, j))],
        [cblk, cblk, cblk], [S((L, FH), bf16), S((L, FH), f32), S((L, FH), f32)],
        [pltpu.VMEM((3, L + 2 * GPAD, CB), f32)], ("parallel",), name, (val, gate, cw, cb_), hosted)
    return out, s_, vds, extra


def ffn_gate_bwd(gate, s_, vds, cw, dact, name):
    L = gate.shape[0]
    nb = FH // CB

    def body(gate_ref, s_ref, vds_ref, w_ref, da_ref, dval_ref, dgate_ref, dw_ref, db_ref, g_ref, d_ref):
        wv = [w_ref[pl.ds(t, 1), :] for t in range(9)]
        _grid_copies(g_ref, gate_ref[...], L)

        def tile1(i, carry):
            r = i * RTB
            da = da_ref[_rowsb(r), :].astype(f32)
            dval_ref[_rowsb(r), :] = (da * s_ref[_rowsb(r), :]).astype(bf16)
            dpre = da * vds_ref[_rowsb(r), :]
            d_ref[_rowsb(r), :] = dpre
            new = [carry[t] + _fold8(dpre * g_ref[t % 3, _rowsb(r, GPAD + (t // 3 - 1) * GRID_W), :]) for t in range(9)]
            new.append(carry[9] + _fold8(dpre))
            return tuple(new)

        acc8 = lax.fori_loop(0, L // RTB, tile1, tuple(jnp.zeros((8, CB), f32) for _ in range(10)), unroll=2)
        for t in range(9):
            dw_ref[pl.ds(t, 1), :] = jnp.sum(acc8[t], axis=0, keepdims=True)
        db_ref[...] = jnp.sum(acc8[9], axis=0, keepdims=True)
        _grid_copies(g_ref, d_ref[...], L)

        def tile2(i, carry):
            r = i * RTB
            dg = jnp.zeros((RTB, CB), f32)
            for dr in range(3):
                for dc in range(3):
                    dg = dg + g_ref[2 - dc, _rowsb(r, GPAD - (dr - 1) * GRID_W), :] * wv[3 * dr + dc]
            dgate_ref[_rowsb(r), :] = dg.astype(bf16)
            return carry

        lax.fori_loop(0, L // RTB, tile2, 0, unroll=4)

    cblk = pl.BlockSpec((L, CB), lambda j: (0, j))
    return pl.pallas_call(
        body, grid=(nb,),
        in_specs=[cblk, cblk, cblk, pl.BlockSpec((9, CB), lambda j: (0, j)), cblk],
        out_specs=[cblk, cblk, pl.BlockSpec((9, CB), lambda j: (0, j)), pl.BlockSpec((1, CB), lambda j: (0, j))],
        out_shape=[S((L, FH), bf16), S((L, FH), bf16), S((9, FH), f32), S((1, FH), f32)],
        scratch_shapes=[pltpu.VMEM((3, L + 2 * GPAD, CB), f32), pltpu.VMEM((L, CB), f32)],
        compiler_params=_cparams("parallel"), name=name)(gate, s_, vds, cw, dact)


CONF_K = 31
CHALF = CONF_K // 2
CPAD = 16


def _shift_copies8(c_ref, base_ref, L):
    n = L + 2 * CPAD - 8
    for b_ in range(8):
        c_ref[b_, pl.ds(0, n), :] = base_ref[pl.ds(b_, n), :]


def _tap_ab(o):
    return o % 8, o - o % 8


def conf_glu_conv_fwd(pa, pg, b1, wdw, bdw, name, hosted=None):
    L = pa.shape[0]
    nb = D // CB

    def body(pa_ref, pg_ref, ba_ref, bg_ref, w_ref, bdw_ref, o_ref, base_ref, c_ref):
        _zero_rows(base_ref, 0, CPAD)
        _zero_rows(base_ref, CPAD + L, CPAD)
        base_ref[pl.ds(CPAD, L), :] = (pa_ref[...] + ba_ref[...]) * _sigmoid(pg_ref[...] + bg_ref[...])
        _shift_copies8(c_ref, base_ref, L)
        bv = bdw_ref[...]

        def tile(i, carry):
            r = i * RT
            acc = jnp.broadcast_to(bv, (RT, CB))
            for k in range(CONF_K):
                b_, a8 = _tap_ab(k - CHALF)
                acc = acc + c_ref[b_, _rows(r, CPAD + a8), :] * w_ref[pl.ds(k, 1), :]
            o_ref[_rows(r), :] = acc
            return carry

        lax.fori_loop(0, L // RT, tile, 0, unroll=2)

    cblk = pl.BlockSpec((L, CB), lambda j: (0, j))
    rblk = pl.BlockSpec((1, CB), lambda j: (0, j))
    rgblk = pl.BlockSpec((1, CB), lambda j: (0, nb + j))
    (out,), extra = _host_call(
        body, (nb,), [cblk, cblk, rblk, rgblk, pl.BlockSpec((CONF_K, CB), lambda j: (0, j)), rblk],
        [cblk], [S((L, D), f32)], [pltpu.VMEM((L + 2 * CPAD, CB), f32), pltpu.VMEM((8, L + 2 * CPAD, CB), f32)],
        ("parallel",), name, (pa, pg, b1, b1, wdw, bdw), hosted)
    return out, extra


def conf_glu_conv_bwd(pa, pg, b1, wdw, dy, name):
    L = pa.shape[0]
    nb = D // CB

    def body(pa_ref, pg_ref, ba_ref, bg_ref, w_ref, dy_ref, dpa_ref, dpg_ref, dba_ref, dbg_ref, dw_ref, dbdw_ref,
             base_ref, c_ref, acc_ref):
        _zero_rows(base_ref, 0, CPAD)
        _zero_rows(base_ref, CPAD + L, CPAD)
        base_ref[pl.ds(CPAD, L), :] = (pa_ref[...] + ba_ref[...]) * _sigmoid(pg_ref[...] + bg_ref[...])
        _shift_copies8(c_ref, base_ref, L)
        acc_ref[...] = jnp.zeros_like(acc_ref)

        def tile1(i, carry):
            r = i * RTB
            dyt = dy_ref[_rowsb(r), :]
            for k in range(CONF_K):
                b_, a8 = _tap_ab(k - CHALF)
                acc_ref[k] += _fold8(dyt * c_ref[b_, _rowsb(r, CPAD + a8), :])
            return carry + _fold8(dyt)

        db8 = lax.fori_loop(0, L // RTB, tile1, jnp.zeros((8, CB), f32), unroll=2)
        dbdw_ref[...] = jnp.sum(db8, axis=0, keepdims=True)
        for k in range(CONF_K):
            dw_ref[pl.ds(k, 1), :] = jnp.sum(acc_ref[k], axis=0, keepdims=True)
        base_ref[pl.ds(CPAD, L), :] = dy_ref[...]
        _shift_copies8(c_ref, base_ref, L)
        ba = ba_ref[...]
        bg = bg_ref[...]

        def tile2(i, carry):
            r = i * RTB
            dglu = jnp.zeros((RTB, CB), f32)
            for k in range(CONF_K):
                b_, a8 = _tap_ab(CHALF - k)
                dglu = dglu + c_ref[b_, _rowsb(r, CPAD + a8), :] * w_ref[pl.ds(k, 1), :]
            a = pa_ref[_rowsb(r), :] + ba
            sg = _sigmoid(pg_ref[_rowsb(r), :] + bg)
            dpa = dglu * sg
            dpg = dglu * a * (sg * (1.0 - sg))
            dpa_ref[_rowsb(r), :] = dpa.astype(bf16)
            dpg_ref[_rowsb(r), :] = dpg.astype(bf16)
            return carry[0] + _fold8(dpa), carry[1] + _fold8(dpg)

        s8 = lax.fori_loop(0, L // RTB, tile2, (jnp.zeros((8, CB), f32), jnp.zeros((8, CB), f32)), unroll=2)
        dba_ref[...] = jnp.sum(s8[0], axis=0, keepdims=True)
        dbg_ref[...] = jnp.sum(s8[1], axis=0, keepdims=True)

    cblk = pl.BlockSpec((L, CB), lambda j: (0, j))
    rblk = pl.BlockSpec((1, CB), lambda j: (0, j))
    rgblk = pl.BlockSpec((1, CB), lambda j: (0, nb + j))
    wblk = pl.BlockSpec((CONF_K, CB), lambda j: (0, j))
    return pl.pallas_call(
        body, grid=(nb,), in_specs=[cblk, cblk, rblk, rgblk, wblk, cblk],
        out_specs=[cblk, cblk, rblk, rblk, wblk, rblk],
        out_shape=[S((L, D), bf16), S((L, D), bf16), S((1, D), f32), S((1, D), f32), S((CONF_K, D), f32), S((1, D), f32)],
        scratch_shapes=[pltpu.VMEM((L + 2 * CPAD, CB), f32), pltpu.VMEM((8, L + 2 * CPAD, CB), f32),
                        pltpu.VMEM((CONF_K, 8, CB), f32)],
        compiler_params=_cparams("parallel"), name=name)(pa, pg, b1, b1, wdw, dy)


def _ln_silu_f(x, w, b):
    mu = jnp.mean(x, axis=-1, keepdims=True)
    d = x - mu
    y = d * lax.rsqrt(jnp.mean(d * d, axis=-1, keepdims=True) + EPS) * w + b
    return y * _sigmoid(y)


def ln_silu_fwd(x, w, b, name):
    T = x.shape[0]

    def body(x_ref, w_ref, b_ref, o_ref):
        o_ref[...] = _ln_silu_f(x_ref[...], w_ref[...], b_ref[...]).astype(bf16)

    blk = pl.BlockSpec((TB, D), lambda i: (i, 0))
    row = pl.BlockSpec((1, D), lambda i: (0, 0))
    return pl.pallas_call(body, grid=(T // TB,), in_specs=[blk, row, row], out_specs=blk, out_shape=S((T, D), bf16),
                          compiler_params=_cparams("parallel"), name=name)(x, w, b)


def ln_silu_bwd(x, w, b, ds, name):
    T = x.shape[0]

    def body(x_ref, w_ref, b_ref, ds_ref, dx_ref, dw_ref, db_ref):
        i = pl.program_id(0)
        _, vjp = jax.vjp(_ln_silu_f, x_ref[...], w_ref[...], b_ref[...])
        dx, dw, db = vjp(ds_ref[...].astype(f32))
        dx_ref[...] = dx

        @pl.when(i == 0)
        def _():
            dw_ref[...] = jnp.zeros_like(dw_ref)
            db_ref[...] = jnp.zeros_like(db_ref)

        dw_ref[...] += dw
        db_ref[...] += db

    blk = pl.BlockSpec((TB, D), lambda i: (i, 0))
    row = pl.BlockSpec((1, D), lambda i: (0, 0))
    return pl.pallas_call(body, grid=(T // TB,), in_specs=[blk, row, row, blk], out_specs=[blk, row, row],
                          out_shape=[S((T, D), f32), S((1, D), f32), S((1, D), f32)],
                          compiler_params=_cparams("arbitrary"), name=name)(x, w, b, ds)


def _mxu(a, b, dims):
    return lax.dot_general(a.astype(bf16), b.astype(bf16), (dims, ((), ())), preferred_element_type=f32)


def _nn(a, b):
    return _mxu(a, b, ((1,), (0,)))


def _nt(a, b):
    return _mxu(a, b, ((1,), (1,)))


def _tn(a, b):
    return _mxu(a, b, ((0,), (0,)))


@jax.custom_vjp
def _dot_nn(a, b):
    return _nn(a, b)


@jax.custom_vjp
def _dot_nt(a, b):
    return _nt(a, b)


@jax.custom_vjp
def _dot_tn(a, b):
    return _tn(a, b)


_dot_nn.defvjp(lambda a, b: (_nn(a, b), (a, b)), lambda res, g: (_nt(g, res[1]), _tn(res[0], g)))
_dot_nt.defvjp(lambda a, b: (_nt(a, b), (a, b)), lambda res, g: (_nn(g, res[1]), _tn(g, res[0])))
_dot_tn.defvjp(lambda a, b: (_tn(a, b), (a, b)), lambda res, g: (_nt(res[1], g), _nn(res[0], g)))


def _exact_dot(a, b, dims, split_first):
    v = a if split_first else b
    p1 = v.astype(bf16)
    r1 = v - p1.astype(f32)
    p2 = r1.astype(bf16)
    p3 = (r1 - p2.astype(f32)).astype(bf16)
    out = None
    for p in (p1, p2, p3):
        lhs, rhs = (p, b.astype(bf16)) if split_first else (a.astype(bf16), p)
        t = lax.dot_general(lhs, rhs, (dims, ((), ())), preferred_element_type=f32)
        out = t if out is None else out + t
    return out


@jax.custom_vjp
def _masked_sum_cols(mf, a):
    return _exact_dot(mf, a, ((1,), (0,)), False)


@jax.custom_vjp
def _masked_sum_rows(mf, a):
    return _exact_dot(a, mf, ((1,), (1,)), True)


_masked_sum_cols.defvjp(lambda mf, a: (_exact_dot(mf, a, ((1,), (0,)), False), mf),
                        lambda mf, g: (jnp.zeros_like(mf), _exact_dot(mf, g, ((0,), (0,)), False)))
_masked_sum_rows.defvjp(lambda mf, a: (_exact_dot(a, mf, ((1,), (1,)), True), mf),
                        lambda mf, g: (jnp.zeros_like(mf), _exact_dot(g, mf, ((1,), (0,)), True)))


def _masked_sum(mf, a, rows):
    return _masked_sum_rows(mf, a) if rows else _masked_sum_cols(mf, a)


def _lanes_to_rows(v):
    r = lax.broadcasted_iota(jnp.int32, (GW, GW), 0)
    c = lax.broadcasted_iota(jnp.int32, (GW, GW), 1)
    return jnp.sum(jnp.where(r == c, jnp.broadcast_to(v, (GW, GW)), 0.0), axis=1, keepdims=True)


def _ssd_chunk(x, B, C, dtc, dtr, bc, br, alc, alr, s_in, is_fwd):
    row = lax.broadcasted_iota(jnp.int32, (Q, Q), 0)
    col = lax.broadcasted_iota(jnp.int32, (Q, Q), 1)
    sgn = jnp.where(is_fwd, 1, -1).astype(jnp.int32)
    mask = (row - col) * sgn >= 0
    mf = mask.astype(f32)
    lane_head = lax.broadcasted_iota(jnp.int32, (1, GW), 1) // P

    def spread(v):
        out = jnp.zeros((v.shape[0], GW), f32)
        for r in range(HPG):
            out = jnp.where(lane_head == r, v[:, r:r + 1], out)
        return out

    dt_c = _softplus(dtc + bc)
    dt_r = _softplus(dtr + br)
    a_c = dt_c * (-jnp.exp(alc))
    a_r = dt_r * (-jnp.exp(alr))
    acum_c = _masked_sum(mf, a_c, False)
    acum_r = _masked_sum(mf, a_r, True)
    tot_c = jnp.sum(a_c, axis=0, keepdims=True)
    dt_e = spread(dt_c)
    acum_e = spread(acum_c)
    tot_e = spread(tot_c)
    xdt = x * dt_e
    cb = _dot_nt(C, B)
    scores, xs = [], []
    for r in range(HPG):
        seg = acum_c[:, r:r + 1] - acum_r[r:r + 1, :]
        scores.append(cb * jnp.exp(jnp.where(mask, seg, -jnp.inf)))
        xs.append(jnp.where(lane_head == r, xdt, 0.0))
    y = _dot_nn(jnp.concatenate(scores, axis=1), jnp.concatenate(xs, axis=0))
    y = y + _dot_nt(C, s_in) * jnp.exp(acum_e)
    xe = xdt * jnp.exp(tot_e - acum_e)
    s_out = _lanes_to_rows(jnp.exp(tot_e)) * s_in + _dot_tn(xe, B)
    return y, s_out


def _chunk_index(d, t, nctx, nc):
    bwd = jnp.where(t < nctx, nctx - 1 - t, nc - 1 - (t - nctx))
    return jnp.where(d == 0, t, bwd)


def _ssd_in_specs(ci):
    small_c = pl.BlockSpec((None, G, 1, HPG), lambda d, t: (d, 0, 0, 0))
    small_r = pl.BlockSpec((None, G, HPG, 1), lambda d, t: (d, 0, 0, 0))
    return [
        pl.BlockSpec((Q, CONVD), lambda d, t: (ci(d, t), 0)),
        pl.BlockSpec((None, G, Q, HPG), lambda d, t: (d, 0, ci(d, t), 0)),
        pl.BlockSpec((None, G, HPG, Q), lambda d, t: (d, 0, 0, ci(d, t))),
        small_c, small_r, small_c, small_r,
    ]


def _group_cols(g):
    return pl.ds(g * GW, GW), pl.ds(DI + g * N, N), pl.ds(DI + G * N + g * N, N)


def ssd_scan_fwd(xbc, dtc, dtr, bc, br, alc, alr, nctx, name, hosted=None):
    T = xbc.shape[0]
    nc = T // Q

    def body(xbc_ref, dtc_ref, dtr_ref, bc_ref, br_ref, alc_ref, alr_ref, y_ref, sin_ref, st_ref):
        d = pl.program_id(0)
        t = pl.program_id(1)

        @pl.when(t == 0)
        def _():
            st_ref[...] = jnp.zeros_like(st_ref)

        for g in range(G):
            xs, bs, cs = _group_cols(g)
            s_in = st_ref[g]
            sin_ref[g] = s_in
            y, s_out = _ssd_chunk(xbc_ref[:, xs], xbc_ref[:, bs], xbc_ref[:, cs], dtc_ref[g], dtr_ref[g], bc_ref[g], br_ref[g],
                                  alc_ref[g], alr_ref[g], s_in, d == 0)
            y_ref[:, xs] = y
            st_ref[g] = s_out

    ci = lambda d, t: _chunk_index(d, t, nctx, nc)
    out_specs = [
        pl.BlockSpec((None, Q, DI), lambda d, t: (d, ci(d, t), 0)),
        pl.BlockSpec((None, None, G, GW, N), lambda d, t: (d, ci(d, t), 0, 0, 0)),
    ]
    return _host_call(
        body, (2, nc), _ssd_in_specs(ci), out_specs, [S((2, T, DI), f32), S((2, nc, G, GW, N), f32)],
        [pltpu.VMEM((G, GW, N), f32)], ("arbitrary", "arbitrary"), name, (xbc, dtc, dtr, bc, br, alc, alr), hosted)


def ssd_scan_bwd(xbc, dtc, dtr, bc, br, alc, alr, s_in_all, dy, nctx, name, hosted=None):
    T = xbc.shape[0]
    nc = T // Q

    def body(xbc_ref, dtc_ref, dtr_ref, bc_ref, br_ref, alc_ref, alr_ref, sin_ref, dy_ref,
             dxbc_ref, ddtc_ref, ddtr_ref, dbc_ref, dbr_ref, dalc_ref, dalr_ref, ds_ref):
        d = pl.program_id(0)
        t = pl.program_id(1)

        @pl.when(t == 0)
        def _():
            ds_ref[...] = jnp.zeros_like(ds_ref)
            dbc_ref[...] = jnp.zeros_like(dbc_ref)
            dbr_ref[...] = jnp.zeros_like(dbr_ref)
            dalc_ref[...] = jnp.zeros_like(dalc_ref)
            dalr_ref[...] = jnp.zeros_like(dalr_ref)

        f = functools.partial(_ssd_chunk, is_fwd=(d == 0))
        for g in range(G):
            xs, bs, cs = _group_cols(g)
            _, vjp = jax.vjp(f, xbc_ref[:, xs], xbc_ref[:, bs], xbc_ref[:, cs], dtc_ref[g], dtr_ref[g], bc_ref[g], br_ref[g],
                             alc_ref[g], alr_ref[g], sin_ref[g])
            dx, dB, dC, ddtc, ddtr, dbc, dbr, dalc, dalr, ds = vjp((dy_ref[:, xs], ds_ref[g]))
            dxbc_ref[:, xs] = dx
            dxbc_ref[:, bs] = dB
            dxbc_ref[:, cs] = dC
            ddtc_ref[g] = ddtc
            ddtr_ref[g] = ddtr
            dbc_ref[g] += dbc
            dbr_ref[g] += dbr
            dalc_ref[g] += dalc
            dalr_ref[g] += dalr
            ds_ref[g] = ds

    ci = lambda d, t: _chunk_index(d, nc - 1 - t, nctx, nc)
    in_specs = _ssd_in_specs(ci) + [
        pl.BlockSpec((None, None, G, GW, N), lambda d, t: (d, ci(d, t), 0, 0, 0)),
        pl.BlockSpec((Q, DI), lambda d, t: (ci(d, t), 0)),
    ]
    small_c = pl.BlockSpec((None, G, 1, HPG), lambda d, t: (d, 0, 0, 0))
    small_r = pl.BlockSpec((None, G, HPG, 1), lambda d, t: (d, 0, 0, 0))
    out_specs = [
        pl.BlockSpec((None, Q, CONVD), lambda d, t: (d, ci(d, t), 0)),
        pl.BlockSpec((None, G, Q, HPG), lambda d, t: (d, 0, ci(d, t), 0)),
        pl.BlockSpec((None, G, HPG, Q), lambda d, t: (d, 0, 0, ci(d, t))),
        small_c, small_r, small_c, small_r,
    ]
    out_shape = [S((2, T, CONVD), f32), S((2, G, T, HPG), f32), S((2, G, HPG, T), f32),
                 S((2, G, 1, HPG), f32), S((2, G, HPG, 1), f32), S((2, G, 1, HPG), f32), S((2, G, HPG, 1), f32)]
    return _host_call(body, (2, nc), in_specs, out_specs, out_shape, [pltpu.VMEM((G, GW, N), f32)],
                      ("arbitrary", "arbitrary"), name, (xbc, dtc, dtr, bc, br, alc, alr, s_in_all, dy), hosted)


GTB = 128


def _gate_norm_f(yf, yb, x, z, dexp, w):
    y = (yf + yb + dexp * x) * (z * _sigmoid(z))
    return y * lax.rsqrt(jnp.mean(y * y, axis=-1, keepdims=True) + EPS) * w


def ssd_gate_fwd(y2, xbc, proj, dexp, w, nctxb, name):
    T = xbc.shape[0]
    L = T - nctxb * GTB

    def body(yf_ref, yb_ref, x_ref, z_ref, d_ref, w_ref, o_ref):
        o_ref[...] = _gate_norm_f(yf_ref[...], yb_ref[...], x_ref[...], z_ref[...], d_ref[...], w_ref[...]).astype(bf16)

    wide = pl.BlockSpec((GTB, DI), lambda i: (i + nctxb, 0))
    row = pl.BlockSpec((1, DI), lambda i: (0, 0))
    return pl.pallas_call(
        body, grid=(L // GTB,),
        in_specs=[pl.BlockSpec((None, GTB, DI), lambda i: (0, i + nctxb, 0)),
                  pl.BlockSpec((None, GTB, DI), lambda i: (1, i + nctxb, 0)), wide, wide, row, row],
        out_specs=pl.BlockSpec((GTB, DI), lambda i: (i, 0)), out_shape=S((L, DI), bf16),
        compiler_params=_cparams("parallel"), name=name)(y2, y2, xbc, proj, dexp, w)


def ssd_gate_bwd(y2, xbc, proj, dexp, w, dyn, nctxb, name, hosted=None):
    T = xbc.shape[0]
    nb = T // GTB

    def body(yf_ref, yb_ref, x_ref, z_ref, d_ref, w_ref, dyn_ref, dy_ref, dz_ref, dd_ref, dw_ref):
        i = pl.program_id(0)

        @pl.when(i == 0)
        def _():
            dd_ref[...] = jnp.zeros_like(dd_ref)
            dw_ref[...] = jnp.zeros_like(dw_ref)

        @pl.when(i < nctxb)
        def _():
            dy_ref[...] = jnp.zeros_like(dy_ref)
            dz_ref[...] = jnp.zeros_like(dz_ref)

        @pl.when(i >= nctxb)
        def _():
            _, vjp = jax.vjp(_gate_norm_f, yf_ref[...], yb_ref[...], x_ref[...], z_ref[...], d_ref[...], w_ref[...])
            dyf, _, _, dz, dd, dw = vjp(dyn_ref[...].astype(f32))
            dy_ref[...] = dyf
            dz_ref[...] = dz.astype(bf16)
            fold = (lax.broadcasted_iota(jnp.int32, (DI, 128), 0) // P == lax.broadcasted_iota(jnp.int32, (DI, 128), 1))
            dd_ref[...] += jnp.dot(dd, fold.astype(f32), precision=HI, preferred_element_type=f32)
            dw_ref[...] += dw

    wide = pl.BlockSpec((GTB, DI), lambda i: (i, 0))
    row = pl.BlockSpec((1, DI), lambda i: (0, 0))
    hrow = pl.BlockSpec((1, 128), lambda i: (0, 0))
    return _host_call(
        body, (nb,),
        [pl.BlockSpec((None, GTB, DI), lambda i: (0, i, 0)), pl.BlockSpec((None, GTB, DI), lambda i: (1, i, 0)),
         wide, wide, row, row, pl.BlockSpec((GTB, DI), lambda i: (jnp.maximum(i - nctxb, 0), 0))],
        [wide, wide, hrow, row],
        [S((T, DI), f32), S((T, DI), bf16), S((1, 128), f32), S((1, DI), f32)],
        [], ("arbitrary",), name, (y2, y2, xbc, proj, dexp, w, dyn), hosted)


CROWS = 2 * N_DEV


def mod_fwd(c16, modw, name):
    nl, _, cols = modw.shape

    def body(c_ref, w_ref, o_ref):
        cv = c_ref[...]
        s = cv * _sigmoid(cv)
        for l in range(nl):
            o_ref[l] = jnp.dot(s, w_ref[l], precision=HI, preferred_element_type=f32)

    return pl.pallas_call(body, in_specs=[VMEM, VMEM], out_specs=VMEM, out_shape=S((nl, CROWS, cols), f32),
                          compiler_params=pltpu.CompilerParams(vmem_limit_bytes=VMEM_LIMIT_BYTES), name=name)(c16, modw)


def mod_bwd(c16, modw, dm_sh, dm_all, name):
    nl, _, cols = modw.shape

    def body(c_ref, w_ref, dm_ref, dmall_ref, dw_ref, dc_ref, db_ref):
        cv = c_ref[...]
        sg = _sigmoid(cv)
        s = cv * sg
        ds_dc = sg * (1.0 + cv * (1.0 - sg))
        is_ctx = lax.broadcasted_iota(jnp.int32, (CROWS, D), 0) >= N_DEV
        dc = jnp.zeros((1, D), f32)
        for l in range(nl):
            dm = dm_ref[l]
            dw_ref[l] = lax.dot_general(s, dm, (((0,), (0,)), ((), ())), precision=HI, preferred_element_type=f32)
            dsv = lax.dot_general(dm, w_ref[l], (((1,), (1,)), ((), ())), precision=HI, preferred_element_type=f32)
            dc = dc + jnp.sum(jnp.where(is_ctx, dsv * ds_dc, 0.0), axis=0, keepdims=True)
            db_ref[pl.ds(l, 1), :] = jnp.sum(dmall_ref[l], axis=0, keepdims=True)
        dc_ref[...] = dc

    return pl.pallas_call(
        body, in_specs=[VMEM, VMEM, VMEM, VMEM], out_specs=[VMEM, VMEM, VMEM],
        out_shape=[S(modw.shape, f32), S((1, D), f32), S((nl, 6 * D), f32)],
        compiler_params=pltpu.CompilerParams(vmem_limit_bytes=VMEM_LIMIT_BYTES), name=name)(c16, modw, dm_sh, dm_all)


def adamw(w, g, m, v, name):
    R, C = w.shape
    rb = R if R <= 512 else max(r_ for r_ in range(8, 513, 8) if R % r_ == 0)
    bc1 = 1.0 - ADAM_B1 ** ADAM_STEP
    bc2 = 1.0 - ADAM_B2 ** ADAM_STEP

    def body(w_ref, g_ref, m_ref, v_ref, d_ref, nm_ref, nv_ref):
        gv = g_ref[...]
        m_new = ADAM_B1 * m_ref[...] + (1.0 - ADAM_B1) * gv
        v_new = ADAM_B2 * v_ref[...] + (1.0 - ADAM_B2) * (gv * gv)
        m_hat = m_new / bc1
        v_hat = v_new / bc2
        d_ref[...] = -ADAM_LR * (m_hat / (jnp.sqrt(v_hat) + ADAM_EPS) + ADAM_WD * w_ref[...])
        nm_ref[...] = m_new
        nv_ref[...] = v_new

    blk = pl.BlockSpec((rb, C), lambda i: (i, 0))
    return pl.pallas_call(body, grid=(R // rb,), in_specs=[blk] * 4, out_specs=[blk] * 3,
                          out_shape=[S((R, C), f32)] * 3, compiler_params=_cparams("parallel"), name=name)(w, g, m, v)


def _me():
    return lax.axis_index("x"), lax.axis_index("y"), lax.axis_index("c")


def allgather_small(x, name, with_sum=False, after=None):
    r, w = x.shape
    extra = () if after is None else (after,)

    def body(x_ref, *refs):
        refs = refs[len(extra):]
        if with_sum:
            out_ref, sum_ref, send_sems, recv_sems = refs
        else:
            out_ref, send_sems, recv_sems = refs
        mx, my, mc = _me()
        me = 4 * mx + 2 * my + mc
        out_ref[me] = x_ref[...]
        peers = []
        for k in range(1, N_DEV):
            kx, ky, kc = (k >> 2) & 1, (k >> 1) & 1, k & 1
            peers.append((mx + kx - 2 * mx * kx, my + ky - 2 * my * ky, mc + kc - 2 * mc * kc))
        copies = []
        for k, peer in enumerate(peers):
            cp = pltpu.make_async_remote_copy(src_ref=x_ref, dst_ref=out_ref.at[me], send_sem=send_sems.at[k],
                                              recv_sem=recv_sems.at[k], device_id=peer, device_id_type=MESH)
            cp.start()
            copies.append(cp)
        for k, (px, py, pc) in enumerate(peers):
            pltpu.make_async_remote_copy(src_ref=x_ref, dst_ref=out_ref.at[4 * px + 2 * py + pc], send_sem=send_sems.at[k],
                                         recv_sem=recv_sems.at[k], device_id=(px, py, pc), device_id_type=MESH).wait_recv()
        for cp in copies:
            cp.wait_send()
        if with_sum:
            acc = out_ref[0]
            for j in range(1, N_DEV):
                acc = acc + out_ref[j]
            sum_ref[...] = acc

    out_shape = [S((N_DEV, r, w), f32)] + ([S((r, w), f32)] if with_sum else [])
    outs = pl.pallas_call(
        body, in_specs=[VMEM] + [ANY] * len(extra), out_specs=[VMEM] * len(out_shape), out_shape=out_shape,
        scratch_shapes=[pltpu.SemaphoreType.DMA((N_DEV - 1,)), pltpu.SemaphoreType.DMA((N_DEV - 1,))],
        compiler_params=pltpu.CompilerParams(vmem_limit_bytes=VMEM_LIMIT_BYTES), name=name)(x, *extra)
    return outs if with_sum else outs[0]


def _tile2d(R, W, max_rows):
    if R <= max_rows:
        return R, W
    fits = [r_ for r_ in range(16, max_rows + 1, 16) if R % r_ == 0]
    return (max(fits), W) if fits else (R, 256)


def add_own(g, r, core, name, twice=False):
    _, _, R, W = g.shape
    rb, wb = _tile2d(R, W, 512)
    nout = 2 if twice else 1

    def body(core_ref, a_ref, b_ref, *o_refs):
        s = (a_ref[...].astype(f32) + b_ref[...].astype(f32)).astype(bf16)
        for o_ref in o_refs:
            o_ref[...] = s

    blk = pl.BlockSpec((None, rb, wb), lambda k, i, j, core_ref: (k, i, j))
    gs = pltpu.PrefetchScalarGridSpec(
        num_scalar_prefetch=1, grid=(4, R // rb, W // wb),
        in_specs=[pl.BlockSpec((None, None, rb, wb), lambda k, i, j, core_ref: (k, core_ref[0], i, j)), blk],
        out_specs=[blk] * nout)
    outs = pl.pallas_call(body, grid_spec=gs, out_shape=[S((4, R, W), bf16)] * nout,
                          compiler_params=_cparams("parallel", "parallel", "parallel"), name=name)(core, g, r)
    return tuple(outs) if twice else outs[0]


HBM_SPEC = pl.BlockSpec(memory_space=pltpu.HBM)
SEM_SPEC = pl.BlockSpec(memory_space=pltpu.SEMAPHORE)


def _chips_copy(p_ref, land_ref, send_sems, recv_sems, a, j):
    x, y, c = _me()
    px, py = [(1 - x, y), (x, 1 - y), (1 - x, 1 - y)][j]
    return pltpu.make_async_remote_copy(src_ref=p_ref.at[2 * px + py], dst_ref=land_ref.at[2 * x + y],
                                        send_sem=send_sems.at[3 * a + j], recv_sem=recv_sems.at[3 * a + j],
                                        device_id=(px, py, c), device_id_type=MESH)


def _chips_wait_copy(p_ref, land_ref, send_sems, recv_sems, a, j):
    x, y, c = _me()
    px, py = [(1 - x, y), (x, 1 - y), (1 - x, 1 - y)][j]
    return pltpu.make_async_remote_copy(src_ref=p_ref.at[2 * px + py], dst_ref=land_ref.at[2 * px + py],
                                        send_sem=send_sems.at[3 * a + j], recv_sem=recv_sems.at[3 * a + j],
                                        device_id=(px, py, c), device_id_type=MESH)


def _xor_peers():
    mx, my, mc = _me()
    peers = []
    for k in range(1, N_DEV):
        kx, ky, kc = (k >> 2) & 1, (k >> 1) & 1, k & 1
        peers.append((mx + kx - 2 * mx * kx, my + ky - 2 * my * ky, mc + kc - 2 * mc * kc))
    return peers


def gather_start(shards, name, after):
    na = len(shards)
    lands = [lax.empty((N_DEV,) + s_.shape, s_.dtype) for s_ in shards]

    def body(*refs):
        x_refs, land_refs = refs[:na], refs[na:2 * na]
        send_sems, recv_sems, local_sems = refs[2 * na + 1:2 * na + 4]
        token = refs[-1]
        mx, my, mc = _me()
        me = 4 * mx + 2 * my + mc
        for a in range(na):
            pltpu.make_async_copy(x_refs[a], land_refs[a].at[me], local_sems.at[a]).start()
            for k, peer in enumerate(_xor_peers()):
                pltpu.make_async_remote_copy(src_ref=x_refs[a], dst_ref=land_refs[a].at[me], send_sem=send_sems.at[7 * a + k],
                                             recv_sem=recv_sems.at[7 * a + k], device_id=peer, device_id_type=MESH).start()
        token[...] = jnp.zeros_like(token)

    arrs = list(shards) + lands
    outs = pl.pallas_call(
        body, name=name, in_specs=[HBM_SPEC] * (2 * na) + [ANY],
        out_shape=[DMA((7 * na,)), DMA((7 * na,)), DMA((na,))] + [pltpu.HBM(t.shape, t.dtype) for t in arrs]
        + [S((8, 128), f32)],
        out_specs=[SEM_SPEC] * 3 + [HBM_SPEC] * (2 * na) + [VMEM],
        input_output_aliases={k: 3 + k for k in range(2 * na)},
        compiler_params=pltpu.CompilerParams(has_side_effects=pltpu.SideEffectType.DATAFLOW_SIDE_EFFECTING),
    )(*[pltpu.with_memory_space_constraint(t, pltpu.HBM) for t in arrs], after)
    return outs[0], outs[1], outs[2], list(outs[3:3 + na]), list(outs[3 + na:3 + 2 * na]), outs[-1]


def gather_wait(send_sems, recv_sems, local_sems, shards, lands, after, name):
    na = len(shards)

    def body(*refs):
        x_refs, land_refs = refs[:na], refs[na:2 * na]
        ssem, rsem, lsem = refs[2 * na:2 * na + 3]
        mx, my, mc = _me()
        me = 4 * mx + 2 * my + mc
        for a in range(na):
            pltpu.make_async_copy(x_refs[a], land_refs[a].at[me], lsem.at[a]).wait()
            for k, (px, py, pc) in enumerate(_xor_peers()):
                cp = pltpu.make_async_remote_copy(src_ref=x_refs[a], dst_ref=land_refs[a].at[4 * px + 2 * py + pc],
                                                  send_sem=ssem.at[7 * a + k], recv_sem=rsem.at[7 * a + k],
                                                  device_id=(px, py, pc), device_id_type=MESH)
                cp.wait_send()
                cp.wait_recv()

    arrs = list(shards) + list(lands)
    outs = pl.pallas_call(
        body, name=name, in_specs=[HBM_SPEC] * (2 * na) + [SEM_SPEC] * 3 + [ANY],
        out_shape=[pltpu.HBM(t.shape, t.dtype) for t in arrs], out_specs=[HBM_SPEC] * (2 * na),
        input_output_aliases={k: k for k in range(2 * na)},
        compiler_params=pltpu.CompilerParams(has_side_effects=pltpu.SideEffectType.DATAFLOW_SIDE_EFFECTING),
    )(*arrs, send_sems, recv_sems, local_sems, after)
    return list(outs[na:])


def chips_start(parts, lands, name):
    na = len(parts)

    def body(*refs):
        p_refs, land_refs = refs[:na], refs[na:2 * na]
        send_sems, recv_sems = refs[2 * na], refs[2 * na + 1]
        token = refs[-1]
        for a in range(na):
            for j in range(3):
                _chips_copy(p_refs[a], land_refs[a], send_sems, recv_sems, a, j).start()
        token[...] = jnp.zeros_like(token)

    arrs = list(parts) + list(lands)
    outs = pl.pallas_call(
        body, name=name, in_specs=[HBM_SPEC] * (2 * na),
        out_shape=[DMA((3 * na,)), DMA((3 * na,))] + [pltpu.HBM(t.shape, t.dtype) for t in arrs] + [S((8, 128), f32)],
        out_specs=[SEM_SPEC, SEM_SPEC] + [HBM_SPEC] * (2 * na) + [VMEM],
        input_output_aliases={k: 2 + k for k in range(2 * na)},
        compiler_params=pltpu.CompilerParams(has_side_effects=pltpu.SideEffectType.DATAFLOW_SIDE_EFFECTING),
    )(*[pltpu.with_memory_space_constraint(t, pltpu.HBM) for t in arrs])
    return outs[0], outs[1], list(outs[2:2 + na]), list(outs[2 + na:2 + 2 * na]), outs[-1]


def chips_wait(send_sems, recv_sems, parts, lands, after, name):
    na = len(parts)

    def body(*refs):
        p_refs, land_refs = refs[:na], refs[na:2 * na]
        ssem, rsem = refs[2 * na], refs[2 * na + 1]
        for a in range(na):
            for j in range(3):
                cp = _chips_wait_copy(p_refs[a], land_refs[a], ssem, rsem, a, j)
                cp.wait_send()
                cp.wait_recv()

    arrs = list(parts) + list(lands)
    outs = pl.pallas_call(
        body, name=name, in_specs=[HBM_SPEC] * (2 * na) + [SEM_SPEC, SEM_SPEC, ANY],
        out_shape=[pltpu.HBM(t.shape, t.dtype) for t in arrs], out_specs=[HBM_SPEC] * (2 * na),
        input_output_aliases={k: k for k in range(2 * na)},
        compiler_params=pltpu.CompilerParams(has_side_effects=pltpu.SideEffectType.DATAFLOW_SIDE_EFFECTING),
    )(*arrs, send_sems, recv_sems, after)
    return list(outs[na:])


def sum_adamw(recv, w, m, v, layer, name, into=None, after=None):
    _, R, W = recv.shape
    rb, wb = _tile2d(R, W, 256)
    bc1 = 1.0 - ADAM_B1 ** ADAM_STEP
    bc2 = 1.0 - ADAM_B2 ** ADAM_STEP
    n_into = 0 if into is None else 4
    extra = () if after is None else (after,)

    def body(r_ref, w_ref, m_ref, v_ref, *refs):
        g_ref, d_ref, nm_ref, nv_ref = refs[n_into + len(extra):]
        gv = r_ref[0].astype(f32)
        for k in range(1, 4):
            gv = gv + r_ref[k].astype(f32)
        m_new = ADAM_B1 * m_ref[...] + (1.0 - ADAM_B1) * gv
        v_new = ADAM_B2 * v_ref[...] + (1.0 - ADAM_B2) * (gv * gv)
        g_ref[...] = gv
        d_ref[...] = -ADAM_LR * ((m_new / bc1) / (jnp.sqrt(v_new / bc2) + ADAM_EPS) + ADAM_WD * w_ref[...])
        nm_ref[...] = m_new
        nv_ref[...] = v_new

    if layer is None:
        wblk = pl.BlockSpec((rb, wb), lambda i, j: (i, j))
        oshape = S((R, W), f32)
    else:
        wblk = pl.BlockSpec((None, rb, wb), lambda i, j: (layer, i, j))
        oshape = S(w.shape, f32)
    return pl.pallas_call(
        body, grid=(R // rb, W // wb),
        in_specs=[pl.BlockSpec((4, rb, wb), lambda i, j: (0, i, j)), wblk, wblk, wblk] + [ANY] * (n_into + len(extra)),
        out_specs=[wblk] * 4, out_shape=[oshape] * 4, input_output_aliases={4 + k: k for k in range(n_into)},
        compiler_params=_cparams("parallel", "parallel"), name=name)(recv, w, m, v, *(into or ()), *extra)


def sum_rows(a, name):
    K, R, W = a.shape
    rb = _pick(R, (512, 256, 128, 64, 32, 16))

    def body(a_ref, o_ref):
        acc = a_ref[0].astype(f32)
        for k in range(1, K):
            acc = acc + a_ref[k].astype(f32)
        o_ref[...] = acc

    return pl.pallas_call(body, grid=(R // rb,), in_specs=[pl.BlockSpec((K, rb, W), lambda i: (0, i, 0))],
                          out_specs=pl.BlockSpec((rb, W), lambda i: (i, 0)), out_shape=S((R, W), f32),
                          compiler_params=_cparams("parallel"), name=name)(a)


DMA = pltpu.SemaphoreType.DMA


class GatherExchange:
    def __init__(self, arrays):
        self.arrays = list(arrays)
        self.na = len(self.arrays)
        self.out_shape = [S((N_DEV,) + a.shape, a.dtype) for a in self.arrays]
        self.scratch = [DMA((7 * self.na,)), DMA((7 * self.na,)), DMA((self.na,))]

    def ops(self, x_refs, out_refs, sems):
        send_sems, recv_sems, local_sems = sems
        na = self.na
        x, y, c = _me()
        me, sibling = (x, y, c), (x, y, 1 - c)
        chips = [(1 - x, y), (x, 1 - y), (1 - x, 1 - y)]

        def rows(a, px, py, pc):
            return out_refs[a].at[4 * px + 2 * py + pc]

        def copy(a, k, block, to, src=None):
            return pltpu.make_async_remote_copy(
                src_ref=rows(a, *block) if src is None else src, dst_ref=rows(a, *block),
                send_sem=send_sems.at[7 * a + k], recv_sem=recv_sems.at[7 * a + k], device_id=to, device_id_type=MESH)

        def local(a):
            return pltpu.make_async_copy(x_refs[a], rows(a, *me), local_sems.at[a])

        def first(a):
            return [copy(a, 0, me, sibling, src=x_refs[a])] + [copy(a, 1 + j, me, (*chip, c), src=x_refs[a])
                                                                for j, chip in enumerate(chips)]

        def start():
            for a in range(na):
                local(a).start()
                for cp in first(a):
                    cp.start()

        def mid():
            for a in range(na):
                for j, chip in enumerate(chips):
                    copy(a, 1 + j, (*chip, c), me).wait_recv()
                    copy(a, 4 + j, (*chip, c), sibling).start()

        def finish():
            for a in range(na):
                copy(a, 0, sibling, me).wait_recv()
                for j, chip in enumerate(chips):
                    copy(a, 4 + j, (*chip, 1 - c), me).wait_recv()
                for cp in first(a) + [copy(a, 4 + j, (*chip, c), sibling) for j, chip in enumerate(chips)]:
                    cp.wait_send()
                local(a).wait()

        return start, mid, finish


class SiblingExchange:
    def __init__(self, arrays):
        self.arrays = list(arrays)
        self.na = len(self.arrays)
        self.out_shape = [S((4,) + g.shape[2:], g.dtype) for g in self.arrays]
        self.scratch = [DMA((self.na,)), DMA((self.na,))]

    def ops(self, g_refs, out_refs, sems):
        send_sems, recv_sems = sems
        x, y, c = _me()

        def copy(a):
            return pltpu.make_async_remote_copy(src_ref=g_refs[a].at[:, 1 - c], dst_ref=out_refs[a],
                                                send_sem=send_sems.at[a], recv_sem=recv_sems.at[a],
                                                device_id=(x, y, 1 - c), device_id_type=MESH)

        def start():
            for a in range(self.na):
                copy(a).start()

        def finish():
            for a in range(self.na):
                copy(a).wait()

        return start, None, finish


class ChipsExchange:
    def __init__(self, arrays):
        self.arrays = list(arrays)
        self.na = len(self.arrays)
        self.out_shape = [S(p.shape, p.dtype) for p in self.arrays]
        self.scratch = [DMA((3 * self.na,)), DMA((3 * self.na,)), DMA((self.na,))]

    def ops(self, p_refs, out_refs, sems):
        send_sems, recv_sems, local_sems = sems
        x, y, c = _me()
        mine = 2 * x + y
        chips = [(1 - x, y), (x, 1 - y), (1 - x, 1 - y)]

        def local(a):
            return pltpu.make_async_copy(p_refs[a].at[mine], out_refs[a].at[mine], local_sems.at[a])

        def send(a, j):
            px, py = chips[j]
            return pltpu.make_async_remote_copy(src_ref=p_refs[a].at[2 * px + py], dst_ref=out_refs[a].at[mine],
                                                send_sem=send_sems.at[3 * a + j], recv_sem=recv_sems.at[3 * a + j],
                                                device_id=(px, py, c), device_id_type=MESH)

        def recv(a, j):
            px, py = chips[j]
            return pltpu.make_async_remote_copy(src_ref=p_refs[a].at[mine], dst_ref=out_refs[a].at[2 * px + py],
                                                send_sem=send_sems.at[3 * a + j], recv_sem=recv_sems.at[3 * a + j],
                                                device_id=(px, py, c), device_id_type=MESH)

        def start():
            for a in range(self.na):
                local(a).start()
                for j in range(3):
                    send(a, j).start()

        def finish():
            for a in range(self.na):
                for j in range(3):
                    recv(a, j).wait_recv()
                for j in range(3):
                    send(a, j).wait_send()
                local(a).wait()

        return start, None, finish


def exchange(ex, name):
    na = ex.na

    def body(*refs):
        start, mid, finish = ex.ops(refs[:na], refs[na:2 * na], refs[2 * na:])
        start()
        if mid is not None:
            mid()
        finish()

    return pl.pallas_call(body, in_specs=[ANY] * na, out_specs=[ANY] * na, out_shape=ex.out_shape,
                          scratch_shapes=ex.scratch, name=name)(*ex.arrays)


def _host_call(body, grid, in_specs, out_specs, out_shape, scratch_shapes, sem, name, args, hosted):
    if hosted is None:
        res = pl.pallas_call(body, grid=grid, in_specs=in_specs, out_specs=out_specs, out_shape=out_shape,
                             scratch_shapes=scratch_shapes, compiler_params=_cparams(*sem), name=name)(*args)
        return res, None
    n_in, n_out, n_sc, na = len(in_specs), len(out_shape), len(scratch_shapes), hosted.na
    nsteps = 1
    for g_ in grid:
        nsteps *= g_
    mid_step = (3 * nsteps) // 4
    i1 = n_in + na
    i2 = i1 + n_out
    i3 = i2 + na
    i4 = i3 + n_sc

    def wrapped(*refs):
        step = pl.program_id(0)
        for ax in range(1, len(grid)):
            step = step * grid[ax] + pl.program_id(ax)
        start, mid, finish = hosted.ops(refs[n_in:i1], refs[i2:i3], refs[i4:])
        pl.when(step == 0)(start)
        if mid is not None:
            pl.when(step == mid_step)(mid)
        body(*refs[:n_in], *refs[i1:i2], *refs[i3:i4])
        pl.when(step == nsteps - 1)(finish)

    res = pl.pallas_call(
        wrapped, grid=grid, in_specs=list(in_specs) + [ANY] * na, out_specs=list(out_specs) + [ANY] * na,
        out_shape=list(out_shape) + hosted.out_shape, scratch_shapes=list(scratch_shapes) + hosted.scratch,
        compiler_params=_cparams(*(("arbitrary",) * len(grid))), name=name)(*args, *hosted.arrays)
    return res[:n_out], res[n_out:]


PACK_ALIGN = 16 * PACK_W


def _pad_to(v, mult):
    n = v.shape[-1]
    extra = (-n) % mult
    if extra == 0:
        return v
    return jnp.concatenate([v, jnp.zeros(v.shape[:-1] + (extra,), v.dtype)], axis=-1)


def _f32_as_bf16_pairs(v):
    return lax.bitcast_convert_type(v.reshape(-1), bf16).reshape(-1)


def _bf16_pairs_as_f32(v):
    return lax.bitcast_convert_type(v.reshape(v.shape[:-1] + (v.shape[-1] // 2, 2)), f32)


def _col_shards(gw):
    lead = gw.shape[:-1]
    n = gw.shape[-1] // N_DEV
    t = gw.reshape(lead + (N_DEV, n))
    t = jnp.moveaxis(t, -2, 0)
    return t.reshape(N_DEV, -1)


def kernel(x, c, ctx, c_ctx, mod_w, mod_b, norm1_w, norm2_w, ssd_w_in, ssd_conv_w, ssd_conv_b, ssd_dt_bias, ssd_a_log, ssd_d, ssd_norm_w, ssd_w_out, conf_w_pw1, conf_b_pw1, conf_w_dw, conf_b_dw, conf_ln_w, conf_ln_b, conf_w_pw2, conf_b_pw2, ffn_w_up, ffn_conv_w, ffn_conv_b, ffn_w_down, final_norm_w, loss_target, m_c_ctx, m_mod_w, m_mod_b, m_norm1_w, m_norm2_w, m_ssd_w_in, m_ssd_conv_w, m_ssd_conv_b, m_ssd_dt_bias, m_ssd_a_log, m_ssd_d, m_ssd_norm_w, m_ssd_w_out, m_conf_w_pw1, m_conf_b_pw1, m_conf_w_dw, m_conf_b_dw, m_conf_ln_w, m_conf_ln_b, m_conf_w_pw2, m_conf_b_pw2, m_ffn_w_up, m_ffn_conv_w, m_ffn_conv_b, m_ffn_w_down, m_final_norm_w, v_c_ctx, v_mod_w, v_mod_b, v_norm1_w, v_norm2_w, v_ssd_w_in, v_ssd_conv_w, v_ssd_conv_b, v_ssd_dt_bias, v_ssd_a_log, v_ssd_d, v_ssd_norm_w, v_ssd_w_out, v_conf_w_pw1, v_conf_b_pw1, v_conf_w_dw, v_conf_b_dw, v_conf_ln_w, v_conf_ln_b, v_conf_w_pw2, v_conf_b_pw2, v_ffn_w_up, v_ffn_conv_w, v_ffn_conv_b, v_ffn_w_down, v_final_norm_w):
    mx, my, mc = _me()
    me = 4 * mx + 2 * my + mc
    L = x.shape[1]
    LC = ctx.shape[1]
    T = LC + L
    w_in_cols = ssd_w_in.shape[2] * N_DEV
    n_dt = w_in_cols - DI - CONVD

    small = [c[0], ssd_conv_w[0], conf_b_pw1[0], conf_w_dw[0], conf_b_dw[0], conf_ln_w[0], conf_ln_b[0], conf_b_pw2[0],
             ffn_conv_w]
    parts = [_f32_as_bf16_pairs(t) for t in small]
    sizes = [p.shape[0] for p in parts]
    small_flat = _pad_to(jnp.concatenate(parts), PACK_ALIGN).reshape(-1, PACK_W)
    w_in, small_g = exchange(GatherExchange([ssd_w_in[0].astype(bf16), small_flat]), "gather_first")
    w_up, w_down = [None, None], [None, None]
    small_g = small_g.reshape(N_DEV, -1)
    offs = [0]
    for s_ in sizes:
        offs.append(offs[-1] + s_)
    sm = [_bf16_pairs_as_f32(small_g[:, offs[i]:offs[i + 1]]) for i in range(len(sizes))]

    def cols(pc, K):
        return jnp.moveaxis(pc.reshape(N_DEV, K, -1), 0, 1).reshape(K, -1)

    c_all = sm[0]
    conv_w5 = cols(sm[1], 5)
    b_pw1 = sm[2].reshape(1, 2 * D)
    w_dw = cols(sm[3], CONF_K)
    b_dw, ln_w, ln_b, b_pw2 = (sm[i].reshape(1, D) for i in (4, 5, 6, 7))
    fcw = sm[8].reshape(N_DEV, 2, 9, FH // N_DEV)
    ffn_cw = [cols(fcw[:, i].reshape(N_DEV, -1), 9) for i in range(2)]
    in_segs = (DI, CONVD, n_dt)
    up_segs = (FH, FH)
    pw1_segs = (D, D)

    c16 = jnp.concatenate([c_all, jnp.broadcast_to(c_ctx[None, :], (N_DEV, D))], axis=0)
    m_sh = mod_fwd(c16, mod_w, "mod_fwd")
    mod_cols = mod_w.shape[2]
    m_gath = allgather_small(m_sh.reshape(2 * CROWS, mod_cols), "gather_mod")
    fly_a = gather_start([ssd_w_out[0].astype(bf16), ffn_w_up[0].astype(bf16), ffn_w_down[0].astype(bf16)],
                         "gather_a_start", m_gath)
    fly_b = gather_start([conf_w_pw1[0].astype(bf16), conf_w_pw2[0].astype(bf16)], "gather_b_start", fly_a[-1])
    fly_c = gather_start([ffn_w_up[1].astype(bf16), ffn_w_down[1].astype(bf16)], "gather_c_start", fly_b[-1])
    m_all = jnp.moveaxis(m_gath.reshape(N_DEV, 2, CROWS, mod_cols), 0, 2).reshape(2, CROWS, 6 * D) + mod_b[:, None, :]
    m_all = m_all + fly_c[-1][0, 0]
    m_lat = lax.dynamic_index_in_dim(m_all, me, axis=1, keepdims=False).reshape(2, 6, 1, D)
    m_ctx = m_all[:, N_DEV].reshape(2, 6, 1, D)
    zero_row = jnp.zeros((1, D), f32)

    def ffn_fwd(h, i, tag, hosted=None):
        a2 = modnorm_fwd(h, norm2_w[i][None], m_lat[i, 4][None], m_lat[i, 3][None], 0, f"ffn{tag}_norm")
        val, gate = smm_fwd(a2, w_up[i], None, up_segs, f"ffn{tag}_up")
        act, gs_, gvds, extra = ffn_gate_fwd(val, gate, ffn_cw[i], ffn_conv_b[i][None], f"ffn{tag}_gate", hosted)
        o2 = matmul(act, w_down[i], "nn", f32, f"ffn{tag}_down")
        h_new = resgate_fwd(h, o2, m_lat[i, 5], zero_row, f"ffn{tag}_res")
        return h_new, (a2, gate, gs_, gvds, act, o2), extra

    def ffn_bwd(dh, h, i, saved, tag):
        a2, gate, gs_, gvds, act, o2 = saved
        do2, dg2, _ = resgate_bwd(dh, o2, m_lat[i, 5], zero_row, f"ffn{tag}_res_bwd")
        g_down = matmul(act, do2, "tn", bf16, f"ffn{tag}_down_dw")
        dact = matmul(do2, w_down[i], "nt", bf16, f"ffn{tag}_down_dx")
        dval, dgate, dcw, dcb = ffn_gate_bwd(gate, gs_, gvds, ffn_cw[i], dact, f"ffn{tag}_gate_bwd")
        g_up = smm_dw(a2, [dval, dgate], FH // 4, up_segs, 2, True, f"ffn{tag}_up_dw")
        da2, _ = smm_dx([dval, dgate], w_up[i], None, up_segs, bf16, f"ffn{tag}_up_dx")
        dh_in, dn2, dsc2, dsh2 = modnorm_bwd(h, norm2_w[i][None], m_lat[i, 4][None], m_lat[i, 3][None], da2, dh, 0,
                                             f"ffn{tag}_norm_bwd")
        return dh_in, dict(w_up=g_up, w_down=g_down, conv_w=dcw, conv_b=dcb, norm2=dn2, sh2=dsh2[0], sc2=dsc2[0], g2=dg2)

    nctx = LC // Q
    hx = x[0]
    sc0 = jnp.stack([m_ctx[0, 1], m_lat[0, 1]])
    sh0 = jnp.stack([m_ctx[0, 0], m_lat[0, 0]])
    a0 = modnorm_fwd(hx, norm1_w[0][None], sc0, sh0, LC // TB, "ssd_norm", ctx=ctx[0])
    z, xbc_pre, dt_raw = smm_fwd(a0, w_in, None, in_segs, "ssd_in")
    segs = ((0, LC), (LC, L))
    xbc, xbc_dsilu, _ = ssd_conv_fwd(xbc_pre, conv_w5, ssd_conv_b, segs, "ssd_conv")
    dt4 = dt_raw[:, :n_dt].reshape(T, 2, G, HPG)
    dtc = jnp.transpose(dt4, (1, 2, 0, 3))
    dtr = jnp.transpose(dt4, (1, 2, 3, 0))
    bias3 = ssd_dt_bias[0].reshape(2, G, HPG)
    alog3 = ssd_a_log[0].reshape(2, G, HPG)
    bc_, br_ = bias3[:, :, None, :], bias3[:, :, :, None]
    alc, alr = alog3[:, :, None, :], alog3[:, :, :, None]
    (y2, s_in_all), _ = ssd_scan_fwd(xbc, dtc, dtr, bc_, br_, alc, alr, nctx, "ssd_scan")
    dexp = jnp.repeat(ssd_d[0], P)[None, :]
    yn = ssd_gate_fwd(y2, xbc, z, dexp, ssd_norm_w, LC // GTB, "ssd_gate")
    w_out_g, w_up[0], w_down0_g = gather_wait(*fly_a[:5], yn, "gather_a_wait")
    w_out = w_out_g.reshape(DI, D)
    w_down[0] = w_down0_g.reshape(FH, D)
    o_ssd = matmul(yn, w_out, "nn", f32, "ssd_out")
    h1 = resgate_fwd(hx, o_ssd, m_lat[0, 2], zero_row, "ssd_res")
    h2, ffn0_saved, _ = ffn_fwd(h1, 0, "0")

    a1 = modnorm_fwd(h2, norm1_w[1][None], m_lat[1, 1][None], m_lat[1, 0][None], 0, "conf_norm")
    w_pw1, w_pw2_g = gather_wait(*fly_b[:5], a1, "gather_b_wait")
    w_pw2 = w_pw2_g.reshape(D, D)
    pa, pg = smm_fwd(a1, w_pw1, None, pw1_segs, "conf_pw1")
    dwc, _ = conf_glu_conv_fwd(pa, pg, b_pw1, w_dw, b_dw, "conf_conv")
    s1 = ln_silu_fwd(dwc, ln_w, ln_b, "conf_ln")
    o_conf = matmul(s1, w_pw2, "nn", f32, "conf_pw2")
    h3 = resgate_fwd(h2, o_conf, m_lat[1, 2], b_pw2, "conf_res")
    w_up[1], w_down1_g = gather_wait(*fly_c[:5], h3, "gather_c_wait")
    w_down[1] = w_down1_g.reshape(FH, D)
    h4, ffn1_saved, _ = ffn_fwd(h3, 1, "1")

    loss_part, dh4, g_final = final_loss(h4, final_norm_w[None], loss_target[0], "loss_head")
    dh3, gf1 = ffn_bwd(dh4, h3, 1, ffn1_saved, "1")

    do_conf, dg1_1, g_b_pw2 = resgate_bwd(dh3, o_conf, m_lat[1, 2], b_pw2, "conf_res_bwd")
    g_pw2 = matmul(s1, do_conf, "tn", bf16, "conf_pw2_dw")
    ds1 = matmul(do_conf, w_pw2, "nt", bf16, "conf_pw2_dx")
    ddwc, g_ln_w, g_ln_b = ln_silu_bwd(dwc, ln_w, ln_b, ds1, "conf_ln_bwd")
    dpa, dpg, dba, dbg, g_w_dw, g_b_dw = conf_glu_conv_bwd(pa, pg, b_pw1, w_dw, ddwc, "conf_conv_bwd")
    g_b_pw1 = jnp.concatenate([dba, dbg], axis=1)
    g_pw1 = smm_dw(a1, [dpa, dpg], 2 * D // N_DEV, pw1_segs, 1, False, "conf_pw1_dw")
    da1, _ = smm_dx([dpa, dpg], w_pw1, None, pw1_segs, bf16, "conf_pw1_dx")
    dh2, g_n1_1, dsc1_1, dsh1_1 = modnorm_bwd(h2, norm1_w[1][None], m_lat[1, 1][None], m_lat[1, 0][None], da1, dh3, 0,
                                              "conf_norm_bwd")
    dh1, gf0 = ffn_bwd(dh2, h1, 0, ffn0_saved, "0")

    do_ssd, dg1_0, _ = resgate_bwd(dh1, o_ssd, m_lat[0, 2], zero_row, "ssd_res_bwd")
    g_w_out = matmul(yn, do_ssd, "tn", bf16, "ssd_out_dw")
    dyn = matmul(do_ssd, w_out, "nt", bf16, "ssd_out_dx")
    core = mc.reshape(1).astype(jnp.int32)

    def by_device(t):
        return t.reshape((4, 2, -1, t.shape[-1]))

    early = [by_device(t) for t in (gf1["w_up"], gf1["w_down"], g_pw2, g_pw1, gf0["w_up"], gf0["w_down"], g_w_out)]
    (dy, dz, g_dexp, g_ssd_norm), early_sib = ssd_gate_bwd(
        y2, xbc, z, dexp, ssd_norm_w, dyn, LC // GTB, "ssd_gate_bwd", SiblingExchange(early))
    early_part = [add_own(t, r_, core, f"reduce_add{i}") for i, (t, r_) in enumerate(zip(early, early_sib))]
    (dxbc2, ddtc, ddtr, dbc, dbr, dalc, dalr), early_red = ssd_scan_bwd(
        xbc, dtc, dtr, bc_, br_, alc, alr, s_in_all, dy, nctx, "ssd_scan_bwd", ChipsExchange(early_part))
    ddt = (jnp.transpose(ddtc, (2, 0, 1, 3)) + jnp.transpose(ddtr, (3, 0, 1, 2))).reshape(T, n_dt)
    g_dt_bias = (dbc[:, :, 0, :] + dbr[:, :, :, 0]).reshape(2, NH_SSD)
    g_a_log = (dalc[:, :, 0, :] + dalr[:, :, :, 0]).reshape(2, NH_SSD)
    g_ssd_d = g_dexp[0, :NH_SSD]
    du, g_conv_w5, g_conv_b5 = ssd_conv_bwd(xbc_pre, conv_w5, xbc_dsilu, dxbc2, dy, dexp, segs, "ssd_conv_bwd")
    ddt_p = _pad_to(ddt, 128).astype(bf16)
    g_w_in = smm_dw(a0, [dz, du, ddt_p], w_in.shape[-1], in_segs, 2, True, "ssd_in_dw")
    g_ffn_cw = jnp.stack([gf0["conv_w"], gf1["conv_w"]])
    small_shards = [_col_shards(t) for t in (g_conv_w5, g_b_pw1, g_w_dw, g_b_dw, g_ln_w, g_ln_b, g_b_pw2, g_ffn_cw)]
    gsizes = [s_.shape[1] for s_ in small_shards]
    g_small = _pad_to(jnp.concatenate(small_shards, axis=1), PACK_ALIGN).astype(bf16)
    late = [by_device(g_w_in), by_device(g_small.reshape(N_DEV, -1, PACK_W))]
    da0, late_sib = smm_dx([dz, du, ddt_p], w_in, None, in_segs, f32, "ssd_in_dx", SiblingExchange(late))
    late_part = [add_own(t, r_, core, f"reduce_add_late{i}", twice=True) for i, (t, r_) in enumerate(zip(late, late_sib))]
    late_flying = chips_start([p_[0] for p_ in late_part], [p_[1] for p_ in late_part], "reduce_chips_late_start")
    dh0, g_n1_0, dsc1_0, dsh1_0 = modnorm_bwd(hx, norm1_w[0][None], sc0, sh0, da0, dh1, LC // TB, "ssd_norm_bwd",
                                              ctx=ctx[0])
    grad_x = dh0[None]

    r_up1, r_down1, r_pw2, r_pw1, r_up0, r_down0, r_out = early_red
    big = {}
    def tr(t):
        return jnp.swapaxes(t, -1, -2)

    up_t, m_up_t, v_up_t = tr(ffn_w_up), tr(m_ffn_w_up), tr(v_ffn_w_up)
    send_sems, recv_sems, late_p, late_land, token = late_flying
    up0 = sum_adamw(r_up0, up_t, m_up_t, v_up_t, 0, "adamw_ffn_w_up0", after=token)
    up1 = sum_adamw(r_up1, up_t, m_up_t, v_up_t, 1, "adamw_ffn_w_up1", into=up0)
    big["ffn_w_up"] = tuple(tr(t) for t in up1)
    big["conf_w_pw1"] = sum_adamw(r_pw1, conf_w_pw1[0], m_conf_w_pw1[0], v_conf_w_pw1[0], None, "adamw_conf_w_pw1",
                                  after=up1[0])
    big["ssd_w_out"] = sum_adamw(r_out, ssd_w_out[0], m_ssd_w_out[0], v_ssd_w_out[0], None, "adamw_ssd_w_out",
                                 after=big["conf_w_pw1"][0])
    dn0 = sum_adamw(r_down0, ffn_w_down, m_ffn_w_down, v_ffn_w_down, 0, "adamw_ffn_w_down0", after=big["ssd_w_out"][0])
    big["ffn_w_down"] = sum_adamw(r_down1, ffn_w_down, m_ffn_w_down, v_ffn_w_down, 1, "adamw_ffn_w_down1", into=dn0)
    big["conf_w_pw2"] = sum_adamw(r_pw2, conf_w_pw2[0], m_conf_w_pw2[0], v_conf_w_pw2[0], None, "adamw_conf_w_pw2",
                                  after=big["ffn_w_down"][0])

    zeros_d = jnp.zeros((1, D), f32)
    dm_lat = jnp.stack([
        jnp.concatenate([dsh1_0[1], dsc1_0[1], dg1_0, gf0["sh2"], gf0["sc2"], gf0["g2"]], axis=1),
        jnp.concatenate([dsh1_1[0], dsc1_1[0], dg1_1, gf1["sh2"], gf1["sc2"], gf1["g2"]], axis=1)])
    dm_ctx = jnp.stack([
        jnp.concatenate([dsh1_0[0], dsc1_0[0]] + [zeros_d] * 4, axis=1), jnp.zeros((1, 6 * D), f32)])
    dm_mine = jnp.concatenate([dm_lat.reshape(2, 6 * D), dm_ctx.reshape(2, 6 * D),
                               jnp.zeros((4, 6 * D), f32)], axis=0)
    dm_g = allgather_small(dm_mine, "gather_dmod", after=big["conf_w_pw2"][0])
    dm_all = jnp.concatenate([jnp.moveaxis(dm_g[:, 0:2], 0, 1), jnp.moveaxis(dm_g[:, 2:4], 0, 1)], axis=1)
    dm_sh = lax.dynamic_slice_in_dim(dm_all, me * mod_cols, mod_cols, axis=2)
    g_mod_w, g_cctx_part, g_mod_b = mod_bwd(c16, mod_w, dm_sh, dm_all, "mod_bwd")

    rep = [jnp.stack([g_n1_0[0], g_n1_1[0]]), jnp.stack([gf0["norm2"][0], gf1["norm2"][0]]), g_conv_b5, g_dt_bias, g_a_log,
           g_ssd_d, g_ssd_norm, jnp.stack([gf0["conv_b"][0], gf1["conv_b"][0]]), g_final, g_cctx_part, loss_part[:, :1]]
    rep_sizes = [r_.size for r_ in rep]
    rep_flat = _pad_to(jnp.concatenate([r_.reshape(-1) for r_ in rep]), 8 * PACK_W).reshape(-1, PACK_W)
    _, rep_sum = allgather_small(rep_flat, "reduce_replicated", with_sum=True)
    rep_sum = rep_sum.reshape(-1)
    roffs = [0]
    for s_ in rep_sizes:
        roffs.append(roffs[-1] + s_)
    rp = [rep_sum[roffs[i]:roffs[i + 1]] for i in range(len(rep_sizes))]
    loss = rp[10].reshape(())

    r_in, r_small = chips_wait(send_sems, recv_sems, late_p, late_land, rep_sum, "reduce_chips_late_wait")
    w_in_res = sum_adamw(r_in, tr(ssd_w_in[0]), tr(m_ssd_w_in[0]), tr(v_ssd_w_in[0]), None, "adamw_ssd_w_in")
    big["ssd_w_in"] = tuple(tr(t) for t in w_in_res)
    g_flat = sum_rows(r_small, "reduce_sum_small").reshape(-1)
    goffs = [0]
    for s_ in gsizes:
        goffs.append(goffs[-1] + s_)
    gs = [g_flat[goffs[i]:goffs[i + 1]] for i in range(len(gsizes))]
    grads = {
        "c_ctx": rp[9], "mod_w": g_mod_w, "mod_b": g_mod_b, "norm1_w": rp[0], "norm2_w": rp[1],
        "ssd_conv_w": gs[0], "ssd_conv_b": rp[2], "ssd_dt_bias": rp[3], "ssd_a_log": rp[4], "ssd_d": rp[5],
        "ssd_norm_w": rp[6], "conf_b_pw1": gs[1], "conf_w_dw": gs[2],
        "conf_b_dw": gs[3], "conf_ln_w": gs[4], "conf_ln_b": gs[5], "conf_b_pw2": gs[6],
        "ffn_conv_w": gs[7], "ffn_conv_b": rp[7], "final_norm_w": rp[8],
    }
    weights = dict(c_ctx=c_ctx, mod_w=mod_w, mod_b=mod_b, norm1_w=norm1_w, norm2_w=norm2_w, ssd_w_in=ssd_w_in, ssd_conv_w=ssd_conv_w, ssd_conv_b=ssd_conv_b, ssd_dt_bias=ssd_dt_bias, ssd_a_log=ssd_a_log, ssd_d=ssd_d, ssd_norm_w=ssd_norm_w, ssd_w_out=ssd_w_out, conf_w_pw1=conf_w_pw1, conf_b_pw1=conf_b_pw1, conf_w_dw=conf_w_dw, conf_b_dw=conf_b_dw, conf_ln_w=conf_ln_w, conf_ln_b=conf_ln_b, conf_w_pw2=conf_w_pw2, conf_b_pw2=conf_b_pw2, ffn_w_up=ffn_w_up, ffn_conv_w=ffn_conv_w, ffn_conv_b=ffn_conv_b, ffn_w_down=ffn_w_down, final_norm_w=final_norm_w)
    m_in = dict(c_ctx=m_c_ctx, mod_w=m_mod_w, mod_b=m_mod_b, norm1_w=m_norm1_w, norm2_w=m_norm2_w, ssd_w_in=m_ssd_w_in, ssd_conv_w=m_ssd_conv_w, ssd_conv_b=m_ssd_conv_b, ssd_dt_bias=m_ssd_dt_bias, ssd_a_log=m_ssd_a_log, ssd_d=m_ssd_d, ssd_norm_w=m_ssd_norm_w, ssd_w_out=m_ssd_w_out, conf_w_pw1=m_conf_w_pw1, conf_b_pw1=m_conf_b_pw1, conf_w_dw=m_conf_w_dw, conf_b_dw=m_conf_b_dw, conf_ln_w=m_conf_ln_w, conf_ln_b=m_conf_ln_b, conf_w_pw2=m_conf_w_pw2, conf_b_pw2=m_conf_b_pw2, ffn_w_up=m_ffn_w_up, ffn_conv_w=m_ffn_conv_w, ffn_conv_b=m_ffn_conv_b, ffn_w_down=m_ffn_w_down, final_norm_w=m_final_norm_w)
    v_in = dict(c_ctx=v_c_ctx, mod_w=v_mod_w, mod_b=v_mod_b, norm1_w=v_norm1_w, norm2_w=v_norm2_w, ssd_w_in=v_ssd_w_in, ssd_conv_w=v_ssd_conv_w, ssd_conv_b=v_ssd_conv_b, ssd_dt_bias=v_ssd_dt_bias, ssd_a_log=v_ssd_a_log, ssd_d=v_ssd_d, ssd_norm_w=v_ssd_norm_w, ssd_w_out=v_ssd_w_out, conf_w_pw1=v_conf_w_pw1, conf_b_pw1=v_conf_b_pw1, conf_w_dw=v_conf_w_dw, conf_b_dw=v_conf_b_dw, conf_ln_w=v_conf_ln_w, conf_ln_b=v_conf_ln_b, conf_w_pw2=v_conf_w_pw2, conf_b_pw2=v_conf_b_pw2, ffn_w_up=v_ffn_w_up, ffn_conv_w=v_ffn_conv_w, ffn_conv_b=v_ffn_conv_b, ffn_w_down=v_ffn_w_down, final_norm_w=v_final_norm_w)

    out_g, out_d, out_m, out_v = [], [], [], []
    for name_, w_ in weights.items():
        shape = w_.shape
        if name_ in big:
            for lst, t in zip((out_g, out_d, out_m, out_v), big[name_]):
                lst.append(t.reshape(shape))
            continue
        cols2 = shape[-1] if len(shape) > 1 else shape[0]
        g2 = grads[name_].reshape(-1, cols2)
        d_, nm_, nv_ = adamw(w_.reshape(-1, cols2), g2, m_in[name_].reshape(-1, cols2), v_in[name_].reshape(-1, cols2),
                             f"adamw_{name_}")
        out_g.append(g2.reshape(shape))
        out_d.append(d_.reshape(shape))
        out_m.append(nm_.reshape(shape))
        out_v.append(nv_.reshape(shape))
    return (loss, grad_x, *out_g, *out_d, *out_m, *out_v)
```

```python
import functools

import jax
import jax.numpy as jnp
from jax import lax
from jax.experimental import pallas as pl
from jax.experimental.pallas import tpu as pltpu

f32 = jnp.float32
bf16 = jnp.bfloat16
HI = lax.Precision.HIGHEST
S = jax.ShapeDtypeStruct
MESH = pl.DeviceIdType.MESH
ANY = pl.BlockSpec(memory_space=pl.ANY)
VMEM = pl.BlockSpec(memory_space=pltpu.VMEM)

N_DEV = 8
D = 1024
DI = 2048
CONVD = 4096
FH = 2816
GRID_W = 64
Q = 128
HPG = 4
P = 64
N = 128
G = 8
GW = HPG * P
NH_SSD = G * HPG
EPS = 1e-6
ADAM_LR, ADAM_B1, ADAM_B2, ADAM_EPS, ADAM_WD, ADAM_STEP = 0.001, 0.9, 0.999, 1e-08, 0.01, 10
VMEM_LIMIT_BYTES = 56 * 1024 * 1024
PACK_W = 1024
TB = 256


def _cparams(*sem):
    return pltpu.CompilerParams(dimension_semantics=sem, vmem_limit_bytes=VMEM_LIMIT_BYTES)


def _pick(n, prefs):
    for p in prefs:
        if n % p == 0:
            return p
    return n


def _sigmoid(x):
    return 1.0 / (1.0 + jnp.exp(-x))


def _softplus(x):
    return jnp.maximum(x, 0.0) + jnp.log(1.0 + jnp.exp(-jnp.abs(x)))


def matmul(a, b, mode, out_dtype, name):
    if mode == "nn":
        (M, K), (_, Nn) = a.shape, b.shape
        bm, bn, bk = _pick(M, (512, 384, 256, 128)), Nn, K
    elif mode == "tn":
        (K, M), (_, Nn) = a.shape, b.shape
        bm, bn, bk = M, Nn, _pick(K, (256, 128))
    else:
        (M, K), (Nn, _) = a.shape, b.shape
        bm, bn, bk = _pick(M, (512, 384, 256, 128)), Nn, K
    nk = K // bk
    dims = {"nn": (((1,), (0,)), ((), ())), "tn": (((0,), (0,)), ((), ())), "nt": (((1,), (1,)), ((), ()))}[mode]

    def body(a_ref, b_ref, o_ref, acc_ref):
        k = pl.program_id(2)

        @pl.when(k == 0)
        def _():
            acc_ref[...] = jnp.zeros_like(acc_ref)

        acc_ref[...] += lax.dot_general(a_ref[...].astype(bf16), b_ref[...].astype(bf16), dims,
                                        preferred_element_type=f32)

        @pl.when(k == nk - 1)
        def _():
            o_ref[...] = acc_ref[...].astype(out_dtype)

    if mode == "nn":
        a_spec = pl.BlockSpec((bm, bk), lambda i, j, k: (i, k))
        b_spec = pl.BlockSpec((bk, bn), lambda i, j, k: (k, j))
    elif mode == "tn":
        a_spec = pl.BlockSpec((bk, bm), lambda i, j, k: (k, i))
        b_spec = pl.BlockSpec((bk, bn), lambda i, j, k: (k, j))
    else:
        a_spec = pl.BlockSpec((bm, bk), lambda i, j, k: (i, k))
        b_spec = pl.BlockSpec((bn, bk), lambda i, j, k: (j, k))
    return pl.pallas_call(
        body, grid=(M // bm, Nn // bn, nk), in_specs=[a_spec, b_spec],
        out_specs=pl.BlockSpec((bm, bn), lambda i, j, k: (i, j)),
        out_shape=S((M, Nn), out_dtype), scratch_shapes=[pltpu.VMEM((bm, bn), f32)],
        compiler_params=_cparams("parallel", "parallel", "arbitrary"), name=name,
    )(a, b)


SMM_BM = 256
SMM_ROWS = (256,)


def _shard_pieces(seg_widths, n):
    bounds = [0]
    for sw in seg_widths:
        bounds.append(bounds[-1] + sw)
    assert bounds[-1] == N_DEV * n, (seg_widths, n)
    out = []
    for j in range(N_DEV):
        lo, hi = j * n, (j + 1) * n
        pcs = []
        for si in range(len(seg_widths)):
            a, b = max(lo, bounds[si]), min(hi, bounds[si + 1])
            if a < b:
                pcs.append((si, a - bounds[si], a - lo, b - a))
        out.append(pcs)
    return out


def _w_spec(w, layer):
    if layer is None:
        return pl.BlockSpec(w.shape, lambda *idx: (0, 0, 0))
    return pl.BlockSpec((N_DEV, None) + w.shape[2:], lambda *idx: (0, layer, 0, 0))


def smm_fwd(a, w, layer, seg_widths, name, hosted=None):
    M, K = a.shape
    n = w.shape[-1]
    pieces = _shard_pieces(seg_widths, n)
    padded = [sw + (-sw) % 128 for sw in seg_widths]
    bm = _pick(M, SMM_ROWS)

    def body(a_ref, w_ref, *o_refs):
        av = a_ref[...]
        for si, sw in enumerate(seg_widths):
            if padded[si] != sw:
                o_refs[si][:, pl.ds(padded[si] - 128, 128)] = jnp.zeros((bm, 128), f32)
        for j in range(N_DEV):
            for si, soff, woff, wd in pieces[j]:
                o_refs[si][:, pl.ds(soff, wd)] = jnp.dot(av, w_ref[j, :, pl.ds(woff, wd)], preferred_element_type=f32)

    outs, extra = _host_call(
        body, (M // bm,), [pl.BlockSpec((bm, K), lambda i: (i, 0)), _w_spec(w, layer)],
        [pl.BlockSpec((bm, pw), lambda i: (i, 0)) for pw in padded], [S((M, pw), f32) for pw in padded], [],
        ("parallel",), name, (a, w), hosted)
    return outs if hosted is None else (outs, extra)


def smm_dx(d_segs, w, layer, seg_widths, out_dtype, name, hosted=None):
    M = d_segs[0].shape[0]
    K, n = w.shape[-2], w.shape[-1]
    pieces = _shard_pieces(seg_widths, n)
    ns = len(d_segs)
    bm = _pick(M, SMM_ROWS)

    def body(*refs):
        d_refs, w_ref, o_ref = refs[:ns], refs[ns], refs[ns + 1]
        acc = jnp.zeros((bm, K), f32)
        for j in range(N_DEV):
            for si, soff, woff, wd in pieces[j]:
                acc = acc + lax.dot_general(d_refs[si][:, pl.ds(soff, wd)], w_ref[j, :, pl.ds(woff, wd)],
                                            (((1,), (1,)), ((), ())), preferred_element_type=f32)
        o_ref[...] = acc.astype(out_dtype)

    (out,), extra = _host_call(
        body, (M // bm,),
        [pl.BlockSpec((bm, d.shape[1]), lambda i: (i, 0)) for d in d_segs] + [_w_spec(w, layer)],
        [pl.BlockSpec((bm, K), lambda i: (i, 0))], [S((M, K), out_dtype)], [], ("parallel",), name,
        (*d_segs, w), hosted)
    return out, extra


def smm_dw(a, d_segs, n, seg_widths, ngrp, transposed, name):
    M, K = a.shape
    pieces = _shard_pieces(seg_widths, n)
    per = N_DEV // ngrp
    nI = M // SMM_BM
    ns = len(d_segs)
    shard = (n, K) if transposed else (K, n)

    def body(*refs):
        a_ref, d_refs, o_ref, acc_ref = refs[0], refs[1:1 + ns], refs[1 + ns], refs[2 + ns]
        grp = pl.program_id(0)
        i = pl.program_id(1)

        @pl.when(i == 0)
        def _():
            acc_ref[...] = jnp.zeros_like(acc_ref)

        av = a_ref[...]
        for gs in range(ngrp):
            def one_group(gs=gs):
                for jj in range(per):
                    for si, soff, woff, wd in pieces[gs * per + jj]:
                        dv = d_refs[si][:, pl.ds(soff, wd)]
                        if transposed:
                            acc_ref[jj, pl.ds(woff, wd), :] += lax.dot_general(
                                dv, av, (((0,), (0,)), ((), ())), preferred_element_type=f32)
                        else:
                            acc_ref[jj, :, pl.ds(woff, wd)] += lax.dot_general(
                                av, dv, (((0,), (0,)), ((), ())), preferred_element_type=f32)
            pl.when(grp == gs)(one_group)

        @pl.when(i == nI - 1)
        def _():
            o_ref[...] = acc_ref[...].astype(bf16)

    return pl.pallas_call(
        body, grid=(ngrp, nI),
        in_specs=[pl.BlockSpec((SMM_BM, K), lambda g, i: (i, 0))]
        + [pl.BlockSpec((SMM_BM, d.shape[1]), lambda g, i: (i, 0)) for d in d_segs],
        out_specs=pl.BlockSpec((per,) + shard, lambda g, i: (g, 0, 0)), out_shape=S((N_DEV,) + shard, bf16),
        scratch_shapes=[pltpu.VMEM((per,) + shard, f32)],
        compiler_params=_cparams("arbitrary", "arbitrary"), name=name)(a, *d_segs)


def _modnorm_f(h, w, sc, sh):
    y = h * lax.rsqrt(jnp.mean(h * h, axis=-1, keepdims=True) + EPS)
    return (y * w) * (1.0 + sc) + sh


def _kind_specs(nctxb):
    if nctxb > 0:
        return pl.BlockSpec((None, 1, D), lambda i: (jnp.where(i < nctxb, 0, 1), 0, 0))
    return pl.BlockSpec((None, 1, D), lambda i: (0, 0, 0))


def _two_part_specs(nctxb):
    return (pl.BlockSpec((TB, D), lambda i: (jnp.minimum(i, nctxb - 1), 0)),
            pl.BlockSpec((TB, D), lambda i: (jnp.maximum(i - nctxb, 0), 0)))


def modnorm_fwd(h, w, sc, sh, nctxb, name, ctx=None):
    if ctx is None:
        T = h.shape[0]

        def body(h_ref, w_ref, sc_ref, sh_ref, o_ref):
            o_ref[...] = _modnorm_f(h_ref[...], w_ref[...], sc_ref[...], sh_ref[...]).astype(bf16)

        hspecs, hargs = [pl.BlockSpec((TB, D), lambda i: (i, 0))], (h,)
    else:
        T = h.shape[0] + ctx.shape[0]

        def body(c_ref, h_ref, w_ref, sc_ref, sh_ref, o_ref):
            hv = jnp.where(pl.program_id(0) < nctxb, c_ref[...], h_ref[...])
            o_ref[...] = _modnorm_f(hv, w_ref[...], sc_ref[...], sh_ref[...]).astype(bf16)

        hspecs, hargs = list(_two_part_specs(nctxb)), (ctx, h)
    row = pl.BlockSpec((1, D), lambda i: (0, 0))
    ks = _kind_specs(nctxb)
    return pl.pallas_call(body, grid=(T // TB,), in_specs=hspecs + [row, ks, ks],
                          out_specs=pl.BlockSpec((TB, D), lambda i: (i, 0)), out_shape=S((T, D), bf16),
                          compiler_params=_cparams("parallel"), name=name)(*hargs, w, sc, sh)


def modnorm_bwd(h, w, sc, sh, da, dres, nctxb, name, ctx=None):
    T = h.shape[0] + (0 if ctx is None else ctx.shape[0])
    kinds = sc.shape[0]
    nh = 1 if ctx is None else 2

    def body(*refs):
        w_ref, sc_ref, sh_ref, da_ref, dres_ref, dh_ref, dw_ref, dsc_ref, dsh_ref = refs[nh:]
        i = pl.program_id(0)
        hv = refs[0][...] if ctx is None else jnp.where(i < nctxb, refs[0][...], refs[1][...])
        _, vjp = jax.vjp(_modnorm_f, hv, w_ref[...], sc_ref[...], sh_ref[...])
        dh, dw, dsc, dsh = vjp(da_ref[...].astype(f32))
        dh_ref[...] = dres_ref[...] + dh

        @pl.when(i == 0)
        def _():
            dw_ref[...] = jnp.zeros_like(dw_ref)

        @pl.when((i == 0) | (i == nctxb))
        def _():
            dsc_ref[...] = jnp.zeros_like(dsc_ref)
            dsh_ref[...] = jnp.zeros_like(dsh_ref)

        dw_ref[...] += dw
        dsc_ref[...] += dsc
        dsh_ref[...] += dsh

    blk = pl.BlockSpec((TB, D), lambda i: (i, 0))
    lat = pl.BlockSpec((TB, D), lambda i: (jnp.maximum(i - nctxb, 0), 0))
    row = pl.BlockSpec((1, D), lambda i: (0, 0))
    ks = _kind_specs(nctxb)
    hspecs, hargs = ([blk], (h,)) if ctx is None else (list(_two_part_specs(nctxb)), (ctx, h))
    return pl.pallas_call(
        body, grid=(T // TB,), in_specs=hspecs + [row, ks, ks, blk, lat], out_specs=[lat, row, ks, ks],
        out_shape=[S((T - nctxb * TB, D), f32), S((1, D), f32), S((kinds, 1, D), f32), S((kinds, 1, D), f32)],
        compiler_params=_cparams("arbitrary"), name=name)(*hargs, w, sc, sh, da, dres)


def resnorm_fwd(h, o, g, b, w, sc, sh, name):
    T = h.shape[0]

    def body(h_ref, o_ref, g_ref, b_ref, w_ref, sc_ref, sh_ref, hn_ref, a_ref):
        hn = h_ref[...] + g_ref[...] * (o_ref[...] + b_ref[...])
        hn_ref[...] = hn
        a_ref[...] = _modnorm_f(hn, w_ref[...], sc_ref[...], sh_ref[...]).astype(bf16)

    blk = pl.BlockSpec((TB, D), lambda i: (i, 0))
    row = pl.BlockSpec((1, D), lambda i: (0, 0))
    return pl.pallas_call(body, grid=(T // TB,), in_specs=[blk, blk, row, row, row, row, row], out_specs=[blk, blk],
                          out_shape=[S((T, D), f32), S((T, D), bf16)], compiler_params=_cparams("parallel"),
                          name=name)(h, o, g, b, w, sc, sh)


def normres_bwd(h, w, sc, sh, da, dres, o, g, b, name):
    T = h.shape[0]

    def body(h_ref, w_ref, sc_ref, sh_ref, da_ref, dres_ref, o_ref, g_ref, b_ref,
             dh_ref, dw_ref, dsc_ref, dsh_ref, do_ref, dg_ref, db_ref):
        _, vjp = jax.vjp(_modnorm_f, h_ref[...], w_ref[...], sc_ref[...], sh_ref[...])
        dhn, dw, dsc, dsh = vjp(da_ref[...].astype(f32))
        dh = dres_ref[...] + dhn
        dh_ref[...] = dh
        do = g_ref[...] * dh
        do_ref[...] = do.astype(bf16)
        sums = (dw, dsc, dsh, jnp.sum(dh * (o_ref[...] + b_ref[...]), axis=0, keepdims=True),
                jnp.sum(do, axis=0, keepdims=True))

        @pl.when(pl.program_id(0) == 0)
        def _():
            for r_ in (dw_ref, dsc_ref, dsh_ref, dg_ref, db_ref):
                r_[...] = jnp.zeros_like(r_)

        for r_, s_ in zip((dw_ref, dsc_ref, dsh_ref, dg_ref, db_ref), sums):
            r_[...] += s_

    blk = pl.BlockSpec((TB, D), lambda i: (i, 0))
    row = pl.BlockSpec((1, D), lambda i: (0, 0))
    return pl.pallas_call(
        body, grid=(T // TB,), in_specs=[blk, row, row, row, blk, blk, blk, row, row],
        out_specs=[blk, row, row, row, blk, row, row],
        out_shape=[S((T, D), f32), S((1, D), f32), S((1, D), f32), S((1, D), f32), S((T, D), bf16), S((1, D), f32),
                   S((1, D), f32)],
        compiler_params=_cparams("arbitrary"), name=name)(h, w, sc, sh, da, dres, o, g, b)


def final_loss(h, o, g, w, tgt, name):
    T = h.shape[0]

    def f(hv, wv, tv):
        y = (hv * lax.rsqrt(jnp.mean(hv * hv, axis=-1, keepdims=True) + EPS)) * wv
        e = y - tv
        return 0.5 * jnp.sum(jnp.sum(e * e, axis=-1, keepdims=True), axis=0, keepdims=True) * (1.0 / D)

    def body(h_ref, o_ref, g_ref, w_ref, t_ref, loss_ref, dh_ref, dw_ref, do_ref, dg_ref):
        i = pl.program_id(0)
        tv = t_ref[...]
        ov = o_ref[...]
        gv = g_ref[...]
        val, vjp = jax.vjp(lambda a, b_: f(a, b_, tv), h_ref[...] + gv * ov, w_ref[...])
        dh, dw = vjp(jnp.ones((1, 1), f32))
        dh_ref[...] = dh
        do_ref[...] = (gv * dh).astype(bf16)

        @pl.when(i == 0)
        def _():
            loss_ref[...] = jnp.zeros_like(loss_ref)
            dw_ref[...] = jnp.zeros_like(dw_ref)
            dg_ref[...] = jnp.zeros_like(dg_ref)

        loss_ref[...] += jnp.broadcast_to(val, (1, 128))
        dw_ref[...] += dw
        dg_ref[...] += jnp.sum(dh * ov, axis=0, keepdims=True)

    blk = pl.BlockSpec((TB, D), lambda i: (i, 0))
    row = pl.BlockSpec((1, D), lambda i: (0, 0))
    return pl.pallas_call(body, grid=(T // TB,), in_specs=[blk, blk, row, row, blk],
                          out_specs=[pl.BlockSpec((1, 128), lambda i: (0, 0)), blk, row, blk, row],
                          out_shape=[S((1, 128), f32), S((T, D), f32), S((1, D), f32), S((T, D), bf16), S((1, D), f32)],
                          compiler_params=_cparams("arbitrary"), name=name)(h, o, g, w, tgt)


CB = 256
RT = 32
RTB = 16


def _fold8(t):
    acc = t[0:8]
    for k in range(1, t.shape[0] // 8):
        acc = acc + t[8 * k:8 * (k + 1)]
    return acc


def _rows(start, off=0, rt=RT):
    return pl.ds(pl.multiple_of(start + off, 8), rt)


def _rowsb(start, off=0):
    return _rows(start, off, RTB)


def _zero_rows(ref, start, n):
    ref[pl.ds(start, n), :] = jnp.zeros((n, ref.shape[1]), f32)


K5, HALF5, PAD5 = 5, 2, 8


def _shift_copies5(base_ref, s_ref, ln, sign):
    for k in range(K5):
        s_ref[k, pl.ds(0, ln), :] = base_ref[pl.ds(PAD5 + sign * (k - HALF5), ln), :]


def ssd_conv_fwd(u, w, b, segs, name, hosted=None):
    T = u.shape[0]
    maxlen = max(ln for _, ln in segs)

    def body(u_ref, w_ref, b_ref, o_ref, ds_ref, base_ref, s_ref):
        wv = [w_ref[pl.ds(k, 1), :] for k in range(K5)]
        bv = b_ref[...]
        for s0, ln in segs:
            _zero_rows(base_ref, 0, PAD5)
            _zero_rows(base_ref, PAD5 + ln, PAD5)
            base_ref[pl.ds(PAD5, ln), :] = u_ref[pl.ds(s0, ln), :]
            _shift_copies5(base_ref, s_ref, ln, 1)

            def tile(i, carry):
                r = i * RT
                acc = jnp.broadcast_to(bv, (RT, CB))
                for k in range(K5):
                    acc = acc + s_ref[k, _rows(r), :] * wv[k]
                sg = _sigmoid(acc)
                o_ref[_rows(r, s0), :] = acc * sg
                ds_ref[_rows(r, s0), :] = sg * (1.0 + acc * (1.0 - sg))
                return carry

            lax.fori_loop(0, ln // RT, tile, 0, unroll=2)

    cblk = pl.BlockSpec((T, CB), lambda j: (0, j))
    (out, dsilu), extra = _host_call(
        body, (CONVD // CB,),
        [cblk, pl.BlockSpec((K5, CB), lambda j: (0, j)), pl.BlockSpec((1, CB), lambda j: (0, j))],
        [cblk, cblk], [S((T, CONVD), f32), S((T, CONVD), f32)],
        [pltpu.VMEM((maxlen + 2 * PAD5, CB), f32), pltpu.VMEM((K5, maxlen, CB), f32)],
        ("parallel",), name, (u, w, b), hosted)
    return out, dsilu, extra


def ssd_conv_bwd(proj, w, dsilu, dy2, dyskip, dexp, segs, name):
    T = proj.shape[0]
    maxlen = max(ln for _, ln in segs)
    nskip = DI // CB

    def body(u_ref, w_ref, ds_ref, dya_ref, dyb_ref, dsk_ref, dexp_ref, du_ref, dw_ref, db_ref, base_ref, s_ref):
        wv = [w_ref[pl.ds(k, 1), :] for k in range(K5)]
        has_skip = (pl.program_id(0) < nskip).astype(f32) * dexp_ref[...]
        acc8 = tuple(jnp.zeros((8, CB), f32) for _ in range(K5 + 1))
        for s0, ln in segs:
            _zero_rows(base_ref, 0, PAD5)
            _zero_rows(base_ref, PAD5 + ln, PAD5)
            base_ref[pl.ds(PAD5, ln), :] = u_ref[pl.ds(s0, ln), :]
            _shift_copies5(base_ref, s_ref, ln, 1)

            def tile1(i, carry):
                r = i * RTB
                dy = dya_ref[_rowsb(r, s0), :] + dyb_ref[_rowsb(r, s0), :] + has_skip * dsk_ref[_rowsb(r, s0), :]
                dpre = dy * ds_ref[_rowsb(r, s0), :]
                base_ref[_rowsb(r, PAD5), :] = dpre
                new = [carry[k] + _fold8(dpre * s_ref[k, _rowsb(r), :]) for k in range(K5)]
                new.append(carry[K5] + _fold8(dpre))
                return tuple(new)

            acc8 = lax.fori_loop(0, ln // RTB, tile1, acc8, unroll=2)
            _shift_copies5(base_ref, s_ref, ln, -1)

            def tile2(i, carry):
                r = i * RTB
                du = jnp.zeros((RTB, CB), f32)
                for k in range(K5):
                    du = du + s_ref[k, _rowsb(r), :] * wv[k]
                du_ref[_rowsb(r, s0), :] = du.astype(bf16)
                return carry

            lax.fori_loop(0, ln // RTB, tile2, 0, unroll=4)
        for k in range(K5):
            dw_ref[pl.ds(k, 1), :] = jnp.sum(acc8[k], axis=0, keepdims=True)
        db_ref[...] = jnp.sum(acc8[K5], axis=0, keepdims=True)

    cblk = pl.BlockSpec((T, CB), lambda j: (0, j))
    return pl.pallas_call(
        body, grid=(CONVD // CB,),
        in_specs=[cblk, pl.BlockSpec((K5, CB), lambda j: (0, j)), cblk,
                  pl.BlockSpec((None, T, CB), lambda j: (0, 0, j)), pl.BlockSpec((None, T, CB), lambda j: (1, 0, j)),
                  pl.BlockSpec((T, CB), lambda j: (0, jnp.minimum(j, nskip - 1))),
                  pl.BlockSpec((1, CB), lambda j: (0, jnp.minimum(j, nskip - 1)))],
        out_specs=[cblk, pl.BlockSpec((K5, CB), lambda j: (0, j)), pl.BlockSpec((1, CB), lambda j: (0, j))],
        out_shape=[S((T, CONVD), bf16), S((K5, CONVD), f32), S((1, CONVD), f32)],
        scratch_shapes=[pltpu.VMEM((maxlen + 2 * PAD5, CB), f32), pltpu.VMEM((K5, maxlen, CB), f32)],
        compiler_params=_cparams("parallel"), name=name)(proj, w, dsilu, dy2, dy2, dyskip, dexp)


GPAD = GRID_W


def _grid_copies(g_ref, src, L):
    col = lax.broadcasted_iota(jnp.int32, (L, CB), 0) & (GRID_W - 1)
    for d in range(3):
        _zero_rows(g_ref.at[d], 0, GPAD)
        _zero_rows(g_ref.at[d], GPAD + L, GPAD)
    g_ref[1, pl.ds(GPAD, L), :] = src
    g_ref[0, pl.ds(GPAD, L), :] = jnp.where(col != 0, g_ref[1, pl.ds(GPAD - 1, L), :], 0.0)
    g_ref[2, pl.ds(GPAD, L), :] = jnp.where(col != GRID_W - 1, g_ref[1, pl.ds(GPAD + 1, L), :], 0.0)


def ffn_gate_fwd(val, gate, cw, cb_, name, hosted=None):
    L = val.shape[0]
    nb = FH // CB

    def body(val_ref, gate_ref, w_ref, b_ref, o_ref, s_ref, vds_ref, g_ref):
        wv = [w_ref[pl.ds(t, 1), :] for t in range(9)]
        bv = b_ref[...]
        _grid_copies(g_ref, gate_ref[...], L)

        def tile(i, carry):
            r = i * RT
            acc = jnp.broadcast_to(bv, (RT, CB))
            for dr in range(3):
                for dc in range(3):
                    acc = acc + g_ref[dc, _rows(r, GPAD + (dr - 1) * GRID_W), :] * wv[3 * dr + dc]
            sg = _sigmoid(acc)
            s = acc * sg
            v = val_ref[_rows(r), :]
            o_ref[_rows(r), :] = (s * v).astype(bf16)
            s_ref[_rows(r), :] = s
            vds_ref[_rows(r), :] = v * (sg * (1.0 + acc * (1.0 - sg)))
            return carry

        lax.fori_loop(0, L // RT, tile, 0, unroll=2)

    cblk = pl.BlockSpec((L, CB), lambda j: (0, j))
    (out, s_, vds), extra = _host_call(
        body, (nb,), [cblk, cblk, pl.BlockSpec((9, CB), lambda j: (0, j)), pl.BlockSpec((1, CB), lambda j: (0, j))],
        [cblk, cblk, cblk], [S((L, FH), bf16), S((L, FH), f32), S((L, FH), f32)],
        [pltpu.VMEM((3, L + 2 * GPAD, CB), f32)], ("parallel",), name, (val, gate, cw, cb_), hosted)
    return out, s_, vds, extra


def ffn_gate_bwd(gate, s_, vds, cw, dact, name):
    L = gate.shape[0]
    nb = FH // CB

    def body(gate_ref, s_ref, vds_ref, w_ref, da_ref, dval_ref, dgate_ref, dw_ref, db_ref, g_ref, d_ref):
        wv = [w_ref[pl.ds(t, 1), :] for t in range(9)]
        _grid_copies(g_ref, gate_ref[...], L)

        def tile1(i, carry):
            r = i * RTB
            da = da_ref[_rowsb(r), :].astype(f32)
            dval_ref[_rowsb(r), :] = (da * s_ref[_rowsb(r), :]).astype(bf16)
            dpre = da * vds_ref[_rowsb(r), :]
            d_ref[_rowsb(r), :] = dpre
            new = [carry[t] + _fold8(dpre * g_ref[t % 3, _rowsb(r, GPAD + (t // 3 - 1) * GRID_W), :]) for t in range(9)]
            new.append(carry[9] + _fold8(dpre))
            return tuple(new)

        acc8 = lax.fori_loop(0, L // RTB, tile1, tuple(jnp.zeros((8, CB), f32) for _ in range(10)), unroll=2)
        for t in range(9):
            dw_ref[pl.ds(t, 1), :] = jnp.sum(acc8[t], axis=0, keepdims=True)
        db_ref[...] = jnp.sum(acc8[9], axis=0, keepdims=True)
        _grid_copies(g_ref, d_ref[...], L)

        def tile2(i, carry):
            r = i * RTB
            dg = jnp.zeros((RTB, CB), f32)
            for dr in range(3):
                for dc in range(3):
                    dg = dg + g_ref[2 - dc, _rowsb(r, GPAD - (dr - 1) * GRID_W), :] * wv[3 * dr + dc]
            dgate_ref[_rowsb(r), :] = dg.astype(bf16)
            return carry

        lax.fori_loop(0, L // RTB, tile2, 0, unroll=4)

    cblk = pl.BlockSpec((L, CB), lambda j: (0, j))
    return pl.pallas_call(
        body, grid=(nb,),
        in_specs=[cblk, cblk, cblk, pl.BlockSpec((9, CB), lambda j: (0, j)), cblk],
        out_specs=[cblk, cblk, pl.BlockSpec((9, CB), lambda j: (0, j)), pl.BlockSpec((1, CB), lambda j: (0, j))],
        out_shape=[S((L, FH), bf16), S((L, FH), bf16), S((9, FH), f32), S((1, FH), f32)],
        scratch_shapes=[pltpu.VMEM((3, L + 2 * GPAD, CB), f32), pltpu.VMEM((L, CB), f32)],
        compiler_params=_cparams("parallel"), name=name)(gate, s_, vds, cw, dact)


CONF_K = 31
CHALF = CONF_K // 2
CPAD = 16


def _shift_copies8(c_ref, base_ref, L):
    n = L + 2 * CPAD - 8
    for b_ in range(8):
        c_ref[b_, pl.ds(0, n), :] = base_ref[pl.ds(b_, n), :]


def _tap_ab(o):
    return o % 8, o - o % 8


def conf_glu_conv_fwd(pa, pg, b1, wdw, bdw, name, hosted=None):
    L = pa.shape[0]
    nb = D // CB

    def body(pa_ref, pg_ref, ba_ref, bg_ref, w_ref, bdw_ref, o_ref, base_ref, c_ref):
        _zero_rows(base_ref, 0, CPAD)
        _zero_rows(base_ref, CPAD + L, CPAD)
        base_ref[pl.ds(CPAD, L), :] = (pa_ref[...] + ba_ref[...]) * _sigmoid(pg_ref[...] + bg_ref[...])
        _shift_copies8(c_ref, base_ref, L)
        bv = bdw_ref[...]

        def tile(i, carry):
            r = i * RT
            acc = jnp.broadcast_to(bv, (RT, CB))
            for k in range(CONF_K):
                b_, a8 = _tap_ab(k - CHALF)
                acc = acc + c_ref[b_, _rows(r, CPAD + a8), :] * w_ref[pl.ds(k, 1), :]
            o_ref[_rows(r), :] = acc
            return carry

        lax.fori_loop(0, L // RT, tile, 0, unroll=2)

    cblk = pl.BlockSpec((L, CB), lambda j: (0, j))
    rblk = pl.BlockSpec((1, CB), lambda j: (0, j))
    rgblk = pl.BlockSpec((1, CB), lambda j: (0, nb + j))
    (out,), extra = _host_call(
        body, (nb,), [cblk, cblk, rblk, rgblk, pl.BlockSpec((CONF_K, CB), lambda j: (0, j)), rblk],
        [cblk], [S((L, D), f32)], [pltpu.VMEM((L + 2 * CPAD, CB), f32), pltpu.VMEM((8, L + 2 * CPAD, CB), f32)],
        ("parallel",), name, (pa, pg, b1, b1, wdw, bdw), hosted)
    return out, extra


def conf_glu_conv_bwd(pa, pg, b1, wdw, dy, name):
    L = pa.shape[0]
    nb = D // CB

    def body(pa_ref, pg_ref, ba_ref, bg_ref, w_ref, dy_ref, dpa_ref, dpg_ref, dba_ref, dbg_ref, dw_ref, dbdw_ref,
             base_ref, c_ref, acc_ref):
        _zero_rows(base_ref, 0, CPAD)
        _zero_rows(base_ref, CPAD + L, CPAD)
        base_ref[pl.ds(CPAD, L), :] = (pa_ref[...] + ba_ref[...]) * _sigmoid(pg_ref[...] + bg_ref[...])
        _shift_copies8(c_ref, base_ref, L)
        acc_ref[...] = jnp.zeros_like(acc_ref)

        def tile1(i, carry):
            r = i * RTB
            dyt = dy_ref[_rowsb(r), :]
            for k in range(CONF_K):
                b_, a8 = _tap_ab(k - CHALF)
                acc_ref[k] += _fold8(dyt * c_ref[b_, _rowsb(r, CPAD + a8), :])
            return carry + _fold8(dyt)

        db8 = lax.fori_loop(0, L // RTB, tile1, jnp.zeros((8, CB), f32), unroll=2)
        dbdw_ref[...] = jnp.sum(db8, axis=0, keepdims=True)
        for k in range(CONF_K):
            dw_ref[pl.ds(k, 1), :] = jnp.sum(acc_ref[k], axis=0, keepdims=True)
        base_ref[pl.ds(CPAD, L), :] = dy_ref[...]
        _shift_copies8(c_ref, base_ref, L)
        ba = ba_ref[...]
        bg = bg_ref[...]

        def tile2(i, carry):
            r = i * RTB
            dglu = jnp.zeros((RTB, CB), f32)
            for k in range(CONF_K):
                b_, a8 = _tap_ab(CHALF - k)
                dglu = dglu + c_ref[b_, _rowsb(r, CPAD + a8), :] * w_ref[pl.ds(k, 1), :]
            a = pa_ref[_rowsb(r), :] + ba
            sg = _sigmoid(pg_ref[_rowsb(r), :] + bg)
            dpa = dglu * sg
            dpg = dglu * a * (sg * (1.0 - sg))
            dpa_ref[_rowsb(r), :] = dpa.astype(bf16)
            dpg_ref[_rowsb(r), :] = dpg.astype(bf16)
            return carry[0] + _fold8(dpa), carry[1] + _fold8(dpg)

        s8 = lax.fori_loop(0, L // RTB, tile2, (jnp.zeros((8, CB), f32), jnp.zeros((8, CB), f32)), unroll=2)
        dba_ref[...] = jnp.sum(s8[0], axis=0, keepdims=True)
        dbg_ref[...] = jnp.sum(s8[1], axis=0, keepdims=True)

    cblk = pl.BlockSpec((L, CB), lambda j: (0, j))
    rblk = pl.BlockSpec((1, CB), lambda j: (0, j))
    rgblk = pl.BlockSpec((1, CB), lambda j: (0, nb + j))
    wblk = pl.BlockSpec((CONF_K, CB), lambda j: (0, j))
    return pl.pallas_call(
        body, grid=(nb,), in_specs=[cblk, cblk, rblk, rgblk, wblk, cblk],
        out_specs=[cblk, cblk, rblk, rblk, wblk, rblk],
        out_shape=[S((L, D), bf16), S((L, D), bf16), S((1, D), f32), S((1, D), f32), S((CONF_K, D), f32), S((1, D), f32)],
        scratch_shapes=[pltpu.VMEM((L + 2 * CPAD, CB), f32), pltpu.VMEM((8, L + 2 * CPAD, CB), f32),
                        pltpu.VMEM((CONF_K, 8, CB), f32)],
        compiler_params=_cparams("parallel"), name=name)(pa, pg, b1, b1, wdw, dy)


def _ln_silu_f(x, w, b):
    mu = jnp.mean(x, axis=-1, keepdims=True)
    d = x - mu
    y = d * lax.rsqrt(jnp.mean(d * d, axis=-1, keepdims=True) + EPS) * w + b
    return y * _sigmoid(y)


def ln_silu_fwd(x, w, b, name):
    T = x.shape[0]

    def body(x_ref, w_ref, b_ref, o_ref):
        o_ref[...] = _ln_silu_f(x_ref[...], w_ref[...], b_ref[...]).astype(bf16)

    blk = pl.BlockSpec((TB, D), lambda i: (i, 0))
    row = pl.BlockSpec((1, D), lambda i: (0, 0))
    return pl.pallas_call(body, grid=(T // TB,), in_specs=[blk, row, row], out_specs=blk, out_shape=S((T, D), bf16),
                          compiler_params=_cparams("parallel"), name=name)(x, w, b)


def ln_silu_bwd(x, w, b, ds, name):
    T = x.shape[0]

    def body(x_ref, w_ref, b_ref, ds_ref, dx_ref, dw_ref, db_ref):
        i = pl.program_id(0)
        _, vjp = jax.vjp(_ln_silu_f, x_ref[...], w_ref[...], b_ref[...])
        dx, dw, db = vjp(ds_ref[...].astype(f32))
        dx_ref[...] = dx

        @pl.when(i == 0)
        def _():
            dw_ref[...] = jnp.zeros_like(dw_ref)
            db_ref[...] = jnp.zeros_like(db_ref)

        dw_ref[...] += dw
        db_ref[...] += db

    blk = pl.BlockSpec((TB, D), lambda i: (i, 0))
    row = pl.BlockSpec((1, D), lambda i: (0, 0))
    return pl.pallas_call(body, grid=(T // TB,), in_specs=[blk, row, row, blk], out_specs=[blk, row, row],
                          out_shape=[S((T, D), f32), S((1, D), f32), S((1, D), f32)],
                          compiler_params=_cparams("arbitrary"), name=name)(x, w, b, ds)


def _mxu(a, b, dims):
    return lax.dot_general(a.astype(bf16), b.astype(bf16), (dims, ((), ())), preferred_element_type=f32)


def _nn(a, b):
    return _mxu(a, b, ((1,), (0,)))


def _nt(a, b):
    return _mxu(a, b, ((1,), (1,)))


def _tn(a, b):
    return _mxu(a, b, ((0,), (0,)))


@jax.custom_vjp
def _dot_nn(a, b):
    return _nn(a, b)


@jax.custom_vjp
def _dot_nt(a, b):
    return _nt(a, b)


@jax.custom_vjp
def _dot_tn(a, b):
    return _tn(a, b)


_dot_nn.defvjp(lambda a, b: (_nn(a, b), (a, b)), lambda res, g: (_nt(g, res[1]), _tn(res[0], g)))
_dot_nt.defvjp(lambda a, b: (_nt(a, b), (a, b)), lambda res, g: (_nn(g, res[1]), _tn(g, res[0])))
_dot_tn.defvjp(lambda a, b: (_tn(a, b), (a, b)), lambda res, g: (_nt(res[1], g), _nn(res[0], g)))


def _exact_dot(a, b, dims, split_first):
    v = a if split_first else b
    p1 = v.astype(bf16)
    r1 = v - p1.astype(f32)
    p2 = r1.astype(bf16)
    p3 = (r1 - p2.astype(f32)).astype(bf16)
    out = None
    for p in (p1, p2, p3):
        lhs, rhs = (p, b.astype(bf16)) if split_first else (a.astype(bf16), p)
        t = lax.dot_general(lhs, rhs, (dims, ((), ())), preferred_element_type=f32)
        out = t if out is None else out + t
    return out


@jax.custom_vjp
def _masked_sum_cols(mf, a):
    return _exact_dot(mf, a, ((1,), (0,)), False)


@jax.custom_vjp
def _masked_sum_rows(mf, a):
    return _exact_dot(a, mf, ((1,), (1,)), True)


_masked_sum_cols.defvjp(lambda mf, a: (_exact_dot(mf, a, ((1,), (0,)), False), mf),
                        lambda mf, g: (jnp.zeros_like(mf), _exact_dot(mf, g, ((0,), (0,)), False)))
_masked_sum_rows.defvjp(lambda mf, a: (_exact_dot(a, mf, ((1,), (1,)), True), mf),
                        lambda mf, g: (jnp.zeros_like(mf), _exact_dot(g, mf, ((1,), (0,)), True)))


def _masked_sum(mf, a, rows):
    return _masked_sum_rows(mf, a) if rows else _masked_sum_cols(mf, a)


def _lanes_to_rows(v):
    r = lax.broadcasted_iota(jnp.int32, (GW, GW), 0)
    c = lax.broadcasted_iota(jnp.int32, (GW, GW), 1)
    return jnp.sum(jnp.where(r == c, jnp.broadcast_to(v, (GW, GW)), 0.0), axis=1, keepdims=True)


def _ssd_chunk(x, B, C, dtc, dtr, bc, br, alc, alr, s_in, is_fwd):
    row = lax.broadcasted_iota(jnp.int32, (Q, Q), 0)
    col = lax.broadcasted_iota(jnp.int32, (Q, Q), 1)
    sgn = jnp.where(is_fwd, 1, -1).astype(jnp.int32)
    mask = (row - col) * sgn >= 0
    mf = mask.astype(f32)
    lane_head = lax.broadcasted_iota(jnp.int32, (1, GW), 1) // P

    def spread(v):
        out = jnp.zeros((v.shape[0], GW), f32)
        for r in range(HPG):
            out = jnp.where(lane_head == r, v[:, r:r + 1], out)
        return out

    dt_c = _softplus(dtc + bc)
    dt_r = _softplus(dtr + br)
    a_c = dt_c * (-jnp.exp(alc))
    a_r = dt_r * (-jnp.exp(alr))
    acum_c = _masked_sum(mf, a_c, False)
    acum_r = _masked_sum(mf, a_r, True)
    tot_c = jnp.sum(a_c, axis=0, keepdims=True)
    dt_e = spread(dt_c)
    acum_e = spread(acum_c)
    tot_e = spread(tot_c)
    xdt = x * dt_e
    cb = _dot_nt(C, B)
    scores, xs = [], []
    for r in range(HPG):
        seg = acum_c[:, r:r + 1] - acum_r[r:r + 1, :]
        scores.append(cb * jnp.exp(jnp.where(mask, seg, -jnp.inf)))
        xs.append(jnp.where(lane_head == r, xdt, 0.0))
    y = _dot_nn(jnp.concatenate(scores, axis=1), jnp.concatenate(xs, axis=0))
    y = y + _dot_nt(C, s_in) * jnp.exp(acum_e)
    xe = xdt * jnp.exp(tot_e - acum_e)
    s_out = _lanes_to_rows(jnp.exp(tot_e)) * s_in + _dot_tn(xe, B)
    return y, s_out


def _chunk_index(d, t, nctx, nc):
    bwd = jnp.where(t < nctx, nctx - 1 - t, nc - 1 - (t - nctx))
    return jnp.where(d == 0, t, bwd)


def _ssd_in_specs(ci):
    small_c = pl.BlockSpec((None, G, 1, HPG), lambda d, t: (d, 0, 0, 0))
    small_r = pl.BlockSpec((None, G, HPG, 1), lambda d, t: (d, 0, 0, 0))
    return [
        pl.BlockSpec((Q, CONVD), lambda d, t: (ci(d, t), 0)),
        pl.BlockSpec((None, G, Q, HPG), lambda d, t: (d, 0, ci(d, t), 0)),
        pl.BlockSpec((None, G, HPG, Q), lambda d, t: (d, 0, 0, ci(d, t))),
        small_c, small_r, small_c, small_r,
    ]


def _group_cols(g):
    return pl.ds(g * GW, GW), pl.ds(DI + g * N, N), pl.ds(DI + G * N + g * N, N)


def ssd_scan_fwd(xbc, dtc, dtr, bc, br, alc, alr, nctx, name, hosted=None):
    T = xbc.shape[0]
    nc = T // Q

    def body(xbc_ref, dtc_ref, dtr_ref, bc_ref, br_ref, alc_ref, alr_ref, y_ref, sin_ref, st_ref):
        d = pl.program_id(0)
        t = pl.program_id(1)

        @pl.when(t == 0)
        def _():
            st_ref[...] = jnp.zeros_like(st_ref)

        for g in range(G):
            xs, bs, cs = _group_cols(g)
            s_in = st_ref[g]
            sin_ref[g] = s_in
            y, s_out = _ssd_chunk(xbc_ref[:, xs], xbc_ref[:, bs], xbc_ref[:, cs], dtc_ref[g], dtr_ref[g], bc_ref[g], br_ref[g],
                                  alc_ref[g], alr_ref[g], s_in, d == 0)
            y_ref[:, xs] = y
            st_ref[g] = s_out

    ci = lambda d, t: _chunk_index(d, t, nctx, nc)
    out_specs = [
        pl.BlockSpec((None, Q, DI), lambda d, t: (d, ci(d, t), 0)),
        pl.BlockSpec((None, None, G, GW, N), lambda d, t: (d, ci(d, t), 0, 0, 0)),
    ]
    return _host_call(
        body, (2, nc), _ssd_in_specs(ci), out_specs, [S((2, T, DI), f32), S((2, nc, G, GW, N), f32)],
        [pltpu.VMEM((G, GW, N), f32)], ("arbitrary", "arbitrary"), name, (xbc, dtc, dtr, bc, br, alc, alr), hosted)


def ssd_scan_bwd(xbc, dtc, dtr, bc, br, alc, alr, s_in_all, dy, nctx, name, hosted=None):
    T = xbc.shape[0]
    nc = T // Q

    def body(xbc_ref, dtc_ref, dtr_ref, bc_ref, br_ref, alc_ref, alr_ref, sin_ref, dy_ref,
             dxbc_ref, ddtc_ref, ddtr_ref, dbc_ref, dbr_ref, dalc_ref, dalr_ref, ds_ref):
        d = pl.program_id(0)
        t = pl.program_id(1)

        @pl.when(t == 0)
        def _():
            ds_ref[...] = jnp.zeros_like(ds_ref)
            dbc_ref[...] = jnp.zeros_like(dbc_ref)
            dbr_ref[...] = jnp.zeros_like(dbr_ref)
            dalc_ref[...] = jnp.zeros_like(dalc_ref)
            dalr_ref[...] = jnp.zeros_like(dalr_ref)

        f = functools.partial(_ssd_chunk, is_fwd=(d == 0))
        for g in range(G):
            xs, bs, cs = _group_cols(g)
            _, vjp = jax.vjp(f, xbc_ref[:, xs], xbc_ref[:, bs], xbc_ref[:, cs], dtc_ref[g], dtr_ref[g], bc_ref[g], br_ref[g],
                             alc_ref[g], alr_ref[g], sin_ref[g])
            dx, dB, dC, ddtc, ddtr, dbc, dbr, dalc, dalr, ds = vjp((dy_ref[:, xs], ds_ref[g]))
            dxbc_ref[:, xs] = dx
            dxbc_ref[:, bs] = dB
            dxbc_ref[:, cs] = dC
            ddtc_ref[g] = ddtc
            ddtr_ref[g] = ddtr
            dbc_ref[g] += dbc
            dbr_ref[g] += dbr
            dalc_ref[g] += dalc
            dalr_ref[g] += dalr
            ds_ref[g] = ds

    ci = lambda d, t: _chunk_index(d, nc - 1 - t, nctx, nc)
    in_specs = _ssd_in_specs(ci) + [
        pl.BlockSpec((None, None, G, GW, N), lambda d, t: (d, ci(d, t), 0, 0, 0)),
        pl.BlockSpec((Q, DI), lambda d, t: (ci(d, t), 0)),
    ]
    small_c = pl.BlockSpec((None, G, 1, HPG), lambda d, t: (d, 0, 0, 0))
    small_r = pl.BlockSpec((None, G, HPG, 1), lambda d, t: (d, 0, 0, 0))
    out_specs = [
        pl.BlockSpec((None, Q, CONVD), lambda d, t: (d, ci(d, t), 0)),
        pl.BlockSpec((None, G, Q, HPG), lambda d, t: (d, 0, ci(d, t), 0)),
        pl.BlockSpec((None, G, HPG, Q), lambda d, t: (d, 0, 0, ci(d, t))),
        small_c, small_r, small_c, small_r,
    ]
    out_shape = [S((2, T, CONVD), f32), S((2, G, T, HPG), f32), S((2, G, HPG, T), f32),
                 S((2, G, 1, HPG), f32), S((2, G, HPG, 1), f32), S((2, G, 1, HPG), f32), S((2, G, HPG, 1), f32)]
    return _host_call(body, (2, nc), in_specs, out_specs, out_shape, [pltpu.VMEM((G, GW, N), f32)],
                      ("arbitrary", "arbitrary"), name, (xbc, dtc, dtr, bc, br, alc, alr, s_in_all, dy), hosted)


GTB = 128


def _gate_norm_f(yf, yb, x, z, dexp, w):
    y = (yf + yb + dexp * x) * (z * _sigmoid(z))
    return y * lax.rsqrt(jnp.mean(y * y, axis=-1, keepdims=True) + EPS) * w


def ssd_gate_fwd(y2, xbc, proj, dexp, w, nctxb, name):
    T = xbc.shape[0]
    L = T - nctxb * GTB

    def body(yf_ref, yb_ref, x_ref, z_ref, d_ref, w_ref, o_ref):
        o_ref[...] = _gate_norm_f(yf_ref[...], yb_ref[...], x_ref[...], z_ref[...], d_ref[...], w_ref[...]).astype(bf16)

    wide = pl.BlockSpec((GTB, DI), lambda i: (i + nctxb, 0))
    row = pl.BlockSpec((1, DI), lambda i: (0, 0))
    return pl.pallas_call(
        body, grid=(L // GTB,),
        in_specs=[pl.BlockSpec((None, GTB, DI), lambda i: (0, i + nctxb, 0)),
                  pl.BlockSpec((None, GTB, DI), lambda i: (1, i + nctxb, 0)), wide, wide, row, row],
        out_specs=pl.BlockSpec((GTB, DI), lambda i: (i, 0)), out_shape=S((L, DI), bf16),
        compiler_params=_cparams("parallel"), name=name)(y2, y2, xbc, proj, dexp, w)


def ssd_gate_bwd(y2, xbc, proj, dexp, w, dyn, nctxb, name, hosted=None):
    T = xbc.shape[0]
    nb = T // GTB

    def body(yf_ref, yb_ref, x_ref, z_ref, d_ref, w_ref, dyn_ref, dy_ref, dz_ref, dd_ref, dw_ref):
        i = pl.program_id(0)

        @pl.when(i == 0)
        def _():
            dd_ref[...] = jnp.zeros_like(dd_ref)
            dw_ref[...] = jnp.zeros_like(dw_ref)

        @pl.when(i < nctxb)
        def _():
            dy_ref[...] = jnp.zeros_like(dy_ref)
            dz_ref[...] = jnp.zeros_like(dz_ref)

        @pl.when(i >= nctxb)
        def _():
            _, vjp = jax.vjp(_gate_norm_f, yf_ref[...], yb_ref[...], x_ref[...], z_ref[...], d_ref[...], w_ref[...])
            dyf, _, _, dz, dd, dw = vjp(dyn_ref[...].astype(f32))
            dy_ref[...] = dyf
            dz_ref[...] = dz.astype(bf16)
            fold = (lax.broadcasted_iota(jnp.int32, (DI, 128), 0) // P == lax.broadcasted_iota(jnp.int32, (DI, 128), 1))
            dd_ref[...] += jnp.dot(dd, fold.astype(f32), precision=HI, preferred_element_type=f32)
            dw_ref[...] += dw

    wide = pl.BlockSpec((GTB, DI), lambda i: (i, 0))
    row = pl.BlockSpec((1, DI), lambda i: (0, 0))
    hrow = pl.BlockSpec((1, 128), lambda i: (0, 0))
    return _host_call(
        body, (nb,),
        [pl.BlockSpec((None, GTB, DI), lambda i: (0, i, 0)), pl.BlockSpec((None, GTB, DI), lambda i: (1, i, 0)),
         wide, wide, row, row, pl.BlockSpec((GTB, DI), lambda i: (jnp.maximum(i - nctxb, 0), 0))],
        [wide, wide, hrow, row],
        [S((T, DI), f32), S((T, DI), bf16), S((1, 128), f32), S((1, DI), f32)],
        [], ("arbitrary",), name, (y2, y2, xbc, proj, dexp, w, dyn), hosted)


CROWS = 2 * N_DEV


def mod_fwd(c16, modw, name):
    nl, _, cols = modw.shape

    def body(c_ref, w_ref, o_ref):
        cv = c_ref[...]
        s = cv * _sigmoid(cv)
        for l in range(nl):
            o_ref[l] = jnp.dot(s, w_ref[l], precision=HI, preferred_element_type=f32)

    return pl.pallas_call(body, in_specs=[VMEM, VMEM], out_specs=VMEM, out_shape=S((nl, CROWS, cols), f32),
                          compiler_params=pltpu.CompilerParams(vmem_limit_bytes=VMEM_LIMIT_BYTES), name=name)(c16, modw)


def mod_bwd(c16, modw, dm_sh, dm_all, name):
    nl, _, cols = modw.shape

    def body(c_ref, w_ref, dm_ref, dmall_ref, dw_ref, dc_ref, db_ref):
        cv = c_ref[...]
        sg = _sigmoid(cv)
        s = cv * sg
        ds_dc = sg * (1.0 + cv * (1.0 - sg))
        is_ctx = lax.broadcasted_iota(jnp.int32, (CROWS, D), 0) >= N_DEV
        dc = jnp.zeros((1, D), f32)
        for l in range(nl):
            dm = dm_ref[l]
            dw_ref[l] = lax.dot_general(s, dm, (((0,), (0,)), ((), ())), precision=HI, preferred_element_type=f32)
            dsv = lax.dot_general(dm, w_ref[l], (((1,), (1,)), ((), ())), precision=HI, preferred_element_type=f32)
            dc = dc + jnp.sum(jnp.where(is_ctx, dsv * ds_dc, 0.0), axis=0, keepdims=True)
            db_ref[pl.ds(l, 1), :] = jnp.sum(dmall_ref[l], axis=0, keepdims=True)
        dc_ref[...] = dc

    return pl.pallas_call(
        body, in_specs=[VMEM, VMEM, VMEM, VMEM], out_specs=[VMEM, VMEM, VMEM],
        out_shape=[S(modw.shape, f32), S((1, D), f32), S((nl, 6 * D), f32)],
        compiler_params=pltpu.CompilerParams(vmem_limit_bytes=VMEM_LIMIT_BYTES), name=name)(c16, modw, dm_sh, dm_all)


def adamw(w, g, m, v, name):
    R, C = w.shape
    rb = R if R <= 512 else max(r_ for r_ in range(8, 513, 8) if R % r_ == 0)
    bc1 = 1.0 - ADAM_B1 ** ADAM_STEP
    bc2 = 1.0 - ADAM_B2 ** ADAM_STEP

    def body(w_ref, g_ref, m_ref, v_ref, d_ref, nm_ref, nv_ref):
        gv = g_ref[...]
        m_new = ADAM_B1 * m_ref[...] + (1.0 - ADAM_B1) * gv
        v_new = ADAM_B2 * v_ref[...] + (1.0 - ADAM_B2) * (gv * gv)
        m_hat = m_new / bc1
        v_hat = v_new / bc2
        d_ref[...] = -ADAM_LR * (m_hat / (jnp.sqrt(v_hat) + ADAM_EPS) + ADAM_WD * w_ref[...])
        nm_ref[...] = m_new
        nv_ref[...] = v_new

    blk = pl.BlockSpec((rb, C), lambda i: (i, 0))
    return pl.pallas_call(body, grid=(R // rb,), in_specs=[blk] * 4, out_specs=[blk] * 3,
                          out_shape=[S((R, C), f32)] * 3, compiler_params=_cparams("parallel"), name=name)(w, g, m, v)


def _me():
    return lax.axis_index("x"), lax.axis_index("y"), lax.axis_index("c")


def allgather_small(x, name, with_sum=False, after=None):
    r, w = x.shape
    extra = () if after is None else (after,)

    def body(x_ref, *refs):
        refs = refs[len(extra):]
        if with_sum:
            out_ref, sum_ref, send_sems, recv_sems = refs
        else:
            out_ref, send_sems, recv_sems = refs
        mx, my, mc = _me()
        me = 4 * mx + 2 * my + mc
        out_ref[me] = x_ref[...]
        peers = []
        for k in range(1, N_DEV):
            kx, ky, kc = (k >> 2) & 1, (k >> 1) & 1, k & 1
            peers.append((mx + kx - 2 * mx * kx, my + ky - 2 * my * ky, mc + kc - 2 * mc * kc))
        copies = []
        for k, peer in enumerate(peers):
            cp = pltpu.make_async_remote_copy(src_ref=x_ref, dst_ref=out_ref.at[me], send_sem=send_sems.at[k],
                                              recv_sem=recv_sems.at[k], device_id=peer, device_id_type=MESH)
            cp.start()
            copies.append(cp)
        for k, (px, py, pc) in enumerate(peers):
            pltpu.make_async_remote_copy(src_ref=x_ref, dst_ref=out_ref.at[4 * px + 2 * py + pc], send_sem=send_sems.at[k],
                                         recv_sem=recv_sems.at[k], device_id=(px, py, pc), device_id_type=MESH).wait_recv()
        for cp in copies:
            cp.wait_send()
        if with_sum:
            acc = out_ref[0]
            for j in range(1, N_DEV):
                acc = acc + out_ref[j]
            sum_ref[...] = acc

    out_shape = [S((N_DEV, r, w), f32)] + ([S((r, w), f32)] if with_sum else [])
    outs = pl.pallas_call(
        body, in_specs=[VMEM] + [ANY] * len(extra), out_specs=[VMEM] * len(out_shape), out_shape=out_shape,
        scratch_shapes=[pltpu.SemaphoreType.DMA((N_DEV - 1,)), pltpu.SemaphoreType.DMA((N_DEV - 1,))],
        compiler_params=pltpu.CompilerParams(vmem_limit_bytes=VMEM_LIMIT_BYTES), name=name)(x, *extra)
    return outs if with_sum else outs[0]


def _tile2d(R, W, max_rows):
    if R <= max_rows:
        return R, W
    fits = [r_ for r_ in range(16, max_rows + 1, 16) if R % r_ == 0]
    return (max(fits), W) if fits else (R, 256)


def add_own(g, r, core, name, twice=False):
    _, _, R, W = g.shape
    rb, wb = _tile2d(R, W, 512)
    nout = 2 if twice else 1

    def body(core_ref, a_ref, b_ref, *o_refs):
        s = (a_ref[...].astype(f32) + b_ref[...].astype(f32)).astype(bf16)
        for o_ref in o_refs:
            o_ref[...] = s

    blk = pl.BlockSpec((None, rb, wb), lambda k, i, j, core_ref: (k, i, j))
    gs = pltpu.PrefetchScalarGridSpec(
        num_scalar_prefetch=1, grid=(4, R // rb, W // wb),
        in_specs=[pl.BlockSpec((None, None, rb, wb), lambda k, i, j, core_ref: (k, core_ref[0], i, j)), blk],
        out_specs=[blk] * nout)
    outs = pl.pallas_call(body, grid_spec=gs, out_shape=[S((4, R, W), bf16)] * nout,
                          compiler_params=_cparams("parallel", "parallel", "parallel"), name=name)(core, g, r)
    return tuple(outs) if twice else outs[0]


HBM_SPEC = pl.BlockSpec(memory_space=pltpu.HBM)
SEM_SPEC = pl.BlockSpec(memory_space=pltpu.SEMAPHORE)


def _chips_copy(p_ref, land_ref, send_sems, recv_sems, a, j):
    x, y, c = _me()
    px, py = [(1 - x, y), (x, 1 - y), (1 - x, 1 - y)][j]
    return pltpu.make_async_remote_copy(src_ref=p_ref.at[2 * px + py], dst_ref=land_ref.at[2 * x + y],
                                        send_sem=send_sems.at[3 * a + j], recv_sem=recv_sems.at[3 * a + j],
                                        device_id=(px, py, c), device_id_type=MESH)


def _chips_wait_copy(p_ref, land_ref, send_sems, recv_sems, a, j):
    x, y, c = _me()
    px, py = [(1 - x, y), (x, 1 - y), (1 - x, 1 - y)][j]
    return pltpu.make_async_remote_copy(src_ref=p_ref.at[2 * px + py], dst_ref=land_ref.at[2 * px + py],
                                        send_sem=send_sems.at[3 * a + j], recv_sem=recv_sems.at[3 * a + j],
                                        device_id=(px, py, c), device_id_type=MESH)


def _xor_peers():
    mx, my, mc = _me()
    peers = []
    for k in range(1, N_DEV):
        kx, ky, kc = (k >> 2) & 1, (k >> 1) & 1, k & 1
        peers.append((mx + kx - 2 * mx * kx, my + ky - 2 * my * ky, mc + kc - 2 * mc * kc))
    return peers


def gather_start(shards, name, after):
    na = len(shards)
    lands = [lax.empty((N_DEV,) + s_.shape, s_.dtype) for s_ in shards]

    def body(*refs):
        x_refs, land_refs = refs[:na], refs[na:2 * na]
        send_sems, recv_sems, local_sems = refs[2 * na + 1:2 * na + 4]
        token = refs[-1]
        mx, my, mc = _me()
        me = 4 * mx + 2 * my + mc
        for a in range(na):
            pltpu.make_async_copy(x_refs[a], land_refs[a].at[me], local_sems.at[a]).start()
            for k, peer in enumerate(_xor_peers()):
                pltpu.make_async_remote_copy(src_ref=x_refs[a], dst_ref=land_refs[a].at[me], send_sem=send_sems.at[7 * a + k],
                                             recv_sem=recv_sems.at[7 * a + k], device_id=peer, device_id_type=MESH).start()
        token[...] = jnp.zeros_like(token)

    arrs = list(shards) + lands
    outs = pl.pallas_call(
        body, name=name, in_specs=[HBM_SPEC] * (2 * na) + [ANY],
        out_shape=[DMA((7 * na,)), DMA((7 * na,)), DMA((na,))] + [pltpu.HBM(t.shape, t.dtype) for t in arrs]
        + [S((8, 128), f32)],
        out_specs=[SEM_SPEC] * 3 + [HBM_SPEC] * (2 * na) + [VMEM],
        input_output_aliases={k: 3 + k for k in range(2 * na)},
        compiler_params=pltpu.CompilerParams(has_side_effects=pltpu.SideEffectType.DATAFLOW_SIDE_EFFECTING),
    )(*[pltpu.with_memory_space_constraint(t, pltpu.HBM) for t in arrs], after)
    return outs[0], outs[1], outs[2], list(outs[3:3 + na]), list(outs[3 + na:3 + 2 * na]), outs[-1]


def gather_wait(send_sems, recv_sems, local_sems, shards, lands, after, name):
    na = len(shards)

    def body(*refs):
        x_refs, land_refs = refs[:na], refs[na:2 * na]
        ssem, rsem, lsem = refs[2 * na:2 * na + 3]
        mx, my, mc = _me()
        me = 4 * mx + 2 * my + mc
        for a in range(na):
            pltpu.make_async_copy(x_refs[a], land_refs[a].at[me], lsem.at[a]).wait()
            for k, (px, py, pc) in enumerate(_xor_peers()):
                cp = pltpu.make_async_remote_copy(src_ref=x_refs[a], dst_ref=land_refs[a].at[4 * px + 2 * py + pc],
                                                  send_sem=ssem.at[7 * a + k], recv_sem=rsem.at[7 * a + k],
                                                  device_id=(px, py, pc), device_id_type=MESH)
                cp.wait_send()
                cp.wait_recv()

    arrs = list(shards) + list(lands)
    outs = pl.pallas_call(
        body, name=name, in_specs=[HBM_SPEC] * (2 * na) + [SEM_SPEC] * 3 + [ANY],
        out_shape=[pltpu.HBM(t.shape, t.dtype) for t in arrs], out_specs=[HBM_SPEC] * (2 * na),
        input_output_aliases={k: k for k in range(2 * na)},
        compiler_params=pltpu.CompilerParams(has_side_effects=pltpu.SideEffectType.DATAFLOW_SIDE_EFFECTING),
    )(*arrs, send_sems, recv_sems, local_sems, after)
    return list(outs[na:])


def chips_start(parts, lands, name):
    na = len(parts)

    def body(*refs):
        p_refs, land_refs = refs[:na], refs[na:2 * na]
        send_sems, recv_sems = refs[2 * na], refs[2 * na + 1]
        token = refs[-1]
        for a in range(na):
            for j in range(3):
                _chips_copy(p_refs[a], land_refs[a], send_sems, recv_sems, a, j).start()
        token[...] = jnp.zeros_like(token)

    arrs = list(parts) + list(lands)
    outs = pl.pallas_call(
        body, name=name, in_specs=[HBM_SPEC] * (2 * na),
        out_shape=[DMA((3 * na,)), DMA((3 * na,))] + [pltpu.HBM(t.shape, t.dtype) for t in arrs] + [S((8, 128), f32)],
        out_specs=[SEM_SPEC, SEM_SPEC] + [HBM_SPEC] * (2 * na) + [VMEM],
        input_output_aliases={k: 2 + k for k in range(2 * na)},
        compiler_params=pltpu.CompilerParams(has_side_effects=pltpu.SideEffectType.DATAFLOW_SIDE_EFFECTING),
    )(*[pltpu.with_memory_space_constraint(t, pltpu.HBM) for t in arrs])
    return outs[0], outs[1], list(outs[2:2 + na]), list(outs[2 + na:2 + 2 * na]), outs[-1]


def chips_wait(send_sems, recv_sems, parts, lands, after, name):
    na = len(parts)

    def body(*refs):
        p_refs, land_refs = refs[:na], refs[na:2 * na]
        ssem, rsem = refs[2 * na], refs[2 * na + 1]
        for a in range(na):
            for j in range(3):
                cp = _chips_wait_copy(p_refs[a], land_refs[a], ssem, rsem, a, j)
                cp.wait_send()
                cp.wait_recv()

    arrs = list(parts) + list(lands)
    outs = pl.pallas_call(
        body, name=name, in_specs=[HBM_SPEC] * (2 * na) + [SEM_SPEC, SEM_SPEC, ANY],
        out_shape=[pltpu.HBM(t.shape, t.dtype) for t in arrs], out_specs=[HBM_SPEC] * (2 * na),
        input_output_aliases={k: k for k in range(2 * na)},
        compiler_params=pltpu.CompilerParams(has_side_effects=pltpu.SideEffectType.DATAFLOW_SIDE_EFFECTING),
    )(*arrs, send_sems, recv_sems, after)
    return list(outs[na:])


def sum_adamw(recv, w, m, v, layer, name, into=None, after=None):
    _, R, W = recv.shape
    rb, wb = _tile2d(R, W, 256)
    bc1 = 1.0 - ADAM_B1 ** ADAM_STEP
    bc2 = 1.0 - ADAM_B2 ** ADAM_STEP
    n_into = 0 if into is None else 4
    extra = () if after is None else (after,)

    def body(r_ref, w_ref, m_ref, v_ref, *refs):
        g_ref, d_ref, nm_ref, nv_ref = refs[n_into + len(extra):]
        gv = r_ref[0].astype(f32)
        for k in range(1, 4):
            gv = gv + r_ref[k].astype(f32)
        m_new = ADAM_B1 * m_ref[...] + (1.0 - ADAM_B1) * gv
        v_new = ADAM_B2 * v_ref[...] + (1.0 - ADAM_B2) * (gv * gv)
        g_ref[...] = gv
        d_ref[...] = -ADAM_LR * ((m_new / bc1) / (jnp.sqrt(v_new / bc2) + ADAM_EPS) + ADAM_WD * w_ref[...])
        nm_ref[...] = m_new
        nv_ref[...] = v_new

    if layer is None:
        wblk = pl.BlockSpec((rb, wb), lambda i, j: (i, j))
        oshape = S((R, W), f32)
    else:
        wblk = pl.BlockSpec((None, rb, wb), lambda i, j: (layer, i, j))
        oshape = S(w.shape, f32)
    return pl.pallas_call(
        body, grid=(R // rb, W // wb),
        in_specs=[pl.BlockSpec((4, rb, wb), lambda i, j: (0, i, j)), wblk, wblk, wblk] + [ANY] * (n_into + len(extra)),
        out_specs=[wblk] * 4, out_shape=[oshape] * 4, input_output_aliases={4 + k: k for k in range(n_into)},
        compiler_params=_cparams("parallel", "parallel"), name=name)(recv, w, m, v, *(into or ()), *extra)


def sum_rows(a, name):
    K, R, W = a.shape
    rb = _pick(R, (512, 256, 128, 64, 32, 16))

    def body(a_ref, o_ref):
        acc = a_ref[0].astype(f32)
        for k in range(1, K):
            acc = acc + a_ref[k].astype(f32)
        o_ref[...] = acc

    return pl.pallas_call(body, grid=(R // rb,), in_specs=[pl.BlockSpec((K, rb, W), lambda i: (0, i, 0))],
                          out_specs=pl.BlockSpec((rb, W), lambda i: (i, 0)), out_shape=S((R, W), f32),
                          compiler_params=_cparams("parallel"), name=name)(a)


DMA = pltpu.SemaphoreType.DMA


class GatherExchange:
    def __init__(self, arrays):
        self.arrays = list(arrays)
        self.na = len(self.arrays)
        self.out_shape = [S((N_DEV,) + a.shape, a.dtype) for a in self.arrays]
        self.scratch = [DMA((7 * self.na,)), DMA((7 * self.na,)), DMA((self.na,))]

    def ops(self, x_refs, out_refs, sems):
        send_sems, recv_sems, local_sems = sems
        na = self.na
        x, y, c = _me()
        me, sibling = (x, y, c), (x, y, 1 - c)
        chips = [(1 - x, y), (x, 1 - y), (1 - x, 1 - y)]

        def rows(a, px, py, pc):
            return out_refs[a].at[4 * px + 2 * py + pc]

        def copy(a, k, block, to, src=None):
            return pltpu.make_async_remote_copy(
                src_ref=rows(a, *block) if src is None else src, dst_ref=rows(a, *block),
                send_sem=send_sems.at[7 * a + k], recv_sem=recv_sems.at[7 * a + k], device_id=to, device_id_type=MESH)

        def local(a):
            return pltpu.make_async_copy(x_refs[a], rows(a, *me), local_sems.at[a])

        def first(a):
            return [copy(a, 0, me, sibling, src=x_refs[a])] + [copy(a, 1 + j, me, (*chip, c), src=x_refs[a])
                                                                for j, chip in enumerate(chips)]

        def start():
            for a in range(na):
                local(a).start()
                for cp in first(a):
                    cp.start()

        def mid():
            for a in range(na):
                for j, chip in enumerate(chips):
                    copy(a, 1 + j, (*chip, c), me).wait_recv()
                    copy(a, 4 + j, (*chip, c), sibling).start()

        def finish():
            for a in range(na):
                copy(a, 0, sibling, me).wait_recv()
                for j, chip in enumerate(chips):
                    copy(a, 4 + j, (*chip, 1 - c), me).wait_recv()
                for cp in first(a) + [copy(a, 4 + j, (*chip, c), sibling) for j, chip in enumerate(chips)]:
                    cp.wait_send()
                local(a).wait()

        return start, mid, finish


class SiblingExchange:
    def __init__(self, arrays):
        self.arrays = list(arrays)
        self.na = len(self.arrays)
        self.out_shape = [S((4,) + g.shape[2:], g.dtype) for g in self.arrays]
        self.scratch = [DMA((self.na,)), DMA((self.na,))]

    def ops(self, g_refs, out_refs, sems):
        send_sems, recv_sems = sems
        x, y, c = _me()

        def copy(a):
            return pltpu.make_async_remote_copy(src_ref=g_refs[a].at[:, 1 - c], dst_ref=out_refs[a],
                                                send_sem=send_sems.at[a], recv_sem=recv_sems.at[a],
                                                device_id=(x, y, 1 - c), device_id_type=MESH)

        def start():
            for a in range(self.na):
                copy(a).start()

        def finish():
            for a in range(self.na):
                copy(a).wait()

        return start, None, finish


class ChipsExchange:
    def __init__(self, arrays):
        self.arrays = list(arrays)
        self.na = len(self.arrays)
        self.out_shape = [S(p.shape, p.dtype) for p in self.arrays]
        self.scratch = [DMA((3 * self.na,)), DMA((3 * self.na,)), DMA((self.na,))]

    def ops(self, p_refs, out_refs, sems):
        send_sems, recv_sems, local_sems = sems
        x, y, c = _me()
        mine = 2 * x + y
        chips = [(1 - x, y), (x, 1 - y), (1 - x, 1 - y)]

        def local(a):
            return pltpu.make_async_copy(p_refs[a].at[mine], out_refs[a].at[mine], local_sems.at[a])

        def send(a, j):
            px, py = chips[j]
            return pltpu.make_async_remote_copy(src_ref=p_refs[a].at[2 * px + py], dst_ref=out_refs[a].at[mine],
                                                send_sem=send_sems.at[3 * a + j], recv_sem=recv_sems.at[3 * a + j],
                                                device_id=(px, py, c), device_id_type=MESH)

        def recv(a, j):
            px, py = chips[j]
            return pltpu.make_async_remote_copy(src_ref=p_refs[a].at[mine], dst_ref=out_refs[a].at[2 * px + py],
                                                send_sem=send_sems.at[3 * a + j], recv_sem=recv_sems.at[3 * a + j],
                                                device_id=(px, py, c), device_id_type=MESH)

        def start():
            for a in range(self.na):
                local(a).start()
                for j in range(3):
                    send(a, j).start()

        def finish():
            for a in range(self.na):
                for j in range(3):
                    recv(a, j).wait_recv()
                for j in range(3):
                    send(a, j).wait_send()
                local(a).wait()

        return start, None, finish


def exchange(ex, name):
    na = ex.na

    def body(*refs):
        start, mid, finish = ex.ops(refs[:na], refs[na:2 * na], refs[2 * na:])
        start()
        if mid is not None:
            mid()
        finish()

    return pl.pallas_call(body, in_specs=[ANY] * na, out_specs=[ANY] * na, out_shape=ex.out_shape,
                          scratch_shapes=ex.scratch, name=name)(*ex.arrays)


def _host_call(body, grid, in_specs, out_specs, out_shape, scratch_shapes, sem, name, args, hosted):
    if hosted is None:
        res = pl.pallas_call(body, grid=grid, in_specs=in_specs, out_specs=out_specs, out_shape=out_shape,
                             scratch_shapes=scratch_shapes, compiler_params=_cparams(*sem), name=name)(*args)
        return res, None
    n_in, n_out, n_sc, na = len(in_specs), len(out_shape), len(scratch_shapes), hosted.na
    nsteps = 1
    for g_ in grid:
        nsteps *= g_
    mid_step = (3 * nsteps) // 4
    i1 = n_in + na
    i2 = i1 + n_out
    i3 = i2 + na
    i4 = i3 + n_sc

    def wrapped(*refs):
        step = pl.program_id(0)
        for ax in range(1, len(grid)):
            step = step * grid[ax] + pl.program_id(ax)
        start, mid, finish = hosted.ops(refs[n_in:i1], refs[i2:i3], refs[i4:])
        pl.when(step == 0)(start)
        if mid is not None:
            pl.when(step == mid_step)(mid)
        body(*refs[:n_in], *refs[i1:i2], *refs[i3:i4])
        pl.when(step == nsteps - 1)(finish)

    res = pl.pallas_call(
        wrapped, grid=grid, in_specs=list(in_specs) + [ANY] * na, out_specs=list(out_specs) + [ANY] * na,
        out_shape=list(out_shape) + hosted.out_shape, scratch_shapes=list(scratch_shapes) + hosted.scratch,
        compiler_params=_cparams(*(("arbitrary",) * len(grid))), name=name)(*args, *hosted.arrays)
    return res[:n_out], res[n_out:]


PACK_ALIGN = 16 * PACK_W


def _pad_to(v, mult):
    n = v.shape[-1]
    extra = (-n) % mult
    if extra == 0:
        return v
    return jnp.concatenate([v, jnp.zeros(v.shape[:-1] + (extra,), v.dtype)], axis=-1)


def _f32_as_bf16_pairs(v):
    return lax.bitcast_convert_type(v.reshape(-1), bf16).reshape(-1)


def _bf16_pairs_as_f32(v):
    return lax.bitcast_convert_type(v.reshape(v.shape[:-1] + (v.shape[-1] // 2, 2)), f32)


def _col_shards(gw):
    lead = gw.shape[:-1]
    n = gw.shape[-1] // N_DEV
    t = gw.reshape(lead + (N_DEV, n))
    t = jnp.moveaxis(t, -2, 0)
    return t.reshape(N_DEV, -1)


def kernel(x, c, ctx, c_ctx, mod_w, mod_b, norm1_w, norm2_w, ssd_w_in, ssd_conv_w, ssd_conv_b, ssd_dt_bias, ssd_a_log, ssd_d, ssd_norm_w, ssd_w_out, conf_w_pw1, conf_b_pw1, conf_w_dw, conf_b_dw, conf_ln_w, conf_ln_b, conf_w_pw2, conf_b_pw2, ffn_w_up, ffn_conv_w, ffn_conv_b, ffn_w_down, final_norm_w, loss_target, m_c_ctx, m_mod_w, m_mod_b, m_norm1_w, m_norm2_w, m_ssd_w_in, m_ssd_conv_w, m_ssd_conv_b, m_ssd_dt_bias, m_ssd_a_log, m_ssd_d, m_ssd_norm_w, m_ssd_w_out, m_conf_w_pw1, m_conf_b_pw1, m_conf_w_dw, m_conf_b_dw, m_conf_ln_w, m_conf_ln_b, m_conf_w_pw2, m_conf_b_pw2, m_ffn_w_up, m_ffn_conv_w, m_ffn_conv_b, m_ffn_w_down, m_final_norm_w, v_c_ctx, v_mod_w, v_mod_b, v_norm1_w, v_norm2_w, v_ssd_w_in, v_ssd_conv_w, v_ssd_conv_b, v_ssd_dt_bias, v_ssd_a_log, v_ssd_d, v_ssd_norm_w, v_ssd_w_out, v_conf_w_pw1, v_conf_b_pw1, v_conf_w_dw, v_conf_b_dw, v_conf_ln_w, v_conf_ln_b, v_conf_w_pw2, v_conf_b_pw2, v_ffn_w_up, v_ffn_conv_w, v_ffn_conv_b, v_ffn_w_down, v_final_norm_w):
    mx, my, mc = _me()
    me = 4 * mx + 2 * my + mc
    L = x.shape[1]
    LC = ctx.shape[1]
    T = LC + L
    w_in_cols = ssd_w_in.shape[2] * N_DEV
    n_dt = w_in_cols - DI - CONVD

    small = [c[0], ssd_conv_w[0], conf_b_pw1[0], conf_w_dw[0], conf_b_dw[0], conf_ln_w[0], conf_ln_b[0], conf_b_pw2[0],
             ffn_conv_w]
    parts = [_f32_as_bf16_pairs(t) for t in small]
    sizes = [p.shape[0] for p in parts]
    small_flat = _pad_to(jnp.concatenate(parts), PACK_ALIGN).reshape(-1, PACK_W)
    w_in, small_g = exchange(GatherExchange([ssd_w_in[0].astype(bf16), small_flat]), "gather_first")
    w_up, w_down = [None, None], [None, None]
    small_g = small_g.reshape(N_DEV, -1)
    offs = [0]
    for s_ in sizes:
        offs.append(offs[-1] + s_)
    sm = [_bf16_pairs_as_f32(small_g[:, offs[i]:offs[i + 1]]) for i in range(len(sizes))]

    def cols(pc, K):
        return jnp.moveaxis(pc.reshape(N_DEV, K, -1), 0, 1).reshape(K, -1)

    c_all = sm[0]
    conv_w5 = cols(sm[1], 5)
    b_pw1 = sm[2].reshape(1, 2 * D)
    w_dw = cols(sm[3], CONF_K)
    b_dw, ln_w, ln_b, b_pw2 = (sm[i].reshape(1, D) for i in (4, 5, 6, 7))
    fcw = sm[8].reshape(N_DEV, 2, 9, FH // N_DEV)
    ffn_cw = [cols(fcw[:, i].reshape(N_DEV, -1), 9) for i in range(2)]
    in_segs = (DI, CONVD, n_dt)
    up_segs = (FH, FH)
    pw1_segs = (D, D)

    c16 = jnp.concatenate([c_all, jnp.broadcast_to(c_ctx[None, :], (N_DEV, D))], axis=0)
    m_sh = mod_fwd(c16, mod_w, "mod_fwd")
    mod_cols = mod_w.shape[2]
    m_gath = allgather_small(m_sh.reshape(2 * CROWS, mod_cols), "gather_mod")
    fly_a = gather_start([ssd_w_out[0].astype(bf16), ffn_w_up[0].astype(bf16), ffn_w_down[0].astype(bf16)],
                         "gather_a_start", m_gath)
    fly_b = gather_start([conf_w_pw1[0].astype(bf16), conf_w_pw2[0].astype(bf16)], "gather_b_start", fly_a[-1])
    fly_c = gather_start([ffn_w_up[1].astype(bf16), ffn_w_down[1].astype(bf16)], "gather_c_start", fly_b[-1])
    m_all = jnp.moveaxis(m_gath.reshape(N_DEV, 2, CROWS, mod_cols), 0, 2).reshape(2, CROWS, 6 * D) + mod_b[:, None, :]
    m_all = m_all + fly_c[-1][0, 0]
    m_lat = lax.dynamic_index_in_dim(m_all, me, axis=1, keepdims=False).reshape(2, 6, 1, D)
    m_ctx = m_all[:, N_DEV].reshape(2, 6, 1, D)
    zero_row = jnp.zeros((1, D), f32)

    def ffn_fwd(a2, i, tag):
        val, gate = smm_fwd(a2, w_up[i], None, up_segs, f"ffn{tag}_up")
        act, gs_, gvds, _ = ffn_gate_fwd(val, gate, ffn_cw[i], ffn_conv_b[i][None], f"ffn{tag}_gate")
        o2 = matmul(act, w_down[i], "nn", f32, f"ffn{tag}_down")
        return o2, (a2, gate, gs_, gvds, act)

    def ffn_bwd(do2, i, saved, tag):
        a2, gate, gs_, gvds, act = saved
        g_down = matmul(act, do2, "tn", bf16, f"ffn{tag}_down_dw")
        dact = matmul(do2, w_down[i], "nt", bf16, f"ffn{tag}_down_dx")
        dval, dgate, dcw, dcb = ffn_gate_bwd(gate, gs_, gvds, ffn_cw[i], dact, f"ffn{tag}_gate_bwd")
        g_up = smm_dw(a2, [dval, dgate], FH // 4, up_segs, 2, True, f"ffn{tag}_up_dw")
        da2, _ = smm_dx([dval, dgate], w_up[i], None, up_segs, bf16, f"ffn{tag}_up_dx")
        return da2, dict(w_up=g_up, w_down=g_down, conv_w=dcw, conv_b=dcb)

    nctx = LC // Q
    hx = x[0]
    sc0 = jnp.stack([m_ctx[0, 1], m_lat[0, 1]])
    sh0 = jnp.stack([m_ctx[0, 0], m_lat[0, 0]])
    a0 = modnorm_fwd(hx, norm1_w[0][None], sc0, sh0, LC // TB, "ssd_norm", ctx=ctx[0])
    z, xbc_pre, dt_raw = smm_fwd(a0, w_in, None, in_segs, "ssd_in")
    segs = ((0, LC), (LC, L))
    xbc, xbc_dsilu, _ = ssd_conv_fwd(xbc_pre, conv_w5, ssd_conv_b, segs, "ssd_conv")
    dt4 = dt_raw[:, :n_dt].reshape(T, 2, G, HPG)
    dtc = jnp.transpose(dt4, (1, 2, 0, 3))
    dtr = jnp.transpose(dt4, (1, 2, 3, 0))
    bias3 = ssd_dt_bias[0].reshape(2, G, HPG)
    alog3 = ssd_a_log[0].reshape(2, G, HPG)
    bc_, br_ = bias3[:, :, None, :], bias3[:, :, :, None]
    alc, alr = alog3[:, :, None, :], alog3[:, :, :, None]
    (y2, s_in_all), _ = ssd_scan_fwd(xbc, dtc, dtr, bc_, br_, alc, alr, nctx, "ssd_scan")
    dexp = jnp.repeat(ssd_d[0], P)[None, :]
    yn = ssd_gate_fwd(y2, xbc, z, dexp, ssd_norm_w, LC // GTB, "ssd_gate")
    w_out_g, w_up[0], w_down0_g = gather_wait(*fly_a[:5], yn, "gather_a_wait")
    w_out = w_out_g.reshape(DI, D)
    w_down[0] = w_down0_g.reshape(FH, D)
    o_ssd = matmul(yn, w_out, "nn", f32, "ssd_out")
    h1, a2_0 = resnorm_fwd(hx, o_ssd, m_lat[0, 2], zero_row, norm2_w[0][None], m_lat[0, 4], m_lat[0, 3], "ssd_res")
    o2_0, ffn0_saved = ffn_fwd(a2_0, 0, "0")

    h2, a1 = resnorm_fwd(h1, o2_0, m_lat[0, 5], zero_row, norm1_w[1][None], m_lat[1, 1], m_lat[1, 0], "ffn0_res")
    w_pw1, w_pw2_g = gather_wait(*fly_b[:5], a1, "gather_b_wait")
    w_pw2 = w_pw2_g.reshape(D, D)
    pa, pg = smm_fwd(a1, w_pw1, None, pw1_segs, "conf_pw1")
    dwc, _ = conf_glu_conv_fwd(pa, pg, b_pw1, w_dw, b_dw, "conf_conv")
    s1 = ln_silu_fwd(dwc, ln_w, ln_b, "conf_ln")
    o_conf = matmul(s1, w_pw2, "nn", f32, "conf_pw2")
    h3, a2_1 = resnorm_fwd(h2, o_conf, m_lat[1, 2], b_pw2, norm2_w[1][None], m_lat[1, 4], m_lat[1, 3], "conf_res")
    w_up[1], w_down1_g = gather_wait(*fly_c[:5], h3, "gather_c_wait")
    w_down[1] = w_down1_g.reshape(FH, D)
    o2_1, ffn1_saved = ffn_fwd(a2_1, 1, "1")

    loss_part, dh4, g_final, do2_1, dg2_1 = final_loss(h3, o2_1, m_lat[1, 5], final_norm_w[None], loss_target[0],
                                                       "loss_head")
    da2_1, gf1 = ffn_bwd(do2_1, 1, ffn1_saved, "1")
    dh3, dn2_1, dsc2_1, dsh2_1, do_conf, dg1_1, g_b_pw2 = normres_bwd(
        h3, norm2_w[1][None], m_lat[1, 4], m_lat[1, 3], da2_1, dh4, o_conf, m_lat[1, 2], b_pw2, "ffn1_norm_bwd")
    gf1.update(norm2=dn2_1, sh2=dsh2_1, sc2=dsc2_1, g2=dg2_1)
    g_pw2 = matmul(s1, do_conf, "tn", bf16, "conf_pw2_dw")
    ds1 = matmul(do_conf, w_pw2, "nt", bf16, "conf_pw2_dx")
    ddwc, g_ln_w, g_ln_b = ln_silu_bwd(dwc, ln_w, ln_b, ds1, "conf_ln_bwd")
    dpa, dpg, dba, dbg, g_w_dw, g_b_dw = conf_glu_conv_bwd(pa, pg, b_pw1, w_dw, ddwc, "conf_conv_bwd")
    g_b_pw1 = jnp.concatenate([dba, dbg], axis=1)
    g_pw1 = smm_dw(a1, [dpa, dpg], 2 * D // N_DEV, pw1_segs, 1, False, "conf_pw1_dw")
    da1, _ = smm_dx([dpa, dpg], w_pw1, None, pw1_segs, bf16, "conf_pw1_dx")
    dh2, g_n1_1, dsc1_1, dsh1_1, do2_0, dg2_0, _ = normres_bwd(
        h2, norm1_w[1][None], m_lat[1, 1], m_lat[1, 0], da1, dh3, o2_0, m_lat[0, 5], zero_row, "conf_norm_bwd")
    da2_0, gf0 = ffn_bwd(do2_0, 0, ffn0_saved, "0")
    dh1, dn2_0, dsc2_0, dsh2_0, do_ssd, dg1_0, _ = normres_bwd(
        h1, norm2_w[0][None], m_lat[0, 4], m_lat[0, 3], da2_0, dh2, o_ssd, m_lat[0, 2], zero_row, "ffn0_norm_bwd")
    gf0.update(norm2=dn2_0, sh2=dsh2_0, sc2=dsc2_0, g2=dg2_0)
    g_w_out = matmul(yn, do_ssd, "tn", bf16, "ssd_out_dw")
    dyn = matmul(do_ssd, w_out, "nt", bf16, "ssd_out_dx")
    core = mc.reshape(1).astype(jnp.int32)

    def by_device(t):
        return t.reshape((4, 2, -1, t.shape[-1]))

    early = [by_device(t) for t in (gf1["w_up"], gf1["w_down"], g_pw2, g_pw1, gf0["w_up"], gf0["w_down"], g_w_out)]
    (dy, dz, g_dexp, g_ssd_norm), early_sib = ssd_gate_bwd(
        y2, xbc, z, dexp, ssd_norm_w, dyn, LC // GTB, "ssd_gate_bwd", SiblingExchange(early))
    early_part = [add_own(t, r_, core, f"reduce_add{i}") for i, (t, r_) in enumerate(zip(early, early_sib))]
    (dxbc2, ddtc, ddtr, dbc, dbr, dalc, dalr), early_red = ssd_scan_bwd(
        xbc, dtc, dtr, bc_, br_, alc, alr, s_in_all, dy, nctx, "ssd_scan_bwd", ChipsExchange(early_part))
    ddt = (jnp.transpose(ddtc, (2, 0, 1, 3)) + jnp.transpose(ddtr, (3, 0, 1, 2))).reshape(T, n_dt)
    g_dt_bias = (dbc[:, :, 0, :] + dbr[:, :, :, 0]).reshape(2, NH_SSD)
    g_a_log = (dalc[:, :, 0, :] + dalr[:, :, :, 0]).reshape(2, NH_SSD)
    g_ssd_d = g_dexp[0, :NH_SSD]
    du, g_conv_w5, g_conv_b5 = ssd_conv_bwd(xbc_pre, conv_w5, xbc_dsilu, dxbc2, dy, dexp, segs, "ssd_conv_bwd")
    ddt_p = _pad_to(ddt, 128).astype(bf16)
    g_w_in = smm_dw(a0, [dz, du, ddt_p], w_in.shape[-1], in_segs, 2, True, "ssd_in_dw")
    g_ffn_cw = jnp.stack([gf0["conv_w"], gf1["conv_w"]])
    small_shards = [_col_shards(t) for t in (g_conv_w5, g_b_pw1, g_w_dw, g_b_dw, g_ln_w, g_ln_b, g_b_pw2, g_ffn_cw)]
    gsizes = [s_.shape[1] for s_ in small_shards]
    g_small = _pad_to(jnp.concatenate(small_shards, axis=1), PACK_ALIGN).astype(bf16)
    late = [by_device(g_w_in), by_device(g_small.reshape(N_DEV, -1, PACK_W))]
    da0, late_sib = smm_dx([dz, du, ddt_p], w_in, None, in_segs, f32, "ssd_in_dx", SiblingExchange(late))
    late_part = [add_own(t, r_, core, f"reduce_add_late{i}", twice=True) for i, (t, r_) in enumerate(zip(late, late_sib))]
    late_flying = chips_start([p_[0] for p_ in late_part], [p_[1] for p_ in late_part], "reduce_chips_late_start")
    dh0, g_n1_0, dsc1_0, dsh1_0 = modnorm_bwd(hx, norm1_w[0][None], sc0, sh0, da0, dh1, LC // TB, "ssd_norm_bwd",
                                              ctx=ctx[0])
    grad_x = dh0[None]

    r_up1, r_down1, r_pw2, r_pw1, r_up0, r_down0, r_out = early_red
    big = {}
    def tr(t):
        return jnp.swapaxes(t, -1, -2)

    up_t, m_up_t, v_up_t = tr(ffn_w_up), tr(m_ffn_w_up), tr(v_ffn_w_up)
    send_sems, recv_sems, late_p, late_land, token = late_flying
    up0 = sum_adamw(r_up0, up_t, m_up_t, v_up_t, 0, "adamw_ffn_w_up0", after=token)
    up1 = sum_adamw(r_up1, up_t, m_up_t, v_up_t, 1, "adamw_ffn_w_up1", into=up0)
    big["ffn_w_up"] = tuple(tr(t) for t in up1)
    big["conf_w_pw1"] = sum_adamw(r_pw1, conf_w_pw1[0], m_conf_w_pw1[0], v_conf_w_pw1[0], None, "adamw_conf_w_pw1",
                                  after=up1[0])
    big["ssd_w_out"] = sum_adamw(r_out, ssd_w_out[0], m_ssd_w_out[0], v_ssd_w_out[0], None, "adamw_ssd_w_out",
                                 after=big["conf_w_pw1"][0])
    dn0 = sum_adamw(r_down0, ffn_w_down, m_ffn_w_down, v_ffn_w_down, 0, "adamw_ffn_w_down0", after=big["ssd_w_out"][0])
    big["ffn_w_down"] = sum_adamw(r_down1, ffn_w_down, m_ffn_w_down, v_ffn_w_down, 1, "adamw_ffn_w_down1", into=dn0)
    big["conf_w_pw2"] = sum_adamw(r_pw2, conf_w_pw2[0], m_conf_w_pw2[0], v_conf_w_pw2[0], None, "adamw_conf_w_pw2",
                                  after=big["ffn_w_down"][0])

    zeros_d = jnp.zeros((1, D), f32)
    dm_lat = jnp.stack([
        jnp.concatenate([dsh1_0[1], dsc1_0[1], dg1_0, gf0["sh2"], gf0["sc2"], gf0["g2"]], axis=1),
        jnp.concatenate([dsh1_1, dsc1_1, dg1_1, gf1["sh2"], gf1["sc2"], gf1["g2"]], axis=1)])
    dm_ctx = jnp.stack([
        jnp.concatenate([dsh1_0[0], dsc1_0[0]] + [zeros_d] * 4, axis=1), jnp.zeros((1, 6 * D), f32)])
    dm_mine = jnp.concatenate([dm_lat.reshape(2, 6 * D), dm_ctx.reshape(2, 6 * D),
                               jnp.zeros((4, 6 * D), f32)], axis=0)
    dm_g = allgather_small(dm_mine, "gather_dmod", after=big["conf_w_pw2"][0])
    dm_all = jnp.concatenate([jnp.moveaxis(dm_g[:, 0:2], 0, 1), jnp.moveaxis(dm_g[:, 2:4], 0, 1)], axis=1)
    dm_sh = lax.dynamic_slice_in_dim(dm_all, me * mod_cols, mod_cols, axis=2)
    g_mod_w, g_cctx_part, g_mod_b = mod_bwd(c16, mod_w, dm_sh, dm_all, "mod_bwd")

    rep = [jnp.stack([g_n1_0[0], g_n1_1[0]]), jnp.stack([gf0["norm2"][0], gf1["norm2"][0]]), g_conv_b5, g_dt_bias, g_a_log,
           g_ssd_d, g_ssd_norm, jnp.stack([gf0["conv_b"][0], gf1["conv_b"][0]]), g_final, g_cctx_part, loss_part[:, :1]]
    rep_sizes = [r_.size for r_ in rep]
    rep_flat = _pad_to(jnp.concatenate([r_.reshape(-1) for r_ in rep]), 8 * PACK_W).reshape(-1, PACK_W)
    _, rep_sum = allgather_small(rep_flat, "reduce_replicated", with_sum=True)
    rep_sum = rep_sum.reshape(-1)
    roffs = [0]
    for s_ in rep_sizes:
        roffs.append(roffs[-1] + s_)
    rp = [rep_sum[roffs[i]:roffs[i + 1]] for i in range(len(rep_sizes))]
    loss = rp[10].reshape(())

    r_in, r_small = chips_wait(send_sems, recv_sems, late_p, late_land, rep_sum, "reduce_chips_late_wait")
    w_in_res = sum_adamw(r_in, tr(ssd_w_in[0]), tr(m_ssd_w_in[0]), tr(v_ssd_w_in[0]), None, "adamw_ssd_w_in")
    big["ssd_w_in"] = tuple(tr(t) for t in w_in_res)
    g_flat = sum_rows(r_small, "reduce_sum_small").reshape(-1)
    goffs = [0]
    for s_ in gsizes:
        goffs.append(goffs[-1] + s_)
    gs = [g_flat[goffs[i]:goffs[i + 1]] for i in range(len(gsizes))]
    grads = {
        "c_ctx": rp[9], "mod_w": g_mod_w, "mod_b": g_mod_b, "norm1_w": rp[0], "norm2_w": rp[1],
        "ssd_conv_w": gs[0], "ssd_conv_b": rp[2], "ssd_dt_bias": rp[3], "ssd_a_log": rp[4], "ssd_d": rp[5],
        "ssd_norm_w": rp[6], "conf_b_pw1": gs[1], "conf_w_dw": gs[2],
        "conf_b_dw": gs[3], "conf_ln_w": gs[4], "conf_ln_b": gs[5], "conf_b_pw2": gs[6],
        "ffn_conv_w": gs[7], "ffn_conv_b": rp[7], "final_norm_w": rp[8],
    }
    weights = dict(c_ctx=c_ctx, mod_w=mod_w, mod_b=mod_b, norm1_w=norm1_w, norm2_w=norm2_w, ssd_w_in=ssd_w_in, ssd_conv_w=ssd_conv_w, ssd_conv_b=ssd_conv_b, ssd_dt_bias=ssd_dt_bias, ssd_a_log=ssd_a_log, ssd_d=ssd_d, ssd_norm_w=ssd_norm_w, ssd_w_out=ssd_w_out, conf_w_pw1=conf_w_pw1, conf_b_pw1=conf_b_pw1, conf_w_dw=conf_w_dw, conf_b_dw=conf_b_dw, conf_ln_w=conf_ln_w, conf_ln_b=conf_ln_b, conf_w_pw2=conf_w_pw2, conf_b_pw2=conf_b_pw2, ffn_w_up=ffn_w_up, ffn_conv_w=ffn_conv_w, ffn_conv_b=ffn_conv_b, ffn_w_down=ffn_w_down, final_norm_w=final_norm_w)
    m_in = dict(c_ctx=m_c_ctx, mod_w=m_mod_w, mod_b=m_mod_b, norm1_w=m_norm1_w, norm2_w=m_norm2_w, ssd_w_in=m_ssd_w_in, ssd_conv_w=m_ssd_conv_w, ssd_conv_b=m_ssd_conv_b, ssd_dt_bias=m_ssd_dt_bias, ssd_a_log=m_ssd_a_log, ssd_d=m_ssd_d, ssd_norm_w=m_ssd_norm_w, ssd_w_out=m_ssd_w_out, conf_w_pw1=m_conf_w_pw1, conf_b_pw1=m_conf_b_pw1, conf_w_dw=m_conf_w_dw, conf_b_dw=m_conf_b_dw, conf_ln_w=m_conf_ln_w, conf_ln_b=m_conf_ln_b, conf_w_pw2=m_conf_w_pw2, conf_b_pw2=m_conf_b_pw2, ffn_w_up=m_ffn_w_up, ffn_conv_w=m_ffn_conv_w, ffn_conv_b=m_ffn_conv_b, ffn_w_down=m_ffn_w_down, final_norm_w=m_final_norm_w)
    v_in = dict(c_ctx=v_c_ctx, mod_w=v_mod_w, mod_b=v_mod_b, norm1_w=v_norm1_w, norm2_w=v_norm2_w, ssd_w_in=v_ssd_w_in, ssd_conv_w=v_ssd_conv_w, ssd_conv_b=v_ssd_conv_b, ssd_dt_bias=v_ssd_dt_bias, ssd_a_log=v_ssd_a_log, ssd_d=v_ssd_d, ssd_norm_w=v_ssd_norm_w, ssd_w_out=v_ssd_w_out, conf_w_pw1=v_conf_w_pw1, conf_b_pw1=v_conf_b_pw1, conf_w_dw=v_conf_w_dw, conf_b_dw=v_conf_b_dw, conf_ln_w=v_conf_ln_w, conf_ln_b=v_conf_ln_b, conf_w_pw2=v_conf_w_pw2, conf_b_pw2=v_conf_b_pw2, ffn_w_up=v_ffn_w_up, ffn_conv_w=v_ffn_conv_w, ffn_conv_b=v_ffn_conv_b, ffn_w_down=v_ffn_w_down, final_norm_w=v_final_norm_w)

    out_g, out_d, out_m, out_v = [], [], [], []
    for name_, w_ in weights.items():
        shape = w_.shape
        if name_ in big:
            for lst, t in zip((out_g, out_d, out_m, out_v), big[name_]):
                lst.append(t.reshape(shape))
            continue
        cols2 = shape[-1] if len(shape) > 1 else shape[0]
        g2 = grads[name_].reshape(-1, cols2)
        d_, nm_, nv_ = adamw(w_.reshape(-1, cols2), g2, m_in[name_].reshape(-1, cols2), v_in[name_].reshape(-1, cols2),
                             f"adamw_{name_}")
        out_g.append(g2.reshape(shape))
        out_d.append(d_.reshape(shape))
        out_m.append(nm_.reshape(shape))
        out_v.append(nv_.reshape(shape))
    return (loss, grad_x, *out_g, *out_d, *out_m, *out_v)
```

```python
import functools

import jax
import jax.numpy as jnp
from jax import lax
from jax.experimental import pallas as pl
from jax.experimental.pallas import tpu as pltpu

f32 = jnp.float32
bf16 = jnp.bfloat16
HI = lax.Precision.HIGHEST
S = jax.ShapeDtypeStruct
MESH = pl.DeviceIdType.MESH
ANY = pl.BlockSpec(memory_space=pl.ANY)
VMEM = pl.BlockSpec(memory_space=pltpu.VMEM)

N_DEV = 8
D = 1024
DI = 2048
CONVD = 4096
FH = 2816
GRID_W = 64
Q = 128
HPG = 4
P = 64
N = 128
G = 8
GW = HPG * P
NH_SSD = G * HPG
EPS = 1e-6
ADAM_LR, ADAM_B1, ADAM_B2, ADAM_EPS, ADAM_WD, ADAM_STEP = 0.001, 0.9, 0.999, 1e-08, 0.01, 10
VMEM_LIMIT_BYTES = 56 * 1024 * 1024
PACK_W = 1024
TB = 256


def _cparams(*sem):
    return pltpu.CompilerParams(dimension_semantics=sem, vmem_limit_bytes=VMEM_LIMIT_BYTES)


def _pick(n, prefs):
    for p in prefs:
        if n % p == 0:
            return p
    return n


def _sigmoid(x):
    return 1.0 / (1.0 + jnp.exp(-x))


def _softplus(x):
    return jnp.maximum(x, 0.0) + jnp.log(1.0 + jnp.exp(-jnp.abs(x)))


def matmul(a, b, mode, out_dtype, name):
    if mode == "nn":
        (M, K), (_, Nn) = a.shape, b.shape
        bm, bn, bk = _pick(M, (512, 384, 256, 128)), Nn, K
    elif mode == "tn":
        (K, M), (_, Nn) = a.shape, b.shape
        bm, bn, bk = M, Nn, _pick(K, (256, 128))
    else:
        (M, K), (Nn, _) = a.shape, b.shape
        bm, bn, bk = _pick(M, (512, 384, 256, 128)), Nn, K
    nk = K // bk
    dims = {"nn": (((1,), (0,)), ((), ())), "tn": (((0,), (0,)), ((), ())), "nt": (((1,), (1,)), ((), ()))}[mode]

    def body(a_ref, b_ref, o_ref, acc_ref):
        k = pl.program_id(2)

        @pl.when(k == 0)
        def _():
            acc_ref[...] = jnp.zeros_like(acc_ref)

        acc_ref[...] += lax.dot_general(a_ref[...].astype(bf16), b_ref[...].astype(bf16), dims,
                                        preferred_element_type=f32)

        @pl.when(k == nk - 1)
        def _():
            o_ref[...] = acc_ref[...].astype(out_dtype)

    if mode == "nn":
        a_spec = pl.BlockSpec((bm, bk), lambda i, j, k: (i, k))
        b_spec = pl.BlockSpec((bk, bn), lambda i, j, k: (k, j))
    elif mode == "tn":
        a_spec = pl.BlockSpec((bk, bm), lambda i, j, k: (k, i))
        b_spec = pl.BlockSpec((bk, bn), lambda i, j, k: (k, j))
    else:
        a_spec = pl.BlockSpec((bm, bk), lambda i, j, k: (i, k))
        b_spec = pl.BlockSpec((bn, bk), lambda i, j, k: (j, k))
    return pl.pallas_call(
        body, grid=(M // bm, Nn // bn, nk), in_specs=[a_spec, b_spec],
        out_specs=pl.BlockSpec((bm, bn), lambda i, j, k: (i, j)),
        out_shape=S((M, Nn), out_dtype), scratch_shapes=[pltpu.VMEM((bm, bn), f32)],
        compiler_params=_cparams("parallel", "parallel", "arbitrary"), name=name,
    )(a, b)


SMM_BM = 256
SMM_ROWS = (256,)


def _shard_pieces(seg_widths, n):
    bounds = [0]
    for sw in seg_widths:
        bounds.append(bounds[-1] + sw)
    assert bounds[-1] == N_DEV * n, (seg_widths, n)
    out = []
    for j in range(N_DEV):
        lo, hi = j * n, (j + 1) * n
        pcs = []
        for si in range(len(seg_widths)):
            a, b = max(lo, bounds[si]), min(hi, bounds[si + 1])
            if a < b:
                pcs.append((si, a - bounds[si], a - lo, b - a))
        out.append(pcs)
    return out


def _w_spec(w, layer):
    if layer is None:
        return pl.BlockSpec(w.shape, lambda *idx: (0, 0, 0))
    return pl.BlockSpec((N_DEV, None) + w.shape[2:], lambda *idx: (0, layer, 0, 0))


def smm_fwd(a, w, layer, seg_widths, name, hosted=None):
    M, K = a.shape
    n = w.shape[-1]
    pieces = _shard_pieces(seg_widths, n)
    padded = [sw + (-sw) % 128 for sw in seg_widths]
    bm = _pick(M, SMM_ROWS)

    def body(a_ref, w_ref, *o_refs):
        av = a_ref[...]
        for si, sw in enumerate(seg_widths):
            if padded[si] != sw:
                o_refs[si][:, pl.ds(padded[si] - 128, 128)] = jnp.zeros((bm, 128), f32)
        for j in range(N_DEV):
            for si, soff, woff, wd in pieces[j]:
                o_refs[si][:, pl.ds(soff, wd)] = jnp.dot(av, w_ref[j, :, pl.ds(woff, wd)], preferred_element_type=f32)

    outs, extra = _host_call(
        body, (M // bm,), [pl.BlockSpec((bm, K), lambda i: (i, 0)), _w_spec(w, layer)],
        [pl.BlockSpec((bm, pw), lambda i: (i, 0)) for pw in padded], [S((M, pw), f32) for pw in padded], [],
        ("parallel",), name, (a, w), hosted)
    return outs if hosted is None else (outs, extra)


def smm_dx(d_segs, w, layer, seg_widths, out_dtype, name, hosted=None):
    M = d_segs[0].shape[0]
    K, n = w.shape[-2], w.shape[-1]
    pieces = _shard_pieces(seg_widths, n)
    ns = len(d_segs)
    bm = _pick(M, SMM_ROWS)

    def body(*refs):
        d_refs, w_ref, o_ref = refs[:ns], refs[ns], refs[ns + 1]
        acc = jnp.zeros((bm, K), f32)
        for j in range(N_DEV):
            for si, soff, woff, wd in pieces[j]:
                acc = acc + lax.dot_general(d_refs[si][:, pl.ds(soff, wd)], w_ref[j, :, pl.ds(woff, wd)],
                                            (((1,), (1,)), ((), ())), preferred_element_type=f32)
        o_ref[...] = acc.astype(out_dtype)

    (out,), extra = _host_call(
        body, (M // bm,),
        [pl.BlockSpec((bm, d.shape[1]), lambda i: (i, 0)) for d in d_segs] + [_w_spec(w, layer)],
        [pl.BlockSpec((bm, K), lambda i: (i, 0))], [S((M, K), out_dtype)], [], ("parallel",), name,
        (*d_segs, w), hosted)
    return out, extra


def smm_dw(a, d_segs, n, seg_widths, ngrp, transposed, name):
    M, K = a.shape
    pieces = _shard_pieces(seg_widths, n)
    per = N_DEV // ngrp
    nI = M // SMM_BM
    ns = len(d_segs)
    shard = (n, K) if transposed else (K, n)

    def body(*refs):
        a_ref, d_refs, o_ref, acc_ref = refs[0], refs[1:1 + ns], refs[1 + ns], refs[2 + ns]
        grp = pl.program_id(0)
        i = pl.program_id(1)

        @pl.when(i == 0)
        def _():
            acc_ref[...] = jnp.zeros_like(acc_ref)

        av = a_ref[...]
        for gs in range(ngrp):
            def one_group(gs=gs):
                for jj in range(per):
                    for si, soff, woff, wd in pieces[gs * per + jj]:
                        dv = d_refs[si][:, pl.ds(soff, wd)]
                        if transposed:
                            acc_ref[jj, pl.ds(woff, wd), :] += lax.dot_general(
                                dv, av, (((0,), (0,)), ((), ())), preferred_element_type=f32)
                        else:
                            acc_ref[jj, :, pl.ds(woff, wd)] += lax.dot_general(
                                av, dv, (((0,), (0,)), ((), ())), preferred_element_type=f32)
            pl.when(grp == gs)(one_group)

        @pl.when(i == nI - 1)
        def _():
            o_ref[...] = acc_ref[...].astype(bf16)

    return pl.pallas_call(
        body, grid=(ngrp, nI),
        in_specs=[pl.BlockSpec((SMM_BM, K), lambda g, i: (i, 0))]
        + [pl.BlockSpec((SMM_BM, d.shape[1]), lambda g, i: (i, 0)) for d in d_segs],
        out_specs=pl.BlockSpec((per,) + shard, lambda g, i: (g, 0, 0)), out_shape=S((N_DEV,) + shard, bf16),
        scratch_shapes=[pltpu.VMEM((per,) + shard, f32)],
        compiler_params=_cparams("arbitrary", "arbitrary"), name=name)(a, *d_segs)


def _modnorm_f(h, w, sc, sh):
    y = h * lax.rsqrt(jnp.mean(h * h, axis=-1, keepdims=True) + EPS)
    return (y * w) * (1.0 + sc) + sh


def _kind_specs(nctxb):
    if nctxb > 0:
        return pl.BlockSpec((None, 1, D), lambda i: (jnp.where(i < nctxb, 0, 1), 0, 0))
    return pl.BlockSpec((None, 1, D), lambda i: (0, 0, 0))


def _two_part_specs(nctxb):
    return (pl.BlockSpec((TB, D), lambda i: (jnp.minimum(i, nctxb - 1), 0)),
            pl.BlockSpec((TB, D), lambda i: (jnp.maximum(i - nctxb, 0), 0)))


def modnorm_fwd(h, w, sc, sh, nctxb, name, ctx=None):
    if ctx is None:
        T = h.shape[0]

        def body(h_ref, w_ref, sc_ref, sh_ref, o_ref):
            o_ref[...] = _modnorm_f(h_ref[...], w_ref[...], sc_ref[...], sh_ref[...]).astype(bf16)

        hspecs, hargs = [pl.BlockSpec((TB, D), lambda i: (i, 0))], (h,)
    else:
        T = h.shape[0] + ctx.shape[0]

        def body(c_ref, h_ref, w_ref, sc_ref, sh_ref, o_ref):
            hv = jnp.where(pl.program_id(0) < nctxb, c_ref[...], h_ref[...])
            o_ref[...] = _modnorm_f(hv, w_ref[...], sc_ref[...], sh_ref[...]).astype(bf16)

        hspecs, hargs = list(_two_part_specs(nctxb)), (ctx, h)
    row = pl.BlockSpec((1, D), lambda i: (0, 0))
    ks = _kind_specs(nctxb)
    return pl.pallas_call(body, grid=(T // TB,), in_specs=hspecs + [row, ks, ks],
                          out_specs=pl.BlockSpec((TB, D), lambda i: (i, 0)), out_shape=S((T, D), bf16),
                          compiler_params=_cparams("parallel"), name=name)(*hargs, w, sc, sh)


def modnorm_bwd(h, w, sc, sh, da, dres, nctxb, name, ctx=None):
    T = h.shape[0] + (0 if ctx is None else ctx.shape[0])
    kinds = sc.shape[0]
    nh = 1 if ctx is None else 2

    def body(*refs):
        w_ref, sc_ref, sh_ref, da_ref, dres_ref, dh_ref, dw_ref, dsc_ref, dsh_ref = refs[nh:]
        i = pl.program_id(0)
        hv = refs[0][...] if ctx is None else jnp.where(i < nctxb, refs[0][...], refs[1][...])
        _, vjp = jax.vjp(_modnorm_f, hv, w_ref[...], sc_ref[...], sh_ref[...])
        dh, dw, dsc, dsh = vjp(da_ref[...].astype(f32))
        dh_ref[...] = dres_ref[...] + dh

        @pl.when(i == 0)
        def _():
            dw_ref[...] = jnp.zeros_like(dw_ref)

        @pl.when((i == 0) | (i == nctxb))
        def _():
            dsc_ref[...] = jnp.zeros_like(dsc_ref)
            dsh_ref[...] = jnp.zeros_like(dsh_ref)

        dw_ref[...] += dw
        dsc_ref[...] += dsc
        dsh_ref[...] += dsh

    blk = pl.BlockSpec((TB, D), lambda i: (i, 0))
    lat = pl.BlockSpec((TB, D), lambda i: (jnp.maximum(i - nctxb, 0), 0))
    row = pl.BlockSpec((1, D), lambda i: (0, 0))
    ks = _kind_specs(nctxb)
    hspecs, hargs = ([blk], (h,)) if ctx is None else (list(_two_part_specs(nctxb)), (ctx, h))
    return pl.pallas_call(
        body, grid=(T // TB,), in_specs=hspecs + [row, ks, ks, blk, lat], out_specs=[lat, row, ks, ks],
        out_shape=[S((T - nctxb * TB, D), f32), S((1, D), f32), S((kinds, 1, D), f32), S((kinds, 1, D), f32)],
        compiler_params=_cparams("arbitrary"), name=name)(*hargs, w, sc, sh, da, dres)


def resnorm_fwd(h, o, g, b, w, sc, sh, name):
    T = h.shape[0]

    def body(h_ref, o_ref, g_ref, b_ref, w_ref, sc_ref, sh_ref, hn_ref, a_ref):
        hn = h_ref[...] + g_ref[...] * (o_ref[...] + b_ref[...])
        hn_ref[...] = hn
        a_ref[...] = _modnorm_f(hn, w_ref[...], sc_ref[...], sh_ref[...]).astype(bf16)

    blk = pl.BlockSpec((TB, D), lambda i: (i, 0))
    row = pl.BlockSpec((1, D), lambda i: (0, 0))
    return pl.pallas_call(body, grid=(T // TB,), in_specs=[blk, blk, row, row, row, row, row], out_specs=[blk, blk],
                          out_shape=[S((T, D), f32), S((T, D), bf16)], compiler_params=_cparams("parallel"),
                          name=name)(h, o, g, b, w, sc, sh)


def normres_bwd(h, w, sc, sh, da, dres, o, g, b, name):
    T = h.shape[0]

    def body(h_ref, w_ref, sc_ref, sh_ref, da_ref, dres_ref, o_ref, g_ref, b_ref,
             dh_ref, dw_ref, dsc_ref, dsh_ref, do_ref, dg_ref, db_ref):
        _, vjp = jax.vjp(_modnorm_f, h_ref[...], w_ref[...], sc_ref[...], sh_ref[...])
        dhn, dw, dsc, dsh = vjp(da_ref[...].astype(f32))
        dh = dres_ref[...] + dhn
        dh_ref[...] = dh
        do = g_ref[...] * dh
        do_ref[...] = do.astype(bf16)
        sums = (dw, dsc, dsh, jnp.sum(dh * (o_ref[...] + b_ref[...]), axis=0, keepdims=True),
                jnp.sum(do, axis=0, keepdims=True))

        @pl.when(pl.program_id(0) == 0)
        def _():
            for r_ in (dw_ref, dsc_ref, dsh_ref, dg_ref, db_ref):
                r_[...] = jnp.zeros_like(r_)

        for r_, s_ in zip((dw_ref, dsc_ref, dsh_ref, dg_ref, db_ref), sums):
            r_[...] += s_

    blk = pl.BlockSpec((TB, D), lambda i: (i, 0))
    row = pl.BlockSpec((1, D), lambda i: (0, 0))
    return pl.pallas_call(
        body, grid=(T // TB,), in_specs=[blk, row, row, row, blk, blk, blk, row, row],
        out_specs=[blk, row, row, row, blk, row, row],
        out_shape=[S((T, D), f32), S((1, D), f32), S((1, D), f32), S((1, D), f32), S((T, D), bf16), S((1, D), f32),
                   S((1, D), f32)],
        compiler_params=_cparams("arbitrary"), name=name)(h, w, sc, sh, da, dres, o, g, b)


def final_loss(h, o, g, w, tgt, name):
    T = h.shape[0]

    def f(hv, wv, tv):
        y = (hv * lax.rsqrt(jnp.mean(hv * hv, axis=-1, keepdims=True) + EPS)) * wv
        e = y - tv
        return 0.5 * jnp.sum(jnp.sum(e * e, axis=-1, keepdims=True), axis=0, keepdims=True) * (1.0 / D)

    def body(h_ref, o_ref, g_ref, w_ref, t_ref, loss_ref, dh_ref, dw_ref, do_ref, dg_ref):
        i = pl.program_id(0)
        tv = t_ref[...]
        ov = o_ref[...]
        gv = g_ref[...]
        val, vjp = jax.vjp(lambda a, b_: f(a, b_, tv), h_ref[...] + gv * ov, w_ref[...])
        dh, dw = vjp(jnp.ones((1, 1), f32))
        dh_ref[...] = dh
        do_ref[...] = (gv * dh).astype(bf16)

        @pl.when(i == 0)
        def _():
            loss_ref[...] = jnp.zeros_like(loss_ref)
            dw_ref[...] = jnp.zeros_like(dw_ref)
            dg_ref[...] = jnp.zeros_like(dg_ref)

        loss_ref[...] += jnp.broadcast_to(val, (1, 128))
        dw_ref[...] += dw
        dg_ref[...] += jnp.sum(dh * ov, axis=0, keepdims=True)

    blk = pl.BlockSpec((TB, D), lambda i: (i, 0))
    row = pl.BlockSpec((1, D), lambda i: (0, 0))
    return pl.pallas_call(body, grid=(T // TB,), in_specs=[blk, blk, row, row, blk],
                          out_specs=[pl.BlockSpec((1, 128), lambda i: (0, 0)), blk, row, blk, row],
                          out_shape=[S((1, 128), f32), S((T, D), f32), S((1, D), f32), S((T, D), bf16), S((1, D), f32)],
                          compiler_params=_cparams("arbitrary"), name=name)(h, o, g, w, tgt)


CB = 256
RT = 32
RTB = 16


def _fold8(t):
    acc = t[0:8]
    for k in range(1, t.shape[0] // 8):
        acc = acc + t[8 * k:8 * (k + 1)]
    return acc


def _rows(start, off=0, rt=RT):
    return pl.ds(pl.multiple_of(start + off, 8), rt)


def _rowsb(start, off=0):
    return _rows(start, off, RTB)


def _zero_rows(ref, start, n):
    ref[pl.ds(start, n), :] = jnp.zeros((n, ref.shape[1]), f32)


K5, HALF5, PAD5 = 5, 2, 8


def _taps5(base_ref, r, rt, sign):
    n = rt + 2 * PAD5
    v = base_ref[pl.ds(pl.multiple_of(r, 8), n), :]
    taps = []
    for k in range(K5):
        o = sign * (k - HALF5)
        rolled = v if o == 0 else pltpu.roll(v, (-o) % n, axis=0)
        taps.append(rolled[PAD5:PAD5 + rt])
    return taps


def ssd_conv_fwd(u, w, b, segs, name, hosted=None):
    T = u.shape[0]
    maxlen = max(ln for _, ln in segs)

    def body(u_ref, w_ref, b_ref, o_ref, ds_ref, base_ref):
        wv = [w_ref[pl.ds(k, 1), :] for k in range(K5)]
        bv = b_ref[...]
        for s0, ln in segs:
            _zero_rows(base_ref, 0, PAD5)
            _zero_rows(base_ref, PAD5 + ln, PAD5)
            base_ref[pl.ds(PAD5, ln), :] = u_ref[pl.ds(s0, ln), :]

            def tile(i, carry):
                r = i * RT
                taps = _taps5(base_ref, r, RT, 1)
                acc = jnp.broadcast_to(bv, (RT, CB))
                for k in range(K5):
                    acc = acc + taps[k] * wv[k]
                sg = _sigmoid(acc)
                o_ref[_rows(r, s0), :] = acc * sg
                ds_ref[_rows(r, s0), :] = sg * (1.0 + acc * (1.0 - sg))
                return carry

            lax.fori_loop(0, ln // RT, tile, 0, unroll=2)

    cblk = pl.BlockSpec((T, CB), lambda j: (0, j))
    (out, dsilu), extra = _host_call(
        body, (CONVD // CB,),
        [cblk, pl.BlockSpec((K5, CB), lambda j: (0, j)), pl.BlockSpec((1, CB), lambda j: (0, j))],
        [cblk, cblk], [S((T, CONVD), f32), S((T, CONVD), f32)],
        [pltpu.VMEM((maxlen + 2 * PAD5, CB), f32)], ("parallel",), name, (u, w, b), hosted)
    return out, dsilu, extra


def ssd_conv_bwd(proj, w, dsilu, dy2, dyskip, dexp, segs, name):
    T = proj.shape[0]
    maxlen = max(ln for _, ln in segs)
    nskip = DI // CB

    def body(u_ref, w_ref, ds_ref, dya_ref, dyb_ref, dsk_ref, dexp_ref, du_ref, dw_ref, db_ref, base_ref, dbase_ref):
        wv = [w_ref[pl.ds(k, 1), :] for k in range(K5)]
        has_skip = (pl.program_id(0) < nskip).astype(f32) * dexp_ref[...]
        acc8 = tuple(jnp.zeros((8, CB), f32) for _ in range(K5 + 1))
        for s0, ln in segs:
            for ref in (base_ref, dbase_ref):
                _zero_rows(ref, 0, PAD5)
                _zero_rows(ref, PAD5 + ln, PAD5)
            base_ref[pl.ds(PAD5, ln), :] = u_ref[pl.ds(s0, ln), :]

            def tile1(i, carry):
                r = i * RTB
                dy = dya_ref[_rowsb(r, s0), :] + dyb_ref[_rowsb(r, s0), :] + has_skip * dsk_ref[_rowsb(r, s0), :]
                dpre = dy * ds_ref[_rowsb(r, s0), :]
                dbase_ref[_rowsb(r, PAD5), :] = dpre
                taps = _taps5(base_ref, r, RTB, 1)
                new = [carry[k] + _fold8(dpre * taps[k]) for k in range(K5)]
                new.append(carry[K5] + _fold8(dpre))
                return tuple(new)

            acc8 = lax.fori_loop(0, ln // RTB, tile1, acc8, unroll=2)

            def tile2(i, carry):
                r = i * RTB
                taps = _taps5(dbase_ref, r, RTB, -1)
                du = jnp.zeros((RTB, CB), f32)
                for k in range(K5):
                    du = du + taps[k] * wv[k]
                du_ref[_rowsb(r, s0), :] = du.astype(bf16)
                return carry

            lax.fori_loop(0, ln // RTB, tile2, 0, unroll=4)
        for k in range(K5):
            dw_ref[pl.ds(k, 1), :] = jnp.sum(acc8[k], axis=0, keepdims=True)
        db_ref[...] = jnp.sum(acc8[K5], axis=0, keepdims=True)

    cblk = pl.BlockSpec((T, CB), lambda j: (0, j))
    return pl.pallas_call(
        body, grid=(CONVD // CB,),
        in_specs=[cblk, pl.BlockSpec((K5, CB), lambda j: (0, j)), cblk,
                  pl.BlockSpec((None, T, CB), lambda j: (0, 0, j)), pl.BlockSpec((None, T, CB), lambda j: (1, 0, j)),
                  pl.BlockSpec((T, CB), lambda j: (0, jnp.minimum(j, nskip - 1))),
                  pl.BlockSpec((1, CB), lambda j: (0, jnp.minimum(j, nskip - 1)))],
        out_specs=[cblk, pl.BlockSpec((K5, CB), lambda j: (0, j)), pl.BlockSpec((1, CB), lambda j: (0, j))],
        out_shape=[S((T, CONVD), bf16), S((K5, CONVD), f32), S((1, CONVD), f32)],
        scratch_shapes=[pltpu.VMEM((maxlen + 2 * PAD5, CB), f32), pltpu.VMEM((maxlen + 2 * PAD5, CB), f32)],
        compiler_params=_cparams("parallel"), name=name)(proj, w, dsilu, dy2, dy2, dyskip, dexp)


GPAD = GRID_W


def _grid_copies(g_ref, src, L):
    col = lax.broadcasted_iota(jnp.int32, (L, CB), 0) & (GRID_W - 1)
    for d in range(3):
        _zero_rows(g_ref.at[d], 0, GPAD)
        _zero_rows(g_ref.at[d], GPAD + L, GPAD)
    g_ref[1, pl.ds(GPAD, L), :] = src
    g_ref[0, pl.ds(GPAD, L), :] = jnp.where(col != 0, g_ref[1, pl.ds(GPAD - 1, L), :], 0.0)
    g_ref[2, pl.ds(GPAD, L), :] = jnp.where(col != GRID_W - 1, g_ref[1, pl.ds(GPAD + 1, L), :], 0.0)


def ffn_gate_fwd(val, gate, cw, cb_, name, hosted=None):
    L = val.shape[0]
    nb = FH // CB

    def body(val_ref, gate_ref, w_ref, b_ref, o_ref, s_ref, vds_ref, g_ref):
        wv = [w_ref[pl.ds(t, 1), :] for t in range(9)]
        bv = b_ref[...]
        _grid_copies(g_ref, gate_ref[...], L)

        def tile(i, carry):
            r = i * RT
            acc = jnp.broadcast_to(bv, (RT, CB))
            for dr in range(3):
                for dc in range(3):
                    acc = acc + g_ref[dc, _rows(r, GPAD + (dr - 1) * GRID_W), :] * wv[3 * dr + dc]
            sg = _sigmoid(acc)
            s = acc * sg
            v = val_ref[_rows(r), :]
            o_ref[_rows(r), :] = (s * v).astype(bf16)
            s_ref[_rows(r), :] = s
            vds_ref[_rows(r), :] = v * (sg * (1.0 + acc * (1.0 - sg)))
            return carry

        lax.fori_loop(0, L // RT, tile, 0, unroll=2)

    cblk = pl.BlockSpec((L, CB), lambda j: (0, j))
    (out, s_, vds), extra = _host_call(
        body, (nb,), [cblk, cblk, pl.BlockSpec((9, CB), lambda j: (0, j)), pl.BlockSpec((1, CB), lambda j: (0, j))],
        [cblk, cblk, cblk], [S((L, FH), bf16), S((L, FH), f32), S((L, FH), f32)],
        [pltpu.VMEM((3, L + 2 * GPAD, CB), f32)], ("parallel",), name, (val, gate, cw, cb_), hosted)
    return out, s_, vds, extra


def ffn_gate_bwd(gate, s_, vds, cw, dact, name):
    L = gate.shape[0]
    nb = FH // CB

    def body(gate_ref, s_ref, vds_ref, w_ref, da_ref, dval_ref, dgate_ref, dw_ref, db_ref, g_ref, d_ref):
        wv = [w_ref[pl.ds(t, 1), :] for t in range(9)]
        _grid_copies(g_ref, gate_ref[...], L)

        def tile1(i, carry):
            r = i * RTB
            da = da_ref[_rowsb(r), :].astype(f32)
            dval_ref[_rowsb(r), :] = (da * s_ref[_rowsb(r), :]).astype(bf16)
            dpre = da * vds_ref[_rowsb(r), :]
            d_ref[_rowsb(r), :] = dpre
            new = [carry[t] + _fold8(dpre * g_ref[t % 3, _rowsb(r, GPAD + (t // 3 - 1) * GRID_W), :]) for t in range(9)]
            new.append(carry[9] + _fold8(dpre))
            return tuple(new)

        acc8 = lax.fori_loop(0, L // RTB, tile1, tuple(jnp.zeros((8, CB), f32) for _ in range(10)), unroll=2)
        for t in range(9):
            dw_ref[pl.ds(t, 1), :] = jnp.sum(acc8[t], axis=0, keepdims=True)
        db_ref[...] = jnp.sum(acc8[9], axis=0, keepdims=True)
        _grid_copies(g_ref, d_ref[...], L)

        def tile2(i, carry):
            r = i * RTB
            dg = jnp.zeros((RTB, CB), f32)
            for dr in range(3):
                for dc in range(3):
                    dg = dg + g_ref[2 - dc, _rowsb(r, GPAD - (dr - 1) * GRID_W), :] * wv[3 * dr + dc]
            dgate_ref[_rowsb(r), :] = dg.astype(bf16)
            return carry

        lax.fori_loop(0, L // RTB, tile2, 0, unroll=4)

    cblk = pl.BlockSpec((L, CB), lambda j: (0, j))
    return pl.pallas_call(
        body, grid=(nb,),
        in_specs=[cblk, cblk, cblk, pl.BlockSpec((9, CB), lambda j: (0, j)), cblk],
        out_specs=[cblk, cblk, pl.BlockSpec((9, CB), lambda j: (0, j)), pl.BlockSpec((1, CB), lambda j: (0, j))],
        out_shape=[S((L, FH), bf16), S((L, FH), bf16), S((9, FH), f32), S((1, FH), f32)],
        scratch_shapes=[pltpu.VMEM((3, L + 2 * GPAD, CB), f32), pltpu.VMEM((L, CB), f32)],
        compiler_params=_cparams("parallel"), name=name)(gate, s_, vds, cw, dact)


CONF_K = 31
CHALF = CONF_K // 2
CPAD = 16


def _shift_copies8(c_ref, base_ref, L):
    n = L + 2 * CPAD - 8
    for b_ in range(8):
        c_ref[b_, pl.ds(0, n), :] = base_ref[pl.ds(b_, n), :]


def _tap_ab(o):
    return o % 8, o - o % 8


def conf_glu_conv_fwd(pa, pg, b1, wdw, bdw, name, hosted=None):
    L = pa.shape[0]
    nb = D // CB

    def body(pa_ref, pg_ref, ba_ref, bg_ref, w_ref, bdw_ref, o_ref, base_ref, c_ref):
        _zero_rows(base_ref, 0, CPAD)
        _zero_rows(base_ref, CPAD + L, CPAD)
        base_ref[pl.ds(CPAD, L), :] = (pa_ref[...] + ba_ref[...]) * _sigmoid(pg_ref[...] + bg_ref[...])
        _shift_copies8(c_ref, base_ref, L)
        bv = bdw_ref[...]

        def tile(i, carry):
            r = i * RT
            acc = jnp.broadcast_to(bv, (RT, CB))
            for k in range(CONF_K):
                b_, a8 = _tap_ab(k - CHALF)
                acc = acc + c_ref[b_, _rows(r, CPAD + a8), :] * w_ref[pl.ds(k, 1), :]
            o_ref[_rows(r), :] = acc
            return carry

        lax.fori_loop(0, L // RT, tile, 0, unroll=2)

    cblk = pl.BlockSpec((L, CB), lambda j: (0, j))
    rblk = pl.BlockSpec((1, CB), lambda j: (0, j))
    rgblk = pl.BlockSpec((1, CB), lambda j: (0, nb + j))
    (out,), extra = _host_call(
        body, (nb,), [cblk, cblk, rblk, rgblk, pl.BlockSpec((CONF_K, CB), lambda j: (0, j)), rblk],
        [cblk], [S((L, D), f32)], [pltpu.VMEM((L + 2 * CPAD, CB), f32), pltpu.VMEM((8, L + 2 * CPAD, CB), f32)],
        ("parallel",), name, (pa, pg, b1, b1, wdw, bdw), hosted)
    return out, extra


def conf_glu_conv_bwd(pa, pg, b1, wdw, dy, name):
    L = pa.shape[0]
    nb = D // CB

    def body(pa_ref, pg_ref, ba_ref, bg_ref, w_ref, dy_ref, dpa_ref, dpg_ref, dba_ref, dbg_ref, dw_ref, dbdw_ref,
             base_ref, c_ref, acc_ref):
        _zero_rows(base_ref, 0, CPAD)
        _zero_rows(base_ref, CPAD + L, CPAD)
        base_ref[pl.ds(CPAD, L), :] = (pa_ref[...] + ba_ref[...]) * _sigmoid(pg_ref[...] + bg_ref[...])
        _shift_copies8(c_ref, base_ref, L)
        acc_ref[...] = jnp.zeros_like(acc_ref)

        def tile1(i, carry):
            r = i * RTB
            dyt = dy_ref[_rowsb(r), :]
            for k in range(CONF_K):
                b_, a8 = _tap_ab(k - CHALF)
                acc_ref[k] += _fold8(dyt * c_ref[b_, _rowsb(r, CPAD + a8), :])
            return carry + _fold8(dyt)

        db8 = lax.fori_loop(0, L // RTB, tile1, jnp.zeros((8, CB), f32), unroll=2)
        dbdw_ref[...] = jnp.sum(db8, axis=0, keepdims=True)
        for k in range(CONF_K):
            dw_ref[pl.ds(k, 1), :] = jnp.sum(acc_ref[k], axis=0, keepdims=True)
        base_ref[pl.ds(CPAD, L), :] = dy_ref[...]
        _shift_copies8(c_ref, base_ref, L)
        ba = ba_ref[...]
        bg = bg_ref[...]

        def tile2(i, carry):
            r = i * RTB
            dglu = jnp.zeros((RTB, CB), f32)
            for k in range(CONF_K):
                b_, a8 = _tap_ab(CHALF - k)
                dglu = dglu + c_ref[b_, _rowsb(r, CPAD + a8), :] * w_ref[pl.ds(k, 1), :]
            a = pa_ref[_rowsb(r), :] + ba
            sg = _sigmoid(pg_ref[_rowsb(r), :] + bg)
            dpa = dglu * sg
            dpg = dglu * a * (sg * (1.0 - sg))
            dpa_ref[_rowsb(r), :] = dpa.astype(bf16)
            dpg_ref[_rowsb(r), :] = dpg.astype(bf16)
            return carry[0] + _fold8(dpa), carry[1] + _fold8(dpg)

        s8 = lax.fori_loop(0, L // RTB, tile2, (jnp.zeros((8, CB), f32), jnp.zeros((8, CB), f32)), unroll=2)
        dba_ref[...] = jnp.sum(s8[0], axis=0, keepdims=True)
        dbg_ref[...] = jnp.sum(s8[1], axis=0, keepdims=True)

    cblk = pl.BlockSpec((L, CB), lambda j: (0, j))
    rblk = pl.BlockSpec((1, CB), lambda j: (0, j))
    rgblk = pl.BlockSpec((1, CB), lambda j: (0, nb + j))
    wblk = pl.BlockSpec((CONF_K, CB), lambda j: (0, j))
    return pl.pallas_call(
        body, grid=(nb,), in_specs=[cblk, cblk, rblk, rgblk, wblk, cblk],
        out_specs=[cblk, cblk, rblk, rblk, wblk, rblk],
        out_shape=[S((L, D), bf16), S((L, D), bf16), S((1, D), f32), S((1, D), f32), S((CONF_K, D), f32), S((1, D), f32)],
        scratch_shapes=[pltpu.VMEM((L + 2 * CPAD, CB), f32), pltpu.VMEM((8, L + 2 * CPAD, CB), f32),
                        pltpu.VMEM((CONF_K, 8, CB), f32)],
        compiler_params=_cparams("parallel"), name=name)(pa, pg, b1, b1, wdw, dy)


def _ln_silu_f(x, w, b):
    mu = jnp.mean(x, axis=-1, keepdims=True)
    d = x - mu
    y = d * lax.rsqrt(jnp.mean(d * d, axis=-1, keepdims=True) + EPS) * w + b
    return y * _sigmoid(y)


def ln_silu_fwd(x, w, b, name):
    T = x.shape[0]

    def body(x_ref, w_ref, b_ref, o_ref):
        o_ref[...] = _ln_silu_f(x_ref[...], w_ref[...], b_ref[...]).astype(bf16)

    blk = pl.BlockSpec((TB, D), lambda i: (i, 0))
    row = pl.BlockSpec((1, D), lambda i: (0, 0))
    return pl.pallas_call(body, grid=(T // TB,), in_specs=[blk, row, row], out_specs=blk, out_shape=S((T, D), bf16),
                          compiler_params=_cparams("parallel"), name=name)(x, w, b)


def ln_silu_bwd(x, w, b, ds, name):
    T = x.shape[0]

    def body(x_ref, w_ref, b_ref, ds_ref, dx_ref, dw_ref, db_ref):
        i = pl.program_id(0)
        _, vjp = jax.vjp(_ln_silu_f, x_ref[...], w_ref[...], b_ref[...])
        dx, dw, db = vjp(ds_ref[...].astype(f32))
        dx_ref[...] = dx

        @pl.when(i == 0)
        def _():
            dw_ref[...] = jnp.zeros_like(dw_ref)
            db_ref[...] = jnp.zeros_like(db_ref)

        dw_ref[...] += dw
        db_ref[...] += db

    blk = pl.BlockSpec((TB, D), lambda i: (i, 0))
    row = pl.BlockSpec((1, D), lambda i: (0, 0))
    return pl.pallas_call(body, grid=(T // TB,), in_specs=[blk, row, row, blk], out_specs=[blk, row, row],
                          out_shape=[S((T, D), f32), S((1, D), f32), S((1, D), f32)],
                          compiler_params=_cparams("arbitrary"), name=name)(x, w, b, ds)


def _mxu(a, b, dims):
    return lax.dot_general(a.astype(bf16), b.astype(bf16), (dims, ((), ())), preferred_element_type=f32)


def _nn(a, b):
    return _mxu(a, b, ((1,), (0,)))


def _nt(a, b):
    return _mxu(a, b, ((1,), (1,)))


def _tn(a, b):
    return _mxu(a, b, ((0,), (0,)))


@jax.custom_vjp
def _dot_nn(a, b):
    return _nn(a, b)


@jax.custom_vjp
def _dot_nt(a, b):
    return _nt(a, b)


@jax.custom_vjp
def _dot_tn(a, b):
    return _tn(a, b)


_dot_nn.defvjp(lambda a, b: (_nn(a, b), (a, b)), lambda res, g: (_nt(g, res[1]), _tn(res[0], g)))
_dot_nt.defvjp(lambda a, b: (_nt(a, b), (a, b)), lambda res, g: (_nn(g, res[1]), _tn(g, res[0])))
_dot_tn.defvjp(lambda a, b: (_tn(a, b), (a, b)), lambda res, g: (_nt(res[1], g), _nn(res[0], g)))


def _exact_dot(a, b, dims, split_first):
    v = a if split_first else b
    p1 = v.astype(bf16)
    r1 = v - p1.astype(f32)
    p2 = r1.astype(bf16)
    p3 = (r1 - p2.astype(f32)).astype(bf16)
    out = None
    for p in (p1, p2, p3):
        lhs, rhs = (p, b.astype(bf16)) if split_first else (a.astype(bf16), p)
        t = lax.dot_general(lhs, rhs, (dims, ((), ())), preferred_element_type=f32)
        out = t if out is None else out + t
    return out


@jax.custom_vjp
def _masked_sum_cols(mf, a):
    return _exact_dot(mf, a, ((1,), (0,)), False)


@jax.custom_vjp
def _masked_sum_rows(mf, a):
    return _exact_dot(a, mf, ((1,), (1,)), True)


_masked_sum_cols.defvjp(lambda mf, a: (_exact_dot(mf, a, ((1,), (0,)), False), mf),
                        lambda mf, g: (jnp.zeros_like(mf), _exact_dot(mf, g, ((0,), (0,)), False)))
_masked_sum_rows.defvjp(lambda mf, a: (_exact_dot(a, mf, ((1,), (1,)), True), mf),
                        lambda mf, g: (jnp.zeros_like(mf), _exact_dot(g, mf, ((1,), (0,)), True)))


def _masked_sum(mf, a, rows):
    return _masked_sum_rows(mf, a) if rows else _masked_sum_cols(mf, a)


def _lanes_to_rows(v):
    r = lax.broadcasted_iota(jnp.int32, (GW, GW), 0)
    c = lax.broadcasted_iota(jnp.int32, (GW, GW), 1)
    return jnp.sum(jnp.where(r == c, jnp.broadcast_to(v, (GW, GW)), 0.0), axis=1, keepdims=True)


def _ssd_chunk(x, B, C, dtc, dtr, bc, br, alc, alr, s_in, is_fwd):
    row = lax.broadcasted_iota(jnp.int32, (Q, Q), 0)
    col = lax.broadcasted_iota(jnp.int32, (Q, Q), 1)
    sgn = jnp.where(is_fwd, 1, -1).astype(jnp.int32)
    mask = (row - col) * sgn >= 0
    mf = mask.astype(f32)
    lane_head = lax.broadcasted_iota(jnp.int32, (1, GW), 1) // P

    def spread(v):
        out = jnp.zeros((v.shape[0], GW), f32)
        for r in range(HPG):
            out = jnp.where(lane_head == r, v[:, r:r + 1], out)
        return out

    dt_c = _softplus(dtc + bc)
    dt_r = _softplus(dtr + br)
    a_c = dt_c * (-jnp.exp(alc))
    a_r = dt_r * (-jnp.exp(alr))
    acum_c = _masked_sum(mf, a_c, False)
    acum_r = _masked_sum(mf, a_r, True)
    tot_c = jnp.sum(a_c, axis=0, keepdims=True)
    dt_e = spread(dt_c)
    acum_e = spread(acum_c)
    tot_e = spread(tot_c)
    xdt = x * dt_e
    cb = _dot_nt(C, B)
    scores, xs = [], []
    for r in range(HPG):
        seg = acum_c[:, r:r + 1] - acum_r[r:r + 1, :]
        scores.append(cb * jnp.exp(jnp.where(mask, seg, -jnp.inf)))
        xs.append(jnp.where(lane_head == r, xdt, 0.0))
    y = _dot_nn(jnp.concatenate(scores, axis=1), jnp.concatenate(xs, axis=0))
    y = y + _dot_nt(C, s_in) * jnp.exp(acum_e)
    xe = xdt * jnp.exp(tot_e - acum_e)
    s_out = _lanes_to_rows(jnp.exp(tot_e)) * s_in + _dot_tn(xe, B)
    return y, s_out


def _chunk_index(d, t, nctx, nc):
    bwd = jnp.where(t < nctx, nctx - 1 - t, nc - 1 - (t - nctx))
    return jnp.where(d == 0, t, bwd)


def _ssd_in_specs(ci):
    small_c = pl.BlockSpec((None, G, 1, HPG), lambda d, t: (d, 0, 0, 0))
    small_r = pl.BlockSpec((None, G, HPG, 1), lambda d, t: (d, 0, 0, 0))
    return [
        pl.BlockSpec((Q, CONVD), lambda d, t: (ci(d, t), 0)),
        pl.BlockSpec((None, G, Q, HPG), lambda d, t: (d, 0, ci(d, t), 0)),
        pl.BlockSpec((None, G, HPG, Q), lambda d, t: (d, 0, 0, ci(d, t))),
        small_c, small_r, small_c, small_r,
    ]


def _group_cols(g):
    return pl.ds(g * GW, GW), pl.ds(DI + g * N, N), pl.ds(DI + G * N + g * N, N)


def ssd_scan_fwd(xbc, dtc, dtr, bc, br, alc, alr, nctx, name, hosted=None):
    T = xbc.shape[0]
    nc = T // Q

    def body(xbc_ref, dtc_ref, dtr_ref, bc_ref, br_ref, alc_ref, alr_ref, y_ref, sin_ref, st_ref):
        d = pl.program_id(0)
        t = pl.program_id(1)

        @pl.when(t == 0)
        def _():
            st_ref[...] = jnp.zeros_like(st_ref)

        for g in range(G):
            xs, bs, cs = _group_cols(g)
            s_in = st_ref[g]
            sin_ref[g] = s_in
            y, s_out = _ssd_chunk(xbc_ref[:, xs], xbc_ref[:, bs], xbc_ref[:, cs], dtc_ref[g], dtr_ref[g], bc_ref[g], br_ref[g],
                                  alc_ref[g], alr_ref[g], s_in, d == 0)
            y_ref[:, xs] = y
            st_ref[g] = s_out

    ci = lambda d, t: _chunk_index(d, t, nctx, nc)
    out_specs = [
        pl.BlockSpec((None, Q, DI), lambda d, t: (d, ci(d, t), 0)),
        pl.BlockSpec((None, None, G, GW, N), lambda d, t: (d, ci(d, t), 0, 0, 0)),
    ]
    return _host_call(
        body, (2, nc), _ssd_in_specs(ci), out_specs, [S((2, T, DI), f32), S((2, nc, G, GW, N), f32)],
        [pltpu.VMEM((G, GW, N), f32)], ("arbitrary", "arbitrary"), name, (xbc, dtc, dtr, bc, br, alc, alr), hosted)


def ssd_scan_bwd(xbc, dtc, dtr, bc, br, alc, alr, s_in_all, dy, nctx, name, hosted=None):
    T = xbc.shape[0]
    nc = T // Q

    def body(xbc_ref, dtc_ref, dtr_ref, bc_ref, br_ref, alc_ref, alr_ref, sin_ref, dy_ref,
             dxbc_ref, ddtc_ref, ddtr_ref, dbc_ref, dbr_ref, dalc_ref, dalr_ref, ds_ref):
        d = pl.program_id(0)
        t = pl.program_id(1)

        @pl.when(t == 0)
        def _():
            ds_ref[...] = jnp.zeros_like(ds_ref)
            dbc_ref[...] = jnp.zeros_like(dbc_ref)
            dbr_ref[...] = jnp.zeros_like(dbr_ref)
            dalc_ref[...] = jnp.zeros_like(dalc_ref)
            dalr_ref[...] = jnp.zeros_like(dalr_ref)

        f = functools.partial(_ssd_chunk, is_fwd=(d == 0))
        for g in range(G):
            xs, bs, cs = _group_cols(g)
            _, vjp = jax.vjp(f, xbc_ref[:, xs], xbc_ref[:, bs], xbc_ref[:, cs], dtc_ref[g], dtr_ref[g], bc_ref[g], br_ref[g],
                             alc_ref[g], alr_ref[g], sin_ref[g])
            dx, dB, dC, ddtc, ddtr, dbc, dbr, dalc, dalr, ds = vjp((dy_ref[:, xs], ds_ref[g]))
            dxbc_ref[:, xs] = dx
            dxbc_ref[:, bs] = dB
            dxbc_ref[:, cs] = dC
            ddtc_ref[g] = ddtc
            ddtr_ref[g] = ddtr
            dbc_ref[g] += dbc
            dbr_ref[g] += dbr
            dalc_ref[g] += dalc
            dalr_ref[g] += dalr
            ds_ref[g] = ds

    ci = lambda d, t: _chunk_index(d, nc - 1 - t, nctx, nc)
    in_specs = _ssd_in_specs(ci) + [
        pl.BlockSpec((None, None, G, GW, N), lambda d, t: (d, ci(d, t), 0, 0, 0)),
        pl.BlockSpec((Q, DI), lambda d, t: (ci(d, t), 0)),
    ]
    small_c = pl.BlockSpec((None, G, 1, HPG), lambda d, t: (d, 0, 0, 0))
    small_r = pl.BlockSpec((None, G, HPG, 1), lambda d, t: (d, 0, 0, 0))
    out_specs = [
        pl.BlockSpec((None, Q, CONVD), lambda d, t: (d, ci(d, t), 0)),
        pl.BlockSpec((None, G, Q, HPG), lambda d, t: (d, 0, ci(d, t), 0)),
        pl.BlockSpec((None, G, HPG, Q), lambda d, t: (d, 0, 0, ci(d, t))),
        small_c, small_r, small_c, small_r,
    ]
    out_shape = [S((2, T, CONVD), f32), S((2, G, T, HPG), f32), S((2, G, HPG, T), f32),
                 S((2, G, 1, HPG), f32), S((2, G, HPG, 1), f32), S((2, G, 1, HPG), f32), S((2, G, HPG, 1), f32)]
    return _host_call(body, (2, nc), in_specs, out_specs, out_shape, [pltpu.VMEM((G, GW, N), f32)],
                      ("arbitrary", "arbitrary"), name, (xbc, dtc, dtr, bc, br, alc, alr, s_in_all, dy), hosted)


GTB = 128


def _gate_norm_f(yf, yb, x, z, dexp, w):
    y = (yf + yb + dexp * x) * (z * _sigmoid(z))
    return y * lax.rsqrt(jnp.mean(y * y, axis=-1, keepdims=True) + EPS) * w


def ssd_gate_fwd(y2, xbc, proj, dexp, w, nctxb, name):
    T = xbc.shape[0]
    L = T - nctxb * GTB

    def body(yf_ref, yb_ref, x_ref, z_ref, d_ref, w_ref, o_ref):
        o_ref[...] = _gate_norm_f(yf_ref[...], yb_ref[...], x_ref[...], z_ref[...], d_ref[...], w_ref[...]).astype(bf16)

    wide = pl.BlockSpec((GTB, DI), lambda i: (i + nctxb, 0))
    row = pl.BlockSpec((1, DI), lambda i: (0, 0))
    return pl.pallas_call(
        body, grid=(L // GTB,),
        in_specs=[pl.BlockSpec((None, GTB, DI), lambda i: (0, i + nctxb, 0)),
                  pl.BlockSpec((None, GTB, DI), lambda i: (1, i + nctxb, 0)), wide, wide, row, row],
        out_specs=pl.BlockSpec((GTB, DI), lambda i: (i, 0)), out_shape=S((L, DI), bf16),
        compiler_params=_cparams("parallel"), name=name)(y2, y2, xbc, proj, dexp, w)


def ssd_gate_bwd(y2, xbc, proj, dexp, w, dyn, nctxb, name, hosted=None):
    T = xbc.shape[0]
    nb = T // GTB

    def body(yf_ref, yb_ref, x_ref, z_ref, d_ref, w_ref, dyn_ref, dy_ref, dz_ref, dd_ref, dw_ref):
        i = pl.program_id(0)

        @pl.when(i == 0)
        def _():
            dd_ref[...] = jnp.zeros_like(dd_ref)
            dw_ref[...] = jnp.zeros_like(dw_ref)

        @pl.when(i < nctxb)
        def _():
            dy_ref[...] = jnp.zeros_like(dy_ref)
            dz_ref[...] = jnp.zeros_like(dz_ref)

        @pl.when(i >= nctxb)
        def _():
            _, vjp = jax.vjp(_gate_norm_f, yf_ref[...], yb_ref[...], x_ref[...], z_ref[...], d_ref[...], w_ref[...])
            dyf, _, _, dz, dd, dw = vjp(dyn_ref[...].astype(f32))
            dy_ref[...] = dyf
            dz_ref[...] = dz.astype(bf16)
            fold = (lax.broadcasted_iota(jnp.int32, (DI, 128), 0) // P == lax.broadcasted_iota(jnp.int32, (DI, 128), 1))
            dd_ref[...] += jnp.dot(dd, fold.astype(f32), precision=HI, preferred_element_type=f32)
            dw_ref[...] += dw

    wide = pl.BlockSpec((GTB, DI), lambda i: (i, 0))
    row = pl.BlockSpec((1, DI), lambda i: (0, 0))
    hrow = pl.BlockSpec((1, 128), lambda i: (0, 0))
    return _host_call(
        body, (nb,),
        [pl.BlockSpec((None, GTB, DI), lambda i: (0, i, 0)), pl.BlockSpec((None, GTB, DI), lambda i: (1, i, 0)),
         wide, wide, row, row, pl.BlockSpec((GTB, DI), lambda i: (jnp.maximum(i - nctxb, 0), 0))],
        [wide, wide, hrow, row],
        [S((T, DI), f32), S((T, DI), bf16), S((1, 128), f32), S((1, DI), f32)],
        [], ("arbitrary",), name, (y2, y2, xbc, proj, dexp, w, dyn), hosted)


CROWS = 2 * N_DEV


def mod_fwd(c16, modw, name):
    nl, _, cols = modw.shape

    def body(c_ref, w_ref, o_ref):
        cv = c_ref[...]
        s = cv * _sigmoid(cv)
        for l in range(nl):
            o_ref[l] = jnp.dot(s, w_ref[l], precision=HI, preferred_element_type=f32)

    return pl.pallas_call(body, in_specs=[VMEM, VMEM], out_specs=VMEM, out_shape=S((nl, CROWS, cols), f32),
                          compiler_params=pltpu.CompilerParams(vmem_limit_bytes=VMEM_LIMIT_BYTES), name=name)(c16, modw)


def mod_bwd(c16, modw, dm_sh, dm_all, name):
    nl, _, cols = modw.shape

    def body(c_ref, w_ref, dm_ref, dmall_ref, dw_ref, dc_ref, db_ref):
        cv = c_ref[...]
        sg = _sigmoid(cv)
        s = cv * sg
        ds_dc = sg * (1.0 + cv * (1.0 - sg))
        is_ctx = lax.broadcasted_iota(jnp.int32, (CROWS, D), 0) >= N_DEV
        dc = jnp.zeros((1, D), f32)
        for l in range(nl):
            dm = dm_ref[l]
            dw_ref[l] = lax.dot_general(s, dm, (((0,), (0,)), ((), ())), precision=HI, preferred_element_type=f32)
            dsv = lax.dot_general(dm, w_ref[l], (((1,), (1,)), ((), ())), precision=HI, preferred_element_type=f32)
            dc = dc + jnp.sum(jnp.where(is_ctx, dsv * ds_dc, 0.0), axis=0, keepdims=True)
            db_ref[pl.ds(l, 1), :] = jnp.sum(dmall_ref[l], axis=0, keepdims=True)
        dc_ref[...] = dc

    return pl.pallas_call(
        body, in_specs=[VMEM, VMEM, VMEM, VMEM], out_specs=[VMEM, VMEM, VMEM],
        out_shape=[S(modw.shape, f32), S((1, D), f32), S((nl, 6 * D), f32)],
        compiler_params=pltpu.CompilerParams(vmem_limit_bytes=VMEM_LIMIT_BYTES), name=name)(c16, modw, dm_sh, dm_all)


def adamw(w, g, m, v, name):
    R, C = w.shape
    rb = R if R <= 512 else max(r_ for r_ in range(8, 513, 8) if R % r_ == 0)
    bc1 = 1.0 - ADAM_B1 ** ADAM_STEP
    bc2 = 1.0 - ADAM_B2 ** ADAM_STEP

    def body(w_ref, g_ref, m_ref, v_ref, d_ref, nm_ref, nv_ref):
        gv = g_ref[...]
        m_new = ADAM_B1 * m_ref[...] + (1.0 - ADAM_B1) * gv
        v_new = ADAM_B2 * v_ref[...] + (1.0 - ADAM_B2) * (gv * gv)
        m_hat = m_new / bc1
        v_hat = v_new / bc2
        d_ref[...] = -ADAM_LR * (m_hat / (jnp.sqrt(v_hat) + ADAM_EPS) + ADAM_WD * w_ref[...])
        nm_ref[...] = m_new
        nv_ref[...] = v_new

    blk = pl.BlockSpec((rb, C), lambda i: (i, 0))
    return pl.pallas_call(body, grid=(R // rb,), in_specs=[blk] * 4, out_specs=[blk] * 3,
                          out_shape=[S((R, C), f32)] * 3, compiler_params=_cparams("parallel"), name=name)(w, g, m, v)


def _me():
    return lax.axis_index("x"), lax.axis_index("y"), lax.axis_index("c")


def allgather_small(x, name, with_sum=False, after=None):
    r, w = x.shape
    extra = () if after is None else (after,)

    def body(x_ref, *refs):
        refs = refs[len(extra):]
        if with_sum:
            out_ref, sum_ref, send_sems, recv_sems = refs
        else:
            out_ref, send_sems, recv_sems = refs
        mx, my, mc = _me()
        me = 4 * mx + 2 * my + mc
        out_ref[me] = x_ref[...]
        peers = []
        for k in range(1, N_DEV):
            kx, ky, kc = (k >> 2) & 1, (k >> 1) & 1, k & 1
            peers.append((mx + kx - 2 * mx * kx, my + ky - 2 * my * ky, mc + kc - 2 * mc * kc))
        copies = []
        for k, peer in enumerate(peers):
            cp = pltpu.make_async_remote_copy(src_ref=x_ref, dst_ref=out_ref.at[me], send_sem=send_sems.at[k],
                                              recv_sem=recv_sems.at[k], device_id=peer, device_id_type=MESH)
            cp.start()
            copies.append(cp)
        for k, (px, py, pc) in enumerate(peers):
            pltpu.make_async_remote_copy(src_ref=x_ref, dst_ref=out_ref.at[4 * px + 2 * py + pc], send_sem=send_sems.at[k],
                                         recv_sem=recv_sems.at[k], device_id=(px, py, pc), device_id_type=MESH).wait_recv()
        for cp in copies:
            cp.wait_send()
        if with_sum:
            acc = out_ref[0]
            for j in range(1, N_DEV):
                acc = acc + out_ref[j]
            sum_ref[...] = acc

    out_shape = [S((N_DEV, r, w), f32)] + ([S((r, w), f32)] if with_sum else [])
    outs = pl.pallas_call(
        body, in_specs=[VMEM] + [ANY] * len(extra), out_specs=[VMEM] * len(out_shape), out_shape=out_shape,
        scratch_shapes=[pltpu.SemaphoreType.DMA((N_DEV - 1,)), pltpu.SemaphoreType.DMA((N_DEV - 1,))],
        compiler_params=pltpu.CompilerParams(vmem_limit_bytes=VMEM_LIMIT_BYTES), name=name)(x, *extra)
    return outs if with_sum else outs[0]


def _tile2d(R, W, max_rows):
    if R <= max_rows:
        return R, W
    fits = [r_ for r_ in range(16, max_rows + 1, 16) if R % r_ == 0]
    return (max(fits), W) if fits else (R, 256)


def add_own(g, r, core, name, twice=False):
    _, _, R, W = g.shape
    rb, wb = _tile2d(R, W, 512)
    nout = 2 if twice else 1

    def body(core_ref, a_ref, b_ref, *o_refs):
        s = (a_ref[...].astype(f32) + b_ref[...].astype(f32)).astype(bf16)
        for o_ref in o_refs:
            o_ref[...] = s

    blk = pl.BlockSpec((None, rb, wb), lambda k, i, j, core_ref: (k, i, j))
    gs = pltpu.PrefetchScalarGridSpec(
        num_scalar_prefetch=1, grid=(4, R // rb, W // wb),
        in_specs=[pl.BlockSpec((None, None, rb, wb), lambda k, i, j, core_ref: (k, core_ref[0], i, j)), blk],
        out_specs=[blk] * nout)
    outs = pl.pallas_call(body, grid_spec=gs, out_shape=[S((4, R, W), bf16)] * nout,
                          compiler_params=_cparams("parallel", "parallel", "parallel"), name=name)(core, g, r)
    return tuple(outs) if twice else outs[0]


HBM_SPEC = pl.BlockSpec(memory_space=pltpu.HBM)
SEM_SPEC = pl.BlockSpec(memory_space=pltpu.SEMAPHORE)


def _chips_copy(p_ref, land_ref, send_sems, recv_sems, a, j):
    x, y, c = _me()
    px, py = [(1 - x, y), (x, 1 - y), (1 - x, 1 - y)][j]
    return pltpu.make_async_remote_copy(src_ref=p_ref.at[2 * px + py], dst_ref=land_ref.at[2 * x + y],
                                        send_sem=send_sems.at[3 * a + j], recv_sem=recv_sems.at[3 * a + j],
                                        device_id=(px, py, c), device_id_type=MESH)


def _chips_wait_copy(p_ref, land_ref, send_sems, recv_sems, a, j):
    x, y, c = _me()
    px, py = [(1 - x, y), (x, 1 - y), (1 - x, 1 - y)][j]
    return pltpu.make_async_remote_copy(src_ref=p_ref.at[2 * px + py], dst_ref=land_ref.at[2 * px + py],
                                        send_sem=send_sems.at[3 * a + j], recv_sem=recv_sems.at[3 * a + j],
                                        device_id=(px, py, c), device_id_type=MESH)


def _xor_peers():
    mx, my, mc = _me()
    peers = []
    for k in range(1, N_DEV):
        kx, ky, kc = (k >> 2) & 1, (k >> 1) & 1, k & 1
        peers.append((mx + kx - 2 * mx * kx, my + ky - 2 * my * ky, mc + kc - 2 * mc * kc))
    return peers


def gather_start(shards, name, after):
    na = len(shards)
    lands = [lax.empty((N_DEV,) + s_.shape, s_.dtype) for s_ in shards]

    def body(*refs):
        x_refs, land_refs = refs[:na], refs[na:2 * na]
        send_sems, recv_sems, local_sems = refs[2 * na + 1:2 * na + 4]
        token = refs[-1]
        mx, my, mc = _me()
        me = 4 * mx + 2 * my + mc
        for a in range(na):
            pltpu.make_async_copy(x_refs[a], land_refs[a].at[me], local_sems.at[a]).start()
            for k, peer in enumerate(_xor_peers()):
                pltpu.make_async_remote_copy(src_ref=x_refs[a], dst_ref=land_refs[a].at[me], send_sem=send_sems.at[7 * a + k],
                                             recv_sem=recv_sems.at[7 * a + k], device_id=peer, device_id_type=MESH).start()
        token[...] = jnp.zeros_like(token)

    arrs = list(shards) + lands
    outs = pl.pallas_call(
        body, name=name, in_specs=[HBM_SPEC] * (2 * na) + [ANY],
        out_shape=[DMA((7 * na,)), DMA((7 * na,)), DMA((na,))] + [pltpu.HBM(t.shape, t.dtype) for t in arrs]
        + [S((8, 128), f32)],
        out_specs=[SEM_SPEC] * 3 + [HBM_SPEC] * (2 * na) + [VMEM],
        input_output_aliases={k: 3 + k for k in range(2 * na)},
        compiler_params=pltpu.CompilerParams(has_side_effects=pltpu.SideEffectType.DATAFLOW_SIDE_EFFECTING),
    )(*[pltpu.with_memory_space_constraint(t, pltpu.HBM) for t in arrs], after)
    return outs[0], outs[1], outs[2], list(outs[3:3 + na]), list(outs[3 + na:3 + 2 * na]), outs[-1]


def gather_wait(send_sems, recv_sems, local_sems, shards, lands, after, name):
    na = len(shards)

    def body(*refs):
        x_refs, land_refs = refs[:na], refs[na:2 * na]
        ssem, rsem, lsem = refs[2 * na:2 * na + 3]
        mx, my, mc = _me()
        me = 4 * mx + 2 * my + mc
        for a in range(na):
            pltpu.make_async_copy(x_refs[a], land_refs[a].at[me], lsem.at[a]).wait()
            for k, (px, py, pc) in enumerate(_xor_peers()):
                cp = pltpu.make_async_remote_copy(src_ref=x_refs[a], dst_ref=land_refs[a].at[4 * px + 2 * py + pc],
                                                  send_sem=ssem.at[7 * a + k], recv_sem=rsem.at[7 * a + k],
                                                  device_id=(px, py, pc), device_id_type=MESH)
                cp.wait_send()
                cp.wait_recv()

    arrs = list(shards) + list(lands)
    outs = pl.pallas_call(
        body, name=name, in_specs=[HBM_SPEC] * (2 * na) + [SEM_SPEC] * 3 + [ANY],
        out_shape=[pltpu.HBM(t.shape, t.dtype) for t in arrs], out_specs=[HBM_SPEC] * (2 * na),
        input_output_aliases={k: k for k in range(2 * na)},
        compiler_params=pltpu.CompilerParams(has_side_effects=pltpu.SideEffectType.DATAFLOW_SIDE_EFFECTING),
    )(*arrs, send_sems, recv_sems, local_sems, after)
    return list(outs[na:])


def chips_start(parts, lands, name):
    na = len(parts)

    def body(*refs):
        p_refs, land_refs = refs[:na], refs[na:2 * na]
        send_sems, recv_sems = refs[2 * na], refs[2 * na + 1]
        token = refs[-1]
        for a in range(na):
            for j in range(3):
                _chips_copy(p_refs[a], land_refs[a], send_sems, recv_sems, a, j).start()
        token[...] = jnp.zeros_like(token)

    arrs = list(parts) + list(lands)
    outs = pl.pallas_call(
        body, name=name, in_specs=[HBM_SPEC] * (2 * na),
        out_shape=[DMA((3 * na,)), DMA((3 * na,))] + [pltpu.HBM(t.shape, t.dtype) for t in arrs] + [S((8, 128), f32)],
        out_specs=[SEM_SPEC, SEM_SPEC] + [HBM_SPEC] * (2 * na) + [VMEM],
        input_output_aliases={k: 2 + k for k in range(2 * na)},
        compiler_params=pltpu.CompilerParams(has_side_effects=pltpu.SideEffectType.DATAFLOW_SIDE_EFFECTING),
    )(*[pltpu.with_memory_space_constraint(t, pltpu.HBM) for t in arrs])
    return outs[0], outs[1], list(outs[2:2 + na]), list(outs[2 + na:2 + 2 * na]), outs[-1]


def chips_wait(send_sems, recv_sems, parts, lands, after, name):
    na = len(parts)

    def body(*refs):
        p_refs, land_refs = refs[:na], refs[na:2 * na]
        ssem, rsem = refs[2 * na], refs[2 * na + 1]
        for a in range(na):
            for j in range(3):
                cp = _chips_wait_copy(p_refs[a], land_refs[a], ssem, rsem, a, j)
                cp.wait_send()
                cp.wait_recv()

    arrs = list(parts) + list(lands)
    outs = pl.pallas_call(
        body, name=name, in_specs=[HBM_SPEC] * (2 * na) + [SEM_SPEC, SEM_SPEC, ANY],
        out_shape=[pltpu.HBM(t.shape, t.dtype) for t in arrs], out_specs=[HBM_SPEC] * (2 * na),
        input_output_aliases={k: k for k in range(2 * na)},
        compiler_params=pltpu.CompilerParams(has_side_effects=pltpu.SideEffectType.DATAFLOW_SIDE_EFFECTING),
    )(*arrs, send_sems, recv_sems, after)
    return list(outs[na:])


def sum_adamw(recv, w, m, v, layer, name, into=None, after=None):
    _, R, W = recv.shape
    rb, wb = _tile2d(R, W, 256)
    bc1 = 1.0 - ADAM_B1 ** ADAM_STEP
    bc2 = 1.0 - ADAM_B2 ** ADAM_STEP
    n_into = 0 if into is None else 4
    extra = () if after is None else (after,)

    def body(r_ref, w_ref, m_ref, v_ref, *refs):
        g_ref, d_ref, nm_ref, nv_ref = refs[n_into + len(extra):]
        gv = r_ref[0].astype(f32)
        for k in range(1, 4):
            gv = gv + r_ref[k].astype(f32)
        m_new = ADAM_B1 * m_ref[...] + (1.0 - ADAM_B1) * gv
        v_new = ADAM_B2 * v_ref[...] + (1.0 - ADAM_B2) * (gv * gv)
        g_ref[...] = gv
        d_ref[...] = -ADAM_LR * ((m_new / bc1) / (jnp.sqrt(v_new / bc2) + ADAM_EPS) + ADAM_WD * w_ref[...])
        nm_ref[...] = m_new
        nv_ref[...] = v_new

    if layer is None:
        wblk = pl.BlockSpec((rb, wb), lambda i, j: (i, j))
        oshape = S((R, W), f32)
    else:
        wblk = pl.BlockSpec((None, rb, wb), lambda i, j: (layer, i, j))
        oshape = S(w.shape, f32)
    return pl.pallas_call(
        body, grid=(R // rb, W // wb),
        in_specs=[pl.BlockSpec((4, rb, wb), lambda i, j: (0, i, j)), wblk, wblk, wblk] + [ANY] * (n_into + len(extra)),
        out_specs=[wblk] * 4, out_shape=[oshape] * 4, input_output_aliases={4 + k: k for k in range(n_into)},
        compiler_params=_cparams("parallel", "parallel"), name=name)(recv, w, m, v, *(into or ()), *extra)


def sum_rows(a, name):
    K, R, W = a.shape
    rb = _pick(R, (512, 256, 128, 64, 32, 16))

    def body(a_ref, o_ref):
        acc = a_ref[0].astype(f32)
        for k in range(1, K):
            acc = acc + a_ref[k].astype(f32)
        o_ref[...] = acc

    return pl.pallas_call(body, grid=(R // rb,), in_specs=[pl.BlockSpec((K, rb, W), lambda i: (0, i, 0))],
                          out_specs=pl.BlockSpec((rb, W), lambda i: (i, 0)), out_shape=S((R, W), f32),
                          compiler_params=_cparams("parallel"), name=name)(a)


DMA = pltpu.SemaphoreType.DMA


class GatherExchange:
    def __init__(self, arrays):
        self.arrays = list(arrays)
        self.na = len(self.arrays)
        self.out_shape = [S((N_DEV,) + a.shape, a.dtype) for a in self.arrays]
        self.scratch = [DMA((7 * self.na,)), DMA((7 * self.na,)), DMA((self.na,))]

    def ops(self, x_refs, out_refs, sems):
        send_sems, recv_sems, local_sems = sems
        na = self.na
        x, y, c = _me()
        me, sibling = (x, y, c), (x, y, 1 - c)
        chips = [(1 - x, y), (x, 1 - y), (1 - x, 1 - y)]

        def rows(a, px, py, pc):
            return out_refs[a].at[4 * px + 2 * py + pc]

        def copy(a, k, block, to, src=None):
            return pltpu.make_async_remote_copy(
                src_ref=rows(a, *block) if src is None else src, dst_ref=rows(a, *block),
                send_sem=send_sems.at[7 * a + k], recv_sem=recv_sems.at[7 * a + k], device_id=to, device_id_type=MESH)

        def local(a):
            return pltpu.make_async_copy(x_refs[a], rows(a, *me), local_sems.at[a])

        def first(a):
            return [copy(a, 0, me, sibling, src=x_refs[a])] + [copy(a, 1 + j, me, (*chip, c), src=x_refs[a])
                                                                for j, chip in enumerate(chips)]

        def start():
            for a in range(na):
                local(a).start()
                for cp in first(a):
                    cp.start()

        def mid():
            for a in range(na):
                for j, chip in enumerate(chips):
                    copy(a, 1 + j, (*chip, c), me).wait_recv()
                    copy(a, 4 + j, (*chip, c), sibling).start()

        def finish():
            for a in range(na):
                copy(a, 0, sibling, me).wait_recv()
                for j, chip in enumerate(chips):
                    copy(a, 4 + j, (*chip, 1 - c), me).wait_recv()
                for cp in first(a) + [copy(a, 4 + j, (*chip, c), sibling) for j, chip in enumerate(chips)]:
                    cp.wait_send()
                local(a).wait()

        return start, mid, finish


class SiblingExchange:
    def __init__(self, arrays):
        self.arrays = list(arrays)
        self.na = len(self.arrays)
        self.out_shape = [S((4,) + g.shape[2:], g.dtype) for g in self.arrays]
        self.scratch = [DMA((self.na,)), DMA((self.na,))]

    def ops(self, g_refs, out_refs, sems):
        send_sems, recv_sems = sems
        x, y, c = _me()

        def copy(a):
            return pltpu.make_async_remote_copy(src_ref=g_refs[a].at[:, 1 - c], dst_ref=out_refs[a],
                                                send_sem=send_sems.at[a], recv_sem=recv_sems.at[a],
                                                device_id=(x, y, 1 - c), device_id_type=MESH)

        def start():
            for a in range(self.na):
                copy(a).start()

        def finish():
            for a in range(self.na):
                copy(a).wait()

        return start, None, finish


class ChipsExchange:
    def __init__(self, arrays):
        self.arrays = list(arrays)
        self.na = len(self.arrays)
        self.out_shape = [S(p.shape, p.dtype) for p in self.arrays]
        self.scratch = [DMA((3 * self.na,)), DMA((3 * self.na,)), DMA((self.na,))]

    def ops(self, p_refs, out_refs, sems):
        send_sems, recv_sems, local_sems = sems
        x, y, c = _me()
        mine = 2 * x + y
        chips = [(1 - x, y), (x, 1 - y), (1 - x, 1 - y)]

        def local(a):
            return pltpu.make_async_copy(p_refs[a].at[mine], out_refs[a].at[mine], local_sems.at[a])

        def send(a, j):
            px, py = chips[j]
            return pltpu.make_async_remote_copy(src_ref=p_refs[a].at[2 * px + py], dst_ref=out_refs[a].at[mine],
                                                send_sem=send_sems.at[3 * a + j], recv_sem=recv_sems.at[3 * a + j],
                                                device_id=(px, py, c), device_id_type=MESH)

        def recv(a, j):
            px, py = chips[j]
            return pltpu.make_async_remote_copy(src_ref=p_refs[a].at[mine], dst_ref=out_refs[a].at[2 * px + py],
                                                send_sem=send_sems.at[3 * a + j], recv_sem=recv_sems.at[3 * a + j],
                                                device_id=(px, py, c), device_id_type=MESH)

        def start():
            for a in range(self.na):
                local(a).start()
                for j in range(3):
                    send(a, j).start()

        def finish():
            for a in range(self.na):
                for j in range(3):
                    recv(a, j).wait_recv()
                for j in range(3):
                    send(a, j).wait_send()
                local(a).wait()

        return start, None, finish


def exchange(ex, name):
    na = ex.na

    def body(*refs):
        start, mid, finish = ex.ops(refs[:na], refs[na:2 * na], refs[2 * na:])
        start()
        if mid is not None:
            mid()
        finish()

    return pl.pallas_call(body, in_specs=[ANY] * na, out_specs=[ANY] * na, out_shape=ex.out_shape,
                          scratch_shapes=ex.scratch, name=name)(*ex.arrays)


def _host_call(body, grid, in_specs, out_specs, out_shape, scratch_shapes, sem, name, args, hosted):
    if hosted is None:
        res = pl.pallas_call(body, grid=grid, in_specs=in_specs, out_specs=out_specs, out_shape=out_shape,
                             scratch_shapes=scratch_shapes, compiler_params=_cparams(*sem), name=name)(*args)
        return res, None
    n_in, n_out, n_sc, na = len(in_specs), len(out_shape), len(scratch_shapes), hosted.na
    nsteps = 1
    for g_ in grid:
        nsteps *= g_
    mid_step = (3 * nsteps) // 4
    i1 = n_in + na
    i2 = i1 + n_out
    i3 = i2 + na
    i4 = i3 + n_sc

    def wrapped(*refs):
        step = pl.program_id(0)
        for ax in range(1, len(grid)):
            step = step * grid[ax] + pl.program_id(ax)
        start, mid, finish = hosted.ops(refs[n_in:i1], refs[i2:i3], refs[i4:])
        pl.when(step == 0)(start)
        if mid is not None:
            pl.when(step == mid_step)(mid)
        body(*refs[:n_in], *refs[i1:i2], *refs[i3:i4])
        pl.when(step == nsteps - 1)(finish)

    res = pl.pallas_call(
        wrapped, grid=grid, in_specs=list(in_specs) + [ANY] * na, out_specs=list(out_specs) + [ANY] * na,
        out_shape=list(out_shape) + hosted.out_shape, scratch_shapes=list(scratch_shapes) + hosted.scratch,
        compiler_params=_cparams(*(("arbitrary",) * len(grid))), name=name)(*args, *hosted.arrays)
    return res[:n_out], res[n_out:]


PACK_ALIGN = 16 * PACK_W


def _pad_to(v, mult):
    n = v.shape[-1]
    extra = (-n) % mult
    if extra == 0:
        return v
    return jnp.concatenate([v, jnp.zeros(v.shape[:-1] + (extra,), v.dtype)], axis=-1)


def _f32_as_bf16_pairs(v):
    return lax.bitcast_convert_type(v.reshape(-1), bf16).reshape(-1)


def _bf16_pairs_as_f32(v):
    return lax.bitcast_convert_type(v.reshape(v.shape[:-1] + (v.shape[-1] // 2, 2)), f32)


def _col_shards(gw):
    lead = gw.shape[:-1]
    n = gw.shape[-1] // N_DEV
    t = gw.reshape(lead + (N_DEV, n))
    t = jnp.moveaxis(t, -2, 0)
    return t.reshape(N_DEV, -1)


def kernel(x, c, ctx, c_ctx, mod_w, mod_b, norm1_w, norm2_w, ssd_w_in, ssd_conv_w, ssd_conv_b, ssd_dt_bias, ssd_a_log, ssd_d, ssd_norm_w, ssd_w_out, conf_w_pw1, conf_b_pw1, conf_w_dw, conf_b_dw, conf_ln_w, conf_ln_b, conf_w_pw2, conf_b_pw2, ffn_w_up, ffn_conv_w, ffn_conv_b, ffn_w_down, final_norm_w, loss_target, m_c_ctx, m_mod_w, m_mod_b, m_norm1_w, m_norm2_w, m_ssd_w_in, m_ssd_conv_w, m_ssd_conv_b, m_ssd_dt_bias, m_ssd_a_log, m_ssd_d, m_ssd_norm_w, m_ssd_w_out, m_conf_w_pw1, m_conf_b_pw1, m_conf_w_dw, m_conf_b_dw, m_conf_ln_w, m_conf_ln_b, m_conf_w_pw2, m_conf_b_pw2, m_ffn_w_up, m_ffn_conv_w, m_ffn_conv_b, m_ffn_w_down, m_final_norm_w, v_c_ctx, v_mod_w, v_mod_b, v_norm1_w, v_norm2_w, v_ssd_w_in, v_ssd_conv_w, v_ssd_conv_b, v_ssd_dt_bias, v_ssd_a_log, v_ssd_d, v_ssd_norm_w, v_ssd_w_out, v_conf_w_pw1, v_conf_b_pw1, v_conf_w_dw, v_conf_b_dw, v_conf_ln_w, v_conf_ln_b, v_conf_w_pw2, v_conf_b_pw2, v_ffn_w_up, v_ffn_conv_w, v_ffn_conv_b, v_ffn_w_down, v_final_norm_w):
    mx, my, mc = _me()
    me = 4 * mx + 2 * my + mc
    L = x.shape[1]
    LC = ctx.shape[1]
    T = LC + L
    w_in_cols = ssd_w_in.shape[2] * N_DEV
    n_dt = w_in_cols - DI - CONVD

    small = [c[0], ssd_conv_w[0], conf_b_pw1[0], conf_w_dw[0], conf_b_dw[0], conf_ln_w[0], conf_ln_b[0], conf_b_pw2[0],
             ffn_conv_w]
    parts = [_f32_as_bf16_pairs(t) for t in small]
    sizes = [p.shape[0] for p in parts]
    small_flat = _pad_to(jnp.concatenate(parts), PACK_ALIGN).reshape(-1, PACK_W)
    w_in, small_g = exchange(GatherExchange([ssd_w_in[0].astype(bf16), small_flat]), "gather_first")
    w_up, w_down = [None, None], [None, None]
    small_g = small_g.reshape(N_DEV, -1)
    offs = [0]
    for s_ in sizes:
        offs.append(offs[-1] + s_)
    sm = [_bf16_pairs_as_f32(small_g[:, offs[i]:offs[i + 1]]) for i in range(len(sizes))]

    def cols(pc, K):
        return jnp.moveaxis(pc.reshape(N_DEV, K, -1), 0, 1).reshape(K, -1)

    c_all = sm[0]
    conv_w5 = cols(sm[1], 5)
    b_pw1 = sm[2].reshape(1, 2 * D)
    w_dw = cols(sm[3], CONF_K)
    b_dw, ln_w, ln_b, b_pw2 = (sm[i].reshape(1, D) for i in (4, 5, 6, 7))
    fcw = sm[8].reshape(N_DEV, 2, 9, FH // N_DEV)
    ffn_cw = [cols(fcw[:, i].reshape(N_DEV, -1), 9) for i in range(2)]
    in_segs = (DI, CONVD, n_dt)
    up_segs = (FH, FH)
    pw1_segs = (D, D)

    c16 = jnp.concatenate([c_all, jnp.broadcast_to(c_ctx[None, :], (N_DEV, D))], axis=0)
    m_sh = mod_fwd(c16, mod_w, "mod_fwd")
    mod_cols = mod_w.shape[2]
    m_gath = allgather_small(m_sh.reshape(2 * CROWS, mod_cols), "gather_mod")
    fly_a = gather_start([ssd_w_out[0].astype(bf16), ffn_w_up[0].astype(bf16), ffn_w_down[0].astype(bf16)],
                         "gather_a_start", m_gath)
    fly_b = gather_start([conf_w_pw1[0].astype(bf16), conf_w_pw2[0].astype(bf16)], "gather_b_start", fly_a[-1])
    fly_c = gather_start([ffn_w_up[1].astype(bf16), ffn_w_down[1].astype(bf16)], "gather_c_start", fly_b[-1])
    m_all = jnp.moveaxis(m_gath.reshape(N_DEV, 2, CROWS, mod_cols), 0, 2).reshape(2, CROWS, 6 * D) + mod_b[:, None, :]
    m_all = m_all + fly_c[-1][0, 0]
    m_lat = lax.dynamic_index_in_dim(m_all, me, axis=1, keepdims=False).reshape(2, 6, 1, D)
    m_ctx = m_all[:, N_DEV].reshape(2, 6, 1, D)
    zero_row = jnp.zeros((1, D), f32)

    def ffn_fwd(a2, i, tag):
        val, gate = smm_fwd(a2, w_up[i], None, up_segs, f"ffn{tag}_up")
        act, gs_, gvds, _ = ffn_gate_fwd(val, gate, ffn_cw[i], ffn_conv_b[i][None], f"ffn{tag}_gate")
        o2 = matmul(act, w_down[i], "nn", f32, f"ffn{tag}_down")
        return o2, (a2, gate, gs_, gvds, act)

    def ffn_bwd(do2, i, saved, tag):
        a2, gate, gs_, gvds, act = saved
        g_down = matmul(act, do2, "tn", bf16, f"ffn{tag}_down_dw")
        dact = matmul(do2, w_down[i], "nt", bf16, f"ffn{tag}_down_dx")
        dval, dgate, dcw, dcb = ffn_gate_bwd(gate, gs_, gvds, ffn_cw[i], dact, f"ffn{tag}_gate_bwd")
        g_up = smm_dw(a2, [dval, dgate], FH // 4, up_segs, 2, True, f"ffn{tag}_up_dw")
        da2, _ = smm_dx([dval, dgate], w_up[i], None, up_segs, bf16, f"ffn{tag}_up_dx")
        return da2, dict(w_up=g_up, w_down=g_down, conv_w=dcw, conv_b=dcb)

    nctx = LC // Q
    hx = x[0]
    sc0 = jnp.stack([m_ctx[0, 1], m_lat[0, 1]])
    sh0 = jnp.stack([m_ctx[0, 0], m_lat[0, 0]])
    a0 = modnorm_fwd(hx, norm1_w[0][None], sc0, sh0, LC // TB, "ssd_norm", ctx=ctx[0])
    z, xbc_pre, dt_raw = smm_fwd(a0, w_in, None, in_segs, "ssd_in")
    segs = ((0, LC), (LC, L))
    xbc, xbc_dsilu, _ = ssd_conv_fwd(xbc_pre, conv_w5, ssd_conv_b, segs, "ssd_conv")
    dt4 = dt_raw[:, :n_dt].reshape(T, 2, G, HPG)
    dtc = jnp.transpose(dt4, (1, 2, 0, 3))
    dtr = jnp.transpose(dt4, (1, 2, 3, 0))
    bias3 = ssd_dt_bias[0].reshape(2, G, HPG)
    alog3 = ssd_a_log[0].reshape(2, G, HPG)
    bc_, br_ = bias3[:, :, None, :], bias3[:, :, :, None]
    alc, alr = alog3[:, :, None, :], alog3[:, :, :, None]
    (y2, s_in_all), _ = ssd_scan_fwd(xbc, dtc, dtr, bc_, br_, alc, alr, nctx, "ssd_scan")
    dexp = jnp.repeat(ssd_d[0], P)[None, :]
    yn = ssd_gate_fwd(y2, xbc, z, dexp, ssd_norm_w, LC // GTB, "ssd_gate")
    w_out_g, w_up[0], w_down0_g = gather_wait(*fly_a[:5], yn, "gather_a_wait")
    w_out = w_out_g.reshape(DI, D)
    w_down[0] = w_down0_g.reshape(FH, D)
    o_ssd = matmul(yn, w_out, "nn", f32, "ssd_out")
    h1, a2_0 = resnorm_fwd(hx, o_ssd, m_lat[0, 2], zero_row, norm2_w[0][None], m_lat[0, 4], m_lat[0, 3], "ssd_res")
    o2_0, ffn0_saved = ffn_fwd(a2_0, 0, "0")

    h2, a1 = resnorm_fwd(h1, o2_0, m_lat[0, 5], zero_row, norm1_w[1][None], m_lat[1, 1], m_lat[1, 0], "ffn0_res")
    w_pw1, w_pw2_g = gather_wait(*fly_b[:5], a1, "gather_b_wait")
    w_pw2 = w_pw2_g.reshape(D, D)
    pa, pg = smm_fwd(a1, w_pw1, None, pw1_segs, "conf_pw1")
    dwc, _ = conf_glu_conv_fwd(pa, pg, b_pw1, w_dw, b_dw, "conf_conv")
    s1 = ln_silu_fwd(dwc, ln_w, ln_b, "conf_ln")
    o_conf = matmul(s1, w_pw2, "nn", f32, "conf_pw2")
    h3, a2_1 = resnorm_fwd(h2, o_conf, m_lat[1, 2], b_pw2, norm2_w[1][None], m_lat[1, 4], m_lat[1, 3], "conf_res")
    w_up[1], w_down1_g = gather_wait(*fly_c[:5], h3, "gather_c_wait")
    w_down[1] = w_down1_g.reshape(FH, D)
    o2_1, ffn1_saved = ffn_fwd(a2_1, 1, "1")

    loss_part, dh4, g_final, do2_1, dg2_1 = final_loss(h3, o2_1, m_lat[1, 5], final_norm_w[None], loss_target[0],
                                                       "loss_head")
    da2_1, gf1 = ffn_bwd(do2_1, 1, ffn1_saved, "1")
    dh3, dn2_1, dsc2_1, dsh2_1, do_conf, dg1_1, g_b_pw2 = normres_bwd(
        h3, norm2_w[1][None], m_lat[1, 4], m_lat[1, 3], da2_1, dh4, o_conf, m_lat[1, 2], b_pw2, "ffn1_norm_bwd")
    gf1.update(norm2=dn2_1, sh2=dsh2_1, sc2=dsc2_1, g2=dg2_1)
    g_pw2 = matmul(s1, do_conf, "tn", bf16, "conf_pw2_dw")
    ds1 = matmul(do_conf, w_pw2, "nt", bf16, "conf_pw2_dx")
    ddwc, g_ln_w, g_ln_b = ln_silu_bwd(dwc, ln_w, ln_b, ds1, "conf_ln_bwd")
    dpa, dpg, dba, dbg, g_w_dw, g_b_dw = conf_glu_conv_bwd(pa, pg, b_pw1, w_dw, ddwc, "conf_conv_bwd")
    g_b_pw1 = jnp.concatenate([dba, dbg], axis=1)
    g_pw1 = smm_dw(a1, [dpa, dpg], 2 * D // N_DEV, pw1_segs, 1, False, "conf_pw1_dw")
    da1, _ = smm_dx([dpa, dpg], w_pw1, None, pw1_segs, bf16, "conf_pw1_dx")
    dh2, g_n1_1, dsc1_1, dsh1_1, do2_0, dg2_0, _ = normres_bwd(
        h2, norm1_w[1][None], m_lat[1, 1], m_lat[1, 0], da1, dh3, o2_0, m_lat[0, 5], zero_row, "conf_norm_bwd")
    da2_0, gf0 = ffn_bwd(do2_0, 0, ffn0_saved, "0")
    dh1, dn2_0, dsc2_0, dsh2_0, do_ssd, dg1_0, _ = normres_bwd(
        h1, norm2_w[0][None], m_lat[0, 4], m_lat[0, 3], da2_0, dh2, o_ssd, m_lat[0, 2], zero_row, "ffn0_norm_bwd")
    gf0.update(norm2=dn2_0, sh2=dsh2_0, sc2=dsc2_0, g2=dg2_0)
    g_w_out = matmul(yn, do_ssd, "tn", bf16, "ssd_out_dw")
    dyn = matmul(do_ssd, w_out, "nt", bf16, "ssd_out_dx")
    core = mc.reshape(1).astype(jnp.int32)

    def by_device(t):
        return t.reshape((4, 2, -1, t.shape[-1]))

    early = [by_device(t) for t in (gf1["w_up"], gf1["w_down"], g_pw2, g_pw1, gf0["w_up"], gf0["w_down"], g_w_out)]
    (dy, dz, g_dexp, g_ssd_norm), early_sib = ssd_gate_bwd(
        y2, xbc, z, dexp, ssd_norm_w, dyn, LC // GTB, "ssd_gate_bwd", SiblingExchange(early))
    early_part = [add_own(t, r_, core, f"reduce_add{i}") for i, (t, r_) in enumerate(zip(early, early_sib))]
    (dxbc2, ddtc, ddtr, dbc, dbr, dalc, dalr), early_red = ssd_scan_bwd(
        xbc, dtc, dtr, bc_, br_, alc, alr, s_in_all, dy, nctx, "ssd_scan_bwd", ChipsExchange(early_part))
    ddt = (jnp.transpose(ddtc, (2, 0, 1, 3)) + jnp.transpose(ddtr, (3, 0, 1, 2))).reshape(T, n_dt)
    g_dt_bias = (dbc[:, :, 0, :] + dbr[:, :, :, 0]).reshape(2, NH_SSD)
    g_a_log = (dalc[:, :, 0, :] + dalr[:, :, :, 0]).reshape(2, NH_SSD)
    g_ssd_d = g_dexp[0, :NH_SSD]
    du, g_conv_w5, g_conv_b5 = ssd_conv_bwd(xbc_pre, conv_w5, xbc_dsilu, dxbc2, dy, dexp, segs, "ssd_conv_bwd")
    ddt_p = _pad_to(ddt, 128).astype(bf16)
    g_w_in = smm_dw(a0, [dz, du, ddt_p], w_in.shape[-1], in_segs, 2, True, "ssd_in_dw")
    g_ffn_cw = jnp.stack([gf0["conv_w"], gf1["conv_w"]])
    small_shards = [_col_shards(t) for t in (g_conv_w5, g_b_pw1, g_w_dw, g_b_dw, g_ln_w, g_ln_b, g_b_pw2, g_ffn_cw)]
    gsizes = [s_.shape[1] for s_ in small_shards]
    g_small = _pad_to(jnp.concatenate(small_shards, axis=1), PACK_ALIGN).astype(bf16)
    late = [by_device(g_w_in), by_device(g_small.reshape(N_DEV, -1, PACK_W))]
    da0, late_sib = smm_dx([dz, du, ddt_p], w_in, None, in_segs, f32, "ssd_in_dx", SiblingExchange(late))
    late_part = [add_own(t, r_, core, f"reduce_add_late{i}", twice=True) for i, (t, r_) in enumerate(zip(late, late_sib))]
    late_flying = chips_start([p_[0] for p_ in late_part], [p_[1] for p_ in late_part], "reduce_chips_late_start")
    dh0, g_n1_0, dsc1_0, dsh1_0 = modnorm_bwd(hx, norm1_w[0][None], sc0, sh0, da0, dh1, LC // TB, "ssd_norm_bwd",
                                              ctx=ctx[0])
    grad_x = dh0[None]

    r_up1, r_down1, r_pw2, r_pw1, r_up0, r_down0, r_out = early_red
    big = {}
    def tr(t):
        return jnp.swapaxes(t, -1, -2)

    up_t, m_up_t, v_up_t = tr(ffn_w_up), tr(m_ffn_w_up), tr(v_ffn_w_up)
    send_sems, recv_sems, late_p, late_land, token = late_flying
    up0 = sum_adamw(r_up0, up_t, m_up_t, v_up_t, 0, "adamw_ffn_w_up0", after=token)
    up1 = sum_adamw(r_up1, up_t, m_up_t, v_up_t, 1, "adamw_ffn_w_up1", into=up0)
    big["ffn_w_up"] = tuple(tr(t) for t in up1)
    big["conf_w_pw1"] = sum_adamw(r_pw1, conf_w_pw1[0], m_conf_w_pw1[0], v_conf_w_pw1[0], None, "adamw_conf_w_pw1",
                                  after=up1[0])
    big["ssd_w_out"] = sum_adamw(r_out, ssd_w_out[0], m_ssd_w_out[0], v_ssd_w_out[0], None, "adamw_ssd_w_out",
                                 after=big["conf_w_pw1"][0])
    dn0 = sum_adamw(r_down0, ffn_w_down, m_ffn_w_down, v_ffn_w_down, 0, "adamw_ffn_w_down0", after=big["ssd_w_out"][0])
    big["ffn_w_down"] = sum_adamw(r_down1, ffn_w_down, m_ffn_w_down, v_ffn_w_down, 1, "adamw_ffn_w_down1", into=dn0)
    big["conf_w_pw2"] = sum_adamw(r_pw2, conf_w_pw2[0], m_conf_w_pw2[0], v_conf_w_pw2[0], None, "adamw_conf_w_pw2",
                                  after=big["ffn_w_down"][0])

    zeros_d = jnp.zeros((1, D), f32)
    dm_lat = jnp.stack([
        jnp.concatenate([dsh1_0[1], dsc1_0[1], dg1_0, gf0["sh2"], gf0["sc2"], gf0["g2"]], axis=1),
        jnp.concatenate([dsh1_1, dsc1_1, dg1_1, gf1["sh2"], gf1["sc2"], gf1["g2"]], axis=1)])
    dm_ctx = jnp.stack([
        jnp.concatenate([dsh1_0[0], dsc1_0[0]] + [zeros_d] * 4, axis=1), jnp.zeros((1, 6 * D), f32)])
    dm_mine = jnp.concatenate([dm_lat.reshape(2, 6 * D), dm_ctx.reshape(2, 6 * D),
                               jnp.zeros((4, 6 * D), f32)], axis=0)
    dm_g = allgather_small(dm_mine, "gather_dmod", after=big["conf_w_pw2"][0])
    dm_all = jnp.concatenate([jnp.moveaxis(dm_g[:, 0:2], 0, 1), jnp.moveaxis(dm_g[:, 2:4], 0, 1)], axis=1)
    dm_sh = lax.dynamic_slice_in_dim(dm_all, me * mod_cols, mod_cols, axis=2)
    g_mod_w, g_cctx_part, g_mod_b = mod_bwd(c16, mod_w, dm_sh, dm_all, "mod_bwd")

    rep = [jnp.stack([g_n1_0[0], g_n1_1[0]]), jnp.stack([gf0["norm2"][0], gf1["norm2"][0]]), g_conv_b5, g_dt_bias, g_a_log,
           g_ssd_d, g_ssd_norm, jnp.stack([gf0["conv_b"][0], gf1["conv_b"][0]]), g_final, g_cctx_part, loss_part[:, :1]]
    rep_sizes = [r_.size for r_ in rep]
    rep_flat = _pad_to(jnp.concatenate([r_.reshape(-1) for r_ in rep]), 8 * PACK_W).reshape(-1, PACK_W)
    _, rep_sum = allgather_small(rep_flat, "reduce_replicated", with_sum=True)
    rep_sum = rep_sum.reshape(-1)
    roffs = [0]
    for s_ in rep_sizes:
        roffs.append(roffs[-1] + s_)
    rp = [rep_sum[roffs[i]:roffs[i + 1]] for i in range(len(rep_sizes))]
    loss = rp[10].reshape(())

    r_in, r_small = chips_wait(send_sems, recv_sems, late_p, late_land, rep_sum, "reduce_chips_late_wait")
    w_in_res = sum_adamw(r_in, tr(ssd_w_in[0]), tr(m_ssd_w_in[0]), tr(v_ssd_w_in[0]), None, "adamw_ssd_w_in")
    big["ssd_w_in"] = tuple(tr(t) for t in w_in_res)
    g_flat = sum_rows(r_small, "reduce_sum_small").reshape(-1)
    goffs = [0]
    for s_ in gsizes:
        goffs.append(goffs[-1] + s_)
    gs = [g_flat[goffs[i]:goffs[i + 1]] for i in range(len(gsizes))]
    grads = {
        "c_ctx": rp[9], "mod_w": g_mod_w, "mod_b": g_mod_b, "norm1_w": rp[0], "norm2_w": rp[1],
        "ssd_conv_w": gs[0], "ssd_conv_b": rp[2], "ssd_dt_bias": rp[3], "ssd_a_log": rp[4], "ssd_d": rp[5],
        "ssd_norm_w": rp[6], "conf_b_pw1": gs[1], "conf_w_dw": gs[2],
        "conf_b_dw": gs[3], "conf_ln_w": gs[4], "conf_ln_b": gs[5], "conf_b_pw2": gs[6],
        "ffn_conv_w": gs[7], "ffn_conv_b": rp[7], "final_norm_w": rp[8],
    }
    weights = dict(c_ctx=c_ctx, mod_w=mod_w, mod_b=mod_b, norm1_w=norm1_w, norm2_w=norm2_w, ssd_w_in=ssd_w_in, ssd_conv_w=ssd_conv_w, ssd_conv_b=ssd_conv_b, ssd_dt_bias=ssd_dt_bias, ssd_a_log=ssd_a_log, ssd_d=ssd_d, ssd_norm_w=ssd_norm_w, ssd_w_out=ssd_w_out, conf_w_pw1=conf_w_pw1, conf_b_pw1=conf_b_pw1, conf_w_dw=conf_w_dw, conf_b_dw=conf_b_dw, conf_ln_w=conf_ln_w, conf_ln_b=conf_ln_b, conf_w_pw2=conf_w_pw2, conf_b_pw2=conf_b_pw2, ffn_w_up=ffn_w_up, ffn_conv_w=ffn_conv_w, ffn_conv_b=ffn_conv_b, ffn_w_down=ffn_w_down, final_norm_w=final_norm_w)
    m_in = dict(c_ctx=m_c_ctx, mod_w=m_mod_w, mod_b=m_mod_b, norm1_w=m_norm1_w, norm2_w=m_norm2_w, ssd_w_in=m_ssd_w_in, ssd_conv_w=m_ssd_conv_w, ssd_conv_b=m_ssd_conv_b, ssd_dt_bias=m_ssd_dt_bias, ssd_a_log=m_ssd_a_log, ssd_d=m_ssd_d, ssd_norm_w=m_ssd_norm_w, ssd_w_out=m_ssd_w_out, conf_w_pw1=m_conf_w_pw1, conf_b_pw1=m_conf_b_pw1, conf_w_dw=m_conf_w_dw, conf_b_dw=m_conf_b_dw, conf_ln_w=m_conf_ln_w, conf_ln_b=m_conf_ln_b, conf_w_pw2=m_conf_w_pw2, conf_b_pw2=m_conf_b_pw2, ffn_w_up=m_ffn_w_up, ffn_conv_w=m_ffn_conv_w, ffn_conv_b=m_ffn_conv_b, ffn_w_down=m_ffn_w_down, final_norm_w=m_final_norm_w)
    v_in = dict(c_ctx=v_c_ctx, mod_w=v_mod_w, mod_b=v_mod_b, norm1_w=v_norm1_w, norm2_w=v_norm2_w, ssd_w_in=v_ssd_w_in, ssd_conv_w=v_ssd_conv_w, ssd_conv_b=v_ssd_conv_b, ssd_dt_bias=v_ssd_dt_bias, ssd_a_log=v_ssd_a_log, ssd_d=v_ssd_d, ssd_norm_w=v_ssd_norm_w, ssd_w_out=v_ssd_w_out, conf_w_pw1=v_conf_w_pw1, conf_b_pw1=v_conf_b_pw1, conf_w_dw=v_conf_w_dw, conf_b_dw=v_conf_b_dw, conf_ln_w=v_conf_ln_w, conf_ln_b=v_conf_ln_b, conf_w_pw2=v_conf_w_pw2, conf_b_pw2=v_conf_b_pw2, ffn_w_up=v_ffn_w_up, ffn_conv_w=v_ffn_conv_w, ffn_conv_b=v_ffn_conv_b, ffn_w_down=v_ffn_w_down, final_norm_w=v_final_norm_w)

    out_g, out_d, out_m, out_v = [], [], [], []
    for name_, w_ in weights.items():
        shape = w_.shape
        if name_ in big:
            for lst, t in zip((out_g, out_d, out_m, out_v), big[name_]):
                lst.append(t.reshape(shape))
            continue
        cols2 = shape[-1] if len(shape) > 1 else shape[0]
        g2 = grads[name_].reshape(-1, cols2)
        d_, nm_, nv_ = adamw(w_.reshape(-1, cols2), g2, m_in[name_].reshape(-1, cols2), v_in[name_].reshape(-1, cols2),
                             f"adamw_{name_}")
        out_g.append(g2.reshape(shape))
        out_d.append(d_.reshape(shape))
        out_m.append(nm_.reshape(shape))
        out_v.append(nv_.reshape(shape))
    return (loss, grad_x, *out_g, *out_d, *out_m, *out_v)
```

```python
import functools

import jax
import jax.numpy as jnp
from jax import lax
from jax.experimental import pallas as pl
from jax.experimental.pallas import tpu as pltpu

f32 = jnp.float32
bf16 = jnp.bfloat16
HI = lax.Precision.HIGHEST
S = jax.ShapeDtypeStruct
MESH = pl.DeviceIdType.MESH
ANY = pl.BlockSpec(memory_space=pl.ANY)
VMEM = pl.BlockSpec(memory_space=pltpu.VMEM)

N_DEV = 8
D = 1024
DI = 2048
CONVD = 4096
FH = 2816
GRID_W = 64
Q = 128
HPG = 4
P = 64
N = 128
G = 8
GW = HPG * P
NH_SSD = G * HPG
EPS = 1e-6
ADAM_LR, ADAM_B1, ADAM_B2, ADAM_EPS, ADAM_WD, ADAM_STEP = 0.001, 0.9, 0.999, 1e-08, 0.01, 10
VMEM_LIMIT_BYTES = 56 * 1024 * 1024
PACK_W = 1024
TB = 256
LTB = 512


def _cparams(*sem):
    return pltpu.CompilerParams(dimension_semantics=sem, vmem_limit_bytes=VMEM_LIMIT_BYTES)


def _pick(n, prefs):
    for p in prefs:
        if n % p == 0:
            return p
    return n


def _sigmoid(x):
    return 1.0 / (1.0 + jnp.exp(-x))


def _softplus(x):
    return jnp.maximum(x, 0.0) + jnp.log(1.0 + jnp.exp(-jnp.abs(x)))


def matmul(a, b, mode, out_dtype, name):
    if mode == "nn":
        (M, K), (_, Nn) = a.shape, b.shape
        bm, bn, bk = _pick(M, (512, 384, 256, 128)), Nn, K
    elif mode == "tn":
        (K, M), (_, Nn) = a.shape, b.shape
        bm, bn, bk = M, Nn, _pick(K, (256, 128))
    else:
        (M, K), (Nn, _) = a.shape, b.shape
        bm, bn, bk = _pick(M, (512, 384, 256, 128)), Nn, K
    nk = K // bk
    dims = {"nn": (((1,), (0,)), ((), ())), "tn": (((0,), (0,)), ((), ())), "nt": (((1,), (1,)), ((), ()))}[mode]

    def body(a_ref, b_ref, o_ref, acc_ref):
        k = pl.program_id(2)

        @pl.when(k == 0)
        def _():
            acc_ref[...] = jnp.zeros_like(acc_ref)

        acc_ref[...] += lax.dot_general(a_ref[...].astype(bf16), b_ref[...].astype(bf16), dims,
                                        preferred_element_type=f32)

        @pl.when(k == nk - 1)
        def _():
            o_ref[...] = acc_ref[...].astype(out_dtype)

    if mode == "nn":
        a_spec = pl.BlockSpec((bm, bk), lambda i, j, k: (i, k))
        b_spec = pl.BlockSpec((bk, bn), lambda i, j, k: (k, j))
    elif mode == "tn":
        a_spec = pl.BlockSpec((bk, bm), lambda i, j, k: (k, i))
        b_spec = pl.BlockSpec((bk, bn), lambda i, j, k: (k, j))
    else:
        a_spec = pl.BlockSpec((bm, bk), lambda i, j, k: (i, k))
        b_spec = pl.BlockSpec((bn, bk), lambda i, j, k: (j, k))
    return pl.pallas_call(
        body, grid=(M // bm, Nn // bn, nk), in_specs=[a_spec, b_spec],
        out_specs=pl.BlockSpec((bm, bn), lambda i, j, k: (i, j)),
        out_shape=S((M, Nn), out_dtype), scratch_shapes=[pltpu.VMEM((bm, bn), f32)],
        compiler_params=_cparams("parallel", "parallel", "arbitrary"), name=name,
    )(a, b)


SMM_BM = 256
SMM_ROWS = (256,)


def _shard_pieces(seg_widths, n):
    bounds = [0]
    for sw in seg_widths:
        bounds.append(bounds[-1] + sw)
    assert bounds[-1] == N_DEV * n, (seg_widths, n)
    out = []
    for j in range(N_DEV):
        lo, hi = j * n, (j + 1) * n
        pcs = []
        for si in range(len(seg_widths)):
            a, b = max(lo, bounds[si]), min(hi, bounds[si + 1])
            if a < b:
                pcs.append((si, a - bounds[si], a - lo, b - a))
        out.append(pcs)
    return out


def _w_spec(w, layer):
    if layer is None:
        return pl.BlockSpec(w.shape, lambda *idx: (0, 0, 0))
    return pl.BlockSpec((N_DEV, None) + w.shape[2:], lambda *idx: (0, layer, 0, 0))


def smm_fwd(a, w, layer, seg_widths, name, hosted=None):
    M, K = a.shape
    n = w.shape[-1]
    pieces = _shard_pieces(seg_widths, n)
    padded = [sw + (-sw) % 128 for sw in seg_widths]
    bm = _pick(M, SMM_ROWS)

    def body(a_ref, w_ref, *o_refs):
        av = a_ref[...]
        for si, sw in enumerate(seg_widths):
            if padded[si] != sw:
                o_refs[si][:, pl.ds(padded[si] - 128, 128)] = jnp.zeros((bm, 128), f32)
        for j in range(N_DEV):
            for si, soff, woff, wd in pieces[j]:
                o_refs[si][:, pl.ds(soff, wd)] = jnp.dot(av, w_ref[j, :, pl.ds(woff, wd)], preferred_element_type=f32)

    outs, extra = _host_call(
        body, (M // bm,), [pl.BlockSpec((bm, K), lambda i: (i, 0)), _w_spec(w, layer)],
        [pl.BlockSpec((bm, pw), lambda i: (i, 0)) for pw in padded], [S((M, pw), f32) for pw in padded], [],
        ("parallel",), name, (a, w), hosted)
    return outs if hosted is None else (outs, extra)


def smm_dx(d_segs, w, layer, seg_widths, out_dtype, name, hosted=None):
    M = d_segs[0].shape[0]
    K, n = w.shape[-2], w.shape[-1]
    pieces = _shard_pieces(seg_widths, n)
    ns = len(d_segs)
    bm = _pick(M, SMM_ROWS)

    def body(*refs):
        d_refs, w_ref, o_ref = refs[:ns], refs[ns], refs[ns + 1]
        acc = jnp.zeros((bm, K), f32)
        for j in range(N_DEV):
            for si, soff, woff, wd in pieces[j]:
                acc = acc + lax.dot_general(d_refs[si][:, pl.ds(soff, wd)], w_ref[j, :, pl.ds(woff, wd)],
                                            (((1,), (1,)), ((), ())), preferred_element_type=f32)
        o_ref[...] = acc.astype(out_dtype)

    (out,), extra = _host_call(
        body, (M // bm,),
        [pl.BlockSpec((bm, d.shape[1]), lambda i: (i, 0)) for d in d_segs] + [_w_spec(w, layer)],
        [pl.BlockSpec((bm, K), lambda i: (i, 0))], [S((M, K), out_dtype)], [], ("parallel",), name,
        (*d_segs, w), hosted)
    return out, extra


def smm_dw(a, d_segs, n, seg_widths, ngrp, transposed, name):
    M, K = a.shape
    pieces = _shard_pieces(seg_widths, n)
    per = N_DEV // ngrp
    nI = M // SMM_BM
    ns = len(d_segs)
    shard = (n, K) if transposed else (K, n)

    def body(*refs):
        a_ref, d_refs, o_ref, acc_ref = refs[0], refs[1:1 + ns], refs[1 + ns], refs[2 + ns]
        grp = pl.program_id(0)
        i = pl.program_id(1)

        @pl.when(i == 0)
        def _():
            acc_ref[...] = jnp.zeros_like(acc_ref)

        av = a_ref[...]
        for gs in range(ngrp):
            def one_group(gs=gs):
                for jj in range(per):
                    for si, soff, woff, wd in pieces[gs * per + jj]:
                        dv = d_refs[si][:, pl.ds(soff, wd)]
                        if transposed:
                            acc_ref[jj, pl.ds(woff, wd), :] += lax.dot_general(
                                dv, av, (((0,), (0,)), ((), ())), preferred_element_type=f32)
                        else:
                            acc_ref[jj, :, pl.ds(woff, wd)] += lax.dot_general(
                                av, dv, (((0,), (0,)), ((), ())), preferred_element_type=f32)
            pl.when(grp == gs)(one_group)

        @pl.when(i == nI - 1)
        def _():
            o_ref[...] = acc_ref[...].astype(bf16)

    return pl.pallas_call(
        body, grid=(ngrp, nI),
        in_specs=[pl.BlockSpec((SMM_BM, K), lambda g, i: (i, 0))]
        + [pl.BlockSpec((SMM_BM, d.shape[1]), lambda g, i: (i, 0)) for d in d_segs],
        out_specs=pl.BlockSpec((per,) + shard, lambda g, i: (g, 0, 0)), out_shape=S((N_DEV,) + shard, bf16),
        scratch_shapes=[pltpu.VMEM((per,) + shard, f32)],
        compiler_params=_cparams("arbitrary", "arbitrary"), name=name)(a, *d_segs)


def _modnorm_f(h, w, sc, sh):
    y = h * lax.rsqrt(jnp.mean(h * h, axis=-1, keepdims=True) + EPS)
    return (y * w) * (1.0 + sc) + sh


def _kind_specs(nctxb):
    if nctxb > 0:
        return pl.BlockSpec((None, 1, D), lambda i: (jnp.where(i < nctxb, 0, 1), 0, 0))
    return pl.BlockSpec((None, 1, D), lambda i: (0, 0, 0))


def _two_part_specs(nctxb):
    return (pl.BlockSpec((TB, D), lambda i: (jnp.minimum(i, nctxb - 1), 0)),
            pl.BlockSpec((TB, D), lambda i: (jnp.maximum(i - nctxb, 0), 0)))


def modnorm_fwd(h, w, sc, sh, nctxb, name, ctx=None):
    if ctx is None:
        T = h.shape[0]

        def body(h_ref, w_ref, sc_ref, sh_ref, o_ref):
            o_ref[...] = _modnorm_f(h_ref[...], w_ref[...], sc_ref[...], sh_ref[...]).astype(bf16)

        hspecs, hargs = [pl.BlockSpec((TB, D), lambda i: (i, 0))], (h,)
    else:
        T = h.shape[0] + ctx.shape[0]

        def body(c_ref, h_ref, w_ref, sc_ref, sh_ref, o_ref):
            hv = jnp.where(pl.program_id(0) < nctxb, c_ref[...], h_ref[...])
            o_ref[...] = _modnorm_f(hv, w_ref[...], sc_ref[...], sh_ref[...]).astype(bf16)

        hspecs, hargs = list(_two_part_specs(nctxb)), (ctx, h)
    row = pl.BlockSpec((1, D), lambda i: (0, 0))
    ks = _kind_specs(nctxb)
    return pl.pallas_call(body, grid=(T // TB,), in_specs=hspecs + [row, ks, ks],
                          out_specs=pl.BlockSpec((TB, D), lambda i: (i, 0)), out_shape=S((T, D), bf16),
                          compiler_params=_cparams("parallel"), name=name)(*hargs, w, sc, sh)


def modnorm_bwd(h, w, sc, sh, da, dres, nctxb, name, ctx=None):
    T = h.shape[0] + (0 if ctx is None else ctx.shape[0])
    kinds = sc.shape[0]
    nh = 1 if ctx is None else 2

    def body(*refs):
        w_ref, sc_ref, sh_ref, da_ref, dres_ref, dh_ref, dw_ref, dsc_ref, dsh_ref = refs[nh:]
        i = pl.program_id(0)
        hv = refs[0][...] if ctx is None else jnp.where(i < nctxb, refs[0][...], refs[1][...])
        _, vjp = jax.vjp(_modnorm_f, hv, w_ref[...], sc_ref[...], sh_ref[...])
        dh, dw, dsc, dsh = vjp(da_ref[...].astype(f32))
        dh_ref[...] = dres_ref[...] + dh

        @pl.when(i == 0)
        def _():
            dw_ref[...] = jnp.zeros_like(dw_ref)

        @pl.when((i == 0) | (i == nctxb))
        def _():
            dsc_ref[...] = jnp.zeros_like(dsc_ref)
            dsh_ref[...] = jnp.zeros_like(dsh_ref)

        dw_ref[...] += dw
        dsc_ref[...] += dsc
        dsh_ref[...] += dsh

    blk = pl.BlockSpec((TB, D), lambda i: (i, 0))
    lat = pl.BlockSpec((TB, D), lambda i: (jnp.maximum(i - nctxb, 0), 0))
    row = pl.BlockSpec((1, D), lambda i: (0, 0))
    ks = _kind_specs(nctxb)
    hspecs, hargs = ([blk], (h,)) if ctx is None else (list(_two_part_specs(nctxb)), (ctx, h))
    return pl.pallas_call(
        body, grid=(T // TB,), in_specs=hspecs + [row, ks, ks, blk, lat], out_specs=[lat, row, ks, ks],
        out_shape=[S((T - nctxb * TB, D), f32), S((1, D), f32), S((kinds, 1, D), f32), S((kinds, 1, D), f32)],
        compiler_params=_cparams("arbitrary"), name=name)(*hargs, w, sc, sh, da, dres)


def resnorm_fwd(h, o, g, b, w, sc, sh, name):
    T = h.shape[0]

    def body(h_ref, o_ref, g_ref, b_ref, w_ref, sc_ref, sh_ref, hn_ref, a_ref):
        hn = h_ref[...] + g_ref[...] * (o_ref[...] + b_ref[...])
        hn_ref[...] = hn
        a_ref[...] = _modnorm_f(hn, w_ref[...], sc_ref[...], sh_ref[...]).astype(bf16)

    blk = pl.BlockSpec((LTB, D), lambda i: (i, 0))
    row = pl.BlockSpec((1, D), lambda i: (0, 0))
    return pl.pallas_call(body, grid=(T // LTB,), in_specs=[blk, blk, row, row, row, row, row], out_specs=[blk, blk],
                          out_shape=[S((T, D), f32), S((T, D), bf16)], compiler_params=_cparams("parallel"),
                          name=name)(h, o, g, b, w, sc, sh)


def normres_bwd(h, w, sc, sh, da, dres, o, g, b, name):
    T = h.shape[0]

    def body(h_ref, w_ref, sc_ref, sh_ref, da_ref, dres_ref, o_ref, g_ref, b_ref,
             dh_ref, dw_ref, dsc_ref, dsh_ref, do_ref, dg_ref, db_ref):
        _, vjp = jax.vjp(_modnorm_f, h_ref[...], w_ref[...], sc_ref[...], sh_ref[...])
        dhn, dw, dsc, dsh = vjp(da_ref[...].astype(f32))
        dh = dres_ref[...] + dhn
        dh_ref[...] = dh
        do = g_ref[...] * dh
        do_ref[...] = do.astype(bf16)
        sums = (dw, dsc, dsh, jnp.sum(dh * (o_ref[...] + b_ref[...]), axis=0, keepdims=True),
                jnp.sum(do, axis=0, keepdims=True))

        @pl.when(pl.program_id(0) == 0)
        def _():
            for r_ in (dw_ref, dsc_ref, dsh_ref, dg_ref, db_ref):
                r_[...] = jnp.zeros_like(r_)

        for r_, s_ in zip((dw_ref, dsc_ref, dsh_ref, dg_ref, db_ref), sums):
            r_[...] += s_

    blk = pl.BlockSpec((LTB, D), lambda i: (i, 0))
    row = pl.BlockSpec((1, D), lambda i: (0, 0))
    return pl.pallas_call(
        body, grid=(T // LTB,), in_specs=[blk, row, row, row, blk, blk, blk, row, row],
        out_specs=[blk, row, row, row, blk, row, row],
        out_shape=[S((T, D), f32), S((1, D), f32), S((1, D), f32), S((1, D), f32), S((T, D), bf16), S((1, D), f32),
                   S((1, D), f32)],
        compiler_params=_cparams("arbitrary"), name=name)(h, w, sc, sh, da, dres, o, g, b)


def final_loss(h, o, g, w, tgt, name):
    T = h.shape[0]

    def f(hv, wv, tv):
        y = (hv * lax.rsqrt(jnp.mean(hv * hv, axis=-1, keepdims=True) + EPS)) * wv
        e = y - tv
        return 0.5 * jnp.sum(jnp.sum(e * e, axis=-1, keepdims=True), axis=0, keepdims=True) * (1.0 / D)

    def body(h_ref, o_ref, g_ref, w_ref, t_ref, loss_ref, dh_ref, dw_ref, do_ref, dg_ref):
        i = pl.program_id(0)
        tv = t_ref[...]
        ov = o_ref[...]
        gv = g_ref[...]
        val, vjp = jax.vjp(lambda a, b_: f(a, b_, tv), h_ref[...] + gv * ov, w_ref[...])
        dh, dw = vjp(jnp.ones((1, 1), f32))
        dh_ref[...] = dh
        do_ref[...] = (gv * dh).astype(bf16)

        @pl.when(i == 0)
        def _():
            loss_ref[...] = jnp.zeros_like(loss_ref)
            dw_ref[...] = jnp.zeros_like(dw_ref)
            dg_ref[...] = jnp.zeros_like(dg_ref)

        loss_ref[...] += jnp.broadcast_to(val, (1, 128))
        dw_ref[...] += dw
        dg_ref[...] += jnp.sum(dh * ov, axis=0, keepdims=True)

    blk = pl.BlockSpec((LTB, D), lambda i: (i, 0))
    row = pl.BlockSpec((1, D), lambda i: (0, 0))
    return pl.pallas_call(body, grid=(T // LTB,), in_specs=[blk, blk, row, row, blk],
                          out_specs=[pl.BlockSpec((1, 128), lambda i: (0, 0)), blk, row, blk, row],
                          out_shape=[S((1, 128), f32), S((T, D), f32), S((1, D), f32), S((T, D), bf16), S((1, D), f32)],
                          compiler_params=_cparams("arbitrary"), name=name)(h, o, g, w, tgt)


CB = 256
RT = 32
RTB = 16


def _fold8(t):
    acc = t[0:8]
    for k in range(1, t.shape[0] // 8):
        acc = acc + t[8 * k:8 * (k + 1)]
    return acc


def _rows(start, off=0, rt=RT):
    return pl.ds(pl.multiple_of(start + off, 8), rt)


def _rowsb(start, off=0):
    return _rows(start, off, RTB)


def _zero_rows(ref, start, n):
    ref[pl.ds(start, n), :] = jnp.zeros((n, ref.shape[1]), f32)


K5, HALF5, PAD5 = 5, 2, 8


def _taps5(base_ref, r, rt, sign):
    n = rt + 2 * PAD5
    v = base_ref[pl.ds(pl.multiple_of(r, 8), n), :]
    taps = []
    for k in range(K5):
        o = sign * (k - HALF5)
        rolled = v if o == 0 else pltpu.roll(v, (-o) % n, axis=0)
        taps.append(rolled[PAD5:PAD5 + rt])
    return taps


def ssd_conv_fwd(u, w, b, segs, name, hosted=None):
    T = u.shape[0]
    maxlen = max(ln for _, ln in segs)

    def body(u_ref, w_ref, b_ref, o_ref, ds_ref, base_ref):
        wv = [w_ref[pl.ds(k, 1), :] for k in range(K5)]
        bv = b_ref[...]
        for s0, ln in segs:
            _zero_rows(base_ref, 0, PAD5)
            _zero_rows(base_ref, PAD5 + ln, PAD5)
            base_ref[pl.ds(PAD5, ln), :] = u_ref[pl.ds(s0, ln), :]

            def tile(i, carry):
                r = i * RT
                taps = _taps5(base_ref, r, RT, 1)
                acc = jnp.broadcast_to(bv, (RT, CB))
                for k in range(K5):
                    acc = acc + taps[k] * wv[k]
                sg = _sigmoid(acc)
                o_ref[_rows(r, s0), :] = acc * sg
                ds_ref[_rows(r, s0), :] = sg * (1.0 + acc * (1.0 - sg))
                return carry

            lax.fori_loop(0, ln // RT, tile, 0, unroll=2)

    cblk = pl.BlockSpec((T, CB), lambda j: (0, j))
    (out, dsilu), extra = _host_call(
        body, (CONVD // CB,),
        [cblk, pl.BlockSpec((K5, CB), lambda j: (0, j)), pl.BlockSpec((1, CB), lambda j: (0, j))],
        [cblk, cblk], [S((T, CONVD), f32), S((T, CONVD), f32)],
        [pltpu.VMEM((maxlen + 2 * PAD5, CB), f32)], ("parallel",), name, (u, w, b), hosted)
    return out, dsilu, extra


def ssd_conv_bwd(proj, w, dsilu, dy2, dyskip, dexp, segs, name):
    T = proj.shape[0]
    maxlen = max(ln for _, ln in segs)
    nskip = DI // CB

    def body(u_ref, w_ref, ds_ref, dya_ref, dyb_ref, dsk_ref, dexp_ref, du_ref, dw_ref, db_ref, base_ref, dbase_ref):
        wv = [w_ref[pl.ds(k, 1), :] for k in range(K5)]
        has_skip = (pl.program_id(0) < nskip).astype(f32) * dexp_ref[...]
        acc8 = tuple(jnp.zeros((8, CB), f32) for _ in range(K5 + 1))
        for s0, ln in segs:
            for ref in (base_ref, dbase_ref):
                _zero_rows(ref, 0, PAD5)
                _zero_rows(ref, PAD5 + ln, PAD5)
            base_ref[pl.ds(PAD5, ln), :] = u_ref[pl.ds(s0, ln), :]

            def tile1(i, carry):
                r = i * RTB
                dy = dya_ref[_rowsb(r, s0), :] + dyb_ref[_rowsb(r, s0), :] + has_skip * dsk_ref[_rowsb(r, s0), :]
                dpre = dy * ds_ref[_rowsb(r, s0), :]
                dbase_ref[_rowsb(r, PAD5), :] = dpre
                taps = _taps5(base_ref, r, RTB, 1)
                new = [carry[k] + _fold8(dpre * taps[k]) for k in range(K5)]
                new.append(carry[K5] + _fold8(dpre))
                return tuple(new)

            acc8 = lax.fori_loop(0, ln // RTB, tile1, acc8, unroll=2)

            def tile2(i, carry):
                r = i * RTB
                taps = _taps5(dbase_ref, r, RTB, -1)
                du = jnp.zeros((RTB, CB), f32)
                for k in range(K5):
                    du = du + taps[k] * wv[k]
                du_ref[_rowsb(r, s0), :] = du.astype(bf16)
                return carry

            lax.fori_loop(0, ln // RTB, tile2, 0, unroll=4)
        for k in range(K5):
            dw_ref[pl.ds(k, 1), :] = jnp.sum(acc8[k], axis=0, keepdims=True)
        db_ref[...] = jnp.sum(acc8[K5], axis=0, keepdims=True)

    cblk = pl.BlockSpec((T, CB), lambda j: (0, j))
    return pl.pallas_call(
        body, grid=(CONVD // CB,),
        in_specs=[cblk, pl.BlockSpec((K5, CB), lambda j: (0, j)), cblk,
                  pl.BlockSpec((None, T, CB), lambda j: (0, 0, j)), pl.BlockSpec((None, T, CB), lambda j: (1, 0, j)),
                  pl.BlockSpec((T, CB), lambda j: (0, jnp.minimum(j, nskip - 1))),
                  pl.BlockSpec((1, CB), lambda j: (0, jnp.minimum(j, nskip - 1)))],
        out_specs=[cblk, pl.BlockSpec((K5, CB), lambda j: (0, j)), pl.BlockSpec((1, CB), lambda j: (0, j))],
        out_shape=[S((T, CONVD), bf16), S((K5, CONVD), f32), S((1, CONVD), f32)],
        scratch_shapes=[pltpu.VMEM((maxlen + 2 * PAD5, CB), f32), pltpu.VMEM((maxlen + 2 * PAD5, CB), f32)],
        compiler_params=_cparams("parallel"), name=name)(proj, w, dsilu, dy2, dy2, dyskip, dexp)


GPAD = GRID_W


def _grid_copies(g_ref, src, L):
    col = lax.broadcasted_iota(jnp.int32, (L, CB), 0) & (GRID_W - 1)
    for d in range(3):
        _zero_rows(g_ref.at[d], 0, GPAD)
        _zero_rows(g_ref.at[d], GPAD + L, GPAD)
    g_ref[1, pl.ds(GPAD, L), :] = src
    g_ref[0, pl.ds(GPAD, L), :] = jnp.where(col != 0, g_ref[1, pl.ds(GPAD - 1, L), :], 0.0)
    g_ref[2, pl.ds(GPAD, L), :] = jnp.where(col != GRID_W - 1, g_ref[1, pl.ds(GPAD + 1, L), :], 0.0)


def ffn_gate_fwd(val, gate, cw, cb_, name, hosted=None):
    L = val.shape[0]
    nb = FH // CB

    def body(val_ref, gate_ref, w_ref, b_ref, o_ref, s_ref, vds_ref, g_ref):
        wv = [w_ref[pl.ds(t, 1), :] for t in range(9)]
        bv = b_ref[...]
        _grid_copies(g_ref, gate_ref[...], L)

        def tile(i, carry):
            r = i * RT
            acc = jnp.broadcast_to(bv, (RT, CB))
            for dr in range(3):
                for dc in range(3):
                    acc = acc + g_ref[dc, _rows(r, GPAD + (dr - 1) * GRID_W), :] * wv[3 * dr + dc]
            sg = _sigmoid(acc)
            s = acc * sg
            v = val_ref[_rows(r), :]
            o_ref[_rows(r), :] = (s * v).astype(bf16)
            s_ref[_rows(r), :] = s
            vds_ref[_rows(r), :] = v * (sg * (1.0 + acc * (1.0 - sg)))
            return carry

        lax.fori_loop(0, L // RT, tile, 0, unroll=2)

    cblk = pl.BlockSpec((L, CB), lambda j: (0, j))
    (out, s_, vds), extra = _host_call(
        body, (nb,), [cblk, cblk, pl.BlockSpec((9, CB), lambda j: (0, j)), pl.BlockSpec((1, CB), lambda j: (0, j))],
        [cblk, cblk, cblk], [S((L, FH), bf16), S((L, FH), f32), S((L, FH), f32)],
        [pltpu.VMEM((3, L + 2 * GPAD, CB), f32)], ("parallel",), name, (val, gate, cw, cb_), hosted)
    return out, s_, vds, extra


def ffn_gate_bwd(gate, s_, vds, cw, dact, name):
    L = gate.shape[0]
    nb = FH // CB

    def body(gate_ref, s_ref, vds_ref, w_ref, da_ref, dval_ref, dgate_ref, dw_ref, db_ref, g_ref, d_ref):
        wv = [w_ref[pl.ds(t, 1), :] for t in range(9)]
        _grid_copies(g_ref, gate_ref[...], L)

        def tile1(i, carry):
            r = i * RTB
            da = da_ref[_rowsb(r), :].astype(f32)
            dval_ref[_rowsb(r), :] = (da * s_ref[_rowsb(r), :]).astype(bf16)
            dpre = da * vds_ref[_rowsb(r), :]
            d_ref[_rowsb(r), :] = dpre
            new = [carry[t] + _fold8(dpre * g_ref[t % 3, _rowsb(r, GPAD + (t // 3 - 1) * GRID_W), :]) for t in range(9)]
            new.append(carry[9] + _fold8(dpre))
            return tuple(new)

        acc8 = lax.fori_loop(0, L // RTB, tile1, tuple(jnp.zeros((8, CB), f32) for _ in range(10)), unroll=2)
        for t in range(9):
            dw_ref[pl.ds(t, 1), :] = jnp.sum(acc8[t], axis=0, keepdims=True)
        db_ref[...] = jnp.sum(acc8[9], axis=0, keepdims=True)
        _grid_copies(g_ref, d_ref[...], L)

        def tile2(i, carry):
            r = i * RTB
            dg = jnp.zeros((RTB, CB), f32)
            for dr in range(3):
                for dc in range(3):
                    dg = dg + g_ref[2 - dc, _rowsb(r, GPAD - (dr - 1) * GRID_W), :] * wv[3 * dr + dc]
            dgate_ref[_rowsb(r), :] = dg.astype(bf16)
            return carry

        lax.fori_loop(0, L // RTB, tile2, 0, unroll=4)

    cblk = pl.BlockSpec((L, CB), lambda j: (0, j))
    return pl.pallas_call(
        body, grid=(nb,),
        in_specs=[cblk, cblk, cblk, pl.BlockSpec((9, CB), lambda j: (0, j)), cblk],
        out_specs=[cblk, cblk, pl.BlockSpec((9, CB), lambda j: (0, j)), pl.BlockSpec((1, CB), lambda j: (0, j))],
        out_shape=[S((L, FH), bf16), S((L, FH), bf16), S((9, FH), f32), S((1, FH), f32)],
        scratch_shapes=[pltpu.VMEM((3, L + 2 * GPAD, CB), f32), pltpu.VMEM((L, CB), f32)],
        compiler_params=_cparams("parallel"), name=name)(gate, s_, vds, cw, dact)


CONF_K = 31
CHALF = CONF_K // 2
CPAD = 16


def _shift_copies8(c_ref, base_ref, L):
    n = L + 2 * CPAD - 8
    for b_ in range(8):
        c_ref[b_, pl.ds(0, n), :] = base_ref[pl.ds(b_, n), :]


def _tap_ab(o):
    return o % 8, o - o % 8


def conf_glu_conv_fwd(pa, pg, b1, wdw, bdw, name, hosted=None):
    L = pa.shape[0]
    nb = D // CB

    def body(pa_ref, pg_ref, ba_ref, bg_ref, w_ref, bdw_ref, o_ref, base_ref, c_ref):
        _zero_rows(base_ref, 0, CPAD)
        _zero_rows(base_ref, CPAD + L, CPAD)
        base_ref[pl.ds(CPAD, L), :] = (pa_ref[...] + ba_ref[...]) * _sigmoid(pg_ref[...] + bg_ref[...])
        _shift_copies8(c_ref, base_ref, L)
        bv = bdw_ref[...]

        def tile(i, carry):
            r = i * RT
            acc = jnp.broadcast_to(bv, (RT, CB))
            for k in range(CONF_K):
                b_, a8 = _tap_ab(k - CHALF)
                acc = acc + c_ref[b_, _rows(r, CPAD + a8), :] * w_ref[pl.ds(k, 1), :]
            o_ref[_rows(r), :] = acc
            return carry

        lax.fori_loop(0, L // RT, tile, 0, unroll=2)

    cblk = pl.BlockSpec((L, CB), lambda j: (0, j))
    rblk = pl.BlockSpec((1, CB), lambda j: (0, j))
    rgblk = pl.BlockSpec((1, CB), lambda j: (0, nb + j))
    (out,), extra = _host_call(
        body, (nb,), [cblk, cblk, rblk, rgblk, pl.BlockSpec((CONF_K, CB), lambda j: (0, j)), rblk],
        [cblk], [S((L, D), f32)], [pltpu.VMEM((L + 2 * CPAD, CB), f32), pltpu.VMEM((8, L + 2 * CPAD, CB), f32)],
        ("parallel",), name, (pa, pg, b1, b1, wdw, bdw), hosted)
    return out, extra


def conf_glu_conv_bwd(pa, pg, b1, wdw, dy, name):
    L = pa.shape[0]
    nb = D // CB

    def body(pa_ref, pg_ref, ba_ref, bg_ref, w_ref, dy_ref, dpa_ref, dpg_ref, dba_ref, dbg_ref, dw_ref, dbdw_ref,
             base_ref, c_ref, acc_ref):
        _zero_rows(base_ref, 0, CPAD)
        _zero_rows(base_ref, CPAD + L, CPAD)
        base_ref[pl.ds(CPAD, L), :] = (pa_ref[...] + ba_ref[...]) * _sigmoid(pg_ref[...] + bg_ref[...])
        _shift_copies8(c_ref, base_ref, L)
        acc_ref[...] = jnp.zeros_like(acc_ref)

        def tile1(i, carry):
            r = i * RTB
            dyt = dy_ref[_rowsb(r), :]
            for k in range(CONF_K):
                b_, a8 = _tap_ab(k - CHALF)
                acc_ref[k] += _fold8(dyt * c_ref[b_, _rowsb(r, CPAD + a8), :])
            return carry + _fold8(dyt)

        db8 = lax.fori_loop(0, L // RTB, tile1, jnp.zeros((8, CB), f32), unroll=2)
        dbdw_ref[...] = jnp.sum(db8, axis=0, keepdims=True)
        for k in range(CONF_K):
            dw_ref[pl.ds(k, 1), :] = jnp.sum(acc_ref[k], axis=0, keepdims=True)
        base_ref[pl.ds(CPAD, L), :] = dy_ref[...]
        _shift_copies8(c_ref, base_ref, L)
        ba = ba_ref[...]
        bg = bg_ref[...]

        def tile2(i, carry):
            r = i * RTB
            dglu = jnp.zeros((RTB, CB), f32)
            for k in range(CONF_K):
                b_, a8 = _tap_ab(CHALF - k)
                dglu = dglu + c_ref[b_, _rowsb(r, CPAD + a8), :] * w_ref[pl.ds(k, 1), :]
            a = pa_ref[_rowsb(r), :] + ba
            sg = _sigmoid(pg_ref[_rowsb(r), :] + bg)
            dpa = dglu * sg
            dpg = dglu * a * (sg * (1.0 - sg))
            dpa_ref[_rowsb(r), :] = dpa.astype(bf16)
            dpg_ref[_rowsb(r), :] = dpg.astype(bf16)
            return carry[0] + _fold8(dpa), carry[1] + _fold8(dpg)

        s8 = lax.fori_loop(0, L // RTB, tile2, (jnp.zeros((8, CB), f32), jnp.zeros((8, CB), f32)), unroll=2)
        dba_ref[...] = jnp.sum(s8[0], axis=0, keepdims=True)
        dbg_ref[...] = jnp.sum(s8[1], axis=0, keepdims=True)

    cblk = pl.BlockSpec((L, CB), lambda j: (0, j))
    rblk = pl.BlockSpec((1, CB), lambda j: (0, j))
    rgblk = pl.BlockSpec((1, CB), lambda j: (0, nb + j))
    wblk = pl.BlockSpec((CONF_K, CB), lambda j: (0, j))
    return pl.pallas_call(
        body, grid=(nb,), in_specs=[cblk, cblk, rblk, rgblk, wblk, cblk],
        out_specs=[cblk, cblk, rblk, rblk, wblk, rblk],
        out_shape=[S((L, D), bf16), S((L, D), bf16), S((1, D), f32), S((1, D), f32), S((CONF_K, D), f32), S((1, D), f32)],
        scratch_shapes=[pltpu.VMEM((L + 2 * CPAD, CB), f32), pltpu.VMEM((8, L + 2 * CPAD, CB), f32),
                        pltpu.VMEM((CONF_K, 8, CB), f32)],
        compiler_params=_cparams("parallel"), name=name)(pa, pg, b1, b1, wdw, dy)


def _ln_silu_f(x, w, b):
    mu = jnp.mean(x, axis=-1, keepdims=True)
    d = x - mu
    y = d * lax.rsqrt(jnp.mean(d * d, axis=-1, keepdims=True) + EPS) * w + b
    return y * _sigmoid(y)


def ln_silu_fwd(x, w, b, name):
    T = x.shape[0]

    def body(x_ref, w_ref, b_ref, o_ref):
        o_ref[...] = _ln_silu_f(x_ref[...], w_ref[...], b_ref[...]).astype(bf16)

    blk = pl.BlockSpec((TB, D), lambda i: (i, 0))
    row = pl.BlockSpec((1, D), lambda i: (0, 0))
    return pl.pallas_call(body, grid=(T // TB,), in_specs=[blk, row, row], out_specs=blk, out_shape=S((T, D), bf16),
                          compiler_params=_cparams("parallel"), name=name)(x, w, b)


def ln_silu_bwd(x, w, b, ds, name):
    T = x.shape[0]

    def body(x_ref, w_ref, b_ref, ds_ref, dx_ref, dw_ref, db_ref):
        i = pl.program_id(0)
        _, vjp = jax.vjp(_ln_silu_f, x_ref[...], w_ref[...], b_ref[...])
        dx, dw, db = vjp(ds_ref[...].astype(f32))
        dx_ref[...] = dx

        @pl.when(i == 0)
        def _():
            dw_ref[...] = jnp.zeros_like(dw_ref)
            db_ref[...] = jnp.zeros_like(db_ref)

        dw_ref[...] += dw
        db_ref[...] += db

    blk = pl.BlockSpec((TB, D), lambda i: (i, 0))
    row = pl.BlockSpec((1, D), lambda i: (0, 0))
    return pl.pallas_call(body, grid=(T // TB,), in_specs=[blk, row, row, blk], out_specs=[blk, row, row],
                          out_shape=[S((T, D), f32), S((1, D), f32), S((1, D), f32)],
                          compiler_params=_cparams("arbitrary"), name=name)(x, w, b, ds)


def _mxu(a, b, dims):
    return lax.dot_general(a.astype(bf16), b.astype(bf16), (dims, ((), ())), preferred_element_type=f32)


def _nn(a, b):
    return _mxu(a, b, ((1,), (0,)))


def _nt(a, b):
    return _mxu(a, b, ((1,), (1,)))


def _tn(a, b):
    return _mxu(a, b, ((0,), (0,)))


@jax.custom_vjp
def _dot_nn(a, b):
    return _nn(a, b)


@jax.custom_vjp
def _dot_nt(a, b):
    return _nt(a, b)


@jax.custom_vjp
def _dot_tn(a, b):
    return _tn(a, b)


_dot_nn.defvjp(lambda a, b: (_nn(a, b), (a, b)), lambda res, g: (_nt(g, res[1]), _tn(res[0], g)))
_dot_nt.defvjp(lambda a, b: (_nt(a, b), (a, b)), lambda res, g: (_nn(g, res[1]), _tn(g, res[0])))
_dot_tn.defvjp(lambda a, b: (_tn(a, b), (a, b)), lambda res, g: (_nt(res[1], g), _nn(res[0], g)))


def _exact_dot(a, b, dims, split_first):
    v = a if split_first else b
    p1 = v.astype(bf16)
    r1 = v - p1.astype(f32)
    p2 = r1.astype(bf16)
    p3 = (r1 - p2.astype(f32)).astype(bf16)
    out = None
    for p in (p1, p2, p3):
        lhs, rhs = (p, b.astype(bf16)) if split_first else (a.astype(bf16), p)
        t = lax.dot_general(lhs, rhs, (dims, ((), ())), preferred_element_type=f32)
        out = t if out is None else out + t
    return out


@jax.custom_vjp
def _masked_sum_cols(mf, a):
    return _exact_dot(mf, a, ((1,), (0,)), False)


@jax.custom_vjp
def _masked_sum_rows(mf, a):
    return _exact_dot(a, mf, ((1,), (1,)), True)


_masked_sum_cols.defvjp(lambda mf, a: (_exact_dot(mf, a, ((1,), (0,)), False), mf),
                        lambda mf, g: (jnp.zeros_like(mf), _exact_dot(mf, g, ((0,), (0,)), False)))
_masked_sum_rows.defvjp(lambda mf, a: (_exact_dot(a, mf, ((1,), (1,)), True), mf),
                        lambda mf, g: (jnp.zeros_like(mf), _exact_dot(g, mf, ((1,), (0,)), True)))


def _masked_sum(mf, a, rows):
    return _masked_sum_rows(mf, a) if rows else _masked_sum_cols(mf, a)


def _lanes_to_rows(v):
    r = lax.broadcasted_iota(jnp.int32, (GW, GW), 0)
    c = lax.broadcasted_iota(jnp.int32, (GW, GW), 1)
    return jnp.sum(jnp.where(r == c, jnp.broadcast_to(v, (GW, GW)), 0.0), axis=1, keepdims=True)


def _ssd_chunk(x, B, C, dtc, dtr, bc, br, alc, alr, s_in, is_fwd):
    row = lax.broadcasted_iota(jnp.int32, (Q, Q), 0)
    col = lax.broadcasted_iota(jnp.int32, (Q, Q), 1)
    sgn = jnp.where(is_fwd, 1, -1).astype(jnp.int32)
    mask = (row - col) * sgn >= 0
    mf = mask.astype(f32)
    lane_head = lax.broadcasted_iota(jnp.int32, (1, GW), 1) // P

    def spread(v):
        out = jnp.zeros((v.shape[0], GW), f32)
        for r in range(HPG):
            out = jnp.where(lane_head == r, v[:, r:r + 1], out)
        return out

    dt_c = _softplus(dtc + bc)
    dt_r = _softplus(dtr + br)
    a_c = dt_c * (-jnp.exp(alc))
    a_r = dt_r * (-jnp.exp(alr))
    acum_c = _masked_sum(mf, a_c, False)
    acum_r = _masked_sum(mf, a_r, True)
    tot_c = jnp.sum(a_c, axis=0, keepdims=True)
    dt_e = spread(dt_c)
    acum_e = spread(acum_c)
    tot_e = spread(tot_c)
    xdt = x * dt_e
    cb = _dot_nt(C, B)
    scores, xs = [], []
    for r in range(HPG):
        seg = acum_c[:, r:r + 1] - acum_r[r:r + 1, :]
        scores.append(cb * jnp.exp(jnp.where(mask, seg, -jnp.inf)))
        xs.append(jnp.where(lane_head == r, xdt, 0.0))
    y = _dot_nn(jnp.concatenate(scores, axis=1), jnp.concatenate(xs, axis=0))
    y = y + _dot_nt(C, s_in) * jnp.exp(acum_e)
    xe = xdt * jnp.exp(tot_e - acum_e)
    s_out = _lanes_to_rows(jnp.exp(tot_e)) * s_in + _dot_tn(xe, B)
    return y, s_out


def _chunk_index(d, t, nctx, nc):
    bwd = jnp.where(t < nctx, nctx - 1 - t, nc - 1 - (t - nctx))
    return jnp.where(d == 0, t, bwd)


def _ssd_in_specs(ci):
    small_c = pl.BlockSpec((None, G, 1, HPG), lambda d, t: (d, 0, 0, 0))
    small_r = pl.BlockSpec((None, G, HPG, 1), lambda d, t: (d, 0, 0, 0))
    return [
        pl.BlockSpec((Q, CONVD), lambda d, t: (ci(d, t), 0)),
        pl.BlockSpec((None, G, Q, HPG), lambda d, t: (d, 0, ci(d, t), 0)),
        pl.BlockSpec((None, G, HPG, Q), lambda d, t: (d, 0, 0, ci(d, t))),
        small_c, small_r, small_c, small_r,
    ]


def _group_cols(g):
    return pl.ds(g * GW, GW), pl.ds(DI + g * N, N), pl.ds(DI + G * N + g * N, N)


def ssd_scan_fwd(xbc, dtc, dtr, bc, br, alc, alr, nctx, name, hosted=None):
    T = xbc.shape[0]
    nc = T // Q

    def body(xbc_ref, dtc_ref, dtr_ref, bc_ref, br_ref, alc_ref, alr_ref, y_ref, sin_ref, st_ref):
        d = pl.program_id(0)
        t = pl.program_id(1)

        @pl.when(t == 0)
        def _():
            st_ref[...] = jnp.zeros_like(st_ref)

        for g in range(G):
            xs, bs, cs = _group_cols(g)
            s_in = st_ref[g]
            sin_ref[g] = s_in
            y, s_out = _ssd_chunk(xbc_ref[:, xs], xbc_ref[:, bs], xbc_ref[:, cs], dtc_ref[g], dtr_ref[g], bc_ref[g], br_ref[g],
                                  alc_ref[g], alr_ref[g], s_in, d == 0)
            y_ref[:, xs] = y
            st_ref[g] = s_out

    ci = lambda d, t: _chunk_index(d, t, nctx, nc)
    out_specs = [
        pl.BlockSpec((None, Q, DI), lambda d, t: (d, ci(d, t), 0)),
        pl.BlockSpec((None, None, G, GW, N), lambda d, t: (d, ci(d, t), 0, 0, 0)),
    ]
    return _host_call(
        body, (2, nc), _ssd_in_specs(ci), out_specs, [S((2, T, DI), f32), S((2, nc, G, GW, N), f32)],
        [pltpu.VMEM((G, GW, N), f32)], ("arbitrary", "arbitrary"), name, (xbc, dtc, dtr, bc, br, alc, alr), hosted)


def ssd_scan_bwd(xbc, dtc, dtr, bc, br, alc, alr, s_in_all, dy, nctx, name, hosted=None):
    T = xbc.shape[0]
    nc = T // Q

    def body(xbc_ref, dtc_ref, dtr_ref, bc_ref, br_ref, alc_ref, alr_ref, sin_ref, dy_ref,
             dxbc_ref, ddtc_ref, ddtr_ref, dbc_ref, dbr_ref, dalc_ref, dalr_ref, ds_ref):
        d = pl.program_id(0)
        t = pl.program_id(1)

        @pl.when(t == 0)
        def _():
            ds_ref[...] = jnp.zeros_like(ds_ref)
            dbc_ref[...] = jnp.zeros_like(dbc_ref)
            dbr_ref[...] = jnp.zeros_like(dbr_ref)
            dalc_ref[...] = jnp.zeros_like(dalc_ref)
            dalr_ref[...] = jnp.zeros_like(dalr_ref)

        f = functools.partial(_ssd_chunk, is_fwd=(d == 0))
        for g in range(G):
            xs, bs, cs = _group_cols(g)
            _, vjp = jax.vjp(f, xbc_ref[:, xs], xbc_ref[:, bs], xbc_ref[:, cs], dtc_ref[g], dtr_ref[g], bc_ref[g], br_ref[g],
                             alc_ref[g], alr_ref[g], sin_ref[g])
            dx, dB, dC, ddtc, ddtr, dbc, dbr, dalc, dalr, ds = vjp((dy_ref[:, xs], ds_ref[g]))
            dxbc_ref[:, xs] = dx
            dxbc_ref[:, bs] = dB
            dxbc_ref[:, cs] = dC
            ddtc_ref[g] = ddtc
            ddtr_ref[g] = ddtr
            dbc_ref[g] += dbc
            dbr_ref[g] += dbr
            dalc_ref[g] += dalc
            dalr_ref[g] += dalr
            ds_ref[g] = ds

    ci = lambda d, t: _chunk_index(d, nc - 1 - t, nctx, nc)
    in_specs = _ssd_in_specs(ci) + [
        pl.BlockSpec((None, None, G, GW, N), lambda d, t: (d, ci(d, t), 0, 0, 0)),
        pl.BlockSpec((Q, DI), lambda d, t: (ci(d, t), 0)),
    ]
    small_c = pl.BlockSpec((None, G, 1, HPG), lambda d, t: (d, 0, 0, 0))
    small_r = pl.BlockSpec((None, G, HPG, 1), lambda d, t: (d, 0, 0, 0))
    out_specs = [
        pl.BlockSpec((None, Q, CONVD), lambda d, t: (d, ci(d, t), 0)),
        pl.BlockSpec((None, G, Q, HPG), lambda d, t: (d, 0, ci(d, t), 0)),
        pl.BlockSpec((None, G, HPG, Q), lambda d, t: (d, 0, 0, ci(d, t))),
        small_c, small_r, small_c, small_r,
    ]
    out_shape = [S((2, T, CONVD), f32), S((2, G, T, HPG), f32), S((2, G, HPG, T), f32),
                 S((2, G, 1, HPG), f32), S((2, G, HPG, 1), f32), S((2, G, 1, HPG), f32), S((2, G, HPG, 1), f32)]
    return _host_call(body, (2, nc), in_specs, out_specs, out_shape, [pltpu.VMEM((G, GW, N), f32)],
                      ("arbitrary", "arbitrary"), name, (xbc, dtc, dtr, bc, br, alc, alr, s_in_all, dy), hosted)


GTB = 128


def _gate_norm_f(yf, yb, x, z, dexp, w):
    y = (yf + yb + dexp * x) * (z * _sigmoid(z))
    return y * lax.rsqrt(jnp.mean(y * y, axis=-1, keepdims=True) + EPS) * w


def ssd_gate_fwd(y2, xbc, proj, dexp, w, nctxb, name):
    T = xbc.shape[0]
    L = T - nctxb * GTB

    def body(yf_ref, yb_ref, x_ref, z_ref, d_ref, w_ref, o_ref):
        o_ref[...] = _gate_norm_f(yf_ref[...], yb_ref[...], x_ref[...], z_ref[...], d_ref[...], w_ref[...]).astype(bf16)

    wide = pl.BlockSpec((GTB, DI), lambda i: (i + nctxb, 0))
    row = pl.BlockSpec((1, DI), lambda i: (0, 0))
    return pl.pallas_call(
        body, grid=(L // GTB,),
        in_specs=[pl.BlockSpec((None, GTB, DI), lambda i: (0, i + nctxb, 0)),
                  pl.BlockSpec((None, GTB, DI), lambda i: (1, i + nctxb, 0)), wide, wide, row, row],
        out_specs=pl.BlockSpec((GTB, DI), lambda i: (i, 0)), out_shape=S((L, DI), bf16),
        compiler_params=_cparams("parallel"), name=name)(y2, y2, xbc, proj, dexp, w)


def ssd_gate_bwd(y2, xbc, proj, dexp, w, dyn, nctxb, name, hosted=None):
    T = xbc.shape[0]
    nb = T // GTB

    def body(yf_ref, yb_ref, x_ref, z_ref, d_ref, w_ref, dyn_ref, dy_ref, dz_ref, dd_ref, dw_ref):
        i = pl.program_id(0)

        @pl.when(i == 0)
        def _():
            dd_ref[...] = jnp.zeros_like(dd_ref)
            dw_ref[...] = jnp.zeros_like(dw_ref)

        @pl.when(i < nctxb)
        def _():
            dy_ref[...] = jnp.zeros_like(dy_ref)
            dz_ref[...] = jnp.zeros_like(dz_ref)

        @pl.when(i >= nctxb)
        def _():
            _, vjp = jax.vjp(_gate_norm_f, yf_ref[...], yb_ref[...], x_ref[...], z_ref[...], d_ref[...], w_ref[...])
            dyf, _, _, dz, dd, dw = vjp(dyn_ref[...].astype(f32))
            dy_ref[...] = dyf
            dz_ref[...] = dz.astype(bf16)
            fold = (lax.broadcasted_iota(jnp.int32, (DI, 128), 0) // P == lax.broadcasted_iota(jnp.int32, (DI, 128), 1))
            dd_ref[...] += jnp.dot(dd, fold.astype(f32), precision=HI, preferred_element_type=f32)
            dw_ref[...] += dw

    wide = pl.BlockSpec((GTB, DI), lambda i: (i, 0))
    row = pl.BlockSpec((1, DI), lambda i: (0, 0))
    hrow = pl.BlockSpec((1, 128), lambda i: (0, 0))
    return _host_call(
        body, (nb,),
        [pl.BlockSpec((None, GTB, DI), lambda i: (0, i, 0)), pl.BlockSpec((None, GTB, DI), lambda i: (1, i, 0)),
         wide, wide, row, row, pl.BlockSpec((GTB, DI), lambda i: (jnp.maximum(i - nctxb, 0), 0))],
        [wide, wide, hrow, row],
        [S((T, DI), f32), S((T, DI), bf16), S((1, 128), f32), S((1, DI), f32)],
        [], ("arbitrary",), name, (y2, y2, xbc, proj, dexp, w, dyn), hosted)


CROWS = 2 * N_DEV


def mod_fwd(c16, modw, name):
    nl, _, cols = modw.shape

    def body(c_ref, w_ref, o_ref):
        cv = c_ref[...]
        s = cv * _sigmoid(cv)
        for l in range(nl):
            o_ref[l] = jnp.dot(s, w_ref[l], precision=HI, preferred_element_type=f32)

    return pl.pallas_call(body, in_specs=[VMEM, VMEM], out_specs=VMEM, out_shape=S((nl, CROWS, cols), f32),
                          compiler_params=pltpu.CompilerParams(vmem_limit_bytes=VMEM_LIMIT_BYTES), name=name)(c16, modw)


def mod_bwd(c16, modw, dm_sh, dm_all, name):
    nl, _, cols = modw.shape

    def body(c_ref, w_ref, dm_ref, dmall_ref, dw_ref, dc_ref, db_ref):
        cv = c_ref[...]
        sg = _sigmoid(cv)
        s = cv * sg
        ds_dc = sg * (1.0 + cv * (1.0 - sg))
        is_ctx = lax.broadcasted_iota(jnp.int32, (CROWS, D), 0) >= N_DEV
        dc = jnp.zeros((1, D), f32)
        for l in range(nl):
            dm = dm_ref[l]
            dw_ref[l] = lax.dot_general(s, dm, (((0,), (0,)), ((), ())), precision=HI, preferred_element_type=f32)
            dsv = lax.dot_general(dm, w_ref[l], (((1,), (1,)), ((), ())), precision=HI, preferred_element_type=f32)
            dc = dc + jnp.sum(jnp.where(is_ctx, dsv * ds_dc, 0.0), axis=0, keepdims=True)
            db_ref[pl.ds(l, 1), :] = jnp.sum(dmall_ref[l], axis=0, keepdims=True)
        dc_ref[...] = dc

    return pl.pallas_call(
        body, in_specs=[VMEM, VMEM, VMEM, VMEM], out_specs=[VMEM, VMEM, VMEM],
        out_shape=[S(modw.shape, f32), S((1, D), f32), S((nl, 6 * D), f32)],
        compiler_params=pltpu.CompilerParams(vmem_limit_bytes=VMEM_LIMIT_BYTES), name=name)(c16, modw, dm_sh, dm_all)


def adamw(w, g, m, v, name):
    R, C = w.shape
    rb = R if R <= 512 else max(r_ for r_ in range(8, 513, 8) if R % r_ == 0)
    bc1 = 1.0 - ADAM_B1 ** ADAM_STEP
    bc2 = 1.0 - ADAM_B2 ** ADAM_STEP

    def body(w_ref, g_ref, m_ref, v_ref, d_ref, nm_ref, nv_ref):
        gv = g_ref[...]
        m_new = ADAM_B1 * m_ref[...] + (1.0 - ADAM_B1) * gv
        v_new = ADAM_B2 * v_ref[...] + (1.0 - ADAM_B2) * (gv * gv)
        m_hat = m_new / bc1
        v_hat = v_new / bc2
        d_ref[...] = -ADAM_LR * (m_hat / (jnp.sqrt(v_hat) + ADAM_EPS) + ADAM_WD * w_ref[...])
        nm_ref[...] = m_new
        nv_ref[...] = v_new

    blk = pl.BlockSpec((rb, C), lambda i: (i, 0))
    return pl.pallas_call(body, grid=(R // rb,), in_specs=[blk] * 4, out_specs=[blk] * 3,
                          out_shape=[S((R, C), f32)] * 3, compiler_params=_cparams("parallel"), name=name)(w, g, m, v)


def _me():
    return lax.axis_index("x"), lax.axis_index("y"), lax.axis_index("c")


def allgather_small(x, name, with_sum=False, after=None):
    r, w = x.shape
    extra = () if after is None else (after,)

    def body(x_ref, *refs):
        refs = refs[len(extra):]
        if with_sum:
            out_ref, sum_ref, send_sems, recv_sems = refs
        else:
            out_ref, send_sems, recv_sems = refs
        mx, my, mc = _me()
        me = 4 * mx + 2 * my + mc
        out_ref[me] = x_ref[...]
        peers = []
        for k in range(1, N_DEV):
            kx, ky, kc = (k >> 2) & 1, (k >> 1) & 1, k & 1
            peers.append((mx + kx - 2 * mx * kx, my + ky - 2 * my * ky, mc + kc - 2 * mc * kc))
        copies = []
        for k, peer in enumerate(peers):
            cp = pltpu.make_async_remote_copy(src_ref=x_ref, dst_ref=out_ref.at[me], send_sem=send_sems.at[k],
                                              recv_sem=recv_sems.at[k], device_id=peer, device_id_type=MESH)
            cp.start()
            copies.append(cp)
        for k, (px, py, pc) in enumerate(peers):
            pltpu.make_async_remote_copy(src_ref=x_ref, dst_ref=out_ref.at[4 * px + 2 * py + pc], send_sem=send_sems.at[k],
                                         recv_sem=recv_sems.at[k], device_id=(px, py, pc), device_id_type=MESH).wait_recv()
        for cp in copies:
            cp.wait_send()
        if with_sum:
            acc = out_ref[0]
            for j in range(1, N_DEV):
                acc = acc + out_ref[j]
            sum_ref[...] = acc

    out_shape = [S((N_DEV, r, w), f32)] + ([S((r, w), f32)] if with_sum else [])
    outs = pl.pallas_call(
        body, in_specs=[VMEM] + [ANY] * len(extra), out_specs=[VMEM] * len(out_shape), out_shape=out_shape,
        scratch_shapes=[pltpu.SemaphoreType.DMA((N_DEV - 1,)), pltpu.SemaphoreType.DMA((N_DEV - 1,))],
        compiler_params=pltpu.CompilerParams(vmem_limit_bytes=VMEM_LIMIT_BYTES), name=name)(x, *extra)
    return outs if with_sum else outs[0]


def _tile2d(R, W, max_rows):
    if R <= max_rows:
        return R, W
    fits = [r_ for r_ in range(16, max_rows + 1, 16) if R % r_ == 0]
    return (max(fits), W) if fits else (R, 256)


def add_own(g, r, core, name, twice=False):
    _, _, R, W = g.shape
    rb, wb = _tile2d(R, W, 512)
    nout = 2 if twice else 1

    def body(core_ref, a_ref, b_ref, *o_refs):
        s = (a_ref[...].astype(f32) + b_ref[...].astype(f32)).astype(bf16)
        for o_ref in o_refs:
            o_ref[...] = s

    blk = pl.BlockSpec((None, rb, wb), lambda k, i, j, core_ref: (k, i, j))
    gs = pltpu.PrefetchScalarGridSpec(
        num_scalar_prefetch=1, grid=(4, R // rb, W // wb),
        in_specs=[pl.BlockSpec((None, None, rb, wb), lambda k, i, j, core_ref: (k, core_ref[0], i, j)), blk],
        out_specs=[blk] * nout)
    outs = pl.pallas_call(body, grid_spec=gs, out_shape=[S((4, R, W), bf16)] * nout,
                          compiler_params=_cparams("parallel", "parallel", "parallel"), name=name)(core, g, r)
    return tuple(outs) if twice else outs[0]


HBM_SPEC = pl.BlockSpec(memory_space=pltpu.HBM)
SEM_SPEC = pl.BlockSpec(memory_space=pltpu.SEMAPHORE)


def _chips_copy(p_ref, land_ref, send_sems, recv_sems, a, j):
    x, y, c = _me()
    px, py = [(1 - x, y), (x, 1 - y), (1 - x, 1 - y)][j]
    return pltpu.make_async_remote_copy(src_ref=p_ref.at[2 * px + py], dst_ref=land_ref.at[2 * x + y],
                                        send_sem=send_sems.at[3 * a + j], recv_sem=recv_sems.at[3 * a + j],
                                        device_id=(px, py, c), device_id_type=MESH)


def _chips_wait_copy(p_ref, land_ref, send_sems, recv_sems, a, j):
    x, y, c = _me()
    px, py = [(1 - x, y), (x, 1 - y), (1 - x, 1 - y)][j]
    return pltpu.make_async_remote_copy(src_ref=p_ref.at[2 * px + py], dst_ref=land_ref.at[2 * px + py],
                                        send_sem=send_sems.at[3 * a + j], recv_sem=recv_sems.at[3 * a + j],
                                        device_id=(px, py, c), device_id_type=MESH)


def _xor_peers():
    mx, my, mc = _me()
    peers = []
    for k in range(1, N_DEV):
        kx, ky, kc = (k >> 2) & 1, (k >> 1) & 1, k & 1
        peers.append((mx + kx - 2 * mx * kx, my + ky - 2 * my * ky, mc + kc - 2 * mc * kc))
    return peers


def gather_start(shards, name, after):
    na = len(shards)
    lands = [lax.empty((N_DEV,) + s_.shape, s_.dtype) for s_ in shards]

    def body(*refs):
        x_refs, land_refs = refs[:na], refs[na:2 * na]
        send_sems, recv_sems, local_sems = refs[2 * na + 1:2 * na + 4]
        token = refs[-1]
        mx, my, mc = _me()
        me = 4 * mx + 2 * my + mc
        for a in range(na):
            pltpu.make_async_copy(x_refs[a], land_refs[a].at[me], local_sems.at[a]).start()
            for k, peer in enumerate(_xor_peers()):
                pltpu.make_async_remote_copy(src_ref=x_refs[a], dst_ref=land_refs[a].at[me], send_sem=send_sems.at[7 * a + k],
                                             recv_sem=recv_sems.at[7 * a + k], device_id=peer, device_id_type=MESH).start()
        token[...] = jnp.zeros_like(token)

    arrs = list(shards) + lands
    outs = pl.pallas_call(
        body, name=name, in_specs=[HBM_SPEC] * (2 * na) + [ANY],
        out_shape=[DMA((7 * na,)), DMA((7 * na,)), DMA((na,))] + [pltpu.HBM(t.shape, t.dtype) for t in arrs]
        + [S((8, 128), f32)],
        out_specs=[SEM_SPEC] * 3 + [HBM_SPEC] * (2 * na) + [VMEM],
        input_output_aliases={k: 3 + k for k in range(2 * na)},
        compiler_params=pltpu.CompilerParams(has_side_effects=pltpu.SideEffectType.DATAFLOW_SIDE_EFFECTING),
    )(*[pltpu.with_memory_space_constraint(t, pltpu.HBM) for t in arrs], after)
    return outs[0], outs[1], outs[2], list(outs[3:3 + na]), list(outs[3 + na:3 + 2 * na]), outs[-1]


def gather_wait(send_sems, recv_sems, local_sems, shards, lands, after, name):
    na = len(shards)

    def body(*refs):
        x_refs, land_refs = refs[:na], refs[na:2 * na]
        ssem, rsem, lsem = refs[2 * na:2 * na + 3]
        mx, my, mc = _me()
        me = 4 * mx + 2 * my + mc
        for a in range(na):
            pltpu.make_async_copy(x_refs[a], land_refs[a].at[me], lsem.at[a]).wait()
            for k, (px, py, pc) in enumerate(_xor_peers()):
                cp = pltpu.make_async_remote_copy(src_ref=x_refs[a], dst_ref=land_refs[a].at[4 * px + 2 * py + pc],
                                                  send_sem=ssem.at[7 * a + k], recv_sem=rsem.at[7 * a + k],
                                                  device_id=(px, py, pc), device_id_type=MESH)
                cp.wait_send()
                cp.wait_recv()

    arrs = list(shards) + list(lands)
    outs = pl.pallas_call(
        body, name=name, in_specs=[HBM_SPEC] * (2 * na) + [SEM_SPEC] * 3 + [ANY],
        out_shape=[pltpu.HBM(t.shape, t.dtype) for t in arrs], out_specs=[HBM_SPEC] * (2 * na),
        input_output_aliases={k: k for k in range(2 * na)},
        compiler_params=pltpu.CompilerParams(has_side_effects=pltpu.SideEffectType.DATAFLOW_SIDE_EFFECTING),
    )(*arrs, send_sems, recv_sems, local_sems, after)
    return list(outs[na:])


def chips_start(parts, lands, name):
    na = len(parts)

    def body(*refs):
        p_refs, land_refs = refs[:na], refs[na:2 * na]
        send_sems, recv_sems = refs[2 * na], refs[2 * na + 1]
        token = refs[-1]
        for a in range(na):
            for j in range(3):
                _chips_copy(p_refs[a], land_refs[a], send_sems, recv_sems, a, j).start()
        token[...] = jnp.zeros_like(token)

    arrs = list(parts) + list(lands)
    outs = pl.pallas_call(
        body, name=name, in_specs=[HBM_SPEC] * (2 * na),
        out_shape=[DMA((3 * na,)), DMA((3 * na,))] + [pltpu.HBM(t.shape, t.dtype) for t in arrs] + [S((8, 128), f32)],
        out_specs=[SEM_SPEC, SEM_SPEC] + [HBM_SPEC] * (2 * na) + [VMEM],
        input_output_aliases={k: 2 + k for k in range(2 * na)},
        compiler_params=pltpu.CompilerParams(has_side_effects=pltpu.SideEffectType.DATAFLOW_SIDE_EFFECTING),
    )(*[pltpu.with_memory_space_constraint(t, pltpu.HBM) for t in arrs])
    return outs[0], outs[1], list(outs[2:2 + na]), list(outs[2 + na:2 + 2 * na]), outs[-1]


def chips_wait(send_sems, recv_sems, parts, lands, after, name):
    na = len(parts)

    def body(*refs):
        p_refs, land_refs = refs[:na], refs[na:2 * na]
        ssem, rsem = refs[2 * na], refs[2 * na + 1]
        for a in range(na):
            for j in range(3):
                cp = _chips_wait_copy(p_refs[a], land_refs[a], ssem, rsem, a, j)
                cp.wait_send()
                cp.wait_recv()

    arrs = list(parts) + list(lands)
    outs = pl.pallas_call(
        body, name=name, in_specs=[HBM_SPEC] * (2 * na) + [SEM_SPEC, SEM_SPEC, ANY],
        out_shape=[pltpu.HBM(t.shape, t.dtype) for t in arrs], out_specs=[HBM_SPEC] * (2 * na),
        input_output_aliases={k: k for k in range(2 * na)},
        compiler_params=pltpu.CompilerParams(has_side_effects=pltpu.SideEffectType.DATAFLOW_SIDE_EFFECTING),
    )(*arrs, send_sems, recv_sems, after)
    return list(outs[na:])


def sum_adamw(recv, w, m, v, layer, name, into=None, after=None):
    _, R, W = recv.shape
    rb, wb = _tile2d(R, W, 256)
    bc1 = 1.0 - ADAM_B1 ** ADAM_STEP
    bc2 = 1.0 - ADAM_B2 ** ADAM_STEP
    n_into = 0 if into is None else 4
    extra = () if after is None else (after,)

    def body(r_ref, w_ref, m_ref, v_ref, *refs):
        g_ref, d_ref, nm_ref, nv_ref = refs[n_into + len(extra):]
        gv = r_ref[0].astype(f32)
        for k in range(1, 4):
            gv = gv + r_ref[k].astype(f32)
        m_new = ADAM_B1 * m_ref[...] + (1.0 - ADAM_B1) * gv
        v_new = ADAM_B2 * v_ref[...] + (1.0 - ADAM_B2) * (gv * gv)
        g_ref[...] = gv
        d_ref[...] = -ADAM_LR * ((m_new / bc1) / (jnp.sqrt(v_new / bc2) + ADAM_EPS) + ADAM_WD * w_ref[...])
        nm_ref[...] = m_new
        nv_ref[...] = v_new

    if layer is None:
        wblk = pl.BlockSpec((rb, wb), lambda i, j: (i, j))
        oshape = S((R, W), f32)
    else:
        wblk = pl.BlockSpec((None, rb, wb), lambda i, j: (layer, i, j))
        oshape = S(w.shape, f32)
    return pl.pallas_call(
        body, grid=(R // rb, W // wb),
        in_specs=[pl.BlockSpec((4, rb, wb), lambda i, j: (0, i, j)), wblk, wblk, wblk] + [ANY] * (n_into + len(extra)),
        out_specs=[wblk] * 4, out_shape=[oshape] * 4, input_output_aliases={4 + k: k for k in range(n_into)},
        compiler_params=_cparams("parallel", "parallel"), name=name)(recv, w, m, v, *(into or ()), *extra)


def sum_rows(a, name):
    K, R, W = a.shape
    rb = _pick(R, (512, 256, 128, 64, 32, 16))

    def body(a_ref, o_ref):
        acc = a_ref[0].astype(f32)
        for k in range(1, K):
            acc = acc + a_ref[k].astype(f32)
        o_ref[...] = acc

    return pl.pallas_call(body, grid=(R // rb,), in_specs=[pl.BlockSpec((K, rb, W), lambda i: (0, i, 0))],
                          out_specs=pl.BlockSpec((rb, W), lambda i: (i, 0)), out_shape=S((R, W), f32),
                          compiler_params=_cparams("parallel"), name=name)(a)


DMA = pltpu.SemaphoreType.DMA


class GatherExchange:
    def __init__(self, arrays):
        self.arrays = list(arrays)
        self.na = len(self.arrays)
        self.out_shape = [S((N_DEV,) + a.shape, a.dtype) for a in self.arrays]
        self.scratch = [DMA((7 * self.na,)), DMA((7 * self.na,)), DMA((self.na,))]

    def ops(self, x_refs, out_refs, sems):
        send_sems, recv_sems, local_sems = sems
        na = self.na
        x, y, c = _me()
        me, sibling = (x, y, c), (x, y, 1 - c)
        chips = [(1 - x, y), (x, 1 - y), (1 - x, 1 - y)]

        def rows(a, px, py, pc):
            return out_refs[a].at[4 * px + 2 * py + pc]

        def copy(a, k, block, to, src=None):
            return pltpu.make_async_remote_copy(
                src_ref=rows(a, *block) if src is None else src, dst_ref=rows(a, *block),
                send_sem=send_sems.at[7 * a + k], recv_sem=recv_sems.at[7 * a + k], device_id=to, device_id_type=MESH)

        def local(a):
            return pltpu.make_async_copy(x_refs[a], rows(a, *me), local_sems.at[a])

        def first(a):
            return [copy(a, 0, me, sibling, src=x_refs[a])] + [copy(a, 1 + j, me, (*chip, c), src=x_refs[a])
                                                                for j, chip in enumerate(chips)]

        def start():
            for a in range(na):
                local(a).start()
                for cp in first(a):
                    cp.start()

        def mid():
            for a in range(na):
                for j, chip in enumerate(chips):
                    copy(a, 1 + j, (*chip, c), me).wait_recv()
                    copy(a, 4 + j, (*chip, c), sibling).start()

        def finish():
            for a in range(na):
                copy(a, 0, sibling, me).wait_recv()
                for j, chip in enumerate(chips):
                    copy(a, 4 + j, (*chip, 1 - c), me).wait_recv()
                for cp in first(a) + [copy(a, 4 + j, (*chip, c), sibling) for j, chip in enumerate(chips)]:
                    cp.wait_send()
                local(a).wait()

        return start, mid, finish


class SiblingExchange:
    def __init__(self, arrays):
        self.arrays = list(arrays)
        self.na = len(self.arrays)
        self.out_shape = [S((4,) + g.shape[2:], g.dtype) for g in self.arrays]
        self.scratch = [DMA((self.na,)), DMA((self.na,))]

    def ops(self, g_refs, out_refs, sems):
        send_sems, recv_sems = sems
        x, y, c = _me()

        def copy(a):
            return pltpu.make_async_remote_copy(src_ref=g_refs[a].at[:, 1 - c], dst_ref=out_refs[a],
                                                send_sem=send_sems.at[a], recv_sem=recv_sems.at[a],
                                                device_id=(x, y, 1 - c), device_id_type=MESH)

        def start():
            for a in range(self.na):
                copy(a).start()

        def finish():
            for a in range(self.na):
                copy(a).wait()

        return start, None, finish


class ChipsExchange:
    def __init__(self, arrays):
        self.arrays = list(arrays)
        self.na = len(self.arrays)
        self.out_shape = [S(p.shape, p.dtype) for p in self.arrays]
        self.scratch = [DMA((3 * self.na,)), DMA((3 * self.na,)), DMA((self.na,))]

    def ops(self, p_refs, out_refs, sems):
        send_sems, recv_sems, local_sems = sems
        x, y, c = _me()
        mine = 2 * x + y
        chips = [(1 - x, y), (x, 1 - y), (1 - x, 1 - y)]

        def local(a):
            return pltpu.make_async_copy(p_refs[a].at[mine], out_refs[a].at[mine], local_sems.at[a])

        def send(a, j):
            px, py = chips[j]
            return pltpu.make_async_remote_copy(src_ref=p_refs[a].at[2 * px + py], dst_ref=out_refs[a].at[mine],
                                                send_sem=send_sems.at[3 * a + j], recv_sem=recv_sems.at[3 * a + j],
                                                device_id=(px, py, c), device_id_type=MESH)

        def recv(a, j):
            px, py = chips[j]
            return pltpu.make_async_remote_copy(src_ref=p_refs[a].at[mine], dst_ref=out_refs[a].at[2 * px + py],
                                                send_sem=send_sems.at[3 * a + j], recv_sem=recv_sems.at[3 * a + j],
                                                device_id=(px, py, c), device_id_type=MESH)

        def start():
            for a in range(self.na):
                local(a).start()
                for j in range(3):
                    send(a, j).start()

        def finish():
            for a in range(self.na):
                for j in range(3):
                    recv(a, j).wait_recv()
                for j in range(3):
                    send(a, j).wait_send()
                local(a).wait()

        return start, None, finish


def exchange(ex, name):
    na = ex.na

    def body(*refs):
        start, mid, finish = ex.ops(refs[:na], refs[na:2 * na], refs[2 * na:])
        start()
        if mid is not None:
            mid()
        finish()

    return pl.pallas_call(body, in_specs=[ANY] * na, out_specs=[ANY] * na, out_shape=ex.out_shape,
                          scratch_shapes=ex.scratch, name=name)(*ex.arrays)


def _host_call(body, grid, in_specs, out_specs, out_shape, scratch_shapes, sem, name, args, hosted):
    if hosted is None:
        res = pl.pallas_call(body, grid=grid, in_specs=in_specs, out_specs=out_specs, out_shape=out_shape,
                             scratch_shapes=scratch_shapes, compiler_params=_cparams(*sem), name=name)(*args)
        return res, None
    n_in, n_out, n_sc, na = len(in_specs), len(out_shape), len(scratch_shapes), hosted.na
    nsteps = 1
    for g_ in grid:
        nsteps *= g_
    mid_step = (3 * nsteps) // 4
    i1 = n_in + na
    i2 = i1 + n_out
    i3 = i2 + na
    i4 = i3 + n_sc

    def wrapped(*refs):
        step = pl.program_id(0)
        for ax in range(1, len(grid)):
            step = step * grid[ax] + pl.program_id(ax)
        start, mid, finish = hosted.ops(refs[n_in:i1], refs[i2:i3], refs[i4:])
        pl.when(step == 0)(start)
        if mid is not None:
            pl.when(step == mid_step)(mid)
        body(*refs[:n_in], *refs[i1:i2], *refs[i3:i4])
        pl.when(step == nsteps - 1)(finish)

    res = pl.pallas_call(
        wrapped, grid=grid, in_specs=list(in_specs) + [ANY] * na, out_specs=list(out_specs) + [ANY] * na,
        out_shape=list(out_shape) + hosted.out_shape, scratch_shapes=list(scratch_shapes) + hosted.scratch,
        compiler_params=_cparams(*(("arbitrary",) * len(grid))), name=name)(*args, *hosted.arrays)
    return res[:n_out], res[n_out:]


PACK_ALIGN = 16 * PACK_W


def _pad_to(v, mult):
    n = v.shape[-1]
    extra = (-n) % mult
    if extra == 0:
        return v
    return jnp.concatenate([v, jnp.zeros(v.shape[:-1] + (extra,), v.dtype)], axis=-1)


def _f32_as_bf16_pairs(v):
    return lax.bitcast_convert_type(v.reshape(-1), bf16).reshape(-1)


def _bf16_pairs_as_f32(v):
    return lax.bitcast_convert_type(v.reshape(v.shape[:-1] + (v.shape[-1] // 2, 2)), f32)


def _col_shards(gw):
    lead = gw.shape[:-1]
    n = gw.shape[-1] // N_DEV
    t = gw.reshape(lead + (N_DEV, n))
    t = jnp.moveaxis(t, -2, 0)
    return t.reshape(N_DEV, -1)


def kernel(x, c, ctx, c_ctx, mod_w, mod_b, norm1_w, norm2_w, ssd_w_in, ssd_conv_w, ssd_conv_b, ssd_dt_bias, ssd_a_log, ssd_d, ssd_norm_w, ssd_w_out, conf_w_pw1, conf_b_pw1, conf_w_dw, conf_b_dw, conf_ln_w, conf_ln_b, conf_w_pw2, conf_b_pw2, ffn_w_up, ffn_conv_w, ffn_conv_b, ffn_w_down, final_norm_w, loss_target, m_c_ctx, m_mod_w, m_mod_b, m_norm1_w, m_norm2_w, m_ssd_w_in, m_ssd_conv_w, m_ssd_conv_b, m_ssd_dt_bias, m_ssd_a_log, m_ssd_d, m_ssd_norm_w, m_ssd_w_out, m_conf_w_pw1, m_conf_b_pw1, m_conf_w_dw, m_conf_b_dw, m_conf_ln_w, m_conf_ln_b, m_conf_w_pw2, m_conf_b_pw2, m_ffn_w_up, m_ffn_conv_w, m_ffn_conv_b, m_ffn_w_down, m_final_norm_w, v_c_ctx, v_mod_w, v_mod_b, v_norm1_w, v_norm2_w, v_ssd_w_in, v_ssd_conv_w, v_ssd_conv_b, v_ssd_dt_bias, v_ssd_a_log, v_ssd_d, v_ssd_norm_w, v_ssd_w_out, v_conf_w_pw1, v_conf_b_pw1, v_conf_w_dw, v_conf_b_dw, v_conf_ln_w, v_conf_ln_b, v_conf_w_pw2, v_conf_b_pw2, v_ffn_w_up, v_ffn_conv_w, v_ffn_conv_b, v_ffn_w_down, v_final_norm_w):
    mx, my, mc = _me()
    me = 4 * mx + 2 * my + mc
    L = x.shape[1]
    LC = ctx.shape[1]
    T = LC + L
    w_in_cols = ssd_w_in.shape[2] * N_DEV
    n_dt = w_in_cols - DI - CONVD

    small = [c[0], ssd_conv_w[0], conf_b_pw1[0], conf_w_dw[0], conf_b_dw[0], conf_ln_w[0], conf_ln_b[0], conf_b_pw2[0],
             ffn_conv_w]
    parts = [_f32_as_bf16_pairs(t) for t in small]
    sizes = [p.shape[0] for p in parts]
    small_flat = _pad_to(jnp.concatenate(parts), PACK_ALIGN).reshape(-1, PACK_W)
    w_in, small_g = exchange(GatherExchange([ssd_w_in[0].astype(bf16), small_flat]), "gather_first")
    w_up, w_down = [None, None], [None, None]
    small_g = small_g.reshape(N_DEV, -1)
    offs = [0]
    for s_ in sizes:
        offs.append(offs[-1] + s_)
    sm = [_bf16_pairs_as_f32(small_g[:, offs[i]:offs[i + 1]]) for i in range(len(sizes))]

    def cols(pc, K):
        return jnp.moveaxis(pc.reshape(N_DEV, K, -1), 0, 1).reshape(K, -1)

    c_all = sm[0]
    conv_w5 = cols(sm[1], 5)
    b_pw1 = sm[2].reshape(1, 2 * D)
    w_dw = cols(sm[3], CONF_K)
    b_dw, ln_w, ln_b, b_pw2 = (sm[i].reshape(1, D) for i in (4, 5, 6, 7))
    fcw = sm[8].reshape(N_DEV, 2, 9, FH // N_DEV)
    ffn_cw = [cols(fcw[:, i].reshape(N_DEV, -1), 9) for i in range(2)]
    in_segs = (DI, CONVD, n_dt)
    up_segs = (FH, FH)
    pw1_segs = (D, D)

    c16 = jnp.concatenate([c_all, jnp.broadcast_to(c_ctx[None, :], (N_DEV, D))], axis=0)
    m_sh = mod_fwd(c16, mod_w, "mod_fwd")
    mod_cols = mod_w.shape[2]
    m_gath = allgather_small(m_sh.reshape(2 * CROWS, mod_cols), "gather_mod")
    fly_a = gather_start([ssd_w_out[0].astype(bf16), ffn_w_up[0].astype(bf16), ffn_w_down[0].astype(bf16)],
                         "gather_a_start", m_gath)
    fly_b = gather_start([conf_w_pw1[0].astype(bf16), conf_w_pw2[0].astype(bf16)], "gather_b_start", fly_a[-1])
    fly_c = gather_start([ffn_w_up[1].astype(bf16), ffn_w_down[1].astype(bf16)], "gather_c_start", fly_b[-1])
    m_all = jnp.moveaxis(m_gath.reshape(N_DEV, 2, CROWS, mod_cols), 0, 2).reshape(2, CROWS, 6 * D) + mod_b[:, None, :]
    m_all = m_all + fly_c[-1][0, 0]
    m_lat = lax.dynamic_index_in_dim(m_all, me, axis=1, keepdims=False).reshape(2, 6, 1, D)
    m_ctx = m_all[:, N_DEV].reshape(2, 6, 1, D)
    zero_row = jnp.zeros((1, D), f32)

    def ffn_fwd(a2, i, tag):
        val, gate = smm_fwd(a2, w_up[i], None, up_segs, f"ffn{tag}_up")
        act, gs_, gvds, _ = ffn_gate_fwd(val, gate, ffn_cw[i], ffn_conv_b[i][None], f"ffn{tag}_gate")
        o2 = matmul(act, w_down[i], "nn", f32, f"ffn{tag}_down")
        return o2, (a2, gate, gs_, gvds, act)

    def ffn_bwd(do2, i, saved, tag):
        a2, gate, gs_, gvds, act = saved
        g_down = matmul(act, do2, "tn", bf16, f"ffn{tag}_down_dw")
        dact = matmul(do2, w_down[i], "nt", bf16, f"ffn{tag}_down_dx")
        dval, dgate, dcw, dcb = ffn_gate_bwd(gate, gs_, gvds, ffn_cw[i], dact, f"ffn{tag}_gate_bwd")
        g_up = smm_dw(a2, [dval, dgate], FH // 4, up_segs, 2, True, f"ffn{tag}_up_dw")
        da2, _ = smm_dx([dval, dgate], w_up[i], None, up_segs, bf16, f"ffn{tag}_up_dx")
        return da2, dict(w_up=g_up, w_down=g_down, conv_w=dcw, conv_b=dcb)

    nctx = LC // Q
    hx = x[0]
    sc0 = jnp.stack([m_ctx[0, 1], m_lat[0, 1]])
    sh0 = jnp.stack([m_ctx[0, 0], m_lat[0, 0]])
    a0 = modnorm_fwd(hx, norm1_w[0][None], sc0, sh0, LC // TB, "ssd_norm", ctx=ctx[0])
    z, xbc_pre, dt_raw = smm_fwd(a0, w_in, None, in_segs, "ssd_in")
    segs = ((0, LC), (LC, L))
    xbc, xbc_dsilu, _ = ssd_conv_fwd(xbc_pre, conv_w5, ssd_conv_b, segs, "ssd_conv")
    dt4 = dt_raw[:, :n_dt].reshape(T, 2, G, HPG)
    dtc = jnp.transpose(dt4, (1, 2, 0, 3))
    dtr = jnp.transpose(dt4, (1, 2, 3, 0))
    bias3 = ssd_dt_bias[0].reshape(2, G, HPG)
    alog3 = ssd_a_log[0].reshape(2, G, HPG)
    bc_, br_ = bias3[:, :, None, :], bias3[:, :, :, None]
    alc, alr = alog3[:, :, None, :], alog3[:, :, :, None]
    (y2, s_in_all), _ = ssd_scan_fwd(xbc, dtc, dtr, bc_, br_, alc, alr, nctx, "ssd_scan")
    dexp = jnp.repeat(ssd_d[0], P)[None, :]
    yn = ssd_gate_fwd(y2, xbc, z, dexp, ssd_norm_w, LC // GTB, "ssd_gate")
    w_out_g, w_up[0], w_down0_g = gather_wait(*fly_a[:5], yn, "gather_a_wait")
    w_out = w_out_g.reshape(DI, D)
    w_down[0] = w_down0_g.reshape(FH, D)
    o_ssd = matmul(yn, w_out, "nn", f32, "ssd_out")
    h1, a2_0 = resnorm_fwd(hx, o_ssd, m_lat[0, 2], zero_row, norm2_w[0][None], m_lat[0, 4], m_lat[0, 3], "ssd_res")
    o2_0, ffn0_saved = ffn_fwd(a2_0, 0, "0")

    h2, a1 = resnorm_fwd(h1, o2_0, m_lat[0, 5], zero_row, norm1_w[1][None], m_lat[1, 1], m_lat[1, 0], "ffn0_res")
    w_pw1, w_pw2_g = gather_wait(*fly_b[:5], a1, "gather_b_wait")
    w_pw2 = w_pw2_g.reshape(D, D)
    pa, pg = smm_fwd(a1, w_pw1, None, pw1_segs, "conf_pw1")
    dwc, _ = conf_glu_conv_fwd(pa, pg, b_pw1, w_dw, b_dw, "conf_conv")
    s1 = ln_silu_fwd(dwc, ln_w, ln_b, "conf_ln")
    o_conf = matmul(s1, w_pw2, "nn", f32, "conf_pw2")
    h3, a2_1 = resnorm_fwd(h2, o_conf, m_lat[1, 2], b_pw2, norm2_w[1][None], m_lat[1, 4], m_lat[1, 3], "conf_res")
    w_up[1], w_down1_g = gather_wait(*fly_c[:5], h3, "gather_c_wait")
    w_down[1] = w_down1_g.reshape(FH, D)
    o2_1, ffn1_saved = ffn_fwd(a2_1, 1, "1")

    loss_part, dh4, g_final, do2_1, dg2_1 = final_loss(h3, o2_1, m_lat[1, 5], final_norm_w[None], loss_target[0],
                                                       "loss_head")
    da2_1, gf1 = ffn_bwd(do2_1, 1, ffn1_saved, "1")
    dh3, dn2_1, dsc2_1, dsh2_1, do_conf, dg1_1, g_b_pw2 = normres_bwd(
        h3, norm2_w[1][None], m_lat[1, 4], m_lat[1, 3], da2_1, dh4, o_conf, m_lat[1, 2], b_pw2, "ffn1_norm_bwd")
    gf1.update(norm2=dn2_1, sh2=dsh2_1, sc2=dsc2_1, g2=dg2_1)
    g_pw2 = matmul(s1, do_conf, "tn", bf16, "conf_pw2_dw")
    ds1 = matmul(do_conf, w_pw2, "nt", bf16, "conf_pw2_dx")
    ddwc, g_ln_w, g_ln_b = ln_silu_bwd(dwc, ln_w, ln_b, ds1, "conf_ln_bwd")
    dpa, dpg, dba, dbg, g_w_dw, g_b_dw = conf_glu_conv_bwd(pa, pg, b_pw1, w_dw, ddwc, "conf_conv_bwd")
    g_b_pw1 = jnp.concatenate([dba, dbg], axis=1)
    g_pw1 = smm_dw(a1, [dpa, dpg], 2 * D // N_DEV, pw1_segs, 1, False, "conf_pw1_dw")
    da1, _ = smm_dx([dpa, dpg], w_pw1, None, pw1_segs, bf16, "conf_pw1_dx")
    dh2, g_n1_1, dsc1_1, dsh1_1, do2_0, dg2_0, _ = normres_bwd(
        h2, norm1_w[1][None], m_lat[1, 1], m_lat[1, 0], da1, dh3, o2_0, m_lat[0, 5], zero_row, "conf_norm_bwd")
    da2_0, gf0 = ffn_bwd(do2_0, 0, ffn0_saved, "0")
    dh1, dn2_0, dsc2_0, dsh2_0, do_ssd, dg1_0, _ = normres_bwd(
        h1, norm2_w[0][None], m_lat[0, 4], m_lat[0, 3], da2_0, dh2, o_ssd, m_lat[0, 2], zero_row, "ffn0_norm_bwd")
    gf0.update(norm2=dn2_0, sh2=dsh2_0, sc2=dsc2_0, g2=dg2_0)
    g_w_out = matmul(yn, do_ssd, "tn", bf16, "ssd_out_dw")
    dyn = matmul(do_ssd, w_out, "nt", bf16, "ssd_out_dx")
    core = mc.reshape(1).astype(jnp.int32)

    def by_device(t):
        return t.reshape((4, 2, -1, t.shape[-1]))

    early = [by_device(t) for t in (gf1["w_up"], gf1["w_down"], g_pw2, g_pw1, gf0["w_up"], gf0["w_down"], g_w_out)]
    (dy, dz, g_dexp, g_ssd_norm), early_sib = ssd_gate_bwd(
        y2, xbc, z, dexp, ssd_norm_w, dyn, LC // GTB, "ssd_gate_bwd", SiblingExchange(early))
    early_part = [add_own(t, r_, core, f"reduce_add{i}") for i, (t, r_) in enumerate(zip(early, early_sib))]
    (dxbc2, ddtc, ddtr, dbc, dbr, dalc, dalr), early_red = ssd_scan_bwd(
        xbc, dtc, dtr, bc_, br_, alc, alr, s_in_all, dy, nctx, "ssd_scan_bwd", ChipsExchange(early_part))
    ddt = (jnp.transpose(ddtc, (2, 0, 1, 3)) + jnp.transpose(ddtr, (3, 0, 1, 2))).reshape(T, n_dt)
    g_dt_bias = (dbc[:, :, 0, :] + dbr[:, :, :, 0]).reshape(2, NH_SSD)
    g_a_log = (dalc[:, :, 0, :] + dalr[:, :, :, 0]).reshape(2, NH_SSD)
    g_ssd_d = g_dexp[0, :NH_SSD]
    du, g_conv_w5, g_conv_b5 = ssd_conv_bwd(xbc_pre, conv_w5, xbc_dsilu, dxbc2, dy, dexp, segs, "ssd_conv_bwd")
    ddt_p = _pad_to(ddt, 128).astype(bf16)
    g_w_in = smm_dw(a0, [dz, du, ddt_p], w_in.shape[-1], in_segs, 2, True, "ssd_in_dw")
    g_ffn_cw = jnp.stack([gf0["conv_w"], gf1["conv_w"]])
    small_shards = [_col_shards(t) for t in (g_conv_w5, g_b_pw1, g_w_dw, g_b_dw, g_ln_w, g_ln_b, g_b_pw2, g_ffn_cw)]
    gsizes = [s_.shape[1] for s_ in small_shards]
    g_small = _pad_to(jnp.concatenate(small_shards, axis=1), PACK_ALIGN).astype(bf16)
    late = [by_device(g_w_in), by_device(g_small.reshape(N_DEV, -1, PACK_W))]
    da0, late_sib = smm_dx([dz, du, ddt_p], w_in, None, in_segs, f32, "ssd_in_dx", SiblingExchange(late))
    late_part = [add_own(t, r_, core, f"reduce_add_late{i}", twice=True) for i, (t, r_) in enumerate(zip(late, late_sib))]
    late_flying = chips_start([p_[0] for p_ in late_part], [p_[1] for p_ in late_part], "reduce_chips_late_start")
    dh0, g_n1_0, dsc1_0, dsh1_0 = modnorm_bwd(hx, norm1_w[0][None], sc0, sh0, da0, dh1, LC // TB, "ssd_norm_bwd",
                                              ctx=ctx[0])
    grad_x = dh0[None]

    r_up1, r_down1, r_pw2, r_pw1, r_up0, r_down0, r_out = early_red
    big = {}
    def tr(t):
        return jnp.swapaxes(t, -1, -2)

    up_t, m_up_t, v_up_t = tr(ffn_w_up), tr(m_ffn_w_up), tr(v_ffn_w_up)
    send_sems, recv_sems, late_p, late_land, token = late_flying
    up0 = sum_adamw(r_up0, up_t, m_up_t, v_up_t, 0, "adamw_ffn_w_up0", after=token)
    up1 = sum_adamw(r_up1, up_t, m_up_t, v_up_t, 1, "adamw_ffn_w_up1", into=up0)
    big["ffn_w_up"] = tuple(tr(t) for t in up1)
    big["conf_w_pw1"] = sum_adamw(r_pw1, conf_w_pw1[0], m_conf_w_pw1[0], v_conf_w_pw1[0], None, "adamw_conf_w_pw1",
                                  after=up1[0])
    big["ssd_w_out"] = sum_adamw(r_out, ssd_w_out[0], m_ssd_w_out[0], v_ssd_w_out[0], None, "adamw_ssd_w_out",
                                 after=big["conf_w_pw1"][0])
    dn0 = sum_adamw(r_down0, ffn_w_down, m_ffn_w_down, v_ffn_w_down, 0, "adamw_ffn_w_down0", after=big["ssd_w_out"][0])
    big["ffn_w_down"] = sum_adamw(r_down1, ffn_w_down, m_ffn_w_down, v_ffn_w_down, 1, "adamw_ffn_w_down1", into=dn0)
    big["conf_w_pw2"] = sum_adamw(r_pw2, conf_w_pw2[0], m_conf_w_pw2[0], v_conf_w_pw2[0], None, "adamw_conf_w_pw2",
                                  after=big["ffn_w_down"][0])

    zeros_d = jnp.zeros((1, D), f32)
    dm_lat = jnp.stack([
        jnp.concatenate([dsh1_0[1], dsc1_0[1], dg1_0, gf0["sh2"], gf0["sc2"], gf0["g2"]], axis=1),
        jnp.concatenate([dsh1_1, dsc1_1, dg1_1, gf1["sh2"], gf1["sc2"], gf1["g2"]], axis=1)])
    dm_ctx = jnp.stack([
        jnp.concatenate([dsh1_0[0], dsc1_0[0]] + [zeros_d] * 4, axis=1), jnp.zeros((1, 6 * D), f32)])
    dm_mine = jnp.concatenate([dm_lat.reshape(2, 6 * D), dm_ctx.reshape(2, 6 * D),
                               jnp.zeros((4, 6 * D), f32)], axis=0)
    dm_g = allgather_small(dm_mine, "gather_dmod", after=big["conf_w_pw2"][0])
    dm_all = jnp.concatenate([jnp.moveaxis(dm_g[:, 0:2], 0, 1), jnp.moveaxis(dm_g[:, 2:4], 0, 1)], axis=1)
    dm_sh = lax.dynamic_slice_in_dim(dm_all, me * mod_cols, mod_cols, axis=2)
    g_mod_w, g_cctx_part, g_mod_b = mod_bwd(c16, mod_w, dm_sh, dm_all, "mod_bwd")

    rep = [jnp.stack([g_n1_0[0], g_n1_1[0]]), jnp.stack([gf0["norm2"][0], gf1["norm2"][0]]), g_conv_b5, g_dt_bias, g_a_log,
           g_ssd_d, g_ssd_norm, jnp.stack([gf0["conv_b"][0], gf1["conv_b"][0]]), g_final, g_cctx_part, loss_part[:, :1]]
    rep_sizes = [r_.size for r_ in rep]
    rep_flat = _pad_to(jnp.concatenate([r_.reshape(-1) for r_ in rep]), 8 * PACK_W).reshape(-1, PACK_W)
    _, rep_sum = allgather_small(rep_flat, "reduce_replicated", with_sum=True)
    rep_sum = rep_sum.reshape(-1)
    roffs = [0]
    for s_ in rep_sizes:
        roffs.append(roffs[-1] + s_)
    rp = [rep_sum[roffs[i]:roffs[i + 1]] for i in range(len(rep_sizes))]
    loss = rp[10].reshape(())

    r_in, r_small = chips_wait(send_sems, recv_sems, late_p, late_land, rep_sum, "reduce_chips_late_wait")
    w_in_res = sum_adamw(r_in, tr(ssd_w_in[0]), tr(m_ssd_w_in[0]), tr(v_ssd_w_in[0]), None, "adamw_ssd_w_in")
    big["ssd_w_in"] = tuple(tr(t) for t in w_in_res)
    g_flat = sum_rows(r_small, "reduce_sum_small").reshape(-1)
    goffs = [0]
    for s_ in gsizes:
        goffs.append(goffs[-1] + s_)
    gs = [g_flat[goffs[i]:goffs[i + 1]] for i in range(len(gsizes))]
    grads = {
        "c_ctx": rp[9], "mod_w": g_mod_w, "mod_b": g_mod_b, "norm1_w": rp[0], "norm2_w": rp[1],
        "ssd_conv_w": gs[0], "ssd_conv_b": rp[2], "ssd_dt_bias": rp[3], "ssd_a_log": rp[4], "ssd_d": rp[5],
        "ssd_norm_w": rp[6], "conf_b_pw1": gs[1], "conf_w_dw": gs[2],
        "conf_b_dw": gs[3], "conf_ln_w": gs[4], "conf_ln_b": gs[5], "conf_b_pw2": gs[6],
        "ffn_conv_w": gs[7], "ffn_conv_b": rp[7], "final_norm_w": rp[8],
    }
    weights = dict(c_ctx=c_ctx, mod_w=mod_w, mod_b=mod_b, norm1_w=norm1_w, norm2_w=norm2_w, ssd_w_in=ssd_w_in, ssd_conv_w=ssd_conv_w, ssd_conv_b=ssd_conv_b, ssd_dt_bias=ssd_dt_bias, ssd_a_log=ssd_a_log, ssd_d=ssd_d, ssd_norm_w=ssd_norm_w, ssd_w_out=ssd_w_out, conf_w_pw1=conf_w_pw1, conf_b_pw1=conf_b_pw1, conf_w_dw=conf_w_dw, conf_b_dw=conf_b_dw, conf_ln_w=conf_ln_w, conf_ln_b=conf_ln_b, conf_w_pw2=conf_w_pw2, conf_b_pw2=conf_b_pw2, ffn_w_up=ffn_w_up, ffn_conv_w=ffn_conv_w, ffn_conv_b=ffn_conv_b, ffn_w_down=ffn_w_down, final_norm_w=final_norm_w)
    m_in = dict(c_ctx=m_c_ctx, mod_w=m_mod_w, mod_b=m_mod_b, norm1_w=m_norm1_w, norm2_w=m_norm2_w, ssd_w_in=m_ssd_w_in, ssd_conv_w=m_ssd_conv_w, ssd_conv_b=m_ssd_conv_b, ssd_dt_bias=m_ssd_dt_bias, ssd_a_log=m_ssd_a_log, ssd_d=m_ssd_d, ssd_norm_w=m_ssd_norm_w, ssd_w_out=m_ssd_w_out, conf_w_pw1=m_conf_w_pw1, conf_b_pw1=m_conf_b_pw1, conf_w_dw=m_conf_w_dw, conf_b_dw=m_conf_b_dw, conf_ln_w=m_conf_ln_w, conf_ln_b=m_conf_ln_b, conf_w_pw2=m_conf_w_pw2, conf_b_pw2=m_conf_b_pw2, ffn_w_up=m_ffn_w_up, ffn_conv_w=m_ffn_conv_w, ffn_conv_b=m_ffn_conv_b, ffn_w_down=m_ffn_w_down, final_norm_w=m_final_norm_w)
    v_in = dict(c_ctx=v_c_ctx, mod_w=v_mod_w, mod_b=v_mod_b, norm1_w=v_norm1_w, norm2_w=v_norm2_w, ssd_w_in=v_ssd_w_in, ssd_conv_w=v_ssd_conv_w, ssd_conv_b=v_ssd_conv_b, ssd_dt_bias=v_ssd_dt_bias, ssd_a_log=v_ssd_a_log, ssd_d=v_ssd_d, ssd_norm_w=v_ssd_norm_w, ssd_w_out=v_ssd_w_out, conf_w_pw1=v_conf_w_pw1, conf_b_pw1=v_conf_b_pw1, conf_w_dw=v_conf_w_dw, conf_b_dw=v_conf_b_dw, conf_ln_w=v_conf_ln_w, conf_ln_b=v_conf_ln_b, conf_w_pw2=v_conf_w_pw2, conf_b_pw2=v_conf_b_pw2, ffn_w_up=v_ffn_w_up, ffn_conv_w=v_ffn_conv_w, ffn_conv_b=v_ffn_conv_b, ffn_w_down=v_ffn_w_down, final_norm_w=v_final_norm_w)

    out_g, out_d, out_m, out_v = [], [], [], []
    for name_, w_ in weights.items():
        shape = w_.shape
        if name_ in big:
            for lst, t in zip((out_g, out_d, out_m, out_v), big[name_]):
                lst.append(t.reshape(shape))
            continue
        cols2 = shape[-1] if len(shape) > 1 else shape[0]
        g2 = grads[name_].reshape(-1, cols2)
        d_, nm_, nv_ = adamw(w_.reshape(-1, cols2), g2, m_in[name_].reshape(-1, cols2), v_in[name_].reshape(-1, cols2),
                             f"adamw_{name_}")
        out_g.append(g2.reshape(shape))
        out_d.append(d_.reshape(shape))
        out_m.append(nm_.reshape(shape))
        out_v.append(nv_.reshape(shape))
    return (loss, grad_x, *out_g, *out_d, *out_m, *out_v)
```

```python
import functools

import jax
import jax.numpy as jnp
from jax import lax
from jax.experimental import pallas as pl
from jax.experimental.pallas import tpu as pltpu

f32 = jnp.float32
bf16 = jnp.bfloat16
HI = lax.Precision.HIGHEST
S = jax.ShapeDtypeStruct
MESH = pl.DeviceIdType.MESH
ANY = pl.BlockSpec(memory_space=pl.ANY)
VMEM = pl.BlockSpec(memory_space=pltpu.VMEM)

N_DEV = 8
D = 1024
DI = 2048
CONVD = 4096
FH = 2816
GRID_W = 64
Q = 128
HPG = 4
P = 64
N = 128
G = 8
GW = HPG * P
NH_SSD = G * HPG
EPS = 1e-6
ADAM_LR, ADAM_B1, ADAM_B2, ADAM_EPS, ADAM_WD, ADAM_STEP = 0.001, 0.9, 0.999, 1e-08, 0.01, 10
VMEM_LIMIT_BYTES = 56 * 1024 * 1024
PACK_W = 1024
TB = 256
LTB = 512


def _cparams(*sem):
    return pltpu.CompilerParams(dimension_semantics=sem, vmem_limit_bytes=VMEM_LIMIT_BYTES)


def _pick(n, prefs):
    for p in prefs:
        if n % p == 0:
            return p
    return n


def _sigmoid(x):
    return 1.0 / (1.0 + jnp.exp(-x))


def _softplus(x):
    return jnp.maximum(x, 0.0) + jnp.log(1.0 + jnp.exp(-jnp.abs(x)))


def matmul(a, b, mode, out_dtype, name):
    if mode == "nn":
        (M, K), (_, Nn) = a.shape, b.shape
        bm, bn, bk = _pick(M, (512, 384, 256, 128)), Nn, K
    elif mode == "tn":
        (K, M), (_, Nn) = a.shape, b.shape
        bm, bn, bk = M, Nn, _pick(K, (256, 128))
    else:
        (M, K), (Nn, _) = a.shape, b.shape
        bm, bn, bk = _pick(M, (512, 384, 256, 128)), Nn, K
    nk = K // bk
    dims = {"nn": (((1,), (0,)), ((), ())), "tn": (((0,), (0,)), ((), ())), "nt": (((1,), (1,)), ((), ()))}[mode]

    def body(a_ref, b_ref, o_ref, acc_ref):
        k = pl.program_id(2)

        @pl.when(k == 0)
        def _():
            acc_ref[...] = jnp.zeros_like(acc_ref)

        acc_ref[...] += lax.dot_general(a_ref[...].astype(bf16), b_ref[...].astype(bf16), dims,
                                        preferred_element_type=f32)

        @pl.when(k == nk - 1)
        def _():
            o_ref[...] = acc_ref[...].astype(out_dtype)

    if mode == "nn":
        a_spec = pl.BlockSpec((bm, bk), lambda i, j, k: (i, k))
        b_spec = pl.BlockSpec((bk, bn), lambda i, j, k: (k, j))
    elif mode == "tn":
        a_spec = pl.BlockSpec((bk, bm), lambda i, j, k: (k, i))
        b_spec = pl.BlockSpec((bk, bn), lambda i, j, k: (k, j))
    else:
        a_spec = pl.BlockSpec((bm, bk), lambda i, j, k: (i, k))
        b_spec = pl.BlockSpec((bn, bk), lambda i, j, k: (j, k))
    return pl.pallas_call(
        body, grid=(M // bm, Nn // bn, nk), in_specs=[a_spec, b_spec],
        out_specs=pl.BlockSpec((bm, bn), lambda i, j, k: (i, j)),
        out_shape=S((M, Nn), out_dtype), scratch_shapes=[pltpu.VMEM((bm, bn), f32)],
        compiler_params=_cparams("parallel", "parallel", "arbitrary"), name=name,
    )(a, b)


SMM_DW_ROWS = (512, 384, 256)
SMM_ROWS = (256,)


def _shard_pieces(seg_widths, n):
    bounds = [0]
    for sw in seg_widths:
        bounds.append(bounds[-1] + sw)
    assert bounds[-1] == N_DEV * n, (seg_widths, n)
    out = []
    for j in range(N_DEV):
        lo, hi = j * n, (j + 1) * n
        pcs = []
        for si in range(len(seg_widths)):
            a, b = max(lo, bounds[si]), min(hi, bounds[si + 1])
            if a < b:
                pcs.append((si, a - bounds[si], a - lo, b - a))
        out.append(pcs)
    return out


def _w_spec(w, layer):
    if layer is None:
        return pl.BlockSpec(w.shape, lambda *idx: (0, 0, 0))
    return pl.BlockSpec((N_DEV, None) + w.shape[2:], lambda *idx: (0, layer, 0, 0))


def smm_fwd(a, w, layer, seg_widths, name, hosted=None):
    M, K = a.shape
    n = w.shape[-1]
    pieces = _shard_pieces(seg_widths, n)
    padded = [sw + (-sw) % 128 for sw in seg_widths]
    bm = _pick(M, SMM_ROWS)

    def body(a_ref, w_ref, *o_refs):
        av = a_ref[...]
        for si, sw in enumerate(seg_widths):
            if padded[si] != sw:
                o_refs[si][:, pl.ds(padded[si] - 128, 128)] = jnp.zeros((bm, 128), f32)
        for j in range(N_DEV):
            for si, soff, woff, wd in pieces[j]:
                o_refs[si][:, pl.ds(soff, wd)] = jnp.dot(av, w_ref[j, :, pl.ds(woff, wd)], preferred_element_type=f32)

    outs, extra = _host_call(
        body, (M // bm,), [pl.BlockSpec((bm, K), lambda i: (i, 0)), _w_spec(w, layer)],
        [pl.BlockSpec((bm, pw), lambda i: (i, 0)) for pw in padded], [S((M, pw), f32) for pw in padded], [],
        ("parallel",), name, (a, w), hosted)
    return outs if hosted is None else (outs, extra)


def smm_dx(d_segs, w, layer, seg_widths, out_dtype, name, hosted=None):
    M = d_segs[0].shape[0]
    K, n = w.shape[-2], w.shape[-1]
    pieces = _shard_pieces(seg_widths, n)
    ns = len(d_segs)
    bm = _pick(M, SMM_ROWS)

    def body(*refs):
        d_refs, w_ref, o_ref = refs[:ns], refs[ns], refs[ns + 1]
        acc = jnp.zeros((bm, K), f32)
        for j in range(N_DEV):
            for si, soff, woff, wd in pieces[j]:
                acc = acc + lax.dot_general(d_refs[si][:, pl.ds(soff, wd)], w_ref[j, :, pl.ds(woff, wd)],
                                            (((1,), (1,)), ((), ())), preferred_element_type=f32)
        o_ref[...] = acc.astype(out_dtype)

    (out,), extra = _host_call(
        body, (M // bm,),
        [pl.BlockSpec((bm, d.shape[1]), lambda i: (i, 0)) for d in d_segs] + [_w_spec(w, layer)],
        [pl.BlockSpec((bm, K), lambda i: (i, 0))], [S((M, K), out_dtype)], [], ("parallel",), name,
        (*d_segs, w), hosted)
    return out, extra


def smm_dw(a, d_segs, n, seg_widths, ngrp, transposed, name):
    M, K = a.shape
    pieces = _shard_pieces(seg_widths, n)
    per = N_DEV // ngrp
    bm = _pick(M, SMM_DW_ROWS)
    nI = M // bm
    ns = len(d_segs)
    shard = (n, K) if transposed else (K, n)

    def body(*refs):
        a_ref, d_refs, o_ref, acc_ref = refs[0], refs[1:1 + ns], refs[1 + ns], refs[2 + ns]
        grp = pl.program_id(0)
        i = pl.program_id(1)

        @pl.when(i == 0)
        def _():
            acc_ref[...] = jnp.zeros_like(acc_ref)

        av = a_ref[...]
        for gs in range(ngrp):
            def one_group(gs=gs):
                for jj in range(per):
                    for si, soff, woff, wd in pieces[gs * per + jj]:
                        dv = d_refs[si][:, pl.ds(soff, wd)]
                        if transposed:
                            acc_ref[jj, pl.ds(woff, wd), :] += lax.dot_general(
                                dv, av, (((0,), (0,)), ((), ())), preferred_element_type=f32)
                        else:
                            acc_ref[jj, :, pl.ds(woff, wd)] += lax.dot_general(
                                av, dv, (((0,), (0,)), ((), ())), preferred_element_type=f32)
            pl.when(grp == gs)(one_group)

        @pl.when(i == nI - 1)
        def _():
            o_ref[...] = acc_ref[...].astype(bf16)

    return pl.pallas_call(
        body, grid=(ngrp, nI),
        in_specs=[pl.BlockSpec((bm, K), lambda g, i: (i, 0))]
        + [pl.BlockSpec((bm, d.shape[1]), lambda g, i: (i, 0)) for d in d_segs],
        out_specs=pl.BlockSpec((per,) + shard, lambda g, i: (g, 0, 0)), out_shape=S((N_DEV,) + shard, bf16),
        scratch_shapes=[pltpu.VMEM((per,) + shard, f32)],
        compiler_params=_cparams("arbitrary", "arbitrary"), name=name)(a, *d_segs)


def _modnorm_f(h, w, sc, sh):
    y = h * lax.rsqrt(jnp.mean(h * h, axis=-1, keepdims=True) + EPS)
    return (y * w) * (1.0 + sc) + sh


def _kind_specs(nctxb):
    if nctxb > 0:
        return pl.BlockSpec((None, 1, D), lambda i: (jnp.where(i < nctxb, 0, 1), 0, 0))
    return pl.BlockSpec((None, 1, D), lambda i: (0, 0, 0))


def _two_part_specs(nctxb):
    return (pl.BlockSpec((TB, D), lambda i: (jnp.minimum(i, nctxb - 1), 0)),
            pl.BlockSpec((TB, D), lambda i: (jnp.maximum(i - nctxb, 0), 0)))


def modnorm_fwd(h, w, sc, sh, nctxb, name, ctx=None):
    if ctx is None:
        T = h.shape[0]

        def body(h_ref, w_ref, sc_ref, sh_ref, o_ref):
            o_ref[...] = _modnorm_f(h_ref[...], w_ref[...], sc_ref[...], sh_ref[...]).astype(bf16)

        hspecs, hargs = [pl.BlockSpec((TB, D), lambda i: (i, 0))], (h,)
    else:
        T = h.shape[0] + ctx.shape[0]

        def body(c_ref, h_ref, w_ref, sc_ref, sh_ref, o_ref):
            hv = jnp.where(pl.program_id(0) < nctxb, c_ref[...], h_ref[...])
            o_ref[...] = _modnorm_f(hv, w_ref[...], sc_ref[...], sh_ref[...]).astype(bf16)

        hspecs, hargs = list(_two_part_specs(nctxb)), (ctx, h)
    row = pl.BlockSpec((1, D), lambda i: (0, 0))
    ks = _kind_specs(nctxb)
    return pl.pallas_call(body, grid=(T // TB,), in_specs=hspecs + [row, ks, ks],
                          out_specs=pl.BlockSpec((TB, D), lambda i: (i, 0)), out_shape=S((T, D), bf16),
                          compiler_params=_cparams("parallel"), name=name)(*hargs, w, sc, sh)


def modnorm_bwd(h, w, sc, sh, da, dres, nctxb, name, ctx=None):
    T = h.shape[0] + (0 if ctx is None else ctx.shape[0])
    kinds = sc.shape[0]
    nh = 1 if ctx is None else 2

    def body(*refs):
        w_ref, sc_ref, sh_ref, da_ref, dres_ref, dh_ref, dw_ref, dsc_ref, dsh_ref = refs[nh:]
        i = pl.program_id(0)
        hv = refs[0][...] if ctx is None else jnp.where(i < nctxb, refs[0][...], refs[1][...])
        _, vjp = jax.vjp(_modnorm_f, hv, w_ref[...], sc_ref[...], sh_ref[...])
        dh, dw, dsc, dsh = vjp(da_ref[...].astype(f32))
        dh_ref[...] = dres_ref[...] + dh

        @pl.when(i == 0)
        def _():
            dw_ref[...] = jnp.zeros_like(dw_ref)

        @pl.when((i == 0) | (i == nctxb))
        def _():
            dsc_ref[...] = jnp.zeros_like(dsc_ref)
            dsh_ref[...] = jnp.zeros_like(dsh_ref)

        dw_ref[...] += dw
        dsc_ref[...] += dsc
        dsh_ref[...] += dsh

    blk = pl.BlockSpec((TB, D), lambda i: (i, 0))
    lat = pl.BlockSpec((TB, D), lambda i: (jnp.maximum(i - nctxb, 0), 0))
    row = pl.BlockSpec((1, D), lambda i: (0, 0))
    ks = _kind_specs(nctxb)
    hspecs, hargs = ([blk], (h,)) if ctx is None else (list(_two_part_specs(nctxb)), (ctx, h))
    return pl.pallas_call(
        body, grid=(T // TB,), in_specs=hspecs + [row, ks, ks, blk, lat], out_specs=[lat, row, ks, ks],
        out_shape=[S((T - nctxb * TB, D), f32), S((1, D), f32), S((kinds, 1, D), f32), S((kinds, 1, D), f32)],
        compiler_params=_cparams("arbitrary"), name=name)(*hargs, w, sc, sh, da, dres)


def resnorm_fwd(h, o, g, b, w, sc, sh, name):
    T = h.shape[0]

    def body(h_ref, o_ref, g_ref, b_ref, w_ref, sc_ref, sh_ref, hn_ref, a_ref):
        hn = h_ref[...] + g_ref[...] * (o_ref[...] + b_ref[...])
        hn_ref[...] = hn
        a_ref[...] = _modnorm_f(hn, w_ref[...], sc_ref[...], sh_ref[...]).astype(bf16)

    blk = pl.BlockSpec((LTB, D), lambda i: (i, 0))
    row = pl.BlockSpec((1, D), lambda i: (0, 0))
    return pl.pallas_call(body, grid=(T // LTB,), in_specs=[blk, blk, row, row, row, row, row], out_specs=[blk, blk],
                          out_shape=[S((T, D), f32), S((T, D), bf16)], compiler_params=_cparams("parallel"),
                          name=name)(h, o, g, b, w, sc, sh)


def normres_bwd(h, w, sc, sh, da, dres, o, g, b, name):
    T = h.shape[0]

    def body(h_ref, w_ref, sc_ref, sh_ref, da_ref, dres_ref, o_ref, g_ref, b_ref,
             dh_ref, dw_ref, dsc_ref, dsh_ref, do_ref, dg_ref, db_ref):
        _, vjp = jax.vjp(_modnorm_f, h_ref[...], w_ref[...], sc_ref[...], sh_ref[...])
        dhn, dw, dsc, dsh = vjp(da_ref[...].astype(f32))
        dh = dres_ref[...] + dhn
        dh_ref[...] = dh
        do = g_ref[...] * dh
        do_ref[...] = do.astype(bf16)
        sums = (dw, dsc, dsh, jnp.sum(dh * (o_ref[...] + b_ref[...]), axis=0, keepdims=True),
                jnp.sum(do, axis=0, keepdims=True))

        @pl.when(pl.program_id(0) == 0)
        def _():
            for r_ in (dw_ref, dsc_ref, dsh_ref, dg_ref, db_ref):
                r_[...] = jnp.zeros_like(r_)

        for r_, s_ in zip((dw_ref, dsc_ref, dsh_ref, dg_ref, db_ref), sums):
            r_[...] += s_

    blk = pl.BlockSpec((LTB, D), lambda i: (i, 0))
    row = pl.BlockSpec((1, D), lambda i: (0, 0))
    return pl.pallas_call(
        body, grid=(T // LTB,), in_specs=[blk, row, row, row, blk, blk, blk, row, row],
        out_specs=[blk, row, row, row, blk, row, row],
        out_shape=[S((T, D), f32), S((1, D), f32), S((1, D), f32), S((1, D), f32), S((T, D), bf16), S((1, D), f32),
                   S((1, D), f32)],
        compiler_params=_cparams("arbitrary"), name=name)(h, w, sc, sh, da, dres, o, g, b)


def final_loss(h, o, g, w, tgt, name):
    T = h.shape[0]

    def f(hv, wv, tv):
        y = (hv * lax.rsqrt(jnp.mean(hv * hv, axis=-1, keepdims=True) + EPS)) * wv
        e = y - tv
        return 0.5 * jnp.sum(jnp.sum(e * e, axis=-1, keepdims=True), axis=0, keepdims=True) * (1.0 / D)

    def body(h_ref, o_ref, g_ref, w_ref, t_ref, loss_ref, dh_ref, dw_ref, do_ref, dg_ref):
        i = pl.program_id(0)
        tv = t_ref[...]
        ov = o_ref[...]
        gv = g_ref[...]
        val, vjp = jax.vjp(lambda a, b_: f(a, b_, tv), h_ref[...] + gv * ov, w_ref[...])
        dh, dw = vjp(jnp.ones((1, 1), f32))
        dh_ref[...] = dh
        do_ref[...] = (gv * dh).astype(bf16)

        @pl.when(i == 0)
        def _():
            loss_ref[...] = jnp.zeros_like(loss_ref)
            dw_ref[...] = jnp.zeros_like(dw_ref)
            dg_ref[...] = jnp.zeros_like(dg_ref)

        loss_ref[...] += jnp.broadcast_to(val, (1, 128))
        dw_ref[...] += dw
        dg_ref[...] += jnp.sum(dh * ov, axis=0, keepdims=True)

    blk = pl.BlockSpec((LTB, D), lambda i: (i, 0))
    row = pl.BlockSpec((1, D), lambda i: (0, 0))
    return pl.pallas_call(body, grid=(T // LTB,), in_specs=[blk, blk, row, row, blk],
                          out_specs=[pl.BlockSpec((1, 128), lambda i: (0, 0)), blk, row, blk, row],
                          out_shape=[S((1, 128), f32), S((T, D), f32), S((1, D), f32), S((T, D), bf16), S((1, D), f32)],
                          compiler_params=_cparams("arbitrary"), name=name)(h, o, g, w, tgt)


CB = 256
RT = 32
RTB = 16


def _fold8(t):
    acc = t[0:8]
    for k in range(1, t.shape[0] // 8):
        acc = acc + t[8 * k:8 * (k + 1)]
    return acc


def _rows(start, off=0, rt=RT):
    return pl.ds(pl.multiple_of(start + off, 8), rt)


def _rowsb(start, off=0):
    return _rows(start, off, RTB)


def _zero_rows(ref, start, n):
    ref[pl.ds(start, n), :] = jnp.zeros((n, ref.shape[1]), f32)


K5, HALF5, PAD5 = 5, 2, 8


def _taps5(base_ref, r, rt, sign):
    n = rt + 2 * PAD5
    v = base_ref[pl.ds(pl.multiple_of(r, 8), n), :]
    taps = []
    for k in range(K5):
        o = sign * (k - HALF5)
        rolled = v if o == 0 else pltpu.roll(v, (-o) % n, axis=0)
        taps.append(rolled[PAD5:PAD5 + rt])
    return taps


def ssd_conv_fwd(u, w, b, segs, name, hosted=None):
    T = u.shape[0]
    maxlen = max(ln for _, ln in segs)

    def body(u_ref, w_ref, b_ref, o_ref, ds_ref, base_ref):
        wv = [w_ref[pl.ds(k, 1), :] for k in range(K5)]
        bv = b_ref[...]
        for s0, ln in segs:
            _zero_rows(base_ref, 0, PAD5)
            _zero_rows(base_ref, PAD5 + ln, PAD5)
            base_ref[pl.ds(PAD5, ln), :] = u_ref[pl.ds(s0, ln), :]

            def tile(i, carry):
                r = i * RT
                taps = _taps5(base_ref, r, RT, 1)
                acc = jnp.broadcast_to(bv, (RT, CB))
                for k in range(K5):
                    acc = acc + taps[k] * wv[k]
                sg = _sigmoid(acc)
                o_ref[_rows(r, s0), :] = acc * sg
                ds_ref[_rows(r, s0), :] = sg * (1.0 + acc * (1.0 - sg))
                return carry

            lax.fori_loop(0, ln // RT, tile, 0, unroll=2)

    cblk = pl.BlockSpec((T, CB), lambda j: (0, j))
    (out, dsilu), extra = _host_call(
        body, (CONVD // CB,),
        [cblk, pl.BlockSpec((K5, CB), lambda j: (0, j)), pl.BlockSpec((1, CB), lambda j: (0, j))],
        [cblk, cblk], [S((T, CONVD), f32), S((T, CONVD), f32)],
        [pltpu.VMEM((maxlen + 2 * PAD5, CB), f32)], ("parallel",), name, (u, w, b), hosted)
    return out, dsilu, extra


def ssd_conv_bwd(proj, w, dsilu, dy2, dyskip, dexp, segs, name):
    T = proj.shape[0]
    maxlen = max(ln for _, ln in segs)
    nskip = DI // CB

    def body(u_ref, w_ref, ds_ref, dya_ref, dyb_ref, dsk_ref, dexp_ref, du_ref, dw_ref, db_ref, base_ref, dbase_ref):
        wv = [w_ref[pl.ds(k, 1), :] for k in range(K5)]
        has_skip = (pl.program_id(0) < nskip).astype(f32) * dexp_ref[...]
        acc8 = tuple(jnp.zeros((8, CB), f32) for _ in range(K5 + 1))
        for s0, ln in segs:
            for ref in (base_ref, dbase_ref):
                _zero_rows(ref, 0, PAD5)
                _zero_rows(ref, PAD5 + ln, PAD5)
            base_ref[pl.ds(PAD5, ln), :] = u_ref[pl.ds(s0, ln), :]

            def tile1(i, carry):
                r = i * RTB
                dy = dya_ref[_rowsb(r, s0), :] + dyb_ref[_rowsb(r, s0), :] + has_skip * dsk_ref[_rowsb(r, s0), :]
                dpre = dy * ds_ref[_rowsb(r, s0), :]
                dbase_ref[_rowsb(r, PAD5), :] = dpre
                taps = _taps5(base_ref, r, RTB, 1)
                new = [carry[k] + _fold8(dpre * taps[k]) for k in range(K5)]
                new.append(carry[K5] + _fold8(dpre))
                return tuple(new)

            acc8 = lax.fori_loop(0, ln // RTB, tile1, acc8, unroll=2)

            def tile2(i, carry):
                r = i * RTB
                taps = _taps5(dbase_ref, r, RTB, -1)
                du = jnp.zeros((RTB, CB), f32)
                for k in range(K5):
                    du = du + taps[k] * wv[k]
                du_ref[_rowsb(r, s0), :] = du.astype(bf16)
                return carry

            lax.fori_loop(0, ln // RTB, tile2, 0, unroll=4)
        for k in range(K5):
            dw_ref[pl.ds(k, 1), :] = jnp.sum(acc8[k], axis=0, keepdims=True)
        db_ref[...] = jnp.sum(acc8[K5], axis=0, keepdims=True)

    cblk = pl.BlockSpec((T, CB), lambda j: (0, j))
    return pl.pallas_call(
        body, grid=(CONVD // CB,),
        in_specs=[cblk, pl.BlockSpec((K5, CB), lambda j: (0, j)), cblk,
                  pl.BlockSpec((None, T, CB), lambda j: (0, 0, j)), pl.BlockSpec((None, T, CB), lambda j: (1, 0, j)),
                  pl.BlockSpec((T, CB), lambda j: (0, jnp.minimum(j, nskip - 1))),
                  pl.BlockSpec((1, CB), lambda j: (0, jnp.minimum(j, nskip - 1)))],
        out_specs=[cblk, pl.BlockSpec((K5, CB), lambda j: (0, j)), pl.BlockSpec((1, CB), lambda j: (0, j))],
        out_shape=[S((T, CONVD), bf16), S((K5, CONVD), f32), S((1, CONVD), f32)],
        scratch_shapes=[pltpu.VMEM((maxlen + 2 * PAD5, CB), f32), pltpu.VMEM((maxlen + 2 * PAD5, CB), f32)],
        compiler_params=_cparams("parallel"), name=name)(proj, w, dsilu, dy2, dy2, dyskip, dexp)


GPAD = GRID_W


def _grid_copies(g_ref, src, L):
    col = lax.broadcasted_iota(jnp.int32, (L, CB), 0) & (GRID_W - 1)
    for d in range(3):
        _zero_rows(g_ref.at[d], 0, GPAD)
        _zero_rows(g_ref.at[d], GPAD + L, GPAD)
    g_ref[1, pl.ds(GPAD, L), :] = src
    g_ref[0, pl.ds(GPAD, L), :] = jnp.where(col != 0, g_ref[1, pl.ds(GPAD - 1, L), :], 0.0)
    g_ref[2, pl.ds(GPAD, L), :] = jnp.where(col != GRID_W - 1, g_ref[1, pl.ds(GPAD + 1, L), :], 0.0)


def ffn_gate_fwd(val, gate, cw, cb_, name, hosted=None):
    L = val.shape[0]
    nb = FH // CB

    def body(val_ref, gate_ref, w_ref, b_ref, o_ref, s_ref, vds_ref, g_ref):
        wv = [w_ref[pl.ds(t, 1), :] for t in range(9)]
        bv = b_ref[...]
        _grid_copies(g_ref, gate_ref[...], L)

        def tile(i, carry):
            r = i * RT
            acc = jnp.broadcast_to(bv, (RT, CB))
            for dr in range(3):
                for dc in range(3):
                    acc = acc + g_ref[dc, _rows(r, GPAD + (dr - 1) * GRID_W), :] * wv[3 * dr + dc]
            sg = _sigmoid(acc)
            s = acc * sg
            v = val_ref[_rows(r), :]
            o_ref[_rows(r), :] = (s * v).astype(bf16)
            s_ref[_rows(r), :] = s
            vds_ref[_rows(r), :] = v * (sg * (1.0 + acc * (1.0 - sg)))
            return carry

        lax.fori_loop(0, L // RT, tile, 0, unroll=2)

    cblk = pl.BlockSpec((L, CB), lambda j: (0, j))
    (out, s_, vds), extra = _host_call(
        body, (nb,), [cblk, cblk, pl.BlockSpec((9, CB), lambda j: (0, j)), pl.BlockSpec((1, CB), lambda j: (0, j))],
        [cblk, cblk, cblk], [S((L, FH), bf16), S((L, FH), f32), S((L, FH), f32)],
        [pltpu.VMEM((3, L + 2 * GPAD, CB), f32)], ("parallel",), name, (val, gate, cw, cb_), hosted)
    return out, s_, vds, extra


def ffn_gate_bwd(gate, s_, vds, cw, dact, name):
    L = gate.shape[0]
    nb = FH // CB

    def body(gate_ref, s_ref, vds_ref, w_ref, da_ref, dval_ref, dgate_ref, dw_ref, db_ref, g_ref, d_ref):
        wv = [w_ref[pl.ds(t, 1), :] for t in range(9)]
        _grid_copies(g_ref, gate_ref[...], L)

        def tile1(i, carry):
            r = i * RTB
            da = da_ref[_rowsb(r), :].astype(f32)
            dval_ref[_rowsb(r), :] = (da * s_ref[_rowsb(r), :]).astype(bf16)
            dpre = da * vds_ref[_rowsb(r), :]
            d_ref[_rowsb(r), :] = dpre
            new = [carry[t] + _fold8(dpre * g_ref[t % 3, _rowsb(r, GPAD + (t // 3 - 1) * GRID_W), :]) for t in range(9)]
            new.append(carry[9] + _fold8(dpre))
            return tuple(new)

        acc8 = lax.fori_loop(0, L // RTB, tile1, tuple(jnp.zeros((8, CB), f32) for _ in range(10)), unroll=2)
        for t in range(9):
            dw_ref[pl.ds(t, 1), :] = jnp.sum(acc8[t], axis=0, keepdims=True)
        db_ref[...] = jnp.sum(acc8[9], axis=0, keepdims=True)
        _grid_copies(g_ref, d_ref[...], L)

        def tile2(i, carry):
            r = i * RTB
            dg = jnp.zeros((RTB, CB), f32)
            for dr in range(3):
                for dc in range(3):
                    dg = dg + g_ref[2 - dc, _rowsb(r, GPAD - (dr - 1) * GRID_W), :] * wv[3 * dr + dc]
            dgate_ref[_rowsb(r), :] = dg.astype(bf16)
            return carry

        lax.fori_loop(0, L // RTB, tile2, 0, unroll=4)

    cblk = pl.BlockSpec((L, CB), lambda j: (0, j))
    return pl.pallas_call(
        body, grid=(nb,),
        in_specs=[cblk, cblk, cblk, pl.BlockSpec((9, CB), lambda j: (0, j)), cblk],
        out_specs=[cblk, cblk, pl.BlockSpec((9, CB), lambda j: (0, j)), pl.BlockSpec((1, CB), lambda j: (0, j))],
        out_shape=[S((L, FH), bf16), S((L, FH), bf16), S((9, FH), f32), S((1, FH), f32)],
        scratch_shapes=[pltpu.VMEM((3, L + 2 * GPAD, CB), f32), pltpu.VMEM((L, CB), f32)],
        compiler_params=_cparams("parallel"), name=name)(gate, s_, vds, cw, dact)


CONF_K = 31
CHALF = CONF_K // 2
CPAD = 16


def _shift_copies8(c_ref, base_ref, L):
    n = L + 2 * CPAD - 8
    for b_ in range(8):
        c_ref[b_, pl.ds(0, n), :] = base_ref[pl.ds(b_, n), :]


def _tap_ab(o):
    return o % 8, o - o % 8


def conf_glu_conv_fwd(pa, pg, b1, wdw, bdw, name, hosted=None):
    L = pa.shape[0]
    nb = D // CB

    def body(pa_ref, pg_ref, ba_ref, bg_ref, w_ref, bdw_ref, o_ref, base_ref, c_ref):
        _zero_rows(base_ref, 0, CPAD)
        _zero_rows(base_ref, CPAD + L, CPAD)
        base_ref[pl.ds(CPAD, L), :] = (pa_ref[...] + ba_ref[...]) * _sigmoid(pg_ref[...] + bg_ref[...])
        _shift_copies8(c_ref, base_ref, L)
        bv = bdw_ref[...]

        def tile(i, carry):
            r = i * RT
            acc = jnp.broadcast_to(bv, (RT, CB))
            for k in range(CONF_K):
                b_, a8 = _tap_ab(k - CHALF)
                acc = acc + c_ref[b_, _rows(r, CPAD + a8), :] * w_ref[pl.ds(k, 1), :]
            o_ref[_rows(r), :] = acc
            return carry

        lax.fori_loop(0, L // RT, tile, 0, unroll=2)

    cblk = pl.BlockSpec((L, CB), lambda j: (0, j))
    rblk = pl.BlockSpec((1, CB), lambda j: (0, j))
    rgblk = pl.BlockSpec((1, CB), lambda j: (0, nb + j))
    (out,), extra = _host_call(
        body, (nb,), [cblk, cblk, rblk, rgblk, pl.BlockSpec((CONF_K, CB), lambda j: (0, j)), rblk],
        [cblk], [S((L, D), f32)], [pltpu.VMEM((L + 2 * CPAD, CB), f32), pltpu.VMEM((8, L + 2 * CPAD, CB), f32)],
        ("parallel",), name, (pa, pg, b1, b1, wdw, bdw), hosted)
    return out, extra


def conf_glu_conv_bwd(pa, pg, b1, wdw, dy, name):
    L = pa.shape[0]
    nb = D // CB

    def body(pa_ref, pg_ref, ba_ref, bg_ref, w_ref, dy_ref, dpa_ref, dpg_ref, dba_ref, dbg_ref, dw_ref, dbdw_ref,
             base_ref, c_ref, acc_ref):
        _zero_rows(base_ref, 0, CPAD)
        _zero_rows(base_ref, CPAD + L, CPAD)
        base_ref[pl.ds(CPAD, L), :] = (pa_ref[...] + ba_ref[...]) * _sigmoid(pg_ref[...] + bg_ref[...])
        _shift_copies8(c_ref, base_ref, L)
        acc_ref[...] = jnp.zeros_like(acc_ref)

        def tile1(i, carry):
            r = i * RTB
            dyt = dy_ref[_rowsb(r), :]
            for k in range(CONF_K):
                b_, a8 = _tap_ab(k - CHALF)
                acc_ref[k] += _fold8(dyt * c_ref[b_, _rowsb(r, CPAD + a8), :])
            return carry + _fold8(dyt)

        db8 = lax.fori_loop(0, L // RTB, tile1, jnp.zeros((8, CB), f32), unroll=2)
        dbdw_ref[...] = jnp.sum(db8, axis=0, keepdims=True)
        for k in range(CONF_K):
            dw_ref[pl.ds(k, 1), :] = jnp.sum(acc_ref[k], axis=0, keepdims=True)
        base_ref[pl.ds(CPAD, L), :] = dy_ref[...]
        _shift_copies8(c_ref, base_ref, L)
        ba = ba_ref[...]
        bg = bg_ref[...]

        def tile2(i, carry):
            r = i * RTB
            dglu = jnp.zeros((RTB, CB), f32)
            for k in range(CONF_K):
                b_, a8 = _tap_ab(CHALF - k)
                dglu = dglu + c_ref[b_, _rowsb(r, CPAD + a8), :] * w_ref[pl.ds(k, 1), :]
            a = pa_ref[_rowsb(r), :] + ba
            sg = _sigmoid(pg_ref[_rowsb(r), :] + bg)
            dpa = dglu * sg
            dpg = dglu * a * (sg * (1.0 - sg))
            dpa_ref[_rowsb(r), :] = dpa.astype(bf16)
            dpg_ref[_rowsb(r), :] = dpg.astype(bf16)
            return carry[0] + _fold8(dpa), carry[1] + _fold8(dpg)

        s8 = lax.fori_loop(0, L // RTB, tile2, (jnp.zeros((8, CB), f32), jnp.zeros((8, CB), f32)), unroll=2)
        dba_ref[...] = jnp.sum(s8[0], axis=0, keepdims=True)
        dbg_ref[...] = jnp.sum(s8[1], axis=0, keepdims=True)

    cblk = pl.BlockSpec((L, CB), lambda j: (0, j))
    rblk = pl.BlockSpec((1, CB), lambda j: (0, j))
    rgblk = pl.BlockSpec((1, CB), lambda j: (0, nb + j))
    wblk = pl.BlockSpec((CONF_K, CB), lambda j: (0, j))
    return pl.pallas_call(
        body, grid=(nb,), in_specs=[cblk, cblk, rblk, rgblk, wblk, cblk],
        out_specs=[cblk, cblk, rblk, rblk, wblk, rblk],
        out_shape=[S((L, D), bf16), S((L, D), bf16), S((1, D), f32), S((1, D), f32), S((CONF_K, D), f32), S((1, D), f32)],
        scratch_shapes=[pltpu.VMEM((L + 2 * CPAD, CB), f32), pltpu.VMEM((8, L + 2 * CPAD, CB), f32),
                        pltpu.VMEM((CONF_K, 8, CB), f32)],
        compiler_params=_cparams("parallel"), name=name)(pa, pg, b1, b1, wdw, dy)


def _ln_silu_f(x, w, b):
    mu = jnp.mean(x, axis=-1, keepdims=True)
    d = x - mu
    y = d * lax.rsqrt(jnp.mean(d * d, axis=-1, keepdims=True) + EPS) * w + b
    return y * _sigmoid(y)


def ln_silu_fwd(x, w, b, name):
    T = x.shape[0]

    def body(x_ref, w_ref, b_ref, o_ref):
        o_ref[...] = _ln_silu_f(x_ref[...], w_ref[...], b_ref[...]).astype(bf16)

    blk = pl.BlockSpec((TB, D), lambda i: (i, 0))
    row = pl.BlockSpec((1, D), lambda i: (0, 0))
    return pl.pallas_call(body, grid=(T // TB,), in_specs=[blk, row, row], out_specs=blk, out_shape=S((T, D), bf16),
                          compiler_params=_cparams("parallel"), name=name)(x, w, b)


def ln_silu_bwd(x, w, b, ds, name):
    T = x.shape[0]

    def body(x_ref, w_ref, b_ref, ds_ref, dx_ref, dw_ref, db_ref):
        i = pl.program_id(0)
        _, vjp = jax.vjp(_ln_silu_f, x_ref[...], w_ref[...], b_ref[...])
        dx, dw, db = vjp(ds_ref[...].astype(f32))
        dx_ref[...] = dx

        @pl.when(i == 0)
        def _():
            dw_ref[...] = jnp.zeros_like(dw_ref)
            db_ref[...] = jnp.zeros_like(db_ref)

        dw_ref[...] += dw
        db_ref[...] += db

    blk = pl.BlockSpec((TB, D), lambda i: (i, 0))
    row = pl.BlockSpec((1, D), lambda i: (0, 0))
    return pl.pallas_call(body, grid=(T // TB,), in_specs=[blk, row, row, blk], out_specs=[blk, row, row],
                          out_shape=[S((T, D), f32), S((1, D), f32), S((1, D), f32)],
                          compiler_params=_cparams("arbitrary"), name=name)(x, w, b, ds)


def _mxu(a, b, dims):
    return lax.dot_general(a.astype(bf16), b.astype(bf16), (dims, ((), ())), preferred_element_type=f32)


def _nn(a, b):
    return _mxu(a, b, ((1,), (0,)))


def _nt(a, b):
    return _mxu(a, b, ((1,), (1,)))


def _tn(a, b):
    return _mxu(a, b, ((0,), (0,)))


@jax.custom_vjp
def _dot_nn(a, b):
    return _nn(a, b)


@jax.custom_vjp
def _dot_nt(a, b):
    return _nt(a, b)


@jax.custom_vjp
def _dot_tn(a, b):
    return _tn(a, b)


_dot_nn.defvjp(lambda a, b: (_nn(a, b), (a, b)), lambda res, g: (_nt(g, res[1]), _tn(res[0], g)))
_dot_nt.defvjp(lambda a, b: (_nt(a, b), (a, b)), lambda res, g: (_nn(g, res[1]), _tn(g, res[0])))
_dot_tn.defvjp(lambda a, b: (_tn(a, b), (a, b)), lambda res, g: (_nt(res[1], g), _nn(res[0], g)))


def _exact_dot(a, b, dims, split_first):
    v = a if split_first else b
    p1 = v.astype(bf16)
    r1 = v - p1.astype(f32)
    p2 = r1.astype(bf16)
    p3 = (r1 - p2.astype(f32)).astype(bf16)
    out = None
    for p in (p1, p2, p3):
        lhs, rhs = (p, b.astype(bf16)) if split_first else (a.astype(bf16), p)
        t = lax.dot_general(lhs, rhs, (dims, ((), ())), preferred_element_type=f32)
        out = t if out is None else out + t
    return out


@jax.custom_vjp
def _masked_sum_cols(mf, a):
    return _exact_dot(mf, a, ((1,), (0,)), False)


@jax.custom_vjp
def _masked_sum_rows(mf, a):
    return _exact_dot(a, mf, ((1,), (1,)), True)


_masked_sum_cols.defvjp(lambda mf, a: (_exact_dot(mf, a, ((1,), (0,)), False), mf),
                        lambda mf, g: (jnp.zeros_like(mf), _exact_dot(mf, g, ((0,), (0,)), False)))
_masked_sum_rows.defvjp(lambda mf, a: (_exact_dot(a, mf, ((1,), (1,)), True), mf),
                        lambda mf, g: (jnp.zeros_like(mf), _exact_dot(g, mf, ((1,), (0,)), True)))


def _masked_sum(mf, a, rows):
    return _masked_sum_rows(mf, a) if rows else _masked_sum_cols(mf, a)


def _lanes_to_rows(v):
    r = lax.broadcasted_iota(jnp.int32, (GW, GW), 0)
    c = lax.broadcasted_iota(jnp.int32, (GW, GW), 1)
    return jnp.sum(jnp.where(r == c, jnp.broadcast_to(v, (GW, GW)), 0.0), axis=1, keepdims=True)


def _ssd_chunk(x, B, C, dtc, dtr, bc, br, alc, alr, s_in, is_fwd):
    row = lax.broadcasted_iota(jnp.int32, (Q, Q), 0)
    col = lax.broadcasted_iota(jnp.int32, (Q, Q), 1)
    sgn = jnp.where(is_fwd, 1, -1).astype(jnp.int32)
    mask = (row - col) * sgn >= 0
    mf = mask.astype(f32)
    lane_head = lax.broadcasted_iota(jnp.int32, (1, GW), 1) // P

    def spread(v):
        out = jnp.zeros((v.shape[0], GW), f32)
        for r in range(HPG):
            out = jnp.where(lane_head == r, v[:, r:r + 1], out)
        return out

    dt_c = _softplus(dtc + bc)
    dt_r = _softplus(dtr + br)
    a_c = dt_c * (-jnp.exp(alc))
    a_r = dt_r * (-jnp.exp(alr))
    acum_c = _masked_sum(mf, a_c, False)
    acum_r = _masked_sum(mf, a_r, True)
    tot_c = jnp.sum(a_c, axis=0, keepdims=True)
    dt_e = spread(dt_c)
    acum_e = spread(acum_c)
    tot_e = spread(tot_c)
    xdt = x * dt_e
    cb = _dot_nt(C, B)
    scores, xs = [], []
    for r in range(HPG):
        seg = acum_c[:, r:r + 1] - acum_r[r:r + 1, :]
        scores.append(cb * jnp.exp(jnp.where(mask, seg, -jnp.inf)))
        xs.append(jnp.where(lane_head == r, xdt, 0.0))
    y = _dot_nn(jnp.concatenate(scores, axis=1), jnp.concatenate(xs, axis=0))
    y = y + _dot_nt(C, s_in) * jnp.exp(acum_e)
    xe = xdt * jnp.exp(tot_e - acum_e)
    s_out = _lanes_to_rows(jnp.exp(tot_e)) * s_in + _dot_tn(xe, B)
    return y, s_out


def _chunk_index(d, t, nctx, nc):
    bwd = jnp.where(t < nctx, nctx - 1 - t, nc - 1 - (t - nctx))
    return jnp.where(d == 0, t, bwd)


def _ssd_in_specs(ci):
    small_c = pl.BlockSpec((None, G, 1, HPG), lambda d, t: (d, 0, 0, 0))
    small_r = pl.BlockSpec((None, G, HPG, 1), lambda d, t: (d, 0, 0, 0))
    return [
        pl.BlockSpec((Q, CONVD), lambda d, t: (ci(d, t), 0)),
        pl.BlockSpec((None, G, Q, HPG), lambda d, t: (d, 0, ci(d, t), 0)),
        pl.BlockSpec((None, G, HPG, Q), lambda d, t: (d, 0, 0, ci(d, t))),
        small_c, small_r, small_c, small_r,
    ]


def _group_cols(g):
    return pl.ds(g * GW, GW), pl.ds(DI + g * N, N), pl.ds(DI + G * N + g * N, N)


def ssd_scan_fwd(xbc, dtc, dtr, bc, br, alc, alr, nctx, name, hosted=None):
    T = xbc.shape[0]
    nc = T // Q

    def body(xbc_ref, dtc_ref, dtr_ref, bc_ref, br_ref, alc_ref, alr_ref, y_ref, sin_ref, st_ref):
        d = pl.program_id(0)
        t = pl.program_id(1)

        @pl.when(t == 0)
        def _():
            st_ref[...] = jnp.zeros_like(st_ref)

        for g in range(G):
            xs, bs, cs = _group_cols(g)
            s_in = st_ref[g]
            sin_ref[g] = s_in
            y, s_out = _ssd_chunk(xbc_ref[:, xs], xbc_ref[:, bs], xbc_ref[:, cs], dtc_ref[g], dtr_ref[g], bc_ref[g], br_ref[g],
                                  alc_ref[g], alr_ref[g], s_in, d == 0)
            y_ref[:, xs] = y
            st_ref[g] = s_out

    ci = lambda d, t: _chunk_index(d, t, nctx, nc)
    out_specs = [
        pl.BlockSpec((None, Q, DI), lambda d, t: (d, ci(d, t), 0)),
        pl.BlockSpec((None, None, G, GW, N), lambda d, t: (d, ci(d, t), 0, 0, 0)),
    ]
    return _host_call(
        body, (2, nc), _ssd_in_specs(ci), out_specs, [S((2, T, DI), f32), S((2, nc, G, GW, N), f32)],
        [pltpu.VMEM((G, GW, N), f32)], ("arbitrary", "arbitrary"), name, (xbc, dtc, dtr, bc, br, alc, alr), hosted)


def ssd_scan_bwd(xbc, dtc, dtr, bc, br, alc, alr, s_in_all, dy, nctx, name, hosted=None):
    T = xbc.shape[0]
    nc = T // Q

    def body(xbc_ref, dtc_ref, dtr_ref, bc_ref, br_ref, alc_ref, alr_ref, sin_ref, dy_ref,
             dxbc_ref, ddtc_ref, ddtr_ref, dbc_ref, dbr_ref, dalc_ref, dalr_ref, ds_ref):
        d = pl.program_id(0)
        t = pl.program_id(1)

        @pl.when(t == 0)
        def _():
            ds_ref[...] = jnp.zeros_like(ds_ref)
            dbc_ref[...] = jnp.zeros_like(dbc_ref)
            dbr_ref[...] = jnp.zeros_like(dbr_ref)
            dalc_ref[...] = jnp.zeros_like(dalc_ref)
            dalr_ref[...] = jnp.zeros_like(dalr_ref)

        f = functools.partial(_ssd_chunk, is_fwd=(d == 0))
        for g in range(G):
            xs, bs, cs = _group_cols(g)
            _, vjp = jax.vjp(f, xbc_ref[:, xs], xbc_ref[:, bs], xbc_ref[:, cs], dtc_ref[g], dtr_ref[g], bc_ref[g], br_ref[g],
                             alc_ref[g], alr_ref[g], sin_ref[g])
            dx, dB, dC, ddtc, ddtr, dbc, dbr, dalc, dalr, ds = vjp((dy_ref[:, xs], ds_ref[g]))
            dxbc_ref[:, xs] = dx
            dxbc_ref[:, bs] = dB
            dxbc_ref[:, cs] = dC
            ddtc_ref[g] = ddtc
            ddtr_ref[g] = ddtr
            dbc_ref[g] += dbc
            dbr_ref[g] += dbr
            dalc_ref[g] += dalc
            dalr_ref[g] += dalr
            ds_ref[g] = ds

    ci = lambda d, t: _chunk_index(d, nc - 1 - t, nctx, nc)
    in_specs = _ssd_in_specs(ci) + [
        pl.BlockSpec((None, None, G, GW, N), lambda d, t: (d, ci(d, t), 0, 0, 0)),
        pl.BlockSpec((Q, DI), lambda d, t: (ci(d, t), 0)),
    ]
    small_c = pl.BlockSpec((None, G, 1, HPG), lambda d, t: (d, 0, 0, 0))
    small_r = pl.BlockSpec((None, G, HPG, 1), lambda d, t: (d, 0, 0, 0))
    out_specs = [
        pl.BlockSpec((None, Q, CONVD), lambda d, t: (d, ci(d, t), 0)),
        pl.BlockSpec((None, G, Q, HPG), lambda d, t: (d, 0, ci(d, t), 0)),
        pl.BlockSpec((None, G, HPG, Q), lambda d, t: (d, 0, 0, ci(d, t))),
        small_c, small_r, small_c, small_r,
    ]
    out_shape = [S((2, T, CONVD), f32), S((2, G, T, HPG), f32), S((2, G, HPG, T), f32),
                 S((2, G, 1, HPG), f32), S((2, G, HPG, 1), f32), S((2, G, 1, HPG), f32), S((2, G, HPG, 1), f32)]
    return _host_call(body, (2, nc), in_specs, out_specs, out_shape, [pltpu.VMEM((G, GW, N), f32)],
                      ("arbitrary", "arbitrary"), name, (xbc, dtc, dtr, bc, br, alc, alr, s_in_all, dy), hosted)


GTB = 128


def _gate_norm_f(yf, yb, x, z, dexp, w):
    y = (yf + yb + dexp * x) * (z * _sigmoid(z))
    return y * lax.rsqrt(jnp.mean(y * y, axis=-1, keepdims=True) + EPS) * w


def ssd_gate_fwd(y2, xbc, proj, dexp, w, nctxb, name):
    T = xbc.shape[0]
    L = T - nctxb * GTB

    def body(yf_ref, yb_ref, x_ref, z_ref, d_ref, w_ref, o_ref):
        o_ref[...] = _gate_norm_f(yf_ref[...], yb_ref[...], x_ref[...], z_ref[...], d_ref[...], w_ref[...]).astype(bf16)

    wide = pl.BlockSpec((GTB, DI), lambda i: (i + nctxb, 0))
    row = pl.BlockSpec((1, DI), lambda i: (0, 0))
    return pl.pallas_call(
        body, grid=(L // GTB,),
        in_specs=[pl.BlockSpec((None, GTB, DI), lambda i: (0, i + nctxb, 0)),
                  pl.BlockSpec((None, GTB, DI), lambda i: (1, i + nctxb, 0)), wide, wide, row, row],
        out_specs=pl.BlockSpec((GTB, DI), lambda i: (i, 0)), out_shape=S((L, DI), bf16),
        compiler_params=_cparams("parallel"), name=name)(y2, y2, xbc, proj, dexp, w)


def ssd_gate_bwd(y2, xbc, proj, dexp, w, dyn, nctxb, name, hosted=None):
    T = xbc.shape[0]
    nb = T // GTB

    def body(yf_ref, yb_ref, x_ref, z_ref, d_ref, w_ref, dyn_ref, dy_ref, dz_ref, dd_ref, dw_ref):
        i = pl.program_id(0)

        @pl.when(i == 0)
        def _():
            dd_ref[...] = jnp.zeros_like(dd_ref)
            dw_ref[...] = jnp.zeros_like(dw_ref)

        @pl.when(i < nctxb)
        def _():
            dy_ref[...] = jnp.zeros_like(dy_ref)
            dz_ref[...] = jnp.zeros_like(dz_ref)

        @pl.when(i >= nctxb)
        def _():
            _, vjp = jax.vjp(_gate_norm_f, yf_ref[...], yb_ref[...], x_ref[...], z_ref[...], d_ref[...], w_ref[...])
            dyf, _, _, dz, dd, dw = vjp(dyn_ref[...].astype(f32))
            dy_ref[...] = dyf
            dz_ref[...] = dz.astype(bf16)
            fold = (lax.broadcasted_iota(jnp.int32, (DI, 128), 0) // P == lax.broadcasted_iota(jnp.int32, (DI, 128), 1))
            dd_ref[...] += jnp.dot(dd, fold.astype(f32), precision=HI, preferred_element_type=f32)
            dw_ref[...] += dw

    wide = pl.BlockSpec((GTB, DI), lambda i: (i, 0))
    row = pl.BlockSpec((1, DI), lambda i: (0, 0))
    hrow = pl.BlockSpec((1, 128), lambda i: (0, 0))
    return _host_call(
        body, (nb,),
        [pl.BlockSpec((None, GTB, DI), lambda i: (0, i, 0)), pl.BlockSpec((None, GTB, DI), lambda i: (1, i, 0)),
         wide, wide, row, row, pl.BlockSpec((GTB, DI), lambda i: (jnp.maximum(i - nctxb, 0), 0))],
        [wide, wide, hrow, row],
        [S((T, DI), f32), S((T, DI), bf16), S((1, 128), f32), S((1, DI), f32)],
        [], ("arbitrary",), name, (y2, y2, xbc, proj, dexp, w, dyn), hosted)


CROWS = 2 * N_DEV


def mod_fwd(c16, modw, name):
    nl, _, cols = modw.shape

    def body(c_ref, w_ref, o_ref):
        cv = c_ref[...]
        s = cv * _sigmoid(cv)
        for l in range(nl):
            o_ref[l] = jnp.dot(s, w_ref[l], precision=HI, preferred_element_type=f32)

    return pl.pallas_call(body, in_specs=[VMEM, VMEM], out_specs=VMEM, out_shape=S((nl, CROWS, cols), f32),
                          compiler_params=pltpu.CompilerParams(vmem_limit_bytes=VMEM_LIMIT_BYTES), name=name)(c16, modw)


def mod_bwd(c16, modw, dm_sh, dm_all, name):
    nl, _, cols = modw.shape

    def body(c_ref, w_ref, dm_ref, dmall_ref, dw_ref, dc_ref, db_ref):
        cv = c_ref[...]
        sg = _sigmoid(cv)
        s = cv * sg
        ds_dc = sg * (1.0 + cv * (1.0 - sg))
        is_ctx = lax.broadcasted_iota(jnp.int32, (CROWS, D), 0) >= N_DEV
        dc = jnp.zeros((1, D), f32)
        for l in range(nl):
            dm = dm_ref[l]
            dw_ref[l] = lax.dot_general(s, dm, (((0,), (0,)), ((), ())), precision=HI, preferred_element_type=f32)
            dsv = lax.dot_general(dm, w_ref[l], (((1,), (1,)), ((), ())), precision=HI, preferred_element_type=f32)
            dc = dc + jnp.sum(jnp.where(is_ctx, dsv * ds_dc, 0.0), axis=0, keepdims=True)
            db_ref[pl.ds(l, 1), :] = jnp.sum(dmall_ref[l], axis=0, keepdims=True)
        dc_ref[...] = dc

    return pl.pallas_call(
        body, in_specs=[VMEM, VMEM, VMEM, VMEM], out_specs=[VMEM, VMEM, VMEM],
        out_shape=[S(modw.shape, f32), S((1, D), f32), S((nl, 6 * D), f32)],
        compiler_params=pltpu.CompilerParams(vmem_limit_bytes=VMEM_LIMIT_BYTES), name=name)(c16, modw, dm_sh, dm_all)


def adamw(w, g, m, v, name):
    R, C = w.shape
    rb = R if R <= 512 else max(r_ for r_ in range(8, 513, 8) if R % r_ == 0)
    bc1 = 1.0 - ADAM_B1 ** ADAM_STEP
    bc2 = 1.0 - ADAM_B2 ** ADAM_STEP

    def body(w_ref, g_ref, m_ref, v_ref, d_ref, nm_ref, nv_ref):
        gv = g_ref[...]
        m_new = ADAM_B1 * m_ref[...] + (1.0 - ADAM_B1) * gv
        v_new = ADAM_B2 * v_ref[...] + (1.0 - ADAM_B2) * (gv * gv)
        m_hat = m_new / bc1
        v_hat = v_new / bc2
        d_ref[...] = -ADAM_LR * (m_hat / (jnp.sqrt(v_hat) + ADAM_EPS) + ADAM_WD * w_ref[...])
        nm_ref[...] = m_new
        nv_ref[...] = v_new

    blk = pl.BlockSpec((rb, C), lambda i: (i, 0))
    return pl.pallas_call(body, grid=(R // rb,), in_specs=[blk] * 4, out_specs=[blk] * 3,
                          out_shape=[S((R, C), f32)] * 3, compiler_params=_cparams("parallel"), name=name)(w, g, m, v)


def _me():
    return lax.axis_index("x"), lax.axis_index("y"), lax.axis_index("c")


def allgather_small(x, name, with_sum=False, after=None):
    r, w = x.shape
    extra = () if after is None else (after,)

    def body(x_ref, *refs):
        refs = refs[len(extra):]
        if with_sum:
            out_ref, sum_ref, send_sems, recv_sems = refs
        else:
            out_ref, send_sems, recv_sems = refs
        mx, my, mc = _me()
        me = 4 * mx + 2 * my + mc
        out_ref[me] = x_ref[...]
        peers = []
        for k in range(1, N_DEV):
            kx, ky, kc = (k >> 2) & 1, (k >> 1) & 1, k & 1
            peers.append((mx + kx - 2 * mx * kx, my + ky - 2 * my * ky, mc + kc - 2 * mc * kc))
        copies = []
        for k, peer in enumerate(peers):
            cp = pltpu.make_async_remote_copy(src_ref=x_ref, dst_ref=out_ref.at[me], send_sem=send_sems.at[k],
                                              recv_sem=recv_sems.at[k], device_id=peer, device_id_type=MESH)
            cp.start()
            copies.append(cp)
        for k, (px, py, pc) in enumerate(peers):
            pltpu.make_async_remote_copy(src_ref=x_ref, dst_ref=out_ref.at[4 * px + 2 * py + pc], send_sem=send_sems.at[k],
                                         recv_sem=recv_sems.at[k], device_id=(px, py, pc), device_id_type=MESH).wait_recv()
        for cp in copies:
            cp.wait_send()
        if with_sum:
            acc = out_ref[0]
            for j in range(1, N_DEV):
                acc = acc + out_ref[j]
            sum_ref[...] = acc

    out_shape = [S((N_DEV, r, w), f32)] + ([S((r, w), f32)] if with_sum else [])
    outs = pl.pallas_call(
        body, in_specs=[VMEM] + [ANY] * len(extra), out_specs=[VMEM] * len(out_shape), out_shape=out_shape,
        scratch_shapes=[pltpu.SemaphoreType.DMA((N_DEV - 1,)), pltpu.SemaphoreType.DMA((N_DEV - 1,))],
        compiler_params=pltpu.CompilerParams(vmem_limit_bytes=VMEM_LIMIT_BYTES), name=name)(x, *extra)
    return outs if with_sum else outs[0]


def _tile2d(R, W, max_rows):
    if R <= max_rows:
        return R, W
    fits = [r_ for r_ in range(16, max_rows + 1, 16) if R % r_ == 0]
    return (max(fits), W) if fits else (R, 256)


def add_own(g, r, core, name, twice=False):
    _, _, R, W = g.shape
    rb, wb = _tile2d(R, W, 512)
    nout = 2 if twice else 1

    def body(core_ref, a_ref, b_ref, *o_refs):
        s = (a_ref[...].astype(f32) + b_ref[...].astype(f32)).astype(bf16)
        for o_ref in o_refs:
            o_ref[...] = s

    blk = pl.BlockSpec((None, rb, wb), lambda k, i, j, core_ref: (k, i, j))
    gs = pltpu.PrefetchScalarGridSpec(
        num_scalar_prefetch=1, grid=(4, R // rb, W // wb),
        in_specs=[pl.BlockSpec((None, None, rb, wb), lambda k, i, j, core_ref: (k, core_ref[0], i, j)), blk],
        out_specs=[blk] * nout)
    outs = pl.pallas_call(body, grid_spec=gs, out_shape=[S((4, R, W), bf16)] * nout,
                          compiler_params=_cparams("parallel", "parallel", "parallel"), name=name)(core, g, r)
    return tuple(outs) if twice else outs[0]


HBM_SPEC = pl.BlockSpec(memory_space=pltpu.HBM)
SEM_SPEC = pl.BlockSpec(memory_space=pltpu.SEMAPHORE)


def _chips_copy(p_ref, land_ref, send_sems, recv_sems, a, j):
    x, y, c = _me()
    px, py = [(1 - x, y), (x, 1 - y), (1 - x, 1 - y)][j]
    return pltpu.make_async_remote_copy(src_ref=p_ref.at[2 * px + py], dst_ref=land_ref.at[2 * x + y],
                                        send_sem=send_sems.at[3 * a + j], recv_sem=recv_sems.at[3 * a + j],
                                        device_id=(px, py, c), device_id_type=MESH)


def _chips_wait_copy(p_ref, land_ref, send_sems, recv_sems, a, j):
    x, y, c = _me()
    px, py = [(1 - x, y), (x, 1 - y), (1 - x, 1 - y)][j]
    return pltpu.make_async_remote_copy(src_ref=p_ref.at[2 * px + py], dst_ref=land_ref.at[2 * px + py],
                                        send_sem=send_sems.at[3 * a + j], recv_sem=recv_sems.at[3 * a + j],
                                        device_id=(px, py, c), device_id_type=MESH)


def _xor_peers():
    mx, my, mc = _me()
    peers = []
    for k in range(1, N_DEV):
        kx, ky, kc = (k >> 2) & 1, (k >> 1) & 1, k & 1
        peers.append((mx + kx - 2 * mx * kx, my + ky - 2 * my * ky, mc + kc - 2 * mc * kc))
    return peers


def gather_start(shards, name, after):
    na = len(shards)
    lands = [lax.empty((N_DEV,) + s_.shape, s_.dtype) for s_ in shards]

    def body(*refs):
        x_refs, land_refs = refs[:na], refs[na:2 * na]
        send_sems, recv_sems, local_sems = refs[2 * na + 1:2 * na + 4]
        token = refs[-1]
        mx, my, mc = _me()
        me = 4 * mx + 2 * my + mc
        for a in range(na):
            pltpu.make_async_copy(x_refs[a], land_refs[a].at[me], local_sems.at[a]).start()
            for k, peer in enumerate(_xor_peers()):
                pltpu.make_async_remote_copy(src_ref=x_refs[a], dst_ref=land_refs[a].at[me], send_sem=send_sems.at[7 * a + k],
                                             recv_sem=recv_sems.at[7 * a + k], device_id=peer, device_id_type=MESH).start()
        token[...] = jnp.zeros_like(token)

    arrs = list(shards) + lands
    outs = pl.pallas_call(
        body, name=name, in_specs=[HBM_SPEC] * (2 * na) + [ANY],
        out_shape=[DMA((7 * na,)), DMA((7 * na,)), DMA((na,))] + [pltpu.HBM(t.shape, t.dtype) for t in arrs]
        + [S((8, 128), f32)],
        out_specs=[SEM_SPEC] * 3 + [HBM_SPEC] * (2 * na) + [VMEM],
        input_output_aliases={k: 3 + k for k in range(2 * na)},
        compiler_params=pltpu.CompilerParams(has_side_effects=pltpu.SideEffectType.DATAFLOW_SIDE_EFFECTING),
    )(*[pltpu.with_memory_space_constraint(t, pltpu.HBM) for t in arrs], after)
    return outs[0], outs[1], outs[2], list(outs[3:3 + na]), list(outs[3 + na:3 + 2 * na]), outs[-1]


def gather_wait(send_sems, recv_sems, local_sems, shards, lands, after, name):
    na = len(shards)

    def body(*refs):
        x_refs, land_refs = refs[:na], refs[na:2 * na]
        ssem, rsem, lsem = refs[2 * na:2 * na + 3]
        mx, my, mc = _me()
        me = 4 * mx + 2 * my + mc
        for a in range(na):
            pltpu.make_async_copy(x_refs[a], land_refs[a].at[me], lsem.at[a]).wait()
            for k, (px, py, pc) in enumerate(_xor_peers()):
                cp = pltpu.make_async_remote_copy(src_ref=x_refs[a], dst_ref=land_refs[a].at[4 * px + 2 * py + pc],
                                                  send_sem=ssem.at[7 * a + k], recv_sem=rsem.at[7 * a + k],
                                                  device_id=(px, py, pc), device_id_type=MESH)
                cp.wait_send()
                cp.wait_recv()

    arrs = list(shards) + list(lands)
    outs = pl.pallas_call(
        body, name=name, in_specs=[HBM_SPEC] * (2 * na) + [SEM_SPEC] * 3 + [ANY],
        out_shape=[pltpu.HBM(t.shape, t.dtype) for t in arrs], out_specs=[HBM_SPEC] * (2 * na),
        input_output_aliases={k: k for k in range(2 * na)},
        compiler_params=pltpu.CompilerParams(has_side_effects=pltpu.SideEffectType.DATAFLOW_SIDE_EFFECTING),
    )(*arrs, send_sems, recv_sems, local_sems, after)
    return list(outs[na:])


def chips_start(parts, lands, name):
    na = len(parts)

    def body(*refs):
        p_refs, land_refs = refs[:na], refs[na:2 * na]
        send_sems, recv_sems = refs[2 * na], refs[2 * na + 1]
        token = refs[-1]
        for a in range(na):
            for j in range(3):
                _chips_copy(p_refs[a], land_refs[a], send_sems, recv_sems, a, j).start()
        token[...] = jnp.zeros_like(token)

    arrs = list(parts) + list(lands)
    outs = pl.pallas_call(
        body, name=name, in_specs=[HBM_SPEC] * (2 * na),
        out_shape=[DMA((3 * na,)), DMA((3 * na,))] + [pltpu.HBM(t.shape, t.dtype) for t in arrs] + [S((8, 128), f32)],
        out_specs=[SEM_SPEC, SEM_SPEC] + [HBM_SPEC] * (2 * na) + [VMEM],
        input_output_aliases={k: 2 + k for k in range(2 * na)},
        compiler_params=pltpu.CompilerParams(has_side_effects=pltpu.SideEffectType.DATAFLOW_SIDE_EFFECTING),
    )(*[pltpu.with_memory_space_constraint(t, pltpu.HBM) for t in arrs])
    return outs[0], outs[1], list(outs[2:2 + na]), list(outs[2 + na:2 + 2 * na]), outs[-1]


def chips_wait(send_sems, recv_sems, parts, lands, after, name):
    na = len(parts)

    def body(*refs):
        p_refs, land_refs = refs[:na], refs[na:2 * na]
        ssem, rsem = refs[2 * na], refs[2 * na + 1]
        for a in range(na):
            for j in range(3):
                cp = _chips_wait_copy(p_refs[a], land_refs[a], ssem, rsem, a, j)
                cp.wait_send()
                cp.wait_recv()

    arrs = list(parts) + list(lands)
    outs = pl.pallas_call(
        body, name=name, in_specs=[HBM_SPEC] * (2 * na) + [SEM_SPEC, SEM_SPEC, ANY],
        out_shape=[pltpu.HBM(t.shape, t.dtype) for t in arrs], out_specs=[HBM_SPEC] * (2 * na),
        input_output_aliases={k: k for k in range(2 * na)},
        compiler_params=pltpu.CompilerParams(has_side_effects=pltpu.SideEffectType.DATAFLOW_SIDE_EFFECTING),
    )(*arrs, send_sems, recv_sems, after)
    return list(outs[na:])


def sum_adamw(recv, w, m, v, layer, name, into=None, after=None):
    _, R, W = recv.shape
    rb, wb = _tile2d(R, W, 256)
    bc1 = 1.0 - ADAM_B1 ** ADAM_STEP
    bc2 = 1.0 - ADAM_B2 ** ADAM_STEP
    n_into = 0 if into is None else 4
    extra = () if after is None else (after,)

    def body(r_ref, w_ref, m_ref, v_ref, *refs):
        g_ref, d_ref, nm_ref, nv_ref = refs[n_into + len(extra):]
        gv = r_ref[0].astype(f32)
        for k in range(1, 4):
            gv = gv + r_ref[k].astype(f32)
        m_new = ADAM_B1 * m_ref[...] + (1.0 - ADAM_B1) * gv
        v_new = ADAM_B2 * v_ref[...] + (1.0 - ADAM_B2) * (gv * gv)
        g_ref[...] = gv
        d_ref[...] = -ADAM_LR * ((m_new / bc1) / (jnp.sqrt(v_new / bc2) + ADAM_EPS) + ADAM_WD * w_ref[...])
        nm_ref[...] = m_new
        nv_ref[...] = v_new

    if layer is None:
        wblk = pl.BlockSpec((rb, wb), lambda i, j: (i, j))
        oshape = S((R, W), f32)
    else:
        wblk = pl.BlockSpec((None, rb, wb), lambda i, j: (layer, i, j))
        oshape = S(w.shape, f32)
    return pl.pallas_call(
        body, grid=(R // rb, W // wb),
        in_specs=[pl.BlockSpec((4, rb, wb), lambda i, j: (0, i, j)), wblk, wblk, wblk] + [ANY] * (n_into + len(extra)),
        out_specs=[wblk] * 4, out_shape=[oshape] * 4, input_output_aliases={4 + k: k for k in range(n_into)},
        compiler_params=_cparams("parallel", "parallel"), name=name)(recv, w, m, v, *(into or ()), *extra)


def sum_rows(a, name):
    K, R, W = a.shape
    rb = _pick(R, (512, 256, 128, 64, 32, 16))

    def body(a_ref, o_ref):
        acc = a_ref[0].astype(f32)
        for k in range(1, K):
            acc = acc + a_ref[k].astype(f32)
        o_ref[...] = acc

    return pl.pallas_call(body, grid=(R // rb,), in_specs=[pl.BlockSpec((K, rb, W), lambda i: (0, i, 0))],
                          out_specs=pl.BlockSpec((rb, W), lambda i: (i, 0)), out_shape=S((R, W), f32),
                          compiler_params=_cparams("parallel"), name=name)(a)


DMA = pltpu.SemaphoreType.DMA


class GatherExchange:
    def __init__(self, arrays):
        self.arrays = list(arrays)
        self.na = len(self.arrays)
        self.out_shape = [S((N_DEV,) + a.shape, a.dtype) for a in self.arrays]
        self.scratch = [DMA((7 * self.na,)), DMA((7 * self.na,)), DMA((self.na,))]

    def ops(self, x_refs, out_refs, sems):
        send_sems, recv_sems, local_sems = sems
        na = self.na
        x, y, c = _me()
        me, sibling = (x, y, c), (x, y, 1 - c)
        chips = [(1 - x, y), (x, 1 - y), (1 - x, 1 - y)]

        def rows(a, px, py, pc):
            return out_refs[a].at[4 * px + 2 * py + pc]

        def copy(a, k, block, to, src=None):
            return pltpu.make_async_remote_copy(
                src_ref=rows(a, *block) if src is None else src, dst_ref=rows(a, *block),
                send_sem=send_sems.at[7 * a + k], recv_sem=recv_sems.at[7 * a + k], device_id=to, device_id_type=MESH)

        def local(a):
            return pltpu.make_async_copy(x_refs[a], rows(a, *me), local_sems.at[a])

        def first(a):
            return [copy(a, 0, me, sibling, src=x_refs[a])] + [copy(a, 1 + j, me, (*chip, c), src=x_refs[a])
                                                                for j, chip in enumerate(chips)]

        def start():
            for a in range(na):
                local(a).start()
                for cp in first(a):
                    cp.start()

        def mid():
            for a in range(na):
                for j, chip in enumerate(chips):
                    copy(a, 1 + j, (*chip, c), me).wait_recv()
                    copy(a, 4 + j, (*chip, c), sibling).start()

        def finish():
            for a in range(na):
                copy(a, 0, sibling, me).wait_recv()
                for j, chip in enumerate(chips):
                    copy(a, 4 + j, (*chip, 1 - c), me).wait_recv()
                for cp in first(a) + [copy(a, 4 + j, (*chip, c), sibling) for j, chip in enumerate(chips)]:
                    cp.wait_send()
                local(a).wait()

        return start, mid, finish


class SiblingExchange:
    def __init__(self, arrays):
        self.arrays = list(arrays)
        self.na = len(self.arrays)
        self.out_shape = [S((4,) + g.shape[2:], g.dtype) for g in self.arrays]
        self.scratch = [DMA((self.na,)), DMA((self.na,))]

    def ops(self, g_refs, out_refs, sems):
        send_sems, recv_sems = sems
        x, y, c = _me()

        def copy(a):
            return pltpu.make_async_remote_copy(src_ref=g_refs[a].at[:, 1 - c], dst_ref=out_refs[a],
                                                send_sem=send_sems.at[a], recv_sem=recv_sems.at[a],
                                                device_id=(x, y, 1 - c), device_id_type=MESH)

        def start():
            for a in range(self.na):
                copy(a).start()

        def finish():
            for a in range(self.na):
                copy(a).wait()

        return start, None, finish


class ChipsExchange:
    def __init__(self, arrays):
        self.arrays = list(arrays)
        self.na = len(self.arrays)
        self.out_shape = [S(p.shape, p.dtype) for p in self.arrays]
        self.scratch = [DMA((3 * self.na,)), DMA((3 * self.na,)), DMA((self.na,))]

    def ops(self, p_refs, out_refs, sems):
        send_sems, recv_sems, local_sems = sems
        x, y, c = _me()
        mine = 2 * x + y
        chips = [(1 - x, y), (x, 1 - y), (1 - x, 1 - y)]

        def local(a):
            return pltpu.make_async_copy(p_refs[a].at[mine], out_refs[a].at[mine], local_sems.at[a])

        def send(a, j):
            px, py = chips[j]
            return pltpu.make_async_remote_copy(src_ref=p_refs[a].at[2 * px + py], dst_ref=out_refs[a].at[mine],
                                                send_sem=send_sems.at[3 * a + j], recv_sem=recv_sems.at[3 * a + j],
                                                device_id=(px, py, c), device_id_type=MESH)

        def recv(a, j):
            px, py = chips[j]
            return pltpu.make_async_remote_copy(src_ref=p_refs[a].at[mine], dst_ref=out_refs[a].at[2 * px + py],
                                                send_sem=send_sems.at[3 * a + j], recv_sem=recv_sems.at[3 * a + j],
                                                device_id=(px, py, c), device_id_type=MESH)

        def start():
            for a in range(self.na):
                local(a).start()
                for j in range(3):
                    send(a, j).start()

        def finish():
            for a in range(self.na):
                for j in range(3):
                    recv(a, j).wait_recv()
                for j in range(3):
                    send(a, j).wait_send()
                local(a).wait()

        return start, None, finish


def exchange(ex, name):
    na = ex.na

    def body(*refs):
        start, mid, finish = ex.ops(refs[:na], refs[na:2 * na], refs[2 * na:])
        start()
        if mid is not None:
            mid()
        finish()

    return pl.pallas_call(body, in_specs=[ANY] * na, out_specs=[ANY] * na, out_shape=ex.out_shape,
                          scratch_shapes=ex.scratch, name=name)(*ex.arrays)


def _host_call(body, grid, in_specs, out_specs, out_shape, scratch_shapes, sem, name, args, hosted):
    if hosted is None:
        res = pl.pallas_call(body, grid=grid, in_specs=in_specs, out_specs=out_specs, out_shape=out_shape,
                             scratch_shapes=scratch_shapes, compiler_params=_cparams(*sem), name=name)(*args)
        return res, None
    n_in, n_out, n_sc, na = len(in_specs), len(out_shape), len(scratch_shapes), hosted.na
    nsteps = 1
    for g_ in grid:
        nsteps *= g_
    mid_step = (3 * nsteps) // 4
    i1 = n_in + na
    i2 = i1 + n_out
    i3 = i2 + na
    i4 = i3 + n_sc

    def wrapped(*refs):
        step = pl.program_id(0)
        for ax in range(1, len(grid)):
            step = step * grid[ax] + pl.program_id(ax)
        start, mid, finish = hosted.ops(refs[n_in:i1], refs[i2:i3], refs[i4:])
        pl.when(step == 0)(start)
        if mid is not None:
            pl.when(step == mid_step)(mid)
        body(*refs[:n_in], *refs[i1:i2], *refs[i3:i4])
        pl.when(step == nsteps - 1)(finish)

    res = pl.pallas_call(
        wrapped, grid=grid, in_specs=list(in_specs) + [ANY] * na, out_specs=list(out_specs) + [ANY] * na,
        out_shape=list(out_shape) + hosted.out_shape, scratch_shapes=list(scratch_shapes) + hosted.scratch,
        compiler_params=_cparams(*(("arbitrary",) * len(grid))), name=name)(*args, *hosted.arrays)
    return res[:n_out], res[n_out:]


PACK_ALIGN = 16 * PACK_W


def _pad_to(v, mult):
    n = v.shape[-1]
    extra = (-n) % mult
    if extra == 0:
        return v
    return jnp.concatenate([v, jnp.zeros(v.shape[:-1] + (extra,), v.dtype)], axis=-1)


def _f32_as_bf16_pairs(v):
    return lax.bitcast_convert_type(v.reshape(-1), bf16).reshape(-1)


def _bf16_pairs_as_f32(v):
    return lax.bitcast_convert_type(v.reshape(v.shape[:-1] + (v.shape[-1] // 2, 2)), f32)


def _col_shards(gw):
    lead = gw.shape[:-1]
    n = gw.shape[-1] // N_DEV
    t = gw.reshape(lead + (N_DEV, n))
    t = jnp.moveaxis(t, -2, 0)
    return t.reshape(N_DEV, -1)


def kernel(x, c, ctx, c_ctx, mod_w, mod_b, norm1_w, norm2_w, ssd_w_in, ssd_conv_w, ssd_conv_b, ssd_dt_bias, ssd_a_log, ssd_d, ssd_norm_w, ssd_w_out, conf_w_pw1, conf_b_pw1, conf_w_dw, conf_b_dw, conf_ln_w, conf_ln_b, conf_w_pw2, conf_b_pw2, ffn_w_up, ffn_conv_w, ffn_conv_b, ffn_w_down, final_norm_w, loss_target, m_c_ctx, m_mod_w, m_mod_b, m_norm1_w, m_norm2_w, m_ssd_w_in, m_ssd_conv_w, m_ssd_conv_b, m_ssd_dt_bias, m_ssd_a_log, m_ssd_d, m_ssd_norm_w, m_ssd_w_out, m_conf_w_pw1, m_conf_b_pw1, m_conf_w_dw, m_conf_b_dw, m_conf_ln_w, m_conf_ln_b, m_conf_w_pw2, m_conf_b_pw2, m_ffn_w_up, m_ffn_conv_w, m_ffn_conv_b, m_ffn_w_down, m_final_norm_w, v_c_ctx, v_mod_w, v_mod_b, v_norm1_w, v_norm2_w, v_ssd_w_in, v_ssd_conv_w, v_ssd_conv_b, v_ssd_dt_bias, v_ssd_a_log, v_ssd_d, v_ssd_norm_w, v_ssd_w_out, v_conf_w_pw1, v_conf_b_pw1, v_conf_w_dw, v_conf_b_dw, v_conf_ln_w, v_conf_ln_b, v_conf_w_pw2, v_conf_b_pw2, v_ffn_w_up, v_ffn_conv_w, v_ffn_conv_b, v_ffn_w_down, v_final_norm_w):
    mx, my, mc = _me()
    me = 4 * mx + 2 * my + mc
    L = x.shape[1]
    LC = ctx.shape[1]
    T = LC + L
    w_in_cols = ssd_w_in.shape[2] * N_DEV
    n_dt = w_in_cols - DI - CONVD

    small = [c[0], ssd_conv_w[0], conf_b_pw1[0], conf_w_dw[0], conf_b_dw[0], conf_ln_w[0], conf_ln_b[0], conf_b_pw2[0],
             ffn_conv_w]
    parts = [_f32_as_bf16_pairs(t) for t in small]
    sizes = [p.shape[0] for p in parts]
    small_flat = _pad_to(jnp.concatenate(parts), PACK_ALIGN).reshape(-1, PACK_W)
    w_in, small_g = exchange(GatherExchange([ssd_w_in[0].astype(bf16), small_flat]), "gather_first")
    w_up, w_down = [None, None], [None, None]
    small_g = small_g.reshape(N_DEV, -1)
    offs = [0]
    for s_ in sizes:
        offs.append(offs[-1] + s_)
    sm = [_bf16_pairs_as_f32(small_g[:, offs[i]:offs[i + 1]]) for i in range(len(sizes))]

    def cols(pc, K):
        return jnp.moveaxis(pc.reshape(N_DEV, K, -1), 0, 1).reshape(K, -1)

    c_all = sm[0]
    conv_w5 = cols(sm[1], 5)
    b_pw1 = sm[2].reshape(1, 2 * D)
    w_dw = cols(sm[3], CONF_K)
    b_dw, ln_w, ln_b, b_pw2 = (sm[i].reshape(1, D) for i in (4, 5, 6, 7))
    fcw = sm[8].reshape(N_DEV, 2, 9, FH // N_DEV)
    ffn_cw = [cols(fcw[:, i].reshape(N_DEV, -1), 9) for i in range(2)]
    in_segs = (DI, CONVD, n_dt)
    up_segs = (FH, FH)
    pw1_segs = (D, D)

    c16 = jnp.concatenate([c_all, jnp.broadcast_to(c_ctx[None, :], (N_DEV, D))], axis=0)
    m_sh = mod_fwd(c16, mod_w, "mod_fwd")
    mod_cols = mod_w.shape[2]
    m_gath = allgather_small(m_sh.reshape(2 * CROWS, mod_cols), "gather_mod")
    fly_a = gather_start([ssd_w_out[0].astype(bf16), ffn_w_up[0].astype(bf16), ffn_w_down[0].astype(bf16)],
                         "gather_a_start", m_gath)
    fly_b = gather_start([conf_w_pw1[0].astype(bf16), conf_w_pw2[0].astype(bf16)], "gather_b_start", fly_a[-1])
    fly_c = gather_start([ffn_w_up[1].astype(bf16), ffn_w_down[1].astype(bf16)], "gather_c_start", fly_b[-1])
    m_all = jnp.moveaxis(m_gath.reshape(N_DEV, 2, CROWS, mod_cols), 0, 2).reshape(2, CROWS, 6 * D) + mod_b[:, None, :]
    m_all = m_all + fly_c[-1][0, 0]
    m_lat = lax.dynamic_index_in_dim(m_all, me, axis=1, keepdims=False).reshape(2, 6, 1, D)
    m_ctx = m_all[:, N_DEV].reshape(2, 6, 1, D)
    zero_row = jnp.zeros((1, D), f32)

    def ffn_fwd(a2, i, tag):
        val, gate = smm_fwd(a2, w_up[i], None, up_segs, f"ffn{tag}_up")
        act, gs_, gvds, _ = ffn_gate_fwd(val, gate, ffn_cw[i], ffn_conv_b[i][None], f"ffn{tag}_gate")
        o2 = matmul(act, w_down[i], "nn", f32, f"ffn{tag}_down")
        return o2, (a2, gate, gs_, gvds, act)

    def ffn_bwd(do2, i, saved, tag):
        a2, gate, gs_, gvds, act = saved
        g_down = matmul(act, do2, "tn", bf16, f"ffn{tag}_down_dw")
        dact = matmul(do2, w_down[i], "nt", bf16, f"ffn{tag}_down_dx")
        dval, dgate, dcw, dcb = ffn_gate_bwd(gate, gs_, gvds, ffn_cw[i], dact, f"ffn{tag}_gate_bwd")
        g_up = smm_dw(a2, [dval, dgate], FH // 4, up_segs, 2, True, f"ffn{tag}_up_dw")
        da2, _ = smm_dx([dval, dgate], w_up[i], None, up_segs, bf16, f"ffn{tag}_up_dx")
        return da2, dict(w_up=g_up, w_down=g_down, conv_w=dcw, conv_b=dcb)

    nctx = LC // Q
    hx = x[0]
    sc0 = jnp.stack([m_ctx[0, 1], m_lat[0, 1]])
    sh0 = jnp.stack([m_ctx[0, 0], m_lat[0, 0]])
    a0 = modnorm_fwd(hx, norm1_w[0][None], sc0, sh0, LC // TB, "ssd_norm", ctx=ctx[0])
    z, xbc_pre, dt_raw = smm_fwd(a0, w_in, None, in_segs, "ssd_in")
    segs = ((0, LC), (LC, L))
    xbc, xbc_dsilu, _ = ssd_conv_fwd(xbc_pre, conv_w5, ssd_conv_b, segs, "ssd_conv")
    dt4 = dt_raw[:, :n_dt].reshape(T, 2, G, HPG)
    dtc = jnp.transpose(dt4, (1, 2, 0, 3))
    dtr = jnp.transpose(dt4, (1, 2, 3, 0))
    bias3 = ssd_dt_bias[0].reshape(2, G, HPG)
    alog3 = ssd_a_log[0].reshape(2, G, HPG)
    bc_, br_ = bias3[:, :, None, :], bias3[:, :, :, None]
    alc, alr = alog3[:, :, None, :], alog3[:, :, :, None]
    (y2, s_in_all), _ = ssd_scan_fwd(xbc, dtc, dtr, bc_, br_, alc, alr, nctx, "ssd_scan")
    dexp = jnp.repeat(ssd_d[0], P)[None, :]
    yn = ssd_gate_fwd(y2, xbc, z, dexp, ssd_norm_w, LC // GTB, "ssd_gate")
    w_out_g, w_up[0], w_down0_g = gather_wait(*fly_a[:5], yn, "gather_a_wait")
    w_out = w_out_g.reshape(DI, D)
    w_down[0] = w_down0_g.reshape(FH, D)
    o_ssd = matmul(yn, w_out, "nn", f32, "ssd_out")
    h1, a2_0 = resnorm_fwd(hx, o_ssd, m_lat[0, 2], zero_row, norm2_w[0][None], m_lat[0, 4], m_lat[0, 3], "ssd_res")
    o2_0, ffn0_saved = ffn_fwd(a2_0, 0, "0")

    h2, a1 = resnorm_fwd(h1, o2_0, m_lat[0, 5], zero_row, norm1_w[1][None], m_lat[1, 1], m_lat[1, 0], "ffn0_res")
    w_pw1, w_pw2_g = gather_wait(*fly_b[:5], a1, "gather_b_wait")
    w_pw2 = w_pw2_g.reshape(D, D)
    pa, pg = smm_fwd(a1, w_pw1, None, pw1_segs, "conf_pw1")
    dwc, _ = conf_glu_conv_fwd(pa, pg, b_pw1, w_dw, b_dw, "conf_conv")
    s1 = ln_silu_fwd(dwc, ln_w, ln_b, "conf_ln")
    o_conf = matmul(s1, w_pw2, "nn", f32, "conf_pw2")
    h3, a2_1 = resnorm_fwd(h2, o_conf, m_lat[1, 2], b_pw2, norm2_w[1][None], m_lat[1, 4], m_lat[1, 3], "conf_res")
    w_up[1], w_down1_g = gather_wait(*fly_c[:5], h3, "gather_c_wait")
    w_down[1] = w_down1_g.reshape(FH, D)
    o2_1, ffn1_saved = ffn_fwd(a2_1, 1, "1")

    loss_part, dh4, g_final, do2_1, dg2_1 = final_loss(h3, o2_1, m_lat[1, 5], final_norm_w[None], loss_target[0],
                                                       "loss_head")
    da2_1, gf1 = ffn_bwd(do2_1, 1, ffn1_saved, "1")
    dh3, dn2_1, dsc2_1, dsh2_1, do_conf, dg1_1, g_b_pw2 = normres_bwd(
        h3, norm2_w[1][None], m_lat[1, 4], m_lat[1, 3], da2_1, dh4, o_conf, m_lat[1, 2], b_pw2, "ffn1_norm_bwd")
    gf1.update(norm2=dn2_1, sh2=dsh2_1, sc2=dsc2_1, g2=dg2_1)
    g_pw2 = matmul(s1, do_conf, "tn", bf16, "conf_pw2_dw")
    ds1 = matmul(do_conf, w_pw2, "nt", bf16, "conf_pw2_dx")
    ddwc, g_ln_w, g_ln_b = ln_silu_bwd(dwc, ln_w, ln_b, ds1, "conf_ln_bwd")
    dpa, dpg, dba, dbg, g_w_dw, g_b_dw = conf_glu_conv_bwd(pa, pg, b_pw1, w_dw, ddwc, "conf_conv_bwd")
    g_b_pw1 = jnp.concatenate([dba, dbg], axis=1)
    g_pw1 = smm_dw(a1, [dpa, dpg], 2 * D // N_DEV, pw1_segs, 1, False, "conf_pw1_dw")
    da1, _ = smm_dx([dpa, dpg], w_pw1, None, pw1_segs, bf16, "conf_pw1_dx")
    dh2, g_n1_1, dsc1_1, dsh1_1, do2_0, dg2_0, _ = normres_bwd(
        h2, norm1_w[1][None], m_lat[1, 1], m_lat[1, 0], da1, dh3, o2_0, m_lat[0, 5], zero_row, "conf_norm_bwd")
    da2_0, gf0 = ffn_bwd(do2_0, 0, ffn0_saved, "0")
    dh1, dn2_0, dsc2_0, dsh2_0, do_ssd, dg1_0, _ = normres_bwd(
        h1, norm2_w[0][None], m_lat[0, 4], m_lat[0, 3], da2_0, dh2, o_ssd, m_lat[0, 2], zero_row, "ffn0_norm_bwd")
    gf0.update(norm2=dn2_0, sh2=dsh2_0, sc2=dsc2_0, g2=dg2_0)
    g_w_out = matmul(yn, do_ssd, "tn", bf16, "ssd_out_dw")
    dyn = matmul(do_ssd, w_out, "nt", bf16, "ssd_out_dx")
    core = mc.reshape(1).astype(jnp.int32)

    def by_device(t):
        return t.reshape((4, 2, -1, t.shape[-1]))

    early = [by_device(t) for t in (gf1["w_up"], gf1["w_down"], g_pw2, g_pw1, gf0["w_up"], gf0["w_down"], g_w_out)]
    (dy, dz, g_dexp, g_ssd_norm), early_sib = ssd_gate_bwd(
        y2, xbc, z, dexp, ssd_norm_w, dyn, LC // GTB, "ssd_gate_bwd", SiblingExchange(early))
    early_part = [add_own(t, r_, core, f"reduce_add{i}") for i, (t, r_) in enumerate(zip(early, early_sib))]
    (dxbc2, ddtc, ddtr, dbc, dbr, dalc, dalr), early_red = ssd_scan_bwd(
        xbc, dtc, dtr, bc_, br_, alc, alr, s_in_all, dy, nctx, "ssd_scan_bwd", ChipsExchange(early_part))
    ddt = (jnp.transpose(ddtc, (2, 0, 1, 3)) + jnp.transpose(ddtr, (3, 0, 1, 2))).reshape(T, n_dt)
    g_dt_bias = (dbc[:, :, 0, :] + dbr[:, :, :, 0]).reshape(2, NH_SSD)
    g_a_log = (dalc[:, :, 0, :] + dalr[:, :, :, 0]).reshape(2, NH_SSD)
    g_ssd_d = g_dexp[0, :NH_SSD]
    du, g_conv_w5, g_conv_b5 = ssd_conv_bwd(xbc_pre, conv_w5, xbc_dsilu, dxbc2, dy, dexp, segs, "ssd_conv_bwd")
    ddt_p = _pad_to(ddt, 128).astype(bf16)
    g_w_in = smm_dw(a0, [dz, du, ddt_p], w_in.shape[-1], in_segs, 2, True, "ssd_in_dw")
    g_ffn_cw = jnp.stack([gf0["conv_w"], gf1["conv_w"]])
    small_shards = [_col_shards(t) for t in (g_conv_w5, g_b_pw1, g_w_dw, g_b_dw, g_ln_w, g_ln_b, g_b_pw2, g_ffn_cw)]
    gsizes = [s_.shape[1] for s_ in small_shards]
    g_small = _pad_to(jnp.concatenate(small_shards, axis=1), PACK_ALIGN).astype(bf16)
    late = [by_device(g_w_in), by_device(g_small.reshape(N_DEV, -1, PACK_W))]
    da0, late_sib = smm_dx([dz, du, ddt_p], w_in, None, in_segs, f32, "ssd_in_dx", SiblingExchange(late))
    late_part = [add_own(t, r_, core, f"reduce_add_late{i}", twice=True) for i, (t, r_) in enumerate(zip(late, late_sib))]
    late_flying = chips_start([p_[0] for p_ in late_part], [p_[1] for p_ in late_part], "reduce_chips_late_start")
    dh0, g_n1_0, dsc1_0, dsh1_0 = modnorm_bwd(hx, norm1_w[0][None], sc0, sh0, da0, dh1, LC // TB, "ssd_norm_bwd",
                                              ctx=ctx[0])
    grad_x = dh0[None]

    r_up1, r_down1, r_pw2, r_pw1, r_up0, r_down0, r_out = early_red
    big = {}
    def tr(t):
        return jnp.swapaxes(t, -1, -2)

    up_t, m_up_t, v_up_t = tr(ffn_w_up), tr(m_ffn_w_up), tr(v_ffn_w_up)
    send_sems, recv_sems, late_p, late_land, token = late_flying
    up0 = sum_adamw(r_up0, up_t, m_up_t, v_up_t, 0, "adamw_ffn_w_up0", after=token)
    up1 = sum_adamw(r_up1, up_t, m_up_t, v_up_t, 1, "adamw_ffn_w_up1", into=up0)
    big["ffn_w_up"] = tuple(tr(t) for t in up1)
    big["conf_w_pw1"] = sum_adamw(r_pw1, conf_w_pw1[0], m_conf_w_pw1[0], v_conf_w_pw1[0], None, "adamw_conf_w_pw1",
                                  after=up1[0])
    big["ssd_w_out"] = sum_adamw(r_out, ssd_w_out[0], m_ssd_w_out[0], v_ssd_w_out[0], None, "adamw_ssd_w_out",
                                 after=big["conf_w_pw1"][0])
    dn0 = sum_adamw(r_down0, ffn_w_down, m_ffn_w_down, v_ffn_w_down, 0, "adamw_ffn_w_down0", after=big["ssd_w_out"][0])
    big["ffn_w_down"] = sum_adamw(r_down1, ffn_w_down, m_ffn_w_down, v_ffn_w_down, 1, "adamw_ffn_w_down1", into=dn0)
    big["conf_w_pw2"] = sum_adamw(r_pw2, conf_w_pw2[0], m_conf_w_pw2[0], v_conf_w_pw2[0], None, "adamw_conf_w_pw2",
                                  after=big["ffn_w_down"][0])

    zeros_d = jnp.zeros((1, D), f32)
    dm_lat = jnp.stack([
        jnp.concatenate([dsh1_0[1], dsc1_0[1], dg1_0, gf0["sh2"], gf0["sc2"], gf0["g2"]], axis=1),
        jnp.concatenate([dsh1_1, dsc1_1, dg1_1, gf1["sh2"], gf1["sc2"], gf1["g2"]], axis=1)])
    dm_ctx = jnp.stack([
        jnp.concatenate([dsh1_0[0], dsc1_0[0]] + [zeros_d] * 4, axis=1), jnp.zeros((1, 6 * D), f32)])
    dm_mine = jnp.concatenate([dm_lat.reshape(2, 6 * D), dm_ctx.reshape(2, 6 * D),
                               jnp.zeros((4, 6 * D), f32)], axis=0)
    dm_g = allgather_small(dm_mine, "gather_dmod", after=big["conf_w_pw2"][0])
    dm_all = jnp.concatenate([jnp.moveaxis(dm_g[:, 0:2], 0, 1), jnp.moveaxis(dm_g[:, 2:4], 0, 1)], axis=1)
    dm_sh = lax.dynamic_slice_in_dim(dm_all, me * mod_cols, mod_cols, axis=2)
    g_mod_w, g_cctx_part, g_mod_b = mod_bwd(c16, mod_w, dm_sh, dm_all, "mod_bwd")

    rep = [jnp.stack([g_n1_0[0], g_n1_1[0]]), jnp.stack([gf0["norm2"][0], gf1["norm2"][0]]), g_conv_b5, g_dt_bias, g_a_log,
           g_ssd_d, g_ssd_norm, jnp.stack([gf0["conv_b"][0], gf1["conv_b"][0]]), g_final, g_cctx_part, loss_part[:, :1]]
    rep_sizes = [r_.size for r_ in rep]
    rep_flat = _pad_to(jnp.concatenate([r_.reshape(-1) for r_ in rep]), 8 * PACK_W).reshape(-1, PACK_W)
    _, rep_sum = allgather_small(rep_flat, "reduce_replicated", with_sum=True)
    rep_sum = rep_sum.reshape(-1)
    roffs = [0]
    for s_ in rep_sizes:
        roffs.append(roffs[-1] + s_)
    rp = [rep_sum[roffs[i]:roffs[i + 1]] for i in range(len(rep_sizes))]
    loss = rp[10].reshape(())

    r_in, r_small = chips_wait(send_sems, recv_sems, late_p, late_land, rep_sum, "reduce_chips_late_wait")
    w_in_res = sum_adamw(r_in, tr(ssd_w_in[0]), tr(m_ssd_w_in[0]), tr(v_ssd_w_in[0]), None, "adamw_ssd_w_in")
    big["ssd_w_in"] = tuple(tr(t) for t in w_in_res)
    g_flat = sum_rows(r_small, "reduce_sum_small").reshape(-1)
    goffs = [0]
    for s_ in gsizes:
        goffs.append(goffs[-1] + s_)
    gs = [g_flat[goffs[i]:goffs[i + 1]] for i in range(len(gsizes))]
    grads = {
        "c_ctx": rp[9], "mod_w": g_mod_w, "mod_b": g_mod_b, "norm1_w": rp[0], "norm2_w": rp[1],
        "ssd_conv_w": gs[0], "ssd_conv_b": rp[2], "ssd_dt_bias": rp[3], "ssd_a_log": rp[4], "ssd_d": rp[5],
        "ssd_norm_w": rp[6], "conf_b_pw1": gs[1], "conf_w_dw": gs[2],
        "conf_b_dw": gs[3], "conf_ln_w": gs[4], "conf_ln_b": gs[5], "conf_b_pw2": gs[6],
        "ffn_conv_w": gs[7], "ffn_conv_b": rp[7], "final_norm_w": rp[8],
    }
    weights = dict(c_ctx=c_ctx, mod_w=mod_w, mod_b=mod_b, norm1_w=norm1_w, norm2_w=norm2_w, ssd_w_in=ssd_w_in, ssd_conv_w=ssd_conv_w, ssd_conv_b=ssd_conv_b, ssd_dt_bias=ssd_dt_bias, ssd_a_log=ssd_a_log, ssd_d=ssd_d, ssd_norm_w=ssd_norm_w, ssd_w_out=ssd_w_out, conf_w_pw1=conf_w_pw1, conf_b_pw1=conf_b_pw1, conf_w_dw=conf_w_dw, conf_b_dw=conf_b_dw, conf_ln_w=conf_ln_w, conf_ln_b=conf_ln_b, conf_w_pw2=conf_w_pw2, conf_b_pw2=conf_b_pw2, ffn_w_up=ffn_w_up, ffn_conv_w=ffn_conv_w, ffn_conv_b=ffn_conv_b, ffn_w_down=ffn_w_down, final_norm_w=final_norm_w)
    m_in = dict(c_ctx=m_c_ctx, mod_w=m_mod_w, mod_b=m_mod_b, norm1_w=m_norm1_w, norm2_w=m_norm2_w, ssd_w_in=m_ssd_w_in, ssd_conv_w=m_ssd_conv_w, ssd_conv_b=m_ssd_conv_b, ssd_dt_bias=m_ssd_dt_bias, ssd_a_log=m_ssd_a_log, ssd_d=m_ssd_d, ssd_norm_w=m_ssd_norm_w, ssd_w_out=m_ssd_w_out, conf_w_pw1=m_conf_w_pw1, conf_b_pw1=m_conf_b_pw1, conf_w_dw=m_conf_w_dw, conf_b_dw=m_conf_b_dw, conf_ln_w=m_conf_ln_w, conf_ln_b=m_conf_ln_b, conf_w_pw2=m_conf_w_pw2, conf_b_pw2=m_conf_b_pw2, ffn_w_up=m_ffn_w_up, ffn_conv_w=m_ffn_conv_w, ffn_conv_b=m_ffn_conv_b, ffn_w_down=m_ffn_w_down, final_norm_w=m_final_norm_w)
    v_in = dict(c_ctx=v_c_ctx, mod_w=v_mod_w, mod_b=v_mod_b, norm1_w=v_norm1_w, norm2_w=v_norm2_w, ssd_w_in=v_ssd_w_in, ssd_conv_w=v_ssd_conv_w, ssd_conv_b=v_ssd_conv_b, ssd_dt_bias=v_ssd_dt_bias, ssd_a_log=v_ssd_a_log, ssd_d=v_ssd_d, ssd_norm_w=v_ssd_norm_w, ssd_w_out=v_ssd_w_out, conf_w_pw1=v_conf_w_pw1, conf_b_pw1=v_conf_b_pw1, conf_w_dw=v_conf_w_dw, conf_b_dw=v_conf_b_dw, conf_ln_w=v_conf_ln_w, conf_ln_b=v_conf_ln_b, conf_w_pw2=v_conf_w_pw2, conf_b_pw2=v_conf_b_pw2, ffn_w_up=v_ffn_w_up, ffn_conv_w=v_ffn_conv_w, ffn_conv_b=v_ffn_conv_b, ffn_w_down=v_ffn_w_down, final_norm_w=v_final_norm_w)

    out_g, out_d, out_m, out_v = [], [], [], []
    for name_, w_ in weights.items():
        shape = w_.shape
        if name_ in big:
            for lst, t in zip((out_g, out_d, out_m, out_v), big[name_]):
                lst.append(t.reshape(shape))
            continue
        cols2 = shape[-1] if len(shape) > 1 else shape[0]
        g2 = grads[name_].reshape(-1, cols2)
        d_, nm_, nv_ = adamw(w_.reshape(-1, cols2), g2, m_in[name_].reshape(-1, cols2), v_in[name_].reshape(-1, cols2),
                             f"adamw_{name_}")
        out_g.append(g2.reshape(shape))
        out_d.append(d_.reshape(shape))
        out_m.append(nm_.reshape(shape))
        out_v.append(nv_.reshape(shape))
    return (loss, grad_x, *out_g, *out_d, *out_m, *out_v)
```

```python
import functools

import jax
import jax.numpy as jnp
from jax import lax
from jax.experimental import pallas as pl
from jax.experimental.pallas import tpu as pltpu

f32 = jnp.float32
bf16 = jnp.bfloat16
HI = lax.Precision.HIGHEST
S = jax.ShapeDtypeStruct
MESH = pl.DeviceIdType.MESH
ANY = pl.BlockSpec(memory_space=pl.ANY)
VMEM = pl.BlockSpec(memory_space=pltpu.VMEM)

N_DEV = 8
D = 1024
DI = 2048
CONVD = 4096
FH = 2816
GRID_W = 64
Q = 128
HPG = 4
P = 64
N = 128
G = 8
GW = HPG * P
NH_SSD = G * HPG
EPS = 1e-6
ADAM_LR, ADAM_B1, ADAM_B2, ADAM_EPS, ADAM_WD, ADAM_STEP = 0.001, 0.9, 0.999, 1e-08, 0.01, 10
VMEM_LIMIT_BYTES = 56 * 1024 * 1024
PACK_W = 1024
TB = 256
LTB = 512


def _cparams(*sem):
    return pltpu.CompilerParams(dimension_semantics=sem, vmem_limit_bytes=VMEM_LIMIT_BYTES)


def _pick(n, prefs):
    for p in prefs:
        if n % p == 0:
            return p
    return n


def _sigmoid(x):
    return 1.0 / (1.0 + jnp.exp(-x))


def _softplus(x):
    return jnp.maximum(x, 0.0) + jnp.log(1.0 + jnp.exp(-jnp.abs(x)))


def matmul(a, b, mode, out_dtype, name):
    if mode == "nn":
        (M, K), (_, Nn) = a.shape, b.shape
        bm, bn, bk = _pick(M, (512, 384, 256, 128)), Nn, K
    elif mode == "tn":
        (K, M), (_, Nn) = a.shape, b.shape
        bm, bn, bk = M, Nn, _pick(K, (256, 128))
    else:
        (M, K), (Nn, _) = a.shape, b.shape
        bm, bn, bk = _pick(M, (512, 384, 256, 128)), Nn, K
    nk = K // bk
    dims = {"nn": (((1,), (0,)), ((), ())), "tn": (((0,), (0,)), ((), ())), "nt": (((1,), (1,)), ((), ()))}[mode]

    def body(a_ref, b_ref, o_ref, acc_ref):
        k = pl.program_id(2)

        @pl.when(k == 0)
        def _():
            acc_ref[...] = jnp.zeros_like(acc_ref)

        acc_ref[...] += lax.dot_general(a_ref[...].astype(bf16), b_ref[...].astype(bf16), dims,
                                        preferred_element_type=f32)

        @pl.when(k == nk - 1)
        def _():
            o_ref[...] = acc_ref[...].astype(out_dtype)

    if mode == "nn":
        a_spec = pl.BlockSpec((bm, bk), lambda i, j, k: (i, k))
        b_spec = pl.BlockSpec((bk, bn), lambda i, j, k: (k, j))
    elif mode == "tn":
        a_spec = pl.BlockSpec((bk, bm), lambda i, j, k: (k, i))
        b_spec = pl.BlockSpec((bk, bn), lambda i, j, k: (k, j))
    else:
        a_spec = pl.BlockSpec((bm, bk), lambda i, j, k: (i, k))
        b_spec = pl.BlockSpec((bn, bk), lambda i, j, k: (j, k))
    return pl.pallas_call(
        body, grid=(M // bm, Nn // bn, nk), in_specs=[a_spec, b_spec],
        out_specs=pl.BlockSpec((bm, bn), lambda i, j, k: (i, j)),
        out_shape=S((M, Nn), out_dtype), scratch_shapes=[pltpu.VMEM((bm, bn), f32)],
        compiler_params=_cparams("parallel", "parallel", "arbitrary"), name=name,
    )(a, b)


SMM_DW_ROWS = (512, 256)
SMM_ROWS = (256,)


def _shard_pieces(seg_widths, n):
    bounds = [0]
    for sw in seg_widths:
        bounds.append(bounds[-1] + sw)
    assert bounds[-1] == N_DEV * n, (seg_widths, n)
    out = []
    for j in range(N_DEV):
        lo, hi = j * n, (j + 1) * n
        pcs = []
        for si in range(len(seg_widths)):
            a, b = max(lo, bounds[si]), min(hi, bounds[si + 1])
            if a < b:
                pcs.append((si, a - bounds[si], a - lo, b - a))
        out.append(pcs)
    return out


def _w_spec(w, layer):
    if layer is None:
        return pl.BlockSpec(w.shape, lambda *idx: (0, 0, 0))
    return pl.BlockSpec((N_DEV, None) + w.shape[2:], lambda *idx: (0, layer, 0, 0))


def smm_fwd(a, w, layer, seg_widths, name, hosted=None):
    M, K = a.shape
    n = w.shape[-1]
    pieces = _shard_pieces(seg_widths, n)
    padded = [sw + (-sw) % 128 for sw in seg_widths]
    bm = _pick(M, SMM_ROWS)

    def body(a_ref, w_ref, *o_refs):
        av = a_ref[...]
        for si, sw in enumerate(seg_widths):
            if padded[si] != sw:
                o_refs[si][:, pl.ds(padded[si] - 128, 128)] = jnp.zeros((bm, 128), f32)
        for j in range(N_DEV):
            for si, soff, woff, wd in pieces[j]:
                o_refs[si][:, pl.ds(soff, wd)] = jnp.dot(av, w_ref[j, :, pl.ds(woff, wd)], preferred_element_type=f32)

    outs, extra = _host_call(
        body, (M // bm,), [pl.BlockSpec((bm, K), lambda i: (i, 0)), _w_spec(w, layer)],
        [pl.BlockSpec((bm, pw), lambda i: (i, 0)) for pw in padded], [S((M, pw), f32) for pw in padded], [],
        ("parallel",), name, (a, w), hosted)
    return outs if hosted is None else (outs, extra)


def smm_dx(d_segs, w, layer, seg_widths, out_dtype, name, hosted=None):
    M = d_segs[0].shape[0]
    K, n = w.shape[-2], w.shape[-1]
    pieces = _shard_pieces(seg_widths, n)
    ns = len(d_segs)
    bm = _pick(M, SMM_ROWS)

    def body(*refs):
        d_refs, w_ref, o_ref = refs[:ns], refs[ns], refs[ns + 1]
        acc = jnp.zeros((bm, K), f32)
        for j in range(N_DEV):
            for si, soff, woff, wd in pieces[j]:
                acc = acc + lax.dot_general(d_refs[si][:, pl.ds(soff, wd)], w_ref[j, :, pl.ds(woff, wd)],
                                            (((1,), (1,)), ((), ())), preferred_element_type=f32)
        o_ref[...] = acc.astype(out_dtype)

    (out,), extra = _host_call(
        body, (M // bm,),
        [pl.BlockSpec((bm, d.shape[1]), lambda i: (i, 0)) for d in d_segs] + [_w_spec(w, layer)],
        [pl.BlockSpec((bm, K), lambda i: (i, 0))], [S((M, K), out_dtype)], [], ("parallel",), name,
        (*d_segs, w), hosted)
    return out, extra


def smm_dw(a, d_segs, n, seg_widths, ngrp, transposed, name):
    M, K = a.shape
    pieces = _shard_pieces(seg_widths, n)
    per = N_DEV // ngrp
    bm = _pick(M, SMM_DW_ROWS)
    nI = M // bm
    ns = len(d_segs)
    shard = (n, K) if transposed else (K, n)

    def body(*refs):
        a_ref, d_refs, o_ref, acc_ref = refs[0], refs[1:1 + ns], refs[1 + ns], refs[2 + ns]
        grp = pl.program_id(0)
        i = pl.program_id(1)

        @pl.when(i == 0)
        def _():
            acc_ref[...] = jnp.zeros_like(acc_ref)

        av = a_ref[...]
        for gs in range(ngrp):
            def one_group(gs=gs):
                for jj in range(per):
                    for si, soff, woff, wd in pieces[gs * per + jj]:
                        dv = d_refs[si][:, pl.ds(soff, wd)]
                        if transposed:
                            acc_ref[jj, pl.ds(woff, wd), :] += lax.dot_general(
                                dv, av, (((0,), (0,)), ((), ())), preferred_element_type=f32)
                        else:
                            acc_ref[jj, :, pl.ds(woff, wd)] += lax.dot_general(
                                av, dv, (((0,), (0,)), ((), ())), preferred_element_type=f32)
            pl.when(grp == gs)(one_group)

        @pl.when(i == nI - 1)
        def _():
            o_ref[...] = acc_ref[...].astype(bf16)

    return pl.pallas_call(
        body, grid=(ngrp, nI),
        in_specs=[pl.BlockSpec((bm, K), lambda g, i: (i, 0))]
        + [pl.BlockSpec((bm, d.shape[1]), lambda g, i: (i, 0)) for d in d_segs],
        out_specs=pl.BlockSpec((per,) + shard, lambda g, i: (g, 0, 0)), out_shape=S((N_DEV,) + shard, bf16),
        scratch_shapes=[pltpu.VMEM((per,) + shard, f32)],
        compiler_params=_cparams("arbitrary", "arbitrary"), name=name)(a, *d_segs)


def _modnorm_f(h, w, sc, sh):
    y = h * lax.rsqrt(jnp.mean(h * h, axis=-1, keepdims=True) + EPS)
    return (y * w) * (1.0 + sc) + sh


def _kind_specs(nctxb):
    if nctxb > 0:
        return pl.BlockSpec((None, 1, D), lambda i: (jnp.where(i < nctxb, 0, 1), 0, 0))
    return pl.BlockSpec((None, 1, D), lambda i: (0, 0, 0))


def _two_part_specs(nctxb):
    return (pl.BlockSpec((TB, D), lambda i: (jnp.minimum(i, nctxb - 1), 0)),
            pl.BlockSpec((TB, D), lambda i: (jnp.maximum(i - nctxb, 0), 0)))


def modnorm_fwd(h, w, sc, sh, nctxb, name, ctx=None):
    if ctx is None:
        T = h.shape[0]

        def body(h_ref, w_ref, sc_ref, sh_ref, o_ref):
            o_ref[...] = _modnorm_f(h_ref[...], w_ref[...], sc_ref[...], sh_ref[...]).astype(bf16)

        hspecs, hargs = [pl.BlockSpec((TB, D), lambda i: (i, 0))], (h,)
    else:
        T = h.shape[0] + ctx.shape[0]

        def body(c_ref, h_ref, w_ref, sc_ref, sh_ref, o_ref):
            hv = jnp.where(pl.program_id(0) < nctxb, c_ref[...], h_ref[...])
            o_ref[...] = _modnorm_f(hv, w_ref[...], sc_ref[...], sh_ref[...]).astype(bf16)

        hspecs, hargs = list(_two_part_specs(nctxb)), (ctx, h)
    row = pl.BlockSpec((1, D), lambda i: (0, 0))
    ks = _kind_specs(nctxb)
    return pl.pallas_call(body, grid=(T // TB,), in_specs=hspecs + [row, ks, ks],
                          out_specs=pl.BlockSpec((TB, D), lambda i: (i, 0)), out_shape=S((T, D), bf16),
                          compiler_params=_cparams("parallel"), name=name)(*hargs, w, sc, sh)


def modnorm_bwd(h, w, sc, sh, da, dres, nctxb, name, ctx=None):
    T = h.shape[0] + (0 if ctx is None else ctx.shape[0])
    kinds = sc.shape[0]
    nh = 1 if ctx is None else 2

    def body(*refs):
        w_ref, sc_ref, sh_ref, da_ref, dres_ref, dh_ref, dw_ref, dsc_ref, dsh_ref = refs[nh:]
        i = pl.program_id(0)
        hv = refs[0][...] if ctx is None else jnp.where(i < nctxb, refs[0][...], refs[1][...])
        _, vjp = jax.vjp(_modnorm_f, hv, w_ref[...], sc_ref[...], sh_ref[...])
        dh, dw, dsc, dsh = vjp(da_ref[...].astype(f32))
        dh_ref[...] = dres_ref[...] + dh

        @pl.when(i == 0)
        def _():
            dw_ref[...] = jnp.zeros_like(dw_ref)

        @pl.when((i == 0) | (i == nctxb))
        def _():
            dsc_ref[...] = jnp.zeros_like(dsc_ref)
            dsh_ref[...] = jnp.zeros_like(dsh_ref)

        dw_ref[...] += dw
        dsc_ref[...] += dsc
        dsh_ref[...] += dsh

    blk = pl.BlockSpec((TB, D), lambda i: (i, 0))
    lat = pl.BlockSpec((TB, D), lambda i: (jnp.maximum(i - nctxb, 0), 0))
    row = pl.BlockSpec((1, D), lambda i: (0, 0))
    ks = _kind_specs(nctxb)
    hspecs, hargs = ([blk], (h,)) if ctx is None else (list(_two_part_specs(nctxb)), (ctx, h))
    return pl.pallas_call(
        body, grid=(T // TB,), in_specs=hspecs + [row, ks, ks, blk, lat], out_specs=[lat, row, ks, ks],
        out_shape=[S((T - nctxb * TB, D), f32), S((1, D), f32), S((kinds, 1, D), f32), S((kinds, 1, D), f32)],
        compiler_params=_cparams("arbitrary"), name=name)(*hargs, w, sc, sh, da, dres)


def resnorm_fwd(h, o, g, b, w, sc, sh, name):
    T = h.shape[0]

    def body(h_ref, o_ref, g_ref, b_ref, w_ref, sc_ref, sh_ref, hn_ref, a_ref):
        hn = h_ref[...] + g_ref[...] * (o_ref[...] + b_ref[...])
        hn_ref[...] = hn
        a_ref[...] = _modnorm_f(hn, w_ref[...], sc_ref[...], sh_ref[...]).astype(bf16)

    blk = pl.BlockSpec((LTB, D), lambda i: (i, 0))
    row = pl.BlockSpec((1, D), lambda i: (0, 0))
    return pl.pallas_call(body, grid=(T // LTB,), in_specs=[blk, blk, row, row, row, row, row], out_specs=[blk, blk],
                          out_shape=[S((T, D), f32), S((T, D), bf16)], compiler_params=_cparams("parallel"),
                          name=name)(h, o, g, b, w, sc, sh)


def normres_bwd(h, w, sc, sh, da, dres, o, g, b, name):
    T = h.shape[0]

    def body(h_ref, w_ref, sc_ref, sh_ref, da_ref, dres_ref, o_ref, g_ref, b_ref,
             dh_ref, dw_ref, dsc_ref, dsh_ref, do_ref, dg_ref, db_ref):
        _, vjp = jax.vjp(_modnorm_f, h_ref[...], w_ref[...], sc_ref[...], sh_ref[...])
        dhn, dw, dsc, dsh = vjp(da_ref[...].astype(f32))
        dh = dres_ref[...] + dhn
        dh_ref[...] = dh
        do = g_ref[...] * dh
        do_ref[...] = do.astype(bf16)
        sums = (dw, dsc, dsh, jnp.sum(dh * (o_ref[...] + b_ref[...]), axis=0, keepdims=True),
                jnp.sum(do, axis=0, keepdims=True))

        @pl.when(pl.program_id(0) == 0)
        def _():
            for r_ in (dw_ref, dsc_ref, dsh_ref, dg_ref, db_ref):
                r_[...] = jnp.zeros_like(r_)

        for r_, s_ in zip((dw_ref, dsc_ref, dsh_ref, dg_ref, db_ref), sums):
            r_[...] += s_

    blk = pl.BlockSpec((LTB, D), lambda i: (i, 0))
    row = pl.BlockSpec((1, D), lambda i: (0, 0))
    return pl.pallas_call(
        body, grid=(T // LTB,), in_specs=[blk, row, row, row, blk, blk, blk, row, row],
        out_specs=[blk, row, row, row, blk, row, row],
        out_shape=[S((T, D), f32), S((1, D), f32), S((1, D), f32), S((1, D), f32), S((T, D), bf16), S((1, D), f32),
                   S((1, D), f32)],
        compiler_params=_cparams("arbitrary"), name=name)(h, w, sc, sh, da, dres, o, g, b)


def final_loss(h, o, g, w, tgt, name):
    T = h.shape[0]

    def f(hv, wv, tv):
        y = (hv * lax.rsqrt(jnp.mean(hv * hv, axis=-1, keepdims=True) + EPS)) * wv
        e = y - tv
        return 0.5 * jnp.sum(jnp.sum(e * e, axis=-1, keepdims=True), axis=0, keepdims=True) * (1.0 / D)

    def body(h_ref, o_ref, g_ref, w_ref, t_ref, loss_ref, dh_ref, dw_ref, do_ref, dg_ref):
        i = pl.program_id(0)
        tv = t_ref[...]
        ov = o_ref[...]
        gv = g_ref[...]
        val, vjp = jax.vjp(lambda a, b_: f(a, b_, tv), h_ref[...] + gv * ov, w_ref[...])
        dh, dw = vjp(jnp.ones((1, 1), f32))
        dh_ref[...] = dh
        do_ref[...] = (gv * dh).astype(bf16)

        @pl.when(i == 0)
        def _():
            loss_ref[...] = jnp.zeros_like(loss_ref)
            dw_ref[...] = jnp.zeros_like(dw_ref)
            dg_ref[...] = jnp.zeros_like(dg_ref)

        loss_ref[...] += jnp.broadcast_to(val, (1, 128))
        dw_ref[...] += dw
        dg_ref[...] += jnp.sum(dh * ov, axis=0, keepdims=True)

    blk = pl.BlockSpec((LTB, D), lambda i: (i, 0))
    row = pl.BlockSpec((1, D), lambda i: (0, 0))
    return pl.pallas_call(body, grid=(T // LTB,), in_specs=[blk, blk, row, row, blk],
                          out_specs=[pl.BlockSpec((1, 128), lambda i: (0, 0)), blk, row, blk, row],
                          out_shape=[S((1, 128), f32), S((T, D), f32), S((1, D), f32), S((T, D), bf16), S((1, D), f32)],
                          compiler_params=_cparams("arbitrary"), name=name)(h, o, g, w, tgt)


CB = 256
RT = 32
RTB = 16


def _fold8(t):
    acc = t[0:8]
    for k in range(1, t.shape[0] // 8):
        acc = acc + t[8 * k:8 * (k + 1)]
    return acc


def _rows(start, off=0, rt=RT):
    return pl.ds(pl.multiple_of(start + off, 8), rt)


def _rowsb(start, off=0):
    return _rows(start, off, RTB)


def _zero_rows(ref, start, n):
    ref[pl.ds(start, n), :] = jnp.zeros((n, ref.shape[1]), f32)


K5, HALF5, PAD5 = 5, 2, 8


def _taps5(base_ref, r, rt, sign):
    n = rt + 2 * PAD5
    v = base_ref[pl.ds(pl.multiple_of(r, 8), n), :]
    taps = []
    for k in range(K5):
        o = sign * (k - HALF5)
        rolled = v if o == 0 else pltpu.roll(v, (-o) % n, axis=0)
        taps.append(rolled[PAD5:PAD5 + rt])
    return taps


def ssd_conv_fwd(u, w, b, segs, name, hosted=None):
    T = u.shape[0]
    maxlen = max(ln for _, ln in segs)

    def body(u_ref, w_ref, b_ref, o_ref, ds_ref, base_ref):
        wv = [w_ref[pl.ds(k, 1), :] for k in range(K5)]
        bv = b_ref[...]
        for s0, ln in segs:
            _zero_rows(base_ref, 0, PAD5)
            _zero_rows(base_ref, PAD5 + ln, PAD5)
            base_ref[pl.ds(PAD5, ln), :] = u_ref[pl.ds(s0, ln), :]

            def tile(i, carry):
                r = i * RT
                taps = _taps5(base_ref, r, RT, 1)
                acc = jnp.broadcast_to(bv, (RT, CB))
                for k in range(K5):
                    acc = acc + taps[k] * wv[k]
                sg = _sigmoid(acc)
                o_ref[_rows(r, s0), :] = acc * sg
                ds_ref[_rows(r, s0), :] = sg * (1.0 + acc * (1.0 - sg))
                return carry

            lax.fori_loop(0, ln // RT, tile, 0, unroll=2)

    cblk = pl.BlockSpec((T, CB), lambda j: (0, j))
    (out, dsilu), extra = _host_call(
        body, (CONVD // CB,),
        [cblk, pl.BlockSpec((K5, CB), lambda j: (0, j)), pl.BlockSpec((1, CB), lambda j: (0, j))],
        [cblk, cblk], [S((T, CONVD), f32), S((T, CONVD), f32)],
        [pltpu.VMEM((maxlen + 2 * PAD5, CB), f32)], ("parallel",), name, (u, w, b), hosted)
    return out, dsilu, extra


def ssd_conv_bwd(proj, w, dsilu, dy2, dyskip, dexp, segs, name):
    T = proj.shape[0]
    maxlen = max(ln for _, ln in segs)
    nskip = DI // CB

    def body(u_ref, w_ref, ds_ref, dya_ref, dyb_ref, dsk_ref, dexp_ref, du_ref, dw_ref, db_ref, base_ref, dbase_ref):
        wv = [w_ref[pl.ds(k, 1), :] for k in range(K5)]
        has_skip = (pl.program_id(0) < nskip).astype(f32) * dexp_ref[...]
        acc8 = tuple(jnp.zeros((8, CB), f32) for _ in range(K5 + 1))
        for s0, ln in segs:
            for ref in (base_ref, dbase_ref):
                _zero_rows(ref, 0, PAD5)
                _zero_rows(ref, PAD5 + ln, PAD5)
            base_ref[pl.ds(PAD5, ln), :] = u_ref[pl.ds(s0, ln), :]

            def tile1(i, carry):
                r = i * RTB
                dy = dya_ref[_rowsb(r, s0), :] + dyb_ref[_rowsb(r, s0), :] + has_skip * dsk_ref[_rowsb(r, s0), :]
                dpre = dy * ds_ref[_rowsb(r, s0), :]
                dbase_ref[_rowsb(r, PAD5), :] = dpre
                taps = _taps5(base_ref, r, RTB, 1)
                new = [carry[k] + _fold8(dpre * taps[k]) for k in range(K5)]
                new.append(carry[K5] + _fold8(dpre))
                return tuple(new)

            acc8 = lax.fori_loop(0, ln // RTB, tile1, acc8, unroll=2)

            def tile2(i, carry):
                r = i * RTB
                taps = _taps5(dbase_ref, r, RTB, -1)
                du = jnp.zeros((RTB, CB), f32)
                for k in range(K5):
                    du = du + taps[k] * wv[k]
                du_ref[_rowsb(r, s0), :] = du.astype(bf16)
                return carry

            lax.fori_loop(0, ln // RTB, tile2, 0, unroll=4)
        for k in range(K5):
            dw_ref[pl.ds(k, 1), :] = jnp.sum(acc8[k], axis=0, keepdims=True)
        db_ref[...] = jnp.sum(acc8[K5], axis=0, keepdims=True)

    cblk = pl.BlockSpec((T, CB), lambda j: (0, j))
    return pl.pallas_call(
        body, grid=(CONVD // CB,),
        in_specs=[cblk, pl.BlockSpec((K5, CB), lambda j: (0, j)), cblk,
                  pl.BlockSpec((None, T, CB), lambda j: (0, 0, j)), pl.BlockSpec((None, T, CB), lambda j: (1, 0, j)),
                  pl.BlockSpec((T, CB), lambda j: (0, jnp.minimum(j, nskip - 1))),
                  pl.BlockSpec((1, CB), lambda j: (0, jnp.minimum(j, nskip - 1)))],
        out_specs=[cblk, pl.BlockSpec((K5, CB), lambda j: (0, j)), pl.BlockSpec((1, CB), lambda j: (0, j))],
        out_shape=[S((T, CONVD), bf16), S((K5, CONVD), f32), S((1, CONVD), f32)],
        scratch_shapes=[pltpu.VMEM((maxlen + 2 * PAD5, CB), f32), pltpu.VMEM((maxlen + 2 * PAD5, CB), f32)],
        compiler_params=_cparams("parallel"), name=name)(proj, w, dsilu, dy2, dy2, dyskip, dexp)


GPAD = GRID_W


def _grid_copies(g_ref, src, L):
    col = lax.broadcasted_iota(jnp.int32, (L, CB), 0) & (GRID_W - 1)
    for d in range(3):
        _zero_rows(g_ref.at[d], 0, GPAD)
        _zero_rows(g_ref.at[d], GPAD + L, GPAD)
    g_ref[1, pl.ds(GPAD, L), :] = src
    g_ref[0, pl.ds(GPAD, L), :] = jnp.where(col != 0, g_ref[1, pl.ds(GPAD - 1, L), :], 0.0)
    g_ref[2, pl.ds(GPAD, L), :] = jnp.where(col != GRID_W - 1, g_ref[1, pl.ds(GPAD + 1, L), :], 0.0)


def ffn_gate_fwd(val, gate, cw, cb_, name, hosted=None):
    L = val.shape[0]
    nb = FH // CB

    def body(val_ref, gate_ref, w_ref, b_ref, o_ref, s_ref, vds_ref, g_ref):
        wv = [w_ref[pl.ds(t, 1), :] for t in range(9)]
        bv = b_ref[...]
        _grid_copies(g_ref, gate_ref[...], L)

        def tile(i, carry):
            r = i * RT
            acc = jnp.broadcast_to(bv, (RT, CB))
            for dr in range(3):
                for dc in range(3):
                    acc = acc + g_ref[dc, _rows(r, GPAD + (dr - 1) * GRID_W), :] * wv[3 * dr + dc]
            sg = _sigmoid(acc)
            s = acc * sg
            v = val_ref[_rows(r), :]
            o_ref[_rows(r), :] = (s * v).astype(bf16)
            s_ref[_rows(r), :] = s
            vds_ref[_rows(r), :] = v * (sg * (1.0 + acc * (1.0 - sg)))
            return carry

        lax.fori_loop(0, L // RT, tile, 0, unroll=2)

    cblk = pl.BlockSpec((L, CB), lambda j: (0, j))
    (out, s_, vds), extra = _host_call(
        body, (nb,), [cblk, cblk, pl.BlockSpec((9, CB), lambda j: (0, j)), pl.BlockSpec((1, CB), lambda j: (0, j))],
        [cblk, cblk, cblk], [S((L, FH), bf16), S((L, FH), f32), S((L, FH), f32)],
        [pltpu.VMEM((3, L + 2 * GPAD, CB), f32)], ("parallel",), name, (val, gate, cw, cb_), hosted)
    return out, s_, vds, extra


def ffn_gate_bwd(gate, s_, vds, cw, dact, name):
    L = gate.shape[0]
    nb = FH // CB

    def body(gate_ref, s_ref, vds_ref, w_ref, da_ref, dval_ref, dgate_ref, dw_ref, db_ref, g_ref, d_ref):
        wv = [w_ref[pl.ds(t, 1), :] for t in range(9)]
        _grid_copies(g_ref, gate_ref[...], L)

        def tile1(i, carry):
            r = i * RTB
            da = da_ref[_rowsb(r), :].astype(f32)
            dval_ref[_rowsb(r), :] = (da * s_ref[_rowsb(r), :]).astype(bf16)
            dpre = da * vds_ref[_rowsb(r), :]
            d_ref[_rowsb(r), :] = dpre
            new = [carry[t] + _fold8(dpre * g_ref[t % 3, _rowsb(r, GPAD + (t // 3 - 1) * GRID_W), :]) for t in range(9)]
            new.append(carry[9] + _fold8(dpre))
            return tuple(new)

        acc8 = lax.fori_loop(0, L // RTB, tile1, tuple(jnp.zeros((8, CB), f32) for _ in range(10)), unroll=2)
        for t in range(9):
            dw_ref[pl.ds(t, 1), :] = jnp.sum(acc8[t], axis=0, keepdims=True)
        db_ref[...] = jnp.sum(acc8[9], axis=0, keepdims=True)
        _grid_copies(g_ref, d_ref[...], L)

        def tile2(i, carry):
            r = i * RTB
            dg = jnp.zeros((RTB, CB), f32)
            for dr in range(3):
                for dc in range(3):
                    dg = dg + g_ref[2 - dc, _rowsb(r, GPAD - (dr - 1) * GRID_W), :] * wv[3 * dr + dc]
            dgate_ref[_rowsb(r), :] = dg.astype(bf16)
            return carry

        lax.fori_loop(0, L // RTB, tile2, 0, unroll=4)

    cblk = pl.BlockSpec((L, CB), lambda j: (0, j))
    return pl.pallas_call(
        body, grid=(nb,),
        in_specs=[cblk, cblk, cblk, pl.BlockSpec((9, CB), lambda j: (0, j)), cblk],
        out_specs=[cblk, cblk, pl.BlockSpec((9, CB), lambda j: (0, j)), pl.BlockSpec((1, CB), lambda j: (0, j))],
        out_shape=[S((L, FH), bf16), S((L, FH), bf16), S((9, FH), f32), S((1, FH), f32)],
        scratch_shapes=[pltpu.VMEM((3, L + 2 * GPAD, CB), f32), pltpu.VMEM((L, CB), f32)],
        compiler_params=_cparams("parallel"), name=name)(gate, s_, vds, cw, dact)


CONF_K = 31
CHALF = CONF_K // 2
CPAD = 16


def _shift_copies8(c_ref, base_ref, L):
    n = L + 2 * CPAD - 8
    for b_ in range(8):
        c_ref[b_, pl.ds(0, n), :] = base_ref[pl.ds(b_, n), :]


def _tap_ab(o):
    return o % 8, o - o % 8


def conf_glu_conv_fwd(pa, pg, b1, wdw, bdw, name, hosted=None):
    L = pa.shape[0]
    nb = D // CB

    def body(pa_ref, pg_ref, ba_ref, bg_ref, w_ref, bdw_ref, o_ref, base_ref, c_ref):
        _zero_rows(base_ref, 0, CPAD)
        _zero_rows(base_ref, CPAD + L, CPAD)
        base_ref[pl.ds(CPAD, L), :] = (pa_ref[...] + ba_ref[...]) * _sigmoid(pg_ref[...] + bg_ref[...])
        _shift_copies8(c_ref, base_ref, L)
        bv = bdw_ref[...]

        def tile(i, carry):
            r = i * RT
            acc = jnp.broadcast_to(bv, (RT, CB))
            for k in range(CONF_K):
                b_, a8 = _tap_ab(k - CHALF)
                acc = acc + c_ref[b_, _rows(r, CPAD + a8), :] * w_ref[pl.ds(k, 1), :]
            o_ref[_rows(r), :] = acc
            return carry

        lax.fori_loop(0, L // RT, tile, 0, unroll=2)

    cblk = pl.BlockSpec((L, CB), lambda j: (0, j))
    rblk = pl.BlockSpec((1, CB), lambda j: (0, j))
    rgblk = pl.BlockSpec((1, CB), lambda j: (0, nb + j))
    (out,), extra = _host_call(
        body, (nb,), [cblk, cblk, rblk, rgblk, pl.BlockSpec((CONF_K, CB), lambda j: (0, j)), rblk],
        [cblk], [S((L, D), f32)], [pltpu.VMEM((L + 2 * CPAD, CB), f32), pltpu.VMEM((8, L + 2 * CPAD, CB), f32)],
        ("parallel",), name, (pa, pg, b1, b1, wdw, bdw), hosted)
    return out, extra


def conf_glu_conv_bwd(pa, pg, b1, wdw, dy, name):
    L = pa.shape[0]
    nb = D // CB

    def body(pa_ref, pg_ref, ba_ref, bg_ref, w_ref, dy_ref, dpa_ref, dpg_ref, dba_ref, dbg_ref, dw_ref, dbdw_ref,
             base_ref, c_ref, acc_ref):
        _zero_rows(base_ref, 0, CPAD)
        _zero_rows(base_ref, CPAD + L, CPAD)
        base_ref[pl.ds(CPAD, L), :] = (pa_ref[...] + ba_ref[...]) * _sigmoid(pg_ref[...] + bg_ref[...])
        _shift_copies8(c_ref, base_ref, L)
        acc_ref[...] = jnp.zeros_like(acc_ref)

        def tile1(i, carry):
            r = i * RTB
            dyt = dy_ref[_rowsb(r), :]
            for k in range(CONF_K):
                b_, a8 = _tap_ab(k - CHALF)
                acc_ref[k] += _fold8(dyt * c_ref[b_, _rowsb(r, CPAD + a8), :])
            return carry + _fold8(dyt)

        db8 = lax.fori_loop(0, L // RTB, tile1, jnp.zeros((8, CB), f32), unroll=2)
        dbdw_ref[...] = jnp.sum(db8, axis=0, keepdims=True)
        for k in range(CONF_K):
            dw_ref[pl.ds(k, 1), :] = jnp.sum(acc_ref[k], axis=0, keepdims=True)
        base_ref[pl.ds(CPAD, L), :] = dy_ref[...]
        _shift_copies8(c_ref, base_ref, L)
        ba = ba_ref[...]
        bg = bg_ref[...]

        def tile2(i, carry):
            r = i * RTB
            dglu = jnp.zeros((RTB, CB), f32)
            for k in range(CONF_K):
                b_, a8 = _tap_ab(CHALF - k)
                dglu = dglu + c_ref[b_, _rowsb(r, CPAD + a8), :] * w_ref[pl.ds(k, 1), :]
            a = pa_ref[_rowsb(r), :] + ba
            sg = _sigmoid(pg_ref[_rowsb(r), :] + bg)
            dpa = dglu * sg
            dpg = dglu * a * (sg * (1.0 - sg))
            dpa_ref[_rowsb(r), :] = dpa.astype(bf16)
            dpg_ref[_rowsb(r), :] = dpg.astype(bf16)
            return carry[0] + _fold8(dpa), carry[1] + _fold8(dpg)

        s8 = lax.fori_loop(0, L // RTB, tile2, (jnp.zeros((8, CB), f32), jnp.zeros((8, CB), f32)), unroll=2)
        dba_ref[...] = jnp.sum(s8[0], axis=0, keepdims=True)
        dbg_ref[...] = jnp.sum(s8[1], axis=0, keepdims=True)

    cblk = pl.BlockSpec((L, CB), lambda j: (0, j))
    rblk = pl.BlockSpec((1, CB), lambda j: (0, j))
    rgblk = pl.BlockSpec((1, CB), lambda j: (0, nb + j))
    wblk = pl.BlockSpec((CONF_K, CB), lambda j: (0, j))
    return pl.pallas_call(
        body, grid=(nb,), in_specs=[cblk, cblk, rblk, rgblk, wblk, cblk],
        out_specs=[cblk, cblk, rblk, rblk, wblk, rblk],
        out_shape=[S((L, D), bf16), S((L, D), bf16), S((1, D), f32), S((1, D), f32), S((CONF_K, D), f32), S((1, D), f32)],
        scratch_shapes=[pltpu.VMEM((L + 2 * CPAD, CB), f32), pltpu.VMEM((8, L + 2 * CPAD, CB), f32),
                        pltpu.VMEM((CONF_K, 8, CB), f32)],
        compiler_params=_cparams("parallel"), name=name)(pa, pg, b1, b1, wdw, dy)


def _ln_silu_f(x, w, b):
    mu = jnp.mean(x, axis=-1, keepdims=True)
    d = x - mu
    y = d * lax.rsqrt(jnp.mean(d * d, axis=-1, keepdims=True) + EPS) * w + b
    return y * _sigmoid(y)


def ln_silu_fwd(x, w, b, name):
    T = x.shape[0]

    def body(x_ref, w_ref, b_ref, o_ref):
        o_ref[...] = _ln_silu_f(x_ref[...], w_ref[...], b_ref[...]).astype(bf16)

    blk = pl.BlockSpec((TB, D), lambda i: (i, 0))
    row = pl.BlockSpec((1, D), lambda i: (0, 0))
    return pl.pallas_call(body, grid=(T // TB,), in_specs=[blk, row, row], out_specs=blk, out_shape=S((T, D), bf16),
                          compiler_params=_cparams("parallel"), name=name)(x, w, b)


def ln_silu_bwd(x, w, b, ds, name):
    T = x.shape[0]

    def body(x_ref, w_ref, b_ref, ds_ref, dx_ref, dw_ref, db_ref):
        i = pl.program_id(0)
        _, vjp = jax.vjp(_ln_silu_f, x_ref[...], w_ref[...], b_ref[...])
        dx, dw, db = vjp(ds_ref[...].astype(f32))
        dx_ref[...] = dx

        @pl.when(i == 0)
        def _():
            dw_ref[...] = jnp.zeros_like(dw_ref)
            db_ref[...] = jnp.zeros_like(db_ref)

        dw_ref[...] += dw
        db_ref[...] += db

    blk = pl.BlockSpec((TB, D), lambda i: (i, 0))
    row = pl.BlockSpec((1, D), lambda i: (0, 0))
    return pl.pallas_call(body, grid=(T // TB,), in_specs=[blk, row, row, blk], out_specs=[blk, row, row],
                          out_shape=[S((T, D), f32), S((1, D), f32), S((1, D), f32)],
                          compiler_params=_cparams("arbitrary"), name=name)(x, w, b, ds)


def _mxu(a, b, dims):
    return lax.dot_general(a.astype(bf16), b.astype(bf16), (dims, ((), ())), preferred_element_type=f32)


def _nn(a, b):
    return _mxu(a, b, ((1,), (0,)))


def _nt(a, b):
    return _mxu(a, b, ((1,), (1,)))


def _tn(a, b):
    return _mxu(a, b, ((0,), (0,)))


@jax.custom_vjp
def _dot_nn(a, b):
    return _nn(a, b)


@jax.custom_vjp
def _dot_nt(a, b):
    return _nt(a, b)


@jax.custom_vjp
def _dot_tn(a, b):
    return _tn(a, b)


_dot_nn.defvjp(lambda a, b: (_nn(a, b), (a, b)), lambda res, g: (_nt(g, res[1]), _tn(res[0], g)))
_dot_nt.defvjp(lambda a, b: (_nt(a, b), (a, b)), lambda res, g: (_nn(g, res[1]), _tn(g, res[0])))
_dot_tn.defvjp(lambda a, b: (_tn(a, b), (a, b)), lambda res, g: (_nt(res[1], g), _nn(res[0], g)))


def _exact_dot(a, b, dims, split_first):
    v = a if split_first else b
    p1 = v.astype(bf16)
    r1 = v - p1.astype(f32)
    p2 = r1.astype(bf16)
    p3 = (r1 - p2.astype(f32)).astype(bf16)
    out = None
    for p in (p1, p2, p3):
        lhs, rhs = (p, b.astype(bf16)) if split_first else (a.astype(bf16), p)
        t = lax.dot_general(lhs, rhs, (dims, ((), ())), preferred_element_type=f32)
        out = t if out is None else out + t
    return out


@jax.custom_vjp
def _masked_sum_cols(mf, a):
    return _exact_dot(mf, a, ((1,), (0,)), False)


@jax.custom_vjp
def _masked_sum_rows(mf, a):
    return _exact_dot(a, mf, ((1,), (1,)), True)


_masked_sum_cols.defvjp(lambda mf, a: (_exact_dot(mf, a, ((1,), (0,)), False), mf),
                        lambda mf, g: (jnp.zeros_like(mf), _exact_dot(mf, g, ((0,), (0,)), False)))
_masked_sum_rows.defvjp(lambda mf, a: (_exact_dot(a, mf, ((1,), (1,)), True), mf),
                        lambda mf, g: (jnp.zeros_like(mf), _exact_dot(g, mf, ((1,), (0,)), True)))


def _masked_sum(mf, a, rows):
    return _masked_sum_rows(mf, a) if rows else _masked_sum_cols(mf, a)


def _lanes_to_rows(v):
    r = lax.broadcasted_iota(jnp.int32, (GW, GW), 0)
    c = lax.broadcasted_iota(jnp.int32, (GW, GW), 1)
    return jnp.sum(jnp.where(r == c, jnp.broadcast_to(v, (GW, GW)), 0.0), axis=1, keepdims=True)


def _ssd_chunk(x, B, C, dtc, dtr, bc, br, alc, alr, s_in, is_fwd):
    row = lax.broadcasted_iota(jnp.int32, (Q, Q), 0)
    col = lax.broadcasted_iota(jnp.int32, (Q, Q), 1)
    sgn = jnp.where(is_fwd, 1, -1).astype(jnp.int32)
    mask = (row - col) * sgn >= 0
    mf = mask.astype(f32)
    lane_head = lax.broadcasted_iota(jnp.int32, (1, GW), 1) // P

    def spread(v):
        out = jnp.zeros((v.shape[0], GW), f32)
        for r in range(HPG):
            out = jnp.where(lane_head == r, v[:, r:r + 1], out)
        return out

    dt_c = _softplus(dtc + bc)
    dt_r = _softplus(dtr + br)
    a_c = dt_c * (-jnp.exp(alc))
    a_r = dt_r * (-jnp.exp(alr))
    acum_c = _masked_sum(mf, a_c, False)
    acum_r = _masked_sum(mf, a_r, True)
    tot_c = jnp.sum(a_c, axis=0, keepdims=True)
    dt_e = spread(dt_c)
    acum_e = spread(acum_c)
    tot_e = spread(tot_c)
    xdt = x * dt_e
    cb = _dot_nt(C, B)
    scores, xs = [], []
    for r in range(HPG):
        seg = acum_c[:, r:r + 1] - acum_r[r:r + 1, :]
        scores.append(cb * jnp.exp(jnp.where(mask, seg, -jnp.inf)))
        xs.append(jnp.where(lane_head == r, xdt, 0.0))
    y = _dot_nn(jnp.concatenate(scores, axis=1), jnp.concatenate(xs, axis=0))
    y = y + _dot_nt(C, s_in) * jnp.exp(acum_e)
    xe = xdt * jnp.exp(tot_e - acum_e)
    s_out = _lanes_to_rows(jnp.exp(tot_e)) * s_in + _dot_tn(xe, B)
    return y, s_out


def _chunk_index(d, t, nctx, nc):
    bwd = jnp.where(t < nctx, nctx - 1 - t, nc - 1 - (t - nctx))
    return jnp.where(d == 0, t, bwd)


def _ssd_in_specs(ci):
    small_c = pl.BlockSpec((None, G, 1, HPG), lambda d, t: (d, 0, 0, 0))
    small_r = pl.BlockSpec((None, G, HPG, 1), lambda d, t: (d, 0, 0, 0))
    return [
        pl.BlockSpec((Q, CONVD), lambda d, t: (ci(d, t), 0)),
        pl.BlockSpec((None, G, Q, HPG), lambda d, t: (d, 0, ci(d, t), 0)),
        pl.BlockSpec((None, G, HPG, Q), lambda d, t: (d, 0, 0, ci(d, t))),
        small_c, small_r, small_c, small_r,
    ]


def _group_cols(g):
    return pl.ds(g * GW, GW), pl.ds(DI + g * N, N), pl.ds(DI + G * N + g * N, N)


def ssd_scan_fwd(xbc, dtc, dtr, bc, br, alc, alr, nctx, name, hosted=None):
    T = xbc.shape[0]
    nc = T // Q

    def body(xbc_ref, dtc_ref, dtr_ref, bc_ref, br_ref, alc_ref, alr_ref, y_ref, sin_ref, st_ref):
        d = pl.program_id(0)
        t = pl.program_id(1)

        @pl.when(t == 0)
        def _():
            st_ref[...] = jnp.zeros_like(st_ref)

        for g in range(G):
            xs, bs, cs = _group_cols(g)
            s_in = st_ref[g]
            sin_ref[g] = s_in
            y, s_out = _ssd_chunk(xbc_ref[:, xs], xbc_ref[:, bs], xbc_ref[:, cs], dtc_ref[g], dtr_ref[g], bc_ref[g], br_ref[g],
                                  alc_ref[g], alr_ref[g], s_in, d == 0)
            y_ref[:, xs] = y
            st_ref[g] = s_out

    ci = lambda d, t: _chunk_index(d, t, nctx, nc)
    out_specs = [
        pl.BlockSpec((None, Q, DI), lambda d, t: (d, ci(d, t), 0)),
        pl.BlockSpec((None, None, G, GW, N), lambda d, t: (d, ci(d, t), 0, 0, 0)),
    ]
    return _host_call(
        body, (2, nc), _ssd_in_specs(ci), out_specs, [S((2, T, DI), f32), S((2, nc, G, GW, N), f32)],
        [pltpu.VMEM((G, GW, N), f32)], ("arbitrary", "arbitrary"), name, (xbc, dtc, dtr, bc, br, alc, alr), hosted)


def ssd_scan_bwd(xbc, dtc, dtr, bc, br, alc, alr, s_in_all, dy, nctx, name, hosted=None):
    T = xbc.shape[0]
    nc = T // Q

    def body(xbc_ref, dtc_ref, dtr_ref, bc_ref, br_ref, alc_ref, alr_ref, sin_ref, dy_ref,
             dxbc_ref, ddtc_ref, ddtr_ref, dbc_ref, dbr_ref, dalc_ref, dalr_ref, ds_ref):
        d = pl.program_id(0)
        t = pl.program_id(1)

        @pl.when(t == 0)
        def _():
            ds_ref[...] = jnp.zeros_like(ds_ref)
            dbc_ref[...] = jnp.zeros_like(dbc_ref)
            dbr_ref[...] = jnp.zeros_like(dbr_ref)
            dalc_ref[...] = jnp.zeros_like(dalc_ref)
            dalr_ref[...] = jnp.zeros_like(dalr_ref)

        f = functools.partial(_ssd_chunk, is_fwd=(d == 0))
        for g in range(G):
            xs, bs, cs = _group_cols(g)
            _, vjp = jax.vjp(f, xbc_ref[:, xs], xbc_ref[:, bs], xbc_ref[:, cs], dtc_ref[g], dtr_ref[g], bc_ref[g], br_ref[g],
                             alc_ref[g], alr_ref[g], sin_ref[g])
            dx, dB, dC, ddtc, ddtr, dbc, dbr, dalc, dalr, ds = vjp((dy_ref[:, xs], ds_ref[g]))
            dxbc_ref[:, xs] = dx
            dxbc_ref[:, bs] = dB
            dxbc_ref[:, cs] = dC
            ddtc_ref[g] = ddtc
            ddtr_ref[g] = ddtr
            dbc_ref[g] += dbc
            dbr_ref[g] += dbr
            dalc_ref[g] += dalc
            dalr_ref[g] += dalr
            ds_ref[g] = ds

    ci = lambda d, t: _chunk_index(d, nc - 1 - t, nctx, nc)
    in_specs = _ssd_in_specs(ci) + [
        pl.BlockSpec((None, None, G, GW, N), lambda d, t: (d, ci(d, t), 0, 0, 0)),
        pl.BlockSpec((Q, DI), lambda d, t: (ci(d, t), 0)),
    ]
    small_c = pl.BlockSpec((None, G, 1, HPG), lambda d, t: (d, 0, 0, 0))
    small_r = pl.BlockSpec((None, G, HPG, 1), lambda d, t: (d, 0, 0, 0))
    out_specs = [
        pl.BlockSpec((None, Q, CONVD), lambda d, t: (d, ci(d, t), 0)),
        pl.BlockSpec((None, G, Q, HPG), lambda d, t: (d, 0, ci(d, t), 0)),
        pl.BlockSpec((None, G, HPG, Q), lambda d, t: (d, 0, 0, ci(d, t))),
        small_c, small_r, small_c, small_r,
    ]
    out_shape = [S((2, T, CONVD), f32), S((2, G, T, HPG), f32), S((2, G, HPG, T), f32),
                 S((2, G, 1, HPG), f32), S((2, G, HPG, 1), f32), S((2, G, 1, HPG), f32), S((2, G, HPG, 1), f32)]
    return _host_call(body, (2, nc), in_specs, out_specs, out_shape, [pltpu.VMEM((G, GW, N), f32)],
                      ("arbitrary", "arbitrary"), name, (xbc, dtc, dtr, bc, br, alc, alr, s_in_all, dy), hosted)


GTB = 128


def _gate_norm_f(yf, yb, x, z, dexp, w):
    y = (yf + yb + dexp * x) * (z * _sigmoid(z))
    return y * lax.rsqrt(jnp.mean(y * y, axis=-1, keepdims=True) + EPS) * w


def ssd_gate_fwd(y2, xbc, proj, dexp, w, nctxb, name):
    T = xbc.shape[0]
    L = T - nctxb * GTB

    def body(yf_ref, yb_ref, x_ref, z_ref, d_ref, w_ref, o_ref):
        o_ref[...] = _gate_norm_f(yf_ref[...], yb_ref[...], x_ref[...], z_ref[...], d_ref[...], w_ref[...]).astype(bf16)

    wide = pl.BlockSpec((GTB, DI), lambda i: (i + nctxb, 0))
    row = pl.BlockSpec((1, DI), lambda i: (0, 0))
    return pl.pallas_call(
        body, grid=(L // GTB,),
        in_specs=[pl.BlockSpec((None, GTB, DI), lambda i: (0, i + nctxb, 0)),
                  pl.BlockSpec((None, GTB, DI), lambda i: (1, i + nctxb, 0)), wide, wide, row, row],
        out_specs=pl.BlockSpec((GTB, DI), lambda i: (i, 0)), out_shape=S((L, DI), bf16),
        compiler_params=_cparams("parallel"), name=name)(y2, y2, xbc, proj, dexp, w)


def ssd_gate_bwd(y2, xbc, proj, dexp, w, dyn, nctxb, name, hosted=None):
    T = xbc.shape[0]
    nb = T // GTB

    def body(yf_ref, yb_ref, x_ref, z_ref, d_ref, w_ref, dyn_ref, dy_ref, dz_ref, dd_ref, dw_ref):
        i = pl.program_id(0)

        @pl.when(i == 0)
        def _():
            dd_ref[...] = jnp.zeros_like(dd_ref)
            dw_ref[...] = jnp.zeros_like(dw_ref)

        @pl.when(i < nctxb)
        def _():
            dy_ref[...] = jnp.zeros_like(dy_ref)
            dz_ref[...] = jnp.zeros_like(dz_ref)

        @pl.when(i >= nctxb)
        def _():
            _, vjp = jax.vjp(_gate_norm_f, yf_ref[...], yb_ref[...], x_ref[...], z_ref[...], d_ref[...], w_ref[...])
            dyf, _, _, dz, dd, dw = vjp(dyn_ref[...].astype(f32))
            dy_ref[...] = dyf
            dz_ref[...] = dz.astype(bf16)
            fold = (lax.broadcasted_iota(jnp.int32, (DI, 128), 0) // P == lax.broadcasted_iota(jnp.int32, (DI, 128), 1))
            dd_ref[...] += jnp.dot(dd, fold.astype(f32), precision=HI, preferred_element_type=f32)
            dw_ref[...] += dw

    wide = pl.BlockSpec((GTB, DI), lambda i: (i, 0))
    row = pl.BlockSpec((1, DI), lambda i: (0, 0))
    hrow = pl.BlockSpec((1, 128), lambda i: (0, 0))
    return _host_call(
        body, (nb,),
        [pl.BlockSpec((None, GTB, DI), lambda i: (0, i, 0)), pl.BlockSpec((None, GTB, DI), lambda i: (1, i, 0)),
         wide, wide, row, row, pl.BlockSpec((GTB, DI), lambda i: (jnp.maximum(i - nctxb, 0), 0))],
        [wide, wide, hrow, row],
        [S((T, DI), f32), S((T, DI), bf16), S((1, 128), f32), S((1, DI), f32)],
        [], ("arbitrary",), name, (y2, y2, xbc, proj, dexp, w, dyn), hosted)


CROWS = 2 * N_DEV


def mod_fwd(c16, modw, name):
    nl, _, cols = modw.shape

    def body(c_ref, w_ref, o_ref):
        cv = c_ref[...]
        s = cv * _sigmoid(cv)
        for l in range(nl):
            o_ref[l] = jnp.dot(s, w_ref[l], precision=HI, preferred_element_type=f32)

    return pl.pallas_call(body, in_specs=[VMEM, VMEM], out_specs=VMEM, out_shape=S((nl, CROWS, cols), f32),
                          compiler_params=pltpu.CompilerParams(vmem_limit_bytes=VMEM_LIMIT_BYTES), name=name)(c16, modw)


def mod_bwd(c16, modw, dm_sh, dm_all, name):
    nl, _, cols = modw.shape

    def body(c_ref, w_ref, dm_ref, dmall_ref, dw_ref, dc_ref, db_ref):
        cv = c_ref[...]
        sg = _sigmoid(cv)
        s = cv * sg
        ds_dc = sg * (1.0 + cv * (1.0 - sg))
        is_ctx = lax.broadcasted_iota(jnp.int32, (CROWS, D), 0) >= N_DEV
        dc = jnp.zeros((1, D), f32)
        for l in range(nl):
            dm = dm_ref[l]
            dw_ref[l] = lax.dot_general(s, dm, (((0,), (0,)), ((), ())), precision=HI, preferred_element_type=f32)
            dsv = lax.dot_general(dm, w_ref[l], (((1,), (1,)), ((), ())), precision=HI, preferred_element_type=f32)
            dc = dc + jnp.sum(jnp.where(is_ctx, dsv * ds_dc, 0.0), axis=0, keepdims=True)
            db_ref[pl.ds(l, 1), :] = jnp.sum(dmall_ref[l], axis=0, keepdims=True)
        dc_ref[...] = dc

    return pl.pallas_call(
        body, in_specs=[VMEM, VMEM, VMEM, VMEM], out_specs=[VMEM, VMEM, VMEM],
        out_shape=[S(modw.shape, f32), S((1, D), f32), S((nl, 6 * D), f32)],
        compiler_params=pltpu.CompilerParams(vmem_limit_bytes=VMEM_LIMIT_BYTES), name=name)(c16, modw, dm_sh, dm_all)


def adamw(w, g, m, v, name):
    R, C = w.shape
    rb = R if R <= 512 else max(r_ for r_ in range(8, 513, 8) if R % r_ == 0)
    bc1 = 1.0 - ADAM_B1 ** ADAM_STEP
    bc2 = 1.0 - ADAM_B2 ** ADAM_STEP

    def body(w_ref, g_ref, m_ref, v_ref, d_ref, nm_ref, nv_ref):
        gv = g_ref[...]
        m_new = ADAM_B1 * m_ref[...] + (1.0 - ADAM_B1) * gv
        v_new = ADAM_B2 * v_ref[...] + (1.0 - ADAM_B2) * (gv * gv)
        m_hat = m_new / bc1
        v_hat = v_new / bc2
        d_ref[...] = -ADAM_LR * (m_hat / (jnp.sqrt(v_hat) + ADAM_EPS) + ADAM_WD * w_ref[...])
        nm_ref[...] = m_new
        nv_ref[...] = v_new

    blk = pl.BlockSpec((rb, C), lambda i: (i, 0))
    return pl.pallas_call(body, grid=(R // rb,), in_specs=[blk] * 4, out_specs=[blk] * 3,
                          out_shape=[S((R, C), f32)] * 3, compiler_params=_cparams("parallel"), name=name)(w, g, m, v)


def _me():
    return lax.axis_index("x"), lax.axis_index("y"), lax.axis_index("c")


def allgather_small(x, name, with_sum=False, after=None):
    r, w = x.shape
    extra = () if after is None else (after,)

    def body(x_ref, *refs):
        refs = refs[len(extra):]
        if with_sum:
            out_ref, sum_ref, send_sems, recv_sems = refs
        else:
            out_ref, send_sems, recv_sems = refs
        mx, my, mc = _me()
        me = 4 * mx + 2 * my + mc
        out_ref[me] = x_ref[...]
        peers = []
        for k in range(1, N_DEV):
            kx, ky, kc = (k >> 2) & 1, (k >> 1) & 1, k & 1
            peers.append((mx + kx - 2 * mx * kx, my + ky - 2 * my * ky, mc + kc - 2 * mc * kc))
        copies = []
        for k, peer in enumerate(peers):
            cp = pltpu.make_async_remote_copy(src_ref=x_ref, dst_ref=out_ref.at[me], send_sem=send_sems.at[k],
                                              recv_sem=recv_sems.at[k], device_id=peer, device_id_type=MESH)
            cp.start()
            copies.append(cp)
        for k, (px, py, pc) in enumerate(peers):
            pltpu.make_async_remote_copy(src_ref=x_ref, dst_ref=out_ref.at[4 * px + 2 * py + pc], send_sem=send_sems.at[k],
                                         recv_sem=recv_sems.at[k], device_id=(px, py, pc), device_id_type=MESH).wait_recv()
        for cp in copies:
            cp.wait_send()
        if with_sum:
            acc = out_ref[0]
            for j in range(1, N_DEV):
                acc = acc + out_ref[j]
            sum_ref[...] = acc

    out_shape = [S((N_DEV, r, w), f32)] + ([S((r, w), f32)] if with_sum else [])
    outs = pl.pallas_call(
        body, in_specs=[VMEM] + [ANY] * len(extra), out_specs=[VMEM] * len(out_shape), out_shape=out_shape,
        scratch_shapes=[pltpu.SemaphoreType.DMA((N_DEV - 1,)), pltpu.SemaphoreType.DMA((N_DEV - 1,))],
        compiler_params=pltpu.CompilerParams(vmem_limit_bytes=VMEM_LIMIT_BYTES), name=name)(x, *extra)
    return outs if with_sum else outs[0]


def _tile2d(R, W, max_rows):
    if R <= max_rows:
        return R, W
    fits = [r_ for r_ in range(16, max_rows + 1, 16) if R % r_ == 0]
    return (max(fits), W) if fits else (R, 256)


def add_own(g, r, core, name, twice=False):
    _, _, R, W = g.shape
    rb, wb = _tile2d(R, W, 512)
    nout = 2 if twice else 1

    def body(core_ref, a_ref, b_ref, *o_refs):
        s = (a_ref[...].astype(f32) + b_ref[...].astype(f32)).astype(bf16)
        for o_ref in o_refs:
            o_ref[...] = s

    blk = pl.BlockSpec((None, rb, wb), lambda k, i, j, core_ref: (k, i, j))
    gs = pltpu.PrefetchScalarGridSpec(
        num_scalar_prefetch=1, grid=(4, R // rb, W // wb),
        in_specs=[pl.BlockSpec((None, None, rb, wb), lambda k, i, j, core_ref: (k, core_ref[0], i, j)), blk],
        out_specs=[blk] * nout)
    outs = pl.pallas_call(body, grid_spec=gs, out_shape=[S((4, R, W), bf16)] * nout,
                          compiler_params=_cparams("parallel", "parallel", "parallel"), name=name)(core, g, r)
    return tuple(outs) if twice else outs[0]


HBM_SPEC = pl.BlockSpec(memory_space=pltpu.HBM)
SEM_SPEC = pl.BlockSpec(memory_space=pltpu.SEMAPHORE)


def _chips_copy(p_ref, land_ref, send_sems, recv_sems, a, j):
    x, y, c = _me()
    px, py = [(1 - x, y), (x, 1 - y), (1 - x, 1 - y)][j]
    return pltpu.make_async_remote_copy(src_ref=p_ref.at[2 * px + py], dst_ref=land_ref.at[2 * x + y],
                                        send_sem=send_sems.at[3 * a + j], recv_sem=recv_sems.at[3 * a + j],
                                        device_id=(px, py, c), device_id_type=MESH)


def _chips_wait_copy(p_ref, land_ref, send_sems, recv_sems, a, j):
    x, y, c = _me()
    px, py = [(1 - x, y), (x, 1 - y), (1 - x, 1 - y)][j]
    return pltpu.make_async_remote_copy(src_ref=p_ref.at[2 * px + py], dst_ref=land_ref.at[2 * px + py],
                                        send_sem=send_sems.at[3 * a + j], recv_sem=recv_sems.at[3 * a + j],
                                        device_id=(px, py, c), device_id_type=MESH)


def _xor_peers():
    mx, my, mc = _me()
    peers = []
    for k in range(1, N_DEV):
        kx, ky, kc = (k >> 2) & 1, (k >> 1) & 1, k & 1
        peers.append((mx + kx - 2 * mx * kx, my + ky - 2 * my * ky, mc + kc - 2 * mc * kc))
    return peers


def gather_start(shards, name, after):
    na = len(shards)
    lands = [lax.empty((N_DEV,) + s_.shape, s_.dtype) for s_ in shards]

    def body(*refs):
        x_refs, land_refs = refs[:na], refs[na:2 * na]
        send_sems, recv_sems, local_sems = refs[2 * na + 1:2 * na + 4]
        token = refs[-1]
        mx, my, mc = _me()
        me = 4 * mx + 2 * my + mc
        for a in range(na):
            pltpu.make_async_copy(x_refs[a], land_refs[a].at[me], local_sems.at[a]).start()
            for k, peer in enumerate(_xor_peers()):
                pltpu.make_async_remote_copy(src_ref=x_refs[a], dst_ref=land_refs[a].at[me], send_sem=send_sems.at[7 * a + k],
                                             recv_sem=recv_sems.at[7 * a + k], device_id=peer, device_id_type=MESH).start()
        token[...] = jnp.zeros_like(token)

    arrs = list(shards) + lands
    outs = pl.pallas_call(
        body, name=name, in_specs=[HBM_SPEC] * (2 * na) + [ANY],
        out_shape=[DMA((7 * na,)), DMA((7 * na,)), DMA((na,))] + [pltpu.HBM(t.shape, t.dtype) for t in arrs]
        + [S((8, 128), f32)],
        out_specs=[SEM_SPEC] * 3 + [HBM_SPEC] * (2 * na) + [VMEM],
        input_output_aliases={k: 3 + k for k in range(2 * na)},
        compiler_params=pltpu.CompilerParams(has_side_effects=pltpu.SideEffectType.DATAFLOW_SIDE_EFFECTING),
    )(*[pltpu.with_memory_space_constraint(t, pltpu.HBM) for t in arrs], after)
    return outs[0], outs[1], outs[2], list(outs[3:3 + na]), list(outs[3 + na:3 + 2 * na]), outs[-1]


def gather_wait(send_sems, recv_sems, local_sems, shards, lands, after, name):
    na = len(shards)

    def body(*refs):
        x_refs, land_refs = refs[:na], refs[na:2 * na]
        ssem, rsem, lsem = refs[2 * na:2 * na + 3]
        mx, my, mc = _me()
        me = 4 * mx + 2 * my + mc
        for a in range(na):
            pltpu.make_async_copy(x_refs[a], land_refs[a].at[me], lsem.at[a]).wait()
            for k, (px, py, pc) in enumerate(_xor_peers()):
                cp = pltpu.make_async_remote_copy(src_ref=x_refs[a], dst_ref=land_refs[a].at[4 * px + 2 * py + pc],
                                                  send_sem=ssem.at[7 * a + k], recv_sem=rsem.at[7 * a + k],
                                                  device_id=(px, py, pc), device_id_type=MESH)
                cp.wait_send()
                cp.wait_recv()

    arrs = list(shards) + list(lands)
    outs = pl.pallas_call(
        body, name=name, in_specs=[HBM_SPEC] * (2 * na) + [SEM_SPEC] * 3 + [ANY],
        out_shape=[pltpu.HBM(t.shape, t.dtype) for t in arrs], out_specs=[HBM_SPEC] * (2 * na),
        input_output_aliases={k: k for k in range(2 * na)},
        compiler_params=pltpu.CompilerParams(has_side_effects=pltpu.SideEffectType.DATAFLOW_SIDE_EFFECTING),
    )(*arrs, send_sems, recv_sems, local_sems, after)
    return list(outs[na:])


def chips_start(parts, lands, name):
    na = len(parts)

    def body(*refs):
        p_refs, land_refs = refs[:na], refs[na:2 * na]
        send_sems, recv_sems = refs[2 * na], refs[2 * na + 1]
        token = refs[-1]
        for a in range(na):
            for j in range(3):
                _chips_copy(p_refs[a], land_refs[a], send_sems, recv_sems, a, j).start()
        token[...] = jnp.zeros_like(token)

    arrs = list(parts) + list(lands)
    outs = pl.pallas_call(
        body, name=name, in_specs=[HBM_SPEC] * (2 * na),
        out_shape=[DMA((3 * na,)), DMA((3 * na,))] + [pltpu.HBM(t.shape, t.dtype) for t in arrs] + [S((8, 128), f32)],
        out_specs=[SEM_SPEC, SEM_SPEC] + [HBM_SPEC] * (2 * na) + [VMEM],
        input_output_aliases={k: 2 + k for k in range(2 * na)},
        compiler_params=pltpu.CompilerParams(has_side_effects=pltpu.SideEffectType.DATAFLOW_SIDE_EFFECTING),
    )(*[pltpu.with_memory_space_constraint(t, pltpu.HBM) for t in arrs])
    return outs[0], outs[1], list(outs[2:2 + na]), list(outs[2 + na:2 + 2 * na]), outs[-1]


def chips_wait(send_sems, recv_sems, parts, lands, after, name):
    na = len(parts)

    def body(*refs):
        p_refs, land_refs = refs[:na], refs[na:2 * na]
        ssem, rsem = refs[2 * na], refs[2 * na + 1]
        for a in range(na):
            for j in range(3):
                cp = _chips_wait_copy(p_refs[a], land_refs[a], ssem, rsem, a, j)
                cp.wait_send()
                cp.wait_recv()

    arrs = list(parts) + list(lands)
    outs = pl.pallas_call(
        body, name=name, in_specs=[HBM_SPEC] * (2 * na) + [SEM_SPEC, SEM_SPEC, ANY],
        out_shape=[pltpu.HBM(t.shape, t.dtype) for t in arrs], out_specs=[HBM_SPEC] * (2 * na),
        input_output_aliases={k: k for k in range(2 * na)},
        compiler_params=pltpu.CompilerParams(has_side_effects=pltpu.SideEffectType.DATAFLOW_SIDE_EFFECTING),
    )(*arrs, send_sems, recv_sems, after)
    return list(outs[na:])


def sum_adamw(recv, w, m, v, layer, name, into=None, after=None):
    _, R, W = recv.shape
    rb, wb = _tile2d(R, W, 256)
    bc1 = 1.0 - ADAM_B1 ** ADAM_STEP
    bc2 = 1.0 - ADAM_B2 ** ADAM_STEP
    n_into = 0 if into is None else 4
    extra = () if after is None else (after,)

    def body(r_ref, w_ref, m_ref, v_ref, *refs):
        g_ref, d_ref, nm_ref, nv_ref = refs[n_into + len(extra):]
        gv = r_ref[0].astype(f32)
        for k in range(1, 4):
            gv = gv + r_ref[k].astype(f32)
        m_new = ADAM_B1 * m_ref[...] + (1.0 - ADAM_B1) * gv
        v_new = ADAM_B2 * v_ref[...] + (1.0 - ADAM_B2) * (gv * gv)
        g_ref[...] = gv
        d_ref[...] = -ADAM_LR * ((m_new / bc1) / (jnp.sqrt(v_new / bc2) + ADAM_EPS) + ADAM_WD * w_ref[...])
        nm_ref[...] = m_new
        nv_ref[...] = v_new

    if layer is None:
        wblk = pl.BlockSpec((rb, wb), lambda i, j: (i, j))
        oshape = S((R, W), f32)
    else:
        wblk = pl.BlockSpec((None, rb, wb), lambda i, j: (layer, i, j))
        oshape = S(w.shape, f32)
    return pl.pallas_call(
        body, grid=(R // rb, W // wb),
        in_specs=[pl.BlockSpec((4, rb, wb), lambda i, j: (0, i, j)), wblk, wblk, wblk] + [ANY] * (n_into + len(extra)),
        out_specs=[wblk] * 4, out_shape=[oshape] * 4, input_output_aliases={4 + k: k for k in range(n_into)},
        compiler_params=_cparams("parallel", "parallel"), name=name)(recv, w, m, v, *(into or ()), *extra)


def sum_rows(a, name):
    K, R, W = a.shape
    rb = _pick(R, (512, 256, 128, 64, 32, 16))

    def body(a_ref, o_ref):
        acc = a_ref[0].astype(f32)
        for k in range(1, K):
            acc = acc + a_ref[k].astype(f32)
        o_ref[...] = acc

    return pl.pallas_call(body, grid=(R // rb,), in_specs=[pl.BlockSpec((K, rb, W), lambda i: (0, i, 0))],
                          out_specs=pl.BlockSpec((rb, W), lambda i: (i, 0)), out_shape=S((R, W), f32),
                          compiler_params=_cparams("parallel"), name=name)(a)


DMA = pltpu.SemaphoreType.DMA


class GatherExchange:
    def __init__(self, arrays):
        self.arrays = list(arrays)
        self.na = len(self.arrays)
        self.out_shape = [S((N_DEV,) + a.shape, a.dtype) for a in self.arrays]
        self.scratch = [DMA((7 * self.na,)), DMA((7 * self.na,)), DMA((self.na,))]

    def ops(self, x_refs, out_refs, sems):
        send_sems, recv_sems, local_sems = sems
        na = self.na
        x, y, c = _me()
        me, sibling = (x, y, c), (x, y, 1 - c)
        chips = [(1 - x, y), (x, 1 - y), (1 - x, 1 - y)]

        def rows(a, px, py, pc):
            return out_refs[a].at[4 * px + 2 * py + pc]

        def copy(a, k, block, to, src=None):
            return pltpu.make_async_remote_copy(
                src_ref=rows(a, *block) if src is None else src, dst_ref=rows(a, *block),
                send_sem=send_sems.at[7 * a + k], recv_sem=recv_sems.at[7 * a + k], device_id=to, device_id_type=MESH)

        def local(a):
            return pltpu.make_async_copy(x_refs[a], rows(a, *me), local_sems.at[a])

        def first(a):
            return [copy(a, 0, me, sibling, src=x_refs[a])] + [copy(a, 1 + j, me, (*chip, c), src=x_refs[a])
                                                                for j, chip in enumerate(chips)]

        def start():
            for a in range(na):
                local(a).start()
                for cp in first(a):
                    cp.start()

        def mid():
            for a in range(na):
                for j, chip in enumerate(chips):
                    copy(a, 1 + j, (*chip, c), me).wait_recv()
                    copy(a, 4 + j, (*chip, c), sibling).start()

        def finish():
            for a in range(na):
                copy(a, 0, sibling, me).wait_recv()
                for j, chip in enumerate(chips):
                    copy(a, 4 + j, (*chip, 1 - c), me).wait_recv()
                for cp in first(a) + [copy(a, 4 + j, (*chip, c), sibling) for j, chip in enumerate(chips)]:
                    cp.wait_send()
                local(a).wait()

        return start, mid, finish


class SiblingExchange:
    def __init__(self, arrays):
        self.arrays = list(arrays)
        self.na = len(self.arrays)
        self.out_shape = [S((4,) + g.shape[2:], g.dtype) for g in self.arrays]
        self.scratch = [DMA((self.na,)), DMA((self.na,))]

    def ops(self, g_refs, out_refs, sems):
        send_sems, recv_sems = sems
        x, y, c = _me()

        def copy(a):
            return pltpu.make_async_remote_copy(src_ref=g_refs[a].at[:, 1 - c], dst_ref=out_refs[a],
                                                send_sem=send_sems.at[a], recv_sem=recv_sems.at[a],
                                                device_id=(x, y, 1 - c), device_id_type=MESH)

        def start():
            for a in range(self.na):
                copy(a).start()

        def finish():
            for a in range(self.na):
                copy(a).wait()

        return start, None, finish


class ChipsExchange:
    def __init__(self, arrays):
        self.arrays = list(arrays)
        self.na = len(self.arrays)
        self.out_shape = [S(p.shape, p.dtype) for p in self.arrays]
        self.scratch = [DMA((3 * self.na,)), DMA((3 * self.na,)), DMA((self.na,))]

    def ops(self, p_refs, out_refs, sems):
        send_sems, recv_sems, local_sems = sems
        x, y, c = _me()
        mine = 2 * x + y
        chips = [(1 - x, y), (x, 1 - y), (1 - x, 1 - y)]

        def local(a):
            return pltpu.make_async_copy(p_refs[a].at[mine], out_refs[a].at[mine], local_sems.at[a])

        def send(a, j):
            px, py = chips[j]
            return pltpu.make_async_remote_copy(src_ref=p_refs[a].at[2 * px + py], dst_ref=out_refs[a].at[mine],
                                                send_sem=send_sems.at[3 * a + j], recv_sem=recv_sems.at[3 * a + j],
                                                device_id=(px, py, c), device_id_type=MESH)

        def recv(a, j):
            px, py = chips[j]
            return pltpu.make_async_remote_copy(src_ref=p_refs[a].at[mine], dst_ref=out_refs[a].at[2 * px + py],
                                                send_sem=send_sems.at[3 * a + j], recv_sem=recv_sems.at[3 * a + j],
                                                device_id=(px, py, c), device_id_type=MESH)

        def start():
            for a in range(self.na):
                local(a).start()
                for j in range(3):
                    send(a, j).start()

        def finish():
            for a in range(self.na):
                for j in range(3):
                    recv(a, j).wait_recv()
                for j in range(3):
                    send(a, j).wait_send()
                local(a).wait()

        return start, None, finish


def exchange(ex, name):
    na = ex.na

    def body(*refs):
        start, mid, finish = ex.ops(refs[:na], refs[na:2 * na], refs[2 * na:])
        start()
        if mid is not None:
            mid()
        finish()

    return pl.pallas_call(body, in_specs=[ANY] * na, out_specs=[ANY] * na, out_shape=ex.out_shape,
                          scratch_shapes=ex.scratch, name=name)(*ex.arrays)


def _host_call(body, grid, in_specs, out_specs, out_shape, scratch_shapes, sem, name, args, hosted):
    if hosted is None:
        res = pl.pallas_call(body, grid=grid, in_specs=in_specs, out_specs=out_specs, out_shape=out_shape,
                             scratch_shapes=scratch_shapes, compiler_params=_cparams(*sem), name=name)(*args)
        return res, None
    n_in, n_out, n_sc, na = len(in_specs), len(out_shape), len(scratch_shapes), hosted.na
    nsteps = 1
    for g_ in grid:
        nsteps *= g_
    mid_step = (3 * nsteps) // 4
    i1 = n_in + na
    i2 = i1 + n_out
    i3 = i2 + na
    i4 = i3 + n_sc

    def wrapped(*refs):
        step = pl.program_id(0)
        for ax in range(1, len(grid)):
            step = step * grid[ax] + pl.program_id(ax)
        start, mid, finish = hosted.ops(refs[n_in:i1], refs[i2:i3], refs[i4:])
        pl.when(step == 0)(start)
        if mid is not None:
            pl.when(step == mid_step)(mid)
        body(*refs[:n_in], *refs[i1:i2], *refs[i3:i4])
        pl.when(step == nsteps - 1)(finish)

    res = pl.pallas_call(
        wrapped, grid=grid, in_specs=list(in_specs) + [ANY] * na, out_specs=list(out_specs) + [ANY] * na,
        out_shape=list(out_shape) + hosted.out_shape, scratch_shapes=list(scratch_shapes) + hosted.scratch,
        compiler_params=_cparams(*(("arbitrary",) * len(grid))), name=name)(*args, *hosted.arrays)
    return res[:n_out], res[n_out:]


PACK_ALIGN = 16 * PACK_W


def _pad_to(v, mult):
    n = v.shape[-1]
    extra = (-n) % mult
    if extra == 0:
        return v
    return jnp.concatenate([v, jnp.zeros(v.shape[:-1] + (extra,), v.dtype)], axis=-1)


def _f32_as_bf16_pairs(v):
    return lax.bitcast_convert_type(v.reshape(-1), bf16).reshape(-1)


def _bf16_pairs_as_f32(v):
    return lax.bitcast_convert_type(v.reshape(v.shape[:-1] + (v.shape[-1] // 2, 2)), f32)


def _col_shards(gw):
    lead = gw.shape[:-1]
    n = gw.shape[-1] // N_DEV
    t = gw.reshape(lead + (N_DEV, n))
    t = jnp.moveaxis(t, -2, 0)
    return t.reshape(N_DEV, -1)


def kernel(x, c, ctx, c_ctx, mod_w, mod_b, norm1_w, norm2_w, ssd_w_in, ssd_conv_w, ssd_conv_b, ssd_dt_bias, ssd_a_log, ssd_d, ssd_norm_w, ssd_w_out, conf_w_pw1, conf_b_pw1, conf_w_dw, conf_b_dw, conf_ln_w, conf_ln_b, conf_w_pw2, conf_b_pw2, ffn_w_up, ffn_conv_w, ffn_conv_b, ffn_w_down, final_norm_w, loss_target, m_c_ctx, m_mod_w, m_mod_b, m_norm1_w, m_norm2_w, m_ssd_w_in, m_ssd_conv_w, m_ssd_conv_b, m_ssd_dt_bias, m_ssd_a_log, m_ssd_d, m_ssd_norm_w, m_ssd_w_out, m_conf_w_pw1, m_conf_b_pw1, m_conf_w_dw, m_conf_b_dw, m_conf_ln_w, m_conf_ln_b, m_conf_w_pw2, m_conf_b_pw2, m_ffn_w_up, m_ffn_conv_w, m_ffn_conv_b, m_ffn_w_down, m_final_norm_w, v_c_ctx, v_mod_w, v_mod_b, v_norm1_w, v_norm2_w, v_ssd_w_in, v_ssd_conv_w, v_ssd_conv_b, v_ssd_dt_bias, v_ssd_a_log, v_ssd_d, v_ssd_norm_w, v_ssd_w_out, v_conf_w_pw1, v_conf_b_pw1, v_conf_w_dw, v_conf_b_dw, v_conf_ln_w, v_conf_ln_b, v_conf_w_pw2, v_conf_b_pw2, v_ffn_w_up, v_ffn_conv_w, v_ffn_conv_b, v_ffn_w_down, v_final_norm_w):
    mx, my, mc = _me()
    me = 4 * mx + 2 * my + mc
    L = x.shape[1]
    LC = ctx.shape[1]
    T = LC + L
    w_in_cols = ssd_w_in.shape[2] * N_DEV
    n_dt = w_in_cols - DI - CONVD

    small = [c[0], ssd_conv_w[0], conf_b_pw1[0], conf_w_dw[0], conf_b_dw[0], conf_ln_w[0], conf_ln_b[0], conf_b_pw2[0],
             ffn_conv_w]
    parts = [_f32_as_bf16_pairs(t) for t in small]
    sizes = [p.shape[0] for p in parts]
    small_flat = _pad_to(jnp.concatenate(parts), PACK_ALIGN).reshape(-1, PACK_W)
    w_in, small_g = exchange(GatherExchange([ssd_w_in[0].astype(bf16), small_flat]), "gather_first")
    w_up, w_down = [None, None], [None, None]
    small_g = small_g.reshape(N_DEV, -1)
    offs = [0]
    for s_ in sizes:
        offs.append(offs[-1] + s_)
    sm = [_bf16_pairs_as_f32(small_g[:, offs[i]:offs[i + 1]]) for i in range(len(sizes))]

    def cols(pc, K):
        return jnp.moveaxis(pc.reshape(N_DEV, K, -1), 0, 1).reshape(K, -1)

    c_all = sm[0]
    conv_w5 = cols(sm[1], 5)
    b_pw1 = sm[2].reshape(1, 2 * D)
    w_dw = cols(sm[3], CONF_K)
    b_dw, ln_w, ln_b, b_pw2 = (sm[i].reshape(1, D) for i in (4, 5, 6, 7))
    fcw = sm[8].reshape(N_DEV, 2, 9, FH // N_DEV)
    ffn_cw = [cols(fcw[:, i].reshape(N_DEV, -1), 9) for i in range(2)]
    in_segs = (DI, CONVD, n_dt)
    up_segs = (FH, FH)
    pw1_segs = (D, D)

    c16 = jnp.concatenate([c_all, jnp.broadcast_to(c_ctx[None, :], (N_DEV, D))], axis=0)
    m_sh = mod_fwd(c16, mod_w, "mod_fwd")
    mod_cols = mod_w.shape[2]
    m_gath = allgather_small(m_sh.reshape(2 * CROWS, mod_cols), "gather_mod")
    fly_a = gather_start([ssd_w_out[0].astype(bf16), ffn_w_up[0].astype(bf16), ffn_w_down[0].astype(bf16)],
                         "gather_a_start", m_gath)
    fly_b = gather_start([conf_w_pw1[0].astype(bf16), conf_w_pw2[0].astype(bf16)], "gather_b_start", fly_a[-1])
    fly_c = gather_start([ffn_w_up[1].astype(bf16), ffn_w_down[1].astype(bf16)], "gather_c_start", fly_b[-1])
    m_all = jnp.moveaxis(m_gath.reshape(N_DEV, 2, CROWS, mod_cols), 0, 2).reshape(2, CROWS, 6 * D) + mod_b[:, None, :]
    m_all = m_all + fly_c[-1][0, 0]
    m_lat = lax.dynamic_index_in_dim(m_all, me, axis=1, keepdims=False).reshape(2, 6, 1, D)
    m_ctx = m_all[:, N_DEV].reshape(2, 6, 1, D)
    zero_row = jnp.zeros((1, D), f32)

    def ffn_fwd(a2, i, tag):
        val, gate = smm_fwd(a2, w_up[i], None, up_segs, f"ffn{tag}_up")
        act, gs_, gvds, _ = ffn_gate_fwd(val, gate, ffn_cw[i], ffn_conv_b[i][None], f"ffn{tag}_gate")
        o2 = matmul(act, w_down[i], "nn", f32, f"ffn{tag}_down")
        return o2, (a2, gate, gs_, gvds, act)

    def ffn_bwd(do2, i, saved, tag):
        a2, gate, gs_, gvds, act = saved
        g_down = matmul(act, do2, "tn", bf16, f"ffn{tag}_down_dw")
        dact = matmul(do2, w_down[i], "nt", bf16, f"ffn{tag}_down_dx")
        dval, dgate, dcw, dcb = ffn_gate_bwd(gate, gs_, gvds, ffn_cw[i], dact, f"ffn{tag}_gate_bwd")
        g_up = smm_dw(a2, [dval, dgate], FH // 4, up_segs, 2, True, f"ffn{tag}_up_dw")
        da2, _ = smm_dx([dval, dgate], w_up[i], None, up_segs, bf16, f"ffn{tag}_up_dx")
        return da2, dict(w_up=g_up, w_down=g_down, conv_w=dcw, conv_b=dcb)

    nctx = LC // Q
    hx = x[0]
    sc0 = jnp.stack([m_ctx[0, 1], m_lat[0, 1]])
    sh0 = jnp.stack([m_ctx[0, 0], m_lat[0, 0]])
    a0 = modnorm_fwd(hx, norm1_w[0][None], sc0, sh0, LC // TB, "ssd_norm", ctx=ctx[0])
    z, xbc_pre, dt_raw = smm_fwd(a0, w_in, None, in_segs, "ssd_in")
    segs = ((0, LC), (LC, L))
    xbc, xbc_dsilu, _ = ssd_conv_fwd(xbc_pre, conv_w5, ssd_conv_b, segs, "ssd_conv")
    dt4 = dt_raw[:, :n_dt].reshape(T, 2, G, HPG)
    dtc = jnp.transpose(dt4, (1, 2, 0, 3))
    dtr = jnp.transpose(dt4, (1, 2, 3, 0))
    bias3 = ssd_dt_bias[0].reshape(2, G, HPG)
    alog3 = ssd_a_log[0].reshape(2, G, HPG)
    bc_, br_ = bias3[:, :, None, :], bias3[:, :, :, None]
    alc, alr = alog3[:, :, None, :], alog3[:, :, :, None]
    (y2, s_in_all), _ = ssd_scan_fwd(xbc, dtc, dtr, bc_, br_, alc, alr, nctx, "ssd_scan")
    dexp = jnp.repeat(ssd_d[0], P)[None, :]
    yn = ssd_gate_fwd(y2, xbc, z, dexp, ssd_norm_w, LC // GTB, "ssd_gate")
    w_out_g, w_up[0], w_down0_g = gather_wait(*fly_a[:5], yn, "gather_a_wait")
    w_out = w_out_g.reshape(DI, D)
    w_down[0] = w_down0_g.reshape(FH, D)
    o_ssd = matmul(yn, w_out, "nn", f32, "ssd_out")
    h1, a2_0 = resnorm_fwd(hx, o_ssd, m_lat[0, 2], zero_row, norm2_w[0][None], m_lat[0, 4], m_lat[0, 3], "ssd_res")
    o2_0, ffn0_saved = ffn_fwd(a2_0, 0, "0")

    h2, a1 = resnorm_fwd(h1, o2_0, m_lat[0, 5], zero_row, norm1_w[1][None], m_lat[1, 1], m_lat[1, 0], "ffn0_res")
    w_pw1, w_pw2_g = gather_wait(*fly_b[:5], a1, "gather_b_wait")
    w_pw2 = w_pw2_g.reshape(D, D)
    pa, pg = smm_fwd(a1, w_pw1, None, pw1_segs, "conf_pw1")
    dwc, _ = conf_glu_conv_fwd(pa, pg, b_pw1, w_dw, b_dw, "conf_conv")
    s1 = ln_silu_fwd(dwc, ln_w, ln_b, "conf_ln")
    o_conf = matmul(s1, w_pw2, "nn", f32, "conf_pw2")
    h3, a2_1 = resnorm_fwd(h2, o_conf, m_lat[1, 2], b_pw2, norm2_w[1][None], m_lat[1, 4], m_lat[1, 3], "conf_res")
    w_up[1], w_down1_g = gather_wait(*fly_c[:5], h3, "gather_c_wait")
    w_down[1] = w_down1_g.reshape(FH, D)
    o2_1, ffn1_saved = ffn_fwd(a2_1, 1, "1")

    loss_part, dh4, g_final, do2_1, dg2_1 = final_loss(h3, o2_1, m_lat[1, 5], final_norm_w[None], loss_target[0],
                                                       "loss_head")
    da2_1, gf1 = ffn_bwd(do2_1, 1, ffn1_saved, "1")
    dh3, dn2_1, dsc2_1, dsh2_1, do_conf, dg1_1, g_b_pw2 = normres_bwd(
        h3, norm2_w[1][None], m_lat[1, 4], m_lat[1, 3], da2_1, dh4, o_conf, m_lat[1, 2], b_pw2, "ffn1_norm_bwd")
    gf1.update(norm2=dn2_1, sh2=dsh2_1, sc2=dsc2_1, g2=dg2_1)
    g_pw2 = matmul(s1, do_conf, "tn", bf16, "conf_pw2_dw")
    ds1 = matmul(do_conf, w_pw2, "nt", bf16, "conf_pw2_dx")
    ddwc, g_ln_w, g_ln_b = ln_silu_bwd(dwc, ln_w, ln_b, ds1, "conf_ln_bwd")
    dpa, dpg, dba, dbg, g_w_dw, g_b_dw = conf_glu_conv_bwd(pa, pg, b_pw1, w_dw, ddwc, "conf_conv_bwd")
    g_b_pw1 = jnp.concatenate([dba, dbg], axis=1)
    g_pw1 = smm_dw(a1, [dpa, dpg], 2 * D // N_DEV, pw1_segs, 1, False, "conf_pw1_dw")
    da1, _ = smm_dx([dpa, dpg], w_pw1, None, pw1_segs, bf16, "conf_pw1_dx")
    dh2, g_n1_1, dsc1_1, dsh1_1, do2_0, dg2_0, _ = normres_bwd(
        h2, norm1_w[1][None], m_lat[1, 1], m_lat[1, 0], da1, dh3, o2_0, m_lat[0, 5], zero_row, "conf_norm_bwd")
    da2_0, gf0 = ffn_bwd(do2_0, 0, ffn0_saved, "0")
    dh1, dn2_0, dsc2_0, dsh2_0, do_ssd, dg1_0, _ = normres_bwd(
        h1, norm2_w[0][None], m_lat[0, 4], m_lat[0, 3], da2_0, dh2, o_ssd, m_lat[0, 2], zero_row, "ffn0_norm_bwd")
    gf0.update(norm2=dn2_0, sh2=dsh2_0, sc2=dsc2_0, g2=dg2_0)
    g_w_out = matmul(yn, do_ssd, "tn", bf16, "ssd_out_dw")
    dyn = matmul(do_ssd, w_out, "nt", bf16, "ssd_out_dx")
    core = mc.reshape(1).astype(jnp.int32)

    def by_device(t):
        return t.reshape((4, 2, -1, t.shape[-1]))

    early = [by_device(t) for t in (gf1["w_up"], gf1["w_down"], g_pw2, g_pw1, gf0["w_up"], gf0["w_down"], g_w_out)]
    (dy, dz, g_dexp, g_ssd_norm), early_sib = ssd_gate_bwd(
        y2, xbc, z, dexp, ssd_norm_w, dyn, LC // GTB, "ssd_gate_bwd", SiblingExchange(early))
    early_part = [add_own(t, r_, core, f"reduce_add{i}") for i, (t, r_) in enumerate(zip(early, early_sib))]
    (dxbc2, ddtc, ddtr, dbc, dbr, dalc, dalr), early_red = ssd_scan_bwd(
        xbc, dtc, dtr, bc_, br_, alc, alr, s_in_all, dy, nctx, "ssd_scan_bwd", ChipsExchange(early_part))
    ddt = (jnp.transpose(ddtc, (2, 0, 1, 3)) + jnp.transpose(ddtr, (3, 0, 1, 2))).reshape(T, n_dt)
    g_dt_bias = (dbc[:, :, 0, :] + dbr[:, :, :, 0]).reshape(2, NH_SSD)
    g_a_log = (dalc[:, :, 0, :] + dalr[:, :, :, 0]).reshape(2, NH_SSD)
    g_ssd_d = g_dexp[0, :NH_SSD]
    du, g_conv_w5, g_conv_b5 = ssd_conv_bwd(xbc_pre, conv_w5, xbc_dsilu, dxbc2, dy, dexp, segs, "ssd_conv_bwd")
    ddt_p = _pad_to(ddt, 128).astype(bf16)
    g_w_in = smm_dw(a0, [dz, du, ddt_p], w_in.shape[-1], in_segs, 2, True, "ssd_in_dw")
    g_ffn_cw = jnp.stack([gf0["conv_w"], gf1["conv_w"]])
    small_shards = [_col_shards(t) for t in (g_conv_w5, g_b_pw1, g_w_dw, g_b_dw, g_ln_w, g_ln_b, g_b_pw2, g_ffn_cw)]
    gsizes = [s_.shape[1] for s_ in small_shards]
    g_small = _pad_to(jnp.concatenate(small_shards, axis=1), PACK_ALIGN).astype(bf16)
    late = [by_device(g_w_in), by_device(g_small.reshape(N_DEV, -1, PACK_W))]
    da0, late_sib = smm_dx([dz, du, ddt_p], w_in, None, in_segs, f32, "ssd_in_dx", SiblingExchange(late))
    late_part = [add_own(t, r_, core, f"reduce_add_late{i}", twice=True) for i, (t, r_) in enumerate(zip(late, late_sib))]
    late_flying = chips_start([p_[0] for p_ in late_part], [p_[1] for p_ in late_part], "reduce_chips_late_start")
    dh0, g_n1_0, dsc1_0, dsh1_0 = modnorm_bwd(hx, norm1_w[0][None], sc0, sh0, da0, dh1, LC // TB, "ssd_norm_bwd",
                                              ctx=ctx[0])
    grad_x = dh0[None]

    r_up1, r_down1, r_pw2, r_pw1, r_up0, r_down0, r_out = early_red
    big = {}
    def tr(t):
        return jnp.swapaxes(t, -1, -2)

    up_t, m_up_t, v_up_t = tr(ffn_w_up), tr(m_ffn_w_up), tr(v_ffn_w_up)
    send_sems, recv_sems, late_p, late_land, token = late_flying
    up0 = sum_adamw(r_up0, up_t, m_up_t, v_up_t, 0, "adamw_ffn_w_up0", after=token)
    up1 = sum_adamw(r_up1, up_t, m_up_t, v_up_t, 1, "adamw_ffn_w_up1", into=up0)
    big["ffn_w_up"] = tuple(tr(t) for t in up1)
    big["conf_w_pw1"] = sum_adamw(r_pw1, conf_w_pw1[0], m_conf_w_pw1[0], v_conf_w_pw1[0], None, "adamw_conf_w_pw1",
                                  after=up1[0])
    big["ssd_w_out"] = sum_adamw(r_out, ssd_w_out[0], m_ssd_w_out[0], v_ssd_w_out[0], None, "adamw_ssd_w_out",
                                 after=big["conf_w_pw1"][0])
    dn0 = sum_adamw(r_down0, ffn_w_down, m_ffn_w_down, v_ffn_w_down, 0, "adamw_ffn_w_down0", after=big["ssd_w_out"][0])
    big["ffn_w_down"] = sum_adamw(r_down1, ffn_w_down, m_ffn_w_down, v_ffn_w_down, 1, "adamw_ffn_w_down1", into=dn0)
    big["conf_w_pw2"] = sum_adamw(r_pw2, conf_w_pw2[0], m_conf_w_pw2[0], v_conf_w_pw2[0], None, "adamw_conf_w_pw2",
                                  after=big["ffn_w_down"][0])

    zeros_d = jnp.zeros((1, D), f32)
    dm_lat = jnp.stack([
        jnp.concatenate([dsh1_0[1], dsc1_0[1], dg1_0, gf0["sh2"], gf0["sc2"], gf0["g2"]], axis=1),
        jnp.concatenate([dsh1_1, dsc1_1, dg1_1, gf1["sh2"], gf1["sc2"], gf1["g2"]], axis=1)])
    dm_ctx = jnp.stack([
        jnp.concatenate([dsh1_0[0], dsc1_0[0]] + [zeros_d] * 4, axis=1), jnp.zeros((1, 6 * D), f32)])
    dm_mine = jnp.concatenate([dm_lat.reshape(2, 6 * D), dm_ctx.reshape(2, 6 * D),
                               jnp.zeros((4, 6 * D), f32)], axis=0)
    dm_g = allgather_small(dm_mine, "gather_dmod", after=big["conf_w_pw2"][0])
    dm_all = jnp.concatenate([jnp.moveaxis(dm_g[:, 0:2], 0, 1), jnp.moveaxis(dm_g[:, 2:4], 0, 1)], axis=1)
    dm_sh = lax.dynamic_slice_in_dim(dm_all, me * mod_cols, mod_cols, axis=2)
    g_mod_w, g_cctx_part, g_mod_b = mod_bwd(c16, mod_w, dm_sh, dm_all, "mod_bwd")

    rep = [jnp.stack([g_n1_0[0], g_n1_1[0]]), jnp.stack([gf0["norm2"][0], gf1["norm2"][0]]), g_conv_b5, g_dt_bias, g_a_log,
           g_ssd_d, g_ssd_norm, jnp.stack([gf0["conv_b"][0], gf1["conv_b"][0]]), g_final, g_cctx_part, loss_part[:, :1]]
    rep_sizes = [r_.size for r_ in rep]
    rep_flat = _pad_to(jnp.concatenate([r_.reshape(-1) for r_ in rep]), 8 * PACK_W).reshape(-1, PACK_W)
    _, rep_sum = allgather_small(rep_flat, "reduce_replicated", with_sum=True)
    rep_sum = rep_sum.reshape(-1)
    roffs = [0]
    for s_ in rep_sizes:
        roffs.append(roffs[-1] + s_)
    rp = [rep_sum[roffs[i]:roffs[i + 1]] for i in range(len(rep_sizes))]
    loss = rp[10].reshape(())

    r_in, r_small = chips_wait(send_sems, recv_sems, late_p, late_land, rep_sum, "reduce_chips_late_wait")
    w_in_res = sum_adamw(r_in, tr(ssd_w_in[0]), tr(m_ssd_w_in[0]), tr(v_ssd_w_in[0]), None, "adamw_ssd_w_in")
    big["ssd_w_in"] = tuple(tr(t) for t in w_in_res)
    g_flat = sum_rows(r_small, "reduce_sum_small").reshape(-1)
    goffs = [0]
    for s_ in gsizes:
        goffs.append(goffs[-1] + s_)
    gs = [g_flat[goffs[i]:goffs[i + 1]] for i in range(len(gsizes))]
    grads = {
        "c_ctx": rp[9], "mod_w": g_mod_w, "mod_b": g_mod_b, "norm1_w": rp[0], "norm2_w": rp[1],
        "ssd_conv_w": gs[0], "ssd_conv_b": rp[2], "ssd_dt_bias": rp[3], "ssd_a_log": rp[4], "ssd_d": rp[5],
        "ssd_norm_w": rp[6], "conf_b_pw1": gs[1], "conf_w_dw": gs[2],
        "conf_b_dw": gs[3], "conf_ln_w": gs[4], "conf_ln_b": gs[5], "conf_b_pw2": gs[6],
        "ffn_conv_w": gs[7], "ffn_conv_b": rp[7], "final_norm_w": rp[8],
    }
    weights = dict(c_ctx=c_ctx, mod_w=mod_w, mod_b=mod_b, norm1_w=norm1_w, norm2_w=norm2_w, ssd_w_in=ssd_w_in, ssd_conv_w=ssd_conv_w, ssd_conv_b=ssd_conv_b, ssd_dt_bias=ssd_dt_bias, ssd_a_log=ssd_a_log, ssd_d=ssd_d, ssd_norm_w=ssd_norm_w, ssd_w_out=ssd_w_out, conf_w_pw1=conf_w_pw1, conf_b_pw1=conf_b_pw1, conf_w_dw=conf_w_dw, conf_b_dw=conf_b_dw, conf_ln_w=conf_ln_w, conf_ln_b=conf_ln_b, conf_w_pw2=conf_w_pw2, conf_b_pw2=conf_b_pw2, ffn_w_up=ffn_w_up, ffn_conv_w=ffn_conv_w, ffn_conv_b=ffn_conv_b, ffn_w_down=ffn_w_down, final_norm_w=final_norm_w)
    m_in = dict(c_ctx=m_c_ctx, mod_w=m_mod_w, mod_b=m_mod_b, norm1_w=m_norm1_w, norm2_w=m_norm2_w, ssd_w_in=m_ssd_w_in, ssd_conv_w=m_ssd_conv_w, ssd_conv_b=m_ssd_conv_b, ssd_dt_bias=m_ssd_dt_bias, ssd_a_log=m_ssd_a_log, ssd_d=m_ssd_d, ssd_norm_w=m_ssd_norm_w, ssd_w_out=m_ssd_w_out, conf_w_pw1=m_conf_w_pw1, conf_b_pw1=m_conf_b_pw1, conf_w_dw=m_conf_w_dw, conf_b_dw=m_conf_b_dw, conf_ln_w=m_conf_ln_w, conf_ln_b=m_conf_ln_b, conf_w_pw2=m_conf_w_pw2, conf_b_pw2=m_conf_b_pw2, ffn_w_up=m_ffn_w_up, ffn_conv_w=m_ffn_conv_w, ffn_conv_b=m_ffn_conv_b, ffn_w_down=m_ffn_w_down, final_norm_w=m_final_norm_w)
    v_in = dict(c_ctx=v_c_ctx, mod_w=v_mod_w, mod_b=v_mod_b, norm1_w=v_norm1_w, norm2_w=v_norm2_w, ssd_w_in=v_ssd_w_in, ssd_conv_w=v_ssd_conv_w, ssd_conv_b=v_ssd_conv_b, ssd_dt_bias=v_ssd_dt_bias, ssd_a_log=v_ssd_a_log, ssd_d=v_ssd_d, ssd_norm_w=v_ssd_norm_w, ssd_w_out=v_ssd_w_out, conf_w_pw1=v_conf_w_pw1, conf_b_pw1=v_conf_b_pw1, conf_w_dw=v_conf_w_dw, conf_b_dw=v_conf_b_dw, conf_ln_w=v_conf_ln_w, conf_ln_b=v_conf_ln_b, conf_w_pw2=v_conf_w_pw2, conf_b_pw2=v_conf_b_pw2, ffn_w_up=v_ffn_w_up, ffn_conv_w=v_ffn_conv_w, ffn_conv_b=v_ffn_conv_b, ffn_w_down=v_ffn_w_down, final_norm_w=v_final_norm_w)

    out_g, out_d, out_m, out_v = [], [], [], []
    for name_, w_ in weights.items():
        shape = w_.shape
        if name_ in big:
            for lst, t in zip((out_g, out_d, out_m, out_v), big[name_]):
                lst.append(t.reshape(shape))
            continue
        cols2 = shape[-1] if len(shape) > 1 else shape[0]
        g2 = grads[name_].reshape(-1, cols2)
        d_, nm_, nv_ = adamw(w_.reshape(-1, cols2), g2, m_in[name_].reshape(-1, cols2), v_in[name_].reshape(-1, cols2),
                             f"adamw_{name_}")
        out_g.append(g2.reshape(shape))
        out_d.append(d_.reshape(shape))
        out_m.append(nm_.reshape(shape))
        out_v.append(nv_.reshape(shape))
    return (loss, grad_x, *out_g, *out_d, *out_m, *out_v)
```

```python
import functools

import jax
import jax.numpy as jnp
from jax import lax
from jax.experimental import pallas as pl
from jax.experimental.pallas import tpu as pltpu

f32 = jnp.float32
bf16 = jnp.bfloat16
HI = lax.Precision.HIGHEST
S = jax.ShapeDtypeStruct
MESH = pl.DeviceIdType.MESH
ANY = pl.BlockSpec(memory_space=pl.ANY)
VMEM = pl.BlockSpec(memory_space=pltpu.VMEM)

N_DEV = 8
D = 1024
DI = 2048
CONVD = 4096
FH = 2816
GRID_W = 64
Q = 128
HPG = 4
P = 64
N = 128
G = 8
GW = HPG * P
NH_SSD = G * HPG
EPS = 1e-6
ADAM_LR, ADAM_B1, ADAM_B2, ADAM_EPS, ADAM_WD, ADAM_STEP = 0.001, 0.9, 0.999, 1e-08, 0.01, 10
VMEM_LIMIT_BYTES = 56 * 1024 * 1024
PACK_W = 1024
TB = 256
LTB = 512


def _cparams(*sem):
    return pltpu.CompilerParams(dimension_semantics=sem, vmem_limit_bytes=VMEM_LIMIT_BYTES)


def _pick(n, prefs):
    for p in prefs:
        if n % p == 0:
            return p
    return n


def _sigmoid(x):
    return 1.0 / (1.0 + jnp.exp(-x))


def _softplus(x):
    return jnp.maximum(x, 0.0) + jnp.log(1.0 + jnp.exp(-jnp.abs(x)))


def matmul(a, b, mode, out_dtype, name):
    if mode == "nn":
        (M, K), (_, Nn) = a.shape, b.shape
        bm, bn, bk = _pick(M, (512, 384, 256, 128)), Nn, K
    elif mode == "tn":
        (K, M), (_, Nn) = a.shape, b.shape
        bm, bn, bk = M, Nn, _pick(K, (512, 256, 128))
    else:
        (M, K), (Nn, _) = a.shape, b.shape
        bm, bn, bk = _pick(M, (512, 384, 256, 128)), Nn, K
    nk = K // bk
    dims = {"nn": (((1,), (0,)), ((), ())), "tn": (((0,), (0,)), ((), ())), "nt": (((1,), (1,)), ((), ()))}[mode]

    def body(a_ref, b_ref, o_ref, acc_ref):
        k = pl.program_id(2)

        @pl.when(k == 0)
        def _():
            acc_ref[...] = jnp.zeros_like(acc_ref)

        acc_ref[...] += lax.dot_general(a_ref[...].astype(bf16), b_ref[...].astype(bf16), dims,
                                        preferred_element_type=f32)

        @pl.when(k == nk - 1)
        def _():
            o_ref[...] = acc_ref[...].astype(out_dtype)

    if mode == "nn":
        a_spec = pl.BlockSpec((bm, bk), lambda i, j, k: (i, k))
        b_spec = pl.BlockSpec((bk, bn), lambda i, j, k: (k, j))
    elif mode == "tn":
        a_spec = pl.BlockSpec((bk, bm), lambda i, j, k: (k, i))
        b_spec = pl.BlockSpec((bk, bn), lambda i, j, k: (k, j))
    else:
        a_spec = pl.BlockSpec((bm, bk), lambda i, j, k: (i, k))
        b_spec = pl.BlockSpec((bn, bk), lambda i, j, k: (j, k))
    return pl.pallas_call(
        body, grid=(M // bm, Nn // bn, nk), in_specs=[a_spec, b_spec],
        out_specs=pl.BlockSpec((bm, bn), lambda i, j, k: (i, j)),
        out_shape=S((M, Nn), out_dtype), scratch_shapes=[pltpu.VMEM((bm, bn), f32)],
        compiler_params=_cparams("parallel", "parallel", "arbitrary"), name=name,
    )(a, b)


SMM_DW_ROWS = (512, 256)
SMM_ROWS = (512, 256)


def _shard_pieces(seg_widths, n):
    bounds = [0]
    for sw in seg_widths:
        bounds.append(bounds[-1] + sw)
    assert bounds[-1] == N_DEV * n, (seg_widths, n)
    out = []
    for j in range(N_DEV):
        lo, hi = j * n, (j + 1) * n
        pcs = []
        for si in range(len(seg_widths)):
            a, b = max(lo, bounds[si]), min(hi, bounds[si + 1])
            if a < b:
                pcs.append((si, a - bounds[si], a - lo, b - a))
        out.append(pcs)
    return out


def _w_spec(w, layer):
    if layer is None:
        return pl.BlockSpec(w.shape, lambda *idx: (0, 0, 0))
    return pl.BlockSpec((N_DEV, None) + w.shape[2:], lambda *idx: (0, layer, 0, 0))


def smm_fwd(a, w, layer, seg_widths, name, hosted=None):
    M, K = a.shape
    n = w.shape[-1]
    pieces = _shard_pieces(seg_widths, n)
    padded = [sw + (-sw) % 128 for sw in seg_widths]
    bm = _pick(M, SMM_ROWS)

    def body(a_ref, w_ref, *o_refs):
        av = a_ref[...]
        for si, sw in enumerate(seg_widths):
            if padded[si] != sw:
                o_refs[si][:, pl.ds(padded[si] - 128, 128)] = jnp.zeros((bm, 128), f32)
        for j in range(N_DEV):
            for si, soff, woff, wd in pieces[j]:
                o_refs[si][:, pl.ds(soff, wd)] = jnp.dot(av, w_ref[j, :, pl.ds(woff, wd)], preferred_element_type=f32)

    outs, extra = _host_call(
        body, (M // bm,), [pl.BlockSpec((bm, K), lambda i: (i, 0)), _w_spec(w, layer)],
        [pl.BlockSpec((bm, pw), lambda i: (i, 0)) for pw in padded], [S((M, pw), f32) for pw in padded], [],
        ("parallel",), name, (a, w), hosted)
    return outs if hosted is None else (outs, extra)


def smm_dx(d_segs, w, layer, seg_widths, out_dtype, name, hosted=None):
    M = d_segs[0].shape[0]
    K, n = w.shape[-2], w.shape[-1]
    pieces = _shard_pieces(seg_widths, n)
    ns = len(d_segs)
    bm = _pick(M, SMM_ROWS)

    def body(*refs):
        d_refs, w_ref, o_ref = refs[:ns], refs[ns], refs[ns + 1]
        acc = jnp.zeros((bm, K), f32)
        for j in range(N_DEV):
            for si, soff, woff, wd in pieces[j]:
                acc = acc + lax.dot_general(d_refs[si][:, pl.ds(soff, wd)], w_ref[j, :, pl.ds(woff, wd)],
                                            (((1,), (1,)), ((), ())), preferred_element_type=f32)
        o_ref[...] = acc.astype(out_dtype)

    (out,), extra = _host_call(
        body, (M // bm,),
        [pl.BlockSpec((bm, d.shape[1]), lambda i: (i, 0)) for d in d_segs] + [_w_spec(w, layer)],
        [pl.BlockSpec((bm, K), lambda i: (i, 0))], [S((M, K), out_dtype)], [], ("parallel",), name,
        (*d_segs, w), hosted)
    return out, extra


def smm_dw(a, d_segs, n, seg_widths, ngrp, transposed, name):
    M, K = a.shape
    pieces = _shard_pieces(seg_widths, n)
    per = N_DEV // ngrp
    bm = _pick(M, SMM_DW_ROWS)
    nI = M // bm
    ns = len(d_segs)
    shard = (n, K) if transposed else (K, n)

    def body(*refs):
        a_ref, d_refs, o_ref, acc_ref = refs[0], refs[1:1 + ns], refs[1 + ns], refs[2 + ns]
        grp = pl.program_id(0)
        i = pl.program_id(1)

        @pl.when(i == 0)
        def _():
            acc_ref[...] = jnp.zeros_like(acc_ref)

        av = a_ref[...]
        for gs in range(ngrp):
            def one_group(gs=gs):
                for jj in range(per):
                    for si, soff, woff, wd in pieces[gs * per + jj]:
                        dv = d_refs[si][:, pl.ds(soff, wd)]
                        if transposed:
                            acc_ref[jj, pl.ds(woff, wd), :] += lax.dot_general(
                                dv, av, (((0,), (0,)), ((), ())), preferred_element_type=f32)
                        else:
                            acc_ref[jj, :, pl.ds(woff, wd)] += lax.dot_general(
                                av, dv, (((0,), (0,)), ((), ())), preferred_element_type=f32)
            pl.when(grp == gs)(one_group)

        @pl.when(i == nI - 1)
        def _():
            o_ref[...] = acc_ref[...].astype(bf16)

    return pl.pallas_call(
        body, grid=(ngrp, nI),
        in_specs=[pl.BlockSpec((bm, K), lambda g, i: (i, 0))]
        + [pl.BlockSpec((bm, d.shape[1]), lambda g, i: (i, 0)) for d in d_segs],
        out_specs=pl.BlockSpec((per,) + shard, lambda g, i: (g, 0, 0)), out_shape=S((N_DEV,) + shard, bf16),
        scratch_shapes=[pltpu.VMEM((per,) + shard, f32)],
        compiler_params=_cparams("arbitrary", "arbitrary"), name=name)(a, *d_segs)


def _modnorm_f(h, w, sc, sh):
    y = h * lax.rsqrt(jnp.mean(h * h, axis=-1, keepdims=True) + EPS)
    return (y * w) * (1.0 + sc) + sh


def _kind_specs(nctxb):
    if nctxb > 0:
        return pl.BlockSpec((None, 1, D), lambda i: (jnp.where(i < nctxb, 0, 1), 0, 0))
    return pl.BlockSpec((None, 1, D), lambda i: (0, 0, 0))


def _two_part_specs(nctxb):
    return (pl.BlockSpec((TB, D), lambda i: (jnp.minimum(i, nctxb - 1), 0)),
            pl.BlockSpec((TB, D), lambda i: (jnp.maximum(i - nctxb, 0), 0)))


def modnorm_fwd(h, w, sc, sh, nctxb, name, ctx=None):
    if ctx is None:
        T = h.shape[0]

        def body(h_ref, w_ref, sc_ref, sh_ref, o_ref):
            o_ref[...] = _modnorm_f(h_ref[...], w_ref[...], sc_ref[...], sh_ref[...]).astype(bf16)

        hspecs, hargs = [pl.BlockSpec((TB, D), lambda i: (i, 0))], (h,)
    else:
        T = h.shape[0] + ctx.shape[0]

        def body(c_ref, h_ref, w_ref, sc_ref, sh_ref, o_ref):
            hv = jnp.where(pl.program_id(0) < nctxb, c_ref[...], h_ref[...])
            o_ref[...] = _modnorm_f(hv, w_ref[...], sc_ref[...], sh_ref[...]).astype(bf16)

        hspecs, hargs = list(_two_part_specs(nctxb)), (ctx, h)
    row = pl.BlockSpec((1, D), lambda i: (0, 0))
    ks = _kind_specs(nctxb)
    return pl.pallas_call(body, grid=(T // TB,), in_specs=hspecs + [row, ks, ks],
                          out_specs=pl.BlockSpec((TB, D), lambda i: (i, 0)), out_shape=S((T, D), bf16),
                          compiler_params=_cparams("parallel"), name=name)(*hargs, w, sc, sh)


def modnorm_bwd(h, w, sc, sh, da, dres, nctxb, name, ctx=None):
    T = h.shape[0] + (0 if ctx is None else ctx.shape[0])
    kinds = sc.shape[0]
    nh = 1 if ctx is None else 2

    def body(*refs):
        w_ref, sc_ref, sh_ref, da_ref, dres_ref, dh_ref, dw_ref, dsc_ref, dsh_ref = refs[nh:]
        i = pl.program_id(0)
        hv = refs[0][...] if ctx is None else jnp.where(i < nctxb, refs[0][...], refs[1][...])
        _, vjp = jax.vjp(_modnorm_f, hv, w_ref[...], sc_ref[...], sh_ref[...])
        dh, dw, dsc, dsh = vjp(da_ref[...].astype(f32))
        dh_ref[...] = dres_ref[...] + dh

        @pl.when(i == 0)
        def _():
            dw_ref[...] = jnp.zeros_like(dw_ref)

        @pl.when((i == 0) | (i == nctxb))
        def _():
            dsc_ref[...] = jnp.zeros_like(dsc_ref)
            dsh_ref[...] = jnp.zeros_like(dsh_ref)

        dw_ref[...] += dw
        dsc_ref[...] += dsc
        dsh_ref[...] += dsh

    blk = pl.BlockSpec((TB, D), lambda i: (i, 0))
    lat = pl.BlockSpec((TB, D), lambda i: (jnp.maximum(i - nctxb, 0), 0))
    row = pl.BlockSpec((1, D), lambda i: (0, 0))
    ks = _kind_specs(nctxb)
    hspecs, hargs = ([blk], (h,)) if ctx is None else (list(_two_part_specs(nctxb)), (ctx, h))
    return pl.pallas_call(
        body, grid=(T // TB,), in_specs=hspecs + [row, ks, ks, blk, lat], out_specs=[lat, row, ks, ks],
        out_shape=[S((T - nctxb * TB, D), f32), S((1, D), f32), S((kinds, 1, D), f32), S((kinds, 1, D), f32)],
        compiler_params=_cparams("arbitrary"), name=name)(*hargs, w, sc, sh, da, dres)


def resnorm_fwd(h, o, g, b, w, sc, sh, name):
    T = h.shape[0]

    def body(h_ref, o_ref, g_ref, b_ref, w_ref, sc_ref, sh_ref, hn_ref, a_ref):
        hn = h_ref[...] + g_ref[...] * (o_ref[...] + b_ref[...])
        hn_ref[...] = hn
        a_ref[...] = _modnorm_f(hn, w_ref[...], sc_ref[...], sh_ref[...]).astype(bf16)

    blk = pl.BlockSpec((LTB, D), lambda i: (i, 0))
    row = pl.BlockSpec((1, D), lambda i: (0, 0))
    return pl.pallas_call(body, grid=(T // LTB,), in_specs=[blk, blk, row, row, row, row, row], out_specs=[blk, blk],
                          out_shape=[S((T, D), f32), S((T, D), bf16)], compiler_params=_cparams("parallel"),
                          name=name)(h, o, g, b, w, sc, sh)


def normres_bwd(h, w, sc, sh, da, dres, o, g, b, name):
    T = h.shape[0]

    def body(h_ref, w_ref, sc_ref, sh_ref, da_ref, dres_ref, o_ref, g_ref, b_ref,
             dh_ref, dw_ref, dsc_ref, dsh_ref, do_ref, dg_ref, db_ref):
        _, vjp = jax.vjp(_modnorm_f, h_ref[...], w_ref[...], sc_ref[...], sh_ref[...])
        dhn, dw, dsc, dsh = vjp(da_ref[...].astype(f32))
        dh = dres_ref[...] + dhn
        dh_ref[...] = dh
        do = g_ref[...] * dh
        do_ref[...] = do.astype(bf16)
        sums = (dw, dsc, dsh, jnp.sum(dh * (o_ref[...] + b_ref[...]), axis=0, keepdims=True),
                jnp.sum(do, axis=0, keepdims=True))

        @pl.when(pl.program_id(0) == 0)
        def _():
            for r_ in (dw_ref, dsc_ref, dsh_ref, dg_ref, db_ref):
                r_[...] = jnp.zeros_like(r_)

        for r_, s_ in zip((dw_ref, dsc_ref, dsh_ref, dg_ref, db_ref), sums):
            r_[...] += s_

    blk = pl.BlockSpec((LTB, D), lambda i: (i, 0))
    row = pl.BlockSpec((1, D), lambda i: (0, 0))
    return pl.pallas_call(
        body, grid=(T // LTB,), in_specs=[blk, row, row, row, blk, blk, blk, row, row],
        out_specs=[blk, row, row, row, blk, row, row],
        out_shape=[S((T, D), f32), S((1, D), f32), S((1, D), f32), S((1, D), f32), S((T, D), bf16), S((1, D), f32),
                   S((1, D), f32)],
        compiler_params=_cparams("arbitrary"), name=name)(h, w, sc, sh, da, dres, o, g, b)


def final_loss(h, o, g, w, tgt, name):
    T = h.shape[0]

    def f(hv, wv, tv):
        y = (hv * lax.rsqrt(jnp.mean(hv * hv, axis=-1, keepdims=True) + EPS)) * wv
        e = y - tv
        return 0.5 * jnp.sum(jnp.sum(e * e, axis=-1, keepdims=True), axis=0, keepdims=True) * (1.0 / D)

    def body(h_ref, o_ref, g_ref, w_ref, t_ref, loss_ref, dh_ref, dw_ref, do_ref, dg_ref):
        i = pl.program_id(0)
        tv = t_ref[...]
        ov = o_ref[...]
        gv = g_ref[...]
        val, vjp = jax.vjp(lambda a, b_: f(a, b_, tv), h_ref[...] + gv * ov, w_ref[...])
        dh, dw = vjp(jnp.ones((1, 1), f32))
        dh_ref[...] = dh
        do_ref[...] = (gv * dh).astype(bf16)

        @pl.when(i == 0)
        def _():
            loss_ref[...] = jnp.zeros_like(loss_ref)
            dw_ref[...] = jnp.zeros_like(dw_ref)
            dg_ref[...] = jnp.zeros_like(dg_ref)

        loss_ref[...] += jnp.broadcast_to(val, (1, 128))
        dw_ref[...] += dw
        dg_ref[...] += jnp.sum(dh * ov, axis=0, keepdims=True)

    blk = pl.BlockSpec((LTB, D), lambda i: (i, 0))
    row = pl.BlockSpec((1, D), lambda i: (0, 0))
    return pl.pallas_call(body, grid=(T // LTB,), in_specs=[blk, blk, row, row, blk],
                          out_specs=[pl.BlockSpec((1, 128), lambda i: (0, 0)), blk, row, blk, row],
                          out_shape=[S((1, 128), f32), S((T, D), f32), S((1, D), f32), S((T, D), bf16), S((1, D), f32)],
                          compiler_params=_cparams("arbitrary"), name=name)(h, o, g, w, tgt)


CB = 256
RT = 32
RTB = 16


def _fold8(t):
    acc = t[0:8]
    for k in range(1, t.shape[0] // 8):
        acc = acc + t[8 * k:8 * (k + 1)]
    return acc


def _rows(start, off=0, rt=RT):
    return pl.ds(pl.multiple_of(start + off, 8), rt)


def _rowsb(start, off=0):
    return _rows(start, off, RTB)


def _zero_rows(ref, start, n):
    ref[pl.ds(start, n), :] = jnp.zeros((n, ref.shape[1]), f32)


K5, HALF5, PAD5 = 5, 2, 8


def _taps5(base_ref, r, rt, sign):
    n = rt + 2 * PAD5
    v = base_ref[pl.ds(pl.multiple_of(r, 8), n), :]
    taps = []
    for k in range(K5):
        o = sign * (k - HALF5)
        rolled = v if o == 0 else pltpu.roll(v, (-o) % n, axis=0)
        taps.append(rolled[PAD5:PAD5 + rt])
    return taps


def ssd_conv_fwd(u, w, b, segs, name, hosted=None):
    T = u.shape[0]
    maxlen = max(ln for _, ln in segs)

    def body(u_ref, w_ref, b_ref, o_ref, ds_ref, base_ref):
        wv = [w_ref[pl.ds(k, 1), :] for k in range(K5)]
        bv = b_ref[...]
        for s0, ln in segs:
            _zero_rows(base_ref, 0, PAD5)
            _zero_rows(base_ref, PAD5 + ln, PAD5)
            base_ref[pl.ds(PAD5, ln), :] = u_ref[pl.ds(s0, ln), :]

            def tile(i, carry):
                r = i * RT
                taps = _taps5(base_ref, r, RT, 1)
                acc = jnp.broadcast_to(bv, (RT, CB))
                for k in range(K5):
                    acc = acc + taps[k] * wv[k]
                sg = _sigmoid(acc)
                o_ref[_rows(r, s0), :] = acc * sg
                ds_ref[_rows(r, s0), :] = sg * (1.0 + acc * (1.0 - sg))
                return carry

            lax.fori_loop(0, ln // RT, tile, 0, unroll=2)

    cblk = pl.BlockSpec((T, CB), lambda j: (0, j))
    (out, dsilu), extra = _host_call(
        body, (CONVD // CB,),
        [cblk, pl.BlockSpec((K5, CB), lambda j: (0, j)), pl.BlockSpec((1, CB), lambda j: (0, j))],
        [cblk, cblk], [S((T, CONVD), f32), S((T, CONVD), f32)],
        [pltpu.VMEM((maxlen + 2 * PAD5, CB), f32)], ("parallel",), name, (u, w, b), hosted)
    return out, dsilu, extra


def ssd_conv_bwd(proj, w, dsilu, dy2, dyskip, dexp, segs, name):
    T = proj.shape[0]
    maxlen = max(ln for _, ln in segs)
    nskip = DI // CB

    def body(u_ref, w_ref, ds_ref, dya_ref, dyb_ref, dsk_ref, dexp_ref, du_ref, dw_ref, db_ref, base_ref, dbase_ref):
        wv = [w_ref[pl.ds(k, 1), :] for k in range(K5)]
        has_skip = (pl.program_id(0) < nskip).astype(f32) * dexp_ref[...]
        acc8 = tuple(jnp.zeros((8, CB), f32) for _ in range(K5 + 1))
        for s0, ln in segs:
            for ref in (base_ref, dbase_ref):
                _zero_rows(ref, 0, PAD5)
                _zero_rows(ref, PAD5 + ln, PAD5)
            base_ref[pl.ds(PAD5, ln), :] = u_ref[pl.ds(s0, ln), :]

            def tile1(i, carry):
                r = i * RTB
                dy = dya_ref[_rowsb(r, s0), :] + dyb_ref[_rowsb(r, s0), :] + has_skip * dsk_ref[_rowsb(r, s0), :]
                dpre = dy * ds_ref[_rowsb(r, s0), :]
                dbase_ref[_rowsb(r, PAD5), :] = dpre
                taps = _taps5(base_ref, r, RTB, 1)
                new = [carry[k] + _fold8(dpre * taps[k]) for k in range(K5)]
                new.append(carry[K5] + _fold8(dpre))
                return tuple(new)

            acc8 = lax.fori_loop(0, ln // RTB, tile1, acc8, unroll=2)

            def tile2(i, carry):
                r = i * RTB
                taps = _taps5(dbase_ref, r, RTB, -1)
                du = jnp.zeros((RTB, CB), f32)
                for k in range(K5):
                    du = du + taps[k] * wv[k]
                du_ref[_rowsb(r, s0), :] = du.astype(bf16)
                return carry

            lax.fori_loop(0, ln // RTB, tile2, 0, unroll=4)
        for k in range(K5):
            dw_ref[pl.ds(k, 1), :] = jnp.sum(acc8[k], axis=0, keepdims=True)
        db_ref[...] = jnp.sum(acc8[K5], axis=0, keepdims=True)

    cblk = pl.BlockSpec((T, CB), lambda j: (0, j))
    return pl.pallas_call(
        body, grid=(CONVD // CB,),
        in_specs=[cblk, pl.BlockSpec((K5, CB), lambda j: (0, j)), cblk,
                  pl.BlockSpec((None, T, CB), lambda j: (0, 0, j)), pl.BlockSpec((None, T, CB), lambda j: (1, 0, j)),
                  pl.BlockSpec((T, CB), lambda j: (0, jnp.minimum(j, nskip - 1))),
                  pl.BlockSpec((1, CB), lambda j: (0, jnp.minimum(j, nskip - 1)))],
        out_specs=[cblk, pl.BlockSpec((K5, CB), lambda j: (0, j)), pl.BlockSpec((1, CB), lambda j: (0, j))],
        out_shape=[S((T, CONVD), bf16), S((K5, CONVD), f32), S((1, CONVD), f32)],
        scratch_shapes=[pltpu.VMEM((maxlen + 2 * PAD5, CB), f32), pltpu.VMEM((maxlen + 2 * PAD5, CB), f32)],
        compiler_params=_cparams("parallel"), name=name)(proj, w, dsilu, dy2, dy2, dyskip, dexp)


GPAD = GRID_W


def _grid_copies(g_ref, src, L):
    col = lax.broadcasted_iota(jnp.int32, (L, CB), 0) & (GRID_W - 1)
    for d in range(3):
        _zero_rows(g_ref.at[d], 0, GPAD)
        _zero_rows(g_ref.at[d], GPAD + L, GPAD)
    g_ref[1, pl.ds(GPAD, L), :] = src
    g_ref[0, pl.ds(GPAD, L), :] = jnp.where(col != 0, g_ref[1, pl.ds(GPAD - 1, L), :], 0.0)
    g_ref[2, pl.ds(GPAD, L), :] = jnp.where(col != GRID_W - 1, g_ref[1, pl.ds(GPAD + 1, L), :], 0.0)


def ffn_gate_fwd(val, gate, cw, cb_, name, hosted=None):
    L = val.shape[0]
    nb = FH // CB

    def body(val_ref, gate_ref, w_ref, b_ref, o_ref, s_ref, vds_ref, g_ref):
        wv = [w_ref[pl.ds(t, 1), :] for t in range(9)]
        bv = b_ref[...]
        _grid_copies(g_ref, gate_ref[...], L)

        def tile(i, carry):
            r = i * RT
            acc = jnp.broadcast_to(bv, (RT, CB))
            for dr in range(3):
                for dc in range(3):
                    acc = acc + g_ref[dc, _rows(r, GPAD + (dr - 1) * GRID_W), :] * wv[3 * dr + dc]
            sg = _sigmoid(acc)
            s = acc * sg
            v = val_ref[_rows(r), :]
            o_ref[_rows(r), :] = (s * v).astype(bf16)
            s_ref[_rows(r), :] = s
            vds_ref[_rows(r), :] = v * (sg * (1.0 + acc * (1.0 - sg)))
            return carry

        lax.fori_loop(0, L // RT, tile, 0, unroll=2)

    cblk = pl.BlockSpec((L, CB), lambda j: (0, j))
    (out, s_, vds), extra = _host_call(
        body, (nb,), [cblk, cblk, pl.BlockSpec((9, CB), lambda j: (0, j)), pl.BlockSpec((1, CB), lambda j: (0, j))],
        [cblk, cblk, cblk], [S((L, FH), bf16), S((L, FH), f32), S((L, FH), f32)],
        [pltpu.VMEM((3, L + 2 * GPAD, CB), f32)], ("parallel",), name, (val, gate, cw, cb_), hosted)
    return out, s_, vds, extra


def ffn_gate_bwd(gate, s_, vds, cw, dact, name):
    L = gate.shape[0]
    nb = FH // CB

    def body(gate_ref, s_ref, vds_ref, w_ref, da_ref, dval_ref, dgate_ref, dw_ref, db_ref, g_ref, d_ref):
        wv = [w_ref[pl.ds(t, 1), :] for t in range(9)]
        _grid_copies(g_ref, gate_ref[...], L)

        def tile1(i, carry):
            r = i * RTB
            da = da_ref[_rowsb(r), :].astype(f32)
            dval_ref[_rowsb(r), :] = (da * s_ref[_rowsb(r), :]).astype(bf16)
            dpre = da * vds_ref[_rowsb(r), :]
            d_ref[_rowsb(r), :] = dpre
            new = [carry[t] + _fold8(dpre * g_ref[t % 3, _rowsb(r, GPAD + (t // 3 - 1) * GRID_W), :]) for t in range(9)]
            new.append(carry[9] + _fold8(dpre))
            return tuple(new)

        acc8 = lax.fori_loop(0, L // RTB, tile1, tuple(jnp.zeros((8, CB), f32) for _ in range(10)), unroll=2)
        for t in range(9):
            dw_ref[pl.ds(t, 1), :] = jnp.sum(acc8[t], axis=0, keepdims=True)
        db_ref[...] = jnp.sum(acc8[9], axis=0, keepdims=True)
        _grid_copies(g_ref, d_ref[...], L)

        def tile2(i, carry):
            r = i * RTB
            dg = jnp.zeros((RTB, CB), f32)
            for dr in range(3):
                for dc in range(3):
                    dg = dg + g_ref[2 - dc, _rowsb(r, GPAD - (dr - 1) * GRID_W), :] * wv[3 * dr + dc]
            dgate_ref[_rowsb(r), :] = dg.astype(bf16)
            return carry

        lax.fori_loop(0, L // RTB, tile2, 0, unroll=4)

    cblk = pl.BlockSpec((L, CB), lambda j: (0, j))
    return pl.pallas_call(
        body, grid=(nb,),
        in_specs=[cblk, cblk, cblk, pl.BlockSpec((9, CB), lambda j: (0, j)), cblk],
        out_specs=[cblk, cblk, pl.BlockSpec((9, CB), lambda j: (0, j)), pl.BlockSpec((1, CB), lambda j: (0, j))],
        out_shape=[S((L, FH), bf16), S((L, FH), bf16), S((9, FH), f32), S((1, FH), f32)],
        scratch_shapes=[pltpu.VMEM((3, L + 2 * GPAD, CB), f32), pltpu.VMEM((L, CB), f32)],
        compiler_params=_cparams("parallel"), name=name)(gate, s_, vds, cw, dact)


CONF_K = 31
CHALF = CONF_K // 2
CPAD = 16


def _shift_copies8(c_ref, base_ref, L):
    n = L + 2 * CPAD - 8
    for b_ in range(8):
        c_ref[b_, pl.ds(0, n), :] = base_ref[pl.ds(b_, n), :]


def _tap_ab(o):
    return o % 8, o - o % 8


def conf_glu_conv_fwd(pa, pg, b1, wdw, bdw, name, hosted=None):
    L = pa.shape[0]
    nb = D // CB

    def body(pa_ref, pg_ref, ba_ref, bg_ref, w_ref, bdw_ref, o_ref, base_ref, c_ref):
        _zero_rows(base_ref, 0, CPAD)
        _zero_rows(base_ref, CPAD + L, CPAD)
        base_ref[pl.ds(CPAD, L), :] = (pa_ref[...] + ba_ref[...]) * _sigmoid(pg_ref[...] + bg_ref[...])
        _shift_copies8(c_ref, base_ref, L)
        bv = bdw_ref[...]

        def tile(i, carry):
            r = i * RT
            acc = jnp.broadcast_to(bv, (RT, CB))
            for k in range(CONF_K):
                b_, a8 = _tap_ab(k - CHALF)
                acc = acc + c_ref[b_, _rows(r, CPAD + a8), :] * w_ref[pl.ds(k, 1), :]
            o_ref[_rows(r), :] = acc
            return carry

        lax.fori_loop(0, L // RT, tile, 0, unroll=2)

    cblk = pl.BlockSpec((L, CB), lambda j: (0, j))
    rblk = pl.BlockSpec((1, CB), lambda j: (0, j))
    rgblk = pl.BlockSpec((1, CB), lambda j: (0, nb + j))
    (out,), extra = _host_call(
        body, (nb,), [cblk, cblk, rblk, rgblk, pl.BlockSpec((CONF_K, CB), lambda j: (0, j)), rblk],
        [cblk], [S((L, D), f32)], [pltpu.VMEM((L + 2 * CPAD, CB), f32), pltpu.VMEM((8, L + 2 * CPAD, CB), f32)],
        ("parallel",), name, (pa, pg, b1, b1, wdw, bdw), hosted)
    return out, extra


def conf_glu_conv_bwd(pa, pg, b1, wdw, dy, name):
    L = pa.shape[0]
    nb = D // CB

    def body(pa_ref, pg_ref, ba_ref, bg_ref, w_ref, dy_ref, dpa_ref, dpg_ref, dba_ref, dbg_ref, dw_ref, dbdw_ref,
             base_ref, c_ref, acc_ref):
        _zero_rows(base_ref, 0, CPAD)
        _zero_rows(base_ref, CPAD + L, CPAD)
        base_ref[pl.ds(CPAD, L), :] = (pa_ref[...] + ba_ref[...]) * _sigmoid(pg_ref[...] + bg_ref[...])
        _shift_copies8(c_ref, base_ref, L)
        acc_ref[...] = jnp.zeros_like(acc_ref)

        def tile1(i, carry):
            r = i * RTB
            dyt = dy_ref[_rowsb(r), :]
            for k in range(CONF_K):
                b_, a8 = _tap_ab(k - CHALF)
                acc_ref[k] += _fold8(dyt * c_ref[b_, _rowsb(r, CPAD + a8), :])
            return carry + _fold8(dyt)

        db8 = lax.fori_loop(0, L // RTB, tile1, jnp.zeros((8, CB), f32), unroll=2)
        dbdw_ref[...] = jnp.sum(db8, axis=0, keepdims=True)
        for k in range(CONF_K):
            dw_ref[pl.ds(k, 1), :] = jnp.sum(acc_ref[k], axis=0, keepdims=True)
        base_ref[pl.ds(CPAD, L), :] = dy_ref[...]
        _shift_copies8(c_ref, base_ref, L)
        ba = ba_ref[...]
        bg = bg_ref[...]

        def tile2(i, carry):
            r = i * RTB
            dglu = jnp.zeros((RTB, CB), f32)
            for k in range(CONF_K):
                b_, a8 = _tap_ab(CHALF - k)
                dglu = dglu + c_ref[b_, _rowsb(r, CPAD + a8), :] * w_ref[pl.ds(k, 1), :]
            a = pa_ref[_rowsb(r), :] + ba
            sg = _sigmoid(pg_ref[_rowsb(r), :] + bg)
            dpa = dglu * sg
            dpg = dglu * a * (sg * (1.0 - sg))
            dpa_ref[_rowsb(r), :] = dpa.astype(bf16)
            dpg_ref[_rowsb(r), :] = dpg.astype(bf16)
            return carry[0] + _fold8(dpa), carry[1] + _fold8(dpg)

        s8 = lax.fori_loop(0, L // RTB, tile2, (jnp.zeros((8, CB), f32), jnp.zeros((8, CB), f32)), unroll=2)
        dba_ref[...] = jnp.sum(s8[0], axis=0, keepdims=True)
        dbg_ref[...] = jnp.sum(s8[1], axis=0, keepdims=True)

    cblk = pl.BlockSpec((L, CB), lambda j: (0, j))
    rblk = pl.BlockSpec((1, CB), lambda j: (0, j))
    rgblk = pl.BlockSpec((1, CB), lambda j: (0, nb + j))
    wblk = pl.BlockSpec((CONF_K, CB), lambda j: (0, j))
    return pl.pallas_call(
        body, grid=(nb,), in_specs=[cblk, cblk, rblk, rgblk, wblk, cblk],
        out_specs=[cblk, cblk, rblk, rblk, wblk, rblk],
        out_shape=[S((L, D), bf16), S((L, D), bf16), S((1, D), f32), S((1, D), f32), S((CONF_K, D), f32), S((1, D), f32)],
        scratch_shapes=[pltpu.VMEM((L + 2 * CPAD, CB), f32), pltpu.VMEM((8, L + 2 * CPAD, CB), f32),
                        pltpu.VMEM((CONF_K, 8, CB), f32)],
        compiler_params=_cparams("parallel"), name=name)(pa, pg, b1, b1, wdw, dy)


def _ln_silu_f(x, w, b):
    mu = jnp.mean(x, axis=-1, keepdims=True)
    d = x - mu
    y = d * lax.rsqrt(jnp.mean(d * d, axis=-1, keepdims=True) + EPS) * w + b
    return y * _sigmoid(y)


def ln_silu_fwd(x, w, b, name):
    T = x.shape[0]

    def body(x_ref, w_ref, b_ref, o_ref):
        o_ref[...] = _ln_silu_f(x_ref[...], w_ref[...], b_ref[...]).astype(bf16)

    blk = pl.BlockSpec((TB, D), lambda i: (i, 0))
    row = pl.BlockSpec((1, D), lambda i: (0, 0))
    return pl.pallas_call(body, grid=(T // TB,), in_specs=[blk, row, row], out_specs=blk, out_shape=S((T, D), bf16),
                          compiler_params=_cparams("parallel"), name=name)(x, w, b)


def ln_silu_bwd(x, w, b, ds, name):
    T = x.shape[0]

    def body(x_ref, w_ref, b_ref, ds_ref, dx_ref, dw_ref, db_ref):
        i = pl.program_id(0)
        _, vjp = jax.vjp(_ln_silu_f, x_ref[...], w_ref[...], b_ref[...])
        dx, dw, db = vjp(ds_ref[...].astype(f32))
        dx_ref[...] = dx

        @pl.when(i == 0)
        def _():
            dw_ref[...] = jnp.zeros_like(dw_ref)
            db_ref[...] = jnp.zeros_like(db_ref)

        dw_ref[...] += dw
        db_ref[...] += db

    blk = pl.BlockSpec((TB, D), lambda i: (i, 0))
    row = pl.BlockSpec((1, D), lambda i: (0, 0))
    return pl.pallas_call(body, grid=(T // TB,), in_specs=[blk, row, row, blk], out_specs=[blk, row, row],
                          out_shape=[S((T, D), f32), S((1, D), f32), S((1, D), f32)],
                          compiler_params=_cparams("arbitrary"), name=name)(x, w, b, ds)


def _mxu(a, b, dims):
    return lax.dot_general(a.astype(bf16), b.astype(bf16), (dims, ((), ())), preferred_element_type=f32)


def _nn(a, b):
    return _mxu(a, b, ((1,), (0,)))


def _nt(a, b):
    return _mxu(a, b, ((1,), (1,)))


def _tn(a, b):
    return _mxu(a, b, ((0,), (0,)))


@jax.custom_vjp
def _dot_nn(a, b):
    return _nn(a, b)


@jax.custom_vjp
def _dot_nt(a, b):
    return _nt(a, b)


@jax.custom_vjp
def _dot_tn(a, b):
    return _tn(a, b)


_dot_nn.defvjp(lambda a, b: (_nn(a, b), (a, b)), lambda res, g: (_nt(g, res[1]), _tn(res[0], g)))
_dot_nt.defvjp(lambda a, b: (_nt(a, b), (a, b)), lambda res, g: (_nn(g, res[1]), _tn(g, res[0])))
_dot_tn.defvjp(lambda a, b: (_tn(a, b), (a, b)), lambda res, g: (_nt(res[1], g), _nn(res[0], g)))


def _exact_dot(a, b, dims, split_first):
    v = a if split_first else b
    p1 = v.astype(bf16)
    r1 = v - p1.astype(f32)
    p2 = r1.astype(bf16)
    p3 = (r1 - p2.astype(f32)).astype(bf16)
    out = None
    for p in (p1, p2, p3):
        lhs, rhs = (p, b.astype(bf16)) if split_first else (a.astype(bf16), p)
        t = lax.dot_general(lhs, rhs, (dims, ((), ())), preferred_element_type=f32)
        out = t if out is None else out + t
    return out


@jax.custom_vjp
def _masked_sum_cols(mf, a):
    return _exact_dot(mf, a, ((1,), (0,)), False)


@jax.custom_vjp
def _masked_sum_rows(mf, a):
    return _exact_dot(a, mf, ((1,), (1,)), True)


_masked_sum_cols.defvjp(lambda mf, a: (_exact_dot(mf, a, ((1,), (0,)), False), mf),
                        lambda mf, g: (jnp.zeros_like(mf), _exact_dot(mf, g, ((0,), (0,)), False)))
_masked_sum_rows.defvjp(lambda mf, a: (_exact_dot(a, mf, ((1,), (1,)), True), mf),
                        lambda mf, g: (jnp.zeros_like(mf), _exact_dot(g, mf, ((1,), (0,)), True)))


def _masked_sum(mf, a, rows):
    return _masked_sum_rows(mf, a) if rows else _masked_sum_cols(mf, a)


def _lanes_to_rows(v):
    r = lax.broadcasted_iota(jnp.int32, (GW, GW), 0)
    c = lax.broadcasted_iota(jnp.int32, (GW, GW), 1)
    return jnp.sum(jnp.where(r == c, jnp.broadcast_to(v, (GW, GW)), 0.0), axis=1, keepdims=True)


def _ssd_chunk(x, B, C, dtc, dtr, bc, br, alc, alr, s_in, is_fwd):
    row = lax.broadcasted_iota(jnp.int32, (Q, Q), 0)
    col = lax.broadcasted_iota(jnp.int32, (Q, Q), 1)
    sgn = jnp.where(is_fwd, 1, -1).astype(jnp.int32)
    mask = (row - col) * sgn >= 0
    mf = mask.astype(f32)
    lane_head = lax.broadcasted_iota(jnp.int32, (1, GW), 1) // P

    def spread(v):
        out = jnp.zeros((v.shape[0], GW), f32)
        for r in range(HPG):
            out = jnp.where(lane_head == r, v[:, r:r + 1], out)
        return out

    dt_c = _softplus(dtc + bc)
    dt_r = _softplus(dtr + br)
    a_c = dt_c * (-jnp.exp(alc))
    a_r = dt_r * (-jnp.exp(alr))
    acum_c = _masked_sum(mf, a_c, False)
    acum_r = _masked_sum(mf, a_r, True)
    tot_c = jnp.sum(a_c, axis=0, keepdims=True)
    dt_e = spread(dt_c)
    acum_e = spread(acum_c)
    tot_e = spread(tot_c)
    xdt = x * dt_e
    cb = _dot_nt(C, B)
    scores, xs = [], []
    for r in range(HPG):
        seg = acum_c[:, r:r + 1] - acum_r[r:r + 1, :]
        scores.append(cb * jnp.exp(jnp.where(mask, seg, -jnp.inf)))
        xs.append(jnp.where(lane_head == r, xdt, 0.0))
    y = _dot_nn(jnp.concatenate(scores, axis=1), jnp.concatenate(xs, axis=0))
    y = y + _dot_nt(C, s_in) * jnp.exp(acum_e)
    xe = xdt * jnp.exp(tot_e - acum_e)
    s_out = _lanes_to_rows(jnp.exp(tot_e)) * s_in + _dot_tn(xe, B)
    return y, s_out


def _chunk_index(d, t, nctx, nc):
    bwd = jnp.where(t < nctx, nctx - 1 - t, nc - 1 - (t - nctx))
    return jnp.where(d == 0, t, bwd)


def _ssd_in_specs(ci):
    small_c = pl.BlockSpec((None, G, 1, HPG), lambda d, t: (d, 0, 0, 0))
    small_r = pl.BlockSpec((None, G, HPG, 1), lambda d, t: (d, 0, 0, 0))
    return [
        pl.BlockSpec((Q, CONVD), lambda d, t: (ci(d, t), 0)),
        pl.BlockSpec((None, G, Q, HPG), lambda d, t: (d, 0, ci(d, t), 0)),
        pl.BlockSpec((None, G, HPG, Q), lambda d, t: (d, 0, 0, ci(d, t))),
        small_c, small_r, small_c, small_r,
    ]


def _group_cols(g):
    return pl.ds(g * GW, GW), pl.ds(DI + g * N, N), pl.ds(DI + G * N + g * N, N)


def ssd_scan_fwd(xbc, dtc, dtr, bc, br, alc, alr, nctx, name, hosted=None):
    T = xbc.shape[0]
    nc = T // Q

    def body(xbc_ref, dtc_ref, dtr_ref, bc_ref, br_ref, alc_ref, alr_ref, y_ref, sin_ref, st_ref):
        d = pl.program_id(0)
        t = pl.program_id(1)

        @pl.when(t == 0)
        def _():
            st_ref[...] = jnp.zeros_like(st_ref)

        for g in range(G):
            xs, bs, cs = _group_cols(g)
            s_in = st_ref[g]
            sin_ref[g] = s_in
            y, s_out = _ssd_chunk(xbc_ref[:, xs], xbc_ref[:, bs], xbc_ref[:, cs], dtc_ref[g], dtr_ref[g], bc_ref[g], br_ref[g],
                                  alc_ref[g], alr_ref[g], s_in, d == 0)
            y_ref[:, xs] = y
            st_ref[g] = s_out

    ci = lambda d, t: _chunk_index(d, t, nctx, nc)
    out_specs = [
        pl.BlockSpec((None, Q, DI), lambda d, t: (d, ci(d, t), 0)),
        pl.BlockSpec((None, None, G, GW, N), lambda d, t: (d, ci(d, t), 0, 0, 0)),
    ]
    return _host_call(
        body, (2, nc), _ssd_in_specs(ci), out_specs, [S((2, T, DI), f32), S((2, nc, G, GW, N), f32)],
        [pltpu.VMEM((G, GW, N), f32)], ("arbitrary", "arbitrary"), name, (xbc, dtc, dtr, bc, br, alc, alr), hosted)


def ssd_scan_bwd(xbc, dtc, dtr, bc, br, alc, alr, s_in_all, dy, nctx, name, hosted=None):
    T = xbc.shape[0]
    nc = T // Q

    def body(xbc_ref, dtc_ref, dtr_ref, bc_ref, br_ref, alc_ref, alr_ref, sin_ref, dy_ref,
             dxbc_ref, ddtc_ref, ddtr_ref, dbc_ref, dbr_ref, dalc_ref, dalr_ref, ds_ref):
        d = pl.program_id(0)
        t = pl.program_id(1)

        @pl.when(t == 0)
        def _():
            ds_ref[...] = jnp.zeros_like(ds_ref)
            dbc_ref[...] = jnp.zeros_like(dbc_ref)
            dbr_ref[...] = jnp.zeros_like(dbr_ref)
            dalc_ref[...] = jnp.zeros_like(dalc_ref)
            dalr_ref[...] = jnp.zeros_like(dalr_ref)

        f = functools.partial(_ssd_chunk, is_fwd=(d == 0))
        for g in range(G):
            xs, bs, cs = _group_cols(g)
            _, vjp = jax.vjp(f, xbc_ref[:, xs], xbc_ref[:, bs], xbc_ref[:, cs], dtc_ref[g], dtr_ref[g], bc_ref[g], br_ref[g],
                             alc_ref[g], alr_ref[g], sin_ref[g])
            dx, dB, dC, ddtc, ddtr, dbc, dbr, dalc, dalr, ds = vjp((dy_ref[:, xs], ds_ref[g]))
            dxbc_ref[:, xs] = dx
            dxbc_ref[:, bs] = dB
            dxbc_ref[:, cs] = dC
            ddtc_ref[g] = ddtc
            ddtr_ref[g] = ddtr
            dbc_ref[g] += dbc
            dbr_ref[g] += dbr
            dalc_ref[g] += dalc
            dalr_ref[g] += dalr
            ds_ref[g] = ds

    ci = lambda d, t: _chunk_index(d, nc - 1 - t, nctx, nc)
    in_specs = _ssd_in_specs(ci) + [
        pl.BlockSpec((None, None, G, GW, N), lambda d, t: (d, ci(d, t), 0, 0, 0)),
        pl.BlockSpec((Q, DI), lambda d, t: (ci(d, t), 0)),
    ]
    small_c = pl.BlockSpec((None, G, 1, HPG), lambda d, t: (d, 0, 0, 0))
    small_r = pl.BlockSpec((None, G, HPG, 1), lambda d, t: (d, 0, 0, 0))
    out_specs = [
        pl.BlockSpec((None, Q, CONVD), lambda d, t: (d, ci(d, t), 0)),
        pl.BlockSpec((None, G, Q, HPG), lambda d, t: (d, 0, ci(d, t), 0)),
        pl.BlockSpec((None, G, HPG, Q), lambda d, t: (d, 0, 0, ci(d, t))),
        small_c, small_r, small_c, small_r,
    ]
    out_shape = [S((2, T, CONVD), f32), S((2, G, T, HPG), f32), S((2, G, HPG, T), f32),
                 S((2, G, 1, HPG), f32), S((2, G, HPG, 1), f32), S((2, G, 1, HPG), f32), S((2, G, HPG, 1), f32)]
    return _host_call(body, (2, nc), in_specs, out_specs, out_shape, [pltpu.VMEM((G, GW, N), f32)],
                      ("arbitrary", "arbitrary"), name, (xbc, dtc, dtr, bc, br, alc, alr, s_in_all, dy), hosted)


GTB = 128


def _gate_norm_f(yf, yb, x, z, dexp, w):
    y = (yf + yb + dexp * x) * (z * _sigmoid(z))
    return y * lax.rsqrt(jnp.mean(y * y, axis=-1, keepdims=True) + EPS) * w


def ssd_gate_fwd(y2, xbc, proj, dexp, w, nctxb, name):
    T = xbc.shape[0]
    L = T - nctxb * GTB

    def body(yf_ref, yb_ref, x_ref, z_ref, d_ref, w_ref, o_ref):
        o_ref[...] = _gate_norm_f(yf_ref[...], yb_ref[...], x_ref[...], z_ref[...], d_ref[...], w_ref[...]).astype(bf16)

    wide = pl.BlockSpec((GTB, DI), lambda i: (i + nctxb, 0))
    row = pl.BlockSpec((1, DI), lambda i: (0, 0))
    return pl.pallas_call(
        body, grid=(L // GTB,),
        in_specs=[pl.BlockSpec((None, GTB, DI), lambda i: (0, i + nctxb, 0)),
                  pl.BlockSpec((None, GTB, DI), lambda i: (1, i + nctxb, 0)), wide, wide, row, row],
        out_specs=pl.BlockSpec((GTB, DI), lambda i: (i, 0)), out_shape=S((L, DI), bf16),
        compiler_params=_cparams("parallel"), name=name)(y2, y2, xbc, proj, dexp, w)


def ssd_gate_bwd(y2, xbc, proj, dexp, w, dyn, nctxb, name, hosted=None):
    T = xbc.shape[0]
    nb = T // GTB

    def body(yf_ref, yb_ref, x_ref, z_ref, d_ref, w_ref, dyn_ref, dy_ref, dz_ref, dd_ref, dw_ref):
        i = pl.program_id(0)

        @pl.when(i == 0)
        def _():
            dd_ref[...] = jnp.zeros_like(dd_ref)
            dw_ref[...] = jnp.zeros_like(dw_ref)

        @pl.when(i < nctxb)
        def _():
            dy_ref[...] = jnp.zeros_like(dy_ref)
            dz_ref[...] = jnp.zeros_like(dz_ref)

        @pl.when(i >= nctxb)
        def _():
            _, vjp = jax.vjp(_gate_norm_f, yf_ref[...], yb_ref[...], x_ref[...], z_ref[...], d_ref[...], w_ref[...])
            dyf, _, _, dz, dd, dw = vjp(dyn_ref[...].astype(f32))
            dy_ref[...] = dyf
            dz_ref[...] = dz.astype(bf16)
            fold = (lax.broadcasted_iota(jnp.int32, (DI, 128), 0) // P == lax.broadcasted_iota(jnp.int32, (DI, 128), 1))
            dd_ref[...] += jnp.dot(dd, fold.astype(f32), precision=HI, preferred_element_type=f32)
            dw_ref[...] += dw

    wide = pl.BlockSpec((GTB, DI), lambda i: (i, 0))
    row = pl.BlockSpec((1, DI), lambda i: (0, 0))
    hrow = pl.BlockSpec((1, 128), lambda i: (0, 0))
    return _host_call(
        body, (nb,),
        [pl.BlockSpec((None, GTB, DI), lambda i: (0, i, 0)), pl.BlockSpec((None, GTB, DI), lambda i: (1, i, 0)),
         wide, wide, row, row, pl.BlockSpec((GTB, DI), lambda i: (jnp.maximum(i - nctxb, 0), 0))],
        [wide, wide, hrow, row],
        [S((T, DI), f32), S((T, DI), bf16), S((1, 128), f32), S((1, DI), f32)],
        [], ("arbitrary",), name, (y2, y2, xbc, proj, dexp, w, dyn), hosted)


CROWS = 2 * N_DEV


def mod_fwd(c16, modw, name):
    nl, _, cols = modw.shape

    def body(c_ref, w_ref, o_ref):
        cv = c_ref[...]
        s = cv * _sigmoid(cv)
        for l in range(nl):
            o_ref[l] = jnp.dot(s, w_ref[l], precision=HI, preferred_element_type=f32)

    return pl.pallas_call(body, in_specs=[VMEM, VMEM], out_specs=VMEM, out_shape=S((nl, CROWS, cols), f32),
                          compiler_params=pltpu.CompilerParams(vmem_limit_bytes=VMEM_LIMIT_BYTES), name=name)(c16, modw)


def mod_bwd(c16, modw, dm_sh, dm_all, name):
    nl, _, cols = modw.shape

    def body(c_ref, w_ref, dm_ref, dmall_ref, dw_ref, dc_ref, db_ref):
        cv = c_ref[...]
        sg = _sigmoid(cv)
        s = cv * sg
        ds_dc = sg * (1.0 + cv * (1.0 - sg))
        is_ctx = lax.broadcasted_iota(jnp.int32, (CROWS, D), 0) >= N_DEV
        dc = jnp.zeros((1, D), f32)
        for l in range(nl):
            dm = dm_ref[l]
            dw_ref[l] = lax.dot_general(s, dm, (((0,), (0,)), ((), ())), precision=HI, preferred_element_type=f32)
            dsv = lax.dot_general(dm, w_ref[l], (((1,), (1,)), ((), ())), precision=HI, preferred_element_type=f32)
            dc = dc + jnp.sum(jnp.where(is_ctx, dsv * ds_dc, 0.0), axis=0, keepdims=True)
            db_ref[pl.ds(l, 1), :] = jnp.sum(dmall_ref[l], axis=0, keepdims=True)
        dc_ref[...] = dc

    return pl.pallas_call(
        body, in_specs=[VMEM, VMEM, VMEM, VMEM], out_specs=[VMEM, VMEM, VMEM],
        out_shape=[S(modw.shape, f32), S((1, D), f32), S((nl, 6 * D), f32)],
        compiler_params=pltpu.CompilerParams(vmem_limit_bytes=VMEM_LIMIT_BYTES), name=name)(c16, modw, dm_sh, dm_all)


def adamw(w, g, m, v, name):
    R, C = w.shape
    rb = R if R <= 512 else max(r_ for r_ in range(8, 513, 8) if R % r_ == 0)
    bc1 = 1.0 - ADAM_B1 ** ADAM_STEP
    bc2 = 1.0 - ADAM_B2 ** ADAM_STEP

    def body(w_ref, g_ref, m_ref, v_ref, d_ref, nm_ref, nv_ref):
        gv = g_ref[...]
        m_new = ADAM_B1 * m_ref[...] + (1.0 - ADAM_B1) * gv
        v_new = ADAM_B2 * v_ref[...] + (1.0 - ADAM_B2) * (gv * gv)
        m_hat = m_new / bc1
        v_hat = v_new / bc2
        d_ref[...] = -ADAM_LR * (m_hat / (jnp.sqrt(v_hat) + ADAM_EPS) + ADAM_WD * w_ref[...])
        nm_ref[...] = m_new
        nv_ref[...] = v_new

    blk = pl.BlockSpec((rb, C), lambda i: (i, 0))
    return pl.pallas_call(body, grid=(R // rb,), in_specs=[blk] * 4, out_specs=[blk] * 3,
                          out_shape=[S((R, C), f32)] * 3, compiler_params=_cparams("parallel"), name=name)(w, g, m, v)


def _me():
    return lax.axis_index("x"), lax.axis_index("y"), lax.axis_index("c")


def allgather_small(x, name, with_sum=False, after=None):
    r, w = x.shape
    extra = () if after is None else (after,)

    def body(x_ref, *refs):
        refs = refs[len(extra):]
        if with_sum:
            out_ref, sum_ref, send_sems, recv_sems = refs
        else:
            out_ref, send_sems, recv_sems = refs
        mx, my, mc = _me()
        me = 4 * mx + 2 * my + mc
        out_ref[me] = x_ref[...]
        peers = []
        for k in range(1, N_DEV):
            kx, ky, kc = (k >> 2) & 1, (k >> 1) & 1, k & 1
            peers.append((mx + kx - 2 * mx * kx, my + ky - 2 * my * ky, mc + kc - 2 * mc * kc))
        copies = []
        for k, peer in enumerate(peers):
            cp = pltpu.make_async_remote_copy(src_ref=x_ref, dst_ref=out_ref.at[me], send_sem=send_sems.at[k],
                                              recv_sem=recv_sems.at[k], device_id=peer, device_id_type=MESH)
            cp.start()
            copies.append(cp)
        for k, (px, py, pc) in enumerate(peers):
            pltpu.make_async_remote_copy(src_ref=x_ref, dst_ref=out_ref.at[4 * px + 2 * py + pc], send_sem=send_sems.at[k],
                                         recv_sem=recv_sems.at[k], device_id=(px, py, pc), device_id_type=MESH).wait_recv()
        for cp in copies:
            cp.wait_send()
        if with_sum:
            acc = out_ref[0]
            for j in range(1, N_DEV):
                acc = acc + out_ref[j]
            sum_ref[...] = acc

    out_shape = [S((N_DEV, r, w), f32)] + ([S((r, w), f32)] if with_sum else [])
    outs = pl.pallas_call(
        body, in_specs=[VMEM] + [ANY] * len(extra), out_specs=[VMEM] * len(out_shape), out_shape=out_shape,
        scratch_shapes=[pltpu.SemaphoreType.DMA((N_DEV - 1,)), pltpu.SemaphoreType.DMA((N_DEV - 1,))],
        compiler_params=pltpu.CompilerParams(vmem_limit_bytes=VMEM_LIMIT_BYTES), name=name)(x, *extra)
    return outs if with_sum else outs[0]


def _tile2d(R, W, max_rows):
    if R <= max_rows:
        return R, W
    fits = [r_ for r_ in range(16, max_rows + 1, 16) if R % r_ == 0]
    return (max(fits), W) if fits else (R, 256)


def add_own(g, r, core, name, twice=False):
    _, _, R, W = g.shape
    rb, wb = _tile2d(R, W, 512)
    nout = 2 if twice else 1

    def body(core_ref, a_ref, b_ref, *o_refs):
        s = (a_ref[...].astype(f32) + b_ref[...].astype(f32)).astype(bf16)
        for o_ref in o_refs:
            o_ref[...] = s

    blk = pl.BlockSpec((None, rb, wb), lambda k, i, j, core_ref: (k, i, j))
    gs = pltpu.PrefetchScalarGridSpec(
        num_scalar_prefetch=1, grid=(4, R // rb, W // wb),
        in_specs=[pl.BlockSpec((None, None, rb, wb), lambda k, i, j, core_ref: (k, core_ref[0], i, j)), blk],
        out_specs=[blk] * nout)
    outs = pl.pallas_call(body, grid_spec=gs, out_shape=[S((4, R, W), bf16)] * nout,
                          compiler_params=_cparams("parallel", "parallel", "parallel"), name=name)(core, g, r)
    return tuple(outs) if twice else outs[0]


HBM_SPEC = pl.BlockSpec(memory_space=pltpu.HBM)
SEM_SPEC = pl.BlockSpec(memory_space=pltpu.SEMAPHORE)


def _chips_copy(p_ref, land_ref, send_sems, recv_sems, a, j):
    x, y, c = _me()
    px, py = [(1 - x, y), (x, 1 - y), (1 - x, 1 - y)][j]
    return pltpu.make_async_remote_copy(src_ref=p_ref.at[2 * px + py], dst_ref=land_ref.at[2 * x + y],
                                        send_sem=send_sems.at[3 * a + j], recv_sem=recv_sems.at[3 * a + j],
                                        device_id=(px, py, c), device_id_type=MESH)


def _chips_wait_copy(p_ref, land_ref, send_sems, recv_sems, a, j):
    x, y, c = _me()
    px, py = [(1 - x, y), (x, 1 - y), (1 - x, 1 - y)][j]
    return pltpu.make_async_remote_copy(src_ref=p_ref.at[2 * px + py], dst_ref=land_ref.at[2 * px + py],
                                        send_sem=send_sems.at[3 * a + j], recv_sem=recv_sems.at[3 * a + j],
                                        device_id=(px, py, c), device_id_type=MESH)


def _xor_peers():
    mx, my, mc = _me()
    peers = []
    for k in range(1, N_DEV):
        kx, ky, kc = (k >> 2) & 1, (k >> 1) & 1, k & 1
        peers.append((mx + kx - 2 * mx * kx, my + ky - 2 * my * ky, mc + kc - 2 * mc * kc))
    return peers


def gather_start(shards, name, after):
    na = len(shards)
    lands = [lax.empty((N_DEV,) + s_.shape, s_.dtype) for s_ in shards]

    def body(*refs):
        x_refs, land_refs = refs[:na], refs[na:2 * na]
        send_sems, recv_sems, local_sems = refs[2 * na + 1:2 * na + 4]
        token = refs[-1]
        mx, my, mc = _me()
        me = 4 * mx + 2 * my + mc
        for a in range(na):
            pltpu.make_async_copy(x_refs[a], land_refs[a].at[me], local_sems.at[a]).start()
            for k, peer in enumerate(_xor_peers()):
                pltpu.make_async_remote_copy(src_ref=x_refs[a], dst_ref=land_refs[a].at[me], send_sem=send_sems.at[7 * a + k],
                                             recv_sem=recv_sems.at[7 * a + k], device_id=peer, device_id_type=MESH).start()
        token[...] = jnp.zeros_like(token)

    arrs = list(shards) + lands
    outs = pl.pallas_call(
        body, name=name, in_specs=[HBM_SPEC] * (2 * na) + [ANY],
        out_shape=[DMA((7 * na,)), DMA((7 * na,)), DMA((na,))] + [pltpu.HBM(t.shape, t.dtype) for t in arrs]
        + [S((8, 128), f32)],
        out_specs=[SEM_SPEC] * 3 + [HBM_SPEC] * (2 * na) + [VMEM],
        input_output_aliases={k: 3 + k for k in range(2 * na)},
        compiler_params=pltpu.CompilerParams(has_side_effects=pltpu.SideEffectType.DATAFLOW_SIDE_EFFECTING),
    )(*[pltpu.with_memory_space_constraint(t, pltpu.HBM) for t in arrs], after)
    return outs[0], outs[1], outs[2], list(outs[3:3 + na]), list(outs[3 + na:3 + 2 * na]), outs[-1]


def gather_wait(send_sems, recv_sems, local_sems, shards, lands, after, name):
    na = len(shards)

    def body(*refs):
        x_refs, land_refs = refs[:na], refs[na:2 * na]
        ssem, rsem, lsem = refs[2 * na:2 * na + 3]
        mx, my, mc = _me()
        me = 4 * mx + 2 * my + mc
        for a in range(na):
            pltpu.make_async_copy(x_refs[a], land_refs[a].at[me], lsem.at[a]).wait()
            for k, (px, py, pc) in enumerate(_xor_peers()):
                cp = pltpu.make_async_remote_copy(src_ref=x_refs[a], dst_ref=land_refs[a].at[4 * px + 2 * py + pc],
                                                  send_sem=ssem.at[7 * a + k], recv_sem=rsem.at[7 * a + k],
                                                  device_id=(px, py, pc), device_id_type=MESH)
                cp.wait_send()
                cp.wait_recv()

    arrs = list(shards) + list(lands)
    outs = pl.pallas_call(
        body, name=name, in_specs=[HBM_SPEC] * (2 * na) + [SEM_SPEC] * 3 + [ANY],
        out_shape=[pltpu.HBM(t.shape, t.dtype) for t in arrs], out_specs=[HBM_SPEC] * (2 * na),
        input_output_aliases={k: k for k in range(2 * na)},
        compiler_params=pltpu.CompilerParams(has_side_effects=pltpu.SideEffectType.DATAFLOW_SIDE_EFFECTING),
    )(*arrs, send_sems, recv_sems, local_sems, after)
    return list(outs[na:])


def chips_start(parts, lands, name):
    na = len(parts)

    def body(*refs):
        p_refs, land_refs = refs[:na], refs[na:2 * na]
        send_sems, recv_sems = refs[2 * na], refs[2 * na + 1]
        token = refs[-1]
        for a in range(na):
            for j in range(3):
                _chips_copy(p_refs[a], land_refs[a], send_sems, recv_sems, a, j).start()
        token[...] = jnp.zeros_like(token)

    arrs = list(parts) + list(lands)
    outs = pl.pallas_call(
        body, name=name, in_specs=[HBM_SPEC] * (2 * na),
        out_shape=[DMA((3 * na,)), DMA((3 * na,))] + [pltpu.HBM(t.shape, t.dtype) for t in arrs] + [S((8, 128), f32)],
        out_specs=[SEM_SPEC, SEM_SPEC] + [HBM_SPEC] * (2 * na) + [VMEM],
        input_output_aliases={k: 2 + k for k in range(2 * na)},
        compiler_params=pltpu.CompilerParams(has_side_effects=pltpu.SideEffectType.DATAFLOW_SIDE_EFFECTING),
    )(*[pltpu.with_memory_space_constraint(t, pltpu.HBM) for t in arrs])
    return outs[0], outs[1], list(outs[2:2 + na]), list(outs[2 + na:2 + 2 * na]), outs[-1]


def chips_wait(send_sems, recv_sems, parts, lands, after, name):
    na = len(parts)

    def body(*refs):
        p_refs, land_refs = refs[:na], refs[na:2 * na]
        ssem, rsem = refs[2 * na], refs[2 * na + 1]
        for a in range(na):
            for j in range(3):
                cp = _chips_wait_copy(p_refs[a], land_refs[a], ssem, rsem, a, j)
                cp.wait_send()
                cp.wait_recv()

    arrs = list(parts) + list(lands)
    outs = pl.pallas_call(
        body, name=name, in_specs=[HBM_SPEC] * (2 * na) + [SEM_SPEC, SEM_SPEC, ANY],
        out_shape=[pltpu.HBM(t.shape, t.dtype) for t in arrs], out_specs=[HBM_SPEC] * (2 * na),
        input_output_aliases={k: k for k in range(2 * na)},
        compiler_params=pltpu.CompilerParams(has_side_effects=pltpu.SideEffectType.DATAFLOW_SIDE_EFFECTING),
    )(*arrs, send_sems, recv_sems, after)
    return list(outs[na:])


def sum_adamw(recv, w, m, v, layer, name, into=None, after=None):
    _, R, W = recv.shape
    rb, wb = _tile2d(R, W, 256)
    bc1 = 1.0 - ADAM_B1 ** ADAM_STEP
    bc2 = 1.0 - ADAM_B2 ** ADAM_STEP
    n_into = 0 if into is None else 4
    extra = () if after is None else (after,)

    def body(r_ref, w_ref, m_ref, v_ref, *refs):
        g_ref, d_ref, nm_ref, nv_ref = refs[n_into + len(extra):]
        gv = r_ref[0].astype(f32)
        for k in range(1, 4):
            gv = gv + r_ref[k].astype(f32)
        m_new = ADAM_B1 * m_ref[...] + (1.0 - ADAM_B1) * gv
        v_new = ADAM_B2 * v_ref[...] + (1.0 - ADAM_B2) * (gv * gv)
        g_ref[...] = gv
        d_ref[...] = -ADAM_LR * ((m_new / bc1) / (jnp.sqrt(v_new / bc2) + ADAM_EPS) + ADAM_WD * w_ref[...])
        nm_ref[...] = m_new
        nv_ref[...] = v_new

    if layer is None:
        wblk = pl.BlockSpec((rb, wb), lambda i, j: (i, j))
        oshape = S((R, W), f32)
    else:
        wblk = pl.BlockSpec((None, rb, wb), lambda i, j: (layer, i, j))
        oshape = S(w.shape, f32)
    return pl.pallas_call(
        body, grid=(R // rb, W // wb),
        in_specs=[pl.BlockSpec((4, rb, wb), lambda i, j: (0, i, j)), wblk, wblk, wblk] + [ANY] * (n_into + len(extra)),
        out_specs=[wblk] * 4, out_shape=[oshape] * 4, input_output_aliases={4 + k: k for k in range(n_into)},
        compiler_params=_cparams("parallel", "parallel"), name=name)(recv, w, m, v, *(into or ()), *extra)


def sum_rows(a, name):
    K, R, W = a.shape
    rb = _pick(R, (512, 256, 128, 64, 32, 16))

    def body(a_ref, o_ref):
        acc = a_ref[0].astype(f32)
        for k in range(1, K):
            acc = acc + a_ref[k].astype(f32)
        o_ref[...] = acc

    return pl.pallas_call(body, grid=(R // rb,), in_specs=[pl.BlockSpec((K, rb, W), lambda i: (0, i, 0))],
                          out_specs=pl.BlockSpec((rb, W), lambda i: (i, 0)), out_shape=S((R, W), f32),
                          compiler_params=_cparams("parallel"), name=name)(a)


DMA = pltpu.SemaphoreType.DMA


class GatherExchange:
    def __init__(self, arrays):
        self.arrays = list(arrays)
        self.na = len(self.arrays)
        self.out_shape = [S((N_DEV,) + a.shape, a.dtype) for a in self.arrays]
        self.scratch = [DMA((7 * self.na,)), DMA((7 * self.na,)), DMA((self.na,))]

    def ops(self, x_refs, out_refs, sems):
        send_sems, recv_sems, local_sems = sems
        na = self.na
        x, y, c = _me()
        me, sibling = (x, y, c), (x, y, 1 - c)
        chips = [(1 - x, y), (x, 1 - y), (1 - x, 1 - y)]

        def rows(a, px, py, pc):
            return out_refs[a].at[4 * px + 2 * py + pc]

        def copy(a, k, block, to, src=None):
            return pltpu.make_async_remote_copy(
                src_ref=rows(a, *block) if src is None else src, dst_ref=rows(a, *block),
                send_sem=send_sems.at[7 * a + k], recv_sem=recv_sems.at[7 * a + k], device_id=to, device_id_type=MESH)

        def local(a):
            return pltpu.make_async_copy(x_refs[a], rows(a, *me), local_sems.at[a])

        def first(a):
            return [copy(a, 0, me, sibling, src=x_refs[a])] + [copy(a, 1 + j, me, (*chip, c), src=x_refs[a])
                                                                for j, chip in enumerate(chips)]

        def start():
            for a in range(na):
                local(a).start()
                for cp in first(a):
                    cp.start()

        def mid():
            for a in range(na):
                for j, chip in enumerate(chips):
                    copy(a, 1 + j, (*chip, c), me).wait_recv()
                    copy(a, 4 + j, (*chip, c), sibling).start()

        def finish():
            for a in range(na):
                copy(a, 0, sibling, me).wait_recv()
                for j, chip in enumerate(chips):
                    copy(a, 4 + j, (*chip, 1 - c), me).wait_recv()
                for cp in first(a) + [copy(a, 4 + j, (*chip, c), sibling) for j, chip in enumerate(chips)]:
                    cp.wait_send()
                local(a).wait()

        return start, mid, finish


class SiblingExchange:
    def __init__(self, arrays):
        self.arrays = list(arrays)
        self.na = len(self.arrays)
        self.out_shape = [S((4,) + g.shape[2:], g.dtype) for g in self.arrays]
        self.scratch = [DMA((self.na,)), DMA((self.na,))]

    def ops(self, g_refs, out_refs, sems):
        send_sems, recv_sems = sems
        x, y, c = _me()

        def copy(a):
            return pltpu.make_async_remote_copy(src_ref=g_refs[a].at[:, 1 - c], dst_ref=out_refs[a],
                                                send_sem=send_sems.at[a], recv_sem=recv_sems.at[a],
                                                device_id=(x, y, 1 - c), device_id_type=MESH)

        def start():
            for a in range(self.na):
                copy(a).start()

        def finish():
            for a in range(self.na):
                copy(a).wait()

        return start, None, finish


class ChipsExchange:
    def __init__(self, arrays):
        self.arrays = list(arrays)
        self.na = len(self.arrays)
        self.out_shape = [S(p.shape, p.dtype) for p in self.arrays]
        self.scratch = [DMA((3 * self.na,)), DMA((3 * self.na,)), DMA((self.na,))]

    def ops(self, p_refs, out_refs, sems):
        send_sems, recv_sems, local_sems = sems
        x, y, c = _me()
        mine = 2 * x + y
        chips = [(1 - x, y), (x, 1 - y), (1 - x, 1 - y)]

        def local(a):
            return pltpu.make_async_copy(p_refs[a].at[mine], out_refs[a].at[mine], local_sems.at[a])

        def send(a, j):
            px, py = chips[j]
            return pltpu.make_async_remote_copy(src_ref=p_refs[a].at[2 * px + py], dst_ref=out_refs[a].at[mine],
                                                send_sem=send_sems.at[3 * a + j], recv_sem=recv_sems.at[3 * a + j],
                                                device_id=(px, py, c), device_id_type=MESH)

        def recv(a, j):
            px, py = chips[j]
            return pltpu.make_async_remote_copy(src_ref=p_refs[a].at[mine], dst_ref=out_refs[a].at[2 * px + py],
                                                send_sem=send_sems.at[3 * a + j], recv_sem=recv_sems.at[3 * a + j],
                                                device_id=(px, py, c), device_id_type=MESH)

        def start():
            for a in range(self.na):
                local(a).start()
                for j in range(3):
                    send(a, j).start()

        def finish():
            for a in range(self.na):
                for j in range(3):
                    recv(a, j).wait_recv()
                for j in range(3):
                    send(a, j).wait_send()
                local(a).wait()

        return start, None, finish


def exchange(ex, name):
    na = ex.na

    def body(*refs):
        start, mid, finish = ex.ops(refs[:na], refs[na:2 * na], refs[2 * na:])
        start()
        if mid is not None:
            mid()
        finish()

    return pl.pallas_call(body, in_specs=[ANY] * na, out_specs=[ANY] * na, out_shape=ex.out_shape,
                          scratch_shapes=ex.scratch, name=name)(*ex.arrays)


def _host_call(body, grid, in_specs, out_specs, out_shape, scratch_shapes, sem, name, args, hosted):
    if hosted is None:
        res = pl.pallas_call(body, grid=grid, in_specs=in_specs, out_specs=out_specs, out_shape=out_shape,
                             scratch_shapes=scratch_shapes, compiler_params=_cparams(*sem), name=name)(*args)
        return res, None
    n_in, n_out, n_sc, na = len(in_specs), len(out_shape), len(scratch_shapes), hosted.na
    nsteps = 1
    for g_ in grid:
        nsteps *= g_
    mid_step = (3 * nsteps) // 4
    i1 = n_in + na
    i2 = i1 + n_out
    i3 = i2 + na
    i4 = i3 + n_sc

    def wrapped(*refs):
        step = pl.program_id(0)
        for ax in range(1, len(grid)):
            step = step * grid[ax] + pl.program_id(ax)
        start, mid, finish = hosted.ops(refs[n_in:i1], refs[i2:i3], refs[i4:])
        pl.when(step == 0)(start)
        if mid is not None:
            pl.when(step == mid_step)(mid)
        body(*refs[:n_in], *refs[i1:i2], *refs[i3:i4])
        pl.when(step == nsteps - 1)(finish)

    res = pl.pallas_call(
        wrapped, grid=grid, in_specs=list(in_specs) + [ANY] * na, out_specs=list(out_specs) + [ANY] * na,
        out_shape=list(out_shape) + hosted.out_shape, scratch_shapes=list(scratch_shapes) + hosted.scratch,
        compiler_params=_cparams(*(("arbitrary",) * len(grid))), name=name)(*args, *hosted.arrays)
    return res[:n_out], res[n_out:]


PACK_ALIGN = 16 * PACK_W


def _pad_to(v, mult):
    n = v.shape[-1]
    extra = (-n) % mult
    if extra == 0:
        return v
    return jnp.concatenate([v, jnp.zeros(v.shape[:-1] + (extra,), v.dtype)], axis=-1)


def _f32_as_bf16_pairs(v):
    return lax.bitcast_convert_type(v.reshape(-1), bf16).reshape(-1)


def _bf16_pairs_as_f32(v):
    return lax.bitcast_convert_type(v.reshape(v.shape[:-1] + (v.shape[-1] // 2, 2)), f32)


def _col_shards(gw):
    lead = gw.shape[:-1]
    n = gw.shape[-1] // N_DEV
    t = gw.reshape(lead + (N_DEV, n))
    t = jnp.moveaxis(t, -2, 0)
    return t.reshape(N_DEV, -1)


def kernel(x, c, ctx, c_ctx, mod_w, mod_b, norm1_w, norm2_w, ssd_w_in, ssd_conv_w, ssd_conv_b, ssd_dt_bias, ssd_a_log, ssd_d, ssd_norm_w, ssd_w_out, conf_w_pw1, conf_b_pw1, conf_w_dw, conf_b_dw, conf_ln_w, conf_ln_b, conf_w_pw2, conf_b_pw2, ffn_w_up, ffn_conv_w, ffn_conv_b, ffn_w_down, final_norm_w, loss_target, m_c_ctx, m_mod_w, m_mod_b, m_norm1_w, m_norm2_w, m_ssd_w_in, m_ssd_conv_w, m_ssd_conv_b, m_ssd_dt_bias, m_ssd_a_log, m_ssd_d, m_ssd_norm_w, m_ssd_w_out, m_conf_w_pw1, m_conf_b_pw1, m_conf_w_dw, m_conf_b_dw, m_conf_ln_w, m_conf_ln_b, m_conf_w_pw2, m_conf_b_pw2, m_ffn_w_up, m_ffn_conv_w, m_ffn_conv_b, m_ffn_w_down, m_final_norm_w, v_c_ctx, v_mod_w, v_mod_b, v_norm1_w, v_norm2_w, v_ssd_w_in, v_ssd_conv_w, v_ssd_conv_b, v_ssd_dt_bias, v_ssd_a_log, v_ssd_d, v_ssd_norm_w, v_ssd_w_out, v_conf_w_pw1, v_conf_b_pw1, v_conf_w_dw, v_conf_b_dw, v_conf_ln_w, v_conf_ln_b, v_conf_w_pw2, v_conf_b_pw2, v_ffn_w_up, v_ffn_conv_w, v_ffn_conv_b, v_ffn_w_down, v_final_norm_w):
    mx, my, mc = _me()
    me = 4 * mx + 2 * my + mc
    L = x.shape[1]
    LC = ctx.shape[1]
    T = LC + L
    w_in_cols = ssd_w_in.shape[2] * N_DEV
    n_dt = w_in_cols - DI - CONVD

    small = [c[0], ssd_conv_w[0], conf_b_pw1[0], conf_w_dw[0], conf_b_dw[0], conf_ln_w[0], conf_ln_b[0], conf_b_pw2[0],
             ffn_conv_w]
    parts = [_f32_as_bf16_pairs(t) for t in small]
    sizes = [p.shape[0] for p in parts]
    small_flat = _pad_to(jnp.concatenate(parts), PACK_ALIGN).reshape(-1, PACK_W)
    w_in, small_g = exchange(GatherExchange([ssd_w_in[0].astype(bf16), small_flat]), "gather_first")
    w_up, w_down = [None, None], [None, None]
    small_g = small_g.reshape(N_DEV, -1)
    offs = [0]
    for s_ in sizes:
        offs.append(offs[-1] + s_)
    sm = [_bf16_pairs_as_f32(small_g[:, offs[i]:offs[i + 1]]) for i in range(len(sizes))]

    def cols(pc, K):
        return jnp.moveaxis(pc.reshape(N_DEV, K, -1), 0, 1).reshape(K, -1)

    c_all = sm[0]
    conv_w5 = cols(sm[1], 5)
    b_pw1 = sm[2].reshape(1, 2 * D)
    w_dw = cols(sm[3], CONF_K)
    b_dw, ln_w, ln_b, b_pw2 = (sm[i].reshape(1, D) for i in (4, 5, 6, 7))
    fcw = sm[8].reshape(N_DEV, 2, 9, FH // N_DEV)
    ffn_cw = [cols(fcw[:, i].reshape(N_DEV, -1), 9) for i in range(2)]
    in_segs = (DI, CONVD, n_dt)
    up_segs = (FH, FH)
    pw1_segs = (D, D)

    c16 = jnp.concatenate([c_all, jnp.broadcast_to(c_ctx[None, :], (N_DEV, D))], axis=0)
    m_sh = mod_fwd(c16, mod_w, "mod_fwd")
    mod_cols = mod_w.shape[2]
    m_gath = allgather_small(m_sh.reshape(2 * CROWS, mod_cols), "gather_mod")
    fly_a = gather_start([ssd_w_out[0].astype(bf16), ffn_w_up[0].astype(bf16), ffn_w_down[0].astype(bf16)],
                         "gather_a_start", m_gath)
    fly_b = gather_start([conf_w_pw1[0].astype(bf16), conf_w_pw2[0].astype(bf16)], "gather_b_start", fly_a[-1])
    fly_c = gather_start([ffn_w_up[1].astype(bf16), ffn_w_down[1].astype(bf16)], "gather_c_start", fly_b[-1])
    m_all = jnp.moveaxis(m_gath.reshape(N_DEV, 2, CROWS, mod_cols), 0, 2).reshape(2, CROWS, 6 * D) + mod_b[:, None, :]
    m_all = m_all + fly_c[-1][0, 0]
    m_lat = lax.dynamic_index_in_dim(m_all, me, axis=1, keepdims=False).reshape(2, 6, 1, D)
    m_ctx = m_all[:, N_DEV].reshape(2, 6, 1, D)
    zero_row = jnp.zeros((1, D), f32)

    def ffn_fwd(a2, i, tag):
        val, gate = smm_fwd(a2, w_up[i], None, up_segs, f"ffn{tag}_up")
        act, gs_, gvds, _ = ffn_gate_fwd(val, gate, ffn_cw[i], ffn_conv_b[i][None], f"ffn{tag}_gate")
        o2 = matmul(act, w_down[i], "nn", f32, f"ffn{tag}_down")
        return o2, (a2, gate, gs_, gvds, act)

    def ffn_bwd(do2, i, saved, tag):
        a2, gate, gs_, gvds, act = saved
        g_down = matmul(act, do2, "tn", bf16, f"ffn{tag}_down_dw")
        dact = matmul(do2, w_down[i], "nt", bf16, f"ffn{tag}_down_dx")
        dval, dgate, dcw, dcb = ffn_gate_bwd(gate, gs_, gvds, ffn_cw[i], dact, f"ffn{tag}_gate_bwd")
        g_up = smm_dw(a2, [dval, dgate], FH // 4, up_segs, 2, True, f"ffn{tag}_up_dw")
        da2, _ = smm_dx([dval, dgate], w_up[i], None, up_segs, bf16, f"ffn{tag}_up_dx")
        return da2, dict(w_up=g_up, w_down=g_down, conv_w=dcw, conv_b=dcb)

    nctx = LC // Q
    hx = x[0]
    sc0 = jnp.stack([m_ctx[0, 1], m_lat[0, 1]])
    sh0 = jnp.stack([m_ctx[0, 0], m_lat[0, 0]])
    a0 = modnorm_fwd(hx, norm1_w[0][None], sc0, sh0, LC // TB, "ssd_norm", ctx=ctx[0])
    z, xbc_pre, dt_raw = smm_fwd(a0, w_in, None, in_segs, "ssd_in")
    segs = ((0, LC), (LC, L))
    xbc, xbc_dsilu, _ = ssd_conv_fwd(xbc_pre, conv_w5, ssd_conv_b, segs, "ssd_conv")
    dt4 = dt_raw[:, :n_dt].reshape(T, 2, G, HPG)
    dtc = jnp.transpose(dt4, (1, 2, 0, 3))
    dtr = jnp.transpose(dt4, (1, 2, 3, 0))
    bias3 = ssd_dt_bias[0].reshape(2, G, HPG)
    alog3 = ssd_a_log[0].reshape(2, G, HPG)
    bc_, br_ = bias3[:, :, None, :], bias3[:, :, :, None]
    alc, alr = alog3[:, :, None, :], alog3[:, :, :, None]
    (y2, s_in_all), _ = ssd_scan_fwd(xbc, dtc, dtr, bc_, br_, alc, alr, nctx, "ssd_scan")
    dexp = jnp.repeat(ssd_d[0], P)[None, :]
    yn = ssd_gate_fwd(y2, xbc, z, dexp, ssd_norm_w, LC // GTB, "ssd_gate")
    w_out_g, w_up[0], w_down0_g = gather_wait(*fly_a[:5], yn, "gather_a_wait")
    w_out = w_out_g.reshape(DI, D)
    w_down[0] = w_down0_g.reshape(FH, D)
    o_ssd = matmul(yn, w_out, "nn", f32, "ssd_out")
    h1, a2_0 = resnorm_fwd(hx, o_ssd, m_lat[0, 2], zero_row, norm2_w[0][None], m_lat[0, 4], m_lat[0, 3], "ssd_res")
    o2_0, ffn0_saved = ffn_fwd(a2_0, 0, "0")

    h2, a1 = resnorm_fwd(h1, o2_0, m_lat[0, 5], zero_row, norm1_w[1][None], m_lat[1, 1], m_lat[1, 0], "ffn0_res")
    w_pw1, w_pw2_g = gather_wait(*fly_b[:5], a1, "gather_b_wait")
    w_pw2 = w_pw2_g.reshape(D, D)
    pa, pg = smm_fwd(a1, w_pw1, None, pw1_segs, "conf_pw1")
    dwc, _ = conf_glu_conv_fwd(pa, pg, b_pw1, w_dw, b_dw, "conf_conv")
    s1 = ln_silu_fwd(dwc, ln_w, ln_b, "conf_ln")
    o_conf = matmul(s1, w_pw2, "nn", f32, "conf_pw2")
    h3, a2_1 = resnorm_fwd(h2, o_conf, m_lat[1, 2], b_pw2, norm2_w[1][None], m_lat[1, 4], m_lat[1, 3], "conf_res")
    w_up[1], w_down1_g = gather_wait(*fly_c[:5], h3, "gather_c_wait")
    w_down[1] = w_down1_g.reshape(FH, D)
    o2_1, ffn1_saved = ffn_fwd(a2_1, 1, "1")

    loss_part, dh4, g_final, do2_1, dg2_1 = final_loss(h3, o2_1, m_lat[1, 5], final_norm_w[None], loss_target[0],
                                                       "loss_head")
    da2_1, gf1 = ffn_bwd(do2_1, 1, ffn1_saved, "1")
    dh3, dn2_1, dsc2_1, dsh2_1, do_conf, dg1_1, g_b_pw2 = normres_bwd(
        h3, norm2_w[1][None], m_lat[1, 4], m_lat[1, 3], da2_1, dh4, o_conf, m_lat[1, 2], b_pw2, "ffn1_norm_bwd")
    gf1.update(norm2=dn2_1, sh2=dsh2_1, sc2=dsc2_1, g2=dg2_1)
    g_pw2 = matmul(s1, do_conf, "tn", bf16, "conf_pw2_dw")
    ds1 = matmul(do_conf, w_pw2, "nt", bf16, "conf_pw2_dx")
    ddwc, g_ln_w, g_ln_b = ln_silu_bwd(dwc, ln_w, ln_b, ds1, "conf_ln_bwd")
    dpa, dpg, dba, dbg, g_w_dw, g_b_dw = conf_glu_conv_bwd(pa, pg, b_pw1, w_dw, ddwc, "conf_conv_bwd")
    g_b_pw1 = jnp.concatenate([dba, dbg], axis=1)
    g_pw1 = smm_dw(a1, [dpa, dpg], 2 * D // N_DEV, pw1_segs, 1, False, "conf_pw1_dw")
    da1, _ = smm_dx([dpa, dpg], w_pw1, None, pw1_segs, bf16, "conf_pw1_dx")
    dh2, g_n1_1, dsc1_1, dsh1_1, do2_0, dg2_0, _ = normres_bwd(
        h2, norm1_w[1][None], m_lat[1, 1], m_lat[1, 0], da1, dh3, o2_0, m_lat[0, 5], zero_row, "conf_norm_bwd")
    da2_0, gf0 = ffn_bwd(do2_0, 0, ffn0_saved, "0")
    dh1, dn2_0, dsc2_0, dsh2_0, do_ssd, dg1_0, _ = normres_bwd(
        h1, norm2_w[0][None], m_lat[0, 4], m_lat[0, 3], da2_0, dh2, o_ssd, m_lat[0, 2], zero_row, "ffn0_norm_bwd")
    gf0.update(norm2=dn2_0, sh2=dsh2_0, sc2=dsc2_0, g2=dg2_0)
    g_w_out = matmul(yn, do_ssd, "tn", bf16, "ssd_out_dw")
    dyn = matmul(do_ssd, w_out, "nt", bf16, "ssd_out_dx")
    core = mc.reshape(1).astype(jnp.int32)

    def by_device(t):
        return t.reshape((4, 2, -1, t.shape[-1]))

    early = [by_device(t) for t in (gf1["w_up"], gf1["w_down"], g_pw2, g_pw1, gf0["w_up"], gf0["w_down"], g_w_out)]
    (dy, dz, g_dexp, g_ssd_norm), early_sib = ssd_gate_bwd(
        y2, xbc, z, dexp, ssd_norm_w, dyn, LC // GTB, "ssd_gate_bwd", SiblingExchange(early))
    early_part = [add_own(t, r_, core, f"reduce_add{i}") for i, (t, r_) in enumerate(zip(early, early_sib))]
    (dxbc2, ddtc, ddtr, dbc, dbr, dalc, dalr), early_red = ssd_scan_bwd(
        xbc, dtc, dtr, bc_, br_, alc, alr, s_in_all, dy, nctx, "ssd_scan_bwd", ChipsExchange(early_part))
    ddt = (jnp.transpose(ddtc, (2, 0, 1, 3)) + jnp.transpose(ddtr, (3, 0, 1, 2))).reshape(T, n_dt)
    g_dt_bias = (dbc[:, :, 0, :] + dbr[:, :, :, 0]).reshape(2, NH_SSD)
    g_a_log = (dalc[:, :, 0, :] + dalr[:, :, :, 0]).reshape(2, NH_SSD)
    g_ssd_d = g_dexp[0, :NH_SSD]
    du, g_conv_w5, g_conv_b5 = ssd_conv_bwd(xbc_pre, conv_w5, xbc_dsilu, dxbc2, dy, dexp, segs, "ssd_conv_bwd")
    ddt_p = _pad_to(ddt, 128).astype(bf16)
    g_w_in = smm_dw(a0, [dz, du, ddt_p], w_in.shape[-1], in_segs, 2, True, "ssd_in_dw")
    g_ffn_cw = jnp.stack([gf0["conv_w"], gf1["conv_w"]])
    small_shards = [_col_shards(t) for t in (g_conv_w5, g_b_pw1, g_w_dw, g_b_dw, g_ln_w, g_ln_b, g_b_pw2, g_ffn_cw)]
    gsizes = [s_.shape[1] for s_ in small_shards]
    g_small = _pad_to(jnp.concatenate(small_shards, axis=1), PACK_ALIGN).astype(bf16)
    late = [by_device(g_w_in), by_device(g_small.reshape(N_DEV, -1, PACK_W))]
    da0, late_sib = smm_dx([dz, du, ddt_p], w_in, None, in_segs, f32, "ssd_in_dx", SiblingExchange(late))
    late_part = [add_own(t, r_, core, f"reduce_add_late{i}", twice=True) for i, (t, r_) in enumerate(zip(late, late_sib))]
    late_flying = chips_start([p_[0] for p_ in late_part], [p_[1] for p_ in late_part], "reduce_chips_late_start")
    dh0, g_n1_0, dsc1_0, dsh1_0 = modnorm_bwd(hx, norm1_w[0][None], sc0, sh0, da0, dh1, LC // TB, "ssd_norm_bwd",
                                              ctx=ctx[0])
    grad_x = dh0[None]

    r_up1, r_down1, r_pw2, r_pw1, r_up0, r_down0, r_out = early_red
    big = {}
    def tr(t):
        return jnp.swapaxes(t, -1, -2)

    up_t, m_up_t, v_up_t = tr(ffn_w_up), tr(m_ffn_w_up), tr(v_ffn_w_up)
    send_sems, recv_sems, late_p, late_land, token = late_flying
    up0 = sum_adamw(r_up0, up_t, m_up_t, v_up_t, 0, "adamw_ffn_w_up0", after=token)
    up1 = sum_adamw(r_up1, up_t, m_up_t, v_up_t, 1, "adamw_ffn_w_up1", into=up0)
    big["ffn_w_up"] = tuple(tr(t) for t in up1)
    big["conf_w_pw1"] = sum_adamw(r_pw1, conf_w_pw1[0], m_conf_w_pw1[0], v_conf_w_pw1[0], None, "adamw_conf_w_pw1",
                                  after=up1[0])
    big["ssd_w_out"] = sum_adamw(r_out, ssd_w_out[0], m_ssd_w_out[0], v_ssd_w_out[0], None, "adamw_ssd_w_out",
                                 after=big["conf_w_pw1"][0])
    dn0 = sum_adamw(r_down0, ffn_w_down, m_ffn_w_down, v_ffn_w_down, 0, "adamw_ffn_w_down0", after=big["ssd_w_out"][0])
    big["ffn_w_down"] = sum_adamw(r_down1, ffn_w_down, m_ffn_w_down, v_ffn_w_down, 1, "adamw_ffn_w_down1", into=dn0)
    big["conf_w_pw2"] = sum_adamw(r_pw2, conf_w_pw2[0], m_conf_w_pw2[0], v_conf_w_pw2[0], None, "adamw_conf_w_pw2",
                                  after=big["ffn_w_down"][0])

    zeros_d = jnp.zeros((1, D), f32)
    dm_lat = jnp.stack([
        jnp.concatenate([dsh1_0[1], dsc1_0[1], dg1_0, gf0["sh2"], gf0["sc2"], gf0["g2"]], axis=1),
        jnp.concatenate([dsh1_1, dsc1_1, dg1_1, gf1["sh2"], gf1["sc2"], gf1["g2"]], axis=1)])
    dm_ctx = jnp.stack([
        jnp.concatenate([dsh1_0[0], dsc1_0[0]] + [zeros_d] * 4, axis=1), jnp.zeros((1, 6 * D), f32)])
    dm_mine = jnp.concatenate([dm_lat.reshape(2, 6 * D), dm_ctx.reshape(2, 6 * D),
                               jnp.zeros((4, 6 * D), f32)], axis=0)
    dm_g = allgather_small(dm_mine, "gather_dmod", after=big["conf_w_pw2"][0])
    dm_all = jnp.concatenate([jnp.moveaxis(dm_g[:, 0:2], 0, 1), jnp.moveaxis(dm_g[:, 2:4], 0, 1)], axis=1)
    dm_sh = lax.dynamic_slice_in_dim(dm_all, me * mod_cols, mod_cols, axis=2)
    g_mod_w, g_cctx_part, g_mod_b = mod_bwd(c16, mod_w, dm_sh, dm_all, "mod_bwd")

    rep = [jnp.stack([g_n1_0[0], g_n1_1[0]]), jnp.stack([gf0["norm2"][0], gf1["norm2"][0]]), g_conv_b5, g_dt_bias, g_a_log,
           g_ssd_d, g_ssd_norm, jnp.stack([gf0["conv_b"][0], gf1["conv_b"][0]]), g_final, g_cctx_part, loss_part[:, :1]]
    rep_sizes = [r_.size for r_ in rep]
    rep_flat = _pad_to(jnp.concatenate([r_.reshape(-1) for r_ in rep]), 8 * PACK_W).reshape(-1, PACK_W)
    _, rep_sum = allgather_small(rep_flat, "reduce_replicated", with_sum=True)
    rep_sum = rep_sum.reshape(-1)
    roffs = [0]
    for s_ in rep_sizes:
        roffs.append(roffs[-1] + s_)
    rp = [rep_sum[roffs[i]:roffs[i + 1]] for i in range(len(rep_sizes))]
    loss = rp[10].reshape(())

    r_in, r_small = chips_wait(send_sems, recv_sems, late_p, late_land, rep_sum, "reduce_chips_late_wait")
    w_in_res = sum_adamw(r_in, tr(ssd_w_in[0]), tr(m_ssd_w_in[0]), tr(v_ssd_w_in[0]), None, "adamw_ssd_w_in")
    big["ssd_w_in"] = tuple(tr(t) for t in w_in_res)
    g_flat = sum_rows(r_small, "reduce_sum_small").reshape(-1)
    goffs = [0]
    for s_ in gsizes:
        goffs.append(goffs[-1] + s_)
    gs = [g_flat[goffs[i]:goffs[i + 1]] for i in range(len(gsizes))]
    grads = {
        "c_ctx": rp[9], "mod_w": g_mod_w, "mod_b": g_mod_b, "norm1_w": rp[0], "norm2_w": rp[1],
        "ssd_conv_w": gs[0], "ssd_conv_b": rp[2], "ssd_dt_bias": rp[3], "ssd_a_log": rp[4], "ssd_d": rp[5],
        "ssd_norm_w": rp[6], "conf_b_pw1": gs[1], "conf_w_dw": gs[2],
        "conf_b_dw": gs[3], "conf_ln_w": gs[4], "conf_ln_b": gs[5], "conf_b_pw2": gs[6],
        "ffn_conv_w": gs[7], "ffn_conv_b": rp[7], "final_norm_w": rp[8],
    }
    weights = dict(c_ctx=c_ctx, mod_w=mod_w, mod_b=mod_b, norm1_w=norm1_w, norm2_w=norm2_w, ssd_w_in=ssd_w_in, ssd_conv_w=ssd_conv_w, ssd_conv_b=ssd_conv_b, ssd_dt_bias=ssd_dt_bias, ssd_a_log=ssd_a_log, ssd_d=ssd_d, ssd_norm_w=ssd_norm_w, ssd_w_out=ssd_w_out, conf_w_pw1=conf_w_pw1, conf_b_pw1=conf_b_pw1, conf_w_dw=conf_w_dw, conf_b_dw=conf_b_dw, conf_ln_w=conf_ln_w, conf_ln_b=conf_ln_b, conf_w_pw2=conf_w_pw2, conf_b_pw2=conf_b_pw2, ffn_w_up=ffn_w_up, ffn_conv_w=ffn_conv_w, ffn_conv_b=ffn_conv_b, ffn_w_down=ffn_w_down, final_norm_w=final_norm_w)
    m_in = dict(c_ctx=m_c_ctx, mod_w=m_mod_w, mod_b=m_mod_b, norm1_w=m_norm1_w, norm2_w=m_norm2_w, ssd_w_in=m_ssd_w_in, ssd_conv_w=m_ssd_conv_w, ssd_conv_b=m_ssd_conv_b, ssd_dt_bias=m_ssd_dt_bias, ssd_a_log=m_ssd_a_log, ssd_d=m_ssd_d, ssd_norm_w=m_ssd_norm_w, ssd_w_out=m_ssd_w_out, conf_w_pw1=m_conf_w_pw1, conf_b_pw1=m_conf_b_pw1, conf_w_dw=m_conf_w_dw, conf_b_dw=m_conf_b_dw, conf_ln_w=m_conf_ln_w, conf_ln_b=m_conf_ln_b, conf_w_pw2=m_conf_w_pw2, conf_b_pw2=m_conf_b_pw2, ffn_w_up=m_ffn_w_up, ffn_conv_w=m_ffn_conv_w, ffn_conv_b=m_ffn_conv_b, ffn_w_down=m_ffn_w_down, final_norm_w=m_final_norm_w)
    v_in = dict(c_ctx=v_c_ctx, mod_w=v_mod_w, mod_b=v_mod_b, norm1_w=v_norm1_w, norm2_w=v_norm2_w, ssd_w_in=v_ssd_w_in, ssd_conv_w=v_ssd_conv_w, ssd_conv_b=v_ssd_conv_b, ssd_dt_bias=v_ssd_dt_bias, ssd_a_log=v_ssd_a_log, ssd_d=v_ssd_d, ssd_norm_w=v_ssd_norm_w, ssd_w_out=v_ssd_w_out, conf_w_pw1=v_conf_w_pw1, conf_b_pw1=v_conf_b_pw1, conf_w_dw=v_conf_w_dw, conf_b_dw=v_conf_b_dw, conf_ln_w=v_conf_ln_w, conf_ln_b=v_conf_ln_b, conf_w_pw2=v_conf_w_pw2, conf_b_pw2=v_conf_b_pw2, ffn_w_up=v_ffn_w_up, ffn_conv_w=v_ffn_conv_w, ffn_conv_b=v_ffn_conv_b, ffn_w_down=v_ffn_w_down, final_norm_w=v_final_norm_w)

    out_g, out_d, out_m, out_v = [], [], [], []
    for name_, w_ in weights.items():
        shape = w_.shape
        if name_ in big:
            for lst, t in zip((out_g, out_d, out_m, out_v), big[name_]):
                lst.append(t.reshape(shape))
            continue
        cols2 = shape[-1] if len(shape) > 1 else shape[0]
        g2 = grads[name_].reshape(-1, cols2)
        d_, nm_, nv_ = adamw(w_.reshape(-1, cols2), g2, m_in[name_].reshape(-1, cols2), v_in[name_].reshape(-1, cols2),
                             f"adamw_{name_}")
        out_g.append(g2.reshape(shape))
        out_d.append(d_.reshape(shape))
        out_m.append(nm_.reshape(shape))
        out_v.append(nv_.reshape(shape))
    return (loss, grad_x, *out_g, *out_d, *out_m, *out_v)
```

```python
import functools

import jax
import jax.numpy as jnp
from jax import lax
from jax.experimental import pallas as pl
from jax.experimental.pallas import tpu as pltpu

f32 = jnp.float32
bf16 = jnp.bfloat16
HI = lax.Precision.HIGHEST
S = jax.ShapeDtypeStruct
MESH = pl.DeviceIdType.MESH
ANY = pl.BlockSpec(memory_space=pl.ANY)
VMEM = pl.BlockSpec(memory_space=pltpu.VMEM)

N_DEV = 8
D = 1024
DI = 2048
CONVD = 4096
FH = 2816
GRID_W = 64
Q = 128
HPG = 4
P = 64
N = 128
G = 8
GW = HPG * P
NH_SSD = G * HPG
EPS = 1e-6
ADAM_LR, ADAM_B1, ADAM_B2, ADAM_EPS, ADAM_WD, ADAM_STEP = 0.001, 0.9, 0.999, 1e-08, 0.01, 10
VMEM_LIMIT_BYTES = 56 * 1024 * 1024
PACK_W = 1024
TB = 256
LTB = 512


def _cparams(*sem):
    return pltpu.CompilerParams(dimension_semantics=sem, vmem_limit_bytes=VMEM_LIMIT_BYTES)


def _pick(n, prefs):
    for p in prefs:
        if n % p == 0:
            return p
    return n


def _sigmoid(x):
    return 1.0 / (1.0 + jnp.exp(-x))


def _softplus(x):
    return jnp.maximum(x, 0.0) + jnp.log(1.0 + jnp.exp(-jnp.abs(x)))


def matmul(a, b, mode, out_dtype, name):
    if mode == "nn":
        (M, K), (_, Nn) = a.shape, b.shape
        bm, bn, bk = _pick(M, (512, 384, 256, 128)), Nn, K
    elif mode == "tn":
        (K, M), (_, Nn) = a.shape, b.shape
        bm, bn, bk = M, Nn, _pick(K, (256, 128))
    else:
        (M, K), (Nn, _) = a.shape, b.shape
        bm, bn, bk = _pick(M, (512, 384, 256, 128)), Nn, K
    nk = K // bk
    dims = {"nn": (((1,), (0,)), ((), ())), "tn": (((0,), (0,)), ((), ())), "nt": (((1,), (1,)), ((), ()))}[mode]

    def body(a_ref, b_ref, o_ref, acc_ref):
        k = pl.program_id(2)

        @pl.when(k == 0)
        def _():
            acc_ref[...] = jnp.zeros_like(acc_ref)

        acc_ref[...] += lax.dot_general(a_ref[...].astype(bf16), b_ref[...].astype(bf16), dims,
                                        preferred_element_type=f32)

        @pl.when(k == nk - 1)
        def _():
            o_ref[...] = acc_ref[...].astype(out_dtype)

    if mode == "nn":
        a_spec = pl.BlockSpec((bm, bk), lambda i, j, k: (i, k))
        b_spec = pl.BlockSpec((bk, bn), lambda i, j, k: (k, j))
    elif mode == "tn":
        a_spec = pl.BlockSpec((bk, bm), lambda i, j, k: (k, i))
        b_spec = pl.BlockSpec((bk, bn), lambda i, j, k: (k, j))
    else:
        a_spec = pl.BlockSpec((bm, bk), lambda i, j, k: (i, k))
        b_spec = pl.BlockSpec((bn, bk), lambda i, j, k: (j, k))
    return pl.pallas_call(
        body, grid=(M // bm, Nn // bn, nk), in_specs=[a_spec, b_spec],
        out_specs=pl.BlockSpec((bm, bn), lambda i, j, k: (i, j)),
        out_shape=S((M, Nn), out_dtype), scratch_shapes=[pltpu.VMEM((bm, bn), f32)],
        compiler_params=_cparams("parallel", "parallel", "arbitrary"), name=name,
    )(a, b)


SMM_DW_ROWS = (512, 256)
SMM_ROWS = (256,)


def _shard_pieces(seg_widths, n):
    bounds = [0]
    for sw in seg_widths:
        bounds.append(bounds[-1] + sw)
    assert bounds[-1] == N_DEV * n, (seg_widths, n)
    out = []
    for j in range(N_DEV):
        lo, hi = j * n, (j + 1) * n
        pcs = []
        for si in range(len(seg_widths)):
            a, b = max(lo, bounds[si]), min(hi, bounds[si + 1])
            if a < b:
                pcs.append((si, a - bounds[si], a - lo, b - a))
        out.append(pcs)
    return out


def _w_spec(w, layer):
    if layer is None:
        return pl.BlockSpec(w.shape, lambda *idx: (0, 0, 0))
    return pl.BlockSpec((N_DEV, None) + w.shape[2:], lambda *idx: (0, layer, 0, 0))


def smm_fwd(a, w, layer, seg_widths, name, hosted=None):
    M, K = a.shape
    n = w.shape[-1]
    pieces = _shard_pieces(seg_widths, n)
    padded = [sw + (-sw) % 128 for sw in seg_widths]
    bm = _pick(M, SMM_ROWS)

    def body(a_ref, w_ref, *o_refs):
        av = a_ref[...]
        for si, sw in enumerate(seg_widths):
            if padded[si] != sw:
                o_refs[si][:, pl.ds(padded[si] - 128, 128)] = jnp.zeros((bm, 128), f32)
        for j in range(N_DEV):
            for si, soff, woff, wd in pieces[j]:
                o_refs[si][:, pl.ds(soff, wd)] = jnp.dot(av, w_ref[j, :, pl.ds(woff, wd)], preferred_element_type=f32)

    outs, extra = _host_call(
        body, (M // bm,), [pl.BlockSpec((bm, K), lambda i: (i, 0)), _w_spec(w, layer)],
        [pl.BlockSpec((bm, pw), lambda i: (i, 0)) for pw in padded], [S((M, pw), f32) for pw in padded], [],
        ("parallel",), name, (a, w), hosted)
    return outs if hosted is None else (outs, extra)


def smm_dx(d_segs, w, layer, seg_widths, out_dtype, name, hosted=None):
    M = d_segs[0].shape[0]
    K, n = w.shape[-2], w.shape[-1]
    pieces = _shard_pieces(seg_widths, n)
    ns = len(d_segs)
    bm = _pick(M, SMM_ROWS)

    def body(*refs):
        d_refs, w_ref, o_ref = refs[:ns], refs[ns], refs[ns + 1]
        acc = jnp.zeros((bm, K), f32)
        for j in range(N_DEV):
            for si, soff, woff, wd in pieces[j]:
                acc = acc + lax.dot_general(d_refs[si][:, pl.ds(soff, wd)], w_ref[j, :, pl.ds(woff, wd)],
                                            (((1,), (1,)), ((), ())), preferred_element_type=f32)
        o_ref[...] = acc.astype(out_dtype)

    (out,), extra = _host_call(
        body, (M // bm,),
        [pl.BlockSpec((bm, d.shape[1]), lambda i: (i, 0)) for d in d_segs] + [_w_spec(w, layer)],
        [pl.BlockSpec((bm, K), lambda i: (i, 0))], [S((M, K), out_dtype)], [], ("parallel",), name,
        (*d_segs, w), hosted)
    return out, extra


def smm_dw(a, d_segs, n, seg_widths, ngrp, transposed, name):
    M, K = a.shape
    pieces = _shard_pieces(seg_widths, n)
    per = N_DEV // ngrp
    bm = _pick(M, SMM_DW_ROWS)
    nI = M // bm
    ns = len(d_segs)
    shard = (n, K) if transposed else (K, n)

    def body(*refs):
        a_ref, d_refs, o_ref, acc_ref = refs[0], refs[1:1 + ns], refs[1 + ns], refs[2 + ns]
        grp = pl.program_id(0)
        i = pl.program_id(1)

        @pl.when(i == 0)
        def _():
            acc_ref[...] = jnp.zeros_like(acc_ref)

        av = a_ref[...]
        for gs in range(ngrp):
            def one_group(gs=gs):
                for jj in range(per):
                    for si, soff, woff, wd in pieces[gs * per + jj]:
                        dv = d_refs[si][:, pl.ds(soff, wd)]
                        if transposed:
                            acc_ref[jj, pl.ds(woff, wd), :] += lax.dot_general(
                                dv, av, (((0,), (0,)), ((), ())), preferred_element_type=f32)
                        else:
                            acc_ref[jj, :, pl.ds(woff, wd)] += lax.dot_general(
                                av, dv, (((0,), (0,)), ((), ())), preferred_element_type=f32)
            pl.when(grp == gs)(one_group)

        @pl.when(i == nI - 1)
        def _():
            o_ref[...] = acc_ref[...].astype(bf16)

    return pl.pallas_call(
        body, grid=(ngrp, nI),
        in_specs=[pl.BlockSpec((bm, K), lambda g, i: (i, 0))]
        + [pl.BlockSpec((bm, d.shape[1]), lambda g, i: (i, 0)) for d in d_segs],
        out_specs=pl.BlockSpec((per,) + shard, lambda g, i: (g, 0, 0)), out_shape=S((N_DEV,) + shard, bf16),
        scratch_shapes=[pltpu.VMEM((per,) + shard, f32)],
        compiler_params=_cparams("arbitrary", "arbitrary"), name=name)(a, *d_segs)


def _modnorm_f(h, w, sc, sh):
    y = h * lax.rsqrt(jnp.mean(h * h, axis=-1, keepdims=True) + EPS)
    return (y * w) * (1.0 + sc) + sh


def _kind_specs(nctxb):
    if nctxb > 0:
        return pl.BlockSpec((None, 1, D), lambda i: (jnp.where(i < nctxb, 0, 1), 0, 0))
    return pl.BlockSpec((None, 1, D), lambda i: (0, 0, 0))


def _two_part_specs(nctxb):
    return (pl.BlockSpec((TB, D), lambda i: (jnp.minimum(i, nctxb - 1), 0)),
            pl.BlockSpec((TB, D), lambda i: (jnp.maximum(i - nctxb, 0), 0)))


def modnorm_fwd(h, w, sc, sh, nctxb, name, ctx=None):
    if ctx is None:
        T = h.shape[0]

        def body(h_ref, w_ref, sc_ref, sh_ref, o_ref):
            o_ref[...] = _modnorm_f(h_ref[...], w_ref[...], sc_ref[...], sh_ref[...]).astype(bf16)

        hspecs, hargs = [pl.BlockSpec((TB, D), lambda i: (i, 0))], (h,)
    else:
        T = h.shape[0] + ctx.shape[0]

        def body(c_ref, h_ref, w_ref, sc_ref, sh_ref, o_ref):
            hv = jnp.where(pl.program_id(0) < nctxb, c_ref[...], h_ref[...])
            o_ref[...] = _modnorm_f(hv, w_ref[...], sc_ref[...], sh_ref[...]).astype(bf16)

        hspecs, hargs = list(_two_part_specs(nctxb)), (ctx, h)
    row = pl.BlockSpec((1, D), lambda i: (0, 0))
    ks = _kind_specs(nctxb)
    return pl.pallas_call(body, grid=(T // TB,), in_specs=hspecs + [row, ks, ks],
                          out_specs=pl.BlockSpec((TB, D), lambda i: (i, 0)), out_shape=S((T, D), bf16),
                          compiler_params=_cparams("parallel"), name=name)(*hargs, w, sc, sh)


def modnorm_bwd(h, w, sc, sh, da, dres, nctxb, name, ctx=None):
    T = h.shape[0] + (0 if ctx is None else ctx.shape[0])
    kinds = sc.shape[0]
    nh = 1 if ctx is None else 2

    def body(*refs):
        w_ref, sc_ref, sh_ref, da_ref, dres_ref, dh_ref, dw_ref, dsc_ref, dsh_ref = refs[nh:]
        i = pl.program_id(0)
        hv = refs[0][...] if ctx is None else jnp.where(i < nctxb, refs[0][...], refs[1][...])
        _, vjp = jax.vjp(_modnorm_f, hv, w_ref[...], sc_ref[...], sh_ref[...])
        dh, dw, dsc, dsh = vjp(da_ref[...].astype(f32))
        dh_ref[...] = dres_ref[...] + dh

        @pl.when(i == 0)
        def _():
            dw_ref[...] = jnp.zeros_like(dw_ref)

        @pl.when((i == 0) | (i == nctxb))
        def _():
            dsc_ref[...] = jnp.zeros_like(dsc_ref)
            dsh_ref[...] = jnp.zeros_like(dsh_ref)

        dw_ref[...] += dw
        dsc_ref[...] += dsc
        dsh_ref[...] += dsh

    blk = pl.BlockSpec((TB, D), lambda i: (i, 0))
    lat = pl.BlockSpec((TB, D), lambda i: (jnp.maximum(i - nctxb, 0), 0))
    row = pl.BlockSpec((1, D), lambda i: (0, 0))
    ks = _kind_specs(nctxb)
    hspecs, hargs = ([blk], (h,)) if ctx is None else (list(_two_part_specs(nctxb)), (ctx, h))
    return pl.pallas_call(
        body, grid=(T // TB,), in_specs=hspecs + [row, ks, ks, blk, lat], out_specs=[lat, row, ks, ks],
        out_shape=[S((T - nctxb * TB, D), f32), S((1, D), f32), S((kinds, 1, D), f32), S((kinds, 1, D), f32)],
        compiler_params=_cparams("arbitrary"), name=name)(*hargs, w, sc, sh, da, dres)


def resnorm_fwd(h, o, g, b, w, sc, sh, name):
    T = h.shape[0]

    def body(h_ref, o_ref, g_ref, b_ref, w_ref, sc_ref, sh_ref, hn_ref, a_ref):
        hn = h_ref[...] + g_ref[...] * (o_ref[...] + b_ref[...])
        hn_ref[...] = hn
        a_ref[...] = _modnorm_f(hn, w_ref[...], sc_ref[...], sh_ref[...]).astype(bf16)

    blk = pl.BlockSpec((LTB, D), lambda i: (i, 0))
    row = pl.BlockSpec((1, D), lambda i: (0, 0))
    return pl.pallas_call(body, grid=(T // LTB,), in_specs=[blk, blk, row, row, row, row, row], out_specs=[blk, blk],
                          out_shape=[S((T, D), f32), S((T, D), bf16)], compiler_params=_cparams("parallel"),
                          name=name)(h, o, g, b, w, sc, sh)


def normres_bwd(h, w, sc, sh, da, dres, o, g, b, name):
    T = h.shape[0]

    def body(h_ref, w_ref, sc_ref, sh_ref, da_ref, dres_ref, o_ref, g_ref, b_ref,
             dh_ref, dw_ref, dsc_ref, dsh_ref, do_ref, dg_ref, db_ref):
        _, vjp = jax.vjp(_modnorm_f, h_ref[...], w_ref[...], sc_ref[...], sh_ref[...])
        dhn, dw, dsc, dsh = vjp(da_ref[...].astype(f32))
        dh = dres_ref[...] + dhn
        dh_ref[...] = dh
        do = g_ref[...] * dh
        do_ref[...] = do.astype(bf16)
        sums = (dw, dsc, dsh, jnp.sum(dh * (o_ref[...] + b_ref[...]), axis=0, keepdims=True),
                jnp.sum(do, axis=0, keepdims=True))

        @pl.when(pl.program_id(0) == 0)
        def _():
            for r_ in (dw_ref, dsc_ref, dsh_ref, dg_ref, db_ref):
                r_[...] = jnp.zeros_like(r_)

        for r_, s_ in zip((dw_ref, dsc_ref, dsh_ref, dg_ref, db_ref), sums):
            r_[...] += s_

    blk = pl.BlockSpec((LTB, D), lambda i: (i, 0))
    row = pl.BlockSpec((1, D), lambda i: (0, 0))
    return pl.pallas_call(
        body, grid=(T // LTB,), in_specs=[blk, row, row, row, blk, blk, blk, row, row],
        out_specs=[blk, row, row, row, blk, row, row],
        out_shape=[S((T, D), f32), S((1, D), f32), S((1, D), f32), S((1, D), f32), S((T, D), bf16), S((1, D), f32),
                   S((1, D), f32)],
        compiler_params=_cparams("arbitrary"), name=name)(h, w, sc, sh, da, dres, o, g, b)


def final_loss(h, o, g, w, tgt, name):
    T = h.shape[0]

    def f(hv, wv, tv):
        y = (hv * lax.rsqrt(jnp.mean(hv * hv, axis=-1, keepdims=True) + EPS)) * wv
        e = y - tv
        return 0.5 * jnp.sum(jnp.sum(e * e, axis=-1, keepdims=True), axis=0, keepdims=True) * (1.0 / D)

    def body(h_ref, o_ref, g_ref, w_ref, t_ref, loss_ref, dh_ref, dw_ref, do_ref, dg_ref):
        i = pl.program_id(0)
        tv = t_ref[...]
        ov = o_ref[...]
        gv = g_ref[...]
        val, vjp = jax.vjp(lambda a, b_: f(a, b_, tv), h_ref[...] + gv * ov, w_ref[...])
        dh, dw = vjp(jnp.ones((1, 1), f32))
        dh_ref[...] = dh
        do_ref[...] = (gv * dh).astype(bf16)

        @pl.when(i == 0)
        def _():
            loss_ref[...] = jnp.zeros_like(loss_ref)
            dw_ref[...] = jnp.zeros_like(dw_ref)
            dg_ref[...] = jnp.zeros_like(dg_ref)

        loss_ref[...] += jnp.broadcast_to(val, (1, 128))
        dw_ref[...] += dw
        dg_ref[...] += jnp.sum(dh * ov, axis=0, keepdims=True)

    blk = pl.BlockSpec((LTB, D), lambda i: (i, 0))
    row = pl.BlockSpec((1, D), lambda i: (0, 0))
    return pl.pallas_call(body, grid=(T // LTB,), in_specs=[blk, blk, row, row, blk],
                          out_specs=[pl.BlockSpec((1, 128), lambda i: (0, 0)), blk, row, blk, row],
                          out_shape=[S((1, 128), f32), S((T, D), f32), S((1, D), f32), S((T, D), bf16), S((1, D), f32)],
                          compiler_params=_cparams("arbitrary"), name=name)(h, o, g, w, tgt)


CB = 256
RT = 32
RTB = 16


def _fold8(t):
    acc = t[0:8]
    for k in range(1, t.shape[0] // 8):
        acc = acc + t[8 * k:8 * (k + 1)]
    return acc


def _rows(start, off=0, rt=RT):
    return pl.ds(pl.multiple_of(start + off, 8), rt)


def _rowsb(start, off=0):
    return _rows(start, off, RTB)


def _zero_rows(ref, start, n):
    ref[pl.ds(start, n), :] = jnp.zeros((n, ref.shape[1]), f32)


K5, HALF5, PAD5 = 5, 2, 8


def _taps5(base_ref, r, rt, sign):
    n = rt + 2 * PAD5
    v = base_ref[pl.ds(pl.multiple_of(r, 8), n), :]
    taps = []
    for k in range(K5):
        o = sign * (k - HALF5)
        rolled = v if o == 0 else pltpu.roll(v, (-o) % n, axis=0)
        taps.append(rolled[PAD5:PAD5 + rt])
    return taps


def ssd_conv_fwd(u, w, b, segs, name, hosted=None):
    T = u.shape[0]
    maxlen = max(ln for _, ln in segs)

    def body(u_ref, w_ref, b_ref, o_ref, ds_ref, base_ref):
        wv = [w_ref[pl.ds(k, 1), :] for k in range(K5)]
        bv = b_ref[...]
        for s0, ln in segs:
            _zero_rows(base_ref, 0, PAD5)
            _zero_rows(base_ref, PAD5 + ln, PAD5)
            base_ref[pl.ds(PAD5, ln), :] = u_ref[pl.ds(s0, ln), :]

            def tile(i, carry):
                r = i * RT
                taps = _taps5(base_ref, r, RT, 1)
                acc = jnp.broadcast_to(bv, (RT, CB))
                for k in range(K5):
                    acc = acc + taps[k] * wv[k]
                sg = _sigmoid(acc)
                o_ref[_rows(r, s0), :] = acc * sg
                ds_ref[_rows(r, s0), :] = sg * (1.0 + acc * (1.0 - sg))
                return carry

            lax.fori_loop(0, ln // RT, tile, 0, unroll=2)

    cblk = pl.BlockSpec((T, CB), lambda j: (0, j))
    (out, dsilu), extra = _host_call(
        body, (CONVD // CB,),
        [cblk, pl.BlockSpec((K5, CB), lambda j: (0, j)), pl.BlockSpec((1, CB), lambda j: (0, j))],
        [cblk, cblk], [S((T, CONVD), f32), S((T, CONVD), f32)],
        [pltpu.VMEM((maxlen + 2 * PAD5, CB), f32)], ("parallel",), name, (u, w, b), hosted)
    return out, dsilu, extra


def ssd_conv_bwd(proj, w, dsilu, dy2, dyskip, dexp, segs, name):
    T = proj.shape[0]
    maxlen = max(ln for _, ln in segs)
    nskip = DI // CB

    def body(u_ref, w_ref, ds_ref, dya_ref, dyb_ref, dsk_ref, dexp_ref, du_ref, dw_ref, db_ref, base_ref, dbase_ref):
        wv = [w_ref[pl.ds(k, 1), :] for k in range(K5)]
        has_skip = (pl.program_id(0) < nskip).astype(f32) * dexp_ref[...]
        acc8 = tuple(jnp.zeros((8, CB), f32) for _ in range(K5 + 1))
        for s0, ln in segs:
            for ref in (base_ref, dbase_ref):
                _zero_rows(ref, 0, PAD5)
                _zero_rows(ref, PAD5 + ln, PAD5)
            base_ref[pl.ds(PAD5, ln), :] = u_ref[pl.ds(s0, ln), :]

            def tile1(i, carry):
                r = i * RTB
                dy = dya_ref[_rowsb(r, s0), :] + dyb_ref[_rowsb(r, s0), :] + has_skip * dsk_ref[_rowsb(r, s0), :]
                dpre = dy * ds_ref[_rowsb(r, s0), :]
                dbase_ref[_rowsb(r, PAD5), :] = dpre
                taps = _taps5(base_ref, r, RTB, 1)
                new = [carry[k] + _fold8(dpre * taps[k]) for k in range(K5)]
                new.append(carry[K5] + _fold8(dpre))
                return tuple(new)

            acc8 = lax.fori_loop(0, ln // RTB, tile1, acc8, unroll=2)

            def tile2(i, carry):
                r = i * RTB
                taps = _taps5(dbase_ref, r, RTB, -1)
                du = jnp.zeros((RTB, CB), f32)
                for k in range(K5):
                    du = du + taps[k] * wv[k]
                du_ref[_rowsb(r, s0), :] = du.astype(bf16)
                return carry

            lax.fori_loop(0, ln // RTB, tile2, 0, unroll=4)
        for k in range(K5):
            dw_ref[pl.ds(k, 1), :] = jnp.sum(acc8[k], axis=0, keepdims=True)
        db_ref[...] = jnp.sum(acc8[K5], axis=0, keepdims=True)

    cblk = pl.BlockSpec((T, CB), lambda j: (0, j))
    return pl.pallas_call(
        body, grid=(CONVD // CB,),
        in_specs=[cblk, pl.BlockSpec((K5, CB), lambda j: (0, j)), cblk,
                  pl.BlockSpec((None, T, CB), lambda j: (0, 0, j)), pl.BlockSpec((None, T, CB), lambda j: (1, 0, j)),
                  pl.BlockSpec((T, CB), lambda j: (0, jnp.minimum(j, nskip - 1))),
                  pl.BlockSpec((1, CB), lambda j: (0, jnp.minimum(j, nskip - 1)))],
        out_specs=[cblk, pl.BlockSpec((K5, CB), lambda j: (0, j)), pl.BlockSpec((1, CB), lambda j: (0, j))],
        out_shape=[S((T, CONVD), bf16), S((K5, CONVD), f32), S((1, CONVD), f32)],
        scratch_shapes=[pltpu.VMEM((maxlen + 2 * PAD5, CB), f32), pltpu.VMEM((maxlen + 2 * PAD5, CB), f32)],
        compiler_params=_cparams("parallel"), name=name)(proj, w, dsilu, dy2, dy2, dyskip, dexp)


GPAD = GRID_W


def _grid_copies(g_ref, src, L):
    col = lax.broadcasted_iota(jnp.int32, (L, CB), 0) & (GRID_W - 1)
    for d in range(3):
        _zero_rows(g_ref.at[d], 0, GPAD)
        _zero_rows(g_ref.at[d], GPAD + L, GPAD)
    g_ref[1, pl.ds(GPAD, L), :] = src
    g_ref[0, pl.ds(GPAD, L), :] = jnp.where(col != 0, g_ref[1, pl.ds(GPAD - 1, L), :], 0.0)
    g_ref[2, pl.ds(GPAD, L), :] = jnp.where(col != GRID_W - 1, g_ref[1, pl.ds(GPAD + 1, L), :], 0.0)


def ffn_gate_fwd(val, gate, cw, cb_, name, hosted=None):
    L = val.shape[0]
    nb = FH // CB

    def body(val_ref, gate_ref, w_ref, b_ref, o_ref, s_ref, vds_ref, g_ref):
        wv = [w_ref[pl.ds(t, 1), :] for t in range(9)]
        bv = b_ref[...]
        _grid_copies(g_ref, gate_ref[...], L)

        def tile(i, carry):
            r = i * RT
            acc = jnp.broadcast_to(bv, (RT, CB))
            for dr in range(3):
                for dc in range(3):
                    acc = acc + g_ref[dc, _rows(r, GPAD + (dr - 1) * GRID_W), :] * wv[3 * dr + dc]
            sg = _sigmoid(acc)
            s = acc * sg
            v = val_ref[_rows(r), :]
            o_ref[_rows(r), :] = (s * v).astype(bf16)
            s_ref[_rows(r), :] = s
            vds_ref[_rows(r), :] = v * (sg * (1.0 + acc * (1.0 - sg)))
            return carry

        lax.fori_loop(0, L // RT, tile, 0, unroll=2)

    cblk = pl.BlockSpec((L, CB), lambda j: (0, j))
    (out, s_, vds), extra = _host_call(
        body, (nb,), [cblk, cblk, pl.BlockSpec((9, CB), lambda j: (0, j)), pl.BlockSpec((1, CB), lambda j: (0, j))],
        [cblk, cblk, cblk], [S((L, FH), bf16), S((L, FH), f32), S((L, FH), f32)],
        [pltpu.VMEM((3, L + 2 * GPAD, CB), f32)], ("parallel",), name, (val, gate, cw, cb_), hosted)
    return out, s_, vds, extra


def ffn_gate_bwd(gate, s_, vds, cw, dact, name):
    L = gate.shape[0]
    nb = FH // CB

    def body(gate_ref, s_ref, vds_ref, w_ref, da_ref, dval_ref, dgate_ref, dw_ref, db_ref, g_ref, d_ref):
        wv = [w_ref[pl.ds(t, 1), :] for t in range(9)]
        _grid_copies(g_ref, gate_ref[...], L)

        def tile1(i, carry):
            r = i * RTB
            da = da_ref[_rowsb(r), :].astype(f32)
            dval_ref[_rowsb(r), :] = (da * s_ref[_rowsb(r), :]).astype(bf16)
            dpre = da * vds_ref[_rowsb(r), :]
            d_ref[_rowsb(r), :] = dpre
            new = [carry[t] + _fold8(dpre * g_ref[t % 3, _rowsb(r, GPAD + (t // 3 - 1) * GRID_W), :]) for t in range(9)]
            new.append(carry[9] + _fold8(dpre))
            return tuple(new)

        acc8 = lax.fori_loop(0, L // RTB, tile1, tuple(jnp.zeros((8, CB), f32) for _ in range(10)), unroll=2)
        for t in range(9):
            dw_ref[pl.ds(t, 1), :] = jnp.sum(acc8[t], axis=0, keepdims=True)
        db_ref[...] = jnp.sum(acc8[9], axis=0, keepdims=True)
        _grid_copies(g_ref, d_ref[...], L)

        def tile2(i, carry):
            r = i * RTB
            dg = jnp.zeros((RTB, CB), f32)
            for dr in range(3):
                for dc in range(3):
                    dg = dg + g_ref[2 - dc, _rowsb(r, GPAD - (dr - 1) * GRID_W), :] * wv[3 * dr + dc]
            dgate_ref[_rowsb(r), :] = dg.astype(bf16)
            return carry

        lax.fori_loop(0, L // RTB, tile2, 0, unroll=4)

    cblk = pl.BlockSpec((L, CB), lambda j: (0, j))
    return pl.pallas_call(
        body, grid=(nb,),
        in_specs=[cblk, cblk, cblk, pl.BlockSpec((9, CB), lambda j: (0, j)), cblk],
        out_specs=[cblk, cblk, pl.BlockSpec((9, CB), lambda j: (0, j)), pl.BlockSpec((1, CB), lambda j: (0, j))],
        out_shape=[S((L, FH), bf16), S((L, FH), bf16), S((9, FH), f32), S((1, FH), f32)],
        scratch_shapes=[pltpu.VMEM((3, L + 2 * GPAD, CB), f32), pltpu.VMEM((L, CB), f32)],
        compiler_params=_cparams("parallel"), name=name)(gate, s_, vds, cw, dact)


CONF_K = 31
CHALF = CONF_K // 2
CPAD = 16


def _shift_copies8(c_ref, base_ref, L):
    n = L + 2 * CPAD - 8
    for b_ in range(8):
        c_ref[b_, pl.ds(0, n), :] = base_ref[pl.ds(b_, n), :]


def _tap_ab(o):
    return o % 8, o - o % 8


def conf_glu_conv_fwd(pa, pg, b1, wdw, bdw, name, hosted=None):
    L = pa.shape[0]
    nb = D // CB

    def body(pa_ref, pg_ref, ba_ref, bg_ref, w_ref, bdw_ref, o_ref, base_ref, c_ref):
        _zero_rows(base_ref, 0, CPAD)
        _zero_rows(base_ref, CPAD + L, CPAD)
        base_ref[pl.ds(CPAD, L), :] = (pa_ref[...] + ba_ref[...]) * _sigmoid(pg_ref[...] + bg_ref[...])
        _shift_copies8(c_ref, base_ref, L)
        bv = bdw_ref[...]

        def tile(i, carry):
            r = i * RT
            acc = jnp.broadcast_to(bv, (RT, CB))
            for k in range(CONF_K):
                b_, a8 = _tap_ab(k - CHALF)
                acc = acc + c_ref[b_, _rows(r, CPAD + a8), :] * w_ref[pl.ds(k, 1), :]
            o_ref[_rows(r), :] = acc
            return carry

        lax.fori_loop(0, L // RT, tile, 0, unroll=2)

    cblk = pl.BlockSpec((L, CB), lambda j: (0, j))
    rblk = pl.BlockSpec((1, CB), lambda j: (0, j))
    rgblk = pl.BlockSpec((1, CB), lambda j: (0, nb + j))
    (out,), extra = _host_call(
        body, (nb,), [cblk, cblk, rblk, rgblk, pl.BlockSpec((CONF_K, CB), lambda j: (0, j)), rblk],
        [cblk], [S((L, D), f32)], [pltpu.VMEM((L + 2 * CPAD, CB), f32), pltpu.VMEM((8, L + 2 * CPAD, CB), f32)],
        ("parallel",), name, (pa, pg, b1, b1, wdw, bdw), hosted)
    return out, extra


def conf_glu_conv_bwd(pa, pg, b1, wdw, dy, name):
    L = pa.shape[0]
    nb = D // CB

    def body(pa_ref, pg_ref, ba_ref, bg_ref, w_ref, dy_ref, dpa_ref, dpg_ref, dba_ref, dbg_ref, dw_ref, dbdw_ref,
             base_ref, c_ref, acc_ref):
        _zero_rows(base_ref, 0, CPAD)
        _zero_rows(base_ref, CPAD + L, CPAD)
        base_ref[pl.ds(CPAD, L), :] = (pa_ref[...] + ba_ref[...]) * _sigmoid(pg_ref[...] + bg_ref[...])
        _shift_copies8(c_ref, base_ref, L)
        acc_ref[...] = jnp.zeros_like(acc_ref)

        def tile1(i, carry):
            r = i * RT
            dyt = dy_ref[_rows(r), :]
            for k in range(CONF_K):
                b_, a8 = _tap_ab(k - CHALF)
                acc_ref[k] += _fold8(dyt * c_ref[b_, _rows(r, CPAD + a8), :])
            return carry + _fold8(dyt)

        db8 = lax.fori_loop(0, L // RT, tile1, jnp.zeros((8, CB), f32), unroll=2)
        dbdw_ref[...] = jnp.sum(db8, axis=0, keepdims=True)
        for k in range(CONF_K):
            dw_ref[pl.ds(k, 1), :] = jnp.sum(acc_ref[k], axis=0, keepdims=True)
        base_ref[pl.ds(CPAD, L), :] = dy_ref[...]
        _shift_copies8(c_ref, base_ref, L)
        ba = ba_ref[...]
        bg = bg_ref[...]

        def tile2(i, carry):
            r = i * RTB
            dglu = jnp.zeros((RTB, CB), f32)
            for k in range(CONF_K):
                b_, a8 = _tap_ab(CHALF - k)
                dglu = dglu + c_ref[b_, _rowsb(r, CPAD + a8), :] * w_ref[pl.ds(k, 1), :]
            a = pa_ref[_rowsb(r), :] + ba
            sg = _sigmoid(pg_ref[_rowsb(r), :] + bg)
            dpa = dglu * sg
            dpg = dglu * a * (sg * (1.0 - sg))
            dpa_ref[_rowsb(r), :] = dpa.astype(bf16)
            dpg_ref[_rowsb(r), :] = dpg.astype(bf16)
            return carry[0] + _fold8(dpa), carry[1] + _fold8(dpg)

        s8 = lax.fori_loop(0, L // RTB, tile2, (jnp.zeros((8, CB), f32), jnp.zeros((8, CB), f32)), unroll=2)
        dba_ref[...] = jnp.sum(s8[0], axis=0, keepdims=True)
        dbg_ref[...] = jnp.sum(s8[1], axis=0, keepdims=True)

    cblk = pl.BlockSpec((L, CB), lambda j: (0, j))
    rblk = pl.BlockSpec((1, CB), lambda j: (0, j))
    rgblk = pl.BlockSpec((1, CB), lambda j: (0, nb + j))
    wblk = pl.BlockSpec((CONF_K, CB), lambda j: (0, j))
    return pl.pallas_call(
        body, grid=(nb,), in_specs=[cblk, cblk, rblk, rgblk, wblk, cblk],
        out_specs=[cblk, cblk, rblk, rblk, wblk, rblk],
        out_shape=[S((L, D), bf16), S((L, D), bf16), S((1, D), f32), S((1, D), f32), S((CONF_K, D), f32), S((1, D), f32)],
        scratch_shapes=[pltpu.VMEM((L + 2 * CPAD, CB), f32), pltpu.VMEM((8, L + 2 * CPAD, CB), f32),
                        pltpu.VMEM((CONF_K, 8, CB), f32)],
        compiler_params=_cparams("parallel"), name=name)(pa, pg, b1, b1, wdw, dy)


def _ln_silu_f(x, w, b):
    mu = jnp.mean(x, axis=-1, keepdims=True)
    d = x - mu
    y = d * lax.rsqrt(jnp.mean(d * d, axis=-1, keepdims=True) + EPS) * w + b
    return y * _sigmoid(y)


def ln_silu_fwd(x, w, b, name):
    T = x.shape[0]

    def body(x_ref, w_ref, b_ref, o_ref):
        o_ref[...] = _ln_silu_f(x_ref[...], w_ref[...], b_ref[...]).astype(bf16)

    blk = pl.BlockSpec((TB, D), lambda i: (i, 0))
    row = pl.BlockSpec((1, D), lambda i: (0, 0))
    return pl.pallas_call(body, grid=(T // TB,), in_specs=[blk, row, row], out_specs=blk, out_shape=S((T, D), bf16),
                          compiler_params=_cparams("parallel"), name=name)(x, w, b)


def ln_silu_bwd(x, w, b, ds, name):
    T = x.shape[0]

    def body(x_ref, w_ref, b_ref, ds_ref, dx_ref, dw_ref, db_ref):
        i = pl.program_id(0)
        _, vjp = jax.vjp(_ln_silu_f, x_ref[...], w_ref[...], b_ref[...])
        dx, dw, db = vjp(ds_ref[...].astype(f32))
        dx_ref[...] = dx

        @pl.when(i == 0)
        def _():
            dw_ref[...] = jnp.zeros_like(dw_ref)
            db_ref[...] = jnp.zeros_like(db_ref)

        dw_ref[...] += dw
        db_ref[...] += db

    blk = pl.BlockSpec((TB, D), lambda i: (i, 0))
    row = pl.BlockSpec((1, D), lambda i: (0, 0))
    return pl.pallas_call(body, grid=(T // TB,), in_specs=[blk, row, row, blk], out_specs=[blk, row, row],
                          out_shape=[S((T, D), f32), S((1, D), f32), S((1, D), f32)],
                          compiler_params=_cparams("arbitrary"), name=name)(x, w, b, ds)


def _mxu(a, b, dims):
    return lax.dot_general(a.astype(bf16), b.astype(bf16), (dims, ((), ())), preferred_element_type=f32)


def _nn(a, b):
    return _mxu(a, b, ((1,), (0,)))


def _nt(a, b):
    return _mxu(a, b, ((1,), (1,)))


def _tn(a, b):
    return _mxu(a, b, ((0,), (0,)))


@jax.custom_vjp
def _dot_nn(a, b):
    return _nn(a, b)


@jax.custom_vjp
def _dot_nt(a, b):
    return _nt(a, b)


@jax.custom_vjp
def _dot_tn(a, b):
    return _tn(a, b)


_dot_nn.defvjp(lambda a, b: (_nn(a, b), (a, b)), lambda res, g: (_nt(g, res[1]), _tn(res[0], g)))
_dot_nt.defvjp(lambda a, b: (_nt(a, b), (a, b)), lambda res, g: (_nn(g, res[1]), _tn(g, res[0])))
_dot_tn.defvjp(lambda a, b: (_tn(a, b), (a, b)), lambda res, g: (_nt(res[1], g), _nn(res[0], g)))


def _exact_dot(a, b, dims, split_first):
    v = a if split_first else b
    p1 = v.astype(bf16)
    r1 = v - p1.astype(f32)
    p2 = r1.astype(bf16)
    p3 = (r1 - p2.astype(f32)).astype(bf16)
    out = None
    for p in (p1, p2, p3):
        lhs, rhs = (p, b.astype(bf16)) if split_first else (a.astype(bf16), p)
        t = lax.dot_general(lhs, rhs, (dims, ((), ())), preferred_element_type=f32)
        out = t if out is None else out + t
    return out


@jax.custom_vjp
def _masked_sum_cols(mf, a):
    return _exact_dot(mf, a, ((1,), (0,)), False)


@jax.custom_vjp
def _masked_sum_rows(mf, a):
    return _exact_dot(a, mf, ((1,), (1,)), True)


_masked_sum_cols.defvjp(lambda mf, a: (_exact_dot(mf, a, ((1,), (0,)), False), mf),
                        lambda mf, g: (jnp.zeros_like(mf), _exact_dot(mf, g, ((0,), (0,)), False)))
_masked_sum_rows.defvjp(lambda mf, a: (_exact_dot(a, mf, ((1,), (1,)), True), mf),
                        lambda mf, g: (jnp.zeros_like(mf), _exact_dot(g, mf, ((1,), (0,)), True)))


def _masked_sum(mf, a, rows):
    return _masked_sum_rows(mf, a) if rows else _masked_sum_cols(mf, a)


def _lanes_to_rows(v):
    r = lax.broadcasted_iota(jnp.int32, (GW, GW), 0)
    c = lax.broadcasted_iota(jnp.int32, (GW, GW), 1)
    return jnp.sum(jnp.where(r == c, jnp.broadcast_to(v, (GW, GW)), 0.0), axis=1, keepdims=True)


def _ssd_chunk(x, B, C, dtc, dtr, bc, br, alc, alr, s_in, is_fwd):
    row = lax.broadcasted_iota(jnp.int32, (Q, Q), 0)
    col = lax.broadcasted_iota(jnp.int32, (Q, Q), 1)
    sgn = jnp.where(is_fwd, 1, -1).astype(jnp.int32)
    mask = (row - col) * sgn >= 0
    mf = mask.astype(f32)
    lane_head = lax.broadcasted_iota(jnp.int32, (1, GW), 1) // P

    def spread(v):
        out = jnp.zeros((v.shape[0], GW), f32)
        for r in range(HPG):
            out = jnp.where(lane_head == r, v[:, r:r + 1], out)
        return out

    dt_c = _softplus(dtc + bc)
    dt_r = _softplus(dtr + br)
    a_c = dt_c * (-jnp.exp(alc))
    a_r = dt_r * (-jnp.exp(alr))
    acum_c = _masked_sum(mf, a_c, False)
    acum_r = _masked_sum(mf, a_r, True)
    tot_c = jnp.sum(a_c, axis=0, keepdims=True)
    dt_e = spread(dt_c)
    acum_e = spread(acum_c)
    tot_e = spread(tot_c)
    xdt = x * dt_e
    cb = _dot_nt(C, B)
    scores, xs = [], []
    for r in range(HPG):
        seg = acum_c[:, r:r + 1] - acum_r[r:r + 1, :]
        scores.append(cb * jnp.exp(jnp.where(mask, seg, -jnp.inf)))
        xs.append(jnp.where(lane_head == r, xdt, 0.0))
    y = _dot_nn(jnp.concatenate(scores, axis=1), jnp.concatenate(xs, axis=0))
    y = y + _dot_nt(C, s_in) * jnp.exp(acum_e)
    xe = xdt * jnp.exp(tot_e - acum_e)
    s_out = _lanes_to_rows(jnp.exp(tot_e)) * s_in + _dot_tn(xe, B)
    return y, s_out


def _chunk_index(d, t, nctx, nc):
    bwd = jnp.where(t < nctx, nctx - 1 - t, nc - 1 - (t - nctx))
    return jnp.where(d == 0, t, bwd)


def _ssd_in_specs(ci):
    small_c = pl.BlockSpec((None, G, 1, HPG), lambda d, t: (d, 0, 0, 0))
    small_r = pl.BlockSpec((None, G, HPG, 1), lambda d, t: (d, 0, 0, 0))
    return [
        pl.BlockSpec((Q, CONVD), lambda d, t: (ci(d, t), 0)),
        pl.BlockSpec((None, G, Q, HPG), lambda d, t: (d, 0, ci(d, t), 0)),
        pl.BlockSpec((None, G, HPG, Q), lambda d, t: (d, 0, 0, ci(d, t))),
        small_c, small_r, small_c, small_r,
    ]


def _group_cols(g):
    return pl.ds(g * GW, GW), pl.ds(DI + g * N, N), pl.ds(DI + G * N + g * N, N)


def ssd_scan_fwd(xbc, dtc, dtr, bc, br, alc, alr, nctx, name, hosted=None):
    T = xbc.shape[0]
    nc = T // Q

    def body(xbc_ref, dtc_ref, dtr_ref, bc_ref, br_ref, alc_ref, alr_ref, y_ref, sin_ref, st_ref):
        d = pl.program_id(0)
        t = pl.program_id(1)

        @pl.when(t == 0)
        def _():
            st_ref[...] = jnp.zeros_like(st_ref)

        for g in range(G):
            xs, bs, cs = _group_cols(g)
            s_in = st_ref[g]
            sin_ref[g] = s_in
            y, s_out = _ssd_chunk(xbc_ref[:, xs], xbc_ref[:, bs], xbc_ref[:, cs], dtc_ref[g], dtr_ref[g], bc_ref[g], br_ref[g],
                                  alc_ref[g], alr_ref[g], s_in, d == 0)
            y_ref[:, xs] = y
            st_ref[g] = s_out

    ci = lambda d, t: _chunk_index(d, t, nctx, nc)
    out_specs = [
        pl.BlockSpec((None, Q, DI), lambda d, t: (d, ci(d, t), 0)),
        pl.BlockSpec((None, None, G, GW, N), lambda d, t: (d, ci(d, t), 0, 0, 0)),
    ]
    return _host_call(
        body, (2, nc), _ssd_in_specs(ci), out_specs, [S((2, T, DI), f32), S((2, nc, G, GW, N), f32)],
        [pltpu.VMEM((G, GW, N), f32)], ("arbitrary", "arbitrary"), name, (xbc, dtc, dtr, bc, br, alc, alr), hosted)


def ssd_scan_bwd(xbc, dtc, dtr, bc, br, alc, alr, s_in_all, dy, nctx, name, hosted=None):
    T = xbc.shape[0]
    nc = T // Q

    def body(xbc_ref, dtc_ref, dtr_ref, bc_ref, br_ref, alc_ref, alr_ref, sin_ref, dy_ref,
             dxbc_ref, ddtc_ref, ddtr_ref, dbc_ref, dbr_ref, dalc_ref, dalr_ref, ds_ref):
        d = pl.program_id(0)
        t = pl.program_id(1)

        @pl.when(t == 0)
        def _():
            ds_ref[...] = jnp.zeros_like(ds_ref)
            dbc_ref[...] = jnp.zeros_like(dbc_ref)
            dbr_ref[...] = jnp.zeros_like(dbr_ref)
            dalc_ref[...] = jnp.zeros_like(dalc_ref)
            dalr_ref[...] = jnp.zeros_like(dalr_ref)

        f = functools.partial(_ssd_chunk, is_fwd=(d == 0))
        for g in range(G):
            xs, bs, cs = _group_cols(g)
            _, vjp = jax.vjp(f, xbc_ref[:, xs], xbc_ref[:, bs], xbc_ref[:, cs], dtc_ref[g], dtr_ref[g], bc_ref[g], br_ref[g],
                             alc_ref[g], alr_ref[g], sin_ref[g])
            dx, dB, dC, ddtc, ddtr, dbc, dbr, dalc, dalr, ds = vjp((dy_ref[:, xs], ds_ref[g]))
            dxbc_ref[:, xs] = dx
            dxbc_ref[:, bs] = dB
            dxbc_ref[:, cs] = dC
            ddtc_ref[g] = ddtc
            ddtr_ref[g] = ddtr
            dbc_ref[g] += dbc
            dbr_ref[g] += dbr
            dalc_ref[g] += dalc
            dalr_ref[g] += dalr
            ds_ref[g] = ds

    ci = lambda d, t: _chunk_index(d, nc - 1 - t, nctx, nc)
    in_specs = _ssd_in_specs(ci) + [
        pl.BlockSpec((None, None, G, GW, N), lambda d, t: (d, ci(d, t), 0, 0, 0)),
        pl.BlockSpec((Q, DI), lambda d, t: (ci(d, t), 0)),
    ]
    small_c = pl.BlockSpec((None, G, 1, HPG), lambda d, t: (d, 0, 0, 0))
    small_r = pl.BlockSpec((None, G, HPG, 1), lambda d, t: (d, 0, 0, 0))
    out_specs = [
        pl.BlockSpec((None, Q, CONVD), lambda d, t: (d, ci(d, t), 0)),
        pl.BlockSpec((None, G, Q, HPG), lambda d, t: (d, 0, ci(d, t), 0)),
        pl.BlockSpec((None, G, HPG, Q), lambda d, t: (d, 0, 0, ci(d, t))),
        small_c, small_r, small_c, small_r,
    ]
    out_shape = [S((2, T, CONVD), f32), S((2, G, T, HPG), f32), S((2, G, HPG, T), f32),
                 S((2, G, 1, HPG), f32), S((2, G, HPG, 1), f32), S((2, G, 1, HPG), f32), S((2, G, HPG, 1), f32)]
    return _host_call(body, (2, nc), in_specs, out_specs, out_shape, [pltpu.VMEM((G, GW, N), f32)],
                      ("arbitrary", "arbitrary"), name, (xbc, dtc, dtr, bc, br, alc, alr, s_in_all, dy), hosted)


GTB = 128


def _gate_norm_f(yf, yb, x, z, dexp, w):
    y = (yf + yb + dexp * x) * (z * _sigmoid(z))
    return y * lax.rsqrt(jnp.mean(y * y, axis=-1, keepdims=True) + EPS) * w


def ssd_gate_fwd(y2, xbc, proj, dexp, w, nctxb, name):
    T = xbc.shape[0]
    L = T - nctxb * GTB

    def body(yf_ref, yb_ref, x_ref, z_ref, d_ref, w_ref, o_ref):
        o_ref[...] = _gate_norm_f(yf_ref[...], yb_ref[...], x_ref[...], z_ref[...], d_ref[...], w_ref[...]).astype(bf16)

    wide = pl.BlockSpec((GTB, DI), lambda i: (i + nctxb, 0))
    row = pl.BlockSpec((1, DI), lambda i: (0, 0))
    return pl.pallas_call(
        body, grid=(L // GTB,),
        in_specs=[pl.BlockSpec((None, GTB, DI), lambda i: (0, i + nctxb, 0)),
                  pl.BlockSpec((None, GTB, DI), lambda i: (1, i + nctxb, 0)), wide, wide, row, row],
        out_specs=pl.BlockSpec((GTB, DI), lambda i: (i, 0)), out_shape=S((L, DI), bf16),
        compiler_params=_cparams("parallel"), name=name)(y2, y2, xbc, proj, dexp, w)


def ssd_gate_bwd(y2, xbc, proj, dexp, w, dyn, nctxb, name, hosted=None):
    T = xbc.shape[0]
    nb = T // GTB

    def body(yf_ref, yb_ref, x_ref, z_ref, d_ref, w_ref, dyn_ref, dy_ref, dz_ref, dd_ref, dw_ref):
        i = pl.program_id(0)

        @pl.when(i == 0)
        def _():
            dd_ref[...] = jnp.zeros_like(dd_ref)
            dw_ref[...] = jnp.zeros_like(dw_ref)

        @pl.when(i < nctxb)
        def _():
            dy_ref[...] = jnp.zeros_like(dy_ref)
            dz_ref[...] = jnp.zeros_like(dz_ref)

        @pl.when(i >= nctxb)
        def _():
            _, vjp = jax.vjp(_gate_norm_f, yf_ref[...], yb_ref[...], x_ref[...], z_ref[...], d_ref[...], w_ref[...])
            dyf, _, _, dz, dd, dw = vjp(dyn_ref[...].astype(f32))
            dy_ref[...] = dyf
            dz_ref[...] = dz.astype(bf16)
            fold = (lax.broadcasted_iota(jnp.int32, (DI, 128), 0) // P == lax.broadcasted_iota(jnp.int32, (DI, 128), 1))
            dd_ref[...] += jnp.dot(dd, fold.astype(f32), precision=HI, preferred_element_type=f32)
            dw_ref[...] += dw

    wide = pl.BlockSpec((GTB, DI), lambda i: (i, 0))
    row = pl.BlockSpec((1, DI), lambda i: (0, 0))
    hrow = pl.BlockSpec((1, 128), lambda i: (0, 0))
    return _host_call(
        body, (nb,),
        [pl.BlockSpec((None, GTB, DI), lambda i: (0, i, 0)), pl.BlockSpec((None, GTB, DI), lambda i: (1, i, 0)),
         wide, wide, row, row, pl.BlockSpec((GTB, DI), lambda i: (jnp.maximum(i - nctxb, 0), 0))],
        [wide, wide, hrow, row],
        [S((T, DI), f32), S((T, DI), bf16), S((1, 128), f32), S((1, DI), f32)],
        [], ("arbitrary",), name, (y2, y2, xbc, proj, dexp, w, dyn), hosted)


CROWS = 2 * N_DEV


def mod_fwd(c16, modw, name):
    nl, _, cols = modw.shape

    def body(c_ref, w_ref, o_ref):
        cv = c_ref[...]
        s = cv * _sigmoid(cv)
        for l in range(nl):
            o_ref[l] = jnp.dot(s, w_ref[l], precision=HI, preferred_element_type=f32)

    return pl.pallas_call(body, in_specs=[VMEM, VMEM], out_specs=VMEM, out_shape=S((nl, CROWS, cols), f32),
                          compiler_params=pltpu.CompilerParams(vmem_limit_bytes=VMEM_LIMIT_BYTES), name=name)(c16, modw)


def mod_bwd(c16, modw, dm_sh, dm_all, name):
    nl, _, cols = modw.shape

    def body(c_ref, w_ref, dm_ref, dmall_ref, dw_ref, dc_ref, db_ref):
        cv = c_ref[...]
        sg = _sigmoid(cv)
        s = cv * sg
        ds_dc = sg * (1.0 + cv * (1.0 - sg))
        is_ctx = lax.broadcasted_iota(jnp.int32, (CROWS, D), 0) >= N_DEV
        dc = jnp.zeros((1, D), f32)
        for l in range(nl):
            dm = dm_ref[l]
            dw_ref[l] = lax.dot_general(s, dm, (((0,), (0,)), ((), ())), precision=HI, preferred_element_type=f32)
            dsv = lax.dot_general(dm, w_ref[l], (((1,), (1,)), ((), ())), precision=HI, preferred_element_type=f32)
            dc = dc + jnp.sum(jnp.where(is_ctx, dsv * ds_dc, 0.0), axis=0, keepdims=True)
            db_ref[pl.ds(l, 1), :] = jnp.sum(dmall_ref[l], axis=0, keepdims=True)
        dc_ref[...] = dc

    return pl.pallas_call(
        body, in_specs=[VMEM, VMEM, VMEM, VMEM], out_specs=[VMEM, VMEM, VMEM],
        out_shape=[S(modw.shape, f32), S((1, D), f32), S((nl, 6 * D), f32)],
        compiler_params=pltpu.CompilerParams(vmem_limit_bytes=VMEM_LIMIT_BYTES), name=name)(c16, modw, dm_sh, dm_all)


def adamw(w, g, m, v, name):
    R, C = w.shape
    rb = R if R <= 512 else max(r_ for r_ in range(8, 513, 8) if R % r_ == 0)
    bc1 = 1.0 - ADAM_B1 ** ADAM_STEP
    bc2 = 1.0 - ADAM_B2 ** ADAM_STEP

    def body(w_ref, g_ref, m_ref, v_ref, d_ref, nm_ref, nv_ref):
        gv = g_ref[...]
        m_new = ADAM_B1 * m_ref[...] + (1.0 - ADAM_B1) * gv
        v_new = ADAM_B2 * v_ref[...] + (1.0 - ADAM_B2) * (gv * gv)
        m_hat = m_new / bc1
        v_hat = v_new / bc2
        d_ref[...] = -ADAM_LR * (m_hat / (jnp.sqrt(v_hat) + ADAM_EPS) + ADAM_WD * w_ref[...])
        nm_ref[...] = m_new
        nv_ref[...] = v_new

    blk = pl.BlockSpec((rb, C), lambda i: (i, 0))
    return pl.pallas_call(body, grid=(R // rb,), in_specs=[blk] * 4, out_specs=[blk] * 3,
                          out_shape=[S((R, C), f32)] * 3, compiler_params=_cparams("parallel"), name=name)(w, g, m, v)


def _me():
    return lax.axis_index("x"), lax.axis_index("y"), lax.axis_index("c")


def allgather_small(x, name, with_sum=False, after=None):
    r, w = x.shape
    extra = () if after is None else (after,)

    def body(x_ref, *refs):
        refs = refs[len(extra):]
        if with_sum:
            out_ref, sum_ref, send_sems, recv_sems = refs
        else:
            out_ref, send_sems, recv_sems = refs
        mx, my, mc = _me()
        me = 4 * mx + 2 * my + mc
        out_ref[me] = x_ref[...]
        peers = []
        for k in range(1, N_DEV):
            kx, ky, kc = (k >> 2) & 1, (k >> 1) & 1, k & 1
            peers.append((mx + kx - 2 * mx * kx, my + ky - 2 * my * ky, mc + kc - 2 * mc * kc))
        copies = []
        for k, peer in enumerate(peers):
            cp = pltpu.make_async_remote_copy(src_ref=x_ref, dst_ref=out_ref.at[me], send_sem=send_sems.at[k],
                                              recv_sem=recv_sems.at[k], device_id=peer, device_id_type=MESH)
            cp.start()
            copies.append(cp)
        for k, (px, py, pc) in enumerate(peers):
            pltpu.make_async_remote_copy(src_ref=x_ref, dst_ref=out_ref.at[4 * px + 2 * py + pc], send_sem=send_sems.at[k],
                                         recv_sem=recv_sems.at[k], device_id=(px, py, pc), device_id_type=MESH).wait_recv()
        for cp in copies:
            cp.wait_send()
        if with_sum:
            acc = out_ref[0]
            for j in range(1, N_DEV):
                acc = acc + out_ref[j]
            sum_ref[...] = acc

    out_shape = [S((N_DEV, r, w), f32)] + ([S((r, w), f32)] if with_sum else [])
    outs = pl.pallas_call(
        body, in_specs=[VMEM] + [ANY] * len(extra), out_specs=[VMEM] * len(out_shape), out_shape=out_shape,
        scratch_shapes=[pltpu.SemaphoreType.DMA((N_DEV - 1,)), pltpu.SemaphoreType.DMA((N_DEV - 1,))],
        compiler_params=pltpu.CompilerParams(vmem_limit_bytes=VMEM_LIMIT_BYTES), name=name)(x, *extra)
    return outs if with_sum else outs[0]


def _tile2d(R, W, max_rows):
    if R <= max_rows:
        return R, W
    fits = [r_ for r_ in range(16, max_rows + 1, 16) if R % r_ == 0]
    return (max(fits), W) if fits else (R, 256)


def add_own(g, r, core, name, twice=False):
    _, _, R, W = g.shape
    rb, wb = _tile2d(R, W, 512)
    nout = 2 if twice else 1

    def body(core_ref, a_ref, b_ref, *o_refs):
        s = (a_ref[...].astype(f32) + b_ref[...].astype(f32)).astype(bf16)
        for o_ref in o_refs:
            o_ref[...] = s

    blk = pl.BlockSpec((None, rb, wb), lambda k, i, j, core_ref: (k, i, j))
    gs = pltpu.PrefetchScalarGridSpec(
        num_scalar_prefetch=1, grid=(4, R // rb, W // wb),
        in_specs=[pl.BlockSpec((None, None, rb, wb), lambda k, i, j, core_ref: (k, core_ref[0], i, j)), blk],
        out_specs=[blk] * nout)
    outs = pl.pallas_call(body, grid_spec=gs, out_shape=[S((4, R, W), bf16)] * nout,
                          compiler_params=_cparams("parallel", "parallel", "parallel"), name=name)(core, g, r)
    return tuple(outs) if twice else outs[0]


HBM_SPEC = pl.BlockSpec(memory_space=pltpu.HBM)
SEM_SPEC = pl.BlockSpec(memory_space=pltpu.SEMAPHORE)


def _chips_copy(p_ref, land_ref, send_sems, recv_sems, a, j):
    x, y, c = _me()
    px, py = [(1 - x, y), (x, 1 - y), (1 - x, 1 - y)][j]
    return pltpu.make_async_remote_copy(src_ref=p_ref.at[2 * px + py], dst_ref=land_ref.at[2 * x + y],
                                        send_sem=send_sems.at[3 * a + j], recv_sem=recv_sems.at[3 * a + j],
                                        device_id=(px, py, c), device_id_type=MESH)


def _chips_wait_copy(p_ref, land_ref, send_sems, recv_sems, a, j):
    x, y, c = _me()
    px, py = [(1 - x, y), (x, 1 - y), (1 - x, 1 - y)][j]
    return pltpu.make_async_remote_copy(src_ref=p_ref.at[2 * px + py], dst_ref=land_ref.at[2 * px + py],
                                        send_sem=send_sems.at[3 * a + j], recv_sem=recv_sems.at[3 * a + j],
                                        device_id=(px, py, c), device_id_type=MESH)


def _xor_peers():
    mx, my, mc = _me()
    peers = []
    for k in range(1, N_DEV):
        kx, ky, kc = (k >> 2) & 1, (k >> 1) & 1, k & 1
        peers.append((mx + kx - 2 * mx * kx, my + ky - 2 * my * ky, mc + kc - 2 * mc * kc))
    return peers


def gather_start(shards, name, after):
    na = len(shards)
    lands = [lax.empty((N_DEV,) + s_.shape, s_.dtype) for s_ in shards]

    def body(*refs):
        x_refs, land_refs = refs[:na], refs[na:2 * na]
        send_sems, recv_sems, local_sems = refs[2 * na + 1:2 * na + 4]
        token = refs[-1]
        mx, my, mc = _me()
        me = 4 * mx + 2 * my + mc
        for a in range(na):
            pltpu.make_async_copy(x_refs[a], land_refs[a].at[me], local_sems.at[a]).start()
            for k, peer in enumerate(_xor_peers()):
                pltpu.make_async_remote_copy(src_ref=x_refs[a], dst_ref=land_refs[a].at[me], send_sem=send_sems.at[7 * a + k],
                                             recv_sem=recv_sems.at[7 * a + k], device_id=peer, device_id_type=MESH).start()
        token[...] = jnp.zeros_like(token)

    arrs = list(shards) + lands
    outs = pl.pallas_call(
        body, name=name, in_specs=[HBM_SPEC] * (2 * na) + [ANY],
        out_shape=[DMA((7 * na,)), DMA((7 * na,)), DMA((na,))] + [pltpu.HBM(t.shape, t.dtype) for t in arrs]
        + [S((8, 128), f32)],
        out_specs=[SEM_SPEC] * 3 + [HBM_SPEC] * (2 * na) + [VMEM],
        input_output_aliases={k: 3 + k for k in range(2 * na)},
        compiler_params=pltpu.CompilerParams(has_side_effects=pltpu.SideEffectType.DATAFLOW_SIDE_EFFECTING),
    )(*[pltpu.with_memory_space_constraint(t, pltpu.HBM) for t in arrs], after)
    return outs[0], outs[1], outs[2], list(outs[3:3 + na]), list(outs[3 + na:3 + 2 * na]), outs[-1]


def gather_wait(send_sems, recv_sems, local_sems, shards, lands, after, name):
    na = len(shards)

    def body(*refs):
        x_refs, land_refs = refs[:na], refs[na:2 * na]
        ssem, rsem, lsem = refs[2 * na:2 * na + 3]
        mx, my, mc = _me()
        me = 4 * mx + 2 * my + mc
        for a in range(na):
            pltpu.make_async_copy(x_refs[a], land_refs[a].at[me], lsem.at[a]).wait()
            for k, (px, py, pc) in enumerate(_xor_peers()):
                cp = pltpu.make_async_remote_copy(src_ref=x_refs[a], dst_ref=land_refs[a].at[4 * px + 2 * py + pc],
                                                  send_sem=ssem.at[7 * a + k], recv_sem=rsem.at[7 * a + k],
                                                  device_id=(px, py, pc), device_id_type=MESH)
                cp.wait_send()
                cp.wait_recv()

    arrs = list(shards) + list(lands)
    outs = pl.pallas_call(
        body, name=name, in_specs=[HBM_SPEC] * (2 * na) + [SEM_SPEC] * 3 + [ANY],
        out_shape=[pltpu.HBM(t.shape, t.dtype) for t in arrs], out_specs=[HBM_SPEC] * (2 * na),
        input_output_aliases={k: k for k in range(2 * na)},
        compiler_params=pltpu.CompilerParams(has_side_effects=pltpu.SideEffectType.DATAFLOW_SIDE_EFFECTING),
    )(*arrs, send_sems, recv_sems, local_sems, after)
    return list(outs[na:])


def chips_start(parts, lands, name):
    na = len(parts)

    def body(*refs):
        p_refs, land_refs = refs[:na], refs[na:2 * na]
        send_sems, recv_sems = refs[2 * na], refs[2 * na + 1]
        token = refs[-1]
        for a in range(na):
            for j in range(3):
                _chips_copy(p_refs[a], land_refs[a], send_sems, recv_sems, a, j).start()
        token[...] = jnp.zeros_like(token)

    arrs = list(parts) + list(lands)
    outs = pl.pallas_call(
        body, name=name, in_specs=[HBM_SPEC] * (2 * na),
        out_shape=[DMA((3 * na,)), DMA((3 * na,))] + [pltpu.HBM(t.shape, t.dtype) for t in arrs] + [S((8, 128), f32)],
        out_specs=[SEM_SPEC, SEM_SPEC] + [HBM_SPEC] * (2 * na) + [VMEM],
        input_output_aliases={k: 2 + k for k in range(2 * na)},
        compiler_params=pltpu.CompilerParams(has_side_effects=pltpu.SideEffectType.DATAFLOW_SIDE_EFFECTING),
    )(*[pltpu.with_memory_space_constraint(t, pltpu.HBM) for t in arrs])
    return outs[0], outs[1], list(outs[2:2 + na]), list(outs[2 + na:2 + 2 * na]), outs[-1]


def chips_wait(send_sems, recv_sems, parts, lands, after, name):
    na = len(parts)

    def body(*refs):
        p_refs, land_refs = refs[:na], refs[na:2 * na]
        ssem, rsem = refs[2 * na], refs[2 * na + 1]
        for a in range(na):
            for j in range(3):
                cp = _chips_wait_copy(p_refs[a], land_refs[a], ssem, rsem, a, j)
                cp.wait_send()
                cp.wait_recv()

    arrs = list(parts) + list(lands)
    outs = pl.pallas_call(
        body, name=name, in_specs=[HBM_SPEC] * (2 * na) + [SEM_SPEC, SEM_SPEC, ANY],
        out_shape=[pltpu.HBM(t.shape, t.dtype) for t in arrs], out_specs=[HBM_SPEC] * (2 * na),
        input_output_aliases={k: k for k in range(2 * na)},
        compiler_params=pltpu.CompilerParams(has_side_effects=pltpu.SideEffectType.DATAFLOW_SIDE_EFFECTING),
    )(*arrs, send_sems, recv_sems, after)
    return list(outs[na:])


def sum_adamw(recv, w, m, v, layer, name, into=None, after=None):
    _, R, W = recv.shape
    rb, wb = _tile2d(R, W, 256)
    bc1 = 1.0 - ADAM_B1 ** ADAM_STEP
    bc2 = 1.0 - ADAM_B2 ** ADAM_STEP
    n_into = 0 if into is None else 4
    extra = () if after is None else (after,)

    def body(r_ref, w_ref, m_ref, v_ref, *refs):
        g_ref, d_ref, nm_ref, nv_ref = refs[n_into + len(extra):]
        gv = r_ref[0].astype(f32)
        for k in range(1, 4):
            gv = gv + r_ref[k].astype(f32)
        m_new = ADAM_B1 * m_ref[...] + (1.0 - ADAM_B1) * gv
        v_new = ADAM_B2 * v_ref[...] + (1.0 - ADAM_B2) * (gv * gv)
        g_ref[...] = gv
        d_ref[...] = -ADAM_LR * ((m_new / bc1) / (jnp.sqrt(v_new / bc2) + ADAM_EPS) + ADAM_WD * w_ref[...])
        nm_ref[...] = m_new
        nv_ref[...] = v_new

    if layer is None:
        wblk = pl.BlockSpec((rb, wb), lambda i, j: (i, j))
        oshape = S((R, W), f32)
    else:
        wblk = pl.BlockSpec((None, rb, wb), lambda i, j: (layer, i, j))
        oshape = S(w.shape, f32)
    return pl.pallas_call(
        body, grid=(R // rb, W // wb),
        in_specs=[pl.BlockSpec((4, rb, wb), lambda i, j: (0, i, j)), wblk, wblk, wblk] + [ANY] * (n_into + len(extra)),
        out_specs=[wblk] * 4, out_shape=[oshape] * 4, input_output_aliases={4 + k: k for k in range(n_into)},
        compiler_params=_cparams("parallel", "parallel"), name=name)(recv, w, m, v, *(into or ()), *extra)


def sum_rows(a, name):
    K, R, W = a.shape
    rb = _pick(R, (512, 256, 128, 64, 32, 16))

    def body(a_ref, o_ref):
        acc = a_ref[0].astype(f32)
        for k in range(1, K):
            acc = acc + a_ref[k].astype(f32)
        o_ref[...] = acc

    return pl.pallas_call(body, grid=(R // rb,), in_specs=[pl.BlockSpec((K, rb, W), lambda i: (0, i, 0))],
                          out_specs=pl.BlockSpec((rb, W), lambda i: (i, 0)), out_shape=S((R, W), f32),
                          compiler_params=_cparams("parallel"), name=name)(a)


DMA = pltpu.SemaphoreType.DMA


class GatherExchange:
    def __init__(self, arrays):
        self.arrays = list(arrays)
        self.na = len(self.arrays)
        self.out_shape = [S((N_DEV,) + a.shape, a.dtype) for a in self.arrays]
        self.scratch = [DMA((7 * self.na,)), DMA((7 * self.na,)), DMA((self.na,))]

    def ops(self, x_refs, out_refs, sems):
        send_sems, recv_sems, local_sems = sems
        na = self.na
        x, y, c = _me()
        me, sibling = (x, y, c), (x, y, 1 - c)
        chips = [(1 - x, y), (x, 1 - y), (1 - x, 1 - y)]

        def rows(a, px, py, pc):
            return out_refs[a].at[4 * px + 2 * py + pc]

        def copy(a, k, block, to, src=None):
            return pltpu.make_async_remote_copy(
                src_ref=rows(a, *block) if src is None else src, dst_ref=rows(a, *block),
                send_sem=send_sems.at[7 * a + k], recv_sem=recv_sems.at[7 * a + k], device_id=to, device_id_type=MESH)

        def local(a):
            return pltpu.make_async_copy(x_refs[a], rows(a, *me), local_sems.at[a])

        def first(a):
            return [copy(a, 0, me, sibling, src=x_refs[a])] + [copy(a, 1 + j, me, (*chip, c), src=x_refs[a])
                                                                for j, chip in enumerate(chips)]

        def start():
            for a in range(na):
                local(a).start()
                for cp in first(a):
                    cp.start()

        def mid():
            for a in range(na):
                for j, chip in enumerate(chips):
                    copy(a, 1 + j, (*chip, c), me).wait_recv()
                    copy(a, 4 + j, (*chip, c), sibling).start()

        def finish():
            for a in range(na):
                copy(a, 0, sibling, me).wait_recv()
                for j, chip in enumerate(chips):
                    copy(a, 4 + j, (*chip, 1 - c), me).wait_recv()
                for cp in first(a) + [copy(a, 4 + j, (*chip, c), sibling) for j, chip in enumerate(chips)]:
                    cp.wait_send()
                local(a).wait()

        return start, mid, finish


class SiblingExchange:
    def __init__(self, arrays):
        self.arrays = list(arrays)
        self.na = len(self.arrays)
        self.out_shape = [S((4,) + g.shape[2:], g.dtype) for g in self.arrays]
        self.scratch = [DMA((self.na,)), DMA((self.na,))]

    def ops(self, g_refs, out_refs, sems):
        send_sems, recv_sems = sems
        x, y, c = _me()

        def copy(a):
            return pltpu.make_async_remote_copy(src_ref=g_refs[a].at[:, 1 - c], dst_ref=out_refs[a],
                                                send_sem=send_sems.at[a], recv_sem=recv_sems.at[a],
                                                device_id=(x, y, 1 - c), device_id_type=MESH)

        def start():
            for a in range(self.na):
                copy(a).start()

        def finish():
            for a in range(self.na):
                copy(a).wait()

        return start, None, finish


class ChipsExchange:
    def __init__(self, arrays):
        self.arrays = list(arrays)
        self.na = len(self.arrays)
        self.out_shape = [S(p.shape, p.dtype) for p in self.arrays]
        self.scratch = [DMA((3 * self.na,)), DMA((3 * self.na,)), DMA((self.na,))]

    def ops(self, p_refs, out_refs, sems):
        send_sems, recv_sems, local_sems = sems
        x, y, c = _me()
        mine = 2 * x + y
        chips = [(1 - x, y), (x, 1 - y), (1 - x, 1 - y)]

        def local(a):
            return pltpu.make_async_copy(p_refs[a].at[mine], out_refs[a].at[mine], local_sems.at[a])

        def send(a, j):
            px, py = chips[j]
            return pltpu.make_async_remote_copy(src_ref=p_refs[a].at[2 * px + py], dst_ref=out_refs[a].at[mine],
                                                send_sem=send_sems.at[3 * a + j], recv_sem=recv_sems.at[3 * a + j],
                                                device_id=(px, py, c), device_id_type=MESH)

        def recv(a, j):
            px, py = chips[j]
            return pltpu.make_async_remote_copy(src_ref=p_refs[a].at[mine], dst_ref=out_refs[a].at[2 * px + py],
                                                send_sem=send_sems.at[3 * a + j], recv_sem=recv_sems.at[3 * a + j],
                                                device_id=(px, py, c), device_id_type=MESH)

        def start():
            for a in range(self.na):
                local(a).start()
                for j in range(3):
                    send(a, j).start()

        def finish():
            for a in range(self.na):
                for j in range(3):
                    recv(a, j).wait_recv()
                for j in range(3):
                    send(a, j).wait_send()
                local(a).wait()

        return start, None, finish


def exchange(ex, name):
    na = ex.na

    def body(*refs):
        start, mid, finish = ex.ops(refs[:na], refs[na:2 * na], refs[2 * na:])
        start()
        if mid is not None:
            mid()
        finish()

    return pl.pallas_call(body, in_specs=[ANY] * na, out_specs=[ANY] * na, out_shape=ex.out_shape,
                          scratch_shapes=ex.scratch, name=name)(*ex.arrays)


def _host_call(body, grid, in_specs, out_specs, out_shape, scratch_shapes, sem, name, args, hosted):
    if hosted is None:
        res = pl.pallas_call(body, grid=grid, in_specs=in_specs, out_specs=out_specs, out_shape=out_shape,
                             scratch_shapes=scratch_shapes, compiler_params=_cparams(*sem), name=name)(*args)
        return res, None
    n_in, n_out, n_sc, na = len(in_specs), len(out_shape), len(scratch_shapes), hosted.na
    nsteps = 1
    for g_ in grid:
        nsteps *= g_
    mid_step = (3 * nsteps) // 4
    i1 = n_in + na
    i2 = i1 + n_out
    i3 = i2 + na
    i4 = i3 + n_sc

    def wrapped(*refs):
        step = pl.program_id(0)
        for ax in range(1, len(grid)):
            step = step * grid[ax] + pl.program_id(ax)
        start, mid, finish = hosted.ops(refs[n_in:i1], refs[i2:i3], refs[i4:])
        pl.when(step == 0)(start)
        if mid is not None:
            pl.when(step == mid_step)(mid)
        body(*refs[:n_in], *refs[i1:i2], *refs[i3:i4])
        pl.when(step == nsteps - 1)(finish)

    res = pl.pallas_call(
        wrapped, grid=grid, in_specs=list(in_specs) + [ANY] * na, out_specs=list(out_specs) + [ANY] * na,
        out_shape=list(out_shape) + hosted.out_shape, scratch_shapes=list(scratch_shapes) + hosted.scratch,
        compiler_params=_cparams(*(("arbitrary",) * len(grid))), name=name)(*args, *hosted.arrays)
    return res[:n_out], res[n_out:]


PACK_ALIGN = 16 * PACK_W


def _pad_to(v, mult):
    n = v.shape[-1]
    extra = (-n) % mult
    if extra == 0:
        return v
    return jnp.concatenate([v, jnp.zeros(v.shape[:-1] + (extra,), v.dtype)], axis=-1)


def _f32_as_bf16_pairs(v):
    return lax.bitcast_convert_type(v.reshape(-1), bf16).reshape(-1)


def _bf16_pairs_as_f32(v):
    return lax.bitcast_convert_type(v.reshape(v.shape[:-1] + (v.shape[-1] // 2, 2)), f32)


def _col_shards(gw):
    lead = gw.shape[:-1]
    n = gw.shape[-1] // N_DEV
    t = gw.reshape(lead + (N_DEV, n))
    t = jnp.moveaxis(t, -2, 0)
    return t.reshape(N_DEV, -1)


def kernel(x, c, ctx, c_ctx, mod_w, mod_b, norm1_w, norm2_w, ssd_w_in, ssd_conv_w, ssd_conv_b, ssd_dt_bias, ssd_a_log, ssd_d, ssd_norm_w, ssd_w_out, conf_w_pw1, conf_b_pw1, conf_w_dw, conf_b_dw, conf_ln_w, conf_ln_b, conf_w_pw2, conf_b_pw2, ffn_w_up, ffn_conv_w, ffn_conv_b, ffn_w_down, final_norm_w, loss_target, m_c_ctx, m_mod_w, m_mod_b, m_norm1_w, m_norm2_w, m_ssd_w_in, m_ssd_conv_w, m_ssd_conv_b, m_ssd_dt_bias, m_ssd_a_log, m_ssd_d, m_ssd_norm_w, m_ssd_w_out, m_conf_w_pw1, m_conf_b_pw1, m_conf_w_dw, m_conf_b_dw, m_conf_ln_w, m_conf_ln_b, m_conf_w_pw2, m_conf_b_pw2, m_ffn_w_up, m_ffn_conv_w, m_ffn_conv_b, m_ffn_w_down, m_final_norm_w, v_c_ctx, v_mod_w, v_mod_b, v_norm1_w, v_norm2_w, v_ssd_w_in, v_ssd_conv_w, v_ssd_conv_b, v_ssd_dt_bias, v_ssd_a_log, v_ssd_d, v_ssd_norm_w, v_ssd_w_out, v_conf_w_pw1, v_conf_b_pw1, v_conf_w_dw, v_conf_b_dw, v_conf_ln_w, v_conf_ln_b, v_conf_w_pw2, v_conf_b_pw2, v_ffn_w_up, v_ffn_conv_w, v_ffn_conv_b, v_ffn_w_down, v_final_norm_w):
    mx, my, mc = _me()
    me = 4 * mx + 2 * my + mc
    L = x.shape[1]
    LC = ctx.shape[1]
    T = LC + L
    w_in_cols = ssd_w_in.shape[2] * N_DEV
    n_dt = w_in_cols - DI - CONVD

    small = [c[0], ssd_conv_w[0], conf_b_pw1[0], conf_w_dw[0], conf_b_dw[0], conf_ln_w[0], conf_ln_b[0], conf_b_pw2[0],
             ffn_conv_w]
    parts = [_f32_as_bf16_pairs(t) for t in small]
    sizes = [p.shape[0] for p in parts]
    small_flat = _pad_to(jnp.concatenate(parts), PACK_ALIGN).reshape(-1, PACK_W)
    w_in, small_g = exchange(GatherExchange([ssd_w_in[0].astype(bf16), small_flat]), "gather_first")
    w_up, w_down = [None, None], [None, None]
    small_g = small_g.reshape(N_DEV, -1)
    offs = [0]
    for s_ in sizes:
        offs.append(offs[-1] + s_)
    sm = [_bf16_pairs_as_f32(small_g[:, offs[i]:offs[i + 1]]) for i in range(len(sizes))]

    def cols(pc, K):
        return jnp.moveaxis(pc.reshape(N_DEV, K, -1), 0, 1).reshape(K, -1)

    c_all = sm[0]
    conv_w5 = cols(sm[1], 5)
    b_pw1 = sm[2].reshape(1, 2 * D)
    w_dw = cols(sm[3], CONF_K)
    b_dw, ln_w, ln_b, b_pw2 = (sm[i].reshape(1, D) for i in (4, 5, 6, 7))
    fcw = sm[8].reshape(N_DEV, 2, 9, FH // N_DEV)
    ffn_cw = [cols(fcw[:, i].reshape(N_DEV, -1), 9) for i in range(2)]
    in_segs = (DI, CONVD, n_dt)
    up_segs = (FH, FH)
    pw1_segs = (D, D)

    c16 = jnp.concatenate([c_all, jnp.broadcast_to(c_ctx[None, :], (N_DEV, D))], axis=0)
    m_sh = mod_fwd(c16, mod_w, "mod_fwd")
    mod_cols = mod_w.shape[2]
    m_gath = allgather_small(m_sh.reshape(2 * CROWS, mod_cols), "gather_mod")
    fly_a = gather_start([ssd_w_out[0].astype(bf16), ffn_w_up[0].astype(bf16), ffn_w_down[0].astype(bf16)],
                         "gather_a_start", m_gath)
    fly_b = gather_start([conf_w_pw1[0].astype(bf16), conf_w_pw2[0].astype(bf16)], "gather_b_start", fly_a[-1])
    fly_c = gather_start([ffn_w_up[1].astype(bf16), ffn_w_down[1].astype(bf16)], "gather_c_start", fly_b[-1])
    m_all = jnp.moveaxis(m_gath.reshape(N_DEV, 2, CROWS, mod_cols), 0, 2).reshape(2, CROWS, 6 * D) + mod_b[:, None, :]
    m_all = m_all + fly_c[-1][0, 0]
    m_lat = lax.dynamic_index_in_dim(m_all, me, axis=1, keepdims=False).reshape(2, 6, 1, D)
    m_ctx = m_all[:, N_DEV].reshape(2, 6, 1, D)
    zero_row = jnp.zeros((1, D), f32)

    def ffn_fwd(a2, i, tag):
        val, gate = smm_fwd(a2, w_up[i], None, up_segs, f"ffn{tag}_up")
        act, gs_, gvds, _ = ffn_gate_fwd(val, gate, ffn_cw[i], ffn_conv_b[i][None], f"ffn{tag}_gate")
        o2 = matmul(act, w_down[i], "nn", f32, f"ffn{tag}_down")
        return o2, (a2, gate, gs_, gvds, act)

    def ffn_bwd(do2, i, saved, tag):
        a2, gate, gs_, gvds, act = saved
        g_down = matmul(act, do2, "tn", bf16, f"ffn{tag}_down_dw")
        dact = matmul(do2, w_down[i], "nt", bf16, f"ffn{tag}_down_dx")
        dval, dgate, dcw, dcb = ffn_gate_bwd(gate, gs_, gvds, ffn_cw[i], dact, f"ffn{tag}_gate_bwd")
        g_up = smm_dw(a2, [dval, dgate], FH // 4, up_segs, 2, True, f"ffn{tag}_up_dw")
        da2, _ = smm_dx([dval, dgate], w_up[i], None, up_segs, bf16, f"ffn{tag}_up_dx")
        return da2, dict(w_up=g_up, w_down=g_down, conv_w=dcw, conv_b=dcb)

    nctx = LC // Q
    hx = x[0]
    sc0 = jnp.stack([m_ctx[0, 1], m_lat[0, 1]])
    sh0 = jnp.stack([m_ctx[0, 0], m_lat[0, 0]])
    a0 = modnorm_fwd(hx, norm1_w[0][None], sc0, sh0, LC // TB, "ssd_norm", ctx=ctx[0])
    z, xbc_pre, dt_raw = smm_fwd(a0, w_in, None, in_segs, "ssd_in")
    segs = ((0, LC), (LC, L))
    xbc, xbc_dsilu, _ = ssd_conv_fwd(xbc_pre, conv_w5, ssd_conv_b, segs, "ssd_conv")
    dt4 = dt_raw[:, :n_dt].reshape(T, 2, G, HPG)
    dtc = jnp.transpose(dt4, (1, 2, 0, 3))
    dtr = jnp.transpose(dt4, (1, 2, 3, 0))
    bias3 = ssd_dt_bias[0].reshape(2, G, HPG)
    alog3 = ssd_a_log[0].reshape(2, G, HPG)
    bc_, br_ = bias3[:, :, None, :], bias3[:, :, :, None]
    alc, alr = alog3[:, :, None, :], alog3[:, :, :, None]
    (y2, s_in_all), _ = ssd_scan_fwd(xbc, dtc, dtr, bc_, br_, alc, alr, nctx, "ssd_scan")
    dexp = jnp.repeat(ssd_d[0], P)[None, :]
    yn = ssd_gate_fwd(y2, xbc, z, dexp, ssd_norm_w, LC // GTB, "ssd_gate")
    w_out_g, w_up[0], w_down0_g = gather_wait(*fly_a[:5], yn, "gather_a_wait")
    w_out = w_out_g.reshape(DI, D)
    w_down[0] = w_down0_g.reshape(FH, D)
    o_ssd = matmul(yn, w_out, "nn", f32, "ssd_out")
    h1, a2_0 = resnorm_fwd(hx, o_ssd, m_lat[0, 2], zero_row, norm2_w[0][None], m_lat[0, 4], m_lat[0, 3], "ssd_res")
    o2_0, ffn0_saved = ffn_fwd(a2_0, 0, "0")

    h2, a1 = resnorm_fwd(h1, o2_0, m_lat[0, 5], zero_row, norm1_w[1][None], m_lat[1, 1], m_lat[1, 0], "ffn0_res")
    w_pw1, w_pw2_g = gather_wait(*fly_b[:5], a1, "gather_b_wait")
    w_pw2 = w_pw2_g.reshape(D, D)
    pa, pg = smm_fwd(a1, w_pw1, None, pw1_segs, "conf_pw1")
    dwc, _ = conf_glu_conv_fwd(pa, pg, b_pw1, w_dw, b_dw, "conf_conv")
    s1 = ln_silu_fwd(dwc, ln_w, ln_b, "conf_ln")
    o_conf = matmul(s1, w_pw2, "nn", f32, "conf_pw2")
    h3, a2_1 = resnorm_fwd(h2, o_conf, m_lat[1, 2], b_pw2, norm2_w[1][None], m_lat[1, 4], m_lat[1, 3], "conf_res")
    w_up[1], w_down1_g = gather_wait(*fly_c[:5], h3, "gather_c_wait")
    w_down[1] = w_down1_g.reshape(FH, D)
    o2_1, ffn1_saved = ffn_fwd(a2_1, 1, "1")

    loss_part, dh4, g_final, do2_1, dg2_1 = final_loss(h3, o2_1, m_lat[1, 5], final_norm_w[None], loss_target[0],
                                                       "loss_head")
    da2_1, gf1 = ffn_bwd(do2_1, 1, ffn1_saved, "1")
    dh3, dn2_1, dsc2_1, dsh2_1, do_conf, dg1_1, g_b_pw2 = normres_bwd(
        h3, norm2_w[1][None], m_lat[1, 4], m_lat[1, 3], da2_1, dh4, o_conf, m_lat[1, 2], b_pw2, "ffn1_norm_bwd")
    gf1.update(norm2=dn2_1, sh2=dsh2_1, sc2=dsc2_1, g2=dg2_1)
    g_pw2 = matmul(s1, do_conf, "tn", bf16, "conf_pw2_dw")
    ds1 = matmul(do_conf, w_pw2, "nt", bf16, "conf_pw2_dx")
    ddwc, g_ln_w, g_ln_b = ln_silu_bwd(dwc, ln_w, ln_b, ds1, "conf_ln_bwd")
    dpa, dpg, dba, dbg, g_w_dw, g_b_dw = conf_glu_conv_bwd(pa, pg, b_pw1, w_dw, ddwc, "conf_conv_bwd")
    g_b_pw1 = jnp.concatenate([dba, dbg], axis=1)
    g_pw1 = smm_dw(a1, [dpa, dpg], 2 * D // N_DEV, pw1_segs, 1, False, "conf_pw1_dw")
    da1, _ = smm_dx([dpa, dpg], w_pw1, None, pw1_segs, bf16, "conf_pw1_dx")
    dh2, g_n1_1, dsc1_1, dsh1_1, do2_0, dg2_0, _ = normres_bwd(
        h2, norm1_w[1][None], m_lat[1, 1], m_lat[1, 0], da1, dh3, o2_0, m_lat[0, 5], zero_row, "conf_norm_bwd")
    da2_0, gf0 = ffn_bwd(do2_0, 0, ffn0_saved, "0")
    dh1, dn2_0, dsc2_0, dsh2_0, do_ssd, dg1_0, _ = normres_bwd(
        h1, norm2_w[0][None], m_lat[0, 4], m_lat[0, 3], da2_0, dh2, o_ssd, m_lat[0, 2], zero_row, "ffn0_norm_bwd")
    gf0.update(norm2=dn2_0, sh2=dsh2_0, sc2=dsc2_0, g2=dg2_0)
    g_w_out = matmul(yn, do_ssd, "tn", bf16, "ssd_out_dw")
    dyn = matmul(do_ssd, w_out, "nt", bf16, "ssd_out_dx")
    core = mc.reshape(1).astype(jnp.int32)

    def by_device(t):
        return t.reshape((4, 2, -1, t.shape[-1]))

    early = [by_device(t) for t in (gf1["w_up"], gf1["w_down"], g_pw2, g_pw1, gf0["w_up"], gf0["w_down"], g_w_out)]
    (dy, dz, g_dexp, g_ssd_norm), early_sib = ssd_gate_bwd(
        y2, xbc, z, dexp, ssd_norm_w, dyn, LC // GTB, "ssd_gate_bwd", SiblingExchange(early))
    early_part = [add_own(t, r_, core, f"reduce_add{i}") for i, (t, r_) in enumerate(zip(early, early_sib))]
    (dxbc2, ddtc, ddtr, dbc, dbr, dalc, dalr), early_red = ssd_scan_bwd(
        xbc, dtc, dtr, bc_, br_, alc, alr, s_in_all, dy, nctx, "ssd_scan_bwd", ChipsExchange(early_part))
    ddt = (jnp.transpose(ddtc, (2, 0, 1, 3)) + jnp.transpose(ddtr, (3, 0, 1, 2))).reshape(T, n_dt)
    g_dt_bias = (dbc[:, :, 0, :] + dbr[:, :, :, 0]).reshape(2, NH_SSD)
    g_a_log = (dalc[:, :, 0, :] + dalr[:, :, :, 0]).reshape(2, NH_SSD)
    g_ssd_d = g_dexp[0, :NH_SSD]
    du, g_conv_w5, g_conv_b5 = ssd_conv_bwd(xbc_pre, conv_w5, xbc_dsilu, dxbc2, dy, dexp, segs, "ssd_conv_bwd")
    ddt_p = _pad_to(ddt, 128).astype(bf16)
    g_w_in = smm_dw(a0, [dz, du, ddt_p], w_in.shape[-1], in_segs, 2, True, "ssd_in_dw")
    g_ffn_cw = jnp.stack([gf0["conv_w"], gf1["conv_w"]])
    small_shards = [_col_shards(t) for t in (g_conv_w5, g_b_pw1, g_w_dw, g_b_dw, g_ln_w, g_ln_b, g_b_pw2, g_ffn_cw)]
    gsizes = [s_.shape[1] for s_ in small_shards]
    g_small = _pad_to(jnp.concatenate(small_shards, axis=1), PACK_ALIGN).astype(bf16)
    late = [by_device(g_w_in), by_device(g_small.reshape(N_DEV, -1, PACK_W))]
    da0, late_sib = smm_dx([dz, du, ddt_p], w_in, None, in_segs, f32, "ssd_in_dx", SiblingExchange(late))
    late_part = [add_own(t, r_, core, f"reduce_add_late{i}", twice=True) for i, (t, r_) in enumerate(zip(late, late_sib))]
    late_flying = chips_start([p_[0] for p_ in late_part], [p_[1] for p_ in late_part], "reduce_chips_late_start")
    dh0, g_n1_0, dsc1_0, dsh1_0 = modnorm_bwd(hx, norm1_w[0][None], sc0, sh0, da0, dh1, LC // TB, "ssd_norm_bwd",
                                              ctx=ctx[0])
    grad_x = dh0[None]

    r_up1, r_down1, r_pw2, r_pw1, r_up0, r_down0, r_out = early_red
    big = {}
    def tr(t):
        return jnp.swapaxes(t, -1, -2)

    up_t, m_up_t, v_up_t = tr(ffn_w_up), tr(m_ffn_w_up), tr(v_ffn_w_up)
    send_sems, recv_sems, late_p, late_land, token = late_flying
    up0 = sum_adamw(r_up0, up_t, m_up_t, v_up_t, 0, "adamw_ffn_w_up0", after=token)
    up1 = sum_adamw(r_up1, up_t, m_up_t, v_up_t, 1, "adamw_ffn_w_up1", into=up0)
    big["ffn_w_up"] = tuple(tr(t) for t in up1)
    big["conf_w_pw1"] = sum_adamw(r_pw1, conf_w_pw1[0], m_conf_w_pw1[0], v_conf_w_pw1[0], None, "adamw_conf_w_pw1",
                                  after=up1[0])
    big["ssd_w_out"] = sum_adamw(r_out, ssd_w_out[0], m_ssd_w_out[0], v_ssd_w_out[0], None, "adamw_ssd_w_out",
                                 after=big["conf_w_pw1"][0])
    dn0 = sum_adamw(r_down0, ffn_w_down, m_ffn_w_down, v_ffn_w_down, 0, "adamw_ffn_w_down0", after=big["ssd_w_out"][0])
    big["ffn_w_down"] = sum_adamw(r_down1, ffn_w_down, m_ffn_w_down, v_ffn_w_down, 1, "adamw_ffn_w_down1", into=dn0)
    big["conf_w_pw2"] = sum_adamw(r_pw2, conf_w_pw2[0], m_conf_w_pw2[0], v_conf_w_pw2[0], None, "adamw_conf_w_pw2",
                                  after=big["ffn_w_down"][0])

    zeros_d = jnp.zeros((1, D), f32)
    dm_lat = jnp.stack([
        jnp.concatenate([dsh1_0[1], dsc1_0[1], dg1_0, gf0["sh2"], gf0["sc2"], gf0["g2"]], axis=1),
        jnp.concatenate([dsh1_1, dsc1_1, dg1_1, gf1["sh2"], gf1["sc2"], gf1["g2"]], axis=1)])
    dm_ctx = jnp.stack([
        jnp.concatenate([dsh1_0[0], dsc1_0[0]] + [zeros_d] * 4, axis=1), jnp.zeros((1, 6 * D), f32)])
    dm_mine = jnp.concatenate([dm_lat.reshape(2, 6 * D), dm_ctx.reshape(2, 6 * D),
                               jnp.zeros((4, 6 * D), f32)], axis=0)
    dm_g = allgather_small(dm_mine, "gather_dmod", after=big["conf_w_pw2"][0])
    dm_all = jnp.concatenate([jnp.moveaxis(dm_g[:, 0:2], 0, 1), jnp.moveaxis(dm_g[:, 2:4], 0, 1)], axis=1)
    dm_sh = lax.dynamic_slice_in_dim(dm_all, me * mod_cols, mod_cols, axis=2)
    g_mod_w, g_cctx_part, g_mod_b = mod_bwd(c16, mod_w, dm_sh, dm_all, "mod_bwd")

    rep = [jnp.stack([g_n1_0[0], g_n1_1[0]]), jnp.stack([gf0["norm2"][0], gf1["norm2"][0]]), g_conv_b5, g_dt_bias, g_a_log,
           g_ssd_d, g_ssd_norm, jnp.stack([gf0["conv_b"][0], gf1["conv_b"][0]]), g_final, g_cctx_part, loss_part[:, :1]]
    rep_sizes = [r_.size for r_ in rep]
    rep_flat = _pad_to(jnp.concatenate([r_.reshape(-1) for r_ in rep]), 8 * PACK_W).reshape(-1, PACK_W)
    _, rep_sum = allgather_small(rep_flat, "reduce_replicated", with_sum=True)
    rep_sum = rep_sum.reshape(-1)
    roffs = [0]
    for s_ in rep_sizes:
        roffs.append(roffs[-1] + s_)
    rp = [rep_sum[roffs[i]:roffs[i + 1]] for i in range(len(rep_sizes))]
    loss = rp[10].reshape(())

    r_in, r_small = chips_wait(send_sems, recv_sems, late_p, late_land, rep_sum, "reduce_chips_late_wait")
    w_in_res = sum_adamw(r_in, tr(ssd_w_in[0]), tr(m_ssd_w_in[0]), tr(v_ssd_w_in[0]), None, "adamw_ssd_w_in")
    big["ssd_w_in"] = tuple(tr(t) for t in w_in_res)
    g_flat = sum_rows(r_small, "reduce_sum_small").reshape(-1)
    goffs = [0]
    for s_ in gsizes:
        goffs.append(goffs[-1] + s_)
    gs = [g_flat[goffs[i]:goffs[i + 1]] for i in range(len(gsizes))]
    grads = {
        "c_ctx": rp[9], "mod_w": g_mod_w, "mod_b": g_mod_b, "norm1_w": rp[0], "norm2_w": rp[1],
        "ssd_conv_w": gs[0], "ssd_conv_b": rp[2], "ssd_dt_bias": rp[3], "ssd_a_log": rp[4], "ssd_d": rp[5],
        "ssd_norm_w": rp[6], "conf_b_pw1": gs[1], "conf_w_dw": gs[2],
        "conf_b_dw": gs[3], "conf_ln_w": gs[4], "conf_ln_b": gs[5], "conf_b_pw2": gs[6],
        "ffn_conv_w": gs[7], "ffn_conv_b": rp[7], "final_norm_w": rp[8],
    }
    weights = dict(c_ctx=c_ctx, mod_w=mod_w, mod_b=mod_b, norm1_w=norm1_w, norm2_w=norm2_w, ssd_w_in=ssd_w_in, ssd_conv_w=ssd_conv_w, ssd_conv_b=ssd_conv_b, ssd_dt_bias=ssd_dt_bias, ssd_a_log=ssd_a_log, ssd_d=ssd_d, ssd_norm_w=ssd_norm_w, ssd_w_out=ssd_w_out, conf_w_pw1=conf_w_pw1, conf_b_pw1=conf_b_pw1, conf_w_dw=conf_w_dw, conf_b_dw=conf_b_dw, conf_ln_w=conf_ln_w, conf_ln_b=conf_ln_b, conf_w_pw2=conf_w_pw2, conf_b_pw2=conf_b_pw2, ffn_w_up=ffn_w_up, ffn_conv_w=ffn_conv_w, ffn_conv_b=ffn_conv_b, ffn_w_down=ffn_w_down, final_norm_w=final_norm_w)
    m_in = dict(c_ctx=m_c_ctx, mod_w=m_mod_w, mod_b=m_mod_b, norm1_w=m_norm1_w, norm2_w=m_norm2_w, ssd_w_in=m_ssd_w_in, ssd_conv_w=m_ssd_conv_w, ssd_conv_b=m_ssd_conv_b, ssd_dt_bias=m_ssd_dt_bias, ssd_a_log=m_ssd_a_log, ssd_d=m_ssd_d, ssd_norm_w=m_ssd_norm_w, ssd_w_out=m_ssd_w_out, conf_w_pw1=m_conf_w_pw1, conf_b_pw1=m_conf_b_pw1, conf_w_dw=m_conf_w_dw, conf_b_dw=m_conf_b_dw, conf_ln_w=m_conf_ln_w, conf_ln_b=m_conf_ln_b, conf_w_pw2=m_conf_w_pw2, conf_b_pw2=m_conf_b_pw2, ffn_w_up=m_ffn_w_up, ffn_conv_w=m_ffn_conv_w, ffn_conv_b=m_ffn_conv_b, ffn_w_down=m_ffn_w_down, final_norm_w=m_final_norm_w)
    v_in = dict(c_ctx=v_c_ctx, mod_w=v_mod_w, mod_b=v_mod_b, norm1_w=v_norm1_w, norm2_w=v_norm2_w, ssd_w_in=v_ssd_w_in, ssd_conv_w=v_ssd_conv_w, ssd_conv_b=v_ssd_conv_b, ssd_dt_bias=v_ssd_dt_bias, ssd_a_log=v_ssd_a_log, ssd_d=v_ssd_d, ssd_norm_w=v_ssd_norm_w, ssd_w_out=v_ssd_w_out, conf_w_pw1=v_conf_w_pw1, conf_b_pw1=v_conf_b_pw1, conf_w_dw=v_conf_w_dw, conf_b_dw=v_conf_b_dw, conf_ln_w=v_conf_ln_w, conf_ln_b=v_conf_ln_b, conf_w_pw2=v_conf_w_pw2, conf_b_pw2=v_conf_b_pw2, ffn_w_up=v_ffn_w_up, ffn_conv_w=v_ffn_conv_w, ffn_conv_b=v_ffn_conv_b, ffn_w_down=v_ffn_w_down, final_norm_w=v_final_norm_w)

    out_g, out_d, out_m, out_v = [], [], [], []
    for name_, w_ in weights.items():
        shape = w_.shape
        if name_ in big:
            for lst, t in zip((out_g, out_d, out_m, out_v), big[name_]):
                lst.append(t.reshape(shape))
            continue
        cols2 = shape[-1] if len(shape) > 1 else shape[0]
        g2 = grads[name_].reshape(-1, cols2)
        d_, nm_, nv_ = adamw(w_.reshape(-1, cols2), g2, m_in[name_].reshape(-1, cols2), v_in[name_].reshape(-1, cols2),
                             f"adamw_{name_}")
        out_g.append(g2.reshape(shape))
        out_d.append(d_.reshape(shape))
        out_m.append(nm_.reshape(shape))
        out_v.append(nv_.reshape(shape))
    return (loss, grad_x, *out_g, *out_d, *out_m, *out_v)
```
